```python
import jax, jax.numpy as jnp
from jax import lax
import numpy as np

D_MODEL = 1024
BATCH = 8
SEQ = 8192
DEPTH = 1

HEAD_DIM = 64
N_HEADS = D_MODEL // HEAD_DIM
N_HEADS_A = N_HEADS // 4
N_HEADS_B = N_HEADS - N_HEADS_A
WIDTH_A = N_HEADS_A * HEAD_DIM
WIDTH_B = N_HEADS_B * HEAD_DIM
CHUNK = 128
BLOCK = 128
DILATED_BRANCHES = ((128, 1), (512, 4), (2048, 16))
ROPE_THETA = 10000.0
D_FF = -(-8 * D_MODEL // (3 * 256)) * 256
PLE_DIM = 256
IN_COLS = 2 * WIDTH_A + 3 * WIDTH_B
EPS = 1e-6

kernel_name = "hybrid_sgu_dilated_attn_block"


def rmsnorm(x, g):
    xf = x.astype(jnp.float32)
    y = xf * lax.rsqrt(jnp.mean(xf * xf, axis=-1, keepdims=True) + EPS)
    return (y * g.astype(jnp.float32)).astype(x.dtype)


def rope(t, pos):
    half = t.shape[-1] // 2
    inv = ROPE_THETA ** (-jnp.arange(half, dtype=jnp.float32) / half)
    ang = pos[:, None] * inv[None, :]
    cos = jnp.cos(ang)[None, :, None, :]
    sin = jnp.sin(ang)[None, :, None, :]
    t = t.astype(jnp.float32)
    t1, t2 = t[..., :half], t[..., half:]
    return jnp.concatenate([t1 * cos - t2 * sin, t1 * sin + t2 * cos], axis=-1)


def chunked_sgu(u, v, w_s, b_s, norm_g):
    b, s, _ = u.shape
    u = jax.nn.gelu(u.astype(jnp.float32))
    vf = jax.nn.gelu(v.astype(jnp.float32))
    mu = jnp.mean(vf, axis=-1, keepdims=True)
    var = jnp.mean(jnp.square(vf - mu), axis=-1, keepdims=True)
    vf = (vf - mu) * lax.rsqrt(var + EPS) * norm_g.astype(jnp.float32)
    vf = vf.reshape(b, s // CHUNK, CHUNK, N_HEADS_A, HEAD_DIM)
    causal = jnp.tril(jnp.ones((CHUNK, CHUNK), jnp.float32))
    w = w_s.astype(jnp.float32) * causal[None]
    mixed = jnp.einsum('hij,bnjhd->bnihd', w, vf) + b_s.astype(jnp.float32).T[None, None, :, :, None]
    return u * mixed.reshape(b, s, WIDTH_A)


def dilated_branch(q, k, v, window, dilation):
    b, h, s, dh = q.shape
    n_back = window // dilation
    span = dilation * BLOCK
    s_pad = -(-s // span) * span
    sub_len = s_pad // dilation
    nb = sub_len // BLOCK

    def to_sub(t):
        t = jnp.pad(t, ((0, 0), (0, 0), (0, s_pad - s), (0, 0)))
        t = t.reshape(b, h, sub_len, dilation, dh)
        t = jnp.swapaxes(t, 2, 3)
        return t.reshape(b, h, dilation, nb, BLOCK, dh)

    qb, kb, vb = to_sub(q), to_sub(k), to_sub(v)
    shift = ((0, 0), (0, 0), (0, 0), (1, 0), (0, 0), (0, 0))
    kw = jnp.concatenate([jnp.pad(kb[:, :, :, :-1], shift), kb], axis=4)
    vw = jnp.concatenate([jnp.pad(vb[:, :, :, :-1], shift), vb], axis=4)
    scores = jnp.einsum('bhrnqd,bhrnkd->bhrnqk', qb, kw) * (dh ** -0.5)
    qi = jnp.arange(BLOCK)[:, None]
    kc = jnp.arange(2 * BLOCK)[None, :]
    dist = BLOCK + qi - kc
    band = (dist >= 0) & (dist <= n_back)
    blk = jnp.arange(nb)[:, None, None]
    valid = band[None] & ((blk > 0) | (kc[None] >= BLOCK))
    scores = jnp.where(valid, scores, -jnp.inf)
    m = jnp.max(scores, axis=-1, keepdims=True)
    pr = jnp.exp(scores - m)
    l = jnp.sum(pr, axis=-1, keepdims=True)
    o = jnp.einsum('bhrnqk,bhrnkd->bhrnqd', pr, vw) / l
    lse = (m + jnp.log(l))[..., 0]
    o = jnp.swapaxes(o.reshape(b, h, dilation, sub_len, dh), 2, 3).reshape(b, h, s_pad, dh)[:, :, :s]
    lse = jnp.swapaxes(lse.reshape(b, h, dilation, sub_len), 2, 3).reshape(b, h, s_pad)[:, :, :s]
    return o, lse


def dilated_mixture_attention(q, k, v):
    outs, lses = [], []
    for window, dilation in DILATED_BRANCHES:
        o, lse = dilated_branch(q, k, v, window, dilation)
        outs.append(o)
        lses.append(lse)
    o = jnp.stack(outs, axis=0)
    wts = jax.nn.softmax(jnp.stack(lses, axis=0), axis=0)
    return jnp.sum(wts[..., None] * o, axis=0)


def _fwd_setup_inputs(seed: int = 0) -> dict:
    key = jax.random.key(seed)
    ks = jax.random.split(key, 20)
    f32 = jnp.float32

    def nrm(k, shape, fan_in):
        return jax.random.normal(k, shape, f32) * (fan_in ** -0.5)

    def gain(k, shape):
        return 1.0 + 0.05 * jax.random.normal(k, shape, f32)

    return {
        "x": jax.random.normal(ks[0], (BATCH, SEQ, D_MODEL), f32),
        "p": jax.random.normal(ks[1], (DEPTH, BATCH, SEQ, PLE_DIM), f32),
        "mix_norm_g": gain(ks[2], (DEPTH, D_MODEL)),
        "w_in": nrm(ks[3], (DEPTH, D_MODEL, IN_COLS), D_MODEL),
        "sgu_w": nrm(ks[4], (DEPTH, N_HEADS_A, CHUNK, CHUNK), CHUNK),
        "sgu_b": 1.0 + 0.1 * jax.random.normal(ks[5], (DEPTH, N_HEADS_A, CHUNK), f32),
        "sgu_norm_g": gain(ks[6], (DEPTH, WIDTH_A)),
        "out_norm_a": gain(ks[7], (DEPTH, WIDTH_A)),
        "out_norm_b": gain(ks[8], (DEPTH, WIDTH_B)),
        "w_out": nrm(ks[9], (DEPTH, D_MODEL, D_MODEL), D_MODEL),
        "ffn_norm_g": gain(ks[10], (DEPTH, D_MODEL)),
        "w_gate": nrm(ks[11], (DEPTH, D_MODEL, D_FF), D_MODEL),
        "w_up": nrm(ks[12], (DEPTH, D_MODEL, D_FF), D_MODEL),
        "w_down": nrm(ks[13], (DEPTH, D_FF, D_MODEL), D_FF),
        "ple_norm_g": gain(ks[14], (DEPTH, D_MODEL)),
        "w_ple_gate": nrm(ks[15], (DEPTH, D_MODEL, D_MODEL), D_MODEL),
        "w_ple_proj": nrm(ks[16], (DEPTH, PLE_DIM, D_MODEL), PLE_DIM),
        "final_norm_g": gain(ks[17], (D_MODEL,)),
    }


def _fwd_reference(x, p, mix_norm_g, w_in, sgu_w, sgu_b, sgu_norm_g, out_norm_a, out_norm_b,
              w_out, ffn_norm_g, w_gate, w_up, w_down, ple_norm_g, w_ple_gate,
              w_ple_proj, final_norm_g):
    b, s, _ = x.shape
    pos = jnp.arange(s, dtype=jnp.float32)
    h = x
    for i in range(DEPTH):
        hn = rmsnorm(h, mix_norm_g[i])
        proj = hn @ w_in[i]
        u_a = proj[..., :WIDTH_A]
        v_a = proj[..., WIDTH_A:2 * WIDTH_A]
        qkv = proj[..., 2 * WIDTH_A:].reshape(b, s, 3, N_HEADS_B, HEAD_DIM)
        y_a = chunked_sgu(u_a, v_a, sgu_w[i], sgu_b[i], sgu_norm_g[i])
        q = jnp.transpose(rope(qkv[:, :, 0], pos), (0, 2, 1, 3))
        k = jnp.transpose(rope(qkv[:, :, 1], pos), (0, 2, 1, 3))
        v = jnp.transpose(qkv[:, :, 2].astype(jnp.float32), (0, 2, 1, 3))
        y_b = dilated_mixture_attention(q, k, v)
        y_b = jnp.transpose(y_b, (0, 2, 1, 3)).reshape(b, s, WIDTH_B)
        y = jnp.concatenate([rmsnorm(y_a, out_norm_a[i]), rmsnorm(y_b, out_norm_b[i])], axis=-1)
        h = h + (y.astype(h.dtype) @ w_out[i])
        hn = rmsnorm(h, ffn_norm_g[i])
        h = h + (jax.nn.silu(hn @ w_gate[i]) * (hn @ w_up[i])) @ w_down[i]
        gate = jax.nn.sigmoid(rmsnorm(h, ple_norm_g[i]) @ w_ple_gate[i])
        h = h + gate * (p[i] @ w_ple_proj[i])
    return rmsnorm(h, final_norm_g)


import jax as _jax
import jax.numpy as _jnp

TWIN_FORMAT = 'train_step'
FWD_PARAMS = ['x', 'p', 'mix_norm_g', 'w_in', 'sgu_w', 'sgu_b', 'sgu_norm_g', 'out_norm_a', 'out_norm_b', 'w_out', 'ffn_norm_g', 'w_gate', 'w_up', 'w_down', 'ple_norm_g', 'w_ple_gate', 'w_ple_proj', 'final_norm_g']
TWIN_WEIGHTS = ['mix_norm_g', 'w_in', 'sgu_w', 'sgu_b', 'sgu_norm_g', 'out_norm_a', 'out_norm_b', 'w_out', 'ffn_norm_g', 'w_gate', 'w_up', 'w_down', 'ple_norm_g', 'w_ple_gate', 'w_ple_proj', 'final_norm_g']
TWIN_DIFF_INPUT = 'x'
TWIN_INPUTS = ['x', 'p', 'mix_norm_g', 'w_in', 'sgu_w', 'sgu_b', 'sgu_norm_g', 'out_norm_a', 'out_norm_b', 'w_out', 'ffn_norm_g', 'w_gate', 'w_up', 'w_down', 'ple_norm_g', 'w_ple_gate', 'w_ple_proj', 'final_norm_g', 'loss_target', 'm_mix_norm_g', 'm_w_in', 'm_sgu_w', 'm_sgu_b', 'm_sgu_norm_g', 'm_out_norm_a', 'm_out_norm_b', 'm_w_out', 'm_ffn_norm_g', 'm_w_gate', 'm_w_up', 'm_w_down', 'm_ple_norm_g', 'm_w_ple_gate', 'm_w_ple_proj', 'm_final_norm_g', 'v_mix_norm_g', 'v_w_in', 'v_sgu_w', 'v_sgu_b', 'v_sgu_norm_g', 'v_out_norm_a', 'v_out_norm_b', 'v_w_out', 'v_ffn_norm_g', 'v_w_gate', 'v_w_up', 'v_w_down', 'v_ple_norm_g', 'v_w_ple_gate', 'v_w_ple_proj', 'v_final_norm_g']
TWIN_OUTPUTS = ['loss', 'grad_x', 'grad_mix_norm_g', 'grad_w_in', 'grad_sgu_w', 'grad_sgu_b', 'grad_sgu_norm_g', 'grad_out_norm_a', 'grad_out_norm_b', 'grad_w_out', 'grad_ffn_norm_g', 'grad_w_gate', 'grad_w_up', 'grad_w_down', 'grad_ple_norm_g', 'grad_w_ple_gate', 'grad_w_ple_proj', 'grad_final_norm_g', 'delta_mix_norm_g', 'delta_w_in', 'delta_sgu_w', 'delta_sgu_b', 'delta_sgu_norm_g', 'delta_out_norm_a', 'delta_out_norm_b', 'delta_w_out', 'delta_ffn_norm_g', 'delta_w_gate', 'delta_w_up', 'delta_w_down', 'delta_ple_norm_g', 'delta_w_ple_gate', 'delta_w_ple_proj', 'delta_final_norm_g', 'new_m_mix_norm_g', 'new_m_w_in', 'new_m_sgu_w', 'new_m_sgu_b', 'new_m_sgu_norm_g', 'new_m_out_norm_a', 'new_m_out_norm_b', 'new_m_w_out', 'new_m_ffn_norm_g', 'new_m_w_gate', 'new_m_w_up', 'new_m_w_down', 'new_m_ple_norm_g', 'new_m_w_ple_gate', 'new_m_w_ple_proj', 'new_m_final_norm_g', 'new_v_mix_norm_g', 'new_v_w_in', 'new_v_sgu_w', 'new_v_sgu_b', 'new_v_sgu_norm_g', 'new_v_out_norm_a', 'new_v_out_norm_b', 'new_v_w_out', 'new_v_ffn_norm_g', 'new_v_w_gate', 'new_v_w_up', 'new_v_w_down', 'new_v_ple_norm_g', 'new_v_w_ple_gate', 'new_v_w_ple_proj', 'new_v_final_norm_g']
TWIN_LEAF_KINDS = {'loss': 'loss', 'grad_x': 'grad_x', 'grad_mix_norm_g': 'grad_w', 'grad_w_in': 'grad_w', 'grad_sgu_w': 'grad_w', 'grad_sgu_b': 'grad_w', 'grad_sgu_norm_g': 'grad_w', 'grad_out_norm_a': 'grad_w', 'grad_out_norm_b': 'grad_w', 'grad_w_out': 'grad_w', 'grad_ffn_norm_g': 'grad_w', 'grad_w_gate': 'grad_w', 'grad_w_up': 'grad_w', 'grad_w_down': 'grad_w', 'grad_ple_norm_g': 'grad_w', 'grad_w_ple_gate': 'grad_w', 'grad_w_ple_proj': 'grad_w', 'grad_final_norm_g': 'grad_w', 'delta_mix_norm_g': 'delta_w', 'delta_w_in': 'delta_w', 'delta_sgu_w': 'delta_w', 'delta_sgu_b': 'delta_w', 'delta_sgu_norm_g': 'delta_w', 'delta_out_norm_a': 'delta_w', 'delta_out_norm_b': 'delta_w', 'delta_w_out': 'delta_w', 'delta_ffn_norm_g': 'delta_w', 'delta_w_gate': 'delta_w', 'delta_w_up': 'delta_w', 'delta_w_down': 'delta_w', 'delta_ple_norm_g': 'delta_w', 'delta_w_ple_gate': 'delta_w', 'delta_w_ple_proj': 'delta_w', 'delta_final_norm_g': 'delta_w', 'new_m_mix_norm_g': 'new_m', 'new_m_w_in': 'new_m', 'new_m_sgu_w': 'new_m', 'new_m_sgu_b': 'new_m', 'new_m_sgu_norm_g': 'new_m', 'new_m_out_norm_a': 'new_m', 'new_m_out_norm_b': 'new_m', 'new_m_w_out': 'new_m', 'new_m_ffn_norm_g': 'new_m', 'new_m_w_gate': 'new_m', 'new_m_w_up': 'new_m', 'new_m_w_down': 'new_m', 'new_m_ple_norm_g': 'new_m', 'new_m_w_ple_gate': 'new_m', 'new_m_w_ple_proj': 'new_m', 'new_m_final_norm_g': 'new_m', 'new_v_mix_norm_g': 'new_v', 'new_v_w_in': 'new_v', 'new_v_sgu_w': 'new_v', 'new_v_sgu_b': 'new_v', 'new_v_sgu_norm_g': 'new_v', 'new_v_out_norm_a': 'new_v', 'new_v_out_norm_b': 'new_v', 'new_v_w_out': 'new_v', 'new_v_ffn_norm_g': 'new_v', 'new_v_w_gate': 'new_v', 'new_v_w_up': 'new_v', 'new_v_w_down': 'new_v', 'new_v_ple_norm_g': 'new_v', 'new_v_w_ple_gate': 'new_v', 'new_v_w_ple_proj': 'new_v', 'new_v_final_norm_g': 'new_v'}


def _forward(args):
    return _fwd_reference(*[args[k] for k in FWD_PARAMS])


def _output_shape():
    def fwd():
        inp = _fwd_setup_inputs(0)
        return _fwd_reference(*[inp[k] for k in FWD_PARAMS])
    out = _jax.eval_shape(fwd)
    return out.shape, out.dtype

N_MICROBATCH = 1
ADAM_LR = 0.001
ADAM_B1 = 0.9
ADAM_B2 = 0.999
ADAM_EPS = 1e-08
ADAM_WD = 0.01
ADAM_STEP = 10
PER_EXAMPLE_BATCH_AXIS = {'x': 0, 'p': 1, 'loss_target': 0}
SHARED_INPUTS = []
_WEIGHT_DTYPES = {'mix_norm_g': _jnp.float32, 'w_in': _jnp.float32, 'sgu_w': _jnp.float32, 'sgu_b': _jnp.float32, 'sgu_norm_g': _jnp.float32, 'out_norm_a': _jnp.float32, 'out_norm_b': _jnp.float32, 'w_out': _jnp.float32, 'ffn_norm_g': _jnp.float32, 'w_gate': _jnp.float32, 'w_up': _jnp.float32, 'w_down': _jnp.float32, 'ple_norm_g': _jnp.float32, 'w_ple_gate': _jnp.float32, 'w_ple_proj': _jnp.float32, 'final_norm_g': _jnp.float32}
MOMENT_SCALE = {'mix_norm_g': 2.964415e-01, 'w_in': 1.816387e-01, 'sgu_w': 7.225616e-02, 'sgu_b': 9.549952e-02, 'sgu_norm_g': 1.023157e-01, 'out_norm_a': 1.800932e-01, 'out_norm_b': 1.999476e-01, 'w_out': 1.966042e-01, 'ffn_norm_g': 1.406700e-01, 'w_gate': 5.862776e-02, 'w_up': 5.774267e-02, 'w_down': 9.613099e-02, 'ple_norm_g': 3.807558e-02, 'w_ple_gate': 3.665349e-02, 'w_ple_proj': 8.678875e-02, 'final_norm_g': 6.417888e+01}


def _to_microbatches(a, axis):
    t = _jnp.moveaxis(a, axis, 0)
    t = t.reshape((N_MICROBATCH, t.shape[0] // N_MICROBATCH) + t.shape[1:])
    return _jnp.moveaxis(t, 1, axis + 1)


def setup_inputs(seed: int = 0) -> dict:
    inp = _fwd_setup_inputs(seed)
    key = _jax.random.fold_in(_jax.random.key(seed), 7919)
    shape, _ = _output_shape()
    out = dict(inp)
    out["loss_target"] = _jax.random.normal(_jax.random.fold_in(key, 0), shape, _jnp.float32)
    for i, name in enumerate(TWIN_WEIGHTS):
        w = inp[name].astype(_jnp.float32)
        if MOMENT_SCALE is None:
            s = _jnp.sqrt(_jnp.mean(_jnp.square(w)) + 1e-30)
        else:
            s = MOMENT_SCALE[name]
        km, kv = _jax.random.split(_jax.random.fold_in(key, i + 1))
        out[name] = w
        out["m_" + name] = s * _jax.random.normal(km, w.shape, _jnp.float32)
        out["v_" + name] = (s * s) * _jax.random.uniform(kv, w.shape, _jnp.float32, 0.5, 1.5)
    if N_MICROBATCH > 1:
        for name, axis in PER_EXAMPLE_BATCH_AXIS.items():
            out[name] = _to_microbatches(out[name], axis)
    return {'x': out['x'], 'p': out['p'], 'mix_norm_g': out['mix_norm_g'], 'w_in': out['w_in'], 'sgu_w': out['sgu_w'], 'sgu_b': out['sgu_b'], 'sgu_norm_g': out['sgu_norm_g'], 'out_norm_a': out['out_norm_a'], 'out_norm_b': out['out_norm_b'], 'w_out': out['w_out'], 'ffn_norm_g': out['ffn_norm_g'], 'w_gate': out['w_gate'], 'w_up': out['w_up'], 'w_down': out['w_down'], 'ple_norm_g': out['ple_norm_g'], 'w_ple_gate': out['w_ple_gate'], 'w_ple_proj': out['w_ple_proj'], 'final_norm_g': out['final_norm_g'], 'loss_target': out['loss_target'], 'm_mix_norm_g': out['m_mix_norm_g'], 'm_w_in': out['m_w_in'], 'm_sgu_w': out['m_sgu_w'], 'm_sgu_b': out['m_sgu_b'], 'm_sgu_norm_g': out['m_sgu_norm_g'], 'm_out_norm_a': out['m_out_norm_a'], 'm_out_norm_b': out['m_out_norm_b'], 'm_w_out': out['m_w_out'], 'm_ffn_norm_g': out['m_ffn_norm_g'], 'm_w_gate': out['m_w_gate'], 'm_w_up': out['m_w_up'], 'm_w_down': out['m_w_down'], 'm_ple_norm_g': out['m_ple_norm_g'], 'm_w_ple_gate': out['m_w_ple_gate'], 'm_w_ple_proj': out['m_w_ple_proj'], 'm_final_norm_g': out['m_final_norm_g'], 'v_mix_norm_g': out['v_mix_norm_g'], 'v_w_in': out['v_w_in'], 'v_sgu_w': out['v_sgu_w'], 'v_sgu_b': out['v_sgu_b'], 'v_sgu_norm_g': out['v_sgu_norm_g'], 'v_out_norm_a': out['v_out_norm_a'], 'v_out_norm_b': out['v_out_norm_b'], 'v_w_out': out['v_w_out'], 'v_ffn_norm_g': out['v_ffn_norm_g'], 'v_w_gate': out['v_w_gate'], 'v_w_up': out['v_w_up'], 'v_w_down': out['v_w_down'], 'v_ple_norm_g': out['v_ple_norm_g'], 'v_w_ple_gate': out['v_w_ple_gate'], 'v_w_ple_proj': out['v_w_ple_proj'], 'v_final_norm_g': out['v_final_norm_g']}


def _loss(weights, diff, rest, loss_target):
    with _jax.named_scope("forward"):
        args = {**rest, TWIN_DIFF_INPUT: diff, **{k: w.astype(_WEIGHT_DTYPES[k]) for k, w in weights.items()}}
        y = _forward(args)
    with _jax.named_scope("loss_head"):
        err = _jnp.square(y.astype(_jnp.float32) - loss_target)
        return 0.5 * _jnp.sum(_jnp.mean(err, axis=-1)) if err.ndim else 0.5 * err


def _adamw(w, g, m, v):
    m = ADAM_B1 * m + (1.0 - ADAM_B1) * g
    v = ADAM_B2 * v + (1.0 - ADAM_B2) * _jnp.square(g)
    m_hat = m / (1.0 - ADAM_B1 ** ADAM_STEP)
    v_hat = v / (1.0 - ADAM_B2 ** ADAM_STEP)
    delta = -ADAM_LR * (m_hat / (_jnp.sqrt(v_hat) + ADAM_EPS) + ADAM_WD * w)
    return delta, m, v


def reference(x, p, mix_norm_g, w_in, sgu_w, sgu_b, sgu_norm_g, out_norm_a, out_norm_b, w_out, ffn_norm_g, w_gate, w_up, w_down, ple_norm_g, w_ple_gate, w_ple_proj, final_norm_g, loss_target, m_mix_norm_g, m_w_in, m_sgu_w, m_sgu_b, m_sgu_norm_g, m_out_norm_a, m_out_norm_b, m_w_out, m_ffn_norm_g, m_w_gate, m_w_up, m_w_down, m_ple_norm_g, m_w_ple_gate, m_w_ple_proj, m_final_norm_g, v_mix_norm_g, v_w_in, v_sgu_w, v_sgu_b, v_sgu_norm_g, v_out_norm_a, v_out_norm_b, v_w_out, v_ffn_norm_g, v_w_gate, v_w_up, v_w_down, v_ple_norm_g, v_w_ple_gate, v_w_ple_proj, v_final_norm_g):
    given = dict(x=x, p=p, mix_norm_g=mix_norm_g, w_in=w_in, sgu_w=sgu_w, sgu_b=sgu_b, sgu_norm_g=sgu_norm_g, out_norm_a=out_norm_a, out_norm_b=out_norm_b, w_out=w_out, ffn_norm_g=ffn_norm_g, w_gate=w_gate, w_up=w_up, w_down=w_down, ple_norm_g=ple_norm_g, w_ple_gate=w_ple_gate, w_ple_proj=w_ple_proj, final_norm_g=final_norm_g, loss_target=loss_target, m_mix_norm_g=m_mix_norm_g, m_w_in=m_w_in, m_sgu_w=m_sgu_w, m_sgu_b=m_sgu_b, m_sgu_norm_g=m_sgu_norm_g, m_out_norm_a=m_out_norm_a, m_out_norm_b=m_out_norm_b, m_w_out=m_w_out, m_ffn_norm_g=m_ffn_norm_g, m_w_gate=m_w_gate, m_w_up=m_w_up, m_w_down=m_w_down, m_ple_norm_g=m_ple_norm_g, m_w_ple_gate=m_w_ple_gate, m_w_ple_proj=m_w_ple_proj, m_final_norm_g=m_final_norm_g, v_mix_norm_g=v_mix_norm_g, v_w_in=v_w_in, v_sgu_w=v_sgu_w, v_sgu_b=v_sgu_b, v_sgu_norm_g=v_sgu_norm_g, v_out_norm_a=v_out_norm_a, v_out_norm_b=v_out_norm_b, v_w_out=v_w_out, v_ffn_norm_g=v_ffn_norm_g, v_w_gate=v_w_gate, v_w_up=v_w_up, v_w_down=v_w_down, v_ple_norm_g=v_ple_norm_g, v_w_ple_gate=v_w_ple_gate, v_w_ple_proj=v_w_ple_proj, v_final_norm_g=v_final_norm_g)
    weights = {n: given[n] for n in TWIN_WEIGHTS}
    shared = {n: given[n] for n in SHARED_INPUTS}
    per_example = {n: given[n] for n in ['x', 'p']}
    grad_fn = _jax.value_and_grad(_loss, argnums=(0, 1))

    def one_microbatch(ex, loss_target):
        ex = dict(ex)
        diff = ex.pop(TWIN_DIFF_INPUT)
        return grad_fn(weights, diff, {**shared, **ex}, loss_target)

    if N_MICROBATCH == 1:
        loss, (grad_w, grad_x) = one_microbatch(per_example, given["loss_target"])
    else:
        def body(carry, xs):
            loss_sum, grad_sum = carry
            l_k, (gw_k, gx_k) = one_microbatch(xs[0], xs[1])
            with _jax.named_scope("update"):
                return (loss_sum + l_k, _jax.tree.map(_jnp.add, grad_sum, gw_k)), gx_k

        init = (_jnp.zeros((), _jnp.float32), _jax.tree.map(_jnp.zeros_like, weights))
        (loss, grad_w), grad_x = _jax.lax.scan(body, init, (per_example, given["loss_target"]))
    with _jax.named_scope("update"):
        delta_w, new_m, new_v = {}, {}, {}
        for n in TWIN_WEIGHTS:
            delta_w[n], new_m[n], new_v[n] = _adamw(weights[n], grad_w[n], given["m_" + n], given["v_" + n])
    return (loss, grad_x, *[grad_w[n] for n in TWIN_WEIGHTS], *[delta_w[n] for n in TWIN_WEIGHTS],
            *[new_m[n] for n in TWIN_WEIGHTS], *[new_v[n] for n in TWIN_WEIGHTS])
```

```python
import functools

import jax
import jax.numpy as jnp
from jax import lax
from jax.experimental import pallas as pl
from jax.experimental.pallas import tpu as pltpu

F32 = jnp.float32
BF16 = jnp.bfloat16

D_MODEL = 1024
WIDTH_A = 256
WIDTH_B = 768
D_FF = 2816
IN_COLS = 2 * WIDTH_A + 3 * WIDTH_B
PLE_DIM = 256
HEAD_DIM = 64
CHUNK = 128
N_BACK = 128
DILATIONS = (1, 4, 16)
ROPE_THETA = 10000.0
EPS = 1e-6
N_DEV = 8

ADAM_LR = 0.001
ADAM_B1 = 0.9
ADAM_B2 = 0.999
ADAM_EPS = 1e-08
ADAM_WD = 0.01
ADAM_STEP = 10

V7X_VMEM_LIMIT_BYTES = 56 * 1024 * 1024
ROW_TILE = 512
MESH = pl.DeviceIdType.MESH
NEG = -1e30

_NT = (((1,), (1,)), ((), ()))
_TN = (((0,), (0,)), ((), ()))


def _params(*semantics):
    return pltpu.CompilerParams(dimension_semantics=semantics, vmem_limit_bytes=V7X_VMEM_LIMIT_BYTES)


def _rows(tm, width):
    return pl.BlockSpec((tm, width), lambda i: (i, 0))


def _whole(shape):
    return pl.BlockSpec(shape, lambda *_: (0,) * len(shape))


def _gelu(x):
    t = jnp.tanh(0.7978845608028654 * (x + 0.044715 * (x * x * x)))
    return 0.5 * x * (1.0 + t)


def _gelu_grad(x):
    t = jnp.tanh(0.7978845608028654 * (x + 0.044715 * (x * x * x)))
    return 0.5 * (1.0 + t) + 0.5 * x * (1.0 - t * t) * (0.7978845608028654 * (1.0 + 3.0 * 0.044715 * (x * x)))


def _rstd(x):
    return lax.rsqrt(jnp.mean(x * x, axis=-1, keepdims=True) + EPS)


def _norm_bwd(dn, h, g):
    r = _rstd(h)
    n = h * r
    t = dn * g
    return r * (t - n * jnp.mean(t * n, axis=-1, keepdims=True)), dn * n


def _swap_halves(x, first_half):
    return jnp.where(first_half, pltpu.roll(x, 96, 1), pltpu.roll(x, 32, 1))


def _inproj(x, g, w, cos, sin):
    s_len = x.shape[0]
    tm = ROW_TILE

    def body(x_ref, g_ref, w_ref, cos_ref, sin_ref, ua_ref, qk_ref, v_ref, hn_ref):
        xf = x_ref[...]
        hn = (xf * _rstd(xf) * g_ref[...]).astype(BF16)
        hn_ref[...] = hn
        c = cos_ref[...]
        s = sin_ref[...]
        first_half = (lax.broadcasted_iota(jnp.int32, (tm, 128), 1) % HEAD_DIM) < HEAD_DIM // 2
        for j in range(IN_COLS // 256):
            col = j * 256
            acc = jnp.dot(hn, w_ref[:, col:col + 256], preferred_element_type=F32)
            if col < 2 * WIDTH_A:
                ua_ref[:, col:col + 256] = acc
            elif col < 2 * WIDTH_A + 2 * WIDTH_B:
                scale = 0.125 if col < 2 * WIDTH_A + WIDTH_B else 1.0
                for half in range(2):
                    t = acc[:, half * 128:(half + 1) * 128]
                    y = (t * c + _swap_halves(t, first_half) * s) * scale
                    o = col - 2 * WIDTH_A + half * 128
                    qk_ref[:, o:o + 128] = y.astype(BF16)
            else:
                o = col - 2 * WIDTH_A - 2 * WIDTH_B
                v_ref[:, o:o + 256] = acc.astype(BF16)

    return pl.pallas_call(
        body, name="inproj", grid=(s_len // tm,),
        in_specs=[_rows(tm, D_MODEL), _whole((1, D_MODEL)), _whole((D_MODEL, IN_COLS)), _rows(tm, 128), _rows(tm, 128)],
        out_specs=[_rows(tm, 2 * WIDTH_A), _rows(tm, 2 * WIDTH_B), _rows(tm, WIDTH_B), _rows(tm, D_MODEL)],
        out_shape=[jax.ShapeDtypeStruct((s_len, 2 * WIDTH_A), F32), jax.ShapeDtypeStruct((s_len, 2 * WIDTH_B), BF16),
                   jax.ShapeDtypeStruct((s_len, WIDTH_B), BF16), jax.ShapeDtypeStruct((s_len, D_MODEL), BF16)],
        compiler_params=_params("parallel"),
    )(x, g, w, cos, sin)


def _sgu_mix_weights(sw_ref):
    keep = lax.broadcasted_iota(jnp.int32, (CHUNK, CHUNK), 0) >= lax.broadcasted_iota(jnp.int32, (CHUNK, CHUNK), 1)
    return [jnp.where(keep, sw_ref[h], 0.0).astype(BF16) for h in range(4)], keep


def _sgu_core(ua_ref, gs_ref):
    u = ua_ref[:, :WIDTH_A]
    va = ua_ref[:, WIDTH_A:]
    vg = _gelu(va)
    xc = vg - jnp.mean(vg, axis=-1, keepdims=True)
    rstd = lax.rsqrt(jnp.mean(xc * xc, axis=-1, keepdims=True) + EPS)
    xhat = xc * rstd
    return u, va, _gelu(u), xhat, rstd, xhat * gs_ref[...]


def _sgu_fwd(ua, sw, b2, gs, ga):
    s_len = ua.shape[0]
    tm = ROW_TILE

    def body(ua_ref, sw_ref, b2_ref, gs_ref, ga_ref, out_ref):
        _, _, ug, _, _, vn = _sgu_core(ua_ref, gs_ref)
        wm, _ = _sgu_mix_weights(sw_ref)
        head = lax.broadcasted_iota(jnp.int32, (CHUNK, WIDTH_A), 1) // HEAD_DIM
        for c in range(tm // CHUNK):
            rows = slice(c * CHUNK, (c + 1) * CHUNK)
            vnc = vn[rows]
            mixed = b2_ref[...]
            for h in range(4):
                mixed = mixed + jnp.dot(wm[h], jnp.where(head == h, vnc, 0.0).astype(BF16), preferred_element_type=F32)
            ya = ug[rows] * mixed
            out_ref[rows, :] = (ya * _rstd(ya) * ga_ref[...]).astype(BF16)

    return pl.pallas_call(
        body, name="sgu_fwd", grid=(s_len // tm,),
        in_specs=[_rows(tm, 2 * WIDTH_A), _whole((4, CHUNK, CHUNK)), _whole((CHUNK, WIDTH_A)), _whole((1, WIDTH_A)), _whole((1, WIDTH_A))],
        out_specs=_rows(tm, WIDTH_A),
        out_shape=jax.ShapeDtypeStruct((s_len, WIDTH_A), BF16),
        compiler_params=_params("parallel"),
    )(ua, sw, b2, gs, ga)


def _attn_geometry(s_len, d):
    sd = s_len // d
    tile = min(ROW_TILE, sd)
    return sd, tile, tile // CHUNK, sd // tile


def _attn_fwd(qk, v, d):
    s_len = qk.shape[0]
    sd, tile, nb, n_tiles = _attn_geometry(s_len, d)
    qk_v = qk.reshape(sd, d * 2 * WIDTH_B)
    v_v = v.reshape(sd, d * WIDTH_B)

    def qcol(g):
        return (g // 6) * 12 + g % 6

    def prev(n):
        return jnp.maximum(n * nb - 1, 0)

    def body(q_ref, k_ref, kp_ref, v_ref, vp_ref, o_ref, l_ref):
        n = pl.program_id(1)
        head_a = lax.broadcasted_iota(jnp.int32, (CHUNK, 128), 1) < HEAD_DIM
        qi = lax.broadcasted_iota(jnp.int32, (CHUNK, 2 * CHUNK), 0)
        kc = lax.broadcasted_iota(jnp.int32, (CHUNK, 2 * CHUNK), 1)
        band = (kc >= qi) & (kc <= qi + N_BACK)
        for j in range(nb):
            rows = slice(j * CHUNK, (j + 1) * CHUNK)
            q = q_ref[rows, :]
            if j == 0:
                kcat = jnp.concatenate([kp_ref[...], k_ref[rows, :]], axis=0)
                vcat = jnp.concatenate([vp_ref[...], v_ref[rows, :]], axis=0)
                valid = band & jnp.logical_or(n > 0, kc >= CHUNK)
            else:
                kcat = k_ref[(j - 1) * CHUNK:(j + 1) * CHUNK, :]
                vcat = v_ref[(j - 1) * CHUNK:(j + 1) * CHUNK, :]
                valid = band
            outs, lses = [], []
            for hm in (head_a, jnp.logical_not(head_a)):
                qm = jnp.where(hm, q, jnp.zeros_like(q))
                s = lax.dot_general(qm, kcat, _NT, preferred_element_type=F32)
                s = jnp.where(valid, s, NEG)
                m = jnp.max(s, axis=-1, keepdims=True)
                p = jnp.exp(s - m)
                l = jnp.sum(p, axis=-1, keepdims=True)
                outs.append(jnp.dot(p.astype(BF16), vcat, preferred_element_type=F32) / l)
                lses.append(m + jnp.log(l))
            o_ref[rows, :] = jnp.where(head_a, outs[0], outs[1]).astype(BF16)
            l_ref[rows, :] = jnp.where(head_a, lses[0], lses[1])

    o, lse = pl.pallas_call(
        body, name=f"attn_fwd_d{d}", grid=(d * 6, n_tiles),
        in_specs=[pl.BlockSpec((tile, 128), lambda g, n: (n, qcol(g))),
                  pl.BlockSpec((tile, 128), lambda g, n: (n, qcol(g) + 6)),
                  pl.BlockSpec((CHUNK, 128), lambda g, n: (prev(n), qcol(g) + 6)),
                  pl.BlockSpec((tile, 128), lambda g, n: (n, g)),
                  pl.BlockSpec((CHUNK, 128), lambda g, n: (prev(n), g))],
        out_specs=[pl.BlockSpec((tile, 128), lambda g, n: (n, g)), pl.BlockSpec((tile, 128), lambda g, n: (n, g))],
        out_shape=[jax.ShapeDtypeStruct((sd, d * WIDTH_B), BF16), jax.ShapeDtypeStruct((sd, d * WIDTH_B), F32)],
        compiler_params=_params("parallel", "parallel"),
    )(qk_v, qk_v, qk_v, v_v, v_v)
    return o.reshape(s_len, WIDTH_B), lse.reshape(s_len, WIDTH_B)


def _combine(outs, lses, ya_n, gb):
    s_len = ya_n.shape[0]
    tm = ROW_TILE

    def body(o1, o2, o3, l1, l2, l3, ya_ref, gb_ref, y_ref, yb_ref, lse_ref):
        ls = [l1[...], l2[...], l3[...]]
        top = jnp.maximum(jnp.maximum(ls[0], ls[1]), ls[2])
        ws = [jnp.exp(l - top) for l in ls]
        den = ws[0] + ws[1] + ws[2]
        inv = 1.0 / den
        yb = (ws[0] * inv) * o1[...].astype(F32) + (ws[1] * inv) * o2[...].astype(F32) + (ws[2] * inv) * o3[...].astype(F32)
        yb_ref[...] = yb
        lse_ref[...] = top + jnp.log(den)
        y_ref[:, :WIDTH_A] = ya_ref[...]
        y_ref[:, WIDTH_A:] = (yb * _rstd(yb) * gb_ref[...]).astype(BF16)

    return pl.pallas_call(
        body, name="attn_combine", grid=(s_len // tm,),
        in_specs=[_rows(tm, WIDTH_B)] * 6 + [_rows(tm, WIDTH_A), _whole((1, WIDTH_B))],
        out_specs=[_rows(tm, D_MODEL), _rows(tm, WIDTH_B), _rows(tm, WIDTH_B)],
        out_shape=[jax.ShapeDtypeStruct((s_len, D_MODEL), BF16), jax.ShapeDtypeStruct((s_len, WIDTH_B), F32),
                   jax.ShapeDtypeStruct((s_len, WIDTH_B), F32)],
        compiler_params=_params("parallel"),
    )(*outs, *lses, ya_n, gb)


def _mm_res(a, w, res, name):
    s_len, k = a.shape
    tm = ROW_TILE if k <= D_MODEL else ROW_TILE // 2

    def body(a_ref, w_ref, r_ref, o_ref):
        o_ref[...] = r_ref[...] + jnp.dot(a_ref[...].astype(BF16), w_ref[...], preferred_element_type=F32)

    return pl.pallas_call(
        body, name=name, grid=(s_len // tm,),
        in_specs=[_rows(tm, k), _whole((k, D_MODEL)), _rows(tm, D_MODEL)],
        out_specs=_rows(tm, D_MODEL),
        out_shape=jax.ShapeDtypeStruct((s_len, D_MODEL), F32),
        compiler_params=_params("parallel"),
    )(a, w, res)


def _ffn_up(h, g, wg, wu):
    s_len = h.shape[0]
    tm = ROW_TILE // 2

    def body(h_ref, g_ref, wg_ref, wu_ref, a_ref, b_ref, act_ref, hn_ref):
        hf = h_ref[...]
        hn = (hf * _rstd(hf) * g_ref[...]).astype(BF16)
        hn_ref[...] = hn
        for j in range(D_FF // 256):
            cols = slice(j * 256, (j + 1) * 256)
            a = jnp.dot(hn, wg_ref[:, cols], preferred_element_type=F32)
            b = jnp.dot(hn, wu_ref[:, cols], preferred_element_type=F32)
            a_ref[:, cols] = a.astype(BF16)
            b_ref[:, cols] = b.astype(BF16)
            act_ref[:, cols] = (a * jax.nn.sigmoid(a) * b).astype(BF16)

    wide = jax.ShapeDtypeStruct((s_len, D_FF), BF16)
    return pl.pallas_call(
        body, name="ffn_up", grid=(s_len // tm,),
        in_specs=[_rows(tm, D_MODEL), _whole((1, D_MODEL)), _whole((D_MODEL, D_FF)), _whole((D_MODEL, D_FF))],
        out_specs=[_rows(tm, D_FF), _rows(tm, D_FF), _rows(tm, D_FF), _rows(tm, D_MODEL)],
        out_shape=[wide, wide, wide, jax.ShapeDtypeStruct((s_len, D_MODEL), BF16)],
        compiler_params=_params("parallel"),
    )(h, g, wg, wu)


def _ple(h, g, wpg, p, wpp):
    s_len = h.shape[0]
    tm = ROW_TILE

    def body(h_ref, g_ref, wpg_ref, p_ref, wpp_ref, h3_ref, gate_ref, pp_ref, hn_ref):
        hf = h_ref[...]
        hn = (hf * _rstd(hf) * g_ref[...]).astype(BF16)
        hn_ref[...] = hn
        gate = jax.nn.sigmoid(jnp.dot(hn, wpg_ref[...], preferred_element_type=F32))
        pp = jnp.dot(p_ref[...].astype(BF16), wpp_ref[...], preferred_element_type=F32)
        h3_ref[...] = hf + gate * pp
        gate_ref[...] = gate.astype(BF16)
        pp_ref[...] = pp.astype(BF16)

    half = jax.ShapeDtypeStruct((s_len, D_MODEL), BF16)
    return pl.pallas_call(
        body, name="ple", grid=(s_len // tm,),
        in_specs=[_rows(tm, D_MODEL), _whole((1, D_MODEL)), _whole((D_MODEL, D_MODEL)), _rows(tm, PLE_DIM), _whole((PLE_DIM, D_MODEL))],
        out_specs=[_rows(tm, D_MODEL)] * 4,
        out_shape=[jax.ShapeDtypeStruct((s_len, D_MODEL), F32), half, half, half],
        compiler_params=_params("parallel"),
    )(h, g, wpg, p, wpp)


def _loss_bwd(h3, target, gf, gate, pp):
    s_len = h3.shape[0]
    tm = ROW_TILE

    def body(h_ref, t_ref, g_ref, gate_ref, pp_ref, dh_ref, dz_ref, dpp_ref, loss_ref, dg_ref):
        @pl.when(pl.program_id(0) == 0)
        def _():
            loss_ref[...] = jnp.zeros_like(loss_ref)
            dg_ref[...] = jnp.zeros_like(dg_ref)

        hf = h_ref[...]
        gfv = g_ref[...]
        err = hf * _rstd(hf) * gfv - t_ref[...]
        loss_ref[...] += 0.5 * jnp.sum(jnp.sum(err * err, axis=-1, keepdims=True), axis=0, keepdims=True) / D_MODEL
        dh, dg_rows = _norm_bwd(err / D_MODEL, hf, gfv)
        dg_ref[...] += jnp.sum(dg_rows, axis=0, keepdims=True)
        dh_ref[...] = dh
        gate = gate_ref[...].astype(F32)
        dz_ref[...] = (dh * pp_ref[...].astype(F32) * gate * (1.0 - gate)).astype(BF16)
        dpp_ref[...] = (dh * gate).astype(BF16)

    half = jax.ShapeDtypeStruct((s_len, D_MODEL), BF16)
    return pl.pallas_call(
        body, name="loss_bwd", grid=(s_len // tm,),
        in_specs=[_rows(tm, D_MODEL), _rows(tm, D_MODEL), _whole((1, D_MODEL)), _rows(tm, D_MODEL), _rows(tm, D_MODEL)],
        out_specs=[_rows(tm, D_MODEL), _rows(tm, D_MODEL), _rows(tm, D_MODEL), _whole((1, 128)), _whole((1, D_MODEL))],
        out_shape=[jax.ShapeDtypeStruct((s_len, D_MODEL), F32), half, half,
                   jax.ShapeDtypeStruct((1, 128), F32), jax.ShapeDtypeStruct((1, D_MODEL), F32)],
        compiler_params=_params("arbitrary"),
    )(h3, target, gf, gate, pp)


def _mm_norm_bwd(parts, h, g, dres, name):
    s_len = h.shape[0]
    tm = ROW_TILE // 2
    n_parts = len(parts)

    def body(*refs):
        a_refs = refs[0:2 * n_parts:2]
        w_refs = refs[1:2 * n_parts:2]
        h_ref, g_ref, r_ref, o_ref, dg_ref = refs[2 * n_parts:]

        @pl.when(pl.program_id(0) == 0)
        def _():
            dg_ref[...] = jnp.zeros_like(dg_ref)

        dn = jnp.dot(a_refs[0][...], w_refs[0][...], preferred_element_type=F32)
        for a_ref, w_ref in zip(a_refs[1:], w_refs[1:]):
            dn = dn + jnp.dot(a_ref[...], w_ref[...], preferred_element_type=F32)
        dh, dg_rows = _norm_bwd(dn, h_ref[...], g_ref[...])
        o_ref[...] = r_ref[...] + dh
        dg_ref[...] += jnp.sum(dg_rows, axis=0, keepdims=True)

    in_specs, args = [], []
    for a, w in parts:
        in_specs += [_rows(tm, a.shape[1]), _whole(w.shape)]
        args += [a, w]
    return pl.pallas_call(
        body, name=name, grid=(s_len // tm,),
        in_specs=in_specs + [_rows(tm, D_MODEL), _whole((1, D_MODEL)), _rows(tm, D_MODEL)],
        out_specs=[_rows(tm, D_MODEL), _whole((1, D_MODEL))],
        out_shape=[jax.ShapeDtypeStruct((s_len, D_MODEL), F32), jax.ShapeDtypeStruct((1, D_MODEL), F32)],
        compiler_params=_params("arbitrary"),
    )(*args, h, g, dres)


def _ffn_down_bwd(dh, wdt, a, b):
    s_len = dh.shape[0]
    tm = ROW_TILE // 2

    def body(dh_ref, w_ref, a_ref, b_ref, da_ref, db_ref):
        dhb = dh_ref[...].astype(BF16)
        for j in range(D_FF // 256):
            cols = slice(j * 256, (j + 1) * 256)
            dact = jnp.dot(dhb, w_ref[:, cols], preferred_element_type=F32)
            av = a_ref[:, cols].astype(F32)
            bv = b_ref[:, cols].astype(F32)
            sig = jax.nn.sigmoid(av)
            da_ref[:, cols] = (dact * bv * sig * (1.0 + av * (1.0 - sig))).astype(BF16)
            db_ref[:, cols] = (dact * av * sig).astype(BF16)

    wide = jax.ShapeDtypeStruct((s_len, D_FF), BF16)
    return pl.pallas_call(
        body, name="ffn_down_bwd", grid=(s_len // tm,),
        in_specs=[_rows(tm, D_MODEL), _whole((D_MODEL, D_FF)), _rows(tm, D_FF), _rows(tm, D_FF)],
        out_specs=[_rows(tm, D_FF), _rows(tm, D_FF)],
        out_shape=[wide, wide],
        compiler_params=_params("parallel"),
    )(dh, wdt, a, b)


def _outproj_bwd(dh1, woutt, yb, gb, head_sum):
    s_len = dh1.shape[0]
    tm = ROW_TILE // 2

    def body(dh_ref, w_ref, yb_ref, gb_ref, e_ref, dya_ref, dyb_ref, dd_ref, dgb_ref):
        @pl.when(pl.program_id(0) == 0)
        def _():
            dgb_ref[...] = jnp.zeros_like(dgb_ref)

        dhb = dh_ref[...].astype(BF16)
        dya_ref[...] = jnp.dot(dhb, w_ref[:, :WIDTH_A], preferred_element_type=F32)
        dyn = jnp.dot(dhb, w_ref[:, WIDTH_A:], preferred_element_type=F32)
        ybv = yb_ref[...]
        dyb, dg_rows = _norm_bwd(dyn, ybv, gb_ref[...])
        dgb_ref[...] += jnp.sum(dg_rows, axis=0, keepdims=True)
        dyb_ref[...] = dyb.astype(BF16)
        prod = dyb * ybv
        hi = prod.astype(BF16)
        lo = (prod - hi.astype(F32)).astype(BF16)
        dd_ref[...] = (jnp.dot(hi, e_ref[...], preferred_element_type=F32)
                       + jnp.dot(lo, e_ref[...], preferred_element_type=F32))

    return pl.pallas_call(
        body, name="outproj_bwd", grid=(s_len // tm,),
        in_specs=[_rows(tm, D_MODEL), _whole((D_MODEL, D_MODEL)), _rows(tm, WIDTH_B), _whole((1, WIDTH_B)), _whole((WIDTH_B, WIDTH_B))],
        out_specs=[_rows(tm, WIDTH_A), _rows(tm, WIDTH_B), _rows(tm, WIDTH_B), _whole((1, WIDTH_B))],
        out_shape=[jax.ShapeDtypeStruct((s_len, WIDTH_A), F32), jax.ShapeDtypeStruct((s_len, WIDTH_B), BF16),
                   jax.ShapeDtypeStruct((s_len, WIDTH_B), F32), jax.ShapeDtypeStruct((1, WIDTH_B), F32)],
        compiler_params=_params("arbitrary"),
    )(dh1, woutt, yb, gb, head_sum)


def _attn_bwd(qk, v, dyb, lse, dd, d):
    s_len = qk.shape[0]
    sd, tile, nb, n_tiles = _attn_geometry(s_len, d)
    qk_v = qk.reshape(sd, d * 2 * WIDTH_B)
    v_v = v.reshape(sd, d * WIDTH_B)
    do_v = dyb.reshape(sd, d * WIDTH_B)
    lse_v = lse.reshape(sd, d * WIDTH_B)
    dd_v = dd.reshape(sd, d * WIDTH_B)
    last_block = sd // CHUNK - 1

    def qcol(g):
        return (g // 6) * 12 + g % 6

    def nxt(n):
        return jnp.minimum((n + 1) * nb, last_block)

    def body(q_ref, qn_ref, k_ref, v_ref, do_ref, don_ref, l_ref, ln_ref, dd_ref, ddn_ref,
             dq_ref, dk_ref, dv_ref, carry_ref):
        n = pl.program_id(1)

        @pl.when(n == 0)
        def _():
            carry_ref[...] = jnp.zeros_like(carry_ref)

        head_a = lax.broadcasted_iota(jnp.int32, (CHUNK, 128), 1) < HEAD_DIM
        qi = lax.broadcasted_iota(jnp.int32, (CHUNK, CHUNK), 0)
        ki = lax.broadcasted_iota(jnp.int32, (CHUNK, CHUNK), 1)
        same_mask = qi >= ki
        after_mask = ki >= qi
        has_next = n < n_tiles - 1
        dq_acc = [carry_ref[...]] + [jnp.zeros((CHUNK, 128), F32) for _ in range(nb)]
        for j in range(nb):
            rows = slice(j * CHUNK, (j + 1) * CHUNK)
            kj = k_ref[rows, :]
            vj = v_ref[rows, :]
            dk = jnp.zeros((CHUNK, 128), F32)
            dv = jnp.zeros((CHUNK, 128), F32)
            for which in range(2):
                if which == 0:
                    qx, dox, lx, ddx, mask = q_ref[rows, :], do_ref[rows, :], l_ref[rows, :], dd_ref[rows, :], same_mask
                elif j + 1 < nb:
                    nrows = slice((j + 1) * CHUNK, (j + 2) * CHUNK)
                    qx, dox, lx, ddx, mask = q_ref[nrows, :], do_ref[nrows, :], l_ref[nrows, :], dd_ref[nrows, :], after_mask
                else:
                    qx, dox, lx, ddx = qn_ref[...], don_ref[...], ln_ref[...], ddn_ref[...]
                    mask = after_mask & has_next
                for hh, hm in enumerate((head_a, jnp.logical_not(head_a))):
                    qm = jnp.where(hm, qx, jnp.zeros_like(qx))
                    dom = jnp.where(hm, dox, jnp.zeros_like(dox))
                    km = jnp.where(hm, kj, jnp.zeros_like(kj))
                    lane0 = hh * HEAD_DIM
                    s = lax.dot_general(qm, kj, _NT, preferred_element_type=F32)
                    p = jnp.exp(jnp.where(mask, s - lx[:, lane0:lane0 + 1], NEG))
                    dp = lax.dot_general(dom, vj, _NT, preferred_element_type=F32)
                    ds = (p * (dp - ddx[:, lane0:lane0 + 1])).astype(BF16)
                    dv = dv + lax.dot_general(p.astype(BF16), dom, _TN, preferred_element_type=F32)
                    dk = dk + lax.dot_general(ds, qm, _TN, preferred_element_type=F32)
                    dq_acc[j + which] = dq_acc[j + which] + jnp.dot(ds, km, preferred_element_type=F32)
            dk_ref[rows, :] = dk
            dv_ref[rows, :] = dv
        for j in range(nb):
            dq_ref[j * CHUNK:(j + 1) * CHUNK, :] = dq_acc[j]
        carry_ref[...] = dq_acc[nb]

    def main(col):
        return pl.BlockSpec((tile, 128), lambda g, n: (n, col(g)))

    def after(col):
        return pl.BlockSpec((CHUNK, 128), lambda g, n: (nxt(n), col(g)))

    ident = lambda g: g
    kcol = lambda g: qcol(g) + 6
    grad = jax.ShapeDtypeStruct((sd, d * WIDTH_B), F32)
    dq, dk, dv = pl.pallas_call(
        body, name=f"attn_bwd_d{d}", grid=(d * 6, n_tiles),
        in_specs=[main(qcol), after(qcol), main(kcol), main(ident), main(ident), after(ident),
                  main(ident), after(ident), main(ident), after(ident)],
        out_specs=[main(ident)] * 3,
        out_shape=[grad, grad, grad],
        scratch_shapes=[pltpu.VMEM((CHUNK, 128), F32)],
        compiler_params=_params("parallel", "arbitrary"),
    )(qk_v, qk_v, qk_v, v_v, do_v, do_v, lse_v, lse_v, dd_v, dd_v)
    return dq.reshape(s_len, WIDTH_B), dk.reshape(s_len, WIDTH_B), dv.reshape(s_len, WIDTH_B)


def _sgu_bwd(ua, sw, b2, gs, ga, dya_n):
    s_len = ua.shape[0]
    tm = ROW_TILE

    def body(ua_ref, sw_ref, b2_ref, gs_ref, ga_ref, dy_ref, dua_ref, dsw_ref, db2_ref, dgs_ref, dga_ref):
        @pl.when(pl.program_id(0) == 0)
        def _():
            dsw_ref[...] = jnp.zeros_like(dsw_ref)
            db2_ref[...] = jnp.zeros_like(db2_ref)
            dgs_ref[...] = jnp.zeros_like(dgs_ref)
            dga_ref[...] = jnp.zeros_like(dga_ref)

        u, va, ug, xhat, rstd, vn = _sgu_core(ua_ref, gs_ref)
        wm, keep = _sgu_mix_weights(sw_ref)
        head = lax.broadcasted_iota(jnp.int32, (CHUNK, WIDTH_A), 1) // HEAD_DIM
        gav = ga_ref[...]
        gsv = gs_ref[...]
        dga = jnp.zeros((1, WIDTH_A), F32)
        dgs = jnp.zeros((1, WIDTH_A), F32)
        db2 = jnp.zeros((CHUNK, WIDTH_A), F32)
        dsw = [jnp.zeros((CHUNK, CHUNK), F32) for _ in range(4)]
        for c in range(tm // CHUNK):
            rows = slice(c * CHUNK, (c + 1) * CHUNK)
            vnc = vn[rows]
            vnb = vnc.astype(BF16)
            mixed = b2_ref[...]
            for h in range(4):
                mixed = mixed + jnp.dot(wm[h], jnp.where(head == h, vnc, 0.0).astype(BF16), preferred_element_type=F32)
            ugc = ug[rows]
            dya, dga_rows = _norm_bwd(dy_ref[rows, :], ugc * mixed, gav)
            dga = dga + jnp.sum(dga_rows, axis=0, keepdims=True)
            dmixed = dya * ugc
            db2 = db2 + dmixed
            dvn = jnp.zeros((CHUNK, WIDTH_A), F32)
            for h in range(4):
                dmh = jnp.where(head == h, dmixed, 0.0).astype(BF16)
                dsw[h] = dsw[h] + lax.dot_general(dmh, vnb, _NT, preferred_element_type=F32)
                dvn = dvn + lax.dot_general(wm[h], dmh, _TN, preferred_element_type=F32)
            xh = xhat[rows]
            dgs = dgs + jnp.sum(dvn * xh, axis=0, keepdims=True)
            dxh = dvn * gsv
            dvg = rstd[rows] * (dxh - jnp.mean(dxh, axis=-1, keepdims=True) - xh * jnp.mean(dxh * xh, axis=-1, keepdims=True))
            dua_ref[rows, :WIDTH_A] = (dya * mixed * _gelu_grad(u[rows])).astype(BF16)
            dua_ref[rows, WIDTH_A:] = (dvg * _gelu_grad(va[rows])).astype(BF16)
        for h in range(4):
            dsw_ref[h] += jnp.where(keep, dsw[h], 0.0)
        db2_ref[...] += db2
        dgs_ref[...] += dgs
        dga_ref[...] += dga

    return pl.pallas_call(
        body, name="sgu_bwd", grid=(s_len // tm,),
        in_specs=[_rows(tm, 2 * WIDTH_A), _whole((4, CHUNK, CHUNK)), _whole((CHUNK, WIDTH_A)), _whole((1, WIDTH_A)),
                  _whole((1, WIDTH_A)), _rows(tm, WIDTH_A)],
        out_specs=[_rows(tm, 2 * WIDTH_A), _whole((4, CHUNK, CHUNK)), _whole((CHUNK, WIDTH_A)), _whole((1, WIDTH_A)), _whole((1, WIDTH_A))],
        out_shape=[jax.ShapeDtypeStruct((s_len, 2 * WIDTH_A), BF16), jax.ShapeDtypeStruct((4, CHUNK, CHUNK), F32),
                   jax.ShapeDtypeStruct((CHUNK, WIDTH_A), F32), jax.ShapeDtypeStruct((1, WIDTH_A), F32),
                   jax.ShapeDtypeStruct((1, WIDTH_A), F32)],
        compiler_params=_params("arbitrary"),
    )(ua, sw, b2, gs, ga, dya_n)


def _dproj(dua, dqs, dks, dvs, cos, sin):
    s_len = dua.shape[0]
    tm = ROW_TILE

    def body(dua_ref, q1, q2, q3, k1, k2, k3, v1, v2, v3, cos_ref, sin_ref, out_ref):
        out_ref[:, :2 * WIDTH_A] = dua_ref[...]
        c = cos_ref[...]
        s = sin_ref[...]
        first_half = (lax.broadcasted_iota(jnp.int32, (tm, 128), 1) % HEAD_DIM) < HEAD_DIM // 2
        for base, refs, scale in ((2 * WIDTH_A, (q1, q2, q3), 0.125), (2 * WIDTH_A + WIDTH_B, (k1, k2, k3), 1.0)):
            for j in range(WIDTH_B // 128):
                cols = slice(j * 128, (j + 1) * 128)
                t = refs[0][:, cols] + refs[1][:, cols] + refs[2][:, cols]
                y = (t * c - _swap_halves(t, first_half) * s) * scale
                out_ref[:, base + j * 128:base + (j + 1) * 128] = y.astype(BF16)
        out_ref[:, 2 * WIDTH_A + 2 * WIDTH_B:] = (v1[...] + v2[...] + v3[...]).astype(BF16)

    return pl.pallas_call(
        body, name="dproj", grid=(s_len // tm,),
        in_specs=[_rows(tm, 2 * WIDTH_A)] + [_rows(tm, WIDTH_B)] * 9 + [_rows(tm, 128), _rows(tm, 128)],
        out_specs=_rows(tm, IN_COLS),
        out_shape=jax.ShapeDtypeStruct((s_len, IN_COLS), BF16),
        compiler_params=_params("parallel"),
    )(dua, *dqs, *dks, *dvs, cos, sin)


def _mm_tn(a, b, name):
    s_len, m = a.shape
    n = b.shape[1]
    tk = ROW_TILE
    tm = m if m <= 512 else (1408 if m == D_FF else 512)
    n_k = s_len // tk

    def body(a_ref, b_ref, o_ref, acc_ref):
        k = pl.program_id(1)

        @pl.when(k == 0)
        def _():
            acc_ref[...] = jnp.zeros_like(acc_ref)

        acc_ref[...] += lax.dot_general(a_ref[...].astype(BF16), b_ref[...].astype(BF16), _TN, preferred_element_type=F32)

        @pl.when(k == n_k - 1)
        def _():
            o_ref[...] = acc_ref[...].astype(BF16)

    return pl.pallas_call(
        body, name=name, grid=(m // tm, n_k),
        in_specs=[pl.BlockSpec((tk, tm), lambda i, k: (k, i)), pl.BlockSpec((tk, n), lambda i, k: (k, 0))],
        out_specs=pl.BlockSpec((tm, n), lambda i, k: (i, 0)),
        out_shape=jax.ShapeDtypeStruct((m, n), BF16),
        scratch_shapes=[pltpu.VMEM((tm, n), F32)],
        compiler_params=_params("parallel", "arbitrary"),
    )(a, b)


def _position():
    x, y, c = lax.axis_index("x"), lax.axis_index("y"), lax.axis_index("c")
    return x, y, c, 4 * x + 2 * y + c


def _peer(x, y, c, rel):
    return (x ^ ((rel >> 2) & 1), y ^ ((rel >> 1) & 1), c ^ (rel & 1))


def _all_gather(shards):
    n_arr = len(shards)

    def body(*refs):
        src = refs[:n_arr]
        dst = refs[n_arr:2 * n_arr]
        send_sems, recv_sems, local_sems = refs[2 * n_arr:]
        x, y, c, me = _position()
        local = [pltpu.make_async_copy(src[k], dst[k].at[me], local_sems.at[k]) for k in range(n_arr)]
        for cp in local:
            cp.start()
        sends = []
        for rel in range(1, N_DEV):
            for k in range(n_arr):
                cp = pltpu.make_async_remote_copy(
                    src_ref=src[k], dst_ref=dst[k].at[me], send_sem=send_sems.at[k, rel], recv_sem=recv_sems.at[k, rel],
                    device_id=_peer(x, y, c, rel), device_id_type=MESH)
                cp.start()
                sends.append(cp)
        for rel in range(1, N_DEV):
            for k in range(n_arr):
                pltpu.make_async_remote_copy(
                    src_ref=src[k], dst_ref=dst[k].at[me ^ rel], send_sem=send_sems.at[k, rel], recv_sem=recv_sems.at[k, rel],
                    device_id=_peer(x, y, c, rel), device_id_type=MESH).wait_recv()
        for cp in sends:
            cp.wait_send()
        for cp in local:
            cp.wait()

    any_spec = pl.BlockSpec(memory_space=pl.ANY)
    return pl.pallas_call(
        body, name="all_gather_weights",
        in_specs=[any_spec] * n_arr, out_specs=[any_spec] * n_arr,
        out_shape=[jax.ShapeDtypeStruct((N_DEV,) + s.shape, s.dtype) for s in shards],
        scratch_shapes=[pltpu.SemaphoreType.DMA((n_arr, N_DEV)), pltpu.SemaphoreType.DMA((n_arr, N_DEV)),
                        pltpu.SemaphoreType.DMA((n_arr,))],
        compiler_params=pltpu.CompilerParams(has_side_effects=True),
    )(*shards)


def _exchange_grads(sliced, small):
    n_arr = len(sliced)

    def body(*refs):
        src = refs[:n_arr]
        small_src = refs[n_arr]
        dst = refs[n_arr + 1:2 * n_arr + 1]
        small_dst = refs[2 * n_arr + 1]
        send_sems, recv_sems, local_sems = refs[2 * n_arr + 2:]
        x, y, c, me = _position()
        local = [pltpu.make_async_copy(src[k].at[me], dst[k].at[me], local_sems.at[k]) for k in range(n_arr)]
        local.append(pltpu.make_async_copy(small_src, small_dst.at[me], local_sems.at[n_arr]))
        for cp in local:
            cp.start()
        sends = []
        for rel in range(1, N_DEV):
            peer = _peer(x, y, c, rel)
            for k in range(n_arr + 1):
                s_ref = small_src if k == n_arr else src[k].at[me ^ rel]
                d_ref = small_dst.at[me] if k == n_arr else dst[k].at[me]
                cp = pltpu.make_async_remote_copy(
                    src_ref=s_ref, dst_ref=d_ref, send_sem=send_sems.at[k, rel], recv_sem=recv_sems.at[k, rel],
                    device_id=peer, device_id_type=MESH)
                cp.start()
                sends.append(cp)
        for rel in range(1, N_DEV):
            peer = _peer(x, y, c, rel)
            for k in range(n_arr + 1):
                s_ref = small_src if k == n_arr else src[k].at[me]
                d_ref = small_dst.at[me ^ rel] if k == n_arr else dst[k].at[me ^ rel]
                pltpu.make_async_remote_copy(
                    src_ref=s_ref, dst_ref=d_ref, send_sem=send_sems.at[k, rel], recv_sem=recv_sems.at[k, rel],
                    device_id=peer, device_id_type=MESH).wait_recv()
        for cp in sends:
            cp.wait_send()
        for cp in local:
            cp.wait()

    any_spec = pl.BlockSpec(memory_space=pl.ANY)
    outs = pl.pallas_call(
        body, name="exchange_grads",
        in_specs=[any_spec] * (n_arr + 1), out_specs=[any_spec] * (n_arr + 1),
        out_shape=[jax.ShapeDtypeStruct(s.shape, s.dtype) for s in sliced]
        + [jax.ShapeDtypeStruct((N_DEV,) + small.shape, small.dtype)],
        scratch_shapes=[pltpu.SemaphoreType.DMA((n_arr + 1, N_DEV)), pltpu.SemaphoreType.DMA((n_arr + 1, N_DEV)),
                        pltpu.SemaphoreType.DMA((n_arr + 1,))],
        compiler_params=pltpu.CompilerParams(has_side_effects=True),
    )(*sliced, small)
    return outs[:n_arr], outs[n_arr]


def _adamw_math(w, g, m, v):
    m = ADAM_B1 * m + (1.0 - ADAM_B1) * g
    v = ADAM_B2 * v + (1.0 - ADAM_B2) * (g * g)
    m_hat = m / (1.0 - ADAM_B1 ** ADAM_STEP)
    v_hat = v / (1.0 - ADAM_B2 ** ADAM_STEP)
    return -ADAM_LR * (m_hat / (jnp.sqrt(v_hat) + ADAM_EPS) + ADAM_WD * w), m, v


def _adamw(parts, w, m, v, name):
    rows, cols = w.shape
    tm = 256 if rows % 256 == 0 and rows > 256 else rows

    def body(p_ref, w_ref, m_ref, v_ref, g_ref, d_ref, nm_ref, nv_ref):
        g = p_ref[0].astype(F32)
        for j in range(1, N_DEV):
            g = g + p_ref[j].astype(F32)
        delta, nm, nv = _adamw_math(w_ref[...], g, m_ref[...], v_ref[...])
        g_ref[...] = g
        d_ref[...] = delta
        nm_ref[...] = nm
        nv_ref[...] = nv

    shard = jax.ShapeDtypeStruct((rows, cols), F32)
    return pl.pallas_call(
        body, name=name, grid=(rows // tm,),
        in_specs=[pl.BlockSpec((N_DEV, tm, cols), lambda i: (0, i, 0))] + [_rows(tm, cols)] * 3,
        out_specs=[_rows(tm, cols)] * 4,
        out_shape=[shard] * 4,
        compiler_params=_params("parallel"),
    )(parts, w, m, v)


_SMALL = ("mix_norm_g", "sgu_w", "sgu_b", "sgu_norm_g", "out_norm_a", "out_norm_b", "ffn_norm_g", "ple_norm_g", "final_norm_g")
_BIG = ("w_in", "w_out", "w_gate", "w_up", "w_down", "w_ple_gate", "w_ple_proj")
_COLUMN_SHARDED = ("w_in", "w_gate", "w_up", "w_ple_proj")
_ORDER = ("mix_norm_g", "w_in", "sgu_w", "sgu_b", "sgu_norm_g", "out_norm_a", "out_norm_b", "w_out", "ffn_norm_g",
          "w_gate", "w_up", "w_down", "ple_norm_g", "w_ple_gate", "w_ple_proj", "final_norm_g")


def _pack_small(values):
    flat = jnp.concatenate([values[n].reshape(-1).astype(F32) for n in _SMALL])
    pad = (-flat.shape[0]) % (8 * 128)
    return jnp.pad(flat, (0, pad)).reshape(-1, 128)


def _unpack_small(packed, like):
    flat = packed.reshape(-1)
    out, at = {}, 0
    for n in _SMALL:
        size = like[n].size
        out[n] = flat[at:at + size].reshape(like[n].shape)
        at += size
    return out


def _full_from_gathered(name, gathered):
    if name in _COLUMN_SHARDED:
        k, n = gathered.shape[1], gathered.shape[2] * N_DEV
        return gathered.transpose(1, 0, 2).reshape(k, n), gathered.transpose(0, 2, 1).reshape(n, k)
    k, n = gathered.shape[1] * N_DEV, gathered.shape[2]
    return gathered.reshape(k, n), gathered.transpose(2, 0, 1).reshape(n, k)


def _sliced_for_devices(name, grad):
    k, n = grad.shape
    if name in _COLUMN_SHARDED:
        return grad.reshape(k, N_DEV, n // N_DEV).transpose(1, 0, 2)
    return grad.reshape(N_DEV, k // N_DEV, n)


def _rope_tables(s_len):
    half = HEAD_DIM // 2
    inv = ROPE_THETA ** (-jnp.arange(half, dtype=F32) / half)
    ang = jnp.arange(s_len, dtype=F32)[:, None] * inv[None, :]
    cos, sin = jnp.cos(ang), jnp.sin(ang)
    return jnp.concatenate([cos, cos, cos, cos], axis=1), jnp.concatenate([-sin, sin, -sin, sin], axis=1)


def _forward_backward(x, p, target, small, full):
    s_len = x.shape[0]
    cos, sin = _rope_tables(s_len)
    g_mix, g_ffn, g_ple = small["mix_norm_g"], small["ffn_norm_g"], small["ple_norm_g"]
    g_fin = small["final_norm_g"].reshape(1, D_MODEL)
    sw, gs, ga, gb = small["sgu_w"], small["sgu_norm_g"], small["out_norm_a"], small["out_norm_b"]
    b2 = jnp.repeat(small["sgu_b"].T, HEAD_DIM, axis=1)
    col_head = jnp.arange(WIDTH_B) // HEAD_DIM
    head_sum = (col_head[:, None] == col_head[None, :]).astype(BF16)

    ua, qk, v, hn1 = _inproj(x, g_mix, full["w_in"][0], cos, sin)
    ya_n = _sgu_fwd(ua, sw, b2, gs, ga)
    branch = [_attn_fwd(qk, v, d) for d in DILATIONS]
    y, yb, lse = _combine([o for o, _ in branch], [l for _, l in branch], ya_n, gb)
    h1 = _mm_res(y, full["w_out"][0], x, "out_proj")
    a, b, act, hn2 = _ffn_up(h1, g_ffn, full["w_gate"][0], full["w_up"][0])
    h2 = _mm_res(act, full["w_down"][0], h1, "ffn_down")
    h3, gate, pp, hn3 = _ple(h2, g_ple, full["w_ple_gate"][0], p, full["w_ple_proj"][0])

    dh3, dz, dpp, loss, d_fin = _loss_bwd(h3, target, g_fin, gate, pp)
    dh2, d_ple = _mm_norm_bwd([(dz, full["w_ple_gate"][1])], h2, g_ple, dh3, "ple_bwd")
    da, db = _ffn_down_bwd(dh2, full["w_down"][1], a, b)
    dh1, d_ffn = _mm_norm_bwd([(da, full["w_gate"][1]), (db, full["w_up"][1])], h1, g_ffn, dh2, "ffn_up_bwd")
    dya_n, dyb, dd, d_gb = _outproj_bwd(dh1, full["w_out"][1], yb, gb, head_sum)
    grads_b = [_attn_bwd(qk, v, dyb, lse, dd, d) for d in DILATIONS]
    dua, d_sw, d_b2, d_gs, d_ga = _sgu_bwd(ua, sw, b2, gs, ga, dya_n)
    dproj = _dproj(dua, [g[0] for g in grads_b], [g[1] for g in grads_b], [g[2] for g in grads_b], cos, sin)
    dx, d_mix = _mm_norm_bwd([(dproj, full["w_in"][1])], x, g_mix, dh1, "inproj_bwd")

    big = {
        "w_in": _mm_tn(hn1, dproj, "dw_in"),
        "w_out": _mm_tn(y, dh1, "dw_out"),
        "w_gate": _mm_tn(hn2, da, "dw_gate"),
        "w_up": _mm_tn(hn2, db, "dw_up"),
        "w_down": _mm_tn(act, dh2, "dw_down"),
        "w_ple_gate": _mm_tn(hn3, dz, "dw_ple_gate"),
        "w_ple_proj": _mm_tn(p, dpp, "dw_ple_proj"),
    }
    small_grads = {
        "mix_norm_g": d_mix, "sgu_w": d_sw, "sgu_b": d_b2.reshape(CHUNK, 4, HEAD_DIM).sum(axis=-1).T,
        "sgu_norm_g": d_gs, "out_norm_a": d_ga, "out_norm_b": d_gb, "ffn_norm_g": d_ffn, "ple_norm_g": d_ple,
        "final_norm_g": d_fin,
    }
    return loss[0, 0], dx, big, small_grads


def kernel(x, p, mix_norm_g, w_in, sgu_w, sgu_b, sgu_norm_g, out_norm_a, out_norm_b, w_out, ffn_norm_g, w_gate, w_up, w_down, ple_norm_g, w_ple_gate, w_ple_proj, final_norm_g, loss_target, m_mix_norm_g, m_w_in, m_sgu_w, m_sgu_b, m_sgu_norm_g, m_out_norm_a, m_out_norm_b, m_w_out, m_ffn_norm_g, m_w_gate, m_w_up, m_w_down, m_ple_norm_g, m_w_ple_gate, m_w_ple_proj, m_final_norm_g, v_mix_norm_g, v_w_in, v_sgu_w, v_sgu_b, v_sgu_norm_g, v_out_norm_a, v_out_norm_b, v_w_out, v_ffn_norm_g, v_w_gate, v_w_up, v_w_down, v_ple_norm_g, v_w_ple_gate, v_w_ple_proj, v_final_norm_g):
    given = dict(locals())
    weights = {n: given[n] for n in _ORDER}
    moments_m = {n: given["m_" + n] for n in _ORDER}
    moments_v = {n: given["v_" + n] for n in _ORDER}

    gathered = _all_gather([weights[n][0].astype(BF16) for n in _BIG])
    full = {n: _full_from_gathered(n, g) for n, g in zip(_BIG, gathered)}
    small = {n: (weights[n][0] if n in ("sgu_w", "sgu_b") else weights[n]) for n in _SMALL}

    loss, dx, big_grads, small_grads = _forward_backward(x[0], p[0, 0], loss_target[0], small, full)
    loss = lax.psum(loss, ("x", "y", "c"))

    small_like = {n: weights[n] for n in _SMALL}
    parts, small_parts = _exchange_grads(
        [_sliced_for_devices(n, big_grads[n]) for n in _BIG],
        _pack_small({n: small_grads[n] for n in _SMALL}))

    grads, deltas, new_m, new_v = {}, {}, {}, {}
    for n, part in zip(_BIG, parts):
        g, d, nm, nv = _adamw(part, weights[n][0], moments_m[n][0], moments_v[n][0], "adamw_" + n)
        grads[n], deltas[n], new_m[n], new_v[n] = g[None], d[None], nm[None], nv[None]
    g, d, nm, nv = _adamw(small_parts, _pack_small(small_like), _pack_small({n: moments_m[n] for n in _SMALL}),
                          _pack_small({n: moments_v[n] for n in _SMALL}), "adamw_small")
    for out, packed in ((grads, g), (deltas, d), (new_m, nm), (new_v, nv)):
        out.update(_unpack_small(packed, small_like))

    return (loss, dx[None], *[grads[n] for n in _ORDER], *[deltas[n] for n in _ORDER],
            *[new_m[n] for n in _ORDER], *[new_v[n] for n in _ORDER])
```

```python
import functools

import jax
import jax.numpy as jnp
from jax import lax
from jax.experimental import pallas as pl
from jax.experimental.pallas import tpu as pltpu

F32 = jnp.float32
BF16 = jnp.bfloat16

D_MODEL = 1024
WIDTH_A = 256
WIDTH_B = 768
D_FF = 2816
IN_COLS = 2 * WIDTH_A + 3 * WIDTH_B
PLE_DIM = 256
HEAD_DIM = 64
N_PAIRS = WIDTH_B // 128
CHUNK = 128
N_BACK = 128
DILATIONS = (1, 4, 16)
ROPE_THETA = 10000.0
EPS = 1e-6
N_DEV = 8

ADAM_LR = 0.001
ADAM_B1 = 0.9
ADAM_B2 = 0.999
ADAM_EPS = 1e-08
ADAM_WD = 0.01
ADAM_STEP = 10

V7X_VMEM_LIMIT_BYTES = 56 * 1024 * 1024
ROW_TILE = 512
MESH = pl.DeviceIdType.MESH
NEG = -1e30

_NT = (((1,), (1,)), ((), ()))
_TN = (((0,), (0,)), ((), ()))


def _params(*semantics):
    return pltpu.CompilerParams(dimension_semantics=semantics, vmem_limit_bytes=V7X_VMEM_LIMIT_BYTES)


def _rows(tm, width):
    return pl.BlockSpec((tm, width), lambda i: (i, 0))


def _whole(shape):
    return pl.BlockSpec(shape, lambda *_: (0,) * len(shape))


def _gelu(x):
    t = jnp.tanh(0.7978845608028654 * (x + 0.044715 * (x * x * x)))
    return 0.5 * x * (1.0 + t)


def _gelu_grad(x):
    t = jnp.tanh(0.7978845608028654 * (x + 0.044715 * (x * x * x)))
    return 0.5 * (1.0 + t) + 0.5 * x * (1.0 - t * t) * (0.7978845608028654 * (1.0 + 3.0 * 0.044715 * (x * x)))


def _rstd(x):
    return lax.rsqrt(jnp.mean(x * x, axis=-1, keepdims=True) + EPS)


def _norm_bwd(dn, h, g):
    r = _rstd(h)
    n = h * r
    t = dn * g
    return r * (t - n * jnp.mean(t * n, axis=-1, keepdims=True)), dn * n


def _swap_halves(x, first_half):
    return jnp.where(first_half, pltpu.roll(x, 96, 1), pltpu.roll(x, 32, 1))


def _sub_spec(d, n_cb, tm):
    return pl.BlockSpec((d, n_cb, tm // d, 128), lambda i: (0, 0, i, 0))


def _sub_shape(s_len, d, n_cb, dtype):
    return jax.ShapeDtypeStruct((d, n_cb, s_len // d, 128), dtype)


def _to_sub(stage_ref, cb_src, out_ref, cb_dst, d, tm):
    slab = stage_ref.at[cb_src]
    for r in range(d):
        out_ref[r, cb_dst] = slab[pl.ds(r, tm // d, stride=d), :].astype(out_ref.dtype)


def _from_sub(in_ref, cb_src, stage_ref, cb_dst, d, tm, accumulate=False):
    slab = stage_ref.at[cb_dst]
    for r in range(d):
        rows = pl.ds(r, tm // d, stride=d)
        val = in_ref[r, cb_src].astype(F32)
        slab[rows, :] = slab[rows, :] + val if accumulate else val


def _inproj(x, g, w, cos, sin):
    s_len = x.shape[0]
    tm = ROW_TILE
    n_cb = 3 * N_PAIRS

    def body(x_ref, g_ref, w_ref, cos_ref, sin_ref, ua_ref, hn_ref, *rest):
        sub_refs, stage = rest[:-1], rest[-1]
        xf = x_ref[...]
        hn = (xf * _rstd(xf) * g_ref[...]).astype(BF16)
        hn_ref[...] = hn
        c = cos_ref[...]
        s = sin_ref[...]
        first_half = (lax.broadcasted_iota(jnp.int32, (tm, 128), 1) % HEAD_DIM) < HEAD_DIM // 2
        for j in range(IN_COLS // 256):
            col = j * 256
            acc = jnp.dot(hn, w_ref[:, col:col + 256], preferred_element_type=F32)
            if col < 2 * WIDTH_A:
                ua_ref[:, col:col + 256] = acc
                continue
            for half in range(2):
                cb = (col - 2 * WIDTH_A) // 128 + half
                t = acc[:, half * 128:(half + 1) * 128]
                if cb < 2 * N_PAIRS:
                    t = (t * c + _swap_halves(t, first_half) * s) * (0.125 if cb < N_PAIRS else 1.0)
                stage[cb] = t
        for cb in range(n_cb):
            for d, out_ref in zip(DILATIONS, sub_refs):
                _to_sub(stage, cb, out_ref, cb, d, tm)

    return pl.pallas_call(
        body, name="inproj", grid=(s_len // tm,),
        in_specs=[_rows(tm, D_MODEL), _whole((1, D_MODEL)), _whole((D_MODEL, IN_COLS)), _rows(tm, 128), _rows(tm, 128)],
        out_specs=[_rows(tm, 2 * WIDTH_A), _rows(tm, D_MODEL)] + [_sub_spec(d, n_cb, tm) for d in DILATIONS],
        out_shape=[jax.ShapeDtypeStruct((s_len, 2 * WIDTH_A), F32), jax.ShapeDtypeStruct((s_len, D_MODEL), BF16)]
        + [_sub_shape(s_len, d, n_cb, BF16) for d in DILATIONS],
        scratch_shapes=[pltpu.VMEM((n_cb, tm, 128), F32)],
        compiler_params=_params("parallel"),
    )(x, g, w, cos, sin)


def _sgu_mix_weights(sw_ref):
    keep = lax.broadcasted_iota(jnp.int32, (CHUNK, CHUNK), 0) >= lax.broadcasted_iota(jnp.int32, (CHUNK, CHUNK), 1)
    return [jnp.where(keep, sw_ref[h], 0.0).astype(BF16) for h in range(4)], keep


def _sgu_core(ua_ref, gs_ref):
    u = ua_ref[:, :WIDTH_A]
    va = ua_ref[:, WIDTH_A:]
    vg = _gelu(va)
    xc = vg - jnp.mean(vg, axis=-1, keepdims=True)
    rstd = lax.rsqrt(jnp.mean(xc * xc, axis=-1, keepdims=True) + EPS)
    xhat = xc * rstd
    return u, va, _gelu(u), xhat, rstd, xhat * gs_ref[...]


def _sgu_fwd(ua, sw, b2, gs, ga):
    s_len = ua.shape[0]
    tm = ROW_TILE

    def body(ua_ref, sw_ref, b2_ref, gs_ref, ga_ref, out_ref):
        _, _, ug, _, _, vn = _sgu_core(ua_ref, gs_ref)
        wm, _ = _sgu_mix_weights(sw_ref)
        head = lax.broadcasted_iota(jnp.int32, (CHUNK, WIDTH_A), 1) // HEAD_DIM
        for c in range(tm // CHUNK):
            rows = slice(c * CHUNK, (c + 1) * CHUNK)
            vnc = vn[rows]
            mixed = b2_ref[...]
            for h in range(4):
                mixed = mixed + jnp.dot(wm[h], jnp.where(head == h, vnc, 0.0).astype(BF16), preferred_element_type=F32)
            ya = ug[rows] * mixed
            out_ref[rows, :] = (ya * _rstd(ya) * ga_ref[...]).astype(BF16)

    return pl.pallas_call(
        body, name="sgu_fwd", grid=(s_len // tm,),
        in_specs=[_rows(tm, 2 * WIDTH_A), _whole((4, CHUNK, CHUNK)), _whole((CHUNK, WIDTH_A)), _whole((1, WIDTH_A)), _whole((1, WIDTH_A))],
        out_specs=_rows(tm, WIDTH_A),
        out_shape=jax.ShapeDtypeStruct((s_len, WIDTH_A), BF16),
        compiler_params=_params("parallel"),
    )(ua, sw, b2, gs, ga)


def _attn_geometry(sd):
    tile = min(ROW_TILE, sd)
    return tile, tile // CHUNK, sd // tile


def _attn_spec(cb0, rows, row_index):
    return pl.BlockSpec((None, None, rows, 128), lambda r, hp, n: (r, cb0 + hp, row_index(n), 0))


def _both_heads(x, head_a):
    zero = jnp.zeros_like(x)
    return [jnp.where(head_a, x, zero), jnp.where(head_a, zero, x)]


def _attn_fwd(qkv, d):
    sd = qkv.shape[2]
    tile, nb, n_tiles = _attn_geometry(sd)

    def prev(n):
        return jnp.maximum(n * nb - 1, 0)

    def body(q_ref, k_ref, kp_ref, v_ref, vp_ref, o_ref, l_ref):
        n = pl.program_id(2)
        head_a = lax.broadcasted_iota(jnp.int32, (CHUNK, 128), 1) < HEAD_DIM
        qi = lax.broadcasted_iota(jnp.int32, (2 * CHUNK, 2 * CHUNK), 0) % CHUNK
        kc = lax.broadcasted_iota(jnp.int32, (2 * CHUNK, 2 * CHUNK), 1)
        band = (kc >= qi) & (kc <= qi + N_BACK)
        for j in range(nb):
            rows = slice(j * CHUNK, (j + 1) * CHUNK)
            if j == 0:
                kcat = jnp.concatenate([kp_ref[...], k_ref[rows, :]], axis=0)
                vcat = jnp.concatenate([vp_ref[...], v_ref[rows, :]], axis=0)
                valid = band & jnp.logical_or(n > 0, kc >= CHUNK)
            else:
                kcat = k_ref[(j - 1) * CHUNK:(j + 1) * CHUNK, :]
                vcat = v_ref[(j - 1) * CHUNK:(j + 1) * CHUNK, :]
                valid = band
            q2 = jnp.concatenate(_both_heads(q_ref[rows, :], head_a), axis=0)
            s = lax.dot_general(q2, kcat, _NT, preferred_element_type=F32)
            s = jnp.where(valid, s, NEG)
            m = jnp.max(s, axis=-1, keepdims=True)
            p = jnp.exp(s - m)
            l = jnp.sum(p, axis=-1, keepdims=True)
            o2 = jnp.dot(p.astype(BF16), vcat, preferred_element_type=F32) / l
            lse2 = m + jnp.log(l)
            o_ref[rows, :] = jnp.where(head_a, o2[:CHUNK], o2[CHUNK:]).astype(BF16)
            l_ref[rows, :] = jnp.where(head_a, lse2[:CHUNK], lse2[CHUNK:])

    same = lambda n: n
    return pl.pallas_call(
        body, name=f"attn_fwd_d{d}", grid=(d, N_PAIRS, n_tiles),
        in_specs=[_attn_spec(0, tile, same), _attn_spec(N_PAIRS, tile, same), _attn_spec(N_PAIRS, CHUNK, prev),
                  _attn_spec(2 * N_PAIRS, tile, same), _attn_spec(2 * N_PAIRS, CHUNK, prev)],
        out_specs=[_attn_spec(0, tile, same), _attn_spec(0, tile, same)],
        out_shape=[jax.ShapeDtypeStruct((d, N_PAIRS, sd, 128), BF16), jax.ShapeDtypeStruct((d, N_PAIRS, sd, 128), F32)],
        compiler_params=_params("parallel", "parallel", "parallel"),
    )(qkv, qkv, qkv, qkv, qkv)


def _combine(outs, lses, ya_n, gb):
    s_len = ya_n.shape[0]
    tm = ROW_TILE
    n_br = len(DILATIONS)

    def body(*refs):
        o_refs, l_refs = refs[:n_br], refs[n_br:2 * n_br]
        ya_ref, gb_ref, y_ref, yb_ref = refs[2 * n_br:2 * n_br + 4]
        lse_refs = refs[2 * n_br + 4:3 * n_br + 4]
        o_nat, l_nat, lse_nat = refs[3 * n_br + 4:]
        sumsq = jnp.zeros((tm, 1), F32)
        for cb in range(N_PAIRS):
            for i, d in enumerate(DILATIONS):
                _from_sub(o_refs[i], cb, o_nat, i, d, tm)
                _from_sub(l_refs[i], cb, l_nat, i, d, tm)
            ls = [l_nat[i] for i in range(n_br)]
            top = jnp.maximum(jnp.maximum(ls[0], ls[1]), ls[2])
            ws = [jnp.exp(l - top) for l in ls]
            den = ws[0] + ws[1] + ws[2]
            inv = 1.0 / den
            yb = (ws[0] * inv) * o_nat[0] + (ws[1] * inv) * o_nat[1] + (ws[2] * inv) * o_nat[2]
            yb_ref[:, cb * 128:(cb + 1) * 128] = yb
            sumsq = sumsq + jnp.sum(yb * yb, axis=-1, keepdims=True)
            lse_nat[cb] = top + jnp.log(den)
            for d, lse_ref in zip(DILATIONS, lse_refs):
                _to_sub(lse_nat, cb, lse_ref, cb, d, tm)
        r = lax.rsqrt(sumsq / WIDTH_B + EPS)
        y_ref[:, :WIDTH_A] = ya_ref[...]
        y_ref[:, WIDTH_A:] = (yb_ref[...] * r * gb_ref[...]).astype(BF16)

    return pl.pallas_call(
        body, name="attn_combine", grid=(s_len // tm,),
        in_specs=[_sub_spec(d, N_PAIRS, tm) for d in DILATIONS] * 2 + [_rows(tm, WIDTH_A), _whole((1, WIDTH_B))],
        out_specs=[_rows(tm, D_MODEL), _rows(tm, WIDTH_B)] + [_sub_spec(d, N_PAIRS, tm) for d in DILATIONS],
        out_shape=[jax.ShapeDtypeStruct((s_len, D_MODEL), BF16), jax.ShapeDtypeStruct((s_len, WIDTH_B), F32)]
        + [_sub_shape(s_len, d, N_PAIRS, F32) for d in DILATIONS],
        scratch_shapes=[pltpu.VMEM((n_br, tm, 128), F32), pltpu.VMEM((n_br, tm, 128), F32), pltpu.VMEM((N_PAIRS, tm, 128), F32)],
        compiler_params=_params("parallel"),
    )(*outs, *lses, ya_n, gb)


def _mm_res(a, w, res, name):
    s_len, k = a.shape
    tm = ROW_TILE if k <= D_MODEL else ROW_TILE // 2

    def body(a_ref, w_ref, r_ref, o_ref):
        o_ref[...] = r_ref[...] + jnp.dot(a_ref[...].astype(BF16), w_ref[...], preferred_element_type=F32)

    return pl.pallas_call(
        body, name=name, grid=(s_len // tm,),
        in_specs=[_rows(tm, k), _whole((k, D_MODEL)), _rows(tm, D_MODEL)],
        out_specs=_rows(tm, D_MODEL),
        out_shape=jax.ShapeDtypeStruct((s_len, D_MODEL), F32),
        compiler_params=_params("parallel"),
    )(a, w, res)


def _ffn_up(h, g, wg, wu):
    s_len = h.shape[0]
    tm = ROW_TILE // 2

    def body(h_ref, g_ref, wg_ref, wu_ref, a_ref, b_ref, act_ref, hn_ref):
        hf = h_ref[...]
        hn = (hf * _rstd(hf) * g_ref[...]).astype(BF16)
        hn_ref[...] = hn
        for j in range(D_FF // 256):
            cols = slice(j * 256, (j + 1) * 256)
            a = jnp.dot(hn, wg_ref[:, cols], preferred_element_type=F32)
            b = jnp.dot(hn, wu_ref[:, cols], preferred_element_type=F32)
            a_ref[:, cols] = a.astype(BF16)
            b_ref[:, cols] = b.astype(BF16)
            act_ref[:, cols] = (a * jax.nn.sigmoid(a) * b).astype(BF16)

    wide = jax.ShapeDtypeStruct((s_len, D_FF), BF16)
    return pl.pallas_call(
        body, name="ffn_up", grid=(s_len // tm,),
        in_specs=[_rows(tm, D_MODEL), _whole((1, D_MODEL)), _whole((D_MODEL, D_FF)), _whole((D_MODEL, D_FF))],
        out_specs=[_rows(tm, D_FF), _rows(tm, D_FF), _rows(tm, D_FF), _rows(tm, D_MODEL)],
        out_shape=[wide, wide, wide, jax.ShapeDtypeStruct((s_len, D_MODEL), BF16)],
        compiler_params=_params("parallel"),
    )(h, g, wg, wu)


def _ple(h, g, wpg, p, wpp):
    s_len = h.shape[0]
    tm = ROW_TILE

    def body(h_ref, g_ref, wpg_ref, p_ref, wpp_ref, h3_ref, gate_ref, pp_ref, hn_ref):
        hf = h_ref[...]
        hn = (hf * _rstd(hf) * g_ref[...]).astype(BF16)
        hn_ref[...] = hn
        gate = jax.nn.sigmoid(jnp.dot(hn, wpg_ref[...], preferred_element_type=F32))
        pp = jnp.dot(p_ref[...].astype(BF16), wpp_ref[...], preferred_element_type=F32)
        h3_ref[...] = hf + gate * pp
        gate_ref[...] = gate.astype(BF16)
        pp_ref[...] = pp.astype(BF16)

    half = jax.ShapeDtypeStruct((s_len, D_MODEL), BF16)
    return pl.pallas_call(
        body, name="ple", grid=(s_len // tm,),
        in_specs=[_rows(tm, D_MODEL), _whole((1, D_MODEL)), _whole((D_MODEL, D_MODEL)), _rows(tm, PLE_DIM), _whole((PLE_DIM, D_MODEL))],
        out_specs=[_rows(tm, D_MODEL)] * 4,
        out_shape=[jax.ShapeDtypeStruct((s_len, D_MODEL), F32), half, half, half],
        compiler_params=_params("parallel"),
    )(h, g, wpg, p, wpp)


def _loss_bwd(h3, target, gf, gate, pp):
    s_len = h3.shape[0]
    tm = ROW_TILE

    def body(h_ref, t_ref, g_ref, gate_ref, pp_ref, dh_ref, dz_ref, dpp_ref, loss_ref, dg_ref):
        @pl.when(pl.program_id(0) == 0)
        def _():
            loss_ref[...] = jnp.zeros_like(loss_ref)
            dg_ref[...] = jnp.zeros_like(dg_ref)

        hf = h_ref[...]
        gfv = g_ref[...]
        err = hf * _rstd(hf) * gfv - t_ref[...]
        loss_ref[...] += 0.5 * jnp.sum(jnp.sum(err * err, axis=-1, keepdims=True), axis=0, keepdims=True) / D_MODEL
        dh, dg_rows = _norm_bwd(err / D_MODEL, hf, gfv)
        dg_ref[...] += jnp.sum(dg_rows, axis=0, keepdims=True)
        dh_ref[...] = dh
        gate = gate_ref[...].astype(F32)
        dz_ref[...] = (dh * pp_ref[...].astype(F32) * gate * (1.0 - gate)).astype(BF16)
        dpp_ref[...] = (dh * gate).astype(BF16)

    half = jax.ShapeDtypeStruct((s_len, D_MODEL), BF16)
    return pl.pallas_call(
        body, name="loss_bwd", grid=(s_len // tm,),
        in_specs=[_rows(tm, D_MODEL), _rows(tm, D_MODEL), _whole((1, D_MODEL)), _rows(tm, D_MODEL), _rows(tm, D_MODEL)],
        out_specs=[_rows(tm, D_MODEL), _rows(tm, D_MODEL), _rows(tm, D_MODEL), _whole((1, 128)), _whole((1, D_MODEL))],
        out_shape=[jax.ShapeDtypeStruct((s_len, D_MODEL), F32), half, half,
                   jax.ShapeDtypeStruct((1, 128), F32), jax.ShapeDtypeStruct((1, D_MODEL), F32)],
        compiler_params=_params("arbitrary"),
    )(h3, target, gf, gate, pp)


def _mm_norm_bwd(parts, h, g, dres, name):
    s_len = h.shape[0]
    tm = ROW_TILE // 2
    n_parts = len(parts)

    def body(*refs):
        a_refs = refs[0:2 * n_parts:2]
        w_refs = refs[1:2 * n_parts:2]
        h_ref, g_ref, r_ref, o_ref, dg_ref = refs[2 * n_parts:]

        @pl.when(pl.program_id(0) == 0)
        def _():
            dg_ref[...] = jnp.zeros_like(dg_ref)

        dn = jnp.dot(a_refs[0][...], w_refs[0][...], preferred_element_type=F32)
        for a_ref, w_ref in zip(a_refs[1:], w_refs[1:]):
            dn = dn + jnp.dot(a_ref[...], w_ref[...], preferred_element_type=F32)
        dh, dg_rows = _norm_bwd(dn, h_ref[...], g_ref[...])
        o_ref[...] = r_ref[...] + dh
        dg_ref[...] += jnp.sum(dg_rows, axis=0, keepdims=True)

    in_specs, args = [], []
    for a, w in parts:
        in_specs += [_rows(tm, a.shape[1]), _whole(w.shape)]
        args += [a, w]
    return pl.pallas_call(
        body, name=name, grid=(s_len // tm,),
        in_specs=in_specs + [_rows(tm, D_MODEL), _whole((1, D_MODEL)), _rows(tm, D_MODEL)],
        out_specs=[_rows(tm, D_MODEL), _whole((1, D_MODEL))],
        out_shape=[jax.ShapeDtypeStruct((s_len, D_MODEL), F32), jax.ShapeDtypeStruct((1, D_MODEL), F32)],
        compiler_params=_params("arbitrary"),
    )(*args, h, g, dres)


def _ffn_down_bwd(dh, wdt, a, b):
    s_len = dh.shape[0]
    tm = ROW_TILE // 2

    def body(dh_ref, w_ref, a_ref, b_ref, da_ref, db_ref):
        dhb = dh_ref[...].astype(BF16)
        for j in range(D_FF // 256):
            cols = slice(j * 256, (j + 1) * 256)
            dact = jnp.dot(dhb, w_ref[:, cols], preferred_element_type=F32)
            av = a_ref[:, cols].astype(F32)
            bv = b_ref[:, cols].astype(F32)
            sig = jax.nn.sigmoid(av)
            da_ref[:, cols] = (dact * bv * sig * (1.0 + av * (1.0 - sig))).astype(BF16)
            db_ref[:, cols] = (dact * av * sig).astype(BF16)

    wide = jax.ShapeDtypeStruct((s_len, D_FF), BF16)
    return pl.pallas_call(
        body, name="ffn_down_bwd", grid=(s_len // tm,),
        in_specs=[_rows(tm, D_MODEL), _whole((D_MODEL, D_FF)), _rows(tm, D_FF), _rows(tm, D_FF)],
        out_specs=[_rows(tm, D_FF), _rows(tm, D_FF)],
        out_shape=[wide, wide],
        compiler_params=_params("parallel"),
    )(dh, wdt, a, b)


def _outproj_bwd(dh1, woutt, yb, gb, head_sum):
    s_len = dh1.shape[0]
    tm = ROW_TILE
    n_br = len(DILATIONS)

    def body(dh_ref, w_ref, yb_ref, gb_ref, e_ref, dya_ref, dgb_ref, *rest):
        do_refs, dd_refs = rest[:n_br], rest[n_br:2 * n_br]
        do_nat, dd_nat = rest[2 * n_br:]

        @pl.when(pl.program_id(0) == 0)
        def _():
            dgb_ref[...] = jnp.zeros_like(dgb_ref)

        dhb = dh_ref[...].astype(BF16)
        dya_ref[...] = jnp.dot(dhb, w_ref[:, :WIDTH_A], preferred_element_type=F32)
        dyn = jnp.dot(dhb, w_ref[:, WIDTH_A:], preferred_element_type=F32)
        ybv = yb_ref[...]
        dyb, dg_rows = _norm_bwd(dyn, ybv, gb_ref[...])
        dgb_ref[...] += jnp.sum(dg_rows, axis=0, keepdims=True)
        prod = dyb * ybv
        for cb in range(N_PAIRS):
            cols = slice(cb * 128, (cb + 1) * 128)
            pc = prod[:, cols]
            hi = pc.astype(BF16)
            lo = (pc - hi.astype(F32)).astype(BF16)
            do_nat[cb] = dyb[:, cols]
            dd_nat[cb] = (jnp.dot(hi, e_ref[...], preferred_element_type=F32)
                          + jnp.dot(lo, e_ref[...], preferred_element_type=F32))
            for i, d in enumerate(DILATIONS):
                _to_sub(do_nat, cb, do_refs[i], cb, d, tm)
                _to_sub(dd_nat, cb, dd_refs[i], cb, d, tm)

    subs = [_sub_spec(d, N_PAIRS, tm) for d in DILATIONS]
    return pl.pallas_call(
        body, name="outproj_bwd", grid=(s_len // tm,),
        in_specs=[_rows(tm, D_MODEL), _whole((D_MODEL, D_MODEL)), _rows(tm, WIDTH_B), _whole((1, WIDTH_B)), _whole((128, 128))],
        out_specs=[_rows(tm, WIDTH_A), _whole((1, WIDTH_B))] + subs + subs,
        out_shape=[jax.ShapeDtypeStruct((s_len, WIDTH_A), F32), jax.ShapeDtypeStruct((1, WIDTH_B), F32)]
        + [_sub_shape(s_len, d, N_PAIRS, BF16) for d in DILATIONS] + [_sub_shape(s_len, d, N_PAIRS, F32) for d in DILATIONS],
        scratch_shapes=[pltpu.VMEM((N_PAIRS, tm, 128), F32), pltpu.VMEM((N_PAIRS, tm, 128), F32)],
        compiler_params=_params("arbitrary"),
    )(dh1, woutt, yb, gb, head_sum)


def _attn_bwd(qkv, do, lse, dd, d):
    sd = qkv.shape[2]
    tile, nb, n_tiles = _attn_geometry(sd)
    last_block = sd // CHUNK - 1

    def nxt(n):
        return jnp.minimum((n + 1) * nb, last_block)

    def body(q_ref, qn_ref, k_ref, v_ref, do_ref, don_ref, l_ref, ln_ref, dd_ref, ddn_ref,
             dq_ref, dk_ref, dv_ref, carry_ref):
        n = pl.program_id(2)

        @pl.when(n == 0)
        def _():
            carry_ref[...] = jnp.zeros_like(carry_ref)

        head_a = lax.broadcasted_iota(jnp.int32, (CHUNK, 128), 1) < HEAD_DIM
        row = lax.broadcasted_iota(jnp.int32, (4 * CHUNK, CHUNK), 0)
        qi = row % CHUNK
        ki = lax.broadcasted_iota(jnp.int32, (4 * CHUNK, CHUNK), 1)
        is_after = row >= 2 * CHUNK
        mask = (is_after & (ki >= qi)) | (jnp.logical_not(is_after) & (qi >= ki))
        mask_last = mask & jnp.logical_or(jnp.logical_not(is_after), n < n_tiles - 1)
        dq_acc = [carry_ref[...]] + [jnp.zeros((CHUNK, 128), F32) for _ in range(nb)]

        def columns(x):
            return [x[:, 0:1], x[:, HEAD_DIM:HEAD_DIM + 1]]

        for j in range(nb):
            rows = slice(j * CHUNK, (j + 1) * CHUNK)
            kj = k_ref[rows, :]
            vj = v_ref[rows, :]
            if j + 1 < nb:
                nrows = slice((j + 1) * CHUNK, (j + 2) * CHUNK)
                q2, do2, l2, dd2, msk = q_ref[nrows, :], do_ref[nrows, :], l_ref[nrows, :], dd_ref[nrows, :], mask
            else:
                q2, do2, l2, dd2, msk = qn_ref[...], don_ref[...], ln_ref[...], ddn_ref[...], mask_last
            qs = jnp.concatenate(_both_heads(q_ref[rows, :], head_a) + _both_heads(q2, head_a), axis=0)
            dos = jnp.concatenate(_both_heads(do_ref[rows, :], head_a) + _both_heads(do2, head_a), axis=0)
            ls = jnp.concatenate(columns(l_ref[rows, :]) + columns(l2), axis=0)
            dds = jnp.concatenate(columns(dd_ref[rows, :]) + columns(dd2), axis=0)
            s = lax.dot_general(qs, kj, _NT, preferred_element_type=F32)
            p = jnp.exp(jnp.where(msk, s - ls, NEG))
            dp = lax.dot_general(dos, vj, _NT, preferred_element_type=F32)
            ds = (p * (dp - dds)).astype(BF16)
            dv_ref[rows, :] = lax.dot_general(p.astype(BF16), dos, _TN, preferred_element_type=F32).astype(BF16)
            dk_ref[rows, :] = lax.dot_general(ds, qs, _TN, preferred_element_type=F32).astype(BF16)
            dqs = jnp.dot(ds, kj, preferred_element_type=F32)
            dq_acc[j] = dq_acc[j] + jnp.where(head_a, dqs[:CHUNK], dqs[CHUNK:2 * CHUNK])
            dq_acc[j + 1] = dq_acc[j + 1] + jnp.where(head_a, dqs[2 * CHUNK:3 * CHUNK], dqs[3 * CHUNK:])
        for j in range(nb):
            dq_ref[j * CHUNK:(j + 1) * CHUNK, :] = dq_acc[j].astype(BF16)
        carry_ref[...] = dq_acc[nb]

    same = lambda n: n
    grad = jax.ShapeDtypeStruct((d, N_PAIRS, sd, 128), BF16)
    return pl.pallas_call(
        body, name=f"attn_bwd_d{d}", grid=(d, N_PAIRS, n_tiles),
        in_specs=[_attn_spec(0, tile, same), _attn_spec(0, CHUNK, nxt), _attn_spec(N_PAIRS, tile, same),
                  _attn_spec(2 * N_PAIRS, tile, same), _attn_spec(0, tile, same), _attn_spec(0, CHUNK, nxt),
                  _attn_spec(0, tile, same), _attn_spec(0, CHUNK, nxt), _attn_spec(0, tile, same), _attn_spec(0, CHUNK, nxt)],
        out_specs=[_attn_spec(0, tile, same)] * 3,
        out_shape=[grad, grad, grad],
        scratch_shapes=[pltpu.VMEM((CHUNK, 128), F32)],
        compiler_params=_params("parallel", "parallel", "arbitrary"),
    )(qkv, qkv, qkv, qkv, do, do, lse, lse, dd, dd)


def _sgu_bwd(ua, sw, b2, gs, ga, dya_n):
    s_len = ua.shape[0]
    tm = ROW_TILE

    def body(ua_ref, sw_ref, b2_ref, gs_ref, ga_ref, dy_ref, dua_ref, dsw_ref, db2_ref, dgs_ref, dga_ref):
        @pl.when(pl.program_id(0) == 0)
        def _():
            dsw_ref[...] = jnp.zeros_like(dsw_ref)
            db2_ref[...] = jnp.zeros_like(db2_ref)
            dgs_ref[...] = jnp.zeros_like(dgs_ref)
            dga_ref[...] = jnp.zeros_like(dga_ref)

        u, va, ug, xhat, rstd, vn = _sgu_core(ua_ref, gs_ref)
        wm, keep = _sgu_mix_weights(sw_ref)
        head = lax.broadcasted_iota(jnp.int32, (CHUNK, WIDTH_A), 1) // HEAD_DIM
        gav = ga_ref[...]
        gsv = gs_ref[...]
        dga = jnp.zeros((1, WIDTH_A), F32)
        dgs = jnp.zeros((1, WIDTH_A), F32)
        db2 = jnp.zeros((CHUNK, WIDTH_A), F32)
        dsw = [jnp.zeros((CHUNK, CHUNK), F32) for _ in range(4)]
        for c in range(tm // CHUNK):
            rows = slice(c * CHUNK, (c + 1) * CHUNK)
            vnc = vn[rows]
            vnb = vnc.astype(BF16)
            mixed = b2_ref[...]
            for h in range(4):
                mixed = mixed + jnp.dot(wm[h], jnp.where(head == h, vnc, 0.0).astype(BF16), preferred_element_type=F32)
            ugc = ug[rows]
            dya, dga_rows = _norm_bwd(dy_ref[rows, :], ugc * mixed, gav)
            dga = dga + jnp.sum(dga_rows, axis=0, keepdims=True)
            dmixed = dya * ugc
            db2 = db2 + dmixed
            dvn = jnp.zeros((CHUNK, WIDTH_A), F32)
            for h in range(4):
                dmh = jnp.where(head == h, dmixed, 0.0).astype(BF16)
                dsw[h] = dsw[h] + lax.dot_general(dmh, vnb, _NT, preferred_element_type=F32)
                dvn = dvn + lax.dot_general(wm[h], dmh, _TN, preferred_element_type=F32)
            xh = xhat[rows]
            dgs = dgs + jnp.sum(dvn * xh, axis=0, keepdims=True)
            dxh = dvn * gsv
            dvg = rstd[rows] * (dxh - jnp.mean(dxh, axis=-1, keepdims=True) - xh * jnp.mean(dxh * xh, axis=-1, keepdims=True))
            dua_ref[rows, :WIDTH_A] = (dya * mixed * _gelu_grad(u[rows])).astype(BF16)
            dua_ref[rows, WIDTH_A:] = (dvg * _gelu_grad(va[rows])).astype(BF16)
        for h in range(4):
            dsw_ref[h] += jnp.where(keep, dsw[h], 0.0)
        db2_ref[...] += db2
        dgs_ref[...] += dgs
        dga_ref[...] += dga

    return pl.pallas_call(
        body, name="sgu_bwd", grid=(s_len // tm,),
        in_specs=[_rows(tm, 2 * WIDTH_A), _whole((4, CHUNK, CHUNK)), _whole((CHUNK, WIDTH_A)), _whole((1, WIDTH_A)),
                  _whole((1, WIDTH_A)), _rows(tm, WIDTH_A)],
        out_specs=[_rows(tm, 2 * WIDTH_A), _whole((4, CHUNK, CHUNK)), _whole((CHUNK, WIDTH_A)), _whole((1, WIDTH_A)), _whole((1, WIDTH_A))],
        out_shape=[jax.ShapeDtypeStruct((s_len, 2 * WIDTH_A), BF16), jax.ShapeDtypeStruct((4, CHUNK, CHUNK), F32),
                   jax.ShapeDtypeStruct((CHUNK, WIDTH_A), F32), jax.ShapeDtypeStruct((1, WIDTH_A), F32),
                   jax.ShapeDtypeStruct((1, WIDTH_A), F32)],
        compiler_params=_params("arbitrary"),
    )(ua, sw, b2, gs, ga, dya_n)


def _dproj(dua, dqs, dks, dvs, cos, sin):
    s_len = dua.shape[0]
    tm = ROW_TILE
    n_br = len(DILATIONS)

    def body(dua_ref, *rest):
        groups = [rest[g * n_br:(g + 1) * n_br] for g in range(3)]
        cos_ref, sin_ref, out_ref, acc = rest[3 * n_br:]
        out_ref[:, :2 * WIDTH_A] = dua_ref[...]
        c = cos_ref[...]
        s = sin_ref[...]
        first_half = (lax.broadcasted_iota(jnp.int32, (tm, 128), 1) % HEAD_DIM) < HEAD_DIM // 2
        for g, refs in enumerate(groups):
            for cb in range(N_PAIRS):
                for i, d in enumerate(DILATIONS):
                    _from_sub(refs[i], cb, acc, 0, d, tm, accumulate=i > 0)
                t = acc[0]
                if g < 2:
                    t = (t * c - _swap_halves(t, first_half) * s) * (0.125 if g == 0 else 1.0)
                col = 2 * WIDTH_A + g * WIDTH_B + cb * 128
                out_ref[:, col:col + 128] = t.astype(BF16)

    subs = [_sub_spec(d, N_PAIRS, tm) for d in DILATIONS]
    return pl.pallas_call(
        body, name="dproj", grid=(s_len // tm,),
        in_specs=[_rows(tm, 2 * WIDTH_A)] + subs * 3 + [_rows(tm, 128), _rows(tm, 128)],
        out_specs=_rows(tm, IN_COLS),
        out_shape=jax.ShapeDtypeStruct((s_len, IN_COLS), BF16),
        scratch_shapes=[pltpu.VMEM((1, tm, 128), F32)],
        compiler_params=_params("parallel"),
    )(dua, *dqs, *dks, *dvs, cos, sin)


def _mm_tn(a, b, name):
    s_len, m = a.shape
    n = b.shape[1]
    tk = ROW_TILE
    tm = m if m <= 512 else (1408 if m == D_FF else 512)
    n_k = s_len // tk

    def body(a_ref, b_ref, o_ref, acc_ref):
        k = pl.program_id(1)

        @pl.when(k == 0)
        def _():
            acc_ref[...] = jnp.zeros_like(acc_ref)

        acc_ref[...] += lax.dot_general(a_ref[...].astype(BF16), b_ref[...].astype(BF16), _TN, preferred_element_type=F32)

        @pl.when(k == n_k - 1)
        def _():
            o_ref[...] = acc_ref[...].astype(BF16)

    return pl.pallas_call(
        body, name=name, grid=(m // tm, n_k),
        in_specs=[pl.BlockSpec((tk, tm), lambda i, k: (k, i)), pl.BlockSpec((tk, n), lambda i, k: (k, 0))],
        out_specs=pl.BlockSpec((tm, n), lambda i, k: (i, 0)),
        out_shape=jax.ShapeDtypeStruct((m, n), BF16),
        scratch_shapes=[pltpu.VMEM((tm, n), F32)],
        compiler_params=_params("parallel", "arbitrary"),
    )(a, b)


def _position():
    x, y, c = lax.axis_index("x"), lax.axis_index("y"), lax.axis_index("c")
    return x, y, c, 4 * x + 2 * y + c


def _peer(x, y, c, rel):
    return (x ^ ((rel >> 2) & 1), y ^ ((rel >> 1) & 1), c ^ (rel & 1))


def _all_gather(shards):
    n_arr = len(shards)

    def body(*refs):
        src = refs[:n_arr]
        dst = refs[n_arr:2 * n_arr]
        send_sems, recv_sems, local_sems = refs[2 * n_arr:]
        x, y, c, me = _position()
        local = [pltpu.make_async_copy(src[k], dst[k].at[me], local_sems.at[k]) for k in range(n_arr)]
        for cp in local:
            cp.start()
        sends = []
        for rel in range(1, N_DEV):
            for k in range(n_arr):
                cp = pltpu.make_async_remote_copy(
                    src_ref=src[k], dst_ref=dst[k].at[me], send_sem=send_sems.at[k, rel], recv_sem=recv_sems.at[k, rel],
                    device_id=_peer(x, y, c, rel), device_id_type=MESH)
                cp.start()
                sends.append(cp)
        for rel in range(1, N_DEV):
            for k in range(n_arr):
                pltpu.make_async_remote_copy(
                    src_ref=src[k], dst_ref=dst[k].at[me ^ rel], send_sem=send_sems.at[k, rel], recv_sem=recv_sems.at[k, rel],
                    device_id=_peer(x, y, c, rel), device_id_type=MESH).wait_recv()
        for cp in sends:
            cp.wait_send()
        for cp in local:
            cp.wait()

    any_spec = pl.BlockSpec(memory_space=pl.ANY)
    return pl.pallas_call(
        body, name="all_gather_weights",
        in_specs=[any_spec] * n_arr, out_specs=[any_spec] * n_arr,
        out_shape=[jax.ShapeDtypeStruct((N_DEV,) + s.shape, s.dtype) for s in shards],
        scratch_shapes=[pltpu.SemaphoreType.DMA((n_arr, N_DEV)), pltpu.SemaphoreType.DMA((n_arr, N_DEV)),
                        pltpu.SemaphoreType.DMA((n_arr,))],
        compiler_params=pltpu.CompilerParams(has_side_effects=True),
    )(*shards)


def _exchange_grads(sliced, small):
    n_arr = len(sliced)

    def body(*refs):
        src = refs[:n_arr]
        small_src = refs[n_arr]
        dst = refs[n_arr + 1:2 * n_arr + 1]
        small_dst = refs[2 * n_arr + 1]
        send_sems, recv_sems, local_sems = refs[2 * n_arr + 2:]
        x, y, c, me = _position()
        local = [pltpu.make_async_copy(src[k].at[me], dst[k].at[me], local_sems.at[k]) for k in range(n_arr)]
        local.append(pltpu.make_async_copy(small_src, small_dst.at[me], local_sems.at[n_arr]))
        for cp in local:
            cp.start()
        sends = []
        for rel in range(1, N_DEV):
            peer = _peer(x, y, c, rel)
            for k in range(n_arr + 1):
                s_ref = small_src if k == n_arr else src[k].at[me ^ rel]
                d_ref = small_dst.at[me] if k == n_arr else dst[k].at[me]
                cp = pltpu.make_async_remote_copy(
                    src_ref=s_ref, dst_ref=d_ref, send_sem=send_sems.at[k, rel], recv_sem=recv_sems.at[k, rel],
                    device_id=peer, device_id_type=MESH)
                cp.start()
                sends.append(cp)
        for rel in range(1, N_DEV):
            peer = _peer(x, y, c, rel)
            for k in range(n_arr + 1):
                s_ref = small_src if k == n_arr else src[k].at[me]
                d_ref = small_dst.at[me ^ rel] if k == n_arr else dst[k].at[me ^ rel]
                pltpu.make_async_remote_copy(
                    src_ref=s_ref, dst_ref=d_ref, send_sem=send_sems.at[k, rel], recv_sem=recv_sems.at[k, rel],
                    device_id=peer, device_id_type=MESH).wait_recv()
        for cp in sends:
            cp.wait_send()
        for cp in local:
            cp.wait()

    any_spec = pl.BlockSpec(memory_space=pl.ANY)
    outs = pl.pallas_call(
        body, name="exchange_grads",
        in_specs=[any_spec] * (n_arr + 1), out_specs=[any_spec] * (n_arr + 1),
        out_shape=[jax.ShapeDtypeStruct(s.shape, s.dtype) for s in sliced]
        + [jax.ShapeDtypeStruct((N_DEV,) + small.shape, small.dtype)],
        scratch_shapes=[pltpu.SemaphoreType.DMA((n_arr + 1, N_DEV)), pltpu.SemaphoreType.DMA((n_arr + 1, N_DEV)),
                        pltpu.SemaphoreType.DMA((n_arr + 1,))],
        compiler_params=pltpu.CompilerParams(has_side_effects=True),
    )(*sliced, small)
    return outs[:n_arr], outs[n_arr]


def _adamw_math(w, g, m, v):
    m = ADAM_B1 * m + (1.0 - ADAM_B1) * g
    v = ADAM_B2 * v + (1.0 - ADAM_B2) * (g * g)
    m_hat = m / (1.0 - ADAM_B1 ** ADAM_STEP)
    v_hat = v / (1.0 - ADAM_B2 ** ADAM_STEP)
    return -ADAM_LR * (m_hat / (jnp.sqrt(v_hat) + ADAM_EPS) + ADAM_WD * w), m, v


def _adamw(parts, w, m, v, name):
    rows, cols = w.shape
    tm = 256 if rows % 256 == 0 and rows > 256 else rows

    def body(p_ref, w_ref, m_ref, v_ref, g_ref, d_ref, nm_ref, nv_ref):
        g = p_ref[0].astype(F32)
        for j in range(1, N_DEV):
            g = g + p_ref[j].astype(F32)
        delta, nm, nv = _adamw_math(w_ref[...], g, m_ref[...], v_ref[...])
        g_ref[...] = g
        d_ref[...] = delta
        nm_ref[...] = nm
        nv_ref[...] = nv

    shard = jax.ShapeDtypeStruct((rows, cols), F32)
    return pl.pallas_call(
        body, name=name, grid=(rows // tm,),
        in_specs=[pl.BlockSpec((N_DEV, tm, cols), lambda i: (0, i, 0))] + [_rows(tm, cols)] * 3,
        out_specs=[_rows(tm, cols)] * 4,
        out_shape=[shard] * 4,
        compiler_params=_params("parallel"),
    )(parts, w, m, v)


_SMALL = ("mix_norm_g", "sgu_w", "sgu_b", "sgu_norm_g", "out_norm_a", "out_norm_b", "ffn_norm_g", "ple_norm_g", "final_norm_g")
_BIG = ("w_in", "w_out", "w_gate", "w_up", "w_down", "w_ple_gate", "w_ple_proj")
_COLUMN_SHARDED = ("w_in", "w_gate", "w_up", "w_ple_proj")
_ORDER = ("mix_norm_g", "w_in", "sgu_w", "sgu_b", "sgu_norm_g", "out_norm_a", "out_norm_b", "w_out", "ffn_norm_g",
          "w_gate", "w_up", "w_down", "ple_norm_g", "w_ple_gate", "w_ple_proj", "final_norm_g")


def _pack_small(values):
    flat = jnp.concatenate([values[n].reshape(-1).astype(F32) for n in _SMALL])
    pad = (-flat.shape[0]) % (8 * 128)
    return jnp.pad(flat, (0, pad)).reshape(-1, 128)


def _unpack_small(packed, like):
    flat = packed.reshape(-1)
    out, at = {}, 0
    for n in _SMALL:
        size = like[n].size
        out[n] = flat[at:at + size].reshape(like[n].shape)
        at += size
    return out


def _full_from_gathered(name, gathered):
    if name in _COLUMN_SHARDED:
        k, n = gathered.shape[1], gathered.shape[2] * N_DEV
        return gathered.transpose(1, 0, 2).reshape(k, n), gathered.transpose(0, 2, 1).reshape(n, k)
    k, n = gathered.shape[1] * N_DEV, gathered.shape[2]
    return gathered.reshape(k, n), gathered.transpose(2, 0, 1).reshape(n, k)


def _sliced_for_devices(name, grad):
    k, n = grad.shape
    if name in _COLUMN_SHARDED:
        return grad.reshape(k, N_DEV, n // N_DEV).transpose(1, 0, 2)
    return grad.reshape(N_DEV, k // N_DEV, n)


def _rope_tables(s_len):
    half = HEAD_DIM // 2
    inv = ROPE_THETA ** (-jnp.arange(half, dtype=F32) / half)
    ang = jnp.arange(s_len, dtype=F32)[:, None] * inv[None, :]
    cos, sin = jnp.cos(ang), jnp.sin(ang)
    return jnp.concatenate([cos, cos, cos, cos], axis=1), jnp.concatenate([-sin, sin, -sin, sin], axis=1)


def _forward_backward(x, p, target, small, full):
    s_len = x.shape[0]
    cos, sin = _rope_tables(s_len)
    g_mix, g_ffn, g_ple = small["mix_norm_g"], small["ffn_norm_g"], small["ple_norm_g"]
    g_fin = small["final_norm_g"].reshape(1, D_MODEL)
    sw, gs, ga, gb = small["sgu_w"], small["sgu_norm_g"], small["out_norm_a"], small["out_norm_b"]
    b2 = jnp.repeat(small["sgu_b"].T, HEAD_DIM, axis=1)
    lane_head = jnp.arange(128) // HEAD_DIM
    head_sum = (lane_head[:, None] == lane_head[None, :]).astype(BF16)
    n_br = len(DILATIONS)

    ua, hn1, *qkv = _inproj(x, g_mix, full["w_in"][0], cos, sin)
    ya_n = _sgu_fwd(ua, sw, b2, gs, ga)
    branch = [_attn_fwd(qkv[i], d) for i, d in enumerate(DILATIONS)]
    y, yb, *lse = _combine([o for o, _ in branch], [l for _, l in branch], ya_n, gb)
    h1 = _mm_res(y, full["w_out"][0], x, "out_proj")
    a, b, act, hn2 = _ffn_up(h1, g_ffn, full["w_gate"][0], full["w_up"][0])
    h2 = _mm_res(act, full["w_down"][0], h1, "ffn_down")
    h3, gate, pp, hn3 = _ple(h2, g_ple, full["w_ple_gate"][0], p, full["w_ple_proj"][0])

    dh3, dz, dpp, loss, d_fin = _loss_bwd(h3, target, g_fin, gate, pp)
    dh2, d_ple = _mm_norm_bwd([(dz, full["w_ple_gate"][1])], h2, g_ple, dh3, "ple_bwd")
    da, db = _ffn_down_bwd(dh2, full["w_down"][1], a, b)
    dh1, d_ffn = _mm_norm_bwd([(da, full["w_gate"][1]), (db, full["w_up"][1])], h1, g_ffn, dh2, "ffn_up_bwd")
    dya_n, d_gb, *do_dd = _outproj_bwd(dh1, full["w_out"][1], yb, gb, head_sum)
    grads_b = [_attn_bwd(qkv[i], do_dd[i], lse[i], do_dd[n_br + i], d) for i, d in enumerate(DILATIONS)]
    dua, d_sw, d_b2, d_gs, d_ga = _sgu_bwd(ua, sw, b2, gs, ga, dya_n)
    dproj = _dproj(dua, [g[0] for g in grads_b], [g[1] for g in grads_b], [g[2] for g in grads_b], cos, sin)
    dx, d_mix = _mm_norm_bwd([(dproj, full["w_in"][1])], x, g_mix, dh1, "inproj_bwd")

    big = {
        "w_in": _mm_tn(hn1, dproj, "dw_in"),
        "w_out": _mm_tn(y, dh1, "dw_out"),
        "w_gate": _mm_tn(hn2, da, "dw_gate"),
        "w_up": _mm_tn(hn2, db, "dw_up"),
        "w_down": _mm_tn(act, dh2, "dw_down"),
        "w_ple_gate": _mm_tn(hn3, dz, "dw_ple_gate"),
        "w_ple_proj": _mm_tn(p, dpp, "dw_ple_proj"),
    }
    small_grads = {
        "mix_norm_g": d_mix, "sgu_w": d_sw, "sgu_b": d_b2.reshape(CHUNK, 4, HEAD_DIM).sum(axis=-1).T,
        "sgu_norm_g": d_gs, "out_norm_a": d_ga, "out_norm_b": d_gb, "ffn_norm_g": d_ffn, "ple_norm_g": d_ple,
        "final_norm_g": d_fin,
    }
    return loss[0, 0], dx, big, small_grads


def kernel(x, p, mix_norm_g, w_in, sgu_w, sgu_b, sgu_norm_g, out_norm_a, out_norm_b, w_out, ffn_norm_g, w_gate, w_up, w_down, ple_norm_g, w_ple_gate, w_ple_proj, final_norm_g, loss_target, m_mix_norm_g, m_w_in, m_sgu_w, m_sgu_b, m_sgu_norm_g, m_out_norm_a, m_out_norm_b, m_w_out, m_ffn_norm_g, m_w_gate, m_w_up, m_w_down, m_ple_norm_g, m_w_ple_gate, m_w_ple_proj, m_final_norm_g, v_mix_norm_g, v_w_in, v_sgu_w, v_sgu_b, v_sgu_norm_g, v_out_norm_a, v_out_norm_b, v_w_out, v_ffn_norm_g, v_w_gate, v_w_up, v_w_down, v_ple_norm_g, v_w_ple_gate, v_w_ple_proj, v_final_norm_g):
    given = dict(locals())
    weights = {n: given[n] for n in _ORDER}
    moments_m = {n: given["m_" + n] for n in _ORDER}
    moments_v = {n: given["v_" + n] for n in _ORDER}

    gathered = _all_gather([weights[n][0].astype(BF16) for n in _BIG])
    full = {n: _full_from_gathered(n, g) for n, g in zip(_BIG, gathered)}
    small = {n: (weights[n][0] if n in ("sgu_w", "sgu_b") else weights[n]) for n in _SMALL}

    loss, dx, big_grads, small_grads = _forward_backward(x[0], p[0, 0], loss_target[0], small, full)
    loss = lax.psum(loss, ("x", "y", "c"))

    small_like = {n: weights[n] for n in _SMALL}
    parts, small_parts = _exchange_grads(
        [_sliced_for_devices(n, big_grads[n]) for n in _BIG],
        _pack_small({n: small_grads[n] for n in _SMALL}))

    grads, deltas, new_m, new_v = {}, {}, {}, {}
    for n, part in zip(_BIG, parts):
        g, d, nm, nv = _adamw(part, weights[n][0], moments_m[n][0], moments_v[n][0], "adamw_" + n)
        grads[n], deltas[n], new_m[n], new_v[n] = g[None], d[None], nm[None], nv[None]
    g, d, nm, nv = _adamw(small_parts, _pack_small(small_like), _pack_small({n: moments_m[n] for n in _SMALL}),
                          _pack_small({n: moments_v[n] for n in _SMALL}), "adamw_small")
    for out, packed in ((grads, g), (deltas, d), (new_m, nm), (new_v, nv)):
        out.update(_unpack_small(packed, small_like))

    return (loss, dx[None], *[grads[n] for n in _ORDER], *[deltas[n] for n in _ORDER],
            *[new_m[n] for n in _ORDER], *[new_v[n] for n in _ORDER])
```

```python
import functools

import jax
import jax.numpy as jnp
from jax import lax
from jax.experimental import pallas as pl
from jax.experimental.pallas import tpu as pltpu

F32 = jnp.float32
BF16 = jnp.bfloat16

D_MODEL = 1024
WIDTH_A = 256
WIDTH_B = 768
D_FF = 2816
IN_COLS = 2 * WIDTH_A + 3 * WIDTH_B
PLE_DIM = 256
HEAD_DIM = 64
N_PAIRS = WIDTH_B // 128
CHUNK = 128
N_BACK = 128
DILATIONS = (1, 4, 16)
ROPE_THETA = 10000.0
EPS = 1e-6
N_DEV = 8

ADAM_LR = 0.001
ADAM_B1 = 0.9
ADAM_B2 = 0.999
ADAM_EPS = 1e-08
ADAM_WD = 0.01
ADAM_STEP = 10

V7X_VMEM_LIMIT_BYTES = 56 * 1024 * 1024
ROW_TILE = 512
MESH = pl.DeviceIdType.MESH
NEG = -1e30

_NT = (((1,), (1,)), ((), ()))
_TN = (((0,), (0,)), ((), ()))


def _params(*semantics):
    return pltpu.CompilerParams(dimension_semantics=semantics, vmem_limit_bytes=V7X_VMEM_LIMIT_BYTES)


def _rows(tm, width):
    return pl.BlockSpec((tm, width), lambda i: (i, 0))


def _whole(shape):
    return pl.BlockSpec(shape, lambda *_: (0,) * len(shape))


def _gelu(x):
    t = jnp.tanh(0.7978845608028654 * (x + 0.044715 * (x * x * x)))
    return 0.5 * x * (1.0 + t)


def _gelu_grad(x):
    t = jnp.tanh(0.7978845608028654 * (x + 0.044715 * (x * x * x)))
    return 0.5 * (1.0 + t) + 0.5 * x * (1.0 - t * t) * (0.7978845608028654 * (1.0 + 3.0 * 0.044715 * (x * x)))


def _rstd(x):
    return lax.rsqrt(jnp.mean(x * x, axis=-1, keepdims=True) + EPS)


def _norm_bwd(dn, h, g):
    r = _rstd(h)
    n = h * r
    t = dn * g
    return r * (t - n * jnp.mean(t * n, axis=-1, keepdims=True)), dn * n


def _swap_halves(x, first_half):
    return jnp.where(first_half, pltpu.roll(x, 96, 1), pltpu.roll(x, 32, 1))


def _sub_spec(d, n_cb, tm):
    return pl.BlockSpec((d, n_cb, tm // d, 128), lambda i: (0, 0, i, 0))


def _sub_shape(s_len, d, n_cb, dtype):
    return jax.ShapeDtypeStruct((d, n_cb, s_len // d, 128), dtype)


def _to_sub(stage_ref, cb_src, out_ref, cb_dst, d, tm):
    slab = stage_ref.at[cb_src]
    for r in range(d):
        out_ref[r, cb_dst] = slab[pl.ds(r, tm // d, stride=d), :].astype(out_ref.dtype)


def _from_sub(in_ref, cb_src, stage_ref, cb_dst, d, tm, accumulate=False):
    slab = stage_ref.at[cb_dst]
    for r in range(d):
        rows = pl.ds(r, tm // d, stride=d)
        val = in_ref[r, cb_src].astype(F32)
        slab[rows, :] = slab[rows, :] + val if accumulate else val


def _inproj(x, g, w, cos, sin, exchange=()):
    s_len = x.shape[0]
    tm = ROW_TILE
    n_cb = 3 * N_PAIRS

    def body(x_ref, g_ref, w_ref, cos_ref, sin_ref, ua_ref, hn_ref, *rest):
        sub_refs, stage = rest[:-1], rest[-1]
        xf = x_ref[...]
        hn = (xf * _rstd(xf) * g_ref[...]).astype(BF16)
        hn_ref[...] = hn
        c = cos_ref[...]
        s = sin_ref[...]
        first_half = (lax.broadcasted_iota(jnp.int32, (tm, 128), 1) % HEAD_DIM) < HEAD_DIM // 2
        for j in range(IN_COLS // 256):
            col = j * 256
            acc = jnp.dot(hn, w_ref[:, col:col + 256], preferred_element_type=F32)
            if col < 2 * WIDTH_A:
                ua_ref[:, col:col + 256] = acc
                continue
            for half in range(2):
                cb = (col - 2 * WIDTH_A) // 128 + half
                t = acc[:, half * 128:(half + 1) * 128]
                if cb < 2 * N_PAIRS:
                    t = (t * c + _swap_halves(t, first_half) * s) * (0.125 if cb < N_PAIRS else 1.0)
                stage[cb] = t
        for cb in range(n_cb):
            for d, out_ref in zip(DILATIONS, sub_refs):
                _to_sub(stage, cb, out_ref, cb, d, tm)

    return _call(
        body, name="inproj", grid=(s_len // tm,),
        in_specs=[_rows(tm, D_MODEL), _whole((1, D_MODEL)), _whole((D_MODEL, IN_COLS)), _rows(tm, 128), _rows(tm, 128)],
        out_specs=[_rows(tm, 2 * WIDTH_A), _rows(tm, D_MODEL)] + [_sub_spec(d, n_cb, tm) for d in DILATIONS],
        out_shape=[jax.ShapeDtypeStruct((s_len, 2 * WIDTH_A), F32), jax.ShapeDtypeStruct((s_len, D_MODEL), BF16)]
        + [_sub_shape(s_len, d, n_cb, BF16) for d in DILATIONS],
        scratch_shapes=[pltpu.VMEM((n_cb, tm, 128), F32)],
        semantics=("parallel",), args=(x, g, w, cos, sin), exchange=exchange)


def _sgu_mix_weights(sw_ref):
    keep = lax.broadcasted_iota(jnp.int32, (CHUNK, CHUNK), 0) >= lax.broadcasted_iota(jnp.int32, (CHUNK, CHUNK), 1)
    return [jnp.where(keep, sw_ref[h], 0.0).astype(BF16) for h in range(4)], keep


def _sgu_core(ua_ref, gs_ref):
    u = ua_ref[:, :WIDTH_A]
    va = ua_ref[:, WIDTH_A:]
    vg = _gelu(va)
    xc = vg - jnp.mean(vg, axis=-1, keepdims=True)
    rstd = lax.rsqrt(jnp.mean(xc * xc, axis=-1, keepdims=True) + EPS)
    xhat = xc * rstd
    return u, va, _gelu(u), xhat, rstd, xhat * gs_ref[...]


def _sgu_fwd(ua, sw, b2, gs, ga):
    s_len = ua.shape[0]
    tm = ROW_TILE

    def body(ua_ref, sw_ref, b2_ref, gs_ref, ga_ref, out_ref):
        _, _, ug, _, _, vn = _sgu_core(ua_ref, gs_ref)
        wm, _ = _sgu_mix_weights(sw_ref)
        head = lax.broadcasted_iota(jnp.int32, (CHUNK, WIDTH_A), 1) // HEAD_DIM
        for c in range(tm // CHUNK):
            rows = slice(c * CHUNK, (c + 1) * CHUNK)
            vnc = vn[rows]
            mixed = b2_ref[...]
            for h in range(4):
                mixed = mixed + jnp.dot(wm[h], jnp.where(head == h, vnc, 0.0).astype(BF16), preferred_element_type=F32)
            ya = ug[rows] * mixed
            out_ref[rows, :] = (ya * _rstd(ya) * ga_ref[...]).astype(BF16)

    return pl.pallas_call(
        body, name="sgu_fwd", grid=(s_len // tm,),
        in_specs=[_rows(tm, 2 * WIDTH_A), _whole((4, CHUNK, CHUNK)), _whole((CHUNK, WIDTH_A)), _whole((1, WIDTH_A)), _whole((1, WIDTH_A))],
        out_specs=_rows(tm, WIDTH_A),
        out_shape=jax.ShapeDtypeStruct((s_len, WIDTH_A), BF16),
        compiler_params=_params("parallel"),
    )(ua, sw, b2, gs, ga)


def _attn_geometry(sd):
    tile = min(ROW_TILE, sd)
    return tile, tile // CHUNK, sd // tile


def _attn_spec(cb0, rows, row_index):
    return pl.BlockSpec((None, None, rows, 128), lambda r, hp, n: (r, cb0 + hp, row_index(n), 0))


def _both_heads(x, head_a):
    zero = jnp.zeros_like(x)
    return [jnp.where(head_a, x, zero), jnp.where(head_a, zero, x)]


def _attn_fwd(qkv, d, exchange=()):
    sd = qkv.shape[2]
    tile, nb, n_tiles = _attn_geometry(sd)

    def prev(n):
        return jnp.maximum(n * nb - 1, 0)

    def body(q_ref, k_ref, kp_ref, v_ref, vp_ref, o_ref, l_ref):
        n = pl.program_id(2)
        head_a = lax.broadcasted_iota(jnp.int32, (CHUNK, 128), 1) < HEAD_DIM
        qi = lax.broadcasted_iota(jnp.int32, (2 * CHUNK, 2 * CHUNK), 0) % CHUNK
        kc = lax.broadcasted_iota(jnp.int32, (2 * CHUNK, 2 * CHUNK), 1)
        band = (kc >= qi) & (kc <= qi + N_BACK)
        for j in range(nb):
            rows = slice(j * CHUNK, (j + 1) * CHUNK)
            if j == 0:
                kcat = jnp.concatenate([kp_ref[...], k_ref[rows, :]], axis=0)
                vcat = jnp.concatenate([vp_ref[...], v_ref[rows, :]], axis=0)
                valid = band & jnp.logical_or(n > 0, kc >= CHUNK)
            else:
                kcat = k_ref[(j - 1) * CHUNK:(j + 1) * CHUNK, :]
                vcat = v_ref[(j - 1) * CHUNK:(j + 1) * CHUNK, :]
                valid = band
            q2 = jnp.concatenate(_both_heads(q_ref[rows, :], head_a), axis=0)
            s = lax.dot_general(q2, kcat, _NT, preferred_element_type=F32)
            s = jnp.where(valid, s, NEG)
            m = jnp.max(s, axis=-1, keepdims=True)
            p = jnp.exp(s - m)
            l = jnp.sum(p, axis=-1, keepdims=True)
            o2 = jnp.dot(p.astype(BF16), vcat, preferred_element_type=F32) / l
            lse2 = m + jnp.log(l)
            o_ref[rows, :] = jnp.where(head_a, o2[:CHUNK], o2[CHUNK:]).astype(BF16)
            l_ref[rows, :] = jnp.where(head_a, lse2[:CHUNK], lse2[CHUNK:])

    same = lambda n: n
    return _call(
        body, name=f"attn_fwd_d{d}", grid=(d, N_PAIRS, n_tiles),
        in_specs=[_attn_spec(0, tile, same), _attn_spec(N_PAIRS, tile, same), _attn_spec(N_PAIRS, CHUNK, prev),
                  _attn_spec(2 * N_PAIRS, tile, same), _attn_spec(2 * N_PAIRS, CHUNK, prev)],
        out_specs=[_attn_spec(0, tile, same), _attn_spec(0, tile, same)],
        out_shape=[jax.ShapeDtypeStruct((d, N_PAIRS, sd, 128), BF16), jax.ShapeDtypeStruct((d, N_PAIRS, sd, 128), F32)],
        semantics=("parallel", "parallel", "parallel"), args=(qkv, qkv, qkv, qkv, qkv), exchange=exchange)


def _combine(outs, lses, ya_n, gb):
    s_len = ya_n.shape[0]
    tm = ROW_TILE
    n_br = len(DILATIONS)

    def body(*refs):
        o_refs, l_refs = refs[:n_br], refs[n_br:2 * n_br]
        ya_ref, gb_ref, y_ref, yb_ref = refs[2 * n_br:2 * n_br + 4]
        lse_refs = refs[2 * n_br + 4:3 * n_br + 4]
        o_nat, l_nat, lse_nat = refs[3 * n_br + 4:]
        sumsq = jnp.zeros((tm, 1), F32)
        for cb in range(N_PAIRS):
            for i, d in enumerate(DILATIONS):
                _from_sub(o_refs[i], cb, o_nat, i, d, tm)
                _from_sub(l_refs[i], cb, l_nat, i, d, tm)
            ls = [l_nat[i] for i in range(n_br)]
            top = jnp.maximum(jnp.maximum(ls[0], ls[1]), ls[2])
            ws = [jnp.exp(l - top) for l in ls]
            den = ws[0] + ws[1] + ws[2]
            inv = 1.0 / den
            yb = (ws[0] * inv) * o_nat[0] + (ws[1] * inv) * o_nat[1] + (ws[2] * inv) * o_nat[2]
            yb_ref[:, cb * 128:(cb + 1) * 128] = yb
            sumsq = sumsq + jnp.sum(yb * yb, axis=-1, keepdims=True)
            lse_nat[cb] = top + jnp.log(den)
            for d, lse_ref in zip(DILATIONS, lse_refs):
                _to_sub(lse_nat, cb, lse_ref, cb, d, tm)
        r = lax.rsqrt(sumsq / WIDTH_B + EPS)
        y_ref[:, :WIDTH_A] = ya_ref[...]
        y_ref[:, WIDTH_A:] = (yb_ref[...] * r * gb_ref[...]).astype(BF16)

    return pl.pallas_call(
        body, name="attn_combine", grid=(s_len // tm,),
        in_specs=[_sub_spec(d, N_PAIRS, tm) for d in DILATIONS] * 2 + [_rows(tm, WIDTH_A), _whole((1, WIDTH_B))],
        out_specs=[_rows(tm, D_MODEL), _rows(tm, WIDTH_B)] + [_sub_spec(d, N_PAIRS, tm) for d in DILATIONS],
        out_shape=[jax.ShapeDtypeStruct((s_len, D_MODEL), BF16), jax.ShapeDtypeStruct((s_len, WIDTH_B), F32)]
        + [_sub_shape(s_len, d, N_PAIRS, F32) for d in DILATIONS],
        scratch_shapes=[pltpu.VMEM((n_br, tm, 128), F32), pltpu.VMEM((n_br, tm, 128), F32), pltpu.VMEM((N_PAIRS, tm, 128), F32)],
        compiler_params=_params("parallel"),
    )(*outs, *lses, ya_n, gb)


def _mm_res(a, w, res, name):
    s_len, k = a.shape
    tm = ROW_TILE if k <= D_MODEL else ROW_TILE // 2

    def body(a_ref, w_ref, r_ref, o_ref):
        o_ref[...] = r_ref[...] + jnp.dot(a_ref[...].astype(BF16), w_ref[...], preferred_element_type=F32)

    return pl.pallas_call(
        body, name=name, grid=(s_len // tm,),
        in_specs=[_rows(tm, k), _whole((k, D_MODEL)), _rows(tm, D_MODEL)],
        out_specs=_rows(tm, D_MODEL),
        out_shape=jax.ShapeDtypeStruct((s_len, D_MODEL), F32),
        compiler_params=_params("parallel"),
    )(a, w, res)


def _ffn_up(h, g, wg, wu):
    s_len = h.shape[0]
    tm = ROW_TILE // 2

    def body(h_ref, g_ref, wg_ref, wu_ref, a_ref, b_ref, act_ref, hn_ref):
        hf = h_ref[...]
        hn = (hf * _rstd(hf) * g_ref[...]).astype(BF16)
        hn_ref[...] = hn
        for j in range(D_FF // 256):
            cols = slice(j * 256, (j + 1) * 256)
            a = jnp.dot(hn, wg_ref[:, cols], preferred_element_type=F32)
            b = jnp.dot(hn, wu_ref[:, cols], preferred_element_type=F32)
            a_ref[:, cols] = a.astype(BF16)
            b_ref[:, cols] = b.astype(BF16)
            act_ref[:, cols] = (a * jax.nn.sigmoid(a) * b).astype(BF16)

    wide = jax.ShapeDtypeStruct((s_len, D_FF), BF16)
    return pl.pallas_call(
        body, name="ffn_up", grid=(s_len // tm,),
        in_specs=[_rows(tm, D_MODEL), _whole((1, D_MODEL)), _whole((D_MODEL, D_FF)), _whole((D_MODEL, D_FF))],
        out_specs=[_rows(tm, D_FF), _rows(tm, D_FF), _rows(tm, D_FF), _rows(tm, D_MODEL)],
        out_shape=[wide, wide, wide, jax.ShapeDtypeStruct((s_len, D_MODEL), BF16)],
        compiler_params=_params("parallel"),
    )(h, g, wg, wu)


def _ple(h, g, wpg, p, wpp):
    s_len = h.shape[0]
    tm = ROW_TILE

    def body(h_ref, g_ref, wpg_ref, p_ref, wpp_ref, h3_ref, gate_ref, pp_ref, hn_ref):
        hf = h_ref[...]
        hn = (hf * _rstd(hf) * g_ref[...]).astype(BF16)
        hn_ref[...] = hn
        gate = jax.nn.sigmoid(jnp.dot(hn, wpg_ref[...], preferred_element_type=F32))
        pp = jnp.dot(p_ref[...].astype(BF16), wpp_ref[...], preferred_element_type=F32)
        h3_ref[...] = hf + gate * pp
        gate_ref[...] = gate.astype(BF16)
        pp_ref[...] = pp.astype(BF16)

    half = jax.ShapeDtypeStruct((s_len, D_MODEL), BF16)
    return pl.pallas_call(
        body, name="ple", grid=(s_len // tm,),
        in_specs=[_rows(tm, D_MODEL), _whole((1, D_MODEL)), _whole((D_MODEL, D_MODEL)), _rows(tm, PLE_DIM), _whole((PLE_DIM, D_MODEL))],
        out_specs=[_rows(tm, D_MODEL)] * 4,
        out_shape=[jax.ShapeDtypeStruct((s_len, D_MODEL), F32), half, half, half],
        compiler_params=_params("parallel"),
    )(h, g, wpg, p, wpp)


def _loss_bwd(h3, target, gf, gate, pp):
    s_len = h3.shape[0]
    tm = ROW_TILE

    def body(h_ref, t_ref, g_ref, gate_ref, pp_ref, dh_ref, dz_ref, dpp_ref, loss_ref, dg_ref):
        @pl.when(pl.program_id(0) == 0)
        def _():
            loss_ref[...] = jnp.zeros_like(loss_ref)
            dg_ref[...] = jnp.zeros_like(dg_ref)

        hf = h_ref[...]
        gfv = g_ref[...]
        err = hf * _rstd(hf) * gfv - t_ref[...]
        loss_ref[...] += 0.5 * jnp.sum(jnp.sum(err * err, axis=-1, keepdims=True), axis=0, keepdims=True) / D_MODEL
        dh, dg_rows = _norm_bwd(err / D_MODEL, hf, gfv)
        dg_ref[...] += jnp.sum(dg_rows, axis=0, keepdims=True)
        dh_ref[...] = dh
        gate = gate_ref[...].astype(F32)
        dz_ref[...] = (dh * pp_ref[...].astype(F32) * gate * (1.0 - gate)).astype(BF16)
        dpp_ref[...] = (dh * gate).astype(BF16)

    half = jax.ShapeDtypeStruct((s_len, D_MODEL), BF16)
    return pl.pallas_call(
        body, name="loss_bwd", grid=(s_len // tm,),
        in_specs=[_rows(tm, D_MODEL), _rows(tm, D_MODEL), _whole((1, D_MODEL)), _rows(tm, D_MODEL), _rows(tm, D_MODEL)],
        out_specs=[_rows(tm, D_MODEL), _rows(tm, D_MODEL), _rows(tm, D_MODEL), _whole((1, 128)), _whole((1, D_MODEL))],
        out_shape=[jax.ShapeDtypeStruct((s_len, D_MODEL), F32), half, half,
                   jax.ShapeDtypeStruct((1, 128), F32), jax.ShapeDtypeStruct((1, D_MODEL), F32)],
        compiler_params=_params("arbitrary"),
    )(h3, target, gf, gate, pp)


def _mm_norm_bwd(parts, h, g, dres, name, exchange=()):
    s_len = h.shape[0]
    tm = ROW_TILE // 2
    n_parts = len(parts)

    def body(*refs):
        a_refs = refs[0:2 * n_parts:2]
        w_refs = refs[1:2 * n_parts:2]
        h_ref, g_ref, r_ref, o_ref, dg_ref = refs[2 * n_parts:]

        @pl.when(pl.program_id(0) == 0)
        def _():
            dg_ref[...] = jnp.zeros_like(dg_ref)

        dn = jnp.dot(a_refs[0][...], w_refs[0][...], preferred_element_type=F32)
        for a_ref, w_ref in zip(a_refs[1:], w_refs[1:]):
            dn = dn + jnp.dot(a_ref[...], w_ref[...], preferred_element_type=F32)
        dh, dg_rows = _norm_bwd(dn, h_ref[...], g_ref[...])
        o_ref[...] = r_ref[...] + dh
        dg_ref[...] += jnp.sum(dg_rows, axis=0, keepdims=True)

    in_specs, args = [], []
    for a, w in parts:
        in_specs += [_rows(tm, a.shape[1]), _whole(w.shape)]
        args += [a, w]
    return _call(
        body, name=name, grid=(s_len // tm,),
        in_specs=in_specs + [_rows(tm, D_MODEL), _whole((1, D_MODEL)), _rows(tm, D_MODEL)],
        out_specs=[_rows(tm, D_MODEL), _whole((1, D_MODEL))],
        out_shape=[jax.ShapeDtypeStruct((s_len, D_MODEL), F32), jax.ShapeDtypeStruct((1, D_MODEL), F32)],
        semantics=("arbitrary",), args=(*args, h, g, dres), exchange=exchange)


def _ffn_down_bwd(dh, wdt, a, b, exchange=()):
    s_len = dh.shape[0]
    tm = ROW_TILE // 2

    def body(dh_ref, w_ref, a_ref, b_ref, da_ref, db_ref):
        dhb = dh_ref[...].astype(BF16)
        for j in range(D_FF // 256):
            cols = slice(j * 256, (j + 1) * 256)
            dact = jnp.dot(dhb, w_ref[:, cols], preferred_element_type=F32)
            av = a_ref[:, cols].astype(F32)
            bv = b_ref[:, cols].astype(F32)
            sig = jax.nn.sigmoid(av)
            da_ref[:, cols] = (dact * bv * sig * (1.0 + av * (1.0 - sig))).astype(BF16)
            db_ref[:, cols] = (dact * av * sig).astype(BF16)

    wide = jax.ShapeDtypeStruct((s_len, D_FF), BF16)
    return _call(
        body, name="ffn_down_bwd", grid=(s_len // tm,),
        in_specs=[_rows(tm, D_MODEL), _whole((D_MODEL, D_FF)), _rows(tm, D_FF), _rows(tm, D_FF)],
        out_specs=[_rows(tm, D_FF), _rows(tm, D_FF)],
        out_shape=[wide, wide],
        semantics=("parallel",), args=(dh, wdt, a, b), exchange=exchange)


def _outproj_bwd(dh1, woutt, yb, gb, head_sum, exchange=()):
    s_len = dh1.shape[0]
    tm = ROW_TILE
    n_br = len(DILATIONS)

    def body(dh_ref, w_ref, yb_ref, gb_ref, e_ref, dya_ref, dgb_ref, *rest):
        do_refs, dd_refs = rest[:n_br], rest[n_br:2 * n_br]
        do_nat, dd_nat = rest[2 * n_br:]

        @pl.when(pl.program_id(0) == 0)
        def _():
            dgb_ref[...] = jnp.zeros_like(dgb_ref)

        dhb = dh_ref[...].astype(BF16)
        dya_ref[...] = jnp.dot(dhb, w_ref[:, :WIDTH_A], preferred_element_type=F32)
        dyn = jnp.dot(dhb, w_ref[:, WIDTH_A:], preferred_element_type=F32)
        ybv = yb_ref[...]
        dyb, dg_rows = _norm_bwd(dyn, ybv, gb_ref[...])
        dgb_ref[...] += jnp.sum(dg_rows, axis=0, keepdims=True)
        prod = dyb * ybv
        for cb in range(N_PAIRS):
            cols = slice(cb * 128, (cb + 1) * 128)
            pc = prod[:, cols]
            hi = pc.astype(BF16)
            lo = (pc - hi.astype(F32)).astype(BF16)
            do_nat[cb] = dyb[:, cols]
            dd_nat[cb] = (jnp.dot(hi, e_ref[...], preferred_element_type=F32)
                          + jnp.dot(lo, e_ref[...], preferred_element_type=F32))
            for i, d in enumerate(DILATIONS):
                _to_sub(do_nat, cb, do_refs[i], cb, d, tm)
                _to_sub(dd_nat, cb, dd_refs[i], cb, d, tm)

    subs = [_sub_spec(d, N_PAIRS, tm) for d in DILATIONS]
    return _call(
        body, name="outproj_bwd", grid=(s_len // tm,),
        in_specs=[_rows(tm, D_MODEL), _whole((D_MODEL, D_MODEL)), _rows(tm, WIDTH_B), _whole((1, WIDTH_B)), _whole((128, 128))],
        out_specs=[_rows(tm, WIDTH_A), _whole((1, WIDTH_B))] + subs + subs,
        out_shape=[jax.ShapeDtypeStruct((s_len, WIDTH_A), F32), jax.ShapeDtypeStruct((1, WIDTH_B), F32)]
        + [_sub_shape(s_len, d, N_PAIRS, BF16) for d in DILATIONS] + [_sub_shape(s_len, d, N_PAIRS, F32) for d in DILATIONS],
        scratch_shapes=[pltpu.VMEM((N_PAIRS, tm, 128), F32), pltpu.VMEM((N_PAIRS, tm, 128), F32)],
        semantics=("arbitrary",), args=(dh1, woutt, yb, gb, head_sum), exchange=exchange)


def _attn_bwd(qkv, do, lse, dd, d, exchange=()):
    sd = qkv.shape[2]
    tile, nb, n_tiles = _attn_geometry(sd)
    last_block = sd // CHUNK - 1

    def nxt(n):
        return jnp.minimum((n + 1) * nb, last_block)

    def body(q_ref, qn_ref, k_ref, v_ref, do_ref, don_ref, l_ref, ln_ref, dd_ref, ddn_ref,
             dq_ref, dk_ref, dv_ref, carry_ref):
        n = pl.program_id(2)

        @pl.when(n == 0)
        def _():
            carry_ref[...] = jnp.zeros_like(carry_ref)

        head_a = lax.broadcasted_iota(jnp.int32, (CHUNK, 128), 1) < HEAD_DIM
        row = lax.broadcasted_iota(jnp.int32, (4 * CHUNK, CHUNK), 0)
        qi = row % CHUNK
        ki = lax.broadcasted_iota(jnp.int32, (4 * CHUNK, CHUNK), 1)
        is_after = row >= 2 * CHUNK
        mask = (is_after & (ki >= qi)) | (jnp.logical_not(is_after) & (qi >= ki))
        mask_last = mask & jnp.logical_or(jnp.logical_not(is_after), n < n_tiles - 1)
        dq_acc = [carry_ref[...]] + [jnp.zeros((CHUNK, 128), F32) for _ in range(nb)]

        def columns(x):
            return [x[:, 0:1], x[:, HEAD_DIM:HEAD_DIM + 1]]

        for j in range(nb):
            rows = slice(j * CHUNK, (j + 1) * CHUNK)
            kj = k_ref[rows, :]
            vj = v_ref[rows, :]
            if j + 1 < nb:
                nrows = slice((j + 1) * CHUNK, (j + 2) * CHUNK)
                q2, do2, l2, dd2, msk = q_ref[nrows, :], do_ref[nrows, :], l_ref[nrows, :], dd_ref[nrows, :], mask
            else:
                q2, do2, l2, dd2, msk = qn_ref[...], don_ref[...], ln_ref[...], ddn_ref[...], mask_last
            qs = jnp.concatenate(_both_heads(q_ref[rows, :], head_a) + _both_heads(q2, head_a), axis=0)
            dos = jnp.concatenate(_both_heads(do_ref[rows, :], head_a) + _both_heads(do2, head_a), axis=0)
            ls = jnp.concatenate(columns(l_ref[rows, :]) + columns(l2), axis=0)
            dds = jnp.concatenate(columns(dd_ref[rows, :]) + columns(dd2), axis=0)
            s = lax.dot_general(qs, kj, _NT, preferred_element_type=F32)
            p = jnp.exp(jnp.where(msk, s - ls, NEG))
            dp = lax.dot_general(dos, vj, _NT, preferred_element_type=F32)
            ds = (p * (dp - dds)).astype(BF16)
            dv_ref[rows, :] = lax.dot_general(p.astype(BF16), dos, _TN, preferred_element_type=F32).astype(BF16)
            dk_ref[rows, :] = lax.dot_general(ds, qs, _TN, preferred_element_type=F32).astype(BF16)
            dqs = jnp.dot(ds, kj, preferred_element_type=F32)
            dq_acc[j] = dq_acc[j] + jnp.where(head_a, dqs[:CHUNK], dqs[CHUNK:2 * CHUNK])
            dq_acc[j + 1] = dq_acc[j + 1] + jnp.where(head_a, dqs[2 * CHUNK:3 * CHUNK], dqs[3 * CHUNK:])
        for j in range(nb):
            dq_ref[j * CHUNK:(j + 1) * CHUNK, :] = dq_acc[j].astype(BF16)
        carry_ref[...] = dq_acc[nb]

    same = lambda n: n
    grad = jax.ShapeDtypeStruct((d, N_PAIRS, sd, 128), BF16)
    return _call(
        body, name=f"attn_bwd_d{d}", grid=(d, N_PAIRS, n_tiles),
        in_specs=[_attn_spec(0, tile, same), _attn_spec(0, CHUNK, nxt), _attn_spec(N_PAIRS, tile, same),
                  _attn_spec(2 * N_PAIRS, tile, same), _attn_spec(0, tile, same), _attn_spec(0, CHUNK, nxt),
                  _attn_spec(0, tile, same), _attn_spec(0, CHUNK, nxt), _attn_spec(0, tile, same), _attn_spec(0, CHUNK, nxt)],
        out_specs=[_attn_spec(0, tile, same)] * 3,
        out_shape=[grad, grad, grad],
        scratch_shapes=[pltpu.VMEM((CHUNK, 128), F32)],
        semantics=("parallel", "parallel", "arbitrary"), args=(qkv, qkv, qkv, qkv, do, do, lse, lse, dd, dd), exchange=exchange)


def _sgu_bwd(ua, sw, b2, gs, ga, dya_n):
    s_len = ua.shape[0]
    tm = ROW_TILE

    def body(ua_ref, sw_ref, b2_ref, gs_ref, ga_ref, dy_ref, dua_ref, dsw_ref, db2_ref, dgs_ref, dga_ref):
        @pl.when(pl.program_id(0) == 0)
        def _():
            dsw_ref[...] = jnp.zeros_like(dsw_ref)
            db2_ref[...] = jnp.zeros_like(db2_ref)
            dgs_ref[...] = jnp.zeros_like(dgs_ref)
            dga_ref[...] = jnp.zeros_like(dga_ref)

        u, va, ug, xhat, rstd, vn = _sgu_core(ua_ref, gs_ref)
        wm, keep = _sgu_mix_weights(sw_ref)
        head = lax.broadcasted_iota(jnp.int32, (CHUNK, WIDTH_A), 1) // HEAD_DIM
        gav = ga_ref[...]
        gsv = gs_ref[...]
        dga = jnp.zeros((1, WIDTH_A), F32)
        dgs = jnp.zeros((1, WIDTH_A), F32)
        db2 = jnp.zeros((CHUNK, WIDTH_A), F32)
        dsw = [jnp.zeros((CHUNK, CHUNK), F32) for _ in range(4)]
        for c in range(tm // CHUNK):
            rows = slice(c * CHUNK, (c + 1) * CHUNK)
            vnc = vn[rows]
            vnb = vnc.astype(BF16)
            mixed = b2_ref[...]
            for h in range(4):
                mixed = mixed + jnp.dot(wm[h], jnp.where(head == h, vnc, 0.0).astype(BF16), preferred_element_type=F32)
            ugc = ug[rows]
            dya, dga_rows = _norm_bwd(dy_ref[rows, :], ugc * mixed, gav)
            dga = dga + jnp.sum(dga_rows, axis=0, keepdims=True)
            dmixed = dya * ugc
            db2 = db2 + dmixed
            dvn = jnp.zeros((CHUNK, WIDTH_A), F32)
            for h in range(4):
                dmh = jnp.where(head == h, dmixed, 0.0).astype(BF16)
                dsw[h] = dsw[h] + lax.dot_general(dmh, vnb, _NT, preferred_element_type=F32)
                dvn = dvn + lax.dot_general(wm[h], dmh, _TN, preferred_element_type=F32)
            xh = xhat[rows]
            dgs = dgs + jnp.sum(dvn * xh, axis=0, keepdims=True)
            dxh = dvn * gsv
            dvg = rstd[rows] * (dxh - jnp.mean(dxh, axis=-1, keepdims=True) - xh * jnp.mean(dxh * xh, axis=-1, keepdims=True))
            dua_ref[rows, :WIDTH_A] = (dya * mixed * _gelu_grad(u[rows])).astype(BF16)
            dua_ref[rows, WIDTH_A:] = (dvg * _gelu_grad(va[rows])).astype(BF16)
        for h in range(4):
            dsw_ref[h] += jnp.where(keep, dsw[h], 0.0)
        db2_ref[...] += db2
        dgs_ref[...] += dgs
        dga_ref[...] += dga

    return pl.pallas_call(
        body, name="sgu_bwd", grid=(s_len // tm,),
        in_specs=[_rows(tm, 2 * WIDTH_A), _whole((4, CHUNK, CHUNK)), _whole((CHUNK, WIDTH_A)), _whole((1, WIDTH_A)),
                  _whole((1, WIDTH_A)), _rows(tm, WIDTH_A)],
        out_specs=[_rows(tm, 2 * WIDTH_A), _whole((4, CHUNK, CHUNK)), _whole((CHUNK, WIDTH_A)), _whole((1, WIDTH_A)), _whole((1, WIDTH_A))],
        out_shape=[jax.ShapeDtypeStruct((s_len, 2 * WIDTH_A), BF16), jax.ShapeDtypeStruct((4, CHUNK, CHUNK), F32),
                   jax.ShapeDtypeStruct((CHUNK, WIDTH_A), F32), jax.ShapeDtypeStruct((1, WIDTH_A), F32),
                   jax.ShapeDtypeStruct((1, WIDTH_A), F32)],
        compiler_params=_params("arbitrary"),
    )(ua, sw, b2, gs, ga, dya_n)


def _dproj(dua, dqs, dks, dvs, cos, sin):
    s_len = dua.shape[0]
    tm = ROW_TILE
    n_br = len(DILATIONS)

    def body(dua_ref, *rest):
        groups = [rest[g * n_br:(g + 1) * n_br] for g in range(3)]
        cos_ref, sin_ref, out_ref, acc = rest[3 * n_br:]
        out_ref[:, :2 * WIDTH_A] = dua_ref[...]
        c = cos_ref[...]
        s = sin_ref[...]
        first_half = (lax.broadcasted_iota(jnp.int32, (tm, 128), 1) % HEAD_DIM) < HEAD_DIM // 2
        for g, refs in enumerate(groups):
            for cb in range(N_PAIRS):
                for i, d in enumerate(DILATIONS):
                    _from_sub(refs[i], cb, acc, 0, d, tm, accumulate=i > 0)
                t = acc[0]
                if g < 2:
                    t = (t * c - _swap_halves(t, first_half) * s) * (0.125 if g == 0 else 1.0)
                col = 2 * WIDTH_A + g * WIDTH_B + cb * 128
                out_ref[:, col:col + 128] = t.astype(BF16)

    subs = [_sub_spec(d, N_PAIRS, tm) for d in DILATIONS]
    return pl.pallas_call(
        body, name="dproj", grid=(s_len // tm,),
        in_specs=[_rows(tm, 2 * WIDTH_A)] + subs * 3 + [_rows(tm, 128), _rows(tm, 128)],
        out_specs=_rows(tm, IN_COLS),
        out_shape=jax.ShapeDtypeStruct((s_len, IN_COLS), BF16),
        scratch_shapes=[pltpu.VMEM((1, tm, 128), F32)],
        compiler_params=_params("parallel"),
    )(dua, *dqs, *dks, *dvs, cos, sin)


def _mm_tn(a, b, name, exchange=()):
    s_len, m = a.shape
    n = b.shape[1]
    tk = ROW_TILE
    tm = m if m <= 512 else (1408 if m == D_FF else 512)
    n_k = s_len // tk

    def body(a_ref, b_ref, o_ref, acc_ref):
        k = pl.program_id(1)

        @pl.when(k == 0)
        def _():
            acc_ref[...] = jnp.zeros_like(acc_ref)

        acc_ref[...] += lax.dot_general(a_ref[...].astype(BF16), b_ref[...].astype(BF16), _TN, preferred_element_type=F32)

        @pl.when(k == n_k - 1)
        def _():
            o_ref[...] = acc_ref[...].astype(BF16)

    (grad,), received = _call(
        body, name=name, grid=(m // tm, n_k),
        in_specs=[pl.BlockSpec((tk, tm), lambda i, k: (k, i)), pl.BlockSpec((tk, n), lambda i, k: (k, 0))],
        out_specs=[pl.BlockSpec((tm, n), lambda i, k: (i, 0))],
        out_shape=[jax.ShapeDtypeStruct((m, n), BF16)],
        scratch_shapes=[pltpu.VMEM((tm, n), F32)],
        semantics=("parallel", "arbitrary"), args=(a, b), exchange=exchange)
    return grad, received


def _position():
    x, y, c = lax.axis_index("x"), lax.axis_index("y"), lax.axis_index("c")
    return x, y, c, 4 * x + 2 * y + c


def _peer(x, y, c, rel):
    return (x ^ ((rel >> 2) & 1), y ^ ((rel >> 1) & 1), c ^ (rel & 1))


def _exchange_out_shape(kind, arr):
    return jax.ShapeDtypeStruct(((N_DEV,) + arr.shape) if kind == "gather" else arr.shape, arr.dtype)


def _exchange_sems(n_items):
    return [pltpu.SemaphoreType.DMA((n_items, N_DEV)), pltpu.SemaphoreType.DMA((n_items, N_DEV)), pltpu.SemaphoreType.DMA((n_items,))]


def _exchange_copies(kinds, srcs, dsts, sems):
    send_sems, recv_sems, local_sems = sems
    x, y, c, me = _position()
    local, sends, recvs = [], [], []
    for k, (kind, src, dst) in enumerate(zip(kinds, srcs, dsts)):
        own = src if kind == "gather" else src.at[me]
        local.append(pltpu.make_async_copy(own, dst.at[me], local_sems.at[k]))
        for rel in range(1, N_DEV):
            going = src if kind == "gather" else src.at[me ^ rel]
            common = dict(send_sem=send_sems.at[k, rel], recv_sem=recv_sems.at[k, rel],
                          device_id=_peer(x, y, c, rel), device_id_type=MESH)
            sends.append(pltpu.make_async_remote_copy(src_ref=going, dst_ref=dst.at[me], **common))
            recvs.append(pltpu.make_async_remote_copy(src_ref=own, dst_ref=dst.at[me ^ rel], **common))
    return local, sends, recvs


def _exchange_start(kinds, srcs, dsts, sems):
    local, sends, _ = _exchange_copies(kinds, srcs, dsts, sems)
    for cp in local + sends:
        cp.start()


def _exchange_finish(kinds, srcs, dsts, sems):
    local, sends, recvs = _exchange_copies(kinds, srcs, dsts, sems)
    for cp in recvs:
        cp.wait_recv()
    for cp in sends:
        cp.wait_send()
    for cp in local:
        cp.wait()


def _exchange_only(items, name):
    kinds = [k for k, _ in items]
    n = len(items)

    def body(*refs):
        srcs, dsts, sems = refs[:n], refs[n:2 * n], refs[2 * n:]
        _exchange_start(kinds, srcs, dsts, sems)
        _exchange_finish(kinds, srcs, dsts, sems)

    any_spec = pl.BlockSpec(memory_space=pl.ANY)
    return pl.pallas_call(
        body, name=name, in_specs=[any_spec] * n, out_specs=[any_spec] * n,
        out_shape=[_exchange_out_shape(k, a) for k, a in items],
        scratch_shapes=_exchange_sems(n),
        compiler_params=pltpu.CompilerParams(has_side_effects=True),
    )(*[a for _, a in items])


def _call(body, *, name, grid, in_specs, out_specs, out_shape, args, scratch_shapes=(), semantics, exchange=()):
    if not exchange:
        outs = pl.pallas_call(body, name=name, grid=grid, in_specs=in_specs, out_specs=out_specs, out_shape=out_shape,
                              scratch_shapes=list(scratch_shapes), compiler_params=_params(*semantics))(*args)
        return outs, []
    kinds = [k for k, _ in exchange]
    n_in, n_out, n_x, n_scr = len(in_specs), len(out_specs), len(exchange), len(scratch_shapes)

    def wrapped(*refs):
        ins, refs = refs[:n_in], refs[n_in:]
        srcs, refs = refs[:n_x], refs[n_x:]
        outs, refs = refs[:n_out], refs[n_out:]
        dsts, refs = refs[:n_x], refs[n_x:]
        scratch, sems = refs[:n_scr], refs[n_scr:]
        ids = [pl.program_id(a) for a in range(len(grid))]
        first = functools.reduce(jnp.logical_and, [i == 0 for i in ids])
        last = functools.reduce(jnp.logical_and, [i == g - 1 for i, g in zip(ids, grid)])

        @pl.when(first)
        def _():
            _exchange_start(kinds, srcs, dsts, sems)

        body(*ins, *outs, *scratch)

        @pl.when(last)
        def _():
            _exchange_finish(kinds, srcs, dsts, sems)

    any_spec = pl.BlockSpec(memory_space=pl.ANY)
    outs = pl.pallas_call(
        wrapped, name=name, grid=grid,
        in_specs=list(in_specs) + [any_spec] * n_x, out_specs=list(out_specs) + [any_spec] * n_x,
        out_shape=list(out_shape) + [_exchange_out_shape(k, a) for k, a in exchange],
        scratch_shapes=list(scratch_shapes) + _exchange_sems(n_x),
        compiler_params=pltpu.CompilerParams(dimension_semantics=("arbitrary",) * len(grid),
                                             vmem_limit_bytes=V7X_VMEM_LIMIT_BYTES, has_side_effects=True),
    )(*args, *[a for _, a in exchange])
    return outs[:n_out], outs[n_out:]


def _adamw_math(w, g, m, v):
    m = ADAM_B1 * m + (1.0 - ADAM_B1) * g
    v = ADAM_B2 * v + (1.0 - ADAM_B2) * (g * g)
    m_hat = m / (1.0 - ADAM_B1 ** ADAM_STEP)
    v_hat = v / (1.0 - ADAM_B2 ** ADAM_STEP)
    return -ADAM_LR * (m_hat / (jnp.sqrt(v_hat) + ADAM_EPS) + ADAM_WD * w), m, v


def _adamw(parts, w, m, v, name):
    rows, cols = w.shape
    tm = 256 if rows % 256 == 0 and rows > 256 else rows

    def body(p_ref, w_ref, m_ref, v_ref, g_ref, d_ref, nm_ref, nv_ref):
        g = p_ref[0].astype(F32)
        for j in range(1, N_DEV):
            g = g + p_ref[j].astype(F32)
        delta, nm, nv = _adamw_math(w_ref[...], g, m_ref[...], v_ref[...])
        g_ref[...] = g
        d_ref[...] = delta
        nm_ref[...] = nm
        nv_ref[...] = nv

    shard = jax.ShapeDtypeStruct((rows, cols), F32)
    return pl.pallas_call(
        body, name=name, grid=(rows // tm,),
        in_specs=[pl.BlockSpec((N_DEV, tm, cols), lambda i: (0, i, 0))] + [_rows(tm, cols)] * 3,
        out_specs=[_rows(tm, cols)] * 4,
        out_shape=[shard] * 4,
        compiler_params=_params("parallel"),
    )(parts, w, m, v)


_SMALL = ("mix_norm_g", "sgu_w", "sgu_b", "sgu_norm_g", "out_norm_a", "out_norm_b", "ffn_norm_g", "ple_norm_g", "final_norm_g")
_BIG = ("w_in", "w_out", "w_gate", "w_up", "w_down", "w_ple_gate", "w_ple_proj")
_COLUMN_SHARDED = ("w_in", "w_gate", "w_up", "w_ple_proj")
_ORDER = ("mix_norm_g", "w_in", "sgu_w", "sgu_b", "sgu_norm_g", "out_norm_a", "out_norm_b", "w_out", "ffn_norm_g",
          "w_gate", "w_up", "w_down", "ple_norm_g", "w_ple_gate", "w_ple_proj", "final_norm_g")


def _pack_small(values):
    flat = jnp.concatenate([values[n].reshape(-1).astype(F32) for n in _SMALL])
    pad = (-flat.shape[0]) % (8 * 128)
    return jnp.pad(flat, (0, pad)).reshape(-1, 128)


def _unpack_small(packed, like):
    flat = packed.reshape(-1)
    out, at = {}, 0
    for n in _SMALL:
        size = like[n].size
        out[n] = flat[at:at + size].reshape(like[n].shape)
        at += size
    return out


def _full_from_gathered(name, gathered):
    if name in _COLUMN_SHARDED:
        k, n = gathered.shape[1], gathered.shape[2] * N_DEV
        return gathered.transpose(1, 0, 2).reshape(k, n), gathered.transpose(0, 2, 1).reshape(n, k)
    k, n = gathered.shape[1] * N_DEV, gathered.shape[2]
    return gathered.reshape(k, n), gathered.transpose(2, 0, 1).reshape(n, k)


def _sliced_for_devices(name, grad):
    k, n = grad.shape
    if name in _COLUMN_SHARDED:
        return grad.reshape(k, N_DEV, n // N_DEV).transpose(1, 0, 2)
    return grad.reshape(N_DEV, k // N_DEV, n)


def _rope_tables(s_len):
    half = HEAD_DIM // 2
    inv = ROPE_THETA ** (-jnp.arange(half, dtype=F32) / half)
    ang = jnp.arange(s_len, dtype=F32)[:, None] * inv[None, :]
    cos, sin = jnp.cos(ang), jnp.sin(ang)
    return jnp.concatenate([cos, cos, cos, cos], axis=1), jnp.concatenate([-sin, sin, -sin, sin], axis=1)


def _forward_backward(x, p, target, small, shards):
    def gather(*names):
        return [("gather", shards[n]) for n in names]

    def scatter(**grads):
        return [("scatter", _sliced_for_devices(n, g)) for n, g in grads.items()]

    full, parts = {}, {}
    (got,) = _exchange_only(gather("w_in"), "gather_w_in")
    full["w_in"] = _full_from_gathered("w_in", got)

    s_len = x.shape[0]
    cos, sin = _rope_tables(s_len)
    g_mix, g_ffn, g_ple = small["mix_norm_g"], small["ffn_norm_g"], small["ple_norm_g"]
    g_fin = small["final_norm_g"].reshape(1, D_MODEL)
    sw, gs, ga, gb = small["sgu_w"], small["sgu_norm_g"], small["out_norm_a"], small["out_norm_b"]
    b2 = jnp.repeat(small["sgu_b"].T, HEAD_DIM, axis=1)
    lane_head = jnp.arange(128) // HEAD_DIM
    head_sum = (lane_head[:, None] == lane_head[None, :]).astype(BF16)
    n_br = len(DILATIONS)

    first_wave = ("w_out", "w_ple_gate", "w_ple_proj")
    (ua, hn1, *qkv), got = _inproj(x, g_mix, full["w_in"][0], cos, sin, exchange=gather(*first_wave))
    for n, g in zip(first_wave, got):
        full[n] = _full_from_gathered(n, g)
    ya_n = _sgu_fwd(ua, sw, b2, gs, ga)
    branch = []
    for i, (d, n) in enumerate(zip(DILATIONS, ("w_gate", "w_up", "w_down"))):
        o_l, (g,) = _attn_fwd(qkv[i], d, exchange=gather(n))
        branch.append(o_l)
        full[n] = _full_from_gathered(n, g)
    y, yb, *lse = _combine([o for o, _ in branch], [l for _, l in branch], ya_n, gb)
    h1 = _mm_res(y, full["w_out"][0], x, "out_proj")
    a, b, act, hn2 = _ffn_up(h1, g_ffn, full["w_gate"][0], full["w_up"][0])
    h2 = _mm_res(act, full["w_down"][0], h1, "ffn_down")
    h3, gate, pp, hn3 = _ple(h2, g_ple, full["w_ple_gate"][0], p, full["w_ple_proj"][0])

    dh3, dz, dpp, loss, d_fin = _loss_bwd(h3, target, g_fin, gate, pp)
    g_ple_proj, _ = _mm_tn(p, dpp, "dw_ple_proj")
    g_ple_gate, _ = _mm_tn(hn3, dz, "dw_ple_gate")
    (dh2, d_ple), _ = _mm_norm_bwd([(dz, full["w_ple_gate"][1])], h2, g_ple, dh3, "ple_bwd")
    g_down, (parts["w_ple_gate"], parts["w_ple_proj"]) = _mm_tn(
        act, dh2, "dw_down", exchange=scatter(w_ple_gate=g_ple_gate, w_ple_proj=g_ple_proj))
    (da, db), (parts["w_down"],) = _ffn_down_bwd(dh2, full["w_down"][1], a, b, exchange=scatter(w_down=g_down))
    g_gate, _ = _mm_tn(hn2, da, "dw_gate")
    g_up, _ = _mm_tn(hn2, db, "dw_up")
    (dh1, d_ffn), (parts["w_gate"],) = _mm_norm_bwd(
        [(da, full["w_gate"][1]), (db, full["w_up"][1])], h1, g_ffn, dh2, "ffn_up_bwd", exchange=scatter(w_gate=g_gate))
    g_out, _ = _mm_tn(y, dh1, "dw_out")
    (dya_n, d_gb, *do_dd), (parts["w_up"],) = _outproj_bwd(dh1, full["w_out"][1], yb, gb, head_sum, exchange=scatter(w_up=g_up))
    grads_b = []
    for i, d in enumerate(DILATIONS):
        g3, got = _attn_bwd(qkv[i], do_dd[i], lse[i], do_dd[n_br + i], d, exchange=scatter(w_out=g_out) if i == 0 else ())
        grads_b.append(g3)
        if i == 0:
            (parts["w_out"],) = got
    dua, d_sw, d_b2, d_gs, d_ga = _sgu_bwd(ua, sw, b2, gs, ga, dya_n)
    dproj = _dproj(dua, [g[0] for g in grads_b], [g[1] for g in grads_b], [g[2] for g in grads_b], cos, sin)
    g_in, _ = _mm_tn(hn1, dproj, "dw_in")
    (dx, d_mix), (parts["w_in"],) = _mm_norm_bwd(
        [(dproj, full["w_in"][1])], x, g_mix, dh1, "inproj_bwd", exchange=scatter(w_in=g_in))

    small_grads = {
        "mix_norm_g": d_mix, "sgu_w": d_sw, "sgu_b": d_b2.reshape(CHUNK, 4, HEAD_DIM).sum(axis=-1).T,
        "sgu_norm_g": d_gs, "out_norm_a": d_ga, "out_norm_b": d_gb, "ffn_norm_g": d_ffn, "ple_norm_g": d_ple,
        "final_norm_g": d_fin,
    }
    (small_parts,) = _exchange_only([("gather", _pack_small(small_grads))], "gather_small_grads")
    return loss[0, 0], dx, parts, small_parts


def kernel(x, p, mix_norm_g, w_in, sgu_w, sgu_b, sgu_norm_g, out_norm_a, out_norm_b, w_out, ffn_norm_g, w_gate, w_up, w_down, ple_norm_g, w_ple_gate, w_ple_proj, final_norm_g, loss_target, m_mix_norm_g, m_w_in, m_sgu_w, m_sgu_b, m_sgu_norm_g, m_out_norm_a, m_out_norm_b, m_w_out, m_ffn_norm_g, m_w_gate, m_w_up, m_w_down, m_ple_norm_g, m_w_ple_gate, m_w_ple_proj, m_final_norm_g, v_mix_norm_g, v_w_in, v_sgu_w, v_sgu_b, v_sgu_norm_g, v_out_norm_a, v_out_norm_b, v_w_out, v_ffn_norm_g, v_w_gate, v_w_up, v_w_down, v_ple_norm_g, v_w_ple_gate, v_w_ple_proj, v_final_norm_g):
    given = dict(locals())
    weights = {n: given[n] for n in _ORDER}
    moments_m = {n: given["m_" + n] for n in _ORDER}
    moments_v = {n: given["v_" + n] for n in _ORDER}

    shards = {n: weights[n][0].astype(BF16) for n in _BIG}
    small = {n: (weights[n][0] if n in ("sgu_w", "sgu_b") else weights[n]) for n in _SMALL}

    loss, dx, parts, small_parts = _forward_backward(x[0], p[0, 0], loss_target[0], small, shards)
    loss = lax.psum(loss, ("x", "y", "c"))

    small_like = {n: weights[n] for n in _SMALL}
    grads, deltas, new_m, new_v = {}, {}, {}, {}
    for n in _BIG:
        g, d, nm, nv = _adamw(parts[n], weights[n][0], moments_m[n][0], moments_v[n][0], "adamw_" + n)
        grads[n], deltas[n], new_m[n], new_v[n] = g[None], d[None], nm[None], nv[None]
    g, d, nm, nv = _adamw(small_parts, _pack_small(small_like), _pack_small({n: moments_m[n] for n in _SMALL}),
                          _pack_small({n: moments_v[n] for n in _SMALL}), "adamw_small")
    for out, packed in ((grads, g), (deltas, d), (new_m, nm), (new_v, nv)):
        out.update(_unpack_small(packed, small_like))

    return (loss, dx[None], *[grads[n] for n in _ORDER], *[deltas[n] for n in _ORDER],
            *[new_m[n] for n in _ORDER], *[new_v[n] for n in _ORDER])
```

```python
import functools

import jax
import jax.numpy as jnp
from jax import lax
from jax.experimental import pallas as pl
from jax.experimental.pallas import tpu as pltpu

F32 = jnp.float32
BF16 = jnp.bfloat16

D_MODEL = 1024
WIDTH_A = 256
WIDTH_B = 768
D_FF = 2816
IN_COLS = 2 * WIDTH_A + 3 * WIDTH_B
PLE_DIM = 256
HEAD_DIM = 64
N_PAIRS = WIDTH_B // 128
CHUNK = 128
N_BACK = 128
DILATIONS = (1, 4, 16)
ROPE_THETA = 10000.0
EPS = 1e-6
N_DEV = 8

ADAM_LR = 0.001
ADAM_B1 = 0.9
ADAM_B2 = 0.999
ADAM_EPS = 1e-08
ADAM_WD = 0.01
ADAM_STEP = 10

V7X_VMEM_LIMIT_BYTES = 56 * 1024 * 1024
ROW_TILE = 512
MESH = pl.DeviceIdType.MESH
NEG = -1e30

_NT = (((1,), (1,)), ((), ()))
_TN = (((0,), (0,)), ((), ()))


def _params(*semantics):
    return pltpu.CompilerParams(dimension_semantics=semantics, vmem_limit_bytes=V7X_VMEM_LIMIT_BYTES)


def _rows(tm, width):
    return pl.BlockSpec((tm, width), lambda i: (i, 0))


def _whole(shape):
    return pl.BlockSpec(shape, lambda *_: (0,) * len(shape))


def _gelu(x):
    t = jnp.tanh(0.7978845608028654 * (x + 0.044715 * (x * x * x)))
    return 0.5 * x * (1.0 + t)


def _gelu_grad(x):
    t = jnp.tanh(0.7978845608028654 * (x + 0.044715 * (x * x * x)))
    return 0.5 * (1.0 + t) + 0.5 * x * (1.0 - t * t) * (0.7978845608028654 * (1.0 + 3.0 * 0.044715 * (x * x)))


def _rstd(x):
    return lax.rsqrt(jnp.mean(x * x, axis=-1, keepdims=True) + EPS)


def _norm_bwd(dn, h, g):
    r = _rstd(h)
    n = h * r
    t = dn * g
    return r * (t - n * jnp.mean(t * n, axis=-1, keepdims=True)), dn * n


def _swap_halves(x, first_half):
    return jnp.where(first_half, pltpu.roll(x, 96, 1), pltpu.roll(x, 32, 1))


def _sub_spec(d, n_cb, tm):
    return pl.BlockSpec((d, n_cb, tm // d, 128), lambda i: (0, 0, i, 0))


def _sub_shape(s_len, d, n_cb, dtype):
    return jax.ShapeDtypeStruct((d, n_cb, s_len // d, 128), dtype)


def _to_sub(stage_ref, cb_src, out_ref, cb_dst, d, tm):
    slab = stage_ref.at[cb_src]
    for r in range(d):
        out_ref[r, cb_dst] = slab[pl.ds(r, tm // d, stride=d), :].astype(out_ref.dtype)


def _from_sub(in_ref, cb_src, stage_ref, cb_dst, d, tm, accumulate=False):
    slab = stage_ref.at[cb_dst]
    for r in range(d):
        rows = pl.ds(r, tm // d, stride=d)
        val = in_ref[r, cb_src].astype(F32)
        slab[rows, :] = slab[rows, :] + val if accumulate else val


def _inproj(x, g, w, cos, sin, exchange=()):
    s_len = x.shape[0]
    tm = ROW_TILE
    n_cb = 3 * N_PAIRS

    def body(x_ref, g_ref, w_ref, cos_ref, sin_ref, ua_ref, hn_ref, *rest):
        sub_refs, stage = rest[:-1], rest[-1]
        xf = x_ref[...]
        hn = (xf * _rstd(xf) * g_ref[...]).astype(BF16)
        hn_ref[...] = hn
        c = cos_ref[...]
        s = sin_ref[...]
        first_half = (lax.broadcasted_iota(jnp.int32, (tm, 128), 1) % HEAD_DIM) < HEAD_DIM // 2
        for j in range(IN_COLS // 256):
            col = j * 256
            acc = jnp.dot(hn, w_ref[:, col:col + 256], preferred_element_type=F32)
            if col < 2 * WIDTH_A:
                ua_ref[:, col:col + 256] = acc
                continue
            for half in range(2):
                cb = (col - 2 * WIDTH_A) // 128 + half
                t = acc[:, half * 128:(half + 1) * 128]
                if cb < 2 * N_PAIRS:
                    t = (t * c + _swap_halves(t, first_half) * s) * (0.125 if cb < N_PAIRS else 1.0)
                stage[cb] = t
        for cb in range(n_cb):
            for d, out_ref in zip(DILATIONS, sub_refs):
                _to_sub(stage, cb, out_ref, cb, d, tm)

    return _call(
        body, name="inproj", grid=(s_len // tm,),
        in_specs=[_rows(tm, D_MODEL), _whole((1, D_MODEL)), _whole((D_MODEL, IN_COLS)), _rows(tm, 128), _rows(tm, 128)],
        out_specs=[_rows(tm, 2 * WIDTH_A), _rows(tm, D_MODEL)] + [_sub_spec(d, n_cb, tm) for d in DILATIONS],
        out_shape=[jax.ShapeDtypeStruct((s_len, 2 * WIDTH_A), F32), jax.ShapeDtypeStruct((s_len, D_MODEL), BF16)]
        + [_sub_shape(s_len, d, n_cb, BF16) for d in DILATIONS],
        scratch_shapes=[pltpu.VMEM((n_cb, tm, 128), F32)],
        semantics=("parallel",), args=(x, g, w, cos, sin), exchange=exchange)


def _sgu_mix_weights(sw_ref):
    keep = lax.broadcasted_iota(jnp.int32, (CHUNK, CHUNK), 0) >= lax.broadcasted_iota(jnp.int32, (CHUNK, CHUNK), 1)
    return [jnp.where(keep, sw_ref[h], 0.0).astype(BF16) for h in range(4)], keep


def _sgu_core(ua_ref, gs_ref):
    u = ua_ref[:, :WIDTH_A]
    va = ua_ref[:, WIDTH_A:]
    vg = _gelu(va)
    xc = vg - jnp.mean(vg, axis=-1, keepdims=True)
    rstd = lax.rsqrt(jnp.mean(xc * xc, axis=-1, keepdims=True) + EPS)
    xhat = xc * rstd
    return u, va, _gelu(u), xhat, rstd, xhat * gs_ref[...]


def _sgu_fwd(ua, sw, b2, gs, ga):
    s_len = ua.shape[0]
    tm = ROW_TILE

    def body(ua_ref, sw_ref, b2_ref, gs_ref, ga_ref, out_ref):
        _, _, ug, _, _, vn = _sgu_core(ua_ref, gs_ref)
        wm, _ = _sgu_mix_weights(sw_ref)
        head = lax.broadcasted_iota(jnp.int32, (CHUNK, WIDTH_A), 1) // HEAD_DIM
        for c in range(tm // CHUNK):
            rows = slice(c * CHUNK, (c + 1) * CHUNK)
            vnc = vn[rows]
            mixed = b2_ref[...]
            for h in range(4):
                mixed = mixed + jnp.dot(wm[h], jnp.where(head == h, vnc, 0.0).astype(BF16), preferred_element_type=F32)
            ya = ug[rows] * mixed
            out_ref[rows, :] = (ya * _rstd(ya) * ga_ref[...]).astype(BF16)

    return pl.pallas_call(
        body, name="sgu_fwd", grid=(s_len // tm,),
        in_specs=[_rows(tm, 2 * WIDTH_A), _whole((4, CHUNK, CHUNK)), _whole((CHUNK, WIDTH_A)), _whole((1, WIDTH_A)), _whole((1, WIDTH_A))],
        out_specs=_rows(tm, WIDTH_A),
        out_shape=jax.ShapeDtypeStruct((s_len, WIDTH_A), BF16),
        compiler_params=_params("parallel"),
    )(ua, sw, b2, gs, ga)


def _attn_geometry(sd):
    tile = min(ROW_TILE, sd)
    return tile, tile // CHUNK, sd // tile


PAIRS_PER_STEP = 6


def _attn_spec(cb0, rows, row_index):
    return pl.BlockSpec((None, PAIRS_PER_STEP, rows, 128), lambda r, g, n: (r, cb0 // PAIRS_PER_STEP + g, row_index(n), 0))


def _per_pair(one_pair):
    def body(*refs):
        for hp in range(PAIRS_PER_STEP):
            one_pair(*[ref.at[hp] for ref in refs])
    return body


def _both_heads(x, head_a):
    zero = jnp.zeros_like(x)
    return [jnp.where(head_a, x, zero), jnp.where(head_a, zero, x)]


def _attn_fwd(qkv, d, exchange=()):
    sd = qkv.shape[2]
    tile, nb, n_tiles = _attn_geometry(sd)

    def prev(n):
        return jnp.maximum(n * nb - 1, 0)

    def one_pair(q_ref, k_ref, kp_ref, v_ref, vp_ref, o_ref, l_ref):
        n = pl.program_id(2)
        head_a = lax.broadcasted_iota(jnp.int32, (CHUNK, 128), 1) < HEAD_DIM
        qi = lax.broadcasted_iota(jnp.int32, (2 * CHUNK, 2 * CHUNK), 0) % CHUNK
        kc = lax.broadcasted_iota(jnp.int32, (2 * CHUNK, 2 * CHUNK), 1)
        band = (kc >= qi) & (kc <= qi + N_BACK)
        for j in range(nb):
            rows = slice(j * CHUNK, (j + 1) * CHUNK)
            if j == 0:
                kcat = jnp.concatenate([kp_ref[...], k_ref[rows, :]], axis=0)
                vcat = jnp.concatenate([vp_ref[...], v_ref[rows, :]], axis=0)
                valid = band & jnp.logical_or(n > 0, kc >= CHUNK)
            else:
                kcat = k_ref[(j - 1) * CHUNK:(j + 1) * CHUNK, :]
                vcat = v_ref[(j - 1) * CHUNK:(j + 1) * CHUNK, :]
                valid = band
            q2 = jnp.concatenate(_both_heads(q_ref[rows, :], head_a), axis=0)
            s = lax.dot_general(q2, kcat, _NT, preferred_element_type=F32)
            s = jnp.where(valid, s, NEG)
            m = jnp.max(s, axis=-1, keepdims=True)
            p = jnp.exp(s - m)
            l = jnp.sum(p, axis=-1, keepdims=True)
            o2 = jnp.dot(p.astype(BF16), vcat, preferred_element_type=F32) / l
            lse2 = m + jnp.log(l)
            o_ref[rows, :] = jnp.where(head_a, o2[:CHUNK], o2[CHUNK:]).astype(BF16)
            l_ref[rows, :] = jnp.where(head_a, lse2[:CHUNK], lse2[CHUNK:])

    same = lambda n: n
    return _call(
        _per_pair(one_pair), name=f"attn_fwd_d{d}", grid=(d, N_PAIRS // PAIRS_PER_STEP, n_tiles),
        in_specs=[_attn_spec(0, tile, same), _attn_spec(N_PAIRS, tile, same), _attn_spec(N_PAIRS, CHUNK, prev),
                  _attn_spec(2 * N_PAIRS, tile, same), _attn_spec(2 * N_PAIRS, CHUNK, prev)],
        out_specs=[_attn_spec(0, tile, same), _attn_spec(0, tile, same)],
        out_shape=[jax.ShapeDtypeStruct((d, N_PAIRS, sd, 128), BF16), jax.ShapeDtypeStruct((d, N_PAIRS, sd, 128), F32)],
        semantics=("parallel", "parallel", "parallel"), args=(qkv, qkv, qkv, qkv, qkv), exchange=exchange)


def _combine(outs, lses, ya_n, gb):
    s_len = ya_n.shape[0]
    tm = ROW_TILE
    n_br = len(DILATIONS)

    def body(*refs):
        o_refs, l_refs = refs[:n_br], refs[n_br:2 * n_br]
        ya_ref, gb_ref, y_ref, yb_ref = refs[2 * n_br:2 * n_br + 4]
        lse_refs = refs[2 * n_br + 4:3 * n_br + 4]
        o_nat, l_nat, lse_nat = refs[3 * n_br + 4:]
        sumsq = jnp.zeros((tm, 1), F32)
        for cb in range(N_PAIRS):
            for i, d in enumerate(DILATIONS):
                _from_sub(o_refs[i], cb, o_nat, i, d, tm)
                _from_sub(l_refs[i], cb, l_nat, i, d, tm)
            ls = [l_nat[i] for i in range(n_br)]
            top = jnp.maximum(jnp.maximum(ls[0], ls[1]), ls[2])
            ws = [jnp.exp(l - top) for l in ls]
            den = ws[0] + ws[1] + ws[2]
            inv = 1.0 / den
            yb = (ws[0] * inv) * o_nat[0] + (ws[1] * inv) * o_nat[1] + (ws[2] * inv) * o_nat[2]
            yb_ref[:, cb * 128:(cb + 1) * 128] = yb
            sumsq = sumsq + jnp.sum(yb * yb, axis=-1, keepdims=True)
            lse_nat[cb] = top + jnp.log(den)
            for d, lse_ref in zip(DILATIONS, lse_refs):
                _to_sub(lse_nat, cb, lse_ref, cb, d, tm)
        r = lax.rsqrt(sumsq / WIDTH_B + EPS)
        y_ref[:, :WIDTH_A] = ya_ref[...]
        y_ref[:, WIDTH_A:] = (yb_ref[...] * r * gb_ref[...]).astype(BF16)

    return pl.pallas_call(
        body, name="attn_combine", grid=(s_len // tm,),
        in_specs=[_sub_spec(d, N_PAIRS, tm) for d in DILATIONS] * 2 + [_rows(tm, WIDTH_A), _whole((1, WIDTH_B))],
        out_specs=[_rows(tm, D_MODEL), _rows(tm, WIDTH_B)] + [_sub_spec(d, N_PAIRS, tm) for d in DILATIONS],
        out_shape=[jax.ShapeDtypeStruct((s_len, D_MODEL), BF16), jax.ShapeDtypeStruct((s_len, WIDTH_B), F32)]
        + [_sub_shape(s_len, d, N_PAIRS, F32) for d in DILATIONS],
        scratch_shapes=[pltpu.VMEM((n_br, tm, 128), F32), pltpu.VMEM((n_br, tm, 128), F32), pltpu.VMEM((N_PAIRS, tm, 128), F32)],
        compiler_params=_params("parallel"),
    )(*outs, *lses, ya_n, gb)


def _mm_res(a, w, res, name):
    s_len, k = a.shape
    tm = ROW_TILE if k <= D_MODEL else ROW_TILE // 2

    def body(a_ref, w_ref, r_ref, o_ref):
        o_ref[...] = r_ref[...] + jnp.dot(a_ref[...].astype(BF16), w_ref[...], preferred_element_type=F32)

    return pl.pallas_call(
        body, name=name, grid=(s_len // tm,),
        in_specs=[_rows(tm, k), _whole((k, D_MODEL)), _rows(tm, D_MODEL)],
        out_specs=_rows(tm, D_MODEL),
        out_shape=jax.ShapeDtypeStruct((s_len, D_MODEL), F32),
        compiler_params=_params("parallel"),
    )(a, w, res)


def _ffn_up(h, g, wg, wu):
    s_len = h.shape[0]
    tm = ROW_TILE // 2

    def body(h_ref, g_ref, wg_ref, wu_ref, a_ref, b_ref, act_ref, hn_ref):
        hf = h_ref[...]
        hn = (hf * _rstd(hf) * g_ref[...]).astype(BF16)
        hn_ref[...] = hn
        for j in range(D_FF // 256):
            cols = slice(j * 256, (j + 1) * 256)
            a = jnp.dot(hn, wg_ref[:, cols], preferred_element_type=F32)
            b = jnp.dot(hn, wu_ref[:, cols], preferred_element_type=F32)
            a_ref[:, cols] = a.astype(BF16)
            b_ref[:, cols] = b.astype(BF16)
            act_ref[:, cols] = (a * jax.nn.sigmoid(a) * b).astype(BF16)

    wide = jax.ShapeDtypeStruct((s_len, D_FF), BF16)
    return pl.pallas_call(
        body, name="ffn_up", grid=(s_len // tm,),
        in_specs=[_rows(tm, D_MODEL), _whole((1, D_MODEL)), _whole((D_MODEL, D_FF)), _whole((D_MODEL, D_FF))],
        out_specs=[_rows(tm, D_FF), _rows(tm, D_FF), _rows(tm, D_FF), _rows(tm, D_MODEL)],
        out_shape=[wide, wide, wide, jax.ShapeDtypeStruct((s_len, D_MODEL), BF16)],
        compiler_params=_params("parallel"),
    )(h, g, wg, wu)


def _ple(h, g, wpg, p, wpp):
    s_len = h.shape[0]
    tm = ROW_TILE

    def body(h_ref, g_ref, wpg_ref, p_ref, wpp_ref, h3_ref, gate_ref, pp_ref, hn_ref):
        hf = h_ref[...]
        hn = (hf * _rstd(hf) * g_ref[...]).astype(BF16)
        hn_ref[...] = hn
        gate = jax.nn.sigmoid(jnp.dot(hn, wpg_ref[...], preferred_element_type=F32))
        pp = jnp.dot(p_ref[...].astype(BF16), wpp_ref[...], preferred_element_type=F32)
        h3_ref[...] = hf + gate * pp
        gate_ref[...] = gate.astype(BF16)
        pp_ref[...] = pp.astype(BF16)

    half = jax.ShapeDtypeStruct((s_len, D_MODEL), BF16)
    return pl.pallas_call(
        body, name="ple", grid=(s_len // tm,),
        in_specs=[_rows(tm, D_MODEL), _whole((1, D_MODEL)), _whole((D_MODEL, D_MODEL)), _rows(tm, PLE_DIM), _whole((PLE_DIM, D_MODEL))],
        out_specs=[_rows(tm, D_MODEL)] * 4,
        out_shape=[jax.ShapeDtypeStruct((s_len, D_MODEL), F32), half, half, half],
        compiler_params=_params("parallel"),
    )(h, g, wpg, p, wpp)


def _loss_bwd(h3, target, gf, gate, pp):
    s_len = h3.shape[0]
    tm = ROW_TILE

    def body(h_ref, t_ref, g_ref, gate_ref, pp_ref, dh_ref, dz_ref, dpp_ref, loss_ref, dg_ref):
        @pl.when(pl.program_id(0) == 0)
        def _():
            loss_ref[...] = jnp.zeros_like(loss_ref)
            dg_ref[...] = jnp.zeros_like(dg_ref)

        hf = h_ref[...]
        gfv = g_ref[...]
        err = hf * _rstd(hf) * gfv - t_ref[...]
        loss_ref[...] += 0.5 * jnp.sum(jnp.sum(err * err, axis=-1, keepdims=True), axis=0, keepdims=True) / D_MODEL
        dh, dg_rows = _norm_bwd(err / D_MODEL, hf, gfv)
        dg_ref[...] += jnp.sum(dg_rows, axis=0, keepdims=True)
        dh_ref[...] = dh
        gate = gate_ref[...].astype(F32)
        dz_ref[...] = (dh * pp_ref[...].astype(F32) * gate * (1.0 - gate)).astype(BF16)
        dpp_ref[...] = (dh * gate).astype(BF16)

    half = jax.ShapeDtypeStruct((s_len, D_MODEL), BF16)
    return pl.pallas_call(
        body, name="loss_bwd", grid=(s_len // tm,),
        in_specs=[_rows(tm, D_MODEL), _rows(tm, D_MODEL), _whole((1, D_MODEL)), _rows(tm, D_MODEL), _rows(tm, D_MODEL)],
        out_specs=[_rows(tm, D_MODEL), _rows(tm, D_MODEL), _rows(tm, D_MODEL), _whole((1, 128)), _whole((1, D_MODEL))],
        out_shape=[jax.ShapeDtypeStruct((s_len, D_MODEL), F32), half, half,
                   jax.ShapeDtypeStruct((1, 128), F32), jax.ShapeDtypeStruct((1, D_MODEL), F32)],
        compiler_params=_params("arbitrary"),
    )(h3, target, gf, gate, pp)


def _mm_norm_bwd(parts, h, g, dres, name, exchange=()):
    s_len = h.shape[0]
    tm = ROW_TILE // 2
    n_parts = len(parts)

    def body(*refs):
        a_refs = refs[0:2 * n_parts:2]
        w_refs = refs[1:2 * n_parts:2]
        h_ref, g_ref, r_ref, o_ref, dg_ref = refs[2 * n_parts:]

        @pl.when(pl.program_id(0) == 0)
        def _():
            dg_ref[...] = jnp.zeros_like(dg_ref)

        dn = jnp.dot(a_refs[0][...], w_refs[0][...], preferred_element_type=F32)
        for a_ref, w_ref in zip(a_refs[1:], w_refs[1:]):
            dn = dn + jnp.dot(a_ref[...], w_ref[...], preferred_element_type=F32)
        dh, dg_rows = _norm_bwd(dn, h_ref[...], g_ref[...])
        o_ref[...] = r_ref[...] + dh
        dg_ref[...] += jnp.sum(dg_rows, axis=0, keepdims=True)

    in_specs, args = [], []
    for a, w in parts:
        in_specs += [_rows(tm, a.shape[1]), _whole(w.shape)]
        args += [a, w]
    return _call(
        body, name=name, grid=(s_len // tm,),
        in_specs=in_specs + [_rows(tm, D_MODEL), _whole((1, D_MODEL)), _rows(tm, D_MODEL)],
        out_specs=[_rows(tm, D_MODEL), _whole((1, D_MODEL))],
        out_shape=[jax.ShapeDtypeStruct((s_len, D_MODEL), F32), jax.ShapeDtypeStruct((1, D_MODEL), F32)],
        semantics=("arbitrary",), args=(*args, h, g, dres), exchange=exchange)


def _ffn_down_bwd(dh, wdt, a, b, exchange=()):
    s_len = dh.shape[0]
    tm = ROW_TILE // 2

    def body(dh_ref, w_ref, a_ref, b_ref, da_ref, db_ref):
        dhb = dh_ref[...].astype(BF16)
        for j in range(D_FF // 256):
            cols = slice(j * 256, (j + 1) * 256)
            dact = jnp.dot(dhb, w_ref[:, cols], preferred_element_type=F32)
            av = a_ref[:, cols].astype(F32)
            bv = b_ref[:, cols].astype(F32)
            sig = jax.nn.sigmoid(av)
            da_ref[:, cols] = (dact * bv * sig * (1.0 + av * (1.0 - sig))).astype(BF16)
            db_ref[:, cols] = (dact * av * sig).astype(BF16)

    wide = jax.ShapeDtypeStruct((s_len, D_FF), BF16)
    return _call(
        body, name="ffn_down_bwd", grid=(s_len // tm,),
        in_specs=[_rows(tm, D_MODEL), _whole((D_MODEL, D_FF)), _rows(tm, D_FF), _rows(tm, D_FF)],
        out_specs=[_rows(tm, D_FF), _rows(tm, D_FF)],
        out_shape=[wide, wide],
        semantics=("parallel",), args=(dh, wdt, a, b), exchange=exchange)


def _outproj_bwd(dh1, woutt, yb, gb, head_sum, exchange=()):
    s_len = dh1.shape[0]
    tm = ROW_TILE
    n_br = len(DILATIONS)

    def body(dh_ref, w_ref, yb_ref, gb_ref, e_ref, dya_ref, dgb_ref, *rest):
        do_refs, dd_refs = rest[:n_br], rest[n_br:2 * n_br]
        do_nat, dd_nat = rest[2 * n_br:]

        @pl.when(pl.program_id(0) == 0)
        def _():
            dgb_ref[...] = jnp.zeros_like(dgb_ref)

        dhb = dh_ref[...].astype(BF16)
        dya_ref[...] = jnp.dot(dhb, w_ref[:, :WIDTH_A], preferred_element_type=F32)
        dyn = jnp.dot(dhb, w_ref[:, WIDTH_A:], preferred_element_type=F32)
        ybv = yb_ref[...]
        dyb, dg_rows = _norm_bwd(dyn, ybv, gb_ref[...])
        dgb_ref[...] += jnp.sum(dg_rows, axis=0, keepdims=True)
        prod = dyb * ybv
        for cb in range(N_PAIRS):
            cols = slice(cb * 128, (cb + 1) * 128)
            pc = prod[:, cols]
            hi = pc.astype(BF16)
            lo = (pc - hi.astype(F32)).astype(BF16)
            do_nat[cb] = dyb[:, cols]
            dd_nat[cb] = (jnp.dot(hi, e_ref[...], preferred_element_type=F32)
                          + jnp.dot(lo, e_ref[...], preferred_element_type=F32))
            for i, d in enumerate(DILATIONS):
                _to_sub(do_nat, cb, do_refs[i], cb, d, tm)
                _to_sub(dd_nat, cb, dd_refs[i], cb, d, tm)

    subs = [_sub_spec(d, N_PAIRS, tm) for d in DILATIONS]
    return _call(
        body, name="outproj_bwd", grid=(s_len // tm,),
        in_specs=[_rows(tm, D_MODEL), _whole((D_MODEL, D_MODEL)), _rows(tm, WIDTH_B), _whole((1, WIDTH_B)), _whole((128, 128))],
        out_specs=[_rows(tm, WIDTH_A), _whole((1, WIDTH_B))] + subs + subs,
        out_shape=[jax.ShapeDtypeStruct((s_len, WIDTH_A), F32), jax.ShapeDtypeStruct((1, WIDTH_B), F32)]
        + [_sub_shape(s_len, d, N_PAIRS, BF16) for d in DILATIONS] + [_sub_shape(s_len, d, N_PAIRS, F32) for d in DILATIONS],
        scratch_shapes=[pltpu.VMEM((N_PAIRS, tm, 128), F32), pltpu.VMEM((N_PAIRS, tm, 128), F32)],
        semantics=("arbitrary",), args=(dh1, woutt, yb, gb, head_sum), exchange=exchange)


def _attn_bwd(qkv, do, lse, dd, d, exchange=()):
    sd = qkv.shape[2]
    tile, nb, n_tiles = _attn_geometry(sd)
    last_block = sd // CHUNK - 1

    def nxt(n):
        return jnp.minimum((n + 1) * nb, last_block)

    def one_pair(q_ref, qn_ref, k_ref, v_ref, do_ref, don_ref, l_ref, ln_ref, dd_ref, ddn_ref,
                 dq_ref, dk_ref, dv_ref, carry_ref):
        n = pl.program_id(2)

        @pl.when(n == 0)
        def _():
            carry_ref[...] = jnp.zeros_like(carry_ref)

        head_a = lax.broadcasted_iota(jnp.int32, (CHUNK, 128), 1) < HEAD_DIM
        row = lax.broadcasted_iota(jnp.int32, (4 * CHUNK, CHUNK), 0)
        qi = row % CHUNK
        ki = lax.broadcasted_iota(jnp.int32, (4 * CHUNK, CHUNK), 1)
        is_after = row >= 2 * CHUNK
        mask = (is_after & (ki >= qi)) | (jnp.logical_not(is_after) & (qi >= ki))
        mask_last = mask & jnp.logical_or(jnp.logical_not(is_after), n < n_tiles - 1)
        dq_acc = [carry_ref[...]] + [jnp.zeros((CHUNK, 128), F32) for _ in range(nb)]

        def columns(x):
            return [x[:, 0:1], x[:, HEAD_DIM:HEAD_DIM + 1]]

        for j in range(nb):
            rows = slice(j * CHUNK, (j + 1) * CHUNK)
            kj = k_ref[rows, :]
            vj = v_ref[rows, :]
            if j + 1 < nb:
                nrows = slice((j + 1) * CHUNK, (j + 2) * CHUNK)
                q2, do2, l2, dd2, msk = q_ref[nrows, :], do_ref[nrows, :], l_ref[nrows, :], dd_ref[nrows, :], mask
            else:
                q2, do2, l2, dd2, msk = qn_ref[...], don_ref[...], ln_ref[...], ddn_ref[...], mask_last
            qs = jnp.concatenate(_both_heads(q_ref[rows, :], head_a) + _both_heads(q2, head_a), axis=0)
            dos = jnp.concatenate(_both_heads(do_ref[rows, :], head_a) + _both_heads(do2, head_a), axis=0)
            ls = jnp.concatenate(columns(l_ref[rows, :]) + columns(l2), axis=0)
            dds = jnp.concatenate(columns(dd_ref[rows, :]) + columns(dd2), axis=0)
            s = lax.dot_general(qs, kj, _NT, preferred_element_type=F32)
            p = jnp.exp(jnp.where(msk, s - ls, NEG))
            dp = lax.dot_general(dos, vj, _NT, preferred_element_type=F32)
            ds = (p * (dp - dds)).astype(BF16)
            dv_ref[rows, :] = lax.dot_general(p.astype(BF16), dos, _TN, preferred_element_type=F32).astype(BF16)
            dk_ref[rows, :] = lax.dot_general(ds, qs, _TN, preferred_element_type=F32).astype(BF16)
            dqs = jnp.dot(ds, kj, preferred_element_type=F32)
            dq_acc[j] = dq_acc[j] + jnp.where(head_a, dqs[:CHUNK], dqs[CHUNK:2 * CHUNK])
            dq_acc[j + 1] = dq_acc[j + 1] + jnp.where(head_a, dqs[2 * CHUNK:3 * CHUNK], dqs[3 * CHUNK:])
        for j in range(nb):
            dq_ref[j * CHUNK:(j + 1) * CHUNK, :] = dq_acc[j].astype(BF16)
        carry_ref[...] = dq_acc[nb]

    same = lambda n: n
    grad = jax.ShapeDtypeStruct((d, N_PAIRS, sd, 128), BF16)
    return _call(
        _per_pair(one_pair), name=f"attn_bwd_d{d}", grid=(d, N_PAIRS // PAIRS_PER_STEP, n_tiles),
        in_specs=[_attn_spec(0, tile, same), _attn_spec(0, CHUNK, nxt), _attn_spec(N_PAIRS, tile, same),
                  _attn_spec(2 * N_PAIRS, tile, same), _attn_spec(0, tile, same), _attn_spec(0, CHUNK, nxt),
                  _attn_spec(0, tile, same), _attn_spec(0, CHUNK, nxt), _attn_spec(0, tile, same), _attn_spec(0, CHUNK, nxt)],
        out_specs=[_attn_spec(0, tile, same)] * 3,
        out_shape=[grad, grad, grad],
        scratch_shapes=[pltpu.VMEM((PAIRS_PER_STEP, CHUNK, 128), F32)],
        semantics=("parallel", "parallel", "arbitrary"), args=(qkv, qkv, qkv, qkv, do, do, lse, lse, dd, dd), exchange=exchange)


def _sgu_bwd(ua, sw, b2, gs, ga, dya_n):
    s_len = ua.shape[0]
    tm = ROW_TILE

    def body(ua_ref, sw_ref, b2_ref, gs_ref, ga_ref, dy_ref, dua_ref, dsw_ref, db2_ref, dgs_ref, dga_ref):
        @pl.when(pl.program_id(0) == 0)
        def _():
            dsw_ref[...] = jnp.zeros_like(dsw_ref)
            db2_ref[...] = jnp.zeros_like(db2_ref)
            dgs_ref[...] = jnp.zeros_like(dgs_ref)
            dga_ref[...] = jnp.zeros_like(dga_ref)

        u, va, ug, xhat, rstd, vn = _sgu_core(ua_ref, gs_ref)
        wm, keep = _sgu_mix_weights(sw_ref)
        head = lax.broadcasted_iota(jnp.int32, (CHUNK, WIDTH_A), 1) // HEAD_DIM
        gav = ga_ref[...]
        gsv = gs_ref[...]
        dga = jnp.zeros((1, WIDTH_A), F32)
        dgs = jnp.zeros((1, WIDTH_A), F32)
        db2 = jnp.zeros((CHUNK, WIDTH_A), F32)
        dsw = [jnp.zeros((CHUNK, CHUNK), F32) for _ in range(4)]
        for c in range(tm // CHUNK):
            rows = slice(c * CHUNK, (c + 1) * CHUNK)
            vnc = vn[rows]
            vnb = vnc.astype(BF16)
            mixed = b2_ref[...]
            for h in range(4):
                mixed = mixed + jnp.dot(wm[h], jnp.where(head == h, vnc, 0.0).astype(BF16), preferred_element_type=F32)
            ugc = ug[rows]
            dya, dga_rows = _norm_bwd(dy_ref[rows, :], ugc * mixed, gav)
            dga = dga + jnp.sum(dga_rows, axis=0, keepdims=True)
            dmixed = dya * ugc
            db2 = db2 + dmixed
            dvn = jnp.zeros((CHUNK, WIDTH_A), F32)
            for h in range(4):
                dmh = jnp.where(head == h, dmixed, 0.0).astype(BF16)
                dsw[h] = dsw[h] + lax.dot_general(dmh, vnb, _NT, preferred_element_type=F32)
                dvn = dvn + lax.dot_general(wm[h], dmh, _TN, preferred_element_type=F32)
            xh = xhat[rows]
            dgs = dgs + jnp.sum(dvn * xh, axis=0, keepdims=True)
            dxh = dvn * gsv
            dvg = rstd[rows] * (dxh - jnp.mean(dxh, axis=-1, keepdims=True) - xh * jnp.mean(dxh * xh, axis=-1, keepdims=True))
            dua_ref[rows, :WIDTH_A] = (dya * mixed * _gelu_grad(u[rows])).astype(BF16)
            dua_ref[rows, WIDTH_A:] = (dvg * _gelu_grad(va[rows])).astype(BF16)
        for h in range(4):
            dsw_ref[h] += jnp.where(keep, dsw[h], 0.0)
        db2_ref[...] += db2
        dgs_ref[...] += dgs
        dga_ref[...] += dga

    return pl.pallas_call(
        body, name="sgu_bwd", grid=(s_len // tm,),
        in_specs=[_rows(tm, 2 * WIDTH_A), _whole((4, CHUNK, CHUNK)), _whole((CHUNK, WIDTH_A)), _whole((1, WIDTH_A)),
                  _whole((1, WIDTH_A)), _rows(tm, WIDTH_A)],
        out_specs=[_rows(tm, 2 * WIDTH_A), _whole((4, CHUNK, CHUNK)), _whole((CHUNK, WIDTH_A)), _whole((1, WIDTH_A)), _whole((1, WIDTH_A))],
        out_shape=[jax.ShapeDtypeStruct((s_len, 2 * WIDTH_A), BF16), jax.ShapeDtypeStruct((4, CHUNK, CHUNK), F32),
                   jax.ShapeDtypeStruct((CHUNK, WIDTH_A), F32), jax.ShapeDtypeStruct((1, WIDTH_A), F32),
                   jax.ShapeDtypeStruct((1, WIDTH_A), F32)],
        compiler_params=_params("arbitrary"),
    )(ua, sw, b2, gs, ga, dya_n)


def _dproj(dua, dqs, dks, dvs, cos, sin):
    s_len = dua.shape[0]
    tm = ROW_TILE
    n_br = len(DILATIONS)

    def body(dua_ref, *rest):
        groups = [rest[g * n_br:(g + 1) * n_br] for g in range(3)]
        cos_ref, sin_ref, out_ref, acc = rest[3 * n_br:]
        out_ref[:, :2 * WIDTH_A] = dua_ref[...]
        c = cos_ref[...]
        s = sin_ref[...]
        first_half = (lax.broadcasted_iota(jnp.int32, (tm, 128), 1) % HEAD_DIM) < HEAD_DIM // 2
        for g, refs in enumerate(groups):
            for cb in range(N_PAIRS):
                for i, d in enumerate(DILATIONS):
                    _from_sub(refs[i], cb, acc, 0, d, tm, accumulate=i > 0)
                t = acc[0]
                if g < 2:
                    t = (t * c - _swap_halves(t, first_half) * s) * (0.125 if g == 0 else 1.0)
                col = 2 * WIDTH_A + g * WIDTH_B + cb * 128
                out_ref[:, col:col + 128] = t.astype(BF16)

    subs = [_sub_spec(d, N_PAIRS, tm) for d in DILATIONS]
    return pl.pallas_call(
        body, name="dproj", grid=(s_len // tm,),
        in_specs=[_rows(tm, 2 * WIDTH_A)] + subs * 3 + [_rows(tm, 128), _rows(tm, 128)],
        out_specs=_rows(tm, IN_COLS),
        out_shape=jax.ShapeDtypeStruct((s_len, IN_COLS), BF16),
        scratch_shapes=[pltpu.VMEM((1, tm, 128), F32)],
        compiler_params=_params("parallel"),
    )(dua, *dqs, *dks, *dvs, cos, sin)


def _mm_tn(a, b, name, exchange=()):
    s_len, m = a.shape
    n = b.shape[1]
    tk = ROW_TILE
    tm = m if m <= 512 else (1408 if m == D_FF else 512)
    n_k = s_len // tk

    def body(a_ref, b_ref, o_ref, acc_ref):
        k = pl.program_id(1)

        @pl.when(k == 0)
        def _():
            acc_ref[...] = jnp.zeros_like(acc_ref)

        acc_ref[...] += lax.dot_general(a_ref[...].astype(BF16), b_ref[...].astype(BF16), _TN, preferred_element_type=F32)

        @pl.when(k == n_k - 1)
        def _():
            o_ref[...] = acc_ref[...].astype(BF16)

    (grad,), received = _call(
        body, name=name, grid=(m // tm, n_k),
        in_specs=[pl.BlockSpec((tk, tm), lambda i, k: (k, i)), pl.BlockSpec((tk, n), lambda i, k: (k, 0))],
        out_specs=[pl.BlockSpec((tm, n), lambda i, k: (i, 0))],
        out_shape=[jax.ShapeDtypeStruct((m, n), BF16)],
        scratch_shapes=[pltpu.VMEM((tm, n), F32)],
        semantics=("parallel", "arbitrary"), args=(a, b), exchange=exchange)
    return grad, received


def _position():
    x, y, c = lax.axis_index("x"), lax.axis_index("y"), lax.axis_index("c")
    return x, y, c, 4 * x + 2 * y + c


def _peer(x, y, c, rel):
    return (x ^ ((rel >> 2) & 1), y ^ ((rel >> 1) & 1), c ^ (rel & 1))


def _exchange_out_shape(kind, arr):
    return jax.ShapeDtypeStruct(((N_DEV,) + arr.shape) if kind == "gather" else arr.shape, arr.dtype)


def _exchange_sems(n_items):
    return [pltpu.SemaphoreType.DMA((n_items, N_DEV)), pltpu.SemaphoreType.DMA((n_items, N_DEV)), pltpu.SemaphoreType.DMA((n_items,))]


def _exchange_copies(kinds, srcs, dsts, sems):
    send_sems, recv_sems, local_sems = sems
    x, y, c, me = _position()
    local, sends, recvs = [], [], []
    for k, (kind, src, dst) in enumerate(zip(kinds, srcs, dsts)):
        own = src if kind == "gather" else src.at[me]
        local.append(pltpu.make_async_copy(own, dst.at[me], local_sems.at[k]))
        for rel in range(1, N_DEV):
            going = src if kind == "gather" else src.at[me ^ rel]
            common = dict(send_sem=send_sems.at[k, rel], recv_sem=recv_sems.at[k, rel],
                          device_id=_peer(x, y, c, rel), device_id_type=MESH)
            sends.append(pltpu.make_async_remote_copy(src_ref=going, dst_ref=dst.at[me], **common))
            recvs.append(pltpu.make_async_remote_copy(src_ref=own, dst_ref=dst.at[me ^ rel], **common))
    return local, sends, recvs


def _exchange_start(kinds, srcs, dsts, sems):
    local, sends, _ = _exchange_copies(kinds, srcs, dsts, sems)
    for cp in local + sends:
        cp.start()


def _exchange_finish(kinds, srcs, dsts, sems):
    local, sends, recvs = _exchange_copies(kinds, srcs, dsts, sems)
    for cp in recvs:
        cp.wait_recv()
    for cp in sends:
        cp.wait_send()
    for cp in local:
        cp.wait()


def _exchange_only(items, name):
    kinds = [k for k, _ in items]
    n = len(items)

    def body(*refs):
        srcs, dsts, sems = refs[:n], refs[n:2 * n], refs[2 * n:]
        _exchange_start(kinds, srcs, dsts, sems)
        _exchange_finish(kinds, srcs, dsts, sems)

    any_spec = pl.BlockSpec(memory_space=pl.ANY)
    return pl.pallas_call(
        body, name=name, in_specs=[any_spec] * n, out_specs=[any_spec] * n,
        out_shape=[_exchange_out_shape(k, a) for k, a in items],
        scratch_shapes=_exchange_sems(n),
        compiler_params=pltpu.CompilerParams(has_side_effects=True),
    )(*[a for _, a in items])


def _call(body, *, name, grid, in_specs, out_specs, out_shape, args, scratch_shapes=(), semantics, exchange=()):
    if not exchange:
        outs = pl.pallas_call(body, name=name, grid=grid, in_specs=in_specs, out_specs=out_specs, out_shape=out_shape,
                              scratch_shapes=list(scratch_shapes), compiler_params=_params(*semantics))(*args)
        return outs, []
    kinds = [k for k, _ in exchange]
    n_in, n_out, n_x, n_scr = len(in_specs), len(out_specs), len(exchange), len(scratch_shapes)

    def wrapped(*refs):
        ins, refs = refs[:n_in], refs[n_in:]
        srcs, refs = refs[:n_x], refs[n_x:]
        outs, refs = refs[:n_out], refs[n_out:]
        dsts, refs = refs[:n_x], refs[n_x:]
        scratch, sems = refs[:n_scr], refs[n_scr:]
        ids = [pl.program_id(a) for a in range(len(grid))]
        first = functools.reduce(jnp.logical_and, [i == 0 for i in ids])
        last = functools.reduce(jnp.logical_and, [i == g - 1 for i, g in zip(ids, grid)])

        @pl.when(first)
        def _():
            _exchange_start(kinds, srcs, dsts, sems)

        body(*ins, *outs, *scratch)

        @pl.when(last)
        def _():
            _exchange_finish(kinds, srcs, dsts, sems)

    any_spec = pl.BlockSpec(memory_space=pl.ANY)
    outs = pl.pallas_call(
        wrapped, name=name, grid=grid,
        in_specs=list(in_specs) + [any_spec] * n_x, out_specs=list(out_specs) + [any_spec] * n_x,
        out_shape=list(out_shape) + [_exchange_out_shape(k, a) for k, a in exchange],
        scratch_shapes=list(scratch_shapes) + _exchange_sems(n_x),
        compiler_params=pltpu.CompilerParams(dimension_semantics=("arbitrary",) * len(grid),
                                             vmem_limit_bytes=V7X_VMEM_LIMIT_BYTES, has_side_effects=True),
    )(*args, *[a for _, a in exchange])
    return outs[:n_out], outs[n_out:]


def _adamw_math(w, g, m, v):
    m = ADAM_B1 * m + (1.0 - ADAM_B1) * g
    v = ADAM_B2 * v + (1.0 - ADAM_B2) * (g * g)
    m_hat = m / (1.0 - ADAM_B1 ** ADAM_STEP)
    v_hat = v / (1.0 - ADAM_B2 ** ADAM_STEP)
    return -ADAM_LR * (m_hat / (jnp.sqrt(v_hat) + ADAM_EPS) + ADAM_WD * w), m, v


def _adamw(parts, w, m, v, name):
    rows, cols = w.shape
    tm = 256 if rows % 256 == 0 and rows > 256 else rows

    def body(p_ref, w_ref, m_ref, v_ref, g_ref, d_ref, nm_ref, nv_ref):
        g = p_ref[0].astype(F32)
        for j in range(1, N_DEV):
            g = g + p_ref[j].astype(F32)
        delta, nm, nv = _adamw_math(w_ref[...], g, m_ref[...], v_ref[...])
        g_ref[...] = g
        d_ref[...] = delta
        nm_ref[...] = nm
        nv_ref[...] = nv

    shard = jax.ShapeDtypeStruct((rows, cols), F32)
    return pl.pallas_call(
        body, name=name, grid=(rows // tm,),
        in_specs=[pl.BlockSpec((N_DEV, tm, cols), lambda i: (0, i, 0))] + [_rows(tm, cols)] * 3,
        out_specs=[_rows(tm, cols)] * 4,
        out_shape=[shard] * 4,
        compiler_params=_params("parallel"),
    )(parts, w, m, v)


_SMALL = ("mix_norm_g", "sgu_w", "sgu_b", "sgu_norm_g", "out_norm_a", "out_norm_b", "ffn_norm_g", "ple_norm_g", "final_norm_g")
_BIG = ("w_in", "w_out", "w_gate", "w_up", "w_down", "w_ple_gate", "w_ple_proj")
_COLUMN_SHARDED = ("w_in", "w_gate", "w_up", "w_ple_proj")
_ORDER = ("mix_norm_g", "w_in", "sgu_w", "sgu_b", "sgu_norm_g", "out_norm_a", "out_norm_b", "w_out", "ffn_norm_g",
          "w_gate", "w_up", "w_down", "ple_norm_g", "w_ple_gate", "w_ple_proj", "final_norm_g")


def _pack_small(values):
    flat = jnp.concatenate([values[n].reshape(-1).astype(F32) for n in _SMALL])
    pad = (-flat.shape[0]) % (8 * 128)
    return jnp.pad(flat, (0, pad)).reshape(-1, 128)


def _unpack_small(packed, like):
    flat = packed.reshape(-1)
    out, at = {}, 0
    for n in _SMALL:
        size = like[n].size
        out[n] = flat[at:at + size].reshape(like[n].shape)
        at += size
    return out


def _full_from_gathered(name, gathered):
    if name in _COLUMN_SHARDED:
        k, n = gathered.shape[1], gathered.shape[2] * N_DEV
        return gathered.transpose(1, 0, 2).reshape(k, n), gathered.transpose(0, 2, 1).reshape(n, k)
    k, n = gathered.shape[1] * N_DEV, gathered.shape[2]
    return gathered.reshape(k, n), gathered.transpose(2, 0, 1).reshape(n, k)


def _sliced_for_devices(name, grad):
    k, n = grad.shape
    if name in _COLUMN_SHARDED:
        return grad.reshape(k, N_DEV, n // N_DEV).transpose(1, 0, 2)
    return grad.reshape(N_DEV, k // N_DEV, n)


def _rope_tables(s_len):
    half = HEAD_DIM // 2
    inv = ROPE_THETA ** (-jnp.arange(half, dtype=F32) / half)
    ang = jnp.arange(s_len, dtype=F32)[:, None] * inv[None, :]
    cos, sin = jnp.cos(ang), jnp.sin(ang)
    return jnp.concatenate([cos, cos, cos, cos], axis=1), jnp.concatenate([-sin, sin, -sin, sin], axis=1)


def _forward_backward(x, p, target, small, shards):
    def gather(*names):
        return [("gather", shards[n]) for n in names]

    def scatter(**grads):
        return [("scatter", _sliced_for_devices(n, g)) for n, g in grads.items()]

    full, parts = {}, {}
    (got,) = _exchange_only(gather("w_in"), "gather_w_in")
    full["w_in"] = _full_from_gathered("w_in", got)

    s_len = x.shape[0]
    cos, sin = _rope_tables(s_len)
    g_mix, g_ffn, g_ple = small["mix_norm_g"], small["ffn_norm_g"], small["ple_norm_g"]
    g_fin = small["final_norm_g"].reshape(1, D_MODEL)
    sw, gs, ga, gb = small["sgu_w"], small["sgu_norm_g"], small["out_norm_a"], small["out_norm_b"]
    b2 = jnp.repeat(small["sgu_b"].T, HEAD_DIM, axis=1)
    lane_head = jnp.arange(128) // HEAD_DIM
    head_sum = (lane_head[:, None] == lane_head[None, :]).astype(BF16)
    n_br = len(DILATIONS)

    first_wave = ("w_out", "w_ple_gate", "w_ple_proj")
    (ua, hn1, *qkv), got = _inproj(x, g_mix, full["w_in"][0], cos, sin, exchange=gather(*first_wave))
    for n, g in zip(first_wave, got):
        full[n] = _full_from_gathered(n, g)
    ya_n = _sgu_fwd(ua, sw, b2, gs, ga)
    branch = []
    for i, (d, n) in enumerate(zip(DILATIONS, ("w_gate", "w_up", "w_down"))):
        o_l, (g,) = _attn_fwd(qkv[i], d, exchange=gather(n))
        branch.append(o_l)
        full[n] = _full_from_gathered(n, g)
    y, yb, *lse = _combine([o for o, _ in branch], [l for _, l in branch], ya_n, gb)
    h1 = _mm_res(y, full["w_out"][0], x, "out_proj")
    a, b, act, hn2 = _ffn_up(h1, g_ffn, full["w_gate"][0], full["w_up"][0])
    h2 = _mm_res(act, full["w_down"][0], h1, "ffn_down")
    h3, gate, pp, hn3 = _ple(h2, g_ple, full["w_ple_gate"][0], p, full["w_ple_proj"][0])

    dh3, dz, dpp, loss, d_fin = _loss_bwd(h3, target, g_fin, gate, pp)
    g_ple_proj, _ = _mm_tn(p, dpp, "dw_ple_proj")
    g_ple_gate, _ = _mm_tn(hn3, dz, "dw_ple_gate")
    (dh2, d_ple), _ = _mm_norm_bwd([(dz, full["w_ple_gate"][1])], h2, g_ple, dh3, "ple_bwd")
    g_down, (parts["w_ple_gate"], parts["w_ple_proj"]) = _mm_tn(
        act, dh2, "dw_down", exchange=scatter(w_ple_gate=g_ple_gate, w_ple_proj=g_ple_proj))
    (da, db), (parts["w_down"],) = _ffn_down_bwd(dh2, full["w_down"][1], a, b, exchange=scatter(w_down=g_down))
    g_gate, _ = _mm_tn(hn2, da, "dw_gate")
    g_up, _ = _mm_tn(hn2, db, "dw_up")
    (dh1, d_ffn), (parts["w_gate"],) = _mm_norm_bwd(
        [(da, full["w_gate"][1]), (db, full["w_up"][1])], h1, g_ffn, dh2, "ffn_up_bwd", exchange=scatter(w_gate=g_gate))
    g_out, _ = _mm_tn(y, dh1, "dw_out")
    (dya_n, d_gb, *do_dd), (parts["w_up"],) = _outproj_bwd(dh1, full["w_out"][1], yb, gb, head_sum, exchange=scatter(w_up=g_up))
    grads_b = []
    for i, d in enumerate(DILATIONS):
        g3, got = _attn_bwd(qkv[i], do_dd[i], lse[i], do_dd[n_br + i], d, exchange=scatter(w_out=g_out) if i == 0 else ())
        grads_b.append(g3)
        if i == 0:
            (parts["w_out"],) = got
    dua, d_sw, d_b2, d_gs, d_ga = _sgu_bwd(ua, sw, b2, gs, ga, dya_n)
    dproj = _dproj(dua, [g[0] for g in grads_b], [g[1] for g in grads_b], [g[2] for g in grads_b], cos, sin)
    g_in, _ = _mm_tn(hn1, dproj, "dw_in")
    (dx, d_mix), (parts["w_in"],) = _mm_norm_bwd(
        [(dproj, full["w_in"][1])], x, g_mix, dh1, "inproj_bwd", exchange=scatter(w_in=g_in))

    small_grads = {
        "mix_norm_g": d_mix, "sgu_w": d_sw, "sgu_b": d_b2.reshape(CHUNK, 4, HEAD_DIM).sum(axis=-1).T,
        "sgu_norm_g": d_gs, "out_norm_a": d_ga, "out_norm_b": d_gb, "ffn_norm_g": d_ffn, "ple_norm_g": d_ple,
        "final_norm_g": d_fin,
    }
    (small_parts,) = _exchange_only([("gather", _pack_small(small_grads))], "gather_small_grads")
    return loss[0, 0], dx, parts, small_parts


def kernel(x, p, mix_norm_g, w_in, sgu_w, sgu_b, sgu_norm_g, out_norm_a, out_norm_b, w_out, ffn_norm_g, w_gate, w_up, w_down, ple_norm_g, w_ple_gate, w_ple_proj, final_norm_g, loss_target, m_mix_norm_g, m_w_in, m_sgu_w, m_sgu_b, m_sgu_norm_g, m_out_norm_a, m_out_norm_b, m_w_out, m_ffn_norm_g, m_w_gate, m_w_up, m_w_down, m_ple_norm_g, m_w_ple_gate, m_w_ple_proj, m_final_norm_g, v_mix_norm_g, v_w_in, v_sgu_w, v_sgu_b, v_sgu_norm_g, v_out_norm_a, v_out_norm_b, v_w_out, v_ffn_norm_g, v_w_gate, v_w_up, v_w_down, v_ple_norm_g, v_w_ple_gate, v_w_ple_proj, v_final_norm_g):
    given = dict(locals())
    weights = {n: given[n] for n in _ORDER}
    moments_m = {n: given["m_" + n] for n in _ORDER}
    moments_v = {n: given["v_" + n] for n in _ORDER}

    shards = {n: weights[n][0].astype(BF16) for n in _BIG}
    small = {n: (weights[n][0] if n in ("sgu_w", "sgu_b") else weights[n]) for n in _SMALL}

    loss, dx, parts, small_parts = _forward_backward(x[0], p[0, 0], loss_target[0], small, shards)
    loss = lax.psum(loss, ("x", "y", "c"))

    small_like = {n: weights[n] for n in _SMALL}
    grads, deltas, new_m, new_v = {}, {}, {}, {}
    for n in _BIG:
        g, d, nm, nv = _adamw(parts[n], weights[n][0], moments_m[n][0], moments_v[n][0], "adamw_" + n)
        grads[n], deltas[n], new_m[n], new_v[n] = g[None], d[None], nm[None], nv[None]
    g, d, nm, nv = _adamw(small_parts, _pack_small(small_like), _pack_small({n: moments_m[n] for n in _SMALL}),
                          _pack_small({n: moments_v[n] for n in _SMALL}), "adamw_small")
    for out, packed in ((grads, g), (deltas, d), (new_m, nm), (new_v, nv)):
        out.update(_unpack_small(packed, small_like))

    return (loss, dx[None], *[grads[n] for n in _ORDER], *[deltas[n] for n in _ORDER],
            *[new_m[n] for n in _ORDER], *[new_v[n] for n in _ORDER])
```

```python
import functools

import jax
import jax.numpy as jnp
from jax import lax
from jax.experimental import pallas as pl
from jax.experimental.pallas import tpu as pltpu

F32 = jnp.float32
BF16 = jnp.bfloat16

D_MODEL = 1024
WIDTH_A = 256
WIDTH_B = 768
D_FF = 2816
IN_COLS = 2 * WIDTH_A + 3 * WIDTH_B
PLE_DIM = 256
HEAD_DIM = 64
N_PAIRS = WIDTH_B // 128
CHUNK = 128
N_BACK = 128
DILATIONS = (1, 4, 16)
ROPE_THETA = 10000.0
EPS = 1e-6
N_DEV = 8

ADAM_LR = 0.001
ADAM_B1 = 0.9
ADAM_B2 = 0.999
ADAM_EPS = 1e-08
ADAM_WD = 0.01
ADAM_STEP = 10

V7X_VMEM_LIMIT_BYTES = 56 * 1024 * 1024
ROW_TILE = 512
MESH = pl.DeviceIdType.MESH
NEG = -1e30

_NT = (((1,), (1,)), ((), ()))
_TN = (((0,), (0,)), ((), ()))


def _params(*semantics):
    return pltpu.CompilerParams(dimension_semantics=semantics, vmem_limit_bytes=V7X_VMEM_LIMIT_BYTES)


def _rows(tm, width):
    return pl.BlockSpec((tm, width), lambda i: (i, 0))


def _whole(shape):
    return pl.BlockSpec(shape, lambda *_: (0,) * len(shape))


def _gelu(x):
    t = jnp.tanh(0.7978845608028654 * (x + 0.044715 * (x * x * x)))
    return 0.5 * x * (1.0 + t)


def _gelu_grad(x):
    t = jnp.tanh(0.7978845608028654 * (x + 0.044715 * (x * x * x)))
    return 0.5 * (1.0 + t) + 0.5 * x * (1.0 - t * t) * (0.7978845608028654 * (1.0 + 3.0 * 0.044715 * (x * x)))


def _rstd(x):
    return lax.rsqrt(jnp.mean(x * x, axis=-1, keepdims=True) + EPS)


def _norm_bwd(dn, h, g):
    r = _rstd(h)
    n = h * r
    t = dn * g
    return r * (t - n * jnp.mean(t * n, axis=-1, keepdims=True)), dn * n


def _swap_halves(x, first_half):
    return jnp.where(first_half, pltpu.roll(x, 96, 1), pltpu.roll(x, 32, 1))


def _sub_spec(d, n_cb, tm):
    return pl.BlockSpec((d, n_cb, tm // d, 128), lambda i: (0, 0, i, 0))


def _sub_shape(s_len, d, n_cb, dtype):
    return jax.ShapeDtypeStruct((d, n_cb, s_len // d, 128), dtype)


def _to_sub(stage_ref, cb_src, out_ref, cb_dst, d, tm):
    slab = stage_ref.at[cb_src]
    for r in range(d):
        out_ref[r, cb_dst] = slab[pl.ds(r, tm // d, stride=d), :].astype(out_ref.dtype)


def _from_sub(in_ref, cb_src, stage_ref, cb_dst, d, tm, accumulate=False):
    slab = stage_ref.at[cb_dst]
    for r in range(d):
        rows = pl.ds(r, tm // d, stride=d)
        val = in_ref[r, cb_src].astype(F32)
        slab[rows, :] = slab[rows, :] + val if accumulate else val


def _inproj(x, g, w, cos, sin, exchange=()):
    s_len = x.shape[0]
    tm = ROW_TILE
    n_cb = 3 * N_PAIRS

    def body(x_ref, g_ref, w_ref, cos_ref, sin_ref, ua_ref, hn_ref, *rest):
        sub_refs, stage = rest[:-1], rest[-1]
        xf = x_ref[...]
        hn = (xf * _rstd(xf) * g_ref[...]).astype(BF16)
        hn_ref[...] = hn
        c = cos_ref[...]
        s = sin_ref[...]
        first_half = (lax.broadcasted_iota(jnp.int32, (tm, 128), 1) % HEAD_DIM) < HEAD_DIM // 2
        for j in range(IN_COLS // 256):
            col = j * 256
            acc = jnp.dot(hn, w_ref[:, col:col + 256], preferred_element_type=F32)
            if col < 2 * WIDTH_A:
                ua_ref[:, col:col + 256] = acc
                continue
            for half in range(2):
                cb = (col - 2 * WIDTH_A) // 128 + half
                t = acc[:, half * 128:(half + 1) * 128]
                if cb < 2 * N_PAIRS:
                    t = (t * c + _swap_halves(t, first_half) * s) * (0.125 if cb < N_PAIRS else 1.0)
                stage[cb] = t
        for cb in range(n_cb):
            for d, out_ref in zip(DILATIONS, sub_refs):
                _to_sub(stage, cb, out_ref, cb, d, tm)

    return _call(
        body, name="inproj", grid=(s_len // tm,),
        in_specs=[_rows(tm, D_MODEL), _whole((1, D_MODEL)), _whole((D_MODEL, IN_COLS)), _rows(tm, 128), _rows(tm, 128)],
        out_specs=[_rows(tm, 2 * WIDTH_A), _rows(tm, D_MODEL)] + [_sub_spec(d, n_cb, tm) for d in DILATIONS],
        out_shape=[jax.ShapeDtypeStruct((s_len, 2 * WIDTH_A), F32), jax.ShapeDtypeStruct((s_len, D_MODEL), BF16)]
        + [_sub_shape(s_len, d, n_cb, BF16) for d in DILATIONS],
        scratch_shapes=[pltpu.VMEM((n_cb, tm, 128), F32)],
        semantics=("parallel",), args=(x, g, w, cos, sin), exchange=exchange)


def _sgu_mix_weights(sw_ref):
    keep = lax.broadcasted_iota(jnp.int32, (CHUNK, CHUNK), 0) >= lax.broadcasted_iota(jnp.int32, (CHUNK, CHUNK), 1)
    return [jnp.where(keep, sw_ref[h], 0.0).astype(BF16) for h in range(4)], keep


def _sgu_core(ua_ref, gs_ref):
    u = ua_ref[:, :WIDTH_A]
    va = ua_ref[:, WIDTH_A:]
    vg = _gelu(va)
    xc = vg - jnp.mean(vg, axis=-1, keepdims=True)
    rstd = lax.rsqrt(jnp.mean(xc * xc, axis=-1, keepdims=True) + EPS)
    xhat = xc * rstd
    return u, va, _gelu(u), xhat, rstd, xhat * gs_ref[...]


def _sgu_fwd(ua, sw, b2, gs, ga):
    s_len = ua.shape[0]
    tm = ROW_TILE

    def body(ua_ref, sw_ref, b2_ref, gs_ref, ga_ref, out_ref):
        _, _, ug, _, _, vn = _sgu_core(ua_ref, gs_ref)
        wm, _ = _sgu_mix_weights(sw_ref)
        head = lax.broadcasted_iota(jnp.int32, (CHUNK, WIDTH_A), 1) // HEAD_DIM
        for c in range(tm // CHUNK):
            rows = slice(c * CHUNK, (c + 1) * CHUNK)
            vnc = vn[rows]
            mixed = b2_ref[...]
            for h in range(4):
                mixed = mixed + jnp.dot(wm[h], jnp.where(head == h, vnc, 0.0).astype(BF16), preferred_element_type=F32)
            ya = ug[rows] * mixed
            out_ref[rows, :] = (ya * _rstd(ya) * ga_ref[...]).astype(BF16)

    return pl.pallas_call(
        body, name="sgu_fwd", grid=(s_len // tm,),
        in_specs=[_rows(tm, 2 * WIDTH_A), _whole((4, CHUNK, CHUNK)), _whole((CHUNK, WIDTH_A)), _whole((1, WIDTH_A)), _whole((1, WIDTH_A))],
        out_specs=_rows(tm, WIDTH_A),
        out_shape=jax.ShapeDtypeStruct((s_len, WIDTH_A), BF16),
        compiler_params=_params("parallel"),
    )(ua, sw, b2, gs, ga)


def _attn_geometry(sd):
    tile = min(ROW_TILE, sd)
    return tile, tile // CHUNK, sd // tile


PAIRS_PER_STEP = 6


def _attn_spec(cb0, rows, row_index):
    return pl.BlockSpec((None, PAIRS_PER_STEP, rows, 128), lambda r, g, n: (r, cb0 // PAIRS_PER_STEP + g, row_index(n), 0))


def _per_pair(one_pair):
    def body(*refs):
        for hp in range(PAIRS_PER_STEP):
            one_pair(*[ref.at[hp] for ref in refs])
    return body


def _both_heads(x, head_a):
    zero = jnp.zeros_like(x)
    return [jnp.where(head_a, x, zero), jnp.where(head_a, zero, x)]


def _attn_fwd(qkv, d, exchange=()):
    sd = qkv.shape[2]
    tile, nb, n_tiles = _attn_geometry(sd)

    def prev(n):
        return jnp.maximum(n * nb - 1, 0)

    def one_pair(q_ref, k_ref, kp_ref, v_ref, vp_ref, o_ref, l_ref):
        n = pl.program_id(2)
        head_a = lax.broadcasted_iota(jnp.int32, (CHUNK, 128), 1) < HEAD_DIM
        qi = lax.broadcasted_iota(jnp.int32, (2 * CHUNK, 2 * CHUNK), 0) % CHUNK
        kc = lax.broadcasted_iota(jnp.int32, (2 * CHUNK, 2 * CHUNK), 1)
        band = (kc >= qi) & (kc <= qi + N_BACK)
        for j in range(nb):
            rows = slice(j * CHUNK, (j + 1) * CHUNK)
            if j == 0:
                kcat = jnp.concatenate([kp_ref[...], k_ref[rows, :]], axis=0)
                vcat = jnp.concatenate([vp_ref[...], v_ref[rows, :]], axis=0)
                valid = band & jnp.logical_or(n > 0, kc >= CHUNK)
            else:
                kcat = k_ref[(j - 1) * CHUNK:(j + 1) * CHUNK, :]
                vcat = v_ref[(j - 1) * CHUNK:(j + 1) * CHUNK, :]
                valid = band
            q2 = jnp.concatenate(_both_heads(q_ref[rows, :], head_a), axis=0)
            s = lax.dot_general(q2, kcat, _NT, preferred_element_type=F32)
            s = jnp.where(valid, s, NEG)
            m = jnp.max(s, axis=-1, keepdims=True)
            p = jnp.exp(s - m)
            l = jnp.sum(p, axis=-1, keepdims=True)
            o2 = jnp.dot(p.astype(BF16), vcat, preferred_element_type=F32) / l
            lse2 = m + jnp.log(l)
            o_ref[rows, :] = jnp.where(head_a, o2[:CHUNK], o2[CHUNK:]).astype(BF16)
            l_ref[rows, :] = jnp.where(head_a, lse2[:CHUNK], lse2[CHUNK:])

    same = lambda n: n
    return _call(
        _per_pair(one_pair), name=f"attn_fwd_d{d}", grid=(d, N_PAIRS // PAIRS_PER_STEP, n_tiles),
        in_specs=[_attn_spec(0, tile, same), _attn_spec(N_PAIRS, tile, same), _attn_spec(N_PAIRS, CHUNK, prev),
                  _attn_spec(2 * N_PAIRS, tile, same), _attn_spec(2 * N_PAIRS, CHUNK, prev)],
        out_specs=[_attn_spec(0, tile, same), _attn_spec(0, tile, same)],
        out_shape=[jax.ShapeDtypeStruct((d, N_PAIRS, sd, 128), BF16), jax.ShapeDtypeStruct((d, N_PAIRS, sd, 128), F32)],
        semantics=("parallel", "parallel", "parallel"), args=(qkv, qkv, qkv, qkv, qkv), exchange=exchange)


def _combine(outs, lses, ya_n, gb, exchange=()):
    s_len = ya_n.shape[0]
    tm = ROW_TILE
    n_br = len(DILATIONS)

    def body(*refs):
        o_refs, l_refs = refs[:n_br], refs[n_br:2 * n_br]
        ya_ref, gb_ref, y_ref, yb_ref = refs[2 * n_br:2 * n_br + 4]
        lse_refs = refs[2 * n_br + 4:3 * n_br + 4]
        o_nat, l_nat, lse_nat = refs[3 * n_br + 4:]
        sumsq = jnp.zeros((tm, 1), F32)
        for cb in range(N_PAIRS):
            for i, d in enumerate(DILATIONS):
                _from_sub(o_refs[i], cb, o_nat, i, d, tm)
                _from_sub(l_refs[i], cb, l_nat, i, d, tm)
            ls = [l_nat[i] for i in range(n_br)]
            top = jnp.maximum(jnp.maximum(ls[0], ls[1]), ls[2])
            ws = [jnp.exp(l - top) for l in ls]
            den = ws[0] + ws[1] + ws[2]
            inv = 1.0 / den
            yb = (ws[0] * inv) * o_nat[0] + (ws[1] * inv) * o_nat[1] + (ws[2] * inv) * o_nat[2]
            yb_ref[:, cb * 128:(cb + 1) * 128] = yb
            sumsq = sumsq + jnp.sum(yb * yb, axis=-1, keepdims=True)
            lse_nat[cb] = top + jnp.log(den)
            for d, lse_ref in zip(DILATIONS, lse_refs):
                _to_sub(lse_nat, cb, lse_ref, cb, d, tm)
        r = lax.rsqrt(sumsq / WIDTH_B + EPS)
        y_ref[:, :WIDTH_A] = ya_ref[...]
        y_ref[:, WIDTH_A:] = (yb_ref[...] * r * gb_ref[...]).astype(BF16)

    return _call(
        body, name="attn_combine", grid=(s_len // tm,),
        in_specs=[_sub_spec(d, N_PAIRS, tm) for d in DILATIONS] * 2 + [_rows(tm, WIDTH_A), _whole((1, WIDTH_B))],
        out_specs=[_rows(tm, D_MODEL), _rows(tm, WIDTH_B)] + [_sub_spec(d, N_PAIRS, tm) for d in DILATIONS],
        out_shape=[jax.ShapeDtypeStruct((s_len, D_MODEL), BF16), jax.ShapeDtypeStruct((s_len, WIDTH_B), F32)]
        + [_sub_shape(s_len, d, N_PAIRS, F32) for d in DILATIONS],
        scratch_shapes=[pltpu.VMEM((n_br, tm, 128), F32), pltpu.VMEM((n_br, tm, 128), F32), pltpu.VMEM((N_PAIRS, tm, 128), F32)],
        semantics=("parallel",), args=(*outs, *lses, ya_n, gb), exchange=exchange)


def _mm_res(a, w, res, name):
    s_len, k = a.shape
    tm = ROW_TILE if k <= D_MODEL else ROW_TILE // 2

    def body(a_ref, w_ref, r_ref, o_ref):
        o_ref[...] = r_ref[...] + jnp.dot(a_ref[...].astype(BF16), w_ref[...], preferred_element_type=F32)

    return pl.pallas_call(
        body, name=name, grid=(s_len // tm,),
        in_specs=[_rows(tm, k), _whole((k, D_MODEL)), _rows(tm, D_MODEL)],
        out_specs=_rows(tm, D_MODEL),
        out_shape=jax.ShapeDtypeStruct((s_len, D_MODEL), F32),
        compiler_params=_params("parallel"),
    )(a, w, res)


def _ffn_up(h, g, wg, wu, exchange=()):
    s_len = h.shape[0]
    tm = ROW_TILE // 2

    def body(h_ref, g_ref, wg_ref, wu_ref, a_ref, b_ref, act_ref, hn_ref):
        hf = h_ref[...]
        hn = (hf * _rstd(hf) * g_ref[...]).astype(BF16)
        hn_ref[...] = hn
        for j in range(D_FF // 256):
            cols = slice(j * 256, (j + 1) * 256)
            a = jnp.dot(hn, wg_ref[:, cols], preferred_element_type=F32)
            b = jnp.dot(hn, wu_ref[:, cols], preferred_element_type=F32)
            a_ref[:, cols] = a.astype(BF16)
            b_ref[:, cols] = b.astype(BF16)
            act_ref[:, cols] = (a * jax.nn.sigmoid(a) * b).astype(BF16)

    wide = jax.ShapeDtypeStruct((s_len, D_FF), BF16)
    return _call(
        body, name="ffn_up", grid=(s_len // tm,),
        in_specs=[_rows(tm, D_MODEL), _whole((1, D_MODEL)), _whole((D_MODEL, D_FF)), _whole((D_MODEL, D_FF))],
        out_specs=[_rows(tm, D_FF), _rows(tm, D_FF), _rows(tm, D_FF), _rows(tm, D_MODEL)],
        out_shape=[wide, wide, wide, jax.ShapeDtypeStruct((s_len, D_MODEL), BF16)],
        semantics=("parallel",), args=(h, g, wg, wu), exchange=exchange)


def _ple(h, g, wpg, p, wpp):
    s_len = h.shape[0]
    tm = ROW_TILE

    def body(h_ref, g_ref, wpg_ref, p_ref, wpp_ref, h3_ref, gate_ref, pp_ref, hn_ref):
        hf = h_ref[...]
        hn = (hf * _rstd(hf) * g_ref[...]).astype(BF16)
        hn_ref[...] = hn
        gate = jax.nn.sigmoid(jnp.dot(hn, wpg_ref[...], preferred_element_type=F32))
        pp = jnp.dot(p_ref[...].astype(BF16), wpp_ref[...], preferred_element_type=F32)
        h3_ref[...] = hf + gate * pp
        gate_ref[...] = gate.astype(BF16)
        pp_ref[...] = pp.astype(BF16)

    half = jax.ShapeDtypeStruct((s_len, D_MODEL), BF16)
    return pl.pallas_call(
        body, name="ple", grid=(s_len // tm,),
        in_specs=[_rows(tm, D_MODEL), _whole((1, D_MODEL)), _whole((D_MODEL, D_MODEL)), _rows(tm, PLE_DIM), _whole((PLE_DIM, D_MODEL))],
        out_specs=[_rows(tm, D_MODEL)] * 4,
        out_shape=[jax.ShapeDtypeStruct((s_len, D_MODEL), F32), half, half, half],
        compiler_params=_params("parallel"),
    )(h, g, wpg, p, wpp)


def _loss_bwd(h3, target, gf, gate, pp):
    s_len = h3.shape[0]
    tm = ROW_TILE

    def body(h_ref, t_ref, g_ref, gate_ref, pp_ref, dh_ref, dz_ref, dpp_ref, loss_ref, dg_ref):
        @pl.when(pl.program_id(0) == 0)
        def _():
            loss_ref[...] = jnp.zeros_like(loss_ref)
            dg_ref[...] = jnp.zeros_like(dg_ref)

        hf = h_ref[...]
        gfv = g_ref[...]
        err = hf * _rstd(hf) * gfv - t_ref[...]
        loss_ref[...] += 0.5 * jnp.sum(jnp.sum(err * err, axis=-1, keepdims=True), axis=0, keepdims=True) / D_MODEL
        dh, dg_rows = _norm_bwd(err / D_MODEL, hf, gfv)
        dg_ref[...] += jnp.sum(dg_rows, axis=0, keepdims=True)
        dh_ref[...] = dh
        gate = gate_ref[...].astype(F32)
        dz_ref[...] = (dh * pp_ref[...].astype(F32) * gate * (1.0 - gate)).astype(BF16)
        dpp_ref[...] = (dh * gate).astype(BF16)

    half = jax.ShapeDtypeStruct((s_len, D_MODEL), BF16)
    return pl.pallas_call(
        body, name="loss_bwd", grid=(s_len // tm,),
        in_specs=[_rows(tm, D_MODEL), _rows(tm, D_MODEL), _whole((1, D_MODEL)), _rows(tm, D_MODEL), _rows(tm, D_MODEL)],
        out_specs=[_rows(tm, D_MODEL), _rows(tm, D_MODEL), _rows(tm, D_MODEL), _whole((1, 128)), _whole((1, D_MODEL))],
        out_shape=[jax.ShapeDtypeStruct((s_len, D_MODEL), F32), half, half,
                   jax.ShapeDtypeStruct((1, 128), F32), jax.ShapeDtypeStruct((1, D_MODEL), F32)],
        compiler_params=_params("arbitrary"),
    )(h3, target, gf, gate, pp)


def _mm_norm_bwd(parts, h, g, dres, name, exchange=()):
    s_len = h.shape[0]
    tm = ROW_TILE // 2
    n_parts = len(parts)

    def body(*refs):
        a_refs = refs[0:2 * n_parts:2]
        w_refs = refs[1:2 * n_parts:2]
        h_ref, g_ref, r_ref, o_ref, dg_ref = refs[2 * n_parts:]

        @pl.when(pl.program_id(0) == 0)
        def _():
            dg_ref[...] = jnp.zeros_like(dg_ref)

        dn = jnp.dot(a_refs[0][...], w_refs[0][...], preferred_element_type=F32)
        for a_ref, w_ref in zip(a_refs[1:], w_refs[1:]):
            dn = dn + jnp.dot(a_ref[...], w_ref[...], preferred_element_type=F32)
        dh, dg_rows = _norm_bwd(dn, h_ref[...], g_ref[...])
        o_ref[...] = r_ref[...] + dh
        dg_ref[...] += jnp.sum(dg_rows, axis=0, keepdims=True)

    in_specs, args = [], []
    for a, w in parts:
        in_specs += [_rows(tm, a.shape[1]), _whole(w.shape)]
        args += [a, w]
    return _call(
        body, name=name, grid=(s_len // tm,),
        in_specs=in_specs + [_rows(tm, D_MODEL), _whole((1, D_MODEL)), _rows(tm, D_MODEL)],
        out_specs=[_rows(tm, D_MODEL), _whole((1, D_MODEL))],
        out_shape=[jax.ShapeDtypeStruct((s_len, D_MODEL), F32), jax.ShapeDtypeStruct((1, D_MODEL), F32)],
        semantics=("arbitrary",), args=(*args, h, g, dres), exchange=exchange)


def _ffn_down_bwd(dh, wdt, a, b, exchange=()):
    s_len = dh.shape[0]
    tm = ROW_TILE // 2

    def body(dh_ref, w_ref, a_ref, b_ref, da_ref, db_ref):
        dhb = dh_ref[...].astype(BF16)
        for j in range(D_FF // 256):
            cols = slice(j * 256, (j + 1) * 256)
            dact = jnp.dot(dhb, w_ref[:, cols], preferred_element_type=F32)
            av = a_ref[:, cols].astype(F32)
            bv = b_ref[:, cols].astype(F32)
            sig = jax.nn.sigmoid(av)
            da_ref[:, cols] = (dact * bv * sig * (1.0 + av * (1.0 - sig))).astype(BF16)
            db_ref[:, cols] = (dact * av * sig).astype(BF16)

    wide = jax.ShapeDtypeStruct((s_len, D_FF), BF16)
    return _call(
        body, name="ffn_down_bwd", grid=(s_len // tm,),
        in_specs=[_rows(tm, D_MODEL), _whole((D_MODEL, D_FF)), _rows(tm, D_FF), _rows(tm, D_FF)],
        out_specs=[_rows(tm, D_FF), _rows(tm, D_FF)],
        out_shape=[wide, wide],
        semantics=("parallel",), args=(dh, wdt, a, b), exchange=exchange)


def _outproj_bwd(dh1, woutt, yb, gb, head_sum, exchange=()):
    s_len = dh1.shape[0]
    tm = ROW_TILE
    n_br = len(DILATIONS)

    def body(dh_ref, w_ref, yb_ref, gb_ref, e_ref, dya_ref, dgb_ref, *rest):
        do_refs, dd_refs = rest[:n_br], rest[n_br:2 * n_br]
        do_nat, dd_nat = rest[2 * n_br:]

        @pl.when(pl.program_id(0) == 0)
        def _():
            dgb_ref[...] = jnp.zeros_like(dgb_ref)

        dhb = dh_ref[...].astype(BF16)
        dya_ref[...] = jnp.dot(dhb, w_ref[:, :WIDTH_A], preferred_element_type=F32)
        dyn = jnp.dot(dhb, w_ref[:, WIDTH_A:], preferred_element_type=F32)
        ybv = yb_ref[...]
        dyb, dg_rows = _norm_bwd(dyn, ybv, gb_ref[...])
        dgb_ref[...] += jnp.sum(dg_rows, axis=0, keepdims=True)
        prod = dyb * ybv
        for cb in range(N_PAIRS):
            cols = slice(cb * 128, (cb + 1) * 128)
            pc = prod[:, cols]
            hi = pc.astype(BF16)
            lo = (pc - hi.astype(F32)).astype(BF16)
            do_nat[cb] = dyb[:, cols]
            dd_nat[cb] = (jnp.dot(hi, e_ref[...], preferred_element_type=F32)
                          + jnp.dot(lo, e_ref[...], preferred_element_type=F32))
            for i, d in enumerate(DILATIONS):
                _to_sub(do_nat, cb, do_refs[i], cb, d, tm)
                _to_sub(dd_nat, cb, dd_refs[i], cb, d, tm)

    subs = [_sub_spec(d, N_PAIRS, tm) for d in DILATIONS]
    return _call(
        body, name="outproj_bwd", grid=(s_len // tm,),
        in_specs=[_rows(tm, D_MODEL), _whole((D_MODEL, D_MODEL)), _rows(tm, WIDTH_B), _whole((1, WIDTH_B)), _whole((128, 128))],
        out_specs=[_rows(tm, WIDTH_A), _whole((1, WIDTH_B))] + subs + subs,
        out_shape=[jax.ShapeDtypeStruct((s_len, WIDTH_A), F32), jax.ShapeDtypeStruct((1, WIDTH_B), F32)]
        + [_sub_shape(s_len, d, N_PAIRS, BF16) for d in DILATIONS] + [_sub_shape(s_len, d, N_PAIRS, F32) for d in DILATIONS],
        scratch_shapes=[pltpu.VMEM((N_PAIRS, tm, 128), F32), pltpu.VMEM((N_PAIRS, tm, 128), F32)],
        semantics=("arbitrary",), args=(dh1, woutt, yb, gb, head_sum), exchange=exchange)


def _attn_bwd(qkv, do, lse, dd, d, exchange=()):
    sd = qkv.shape[2]
    tile, nb, n_tiles = _attn_geometry(sd)
    last_block = sd // CHUNK - 1

    def nxt(n):
        return jnp.minimum((n + 1) * nb, last_block)

    def one_pair(q_ref, qn_ref, k_ref, v_ref, do_ref, don_ref, l_ref, ln_ref, dd_ref, ddn_ref,
                 dq_ref, dk_ref, dv_ref, carry_ref):
        n = pl.program_id(2)

        @pl.when(n == 0)
        def _():
            carry_ref[...] = jnp.zeros_like(carry_ref)

        head_a = lax.broadcasted_iota(jnp.int32, (CHUNK, 128), 1) < HEAD_DIM
        row = lax.broadcasted_iota(jnp.int32, (4 * CHUNK, CHUNK), 0)
        qi = row % CHUNK
        ki = lax.broadcasted_iota(jnp.int32, (4 * CHUNK, CHUNK), 1)
        is_after = row >= 2 * CHUNK
        mask = (is_after & (ki >= qi)) | (jnp.logical_not(is_after) & (qi >= ki))
        mask_last = mask & jnp.logical_or(jnp.logical_not(is_after), n < n_tiles - 1)
        dq_acc = [carry_ref[...]] + [jnp.zeros((CHUNK, 128), F32) for _ in range(nb)]

        def per_head(x):
            other = pltpu.roll(x, HEAD_DIM, 1)
            return [jnp.where(head_a, x, other), jnp.where(head_a, other, x)]

        for j in range(nb):
            rows = slice(j * CHUNK, (j + 1) * CHUNK)
            kj = k_ref[rows, :]
            vj = v_ref[rows, :]
            if j + 1 < nb:
                nrows = slice((j + 1) * CHUNK, (j + 2) * CHUNK)
                q2, do2, l2, dd2, msk = q_ref[nrows, :], do_ref[nrows, :], l_ref[nrows, :], dd_ref[nrows, :], mask
            else:
                q2, do2, l2, dd2, msk = qn_ref[...], don_ref[...], ln_ref[...], ddn_ref[...], mask_last
            qs = jnp.concatenate(_both_heads(q_ref[rows, :], head_a) + _both_heads(q2, head_a), axis=0)
            dos = jnp.concatenate(_both_heads(do_ref[rows, :], head_a) + _both_heads(do2, head_a), axis=0)
            ls = jnp.concatenate(per_head(l_ref[rows, :]) + per_head(l2), axis=0)
            dds = jnp.concatenate(per_head(dd_ref[rows, :]) + per_head(dd2), axis=0)
            s = lax.dot_general(qs, kj, _NT, preferred_element_type=F32)
            p = jnp.exp(jnp.where(msk, s - ls, NEG))
            dp = lax.dot_general(dos, vj, _NT, preferred_element_type=F32)
            ds = (p * (dp - dds)).astype(BF16)
            dv_ref[rows, :] = lax.dot_general(p.astype(BF16), dos, _TN, preferred_element_type=F32).astype(BF16)
            dk_ref[rows, :] = lax.dot_general(ds, qs, _TN, preferred_element_type=F32).astype(BF16)
            dqs = jnp.dot(ds, kj, preferred_element_type=F32)
            dq_acc[j] = dq_acc[j] + jnp.where(head_a, dqs[:CHUNK], dqs[CHUNK:2 * CHUNK])
            dq_acc[j + 1] = dq_acc[j + 1] + jnp.where(head_a, dqs[2 * CHUNK:3 * CHUNK], dqs[3 * CHUNK:])
        for j in range(nb):
            dq_ref[j * CHUNK:(j + 1) * CHUNK, :] = dq_acc[j].astype(BF16)
        carry_ref[...] = dq_acc[nb]

    same = lambda n: n
    grad = jax.ShapeDtypeStruct((d, N_PAIRS, sd, 128), BF16)
    return _call(
        _per_pair(one_pair), name=f"attn_bwd_d{d}", grid=(d, N_PAIRS // PAIRS_PER_STEP, n_tiles),
        in_specs=[_attn_spec(0, tile, same), _attn_spec(0, CHUNK, nxt), _attn_spec(N_PAIRS, tile, same),
                  _attn_spec(2 * N_PAIRS, tile, same), _attn_spec(0, tile, same), _attn_spec(0, CHUNK, nxt),
                  _attn_spec(0, tile, same), _attn_spec(0, CHUNK, nxt), _attn_spec(0, tile, same), _attn_spec(0, CHUNK, nxt)],
        out_specs=[_attn_spec(0, tile, same)] * 3,
        out_shape=[grad, grad, grad],
        scratch_shapes=[pltpu.VMEM((PAIRS_PER_STEP, CHUNK, 128), F32)],
        semantics=("parallel", "parallel", "arbitrary"), args=(qkv, qkv, qkv, qkv, do, do, lse, lse, dd, dd), exchange=exchange)


def _sgu_bwd(ua, sw, b2, gs, ga, dya_n):
    s_len = ua.shape[0]
    tm = ROW_TILE

    def body(ua_ref, sw_ref, b2_ref, gs_ref, ga_ref, dy_ref, dua_ref, dsw_ref, db2_ref, dgs_ref, dga_ref):
        @pl.when(pl.program_id(0) == 0)
        def _():
            dsw_ref[...] = jnp.zeros_like(dsw_ref)
            db2_ref[...] = jnp.zeros_like(db2_ref)
            dgs_ref[...] = jnp.zeros_like(dgs_ref)
            dga_ref[...] = jnp.zeros_like(dga_ref)

        u, va, ug, xhat, rstd, vn = _sgu_core(ua_ref, gs_ref)
        wm, keep = _sgu_mix_weights(sw_ref)
        head = lax.broadcasted_iota(jnp.int32, (CHUNK, WIDTH_A), 1) // HEAD_DIM
        gav = ga_ref[...]
        gsv = gs_ref[...]
        dga = jnp.zeros((1, WIDTH_A), F32)
        dgs = jnp.zeros((1, WIDTH_A), F32)
        db2 = jnp.zeros((CHUNK, WIDTH_A), F32)
        dsw = [jnp.zeros((CHUNK, CHUNK), F32) for _ in range(4)]
        for c in range(tm // CHUNK):
            rows = slice(c * CHUNK, (c + 1) * CHUNK)
            vnc = vn[rows]
            vnb = vnc.astype(BF16)
            mixed = b2_ref[...]
            for h in range(4):
                mixed = mixed + jnp.dot(wm[h], jnp.where(head == h, vnc, 0.0).astype(BF16), preferred_element_type=F32)
            ugc = ug[rows]
            dya, dga_rows = _norm_bwd(dy_ref[rows, :], ugc * mixed, gav)
            dga = dga + jnp.sum(dga_rows, axis=0, keepdims=True)
            dmixed = dya * ugc
            db2 = db2 + dmixed
            dvn = jnp.zeros((CHUNK, WIDTH_A), F32)
            for h in range(4):
                dmh = jnp.where(head == h, dmixed, 0.0).astype(BF16)
                dsw[h] = dsw[h] + lax.dot_general(dmh, vnb, _NT, preferred_element_type=F32)
                dvn = dvn + lax.dot_general(wm[h], dmh, _TN, preferred_element_type=F32)
            xh = xhat[rows]
            dgs = dgs + jnp.sum(dvn * xh, axis=0, keepdims=True)
            dxh = dvn * gsv
            dvg = rstd[rows] * (dxh - jnp.mean(dxh, axis=-1, keepdims=True) - xh * jnp.mean(dxh * xh, axis=-1, keepdims=True))
            dua_ref[rows, :WIDTH_A] = (dya * mixed * _gelu_grad(u[rows])).astype(BF16)
            dua_ref[rows, WIDTH_A:] = (dvg * _gelu_grad(va[rows])).astype(BF16)
        for h in range(4):
            dsw_ref[h] += jnp.where(keep, dsw[h], 0.0)
        db2_ref[...] += db2
        dgs_ref[...] += dgs
        dga_ref[...] += dga

    return pl.pallas_call(
        body, name="sgu_bwd", grid=(s_len // tm,),
        in_specs=[_rows(tm, 2 * WIDTH_A), _whole((4, CHUNK, CHUNK)), _whole((CHUNK, WIDTH_A)), _whole((1, WIDTH_A)),
                  _whole((1, WIDTH_A)), _rows(tm, WIDTH_A)],
        out_specs=[_rows(tm, 2 * WIDTH_A), _whole((4, CHUNK, CHUNK)), _whole((CHUNK, WIDTH_A)), _whole((1, WIDTH_A)), _whole((1, WIDTH_A))],
        out_shape=[jax.ShapeDtypeStruct((s_len, 2 * WIDTH_A), BF16), jax.ShapeDtypeStruct((4, CHUNK, CHUNK), F32),
                   jax.ShapeDtypeStruct((CHUNK, WIDTH_A), F32), jax.ShapeDtypeStruct((1, WIDTH_A), F32),
                   jax.ShapeDtypeStruct((1, WIDTH_A), F32)],
        compiler_params=_params("arbitrary"),
    )(ua, sw, b2, gs, ga, dya_n)


def _dproj(dua, dqs, dks, dvs, cos, sin):
    s_len = dua.shape[0]
    tm = ROW_TILE
    n_br = len(DILATIONS)

    def body(dua_ref, *rest):
        groups = [rest[g * n_br:(g + 1) * n_br] for g in range(3)]
        cos_ref, sin_ref, out_ref, acc = rest[3 * n_br:]
        out_ref[:, :2 * WIDTH_A] = dua_ref[...]
        c = cos_ref[...]
        s = sin_ref[...]
        first_half = (lax.broadcasted_iota(jnp.int32, (tm, 128), 1) % HEAD_DIM) < HEAD_DIM // 2
        for g, refs in enumerate(groups):
            for cb in range(N_PAIRS):
                for i, d in enumerate(DILATIONS):
                    _from_sub(refs[i], cb, acc, 0, d, tm, accumulate=i > 0)
                t = acc[0]
                if g < 2:
                    t = (t * c - _swap_halves(t, first_half) * s) * (0.125 if g == 0 else 1.0)
                col = 2 * WIDTH_A + g * WIDTH_B + cb * 128
                out_ref[:, col:col + 128] = t.astype(BF16)

    subs = [_sub_spec(d, N_PAIRS, tm) for d in DILATIONS]
    return pl.pallas_call(
        body, name="dproj", grid=(s_len // tm,),
        in_specs=[_rows(tm, 2 * WIDTH_A)] + subs * 3 + [_rows(tm, 128), _rows(tm, 128)],
        out_specs=_rows(tm, IN_COLS),
        out_shape=jax.ShapeDtypeStruct((s_len, IN_COLS), BF16),
        scratch_shapes=[pltpu.VMEM((1, tm, 128), F32)],
        compiler_params=_params("parallel"),
    )(dua, *dqs, *dks, *dvs, cos, sin)


def _mm_tn(a, b, name, exchange=()):
    s_len, m = a.shape
    n = b.shape[1]
    tk = ROW_TILE
    tm = m if m <= 512 else (1408 if m == D_FF else 512)
    n_k = s_len // tk

    def body(a_ref, b_ref, o_ref, acc_ref):
        k = pl.program_id(1)

        @pl.when(k == 0)
        def _():
            acc_ref[...] = jnp.zeros_like(acc_ref)

        acc_ref[...] += lax.dot_general(a_ref[...].astype(BF16), b_ref[...].astype(BF16), _TN, preferred_element_type=F32)

        @pl.when(k == n_k - 1)
        def _():
            o_ref[...] = acc_ref[...].astype(BF16)

    (grad,), received = _call(
        body, name=name, grid=(m // tm, n_k),
        in_specs=[pl.BlockSpec((tk, tm), lambda i, k: (k, i)), pl.BlockSpec((tk, n), lambda i, k: (k, 0))],
        out_specs=[pl.BlockSpec((tm, n), lambda i, k: (i, 0))],
        out_shape=[jax.ShapeDtypeStruct((m, n), BF16)],
        scratch_shapes=[pltpu.VMEM((tm, n), F32)],
        semantics=("parallel", "arbitrary"), args=(a, b), exchange=exchange)
    return grad, received


def _position():
    x, y, c = lax.axis_index("x"), lax.axis_index("y"), lax.axis_index("c")
    return x, y, c, 4 * x + 2 * y + c


def _peer(x, y, c, rel):
    return (x ^ ((rel >> 2) & 1), y ^ ((rel >> 1) & 1), c ^ (rel & 1))


def _exchange_out_shape(kind, arr):
    return jax.ShapeDtypeStruct(((N_DEV,) + arr.shape) if kind == "gather" else arr.shape, arr.dtype)


def _exchange_sems(n_items):
    return [pltpu.SemaphoreType.DMA((n_items, N_DEV)), pltpu.SemaphoreType.DMA((n_items, N_DEV)), pltpu.SemaphoreType.DMA((n_items,))]


def _exchange_copies(kinds, srcs, dsts, sems):
    send_sems, recv_sems, local_sems = sems
    x, y, c, me = _position()
    local, sends, recvs = [], [], []
    for k, (kind, src, dst) in enumerate(zip(kinds, srcs, dsts)):
        own = src if kind == "gather" else src.at[me]
        local.append(pltpu.make_async_copy(own, dst.at[me], local_sems.at[k]))
        for rel in range(1, N_DEV):
            going = src if kind == "gather" else src.at[me ^ rel]
            common = dict(send_sem=send_sems.at[k, rel], recv_sem=recv_sems.at[k, rel],
                          device_id=_peer(x, y, c, rel), device_id_type=MESH)
            sends.append(pltpu.make_async_remote_copy(src_ref=going, dst_ref=dst.at[me], **common))
            recvs.append(pltpu.make_async_remote_copy(src_ref=own, dst_ref=dst.at[me ^ rel], **common))
    return local, sends, recvs


def _exchange_start(kinds, srcs, dsts, sems):
    local, sends, _ = _exchange_copies(kinds, srcs, dsts, sems)
    for cp in local + sends:
        cp.start()


def _exchange_finish(kinds, srcs, dsts, sems):
    local, sends, recvs = _exchange_copies(kinds, srcs, dsts, sems)
    for cp in recvs:
        cp.wait_recv()
    for cp in sends:
        cp.wait_send()
    for cp in local:
        cp.wait()


def _exchange_only(items, name):
    kinds = [k for k, _ in items]
    n = len(items)

    def body(*refs):
        srcs, dsts, sems = refs[:n], refs[n:2 * n], refs[2 * n:]
        _exchange_start(kinds, srcs, dsts, sems)
        _exchange_finish(kinds, srcs, dsts, sems)

    any_spec = pl.BlockSpec(memory_space=pl.ANY)
    return pl.pallas_call(
        body, name=name, in_specs=[any_spec] * n, out_specs=[any_spec] * n,
        out_shape=[_exchange_out_shape(k, a) for k, a in items],
        scratch_shapes=_exchange_sems(n),
        compiler_params=pltpu.CompilerParams(has_side_effects=True),
    )(*[a for _, a in items])


def _call(body, *, name, grid, in_specs, out_specs, out_shape, args, scratch_shapes=(), semantics, exchange=()):
    if not exchange:
        outs = pl.pallas_call(body, name=name, grid=grid, in_specs=in_specs, out_specs=out_specs, out_shape=out_shape,
                              scratch_shapes=list(scratch_shapes), compiler_params=_params(*semantics))(*args)
        return outs, []
    kinds = [k for k, _ in exchange]
    n_in, n_out, n_x, n_scr = len(in_specs), len(out_specs), len(exchange), len(scratch_shapes)

    def wrapped(*refs):
        ins, refs = refs[:n_in], refs[n_in:]
        srcs, refs = refs[:n_x], refs[n_x:]
        outs, refs = refs[:n_out], refs[n_out:]
        dsts, refs = refs[:n_x], refs[n_x:]
        scratch, sems = refs[:n_scr], refs[n_scr:]
        ids = [pl.program_id(a) for a in range(len(grid))]
        first = functools.reduce(jnp.logical_and, [i == 0 for i in ids])
        last = functools.reduce(jnp.logical_and, [i == g - 1 for i, g in zip(ids, grid)])

        @pl.when(first)
        def _():
            _exchange_start(kinds, srcs, dsts, sems)

        body(*ins, *outs, *scratch)

        @pl.when(last)
        def _():
            _exchange_finish(kinds, srcs, dsts, sems)

    any_spec = pl.BlockSpec(memory_space=pl.ANY)
    outs = pl.pallas_call(
        wrapped, name=name, grid=grid,
        in_specs=list(in_specs) + [any_spec] * n_x, out_specs=list(out_specs) + [any_spec] * n_x,
        out_shape=list(out_shape) + [_exchange_out_shape(k, a) for k, a in exchange],
        scratch_shapes=list(scratch_shapes) + _exchange_sems(n_x),
        compiler_params=pltpu.CompilerParams(dimension_semantics=("arbitrary",) * len(grid),
                                             vmem_limit_bytes=V7X_VMEM_LIMIT_BYTES, has_side_effects=True),
    )(*args, *[a for _, a in exchange])
    return outs[:n_out], outs[n_out:]


def _adamw_math(w, g, m, v):
    m = ADAM_B1 * m + (1.0 - ADAM_B1) * g
    v = ADAM_B2 * v + (1.0 - ADAM_B2) * (g * g)
    m_hat = m / (1.0 - ADAM_B1 ** ADAM_STEP)
    v_hat = v / (1.0 - ADAM_B2 ** ADAM_STEP)
    return -ADAM_LR * (m_hat / (jnp.sqrt(v_hat) + ADAM_EPS) + ADAM_WD * w), m, v


def _adamw(parts, w, m, v, name):
    rows, cols = w.shape
    tm = 256 if rows % 256 == 0 and rows > 256 else rows

    def body(p_ref, w_ref, m_ref, v_ref, g_ref, d_ref, nm_ref, nv_ref):
        g = p_ref[0].astype(F32)
        for j in range(1, N_DEV):
            g = g + p_ref[j].astype(F32)
        delta, nm, nv = _adamw_math(w_ref[...], g, m_ref[...], v_ref[...])
        g_ref[...] = g
        d_ref[...] = delta
        nm_ref[...] = nm
        nv_ref[...] = nv

    shard = jax.ShapeDtypeStruct((rows, cols), F32)
    return pl.pallas_call(
        body, name=name, grid=(rows // tm,),
        in_specs=[pl.BlockSpec((N_DEV, tm, cols), lambda i: (0, i, 0))] + [_rows(tm, cols)] * 3,
        out_specs=[_rows(tm, cols)] * 4,
        out_shape=[shard] * 4,
        compiler_params=_params("parallel"),
    )(parts, w, m, v)


_SMALL = ("mix_norm_g", "sgu_w", "sgu_b", "sgu_norm_g", "out_norm_a", "out_norm_b", "ffn_norm_g", "ple_norm_g", "final_norm_g")
_BIG = ("w_in", "w_out", "w_gate", "w_up", "w_down", "w_ple_gate", "w_ple_proj")
_COLUMN_SHARDED = ("w_in", "w_gate", "w_up", "w_ple_proj")
_ORDER = ("mix_norm_g", "w_in", "sgu_w", "sgu_b", "sgu_norm_g", "out_norm_a", "out_norm_b", "w_out", "ffn_norm_g",
          "w_gate", "w_up", "w_down", "ple_norm_g", "w_ple_gate", "w_ple_proj", "final_norm_g")


def _pack_small(values, names=_SMALL):
    flat = jnp.concatenate([values[n].reshape(-1).astype(F32) for n in names])
    pad = (-flat.shape[0]) % (8 * 128)
    return jnp.pad(flat, (0, pad)).reshape(-1, 128)


def _unpack_small(packed, like):
    flat = packed.reshape(-1)
    out, at = {}, 0
    for n in _SMALL:
        size = like[n].size
        out[n] = flat[at:at + size].reshape(like[n].shape)
        at += size
    return out


def _full_from_gathered(name, gathered):
    if name in _COLUMN_SHARDED:
        k, n = gathered.shape[1], gathered.shape[2] * N_DEV
        return gathered.transpose(1, 0, 2).reshape(k, n), gathered.transpose(0, 2, 1).reshape(n, k)
    k, n = gathered.shape[1] * N_DEV, gathered.shape[2]
    return gathered.reshape(k, n), gathered.transpose(2, 0, 1).reshape(n, k)


def _sliced_for_devices(name, grad):
    k, n = grad.shape
    if name in _COLUMN_SHARDED:
        return grad.reshape(k, N_DEV, n // N_DEV).transpose(1, 0, 2)
    return grad.reshape(N_DEV, k // N_DEV, n)


def _rope_tables(s_len):
    half = HEAD_DIM // 2
    inv = ROPE_THETA ** (-jnp.arange(half, dtype=F32) / half)
    ang = jnp.arange(s_len, dtype=F32)[:, None] * inv[None, :]
    cos, sin = jnp.cos(ang), jnp.sin(ang)
    return jnp.concatenate([cos, cos, cos, cos], axis=1), jnp.concatenate([-sin, sin, -sin, sin], axis=1)


def _forward_backward(x, p, target, small, shards):
    def gather(*names):
        return [("gather", shards[n]) for n in names]

    def scatter(**grads):
        return [("scatter", _sliced_for_devices(n, g)) for n, g in grads.items()]

    full, parts = {}, {}
    (got,) = _exchange_only(gather("w_in"), "gather_w_in")
    full["w_in"] = _full_from_gathered("w_in", got)

    s_len = x.shape[0]
    cos, sin = _rope_tables(s_len)
    g_mix, g_ffn, g_ple = small["mix_norm_g"], small["ffn_norm_g"], small["ple_norm_g"]
    g_fin = small["final_norm_g"].reshape(1, D_MODEL)
    sw, gs, ga, gb = small["sgu_w"], small["sgu_norm_g"], small["out_norm_a"], small["out_norm_b"]
    b2 = jnp.repeat(small["sgu_b"].T, HEAD_DIM, axis=1)
    lane_head = jnp.arange(128) // HEAD_DIM
    head_sum = (lane_head[:, None] == lane_head[None, :]).astype(BF16)
    n_br = len(DILATIONS)

    def arrived(names, got):
        for n, g in zip(names, got):
            full[n] = _full_from_gathered(n, g)

    (ua, hn1, *qkv), got = _inproj(x, g_mix, full["w_in"][0], cos, sin, exchange=gather("w_out", "w_gate"))
    arrived(("w_out", "w_gate"), got)
    ya_n = _sgu_fwd(ua, sw, b2, gs, ga)
    branch = []
    for i, d in enumerate(DILATIONS):
        o_l, got = _attn_fwd(qkv[i], d, exchange=gather("w_up") if i == 0 else ())
        branch.append(o_l)
        arrived(("w_up",), got)
    (y, yb, *lse), got = _combine([o for o, _ in branch], [l for _, l in branch], ya_n, gb, exchange=gather("w_down"))
    arrived(("w_down",), got)
    h1 = _mm_res(y, full["w_out"][0], x, "out_proj")
    (a, b, act, hn2), got = _ffn_up(h1, g_ffn, full["w_gate"][0], full["w_up"][0], exchange=gather("w_ple_gate", "w_ple_proj"))
    arrived(("w_ple_gate", "w_ple_proj"), got)
    h2 = _mm_res(act, full["w_down"][0], h1, "ffn_down")
    h3, gate, pp, hn3 = _ple(h2, g_ple, full["w_ple_gate"][0], p, full["w_ple_proj"][0])

    dh3, dz, dpp, loss, d_fin = _loss_bwd(h3, target, g_fin, gate, pp)
    g_ple_proj, _ = _mm_tn(p, dpp, "dw_ple_proj")
    g_ple_gate, _ = _mm_tn(hn3, dz, "dw_ple_gate")
    (dh2, d_ple), _ = _mm_norm_bwd([(dz, full["w_ple_gate"][1])], h2, g_ple, dh3, "ple_bwd")
    g_down, (parts["w_ple_gate"], parts["w_ple_proj"]) = _mm_tn(
        act, dh2, "dw_down", exchange=scatter(w_ple_gate=g_ple_gate, w_ple_proj=g_ple_proj))
    (da, db), (parts["w_down"],) = _ffn_down_bwd(dh2, full["w_down"][1], a, b, exchange=scatter(w_down=g_down))
    g_gate, _ = _mm_tn(hn2, da, "dw_gate")
    g_up, _ = _mm_tn(hn2, db, "dw_up")
    (dh1, d_ffn), (parts["w_gate"],) = _mm_norm_bwd(
        [(da, full["w_gate"][1]), (db, full["w_up"][1])], h1, g_ffn, dh2, "ffn_up_bwd", exchange=scatter(w_gate=g_gate))
    g_out, _ = _mm_tn(y, dh1, "dw_out")
    (dya_n, d_gb, *do_dd), (parts["w_out"],) = _outproj_bwd(dh1, full["w_out"][1], yb, gb, head_sum, exchange=scatter(w_out=g_out))
    grads_b = []
    for i, d in enumerate(DILATIONS):
        g3, got = _attn_bwd(qkv[i], do_dd[i], lse[i], do_dd[n_br + i], d, exchange=scatter(w_up=g_up) if i == 0 else ())
        grads_b.append(g3)
        if i == 0:
            (parts["w_up"],) = got
    dua, d_sw, d_b2, d_gs, d_ga = _sgu_bwd(ua, sw, b2, gs, ga, dya_n)
    dproj = _dproj(dua, [g[0] for g in grads_b], [g[1] for g in grads_b], [g[2] for g in grads_b], cos, sin)
    g_in, _ = _mm_tn(hn1, dproj, "dw_in")
    early = {
        "sgu_w": d_sw, "sgu_b": d_b2.reshape(CHUNK, 4, HEAD_DIM).sum(axis=-1).T, "sgu_norm_g": d_gs, "out_norm_a": d_ga,
        "out_norm_b": d_gb, "ffn_norm_g": d_ffn, "ple_norm_g": d_ple, "final_norm_g": d_fin,
    }
    (dx, d_mix), (parts["w_in"], early_parts) = _mm_norm_bwd(
        [(dproj, full["w_in"][1])], x, g_mix, dh1, "inproj_bwd",
        exchange=scatter(w_in=g_in) + [("gather", _pack_small(early, _SMALL[1:]))])
    (late_parts,) = _exchange_only([("gather", _pack_small({"mix_norm_g": d_mix}, _SMALL[:1]))], "gather_mix_norm_grad")
    return loss[0, 0], dx, parts, jnp.concatenate([late_parts, early_parts], axis=1)


def kernel(x, p, mix_norm_g, w_in, sgu_w, sgu_b, sgu_norm_g, out_norm_a, out_norm_b, w_out, ffn_norm_g, w_gate, w_up, w_down, ple_norm_g, w_ple_gate, w_ple_proj, final_norm_g, loss_target, m_mix_norm_g, m_w_in, m_sgu_w, m_sgu_b, m_sgu_norm_g, m_out_norm_a, m_out_norm_b, m_w_out, m_ffn_norm_g, m_w_gate, m_w_up, m_w_down, m_ple_norm_g, m_w_ple_gate, m_w_ple_proj, m_final_norm_g, v_mix_norm_g, v_w_in, v_sgu_w, v_sgu_b, v_sgu_norm_g, v_out_norm_a, v_out_norm_b, v_w_out, v_ffn_norm_g, v_w_gate, v_w_up, v_w_down, v_ple_norm_g, v_w_ple_gate, v_w_ple_proj, v_final_norm_g):
    given = dict(locals())
    weights = {n: given[n] for n in _ORDER}
    moments_m = {n: given["m_" + n] for n in _ORDER}
    moments_v = {n: given["v_" + n] for n in _ORDER}

    shards = {n: weights[n][0].astype(BF16) for n in _BIG}
    small = {n: (weights[n][0] if n in ("sgu_w", "sgu_b") else weights[n]) for n in _SMALL}

    loss, dx, parts, small_parts = _forward_backward(x[0], p[0, 0], loss_target[0], small, shards)
    loss = lax.psum(loss, ("x", "y", "c"))

    small_like = {n: weights[n] for n in _SMALL}
    grads, deltas, new_m, new_v = {}, {}, {}, {}
    for n in _BIG:
        g, d, nm, nv = _adamw(parts[n], weights[n][0], moments_m[n][0], moments_v[n][0], "adamw_" + n)
        grads[n], deltas[n], new_m[n], new_v[n] = g[None], d[None], nm[None], nv[None]
    g, d, nm, nv = _adamw(small_parts, _pack_small(small_like), _pack_small({n: moments_m[n] for n in _SMALL}),
                          _pack_small({n: moments_v[n] for n in _SMALL}), "adamw_small")
    for out, packed in ((grads, g), (deltas, d), (new_m, nm), (new_v, nv)):
        out.update(_unpack_small(packed, small_like))

    return (loss, dx[None], *[grads[n] for n in _ORDER], *[deltas[n] for n in _ORDER],
            *[new_m[n] for n in _ORDER], *[new_v[n] for n in _ORDER])
```

```python
import functools

import jax
import jax.numpy as jnp
from jax import lax
from jax.experimental import pallas as pl
from jax.experimental.pallas import tpu as pltpu

F32 = jnp.float32
BF16 = jnp.bfloat16

D_MODEL = 1024
WIDTH_A = 256
WIDTH_B = 768
D_FF = 2816
IN_COLS = 2 * WIDTH_A + 3 * WIDTH_B
PLE_DIM = 256
HEAD_DIM = 64
N_PAIRS = WIDTH_B // 128
CHUNK = 128
N_BACK = 128
DILATIONS = (1, 4, 16)
ROPE_THETA = 10000.0
EPS = 1e-6
N_DEV = 8

ADAM_LR = 0.001
ADAM_B1 = 0.9
ADAM_B2 = 0.999
ADAM_EPS = 1e-08
ADAM_WD = 0.01
ADAM_STEP = 10

V7X_VMEM_LIMIT_BYTES = 56 * 1024 * 1024
ROW_TILE = 512
MESH = pl.DeviceIdType.MESH
NEG = -1e30

_NT = (((1,), (1,)), ((), ()))
_TN = (((0,), (0,)), ((), ()))


def _params(*semantics):
    return pltpu.CompilerParams(dimension_semantics=semantics, vmem_limit_bytes=V7X_VMEM_LIMIT_BYTES)


def _rows(tm, width):
    return pl.BlockSpec((tm, width), lambda i: (i, 0))


def _whole(shape):
    return pl.BlockSpec(shape, lambda *_: (0,) * len(shape))


def _gelu(x):
    t = jnp.tanh(0.7978845608028654 * (x + 0.044715 * (x * x * x)))
    return 0.5 * x * (1.0 + t)


def _gelu_grad(x):
    t = jnp.tanh(0.7978845608028654 * (x + 0.044715 * (x * x * x)))
    return 0.5 * (1.0 + t) + 0.5 * x * (1.0 - t * t) * (0.7978845608028654 * (1.0 + 3.0 * 0.044715 * (x * x)))


def _rstd(x):
    return lax.rsqrt(jnp.mean(x * x, axis=-1, keepdims=True) + EPS)


def _norm_bwd(dn, h, g):
    r = _rstd(h)
    n = h * r
    t = dn * g
    return r * (t - n * jnp.mean(t * n, axis=-1, keepdims=True)), dn * n


def _swap_halves(x, first_half):
    return jnp.where(first_half, pltpu.roll(x, 96, 1), pltpu.roll(x, 32, 1))


def _sub_spec(d, n_cb, tm):
    return pl.BlockSpec((d, n_cb, tm // d, 128), lambda i: (0, 0, i, 0))


def _sub_shape(s_len, d, n_cb, dtype):
    return jax.ShapeDtypeStruct((d, n_cb, s_len // d, 128), dtype)


def _to_sub(stage_ref, cb_src, out_ref, cb_dst, d, tm):
    slab = stage_ref.at[cb_src]
    for r in range(d):
        out_ref[r, cb_dst] = slab[pl.ds(r, tm // d, stride=d), :].astype(out_ref.dtype)


def _from_sub(in_ref, cb_src, stage_ref, cb_dst, d, tm, accumulate=False):
    slab = stage_ref.at[cb_dst]
    for r in range(d):
        rows = pl.ds(r, tm // d, stride=d)
        val = in_ref[r, cb_src].astype(F32)
        slab[rows, :] = slab[rows, :] + val if accumulate else val


def _inproj(x, g, w, cos, sin, exchange=()):
    s_len = x.shape[0]
    tm = ROW_TILE
    n_cb = 3 * N_PAIRS

    def body(x_ref, g_ref, w_ref, cos_ref, sin_ref, ua_ref, hn_ref, *rest):
        sub_refs, stage = rest[:-1], rest[-1]
        xf = x_ref[...]
        hn = (xf * _rstd(xf) * g_ref[...]).astype(BF16)
        hn_ref[...] = hn
        c = cos_ref[...]
        s = sin_ref[...]
        first_half = (lax.broadcasted_iota(jnp.int32, (tm, 128), 1) % HEAD_DIM) < HEAD_DIM // 2
        for j in range(IN_COLS // 256):
            col = j * 256
            acc = jnp.dot(hn, w_ref[:, col:col + 256], preferred_element_type=F32)
            if col < 2 * WIDTH_A:
                ua_ref[:, col:col + 256] = acc
                continue
            for half in range(2):
                cb = (col - 2 * WIDTH_A) // 128 + half
                t = acc[:, half * 128:(half + 1) * 128]
                if cb < 2 * N_PAIRS:
                    t = (t * c + _swap_halves(t, first_half) * s) * (0.125 if cb < N_PAIRS else 1.0)
                stage[cb] = t
        for cb in range(n_cb):
            for d, out_ref in zip(DILATIONS, sub_refs):
                _to_sub(stage, cb, out_ref, cb, d, tm)

    return _call(
        body, name="inproj", grid=(s_len // tm,),
        in_specs=[_rows(tm, D_MODEL), _whole((1, D_MODEL)), _whole((D_MODEL, IN_COLS)), _rows(tm, 128), _rows(tm, 128)],
        out_specs=[_rows(tm, 2 * WIDTH_A), _rows(tm, D_MODEL)] + [_sub_spec(d, n_cb, tm) for d in DILATIONS],
        out_shape=[jax.ShapeDtypeStruct((s_len, 2 * WIDTH_A), F32), jax.ShapeDtypeStruct((s_len, D_MODEL), BF16)]
        + [_sub_shape(s_len, d, n_cb, BF16) for d in DILATIONS],
        scratch_shapes=[pltpu.VMEM((n_cb, tm, 128), F32)],
        semantics=("parallel",), args=(x, g, w, cos, sin), exchange=exchange)


def _sgu_mix_weights(sw_ref):
    keep = lax.broadcasted_iota(jnp.int32, (CHUNK, CHUNK), 0) >= lax.broadcasted_iota(jnp.int32, (CHUNK, CHUNK), 1)
    return [jnp.where(keep, sw_ref[h], 0.0).astype(BF16) for h in range(4)], keep


def _sgu_core(ua_ref, gs_ref):
    u = ua_ref[:, :WIDTH_A]
    va = ua_ref[:, WIDTH_A:]
    vg = _gelu(va)
    xc = vg - jnp.mean(vg, axis=-1, keepdims=True)
    rstd = lax.rsqrt(jnp.mean(xc * xc, axis=-1, keepdims=True) + EPS)
    xhat = xc * rstd
    return u, va, _gelu(u), xhat, rstd, xhat * gs_ref[...]


def _sgu_fwd(ua, sw, b2, gs, ga):
    s_len = ua.shape[0]
    tm = ROW_TILE

    def body(ua_ref, sw_ref, b2_ref, gs_ref, ga_ref, out_ref):
        _, _, ug, _, _, vn = _sgu_core(ua_ref, gs_ref)
        wm, _ = _sgu_mix_weights(sw_ref)
        head = lax.broadcasted_iota(jnp.int32, (CHUNK, WIDTH_A), 1) // HEAD_DIM
        for c in range(tm // CHUNK):
            rows = slice(c * CHUNK, (c + 1) * CHUNK)
            vnc = vn[rows]
            mixed = b2_ref[...]
            for h in range(4):
                mixed = mixed + jnp.dot(wm[h], jnp.where(head == h, vnc, 0.0).astype(BF16), preferred_element_type=F32)
            ya = ug[rows] * mixed
            out_ref[rows, :] = (ya * _rstd(ya) * ga_ref[...]).astype(BF16)

    return pl.pallas_call(
        body, name="sgu_fwd", grid=(s_len // tm,),
        in_specs=[_rows(tm, 2 * WIDTH_A), _whole((4, CHUNK, CHUNK)), _whole((CHUNK, WIDTH_A)), _whole((1, WIDTH_A)), _whole((1, WIDTH_A))],
        out_specs=_rows(tm, WIDTH_A),
        out_shape=jax.ShapeDtypeStruct((s_len, WIDTH_A), BF16),
        compiler_params=_params("parallel"),
    )(ua, sw, b2, gs, ga)


def _attn_geometry(sd):
    tile = min(ROW_TILE, sd)
    return tile, tile // CHUNK, sd // tile


PAIRS_PER_STEP = 6


def _attn_spec(cb0, rows, row_index):
    return pl.BlockSpec((None, PAIRS_PER_STEP, rows, 128), lambda r, g, n: (r, cb0 // PAIRS_PER_STEP + g, row_index(n), 0))


def _per_pair(one_pair):
    def body(*refs):
        for hp in range(PAIRS_PER_STEP):
            one_pair(*[ref.at[hp] for ref in refs])
    return body


def _both_heads(x, head_a):
    zero = jnp.zeros_like(x)
    return [jnp.where(head_a, x, zero), jnp.where(head_a, zero, x)]


def _attn_fwd(qkv, d, exchange=()):
    sd = qkv.shape[2]
    tile, nb, n_tiles = _attn_geometry(sd)

    def prev(n):
        return jnp.maximum(n * nb - 1, 0)

    def one_pair(q_ref, k_ref, kp_ref, v_ref, vp_ref, o_ref, l_ref):
        n = pl.program_id(2)
        head_a = lax.broadcasted_iota(jnp.int32, (CHUNK, 128), 1) < HEAD_DIM
        qi = lax.broadcasted_iota(jnp.int32, (2 * CHUNK, 2 * CHUNK), 0) % CHUNK
        kc = lax.broadcasted_iota(jnp.int32, (2 * CHUNK, 2 * CHUNK), 1)
        band = (kc >= qi) & (kc <= qi + N_BACK)
        for j in range(nb):
            rows = slice(j * CHUNK, (j + 1) * CHUNK)
            if j == 0:
                kcat = jnp.concatenate([kp_ref[...], k_ref[rows, :]], axis=0)
                vcat = jnp.concatenate([vp_ref[...], v_ref[rows, :]], axis=0)
                valid = band & jnp.logical_or(n > 0, kc >= CHUNK)
            else:
                kcat = k_ref[(j - 1) * CHUNK:(j + 1) * CHUNK, :]
                vcat = v_ref[(j - 1) * CHUNK:(j + 1) * CHUNK, :]
                valid = band
            q2 = jnp.concatenate(_both_heads(q_ref[rows, :], head_a), axis=0)
            s = lax.dot_general(q2, kcat, _NT, preferred_element_type=F32)
            s = jnp.where(valid, s, NEG)
            m = jnp.max(s, axis=-1, keepdims=True)
            p = jnp.exp(s - m)
            l = jnp.sum(p, axis=-1, keepdims=True)
            o2 = jnp.dot(p.astype(BF16), vcat, preferred_element_type=F32) / l
            lse2 = m + jnp.log(l)
            o_ref[rows, :] = jnp.where(head_a, o2[:CHUNK], o2[CHUNK:]).astype(BF16)
            l_ref[rows, :] = jnp.where(head_a, lse2[:CHUNK], lse2[CHUNK:])

    same = lambda n: n
    return _call(
        _per_pair(one_pair), name=f"attn_fwd_d{d}", grid=(d, N_PAIRS // PAIRS_PER_STEP, n_tiles),
        in_specs=[_attn_spec(0, tile, same), _attn_spec(N_PAIRS, tile, same), _attn_spec(N_PAIRS, CHUNK, prev),
                  _attn_spec(2 * N_PAIRS, tile, same), _attn_spec(2 * N_PAIRS, CHUNK, prev)],
        out_specs=[_attn_spec(0, tile, same), _attn_spec(0, tile, same)],
        out_shape=[jax.ShapeDtypeStruct((d, N_PAIRS, sd, 128), BF16), jax.ShapeDtypeStruct((d, N_PAIRS, sd, 128), F32)],
        semantics=("parallel", "parallel", "parallel"), args=(qkv, qkv, qkv, qkv, qkv), exchange=exchange)


def _combine(outs, lses, ya_n, gb, exchange=()):
    s_len = ya_n.shape[0]
    tm = ROW_TILE
    n_br = len(DILATIONS)

    def body(*refs):
        o_refs, l_refs = refs[:n_br], refs[n_br:2 * n_br]
        ya_ref, gb_ref, y_ref, yb_ref = refs[2 * n_br:2 * n_br + 4]
        lse_refs = refs[2 * n_br + 4:3 * n_br + 4]
        o_nat, l_nat, lse_nat = refs[3 * n_br + 4:]
        sumsq = jnp.zeros((tm, 1), F32)
        for cb in range(N_PAIRS):
            for i, d in enumerate(DILATIONS):
                _from_sub(o_refs[i], cb, o_nat, i, d, tm)
                _from_sub(l_refs[i], cb, l_nat, i, d, tm)
            ls = [l_nat[i] for i in range(n_br)]
            top = jnp.maximum(jnp.maximum(ls[0], ls[1]), ls[2])
            ws = [jnp.exp(l - top) for l in ls]
            den = ws[0] + ws[1] + ws[2]
            inv = 1.0 / den
            yb = (ws[0] * inv) * o_nat[0] + (ws[1] * inv) * o_nat[1] + (ws[2] * inv) * o_nat[2]
            yb_ref[:, cb * 128:(cb + 1) * 128] = yb
            sumsq = sumsq + jnp.sum(yb * yb, axis=-1, keepdims=True)
            lse_nat[cb] = top + jnp.log(den)
            for d, lse_ref in zip(DILATIONS, lse_refs):
                _to_sub(lse_nat, cb, lse_ref, cb, d, tm)
        r = lax.rsqrt(sumsq / WIDTH_B + EPS)
        y_ref[:, :WIDTH_A] = ya_ref[...]
        y_ref[:, WIDTH_A:] = (yb_ref[...] * r * gb_ref[...]).astype(BF16)

    return _call(
        body, name="attn_combine", grid=(s_len // tm,),
        in_specs=[_sub_spec(d, N_PAIRS, tm) for d in DILATIONS] * 2 + [_rows(tm, WIDTH_A), _whole((1, WIDTH_B))],
        out_specs=[_rows(tm, D_MODEL), _rows(tm, WIDTH_B)] + [_sub_spec(d, N_PAIRS, tm) for d in DILATIONS],
        out_shape=[jax.ShapeDtypeStruct((s_len, D_MODEL), BF16), jax.ShapeDtypeStruct((s_len, WIDTH_B), F32)]
        + [_sub_shape(s_len, d, N_PAIRS, F32) for d in DILATIONS],
        scratch_shapes=[pltpu.VMEM((n_br, tm, 128), F32), pltpu.VMEM((n_br, tm, 128), F32), pltpu.VMEM((N_PAIRS, tm, 128), F32)],
        semantics=("parallel",), args=(*outs, *lses, ya_n, gb), exchange=exchange)


def _mm_res(a, w, res, name, exchange=()):
    s_len, k = a.shape
    tm = ROW_TILE if k <= D_MODEL else ROW_TILE // 2

    def body(a_ref, w_ref, r_ref, o_ref):
        o_ref[...] = r_ref[...] + jnp.dot(a_ref[...].astype(BF16), w_ref[...], preferred_element_type=F32)

    (out,), received = _call(
        body, name=name, grid=(s_len // tm,),
        in_specs=[_rows(tm, k), _whole((k, D_MODEL)), _rows(tm, D_MODEL)],
        out_specs=[_rows(tm, D_MODEL)],
        out_shape=[jax.ShapeDtypeStruct((s_len, D_MODEL), F32)],
        semantics=("parallel",), args=(a, w, res), exchange=exchange)
    return out, received


def _ffn_up(h, g, wg, wu, exchange=()):
    s_len = h.shape[0]
    tm = ROW_TILE // 2

    def body(h_ref, g_ref, wg_ref, wu_ref, a_ref, b_ref, act_ref, hn_ref):
        hf = h_ref[...]
        hn = (hf * _rstd(hf) * g_ref[...]).astype(BF16)
        hn_ref[...] = hn
        for j in range(D_FF // 256):
            cols = slice(j * 256, (j + 1) * 256)
            a = jnp.dot(hn, wg_ref[:, cols], preferred_element_type=F32)
            b = jnp.dot(hn, wu_ref[:, cols], preferred_element_type=F32)
            a_ref[:, cols] = a.astype(BF16)
            b_ref[:, cols] = b.astype(BF16)
            act_ref[:, cols] = (a * jax.nn.sigmoid(a) * b).astype(BF16)

    wide = jax.ShapeDtypeStruct((s_len, D_FF), BF16)
    return _call(
        body, name="ffn_up", grid=(s_len // tm,),
        in_specs=[_rows(tm, D_MODEL), _whole((1, D_MODEL)), _whole((D_MODEL, D_FF)), _whole((D_MODEL, D_FF))],
        out_specs=[_rows(tm, D_FF), _rows(tm, D_FF), _rows(tm, D_FF), _rows(tm, D_MODEL)],
        out_shape=[wide, wide, wide, jax.ShapeDtypeStruct((s_len, D_MODEL), BF16)],
        semantics=("parallel",), args=(h, g, wg, wu), exchange=exchange)


def _ple(h, g, wpg, p, wpp):
    s_len = h.shape[0]
    tm = ROW_TILE

    def body(h_ref, g_ref, wpg_ref, p_ref, wpp_ref, h3_ref, gate_ref, pp_ref, hn_ref):
        hf = h_ref[...]
        hn = (hf * _rstd(hf) * g_ref[...]).astype(BF16)
        hn_ref[...] = hn
        gate = jax.nn.sigmoid(jnp.dot(hn, wpg_ref[...], preferred_element_type=F32))
        pp = jnp.dot(p_ref[...].astype(BF16), wpp_ref[...], preferred_element_type=F32)
        h3_ref[...] = hf + gate * pp
        gate_ref[...] = gate.astype(BF16)
        pp_ref[...] = pp.astype(BF16)

    half = jax.ShapeDtypeStruct((s_len, D_MODEL), BF16)
    return pl.pallas_call(
        body, name="ple", grid=(s_len // tm,),
        in_specs=[_rows(tm, D_MODEL), _whole((1, D_MODEL)), _whole((D_MODEL, D_MODEL)), _rows(tm, PLE_DIM), _whole((PLE_DIM, D_MODEL))],
        out_specs=[_rows(tm, D_MODEL)] * 4,
        out_shape=[jax.ShapeDtypeStruct((s_len, D_MODEL), F32), half, half, half],
        compiler_params=_params("parallel"),
    )(h, g, wpg, p, wpp)


def _loss_ple_bwd(h3, target, gf, gate, pp, h2, g_ple, wpgt):
    s_len = h3.shape[0]
    tm = ROW_TILE

    def body(h_ref, t_ref, g_ref, gate_ref, pp_ref, h2_ref, gp_ref, w_ref,
             dz_ref, dpp_ref, dh2_ref, loss_ref, dg_ref, dgp_ref):
        @pl.when(pl.program_id(0) == 0)
        def _():
            loss_ref[...] = jnp.zeros_like(loss_ref)
            dg_ref[...] = jnp.zeros_like(dg_ref)
            dgp_ref[...] = jnp.zeros_like(dgp_ref)

        hf = h_ref[...]
        gfv = g_ref[...]
        err = hf * _rstd(hf) * gfv - t_ref[...]
        loss_ref[...] += 0.5 * jnp.sum(jnp.sum(err * err, axis=-1, keepdims=True), axis=0, keepdims=True) / D_MODEL
        dh, dg_rows = _norm_bwd(err / D_MODEL, hf, gfv)
        dg_ref[...] += jnp.sum(dg_rows, axis=0, keepdims=True)
        gate = gate_ref[...].astype(F32)
        dz = (dh * pp_ref[...].astype(F32) * gate * (1.0 - gate)).astype(BF16)
        dz_ref[...] = dz
        dpp_ref[...] = (dh * gate).astype(BF16)
        dn = jnp.dot(dz, w_ref[...], preferred_element_type=F32)
        dh2, dgp_rows = _norm_bwd(dn, h2_ref[...], gp_ref[...])
        dh2_ref[...] = dh + dh2
        dgp_ref[...] += jnp.sum(dgp_rows, axis=0, keepdims=True)

    half = jax.ShapeDtypeStruct((s_len, D_MODEL), BF16)
    gain = jax.ShapeDtypeStruct((1, D_MODEL), F32)
    return pl.pallas_call(
        body, name="loss_ple_bwd", grid=(s_len // tm,),
        in_specs=[_rows(tm, D_MODEL), _rows(tm, D_MODEL), _whole((1, D_MODEL)), _rows(tm, D_MODEL), _rows(tm, D_MODEL),
                  _rows(tm, D_MODEL), _whole((1, D_MODEL)), _whole((D_MODEL, D_MODEL))],
        out_specs=[_rows(tm, D_MODEL), _rows(tm, D_MODEL), _rows(tm, D_MODEL), _whole((1, 128)), _whole((1, D_MODEL)),
                   _whole((1, D_MODEL))],
        out_shape=[half, half, jax.ShapeDtypeStruct((s_len, D_MODEL), F32), jax.ShapeDtypeStruct((1, 128), F32), gain, gain],
        compiler_params=_params("arbitrary"),
    )(h3, target, gf, gate, pp, h2, g_ple, wpgt)


def _mm_norm_bwd(parts, h, g, dres, name, exchange=()):
    s_len = h.shape[0]
    tm = ROW_TILE // 2
    n_parts = len(parts)

    def body(*refs):
        a_refs = refs[0:2 * n_parts:2]
        w_refs = refs[1:2 * n_parts:2]
        h_ref, g_ref, r_ref, o_ref, dg_ref = refs[2 * n_parts:]

        @pl.when(pl.program_id(0) == 0)
        def _():
            dg_ref[...] = jnp.zeros_like(dg_ref)

        dn = jnp.dot(a_refs[0][...], w_refs[0][...], preferred_element_type=F32)
        for a_ref, w_ref in zip(a_refs[1:], w_refs[1:]):
            dn = dn + jnp.dot(a_ref[...], w_ref[...], preferred_element_type=F32)
        dh, dg_rows = _norm_bwd(dn, h_ref[...], g_ref[...])
        o_ref[...] = r_ref[...] + dh
        dg_ref[...] += jnp.sum(dg_rows, axis=0, keepdims=True)

    in_specs, args = [], []
    for a, w in parts:
        in_specs += [_rows(tm, a.shape[1]), _whole(w.shape)]
        args += [a, w]
    return _call(
        body, name=name, grid=(s_len // tm,),
        in_specs=in_specs + [_rows(tm, D_MODEL), _whole((1, D_MODEL)), _rows(tm, D_MODEL)],
        out_specs=[_rows(tm, D_MODEL), _whole((1, D_MODEL))],
        out_shape=[jax.ShapeDtypeStruct((s_len, D_MODEL), F32), jax.ShapeDtypeStruct((1, D_MODEL), F32)],
        semantics=("arbitrary",), args=(*args, h, g, dres), exchange=exchange)


def _ffn_down_bwd(dh, wdt, a, b, exchange=()):
    s_len = dh.shape[0]
    tm = ROW_TILE // 2

    def body(dh_ref, w_ref, a_ref, b_ref, da_ref, db_ref):
        dhb = dh_ref[...].astype(BF16)
        for j in range(D_FF // 256):
            cols = slice(j * 256, (j + 1) * 256)
            dact = jnp.dot(dhb, w_ref[:, cols], preferred_element_type=F32)
            av = a_ref[:, cols].astype(F32)
            bv = b_ref[:, cols].astype(F32)
            sig = jax.nn.sigmoid(av)
            da_ref[:, cols] = (dact * bv * sig * (1.0 + av * (1.0 - sig))).astype(BF16)
            db_ref[:, cols] = (dact * av * sig).astype(BF16)

    wide = jax.ShapeDtypeStruct((s_len, D_FF), BF16)
    return _call(
        body, name="ffn_down_bwd", grid=(s_len // tm,),
        in_specs=[_rows(tm, D_MODEL), _whole((D_MODEL, D_FF)), _rows(tm, D_FF), _rows(tm, D_FF)],
        out_specs=[_rows(tm, D_FF), _rows(tm, D_FF)],
        out_shape=[wide, wide],
        semantics=("parallel",), args=(dh, wdt, a, b), exchange=exchange)


def _outproj_bwd(dh1, woutt, yb, gb, head_sum, exchange=()):
    s_len = dh1.shape[0]
    tm = ROW_TILE
    n_br = len(DILATIONS)

    def body(dh_ref, w_ref, yb_ref, gb_ref, e_ref, dya_ref, dgb_ref, *rest):
        do_refs, dd_refs = rest[:n_br], rest[n_br:2 * n_br]
        do_nat, dd_nat = rest[2 * n_br:]

        @pl.when(pl.program_id(0) == 0)
        def _():
            dgb_ref[...] = jnp.zeros_like(dgb_ref)

        dhb = dh_ref[...].astype(BF16)
        dya_ref[...] = jnp.dot(dhb, w_ref[:, :WIDTH_A], preferred_element_type=F32)
        dyn = jnp.dot(dhb, w_ref[:, WIDTH_A:], preferred_element_type=F32)
        ybv = yb_ref[...]
        dyb, dg_rows = _norm_bwd(dyn, ybv, gb_ref[...])
        dgb_ref[...] += jnp.sum(dg_rows, axis=0, keepdims=True)
        prod = dyb * ybv
        for cb in range(N_PAIRS):
            cols = slice(cb * 128, (cb + 1) * 128)
            pc = prod[:, cols]
            hi = pc.astype(BF16)
            lo = (pc - hi.astype(F32)).astype(BF16)
            do_nat[cb] = dyb[:, cols]
            dd_nat[cb] = (jnp.dot(hi, e_ref[...], preferred_element_type=F32)
                          + jnp.dot(lo, e_ref[...], preferred_element_type=F32))
            for i, d in enumerate(DILATIONS):
                _to_sub(do_nat, cb, do_refs[i], cb, d, tm)
                _to_sub(dd_nat, cb, dd_refs[i], cb, d, tm)

    subs = [_sub_spec(d, N_PAIRS, tm) for d in DILATIONS]
    return _call(
        body, name="outproj_bwd", grid=(s_len // tm,),
        in_specs=[_rows(tm, D_MODEL), _whole((D_MODEL, D_MODEL)), _rows(tm, WIDTH_B), _whole((1, WIDTH_B)), _whole((128, 128))],
        out_specs=[_rows(tm, WIDTH_A), _whole((1, WIDTH_B))] + subs + subs,
        out_shape=[jax.ShapeDtypeStruct((s_len, WIDTH_A), F32), jax.ShapeDtypeStruct((1, WIDTH_B), F32)]
        + [_sub_shape(s_len, d, N_PAIRS, BF16) for d in DILATIONS] + [_sub_shape(s_len, d, N_PAIRS, F32) for d in DILATIONS],
        scratch_shapes=[pltpu.VMEM((N_PAIRS, tm, 128), F32), pltpu.VMEM((N_PAIRS, tm, 128), F32)],
        semantics=("arbitrary",), args=(dh1, woutt, yb, gb, head_sum), exchange=exchange)


def _attn_bwd(qkv, do, lse, dd, d, exchange=()):
    sd = qkv.shape[2]
    tile, nb, n_tiles = _attn_geometry(sd)
    last_block = sd // CHUNK - 1

    def nxt(n):
        return jnp.minimum((n + 1) * nb, last_block)

    def one_pair(q_ref, qn_ref, k_ref, v_ref, do_ref, don_ref, l_ref, ln_ref, dd_ref, ddn_ref,
                 dq_ref, dk_ref, dv_ref, carry_ref):
        n = pl.program_id(2)

        @pl.when(n == 0)
        def _():
            carry_ref[...] = jnp.zeros_like(carry_ref)

        head_a = lax.broadcasted_iota(jnp.int32, (CHUNK, 128), 1) < HEAD_DIM
        row = lax.broadcasted_iota(jnp.int32, (4 * CHUNK, CHUNK), 0)
        qi = row % CHUNK
        ki = lax.broadcasted_iota(jnp.int32, (4 * CHUNK, CHUNK), 1)
        is_after = row >= 2 * CHUNK
        mask = (is_after & (ki >= qi)) | (jnp.logical_not(is_after) & (qi >= ki))
        mask_last = mask & jnp.logical_or(jnp.logical_not(is_after), n < n_tiles - 1)
        dq_acc = [carry_ref[...]] + [jnp.zeros((CHUNK, 128), F32) for _ in range(nb)]

        def per_head(x):
            other = pltpu.roll(x, HEAD_DIM, 1)
            return [jnp.where(head_a, x, other), jnp.where(head_a, other, x)]

        for j in range(nb):
            rows = slice(j * CHUNK, (j + 1) * CHUNK)
            kj = k_ref[rows, :]
            vj = v_ref[rows, :]
            if j + 1 < nb:
                nrows = slice((j + 1) * CHUNK, (j + 2) * CHUNK)
                q2, do2, l2, dd2, msk = q_ref[nrows, :], do_ref[nrows, :], l_ref[nrows, :], dd_ref[nrows, :], mask
            else:
                q2, do2, l2, dd2, msk = qn_ref[...], don_ref[...], ln_ref[...], ddn_ref[...], mask_last
            qs = jnp.concatenate(_both_heads(q_ref[rows, :], head_a) + _both_heads(q2, head_a), axis=0)
            dos = jnp.concatenate(_both_heads(do_ref[rows, :], head_a) + _both_heads(do2, head_a), axis=0)
            ls = jnp.concatenate(per_head(l_ref[rows, :]) + per_head(l2), axis=0)
            dds = jnp.concatenate(per_head(dd_ref[rows, :]) + per_head(dd2), axis=0)
            s = lax.dot_general(qs, kj, _NT, preferred_element_type=F32)
            p = jnp.exp(jnp.where(msk, s - ls, NEG))
            dp = lax.dot_general(dos, vj, _NT, preferred_element_type=F32)
            ds = (p * (dp - dds)).astype(BF16)
            dv_ref[rows, :] = lax.dot_general(p.astype(BF16), dos, _TN, preferred_element_type=F32).astype(BF16)
            dk_ref[rows, :] = lax.dot_general(ds, qs, _TN, preferred_element_type=F32).astype(BF16)
            dqs = jnp.dot(ds, kj, preferred_element_type=F32)
            dq_acc[j] = dq_acc[j] + jnp.where(head_a, dqs[:CHUNK], dqs[CHUNK:2 * CHUNK])
            dq_acc[j + 1] = dq_acc[j + 1] + jnp.where(head_a, dqs[2 * CHUNK:3 * CHUNK], dqs[3 * CHUNK:])
        for j in range(nb):
            dq_ref[j * CHUNK:(j + 1) * CHUNK, :] = dq_acc[j].astype(BF16)
        carry_ref[...] = dq_acc[nb]

    same = lambda n: n
    grad = jax.ShapeDtypeStruct((d, N_PAIRS, sd, 128), BF16)
    return _call(
        _per_pair(one_pair), name=f"attn_bwd_d{d}", grid=(d, N_PAIRS // PAIRS_PER_STEP, n_tiles),
        in_specs=[_attn_spec(0, tile, same), _attn_spec(0, CHUNK, nxt), _attn_spec(N_PAIRS, tile, same),
                  _attn_spec(2 * N_PAIRS, tile, same), _attn_spec(0, tile, same), _attn_spec(0, CHUNK, nxt),
                  _attn_spec(0, tile, same), _attn_spec(0, CHUNK, nxt), _attn_spec(0, tile, same), _attn_spec(0, CHUNK, nxt)],
        out_specs=[_attn_spec(0, tile, same)] * 3,
        out_shape=[grad, grad, grad],
        scratch_shapes=[pltpu.VMEM((PAIRS_PER_STEP, CHUNK, 128), F32)],
        semantics=("parallel", "parallel", "arbitrary"), args=(qkv, qkv, qkv, qkv, do, do, lse, lse, dd, dd), exchange=exchange)


def _sgu_bwd(ua, sw, b2, gs, ga, dya_n):
    s_len = ua.shape[0]
    tm = ROW_TILE

    def body(ua_ref, sw_ref, b2_ref, gs_ref, ga_ref, dy_ref, dua_ref, dsw_ref, db2_ref, dgs_ref, dga_ref):
        @pl.when(pl.program_id(0) == 0)
        def _():
            dsw_ref[...] = jnp.zeros_like(dsw_ref)
            db2_ref[...] = jnp.zeros_like(db2_ref)
            dgs_ref[...] = jnp.zeros_like(dgs_ref)
            dga_ref[...] = jnp.zeros_like(dga_ref)

        u, va, ug, xhat, rstd, vn = _sgu_core(ua_ref, gs_ref)
        wm, keep = _sgu_mix_weights(sw_ref)
        head = lax.broadcasted_iota(jnp.int32, (CHUNK, WIDTH_A), 1) // HEAD_DIM
        gav = ga_ref[...]
        gsv = gs_ref[...]
        dga = jnp.zeros((1, WIDTH_A), F32)
        dgs = jnp.zeros((1, WIDTH_A), F32)
        db2 = jnp.zeros((CHUNK, WIDTH_A), F32)
        dsw = [jnp.zeros((CHUNK, CHUNK), F32) for _ in range(4)]
        for c in range(tm // CHUNK):
            rows = slice(c * CHUNK, (c + 1) * CHUNK)
            vnc = vn[rows]
            vnb = vnc.astype(BF16)
            mixed = b2_ref[...]
            for h in range(4):
                mixed = mixed + jnp.dot(wm[h], jnp.where(head == h, vnc, 0.0).astype(BF16), preferred_element_type=F32)
            ugc = ug[rows]
            dya, dga_rows = _norm_bwd(dy_ref[rows, :], ugc * mixed, gav)
            dga = dga + jnp.sum(dga_rows, axis=0, keepdims=True)
            dmixed = dya * ugc
            db2 = db2 + dmixed
            dvn = jnp.zeros((CHUNK, WIDTH_A), F32)
            for h in range(4):
                dmh = jnp.where(head == h, dmixed, 0.0).astype(BF16)
                dsw[h] = dsw[h] + lax.dot_general(dmh, vnb, _NT, preferred_element_type=F32)
                dvn = dvn + lax.dot_general(wm[h], dmh, _TN, preferred_element_type=F32)
            xh = xhat[rows]
            dgs = dgs + jnp.sum(dvn * xh, axis=0, keepdims=True)
            dxh = dvn * gsv
            dvg = rstd[rows] * (dxh - jnp.mean(dxh, axis=-1, keepdims=True) - xh * jnp.mean(dxh * xh, axis=-1, keepdims=True))
            dua_ref[rows, :WIDTH_A] = (dya * mixed * _gelu_grad(u[rows])).astype(BF16)
            dua_ref[rows, WIDTH_A:] = (dvg * _gelu_grad(va[rows])).astype(BF16)
        for h in range(4):
            dsw_ref[h] += jnp.where(keep, dsw[h], 0.0)
        db2_ref[...] += db2
        dgs_ref[...] += dgs
        dga_ref[...] += dga

    return pl.pallas_call(
        body, name="sgu_bwd", grid=(s_len // tm,),
        in_specs=[_rows(tm, 2 * WIDTH_A), _whole((4, CHUNK, CHUNK)), _whole((CHUNK, WIDTH_A)), _whole((1, WIDTH_A)),
                  _whole((1, WIDTH_A)), _rows(tm, WIDTH_A)],
        out_specs=[_rows(tm, 2 * WIDTH_A), _whole((4, CHUNK, CHUNK)), _whole((CHUNK, WIDTH_A)), _whole((1, WIDTH_A)), _whole((1, WIDTH_A))],
        out_shape=[jax.ShapeDtypeStruct((s_len, 2 * WIDTH_A), BF16), jax.ShapeDtypeStruct((4, CHUNK, CHUNK), F32),
                   jax.ShapeDtypeStruct((CHUNK, WIDTH_A), F32), jax.ShapeDtypeStruct((1, WIDTH_A), F32),
                   jax.ShapeDtypeStruct((1, WIDTH_A), F32)],
        compiler_params=_params("arbitrary"),
    )(ua, sw, b2, gs, ga, dya_n)


def _dproj(dua, dqs, dks, dvs, cos, sin):
    s_len = dua.shape[0]
    tm = ROW_TILE
    n_br = len(DILATIONS)

    def body(dua_ref, *rest):
        groups = [rest[g * n_br:(g + 1) * n_br] for g in range(3)]
        cos_ref, sin_ref, out_ref, acc = rest[3 * n_br:]
        out_ref[:, :2 * WIDTH_A] = dua_ref[...]
        c = cos_ref[...]
        s = sin_ref[...]
        first_half = (lax.broadcasted_iota(jnp.int32, (tm, 128), 1) % HEAD_DIM) < HEAD_DIM // 2
        for g, refs in enumerate(groups):
            for cb in range(N_PAIRS):
                for i, d in enumerate(DILATIONS):
                    _from_sub(refs[i], cb, acc, 0, d, tm, accumulate=i > 0)
                t = acc[0]
                if g < 2:
                    t = (t * c - _swap_halves(t, first_half) * s) * (0.125 if g == 0 else 1.0)
                col = 2 * WIDTH_A + g * WIDTH_B + cb * 128
                out_ref[:, col:col + 128] = t.astype(BF16)

    subs = [_sub_spec(d, N_PAIRS, tm) for d in DILATIONS]
    return pl.pallas_call(
        body, name="dproj", grid=(s_len // tm,),
        in_specs=[_rows(tm, 2 * WIDTH_A)] + subs * 3 + [_rows(tm, 128), _rows(tm, 128)],
        out_specs=_rows(tm, IN_COLS),
        out_shape=jax.ShapeDtypeStruct((s_len, IN_COLS), BF16),
        scratch_shapes=[pltpu.VMEM((1, tm, 128), F32)],
        compiler_params=_params("parallel"),
    )(dua, *dqs, *dks, *dvs, cos, sin)


def _mm_tn(a, b, name, exchange=()):
    s_len, m = a.shape
    n = b.shape[1]
    tk = ROW_TILE
    tm = m if m <= 512 else (1408 if m == D_FF else 512)
    n_k = s_len // tk

    def body(a_ref, b_ref, o_ref, acc_ref):
        k = pl.program_id(1)

        @pl.when(k == 0)
        def _():
            acc_ref[...] = jnp.zeros_like(acc_ref)

        acc_ref[...] += lax.dot_general(a_ref[...].astype(BF16), b_ref[...].astype(BF16), _TN, preferred_element_type=F32)

        @pl.when(k == n_k - 1)
        def _():
            o_ref[...] = acc_ref[...].astype(BF16)

    (grad,), received = _call(
        body, name=name, grid=(m // tm, n_k),
        in_specs=[pl.BlockSpec((tk, tm), lambda i, k: (k, i)), pl.BlockSpec((tk, n), lambda i, k: (k, 0))],
        out_specs=[pl.BlockSpec((tm, n), lambda i, k: (i, 0))],
        out_shape=[jax.ShapeDtypeStruct((m, n), BF16)],
        scratch_shapes=[pltpu.VMEM((tm, n), F32)],
        semantics=("parallel", "arbitrary"), args=(a, b), exchange=exchange)
    return grad, received


def _position():
    x, y, c = lax.axis_index("x"), lax.axis_index("y"), lax.axis_index("c")
    return x, y, c, 4 * x + 2 * y + c


def _peer(x, y, c, rel):
    return (x ^ ((rel >> 2) & 1), y ^ ((rel >> 1) & 1), c ^ (rel & 1))


def _exchange_out_shape(kind, arr):
    return jax.ShapeDtypeStruct(((N_DEV,) + arr.shape) if kind == "gather" else arr.shape, arr.dtype)


def _exchange_sems(n_items):
    return [pltpu.SemaphoreType.DMA((n_items, N_DEV)), pltpu.SemaphoreType.DMA((n_items, N_DEV)), pltpu.SemaphoreType.DMA((n_items,))]


def _exchange_copies(kinds, srcs, dsts, sems):
    send_sems, recv_sems, local_sems = sems
    x, y, c, me = _position()
    local, sends, recvs = [], [], []
    for k, (kind, src, dst) in enumerate(zip(kinds, srcs, dsts)):
        own = src if kind == "gather" else src.at[me]
        local.append(pltpu.make_async_copy(own, dst.at[me], local_sems.at[k]))
        for rel in range(1, N_DEV):
            going = src if kind == "gather" else src.at[me ^ rel]
            common = dict(send_sem=send_sems.at[k, rel], recv_sem=recv_sems.at[k, rel],
                          device_id=_peer(x, y, c, rel), device_id_type=MESH)
            sends.append(pltpu.make_async_remote_copy(src_ref=going, dst_ref=dst.at[me], **common))
            recvs.append(pltpu.make_async_remote_copy(src_ref=own, dst_ref=dst.at[me ^ rel], **common))
    return local, sends, recvs


def _exchange_start(kinds, srcs, dsts, sems):
    local, sends, _ = _exchange_copies(kinds, srcs, dsts, sems)
    for cp in local + sends:
        cp.start()


def _exchange_finish(kinds, srcs, dsts, sems):
    local, sends, recvs = _exchange_copies(kinds, srcs, dsts, sems)
    for cp in recvs:
        cp.wait_recv()
    for cp in sends:
        cp.wait_send()
    for cp in local:
        cp.wait()


def _exchange_only(items, name):
    kinds = [k for k, _ in items]
    n = len(items)

    def body(*refs):
        srcs, dsts, sems = refs[:n], refs[n:2 * n], refs[2 * n:]
        _exchange_start(kinds, srcs, dsts, sems)
        _exchange_finish(kinds, srcs, dsts, sems)

    any_spec = pl.BlockSpec(memory_space=pl.ANY)
    return pl.pallas_call(
        body, name=name, in_specs=[any_spec] * n, out_specs=[any_spec] * n,
        out_shape=[_exchange_out_shape(k, a) for k, a in items],
        scratch_shapes=_exchange_sems(n),
        compiler_params=pltpu.CompilerParams(has_side_effects=True),
    )(*[a for _, a in items])


def _gather_two_level(shard, name):
    def body(src, dst, send_sems, recv_sems, local_sem):
        x, y, c, me = _position()
        sibling = (x, y, 1 - c)
        chips = [(1 - x, y), (x, 1 - y), (1 - x, 1 - y)]

        def block(px, py, pc):
            return dst.at[4 * px + 2 * py + pc]

        def copy(k, blk, to, src_ref=None):
            return pltpu.make_async_remote_copy(
                src_ref=block(*blk) if src_ref is None else src_ref, dst_ref=block(*blk),
                send_sem=send_sems.at[k], recv_sem=recv_sems.at[k], device_id=to, device_id_type=MESH)

        mine = pltpu.make_async_copy(src, dst.at[me], local_sem)
        mine.start()
        first = [copy(0, (x, y, c), sibling, src)] + [copy(1 + j, (x, y, c), (*chip, c), src) for j, chip in enumerate(chips)]
        for cp in first:
            cp.start()
        passed = [copy(4 + j, (*chip, c), sibling) for j, chip in enumerate(chips)]
        for j, chip in enumerate(chips):
            copy(1 + j, (*chip, c), (x, y, c)).wait_recv()
            passed[j].start()
        copy(0, (x, y, 1 - c), (x, y, c)).wait_recv()
        for j, chip in enumerate(chips):
            copy(4 + j, (*chip, 1 - c), (x, y, c)).wait_recv()
        for cp in first + passed:
            cp.wait_send()
        mine.wait()

    any_spec = pl.BlockSpec(memory_space=pl.ANY)
    return pl.pallas_call(
        body, name=name, in_specs=[any_spec], out_specs=any_spec,
        out_shape=_exchange_out_shape("gather", shard),
        scratch_shapes=[pltpu.SemaphoreType.DMA((N_DEV - 1,)), pltpu.SemaphoreType.DMA((N_DEV - 1,)), pltpu.SemaphoreType.DMA],
        compiler_params=pltpu.CompilerParams(has_side_effects=True),
    )(shard)


def _call(body, *, name, grid, in_specs, out_specs, out_shape, args, scratch_shapes=(), semantics, exchange=()):
    if not exchange:
        outs = pl.pallas_call(body, name=name, grid=grid, in_specs=in_specs, out_specs=out_specs, out_shape=out_shape,
                              scratch_shapes=list(scratch_shapes), compiler_params=_params(*semantics))(*args)
        return outs, []
    kinds = [k for k, _ in exchange]
    n_in, n_out, n_x, n_scr = len(in_specs), len(out_specs), len(exchange), len(scratch_shapes)

    def wrapped(*refs):
        ins, refs = refs[:n_in], refs[n_in:]
        srcs, refs = refs[:n_x], refs[n_x:]
        outs, refs = refs[:n_out], refs[n_out:]
        dsts, refs = refs[:n_x], refs[n_x:]
        scratch, sems = refs[:n_scr], refs[n_scr:]
        ids = [pl.program_id(a) for a in range(len(grid))]
        first = functools.reduce(jnp.logical_and, [i == 0 for i in ids])
        last = functools.reduce(jnp.logical_and, [i == g - 1 for i, g in zip(ids, grid)])

        @pl.when(first)
        def _():
            _exchange_start(kinds, srcs, dsts, sems)

        body(*ins, *outs, *scratch)

        @pl.when(last)
        def _():
            _exchange_finish(kinds, srcs, dsts, sems)

    any_spec = pl.BlockSpec(memory_space=pl.ANY)
    outs = pl.pallas_call(
        wrapped, name=name, grid=grid,
        in_specs=list(in_specs) + [any_spec] * n_x, out_specs=list(out_specs) + [any_spec] * n_x,
        out_shape=list(out_shape) + [_exchange_out_shape(k, a) for k, a in exchange],
        scratch_shapes=list(scratch_shapes) + _exchange_sems(n_x),
        compiler_params=pltpu.CompilerParams(dimension_semantics=("arbitrary",) * len(grid),
                                             vmem_limit_bytes=V7X_VMEM_LIMIT_BYTES, has_side_effects=True),
    )(*args, *[a for _, a in exchange])
    return outs[:n_out], outs[n_out:]


def _adamw_math(w, g, m, v):
    m = ADAM_B1 * m + (1.0 - ADAM_B1) * g
    v = ADAM_B2 * v + (1.0 - ADAM_B2) * (g * g)
    m_hat = m / (1.0 - ADAM_B1 ** ADAM_STEP)
    v_hat = v / (1.0 - ADAM_B2 ** ADAM_STEP)
    return -ADAM_LR * (m_hat / (jnp.sqrt(v_hat) + ADAM_EPS) + ADAM_WD * w), m, v


def _adamw(parts, w, m, v, name):
    rows, cols = w.shape
    tm = 256 if rows % 256 == 0 and rows > 256 else rows

    def body(p_ref, w_ref, m_ref, v_ref, g_ref, d_ref, nm_ref, nv_ref):
        g = p_ref[0].astype(F32)
        for j in range(1, N_DEV):
            g = g + p_ref[j].astype(F32)
        delta, nm, nv = _adamw_math(w_ref[...], g, m_ref[...], v_ref[...])
        g_ref[...] = g
        d_ref[...] = delta
        nm_ref[...] = nm
        nv_ref[...] = nv

    shard = jax.ShapeDtypeStruct((rows, cols), F32)
    return pl.pallas_call(
        body, name=name, grid=(rows // tm,),
        in_specs=[pl.BlockSpec((N_DEV, tm, cols), lambda i: (0, i, 0))] + [_rows(tm, cols)] * 3,
        out_specs=[_rows(tm, cols)] * 4,
        out_shape=[shard] * 4,
        compiler_params=_params("parallel"),
    )(parts, w, m, v)


_SMALL = ("mix_norm_g", "sgu_w", "sgu_b", "sgu_norm_g", "out_norm_a", "out_norm_b", "ffn_norm_g", "ple_norm_g", "final_norm_g")
_BIG = ("w_in", "w_out", "w_gate", "w_up", "w_down", "w_ple_gate", "w_ple_proj")
_COLUMN_SHARDED = ("w_in", "w_gate", "w_up", "w_ple_proj")
_ORDER = ("mix_norm_g", "w_in", "sgu_w", "sgu_b", "sgu_norm_g", "out_norm_a", "out_norm_b", "w_out", "ffn_norm_g",
          "w_gate", "w_up", "w_down", "ple_norm_g", "w_ple_gate", "w_ple_proj", "final_norm_g")


def _pack_small(values, names=_SMALL):
    flat = jnp.concatenate([values[n].reshape(-1).astype(F32) for n in names])
    pad = (-flat.shape[0]) % (8 * 128)
    return jnp.pad(flat, (0, pad)).reshape(-1, 128)


def _unpack_small(packed, like):
    flat = packed.reshape(-1)
    out, at = {}, 0
    for n in _SMALL:
        size = like[n].size
        out[n] = flat[at:at + size].reshape(like[n].shape)
        at += size
    return out


def _full_from_gathered(name, gathered):
    if name in _COLUMN_SHARDED:
        k, n = gathered.shape[1], gathered.shape[2] * N_DEV
        return gathered.transpose(1, 0, 2).reshape(k, n), gathered.transpose(0, 2, 1).reshape(n, k)
    k, n = gathered.shape[1] * N_DEV, gathered.shape[2]
    return gathered.reshape(k, n), gathered.transpose(2, 0, 1).reshape(n, k)


def _sliced_for_devices(name, grad):
    k, n = grad.shape
    if name in _COLUMN_SHARDED:
        return grad.reshape(k, N_DEV, n // N_DEV).transpose(1, 0, 2)
    return grad.reshape(N_DEV, k // N_DEV, n)


def _rope_tables(s_len):
    half = HEAD_DIM // 2
    inv = ROPE_THETA ** (-jnp.arange(half, dtype=F32) / half)
    ang = jnp.arange(s_len, dtype=F32)[:, None] * inv[None, :]
    cos, sin = jnp.cos(ang), jnp.sin(ang)
    return jnp.concatenate([cos, cos, cos, cos], axis=1), jnp.concatenate([-sin, sin, -sin, sin], axis=1)


def _forward_backward(x, p, target, small, shards):
    def gather(*names):
        return [("gather", shards[n]) for n in names]

    def scatter(**grads):
        return [("scatter", _sliced_for_devices(n, g)) for n, g in grads.items()]

    full, parts = {}, {}
    full["w_in"] = _full_from_gathered("w_in", _gather_two_level(shards["w_in"], "gather_w_in"))

    s_len = x.shape[0]
    cos, sin = _rope_tables(s_len)
    g_mix, g_ffn, g_ple = small["mix_norm_g"], small["ffn_norm_g"], small["ple_norm_g"]
    g_fin = small["final_norm_g"].reshape(1, D_MODEL)
    sw, gs, ga, gb = small["sgu_w"], small["sgu_norm_g"], small["out_norm_a"], small["out_norm_b"]
    b2 = jnp.repeat(small["sgu_b"].T, HEAD_DIM, axis=1)
    lane_head = jnp.arange(128) // HEAD_DIM
    head_sum = (lane_head[:, None] == lane_head[None, :]).astype(BF16)
    n_br = len(DILATIONS)

    def arrived(names, got):
        for n, g in zip(names, got):
            full[n] = _full_from_gathered(n, g)

    (ua, hn1, *qkv), got = _inproj(x, g_mix, full["w_in"][0], cos, sin, exchange=gather("w_gate"))
    arrived(("w_gate",), got)
    ya_n = _sgu_fwd(ua, sw, b2, gs, ga)
    half = shards["w_up"].shape[0] // 2
    riders = [[("gather", shards["w_up"][:half])], [("gather", shards["w_up"][half:])], gather("w_out")]
    branch, got = [], []
    for i, d in enumerate(DILATIONS):
        o_l, g = _attn_fwd(qkv[i], d, exchange=riders[i])
        branch.append(o_l)
        got += g
    arrived(("w_up", "w_out"), [jnp.concatenate(got[:2], axis=1), got[2]])
    (y, yb, *lse), _ = _combine([o for o, _ in branch], [l for _, l in branch], ya_n, gb)
    h1, _ = _mm_res(y, full["w_out"][0], x, "out_proj")
    (a, b, act, hn2), got = _ffn_up(h1, g_ffn, full["w_gate"][0], full["w_up"][0], exchange=gather("w_down"))
    arrived(("w_down",), got)
    h2, got = _mm_res(act, full["w_down"][0], h1, "ffn_down", exchange=gather("w_ple_gate", "w_ple_proj"))
    arrived(("w_ple_gate", "w_ple_proj"), got)
    h3, gate, pp, hn3 = _ple(h2, g_ple, full["w_ple_gate"][0], p, full["w_ple_proj"][0])

    dz, dpp, dh2, loss, d_fin, d_ple = _loss_ple_bwd(h3, target, g_fin, gate, pp, h2, g_ple, full["w_ple_gate"][1])
    g_ple_proj, _ = _mm_tn(p, dpp, "dw_ple_proj")
    g_ple_gate, _ = _mm_tn(hn3, dz, "dw_ple_gate")
    g_down, (parts["w_ple_gate"], parts["w_ple_proj"]) = _mm_tn(
        act, dh2, "dw_down", exchange=scatter(w_ple_gate=g_ple_gate, w_ple_proj=g_ple_proj))
    (da, db), (parts["w_down"],) = _ffn_down_bwd(dh2, full["w_down"][1], a, b, exchange=scatter(w_down=g_down))
    g_gate, _ = _mm_tn(hn2, da, "dw_gate")
    g_up, _ = _mm_tn(hn2, db, "dw_up")
    (dh1, d_ffn), (parts["w_gate"],) = _mm_norm_bwd(
        [(da, full["w_gate"][1]), (db, full["w_up"][1])], h1, g_ffn, dh2, "ffn_up_bwd", exchange=scatter(w_gate=g_gate))
    g_out, _ = _mm_tn(y, dh1, "dw_out")
    (dya_n, d_gb, *do_dd), (parts["w_out"],) = _outproj_bwd(dh1, full["w_out"][1], yb, gb, head_sum, exchange=scatter(w_out=g_out))
    grads_b = []
    for i, d in enumerate(DILATIONS):
        g3, got = _attn_bwd(qkv[i], do_dd[i], lse[i], do_dd[n_br + i], d, exchange=scatter(w_up=g_up) if i == 0 else ())
        grads_b.append(g3)
        if i == 0:
            (parts["w_up"],) = got
    dua, d_sw, d_b2, d_gs, d_ga = _sgu_bwd(ua, sw, b2, gs, ga, dya_n)
    dproj = _dproj(dua, [g[0] for g in grads_b], [g[1] for g in grads_b], [g[2] for g in grads_b], cos, sin)
    early = {
        "sgu_w": d_sw, "sgu_b": d_b2.reshape(CHUNK, 4, HEAD_DIM).sum(axis=-1).T, "sgu_norm_g": d_gs, "out_norm_a": d_ga,
        "out_norm_b": d_gb, "ffn_norm_g": d_ffn, "ple_norm_g": d_ple, "final_norm_g": d_fin,
    }
    g_in, (early_parts,) = _mm_tn(hn1, dproj, "dw_in", exchange=[("gather", _pack_small(early, _SMALL[1:]))])
    (dx, d_mix), (parts["w_in"],) = _mm_norm_bwd(
        [(dproj, full["w_in"][1])], x, g_mix, dh1, "inproj_bwd", exchange=scatter(w_in=g_in))
    (late_parts,) = _exchange_only([("gather", _pack_small({"mix_norm_g": d_mix}, _SMALL[:1]))], "gather_mix_norm_grad")
    return loss[0, 0], dx, parts, jnp.concatenate([late_parts, early_parts], axis=1)


def kernel(x, p, mix_norm_g, w_in, sgu_w, sgu_b, sgu_norm_g, out_norm_a, out_norm_b, w_out, ffn_norm_g, w_gate, w_up, w_down, ple_norm_g, w_ple_gate, w_ple_proj, final_norm_g, loss_target, m_mix_norm_g, m_w_in, m_sgu_w, m_sgu_b, m_sgu_norm_g, m_out_norm_a, m_out_norm_b, m_w_out, m_ffn_norm_g, m_w_gate, m_w_up, m_w_down, m_ple_norm_g, m_w_ple_gate, m_w_ple_proj, m_final_norm_g, v_mix_norm_g, v_w_in, v_sgu_w, v_sgu_b, v_sgu_norm_g, v_out_norm_a, v_out_norm_b, v_w_out, v_ffn_norm_g, v_w_gate, v_w_up, v_w_down, v_ple_norm_g, v_w_ple_gate, v_w_ple_proj, v_final_norm_g):
    given = dict(locals())
    weights = {n: given[n] for n in _ORDER}
    moments_m = {n: given["m_" + n] for n in _ORDER}
    moments_v = {n: given["v_" + n] for n in _ORDER}

    shards = {n: weights[n][0].astype(BF16) for n in _BIG}
    small = {n: (weights[n][0] if n in ("sgu_w", "sgu_b") else weights[n]) for n in _SMALL}

    loss, dx, parts, small_parts = _forward_backward(x[0], p[0, 0], loss_target[0], small, shards)
    loss = lax.psum(loss, ("x", "y", "c"))

    small_like = {n: weights[n] for n in _SMALL}
    grads, deltas, new_m, new_v = {}, {}, {}, {}
    for n in _BIG:
        g, d, nm, nv = _adamw(parts[n], weights[n][0], moments_m[n][0], moments_v[n][0], "adamw_" + n)
        grads[n], deltas[n], new_m[n], new_v[n] = g[None], d[None], nm[None], nv[None]
    g, d, nm, nv = _adamw(small_parts, _pack_small(small_like), _pack_small({n: moments_m[n] for n in _SMALL}),
                          _pack_small({n: moments_v[n] for n in _SMALL}), "adamw_small")
    for out, packed in ((grads, g), (deltas, d), (new_m, nm), (new_v, nv)):
        out.update(_unpack_small(packed, small_like))

    return (loss, dx[None], *[grads[n] for n in _ORDER], *[deltas[n] for n in _ORDER],
            *[new_m[n] for n in _ORDER], *[new_v[n] for n in _ORDER])
```

```python
import functools

import jax
import jax.numpy as jnp
from jax import lax
from jax.experimental import pallas as pl
from jax.experimental.pallas import tpu as pltpu

F32 = jnp.float32
BF16 = jnp.bfloat16

D_MODEL = 1024
WIDTH_A = 256
WIDTH_B = 768
D_FF = 2816
IN_COLS = 2 * WIDTH_A + 3 * WIDTH_B
PLE_DIM = 256
HEAD_DIM = 64
N_PAIRS = WIDTH_B // 128
CHUNK = 128
N_BACK = 128
DILATIONS = (1, 4, 16)
ROPE_THETA = 10000.0
EPS = 1e-6
N_DEV = 8

ADAM_LR = 0.001
ADAM_B1 = 0.9
ADAM_B2 = 0.999
ADAM_EPS = 1e-08
ADAM_WD = 0.01
ADAM_STEP = 10

V7X_VMEM_LIMIT_BYTES = 56 * 1024 * 1024
ROW_TILE = 512
MESH = pl.DeviceIdType.MESH
NEG = -1e30

_NT = (((1,), (1,)), ((), ()))
_TN = (((0,), (0,)), ((), ()))


def _params(*semantics):
    return pltpu.CompilerParams(dimension_semantics=semantics, vmem_limit_bytes=V7X_VMEM_LIMIT_BYTES)


def _rows(tm, width):
    return pl.BlockSpec((tm, width), lambda i: (i, 0))


def _whole(shape):
    return pl.BlockSpec(shape, lambda *_: (0,) * len(shape))


def _gelu(x):
    t = jnp.tanh(0.7978845608028654 * (x + 0.044715 * (x * x * x)))
    return 0.5 * x * (1.0 + t)


def _gelu_grad(x):
    t = jnp.tanh(0.7978845608028654 * (x + 0.044715 * (x * x * x)))
    return 0.5 * (1.0 + t) + 0.5 * x * (1.0 - t * t) * (0.7978845608028654 * (1.0 + 3.0 * 0.044715 * (x * x)))


def _rstd(x):
    return lax.rsqrt(jnp.mean(x * x, axis=-1, keepdims=True) + EPS)


def _norm_bwd(dn, h, g):
    r = _rstd(h)
    n = h * r
    t = dn * g
    return r * (t - n * jnp.mean(t * n, axis=-1, keepdims=True)), dn * n


def _swap_halves(x, first_half):
    return jnp.where(first_half, pltpu.roll(x, 96, 1), pltpu.roll(x, 32, 1))


def _sub_spec(d, n_cb, tm):
    return pl.BlockSpec((d, n_cb, tm // d, 128), lambda i: (0, 0, i, 0))


def _sub_shape(s_len, d, n_cb, dtype):
    return jax.ShapeDtypeStruct((d, n_cb, s_len // d, 128), dtype)


def _to_sub(stage_ref, cb_src, out_ref, cb_dst, d, tm):
    slab = stage_ref.at[cb_src]
    for r in range(d):
        out_ref[r, cb_dst] = slab[pl.ds(r, tm // d, stride=d), :].astype(out_ref.dtype)


def _from_sub(in_ref, cb_src, stage_ref, cb_dst, d, tm, accumulate=False):
    slab = stage_ref.at[cb_dst]
    for r in range(d):
        rows = pl.ds(r, tm // d, stride=d)
        val = in_ref[r, cb_src].astype(F32)
        slab[rows, :] = slab[rows, :] + val if accumulate else val


def _inproj(x, g, w, cos, sin, exchange=()):
    s_len = x.shape[0]
    tm = ROW_TILE
    n_cb = 3 * N_PAIRS

    def body(x_ref, g_ref, w_ref, cos_ref, sin_ref, ua_ref, hn_ref, *rest):
        sub_refs, stage = rest[:-1], rest[-1]
        xf = x_ref[...]
        hn = (xf * _rstd(xf) * g_ref[...]).astype(BF16)
        hn_ref[...] = hn
        c = cos_ref[...]
        s = sin_ref[...]
        first_half = (lax.broadcasted_iota(jnp.int32, (tm, 128), 1) % HEAD_DIM) < HEAD_DIM // 2
        for j in range(IN_COLS // 256):
            col = j * 256
            acc = lax.dot_general(hn, w_ref[col:col + 256, :], _NT, preferred_element_type=F32)
            if col < 2 * WIDTH_A:
                ua_ref[:, col:col + 256] = acc
                continue
            for half in range(2):
                cb = (col - 2 * WIDTH_A) // 128 + half
                t = acc[:, half * 128:(half + 1) * 128]
                if cb < 2 * N_PAIRS:
                    t = (t * c + _swap_halves(t, first_half) * s) * (0.125 if cb < N_PAIRS else 1.0)
                stage[cb] = t
        for cb in range(n_cb):
            for d, out_ref in zip(DILATIONS, sub_refs):
                _to_sub(stage, cb, out_ref, cb, d, tm)

    return _call(
        body, name="inproj", grid=(s_len // tm,),
        in_specs=[_rows(tm, D_MODEL), _whole((1, D_MODEL)), _whole((IN_COLS, D_MODEL)), _rows(tm, 128), _rows(tm, 128)],
        out_specs=[_rows(tm, 2 * WIDTH_A), _rows(tm, D_MODEL)] + [_sub_spec(d, n_cb, tm) for d in DILATIONS],
        out_shape=[jax.ShapeDtypeStruct((s_len, 2 * WIDTH_A), F32), jax.ShapeDtypeStruct((s_len, D_MODEL), BF16)]
        + [_sub_shape(s_len, d, n_cb, BF16) for d in DILATIONS],
        scratch_shapes=[pltpu.VMEM((n_cb, tm, 128), F32)],
        semantics=("parallel",), args=(x, g, w, cos, sin), exchange=exchange)


def _sgu_mix_weights(sw_ref):
    keep = lax.broadcasted_iota(jnp.int32, (CHUNK, CHUNK), 0) >= lax.broadcasted_iota(jnp.int32, (CHUNK, CHUNK), 1)
    return [jnp.where(keep, sw_ref[h], 0.0).astype(BF16) for h in range(4)], keep


def _sgu_core(ua_ref, gs_ref):
    u = ua_ref[:, :WIDTH_A]
    va = ua_ref[:, WIDTH_A:]
    vg = _gelu(va)
    xc = vg - jnp.mean(vg, axis=-1, keepdims=True)
    rstd = lax.rsqrt(jnp.mean(xc * xc, axis=-1, keepdims=True) + EPS)
    xhat = xc * rstd
    return u, va, _gelu(u), xhat, rstd, xhat * gs_ref[...]


def _sgu_fwd(ua, sw, b2, gs, ga):
    s_len = ua.shape[0]
    tm = ROW_TILE

    def body(ua_ref, sw_ref, b2_ref, gs_ref, ga_ref, out_ref):
        _, _, ug, _, _, vn = _sgu_core(ua_ref, gs_ref)
        wm, _ = _sgu_mix_weights(sw_ref)
        head = lax.broadcasted_iota(jnp.int32, (CHUNK, WIDTH_A), 1) // HEAD_DIM
        for c in range(tm // CHUNK):
            rows = slice(c * CHUNK, (c + 1) * CHUNK)
            vnc = vn[rows]
            mixed = b2_ref[...]
            for h in range(4):
                mixed = mixed + jnp.dot(wm[h], jnp.where(head == h, vnc, 0.0).astype(BF16), preferred_element_type=F32)
            ya = ug[rows] * mixed
            out_ref[rows, :] = (ya * _rstd(ya) * ga_ref[...]).astype(BF16)

    return pl.pallas_call(
        body, name="sgu_fwd", grid=(s_len // tm,),
        in_specs=[_rows(tm, 2 * WIDTH_A), _whole((4, CHUNK, CHUNK)), _whole((CHUNK, WIDTH_A)), _whole((1, WIDTH_A)), _whole((1, WIDTH_A))],
        out_specs=_rows(tm, WIDTH_A),
        out_shape=jax.ShapeDtypeStruct((s_len, WIDTH_A), BF16),
        compiler_params=_params("parallel"),
    )(ua, sw, b2, gs, ga)


def _attn_geometry(sd):
    tile = min(ROW_TILE, sd)
    return tile, tile // CHUNK, sd // tile


PAIRS_PER_STEP = 6


def _attn_spec(cb0, rows, row_index):
    return pl.BlockSpec((None, PAIRS_PER_STEP, rows, 128), lambda r, g, n: (r, cb0 // PAIRS_PER_STEP + g, row_index(n), 0))


def _per_pair(one_pair):
    def body(*refs):
        for hp in range(PAIRS_PER_STEP):
            one_pair(*[ref.at[hp] for ref in refs])
    return body


def _both_heads(x, head_a):
    zero = jnp.zeros_like(x)
    return [jnp.where(head_a, x, zero), jnp.where(head_a, zero, x)]


def _attn_fwd(qkv, d, exchange=()):
    sd = qkv.shape[2]
    tile, nb, n_tiles = _attn_geometry(sd)

    def prev(n):
        return jnp.maximum(n * nb - 1, 0)

    def one_pair(q_ref, k_ref, kp_ref, v_ref, vp_ref, o_ref, l_ref):
        n = pl.program_id(2)
        head_a = lax.broadcasted_iota(jnp.int32, (CHUNK, 128), 1) < HEAD_DIM
        qi = lax.broadcasted_iota(jnp.int32, (2 * CHUNK, 2 * CHUNK), 0) % CHUNK
        kc = lax.broadcasted_iota(jnp.int32, (2 * CHUNK, 2 * CHUNK), 1)
        band = (kc >= qi) & (kc <= qi + N_BACK)
        for j in range(nb):
            rows = slice(j * CHUNK, (j + 1) * CHUNK)
            if j == 0:
                kcat = jnp.concatenate([kp_ref[...], k_ref[rows, :]], axis=0)
                vcat = jnp.concatenate([vp_ref[...], v_ref[rows, :]], axis=0)
                valid = band & jnp.logical_or(n > 0, kc >= CHUNK)
            else:
                kcat = k_ref[(j - 1) * CHUNK:(j + 1) * CHUNK, :]
                vcat = v_ref[(j - 1) * CHUNK:(j + 1) * CHUNK, :]
                valid = band
            q2 = jnp.concatenate(_both_heads(q_ref[rows, :], head_a), axis=0)
            s = lax.dot_general(q2, kcat, _NT, preferred_element_type=F32)
            s = jnp.where(valid, s, NEG)
            m = jnp.max(s, axis=-1, keepdims=True)
            p = jnp.exp(s - m)
            l = jnp.sum(p, axis=-1, keepdims=True)
            o2 = jnp.dot(p.astype(BF16), vcat, preferred_element_type=F32) / l
            lse2 = m + jnp.log(l)
            o_ref[rows, :] = jnp.where(head_a, o2[:CHUNK], o2[CHUNK:]).astype(BF16)
            l_ref[rows, :] = jnp.where(head_a, lse2[:CHUNK], lse2[CHUNK:])

    same = lambda n: n
    return _call(
        _per_pair(one_pair), name=f"attn_fwd_d{d}", grid=(d, N_PAIRS // PAIRS_PER_STEP, n_tiles),
        in_specs=[_attn_spec(0, tile, same), _attn_spec(N_PAIRS, tile, same), _attn_spec(N_PAIRS, CHUNK, prev),
                  _attn_spec(2 * N_PAIRS, tile, same), _attn_spec(2 * N_PAIRS, CHUNK, prev)],
        out_specs=[_attn_spec(0, tile, same), _attn_spec(0, tile, same)],
        out_shape=[jax.ShapeDtypeStruct((d, N_PAIRS, sd, 128), BF16), jax.ShapeDtypeStruct((d, N_PAIRS, sd, 128), F32)],
        semantics=("parallel", "parallel", "parallel"), args=(qkv, qkv, qkv, qkv, qkv), exchange=exchange)


def _combine(outs, lses, ya_n, gb, exchange=()):
    s_len = ya_n.shape[0]
    tm = ROW_TILE
    n_br = len(DILATIONS)

    def body(*refs):
        o_refs, l_refs = refs[:n_br], refs[n_br:2 * n_br]
        ya_ref, gb_ref, y_ref, yb_ref = refs[2 * n_br:2 * n_br + 4]
        lse_refs = refs[2 * n_br + 4:3 * n_br + 4]
        o_nat, l_nat, lse_nat = refs[3 * n_br + 4:]
        sumsq = jnp.zeros((tm, 1), F32)
        for cb in range(N_PAIRS):
            for i, d in enumerate(DILATIONS):
                _from_sub(o_refs[i], cb, o_nat, i, d, tm)
                _from_sub(l_refs[i], cb, l_nat, i, d, tm)
            ls = [l_nat[i] for i in range(n_br)]
            top = jnp.maximum(jnp.maximum(ls[0], ls[1]), ls[2])
            ws = [jnp.exp(l - top) for l in ls]
            den = ws[0] + ws[1] + ws[2]
            inv = 1.0 / den
            yb = (ws[0] * inv) * o_nat[0] + (ws[1] * inv) * o_nat[1] + (ws[2] * inv) * o_nat[2]
            yb_ref[:, cb * 128:(cb + 1) * 128] = yb
            sumsq = sumsq + jnp.sum(yb * yb, axis=-1, keepdims=True)
            lse_nat[cb] = top + jnp.log(den)
            for d, lse_ref in zip(DILATIONS, lse_refs):
                _to_sub(lse_nat, cb, lse_ref, cb, d, tm)
        r = lax.rsqrt(sumsq / WIDTH_B + EPS)
        y_ref[:, :WIDTH_A] = ya_ref[...]
        y_ref[:, WIDTH_A:] = (yb_ref[...] * r * gb_ref[...]).astype(BF16)

    return _call(
        body, name="attn_combine", grid=(s_len // tm,),
        in_specs=[_sub_spec(d, N_PAIRS, tm) for d in DILATIONS] * 2 + [_rows(tm, WIDTH_A), _whole((1, WIDTH_B))],
        out_specs=[_rows(tm, D_MODEL), _rows(tm, WIDTH_B)] + [_sub_spec(d, N_PAIRS, tm) for d in DILATIONS],
        out_shape=[jax.ShapeDtypeStruct((s_len, D_MODEL), BF16), jax.ShapeDtypeStruct((s_len, WIDTH_B), F32)]
        + [_sub_shape(s_len, d, N_PAIRS, F32) for d in DILATIONS],
        scratch_shapes=[pltpu.VMEM((n_br, tm, 128), F32), pltpu.VMEM((n_br, tm, 128), F32), pltpu.VMEM((N_PAIRS, tm, 128), F32)],
        semantics=("parallel",), args=(*outs, *lses, ya_n, gb), exchange=exchange)


def _mm_res(a, w, res, name, exchange=()):
    s_len, k = a.shape
    tm = ROW_TILE if k <= D_MODEL else ROW_TILE // 2

    def body(a_ref, w_ref, r_ref, o_ref):
        o_ref[...] = r_ref[...] + jnp.dot(a_ref[...].astype(BF16), w_ref[...], preferred_element_type=F32)

    (out,), received = _call(
        body, name=name, grid=(s_len // tm,),
        in_specs=[_rows(tm, k), _whole((k, D_MODEL)), _rows(tm, D_MODEL)],
        out_specs=[_rows(tm, D_MODEL)],
        out_shape=[jax.ShapeDtypeStruct((s_len, D_MODEL), F32)],
        semantics=("parallel",), args=(a, w, res), exchange=exchange)
    return out, received


def _ffn_up(h, g, wg, wu, exchange=()):
    s_len = h.shape[0]
    tm = ROW_TILE // 2

    def body(h_ref, g_ref, wg_ref, wu_ref, a_ref, b_ref, act_ref, hn_ref):
        hf = h_ref[...]
        hn = (hf * _rstd(hf) * g_ref[...]).astype(BF16)
        hn_ref[...] = hn
        for j in range(D_FF // 256):
            cols = slice(j * 256, (j + 1) * 256)
            a = lax.dot_general(hn, wg_ref[cols, :], _NT, preferred_element_type=F32)
            b = lax.dot_general(hn, wu_ref[cols, :], _NT, preferred_element_type=F32)
            a_ref[:, cols] = a.astype(BF16)
            b_ref[:, cols] = b.astype(BF16)
            act_ref[:, cols] = (a * jax.nn.sigmoid(a) * b).astype(BF16)

    wide = jax.ShapeDtypeStruct((s_len, D_FF), BF16)
    return _call(
        body, name="ffn_up", grid=(s_len // tm,),
        in_specs=[_rows(tm, D_MODEL), _whole((1, D_MODEL)), _whole((D_FF, D_MODEL)), _whole((D_FF, D_MODEL))],
        out_specs=[_rows(tm, D_FF), _rows(tm, D_FF), _rows(tm, D_FF), _rows(tm, D_MODEL)],
        out_shape=[wide, wide, wide, jax.ShapeDtypeStruct((s_len, D_MODEL), BF16)],
        semantics=("parallel",), args=(h, g, wg, wu), exchange=exchange)


def _ple(h, g, wpg, p, wpp):
    s_len = h.shape[0]
    tm = ROW_TILE

    def body(h_ref, g_ref, wpg_ref, p_ref, wpp_ref, h3_ref, gate_ref, pp_ref, hn_ref):
        hf = h_ref[...]
        hn = (hf * _rstd(hf) * g_ref[...]).astype(BF16)
        hn_ref[...] = hn
        gate = jax.nn.sigmoid(jnp.dot(hn, wpg_ref[...], preferred_element_type=F32))
        pp = lax.dot_general(p_ref[...].astype(BF16), wpp_ref[...], _NT, preferred_element_type=F32)
        h3_ref[...] = hf + gate * pp
        gate_ref[...] = gate.astype(BF16)
        pp_ref[...] = pp.astype(BF16)

    half = jax.ShapeDtypeStruct((s_len, D_MODEL), BF16)
    return pl.pallas_call(
        body, name="ple", grid=(s_len // tm,),
        in_specs=[_rows(tm, D_MODEL), _whole((1, D_MODEL)), _whole((D_MODEL, D_MODEL)), _rows(tm, PLE_DIM), _whole((D_MODEL, PLE_DIM))],
        out_specs=[_rows(tm, D_MODEL)] * 4,
        out_shape=[jax.ShapeDtypeStruct((s_len, D_MODEL), F32), half, half, half],
        compiler_params=_params("parallel"),
    )(h, g, wpg, p, wpp)


def _loss_ple_bwd(h3, target, gf, gate, pp, h2, g_ple, wpgt):
    s_len = h3.shape[0]
    tm = ROW_TILE

    def body(h_ref, t_ref, g_ref, gate_ref, pp_ref, h2_ref, gp_ref, w_ref,
             dz_ref, dpp_ref, dh2_ref, loss_ref, dg_ref, dgp_ref):
        @pl.when(pl.program_id(0) == 0)
        def _():
            loss_ref[...] = jnp.zeros_like(loss_ref)
            dg_ref[...] = jnp.zeros_like(dg_ref)
            dgp_ref[...] = jnp.zeros_like(dgp_ref)

        hf = h_ref[...]
        gfv = g_ref[...]
        err = hf * _rstd(hf) * gfv - t_ref[...]
        loss_ref[...] += 0.5 * jnp.sum(jnp.sum(err * err, axis=-1, keepdims=True), axis=0, keepdims=True) / D_MODEL
        dh, dg_rows = _norm_bwd(err / D_MODEL, hf, gfv)
        dg_ref[...] += jnp.sum(dg_rows, axis=0, keepdims=True)
        gate = gate_ref[...].astype(F32)
        dz = (dh * pp_ref[...].astype(F32) * gate * (1.0 - gate)).astype(BF16)
        dz_ref[...] = dz
        dpp_ref[...] = (dh * gate).astype(BF16)
        dn = lax.dot_general(dz, w_ref[...], _NT, preferred_element_type=F32)
        dh2, dgp_rows = _norm_bwd(dn, h2_ref[...], gp_ref[...])
        dh2_ref[...] = dh + dh2
        dgp_ref[...] += jnp.sum(dgp_rows, axis=0, keepdims=True)

    half = jax.ShapeDtypeStruct((s_len, D_MODEL), BF16)
    gain = jax.ShapeDtypeStruct((1, D_MODEL), F32)
    return pl.pallas_call(
        body, name="loss_ple_bwd", grid=(s_len // tm,),
        in_specs=[_rows(tm, D_MODEL), _rows(tm, D_MODEL), _whole((1, D_MODEL)), _rows(tm, D_MODEL), _rows(tm, D_MODEL),
                  _rows(tm, D_MODEL), _whole((1, D_MODEL)), _whole((D_MODEL, D_MODEL))],
        out_specs=[_rows(tm, D_MODEL), _rows(tm, D_MODEL), _rows(tm, D_MODEL), _whole((1, 128)), _whole((1, D_MODEL)),
                   _whole((1, D_MODEL))],
        out_shape=[half, half, jax.ShapeDtypeStruct((s_len, D_MODEL), F32), jax.ShapeDtypeStruct((1, 128), F32), gain, gain],
        compiler_params=_params("arbitrary"),
    )(h3, target, gf, gate, pp, h2, g_ple, wpgt)


def _mm_norm_bwd(parts, h, g, dres, name, exchange=()):
    s_len = h.shape[0]
    tm = ROW_TILE // 2
    n_parts = len(parts)

    def body(*refs):
        a_refs = refs[0:2 * n_parts:2]
        w_refs = refs[1:2 * n_parts:2]
        h_ref, g_ref, r_ref, o_ref, dg_ref = refs[2 * n_parts:]

        @pl.when(pl.program_id(0) == 0)
        def _():
            dg_ref[...] = jnp.zeros_like(dg_ref)

        dn = jnp.dot(a_refs[0][...], w_refs[0][...], preferred_element_type=F32)
        for a_ref, w_ref in zip(a_refs[1:], w_refs[1:]):
            dn = dn + jnp.dot(a_ref[...], w_ref[...], preferred_element_type=F32)
        dh, dg_rows = _norm_bwd(dn, h_ref[...], g_ref[...])
        o_ref[...] = r_ref[...] + dh
        dg_ref[...] += jnp.sum(dg_rows, axis=0, keepdims=True)

    in_specs, args = [], []
    for a, w in parts:
        in_specs += [_rows(tm, a.shape[1]), _whole(w.shape)]
        args += [a, w]
    return _call(
        body, name=name, grid=(s_len // tm,),
        in_specs=in_specs + [_rows(tm, D_MODEL), _whole((1, D_MODEL)), _rows(tm, D_MODEL)],
        out_specs=[_rows(tm, D_MODEL), _whole((1, D_MODEL))],
        out_shape=[jax.ShapeDtypeStruct((s_len, D_MODEL), F32), jax.ShapeDtypeStruct((1, D_MODEL), F32)],
        semantics=("arbitrary",), args=(*args, h, g, dres), exchange=exchange)


def _ffn_down_bwd(dh, wdt, a, b, exchange=()):
    s_len = dh.shape[0]
    tm = ROW_TILE // 2

    def body(dh_ref, w_ref, a_ref, b_ref, da_ref, db_ref):
        dhb = dh_ref[...].astype(BF16)
        for j in range(D_FF // 256):
            cols = slice(j * 256, (j + 1) * 256)
            dact = lax.dot_general(dhb, w_ref[cols, :], _NT, preferred_element_type=F32)
            av = a_ref[:, cols].astype(F32)
            bv = b_ref[:, cols].astype(F32)
            sig = jax.nn.sigmoid(av)
            da_ref[:, cols] = (dact * bv * sig * (1.0 + av * (1.0 - sig))).astype(BF16)
            db_ref[:, cols] = (dact * av * sig).astype(BF16)

    wide = jax.ShapeDtypeStruct((s_len, D_FF), BF16)
    return _call(
        body, name="ffn_down_bwd", grid=(s_len // tm,),
        in_specs=[_rows(tm, D_MODEL), _whole((D_FF, D_MODEL)), _rows(tm, D_FF), _rows(tm, D_FF)],
        out_specs=[_rows(tm, D_FF), _rows(tm, D_FF)],
        out_shape=[wide, wide],
        semantics=("parallel",), args=(dh, wdt, a, b), exchange=exchange)


def _outproj_bwd(dh1, woutt, yb, gb, head_sum, exchange=()):
    s_len = dh1.shape[0]
    tm = ROW_TILE
    n_br = len(DILATIONS)

    def body(dh_ref, w_ref, yb_ref, gb_ref, e_ref, dya_ref, dgb_ref, *rest):
        do_refs, dd_refs = rest[:n_br], rest[n_br:2 * n_br]
        do_nat, dd_nat = rest[2 * n_br:]

        @pl.when(pl.program_id(0) == 0)
        def _():
            dgb_ref[...] = jnp.zeros_like(dgb_ref)

        dhb = dh_ref[...].astype(BF16)
        dya_ref[...] = lax.dot_general(dhb, w_ref[:WIDTH_A, :], _NT, preferred_element_type=F32)
        dyn = lax.dot_general(dhb, w_ref[WIDTH_A:, :], _NT, preferred_element_type=F32)
        ybv = yb_ref[...]
        dyb, dg_rows = _norm_bwd(dyn, ybv, gb_ref[...])
        dgb_ref[...] += jnp.sum(dg_rows, axis=0, keepdims=True)
        prod = dyb * ybv
        for cb in range(N_PAIRS):
            cols = slice(cb * 128, (cb + 1) * 128)
            pc = prod[:, cols]
            hi = pc.astype(BF16)
            lo = (pc - hi.astype(F32)).astype(BF16)
            do_nat[cb] = dyb[:, cols]
            dd_nat[cb] = (jnp.dot(hi, e_ref[...], preferred_element_type=F32)
                          + jnp.dot(lo, e_ref[...], preferred_element_type=F32))
            for i, d in enumerate(DILATIONS):
                _to_sub(do_nat, cb, do_refs[i], cb, d, tm)
                _to_sub(dd_nat, cb, dd_refs[i], cb, d, tm)

    subs = [_sub_spec(d, N_PAIRS, tm) for d in DILATIONS]
    return _call(
        body, name="outproj_bwd", grid=(s_len // tm,),
        in_specs=[_rows(tm, D_MODEL), _whole((D_MODEL, D_MODEL)), _rows(tm, WIDTH_B), _whole((1, WIDTH_B)), _whole((128, 128))],
        out_specs=[_rows(tm, WIDTH_A), _whole((1, WIDTH_B))] + subs + subs,
        out_shape=[jax.ShapeDtypeStruct((s_len, WIDTH_A), F32), jax.ShapeDtypeStruct((1, WIDTH_B), F32)]
        + [_sub_shape(s_len, d, N_PAIRS, BF16) for d in DILATIONS] + [_sub_shape(s_len, d, N_PAIRS, F32) for d in DILATIONS],
        scratch_shapes=[pltpu.VMEM((N_PAIRS, tm, 128), F32), pltpu.VMEM((N_PAIRS, tm, 128), F32)],
        semantics=("arbitrary",), args=(dh1, woutt, yb, gb, head_sum), exchange=exchange)


def _attn_bwd(qkv, do, lse, dd, d, exchange=()):
    sd = qkv.shape[2]
    tile, nb, n_tiles = _attn_geometry(sd)
    last_block = sd // CHUNK - 1

    def nxt(n):
        return jnp.minimum((n + 1) * nb, last_block)

    def one_pair(q_ref, qn_ref, k_ref, v_ref, do_ref, don_ref, l_ref, ln_ref, dd_ref, ddn_ref,
                 dq_ref, dk_ref, dv_ref, carry_ref):
        n = pl.program_id(2)

        @pl.when(n == 0)
        def _():
            carry_ref[...] = jnp.zeros_like(carry_ref)

        head_a = lax.broadcasted_iota(jnp.int32, (CHUNK, 128), 1) < HEAD_DIM
        row = lax.broadcasted_iota(jnp.int32, (4 * CHUNK, CHUNK), 0)
        qi = row % CHUNK
        ki = lax.broadcasted_iota(jnp.int32, (4 * CHUNK, CHUNK), 1)
        is_after = row >= 2 * CHUNK
        mask = (is_after & (ki >= qi)) | (jnp.logical_not(is_after) & (qi >= ki))
        mask_last = mask & jnp.logical_or(jnp.logical_not(is_after), n < n_tiles - 1)
        dq_acc = [carry_ref[...]] + [jnp.zeros((CHUNK, 128), F32) for _ in range(nb)]

        def per_head(x):
            other = pltpu.roll(x, HEAD_DIM, 1)
            return [jnp.where(head_a, x, other), jnp.where(head_a, other, x)]

        for j in range(nb):
            rows = slice(j * CHUNK, (j + 1) * CHUNK)
            kj = k_ref[rows, :]
            vj = v_ref[rows, :]
            if j + 1 < nb:
                nrows = slice((j + 1) * CHUNK, (j + 2) * CHUNK)
                q2, do2, l2, dd2, msk = q_ref[nrows, :], do_ref[nrows, :], l_ref[nrows, :], dd_ref[nrows, :], mask
            else:
                q2, do2, l2, dd2, msk = qn_ref[...], don_ref[...], ln_ref[...], ddn_ref[...], mask_last
            qs = jnp.concatenate(_both_heads(q_ref[rows, :], head_a) + _both_heads(q2, head_a), axis=0)
            dos = jnp.concatenate(_both_heads(do_ref[rows, :], head_a) + _both_heads(do2, head_a), axis=0)
            ls = jnp.concatenate(per_head(l_ref[rows, :]) + per_head(l2), axis=0)
            dds = jnp.concatenate(per_head(dd_ref[rows, :]) + per_head(dd2), axis=0)
            s = lax.dot_general(qs, kj, _NT, preferred_element_type=F32)
            p = jnp.exp(jnp.where(msk, s - ls, NEG))
            dp = lax.dot_general(dos, vj, _NT, preferred_element_type=F32)
            ds = (p * (dp - dds)).astype(BF16)
            dv_ref[rows, :] = lax.dot_general(p.astype(BF16), dos, _TN, preferred_element_type=F32).astype(BF16)
            dk_ref[rows, :] = lax.dot_general(ds, qs, _TN, preferred_element_type=F32).astype(BF16)
            dqs = jnp.dot(ds, kj, preferred_element_type=F32)
            dq_acc[j] = dq_acc[j] + jnp.where(head_a, dqs[:CHUNK], dqs[CHUNK:2 * CHUNK])
            dq_acc[j + 1] = dq_acc[j + 1] + jnp.where(head_a, dqs[2 * CHUNK:3 * CHUNK], dqs[3 * CHUNK:])
        for j in range(nb):
            dq_ref[j * CHUNK:(j + 1) * CHUNK, :] = dq_acc[j].astype(BF16)
        carry_ref[...] = dq_acc[nb]

    same = lambda n: n
    grad = jax.ShapeDtypeStruct((d, N_PAIRS, sd, 128), BF16)
    return _call(
        _per_pair(one_pair), name=f"attn_bwd_d{d}", grid=(d, N_PAIRS // PAIRS_PER_STEP, n_tiles),
        in_specs=[_attn_spec(0, tile, same), _attn_spec(0, CHUNK, nxt), _attn_spec(N_PAIRS, tile, same),
                  _attn_spec(2 * N_PAIRS, tile, same), _attn_spec(0, tile, same), _attn_spec(0, CHUNK, nxt),
                  _attn_spec(0, tile, same), _attn_spec(0, CHUNK, nxt), _attn_spec(0, tile, same), _attn_spec(0, CHUNK, nxt)],
        out_specs=[_attn_spec(0, tile, same)] * 3,
        out_shape=[grad, grad, grad],
        scratch_shapes=[pltpu.VMEM((PAIRS_PER_STEP, CHUNK, 128), F32)],
        semantics=("parallel", "parallel", "arbitrary"), args=(qkv, qkv, qkv, qkv, do, do, lse, lse, dd, dd), exchange=exchange)


def _sgu_bwd(ua, sw, b2, gs, ga, dya_n):
    s_len = ua.shape[0]
    tm = ROW_TILE

    def body(ua_ref, sw_ref, b2_ref, gs_ref, ga_ref, dy_ref, dua_ref, dsw_ref, db2_ref, dgs_ref, dga_ref):
        @pl.when(pl.program_id(0) == 0)
        def _():
            dsw_ref[...] = jnp.zeros_like(dsw_ref)
            db2_ref[...] = jnp.zeros_like(db2_ref)
            dgs_ref[...] = jnp.zeros_like(dgs_ref)
            dga_ref[...] = jnp.zeros_like(dga_ref)

        u, va, ug, xhat, rstd, vn = _sgu_core(ua_ref, gs_ref)
        wm, keep = _sgu_mix_weights(sw_ref)
        head = lax.broadcasted_iota(jnp.int32, (CHUNK, WIDTH_A), 1) // HEAD_DIM
        gav = ga_ref[...]
        gsv = gs_ref[...]
        dga = jnp.zeros((1, WIDTH_A), F32)
        dgs = jnp.zeros((1, WIDTH_A), F32)
        db2 = jnp.zeros((CHUNK, WIDTH_A), F32)
        dsw = [jnp.zeros((CHUNK, CHUNK), F32) for _ in range(4)]
        for c in range(tm // CHUNK):
            rows = slice(c * CHUNK, (c + 1) * CHUNK)
            vnc = vn[rows]
            vnb = vnc.astype(BF16)
            mixed = b2_ref[...]
            for h in range(4):
                mixed = mixed + jnp.dot(wm[h], jnp.where(head == h, vnc, 0.0).astype(BF16), preferred_element_type=F32)
            ugc = ug[rows]
            dya, dga_rows = _norm_bwd(dy_ref[rows, :], ugc * mixed, gav)
            dga = dga + jnp.sum(dga_rows, axis=0, keepdims=True)
            dmixed = dya * ugc
            db2 = db2 + dmixed
            dvn = jnp.zeros((CHUNK, WIDTH_A), F32)
            for h in range(4):
                dmh = jnp.where(head == h, dmixed, 0.0).astype(BF16)
                dsw[h] = dsw[h] + lax.dot_general(dmh, vnb, _NT, preferred_element_type=F32)
                dvn = dvn + lax.dot_general(wm[h], dmh, _TN, preferred_element_type=F32)
            xh = xhat[rows]
            dgs = dgs + jnp.sum(dvn * xh, axis=0, keepdims=True)
            dxh = dvn * gsv
            dvg = rstd[rows] * (dxh - jnp.mean(dxh, axis=-1, keepdims=True) - xh * jnp.mean(dxh * xh, axis=-1, keepdims=True))
            dua_ref[rows, :WIDTH_A] = (dya * mixed * _gelu_grad(u[rows])).astype(BF16)
            dua_ref[rows, WIDTH_A:] = (dvg * _gelu_grad(va[rows])).astype(BF16)
        for h in range(4):
            dsw_ref[h] += jnp.where(keep, dsw[h], 0.0)
        db2_ref[...] += db2
        dgs_ref[...] += dgs
        dga_ref[...] += dga

    return pl.pallas_call(
        body, name="sgu_bwd", grid=(s_len // tm,),
        in_specs=[_rows(tm, 2 * WIDTH_A), _whole((4, CHUNK, CHUNK)), _whole((CHUNK, WIDTH_A)), _whole((1, WIDTH_A)),
                  _whole((1, WIDTH_A)), _rows(tm, WIDTH_A)],
        out_specs=[_rows(tm, 2 * WIDTH_A), _whole((4, CHUNK, CHUNK)), _whole((CHUNK, WIDTH_A)), _whole((1, WIDTH_A)), _whole((1, WIDTH_A))],
        out_shape=[jax.ShapeDtypeStruct((s_len, 2 * WIDTH_A), BF16), jax.ShapeDtypeStruct((4, CHUNK, CHUNK), F32),
                   jax.ShapeDtypeStruct((CHUNK, WIDTH_A), F32), jax.ShapeDtypeStruct((1, WIDTH_A), F32),
                   jax.ShapeDtypeStruct((1, WIDTH_A), F32)],
        compiler_params=_params("arbitrary"),
    )(ua, sw, b2, gs, ga, dya_n)


def _dproj(dua, dqs, dks, dvs, cos, sin):
    s_len = dua.shape[0]
    tm = ROW_TILE
    n_br = len(DILATIONS)

    def body(dua_ref, *rest):
        groups = [rest[g * n_br:(g + 1) * n_br] for g in range(3)]
        cos_ref, sin_ref, out_ref, acc = rest[3 * n_br:]
        out_ref[:, :2 * WIDTH_A] = dua_ref[...]
        c = cos_ref[...]
        s = sin_ref[...]
        first_half = (lax.broadcasted_iota(jnp.int32, (tm, 128), 1) % HEAD_DIM) < HEAD_DIM // 2
        for g, refs in enumerate(groups):
            for cb in range(N_PAIRS):
                for i, d in enumerate(DILATIONS):
                    _from_sub(refs[i], cb, acc, 0, d, tm, accumulate=i > 0)
                t = acc[0]
                if g < 2:
                    t = (t * c - _swap_halves(t, first_half) * s) * (0.125 if g == 0 else 1.0)
                col = 2 * WIDTH_A + g * WIDTH_B + cb * 128
                out_ref[:, col:col + 128] = t.astype(BF16)

    subs = [_sub_spec(d, N_PAIRS, tm) for d in DILATIONS]
    return pl.pallas_call(
        body, name="dproj", grid=(s_len // tm,),
        in_specs=[_rows(tm, 2 * WIDTH_A)] + subs * 3 + [_rows(tm, 128), _rows(tm, 128)],
        out_specs=_rows(tm, IN_COLS),
        out_shape=jax.ShapeDtypeStruct((s_len, IN_COLS), BF16),
        scratch_shapes=[pltpu.VMEM((1, tm, 128), F32)],
        compiler_params=_params("parallel"),
    )(dua, *dqs, *dks, *dvs, cos, sin)


def _mm_tn(a, b, name, exchange=()):
    s_len, m = a.shape
    n = b.shape[1]
    tk = ROW_TILE
    tm = m if m <= 512 else (1408 if m == D_FF else 512)
    n_k = s_len // tk

    def body(a_ref, b_ref, o_ref, acc_ref):
        k = pl.program_id(1)

        @pl.when(k == 0)
        def _():
            acc_ref[...] = jnp.zeros_like(acc_ref)

        acc_ref[...] += lax.dot_general(a_ref[...].astype(BF16), b_ref[...].astype(BF16), _TN, preferred_element_type=F32)

        @pl.when(k == n_k - 1)
        def _():
            o_ref[...] = acc_ref[...].astype(BF16)

    (grad,), received = _call(
        body, name=name, grid=(m // tm, n_k),
        in_specs=[pl.BlockSpec((tk, tm), lambda i, k: (k, i)), pl.BlockSpec((tk, n), lambda i, k: (k, 0))],
        out_specs=[pl.BlockSpec((tm, n), lambda i, k: (i, 0))],
        out_shape=[jax.ShapeDtypeStruct((m, n), BF16)],
        scratch_shapes=[pltpu.VMEM((tm, n), F32)],
        semantics=("parallel", "arbitrary"), args=(a, b), exchange=exchange)
    return grad, received


def _position():
    x, y, c = lax.axis_index("x"), lax.axis_index("y"), lax.axis_index("c")
    return x, y, c, 4 * x + 2 * y + c


def _peer(x, y, c, rel):
    return (x ^ ((rel >> 2) & 1), y ^ ((rel >> 1) & 1), c ^ (rel & 1))


def _exchange_out_shape(kind, arr):
    return jax.ShapeDtypeStruct(((N_DEV,) + arr.shape) if kind == "gather" else arr.shape, arr.dtype)


def _exchange_sems(n_items):
    return [pltpu.SemaphoreType.DMA((n_items, N_DEV)), pltpu.SemaphoreType.DMA((n_items, N_DEV)), pltpu.SemaphoreType.DMA((n_items,))]


def _exchange_copies(kinds, srcs, dsts, sems):
    send_sems, recv_sems, local_sems = sems
    x, y, c, me = _position()
    local, sends, recvs = [], [], []
    for k, (kind, src, dst) in enumerate(zip(kinds, srcs, dsts)):
        own = src if kind == "gather" else src.at[me]
        local.append(pltpu.make_async_copy(own, dst.at[me], local_sems.at[k]))
        for rel in range(1, N_DEV):
            going = src if kind == "gather" else src.at[me ^ rel]
            common = dict(send_sem=send_sems.at[k, rel], recv_sem=recv_sems.at[k, rel],
                          device_id=_peer(x, y, c, rel), device_id_type=MESH)
            sends.append(pltpu.make_async_remote_copy(src_ref=going, dst_ref=dst.at[me], **common))
            recvs.append(pltpu.make_async_remote_copy(src_ref=own, dst_ref=dst.at[me ^ rel], **common))
    return local, sends, recvs


def _exchange_start(kinds, srcs, dsts, sems):
    local, sends, _ = _exchange_copies(kinds, srcs, dsts, sems)
    for cp in local + sends:
        cp.start()


def _exchange_finish(kinds, srcs, dsts, sems):
    local, sends, recvs = _exchange_copies(kinds, srcs, dsts, sems)
    for cp in recvs:
        cp.wait_recv()
    for cp in sends:
        cp.wait_send()
    for cp in local:
        cp.wait()


def _exchange_only(items, name):
    kinds = [k for k, _ in items]
    n = len(items)

    def body(*refs):
        srcs, dsts, sems = refs[:n], refs[n:2 * n], refs[2 * n:]
        _exchange_start(kinds, srcs, dsts, sems)
        _exchange_finish(kinds, srcs, dsts, sems)

    any_spec = pl.BlockSpec(memory_space=pl.ANY)
    return pl.pallas_call(
        body, name=name, in_specs=[any_spec] * n, out_specs=[any_spec] * n,
        out_shape=[_exchange_out_shape(k, a) for k, a in items],
        scratch_shapes=_exchange_sems(n),
        compiler_params=pltpu.CompilerParams(has_side_effects=True),
    )(*[a for _, a in items])


def _gather_two_level(shard, name):
    def body(src, dst, send_sems, recv_sems, local_sem):
        x, y, c, me = _position()
        sibling = (x, y, 1 - c)
        chips = [(1 - x, y), (x, 1 - y), (1 - x, 1 - y)]

        def block(px, py, pc):
            return dst.at[4 * px + 2 * py + pc]

        def copy(k, blk, to, src_ref=None):
            return pltpu.make_async_remote_copy(
                src_ref=block(*blk) if src_ref is None else src_ref, dst_ref=block(*blk),
                send_sem=send_sems.at[k], recv_sem=recv_sems.at[k], device_id=to, device_id_type=MESH)

        mine = pltpu.make_async_copy(src, dst.at[me], local_sem)
        mine.start()
        first = [copy(0, (x, y, c), sibling, src)] + [copy(1 + j, (x, y, c), (*chip, c), src) for j, chip in enumerate(chips)]
        for cp in first:
            cp.start()
        passed = [copy(4 + j, (*chip, c), sibling) for j, chip in enumerate(chips)]
        for j, chip in enumerate(chips):
            copy(1 + j, (*chip, c), (x, y, c)).wait_recv()
            passed[j].start()
        copy(0, (x, y, 1 - c), (x, y, c)).wait_recv()
        for j, chip in enumerate(chips):
            copy(4 + j, (*chip, 1 - c), (x, y, c)).wait_recv()
        for cp in first + passed:
            cp.wait_send()
        mine.wait()

    any_spec = pl.BlockSpec(memory_space=pl.ANY)
    return pl.pallas_call(
        body, name=name, in_specs=[any_spec], out_specs=any_spec,
        out_shape=_exchange_out_shape("gather", shard),
        scratch_shapes=[pltpu.SemaphoreType.DMA((N_DEV - 1,)), pltpu.SemaphoreType.DMA((N_DEV - 1,)), pltpu.SemaphoreType.DMA],
        compiler_params=pltpu.CompilerParams(has_side_effects=True),
    )(shard)


def _call(body, *, name, grid, in_specs, out_specs, out_shape, args, scratch_shapes=(), semantics, exchange=()):
    if not exchange:
        outs = pl.pallas_call(body, name=name, grid=grid, in_specs=in_specs, out_specs=out_specs, out_shape=out_shape,
                              scratch_shapes=list(scratch_shapes), compiler_params=_params(*semantics))(*args)
        return outs, []
    kinds = [k for k, _ in exchange]
    n_in, n_out, n_x, n_scr = len(in_specs), len(out_specs), len(exchange), len(scratch_shapes)

    def wrapped(*refs):
        ins, refs = refs[:n_in], refs[n_in:]
        srcs, refs = refs[:n_x], refs[n_x:]
        outs, refs = refs[:n_out], refs[n_out:]
        dsts, refs = refs[:n_x], refs[n_x:]
        scratch, sems = refs[:n_scr], refs[n_scr:]
        ids = [pl.program_id(a) for a in range(len(grid))]
        first = functools.reduce(jnp.logical_and, [i == 0 for i in ids])
        last = functools.reduce(jnp.logical_and, [i == g - 1 for i, g in zip(ids, grid)])

        @pl.when(first)
        def _():
            _exchange_start(kinds, srcs, dsts, sems)

        body(*ins, *outs, *scratch)

        @pl.when(last)
        def _():
            _exchange_finish(kinds, srcs, dsts, sems)

    any_spec = pl.BlockSpec(memory_space=pl.ANY)
    outs = pl.pallas_call(
        wrapped, name=name, grid=grid,
        in_specs=list(in_specs) + [any_spec] * n_x, out_specs=list(out_specs) + [any_spec] * n_x,
        out_shape=list(out_shape) + [_exchange_out_shape(k, a) for k, a in exchange],
        scratch_shapes=list(scratch_shapes) + _exchange_sems(n_x),
        compiler_params=pltpu.CompilerParams(dimension_semantics=("arbitrary",) * len(grid),
                                             vmem_limit_bytes=V7X_VMEM_LIMIT_BYTES, has_side_effects=True),
    )(*args, *[a for _, a in exchange])
    return outs[:n_out], outs[n_out:]


def _adamw_math(w, g, m, v):
    m = ADAM_B1 * m + (1.0 - ADAM_B1) * g
    v = ADAM_B2 * v + (1.0 - ADAM_B2) * (g * g)
    m_hat = m / (1.0 - ADAM_B1 ** ADAM_STEP)
    v_hat = v / (1.0 - ADAM_B2 ** ADAM_STEP)
    return -ADAM_LR * (m_hat / (jnp.sqrt(v_hat) + ADAM_EPS) + ADAM_WD * w), m, v


def _adamw(parts, w, m, v, name):
    rows, cols = w.shape
    tm = 256 if rows % 256 == 0 and rows > 256 else rows

    def body(p_ref, w_ref, m_ref, v_ref, g_ref, d_ref, nm_ref, nv_ref):
        g = p_ref[0].astype(F32)
        for j in range(1, N_DEV):
            g = g + p_ref[j].astype(F32)
        delta, nm, nv = _adamw_math(w_ref[...], g, m_ref[...], v_ref[...])
        g_ref[...] = g
        d_ref[...] = delta
        nm_ref[...] = nm
        nv_ref[...] = nv

    shard = jax.ShapeDtypeStruct((rows, cols), F32)
    return pl.pallas_call(
        body, name=name, grid=(rows // tm,),
        in_specs=[pl.BlockSpec((N_DEV, tm, cols), lambda i: (0, i, 0))] + [_rows(tm, cols)] * 3,
        out_specs=[_rows(tm, cols)] * 4,
        out_shape=[shard] * 4,
        compiler_params=_params("parallel"),
    )(parts, w, m, v)


_SMALL = ("mix_norm_g", "sgu_w", "sgu_b", "sgu_norm_g", "out_norm_a", "out_norm_b", "ffn_norm_g", "ple_norm_g", "final_norm_g")
_BIG = ("w_in", "w_out", "w_gate", "w_up", "w_down", "w_ple_gate", "w_ple_proj")
_COLUMN_SHARDED = ("w_in", "w_gate", "w_up", "w_ple_proj")
_ORDER = ("mix_norm_g", "w_in", "sgu_w", "sgu_b", "sgu_norm_g", "out_norm_a", "out_norm_b", "w_out", "ffn_norm_g",
          "w_gate", "w_up", "w_down", "ple_norm_g", "w_ple_gate", "w_ple_proj", "final_norm_g")


def _pack_small(values, names=_SMALL):
    flat = jnp.concatenate([values[n].reshape(-1).astype(F32) for n in names])
    pad = (-flat.shape[0]) % (8 * 128)
    return jnp.pad(flat, (0, pad)).reshape(-1, 128)


def _unpack_small(packed, like):
    flat = packed.reshape(-1)
    out, at = {}, 0
    for n in _SMALL:
        size = like[n].size
        out[n] = flat[at:at + size].reshape(like[n].shape)
        at += size
    return out


def _own_orientation(name, value):
    return value[0].T if name in _COLUMN_SHARDED else value[0]


def _reference_orientation(name, value):
    return (value.T if name in _COLUMN_SHARDED else value)[None]


def _full_from_gathered(gathered):
    return gathered.reshape(N_DEV * gathered.shape[1], gathered.shape[2])


def _sliced_for_devices(grad):
    return grad.reshape(N_DEV, grad.shape[0] // N_DEV, grad.shape[1])


def _rope_tables(s_len):
    half = HEAD_DIM // 2
    inv = ROPE_THETA ** (-jnp.arange(half, dtype=F32) / half)
    ang = jnp.arange(s_len, dtype=F32)[:, None] * inv[None, :]
    cos, sin = jnp.cos(ang), jnp.sin(ang)
    return jnp.concatenate([cos, cos, cos, cos], axis=1), jnp.concatenate([-sin, sin, -sin, sin], axis=1)


def _forward_backward(x, p, target, small, shards):
    def gather(*names):
        return [("gather", shards[n]) for n in names]

    def scatter(**grads):
        return [("scatter", _sliced_for_devices(g)) for g in grads.values()]

    full, parts = {}, {}
    full["w_in"] = _full_from_gathered(_gather_two_level(shards["w_in"], "gather_w_in"))

    s_len = x.shape[0]
    cos, sin = _rope_tables(s_len)
    g_mix, g_ffn, g_ple = small["mix_norm_g"], small["ffn_norm_g"], small["ple_norm_g"]
    g_fin = small["final_norm_g"].reshape(1, D_MODEL)
    sw, gs, ga, gb = small["sgu_w"], small["sgu_norm_g"], small["out_norm_a"], small["out_norm_b"]
    b2 = jnp.repeat(small["sgu_b"].T, HEAD_DIM, axis=1)
    lane_head = jnp.arange(128) // HEAD_DIM
    head_sum = (lane_head[:, None] == lane_head[None, :]).astype(BF16)
    n_br = len(DILATIONS)

    def arrived(names, got):
        for n, g in zip(names, got):
            full[n] = _full_from_gathered(g)

    (ua, hn1, *qkv), got = _inproj(x, g_mix, full["w_in"], cos, sin, exchange=gather("w_gate"))
    arrived(("w_gate",), got)
    ya_n = _sgu_fwd(ua, sw, b2, gs, ga)
    half = shards["w_up"].shape[0] // 2
    riders = [[("gather", shards["w_up"][:half])], [("gather", shards["w_up"][half:])], gather("w_out")]
    branch, got = [], []
    for i, d in enumerate(DILATIONS):
        o_l, g = _attn_fwd(qkv[i], d, exchange=riders[i])
        branch.append(o_l)
        got += g
    arrived(("w_up", "w_out"), [jnp.concatenate(got[:2], axis=1), got[2]])
    (y, yb, *lse), _ = _combine([o for o, _ in branch], [l for _, l in branch], ya_n, gb)
    h1, _ = _mm_res(y, full["w_out"], x, "out_proj")
    (a, b, act, hn2), got = _ffn_up(h1, g_ffn, full["w_gate"], full["w_up"], exchange=gather("w_down"))
    arrived(("w_down",), got)
    h2, got = _mm_res(act, full["w_down"], h1, "ffn_down", exchange=gather("w_ple_gate", "w_ple_proj"))
    arrived(("w_ple_gate", "w_ple_proj"), got)
    h3, gate, pp, hn3 = _ple(h2, g_ple, full["w_ple_gate"], p, full["w_ple_proj"])

    dz, dpp, dh2, loss, d_fin, d_ple = _loss_ple_bwd(h3, target, g_fin, gate, pp, h2, g_ple, full["w_ple_gate"])
    g_ple_proj, _ = _mm_tn(dpp, p, "dw_ple_proj")
    g_ple_gate, _ = _mm_tn(hn3, dz, "dw_ple_gate")
    g_down, (parts["w_ple_gate"], parts["w_ple_proj"]) = _mm_tn(
        act, dh2, "dw_down", exchange=scatter(w_ple_gate=g_ple_gate, w_ple_proj=g_ple_proj))
    (da, db), (parts["w_down"],) = _ffn_down_bwd(dh2, full["w_down"], a, b, exchange=scatter(w_down=g_down))
    g_gate, _ = _mm_tn(da, hn2, "dw_gate")
    g_up, _ = _mm_tn(db, hn2, "dw_up")
    (dh1, d_ffn), (parts["w_gate"],) = _mm_norm_bwd(
        [(da, full["w_gate"]), (db, full["w_up"])], h1, g_ffn, dh2, "ffn_up_bwd", exchange=scatter(w_gate=g_gate))
    g_out, _ = _mm_tn(y, dh1, "dw_out")
    (dya_n, d_gb, *do_dd), (parts["w_out"],) = _outproj_bwd(dh1, full["w_out"], yb, gb, head_sum, exchange=scatter(w_out=g_out))
    grads_b = []
    for i, d in enumerate(DILATIONS):
        g3, got = _attn_bwd(qkv[i], do_dd[i], lse[i], do_dd[n_br + i], d, exchange=scatter(w_up=g_up) if i == 0 else ())
        grads_b.append(g3)
        if i == 0:
            (parts["w_up"],) = got
    dua, d_sw, d_b2, d_gs, d_ga = _sgu_bwd(ua, sw, b2, gs, ga, dya_n)
    dproj = _dproj(dua, [g[0] for g in grads_b], [g[1] for g in grads_b], [g[2] for g in grads_b], cos, sin)
    early = {
        "sgu_w": d_sw, "sgu_b": d_b2.reshape(CHUNK, 4, HEAD_DIM).sum(axis=-1).T, "sgu_norm_g": d_gs, "out_norm_a": d_ga,
        "out_norm_b": d_gb, "ffn_norm_g": d_ffn, "ple_norm_g": d_ple, "final_norm_g": d_fin,
    }
    g_in, (early_parts,) = _mm_tn(dproj, hn1, "dw_in", exchange=[("gather", _pack_small(early, _SMALL[1:]))])
    (dx, d_mix), (parts["w_in"],) = _mm_norm_bwd(
        [(dproj, full["w_in"])], x, g_mix, dh1, "inproj_bwd", exchange=scatter(w_in=g_in))
    (late_parts,) = _exchange_only([("gather", _pack_small({"mix_norm_g": d_mix}, _SMALL[:1]))], "gather_mix_norm_grad")
    return loss[0, 0], dx, parts, jnp.concatenate([late_parts, early_parts], axis=1)


def kernel(x, p, mix_norm_g, w_in, sgu_w, sgu_b, sgu_norm_g, out_norm_a, out_norm_b, w_out, ffn_norm_g, w_gate, w_up, w_down, ple_norm_g, w_ple_gate, w_ple_proj, final_norm_g, loss_target, m_mix_norm_g, m_w_in, m_sgu_w, m_sgu_b, m_sgu_norm_g, m_out_norm_a, m_out_norm_b, m_w_out, m_ffn_norm_g, m_w_gate, m_w_up, m_w_down, m_ple_norm_g, m_w_ple_gate, m_w_ple_proj, m_final_norm_g, v_mix_norm_g, v_w_in, v_sgu_w, v_sgu_b, v_sgu_norm_g, v_out_norm_a, v_out_norm_b, v_w_out, v_ffn_norm_g, v_w_gate, v_w_up, v_w_down, v_ple_norm_g, v_w_ple_gate, v_w_ple_proj, v_final_norm_g):
    given = dict(locals())
    weights = {n: given[n] for n in _ORDER}
    moments_m = {n: given["m_" + n] for n in _ORDER}
    moments_v = {n: given["v_" + n] for n in _ORDER}

    shards = {n: _own_orientation(n, weights[n]).astype(BF16) for n in _BIG}
    small = {n: (weights[n][0] if n in ("sgu_w", "sgu_b") else weights[n]) for n in _SMALL}

    loss, dx, parts, small_parts = _forward_backward(x[0], p[0, 0], loss_target[0], small, shards)
    loss = lax.psum(loss, ("x", "y", "c"))

    small_like = {n: weights[n] for n in _SMALL}
    grads, deltas, new_m, new_v = {}, {}, {}, {}
    for n in _BIG:
        outs = _adamw(parts[n], _own_orientation(n, weights[n]), _own_orientation(n, moments_m[n]),
                      _own_orientation(n, moments_v[n]), "adamw_" + n)
        grads[n], deltas[n], new_m[n], new_v[n] = [_reference_orientation(n, o) for o in outs]
    g, d, nm, nv = _adamw(small_parts, _pack_small(small_like), _pack_small({n: moments_m[n] for n in _SMALL}),
                          _pack_small({n: moments_v[n] for n in _SMALL}), "adamw_small")
    for out, packed in ((grads, g), (deltas, d), (new_m, nm), (new_v, nv)):
        out.update(_unpack_small(packed, small_like))

    return (loss, dx[None], *[grads[n] for n in _ORDER], *[deltas[n] for n in _ORDER],
            *[new_m[n] for n in _ORDER], *[new_v[n] for n in _ORDER])
```

```python
import functools

import jax
import jax.numpy as jnp
from jax import lax
from jax.experimental import pallas as pl
from jax.experimental.pallas import tpu as pltpu

F32 = jnp.float32
BF16 = jnp.bfloat16

D_MODEL = 1024
WIDTH_A = 256
WIDTH_B = 768
D_FF = 2816
IN_COLS = 2 * WIDTH_A + 3 * WIDTH_B
PLE_DIM = 256
HEAD_DIM = 64
N_PAIRS = WIDTH_B // 128
CHUNK = 128
N_BACK = 128
DILATIONS = (1, 4, 16)
ROPE_THETA = 10000.0
EPS = 1e-6
N_DEV = 8

ADAM_LR = 0.001
ADAM_B1 = 0.9
ADAM_B2 = 0.999
ADAM_EPS = 1e-08
ADAM_WD = 0.01
ADAM_STEP = 10

V7X_VMEM_LIMIT_BYTES = 56 * 1024 * 1024
ROW_TILE = 512
MESH = pl.DeviceIdType.MESH
NEG = -1e30

_NT = (((1,), (1,)), ((), ()))
_TN = (((0,), (0,)), ((), ()))


def _params(*semantics):
    return pltpu.CompilerParams(dimension_semantics=semantics, vmem_limit_bytes=V7X_VMEM_LIMIT_BYTES)


def _rows(tm, width):
    return pl.BlockSpec((tm, width), lambda i: (i, 0))


def _whole(shape):
    return pl.BlockSpec(shape, lambda *_: (0,) * len(shape))


def _gelu(x):
    t = jnp.tanh(0.7978845608028654 * (x + 0.044715 * (x * x * x)))
    return 0.5 * x * (1.0 + t)


def _gelu_grad(x):
    t = jnp.tanh(0.7978845608028654 * (x + 0.044715 * (x * x * x)))
    return 0.5 * (1.0 + t) + 0.5 * x * (1.0 - t * t) * (0.7978845608028654 * (1.0 + 3.0 * 0.044715 * (x * x)))


def _rstd(x):
    return lax.rsqrt(jnp.mean(x * x, axis=-1, keepdims=True) + EPS)


def _norm_bwd(dn, h, g):
    r = _rstd(h)
    n = h * r
    t = dn * g
    return r * (t - n * jnp.mean(t * n, axis=-1, keepdims=True)), dn * n


def _swap_halves(x, first_half):
    return jnp.where(first_half, pltpu.roll(x, 96, 1), pltpu.roll(x, 32, 1))


def _sub_spec(d, n_cb, tm):
    return pl.BlockSpec((d, n_cb, tm // d, 128), lambda i: (0, 0, i, 0))


def _sub_shape(s_len, d, n_cb, dtype):
    return jax.ShapeDtypeStruct((d, n_cb, s_len // d, 128), dtype)


def _to_sub(stage_ref, cb_src, out_ref, cb_dst, d, tm):
    slab = stage_ref.at[cb_src]
    for r in range(d):
        out_ref[r, cb_dst] = slab[pl.ds(r, tm // d, stride=d), :].astype(out_ref.dtype)


def _from_sub(in_ref, cb_src, stage_ref, cb_dst, d, tm, accumulate=False):
    slab = stage_ref.at[cb_dst]
    for r in range(d):
        rows = pl.ds(r, tm // d, stride=d)
        val = in_ref[r, cb_src].astype(F32)
        slab[rows, :] = slab[rows, :] + val if accumulate else val


def _inproj(x, g, w, cos, sin, exchange=()):
    s_len = x.shape[0]
    tm = ROW_TILE
    n_cb = 3 * N_PAIRS

    def body(x_ref, g_ref, w_ref, cos_ref, sin_ref, ua_ref, hn_ref, *rest):
        sub_refs, stage = rest[:-1], rest[-1]
        xf = x_ref[...]
        hn = (xf * _rstd(xf) * g_ref[...]).astype(BF16)
        hn_ref[...] = hn
        c = cos_ref[...]
        s = sin_ref[...]
        first_half = (lax.broadcasted_iota(jnp.int32, (tm, 128), 1) % HEAD_DIM) < HEAD_DIM // 2
        for j in range(IN_COLS // 256):
            col = j * 256
            acc = lax.dot_general(hn, w_ref[col:col + 256, :], _NT, preferred_element_type=F32)
            if col < 2 * WIDTH_A:
                ua_ref[:, col:col + 256] = acc
                continue
            for half in range(2):
                cb = (col - 2 * WIDTH_A) // 128 + half
                t = acc[:, half * 128:(half + 1) * 128]
                if cb < 2 * N_PAIRS:
                    t = (t * c + _swap_halves(t, first_half) * s) * (0.125 if cb < N_PAIRS else 1.0)
                stage[cb] = t
        for cb in range(n_cb):
            for d, out_ref in zip(DILATIONS, sub_refs):
                _to_sub(stage, cb, out_ref, cb, d, tm)

    return _call(
        body, name="inproj", grid=(s_len // tm,),
        in_specs=[_rows(tm, D_MODEL), _whole((1, D_MODEL)), _whole((IN_COLS, D_MODEL)), _rows(tm, 128), _rows(tm, 128)],
        out_specs=[_rows(tm, 2 * WIDTH_A), _rows(tm, D_MODEL)] + [_sub_spec(d, n_cb, tm) for d in DILATIONS],
        out_shape=[jax.ShapeDtypeStruct((s_len, 2 * WIDTH_A), F32), jax.ShapeDtypeStruct((s_len, D_MODEL), BF16)]
        + [_sub_shape(s_len, d, n_cb, BF16) for d in DILATIONS],
        scratch_shapes=[pltpu.VMEM((n_cb, tm, 128), F32)],
        semantics=("parallel",), args=(x, g, w, cos, sin), exchange=exchange)


def _sgu_mix_weights(sw_ref):
    keep = lax.broadcasted_iota(jnp.int32, (CHUNK, CHUNK), 0) >= lax.broadcasted_iota(jnp.int32, (CHUNK, CHUNK), 1)
    return [jnp.where(keep, sw_ref[h], 0.0).astype(BF16) for h in range(4)], keep


def _sgu_core(ua_ref, gs_ref):
    u = ua_ref[:, :WIDTH_A]
    va = ua_ref[:, WIDTH_A:]
    vg = _gelu(va)
    xc = vg - jnp.mean(vg, axis=-1, keepdims=True)
    rstd = lax.rsqrt(jnp.mean(xc * xc, axis=-1, keepdims=True) + EPS)
    xhat = xc * rstd
    return u, va, _gelu(u), xhat, rstd, xhat * gs_ref[...]


def _sgu_fwd(ua, sw, b2, gs, ga):
    s_len = ua.shape[0]
    tm = ROW_TILE

    def body(ua_ref, sw_ref, b2_ref, gs_ref, ga_ref, out_ref):
        _, _, ug, _, _, vn = _sgu_core(ua_ref, gs_ref)
        wm, _ = _sgu_mix_weights(sw_ref)
        head = lax.broadcasted_iota(jnp.int32, (CHUNK, WIDTH_A), 1) // HEAD_DIM
        for c in range(tm // CHUNK):
            rows = slice(c * CHUNK, (c + 1) * CHUNK)
            vnc = vn[rows]
            mixed = b2_ref[...]
            for h in range(4):
                mixed = mixed + jnp.dot(wm[h], jnp.where(head == h, vnc, 0.0).astype(BF16), preferred_element_type=F32)
            ya = ug[rows] * mixed
            out_ref[rows, :] = (ya * _rstd(ya) * ga_ref[...]).astype(BF16)

    return pl.pallas_call(
        body, name="sgu_fwd", grid=(s_len // tm,),
        in_specs=[_rows(tm, 2 * WIDTH_A), _whole((4, CHUNK, CHUNK)), _whole((CHUNK, WIDTH_A)), _whole((1, WIDTH_A)), _whole((1, WIDTH_A))],
        out_specs=_rows(tm, WIDTH_A),
        out_shape=jax.ShapeDtypeStruct((s_len, WIDTH_A), BF16),
        compiler_params=_params("parallel"),
    )(ua, sw, b2, gs, ga)


def _attn_geometry(sd):
    tile = min(ROW_TILE, sd)
    return tile, tile // CHUNK, sd // tile


PAIRS_PER_STEP = 6


def _attn_spec(cb0, rows, row_index):
    return pl.BlockSpec((None, PAIRS_PER_STEP, rows, 128), lambda r, g, n: (r, cb0 // PAIRS_PER_STEP + g, row_index(n), 0))


def _per_pair(one_pair):
    def body(*refs):
        for hp in range(PAIRS_PER_STEP):
            one_pair(*[ref.at[hp] for ref in refs])
    return body


def _both_heads(x, head_a):
    zero = jnp.zeros_like(x)
    return [jnp.where(head_a, x, zero), jnp.where(head_a, zero, x)]


def _attn_fwd(qkv, d, exchange=()):
    sd = qkv.shape[2]
    tile, nb, n_tiles = _attn_geometry(sd)

    def prev(n):
        return jnp.maximum(n * nb - 1, 0)

    def one_pair(q_ref, k_ref, kp_ref, v_ref, vp_ref, o_ref, l_ref):
        n = pl.program_id(2)
        head_a = lax.broadcasted_iota(jnp.int32, (CHUNK, 128), 1) < HEAD_DIM
        qi = lax.broadcasted_iota(jnp.int32, (2 * CHUNK, 2 * CHUNK), 0) % CHUNK
        kc = lax.broadcasted_iota(jnp.int32, (2 * CHUNK, 2 * CHUNK), 1)
        band = (kc >= qi) & (kc <= qi + N_BACK)
        for j in range(nb):
            rows = slice(j * CHUNK, (j + 1) * CHUNK)
            if j == 0:
                kcat = jnp.concatenate([kp_ref[...], k_ref[rows, :]], axis=0)
                vcat = jnp.concatenate([vp_ref[...], v_ref[rows, :]], axis=0)
                valid = band & jnp.logical_or(n > 0, kc >= CHUNK)
            else:
                kcat = k_ref[(j - 1) * CHUNK:(j + 1) * CHUNK, :]
                vcat = v_ref[(j - 1) * CHUNK:(j + 1) * CHUNK, :]
                valid = band
            q2 = jnp.concatenate(_both_heads(q_ref[rows, :], head_a), axis=0)
            s = lax.dot_general(q2, kcat, _NT, preferred_element_type=F32)
            s = jnp.where(valid, s, NEG)
            m = jnp.max(s, axis=-1, keepdims=True)
            p = jnp.exp(s - m)
            l = jnp.sum(p, axis=-1, keepdims=True)
            o2 = jnp.dot(p.astype(BF16), vcat, preferred_element_type=F32) / l
            lse2 = m + jnp.log(l)
            o_ref[rows, :] = jnp.where(head_a, o2[:CHUNK], o2[CHUNK:]).astype(BF16)
            l_ref[rows, :] = jnp.where(head_a, lse2[:CHUNK], lse2[CHUNK:])

    same = lambda n: n
    return _call(
        _per_pair(one_pair), name=f"attn_fwd_d{d}", grid=(d, N_PAIRS // PAIRS_PER_STEP, n_tiles),
        in_specs=[_attn_spec(0, tile, same), _attn_spec(N_PAIRS, tile, same), _attn_spec(N_PAIRS, CHUNK, prev),
                  _attn_spec(2 * N_PAIRS, tile, same), _attn_spec(2 * N_PAIRS, CHUNK, prev)],
        out_specs=[_attn_spec(0, tile, same), _attn_spec(0, tile, same)],
        out_shape=[jax.ShapeDtypeStruct((d, N_PAIRS, sd, 128), BF16), jax.ShapeDtypeStruct((d, N_PAIRS, sd, 128), F32)],
        semantics=("parallel", "parallel", "parallel"), args=(qkv, qkv, qkv, qkv, qkv), exchange=exchange)


def _combine(outs, lses, ya_n, gb, exchange=()):
    s_len = ya_n.shape[0]
    tm = ROW_TILE
    n_br = len(DILATIONS)

    def body(*refs):
        o_refs, l_refs = refs[:n_br], refs[n_br:2 * n_br]
        ya_ref, gb_ref, y_ref, yb_ref = refs[2 * n_br:2 * n_br + 4]
        lse_refs = refs[2 * n_br + 4:3 * n_br + 4]
        o_nat, l_nat, lse_nat = refs[3 * n_br + 4:]
        sumsq = jnp.zeros((tm, 1), F32)
        for cb in range(N_PAIRS):
            for i, d in enumerate(DILATIONS):
                _from_sub(o_refs[i], cb, o_nat, i, d, tm)
                _from_sub(l_refs[i], cb, l_nat, i, d, tm)
            ls = [l_nat[i] for i in range(n_br)]
            top = jnp.maximum(jnp.maximum(ls[0], ls[1]), ls[2])
            ws = [jnp.exp(l - top) for l in ls]
            den = ws[0] + ws[1] + ws[2]
            inv = 1.0 / den
            yb = (ws[0] * inv) * o_nat[0] + (ws[1] * inv) * o_nat[1] + (ws[2] * inv) * o_nat[2]
            yb_ref[:, cb * 128:(cb + 1) * 128] = yb
            sumsq = sumsq + jnp.sum(yb * yb, axis=-1, keepdims=True)
            lse_nat[cb] = top + jnp.log(den)
            for d, lse_ref in zip(DILATIONS, lse_refs):
                _to_sub(lse_nat, cb, lse_ref, cb, d, tm)
        r = lax.rsqrt(sumsq / WIDTH_B + EPS)
        y_ref[:, :WIDTH_A] = ya_ref[...]
        y_ref[:, WIDTH_A:] = (yb_ref[...] * r * gb_ref[...]).astype(BF16)

    return _call(
        body, name="attn_combine", grid=(s_len // tm,),
        in_specs=[_sub_spec(d, N_PAIRS, tm) for d in DILATIONS] * 2 + [_rows(tm, WIDTH_A), _whole((1, WIDTH_B))],
        out_specs=[_rows(tm, D_MODEL), _rows(tm, WIDTH_B)] + [_sub_spec(d, N_PAIRS, tm) for d in DILATIONS],
        out_shape=[jax.ShapeDtypeStruct((s_len, D_MODEL), BF16), jax.ShapeDtypeStruct((s_len, WIDTH_B), F32)]
        + [_sub_shape(s_len, d, N_PAIRS, F32) for d in DILATIONS],
        scratch_shapes=[pltpu.VMEM((n_br, tm, 128), F32), pltpu.VMEM((n_br, tm, 128), F32), pltpu.VMEM((N_PAIRS, tm, 128), F32)],
        semantics=("parallel",), args=(*outs, *lses, ya_n, gb), exchange=exchange)


def _mm_res(a, w, res, name, exchange=()):
    s_len, k = a.shape
    tm = ROW_TILE if k <= D_MODEL else ROW_TILE // 2

    def body(a_ref, w_ref, r_ref, o_ref):
        o_ref[...] = r_ref[...] + jnp.dot(a_ref[...].astype(BF16), w_ref[...], preferred_element_type=F32)

    (out,), received = _call(
        body, name=name, grid=(s_len // tm,),
        in_specs=[_rows(tm, k), _whole((k, D_MODEL)), _rows(tm, D_MODEL)],
        out_specs=[_rows(tm, D_MODEL)],
        out_shape=[jax.ShapeDtypeStruct((s_len, D_MODEL), F32)],
        semantics=("parallel",), args=(a, w, res), exchange=exchange)
    return out, received


def _ffn_up(h, g, wg, wu, exchange=()):
    s_len = h.shape[0]
    tm = ROW_TILE // 2

    def body(h_ref, g_ref, wg_ref, wu_ref, a_ref, b_ref, act_ref, hn_ref):
        hf = h_ref[...]
        hn = (hf * _rstd(hf) * g_ref[...]).astype(BF16)
        hn_ref[...] = hn
        for j in range(D_FF // 256):
            cols = slice(j * 256, (j + 1) * 256)
            a = lax.dot_general(hn, wg_ref[cols, :], _NT, preferred_element_type=F32)
            b = lax.dot_general(hn, wu_ref[cols, :], _NT, preferred_element_type=F32)
            a_ref[:, cols] = a.astype(BF16)
            b_ref[:, cols] = b.astype(BF16)
            act_ref[:, cols] = (a * jax.nn.sigmoid(a) * b).astype(BF16)

    wide = jax.ShapeDtypeStruct((s_len, D_FF), BF16)
    return _call(
        body, name="ffn_up", grid=(s_len // tm,),
        in_specs=[_rows(tm, D_MODEL), _whole((1, D_MODEL)), _whole((D_FF, D_MODEL)), _whole((D_FF, D_MODEL))],
        out_specs=[_rows(tm, D_FF), _rows(tm, D_FF), _rows(tm, D_FF), _rows(tm, D_MODEL)],
        out_shape=[wide, wide, wide, jax.ShapeDtypeStruct((s_len, D_MODEL), BF16)],
        semantics=("parallel",), args=(h, g, wg, wu), exchange=exchange)


def _ple(h, g, wpg, p, wpp):
    s_len = h.shape[0]
    tm = ROW_TILE

    def body(h_ref, g_ref, wpg_ref, p_ref, wpp_ref, h3_ref, gate_ref, pp_ref, hn_ref):
        hf = h_ref[...]
        hn = (hf * _rstd(hf) * g_ref[...]).astype(BF16)
        hn_ref[...] = hn
        gate = jax.nn.sigmoid(jnp.dot(hn, wpg_ref[...], preferred_element_type=F32))
        pp = lax.dot_general(p_ref[...].astype(BF16), wpp_ref[...], _NT, preferred_element_type=F32)
        h3_ref[...] = hf + gate * pp
        gate_ref[...] = gate.astype(BF16)
        pp_ref[...] = pp.astype(BF16)

    half = jax.ShapeDtypeStruct((s_len, D_MODEL), BF16)
    return pl.pallas_call(
        body, name="ple", grid=(s_len // tm,),
        in_specs=[_rows(tm, D_MODEL), _whole((1, D_MODEL)), _whole((D_MODEL, D_MODEL)), _rows(tm, PLE_DIM), _whole((D_MODEL, PLE_DIM))],
        out_specs=[_rows(tm, D_MODEL)] * 4,
        out_shape=[jax.ShapeDtypeStruct((s_len, D_MODEL), F32), half, half, half],
        compiler_params=_params("parallel"),
    )(h, g, wpg, p, wpp)


def _loss_ple_bwd(h3, target, gf, gate, pp, h2, g_ple, wpg, hn3, p):
    s_len = h3.shape[0]
    tm = ROW_TILE
    n_steps = s_len // tm

    def body(h_ref, t_ref, g_ref, gate_ref, pp_ref, h2_ref, gp_ref, w_ref, hn_ref, p_ref,
             dh2_ref, loss_ref, dg_ref, dgp_ref, dwg_ref, dwp_ref, acc_g, acc_p):
        step = pl.program_id(0)

        @pl.when(step == 0)
        def _():
            loss_ref[...] = jnp.zeros_like(loss_ref)
            dg_ref[...] = jnp.zeros_like(dg_ref)
            dgp_ref[...] = jnp.zeros_like(dgp_ref)
            acc_g[...] = jnp.zeros_like(acc_g)
            acc_p[...] = jnp.zeros_like(acc_p)

        hf = h_ref[...]
        gfv = g_ref[...]
        err = hf * _rstd(hf) * gfv - t_ref[...]
        loss_ref[...] += 0.5 * jnp.sum(jnp.sum(err * err, axis=-1, keepdims=True), axis=0, keepdims=True) / D_MODEL
        dh, dg_rows = _norm_bwd(err / D_MODEL, hf, gfv)
        dg_ref[...] += jnp.sum(dg_rows, axis=0, keepdims=True)
        gate = gate_ref[...].astype(F32)
        dz = (dh * pp_ref[...].astype(F32) * gate * (1.0 - gate)).astype(BF16)
        dpp = (dh * gate).astype(BF16)
        dn = lax.dot_general(dz, w_ref[...], _NT, preferred_element_type=F32)
        dh2, dgp_rows = _norm_bwd(dn, h2_ref[...], gp_ref[...])
        dh2 = dh + dh2
        dh2_ref[...] = dh2
        dgp_ref[...] += jnp.sum(dgp_rows, axis=0, keepdims=True)
        acc_g[...] += lax.dot_general(hn_ref[...], dz, _TN, preferred_element_type=F32)
        acc_p[...] += lax.dot_general(dpp, p_ref[...].astype(BF16), _TN, preferred_element_type=F32)

        @pl.when(step == n_steps - 1)
        def _():
            dwg_ref[...] = acc_g[...].astype(BF16)
            dwp_ref[...] = acc_p[...].astype(BF16)

    gain = jax.ShapeDtypeStruct((1, D_MODEL), F32)
    return pl.pallas_call(
        body, name="loss_ple_bwd", grid=(n_steps,),
        in_specs=[_rows(tm, D_MODEL), _rows(tm, D_MODEL), _whole((1, D_MODEL)), _rows(tm, D_MODEL), _rows(tm, D_MODEL),
                  _rows(tm, D_MODEL), _whole((1, D_MODEL)), _whole((D_MODEL, D_MODEL)), _rows(tm, D_MODEL),
                  _rows(tm, PLE_DIM)],
        out_specs=[_rows(tm, D_MODEL), _whole((1, 128)), _whole((1, D_MODEL)), _whole((1, D_MODEL)),
                   _whole((D_MODEL, D_MODEL)), _whole((D_MODEL, PLE_DIM))],
        out_shape=[jax.ShapeDtypeStruct((s_len, D_MODEL), F32), jax.ShapeDtypeStruct((1, 128), F32), gain, gain,
                   jax.ShapeDtypeStruct((D_MODEL, D_MODEL), BF16), jax.ShapeDtypeStruct((D_MODEL, PLE_DIM), BF16)],
        scratch_shapes=[pltpu.VMEM((D_MODEL, D_MODEL), F32), pltpu.VMEM((D_MODEL, PLE_DIM), F32)],
        compiler_params=_params("arbitrary"),
    )(h3, target, gf, gate, pp, h2, g_ple, wpg, hn3, p)


def _mm_norm_bwd(parts, h, g, dres, name, exchange=(), dw_lhs=None):
    s_len = h.shape[0]
    tm = ROW_TILE // 2
    n_parts = len(parts)
    n_steps = s_len // tm
    has_dw = dw_lhs is not None

    def body(*refs):
        a_refs = refs[0:2 * n_parts:2]
        w_refs = refs[1:2 * n_parts:2]
        h_ref, g_ref, r_ref = refs[2 * n_parts:2 * n_parts + 3]
        rest = refs[2 * n_parts + 3:]
        step = pl.program_id(0)
        if has_dw:
            lhs_ref, o_ref, dg_ref, dw_ref, acc_ref = rest
        else:
            o_ref, dg_ref = rest

        @pl.when(step == 0)
        def _():
            dg_ref[...] = jnp.zeros_like(dg_ref)
            if has_dw:
                acc_ref[...] = jnp.zeros_like(acc_ref)

        dn = jnp.dot(a_refs[0][...], w_refs[0][...], preferred_element_type=F32)
        for a_ref, w_ref in zip(a_refs[1:], w_refs[1:]):
            dn = dn + jnp.dot(a_ref[...], w_ref[...], preferred_element_type=F32)
        dh, dg_rows = _norm_bwd(dn, h_ref[...], g_ref[...])
        out = r_ref[...] + dh
        o_ref[...] = out
        dg_ref[...] += jnp.sum(dg_rows, axis=0, keepdims=True)
        if has_dw:
            acc_ref[...] += lax.dot_general(lhs_ref[...], out.astype(BF16), _TN, preferred_element_type=F32)

            @pl.when(step == n_steps - 1)
            def _():
                dw_ref[...] = acc_ref[...].astype(BF16)

    in_specs, args = [], []
    for a, w in parts:
        in_specs += [_rows(tm, a.shape[1]), _whole(w.shape)]
        args += [a, w]
    in_specs += [_rows(tm, D_MODEL), _whole((1, D_MODEL)), _rows(tm, D_MODEL)]
    args += [h, g, dres]
    out_specs = [_rows(tm, D_MODEL), _whole((1, D_MODEL))]
    out_shape = [jax.ShapeDtypeStruct((s_len, D_MODEL), F32), jax.ShapeDtypeStruct((1, D_MODEL), F32)]
    scratch = []
    if has_dw:
        m = dw_lhs.shape[1]
        in_specs.append(_rows(tm, m))
        args.append(dw_lhs)
        out_specs.append(_whole((m, D_MODEL)))
        out_shape.append(jax.ShapeDtypeStruct((m, D_MODEL), BF16))
        scratch.append(pltpu.VMEM((m, D_MODEL), F32))
    return _call(
        body, name=name, grid=(n_steps,), in_specs=in_specs, out_specs=out_specs, out_shape=out_shape,
        scratch_shapes=scratch, semantics=("arbitrary",), args=tuple(args), exchange=exchange)


def _ffn_down_bwd(dh, wdt, a, b, exchange=()):
    s_len = dh.shape[0]
    tm = ROW_TILE // 2

    def body(dh_ref, w_ref, a_ref, b_ref, da_ref, db_ref):
        dhb = dh_ref[...].astype(BF16)
        for j in range(D_FF // 256):
            cols = slice(j * 256, (j + 1) * 256)
            dact = lax.dot_general(dhb, w_ref[cols, :], _NT, preferred_element_type=F32)
            av = a_ref[:, cols].astype(F32)
            bv = b_ref[:, cols].astype(F32)
            sig = jax.nn.sigmoid(av)
            da_ref[:, cols] = (dact * bv * sig * (1.0 + av * (1.0 - sig))).astype(BF16)
            db_ref[:, cols] = (dact * av * sig).astype(BF16)

    wide = jax.ShapeDtypeStruct((s_len, D_FF), BF16)
    return _call(
        body, name="ffn_down_bwd", grid=(s_len // tm,),
        in_specs=[_rows(tm, D_MODEL), _whole((D_FF, D_MODEL)), _rows(tm, D_FF), _rows(tm, D_FF)],
        out_specs=[_rows(tm, D_FF), _rows(tm, D_FF)],
        out_shape=[wide, wide],
        semantics=("parallel",), args=(dh, wdt, a, b), exchange=exchange)


def _outproj_bwd(dh1, woutt, yb, gb, head_sum, exchange=()):
    s_len = dh1.shape[0]
    tm = ROW_TILE
    n_br = len(DILATIONS)

    def body(dh_ref, w_ref, yb_ref, gb_ref, e_ref, dya_ref, dgb_ref, *rest):
        do_refs, dd_refs = rest[:n_br], rest[n_br:2 * n_br]
        do_nat, dd_nat = rest[2 * n_br:]

        @pl.when(pl.program_id(0) == 0)
        def _():
            dgb_ref[...] = jnp.zeros_like(dgb_ref)

        dhb = dh_ref[...].astype(BF16)
        dya_ref[...] = lax.dot_general(dhb, w_ref[:WIDTH_A, :], _NT, preferred_element_type=F32)
        dyn = lax.dot_general(dhb, w_ref[WIDTH_A:, :], _NT, preferred_element_type=F32)
        ybv = yb_ref[...]
        dyb, dg_rows = _norm_bwd(dyn, ybv, gb_ref[...])
        dgb_ref[...] += jnp.sum(dg_rows, axis=0, keepdims=True)
        prod = dyb * ybv
        for cb in range(N_PAIRS):
            cols = slice(cb * 128, (cb + 1) * 128)
            pc = prod[:, cols]
            hi = pc.astype(BF16)
            lo = (pc - hi.astype(F32)).astype(BF16)
            do_nat[cb] = dyb[:, cols]
            dd_nat[cb] = (jnp.dot(hi, e_ref[...], preferred_element_type=F32)
                          + jnp.dot(lo, e_ref[...], preferred_element_type=F32))
            for i, d in enumerate(DILATIONS):
                _to_sub(do_nat, cb, do_refs[i], cb, d, tm)
                _to_sub(dd_nat, cb, dd_refs[i], cb, d, tm)

    subs = [_sub_spec(d, N_PAIRS, tm) for d in DILATIONS]
    return _call(
        body, name="outproj_bwd", grid=(s_len // tm,),
        in_specs=[_rows(tm, D_MODEL), _whole((D_MODEL, D_MODEL)), _rows(tm, WIDTH_B), _whole((1, WIDTH_B)), _whole((128, 128))],
        out_specs=[_rows(tm, WIDTH_A), _whole((1, WIDTH_B))] + subs + subs,
        out_shape=[jax.ShapeDtypeStruct((s_len, WIDTH_A), F32), jax.ShapeDtypeStruct((1, WIDTH_B), F32)]
        + [_sub_shape(s_len, d, N_PAIRS, BF16) for d in DILATIONS] + [_sub_shape(s_len, d, N_PAIRS, F32) for d in DILATIONS],
        scratch_shapes=[pltpu.VMEM((N_PAIRS, tm, 128), F32), pltpu.VMEM((N_PAIRS, tm, 128), F32)],
        semantics=("arbitrary",), args=(dh1, woutt, yb, gb, head_sum), exchange=exchange)


def _attn_bwd(qkv, do, lse, dd, d, exchange=()):
    sd = qkv.shape[2]
    tile, nb, n_tiles = _attn_geometry(sd)
    last_block = sd // CHUNK - 1

    def nxt(n):
        return jnp.minimum((n + 1) * nb, last_block)

    def one_pair(q_ref, qn_ref, k_ref, v_ref, do_ref, don_ref, l_ref, ln_ref, dd_ref, ddn_ref,
                 dq_ref, dk_ref, dv_ref, carry_ref):
        n = pl.program_id(2)

        @pl.when(n == 0)
        def _():
            carry_ref[...] = jnp.zeros_like(carry_ref)

        head_a = lax.broadcasted_iota(jnp.int32, (CHUNK, 128), 1) < HEAD_DIM
        row = lax.broadcasted_iota(jnp.int32, (4 * CHUNK, CHUNK), 0)
        qi = row % CHUNK
        ki = lax.broadcasted_iota(jnp.int32, (4 * CHUNK, CHUNK), 1)
        is_after = row >= 2 * CHUNK
        mask = (is_after & (ki >= qi)) | (jnp.logical_not(is_after) & (qi >= ki))
        mask_last = mask & jnp.logical_or(jnp.logical_not(is_after), n < n_tiles - 1)
        dq_acc = [carry_ref[...]] + [jnp.zeros((CHUNK, 128), F32) for _ in range(nb)]

        def per_head(x):
            other = pltpu.roll(x, HEAD_DIM, 1)
            return [jnp.where(head_a, x, other), jnp.where(head_a, other, x)]

        for j in range(nb):
            rows = slice(j * CHUNK, (j + 1) * CHUNK)
            kj = k_ref[rows, :]
            vj = v_ref[rows, :]
            if j + 1 < nb:
                nrows = slice((j + 1) * CHUNK, (j + 2) * CHUNK)
                q2, do2, l2, dd2, msk = q_ref[nrows, :], do_ref[nrows, :], l_ref[nrows, :], dd_ref[nrows, :], mask
            else:
                q2, do2, l2, dd2, msk = qn_ref[...], don_ref[...], ln_ref[...], ddn_ref[...], mask_last
            qs = jnp.concatenate(_both_heads(q_ref[rows, :], head_a) + _both_heads(q2, head_a), axis=0)
            dos = jnp.concatenate(_both_heads(do_ref[rows, :], head_a) + _both_heads(do2, head_a), axis=0)
            ls = jnp.concatenate(per_head(l_ref[rows, :]) + per_head(l2), axis=0)
            dds = jnp.concatenate(per_head(dd_ref[rows, :]) + per_head(dd2), axis=0)
            s = lax.dot_general(qs, kj, _NT, preferred_element_type=F32)
            p = jnp.exp(jnp.where(msk, s - ls, NEG))
            dp = lax.dot_general(dos, vj, _NT, preferred_element_type=F32)
            ds = (p * (dp - dds)).astype(BF16)
            dv_ref[rows, :] = lax.dot_general(p.astype(BF16), dos, _TN, preferred_element_type=F32).astype(BF16)
            dk_ref[rows, :] = lax.dot_general(ds, qs, _TN, preferred_element_type=F32).astype(BF16)
            dqs = jnp.dot(ds, kj, preferred_element_type=F32)
            dq_acc[j] = dq_acc[j] + jnp.where(head_a, dqs[:CHUNK], dqs[CHUNK:2 * CHUNK])
            dq_acc[j + 1] = dq_acc[j + 1] + jnp.where(head_a, dqs[2 * CHUNK:3 * CHUNK], dqs[3 * CHUNK:])
        for j in range(nb):
            dq_ref[j * CHUNK:(j + 1) * CHUNK, :] = dq_acc[j].astype(BF16)
        carry_ref[...] = dq_acc[nb]

    same = lambda n: n
    grad = jax.ShapeDtypeStruct((d, N_PAIRS, sd, 128), BF16)
    return _call(
        _per_pair(one_pair), name=f"attn_bwd_d{d}", grid=(d, N_PAIRS // PAIRS_PER_STEP, n_tiles),
        in_specs=[_attn_spec(0, tile, same), _attn_spec(0, CHUNK, nxt), _attn_spec(N_PAIRS, tile, same),
                  _attn_spec(2 * N_PAIRS, tile, same), _attn_spec(0, tile, same), _attn_spec(0, CHUNK, nxt),
                  _attn_spec(0, tile, same), _attn_spec(0, CHUNK, nxt), _attn_spec(0, tile, same), _attn_spec(0, CHUNK, nxt)],
        out_specs=[_attn_spec(0, tile, same)] * 3,
        out_shape=[grad, grad, grad],
        scratch_shapes=[pltpu.VMEM((PAIRS_PER_STEP, CHUNK, 128), F32)],
        semantics=("parallel", "parallel", "arbitrary"), args=(qkv, qkv, qkv, qkv, do, do, lse, lse, dd, dd), exchange=exchange)


def _sgu_bwd(ua, sw, b2, gs, ga, dya_n):
    s_len = ua.shape[0]
    tm = ROW_TILE

    def body(ua_ref, sw_ref, b2_ref, gs_ref, ga_ref, dy_ref, dua_ref, dsw_ref, db2_ref, dgs_ref, dga_ref):
        @pl.when(pl.program_id(0) == 0)
        def _():
            dsw_ref[...] = jnp.zeros_like(dsw_ref)
            db2_ref[...] = jnp.zeros_like(db2_ref)
            dgs_ref[...] = jnp.zeros_like(dgs_ref)
            dga_ref[...] = jnp.zeros_like(dga_ref)

        u, va, ug, xhat, rstd, vn = _sgu_core(ua_ref, gs_ref)
        wm, keep = _sgu_mix_weights(sw_ref)
        head = lax.broadcasted_iota(jnp.int32, (CHUNK, WIDTH_A), 1) // HEAD_DIM
        gav = ga_ref[...]
        gsv = gs_ref[...]
        dga = jnp.zeros((1, WIDTH_A), F32)
        dgs = jnp.zeros((1, WIDTH_A), F32)
        db2 = jnp.zeros((CHUNK, WIDTH_A), F32)
        dsw = [jnp.zeros((CHUNK, CHUNK), F32) for _ in range(4)]
        for c in range(tm // CHUNK):
            rows = slice(c * CHUNK, (c + 1) * CHUNK)
            vnc = vn[rows]
            vnb = vnc.astype(BF16)
            mixed = b2_ref[...]
            for h in range(4):
                mixed = mixed + jnp.dot(wm[h], jnp.where(head == h, vnc, 0.0).astype(BF16), preferred_element_type=F32)
            ugc = ug[rows]
            dya, dga_rows = _norm_bwd(dy_ref[rows, :], ugc * mixed, gav)
            dga = dga + jnp.sum(dga_rows, axis=0, keepdims=True)
            dmixed = dya * ugc
            db2 = db2 + dmixed
            dvn = jnp.zeros((CHUNK, WIDTH_A), F32)
            for h in range(4):
                dmh = jnp.where(head == h, dmixed, 0.0).astype(BF16)
                dsw[h] = dsw[h] + lax.dot_general(dmh, vnb, _NT, preferred_element_type=F32)
                dvn = dvn + lax.dot_general(wm[h], dmh, _TN, preferred_element_type=F32)
            xh = xhat[rows]
            dgs = dgs + jnp.sum(dvn * xh, axis=0, keepdims=True)
            dxh = dvn * gsv
            dvg = rstd[rows] * (dxh - jnp.mean(dxh, axis=-1, keepdims=True) - xh * jnp.mean(dxh * xh, axis=-1, keepdims=True))
            dua_ref[rows, :WIDTH_A] = (dya * mixed * _gelu_grad(u[rows])).astype(BF16)
            dua_ref[rows, WIDTH_A:] = (dvg * _gelu_grad(va[rows])).astype(BF16)
        for h in range(4):
            dsw_ref[h] += jnp.where(keep, dsw[h], 0.0)
        db2_ref[...] += db2
        dgs_ref[...] += dgs
        dga_ref[...] += dga

    return pl.pallas_call(
        body, name="sgu_bwd", grid=(s_len // tm,),
        in_specs=[_rows(tm, 2 * WIDTH_A), _whole((4, CHUNK, CHUNK)), _whole((CHUNK, WIDTH_A)), _whole((1, WIDTH_A)),
                  _whole((1, WIDTH_A)), _rows(tm, WIDTH_A)],
        out_specs=[_rows(tm, 2 * WIDTH_A), _whole((4, CHUNK, CHUNK)), _whole((CHUNK, WIDTH_A)), _whole((1, WIDTH_A)), _whole((1, WIDTH_A))],
        out_shape=[jax.ShapeDtypeStruct((s_len, 2 * WIDTH_A), BF16), jax.ShapeDtypeStruct((4, CHUNK, CHUNK), F32),
                   jax.ShapeDtypeStruct((CHUNK, WIDTH_A), F32), jax.ShapeDtypeStruct((1, WIDTH_A), F32),
                   jax.ShapeDtypeStruct((1, WIDTH_A), F32)],
        compiler_params=_params("arbitrary"),
    )(ua, sw, b2, gs, ga, dya_n)


def _dproj(dua, dqs, dks, dvs, cos, sin):
    s_len = dua.shape[0]
    tm = ROW_TILE
    n_br = len(DILATIONS)

    def body(dua_ref, *rest):
        groups = [rest[g * n_br:(g + 1) * n_br] for g in range(3)]
        cos_ref, sin_ref, out_ref, acc = rest[3 * n_br:]
        out_ref[:, :2 * WIDTH_A] = dua_ref[...]
        c = cos_ref[...]
        s = sin_ref[...]
        first_half = (lax.broadcasted_iota(jnp.int32, (tm, 128), 1) % HEAD_DIM) < HEAD_DIM // 2
        for g, refs in enumerate(groups):
            for cb in range(N_PAIRS):
                for i, d in enumerate(DILATIONS):
                    _from_sub(refs[i], cb, acc, 0, d, tm, accumulate=i > 0)
                t = acc[0]
                if g < 2:
                    t = (t * c - _swap_halves(t, first_half) * s) * (0.125 if g == 0 else 1.0)
                col = 2 * WIDTH_A + g * WIDTH_B + cb * 128
                out_ref[:, col:col + 128] = t.astype(BF16)

    subs = [_sub_spec(d, N_PAIRS, tm) for d in DILATIONS]
    return pl.pallas_call(
        body, name="dproj", grid=(s_len // tm,),
        in_specs=[_rows(tm, 2 * WIDTH_A)] + subs * 3 + [_rows(tm, 128), _rows(tm, 128)],
        out_specs=_rows(tm, IN_COLS),
        out_shape=jax.ShapeDtypeStruct((s_len, IN_COLS), BF16),
        scratch_shapes=[pltpu.VMEM((1, tm, 128), F32)],
        compiler_params=_params("parallel"),
    )(dua, *dqs, *dks, *dvs, cos, sin)


def _mm_tn(a, b, name, exchange=()):
    s_len, m = a.shape
    n = b.shape[1]
    tk = ROW_TILE
    tm = m if m <= 512 else (1408 if m == D_FF else 512)
    n_k = s_len // tk

    def body(a_ref, b_ref, o_ref, acc_ref):
        k = pl.program_id(1)

        @pl.when(k == 0)
        def _():
            acc_ref[...] = jnp.zeros_like(acc_ref)

        acc_ref[...] += lax.dot_general(a_ref[...].astype(BF16), b_ref[...].astype(BF16), _TN, preferred_element_type=F32)

        @pl.when(k == n_k - 1)
        def _():
            o_ref[...] = acc_ref[...].astype(BF16)

    (grad,), received = _call(
        body, name=name, grid=(m // tm, n_k),
        in_specs=[pl.BlockSpec((tk, tm), lambda i, k: (k, i)), pl.BlockSpec((tk, n), lambda i, k: (k, 0))],
        out_specs=[pl.BlockSpec((tm, n), lambda i, k: (i, 0))],
        out_shape=[jax.ShapeDtypeStruct((m, n), BF16)],
        scratch_shapes=[pltpu.VMEM((tm, n), F32)],
        semantics=("parallel", "arbitrary"), args=(a, b), exchange=exchange)
    return grad, received


def _position():
    x, y, c = lax.axis_index("x"), lax.axis_index("y"), lax.axis_index("c")
    return x, y, c, 4 * x + 2 * y + c


def _peer(x, y, c, rel):
    return (x ^ ((rel >> 2) & 1), y ^ ((rel >> 1) & 1), c ^ (rel & 1))


def _exchange_out_shape(kind, arr):
    return jax.ShapeDtypeStruct(((N_DEV,) + arr.shape) if kind == "gather" else arr.shape, arr.dtype)


def _exchange_sems(n_items):
    return [pltpu.SemaphoreType.DMA((n_items, N_DEV)), pltpu.SemaphoreType.DMA((n_items, N_DEV)), pltpu.SemaphoreType.DMA((n_items,))]


def _exchange_copies(kinds, srcs, dsts, sems):
    send_sems, recv_sems, local_sems = sems
    x, y, c, me = _position()
    local, sends, recvs = [], [], []
    for k, (kind, src, dst) in enumerate(zip(kinds, srcs, dsts)):
        own = src if kind == "gather" else src.at[me]
        local.append(pltpu.make_async_copy(own, dst.at[me], local_sems.at[k]))
        for rel in range(1, N_DEV):
            going = src if kind == "gather" else src.at[me ^ rel]
            common = dict(send_sem=send_sems.at[k, rel], recv_sem=recv_sems.at[k, rel],
                          device_id=_peer(x, y, c, rel), device_id_type=MESH)
            sends.append(pltpu.make_async_remote_copy(src_ref=going, dst_ref=dst.at[me], **common))
            recvs.append(pltpu.make_async_remote_copy(src_ref=own, dst_ref=dst.at[me ^ rel], **common))
    return local, sends, recvs


def _exchange_start(kinds, srcs, dsts, sems):
    local, sends, _ = _exchange_copies(kinds, srcs, dsts, sems)
    for cp in local + sends:
        cp.start()


def _exchange_finish(kinds, srcs, dsts, sems):
    local, sends, recvs = _exchange_copies(kinds, srcs, dsts, sems)
    for cp in recvs:
        cp.wait_recv()
    for cp in sends:
        cp.wait_send()
    for cp in local:
        cp.wait()


def _exchange_only(items, name):
    kinds = [k for k, _ in items]
    n = len(items)

    def body(*refs):
        srcs, dsts, sems = refs[:n], refs[n:2 * n], refs[2 * n:]
        _exchange_start(kinds, srcs, dsts, sems)
        _exchange_finish(kinds, srcs, dsts, sems)

    any_spec = pl.BlockSpec(memory_space=pl.ANY)
    return pl.pallas_call(
        body, name=name, in_specs=[any_spec] * n, out_specs=[any_spec] * n,
        out_shape=[_exchange_out_shape(k, a) for k, a in items],
        scratch_shapes=_exchange_sems(n),
        compiler_params=pltpu.CompilerParams(has_side_effects=True),
    )(*[a for _, a in items])


def _gather_two_level(shard, name):
    def body(src, dst, send_sems, recv_sems, local_sem):
        x, y, c, me = _position()
        sibling = (x, y, 1 - c)
        chips = [(1 - x, y), (x, 1 - y), (1 - x, 1 - y)]

        def block(px, py, pc):
            return dst.at[4 * px + 2 * py + pc]

        def copy(k, blk, to, src_ref=None):
            return pltpu.make_async_remote_copy(
                src_ref=block(*blk) if src_ref is None else src_ref, dst_ref=block(*blk),
                send_sem=send_sems.at[k], recv_sem=recv_sems.at[k], device_id=to, device_id_type=MESH)

        mine = pltpu.make_async_copy(src, dst.at[me], local_sem)
        mine.start()
        first = [copy(0, (x, y, c), sibling, src)] + [copy(1 + j, (x, y, c), (*chip, c), src) for j, chip in enumerate(chips)]
        for cp in first:
            cp.start()
        passed = [copy(4 + j, (*chip, c), sibling) for j, chip in enumerate(chips)]
        for j, chip in enumerate(chips):
            copy(1 + j, (*chip, c), (x, y, c)).wait_recv()
            passed[j].start()
        copy(0, (x, y, 1 - c), (x, y, c)).wait_recv()
        for j, chip in enumerate(chips):
            copy(4 + j, (*chip, 1 - c), (x, y, c)).wait_recv()
        for cp in first + passed:
            cp.wait_send()
        mine.wait()

    any_spec = pl.BlockSpec(memory_space=pl.ANY)
    return pl.pallas_call(
        body, name=name, in_specs=[any_spec], out_specs=any_spec,
        out_shape=_exchange_out_shape("gather", shard),
        scratch_shapes=[pltpu.SemaphoreType.DMA((N_DEV - 1,)), pltpu.SemaphoreType.DMA((N_DEV - 1,)), pltpu.SemaphoreType.DMA],
        compiler_params=pltpu.CompilerParams(has_side_effects=True),
    )(shard)


def _call(body, *, name, grid, in_specs, out_specs, out_shape, args, scratch_shapes=(), semantics, exchange=()):
    if not exchange:
        outs = pl.pallas_call(body, name=name, grid=grid, in_specs=in_specs, out_specs=out_specs, out_shape=out_shape,
                              scratch_shapes=list(scratch_shapes), compiler_params=_params(*semantics))(*args)
        return outs, []
    kinds = [k for k, _ in exchange]
    n_in, n_out, n_x, n_scr = len(in_specs), len(out_specs), len(exchange), len(scratch_shapes)

    def wrapped(*refs):
        ins, refs = refs[:n_in], refs[n_in:]
        srcs, refs = refs[:n_x], refs[n_x:]
        outs, refs = refs[:n_out], refs[n_out:]
        dsts, refs = refs[:n_x], refs[n_x:]
        scratch, sems = refs[:n_scr], refs[n_scr:]
        ids = [pl.program_id(a) for a in range(len(grid))]
        first = functools.reduce(jnp.logical_and, [i == 0 for i in ids])
        last = functools.reduce(jnp.logical_and, [i == g - 1 for i, g in zip(ids, grid)])

        @pl.when(first)
        def _():
            _exchange_start(kinds, srcs, dsts, sems)

        body(*ins, *outs, *scratch)

        @pl.when(last)
        def _():
            _exchange_finish(kinds, srcs, dsts, sems)

    any_spec = pl.BlockSpec(memory_space=pl.ANY)
    outs = pl.pallas_call(
        wrapped, name=name, grid=grid,
        in_specs=list(in_specs) + [any_spec] * n_x, out_specs=list(out_specs) + [any_spec] * n_x,
        out_shape=list(out_shape) + [_exchange_out_shape(k, a) for k, a in exchange],
        scratch_shapes=list(scratch_shapes) + _exchange_sems(n_x),
        compiler_params=pltpu.CompilerParams(dimension_semantics=("arbitrary",) * len(grid),
                                             vmem_limit_bytes=V7X_VMEM_LIMIT_BYTES, has_side_effects=True),
    )(*args, *[a for _, a in exchange])
    return outs[:n_out], outs[n_out:]


def _adamw_math(w, g, m, v):
    m = ADAM_B1 * m + (1.0 - ADAM_B1) * g
    v = ADAM_B2 * v + (1.0 - ADAM_B2) * (g * g)
    m_hat = m / (1.0 - ADAM_B1 ** ADAM_STEP)
    v_hat = v / (1.0 - ADAM_B2 ** ADAM_STEP)
    return -ADAM_LR * (m_hat / (jnp.sqrt(v_hat) + ADAM_EPS) + ADAM_WD * w), m, v


def _adamw(parts, w, m, v, name):
    rows, cols = w.shape
    tm = 256 if rows % 256 == 0 and rows > 256 else rows

    def body(p_ref, w_ref, m_ref, v_ref, g_ref, d_ref, nm_ref, nv_ref):
        g = p_ref[0].astype(F32)
        for j in range(1, N_DEV):
            g = g + p_ref[j].astype(F32)
        delta, nm, nv = _adamw_math(w_ref[...], g, m_ref[...], v_ref[...])
        g_ref[...] = g
        d_ref[...] = delta
        nm_ref[...] = nm
        nv_ref[...] = nv

    shard = jax.ShapeDtypeStruct((rows, cols), F32)
    return pl.pallas_call(
        body, name=name, grid=(rows // tm,),
        in_specs=[pl.BlockSpec((N_DEV, tm, cols), lambda i: (0, i, 0))] + [_rows(tm, cols)] * 3,
        out_specs=[_rows(tm, cols)] * 4,
        out_shape=[shard] * 4,
        compiler_params=_params("parallel"),
    )(parts, w, m, v)


_SMALL = ("mix_norm_g", "sgu_w", "sgu_b", "sgu_norm_g", "out_norm_a", "out_norm_b", "ffn_norm_g", "ple_norm_g", "final_norm_g")
_BIG = ("w_in", "w_out", "w_gate", "w_up", "w_down", "w_ple_gate", "w_ple_proj")
_COLUMN_SHARDED = ("w_in", "w_gate", "w_up", "w_ple_proj")
_ORDER = ("mix_norm_g", "w_in", "sgu_w", "sgu_b", "sgu_norm_g", "out_norm_a", "out_norm_b", "w_out", "ffn_norm_g",
          "w_gate", "w_up", "w_down", "ple_norm_g", "w_ple_gate", "w_ple_proj", "final_norm_g")


def _pack_small(values, names=_SMALL):
    flat = jnp.concatenate([values[n].reshape(-1).astype(F32) for n in names])
    pad = (-flat.shape[0]) % (8 * 128)
    return jnp.pad(flat, (0, pad)).reshape(-1, 128)


def _unpack_small(packed, like):
    flat = packed.reshape(-1)
    out, at = {}, 0
    for n in _SMALL:
        size = like[n].size
        out[n] = flat[at:at + size].reshape(like[n].shape)
        at += size
    return out


def _own_orientation(name, value):
    return value[0].T if name in _COLUMN_SHARDED else value[0]


def _reference_orientation(name, value):
    return (value.T if name in _COLUMN_SHARDED else value)[None]


def _full_from_gathered(gathered):
    return gathered.reshape(N_DEV * gathered.shape[1], gathered.shape[2])


def _sliced_for_devices(grad):
    return grad.reshape(N_DEV, grad.shape[0] // N_DEV, grad.shape[1])


def _rope_tables(s_len):
    half = HEAD_DIM // 2
    inv = ROPE_THETA ** (-jnp.arange(half, dtype=F32) / half)
    ang = jnp.arange(s_len, dtype=F32)[:, None] * inv[None, :]
    cos, sin = jnp.cos(ang), jnp.sin(ang)
    return jnp.concatenate([cos, cos, cos, cos], axis=1), jnp.concatenate([-sin, sin, -sin, sin], axis=1)


def _forward_backward(x, p, target, small, shards):
    def gather(*names):
        return [("gather", shards[n]) for n in names]

    def scatter(**grads):
        return [("scatter", _sliced_for_devices(g)) for g in grads.values()]

    full, parts = {}, {}
    full["w_in"] = _full_from_gathered(_gather_two_level(shards["w_in"], "gather_w_in"))

    s_len = x.shape[0]
    cos, sin = _rope_tables(s_len)
    g_mix, g_ffn, g_ple = small["mix_norm_g"], small["ffn_norm_g"], small["ple_norm_g"]
    g_fin = small["final_norm_g"].reshape(1, D_MODEL)
    sw, gs, ga, gb = small["sgu_w"], small["sgu_norm_g"], small["out_norm_a"], small["out_norm_b"]
    b2 = jnp.repeat(small["sgu_b"].T, HEAD_DIM, axis=1)
    lane_head = jnp.arange(128) // HEAD_DIM
    head_sum = (lane_head[:, None] == lane_head[None, :]).astype(BF16)
    n_br = len(DILATIONS)

    def arrived(names, got):
        for n, g in zip(names, got):
            full[n] = _full_from_gathered(g)

    (ua, hn1, *qkv), got = _inproj(x, g_mix, full["w_in"], cos, sin, exchange=gather("w_gate"))
    arrived(("w_gate",), got)
    ya_n = _sgu_fwd(ua, sw, b2, gs, ga)
    half = shards["w_up"].shape[0] // 2
    riders = [[("gather", shards["w_up"][:half])], [("gather", shards["w_up"][half:])], gather("w_out")]
    branch, got = [], []
    for i, d in enumerate(DILATIONS):
        o_l, g = _attn_fwd(qkv[i], d, exchange=riders[i])
        branch.append(o_l)
        got += g
    arrived(("w_up", "w_out"), [jnp.concatenate(got[:2], axis=1), got[2]])
    (y, yb, *lse), _ = _combine([o for o, _ in branch], [l for _, l in branch], ya_n, gb)
    h1, _ = _mm_res(y, full["w_out"], x, "out_proj")
    (a, b, act, hn2), got = _ffn_up(h1, g_ffn, full["w_gate"], full["w_up"], exchange=gather("w_down"))
    arrived(("w_down",), got)
    h2, got = _mm_res(act, full["w_down"], h1, "ffn_down", exchange=gather("w_ple_gate", "w_ple_proj"))
    arrived(("w_ple_gate", "w_ple_proj"), got)
    h3, gate, pp, hn3 = _ple(h2, g_ple, full["w_ple_gate"], p, full["w_ple_proj"])

    dh2, loss, d_fin, d_ple, g_ple_gate, g_ple_proj = _loss_ple_bwd(
        h3, target, g_fin, gate, pp, h2, g_ple, full["w_ple_gate"], hn3, p)
    g_down, (parts["w_ple_gate"], parts["w_ple_proj"]) = _mm_tn(
        act, dh2, "dw_down", exchange=scatter(w_ple_gate=g_ple_gate, w_ple_proj=g_ple_proj))
    (da, db), (parts["w_down"],) = _ffn_down_bwd(dh2, full["w_down"], a, b, exchange=scatter(w_down=g_down))
    g_gate, _ = _mm_tn(da, hn2, "dw_gate")
    g_up, _ = _mm_tn(db, hn2, "dw_up")
    (dh1, d_ffn, g_out), (parts["w_gate"],) = _mm_norm_bwd(
        [(da, full["w_gate"]), (db, full["w_up"])], h1, g_ffn, dh2, "ffn_up_bwd", exchange=scatter(w_gate=g_gate), dw_lhs=y)
    (dya_n, d_gb, *do_dd), (parts["w_out"],) = _outproj_bwd(dh1, full["w_out"], yb, gb, head_sum, exchange=scatter(w_out=g_out))
    grads_b = []
    for i, d in enumerate(DILATIONS):
        g3, got = _attn_bwd(qkv[i], do_dd[i], lse[i], do_dd[n_br + i], d, exchange=scatter(w_up=g_up) if i == 0 else ())
        grads_b.append(g3)
        if i == 0:
            (parts["w_up"],) = got
    dua, d_sw, d_b2, d_gs, d_ga = _sgu_bwd(ua, sw, b2, gs, ga, dya_n)
    dproj = _dproj(dua, [g[0] for g in grads_b], [g[1] for g in grads_b], [g[2] for g in grads_b], cos, sin)
    early = {
        "sgu_w": d_sw, "sgu_b": d_b2.reshape(CHUNK, 4, HEAD_DIM).sum(axis=-1).T, "sgu_norm_g": d_gs, "out_norm_a": d_ga,
        "out_norm_b": d_gb, "ffn_norm_g": d_ffn, "ple_norm_g": d_ple, "final_norm_g": d_fin,
    }
    g_in, (early_parts,) = _mm_tn(dproj, hn1, "dw_in", exchange=[("gather", _pack_small(early, _SMALL[1:]))])
    (dx, d_mix), (parts["w_in"],) = _mm_norm_bwd(
        [(dproj, full["w_in"])], x, g_mix, dh1, "inproj_bwd", exchange=scatter(w_in=g_in))
    (late_parts,) = _exchange_only([("gather", _pack_small({"mix_norm_g": d_mix}, _SMALL[:1]))], "gather_mix_norm_grad")
    return loss[0, 0], dx, parts, jnp.concatenate([late_parts, early_parts], axis=1)


def kernel(x, p, mix_norm_g, w_in, sgu_w, sgu_b, sgu_norm_g, out_norm_a, out_norm_b, w_out, ffn_norm_g, w_gate, w_up, w_down, ple_norm_g, w_ple_gate, w_ple_proj, final_norm_g, loss_target, m_mix_norm_g, m_w_in, m_sgu_w, m_sgu_b, m_sgu_norm_g, m_out_norm_a, m_out_norm_b, m_w_out, m_ffn_norm_g, m_w_gate, m_w_up, m_w_down, m_ple_norm_g, m_w_ple_gate, m_w_ple_proj, m_final_norm_g, v_mix_norm_g, v_w_in, v_sgu_w, v_sgu_b, v_sgu_norm_g, v_out_norm_a, v_out_norm_b, v_w_out, v_ffn_norm_g, v_w_gate, v_w_up, v_w_down, v_ple_norm_g, v_w_ple_gate, v_w_ple_proj, v_final_norm_g):
    given = dict(locals())
    weights = {n: given[n] for n in _ORDER}
    moments_m = {n: given["m_" + n] for n in _ORDER}
    moments_v = {n: given["v_" + n] for n in _ORDER}

    shards = {n: _own_orientation(n, weights[n]).astype(BF16) for n in _BIG}
    small = {n: (weights[n][0] if n in ("sgu_w", "sgu_b") else weights[n]) for n in _SMALL}

    loss, dx, parts, small_parts = _forward_backward(x[0], p[0, 0], loss_target[0], small, shards)
    loss = lax.psum(loss, ("x", "y", "c"))

    small_like = {n: weights[n] for n in _SMALL}
    grads, deltas, new_m, new_v = {}, {}, {}, {}
    for n in _BIG:
        outs = _adamw(parts[n], _own_orientation(n, weights[n]), _own_orientation(n, moments_m[n]),
                      _own_orientation(n, moments_v[n]), "adamw_" + n)
        grads[n], deltas[n], new_m[n], new_v[n] = [_reference_orientation(n, o) for o in outs]
    g, d, nm, nv = _adamw(small_parts, _pack_small(small_like), _pack_small({n: moments_m[n] for n in _SMALL}),
                          _pack_small({n: moments_v[n] for n in _SMALL}), "adamw_small")
    for out, packed in ((grads, g), (deltas, d), (new_m, nm), (new_v, nv)):
        out.update(_unpack_small(packed, small_like))

    return (loss, dx[None], *[grads[n] for n in _ORDER], *[deltas[n] for n in _ORDER],
            *[new_m[n] for n in _ORDER], *[new_v[n] for n in _ORDER])
```

```python
import functools

import jax
import jax.numpy as jnp
from jax import lax
from jax.experimental import pallas as pl
from jax.experimental.pallas import tpu as pltpu

F32 = jnp.float32
BF16 = jnp.bfloat16

D_MODEL = 1024
WIDTH_A = 256
WIDTH_B = 768
D_FF = 2816
IN_COLS = 2 * WIDTH_A + 3 * WIDTH_B
PLE_DIM = 256
HEAD_DIM = 64
N_PAIRS = WIDTH_B // 128
CHUNK = 128
N_BACK = 128
DILATIONS = (1, 4, 16)
ROPE_THETA = 10000.0
EPS = 1e-6
N_DEV = 8

ADAM_LR = 0.001
ADAM_B1 = 0.9
ADAM_B2 = 0.999
ADAM_EPS = 1e-08
ADAM_WD = 0.01
ADAM_STEP = 10

V7X_VMEM_LIMIT_BYTES = 56 * 1024 * 1024
ROW_TILE = 512
MESH = pl.DeviceIdType.MESH
NEG = -1e30

_NT = (((1,), (1,)), ((), ()))
_TN = (((0,), (0,)), ((), ()))


def _params(*semantics):
    return pltpu.CompilerParams(dimension_semantics=semantics, vmem_limit_bytes=V7X_VMEM_LIMIT_BYTES)


def _rows(tm, width):
    return pl.BlockSpec((tm, width), lambda i: (i, 0))


def _whole(shape):
    return pl.BlockSpec(shape, lambda *_: (0,) * len(shape))


def _gelu(x):
    t = jnp.tanh(0.7978845608028654 * (x + 0.044715 * (x * x * x)))
    return 0.5 * x * (1.0 + t)


def _gelu_grad(x):
    t = jnp.tanh(0.7978845608028654 * (x + 0.044715 * (x * x * x)))
    return 0.5 * (1.0 + t) + 0.5 * x * (1.0 - t * t) * (0.7978845608028654 * (1.0 + 3.0 * 0.044715 * (x * x)))


def _rstd(x):
    return lax.rsqrt(jnp.mean(x * x, axis=-1, keepdims=True) + EPS)


def _norm_bwd(dn, h, g):
    r = _rstd(h)
    n = h * r
    t = dn * g
    return r * (t - n * jnp.mean(t * n, axis=-1, keepdims=True)), dn * n


def _swap_halves(x, first_half):
    return jnp.where(first_half, pltpu.roll(x, 96, 1), pltpu.roll(x, 32, 1))


def _sub_spec(d, n_cb, tm):
    return pl.BlockSpec((d, n_cb, tm // d, 128), lambda i: (0, 0, i, 0))


def _sub_shape(s_len, d, n_cb, dtype):
    return jax.ShapeDtypeStruct((d, n_cb, s_len // d, 128), dtype)


def _to_sub(stage_ref, cb_src, out_ref, cb_dst, d, tm):
    slab = stage_ref.at[cb_src]
    for r in range(d):
        out_ref[r, cb_dst] = slab[pl.ds(r, tm // d, stride=d), :].astype(out_ref.dtype)


def _from_sub(in_ref, cb_src, stage_ref, cb_dst, d, tm, accumulate=False):
    slab = stage_ref.at[cb_dst]
    for r in range(d):
        rows = pl.ds(r, tm // d, stride=d)
        val = in_ref[r, cb_src].astype(F32)
        slab[rows, :] = slab[rows, :] + val if accumulate else val


def _inproj(x, g, w, cos, sin, exchange=()):
    s_len = x.shape[0]
    tm = ROW_TILE
    n_cb = 3 * N_PAIRS

    def body(x_ref, g_ref, w_ref, cos_ref, sin_ref, ua_ref, hn_ref, *rest):
        sub_refs, stage = rest[:-1], rest[-1]
        xf = x_ref[...]
        hn = (xf * _rstd(xf) * g_ref[...]).astype(BF16)
        hn_ref[...] = hn
        c = cos_ref[...]
        s = sin_ref[...]
        first_half = (lax.broadcasted_iota(jnp.int32, (tm, 128), 1) % HEAD_DIM) < HEAD_DIM // 2
        for j in range(IN_COLS // 256):
            col = j * 256
            acc = lax.dot_general(hn, w_ref[col:col + 256, :], _NT, preferred_element_type=F32)
            if col < 2 * WIDTH_A:
                ua_ref[:, col:col + 256] = acc
                continue
            for half in range(2):
                cb = (col - 2 * WIDTH_A) // 128 + half
                t = acc[:, half * 128:(half + 1) * 128]
                if cb < 2 * N_PAIRS:
                    t = (t * c + _swap_halves(t, first_half) * s) * (0.125 if cb < N_PAIRS else 1.0)
                stage[cb] = t
        for cb in range(n_cb):
            for d, out_ref in zip(DILATIONS, sub_refs):
                _to_sub(stage, cb, out_ref, cb, d, tm)

    return _call(
        body, name="inproj", grid=(s_len // tm,),
        in_specs=[_rows(tm, D_MODEL), _whole((1, D_MODEL)), _whole((IN_COLS, D_MODEL)), _rows(tm, 128), _rows(tm, 128)],
        out_specs=[_rows(tm, 2 * WIDTH_A), _rows(tm, D_MODEL)] + [_sub_spec(d, n_cb, tm) for d in DILATIONS],
        out_shape=[jax.ShapeDtypeStruct((s_len, 2 * WIDTH_A), F32), jax.ShapeDtypeStruct((s_len, D_MODEL), BF16)]
        + [_sub_shape(s_len, d, n_cb, BF16) for d in DILATIONS],
        scratch_shapes=[pltpu.VMEM((n_cb, tm, 128), F32)],
        semantics=("parallel",), args=(x, g, w, cos, sin), exchange=exchange)


def _sgu_mix_weights(sw_ref):
    keep = lax.broadcasted_iota(jnp.int32, (CHUNK, CHUNK), 0) >= lax.broadcasted_iota(jnp.int32, (CHUNK, CHUNK), 1)
    return [jnp.where(keep, sw_ref[h], 0.0).astype(BF16) for h in range(4)], keep


def _sgu_core(ua_ref, gs_ref):
    u = ua_ref[:, :WIDTH_A]
    va = ua_ref[:, WIDTH_A:]
    vg = _gelu(va)
    xc = vg - jnp.mean(vg, axis=-1, keepdims=True)
    rstd = lax.rsqrt(jnp.mean(xc * xc, axis=-1, keepdims=True) + EPS)
    xhat = xc * rstd
    return u, va, _gelu(u), xhat, rstd, xhat * gs_ref[...]


def _sgu_fwd(ua, sw, b2, gs, ga):
    s_len = ua.shape[0]
    tm = ROW_TILE

    def body(ua_ref, sw_ref, b2_ref, gs_ref, ga_ref, out_ref):
        _, _, ug, _, _, vn = _sgu_core(ua_ref, gs_ref)
        wm, _ = _sgu_mix_weights(sw_ref)
        head = lax.broadcasted_iota(jnp.int32, (CHUNK, WIDTH_A), 1) // HEAD_DIM
        for c in range(tm // CHUNK):
            rows = slice(c * CHUNK, (c + 1) * CHUNK)
            vnc = vn[rows]
            mixed = b2_ref[...]
            for h in range(4):
                mixed = mixed + jnp.dot(wm[h], jnp.where(head == h, vnc, 0.0).astype(BF16), preferred_element_type=F32)
            ya = ug[rows] * mixed
            out_ref[rows, :] = (ya * _rstd(ya) * ga_ref[...]).astype(BF16)

    return pl.pallas_call(
        body, name="sgu_fwd", grid=(s_len // tm,),
        in_specs=[_rows(tm, 2 * WIDTH_A), _whole((4, CHUNK, CHUNK)), _whole((CHUNK, WIDTH_A)), _whole((1, WIDTH_A)), _whole((1, WIDTH_A))],
        out_specs=_rows(tm, WIDTH_A),
        out_shape=jax.ShapeDtypeStruct((s_len, WIDTH_A), BF16),
        compiler_params=_params("parallel"),
    )(ua, sw, b2, gs, ga)


def _attn_geometry(sd):
    tile = min(ROW_TILE, sd)
    return tile, tile // CHUNK, sd // tile


PAIRS_PER_STEP = 6


def _attn_spec(cb0, rows, row_index):
    return pl.BlockSpec((None, PAIRS_PER_STEP, rows, 128), lambda r, g, n: (r, cb0 // PAIRS_PER_STEP + g, row_index(n), 0))


def _per_pair(one_pair):
    def body(*refs):
        for hp in range(PAIRS_PER_STEP):
            one_pair(*[ref.at[hp] for ref in refs])
    return body


def _both_heads(x, head_a):
    zero = jnp.zeros_like(x)
    return [jnp.where(head_a, x, zero), jnp.where(head_a, zero, x)]


def _attn_fwd(qkv, d, exchange=()):
    sd = qkv.shape[2]
    tile, nb, n_tiles = _attn_geometry(sd)

    def prev(n):
        return jnp.maximum(n * nb - 1, 0)

    def one_pair(q_ref, k_ref, kp_ref, v_ref, vp_ref, o_ref, l_ref):
        n = pl.program_id(2)
        head_a = lax.broadcasted_iota(jnp.int32, (CHUNK, 128), 1) < HEAD_DIM
        qi = lax.broadcasted_iota(jnp.int32, (2 * CHUNK, 2 * CHUNK), 0) % CHUNK
        kc = lax.broadcasted_iota(jnp.int32, (2 * CHUNK, 2 * CHUNK), 1)
        band = (kc >= qi) & (kc <= qi + N_BACK)
        for j in range(nb):
            rows = slice(j * CHUNK, (j + 1) * CHUNK)
            if j == 0:
                kcat = jnp.concatenate([kp_ref[...], k_ref[rows, :]], axis=0)
                vcat = jnp.concatenate([vp_ref[...], v_ref[rows, :]], axis=0)
                valid = band & jnp.logical_or(n > 0, kc >= CHUNK)
            else:
                kcat = k_ref[(j - 1) * CHUNK:(j + 1) * CHUNK, :]
                vcat = v_ref[(j - 1) * CHUNK:(j + 1) * CHUNK, :]
                valid = band
            q2 = jnp.concatenate(_both_heads(q_ref[rows, :], head_a), axis=0)
            s = lax.dot_general(q2, kcat, _NT, preferred_element_type=F32)
            s = jnp.where(valid, s, NEG)
            m = jnp.max(s, axis=-1, keepdims=True)
            p = jnp.exp(s - m)
            l = jnp.sum(p, axis=-1, keepdims=True)
            o2 = jnp.dot(p.astype(BF16), vcat, preferred_element_type=F32) / l
            lse2 = m + jnp.log(l)
            o_ref[rows, :] = jnp.where(head_a, o2[:CHUNK], o2[CHUNK:]).astype(BF16)
            l_ref[rows, :] = jnp.where(head_a, lse2[:CHUNK], lse2[CHUNK:])

    same = lambda n: n
    return _call(
        _per_pair(one_pair), name=f"attn_fwd_d{d}", grid=(d, N_PAIRS // PAIRS_PER_STEP, n_tiles),
        in_specs=[_attn_spec(0, tile, same), _attn_spec(N_PAIRS, tile, same), _attn_spec(N_PAIRS, CHUNK, prev),
                  _attn_spec(2 * N_PAIRS, tile, same), _attn_spec(2 * N_PAIRS, CHUNK, prev)],
        out_specs=[_attn_spec(0, tile, same), _attn_spec(0, tile, same)],
        out_shape=[jax.ShapeDtypeStruct((d, N_PAIRS, sd, 128), BF16), jax.ShapeDtypeStruct((d, N_PAIRS, sd, 128), F32)],
        semantics=("parallel", "parallel", "parallel"), args=(qkv, qkv, qkv, qkv, qkv), exchange=exchange)


def _combine(outs, lses, ya_n, gb, exchange=()):
    s_len = ya_n.shape[0]
    tm = ROW_TILE
    n_br = len(DILATIONS)

    def body(*refs):
        o_refs, l_refs = refs[:n_br], refs[n_br:2 * n_br]
        ya_ref, gb_ref, y_ref, yb_ref = refs[2 * n_br:2 * n_br + 4]
        lse_refs = refs[2 * n_br + 4:3 * n_br + 4]
        o_nat, l_nat, lse_nat = refs[3 * n_br + 4:]
        sumsq = jnp.zeros((tm, 1), F32)
        for cb in range(N_PAIRS):
            for i, d in enumerate(DILATIONS):
                _from_sub(o_refs[i], cb, o_nat, i, d, tm)
                _from_sub(l_refs[i], cb, l_nat, i, d, tm)
            ls = [l_nat[i] for i in range(n_br)]
            top = jnp.maximum(jnp.maximum(ls[0], ls[1]), ls[2])
            ws = [jnp.exp(l - top) for l in ls]
            den = ws[0] + ws[1] + ws[2]
            inv = 1.0 / den
            yb = (ws[0] * inv) * o_nat[0] + (ws[1] * inv) * o_nat[1] + (ws[2] * inv) * o_nat[2]
            yb_ref[:, cb * 128:(cb + 1) * 128] = yb
            sumsq = sumsq + jnp.sum(yb * yb, axis=-1, keepdims=True)
            lse_nat[cb] = top + jnp.log(den)
            for d, lse_ref in zip(DILATIONS, lse_refs):
                _to_sub(lse_nat, cb, lse_ref, cb, d, tm)
        r = lax.rsqrt(sumsq / WIDTH_B + EPS)
        y_ref[:, :WIDTH_A] = ya_ref[...]
        y_ref[:, WIDTH_A:] = (yb_ref[...] * r * gb_ref[...]).astype(BF16)

    return _call(
        body, name="attn_combine", grid=(s_len // tm,),
        in_specs=[_sub_spec(d, N_PAIRS, tm) for d in DILATIONS] * 2 + [_rows(tm, WIDTH_A), _whole((1, WIDTH_B))],
        out_specs=[_rows(tm, D_MODEL), _rows(tm, WIDTH_B)] + [_sub_spec(d, N_PAIRS, tm) for d in DILATIONS],
        out_shape=[jax.ShapeDtypeStruct((s_len, D_MODEL), BF16), jax.ShapeDtypeStruct((s_len, WIDTH_B), F32)]
        + [_sub_shape(s_len, d, N_PAIRS, F32) for d in DILATIONS],
        scratch_shapes=[pltpu.VMEM((n_br, tm, 128), F32), pltpu.VMEM((n_br, tm, 128), F32), pltpu.VMEM((N_PAIRS, tm, 128), F32)],
        semantics=("parallel",), args=(*outs, *lses, ya_n, gb), exchange=exchange)


def _ffn_up(y, wout, x, g, wg, wu, exchange=()):
    s_len = x.shape[0]
    tm = ROW_TILE // 2

    def body(y_ref, wo_ref, x_ref, g_ref, wg_ref, wu_ref, h_ref, a_ref, b_ref, act_ref, hn_ref):
        hf = x_ref[...] + jnp.dot(y_ref[...], wo_ref[...], preferred_element_type=F32)
        h_ref[...] = hf
        hn = (hf * _rstd(hf) * g_ref[...]).astype(BF16)
        hn_ref[...] = hn
        for j in range(D_FF // 256):
            cols = slice(j * 256, (j + 1) * 256)
            a = lax.dot_general(hn, wg_ref[cols, :], _NT, preferred_element_type=F32)
            b = lax.dot_general(hn, wu_ref[cols, :], _NT, preferred_element_type=F32)
            a_ref[:, cols] = a.astype(BF16)
            b_ref[:, cols] = b.astype(BF16)
            act_ref[:, cols] = (a * jax.nn.sigmoid(a) * b).astype(BF16)

    wide = jax.ShapeDtypeStruct((s_len, D_FF), BF16)
    return _call(
        body, name="ffn_up", grid=(s_len // tm,),
        in_specs=[_rows(tm, D_MODEL), _whole((D_MODEL, D_MODEL)), _rows(tm, D_MODEL), _whole((1, D_MODEL)),
                  _whole((D_FF, D_MODEL)), _whole((D_FF, D_MODEL))],
        out_specs=[_rows(tm, D_MODEL), _rows(tm, D_FF), _rows(tm, D_FF), _rows(tm, D_FF), _rows(tm, D_MODEL)],
        out_shape=[jax.ShapeDtypeStruct((s_len, D_MODEL), F32), wide, wide, wide, jax.ShapeDtypeStruct((s_len, D_MODEL), BF16)],
        semantics=("parallel",), args=(y, wout, x, g, wg, wu), exchange=exchange)


def _ffn_down_ple(act, wd, h1, g, wpg, p, wpp):
    s_len = h1.shape[0]
    tm = ROW_TILE // 2

    def body(act_ref, wd_ref, h1_ref, g_ref, wpg_ref, p_ref, wpp_ref, h2_ref, h3_ref, gate_ref, pp_ref, hn_ref):
        hf = h1_ref[...] + jnp.dot(act_ref[...], wd_ref[...], preferred_element_type=F32)
        h2_ref[...] = hf
        hn = (hf * _rstd(hf) * g_ref[...]).astype(BF16)
        hn_ref[...] = hn
        gate = jax.nn.sigmoid(jnp.dot(hn, wpg_ref[...], preferred_element_type=F32))
        pp = lax.dot_general(p_ref[...].astype(BF16), wpp_ref[...], _NT, preferred_element_type=F32)
        h3_ref[...] = hf + gate * pp
        gate_ref[...] = gate.astype(BF16)
        pp_ref[...] = pp.astype(BF16)

    full = jax.ShapeDtypeStruct((s_len, D_MODEL), F32)
    half = jax.ShapeDtypeStruct((s_len, D_MODEL), BF16)
    return pl.pallas_call(
        body, name="ffn_down_ple", grid=(s_len // tm,),
        in_specs=[_rows(tm, D_FF), _whole((D_FF, D_MODEL)), _rows(tm, D_MODEL), _whole((1, D_MODEL)),
                  _whole((D_MODEL, D_MODEL)), _rows(tm, PLE_DIM), _whole((D_MODEL, PLE_DIM))],
        out_specs=[_rows(tm, D_MODEL)] * 5,
        out_shape=[full, full, half, half, half],
        compiler_params=_params("parallel"),
    )(act, wd, h1, g, wpg, p, wpp)


def _loss_ple_bwd(h3, target, gf, gate, pp, h2, g_ple, wpg, hn3, p):
    s_len = h3.shape[0]
    tm = ROW_TILE
    n_steps = s_len // tm

    def body(h_ref, t_ref, g_ref, gate_ref, pp_ref, h2_ref, gp_ref, w_ref, hn_ref, p_ref,
             dh2_ref, loss_ref, dg_ref, dgp_ref, dwg_ref, dwp_ref, acc_g, acc_p):
        step = pl.program_id(0)

        @pl.when(step == 0)
        def _():
            loss_ref[...] = jnp.zeros_like(loss_ref)
            dg_ref[...] = jnp.zeros_like(dg_ref)
            dgp_ref[...] = jnp.zeros_like(dgp_ref)
            acc_g[...] = jnp.zeros_like(acc_g)
            acc_p[...] = jnp.zeros_like(acc_p)

        hf = h_ref[...]
        gfv = g_ref[...]
        err = hf * _rstd(hf) * gfv - t_ref[...]
        loss_ref[...] += 0.5 * jnp.sum(jnp.sum(err * err, axis=-1, keepdims=True), axis=0, keepdims=True) / D_MODEL
        dh, dg_rows = _norm_bwd(err / D_MODEL, hf, gfv)
        dg_ref[...] += jnp.sum(dg_rows, axis=0, keepdims=True)
        gate = gate_ref[...].astype(F32)
        dz = (dh * pp_ref[...].astype(F32) * gate * (1.0 - gate)).astype(BF16)
        dpp = (dh * gate).astype(BF16)
        dn = lax.dot_general(dz, w_ref[...], _NT, preferred_element_type=F32)
        dh2, dgp_rows = _norm_bwd(dn, h2_ref[...], gp_ref[...])
        dh2 = dh + dh2
        dh2_ref[...] = dh2
        dgp_ref[...] += jnp.sum(dgp_rows, axis=0, keepdims=True)
        acc_g[...] += lax.dot_general(hn_ref[...], dz, _TN, preferred_element_type=F32)
        acc_p[...] += lax.dot_general(dpp, p_ref[...].astype(BF16), _TN, preferred_element_type=F32)

        @pl.when(step == n_steps - 1)
        def _():
            dwg_ref[...] = acc_g[...].astype(BF16)
            dwp_ref[...] = acc_p[...].astype(BF16)

    gain = jax.ShapeDtypeStruct((1, D_MODEL), F32)
    return pl.pallas_call(
        body, name="loss_ple_bwd", grid=(n_steps,),
        in_specs=[_rows(tm, D_MODEL), _rows(tm, D_MODEL), _whole((1, D_MODEL)), _rows(tm, D_MODEL), _rows(tm, D_MODEL),
                  _rows(tm, D_MODEL), _whole((1, D_MODEL)), _whole((D_MODEL, D_MODEL)), _rows(tm, D_MODEL),
                  _rows(tm, PLE_DIM)],
        out_specs=[_rows(tm, D_MODEL), _whole((1, 128)), _whole((1, D_MODEL)), _whole((1, D_MODEL)),
                   _whole((D_MODEL, D_MODEL)), _whole((D_MODEL, PLE_DIM))],
        out_shape=[jax.ShapeDtypeStruct((s_len, D_MODEL), F32), jax.ShapeDtypeStruct((1, 128), F32), gain, gain,
                   jax.ShapeDtypeStruct((D_MODEL, D_MODEL), BF16), jax.ShapeDtypeStruct((D_MODEL, PLE_DIM), BF16)],
        scratch_shapes=[pltpu.VMEM((D_MODEL, D_MODEL), F32), pltpu.VMEM((D_MODEL, PLE_DIM), F32)],
        compiler_params=_params("arbitrary"),
    )(h3, target, gf, gate, pp, h2, g_ple, wpg, hn3, p)


def _mm_norm_bwd(parts, h, g, dres, name, exchange=(), dw_lhs=None):
    s_len = h.shape[0]
    tm = ROW_TILE // 2
    n_parts = len(parts)
    n_steps = s_len // tm
    has_dw = dw_lhs is not None

    def body(*refs):
        a_refs = refs[0:2 * n_parts:2]
        w_refs = refs[1:2 * n_parts:2]
        h_ref, g_ref, r_ref = refs[2 * n_parts:2 * n_parts + 3]
        rest = refs[2 * n_parts + 3:]
        step = pl.program_id(0)
        if has_dw:
            lhs_ref, o_ref, dg_ref, dw_ref, acc_ref = rest
        else:
            o_ref, dg_ref = rest

        @pl.when(step == 0)
        def _():
            dg_ref[...] = jnp.zeros_like(dg_ref)
            if has_dw:
                acc_ref[...] = jnp.zeros_like(acc_ref)

        dn = jnp.dot(a_refs[0][...], w_refs[0][...], preferred_element_type=F32)
        for a_ref, w_ref in zip(a_refs[1:], w_refs[1:]):
            dn = dn + jnp.dot(a_ref[...], w_ref[...], preferred_element_type=F32)
        dh, dg_rows = _norm_bwd(dn, h_ref[...], g_ref[...])
        out = r_ref[...] + dh
        o_ref[...] = out
        dg_ref[...] += jnp.sum(dg_rows, axis=0, keepdims=True)
        if has_dw:
            acc_ref[...] += lax.dot_general(lhs_ref[...], out.astype(BF16), _TN, preferred_element_type=F32)

            @pl.when(step == n_steps - 1)
            def _():
                dw_ref[...] = acc_ref[...].astype(BF16)

    in_specs, args = [], []
    for a, w in parts:
        in_specs += [_rows(tm, a.shape[1]), _whole(w.shape)]
        args += [a, w]
    in_specs += [_rows(tm, D_MODEL), _whole((1, D_MODEL)), _rows(tm, D_MODEL)]
    args += [h, g, dres]
    out_specs = [_rows(tm, D_MODEL), _whole((1, D_MODEL))]
    out_shape = [jax.ShapeDtypeStruct((s_len, D_MODEL), F32), jax.ShapeDtypeStruct((1, D_MODEL), F32)]
    scratch = []
    if has_dw:
        m = dw_lhs.shape[1]
        in_specs.append(_rows(tm, m))
        args.append(dw_lhs)
        out_specs.append(_whole((m, D_MODEL)))
        out_shape.append(jax.ShapeDtypeStruct((m, D_MODEL), BF16))
        scratch.append(pltpu.VMEM((m, D_MODEL), F32))
    return _call(
        body, name=name, grid=(n_steps,), in_specs=in_specs, out_specs=out_specs, out_shape=out_shape,
        scratch_shapes=scratch, semantics=("arbitrary",), args=tuple(args), exchange=exchange)


def _ffn_down_bwd(dh, wdt, a, b, exchange=()):
    s_len = dh.shape[0]
    tm = ROW_TILE // 2

    def body(dh_ref, w_ref, a_ref, b_ref, da_ref, db_ref):
        dhb = dh_ref[...].astype(BF16)
        for j in range(D_FF // 256):
            cols = slice(j * 256, (j + 1) * 256)
            dact = lax.dot_general(dhb, w_ref[cols, :], _NT, preferred_element_type=F32)
            av = a_ref[:, cols].astype(F32)
            bv = b_ref[:, cols].astype(F32)
            sig = jax.nn.sigmoid(av)
            da_ref[:, cols] = (dact * bv * sig * (1.0 + av * (1.0 - sig))).astype(BF16)
            db_ref[:, cols] = (dact * av * sig).astype(BF16)

    wide = jax.ShapeDtypeStruct((s_len, D_FF), BF16)
    return _call(
        body, name="ffn_down_bwd", grid=(s_len // tm,),
        in_specs=[_rows(tm, D_MODEL), _whole((D_FF, D_MODEL)), _rows(tm, D_FF), _rows(tm, D_FF)],
        out_specs=[_rows(tm, D_FF), _rows(tm, D_FF)],
        out_shape=[wide, wide],
        semantics=("parallel",), args=(dh, wdt, a, b), exchange=exchange)


def _outproj_bwd(dh1, woutt, yb, gb, head_sum, exchange=()):
    s_len = dh1.shape[0]
    tm = ROW_TILE
    n_br = len(DILATIONS)

    def body(dh_ref, w_ref, yb_ref, gb_ref, e_ref, dya_ref, dgb_ref, *rest):
        do_refs, dd_refs = rest[:n_br], rest[n_br:2 * n_br]
        do_nat, dd_nat = rest[2 * n_br:]

        @pl.when(pl.program_id(0) == 0)
        def _():
            dgb_ref[...] = jnp.zeros_like(dgb_ref)

        dhb = dh_ref[...].astype(BF16)
        dya_ref[...] = lax.dot_general(dhb, w_ref[:WIDTH_A, :], _NT, preferred_element_type=F32)
        dyn = lax.dot_general(dhb, w_ref[WIDTH_A:, :], _NT, preferred_element_type=F32)
        ybv = yb_ref[...]
        dyb, dg_rows = _norm_bwd(dyn, ybv, gb_ref[...])
        dgb_ref[...] += jnp.sum(dg_rows, axis=0, keepdims=True)
        prod = dyb * ybv
        for cb in range(N_PAIRS):
            cols = slice(cb * 128, (cb + 1) * 128)
            pc = prod[:, cols]
            hi = pc.astype(BF16)
            lo = (pc - hi.astype(F32)).astype(BF16)
            do_nat[cb] = dyb[:, cols]
            dd_nat[cb] = (jnp.dot(hi, e_ref[...], preferred_element_type=F32)
                          + jnp.dot(lo, e_ref[...], preferred_element_type=F32))
            for i, d in enumerate(DILATIONS):
                _to_sub(do_nat, cb, do_refs[i], cb, d, tm)
                _to_sub(dd_nat, cb, dd_refs[i], cb, d, tm)

    subs = [_sub_spec(d, N_PAIRS, tm) for d in DILATIONS]
    return _call(
        body, name="outproj_bwd", grid=(s_len // tm,),
        in_specs=[_rows(tm, D_MODEL), _whole((D_MODEL, D_MODEL)), _rows(tm, WIDTH_B), _whole((1, WIDTH_B)), _whole((128, 128))],
        out_specs=[_rows(tm, WIDTH_A), _whole((1, WIDTH_B))] + subs + subs,
        out_shape=[jax.ShapeDtypeStruct((s_len, WIDTH_A), F32), jax.ShapeDtypeStruct((1, WIDTH_B), F32)]
        + [_sub_shape(s_len, d, N_PAIRS, BF16) for d in DILATIONS] + [_sub_shape(s_len, d, N_PAIRS, F32) for d in DILATIONS],
        scratch_shapes=[pltpu.VMEM((N_PAIRS, tm, 128), F32), pltpu.VMEM((N_PAIRS, tm, 128), F32)],
        semantics=("arbitrary",), args=(dh1, woutt, yb, gb, head_sum), exchange=exchange)


def _attn_bwd(qkv, do, lse, dd, d, exchange=()):
    sd = qkv.shape[2]
    tile, nb, n_tiles = _attn_geometry(sd)
    last_block = sd // CHUNK - 1

    def nxt(n):
        return jnp.minimum((n + 1) * nb, last_block)

    def one_pair(q_ref, qn_ref, k_ref, v_ref, do_ref, don_ref, l_ref, ln_ref, dd_ref, ddn_ref,
                 dq_ref, dk_ref, dv_ref, carry_ref):
        n = pl.program_id(2)

        @pl.when(n == 0)
        def _():
            carry_ref[...] = jnp.zeros_like(carry_ref)

        head_a = lax.broadcasted_iota(jnp.int32, (CHUNK, 128), 1) < HEAD_DIM
        row = lax.broadcasted_iota(jnp.int32, (4 * CHUNK, CHUNK), 0)
        qi = row % CHUNK
        ki = lax.broadcasted_iota(jnp.int32, (4 * CHUNK, CHUNK), 1)
        is_after = row >= 2 * CHUNK
        mask = (is_after & (ki >= qi)) | (jnp.logical_not(is_after) & (qi >= ki))
        mask_last = mask & jnp.logical_or(jnp.logical_not(is_after), n < n_tiles - 1)
        dq_acc = [carry_ref[...]] + [jnp.zeros((CHUNK, 128), F32) for _ in range(nb)]

        def per_head(x):
            other = pltpu.roll(x, HEAD_DIM, 1)
            return [jnp.where(head_a, x, other), jnp.where(head_a, other, x)]

        for j in range(nb):
            rows = slice(j * CHUNK, (j + 1) * CHUNK)
            kj = k_ref[rows, :]
            vj = v_ref[rows, :]
            if j + 1 < nb:
                nrows = slice((j + 1) * CHUNK, (j + 2) * CHUNK)
                q2, do2, l2, dd2, msk = q_ref[nrows, :], do_ref[nrows, :], l_ref[nrows, :], dd_ref[nrows, :], mask
            else:
                q2, do2, l2, dd2, msk = qn_ref[...], don_ref[...], ln_ref[...], ddn_ref[...], mask_last
            qs = jnp.concatenate(_both_heads(q_ref[rows, :], head_a) + _both_heads(q2, head_a), axis=0)
            dos = jnp.concatenate(_both_heads(do_ref[rows, :], head_a) + _both_heads(do2, head_a), axis=0)
            ls = jnp.concatenate(per_head(l_ref[rows, :]) + per_head(l2), axis=0)
            dds = jnp.concatenate(per_head(dd_ref[rows, :]) + per_head(dd2), axis=0)
            s = lax.dot_general(qs, kj, _NT, preferred_element_type=F32)
            p = jnp.exp(jnp.where(msk, s - ls, NEG))
            dp = lax.dot_general(dos, vj, _NT, preferred_element_type=F32)
            ds = (p * (dp - dds)).astype(BF16)
            dv_ref[rows, :] = lax.dot_general(p.astype(BF16), dos, _TN, preferred_element_type=F32).astype(BF16)
            dk_ref[rows, :] = lax.dot_general(ds, qs, _TN, preferred_element_type=F32).astype(BF16)
            dqs = jnp.dot(ds, kj, preferred_element_type=F32)
            dq_acc[j] = dq_acc[j] + jnp.where(head_a, dqs[:CHUNK], dqs[CHUNK:2 * CHUNK])
            dq_acc[j + 1] = dq_acc[j + 1] + jnp.where(head_a, dqs[2 * CHUNK:3 * CHUNK], dqs[3 * CHUNK:])
        for j in range(nb):
            dq_ref[j * CHUNK:(j + 1) * CHUNK, :] = dq_acc[j].astype(BF16)
        carry_ref[...] = dq_acc[nb]

    same = lambda n: n
    grad = jax.ShapeDtypeStruct((d, N_PAIRS, sd, 128), BF16)
    return _call(
        _per_pair(one_pair), name=f"attn_bwd_d{d}", grid=(d, N_PAIRS // PAIRS_PER_STEP, n_tiles),
        in_specs=[_attn_spec(0, tile, same), _attn_spec(0, CHUNK, nxt), _attn_spec(N_PAIRS, tile, same),
                  _attn_spec(2 * N_PAIRS, tile, same), _attn_spec(0, tile, same), _attn_spec(0, CHUNK, nxt),
                  _attn_spec(0, tile, same), _attn_spec(0, CHUNK, nxt), _attn_spec(0, tile, same), _attn_spec(0, CHUNK, nxt)],
        out_specs=[_attn_spec(0, tile, same)] * 3,
        out_shape=[grad, grad, grad],
        scratch_shapes=[pltpu.VMEM((PAIRS_PER_STEP, CHUNK, 128), F32)],
        semantics=("parallel", "parallel", "arbitrary"), args=(qkv, qkv, qkv, qkv, do, do, lse, lse, dd, dd), exchange=exchange)


def _sgu_bwd(ua, sw, b2, gs, ga, dya_n):
    s_len = ua.shape[0]
    tm = ROW_TILE

    def body(ua_ref, sw_ref, b2_ref, gs_ref, ga_ref, dy_ref, dua_ref, dsw_ref, db2_ref, dgs_ref, dga_ref):
        @pl.when(pl.program_id(0) == 0)
        def _():
            dsw_ref[...] = jnp.zeros_like(dsw_ref)
            db2_ref[...] = jnp.zeros_like(db2_ref)
            dgs_ref[...] = jnp.zeros_like(dgs_ref)
            dga_ref[...] = jnp.zeros_like(dga_ref)

        u, va, ug, xhat, rstd, vn = _sgu_core(ua_ref, gs_ref)
        wm, keep = _sgu_mix_weights(sw_ref)
        head = lax.broadcasted_iota(jnp.int32, (CHUNK, WIDTH_A), 1) // HEAD_DIM
        gav = ga_ref[...]
        gsv = gs_ref[...]
        dga = jnp.zeros((1, WIDTH_A), F32)
        dgs = jnp.zeros((1, WIDTH_A), F32)
        db2 = jnp.zeros((CHUNK, WIDTH_A), F32)
        dsw = [jnp.zeros((CHUNK, CHUNK), F32) for _ in range(4)]
        for c in range(tm // CHUNK):
            rows = slice(c * CHUNK, (c + 1) * CHUNK)
            vnc = vn[rows]
            vnb = vnc.astype(BF16)
            mixed = b2_ref[...]
            for h in range(4):
                mixed = mixed + jnp.dot(wm[h], jnp.where(head == h, vnc, 0.0).astype(BF16), preferred_element_type=F32)
            ugc = ug[rows]
            dya, dga_rows = _norm_bwd(dy_ref[rows, :], ugc * mixed, gav)
            dga = dga + jnp.sum(dga_rows, axis=0, keepdims=True)
            dmixed = dya * ugc
            db2 = db2 + dmixed
            dvn = jnp.zeros((CHUNK, WIDTH_A), F32)
            for h in range(4):
                dmh = jnp.where(head == h, dmixed, 0.0).astype(BF16)
                dsw[h] = dsw[h] + lax.dot_general(dmh, vnb, _NT, preferred_element_type=F32)
                dvn = dvn + lax.dot_general(wm[h], dmh, _TN, preferred_element_type=F32)
            xh = xhat[rows]
            dgs = dgs + jnp.sum(dvn * xh, axis=0, keepdims=True)
            dxh = dvn * gsv
            dvg = rstd[rows] * (dxh - jnp.mean(dxh, axis=-1, keepdims=True) - xh * jnp.mean(dxh * xh, axis=-1, keepdims=True))
            dua_ref[rows, :WIDTH_A] = (dya * mixed * _gelu_grad(u[rows])).astype(BF16)
            dua_ref[rows, WIDTH_A:] = (dvg * _gelu_grad(va[rows])).astype(BF16)
        for h in range(4):
            dsw_ref[h] += jnp.where(keep, dsw[h], 0.0)
        db2_ref[...] += db2
        dgs_ref[...] += dgs
        dga_ref[...] += dga

    return pl.pallas_call(
        body, name="sgu_bwd", grid=(s_len // tm,),
        in_specs=[_rows(tm, 2 * WIDTH_A), _whole((4, CHUNK, CHUNK)), _whole((CHUNK, WIDTH_A)), _whole((1, WIDTH_A)),
                  _whole((1, WIDTH_A)), _rows(tm, WIDTH_A)],
        out_specs=[_rows(tm, 2 * WIDTH_A), _whole((4, CHUNK, CHUNK)), _whole((CHUNK, WIDTH_A)), _whole((1, WIDTH_A)), _whole((1, WIDTH_A))],
        out_shape=[jax.ShapeDtypeStruct((s_len, 2 * WIDTH_A), BF16), jax.ShapeDtypeStruct((4, CHUNK, CHUNK), F32),
                   jax.ShapeDtypeStruct((CHUNK, WIDTH_A), F32), jax.ShapeDtypeStruct((1, WIDTH_A), F32),
                   jax.ShapeDtypeStruct((1, WIDTH_A), F32)],
        compiler_params=_params("arbitrary"),
    )(ua, sw, b2, gs, ga, dya_n)


def _dproj(dua, dqs, dks, dvs, cos, sin):
    s_len = dua.shape[0]
    tm = ROW_TILE
    n_br = len(DILATIONS)

    def body(dua_ref, *rest):
        groups = [rest[g * n_br:(g + 1) * n_br] for g in range(3)]
        cos_ref, sin_ref, out_ref, acc = rest[3 * n_br:]
        out_ref[:, :2 * WIDTH_A] = dua_ref[...]
        c = cos_ref[...]
        s = sin_ref[...]
        first_half = (lax.broadcasted_iota(jnp.int32, (tm, 128), 1) % HEAD_DIM) < HEAD_DIM // 2
        for g, refs in enumerate(groups):
            for cb in range(N_PAIRS):
                for i, d in enumerate(DILATIONS):
                    _from_sub(refs[i], cb, acc, 0, d, tm, accumulate=i > 0)
                t = acc[0]
                if g < 2:
                    t = (t * c - _swap_halves(t, first_half) * s) * (0.125 if g == 0 else 1.0)
                col = 2 * WIDTH_A + g * WIDTH_B + cb * 128
                out_ref[:, col:col + 128] = t.astype(BF16)

    subs = [_sub_spec(d, N_PAIRS, tm) for d in DILATIONS]
    return pl.pallas_call(
        body, name="dproj", grid=(s_len // tm,),
        in_specs=[_rows(tm, 2 * WIDTH_A)] + subs * 3 + [_rows(tm, 128), _rows(tm, 128)],
        out_specs=_rows(tm, IN_COLS),
        out_shape=jax.ShapeDtypeStruct((s_len, IN_COLS), BF16),
        scratch_shapes=[pltpu.VMEM((1, tm, 128), F32)],
        compiler_params=_params("parallel"),
    )(dua, *dqs, *dks, *dvs, cos, sin)


def _mm_tn(a, b, name, exchange=()):
    s_len, m = a.shape
    n = b.shape[1]
    tk = ROW_TILE
    tm = m if m <= 512 else (1408 if m == D_FF else 512)
    n_k = s_len // tk

    def body(a_ref, b_ref, o_ref, acc_ref):
        k = pl.program_id(1)

        @pl.when(k == 0)
        def _():
            acc_ref[...] = jnp.zeros_like(acc_ref)

        acc_ref[...] += lax.dot_general(a_ref[...].astype(BF16), b_ref[...].astype(BF16), _TN, preferred_element_type=F32)

        @pl.when(k == n_k - 1)
        def _():
            o_ref[...] = acc_ref[...].astype(BF16)

    (grad,), received = _call(
        body, name=name, grid=(m // tm, n_k),
        in_specs=[pl.BlockSpec((tk, tm), lambda i, k: (k, i)), pl.BlockSpec((tk, n), lambda i, k: (k, 0))],
        out_specs=[pl.BlockSpec((tm, n), lambda i, k: (i, 0))],
        out_shape=[jax.ShapeDtypeStruct((m, n), BF16)],
        scratch_shapes=[pltpu.VMEM((tm, n), F32)],
        semantics=("parallel", "arbitrary"), args=(a, b), exchange=exchange)
    return grad, received


def _position():
    x, y, c = lax.axis_index("x"), lax.axis_index("y"), lax.axis_index("c")
    return x, y, c, 4 * x + 2 * y + c


def _peer(x, y, c, rel):
    return (x ^ ((rel >> 2) & 1), y ^ ((rel >> 1) & 1), c ^ (rel & 1))


def _exchange_out_shape(kind, arr):
    return jax.ShapeDtypeStruct(((N_DEV,) + arr.shape) if kind == "gather" else arr.shape, arr.dtype)


def _exchange_sems(n_items):
    return [pltpu.SemaphoreType.DMA((n_items, N_DEV)), pltpu.SemaphoreType.DMA((n_items, N_DEV)), pltpu.SemaphoreType.DMA((n_items,))]


def _exchange_copies(kinds, srcs, dsts, sems):
    send_sems, recv_sems, local_sems = sems
    x, y, c, me = _position()
    local, sends, recvs = [], [], []
    for k, (kind, src, dst) in enumerate(zip(kinds, srcs, dsts)):
        own = src if kind == "gather" else src.at[me]
        local.append(pltpu.make_async_copy(own, dst.at[me], local_sems.at[k]))
        for rel in range(1, N_DEV):
            going = src if kind == "gather" else src.at[me ^ rel]
            common = dict(send_sem=send_sems.at[k, rel], recv_sem=recv_sems.at[k, rel],
                          device_id=_peer(x, y, c, rel), device_id_type=MESH)
            sends.append(pltpu.make_async_remote_copy(src_ref=going, dst_ref=dst.at[me], **common))
            recvs.append(pltpu.make_async_remote_copy(src_ref=own, dst_ref=dst.at[me ^ rel], **common))
    return local, sends, recvs


def _exchange_start(kinds, srcs, dsts, sems):
    local, sends, _ = _exchange_copies(kinds, srcs, dsts, sems)
    for cp in local + sends:
        cp.start()


def _exchange_finish(kinds, srcs, dsts, sems):
    local, sends, recvs = _exchange_copies(kinds, srcs, dsts, sems)
    for cp in recvs:
        cp.wait_recv()
    for cp in sends:
        cp.wait_send()
    for cp in local:
        cp.wait()


def _exchange_only(items, name):
    kinds = [k for k, _ in items]
    n = len(items)

    def body(*refs):
        srcs, dsts, sems = refs[:n], refs[n:2 * n], refs[2 * n:]
        _exchange_start(kinds, srcs, dsts, sems)
        _exchange_finish(kinds, srcs, dsts, sems)

    any_spec = pl.BlockSpec(memory_space=pl.ANY)
    return pl.pallas_call(
        body, name=name, in_specs=[any_spec] * n, out_specs=[any_spec] * n,
        out_shape=[_exchange_out_shape(k, a) for k, a in items],
        scratch_shapes=_exchange_sems(n),
        compiler_params=pltpu.CompilerParams(has_side_effects=True),
    )(*[a for _, a in items])


def _gather_two_level(shard, name):
    def body(src, dst, send_sems, recv_sems, local_sem):
        x, y, c, me = _position()
        sibling = (x, y, 1 - c)
        chips = [(1 - x, y), (x, 1 - y), (1 - x, 1 - y)]

        def block(px, py, pc):
            return dst.at[4 * px + 2 * py + pc]

        def copy(k, blk, to, src_ref=None):
            return pltpu.make_async_remote_copy(
                src_ref=block(*blk) if src_ref is None else src_ref, dst_ref=block(*blk),
                send_sem=send_sems.at[k], recv_sem=recv_sems.at[k], device_id=to, device_id_type=MESH)

        mine = pltpu.make_async_copy(src, dst.at[me], local_sem)
        mine.start()
        first = [copy(0, (x, y, c), sibling, src)] + [copy(1 + j, (x, y, c), (*chip, c), src) for j, chip in enumerate(chips)]
        for cp in first:
            cp.start()
        passed = [copy(4 + j, (*chip, c), sibling) for j, chip in enumerate(chips)]
        for j, chip in enumerate(chips):
            copy(1 + j, (*chip, c), (x, y, c)).wait_recv()
            passed[j].start()
        copy(0, (x, y, 1 - c), (x, y, c)).wait_recv()
        for j, chip in enumerate(chips):
            copy(4 + j, (*chip, 1 - c), (x, y, c)).wait_recv()
        for cp in first + passed:
            cp.wait_send()
        mine.wait()

    any_spec = pl.BlockSpec(memory_space=pl.ANY)
    return pl.pallas_call(
        body, name=name, in_specs=[any_spec], out_specs=any_spec,
        out_shape=_exchange_out_shape("gather", shard),
        scratch_shapes=[pltpu.SemaphoreType.DMA((N_DEV - 1,)), pltpu.SemaphoreType.DMA((N_DEV - 1,)), pltpu.SemaphoreType.DMA],
        compiler_params=pltpu.CompilerParams(has_side_effects=True),
    )(shard)


def _call(body, *, name, grid, in_specs, out_specs, out_shape, args, scratch_shapes=(), semantics, exchange=()):
    if not exchange:
        outs = pl.pallas_call(body, name=name, grid=grid, in_specs=in_specs, out_specs=out_specs, out_shape=out_shape,
                              scratch_shapes=list(scratch_shapes), compiler_params=_params(*semantics))(*args)
        return outs, []
    kinds = [k for k, _ in exchange]
    n_in, n_out, n_x, n_scr = len(in_specs), len(out_specs), len(exchange), len(scratch_shapes)

    def wrapped(*refs):
        ins, refs = refs[:n_in], refs[n_in:]
        srcs, refs = refs[:n_x], refs[n_x:]
        outs, refs = refs[:n_out], refs[n_out:]
        dsts, refs = refs[:n_x], refs[n_x:]
        scratch, sems = refs[:n_scr], refs[n_scr:]
        ids = [pl.program_id(a) for a in range(len(grid))]
        first = functools.reduce(jnp.logical_and, [i == 0 for i in ids])
        last = functools.reduce(jnp.logical_and, [i == g - 1 for i, g in zip(ids, grid)])

        @pl.when(first)
        def _():
            _exchange_start(kinds, srcs, dsts, sems)

        body(*ins, *outs, *scratch)

        @pl.when(last)
        def _():
            _exchange_finish(kinds, srcs, dsts, sems)

    any_spec = pl.BlockSpec(memory_space=pl.ANY)
    outs = pl.pallas_call(
        wrapped, name=name, grid=grid,
        in_specs=list(in_specs) + [any_spec] * n_x, out_specs=list(out_specs) + [any_spec] * n_x,
        out_shape=list(out_shape) + [_exchange_out_shape(k, a) for k, a in exchange],
        scratch_shapes=list(scratch_shapes) + _exchange_sems(n_x),
        compiler_params=pltpu.CompilerParams(dimension_semantics=("arbitrary",) * len(grid),
                                             vmem_limit_bytes=V7X_VMEM_LIMIT_BYTES, has_side_effects=True),
    )(*args, *[a for _, a in exchange])
    return outs[:n_out], outs[n_out:]


def _adamw_math(w, g, m, v):
    m = ADAM_B1 * m + (1.0 - ADAM_B1) * g
    v = ADAM_B2 * v + (1.0 - ADAM_B2) * (g * g)
    m_hat = m / (1.0 - ADAM_B1 ** ADAM_STEP)
    v_hat = v / (1.0 - ADAM_B2 ** ADAM_STEP)
    return -ADAM_LR * (m_hat / (jnp.sqrt(v_hat) + ADAM_EPS) + ADAM_WD * w), m, v


def _adamw(parts, w, m, v, name):
    rows, cols = w.shape
    tm = 256 if rows % 256 == 0 and rows > 256 else rows

    def body(p_ref, w_ref, m_ref, v_ref, g_ref, d_ref, nm_ref, nv_ref):
        g = p_ref[0].astype(F32)
        for j in range(1, N_DEV):
            g = g + p_ref[j].astype(F32)
        delta, nm, nv = _adamw_math(w_ref[...], g, m_ref[...], v_ref[...])
        g_ref[...] = g
        d_ref[...] = delta
        nm_ref[...] = nm
        nv_ref[...] = nv

    shard = jax.ShapeDtypeStruct((rows, cols), F32)
    return pl.pallas_call(
        body, name=name, grid=(rows // tm,),
        in_specs=[pl.BlockSpec((N_DEV, tm, cols), lambda i: (0, i, 0))] + [_rows(tm, cols)] * 3,
        out_specs=[_rows(tm, cols)] * 4,
        out_shape=[shard] * 4,
        compiler_params=_params("parallel"),
    )(parts, w, m, v)


_SMALL = ("mix_norm_g", "sgu_w", "sgu_b", "sgu_norm_g", "out_norm_a", "out_norm_b", "ffn_norm_g", "ple_norm_g", "final_norm_g")
_BIG = ("w_in", "w_out", "w_gate", "w_up", "w_down", "w_ple_gate", "w_ple_proj")
_COLUMN_SHARDED = ("w_in", "w_gate", "w_up", "w_ple_proj")
_ORDER = ("mix_norm_g", "w_in", "sgu_w", "sgu_b", "sgu_norm_g", "out_norm_a", "out_norm_b", "w_out", "ffn_norm_g",
          "w_gate", "w_up", "w_down", "ple_norm_g", "w_ple_gate", "w_ple_proj", "final_norm_g")


def _pack_small(values, names=_SMALL):
    flat = jnp.concatenate([values[n].reshape(-1).astype(F32) for n in names])
    pad = (-flat.shape[0]) % (8 * 128)
    return jnp.pad(flat, (0, pad)).reshape(-1, 128)


def _unpack_small(packed, like):
    flat = packed.reshape(-1)
    out, at = {}, 0
    for n in _SMALL:
        size = like[n].size
        out[n] = flat[at:at + size].reshape(like[n].shape)
        at += size
    return out


def _own_orientation(name, value):
    return value[0].T if name in _COLUMN_SHARDED else value[0]


def _reference_orientation(name, value):
    return (value.T if name in _COLUMN_SHARDED else value)[None]


def _full_from_gathered(gathered):
    return gathered.reshape(N_DEV * gathered.shape[1], gathered.shape[2])


def _sliced_for_devices(grad):
    return grad.reshape(N_DEV, grad.shape[0] // N_DEV, grad.shape[1])


def _rope_tables(s_len):
    half = HEAD_DIM // 2
    inv = ROPE_THETA ** (-jnp.arange(half, dtype=F32) / half)
    ang = jnp.arange(s_len, dtype=F32)[:, None] * inv[None, :]
    cos, sin = jnp.cos(ang), jnp.sin(ang)
    return jnp.concatenate([cos, cos, cos, cos], axis=1), jnp.concatenate([-sin, sin, -sin, sin], axis=1)


def _forward_backward(x, p, target, small, shards):
    def gather(*names):
        return [("gather", shards[n]) for n in names]

    def scatter(**grads):
        return [("scatter", _sliced_for_devices(g)) for g in grads.values()]

    full, parts = {}, {}
    full["w_in"] = _full_from_gathered(_gather_two_level(shards["w_in"], "gather_w_in"))

    s_len = x.shape[0]
    cos, sin = _rope_tables(s_len)
    g_mix, g_ffn, g_ple = small["mix_norm_g"], small["ffn_norm_g"], small["ple_norm_g"]
    g_fin = small["final_norm_g"].reshape(1, D_MODEL)
    sw, gs, ga, gb = small["sgu_w"], small["sgu_norm_g"], small["out_norm_a"], small["out_norm_b"]
    b2 = jnp.repeat(small["sgu_b"].T, HEAD_DIM, axis=1)
    lane_head = jnp.arange(128) // HEAD_DIM
    head_sum = (lane_head[:, None] == lane_head[None, :]).astype(BF16)
    n_br = len(DILATIONS)

    def arrived(names, got):
        for n, g in zip(names, got):
            full[n] = _full_from_gathered(g)

    (ua, hn1, *qkv), got = _inproj(x, g_mix, full["w_in"], cos, sin, exchange=gather("w_gate"))
    arrived(("w_gate",), got)
    ya_n = _sgu_fwd(ua, sw, b2, gs, ga)
    half = shards["w_up"].shape[0] // 2
    riders = [[("gather", shards["w_up"][:half])], [("gather", shards["w_up"][half:])], gather("w_out")]
    branch, got = [], []
    for i, d in enumerate(DILATIONS):
        o_l, g = _attn_fwd(qkv[i], d, exchange=riders[i])
        branch.append(o_l)
        got += g
    arrived(("w_up", "w_out"), [jnp.concatenate(got[:2], axis=1), got[2]])
    (y, yb, *lse), _ = _combine([o for o, _ in branch], [l for _, l in branch], ya_n, gb)
    last_wave = ("w_down", "w_ple_gate", "w_ple_proj")
    (h1, a, b, act, hn2), got = _ffn_up(y, full["w_out"], x, g_ffn, full["w_gate"], full["w_up"], exchange=gather(*last_wave))
    arrived(last_wave, got)
    h2, h3, gate, pp, hn3 = _ffn_down_ple(act, full["w_down"], h1, g_ple, full["w_ple_gate"], p, full["w_ple_proj"])

    dh2, loss, d_fin, d_ple, g_ple_gate, g_ple_proj = _loss_ple_bwd(
        h3, target, g_fin, gate, pp, h2, g_ple, full["w_ple_gate"], hn3, p)
    g_down, (parts["w_ple_gate"], parts["w_ple_proj"]) = _mm_tn(
        act, dh2, "dw_down", exchange=scatter(w_ple_gate=g_ple_gate, w_ple_proj=g_ple_proj))
    (da, db), (parts["w_down"],) = _ffn_down_bwd(dh2, full["w_down"], a, b, exchange=scatter(w_down=g_down))
    g_gate, _ = _mm_tn(da, hn2, "dw_gate")
    g_up, _ = _mm_tn(db, hn2, "dw_up")
    (dh1, d_ffn, g_out), (parts["w_gate"],) = _mm_norm_bwd(
        [(da, full["w_gate"]), (db, full["w_up"])], h1, g_ffn, dh2, "ffn_up_bwd", exchange=scatter(w_gate=g_gate), dw_lhs=y)
    (dya_n, d_gb, *do_dd), (parts["w_out"],) = _outproj_bwd(dh1, full["w_out"], yb, gb, head_sum, exchange=scatter(w_out=g_out))
    grads_b = []
    for i, d in enumerate(DILATIONS):
        g3, got = _attn_bwd(qkv[i], do_dd[i], lse[i], do_dd[n_br + i], d, exchange=scatter(w_up=g_up) if i == 0 else ())
        grads_b.append(g3)
        if i == 0:
            (parts["w_up"],) = got
    dua, d_sw, d_b2, d_gs, d_ga = _sgu_bwd(ua, sw, b2, gs, ga, dya_n)
    dproj = _dproj(dua, [g[0] for g in grads_b], [g[1] for g in grads_b], [g[2] for g in grads_b], cos, sin)
    early = {
        "sgu_w": d_sw, "sgu_b": d_b2.reshape(CHUNK, 4, HEAD_DIM).sum(axis=-1).T, "sgu_norm_g": d_gs, "out_norm_a": d_ga,
        "out_norm_b": d_gb, "ffn_norm_g": d_ffn, "ple_norm_g": d_ple, "final_norm_g": d_fin,
    }
    g_in, (early_parts,) = _mm_tn(dproj, hn1, "dw_in", exchange=[("gather", _pack_small(early, _SMALL[1:]))])
    (dx, d_mix), (parts["w_in"],) = _mm_norm_bwd(
        [(dproj, full["w_in"])], x, g_mix, dh1, "inproj_bwd", exchange=scatter(w_in=g_in))
    late = jnp.concatenate([_pack_small({"mix_norm_g": d_mix}, _SMALL[:1]), jnp.broadcast_to(loss, (8, 128))])
    (late_parts,) = _exchange_only([("gather", late)], "gather_mix_norm_grad_and_loss")
    total_loss = jnp.sum(late_parts[:, 8, 0])
    return total_loss, dx, parts, jnp.concatenate([late_parts[:, :8], early_parts], axis=1)


def kernel(x, p, mix_norm_g, w_in, sgu_w, sgu_b, sgu_norm_g, out_norm_a, out_norm_b, w_out, ffn_norm_g, w_gate, w_up, w_down, ple_norm_g, w_ple_gate, w_ple_proj, final_norm_g, loss_target, m_mix_norm_g, m_w_in, m_sgu_w, m_sgu_b, m_sgu_norm_g, m_out_norm_a, m_out_norm_b, m_w_out, m_ffn_norm_g, m_w_gate, m_w_up, m_w_down, m_ple_norm_g, m_w_ple_gate, m_w_ple_proj, m_final_norm_g, v_mix_norm_g, v_w_in, v_sgu_w, v_sgu_b, v_sgu_norm_g, v_out_norm_a, v_out_norm_b, v_w_out, v_ffn_norm_g, v_w_gate, v_w_up, v_w_down, v_ple_norm_g, v_w_ple_gate, v_w_ple_proj, v_final_norm_g):
    given = dict(locals())
    weights = {n: given[n] for n in _ORDER}
    moments_m = {n: given["m_" + n] for n in _ORDER}
    moments_v = {n: given["v_" + n] for n in _ORDER}

    shards = {n: _own_orientation(n, weights[n]).astype(BF16) for n in _BIG}
    small = {n: (weights[n][0] if n in ("sgu_w", "sgu_b") else weights[n]) for n in _SMALL}

    loss, dx, parts, small_parts = _forward_backward(x[0], p[0, 0], loss_target[0], small, shards)

    small_like = {n: weights[n] for n in _SMALL}
    grads, deltas, new_m, new_v = {}, {}, {}, {}
    for n in _BIG:
        outs = _adamw(parts[n], _own_orientation(n, weights[n]), _own_orientation(n, moments_m[n]),
                      _own_orientation(n, moments_v[n]), "adamw_" + n)
        grads[n], deltas[n], new_m[n], new_v[n] = [_reference_orientation(n, o) for o in outs]
    g, d, nm, nv = _adamw(small_parts, _pack_small(small_like), _pack_small({n: moments_m[n] for n in _SMALL}),
                          _pack_small({n: moments_v[n] for n in _SMALL}), "adamw_small")
    for out, packed in ((grads, g), (deltas, d), (new_m, nm), (new_v, nv)):
        out.update(_unpack_small(packed, small_like))

    return (loss, dx[None], *[grads[n] for n in _ORDER], *[deltas[n] for n in _ORDER],
            *[new_m[n] for n in _ORDER], *[new_v[n] for n in _ORDER])
```

```python
import functools

import jax
import jax.numpy as jnp
from jax import lax
from jax.experimental import pallas as pl
from jax.experimental.pallas import tpu as pltpu

F32 = jnp.float32
BF16 = jnp.bfloat16

D_MODEL = 1024
WIDTH_A = 256
WIDTH_B = 768
D_FF = 2816
IN_COLS = 2 * WIDTH_A + 3 * WIDTH_B
PLE_DIM = 256
HEAD_DIM = 64
N_PAIRS = WIDTH_B // 128
CHUNK = 128
N_BACK = 128
DILATIONS = (1, 4, 16)
ROPE_THETA = 10000.0
EPS = 1e-6
N_DEV = 8

ADAM_LR = 0.001
ADAM_B1 = 0.9
ADAM_B2 = 0.999
ADAM_EPS = 1e-08
ADAM_WD = 0.01
ADAM_STEP = 10

V7X_VMEM_LIMIT_BYTES = 56 * 1024 * 1024
ROW_TILE = 512
MESH = pl.DeviceIdType.MESH
NEG = -1e30

_NT = (((1,), (1,)), ((), ()))
_TN = (((0,), (0,)), ((), ()))


def _params(*semantics):
    return pltpu.CompilerParams(dimension_semantics=semantics, vmem_limit_bytes=V7X_VMEM_LIMIT_BYTES)


def _rows(tm, width):
    return pl.BlockSpec((tm, width), lambda i: (i, 0))


def _whole(shape):
    return pl.BlockSpec(shape, lambda *_: (0,) * len(shape))


def _gelu(x):
    t = jnp.tanh(0.7978845608028654 * (x + 0.044715 * (x * x * x)))
    return 0.5 * x * (1.0 + t)


def _gelu_grad(x):
    t = jnp.tanh(0.7978845608028654 * (x + 0.044715 * (x * x * x)))
    return 0.5 * (1.0 + t) + 0.5 * x * (1.0 - t * t) * (0.7978845608028654 * (1.0 + 3.0 * 0.044715 * (x * x)))


def _rstd(x):
    return lax.rsqrt(jnp.mean(x * x, axis=-1, keepdims=True) + EPS)


def _norm_bwd(dn, h, g):
    r = _rstd(h)
    n = h * r
    t = dn * g
    return r * (t - n * jnp.mean(t * n, axis=-1, keepdims=True)), dn * n


def _swap_halves(x, first_half):
    return jnp.where(first_half, pltpu.roll(x, 96, 1), pltpu.roll(x, 32, 1))


def _sub_spec(d, n_cb, tm):
    return pl.BlockSpec((d, n_cb, tm // d, 128), lambda i: (0, 0, i, 0))


def _sub_shape(s_len, d, n_cb, dtype):
    return jax.ShapeDtypeStruct((d, n_cb, s_len // d, 128), dtype)


def _to_sub(stage_ref, cb_src, out_ref, cb_dst, d, tm):
    slab = stage_ref.at[cb_src]
    for r in range(d):
        out_ref[r, cb_dst] = slab[pl.ds(r, tm // d, stride=d), :].astype(out_ref.dtype)


def _from_sub(in_ref, cb_src, stage_ref, cb_dst, d, tm, accumulate=False):
    slab = stage_ref.at[cb_dst]
    for r in range(d):
        rows = pl.ds(r, tm // d, stride=d)
        val = in_ref[r, cb_src].astype(F32)
        slab[rows, :] = slab[rows, :] + val if accumulate else val


def _inproj(x, g, w, cos, sin, exchange=()):
    s_len = x.shape[0]
    tm = ROW_TILE
    n_cb = 3 * N_PAIRS

    def body(x_ref, g_ref, w_ref, cos_ref, sin_ref, ua_ref, hn_ref, *rest):
        sub_refs, stage = rest[:-1], rest[-1]
        xf = x_ref[...]
        hn = (xf * _rstd(xf) * g_ref[...]).astype(BF16)
        hn_ref[...] = hn
        c = cos_ref[...]
        s = sin_ref[...]
        first_half = (lax.broadcasted_iota(jnp.int32, (tm, 128), 1) % HEAD_DIM) < HEAD_DIM // 2
        for j in range(IN_COLS // 256):
            col = j * 256
            acc = lax.dot_general(hn, w_ref[col:col + 256, :], _NT, preferred_element_type=F32)
            if col < 2 * WIDTH_A:
                ua_ref[:, col:col + 256] = acc
                continue
            for half in range(2):
                cb = (col - 2 * WIDTH_A) // 128 + half
                t = acc[:, half * 128:(half + 1) * 128]
                if cb < 2 * N_PAIRS:
                    t = (t * c + _swap_halves(t, first_half) * s) * (0.125 if cb < N_PAIRS else 1.0)
                stage[cb] = t
        for cb in range(n_cb):
            for d, out_ref in zip(DILATIONS, sub_refs):
                _to_sub(stage, cb, out_ref, cb, d, tm)

    return _call(
        body, name="inproj", grid=(s_len // tm,),
        in_specs=[_rows(tm, D_MODEL), _whole((1, D_MODEL)), _whole((IN_COLS, D_MODEL)), _rows(tm, 128), _rows(tm, 128)],
        out_specs=[_rows(tm, 2 * WIDTH_A), _rows(tm, D_MODEL)] + [_sub_spec(d, n_cb, tm) for d in DILATIONS],
        out_shape=[jax.ShapeDtypeStruct((s_len, 2 * WIDTH_A), F32), jax.ShapeDtypeStruct((s_len, D_MODEL), BF16)]
        + [_sub_shape(s_len, d, n_cb, BF16) for d in DILATIONS],
        scratch_shapes=[pltpu.VMEM((n_cb, tm, 128), F32)],
        semantics=("parallel",), args=(x, g, w, cos, sin), exchange=exchange)


def _sgu_mix_weights(sw_ref):
    keep = lax.broadcasted_iota(jnp.int32, (CHUNK, CHUNK), 0) >= lax.broadcasted_iota(jnp.int32, (CHUNK, CHUNK), 1)
    return [jnp.where(keep, sw_ref[h], 0.0).astype(BF16) for h in range(4)], keep


def _sgu_core(ua_ref, gs_ref):
    u = ua_ref[:, :WIDTH_A]
    va = ua_ref[:, WIDTH_A:]
    vg = _gelu(va)
    xc = vg - jnp.mean(vg, axis=-1, keepdims=True)
    rstd = lax.rsqrt(jnp.mean(xc * xc, axis=-1, keepdims=True) + EPS)
    xhat = xc * rstd
    return u, va, _gelu(u), xhat, rstd, xhat * gs_ref[...]


def _sgu_fwd(ua, sw, b2, gs, ga):
    s_len = ua.shape[0]
    tm = ROW_TILE

    def body(ua_ref, sw_ref, b2_ref, gs_ref, ga_ref, out_ref):
        _, _, ug, _, _, vn = _sgu_core(ua_ref, gs_ref)
        wm, _ = _sgu_mix_weights(sw_ref)
        head = lax.broadcasted_iota(jnp.int32, (CHUNK, WIDTH_A), 1) // HEAD_DIM
        for c in range(tm // CHUNK):
            rows = slice(c * CHUNK, (c + 1) * CHUNK)
            vnc = vn[rows]
            mixed = b2_ref[...]
            for h in range(4):
                mixed = mixed + jnp.dot(wm[h], jnp.where(head == h, vnc, 0.0).astype(BF16), preferred_element_type=F32)
            ya = ug[rows] * mixed
            out_ref[rows, :] = (ya * _rstd(ya) * ga_ref[...]).astype(BF16)

    return pl.pallas_call(
        body, name="sgu_fwd", grid=(s_len // tm,),
        in_specs=[_rows(tm, 2 * WIDTH_A), _whole((4, CHUNK, CHUNK)), _whole((CHUNK, WIDTH_A)), _whole((1, WIDTH_A)), _whole((1, WIDTH_A))],
        out_specs=_rows(tm, WIDTH_A),
        out_shape=jax.ShapeDtypeStruct((s_len, WIDTH_A), BF16),
        compiler_params=_params("parallel"),
    )(ua, sw, b2, gs, ga)


def _attn_geometry(sd):
    tile = min(ROW_TILE, sd)
    return tile, tile // CHUNK, sd // tile


PAIRS_PER_STEP = 6


def _attn_spec(cb0, rows, row_index):
    return pl.BlockSpec((None, PAIRS_PER_STEP, rows, 128), lambda r, g, n: (r, cb0 // PAIRS_PER_STEP + g, row_index(n), 0))


def _per_pair(one_pair):
    def body(*refs):
        for hp in range(PAIRS_PER_STEP):
            one_pair(*[ref.at[hp] for ref in refs])
    return body


def _both_heads(x, head_a):
    zero = jnp.zeros_like(x)
    return [jnp.where(head_a, x, zero), jnp.where(head_a, zero, x)]


def _attn_fwd(qkv, d, exchange=()):
    sd = qkv.shape[2]
    tile, nb, n_tiles = _attn_geometry(sd)

    def prev(n):
        return jnp.maximum(n * nb - 1, 0)

    def one_pair(q_ref, k_ref, kp_ref, v_ref, vp_ref, o_ref, l_ref):
        n = pl.program_id(2)
        head_a = lax.broadcasted_iota(jnp.int32, (CHUNK, 128), 1) < HEAD_DIM
        qi = lax.broadcasted_iota(jnp.int32, (2 * CHUNK, 2 * CHUNK), 0) % CHUNK
        kc = lax.broadcasted_iota(jnp.int32, (2 * CHUNK, 2 * CHUNK), 1)
        band = (kc >= qi) & (kc <= qi + N_BACK)
        for j in range(nb):
            rows = slice(j * CHUNK, (j + 1) * CHUNK)
            if j == 0:
                kcat = jnp.concatenate([kp_ref[...], k_ref[rows, :]], axis=0)
                vcat = jnp.concatenate([vp_ref[...], v_ref[rows, :]], axis=0)
                valid = band & jnp.logical_or(n > 0, kc >= CHUNK)
            else:
                kcat = k_ref[(j - 1) * CHUNK:(j + 1) * CHUNK, :]
                vcat = v_ref[(j - 1) * CHUNK:(j + 1) * CHUNK, :]
                valid = band
            q2 = jnp.concatenate(_both_heads(q_ref[rows, :], head_a), axis=0)
            s = lax.dot_general(q2, kcat, _NT, preferred_element_type=F32)
            s = jnp.where(valid, s, NEG)
            m = jnp.max(s, axis=-1, keepdims=True)
            p = jnp.exp(s - m)
            l = jnp.sum(p, axis=-1, keepdims=True)
            o2 = jnp.dot(p.astype(BF16), vcat, preferred_element_type=F32) / l
            lse2 = m + jnp.log(l)
            o_ref[rows, :] = jnp.where(head_a, o2[:CHUNK], o2[CHUNK:]).astype(BF16)
            l_ref[rows, :] = jnp.where(head_a, lse2[:CHUNK], lse2[CHUNK:])

    same = lambda n: n
    return _call(
        _per_pair(one_pair), name=f"attn_fwd_d{d}", grid=(d, N_PAIRS // PAIRS_PER_STEP, n_tiles),
        in_specs=[_attn_spec(0, tile, same), _attn_spec(N_PAIRS, tile, same), _attn_spec(N_PAIRS, CHUNK, prev),
                  _attn_spec(2 * N_PAIRS, tile, same), _attn_spec(2 * N_PAIRS, CHUNK, prev)],
        out_specs=[_attn_spec(0, tile, same), _attn_spec(0, tile, same)],
        out_shape=[jax.ShapeDtypeStruct((d, N_PAIRS, sd, 128), BF16), jax.ShapeDtypeStruct((d, N_PAIRS, sd, 128), F32)],
        semantics=("parallel", "parallel", "parallel"), args=(qkv, qkv, qkv, qkv, qkv), exchange=exchange)


def _combine(outs, lses, ya_n, gb, exchange=()):
    s_len = ya_n.shape[0]
    tm = ROW_TILE
    n_br = len(DILATIONS)

    def body(*refs):
        o_refs, l_refs = refs[:n_br], refs[n_br:2 * n_br]
        ya_ref, gb_ref, y_ref, yb_ref = refs[2 * n_br:2 * n_br + 4]
        lse_refs = refs[2 * n_br + 4:3 * n_br + 4]
        o_nat, l_nat, lse_nat = refs[3 * n_br + 4:]
        sumsq = jnp.zeros((tm, 1), F32)
        for cb in range(N_PAIRS):
            for i, d in enumerate(DILATIONS):
                _from_sub(o_refs[i], cb, o_nat, i, d, tm)
                _from_sub(l_refs[i], cb, l_nat, i, d, tm)
            ls = [l_nat[i] for i in range(n_br)]
            top = jnp.maximum(jnp.maximum(ls[0], ls[1]), ls[2])
            ws = [jnp.exp(l - top) for l in ls]
            den = ws[0] + ws[1] + ws[2]
            inv = 1.0 / den
            yb = (ws[0] * inv) * o_nat[0] + (ws[1] * inv) * o_nat[1] + (ws[2] * inv) * o_nat[2]
            yb_ref[:, cb * 128:(cb + 1) * 128] = yb
            sumsq = sumsq + jnp.sum(yb * yb, axis=-1, keepdims=True)
            lse_nat[cb] = top + jnp.log(den)
            for d, lse_ref in zip(DILATIONS, lse_refs):
                _to_sub(lse_nat, cb, lse_ref, cb, d, tm)
        r = lax.rsqrt(sumsq / WIDTH_B + EPS)
        y_ref[:, :WIDTH_A] = ya_ref[...]
        y_ref[:, WIDTH_A:] = (yb_ref[...] * r * gb_ref[...]).astype(BF16)

    return _call(
        body, name="attn_combine", grid=(s_len // tm,),
        in_specs=[_sub_spec(d, N_PAIRS, tm) for d in DILATIONS] * 2 + [_rows(tm, WIDTH_A), _whole((1, WIDTH_B))],
        out_specs=[_rows(tm, D_MODEL), _rows(tm, WIDTH_B)] + [_sub_spec(d, N_PAIRS, tm) for d in DILATIONS],
        out_shape=[jax.ShapeDtypeStruct((s_len, D_MODEL), BF16), jax.ShapeDtypeStruct((s_len, WIDTH_B), F32)]
        + [_sub_shape(s_len, d, N_PAIRS, F32) for d in DILATIONS],
        scratch_shapes=[pltpu.VMEM((n_br, tm, 128), F32), pltpu.VMEM((n_br, tm, 128), F32), pltpu.VMEM((N_PAIRS, tm, 128), F32)],
        semantics=("parallel",), args=(*outs, *lses, ya_n, gb), exchange=exchange)


def _ffn_up(y, wout, x, g, wg, wu, exchange=()):
    s_len = x.shape[0]
    tm = ROW_TILE // 2

    def body(y_ref, wo_ref, x_ref, g_ref, wg_ref, wu_ref, h_ref, a_ref, b_ref, act_ref, hn_ref):
        hf = x_ref[...] + jnp.dot(y_ref[...], wo_ref[...], preferred_element_type=F32)
        h_ref[...] = hf
        hn = (hf * _rstd(hf) * g_ref[...]).astype(BF16)
        hn_ref[...] = hn
        for j in range(D_FF // 256):
            cols = slice(j * 256, (j + 1) * 256)
            a = lax.dot_general(hn, wg_ref[cols, :], _NT, preferred_element_type=F32)
            b = lax.dot_general(hn, wu_ref[cols, :], _NT, preferred_element_type=F32)
            a_ref[:, cols] = a.astype(BF16)
            b_ref[:, cols] = b.astype(BF16)
            act_ref[:, cols] = (a * jax.nn.sigmoid(a) * b).astype(BF16)

    wide = jax.ShapeDtypeStruct((s_len, D_FF), BF16)
    return _call(
        body, name="ffn_up", grid=(s_len // tm,),
        in_specs=[_rows(tm, D_MODEL), _whole((D_MODEL, D_MODEL)), _rows(tm, D_MODEL), _whole((1, D_MODEL)),
                  _whole((D_FF, D_MODEL)), _whole((D_FF, D_MODEL))],
        out_specs=[_rows(tm, D_MODEL), _rows(tm, D_FF), _rows(tm, D_FF), _rows(tm, D_FF), _rows(tm, D_MODEL)],
        out_shape=[jax.ShapeDtypeStruct((s_len, D_MODEL), F32), wide, wide, wide, jax.ShapeDtypeStruct((s_len, D_MODEL), BF16)],
        semantics=("parallel",), args=(y, wout, x, g, wg, wu), exchange=exchange)


def _ffn_down_ple(act, wd, h1, g, wpg, p, wpp):
    s_len = h1.shape[0]
    tm = ROW_TILE // 2

    def body(act_ref, wd_ref, h1_ref, g_ref, wpg_ref, p_ref, wpp_ref, h2_ref, h3_ref, gate_ref, pp_ref, hn_ref):
        hf = h1_ref[...] + jnp.dot(act_ref[...], wd_ref[...], preferred_element_type=F32)
        h2_ref[...] = hf
        hn = (hf * _rstd(hf) * g_ref[...]).astype(BF16)
        hn_ref[...] = hn
        gate = jax.nn.sigmoid(jnp.dot(hn, wpg_ref[...], preferred_element_type=F32))
        pp = lax.dot_general(p_ref[...].astype(BF16), wpp_ref[...], _NT, preferred_element_type=F32)
        h3_ref[...] = hf + gate * pp
        gate_ref[...] = gate.astype(BF16)
        pp_ref[...] = pp.astype(BF16)

    full = jax.ShapeDtypeStruct((s_len, D_MODEL), F32)
    half = jax.ShapeDtypeStruct((s_len, D_MODEL), BF16)
    return pl.pallas_call(
        body, name="ffn_down_ple", grid=(s_len // tm,),
        in_specs=[_rows(tm, D_FF), _whole((D_FF, D_MODEL)), _rows(tm, D_MODEL), _whole((1, D_MODEL)),
                  _whole((D_MODEL, D_MODEL)), _rows(tm, PLE_DIM), _whole((D_MODEL, PLE_DIM))],
        out_specs=[_rows(tm, D_MODEL)] * 5,
        out_shape=[full, full, half, half, half],
        compiler_params=_params("parallel"),
    )(act, wd, h1, g, wpg, p, wpp)


def _loss_ple_bwd(h3, target, gf, gate, pp, h2, g_ple, wpg, hn3, p):
    s_len = h3.shape[0]
    tm = ROW_TILE
    n_steps = s_len // tm

    def body(h_ref, t_ref, g_ref, gate_ref, pp_ref, h2_ref, gp_ref, w_ref, hn_ref, p_ref,
             dh2_ref, loss_ref, dg_ref, dgp_ref, dwg_ref, dwp_ref, acc_g, acc_p):
        step = pl.program_id(0)

        @pl.when(step == 0)
        def _():
            loss_ref[...] = jnp.zeros_like(loss_ref)
            dg_ref[...] = jnp.zeros_like(dg_ref)
            dgp_ref[...] = jnp.zeros_like(dgp_ref)
            acc_g[...] = jnp.zeros_like(acc_g)
            acc_p[...] = jnp.zeros_like(acc_p)

        hf = h_ref[...]
        gfv = g_ref[...]
        err = hf * _rstd(hf) * gfv - t_ref[...]
        loss_ref[...] += 0.5 * jnp.sum(jnp.sum(err * err, axis=-1, keepdims=True), axis=0, keepdims=True) / D_MODEL
        dh, dg_rows = _norm_bwd(err / D_MODEL, hf, gfv)
        dg_ref[...] += jnp.sum(dg_rows, axis=0, keepdims=True)
        gate = gate_ref[...].astype(F32)
        dz = (dh * pp_ref[...].astype(F32) * gate * (1.0 - gate)).astype(BF16)
        dpp = (dh * gate).astype(BF16)
        dn = lax.dot_general(dz, w_ref[...], _NT, preferred_element_type=F32)
        dh2, dgp_rows = _norm_bwd(dn, h2_ref[...], gp_ref[...])
        dh2 = dh + dh2
        dh2_ref[...] = dh2
        dgp_ref[...] += jnp.sum(dgp_rows, axis=0, keepdims=True)
        acc_g[...] += lax.dot_general(hn_ref[...], dz, _TN, preferred_element_type=F32)
        acc_p[...] += lax.dot_general(dpp, p_ref[...].astype(BF16), _TN, preferred_element_type=F32)

        @pl.when(step == n_steps - 1)
        def _():
            dwg_ref[...] = acc_g[...].astype(BF16)
            dwp_ref[...] = acc_p[...].astype(BF16)

    gain = jax.ShapeDtypeStruct((1, D_MODEL), F32)
    return pl.pallas_call(
        body, name="loss_ple_bwd", grid=(n_steps,),
        in_specs=[_rows(tm, D_MODEL), _rows(tm, D_MODEL), _whole((1, D_MODEL)), _rows(tm, D_MODEL), _rows(tm, D_MODEL),
                  _rows(tm, D_MODEL), _whole((1, D_MODEL)), _whole((D_MODEL, D_MODEL)), _rows(tm, D_MODEL),
                  _rows(tm, PLE_DIM)],
        out_specs=[_rows(tm, D_MODEL), _whole((1, 128)), _whole((1, D_MODEL)), _whole((1, D_MODEL)),
                   _whole((D_MODEL, D_MODEL)), _whole((D_MODEL, PLE_DIM))],
        out_shape=[jax.ShapeDtypeStruct((s_len, D_MODEL), F32), jax.ShapeDtypeStruct((1, 128), F32), gain, gain,
                   jax.ShapeDtypeStruct((D_MODEL, D_MODEL), BF16), jax.ShapeDtypeStruct((D_MODEL, PLE_DIM), BF16)],
        scratch_shapes=[pltpu.VMEM((D_MODEL, D_MODEL), F32), pltpu.VMEM((D_MODEL, PLE_DIM), F32)],
        compiler_params=_params("arbitrary"),
    )(h3, target, gf, gate, pp, h2, g_ple, wpg, hn3, p)


def _mm_norm_bwd(parts, h, g, dres, name, exchange=(), dw_lhs=None):
    s_len = h.shape[0]
    tm = ROW_TILE // 2
    n_parts = len(parts)
    n_steps = s_len // tm
    has_dw = dw_lhs is not None

    def body(*refs):
        a_refs = refs[0:2 * n_parts:2]
        w_refs = refs[1:2 * n_parts:2]
        h_ref, g_ref, r_ref = refs[2 * n_parts:2 * n_parts + 3]
        rest = refs[2 * n_parts + 3:]
        step = pl.program_id(0)
        if has_dw:
            lhs_ref, o_ref, dg_ref, dw_ref, acc_ref = rest
        else:
            o_ref, dg_ref = rest

        @pl.when(step == 0)
        def _():
            dg_ref[...] = jnp.zeros_like(dg_ref)
            if has_dw:
                acc_ref[...] = jnp.zeros_like(acc_ref)

        dn = jnp.dot(a_refs[0][...], w_refs[0][...], preferred_element_type=F32)
        for a_ref, w_ref in zip(a_refs[1:], w_refs[1:]):
            dn = dn + jnp.dot(a_ref[...], w_ref[...], preferred_element_type=F32)
        dh, dg_rows = _norm_bwd(dn, h_ref[...], g_ref[...])
        out = r_ref[...] + dh
        o_ref[...] = out
        dg_ref[...] += jnp.sum(dg_rows, axis=0, keepdims=True)
        if has_dw:
            acc_ref[...] += lax.dot_general(lhs_ref[...], out.astype(BF16), _TN, preferred_element_type=F32)

            @pl.when(step == n_steps - 1)
            def _():
                dw_ref[...] = acc_ref[...].astype(BF16)

    in_specs, args = [], []
    for a, w in parts:
        in_specs += [_rows(tm, a.shape[1]), _whole(w.shape)]
        args += [a, w]
    in_specs += [_rows(tm, D_MODEL), _whole((1, D_MODEL)), _rows(tm, D_MODEL)]
    args += [h, g, dres]
    out_specs = [_rows(tm, D_MODEL), _whole((1, D_MODEL))]
    out_shape = [jax.ShapeDtypeStruct((s_len, D_MODEL), F32), jax.ShapeDtypeStruct((1, D_MODEL), F32)]
    scratch = []
    if has_dw:
        m = dw_lhs.shape[1]
        in_specs.append(_rows(tm, m))
        args.append(dw_lhs)
        out_specs.append(_whole((m, D_MODEL)))
        out_shape.append(jax.ShapeDtypeStruct((m, D_MODEL), BF16))
        scratch.append(pltpu.VMEM((m, D_MODEL), F32))
    return _call(
        body, name=name, grid=(n_steps,), in_specs=in_specs, out_specs=out_specs, out_shape=out_shape,
        scratch_shapes=scratch, semantics=("arbitrary",), args=tuple(args), exchange=exchange)


def _ffn_down_bwd(dh, wdt, a, b, exchange=()):
    s_len = dh.shape[0]
    tm = ROW_TILE // 2

    def body(dh_ref, w_ref, a_ref, b_ref, da_ref, db_ref):
        dhb = dh_ref[...].astype(BF16)
        for j in range(D_FF // 256):
            cols = slice(j * 256, (j + 1) * 256)
            dact = lax.dot_general(dhb, w_ref[cols, :], _NT, preferred_element_type=F32)
            av = a_ref[:, cols].astype(F32)
            bv = b_ref[:, cols].astype(F32)
            sig = jax.nn.sigmoid(av)
            da_ref[:, cols] = (dact * bv * sig * (1.0 + av * (1.0 - sig))).astype(BF16)
            db_ref[:, cols] = (dact * av * sig).astype(BF16)

    wide = jax.ShapeDtypeStruct((s_len, D_FF), BF16)
    return _call(
        body, name="ffn_down_bwd", grid=(s_len // tm,),
        in_specs=[_rows(tm, D_MODEL), _whole((D_FF, D_MODEL)), _rows(tm, D_FF), _rows(tm, D_FF)],
        out_specs=[_rows(tm, D_FF), _rows(tm, D_FF)],
        out_shape=[wide, wide],
        semantics=("parallel",), args=(dh, wdt, a, b), exchange=exchange)


def _outproj_bwd(dh1, woutt, yb, gb, head_sum, exchange=()):
    s_len = dh1.shape[0]
    tm = ROW_TILE
    n_br = len(DILATIONS)

    def body(dh_ref, w_ref, yb_ref, gb_ref, e_ref, dya_ref, dgb_ref, *rest):
        do_refs, dd_refs = rest[:n_br], rest[n_br:2 * n_br]
        do_nat, dd_nat = rest[2 * n_br:]

        @pl.when(pl.program_id(0) == 0)
        def _():
            dgb_ref[...] = jnp.zeros_like(dgb_ref)

        dhb = dh_ref[...].astype(BF16)
        dya_ref[...] = lax.dot_general(dhb, w_ref[:WIDTH_A, :], _NT, preferred_element_type=F32)
        dyn = lax.dot_general(dhb, w_ref[WIDTH_A:, :], _NT, preferred_element_type=F32)
        ybv = yb_ref[...]
        dyb, dg_rows = _norm_bwd(dyn, ybv, gb_ref[...])
        dgb_ref[...] += jnp.sum(dg_rows, axis=0, keepdims=True)
        prod = dyb * ybv
        for cb in range(N_PAIRS):
            cols = slice(cb * 128, (cb + 1) * 128)
            pc = prod[:, cols]
            hi = pc.astype(BF16)
            lo = (pc - hi.astype(F32)).astype(BF16)
            do_nat[cb] = dyb[:, cols]
            dd_nat[cb] = (jnp.dot(hi, e_ref[...], preferred_element_type=F32)
                          + jnp.dot(lo, e_ref[...], preferred_element_type=F32))
            for i, d in enumerate(DILATIONS):
                _to_sub(do_nat, cb, do_refs[i], cb, d, tm)
                _to_sub(dd_nat, cb, dd_refs[i], cb, d, tm)

    subs = [_sub_spec(d, N_PAIRS, tm) for d in DILATIONS]
    return _call(
        body, name="outproj_bwd", grid=(s_len // tm,),
        in_specs=[_rows(tm, D_MODEL), _whole((D_MODEL, D_MODEL)), _rows(tm, WIDTH_B), _whole((1, WIDTH_B)), _whole((128, 128))],
        out_specs=[_rows(tm, WIDTH_A), _whole((1, WIDTH_B))] + subs + subs,
        out_shape=[jax.ShapeDtypeStruct((s_len, WIDTH_A), F32), jax.ShapeDtypeStruct((1, WIDTH_B), F32)]
        + [_sub_shape(s_len, d, N_PAIRS, BF16) for d in DILATIONS] + [_sub_shape(s_len, d, N_PAIRS, F32) for d in DILATIONS],
        scratch_shapes=[pltpu.VMEM((N_PAIRS, tm, 128), F32), pltpu.VMEM((N_PAIRS, tm, 128), F32)],
        semantics=("arbitrary",), args=(dh1, woutt, yb, gb, head_sum), exchange=exchange)


def _attn_bwd(qkv, do, lse, dd, d, exchange=()):
    sd = qkv.shape[2]
    tile, nb, n_tiles = _attn_geometry(sd)
    last_block = sd // CHUNK - 1

    def nxt(n):
        return jnp.minimum((n + 1) * nb, last_block)

    def one_pair(q_ref, qn_ref, k_ref, v_ref, do_ref, don_ref, l_ref, ln_ref, dd_ref, ddn_ref,
                 dq_ref, dk_ref, dv_ref, carry_ref):
        n = pl.program_id(2)

        @pl.when(n == 0)
        def _():
            carry_ref[...] = jnp.zeros_like(carry_ref)

        head_a = lax.broadcasted_iota(jnp.int32, (CHUNK, 128), 1) < HEAD_DIM
        row = lax.broadcasted_iota(jnp.int32, (4 * CHUNK, CHUNK), 0)
        qi = row % CHUNK
        ki = lax.broadcasted_iota(jnp.int32, (4 * CHUNK, CHUNK), 1)
        is_after = row >= 2 * CHUNK
        mask = (is_after & (ki >= qi)) | (jnp.logical_not(is_after) & (qi >= ki))
        mask_last = mask & jnp.logical_or(jnp.logical_not(is_after), n < n_tiles - 1)
        dq_acc = [carry_ref[...]] + [jnp.zeros((CHUNK, 128), F32) for _ in range(nb)]

        def per_head(x):
            other = pltpu.roll(x, HEAD_DIM, 1)
            return [jnp.where(head_a, x, other), jnp.where(head_a, other, x)]

        for j in range(nb):
            rows = slice(j * CHUNK, (j + 1) * CHUNK)
            kj = k_ref[rows, :]
            vj = v_ref[rows, :]
            if j + 1 < nb:
                nrows = slice((j + 1) * CHUNK, (j + 2) * CHUNK)
                q2, do2, l2, dd2, msk = q_ref[nrows, :], do_ref[nrows, :], l_ref[nrows, :], dd_ref[nrows, :], mask
            else:
                q2, do2, l2, dd2, msk = qn_ref[...], don_ref[...], ln_ref[...], ddn_ref[...], mask_last
            qs = jnp.concatenate(_both_heads(q_ref[rows, :], head_a) + _both_heads(q2, head_a), axis=0)
            dos = jnp.concatenate(_both_heads(do_ref[rows, :], head_a) + _both_heads(do2, head_a), axis=0)
            ls = jnp.concatenate(per_head(l_ref[rows, :]) + per_head(l2), axis=0)
            dds = jnp.concatenate(per_head(dd_ref[rows, :]) + per_head(dd2), axis=0)
            s = lax.dot_general(qs, kj, _NT, preferred_element_type=F32)
            p = jnp.exp(jnp.where(msk, s - ls, NEG))
            dp = lax.dot_general(dos, vj, _NT, preferred_element_type=F32)
            ds = (p * (dp - dds)).astype(BF16)
            dv_ref[rows, :] = lax.dot_general(p.astype(BF16), dos, _TN, preferred_element_type=F32).astype(BF16)
            dk_ref[rows, :] = lax.dot_general(ds, qs, _TN, preferred_element_type=F32).astype(BF16)
            dqs = jnp.dot(ds, kj, preferred_element_type=F32)
            dq_acc[j] = dq_acc[j] + jnp.where(head_a, dqs[:CHUNK], dqs[CHUNK:2 * CHUNK])
            dq_acc[j + 1] = dq_acc[j + 1] + jnp.where(head_a, dqs[2 * CHUNK:3 * CHUNK], dqs[3 * CHUNK:])
        for j in range(nb):
            dq_ref[j * CHUNK:(j + 1) * CHUNK, :] = dq_acc[j].astype(BF16)
        carry_ref[...] = dq_acc[nb]

    same = lambda n: n
    grad = jax.ShapeDtypeStruct((d, N_PAIRS, sd, 128), BF16)
    return _call(
        _per_pair(one_pair), name=f"attn_bwd_d{d}", grid=(d, N_PAIRS // PAIRS_PER_STEP, n_tiles),
        in_specs=[_attn_spec(0, tile, same), _attn_spec(0, CHUNK, nxt), _attn_spec(N_PAIRS, tile, same),
                  _attn_spec(2 * N_PAIRS, tile, same), _attn_spec(0, tile, same), _attn_spec(0, CHUNK, nxt),
                  _attn_spec(0, tile, same), _attn_spec(0, CHUNK, nxt), _attn_spec(0, tile, same), _attn_spec(0, CHUNK, nxt)],
        out_specs=[_attn_spec(0, tile, same)] * 3,
        out_shape=[grad, grad, grad],
        scratch_shapes=[pltpu.VMEM((PAIRS_PER_STEP, CHUNK, 128), F32)],
        semantics=("parallel", "parallel", "arbitrary"), args=(qkv, qkv, qkv, qkv, do, do, lse, lse, dd, dd), exchange=exchange)


def _sgu_bwd(ua, sw, b2, gs, ga, dya_n):
    s_len = ua.shape[0]
    tm = ROW_TILE

    def body(ua_ref, sw_ref, b2_ref, gs_ref, ga_ref, dy_ref, dua_ref, dsw_ref, db2_ref, dgs_ref, dga_ref):
        @pl.when(pl.program_id(0) == 0)
        def _():
            dsw_ref[...] = jnp.zeros_like(dsw_ref)
            db2_ref[...] = jnp.zeros_like(db2_ref)
            dgs_ref[...] = jnp.zeros_like(dgs_ref)
            dga_ref[...] = jnp.zeros_like(dga_ref)

        u, va, ug, xhat, rstd, vn = _sgu_core(ua_ref, gs_ref)
        wm, keep = _sgu_mix_weights(sw_ref)
        head = lax.broadcasted_iota(jnp.int32, (CHUNK, WIDTH_A), 1) // HEAD_DIM
        gav = ga_ref[...]
        gsv = gs_ref[...]
        dga = jnp.zeros((1, WIDTH_A), F32)
        dgs = jnp.zeros((1, WIDTH_A), F32)
        db2 = jnp.zeros((CHUNK, WIDTH_A), F32)
        dsw = [jnp.zeros((CHUNK, CHUNK), F32) for _ in range(4)]
        for c in range(tm // CHUNK):
            rows = slice(c * CHUNK, (c + 1) * CHUNK)
            vnc = vn[rows]
            vnb = vnc.astype(BF16)
            mixed = b2_ref[...]
            for h in range(4):
                mixed = mixed + jnp.dot(wm[h], jnp.where(head == h, vnc, 0.0).astype(BF16), preferred_element_type=F32)
            ugc = ug[rows]
            dya, dga_rows = _norm_bwd(dy_ref[rows, :], ugc * mixed, gav)
            dga = dga + jnp.sum(dga_rows, axis=0, keepdims=True)
            dmixed = dya * ugc
            db2 = db2 + dmixed
            dvn = jnp.zeros((CHUNK, WIDTH_A), F32)
            for h in range(4):
                dmh = jnp.where(head == h, dmixed, 0.0).astype(BF16)
                dsw[h] = dsw[h] + lax.dot_general(dmh, vnb, _NT, preferred_element_type=F32)
                dvn = dvn + lax.dot_general(wm[h], dmh, _TN, preferred_element_type=F32)
            xh = xhat[rows]
            dgs = dgs + jnp.sum(dvn * xh, axis=0, keepdims=True)
            dxh = dvn * gsv
            dvg = rstd[rows] * (dxh - jnp.mean(dxh, axis=-1, keepdims=True) - xh * jnp.mean(dxh * xh, axis=-1, keepdims=True))
            dua_ref[rows, :WIDTH_A] = (dya * mixed * _gelu_grad(u[rows])).astype(BF16)
            dua_ref[rows, WIDTH_A:] = (dvg * _gelu_grad(va[rows])).astype(BF16)
        for h in range(4):
            dsw_ref[h] += jnp.where(keep, dsw[h], 0.0)
        db2_ref[...] += db2
        dgs_ref[...] += dgs
        dga_ref[...] += dga

    return pl.pallas_call(
        body, name="sgu_bwd", grid=(s_len // tm,),
        in_specs=[_rows(tm, 2 * WIDTH_A), _whole((4, CHUNK, CHUNK)), _whole((CHUNK, WIDTH_A)), _whole((1, WIDTH_A)),
                  _whole((1, WIDTH_A)), _rows(tm, WIDTH_A)],
        out_specs=[_rows(tm, 2 * WIDTH_A), _whole((4, CHUNK, CHUNK)), _whole((CHUNK, WIDTH_A)), _whole((1, WIDTH_A)), _whole((1, WIDTH_A))],
        out_shape=[jax.ShapeDtypeStruct((s_len, 2 * WIDTH_A), BF16), jax.ShapeDtypeStruct((4, CHUNK, CHUNK), F32),
                   jax.ShapeDtypeStruct((CHUNK, WIDTH_A), F32), jax.ShapeDtypeStruct((1, WIDTH_A), F32),
                   jax.ShapeDtypeStruct((1, WIDTH_A), F32)],
        compiler_params=_params("arbitrary"),
    )(ua, sw, b2, gs, ga, dya_n)


def _dproj(dua, dqs, dks, dvs, cos, sin):
    s_len = dua.shape[0]
    tm = ROW_TILE
    n_br = len(DILATIONS)

    def body(dua_ref, *rest):
        groups = [rest[g * n_br:(g + 1) * n_br] for g in range(3)]
        cos_ref, sin_ref, out_ref, acc = rest[3 * n_br:]
        out_ref[:, :2 * WIDTH_A] = dua_ref[...]
        c = cos_ref[...]
        s = sin_ref[...]
        first_half = (lax.broadcasted_iota(jnp.int32, (tm, 128), 1) % HEAD_DIM) < HEAD_DIM // 2
        for g, refs in enumerate(groups):
            for cb in range(N_PAIRS):
                for i, d in enumerate(DILATIONS):
                    _from_sub(refs[i], cb, acc, 0, d, tm, accumulate=i > 0)
                t = acc[0]
                if g < 2:
                    t = (t * c - _swap_halves(t, first_half) * s) * (0.125 if g == 0 else 1.0)
                col = 2 * WIDTH_A + g * WIDTH_B + cb * 128
                out_ref[:, col:col + 128] = t.astype(BF16)

    subs = [_sub_spec(d, N_PAIRS, tm) for d in DILATIONS]
    return pl.pallas_call(
        body, name="dproj", grid=(s_len // tm,),
        in_specs=[_rows(tm, 2 * WIDTH_A)] + subs * 3 + [_rows(tm, 128), _rows(tm, 128)],
        out_specs=_rows(tm, IN_COLS),
        out_shape=jax.ShapeDtypeStruct((s_len, IN_COLS), BF16),
        scratch_shapes=[pltpu.VMEM((1, tm, 128), F32)],
        compiler_params=_params("parallel"),
    )(dua, *dqs, *dks, *dvs, cos, sin)


def _mm_tn(a, b, name, exchange=()):
    s_len, m = a.shape
    n = b.shape[1]
    tk = ROW_TILE
    tm = m if m <= 512 else (1408 if m == D_FF else 512)
    n_k = s_len // tk

    def body(a_ref, b_ref, o_ref, acc_ref):
        k = pl.program_id(1)

        @pl.when(k == 0)
        def _():
            acc_ref[...] = jnp.zeros_like(acc_ref)

        acc_ref[...] += lax.dot_general(a_ref[...].astype(BF16), b_ref[...].astype(BF16), _TN, preferred_element_type=F32)

        @pl.when(k == n_k - 1)
        def _():
            o_ref[...] = acc_ref[...].astype(BF16)

    (grad,), received = _call(
        body, name=name, grid=(m // tm, n_k),
        in_specs=[pl.BlockSpec((tk, tm), lambda i, k: (k, i)), pl.BlockSpec((tk, n), lambda i, k: (k, 0))],
        out_specs=[pl.BlockSpec((tm, n), lambda i, k: (i, 0))],
        out_shape=[jax.ShapeDtypeStruct((m, n), BF16)],
        scratch_shapes=[pltpu.VMEM((tm, n), F32)],
        semantics=("parallel", "arbitrary"), args=(a, b), exchange=exchange)
    return grad, received


def _position():
    x, y, c = lax.axis_index("x"), lax.axis_index("y"), lax.axis_index("c")
    return x, y, c, 4 * x + 2 * y + c


def _peer(x, y, c, rel):
    return (x ^ ((rel >> 2) & 1), y ^ ((rel >> 1) & 1), c ^ (rel & 1))


def _exchange_out_shape(kind, arr):
    return jax.ShapeDtypeStruct(((N_DEV,) + arr.shape) if kind == "gather" else arr.shape, arr.dtype)


def _exchange_sems(n_items):
    return [pltpu.SemaphoreType.DMA((n_items, N_DEV)), pltpu.SemaphoreType.DMA((n_items, N_DEV)), pltpu.SemaphoreType.DMA((n_items,))]


def _exchange_copies(kinds, srcs, dsts, sems):
    send_sems, recv_sems, local_sems = sems
    x, y, c, me = _position()
    local, sends, recvs = [], [], []
    for k, (kind, src, dst) in enumerate(zip(kinds, srcs, dsts)):
        own = src if kind == "gather" else src.at[me]
        local.append(pltpu.make_async_copy(own, dst.at[me], local_sems.at[k]))
        for rel in range(1, N_DEV):
            going = src if kind == "gather" else src.at[me ^ rel]
            common = dict(send_sem=send_sems.at[k, rel], recv_sem=recv_sems.at[k, rel],
                          device_id=_peer(x, y, c, rel), device_id_type=MESH)
            sends.append(pltpu.make_async_remote_copy(src_ref=going, dst_ref=dst.at[me], **common))
            recvs.append(pltpu.make_async_remote_copy(src_ref=own, dst_ref=dst.at[me ^ rel], **common))
    return local, sends, recvs


def _exchange_start(kinds, srcs, dsts, sems):
    local, sends, _ = _exchange_copies(kinds, srcs, dsts, sems)
    for cp in local + sends:
        cp.start()


def _exchange_finish(kinds, srcs, dsts, sems):
    local, sends, recvs = _exchange_copies(kinds, srcs, dsts, sems)
    for cp in recvs:
        cp.wait_recv()
    for cp in sends:
        cp.wait_send()
    for cp in local:
        cp.wait()


def _exchange_only(items, name):
    kinds = [k for k, _ in items]
    n = len(items)

    def body(*refs):
        srcs, dsts, sems = refs[:n], refs[n:2 * n], refs[2 * n:]
        _exchange_start(kinds, srcs, dsts, sems)
        _exchange_finish(kinds, srcs, dsts, sems)

    any_spec = pl.BlockSpec(memory_space=pl.ANY)
    return pl.pallas_call(
        body, name=name, in_specs=[any_spec] * n, out_specs=[any_spec] * n,
        out_shape=[_exchange_out_shape(k, a) for k, a in items],
        scratch_shapes=_exchange_sems(n),
        compiler_params=pltpu.CompilerParams(has_side_effects=True),
    )(*[a for _, a in items])


def _gather_two_level_with_rope_tables(shard, inv_freq, s_len, name):
    rows = ROW_TILE

    def body(inv_ref, src, cos_ref, sin_ref, dst, send_sems, recv_sems, local_sem):
        x, y, c, me = _position()
        sibling = (x, y, 1 - c)
        chips = [(1 - x, y), (x, 1 - y), (1 - x, 1 - y)]

        def block(px, py, pc):
            return dst.at[4 * px + 2 * py + pc]

        def copy(k, blk, to, src_ref=None):
            return pltpu.make_async_remote_copy(
                src_ref=block(*blk) if src_ref is None else src_ref, dst_ref=block(*blk),
                send_sem=send_sems.at[k], recv_sem=recv_sems.at[k], device_id=to, device_id_type=MESH)

        mine = pltpu.make_async_copy(src, dst.at[me], local_sem)
        mine.start()
        first = [copy(0, (x, y, c), sibling, src)] + [copy(1 + j, (x, y, c), (*chip, c), src) for j, chip in enumerate(chips)]
        for cp in first:
            cp.start()

        inv = inv_ref[...]
        lane = lax.broadcasted_iota(jnp.int32, (rows, 128), 1)
        sign = jnp.where((lane // (HEAD_DIM // 2)) % 2 == 0, -1.0, 1.0)
        row = lax.broadcasted_iota(jnp.int32, (rows, 128), 0)

        @pl.loop(0, s_len // rows)
        def _(i):
            at = pl.multiple_of(i * rows, rows)
            ang = (row + at).astype(F32) * inv
            cos_ref[pl.ds(at, rows), :] = jnp.cos(ang)
            sin_ref[pl.ds(at, rows), :] = jnp.sin(ang) * sign

        passed = [copy(4 + j, (*chip, c), sibling) for j, chip in enumerate(chips)]
        for j, chip in enumerate(chips):
            copy(1 + j, (*chip, c), (x, y, c)).wait_recv()
            passed[j].start()
        copy(0, (x, y, 1 - c), (x, y, c)).wait_recv()
        for j, chip in enumerate(chips):
            copy(4 + j, (*chip, 1 - c), (x, y, c)).wait_recv()
        for cp in first + passed:
            cp.wait_send()
        mine.wait()

    any_spec = pl.BlockSpec(memory_space=pl.ANY)
    vmem = pl.BlockSpec(memory_space=pltpu.VMEM)
    table = jax.ShapeDtypeStruct((s_len, 128), F32)
    return pl.pallas_call(
        body, name=name, in_specs=[vmem, any_spec], out_specs=[vmem, vmem, any_spec],
        out_shape=[table, table, _exchange_out_shape("gather", shard)],
        scratch_shapes=[pltpu.SemaphoreType.DMA((N_DEV - 1,)), pltpu.SemaphoreType.DMA((N_DEV - 1,)), pltpu.SemaphoreType.DMA],
        compiler_params=pltpu.CompilerParams(has_side_effects=True, vmem_limit_bytes=V7X_VMEM_LIMIT_BYTES),
    )(inv_freq, shard)


def _call(body, *, name, grid, in_specs, out_specs, out_shape, args, scratch_shapes=(), semantics, exchange=()):
    if not exchange:
        outs = pl.pallas_call(body, name=name, grid=grid, in_specs=in_specs, out_specs=out_specs, out_shape=out_shape,
                              scratch_shapes=list(scratch_shapes), compiler_params=_params(*semantics))(*args)
        return outs, []
    kinds = [k for k, _ in exchange]
    n_in, n_out, n_x, n_scr = len(in_specs), len(out_specs), len(exchange), len(scratch_shapes)

    def wrapped(*refs):
        ins, refs = refs[:n_in], refs[n_in:]
        srcs, refs = refs[:n_x], refs[n_x:]
        outs, refs = refs[:n_out], refs[n_out:]
        dsts, refs = refs[:n_x], refs[n_x:]
        scratch, sems = refs[:n_scr], refs[n_scr:]
        ids = [pl.program_id(a) for a in range(len(grid))]
        first = functools.reduce(jnp.logical_and, [i == 0 for i in ids])
        last = functools.reduce(jnp.logical_and, [i == g - 1 for i, g in zip(ids, grid)])

        @pl.when(first)
        def _():
            _exchange_start(kinds, srcs, dsts, sems)

        body(*ins, *outs, *scratch)

        @pl.when(last)
        def _():
            _exchange_finish(kinds, srcs, dsts, sems)

    any_spec = pl.BlockSpec(memory_space=pl.ANY)
    outs = pl.pallas_call(
        wrapped, name=name, grid=grid,
        in_specs=list(in_specs) + [any_spec] * n_x, out_specs=list(out_specs) + [any_spec] * n_x,
        out_shape=list(out_shape) + [_exchange_out_shape(k, a) for k, a in exchange],
        scratch_shapes=list(scratch_shapes) + _exchange_sems(n_x),
        compiler_params=pltpu.CompilerParams(dimension_semantics=("arbitrary",) * len(grid),
                                             vmem_limit_bytes=V7X_VMEM_LIMIT_BYTES, has_side_effects=True),
    )(*args, *[a for _, a in exchange])
    return outs[:n_out], outs[n_out:]


def _adamw_math(w, g, m, v):
    m = ADAM_B1 * m + (1.0 - ADAM_B1) * g
    v = ADAM_B2 * v + (1.0 - ADAM_B2) * (g * g)
    m_hat = m / (1.0 - ADAM_B1 ** ADAM_STEP)
    v_hat = v / (1.0 - ADAM_B2 ** ADAM_STEP)
    return -ADAM_LR * (m_hat / (jnp.sqrt(v_hat) + ADAM_EPS) + ADAM_WD * w), m, v


def _adamw(parts, w, m, v, name):
    rows, cols = w.shape
    tm = 256 if rows % 256 == 0 and rows > 256 else rows

    def body(p_ref, w_ref, m_ref, v_ref, g_ref, d_ref, nm_ref, nv_ref):
        g = p_ref[0].astype(F32)
        for j in range(1, N_DEV):
            g = g + p_ref[j].astype(F32)
        delta, nm, nv = _adamw_math(w_ref[...], g, m_ref[...], v_ref[...])
        g_ref[...] = g
        d_ref[...] = delta
        nm_ref[...] = nm
        nv_ref[...] = nv

    shard = jax.ShapeDtypeStruct((rows, cols), F32)
    return pl.pallas_call(
        body, name=name, grid=(rows // tm,),
        in_specs=[pl.BlockSpec((N_DEV, tm, cols), lambda i: (0, i, 0))] + [_rows(tm, cols)] * 3,
        out_specs=[_rows(tm, cols)] * 4,
        out_shape=[shard] * 4,
        compiler_params=_params("parallel"),
    )(parts, w, m, v)


_SMALL = ("mix_norm_g", "sgu_w", "sgu_b", "sgu_norm_g", "out_norm_a", "out_norm_b", "ffn_norm_g", "ple_norm_g", "final_norm_g")
_BIG = ("w_in", "w_out", "w_gate", "w_up", "w_down", "w_ple_gate", "w_ple_proj")
_COLUMN_SHARDED = ("w_in", "w_gate", "w_up", "w_ple_proj")
_ORDER = ("mix_norm_g", "w_in", "sgu_w", "sgu_b", "sgu_norm_g", "out_norm_a", "out_norm_b", "w_out", "ffn_norm_g",
          "w_gate", "w_up", "w_down", "ple_norm_g", "w_ple_gate", "w_ple_proj", "final_norm_g")


def _pack_small(values, names=_SMALL):
    flat = jnp.concatenate([values[n].reshape(-1).astype(F32) for n in names])
    pad = (-flat.shape[0]) % (8 * 128)
    return jnp.pad(flat, (0, pad)).reshape(-1, 128)


def _unpack_small(packed, like):
    flat = packed.reshape(-1)
    out, at = {}, 0
    for n in _SMALL:
        size = like[n].size
        out[n] = flat[at:at + size].reshape(like[n].shape)
        at += size
    return out


def _own_orientation(name, value):
    return value[0].T if name in _COLUMN_SHARDED else value[0]


def _reference_orientation(name, value):
    return (value.T if name in _COLUMN_SHARDED else value)[None]


def _full_from_gathered(gathered):
    return gathered.reshape(N_DEV * gathered.shape[1], gathered.shape[2])


def _sliced_for_devices(grad):
    return grad.reshape(N_DEV, grad.shape[0] // N_DEV, grad.shape[1])


def _rope_inv_freq():
    half = HEAD_DIM // 2
    inv = ROPE_THETA ** (-jnp.arange(half, dtype=F32) / half)
    return jnp.tile(inv, 128 // half)[None, :]


def _forward_backward(x, p, target, small, shards):
    def gather(*names):
        return [("gather", shards[n]) for n in names]

    def scatter(**grads):
        return [("scatter", _sliced_for_devices(g)) for g in grads.values()]

    full, parts = {}, {}
    s_len = x.shape[0]
    cos, sin, got = _gather_two_level_with_rope_tables(shards["w_in"], _rope_inv_freq(), s_len, "gather_w_in")
    full["w_in"] = _full_from_gathered(got)

    g_mix, g_ffn, g_ple = small["mix_norm_g"], small["ffn_norm_g"], small["ple_norm_g"]
    g_fin = small["final_norm_g"].reshape(1, D_MODEL)
    sw, gs, ga, gb = small["sgu_w"], small["sgu_norm_g"], small["out_norm_a"], small["out_norm_b"]
    b2 = jnp.repeat(small["sgu_b"].T, HEAD_DIM, axis=1)
    lane_head = jnp.arange(128) // HEAD_DIM
    head_sum = (lane_head[:, None] == lane_head[None, :]).astype(BF16)
    n_br = len(DILATIONS)

    def arrived(names, got):
        for n, g in zip(names, got):
            full[n] = _full_from_gathered(g)

    (ua, hn1, *qkv), got = _inproj(x, g_mix, full["w_in"], cos, sin, exchange=gather("w_gate"))
    arrived(("w_gate",), got)
    ya_n = _sgu_fwd(ua, sw, b2, gs, ga)
    half = shards["w_up"].shape[0] // 2
    riders = [[("gather", shards["w_up"][:half])], [("gather", shards["w_up"][half:])], gather("w_out")]
    branch, got = [], []
    for i, d in enumerate(DILATIONS):
        o_l, g = _attn_fwd(qkv[i], d, exchange=riders[i])
        branch.append(o_l)
        got += g
    arrived(("w_up", "w_out"), [jnp.concatenate(got[:2], axis=1), got[2]])
    (y, yb, *lse), _ = _combine([o for o, _ in branch], [l for _, l in branch], ya_n, gb)
    last_wave = ("w_down", "w_ple_gate", "w_ple_proj")
    (h1, a, b, act, hn2), got = _ffn_up(y, full["w_out"], x, g_ffn, full["w_gate"], full["w_up"], exchange=gather(*last_wave))
    arrived(last_wave, got)
    h2, h3, gate, pp, hn3 = _ffn_down_ple(act, full["w_down"], h1, g_ple, full["w_ple_gate"], p, full["w_ple_proj"])

    dh2, loss, d_fin, d_ple, g_ple_gate, g_ple_proj = _loss_ple_bwd(
        h3, target, g_fin, gate, pp, h2, g_ple, full["w_ple_gate"], hn3, p)
    g_down, (parts["w_ple_gate"], parts["w_ple_proj"]) = _mm_tn(
        act, dh2, "dw_down", exchange=scatter(w_ple_gate=g_ple_gate, w_ple_proj=g_ple_proj))
    (da, db), (parts["w_down"],) = _ffn_down_bwd(dh2, full["w_down"], a, b, exchange=scatter(w_down=g_down))
    g_gate, _ = _mm_tn(da, hn2, "dw_gate")
    g_up, _ = _mm_tn(db, hn2, "dw_up")
    (dh1, d_ffn, g_out), (parts["w_gate"],) = _mm_norm_bwd(
        [(da, full["w_gate"]), (db, full["w_up"])], h1, g_ffn, dh2, "ffn_up_bwd", exchange=scatter(w_gate=g_gate), dw_lhs=y)
    (dya_n, d_gb, *do_dd), (parts["w_out"],) = _outproj_bwd(dh1, full["w_out"], yb, gb, head_sum, exchange=scatter(w_out=g_out))
    grads_b = []
    for i, d in enumerate(DILATIONS):
        g3, got = _attn_bwd(qkv[i], do_dd[i], lse[i], do_dd[n_br + i], d, exchange=scatter(w_up=g_up) if i == 0 else ())
        grads_b.append(g3)
        if i == 0:
            (parts["w_up"],) = got
    dua, d_sw, d_b2, d_gs, d_ga = _sgu_bwd(ua, sw, b2, gs, ga, dya_n)
    dproj = _dproj(dua, [g[0] for g in grads_b], [g[1] for g in grads_b], [g[2] for g in grads_b], cos, sin)
    early = {
        "sgu_w": d_sw, "sgu_b": d_b2.reshape(CHUNK, 4, HEAD_DIM).sum(axis=-1).T, "sgu_norm_g": d_gs, "out_norm_a": d_ga,
        "out_norm_b": d_gb, "ffn_norm_g": d_ffn, "ple_norm_g": d_ple, "final_norm_g": d_fin,
    }
    g_in, (early_parts,) = _mm_tn(dproj, hn1, "dw_in", exchange=[("gather", _pack_small(early, _SMALL[1:]))])
    (dx, d_mix), (parts["w_in"],) = _mm_norm_bwd(
        [(dproj, full["w_in"])], x, g_mix, dh1, "inproj_bwd", exchange=scatter(w_in=g_in))
    late = jnp.concatenate([_pack_small({"mix_norm_g": d_mix}, _SMALL[:1]), jnp.broadcast_to(loss, (8, 128))])
    (late_parts,) = _exchange_only([("gather", late)], "gather_mix_norm_grad_and_loss")
    total_loss = jnp.sum(late_parts[:, 8, 0])
    return total_loss, dx, parts, jnp.concatenate([late_parts[:, :8], early_parts], axis=1)


def kernel(x, p, mix_norm_g, w_in, sgu_w, sgu_b, sgu_norm_g, out_norm_a, out_norm_b, w_out, ffn_norm_g, w_gate, w_up, w_down, ple_norm_g, w_ple_gate, w_ple_proj, final_norm_g, loss_target, m_mix_norm_g, m_w_in, m_sgu_w, m_sgu_b, m_sgu_norm_g, m_out_norm_a, m_out_norm_b, m_w_out, m_ffn_norm_g, m_w_gate, m_w_up, m_w_down, m_ple_norm_g, m_w_ple_gate, m_w_ple_proj, m_final_norm_g, v_mix_norm_g, v_w_in, v_sgu_w, v_sgu_b, v_sgu_norm_g, v_out_norm_a, v_out_norm_b, v_w_out, v_ffn_norm_g, v_w_gate, v_w_up, v_w_down, v_ple_norm_g, v_w_ple_gate, v_w_ple_proj, v_final_norm_g):
    given = dict(locals())
    weights = {n: given[n] for n in _ORDER}
    moments_m = {n: given["m_" + n] for n in _ORDER}
    moments_v = {n: given["v_" + n] for n in _ORDER}

    shards = {n: _own_orientation(n, weights[n]).astype(BF16) for n in _BIG}
    small = {n: (weights[n][0] if n in ("sgu_w", "sgu_b") else weights[n]) for n in _SMALL}

    loss, dx, parts, small_parts = _forward_backward(x[0], p[0, 0], loss_target[0], small, shards)

    small_like = {n: weights[n] for n in _SMALL}
    grads, deltas, new_m, new_v = {}, {}, {}, {}
    for n in _BIG:
        outs = _adamw(parts[n], _own_orientation(n, weights[n]), _own_orientation(n, moments_m[n]),
                      _own_orientation(n, moments_v[n]), "adamw_" + n)
        grads[n], deltas[n], new_m[n], new_v[n] = [_reference_orientation(n, o) for o in outs]
    g, d, nm, nv = _adamw(small_parts, _pack_small(small_like), _pack_small({n: moments_m[n] for n in _SMALL}),
                          _pack_small({n: moments_v[n] for n in _SMALL}), "adamw_small")
    for out, packed in ((grads, g), (deltas, d), (new_m, nm), (new_v, nv)):
        out.update(_unpack_small(packed, small_like))

    return (loss, dx[None], *[grads[n] for n in _ORDER], *[deltas[n] for n in _ORDER],
            *[new_m[n] for n in _ORDER], *[new_v[n] for n in _ORDER])
```

```python
import functools

import jax
import jax.numpy as jnp
from jax import lax
from jax.experimental import pallas as pl
from jax.experimental.pallas import tpu as pltpu

F32 = jnp.float32
BF16 = jnp.bfloat16

D_MODEL = 1024
WIDTH_A = 256
WIDTH_B = 768
D_FF = 2816
IN_COLS = 2 * WIDTH_A + 3 * WIDTH_B
PLE_DIM = 256
HEAD_DIM = 64
N_PAIRS = WIDTH_B // 128
CHUNK = 128
N_BACK = 128
DILATIONS = (1, 4, 16)
ROPE_THETA = 10000.0
EPS = 1e-6
N_DEV = 8

ADAM_LR = 0.001
ADAM_B1 = 0.9
ADAM_B2 = 0.999
ADAM_EPS = 1e-08
ADAM_WD = 0.01
ADAM_STEP = 10

V7X_VMEM_LIMIT_BYTES = 56 * 1024 * 1024
ROW_TILE = 512
MESH = pl.DeviceIdType.MESH
NEG = -1e30

_NT = (((1,), (1,)), ((), ()))
_TN = (((0,), (0,)), ((), ()))


def _params(*semantics):
    return pltpu.CompilerParams(dimension_semantics=semantics, vmem_limit_bytes=V7X_VMEM_LIMIT_BYTES)


def _rows(tm, width):
    return pl.BlockSpec((tm, width), lambda i: (i, 0))


def _whole(shape):
    return pl.BlockSpec(shape, lambda *_: (0,) * len(shape))


def _gelu(x):
    t = jnp.tanh(0.7978845608028654 * (x + 0.044715 * (x * x * x)))
    return 0.5 * x * (1.0 + t)


def _gelu_grad(x):
    t = jnp.tanh(0.7978845608028654 * (x + 0.044715 * (x * x * x)))
    return 0.5 * (1.0 + t) + 0.5 * x * (1.0 - t * t) * (0.7978845608028654 * (1.0 + 3.0 * 0.044715 * (x * x)))


def _rstd(x):
    return lax.rsqrt(jnp.mean(x * x, axis=-1, keepdims=True) + EPS)


def _norm_bwd(dn, h, g):
    r = _rstd(h)
    n = h * r
    t = dn * g
    return r * (t - n * jnp.mean(t * n, axis=-1, keepdims=True)), dn * n


def _swap_halves(x, first_half):
    return jnp.where(first_half, pltpu.roll(x, 96, 1), pltpu.roll(x, 32, 1))


def _sub_spec(d, n_cb, tm):
    return pl.BlockSpec((d, n_cb, tm // d, 128), lambda i: (0, 0, i, 0))


def _sub_shape(s_len, d, n_cb, dtype):
    return jax.ShapeDtypeStruct((d, n_cb, s_len // d, 128), dtype)


def _to_sub(stage_ref, cb_src, out_ref, cb_dst, d, tm):
    slab = stage_ref.at[cb_src]
    for r in range(d):
        out_ref[r, cb_dst] = slab[pl.ds(r, tm // d, stride=d), :].astype(out_ref.dtype)


def _from_sub(in_ref, cb_src, stage_ref, cb_dst, d, tm, accumulate=False):
    slab = stage_ref.at[cb_dst]
    for r in range(d):
        rows = pl.ds(r, tm // d, stride=d)
        val = in_ref[r, cb_src].astype(F32)
        slab[rows, :] = slab[rows, :] + val if accumulate else val


def _inproj(x, g, w, cos, sin, exchange=()):
    s_len = x.shape[0]
    tm = ROW_TILE
    n_cb = 3 * N_PAIRS

    def body(x_ref, g_ref, w_ref, cos_ref, sin_ref, ua_ref, hn_ref, *rest):
        sub_refs, stage = rest[:-1], rest[-1]
        xf = x_ref[...]
        hn = (xf * _rstd(xf) * g_ref[...]).astype(BF16)
        hn_ref[...] = hn
        c = cos_ref[...]
        s = sin_ref[...]
        first_half = (lax.broadcasted_iota(jnp.int32, (tm, 128), 1) % HEAD_DIM) < HEAD_DIM // 2
        for j in range(IN_COLS // 256):
            col = j * 256
            acc = lax.dot_general(hn, w_ref[col:col + 256, :], _NT, preferred_element_type=F32)
            if col < 2 * WIDTH_A:
                ua_ref[:, col:col + 256] = acc
                continue
            for half in range(2):
                cb = (col - 2 * WIDTH_A) // 128 + half
                t = acc[:, half * 128:(half + 1) * 128]
                if cb < 2 * N_PAIRS:
                    t = (t * c + _swap_halves(t, first_half) * s) * (0.125 if cb < N_PAIRS else 1.0)
                stage[cb] = t
        for cb in range(n_cb):
            for d, out_ref in zip(DILATIONS, sub_refs):
                _to_sub(stage, cb, out_ref, cb, d, tm)

    return _call(
        body, name="inproj", grid=(s_len // tm,),
        in_specs=[_rows(tm, D_MODEL), _whole((1, D_MODEL)), _whole((IN_COLS, D_MODEL)), _rows(tm, 128), _rows(tm, 128)],
        out_specs=[_rows(tm, 2 * WIDTH_A), _rows(tm, D_MODEL)] + [_sub_spec(d, n_cb, tm) for d in DILATIONS],
        out_shape=[jax.ShapeDtypeStruct((s_len, 2 * WIDTH_A), F32), jax.ShapeDtypeStruct((s_len, D_MODEL), BF16)]
        + [_sub_shape(s_len, d, n_cb, BF16) for d in DILATIONS],
        scratch_shapes=[pltpu.VMEM((n_cb, tm, 128), F32)],
        semantics=("parallel",), args=(x, g, w, cos, sin), exchange=exchange)


def _sgu_mix_weights(sw_ref):
    keep = lax.broadcasted_iota(jnp.int32, (CHUNK, CHUNK), 0) >= lax.broadcasted_iota(jnp.int32, (CHUNK, CHUNK), 1)
    return [jnp.where(keep, sw_ref[h], 0.0).astype(BF16) for h in range(4)], keep


def _sgu_core(ua_ref, gs_ref):
    u = ua_ref[:, :WIDTH_A]
    va = ua_ref[:, WIDTH_A:]
    vg = _gelu(va)
    xc = vg - jnp.mean(vg, axis=-1, keepdims=True)
    rstd = lax.rsqrt(jnp.mean(xc * xc, axis=-1, keepdims=True) + EPS)
    xhat = xc * rstd
    return u, va, _gelu(u), xhat, rstd, xhat * gs_ref[...]


def _sgu_fwd(ua, sw, b2, gs, ga):
    s_len = ua.shape[0]
    tm = ROW_TILE

    def body(ua_ref, sw_ref, b2_ref, gs_ref, ga_ref, out_ref):
        _, _, ug, _, _, vn = _sgu_core(ua_ref, gs_ref)
        wm, _ = _sgu_mix_weights(sw_ref)
        head = lax.broadcasted_iota(jnp.int32, (CHUNK, WIDTH_A), 1) // HEAD_DIM
        for c in range(tm // CHUNK):
            rows = slice(c * CHUNK, (c + 1) * CHUNK)
            vnc = vn[rows]
            mixed = b2_ref[...]
            for h in range(4):
                mixed = mixed + jnp.dot(wm[h], jnp.where(head == h, vnc, 0.0).astype(BF16), preferred_element_type=F32)
            ya = ug[rows] * mixed
            out_ref[rows, :] = (ya * _rstd(ya) * ga_ref[...]).astype(BF16)

    return pl.pallas_call(
        body, name="sgu_fwd", grid=(s_len // tm,),
        in_specs=[_rows(tm, 2 * WIDTH_A), _whole((4, CHUNK, CHUNK)), _whole((CHUNK, WIDTH_A)), _whole((1, WIDTH_A)), _whole((1, WIDTH_A))],
        out_specs=_rows(tm, WIDTH_A),
        out_shape=jax.ShapeDtypeStruct((s_len, WIDTH_A), BF16),
        compiler_params=_params("parallel"),
    )(ua, sw, b2, gs, ga)


def _attn_geometry(sd):
    tile = min(ROW_TILE, sd)
    return tile, tile // CHUNK, sd // tile


PAIRS_PER_STEP = 6


def _attn_spec(cb0, rows, row_index):
    return pl.BlockSpec((None, PAIRS_PER_STEP, rows, 128), lambda r, g, n: (r, cb0 // PAIRS_PER_STEP + g, row_index(n), 0))


def _per_pair(one_pair):
    def body(*refs):
        for hp in range(PAIRS_PER_STEP):
            one_pair(*[ref.at[hp] for ref in refs])
    return body


def _both_heads(x, head_a):
    zero = jnp.zeros_like(x)
    return [jnp.where(head_a, x, zero), jnp.where(head_a, zero, x)]


def _attn_fwd(qkv, d, exchange=()):
    sd = qkv.shape[2]
    tile, nb, n_tiles = _attn_geometry(sd)

    def prev(n):
        return jnp.maximum(n * nb - 1, 0)

    def one_pair(q_ref, k_ref, kp_ref, v_ref, vp_ref, o_ref, l_ref):
        n = pl.program_id(2)
        head_a = lax.broadcasted_iota(jnp.int32, (CHUNK, 128), 1) < HEAD_DIM
        qi = lax.broadcasted_iota(jnp.int32, (2 * CHUNK, 2 * CHUNK), 0) % CHUNK
        kc = lax.broadcasted_iota(jnp.int32, (2 * CHUNK, 2 * CHUNK), 1)
        band = (kc >= qi) & (kc <= qi + N_BACK)
        for j in range(nb):
            rows = slice(j * CHUNK, (j + 1) * CHUNK)
            if j == 0:
                kcat = jnp.concatenate([kp_ref[...], k_ref[rows, :]], axis=0)
                vcat = jnp.concatenate([vp_ref[...], v_ref[rows, :]], axis=0)
                valid = band & jnp.logical_or(n > 0, kc >= CHUNK)
            else:
                kcat = k_ref[(j - 1) * CHUNK:(j + 1) * CHUNK, :]
                vcat = v_ref[(j - 1) * CHUNK:(j + 1) * CHUNK, :]
                valid = band
            q2 = jnp.concatenate(_both_heads(q_ref[rows, :], head_a), axis=0)
            s = lax.dot_general(q2, kcat, _NT, preferred_element_type=F32)
            s = jnp.where(valid, s, NEG)
            m = jnp.max(s, axis=-1, keepdims=True)
            p = jnp.exp(s - m)
            l = jnp.sum(p, axis=-1, keepdims=True)
            o2 = jnp.dot(p.astype(BF16), vcat, preferred_element_type=F32) / l
            lse2 = m + jnp.log(l)
            o_ref[rows, :] = jnp.where(head_a, o2[:CHUNK], o2[CHUNK:]).astype(BF16)
            l_ref[rows, :] = jnp.where(head_a, lse2[:CHUNK], lse2[CHUNK:])

    same = lambda n: n
    return _call(
        _per_pair(one_pair), name=f"attn_fwd_d{d}", grid=(d, N_PAIRS // PAIRS_PER_STEP, n_tiles),
        in_specs=[_attn_spec(0, tile, same), _attn_spec(N_PAIRS, tile, same), _attn_spec(N_PAIRS, CHUNK, prev),
                  _attn_spec(2 * N_PAIRS, tile, same), _attn_spec(2 * N_PAIRS, CHUNK, prev)],
        out_specs=[_attn_spec(0, tile, same), _attn_spec(0, tile, same)],
        out_shape=[jax.ShapeDtypeStruct((d, N_PAIRS, sd, 128), BF16), jax.ShapeDtypeStruct((d, N_PAIRS, sd, 128), F32)],
        semantics=("parallel", "parallel", "parallel"), args=(qkv, qkv, qkv, qkv, qkv), exchange=exchange)


def _combine(outs, lses, ya_n, gb, exchange=()):
    s_len = ya_n.shape[0]
    tm = ROW_TILE
    n_br = len(DILATIONS)

    def body(*refs):
        o_refs, l_refs = refs[:n_br], refs[n_br:2 * n_br]
        ya_ref, gb_ref, y_ref, yb_ref = refs[2 * n_br:2 * n_br + 4]
        lse_refs = refs[2 * n_br + 4:3 * n_br + 4]
        o_nat, l_nat, lse_nat = refs[3 * n_br + 4:]
        sumsq = jnp.zeros((tm, 1), F32)
        for cb in range(N_PAIRS):
            for i, d in enumerate(DILATIONS):
                _from_sub(o_refs[i], cb, o_nat, i, d, tm)
                _from_sub(l_refs[i], cb, l_nat, i, d, tm)
            ls = [l_nat[i] for i in range(n_br)]
            top = jnp.maximum(jnp.maximum(ls[0], ls[1]), ls[2])
            ws = [jnp.exp(l - top) for l in ls]
            den = ws[0] + ws[1] + ws[2]
            inv = 1.0 / den
            yb = (ws[0] * inv) * o_nat[0] + (ws[1] * inv) * o_nat[1] + (ws[2] * inv) * o_nat[2]
            yb_ref[:, cb * 128:(cb + 1) * 128] = yb
            sumsq = sumsq + jnp.sum(yb * yb, axis=-1, keepdims=True)
            lse_nat[cb] = top + jnp.log(den)
            for d, lse_ref in zip(DILATIONS, lse_refs):
                _to_sub(lse_nat, cb, lse_ref, cb, d, tm)
        r = lax.rsqrt(sumsq / WIDTH_B + EPS)
        y_ref[:, :WIDTH_A] = ya_ref[...]
        y_ref[:, WIDTH_A:] = (yb_ref[...] * r * gb_ref[...]).astype(BF16)

    return _call(
        body, name="attn_combine", grid=(s_len // tm,),
        in_specs=[_sub_spec(d, N_PAIRS, tm) for d in DILATIONS] * 2 + [_rows(tm, WIDTH_A), _whole((1, WIDTH_B))],
        out_specs=[_rows(tm, D_MODEL), _rows(tm, WIDTH_B)] + [_sub_spec(d, N_PAIRS, tm) for d in DILATIONS],
        out_shape=[jax.ShapeDtypeStruct((s_len, D_MODEL), BF16), jax.ShapeDtypeStruct((s_len, WIDTH_B), F32)]
        + [_sub_shape(s_len, d, N_PAIRS, F32) for d in DILATIONS],
        scratch_shapes=[pltpu.VMEM((n_br, tm, 128), F32), pltpu.VMEM((n_br, tm, 128), F32), pltpu.VMEM((N_PAIRS, tm, 128), F32)],
        semantics=("parallel",), args=(*outs, *lses, ya_n, gb), exchange=exchange)


def _ffn_up(y, wout, x, g, wg, wu, exchange=()):
    s_len = x.shape[0]
    tm = ROW_TILE // 2

    def body(y_ref, wo_ref, x_ref, g_ref, wg_ref, wu_ref, h_ref, a_ref, b_ref, act_ref, hn_ref):
        hf = x_ref[...] + jnp.dot(y_ref[...], wo_ref[...], preferred_element_type=F32)
        h_ref[...] = hf
        hn = (hf * _rstd(hf) * g_ref[...]).astype(BF16)
        hn_ref[...] = hn
        for j in range(D_FF // 256):
            cols = slice(j * 256, (j + 1) * 256)
            a = lax.dot_general(hn, wg_ref[cols, :], _NT, preferred_element_type=F32)
            b = lax.dot_general(hn, wu_ref[cols, :], _NT, preferred_element_type=F32)
            a_ref[:, cols] = a.astype(BF16)
            b_ref[:, cols] = b.astype(BF16)
            act_ref[:, cols] = (a * jax.nn.sigmoid(a) * b).astype(BF16)

    wide = jax.ShapeDtypeStruct((s_len, D_FF), BF16)
    return _call(
        body, name="ffn_up", grid=(s_len // tm,),
        in_specs=[_rows(tm, D_MODEL), _whole((D_MODEL, D_MODEL)), _rows(tm, D_MODEL), _whole((1, D_MODEL)),
                  _whole((D_FF, D_MODEL)), _whole((D_FF, D_MODEL))],
        out_specs=[_rows(tm, D_MODEL), _rows(tm, D_FF), _rows(tm, D_FF), _rows(tm, D_FF), _rows(tm, D_MODEL)],
        out_shape=[jax.ShapeDtypeStruct((s_len, D_MODEL), F32), wide, wide, wide, jax.ShapeDtypeStruct((s_len, D_MODEL), BF16)],
        semantics=("parallel",), args=(y, wout, x, g, wg, wu), exchange=exchange)


def _ffn_down_ple(act, wd, h1, g, wpg, p, wpp):
    s_len = h1.shape[0]
    tm = ROW_TILE // 2

    def body(act_ref, wd_ref, h1_ref, g_ref, wpg_ref, p_ref, wpp_ref, h2_ref, h3_ref, gate_ref, pp_ref, hn_ref):
        hf = h1_ref[...] + jnp.dot(act_ref[...], wd_ref[...], preferred_element_type=F32)
        h2_ref[...] = hf
        hn = (hf * _rstd(hf) * g_ref[...]).astype(BF16)
        hn_ref[...] = hn
        gate = jax.nn.sigmoid(jnp.dot(hn, wpg_ref[...], preferred_element_type=F32))
        pp = lax.dot_general(p_ref[...].astype(BF16), wpp_ref[...], _NT, preferred_element_type=F32)
        h3_ref[...] = hf + gate * pp
        gate_ref[...] = gate.astype(BF16)
        pp_ref[...] = pp.astype(BF16)

    full = jax.ShapeDtypeStruct((s_len, D_MODEL), F32)
    half = jax.ShapeDtypeStruct((s_len, D_MODEL), BF16)
    return pl.pallas_call(
        body, name="ffn_down_ple", grid=(s_len // tm,),
        in_specs=[_rows(tm, D_FF), _whole((D_FF, D_MODEL)), _rows(tm, D_MODEL), _whole((1, D_MODEL)),
                  _whole((D_MODEL, D_MODEL)), _rows(tm, PLE_DIM), _whole((D_MODEL, PLE_DIM))],
        out_specs=[_rows(tm, D_MODEL)] * 5,
        out_shape=[full, full, half, half, half],
        compiler_params=_params("parallel"),
    )(act, wd, h1, g, wpg, p, wpp)


def _loss_ple_bwd(h3, target, gf, gate, pp, h2, g_ple, wpg, hn3, p):
    s_len = h3.shape[0]
    tm = ROW_TILE
    n_steps = s_len // tm

    def body(h_ref, t_ref, g_ref, gate_ref, pp_ref, h2_ref, gp_ref, w_ref, hn_ref, p_ref,
             dh2_ref, loss_ref, dg_ref, dgp_ref, dwg_ref, dwp_ref, acc_g, acc_p):
        step = pl.program_id(0)

        @pl.when(step == 0)
        def _():
            loss_ref[...] = jnp.zeros_like(loss_ref)
            dg_ref[...] = jnp.zeros_like(dg_ref)
            dgp_ref[...] = jnp.zeros_like(dgp_ref)
            acc_g[...] = jnp.zeros_like(acc_g)
            acc_p[...] = jnp.zeros_like(acc_p)

        hf = h_ref[...]
        gfv = g_ref[...]
        err = hf * _rstd(hf) * gfv - t_ref[...]
        loss_ref[...] += 0.5 * jnp.sum(jnp.sum(err * err, axis=-1, keepdims=True), axis=0, keepdims=True) / D_MODEL
        dh, dg_rows = _norm_bwd(err / D_MODEL, hf, gfv)
        dg_ref[...] += jnp.sum(dg_rows, axis=0, keepdims=True)
        gate = gate_ref[...].astype(F32)
        dz = (dh * pp_ref[...].astype(F32) * gate * (1.0 - gate)).astype(BF16)
        dpp = (dh * gate).astype(BF16)
        dn = lax.dot_general(dz, w_ref[...], _NT, preferred_element_type=F32)
        dh2, dgp_rows = _norm_bwd(dn, h2_ref[...], gp_ref[...])
        dh2 = dh + dh2
        dh2_ref[...] = dh2
        dgp_ref[...] += jnp.sum(dgp_rows, axis=0, keepdims=True)
        acc_g[...] += lax.dot_general(hn_ref[...], dz, _TN, preferred_element_type=F32)
        acc_p[...] += lax.dot_general(dpp, p_ref[...].astype(BF16), _TN, preferred_element_type=F32)

        @pl.when(step == n_steps - 1)
        def _():
            dwg_ref[...] = acc_g[...].astype(BF16)
            dwp_ref[...] = acc_p[...].astype(BF16)

    gain = jax.ShapeDtypeStruct((1, D_MODEL), F32)
    return pl.pallas_call(
        body, name="loss_ple_bwd", grid=(n_steps,),
        in_specs=[_rows(tm, D_MODEL), _rows(tm, D_MODEL), _whole((1, D_MODEL)), _rows(tm, D_MODEL), _rows(tm, D_MODEL),
                  _rows(tm, D_MODEL), _whole((1, D_MODEL)), _whole((D_MODEL, D_MODEL)), _rows(tm, D_MODEL),
                  _rows(tm, PLE_DIM)],
        out_specs=[_rows(tm, D_MODEL), _whole((1, 128)), _whole((1, D_MODEL)), _whole((1, D_MODEL)),
                   _whole((D_MODEL, D_MODEL)), _whole((D_MODEL, PLE_DIM))],
        out_shape=[jax.ShapeDtypeStruct((s_len, D_MODEL), F32), jax.ShapeDtypeStruct((1, 128), F32), gain, gain,
                   jax.ShapeDtypeStruct((D_MODEL, D_MODEL), BF16), jax.ShapeDtypeStruct((D_MODEL, PLE_DIM), BF16)],
        scratch_shapes=[pltpu.VMEM((D_MODEL, D_MODEL), F32), pltpu.VMEM((D_MODEL, PLE_DIM), F32)],
        compiler_params=_params("arbitrary"),
    )(h3, target, gf, gate, pp, h2, g_ple, wpg, hn3, p)


def _mm_norm_bwd(parts, h, g, dres, name, exchange=(), dw_lhs=None):
    s_len = h.shape[0]
    tm = ROW_TILE // 2
    n_parts = len(parts)
    n_steps = s_len // tm
    has_dw = dw_lhs is not None

    def body(*refs):
        a_refs = refs[0:2 * n_parts:2]
        w_refs = refs[1:2 * n_parts:2]
        h_ref, g_ref, r_ref = refs[2 * n_parts:2 * n_parts + 3]
        rest = refs[2 * n_parts + 3:]
        step = pl.program_id(0)
        if has_dw:
            lhs_ref, o_ref, dg_ref, dw_ref, acc_ref = rest
        else:
            o_ref, dg_ref = rest

        @pl.when(step == 0)
        def _():
            dg_ref[...] = jnp.zeros_like(dg_ref)
            if has_dw:
                acc_ref[...] = jnp.zeros_like(acc_ref)

        dn = jnp.dot(a_refs[0][...], w_refs[0][...], preferred_element_type=F32)
        for a_ref, w_ref in zip(a_refs[1:], w_refs[1:]):
            dn = dn + jnp.dot(a_ref[...], w_ref[...], preferred_element_type=F32)
        dh, dg_rows = _norm_bwd(dn, h_ref[...], g_ref[...])
        out = r_ref[...] + dh
        o_ref[...] = out
        dg_ref[...] += jnp.sum(dg_rows, axis=0, keepdims=True)
        if has_dw:
            acc_ref[...] += lax.dot_general(lhs_ref[...], out.astype(BF16), _TN, preferred_element_type=F32)

            @pl.when(step == n_steps - 1)
            def _():
                dw_ref[...] = acc_ref[...].astype(BF16)

    in_specs, args = [], []
    for a, w in parts:
        in_specs += [_rows(tm, a.shape[1]), _whole(w.shape)]
        args += [a, w]
    in_specs += [_rows(tm, D_MODEL), _whole((1, D_MODEL)), _rows(tm, D_MODEL)]
    args += [h, g, dres]
    out_specs = [_rows(tm, D_MODEL), _whole((1, D_MODEL))]
    out_shape = [jax.ShapeDtypeStruct((s_len, D_MODEL), F32), jax.ShapeDtypeStruct((1, D_MODEL), F32)]
    scratch = []
    if has_dw:
        m = dw_lhs.shape[1]
        in_specs.append(_rows(tm, m))
        args.append(dw_lhs)
        out_specs.append(_whole((m, D_MODEL)))
        out_shape.append(jax.ShapeDtypeStruct((m, D_MODEL), BF16))
        scratch.append(pltpu.VMEM((m, D_MODEL), F32))
    return _call(
        body, name=name, grid=(n_steps,), in_specs=in_specs, out_specs=out_specs, out_shape=out_shape,
        scratch_shapes=scratch, semantics=("arbitrary",), args=tuple(args), exchange=exchange)


def _ffn_down_bwd(dh, wdt, a, b, exchange=()):
    s_len = dh.shape[0]
    tm = ROW_TILE // 2

    def body(dh_ref, w_ref, a_ref, b_ref, da_ref, db_ref):
        dhb = dh_ref[...].astype(BF16)
        for j in range(D_FF // 256):
            cols = slice(j * 256, (j + 1) * 256)
            dact = lax.dot_general(dhb, w_ref[cols, :], _NT, preferred_element_type=F32)
            av = a_ref[:, cols].astype(F32)
            bv = b_ref[:, cols].astype(F32)
            sig = jax.nn.sigmoid(av)
            da_ref[:, cols] = (dact * bv * sig * (1.0 + av * (1.0 - sig))).astype(BF16)
            db_ref[:, cols] = (dact * av * sig).astype(BF16)

    wide = jax.ShapeDtypeStruct((s_len, D_FF), BF16)
    return _call(
        body, name="ffn_down_bwd", grid=(s_len // tm,),
        in_specs=[_rows(tm, D_MODEL), _whole((D_FF, D_MODEL)), _rows(tm, D_FF), _rows(tm, D_FF)],
        out_specs=[_rows(tm, D_FF), _rows(tm, D_FF)],
        out_shape=[wide, wide],
        semantics=("parallel",), args=(dh, wdt, a, b), exchange=exchange)


def _outproj_bwd(dh1, woutt, yb, gb, head_sum, exchange=()):
    s_len = dh1.shape[0]
    tm = ROW_TILE
    n_br = len(DILATIONS)

    def body(dh_ref, w_ref, yb_ref, gb_ref, e_ref, dya_ref, dgb_ref, *rest):
        do_refs, dd_refs = rest[:n_br], rest[n_br:2 * n_br]
        do_nat, dd_nat = rest[2 * n_br:]

        @pl.when(pl.program_id(0) == 0)
        def _():
            dgb_ref[...] = jnp.zeros_like(dgb_ref)

        dhb = dh_ref[...].astype(BF16)
        dya_ref[...] = lax.dot_general(dhb, w_ref[:WIDTH_A, :], _NT, preferred_element_type=F32)
        dyn = lax.dot_general(dhb, w_ref[WIDTH_A:, :], _NT, preferred_element_type=F32)
        ybv = yb_ref[...]
        dyb, dg_rows = _norm_bwd(dyn, ybv, gb_ref[...])
        dgb_ref[...] += jnp.sum(dg_rows, axis=0, keepdims=True)
        prod = dyb * ybv
        for cb in range(N_PAIRS):
            cols = slice(cb * 128, (cb + 1) * 128)
            pc = prod[:, cols]
            hi = pc.astype(BF16)
            lo = (pc - hi.astype(F32)).astype(BF16)
            do_nat[cb] = dyb[:, cols]
            dd_nat[cb] = (jnp.dot(hi, e_ref[...], preferred_element_type=F32)
                          + jnp.dot(lo, e_ref[...], preferred_element_type=F32))
            for i, d in enumerate(DILATIONS):
                _to_sub(do_nat, cb, do_refs[i], cb, d, tm)
                _to_sub(dd_nat, cb, dd_refs[i], cb, d, tm)

    subs = [_sub_spec(d, N_PAIRS, tm) for d in DILATIONS]
    return _call(
        body, name="outproj_bwd", grid=(s_len // tm,),
        in_specs=[_rows(tm, D_MODEL), _whole((D_MODEL, D_MODEL)), _rows(tm, WIDTH_B), _whole((1, WIDTH_B)), _whole((128, 128))],
        out_specs=[_rows(tm, WIDTH_A), _whole((1, WIDTH_B))] + subs + subs,
        out_shape=[jax.ShapeDtypeStruct((s_len, WIDTH_A), F32), jax.ShapeDtypeStruct((1, WIDTH_B), F32)]
        + [_sub_shape(s_len, d, N_PAIRS, BF16) for d in DILATIONS] + [_sub_shape(s_len, d, N_PAIRS, F32) for d in DILATIONS],
        scratch_shapes=[pltpu.VMEM((N_PAIRS, tm, 128), F32), pltpu.VMEM((N_PAIRS, tm, 128), F32)],
        semantics=("arbitrary",), args=(dh1, woutt, yb, gb, head_sum), exchange=exchange)


def _attn_bwd(qkv, do, lse, dd, d, exchange=()):
    sd = qkv.shape[2]
    tile, nb, n_tiles = _attn_geometry(sd)
    last_block = sd // CHUNK - 1

    def nxt(n):
        return jnp.minimum((n + 1) * nb, last_block)

    def one_pair(q_ref, qn_ref, k_ref, v_ref, do_ref, don_ref, l_ref, ln_ref, dd_ref, ddn_ref,
                 dq_ref, dk_ref, dv_ref, carry_ref):
        n = pl.program_id(2)

        @pl.when(n == 0)
        def _():
            carry_ref[...] = jnp.zeros_like(carry_ref)

        head_a = lax.broadcasted_iota(jnp.int32, (CHUNK, 128), 1) < HEAD_DIM
        col = lax.broadcasted_iota(jnp.int32, (CHUNK, 4 * CHUNK), 1)
        qi = col % CHUNK
        ki = lax.broadcasted_iota(jnp.int32, (CHUNK, 4 * CHUNK), 0)
        is_after = col >= 2 * CHUNK
        mask = (is_after & (ki >= qi)) | (jnp.logical_not(is_after) & (qi >= ki))
        mask_last = mask & jnp.logical_or(jnp.logical_not(is_after), n < n_tiles - 1)
        dq_acc = [carry_ref[...]] + [jnp.zeros((CHUNK, 128), F32) for _ in range(nb)]

        def as_rows(x):
            xt = x.T
            return jnp.concatenate([xt[0:1, :], xt[HEAD_DIM:HEAD_DIM + 1, :]], axis=1)

        def block(ref, next_ref, j):
            return ref[j * CHUNK:(j + 1) * CHUNK, :] if j < nb else next_ref[...]

        q_st = [jnp.concatenate(_both_heads(block(q_ref, qn_ref, j), head_a), axis=0) for j in range(nb + 1)]
        do_st = [jnp.concatenate(_both_heads(block(do_ref, don_ref, j), head_a), axis=0) for j in range(nb + 1)]
        l_rows = [as_rows(block(l_ref, ln_ref, j)) for j in range(nb + 1)]
        dd_rows = [as_rows(block(dd_ref, ddn_ref, j)) for j in range(nb + 1)]

        for j in range(nb):
            rows = slice(j * CHUNK, (j + 1) * CHUNK)
            kj = k_ref[rows, :]
            vj = v_ref[rows, :]
            msk = mask if j + 1 < nb else mask_last
            qs = jnp.concatenate([q_st[j], q_st[j + 1]], axis=0)
            dos = jnp.concatenate([do_st[j], do_st[j + 1]], axis=0)
            ls = jnp.concatenate([l_rows[j], l_rows[j + 1]], axis=1)
            dds = jnp.concatenate([dd_rows[j], dd_rows[j + 1]], axis=1)
            st = lax.dot_general(kj, qs, _NT, preferred_element_type=F32)
            pt = jnp.exp(jnp.where(msk, st - ls, NEG))
            dpt = lax.dot_general(vj, dos, _NT, preferred_element_type=F32)
            dst = (pt * (dpt - dds)).astype(BF16)
            dv_ref[rows, :] = jnp.dot(pt.astype(BF16), dos, preferred_element_type=F32).astype(BF16)
            dk_ref[rows, :] = jnp.dot(dst, qs, preferred_element_type=F32).astype(BF16)
            dqs = lax.dot_general(dst, kj, _TN, preferred_element_type=F32)
            dq_acc[j] = dq_acc[j] + jnp.where(head_a, dqs[:CHUNK], dqs[CHUNK:2 * CHUNK])
            dq_acc[j + 1] = dq_acc[j + 1] + jnp.where(head_a, dqs[2 * CHUNK:3 * CHUNK], dqs[3 * CHUNK:])
        for j in range(nb):
            dq_ref[j * CHUNK:(j + 1) * CHUNK, :] = dq_acc[j].astype(BF16)
        carry_ref[...] = dq_acc[nb]

    same = lambda n: n
    grad = jax.ShapeDtypeStruct((d, N_PAIRS, sd, 128), BF16)
    return _call(
        _per_pair(one_pair), name=f"attn_bwd_d{d}", grid=(d, N_PAIRS // PAIRS_PER_STEP, n_tiles),
        in_specs=[_attn_spec(0, tile, same), _attn_spec(0, CHUNK, nxt), _attn_spec(N_PAIRS, tile, same),
                  _attn_spec(2 * N_PAIRS, tile, same), _attn_spec(0, tile, same), _attn_spec(0, CHUNK, nxt),
                  _attn_spec(0, tile, same), _attn_spec(0, CHUNK, nxt), _attn_spec(0, tile, same), _attn_spec(0, CHUNK, nxt)],
        out_specs=[_attn_spec(0, tile, same)] * 3,
        out_shape=[grad, grad, grad],
        scratch_shapes=[pltpu.VMEM((PAIRS_PER_STEP, CHUNK, 128), F32)],
        semantics=("parallel", "parallel", "arbitrary"), args=(qkv, qkv, qkv, qkv, do, do, lse, lse, dd, dd), exchange=exchange)


def _sgu_bwd(ua, sw, b2, gs, ga, dya_n):
    s_len = ua.shape[0]
    tm = ROW_TILE

    def body(ua_ref, sw_ref, b2_ref, gs_ref, ga_ref, dy_ref, dua_ref, dsw_ref, db2_ref, dgs_ref, dga_ref):
        @pl.when(pl.program_id(0) == 0)
        def _():
            dsw_ref[...] = jnp.zeros_like(dsw_ref)
            db2_ref[...] = jnp.zeros_like(db2_ref)
            dgs_ref[...] = jnp.zeros_like(dgs_ref)
            dga_ref[...] = jnp.zeros_like(dga_ref)

        u, va, ug, xhat, rstd, vn = _sgu_core(ua_ref, gs_ref)
        wm, keep = _sgu_mix_weights(sw_ref)
        head = lax.broadcasted_iota(jnp.int32, (CHUNK, WIDTH_A), 1) // HEAD_DIM
        gav = ga_ref[...]
        gsv = gs_ref[...]
        dga = jnp.zeros((1, WIDTH_A), F32)
        dgs = jnp.zeros((1, WIDTH_A), F32)
        db2 = jnp.zeros((CHUNK, WIDTH_A), F32)
        dsw = [jnp.zeros((CHUNK, CHUNK), F32) for _ in range(4)]
        for c in range(tm // CHUNK):
            rows = slice(c * CHUNK, (c + 1) * CHUNK)
            vnc = vn[rows]
            vnb = vnc.astype(BF16)
            mixed = b2_ref[...]
            for h in range(4):
                mixed = mixed + jnp.dot(wm[h], jnp.where(head == h, vnc, 0.0).astype(BF16), preferred_element_type=F32)
            ugc = ug[rows]
            dya, dga_rows = _norm_bwd(dy_ref[rows, :], ugc * mixed, gav)
            dga = dga + jnp.sum(dga_rows, axis=0, keepdims=True)
            dmixed = dya * ugc
            db2 = db2 + dmixed
            dvn = jnp.zeros((CHUNK, WIDTH_A), F32)
            for h in range(4):
                dmh = jnp.where(head == h, dmixed, 0.0).astype(BF16)
                dsw[h] = dsw[h] + lax.dot_general(dmh, vnb, _NT, preferred_element_type=F32)
                dvn = dvn + lax.dot_general(wm[h], dmh, _TN, preferred_element_type=F32)
            xh = xhat[rows]
            dgs = dgs + jnp.sum(dvn * xh, axis=0, keepdims=True)
            dxh = dvn * gsv
            dvg = rstd[rows] * (dxh - jnp.mean(dxh, axis=-1, keepdims=True) - xh * jnp.mean(dxh * xh, axis=-1, keepdims=True))
            dua_ref[rows, :WIDTH_A] = (dya * mixed * _gelu_grad(u[rows])).astype(BF16)
            dua_ref[rows, WIDTH_A:] = (dvg * _gelu_grad(va[rows])).astype(BF16)
        for h in range(4):
            dsw_ref[h] += jnp.where(keep, dsw[h], 0.0)
        db2_ref[...] += db2
        dgs_ref[...] += dgs
        dga_ref[...] += dga

    return pl.pallas_call(
        body, name="sgu_bwd", grid=(s_len // tm,),
        in_specs=[_rows(tm, 2 * WIDTH_A), _whole((4, CHUNK, CHUNK)), _whole((CHUNK, WIDTH_A)), _whole((1, WIDTH_A)),
                  _whole((1, WIDTH_A)), _rows(tm, WIDTH_A)],
        out_specs=[_rows(tm, 2 * WIDTH_A), _whole((4, CHUNK, CHUNK)), _whole((CHUNK, WIDTH_A)), _whole((1, WIDTH_A)), _whole((1, WIDTH_A))],
        out_shape=[jax.ShapeDtypeStruct((s_len, 2 * WIDTH_A), BF16), jax.ShapeDtypeStruct((4, CHUNK, CHUNK), F32),
                   jax.ShapeDtypeStruct((CHUNK, WIDTH_A), F32), jax.ShapeDtypeStruct((1, WIDTH_A), F32),
                   jax.ShapeDtypeStruct((1, WIDTH_A), F32)],
        compiler_params=_params("arbitrary"),
    )(ua, sw, b2, gs, ga, dya_n)


def _dproj(dua, dqs, dks, dvs, cos, sin):
    s_len = dua.shape[0]
    tm = ROW_TILE
    n_br = len(DILATIONS)

    def body(dua_ref, *rest):
        groups = [rest[g * n_br:(g + 1) * n_br] for g in range(3)]
        cos_ref, sin_ref, out_ref, acc = rest[3 * n_br:]
        out_ref[:, :2 * WIDTH_A] = dua_ref[...]
        c = cos_ref[...]
        s = sin_ref[...]
        first_half = (lax.broadcasted_iota(jnp.int32, (tm, 128), 1) % HEAD_DIM) < HEAD_DIM // 2
        for g, refs in enumerate(groups):
            for cb in range(N_PAIRS):
                for i, d in enumerate(DILATIONS):
                    _from_sub(refs[i], cb, acc, 0, d, tm, accumulate=i > 0)
                t = acc[0]
                if g < 2:
                    t = (t * c - _swap_halves(t, first_half) * s) * (0.125 if g == 0 else 1.0)
                col = 2 * WIDTH_A + g * WIDTH_B + cb * 128
                out_ref[:, col:col + 128] = t.astype(BF16)

    subs = [_sub_spec(d, N_PAIRS, tm) for d in DILATIONS]
    return pl.pallas_call(
        body, name="dproj", grid=(s_len // tm,),
        in_specs=[_rows(tm, 2 * WIDTH_A)] + subs * 3 + [_rows(tm, 128), _rows(tm, 128)],
        out_specs=_rows(tm, IN_COLS),
        out_shape=jax.ShapeDtypeStruct((s_len, IN_COLS), BF16),
        scratch_shapes=[pltpu.VMEM((1, tm, 128), F32)],
        compiler_params=_params("parallel"),
    )(dua, *dqs, *dks, *dvs, cos, sin)


def _mm_tn(a, b, name, exchange=()):
    s_len, m = a.shape
    n = b.shape[1]
    tk = ROW_TILE
    tm = m if m <= 512 else (1408 if m == D_FF else 512)
    n_k = s_len // tk

    def body(a_ref, b_ref, o_ref, acc_ref):
        k = pl.program_id(1)

        @pl.when(k == 0)
        def _():
            acc_ref[...] = jnp.zeros_like(acc_ref)

        acc_ref[...] += lax.dot_general(a_ref[...].astype(BF16), b_ref[...].astype(BF16), _TN, preferred_element_type=F32)

        @pl.when(k == n_k - 1)
        def _():
            o_ref[...] = acc_ref[...].astype(BF16)

    (grad,), received = _call(
        body, name=name, grid=(m // tm, n_k),
        in_specs=[pl.BlockSpec((tk, tm), lambda i, k: (k, i)), pl.BlockSpec((tk, n), lambda i, k: (k, 0))],
        out_specs=[pl.BlockSpec((tm, n), lambda i, k: (i, 0))],
        out_shape=[jax.ShapeDtypeStruct((m, n), BF16)],
        scratch_shapes=[pltpu.VMEM((tm, n), F32)],
        semantics=("parallel", "arbitrary"), args=(a, b), exchange=exchange)
    return grad, received


def _position():
    x, y, c = lax.axis_index("x"), lax.axis_index("y"), lax.axis_index("c")
    return x, y, c, 4 * x + 2 * y + c


def _peer(x, y, c, rel):
    return (x ^ ((rel >> 2) & 1), y ^ ((rel >> 1) & 1), c ^ (rel & 1))


def _exchange_out_shape(kind, arr):
    return jax.ShapeDtypeStruct(((N_DEV,) + arr.shape) if kind == "gather" else arr.shape, arr.dtype)


def _exchange_sems(n_items):
    return [pltpu.SemaphoreType.DMA((n_items, N_DEV)), pltpu.SemaphoreType.DMA((n_items, N_DEV)), pltpu.SemaphoreType.DMA((n_items,))]


def _exchange_copies(kinds, srcs, dsts, sems):
    send_sems, recv_sems, local_sems = sems
    x, y, c, me = _position()
    local, sends, recvs = [], [], []
    for k, (kind, src, dst) in enumerate(zip(kinds, srcs, dsts)):
        own = src if kind == "gather" else src.at[me]
        local.append(pltpu.make_async_copy(own, dst.at[me], local_sems.at[k]))
        for rel in range(1, N_DEV):
            going = src if kind == "gather" else src.at[me ^ rel]
            common = dict(send_sem=send_sems.at[k, rel], recv_sem=recv_sems.at[k, rel],
                          device_id=_peer(x, y, c, rel), device_id_type=MESH)
            sends.append(pltpu.make_async_remote_copy(src_ref=going, dst_ref=dst.at[me], **common))
            recvs.append(pltpu.make_async_remote_copy(src_ref=own, dst_ref=dst.at[me ^ rel], **common))
    return local, sends, recvs


def _exchange_start(kinds, srcs, dsts, sems):
    local, sends, _ = _exchange_copies(kinds, srcs, dsts, sems)
    for cp in local + sends:
        cp.start()


def _exchange_finish(kinds, srcs, dsts, sems):
    local, sends, recvs = _exchange_copies(kinds, srcs, dsts, sems)
    for cp in recvs:
        cp.wait_recv()
    for cp in sends:
        cp.wait_send()
    for cp in local:
        cp.wait()


def _exchange_only(items, name):
    kinds = [k for k, _ in items]
    n = len(items)

    def body(*refs):
        srcs, dsts, sems = refs[:n], refs[n:2 * n], refs[2 * n:]
        _exchange_start(kinds, srcs, dsts, sems)
        _exchange_finish(kinds, srcs, dsts, sems)

    any_spec = pl.BlockSpec(memory_space=pl.ANY)
    return pl.pallas_call(
        body, name=name, in_specs=[any_spec] * n, out_specs=[any_spec] * n,
        out_shape=[_exchange_out_shape(k, a) for k, a in items],
        scratch_shapes=_exchange_sems(n),
        compiler_params=pltpu.CompilerParams(has_side_effects=True),
    )(*[a for _, a in items])


def _gather_two_level_with_rope_tables(shard, inv_freq, s_len, name):
    rows = ROW_TILE

    def body(inv_ref, src, cos_ref, sin_ref, dst, send_sems, recv_sems, local_sem):
        x, y, c, me = _position()
        sibling = (x, y, 1 - c)
        chips = [(1 - x, y), (x, 1 - y), (1 - x, 1 - y)]

        def block(px, py, pc):
            return dst.at[4 * px + 2 * py + pc]

        def copy(k, blk, to, src_ref=None):
            return pltpu.make_async_remote_copy(
                src_ref=block(*blk) if src_ref is None else src_ref, dst_ref=block(*blk),
                send_sem=send_sems.at[k], recv_sem=recv_sems.at[k], device_id=to, device_id_type=MESH)

        mine = pltpu.make_async_copy(src, dst.at[me], local_sem)
        mine.start()
        first = [copy(0, (x, y, c), sibling, src)] + [copy(1 + j, (x, y, c), (*chip, c), src) for j, chip in enumerate(chips)]
        for cp in first:
            cp.start()

        inv = inv_ref[...]
        lane = lax.broadcasted_iota(jnp.int32, (rows, 128), 1)
        sign = jnp.where((lane // (HEAD_DIM // 2)) % 2 == 0, -1.0, 1.0)
        row = lax.broadcasted_iota(jnp.int32, (rows, 128), 0)

        @pl.loop(0, s_len // rows)
        def _(i):
            at = pl.multiple_of(i * rows, rows)
            ang = (row + at).astype(F32) * inv
            cos_ref[pl.ds(at, rows), :] = jnp.cos(ang)
            sin_ref[pl.ds(at, rows), :] = jnp.sin(ang) * sign

        passed = [copy(4 + j, (*chip, c), sibling) for j, chip in enumerate(chips)]
        for j, chip in enumerate(chips):
            copy(1 + j, (*chip, c), (x, y, c)).wait_recv()
            passed[j].start()
        copy(0, (x, y, 1 - c), (x, y, c)).wait_recv()
        for j, chip in enumerate(chips):
            copy(4 + j, (*chip, 1 - c), (x, y, c)).wait_recv()
        for cp in first + passed:
            cp.wait_send()
        mine.wait()

    any_spec = pl.BlockSpec(memory_space=pl.ANY)
    vmem = pl.BlockSpec(memory_space=pltpu.VMEM)
    table = jax.ShapeDtypeStruct((s_len, 128), F32)
    return pl.pallas_call(
        body, name=name, in_specs=[vmem, any_spec], out_specs=[vmem, vmem, any_spec],
        out_shape=[table, table, _exchange_out_shape("gather", shard)],
        scratch_shapes=[pltpu.SemaphoreType.DMA((N_DEV - 1,)), pltpu.SemaphoreType.DMA((N_DEV - 1,)), pltpu.SemaphoreType.DMA],
        compiler_params=pltpu.CompilerParams(has_side_effects=True, vmem_limit_bytes=V7X_VMEM_LIMIT_BYTES),
    )(inv_freq, shard)


def _call(body, *, name, grid, in_specs, out_specs, out_shape, args, scratch_shapes=(), semantics, exchange=()):
    if not exchange:
        outs = pl.pallas_call(body, name=name, grid=grid, in_specs=in_specs, out_specs=out_specs, out_shape=out_shape,
                              scratch_shapes=list(scratch_shapes), compiler_params=_params(*semantics))(*args)
        return outs, []
    kinds = [k for k, _ in exchange]
    n_in, n_out, n_x, n_scr = len(in_specs), len(out_specs), len(exchange), len(scratch_shapes)

    def wrapped(*refs):
        ins, refs = refs[:n_in], refs[n_in:]
        srcs, refs = refs[:n_x], refs[n_x:]
        outs, refs = refs[:n_out], refs[n_out:]
        dsts, refs = refs[:n_x], refs[n_x:]
        scratch, sems = refs[:n_scr], refs[n_scr:]
        ids = [pl.program_id(a) for a in range(len(grid))]
        first = functools.reduce(jnp.logical_and, [i == 0 for i in ids])
        last = functools.reduce(jnp.logical_and, [i == g - 1 for i, g in zip(ids, grid)])

        @pl.when(first)
        def _():
            _exchange_start(kinds, srcs, dsts, sems)

        body(*ins, *outs, *scratch)

        @pl.when(last)
        def _():
            _exchange_finish(kinds, srcs, dsts, sems)

    any_spec = pl.BlockSpec(memory_space=pl.ANY)
    outs = pl.pallas_call(
        wrapped, name=name, grid=grid,
        in_specs=list(in_specs) + [any_spec] * n_x, out_specs=list(out_specs) + [any_spec] * n_x,
        out_shape=list(out_shape) + [_exchange_out_shape(k, a) for k, a in exchange],
        scratch_shapes=list(scratch_shapes) + _exchange_sems(n_x),
        compiler_params=pltpu.CompilerParams(dimension_semantics=("arbitrary",) * len(grid),
                                             vmem_limit_bytes=V7X_VMEM_LIMIT_BYTES, has_side_effects=True),
    )(*args, *[a for _, a in exchange])
    return outs[:n_out], outs[n_out:]


def _adamw_math(w, g, m, v):
    m = ADAM_B1 * m + (1.0 - ADAM_B1) * g
    v = ADAM_B2 * v + (1.0 - ADAM_B2) * (g * g)
    m_hat = m / (1.0 - ADAM_B1 ** ADAM_STEP)
    v_hat = v / (1.0 - ADAM_B2 ** ADAM_STEP)
    return -ADAM_LR * (m_hat / (jnp.sqrt(v_hat) + ADAM_EPS) + ADAM_WD * w), m, v


def _adamw(parts, w, m, v, name):
    rows, cols = w.shape
    tm = 256 if rows % 256 == 0 and rows > 256 else rows

    def body(p_ref, w_ref, m_ref, v_ref, g_ref, d_ref, nm_ref, nv_ref):
        g = p_ref[0].astype(F32)
        for j in range(1, N_DEV):
            g = g + p_ref[j].astype(F32)
        delta, nm, nv = _adamw_math(w_ref[...], g, m_ref[...], v_ref[...])
        g_ref[...] = g
        d_ref[...] = delta
        nm_ref[...] = nm
        nv_ref[...] = nv

    shard = jax.ShapeDtypeStruct((rows, cols), F32)
    return pl.pallas_call(
        body, name=name, grid=(rows // tm,),
        in_specs=[pl.BlockSpec((N_DEV, tm, cols), lambda i: (0, i, 0))] + [_rows(tm, cols)] * 3,
        out_specs=[_rows(tm, cols)] * 4,
        out_shape=[shard] * 4,
        compiler_params=_params("parallel"),
    )(parts, w, m, v)


_SMALL = ("mix_norm_g", "sgu_w", "sgu_b", "sgu_norm_g", "out_norm_a", "out_norm_b", "ffn_norm_g", "ple_norm_g", "final_norm_g")
_BIG = ("w_in", "w_out", "w_gate", "w_up", "w_down", "w_ple_gate", "w_ple_proj")
_COLUMN_SHARDED = ("w_in", "w_gate", "w_up", "w_ple_proj")
_ORDER = ("mix_norm_g", "w_in", "sgu_w", "sgu_b", "sgu_norm_g", "out_norm_a", "out_norm_b", "w_out", "ffn_norm_g",
          "w_gate", "w_up", "w_down", "ple_norm_g", "w_ple_gate", "w_ple_proj", "final_norm_g")


def _pack_small(values, names=_SMALL):
    flat = jnp.concatenate([values[n].reshape(-1).astype(F32) for n in names])
    pad = (-flat.shape[0]) % (8 * 128)
    return jnp.pad(flat, (0, pad)).reshape(-1, 128)


def _unpack_small(packed, like):
    flat = packed.reshape(-1)
    out, at = {}, 0
    for n in _SMALL:
        size = like[n].size
        out[n] = flat[at:at + size].reshape(like[n].shape)
        at += size
    return out


def _own_orientation(name, value):
    return value[0].T if name in _COLUMN_SHARDED else value[0]


def _reference_orientation(name, value):
    return (value.T if name in _COLUMN_SHARDED else value)[None]


def _full_from_gathered(gathered):
    return gathered.reshape(N_DEV * gathered.shape[1], gathered.shape[2])


def _sliced_for_devices(grad):
    return grad.reshape(N_DEV, grad.shape[0] // N_DEV, grad.shape[1])


def _rope_inv_freq():
    half = HEAD_DIM // 2
    inv = ROPE_THETA ** (-jnp.arange(half, dtype=F32) / half)
    return jnp.tile(inv, 128 // half)[None, :]


def _forward_backward(x, p, target, small, shards):
    def gather(*names):
        return [("gather", shards[n]) for n in names]

    def scatter(**grads):
        return [("scatter", _sliced_for_devices(g)) for g in grads.values()]

    full, parts = {}, {}
    s_len = x.shape[0]
    cos, sin, got = _gather_two_level_with_rope_tables(shards["w_in"], _rope_inv_freq(), s_len, "gather_w_in")
    full["w_in"] = _full_from_gathered(got)

    g_mix, g_ffn, g_ple = small["mix_norm_g"], small["ffn_norm_g"], small["ple_norm_g"]
    g_fin = small["final_norm_g"].reshape(1, D_MODEL)
    sw, gs, ga, gb = small["sgu_w"], small["sgu_norm_g"], small["out_norm_a"], small["out_norm_b"]
    b2 = jnp.repeat(small["sgu_b"].T, HEAD_DIM, axis=1)
    lane_head = jnp.arange(128) // HEAD_DIM
    head_sum = (lane_head[:, None] == lane_head[None, :]).astype(BF16)
    n_br = len(DILATIONS)

    def arrived(names, got):
        for n, g in zip(names, got):
            full[n] = _full_from_gathered(g)

    (ua, hn1, *qkv), got = _inproj(x, g_mix, full["w_in"], cos, sin, exchange=gather("w_gate"))
    arrived(("w_gate",), got)
    ya_n = _sgu_fwd(ua, sw, b2, gs, ga)
    half = shards["w_up"].shape[0] // 2
    riders = [[("gather", shards["w_up"][:half])], [("gather", shards["w_up"][half:])], gather("w_out")]
    branch, got = [], []
    for i, d in enumerate(DILATIONS):
        o_l, g = _attn_fwd(qkv[i], d, exchange=riders[i])
        branch.append(o_l)
        got += g
    arrived(("w_up", "w_out"), [jnp.concatenate(got[:2], axis=1), got[2]])
    (y, yb, *lse), _ = _combine([o for o, _ in branch], [l for _, l in branch], ya_n, gb)
    last_wave = ("w_down", "w_ple_gate", "w_ple_proj")
    (h1, a, b, act, hn2), got = _ffn_up(y, full["w_out"], x, g_ffn, full["w_gate"], full["w_up"], exchange=gather(*last_wave))
    arrived(last_wave, got)
    h2, h3, gate, pp, hn3 = _ffn_down_ple(act, full["w_down"], h1, g_ple, full["w_ple_gate"], p, full["w_ple_proj"])

    dh2, loss, d_fin, d_ple, g_ple_gate, g_ple_proj = _loss_ple_bwd(
        h3, target, g_fin, gate, pp, h2, g_ple, full["w_ple_gate"], hn3, p)
    g_down, (parts["w_ple_gate"], parts["w_ple_proj"]) = _mm_tn(
        act, dh2, "dw_down", exchange=scatter(w_ple_gate=g_ple_gate, w_ple_proj=g_ple_proj))
    (da, db), (parts["w_down"],) = _ffn_down_bwd(dh2, full["w_down"], a, b, exchange=scatter(w_down=g_down))
    g_gate, _ = _mm_tn(da, hn2, "dw_gate")
    g_up, _ = _mm_tn(db, hn2, "dw_up")
    (dh1, d_ffn, g_out), (parts["w_gate"],) = _mm_norm_bwd(
        [(da, full["w_gate"]), (db, full["w_up"])], h1, g_ffn, dh2, "ffn_up_bwd", exchange=scatter(w_gate=g_gate), dw_lhs=y)
    (dya_n, d_gb, *do_dd), (parts["w_out"],) = _outproj_bwd(dh1, full["w_out"], yb, gb, head_sum, exchange=scatter(w_out=g_out))
    grads_b = []
    for i, d in enumerate(DILATIONS):
        g3, got = _attn_bwd(qkv[i], do_dd[i], lse[i], do_dd[n_br + i], d, exchange=scatter(w_up=g_up) if i == 0 else ())
        grads_b.append(g3)
        if i == 0:
            (parts["w_up"],) = got
    dua, d_sw, d_b2, d_gs, d_ga = _sgu_bwd(ua, sw, b2, gs, ga, dya_n)
    dproj = _dproj(dua, [g[0] for g in grads_b], [g[1] for g in grads_b], [g[2] for g in grads_b], cos, sin)
    early = {
        "sgu_w": d_sw, "sgu_b": d_b2.reshape(CHUNK, 4, HEAD_DIM).sum(axis=-1).T, "sgu_norm_g": d_gs, "out_norm_a": d_ga,
        "out_norm_b": d_gb, "ffn_norm_g": d_ffn, "ple_norm_g": d_ple, "final_norm_g": d_fin,
    }
    g_in, (early_parts,) = _mm_tn(dproj, hn1, "dw_in", exchange=[("gather", _pack_small(early, _SMALL[1:]))])
    (dx, d_mix), (parts["w_in"],) = _mm_norm_bwd(
        [(dproj, full["w_in"])], x, g_mix, dh1, "inproj_bwd", exchange=scatter(w_in=g_in))
    late = jnp.concatenate([_pack_small({"mix_norm_g": d_mix}, _SMALL[:1]), jnp.broadcast_to(loss, (8, 128))])
    (late_parts,) = _exchange_only([("gather", late)], "gather_mix_norm_grad_and_loss")
    total_loss = jnp.sum(late_parts[:, 8, 0])
    return total_loss, dx, parts, jnp.concatenate([late_parts[:, :8], early_parts], axis=1)


def kernel(x, p, mix_norm_g, w_in, sgu_w, sgu_b, sgu_norm_g, out_norm_a, out_norm_b, w_out, ffn_norm_g, w_gate, w_up, w_down, ple_norm_g, w_ple_gate, w_ple_proj, final_norm_g, loss_target, m_mix_norm_g, m_w_in, m_sgu_w, m_sgu_b, m_sgu_norm_g, m_out_norm_a, m_out_norm_b, m_w_out, m_ffn_norm_g, m_w_gate, m_w_up, m_w_down, m_ple_norm_g, m_w_ple_gate, m_w_ple_proj, m_final_norm_g, v_mix_norm_g, v_w_in, v_sgu_w, v_sgu_b, v_sgu_norm_g, v_out_norm_a, v_out_norm_b, v_w_out, v_ffn_norm_g, v_w_gate, v_w_up, v_w_down, v_ple_norm_g, v_w_ple_gate, v_w_ple_proj, v_final_norm_g):
    given = dict(locals())
    weights = {n: given[n] for n in _ORDER}
    moments_m = {n: given["m_" + n] for n in _ORDER}
    moments_v = {n: given["v_" + n] for n in _ORDER}

    shards = {n: _own_orientation(n, weights[n]).astype(BF16) for n in _BIG}
    small = {n: (weights[n][0] if n in ("sgu_w", "sgu_b") else weights[n]) for n in _SMALL}

    loss, dx, parts, small_parts = _forward_backward(x[0], p[0, 0], loss_target[0], small, shards)

    small_like = {n: weights[n] for n in _SMALL}
    grads, deltas, new_m, new_v = {}, {}, {}, {}
    for n in _BIG:
        outs = _adamw(parts[n], _own_orientation(n, weights[n]), _own_orientation(n, moments_m[n]),
                      _own_orientation(n, moments_v[n]), "adamw_" + n)
        grads[n], deltas[n], new_m[n], new_v[n] = [_reference_orientation(n, o) for o in outs]
    g, d, nm, nv = _adamw(small_parts, _pack_small(small_like), _pack_small({n: moments_m[n] for n in _SMALL}),
                          _pack_small({n: moments_v[n] for n in _SMALL}), "adamw_small")
    for out, packed in ((grads, g), (deltas, d), (new_m, nm), (new_v, nv)):
        out.update(_unpack_small(packed, small_like))

    return (loss, dx[None], *[grads[n] for n in _ORDER], *[deltas[n] for n in _ORDER],
            *[new_m[n] for n in _ORDER], *[new_v[n] for n in _ORDER])
```

```python
import functools

import jax
import jax.numpy as jnp
from jax import lax
from jax.experimental import pallas as pl
from jax.experimental.pallas import tpu as pltpu

F32 = jnp.float32
BF16 = jnp.bfloat16

D_MODEL = 1024
WIDTH_A = 256
WIDTH_B = 768
D_FF = 2816
IN_COLS = 2 * WIDTH_A + 3 * WIDTH_B
PLE_DIM = 256
HEAD_DIM = 64
N_PAIRS = WIDTH_B // 128
CHUNK = 128
N_BACK = 128
DILATIONS = (1, 4, 16)
ROPE_THETA = 10000.0
EPS = 1e-6
N_DEV = 8

ADAM_LR = 0.001
ADAM_B1 = 0.9
ADAM_B2 = 0.999
ADAM_EPS = 1e-08
ADAM_WD = 0.01
ADAM_STEP = 10

V7X_VMEM_LIMIT_BYTES = 56 * 1024 * 1024
ROW_TILE = 512
MESH = pl.DeviceIdType.MESH
NEG = -1e30

_NT = (((1,), (1,)), ((), ()))
_TN = (((0,), (0,)), ((), ()))


def _params(*semantics):
    return pltpu.CompilerParams(dimension_semantics=semantics, vmem_limit_bytes=V7X_VMEM_LIMIT_BYTES)


def _rows(tm, width):
    return pl.BlockSpec((tm, width), lambda i: (i, 0))


def _whole(shape):
    return pl.BlockSpec(shape, lambda *_: (0,) * len(shape))


def _gelu(x):
    t = jnp.tanh(0.7978845608028654 * (x + 0.044715 * (x * x * x)))
    return 0.5 * x * (1.0 + t)


def _gelu_grad(x):
    t = jnp.tanh(0.7978845608028654 * (x + 0.044715 * (x * x * x)))
    return 0.5 * (1.0 + t) + 0.5 * x * (1.0 - t * t) * (0.7978845608028654 * (1.0 + 3.0 * 0.044715 * (x * x)))


def _rstd(x):
    return lax.rsqrt(jnp.mean(x * x, axis=-1, keepdims=True) + EPS)


def _norm_bwd(dn, h, g):
    r = _rstd(h)
    n = h * r
    t = dn * g
    return r * (t - n * jnp.mean(t * n, axis=-1, keepdims=True)), dn * n


def _swap_halves(x, first_half):
    return jnp.where(first_half, pltpu.roll(x, 96, 1), pltpu.roll(x, 32, 1))


def _sub_spec(d, n_cb, tm):
    return pl.BlockSpec((d, n_cb, tm // d, 128), lambda i: (0, 0, i, 0))


def _sub_shape(s_len, d, n_cb, dtype):
    return jax.ShapeDtypeStruct((d, n_cb, s_len // d, 128), dtype)


def _to_sub(stage_ref, cb_src, out_ref, cb_dst, d, tm):
    slab = stage_ref.at[cb_src]
    for r in range(d):
        out_ref[r, cb_dst] = slab[pl.ds(r, tm // d, stride=d), :].astype(out_ref.dtype)


def _from_sub(in_ref, cb_src, stage_ref, cb_dst, d, tm, accumulate=False):
    slab = stage_ref.at[cb_dst]
    for r in range(d):
        rows = pl.ds(r, tm // d, stride=d)
        val = in_ref[r, cb_src].astype(F32)
        slab[rows, :] = slab[rows, :] + val if accumulate else val


def _inproj(x, g, w, cos, sin, exchange=()):
    s_len = x.shape[0]
    tm = ROW_TILE
    n_cb = 3 * N_PAIRS

    def body(x_ref, g_ref, w_ref, cos_ref, sin_ref, ua_ref, hn_ref, *rest):
        sub_refs, stage = rest[:-1], rest[-1]
        xf = x_ref[...]
        hn = (xf * _rstd(xf) * g_ref[...]).astype(BF16)
        hn_ref[...] = hn
        c = cos_ref[...]
        s = sin_ref[...]
        first_half = (lax.broadcasted_iota(jnp.int32, (tm, 128), 1) % HEAD_DIM) < HEAD_DIM // 2
        for j in range(IN_COLS // 256):
            col = j * 256
            acc = lax.dot_general(hn, w_ref[col:col + 256, :], _NT, preferred_element_type=F32)
            if col < 2 * WIDTH_A:
                ua_ref[:, col:col + 256] = acc
                continue
            for half in range(2):
                cb = (col - 2 * WIDTH_A) // 128 + half
                t = acc[:, half * 128:(half + 1) * 128]
                if cb < 2 * N_PAIRS:
                    t = (t * c + _swap_halves(t, first_half) * s) * (0.125 if cb < N_PAIRS else 1.0)
                stage[cb] = t
        for cb in range(n_cb):
            for d, out_ref in zip(DILATIONS, sub_refs):
                _to_sub(stage, cb, out_ref, cb, d, tm)

    return _call(
        body, name="inproj", grid=(s_len // tm,),
        in_specs=[_rows(tm, D_MODEL), _whole((1, D_MODEL)), _whole((IN_COLS, D_MODEL)), _rows(tm, 128), _rows(tm, 128)],
        out_specs=[_rows(tm, 2 * WIDTH_A), _rows(tm, D_MODEL)] + [_sub_spec(d, n_cb, tm) for d in DILATIONS],
        out_shape=[jax.ShapeDtypeStruct((s_len, 2 * WIDTH_A), F32), jax.ShapeDtypeStruct((s_len, D_MODEL), BF16)]
        + [_sub_shape(s_len, d, n_cb, BF16) for d in DILATIONS],
        scratch_shapes=[pltpu.VMEM((n_cb, tm, 128), F32)],
        semantics=("parallel",), args=(x, g, w, cos, sin), exchange=exchange)


def _sgu_mix_weights(sw_ref):
    keep = lax.broadcasted_iota(jnp.int32, (CHUNK, CHUNK), 0) >= lax.broadcasted_iota(jnp.int32, (CHUNK, CHUNK), 1)
    return [jnp.where(keep, sw_ref[h], 0.0).astype(BF16) for h in range(4)], keep


def _sgu_core(ua_ref, gs_ref):
    u = ua_ref[:, :WIDTH_A]
    va = ua_ref[:, WIDTH_A:]
    vg = _gelu(va)
    xc = vg - jnp.mean(vg, axis=-1, keepdims=True)
    rstd = lax.rsqrt(jnp.mean(xc * xc, axis=-1, keepdims=True) + EPS)
    xhat = xc * rstd
    return u, va, _gelu(u), xhat, rstd, xhat * gs_ref[...]


def _sgu_fwd(ua, sw, b2, gs, ga):
    s_len = ua.shape[0]
    tm = ROW_TILE

    def body(ua_ref, sw_ref, b2_ref, gs_ref, ga_ref, out_ref):
        _, _, ug, _, _, vn = _sgu_core(ua_ref, gs_ref)
        wm, _ = _sgu_mix_weights(sw_ref)
        head = lax.broadcasted_iota(jnp.int32, (CHUNK, WIDTH_A), 1) // HEAD_DIM
        for c in range(tm // CHUNK):
            rows = slice(c * CHUNK, (c + 1) * CHUNK)
            vnc = vn[rows]
            mixed = b2_ref[...]
            for h in range(4):
                mixed = mixed + jnp.dot(wm[h], jnp.where(head == h, vnc, 0.0).astype(BF16), preferred_element_type=F32)
            ya = ug[rows] * mixed
            out_ref[rows, :] = (ya * _rstd(ya) * ga_ref[...]).astype(BF16)

    return pl.pallas_call(
        body, name="sgu_fwd", grid=(s_len // tm,),
        in_specs=[_rows(tm, 2 * WIDTH_A), _whole((4, CHUNK, CHUNK)), _whole((CHUNK, WIDTH_A)), _whole((1, WIDTH_A)), _whole((1, WIDTH_A))],
        out_specs=_rows(tm, WIDTH_A),
        out_shape=jax.ShapeDtypeStruct((s_len, WIDTH_A), BF16),
        compiler_params=_params("parallel"),
    )(ua, sw, b2, gs, ga)


def _attn_geometry(sd):
    tile = min(ROW_TILE, sd)
    return tile, tile // CHUNK, sd // tile


PAIRS_PER_STEP = 6


def _attn_spec(cb0, rows, row_index):
    return pl.BlockSpec((None, PAIRS_PER_STEP, rows, 128), lambda r, g, n: (r, cb0 // PAIRS_PER_STEP + g, row_index(n), 0))


assert PAIRS_PER_STEP == N_PAIRS


def _stats_spec(rows, row_index):
    return pl.BlockSpec((None, None, rows, 128), lambda r, g, n: (r, 0, row_index(n), 0))


def _stats_shape(sd, d):
    return jax.ShapeDtypeStruct((d, 1, sd, 128), F32)


def _both_heads(x, head_a):
    zero = jnp.zeros_like(x)
    return [jnp.where(head_a, x, zero), jnp.where(head_a, zero, x)]


def _attn_fwd(qkv, d, exchange=()):
    sd = qkv.shape[2]
    tile, nb, n_tiles = _attn_geometry(sd)

    def prev(n):
        return jnp.maximum(n * nb - 1, 0)

    def body(q_ref, k_ref, kp_ref, v_ref, vp_ref, o_ref, l_ref):
        for hp in range(PAIRS_PER_STEP):
            one_pair(hp, q_ref.at[hp], k_ref.at[hp], kp_ref.at[hp], v_ref.at[hp], vp_ref.at[hp], o_ref.at[hp], l_ref)

    def one_pair(hp, q_ref, k_ref, kp_ref, v_ref, vp_ref, o_ref, l_ref):
        n = pl.program_id(2)
        lane = lax.broadcasted_iota(jnp.int32, (CHUNK, 128), 1)
        head_a = lane < HEAD_DIM
        qi = lax.broadcasted_iota(jnp.int32, (2 * CHUNK, 2 * CHUNK), 0) % CHUNK
        kc = lax.broadcasted_iota(jnp.int32, (2 * CHUNK, 2 * CHUNK), 1)
        band = (kc >= qi) & (kc <= qi + N_BACK)
        for j in range(nb):
            rows = slice(j * CHUNK, (j + 1) * CHUNK)
            if j == 0:
                kcat = jnp.concatenate([kp_ref[...], k_ref[rows, :]], axis=0)
                vcat = jnp.concatenate([vp_ref[...], v_ref[rows, :]], axis=0)
                valid = band & jnp.logical_or(n > 0, kc >= CHUNK)
            else:
                kcat = k_ref[(j - 1) * CHUNK:(j + 1) * CHUNK, :]
                vcat = v_ref[(j - 1) * CHUNK:(j + 1) * CHUNK, :]
                valid = band
            q2 = jnp.concatenate(_both_heads(q_ref[rows, :], head_a), axis=0)
            s = lax.dot_general(q2, kcat, _NT, preferred_element_type=F32)
            s = jnp.where(valid, s, NEG)
            m = jnp.max(s, axis=-1, keepdims=True)
            p = jnp.exp(s - m)
            l = jnp.sum(p, axis=-1, keepdims=True)
            o2 = jnp.dot(p.astype(BF16), vcat, preferred_element_type=F32) / l
            lse2 = m + jnp.log(l)
            o_ref[rows, :] = jnp.where(head_a, o2[:CHUNK], o2[CHUNK:]).astype(BF16)
            others = l_ref[rows, :] if hp > 0 else jnp.zeros((CHUNK, 128), F32)
            l_ref[rows, :] = jnp.where(lane == 2 * hp, lse2[:CHUNK], jnp.where(lane == 2 * hp + 1, lse2[CHUNK:], others))

    same = lambda n: n
    return _call(
        body, name=f"attn_fwd_d{d}", grid=(d, N_PAIRS // PAIRS_PER_STEP, n_tiles),
        in_specs=[_attn_spec(0, tile, same), _attn_spec(N_PAIRS, tile, same), _attn_spec(N_PAIRS, CHUNK, prev),
                  _attn_spec(2 * N_PAIRS, tile, same), _attn_spec(2 * N_PAIRS, CHUNK, prev)],
        out_specs=[_attn_spec(0, tile, same), _stats_spec(tile, same)],
        out_shape=[jax.ShapeDtypeStruct((d, N_PAIRS, sd, 128), BF16), _stats_shape(sd, d)],
        semantics=("parallel", "parallel", "parallel"), args=(qkv, qkv, qkv, qkv, qkv), exchange=exchange)


def _combine(outs, lses, ya_n, gb, exchange=()):
    s_len = ya_n.shape[0]
    tm = ROW_TILE
    n_br = len(DILATIONS)

    def body(*refs):
        o_refs, l_refs = refs[:n_br], refs[n_br:2 * n_br]
        ya_ref, gb_ref, y_ref, yb_ref = refs[2 * n_br:2 * n_br + 4]
        lse_refs = refs[2 * n_br + 4:3 * n_br + 4]
        o_nat, w_nat, lse_nat = refs[3 * n_br + 4:]
        for i, d in enumerate(DILATIONS):
            _from_sub(l_refs[i], 0, w_nat, i, d, tm)
        ls = [w_nat[i] for i in range(n_br)]
        top = jnp.maximum(jnp.maximum(ls[0], ls[1]), ls[2])
        ws = [jnp.exp(l - top) for l in ls]
        den = ws[0] + ws[1] + ws[2]
        inv = 1.0 / den
        for i in range(n_br):
            w_nat[i] = ws[i] * inv
        lse_nat[0] = top + jnp.log(den)
        for d, lse_ref in zip(DILATIONS, lse_refs):
            _to_sub(lse_nat, 0, lse_ref, 0, d, tm)
        head_a = lax.broadcasted_iota(jnp.int32, (tm, 128), 1) < HEAD_DIM
        sumsq = jnp.zeros((tm, 1), F32)
        for cb in range(N_PAIRS):
            yb = jnp.zeros((tm, 128), F32)
            for i, d in enumerate(DILATIONS):
                _from_sub(o_refs[i], cb, o_nat, i, d, tm)
                w = jnp.where(head_a, w_nat[i, :, 2 * cb:2 * cb + 1], w_nat[i, :, 2 * cb + 1:2 * cb + 2])
                yb = yb + w * o_nat[i]
            yb_ref[:, cb * 128:(cb + 1) * 128] = yb
            sumsq = sumsq + jnp.sum(yb * yb, axis=-1, keepdims=True)
        r = lax.rsqrt(sumsq / WIDTH_B + EPS)
        y_ref[:, :WIDTH_A] = ya_ref[...]
        y_ref[:, WIDTH_A:] = (yb_ref[...] * r * gb_ref[...]).astype(BF16)

    stats = [_sub_spec(d, 1, tm) for d in DILATIONS]
    return _call(
        body, name="attn_combine", grid=(s_len // tm,),
        in_specs=[_sub_spec(d, N_PAIRS, tm) for d in DILATIONS] + stats + [_rows(tm, WIDTH_A), _whole((1, WIDTH_B))],
        out_specs=[_rows(tm, D_MODEL), _rows(tm, WIDTH_B)] + stats,
        out_shape=[jax.ShapeDtypeStruct((s_len, D_MODEL), BF16), jax.ShapeDtypeStruct((s_len, WIDTH_B), F32)]
        + [_sub_shape(s_len, d, 1, F32) for d in DILATIONS],
        scratch_shapes=[pltpu.VMEM((n_br, tm, 128), F32), pltpu.VMEM((n_br, tm, 128), F32), pltpu.VMEM((1, tm, 128), F32)],
        semantics=("parallel",), args=(*outs, *lses, ya_n, gb), exchange=exchange)


def _ffn_up(y, wout, x, g, wg, wu, exchange=()):
    s_len = x.shape[0]
    tm = ROW_TILE // 2

    def body(y_ref, wo_ref, x_ref, g_ref, wg_ref, wu_ref, h_ref, a_ref, b_ref, act_ref, hn_ref):
        hf = x_ref[...] + jnp.dot(y_ref[...], wo_ref[...], preferred_element_type=F32)
        h_ref[...] = hf
        hn = (hf * _rstd(hf) * g_ref[...]).astype(BF16)
        hn_ref[...] = hn
        for j in range(D_FF // 256):
            cols = slice(j * 256, (j + 1) * 256)
            a = lax.dot_general(hn, wg_ref[cols, :], _NT, preferred_element_type=F32)
            b = lax.dot_general(hn, wu_ref[cols, :], _NT, preferred_element_type=F32)
            a_ref[:, cols] = a.astype(BF16)
            b_ref[:, cols] = b.astype(BF16)
            act_ref[:, cols] = (a * jax.nn.sigmoid(a) * b).astype(BF16)

    wide = jax.ShapeDtypeStruct((s_len, D_FF), BF16)
    return _call(
        body, name="ffn_up", grid=(s_len // tm,),
        in_specs=[_rows(tm, D_MODEL), _whole((D_MODEL, D_MODEL)), _rows(tm, D_MODEL), _whole((1, D_MODEL)),
                  _whole((D_FF, D_MODEL)), _whole((D_FF, D_MODEL))],
        out_specs=[_rows(tm, D_MODEL), _rows(tm, D_FF), _rows(tm, D_FF), _rows(tm, D_FF), _rows(tm, D_MODEL)],
        out_shape=[jax.ShapeDtypeStruct((s_len, D_MODEL), F32), wide, wide, wide, jax.ShapeDtypeStruct((s_len, D_MODEL), BF16)],
        semantics=("parallel",), args=(y, wout, x, g, wg, wu), exchange=exchange)


def _ffn_down_ple(act, wd, h1, g, wpg, p, wpp):
    s_len = h1.shape[0]
    tm = ROW_TILE // 2

    def body(act_ref, wd_ref, h1_ref, g_ref, wpg_ref, p_ref, wpp_ref, h2_ref, h3_ref, gate_ref, pp_ref, hn_ref):
        hf = h1_ref[...] + jnp.dot(act_ref[...], wd_ref[...], preferred_element_type=F32)
        h2_ref[...] = hf
        hn = (hf * _rstd(hf) * g_ref[...]).astype(BF16)
        hn_ref[...] = hn
        gate = jax.nn.sigmoid(jnp.dot(hn, wpg_ref[...], preferred_element_type=F32))
        pp = lax.dot_general(p_ref[...].astype(BF16), wpp_ref[...], _NT, preferred_element_type=F32)
        h3_ref[...] = hf + gate * pp
        gate_ref[...] = gate.astype(BF16)
        pp_ref[...] = pp.astype(BF16)

    full = jax.ShapeDtypeStruct((s_len, D_MODEL), F32)
    half = jax.ShapeDtypeStruct((s_len, D_MODEL), BF16)
    return pl.pallas_call(
        body, name="ffn_down_ple", grid=(s_len // tm,),
        in_specs=[_rows(tm, D_FF), _whole((D_FF, D_MODEL)), _rows(tm, D_MODEL), _whole((1, D_MODEL)),
                  _whole((D_MODEL, D_MODEL)), _rows(tm, PLE_DIM), _whole((D_MODEL, PLE_DIM))],
        out_specs=[_rows(tm, D_MODEL)] * 5,
        out_shape=[full, full, half, half, half],
        compiler_params=_params("parallel"),
    )(act, wd, h1, g, wpg, p, wpp)


def _loss_ple_bwd(h3, target, gf, gate, pp, h2, g_ple, wpg, hn3, p):
    s_len = h3.shape[0]
    tm = ROW_TILE
    n_steps = s_len // tm

    def body(h_ref, t_ref, g_ref, gate_ref, pp_ref, h2_ref, gp_ref, w_ref, hn_ref, p_ref,
             dh2_ref, loss_ref, dg_ref, dgp_ref, dwg_ref, dwp_ref, acc_g, acc_p):
        step = pl.program_id(0)

        @pl.when(step == 0)
        def _():
            loss_ref[...] = jnp.zeros_like(loss_ref)
            dg_ref[...] = jnp.zeros_like(dg_ref)
            dgp_ref[...] = jnp.zeros_like(dgp_ref)
            acc_g[...] = jnp.zeros_like(acc_g)
            acc_p[...] = jnp.zeros_like(acc_p)

        hf = h_ref[...]
        gfv = g_ref[...]
        err = hf * _rstd(hf) * gfv - t_ref[...]
        loss_ref[...] += 0.5 * jnp.sum(jnp.sum(err * err, axis=-1, keepdims=True), axis=0, keepdims=True) / D_MODEL
        dh, dg_rows = _norm_bwd(err / D_MODEL, hf, gfv)
        dg_ref[...] += jnp.sum(dg_rows, axis=0, keepdims=True)
        gate = gate_ref[...].astype(F32)
        dz = (dh * pp_ref[...].astype(F32) * gate * (1.0 - gate)).astype(BF16)
        dpp = (dh * gate).astype(BF16)
        dn = lax.dot_general(dz, w_ref[...], _NT, preferred_element_type=F32)
        dh2, dgp_rows = _norm_bwd(dn, h2_ref[...], gp_ref[...])
        dh2 = dh + dh2
        dh2_ref[...] = dh2
        dgp_ref[...] += jnp.sum(dgp_rows, axis=0, keepdims=True)
        acc_g[...] += lax.dot_general(hn_ref[...], dz, _TN, preferred_element_type=F32)
        acc_p[...] += lax.dot_general(dpp, p_ref[...].astype(BF16), _TN, preferred_element_type=F32)

        @pl.when(step == n_steps - 1)
        def _():
            dwg_ref[...] = acc_g[...].astype(BF16)
            dwp_ref[...] = acc_p[...].astype(BF16)

    gain = jax.ShapeDtypeStruct((1, D_MODEL), F32)
    return pl.pallas_call(
        body, name="loss_ple_bwd", grid=(n_steps,),
        in_specs=[_rows(tm, D_MODEL), _rows(tm, D_MODEL), _whole((1, D_MODEL)), _rows(tm, D_MODEL), _rows(tm, D_MODEL),
                  _rows(tm, D_MODEL), _whole((1, D_MODEL)), _whole((D_MODEL, D_MODEL)), _rows(tm, D_MODEL),
                  _rows(tm, PLE_DIM)],
        out_specs=[_rows(tm, D_MODEL), _whole((1, 128)), _whole((1, D_MODEL)), _whole((1, D_MODEL)),
                   _whole((D_MODEL, D_MODEL)), _whole((D_MODEL, PLE_DIM))],
        out_shape=[jax.ShapeDtypeStruct((s_len, D_MODEL), F32), jax.ShapeDtypeStruct((1, 128), F32), gain, gain,
                   jax.ShapeDtypeStruct((D_MODEL, D_MODEL), BF16), jax.ShapeDtypeStruct((D_MODEL, PLE_DIM), BF16)],
        scratch_shapes=[pltpu.VMEM((D_MODEL, D_MODEL), F32), pltpu.VMEM((D_MODEL, PLE_DIM), F32)],
        compiler_params=_params("arbitrary"),
    )(h3, target, gf, gate, pp, h2, g_ple, wpg, hn3, p)


def _mm_norm_bwd(parts, h, g, dres, name, exchange=(), dw_lhs=None):
    s_len = h.shape[0]
    tm = ROW_TILE // 2
    n_parts = len(parts)
    n_steps = s_len // tm
    has_dw = dw_lhs is not None

    def body(*refs):
        a_refs = refs[0:2 * n_parts:2]
        w_refs = refs[1:2 * n_parts:2]
        h_ref, g_ref, r_ref = refs[2 * n_parts:2 * n_parts + 3]
        rest = refs[2 * n_parts + 3:]
        step = pl.program_id(0)
        if has_dw:
            lhs_ref, o_ref, dg_ref, dw_ref, acc_ref = rest
        else:
            o_ref, dg_ref = rest

        @pl.when(step == 0)
        def _():
            dg_ref[...] = jnp.zeros_like(dg_ref)
            if has_dw:
                acc_ref[...] = jnp.zeros_like(acc_ref)

        dn = jnp.dot(a_refs[0][...], w_refs[0][...], preferred_element_type=F32)
        for a_ref, w_ref in zip(a_refs[1:], w_refs[1:]):
            dn = dn + jnp.dot(a_ref[...], w_ref[...], preferred_element_type=F32)
        dh, dg_rows = _norm_bwd(dn, h_ref[...], g_ref[...])
        out = r_ref[...] + dh
        o_ref[...] = out
        dg_ref[...] += jnp.sum(dg_rows, axis=0, keepdims=True)
        if has_dw:
            acc_ref[...] += lax.dot_general(lhs_ref[...], out.astype(BF16), _TN, preferred_element_type=F32)

            @pl.when(step == n_steps - 1)
            def _():
                dw_ref[...] = acc_ref[...].astype(BF16)

    in_specs, args = [], []
    for a, w in parts:
        in_specs += [_rows(tm, a.shape[1]), _whole(w.shape)]
        args += [a, w]
    in_specs += [_rows(tm, D_MODEL), _whole((1, D_MODEL)), _rows(tm, D_MODEL)]
    args += [h, g, dres]
    out_specs = [_rows(tm, D_MODEL), _whole((1, D_MODEL))]
    out_shape = [jax.ShapeDtypeStruct((s_len, D_MODEL), F32), jax.ShapeDtypeStruct((1, D_MODEL), F32)]
    scratch = []
    if has_dw:
        m = dw_lhs.shape[1]
        in_specs.append(_rows(tm, m))
        args.append(dw_lhs)
        out_specs.append(_whole((m, D_MODEL)))
        out_shape.append(jax.ShapeDtypeStruct((m, D_MODEL), BF16))
        scratch.append(pltpu.VMEM((m, D_MODEL), F32))
    return _call(
        body, name=name, grid=(n_steps,), in_specs=in_specs, out_specs=out_specs, out_shape=out_shape,
        scratch_shapes=scratch, semantics=("arbitrary",), args=tuple(args), exchange=exchange)


def _ffn_down_bwd(dh, wdt, a, b, exchange=()):
    s_len = dh.shape[0]
    tm = ROW_TILE // 2

    def body(dh_ref, w_ref, a_ref, b_ref, da_ref, db_ref):
        dhb = dh_ref[...].astype(BF16)
        for j in range(D_FF // 256):
            cols = slice(j * 256, (j + 1) * 256)
            dact = lax.dot_general(dhb, w_ref[cols, :], _NT, preferred_element_type=F32)
            av = a_ref[:, cols].astype(F32)
            bv = b_ref[:, cols].astype(F32)
            sig = jax.nn.sigmoid(av)
            da_ref[:, cols] = (dact * bv * sig * (1.0 + av * (1.0 - sig))).astype(BF16)
            db_ref[:, cols] = (dact * av * sig).astype(BF16)

    wide = jax.ShapeDtypeStruct((s_len, D_FF), BF16)
    return _call(
        body, name="ffn_down_bwd", grid=(s_len // tm,),
        in_specs=[_rows(tm, D_MODEL), _whole((D_FF, D_MODEL)), _rows(tm, D_FF), _rows(tm, D_FF)],
        out_specs=[_rows(tm, D_FF), _rows(tm, D_FF)],
        out_shape=[wide, wide],
        semantics=("parallel",), args=(dh, wdt, a, b), exchange=exchange)


def _outproj_bwd(dh1, woutt, yb, gb, head_sum, exchange=()):
    s_len = dh1.shape[0]
    tm = ROW_TILE
    n_br = len(DILATIONS)

    def body(dh_ref, w_ref, yb_ref, gb_ref, e_ref, dya_ref, dgb_ref, *rest):
        do_refs, dd_refs = rest[:n_br], rest[n_br:2 * n_br]
        do_nat, dd_nat = rest[2 * n_br:]

        @pl.when(pl.program_id(0) == 0)
        def _():
            dgb_ref[...] = jnp.zeros_like(dgb_ref)

        dhb = dh_ref[...].astype(BF16)
        dya_ref[...] = lax.dot_general(dhb, w_ref[:WIDTH_A, :], _NT, preferred_element_type=F32)
        dyn = lax.dot_general(dhb, w_ref[WIDTH_A:, :], _NT, preferred_element_type=F32)
        ybv = yb_ref[...]
        dyb, dg_rows = _norm_bwd(dyn, ybv, gb_ref[...])
        dgb_ref[...] += jnp.sum(dg_rows, axis=0, keepdims=True)
        prod = dyb * ybv
        hi = prod.astype(BF16)
        lo = (prod - hi.astype(F32)).astype(BF16)
        dd_nat[0] = (jnp.dot(hi, e_ref[...], preferred_element_type=F32)
                     + jnp.dot(lo, e_ref[...], preferred_element_type=F32))
        for i, d in enumerate(DILATIONS):
            _to_sub(dd_nat, 0, dd_refs[i], 0, d, tm)
        for cb in range(N_PAIRS):
            do_nat[cb] = dyb[:, cb * 128:(cb + 1) * 128]
            for i, d in enumerate(DILATIONS):
                _to_sub(do_nat, cb, do_refs[i], cb, d, tm)

    return _call(
        body, name="outproj_bwd", grid=(s_len // tm,),
        in_specs=[_rows(tm, D_MODEL), _whole((D_MODEL, D_MODEL)), _rows(tm, WIDTH_B), _whole((1, WIDTH_B)), _whole((WIDTH_B, 128))],
        out_specs=[_rows(tm, WIDTH_A), _whole((1, WIDTH_B))] + [_sub_spec(d, N_PAIRS, tm) for d in DILATIONS]
        + [_sub_spec(d, 1, tm) for d in DILATIONS],
        out_shape=[jax.ShapeDtypeStruct((s_len, WIDTH_A), F32), jax.ShapeDtypeStruct((1, WIDTH_B), F32)]
        + [_sub_shape(s_len, d, N_PAIRS, BF16) for d in DILATIONS] + [_sub_shape(s_len, d, 1, F32) for d in DILATIONS],
        scratch_shapes=[pltpu.VMEM((N_PAIRS, tm, 128), F32), pltpu.VMEM((1, tm, 128), F32)],
        semantics=("arbitrary",), args=(dh1, woutt, yb, gb, head_sum), exchange=exchange)


def _attn_bwd(qkv, do, lse, dd, d, exchange=()):
    sd = qkv.shape[2]
    tile, nb, n_tiles = _attn_geometry(sd)
    last_block = sd // CHUNK - 1

    def nxt(n):
        return jnp.minimum((n + 1) * nb, last_block)

    def block(ref, next_ref, j):
        return ref[j * CHUNK:(j + 1) * CHUNK, :] if j < nb else next_ref[...]

    def body(q_ref, qn_ref, k_ref, v_ref, do_ref, don_ref, l_ref, ln_ref, dd_ref, ddn_ref,
             dq_ref, dk_ref, dv_ref, carry_ref):
        l_t = [block(l_ref, ln_ref, j).T for j in range(nb + 1)]
        dd_t = [block(dd_ref, ddn_ref, j).T for j in range(nb + 1)]
        for hp in range(PAIRS_PER_STEP):
            l_rows = [jnp.concatenate([t[2 * hp:2 * hp + 1, :], t[2 * hp + 1:2 * hp + 2, :]], axis=1) for t in l_t]
            dd_rows = [jnp.concatenate([t[2 * hp:2 * hp + 1, :], t[2 * hp + 1:2 * hp + 2, :]], axis=1) for t in dd_t]
            one_pair(q_ref.at[hp], qn_ref.at[hp], k_ref.at[hp], v_ref.at[hp], do_ref.at[hp], don_ref.at[hp],
                     l_rows, dd_rows, dq_ref.at[hp], dk_ref.at[hp], dv_ref.at[hp], carry_ref.at[hp])

    def one_pair(q_ref, qn_ref, k_ref, v_ref, do_ref, don_ref, l_rows, dd_rows, dq_ref, dk_ref, dv_ref, carry_ref):
        n = pl.program_id(2)

        @pl.when(n == 0)
        def _():
            carry_ref[...] = jnp.zeros_like(carry_ref)

        head_a = lax.broadcasted_iota(jnp.int32, (CHUNK, 128), 1) < HEAD_DIM
        col = lax.broadcasted_iota(jnp.int32, (CHUNK, 4 * CHUNK), 1)
        qi = col % CHUNK
        ki = lax.broadcasted_iota(jnp.int32, (CHUNK, 4 * CHUNK), 0)
        is_after = col >= 2 * CHUNK
        mask = (is_after & (ki >= qi)) | (jnp.logical_not(is_after) & (qi >= ki))
        mask_last = mask & jnp.logical_or(jnp.logical_not(is_after), n < n_tiles - 1)
        dq_acc = [carry_ref[...]] + [jnp.zeros((CHUNK, 128), F32) for _ in range(nb)]

        q_st = [jnp.concatenate(_both_heads(block(q_ref, qn_ref, j), head_a), axis=0) for j in range(nb + 1)]
        do_st = [jnp.concatenate(_both_heads(block(do_ref, don_ref, j), head_a), axis=0) for j in range(nb + 1)]

        for j in range(nb):
            rows = slice(j * CHUNK, (j + 1) * CHUNK)
            kj = k_ref[rows, :]
            vj = v_ref[rows, :]
            msk = mask if j + 1 < nb else mask_last
            qs = jnp.concatenate([q_st[j], q_st[j + 1]], axis=0)
            dos = jnp.concatenate([do_st[j], do_st[j + 1]], axis=0)
            ls = jnp.concatenate([l_rows[j], l_rows[j + 1]], axis=1)
            dds = jnp.concatenate([dd_rows[j], dd_rows[j + 1]], axis=1)
            st = lax.dot_general(kj, qs, _NT, preferred_element_type=F32)
            pt = jnp.exp(jnp.where(msk, st - ls, NEG))
            dpt = lax.dot_general(vj, dos, _NT, preferred_element_type=F32)
            dst = (pt * (dpt - dds)).astype(BF16)
            dv_ref[rows, :] = jnp.dot(pt.astype(BF16), dos, preferred_element_type=F32).astype(BF16)
            dk_ref[rows, :] = jnp.dot(dst, qs, preferred_element_type=F32).astype(BF16)
            dqs = lax.dot_general(dst, kj, _TN, preferred_element_type=F32)
            dq_acc[j] = dq_acc[j] + jnp.where(head_a, dqs[:CHUNK], dqs[CHUNK:2 * CHUNK])
            dq_acc[j + 1] = dq_acc[j + 1] + jnp.where(head_a, dqs[2 * CHUNK:3 * CHUNK], dqs[3 * CHUNK:])
        for j in range(nb):
            dq_ref[j * CHUNK:(j + 1) * CHUNK, :] = dq_acc[j].astype(BF16)
        carry_ref[...] = dq_acc[nb]

    same = lambda n: n
    grad = jax.ShapeDtypeStruct((d, N_PAIRS, sd, 128), BF16)
    return _call(
        body, name=f"attn_bwd_d{d}", grid=(d, N_PAIRS // PAIRS_PER_STEP, n_tiles),
        in_specs=[_attn_spec(0, tile, same), _attn_spec(0, CHUNK, nxt), _attn_spec(N_PAIRS, tile, same),
                  _attn_spec(2 * N_PAIRS, tile, same), _attn_spec(0, tile, same), _attn_spec(0, CHUNK, nxt),
                  _stats_spec(tile, same), _stats_spec(CHUNK, nxt), _stats_spec(tile, same), _stats_spec(CHUNK, nxt)],
        out_specs=[_attn_spec(0, tile, same)] * 3,
        out_shape=[grad, grad, grad],
        scratch_shapes=[pltpu.VMEM((PAIRS_PER_STEP, CHUNK, 128), F32)],
        semantics=("parallel", "parallel", "arbitrary"), args=(qkv, qkv, qkv, qkv, do, do, lse, lse, dd, dd), exchange=exchange)


def _sgu_bwd(ua, sw, b2, gs, ga, dya_n):
    s_len = ua.shape[0]
    tm = ROW_TILE

    def body(ua_ref, sw_ref, b2_ref, gs_ref, ga_ref, dy_ref, dua_ref, dsw_ref, db2_ref, dgs_ref, dga_ref):
        @pl.when(pl.program_id(0) == 0)
        def _():
            dsw_ref[...] = jnp.zeros_like(dsw_ref)
            db2_ref[...] = jnp.zeros_like(db2_ref)
            dgs_ref[...] = jnp.zeros_like(dgs_ref)
            dga_ref[...] = jnp.zeros_like(dga_ref)

        u, va, ug, xhat, rstd, vn = _sgu_core(ua_ref, gs_ref)
        wm, keep = _sgu_mix_weights(sw_ref)
        head = lax.broadcasted_iota(jnp.int32, (CHUNK, WIDTH_A), 1) // HEAD_DIM
        gav = ga_ref[...]
        gsv = gs_ref[...]
        dga = jnp.zeros((1, WIDTH_A), F32)
        dgs = jnp.zeros((1, WIDTH_A), F32)
        db2 = jnp.zeros((CHUNK, WIDTH_A), F32)
        dsw = [jnp.zeros((CHUNK, CHUNK), F32) for _ in range(4)]
        for c in range(tm // CHUNK):
            rows = slice(c * CHUNK, (c + 1) * CHUNK)
            vnc = vn[rows]
            vnb = vnc.astype(BF16)
            mixed = b2_ref[...]
            for h in range(4):
                mixed = mixed + jnp.dot(wm[h], jnp.where(head == h, vnc, 0.0).astype(BF16), preferred_element_type=F32)
            ugc = ug[rows]
            dya, dga_rows = _norm_bwd(dy_ref[rows, :], ugc * mixed, gav)
            dga = dga + jnp.sum(dga_rows, axis=0, keepdims=True)
            dmixed = dya * ugc
            db2 = db2 + dmixed
            dvn = jnp.zeros((CHUNK, WIDTH_A), F32)
            for h in range(4):
                dmh = jnp.where(head == h, dmixed, 0.0).astype(BF16)
                dsw[h] = dsw[h] + lax.dot_general(dmh, vnb, _NT, preferred_element_type=F32)
                dvn = dvn + lax.dot_general(wm[h], dmh, _TN, preferred_element_type=F32)
            xh = xhat[rows]
            dgs = dgs + jnp.sum(dvn * xh, axis=0, keepdims=True)
            dxh = dvn * gsv
            dvg = rstd[rows] * (dxh - jnp.mean(dxh, axis=-1, keepdims=True) - xh * jnp.mean(dxh * xh, axis=-1, keepdims=True))
            dua_ref[rows, :WIDTH_A] = (dya * mixed * _gelu_grad(u[rows])).astype(BF16)
            dua_ref[rows, WIDTH_A:] = (dvg * _gelu_grad(va[rows])).astype(BF16)
        for h in range(4):
            dsw_ref[h] += jnp.where(keep, dsw[h], 0.0)
        db2_ref[...] += db2
        dgs_ref[...] += dgs
        dga_ref[...] += dga

    return pl.pallas_call(
        body, name="sgu_bwd", grid=(s_len // tm,),
        in_specs=[_rows(tm, 2 * WIDTH_A), _whole((4, CHUNK, CHUNK)), _whole((CHUNK, WIDTH_A)), _whole((1, WIDTH_A)),
                  _whole((1, WIDTH_A)), _rows(tm, WIDTH_A)],
        out_specs=[_rows(tm, 2 * WIDTH_A), _whole((4, CHUNK, CHUNK)), _whole((CHUNK, WIDTH_A)), _whole((1, WIDTH_A)), _whole((1, WIDTH_A))],
        out_shape=[jax.ShapeDtypeStruct((s_len, 2 * WIDTH_A), BF16), jax.ShapeDtypeStruct((4, CHUNK, CHUNK), F32),
                   jax.ShapeDtypeStruct((CHUNK, WIDTH_A), F32), jax.ShapeDtypeStruct((1, WIDTH_A), F32),
                   jax.ShapeDtypeStruct((1, WIDTH_A), F32)],
        compiler_params=_params("arbitrary"),
    )(ua, sw, b2, gs, ga, dya_n)


def _dproj(dua, dqs, dks, dvs, cos, sin):
    s_len = dua.shape[0]
    tm = ROW_TILE
    n_br = len(DILATIONS)

    def body(dua_ref, *rest):
        groups = [rest[g * n_br:(g + 1) * n_br] for g in range(3)]
        cos_ref, sin_ref, out_ref, acc = rest[3 * n_br:]
        out_ref[:, :2 * WIDTH_A] = dua_ref[...]
        c = cos_ref[...]
        s = sin_ref[...]
        first_half = (lax.broadcasted_iota(jnp.int32, (tm, 128), 1) % HEAD_DIM) < HEAD_DIM // 2
        for g, refs in enumerate(groups):
            for cb in range(N_PAIRS):
                for i, d in enumerate(DILATIONS):
                    _from_sub(refs[i], cb, acc, 0, d, tm, accumulate=i > 0)
                t = acc[0]
                if g < 2:
                    t = (t * c - _swap_halves(t, first_half) * s) * (0.125 if g == 0 else 1.0)
                col = 2 * WIDTH_A + g * WIDTH_B + cb * 128
                out_ref[:, col:col + 128] = t.astype(BF16)

    subs = [_sub_spec(d, N_PAIRS, tm) for d in DILATIONS]
    return pl.pallas_call(
        body, name="dproj", grid=(s_len // tm,),
        in_specs=[_rows(tm, 2 * WIDTH_A)] + subs * 3 + [_rows(tm, 128), _rows(tm, 128)],
        out_specs=_rows(tm, IN_COLS),
        out_shape=jax.ShapeDtypeStruct((s_len, IN_COLS), BF16),
        scratch_shapes=[pltpu.VMEM((1, tm, 128), F32)],
        compiler_params=_params("parallel"),
    )(dua, *dqs, *dks, *dvs, cos, sin)


def _mm_tn(a, b, name, exchange=()):
    s_len, m = a.shape
    n = b.shape[1]
    tk = ROW_TILE
    tm = m if m <= 512 else (1408 if m == D_FF else 512)
    n_k = s_len // tk

    def body(a_ref, b_ref, o_ref, acc_ref):
        k = pl.program_id(1)

        @pl.when(k == 0)
        def _():
            acc_ref[...] = jnp.zeros_like(acc_ref)

        acc_ref[...] += lax.dot_general(a_ref[...].astype(BF16), b_ref[...].astype(BF16), _TN, preferred_element_type=F32)

        @pl.when(k == n_k - 1)
        def _():
            o_ref[...] = acc_ref[...].astype(BF16)

    (grad,), received = _call(
        body, name=name, grid=(m // tm, n_k),
        in_specs=[pl.BlockSpec((tk, tm), lambda i, k: (k, i)), pl.BlockSpec((tk, n), lambda i, k: (k, 0))],
        out_specs=[pl.BlockSpec((tm, n), lambda i, k: (i, 0))],
        out_shape=[jax.ShapeDtypeStruct((m, n), BF16)],
        scratch_shapes=[pltpu.VMEM((tm, n), F32)],
        semantics=("parallel", "arbitrary"), args=(a, b), exchange=exchange)
    return grad, received


def _position():
    x, y, c = lax.axis_index("x"), lax.axis_index("y"), lax.axis_index("c")
    return x, y, c, 4 * x + 2 * y + c


def _peer(x, y, c, rel):
    return (x ^ ((rel >> 2) & 1), y ^ ((rel >> 1) & 1), c ^ (rel & 1))


def _exchange_out_shape(kind, arr):
    return jax.ShapeDtypeStruct(((N_DEV,) + arr.shape) if kind == "gather" else arr.shape, arr.dtype)


def _exchange_sems(n_items):
    return [pltpu.SemaphoreType.DMA((n_items, N_DEV)), pltpu.SemaphoreType.DMA((n_items, N_DEV)), pltpu.SemaphoreType.DMA((n_items,))]


def _exchange_copies(kinds, srcs, dsts, sems):
    send_sems, recv_sems, local_sems = sems
    x, y, c, me = _position()
    local, sends, recvs = [], [], []
    for k, (kind, src, dst) in enumerate(zip(kinds, srcs, dsts)):
        own = src if kind == "gather" else src.at[me]
        local.append(pltpu.make_async_copy(own, dst.at[me], local_sems.at[k]))
        for rel in range(1, N_DEV):
            going = src if kind == "gather" else src.at[me ^ rel]
            common = dict(send_sem=send_sems.at[k, rel], recv_sem=recv_sems.at[k, rel],
                          device_id=_peer(x, y, c, rel), device_id_type=MESH)
            sends.append(pltpu.make_async_remote_copy(src_ref=going, dst_ref=dst.at[me], **common))
            recvs.append(pltpu.make_async_remote_copy(src_ref=own, dst_ref=dst.at[me ^ rel], **common))
    return local, sends, recvs


def _exchange_start(kinds, srcs, dsts, sems):
    local, sends, _ = _exchange_copies(kinds, srcs, dsts, sems)
    for cp in local + sends:
        cp.start()


def _exchange_finish(kinds, srcs, dsts, sems):
    local, sends, recvs = _exchange_copies(kinds, srcs, dsts, sems)
    for cp in recvs:
        cp.wait_recv()
    for cp in sends:
        cp.wait_send()
    for cp in local:
        cp.wait()


def _exchange_only(items, name):
    kinds = [k for k, _ in items]
    n = len(items)

    def body(*refs):
        srcs, dsts, sems = refs[:n], refs[n:2 * n], refs[2 * n:]
        _exchange_start(kinds, srcs, dsts, sems)
        _exchange_finish(kinds, srcs, dsts, sems)

    any_spec = pl.BlockSpec(memory_space=pl.ANY)
    return pl.pallas_call(
        body, name=name, in_specs=[any_spec] * n, out_specs=[any_spec] * n,
        out_shape=[_exchange_out_shape(k, a) for k, a in items],
        scratch_shapes=_exchange_sems(n),
        compiler_params=pltpu.CompilerParams(has_side_effects=True),
    )(*[a for _, a in items])


def _gather_two_level_with_rope_tables(shard, inv_freq, s_len, name):
    rows = ROW_TILE

    def body(inv_ref, src, cos_ref, sin_ref, dst, send_sems, recv_sems, local_sem):
        x, y, c, me = _position()
        sibling = (x, y, 1 - c)
        chips = [(1 - x, y), (x, 1 - y), (1 - x, 1 - y)]

        def block(px, py, pc):
            return dst.at[4 * px + 2 * py + pc]

        def copy(k, blk, to, src_ref=None):
            return pltpu.make_async_remote_copy(
                src_ref=block(*blk) if src_ref is None else src_ref, dst_ref=block(*blk),
                send_sem=send_sems.at[k], recv_sem=recv_sems.at[k], device_id=to, device_id_type=MESH)

        mine = pltpu.make_async_copy(src, dst.at[me], local_sem)
        mine.start()
        first = [copy(0, (x, y, c), sibling, src)] + [copy(1 + j, (x, y, c), (*chip, c), src) for j, chip in enumerate(chips)]
        for cp in first:
            cp.start()

        inv = inv_ref[...]
        lane = lax.broadcasted_iota(jnp.int32, (rows, 128), 1)
        sign = jnp.where((lane // (HEAD_DIM // 2)) % 2 == 0, -1.0, 1.0)
        row = lax.broadcasted_iota(jnp.int32, (rows, 128), 0)

        @pl.loop(0, s_len // rows)
        def _(i):
            at = pl.multiple_of(i * rows, rows)
            ang = (row + at).astype(F32) * inv
            cos_ref[pl.ds(at, rows), :] = jnp.cos(ang)
            sin_ref[pl.ds(at, rows), :] = jnp.sin(ang) * sign

        passed = [copy(4 + j, (*chip, c), sibling) for j, chip in enumerate(chips)]
        for j, chip in enumerate(chips):
            copy(1 + j, (*chip, c), (x, y, c)).wait_recv()
            passed[j].start()
        copy(0, (x, y, 1 - c), (x, y, c)).wait_recv()
        for j, chip in enumerate(chips):
            copy(4 + j, (*chip, 1 - c), (x, y, c)).wait_recv()
        for cp in first + passed:
            cp.wait_send()
        mine.wait()

    any_spec = pl.BlockSpec(memory_space=pl.ANY)
    vmem = pl.BlockSpec(memory_space=pltpu.VMEM)
    table = jax.ShapeDtypeStruct((s_len, 128), F32)
    return pl.pallas_call(
        body, name=name, in_specs=[vmem, any_spec], out_specs=[vmem, vmem, any_spec],
        out_shape=[table, table, _exchange_out_shape("gather", shard)],
        scratch_shapes=[pltpu.SemaphoreType.DMA((N_DEV - 1,)), pltpu.SemaphoreType.DMA((N_DEV - 1,)), pltpu.SemaphoreType.DMA],
        compiler_params=pltpu.CompilerParams(has_side_effects=True, vmem_limit_bytes=V7X_VMEM_LIMIT_BYTES),
    )(inv_freq, shard)


def _call(body, *, name, grid, in_specs, out_specs, out_shape, args, scratch_shapes=(), semantics, exchange=()):
    if not exchange:
        outs = pl.pallas_call(body, name=name, grid=grid, in_specs=in_specs, out_specs=out_specs, out_shape=out_shape,
                              scratch_shapes=list(scratch_shapes), compiler_params=_params(*semantics))(*args)
        return outs, []
    kinds = [k for k, _ in exchange]
    n_in, n_out, n_x, n_scr = len(in_specs), len(out_specs), len(exchange), len(scratch_shapes)

    def wrapped(*refs):
        ins, refs = refs[:n_in], refs[n_in:]
        srcs, refs = refs[:n_x], refs[n_x:]
        outs, refs = refs[:n_out], refs[n_out:]
        dsts, refs = refs[:n_x], refs[n_x:]
        scratch, sems = refs[:n_scr], refs[n_scr:]
        ids = [pl.program_id(a) for a in range(len(grid))]
        first = functools.reduce(jnp.logical_and, [i == 0 for i in ids])
        last = functools.reduce(jnp.logical_and, [i == g - 1 for i, g in zip(ids, grid)])

        @pl.when(first)
        def _():
            _exchange_start(kinds, srcs, dsts, sems)

        body(*ins, *outs, *scratch)

        @pl.when(last)
        def _():
            _exchange_finish(kinds, srcs, dsts, sems)

    any_spec = pl.BlockSpec(memory_space=pl.ANY)
    outs = pl.pallas_call(
        wrapped, name=name, grid=grid,
        in_specs=list(in_specs) + [any_spec] * n_x, out_specs=list(out_specs) + [any_spec] * n_x,
        out_shape=list(out_shape) + [_exchange_out_shape(k, a) for k, a in exchange],
        scratch_shapes=list(scratch_shapes) + _exchange_sems(n_x),
        compiler_params=pltpu.CompilerParams(dimension_semantics=("arbitrary",) * len(grid),
                                             vmem_limit_bytes=V7X_VMEM_LIMIT_BYTES, has_side_effects=True),
    )(*args, *[a for _, a in exchange])
    return outs[:n_out], outs[n_out:]


def _adamw_math(w, g, m, v):
    m = ADAM_B1 * m + (1.0 - ADAM_B1) * g
    v = ADAM_B2 * v + (1.0 - ADAM_B2) * (g * g)
    m_hat = m / (1.0 - ADAM_B1 ** ADAM_STEP)
    v_hat = v / (1.0 - ADAM_B2 ** ADAM_STEP)
    return -ADAM_LR * (m_hat / (jnp.sqrt(v_hat) + ADAM_EPS) + ADAM_WD * w), m, v


def _adamw(parts, w, m, v, name):
    rows, cols = w.shape
    tm = 256 if rows % 256 == 0 and rows > 256 else rows

    def body(p_ref, w_ref, m_ref, v_ref, g_ref, d_ref, nm_ref, nv_ref):
        g = p_ref[0].astype(F32)
        for j in range(1, N_DEV):
            g = g + p_ref[j].astype(F32)
        delta, nm, nv = _adamw_math(w_ref[...], g, m_ref[...], v_ref[...])
        g_ref[...] = g
        d_ref[...] = delta
        nm_ref[...] = nm
        nv_ref[...] = nv

    shard = jax.ShapeDtypeStruct((rows, cols), F32)
    return pl.pallas_call(
        body, name=name, grid=(rows // tm,),
        in_specs=[pl.BlockSpec((N_DEV, tm, cols), lambda i: (0, i, 0))] + [_rows(tm, cols)] * 3,
        out_specs=[_rows(tm, cols)] * 4,
        out_shape=[shard] * 4,
        compiler_params=_params("parallel"),
    )(parts, w, m, v)


_SMALL = ("mix_norm_g", "sgu_w", "sgu_b", "sgu_norm_g", "out_norm_a", "out_norm_b", "ffn_norm_g", "ple_norm_g", "final_norm_g")
_BIG = ("w_in", "w_out", "w_gate", "w_up", "w_down", "w_ple_gate", "w_ple_proj")
_COLUMN_SHARDED = ("w_in", "w_gate", "w_up", "w_ple_proj")
_ORDER = ("mix_norm_g", "w_in", "sgu_w", "sgu_b", "sgu_norm_g", "out_norm_a", "out_norm_b", "w_out", "ffn_norm_g",
          "w_gate", "w_up", "w_down", "ple_norm_g", "w_ple_gate", "w_ple_proj", "final_norm_g")


def _pack_small(values, names=_SMALL):
    flat = jnp.concatenate([values[n].reshape(-1).astype(F32) for n in names])
    pad = (-flat.shape[0]) % (8 * 128)
    return jnp.pad(flat, (0, pad)).reshape(-1, 128)


def _unpack_small(packed, like):
    flat = packed.reshape(-1)
    out, at = {}, 0
    for n in _SMALL:
        size = like[n].size
        out[n] = flat[at:at + size].reshape(like[n].shape)
        at += size
    return out


def _own_orientation(name, value):
    return value[0].T if name in _COLUMN_SHARDED else value[0]


def _reference_orientation(name, value):
    return (value.T if name in _COLUMN_SHARDED else value)[None]


def _full_from_gathered(gathered):
    return gathered.reshape(N_DEV * gathered.shape[1], gathered.shape[2])


def _sliced_for_devices(grad):
    return grad.reshape(N_DEV, grad.shape[0] // N_DEV, grad.shape[1])


def _rope_inv_freq():
    half = HEAD_DIM // 2
    inv = ROPE_THETA ** (-jnp.arange(half, dtype=F32) / half)
    return jnp.tile(inv, 128 // half)[None, :]


def _forward_backward(x, p, target, small, shards):
    def gather(*names):
        return [("gather", shards[n]) for n in names]

    def scatter(**grads):
        return [("scatter", _sliced_for_devices(g)) for g in grads.values()]

    full, parts = {}, {}
    s_len = x.shape[0]
    cos, sin, got = _gather_two_level_with_rope_tables(shards["w_in"], _rope_inv_freq(), s_len, "gather_w_in")
    full["w_in"] = _full_from_gathered(got)

    g_mix, g_ffn, g_ple = small["mix_norm_g"], small["ffn_norm_g"], small["ple_norm_g"]
    g_fin = small["final_norm_g"].reshape(1, D_MODEL)
    sw, gs, ga, gb = small["sgu_w"], small["sgu_norm_g"], small["out_norm_a"], small["out_norm_b"]
    b2 = jnp.repeat(small["sgu_b"].T, HEAD_DIM, axis=1)
    head_sum = (jnp.arange(WIDTH_B)[:, None] // HEAD_DIM == jnp.arange(128)[None, :]).astype(BF16)
    n_br = len(DILATIONS)

    def arrived(names, got):
        for n, g in zip(names, got):
            full[n] = _full_from_gathered(g)

    (ua, hn1, *qkv), got = _inproj(x, g_mix, full["w_in"], cos, sin, exchange=gather("w_gate"))
    arrived(("w_gate",), got)
    ya_n = _sgu_fwd(ua, sw, b2, gs, ga)
    half = shards["w_up"].shape[0] // 2
    riders = [[("gather", shards["w_up"][:half])], [("gather", shards["w_up"][half:])], gather("w_out")]
    branch, got = [], []
    for i, d in enumerate(DILATIONS):
        o_l, g = _attn_fwd(qkv[i], d, exchange=riders[i])
        branch.append(o_l)
        got += g
    arrived(("w_up", "w_out"), [jnp.concatenate(got[:2], axis=1), got[2]])
    (y, yb, *lse), _ = _combine([o for o, _ in branch], [l for _, l in branch], ya_n, gb)
    last_wave = ("w_down", "w_ple_gate", "w_ple_proj")
    (h1, a, b, act, hn2), got = _ffn_up(y, full["w_out"], x, g_ffn, full["w_gate"], full["w_up"], exchange=gather(*last_wave))
    arrived(last_wave, got)
    h2, h3, gate, pp, hn3 = _ffn_down_ple(act, full["w_down"], h1, g_ple, full["w_ple_gate"], p, full["w_ple_proj"])

    dh2, loss, d_fin, d_ple, g_ple_gate, g_ple_proj = _loss_ple_bwd(
        h3, target, g_fin, gate, pp, h2, g_ple, full["w_ple_gate"], hn3, p)
    g_down, (parts["w_ple_gate"], parts["w_ple_proj"]) = _mm_tn(
        act, dh2, "dw_down", exchange=scatter(w_ple_gate=g_ple_gate, w_ple_proj=g_ple_proj))
    (da, db), (parts["w_down"],) = _ffn_down_bwd(dh2, full["w_down"], a, b, exchange=scatter(w_down=g_down))
    g_gate, _ = _mm_tn(da, hn2, "dw_gate")
    g_up, _ = _mm_tn(db, hn2, "dw_up")
    (dh1, d_ffn, g_out), (parts["w_gate"],) = _mm_norm_bwd(
        [(da, full["w_gate"]), (db, full["w_up"])], h1, g_ffn, dh2, "ffn_up_bwd", exchange=scatter(w_gate=g_gate), dw_lhs=y)
    (dya_n, d_gb, *do_dd), (parts["w_out"],) = _outproj_bwd(dh1, full["w_out"], yb, gb, head_sum, exchange=scatter(w_out=g_out))
    grads_b = []
    for i, d in enumerate(DILATIONS):
        g3, got = _attn_bwd(qkv[i], do_dd[i], lse[i], do_dd[n_br + i], d, exchange=scatter(w_up=g_up) if i == 0 else ())
        grads_b.append(g3)
        if i == 0:
            (parts["w_up"],) = got
    dua, d_sw, d_b2, d_gs, d_ga = _sgu_bwd(ua, sw, b2, gs, ga, dya_n)
    dproj = _dproj(dua, [g[0] for g in grads_b], [g[1] for g in grads_b], [g[2] for g in grads_b], cos, sin)
    early = {
        "sgu_w": d_sw, "sgu_b": d_b2.reshape(CHUNK, 4, HEAD_DIM).sum(axis=-1).T, "sgu_norm_g": d_gs, "out_norm_a": d_ga,
        "out_norm_b": d_gb, "ffn_norm_g": d_ffn, "ple_norm_g": d_ple, "final_norm_g": d_fin,
    }
    g_in, (early_parts,) = _mm_tn(dproj, hn1, "dw_in", exchange=[("gather", _pack_small(early, _SMALL[1:]))])
    (dx, d_mix), (parts["w_in"],) = _mm_norm_bwd(
        [(dproj, full["w_in"])], x, g_mix, dh1, "inproj_bwd", exchange=scatter(w_in=g_in))
    late = jnp.concatenate([_pack_small({"mix_norm_g": d_mix}, _SMALL[:1]), jnp.broadcast_to(loss, (8, 128))])
    (late_parts,) = _exchange_only([("gather", late)], "gather_mix_norm_grad_and_loss")
    total_loss = jnp.sum(late_parts[:, 8, 0])
    return total_loss, dx, parts, jnp.concatenate([late_parts[:, :8], early_parts], axis=1)


def kernel(x, p, mix_norm_g, w_in, sgu_w, sgu_b, sgu_norm_g, out_norm_a, out_norm_b, w_out, ffn_norm_g, w_gate, w_up, w_down, ple_norm_g, w_ple_gate, w_ple_proj, final_norm_g, loss_target, m_mix_norm_g, m_w_in, m_sgu_w, m_sgu_b, m_sgu_norm_g, m_out_norm_a, m_out_norm_b, m_w_out, m_ffn_norm_g, m_w_gate, m_w_up, m_w_down, m_ple_norm_g, m_w_ple_gate, m_w_ple_proj, m_final_norm_g, v_mix_norm_g, v_w_in, v_sgu_w, v_sgu_b, v_sgu_norm_g, v_out_norm_a, v_out_norm_b, v_w_out, v_ffn_norm_g, v_w_gate, v_w_up, v_w_down, v_ple_norm_g, v_w_ple_gate, v_w_ple_proj, v_final_norm_g):
    given = dict(locals())
    weights = {n: given[n] for n in _ORDER}
    moments_m = {n: given["m_" + n] for n in _ORDER}
    moments_v = {n: given["v_" + n] for n in _ORDER}

    shards = {n: _own_orientation(n, weights[n]).astype(BF16) for n in _BIG}
    small = {n: (weights[n][0] if n in ("sgu_w", "sgu_b") else weights[n]) for n in _SMALL}

    loss, dx, parts, small_parts = _forward_backward(x[0], p[0, 0], loss_target[0], small, shards)

    small_like = {n: weights[n] for n in _SMALL}
    grads, deltas, new_m, new_v = {}, {}, {}, {}
    for n in _BIG:
        outs = _adamw(parts[n], _own_orientation(n, weights[n]), _own_orientation(n, moments_m[n]),
                      _own_orientation(n, moments_v[n]), "adamw_" + n)
        grads[n], deltas[n], new_m[n], new_v[n] = [_reference_orientation(n, o) for o in outs]
    g, d, nm, nv = _adamw(small_parts, _pack_small(small_like), _pack_small({n: moments_m[n] for n in _SMALL}),
                          _pack_small({n: moments_v[n] for n in _SMALL}), "adamw_small")
    for out, packed in ((grads, g), (deltas, d), (new_m, nm), (new_v, nv)):
        out.update(_unpack_small(packed, small_like))

    return (loss, dx[None], *[grads[n] for n in _ORDER], *[deltas[n] for n in _ORDER],
            *[new_m[n] for n in _ORDER], *[new_v[n] for n in _ORDER])
```

```python
import functools

import jax
import jax.numpy as jnp
from jax import lax
from jax.experimental import pallas as pl
from jax.experimental.pallas import tpu as pltpu

F32 = jnp.float32
BF16 = jnp.bfloat16

D_MODEL = 1024
WIDTH_A = 256
WIDTH_B = 768
D_FF = 2816
IN_COLS = 2 * WIDTH_A + 3 * WIDTH_B
PLE_DIM = 256
HEAD_DIM = 64
N_PAIRS = WIDTH_B // 128
CHUNK = 128
N_BACK = 128
DILATIONS = (1, 4, 16)
ROPE_THETA = 10000.0
EPS = 1e-6
N_DEV = 8

ADAM_LR = 0.001
ADAM_B1 = 0.9
ADAM_B2 = 0.999
ADAM_EPS = 1e-08
ADAM_WD = 0.01
ADAM_STEP = 10

V7X_VMEM_LIMIT_BYTES = 56 * 1024 * 1024
ROW_TILE = 512
MESH = pl.DeviceIdType.MESH
NEG = -1e30

_NT = (((1,), (1,)), ((), ()))
_TN = (((0,), (0,)), ((), ()))


def _params(*semantics):
    return pltpu.CompilerParams(dimension_semantics=semantics, vmem_limit_bytes=V7X_VMEM_LIMIT_BYTES)


def _rows(tm, width):
    return pl.BlockSpec((tm, width), lambda i: (i, 0))


def _whole(shape):
    return pl.BlockSpec(shape, lambda *_: (0,) * len(shape))


def _gelu(x):
    t = jnp.tanh(0.7978845608028654 * (x + 0.044715 * (x * x * x)))
    return 0.5 * x * (1.0 + t)


def _gelu_grad(x):
    t = jnp.tanh(0.7978845608028654 * (x + 0.044715 * (x * x * x)))
    return 0.5 * (1.0 + t) + 0.5 * x * (1.0 - t * t) * (0.7978845608028654 * (1.0 + 3.0 * 0.044715 * (x * x)))


def _rstd(x):
    return lax.rsqrt(jnp.mean(x * x, axis=-1, keepdims=True) + EPS)


def _norm_bwd(dn, h, g):
    r = _rstd(h)
    n = h * r
    t = dn * g
    return r * (t - n * jnp.mean(t * n, axis=-1, keepdims=True)), dn * n


def _swap_halves(x, first_half):
    return jnp.where(first_half, pltpu.roll(x, 96, 1), pltpu.roll(x, 32, 1))


def _sub_spec(d, n_cb, tm):
    return pl.BlockSpec((d, n_cb, tm // d, 128), lambda i: (0, 0, i, 0))


def _sub_shape(s_len, d, n_cb, dtype):
    return jax.ShapeDtypeStruct((d, n_cb, s_len // d, 128), dtype)


def _to_sub(stage_ref, cb_src, out_ref, cb_dst, d, tm):
    slab = stage_ref.at[cb_src]
    for r in range(d):
        out_ref[r, cb_dst] = slab[pl.ds(r, tm // d, stride=d), :].astype(out_ref.dtype)


def _from_sub(in_ref, cb_src, stage_ref, cb_dst, d, tm, accumulate=False):
    slab = stage_ref.at[cb_dst]
    for r in range(d):
        rows = pl.ds(r, tm // d, stride=d)
        val = in_ref[r, cb_src].astype(F32)
        slab[rows, :] = slab[rows, :] + val if accumulate else val


def _inproj(x, g, w, cos, sin, exchange=()):
    s_len = x.shape[0]
    tm = ROW_TILE
    n_cb = 3 * N_PAIRS

    def body(x_ref, g_ref, w_ref, cos_ref, sin_ref, ua_ref, hn_ref, *rest):
        sub_refs, stage = rest[:-1], rest[-1]
        xf = x_ref[...]
        hn = (xf * _rstd(xf) * g_ref[...]).astype(BF16)
        hn_ref[...] = hn
        c = cos_ref[...]
        s = sin_ref[...]
        first_half = (lax.broadcasted_iota(jnp.int32, (tm, 128), 1) % HEAD_DIM) < HEAD_DIM // 2
        for j in range(IN_COLS // 256):
            col = j * 256
            acc = lax.dot_general(hn, w_ref[col:col + 256, :], _NT, preferred_element_type=F32)
            if col < 2 * WIDTH_A:
                ua_ref[:, col:col + 256] = acc
                continue
            for half in range(2):
                cb = (col - 2 * WIDTH_A) // 128 + half
                t = acc[:, half * 128:(half + 1) * 128]
                if cb < 2 * N_PAIRS:
                    t = (t * c + _swap_halves(t, first_half) * s) * (0.125 if cb < N_PAIRS else 1.0)
                stage[cb] = t
                sub_refs[0][0, cb] = t.astype(BF16)
        for cb in range(n_cb):
            for d, out_ref in zip(DILATIONS[1:], sub_refs[1:]):
                _to_sub(stage, cb, out_ref, cb, d, tm)

    return _call(
        body, name="inproj", grid=(s_len // tm,),
        in_specs=[_rows(tm, D_MODEL), _whole((1, D_MODEL)), _whole((IN_COLS, D_MODEL)), _rows(tm, 128), _rows(tm, 128)],
        out_specs=[_rows(tm, 2 * WIDTH_A), _rows(tm, D_MODEL)] + [_sub_spec(d, n_cb, tm) for d in DILATIONS],
        out_shape=[jax.ShapeDtypeStruct((s_len, 2 * WIDTH_A), F32), jax.ShapeDtypeStruct((s_len, D_MODEL), BF16)]
        + [_sub_shape(s_len, d, n_cb, BF16) for d in DILATIONS],
        scratch_shapes=[pltpu.VMEM((n_cb, tm, 128), F32)],
        semantics=("parallel",), args=(x, g, w, cos, sin), exchange=exchange)


def _sgu_mix_weights(sw_ref):
    keep = lax.broadcasted_iota(jnp.int32, (CHUNK, CHUNK), 0) >= lax.broadcasted_iota(jnp.int32, (CHUNK, CHUNK), 1)
    return [jnp.where(keep, sw_ref[h], 0.0).astype(BF16) for h in range(4)], keep


def _sgu_core(ua_ref, gs_ref):
    u = ua_ref[:, :WIDTH_A]
    va = ua_ref[:, WIDTH_A:]
    vg = _gelu(va)
    xc = vg - jnp.mean(vg, axis=-1, keepdims=True)
    rstd = lax.rsqrt(jnp.mean(xc * xc, axis=-1, keepdims=True) + EPS)
    xhat = xc * rstd
    return u, va, _gelu(u), xhat, rstd, xhat * gs_ref[...]


def _sgu_fwd(ua, sw, b2, gs, ga):
    s_len = ua.shape[0]
    tm = ROW_TILE

    def body(ua_ref, sw_ref, b2_ref, gs_ref, ga_ref, out_ref):
        _, _, ug, _, _, vn = _sgu_core(ua_ref, gs_ref)
        wm, _ = _sgu_mix_weights(sw_ref)
        head = lax.broadcasted_iota(jnp.int32, (CHUNK, WIDTH_A), 1) // HEAD_DIM
        for c in range(tm // CHUNK):
            rows = slice(c * CHUNK, (c + 1) * CHUNK)
            vnc = vn[rows]
            mixed = b2_ref[...]
            for h in range(4):
                mixed = mixed + jnp.dot(wm[h], jnp.where(head == h, vnc, 0.0).astype(BF16), preferred_element_type=F32)
            ya = ug[rows] * mixed
            out_ref[rows, :] = (ya * _rstd(ya) * ga_ref[...]).astype(BF16)

    return pl.pallas_call(
        body, name="sgu_fwd", grid=(s_len // tm,),
        in_specs=[_rows(tm, 2 * WIDTH_A), _whole((4, CHUNK, CHUNK)), _whole((CHUNK, WIDTH_A)), _whole((1, WIDTH_A)), _whole((1, WIDTH_A))],
        out_specs=_rows(tm, WIDTH_A),
        out_shape=jax.ShapeDtypeStruct((s_len, WIDTH_A), BF16),
        compiler_params=_params("parallel"),
    )(ua, sw, b2, gs, ga)


def _attn_geometry(sd):
    tile = min(ROW_TILE, sd)
    return tile, tile // CHUNK, sd // tile


PAIRS_PER_STEP = 6


def _attn_spec(cb0, rows, row_index):
    return pl.BlockSpec((None, PAIRS_PER_STEP, rows, 128), lambda r, g, n: (r, cb0 // PAIRS_PER_STEP + g, row_index(n), 0))


assert PAIRS_PER_STEP == N_PAIRS
assert DILATIONS[0] == 1


def _stats_spec(rows, row_index):
    return pl.BlockSpec((None, None, rows, 128), lambda r, g, n: (r, 0, row_index(n), 0))


def _stats_shape(sd, d):
    return jax.ShapeDtypeStruct((d, 1, sd, 128), F32)


def _both_heads(x, head_a):
    zero = jnp.zeros_like(x)
    return [jnp.where(head_a, x, zero), jnp.where(head_a, zero, x)]


def _attn_fwd(qkv, d, exchange=()):
    sd = qkv.shape[2]
    tile, nb, n_tiles = _attn_geometry(sd)

    def prev(n):
        return jnp.maximum(n * nb - 1, 0)

    def body(q_ref, k_ref, kp_ref, v_ref, vp_ref, o_ref, l_ref):
        for hp in range(PAIRS_PER_STEP):
            one_pair(hp, q_ref.at[hp], k_ref.at[hp], kp_ref.at[hp], v_ref.at[hp], vp_ref.at[hp], o_ref.at[hp], l_ref)

    def one_pair(hp, q_ref, k_ref, kp_ref, v_ref, vp_ref, o_ref, l_ref):
        n = pl.program_id(2)
        lane = lax.broadcasted_iota(jnp.int32, (CHUNK, 128), 1)
        head_a = lane < HEAD_DIM
        qi = lax.broadcasted_iota(jnp.int32, (2 * CHUNK, 2 * CHUNK), 0) % CHUNK
        kc = lax.broadcasted_iota(jnp.int32, (2 * CHUNK, 2 * CHUNK), 1)
        band = (kc >= qi) & (kc <= qi + N_BACK)
        for j in range(nb):
            rows = slice(j * CHUNK, (j + 1) * CHUNK)
            if j == 0:
                kcat = jnp.concatenate([kp_ref[...], k_ref[rows, :]], axis=0)
                vcat = jnp.concatenate([vp_ref[...], v_ref[rows, :]], axis=0)
                valid = band & jnp.logical_or(n > 0, kc >= CHUNK)
            else:
                kcat = k_ref[(j - 1) * CHUNK:(j + 1) * CHUNK, :]
                vcat = v_ref[(j - 1) * CHUNK:(j + 1) * CHUNK, :]
                valid = band
            q2 = jnp.concatenate(_both_heads(q_ref[rows, :], head_a), axis=0)
            s = lax.dot_general(q2, kcat, _NT, preferred_element_type=F32)
            s = jnp.where(valid, s, NEG)
            m = jnp.max(s, axis=-1, keepdims=True)
            p = jnp.exp(s - m)
            l = jnp.sum(p, axis=-1, keepdims=True)
            o2 = jnp.dot(p.astype(BF16), vcat, preferred_element_type=F32) / l
            lse2 = m + jnp.log(l)
            o_ref[rows, :] = jnp.where(head_a, o2[:CHUNK], o2[CHUNK:]).astype(BF16)
            others = l_ref[rows, :] if hp > 0 else jnp.zeros((CHUNK, 128), F32)
            l_ref[rows, :] = jnp.where(lane == 2 * hp, lse2[:CHUNK], jnp.where(lane == 2 * hp + 1, lse2[CHUNK:], others))

    same = lambda n: n
    return _call(
        body, name=f"attn_fwd_d{d}", grid=(d, N_PAIRS // PAIRS_PER_STEP, n_tiles),
        in_specs=[_attn_spec(0, tile, same), _attn_spec(N_PAIRS, tile, same), _attn_spec(N_PAIRS, CHUNK, prev),
                  _attn_spec(2 * N_PAIRS, tile, same), _attn_spec(2 * N_PAIRS, CHUNK, prev)],
        out_specs=[_attn_spec(0, tile, same), _stats_spec(tile, same)],
        out_shape=[jax.ShapeDtypeStruct((d, N_PAIRS, sd, 128), BF16), _stats_shape(sd, d)],
        semantics=("parallel", "parallel", "parallel"), args=(qkv, qkv, qkv, qkv, qkv), exchange=exchange)


def _combine(outs, lses, ya_n, gb, exchange=()):
    s_len = ya_n.shape[0]
    tm = ROW_TILE
    n_br = len(DILATIONS)

    def body(*refs):
        o_refs, l_refs = refs[:n_br], refs[n_br:2 * n_br]
        ya_ref, gb_ref, y_ref, yb_ref = refs[2 * n_br:2 * n_br + 4]
        lse_refs = refs[2 * n_br + 4:3 * n_br + 4]
        o_nat, w_nat, lse_nat = refs[3 * n_br + 4:]
        for i, d in enumerate(DILATIONS):
            _from_sub(l_refs[i], 0, w_nat, i, d, tm)
        ls = [w_nat[i] for i in range(n_br)]
        top = jnp.maximum(jnp.maximum(ls[0], ls[1]), ls[2])
        ws = [jnp.exp(l - top) for l in ls]
        den = ws[0] + ws[1] + ws[2]
        inv = 1.0 / den
        for i in range(n_br):
            w_nat[i] = ws[i] * inv
        lse_nat[0] = top + jnp.log(den)
        for d, lse_ref in zip(DILATIONS, lse_refs):
            _to_sub(lse_nat, 0, lse_ref, 0, d, tm)
        head_a = lax.broadcasted_iota(jnp.int32, (tm, 128), 1) < HEAD_DIM
        sumsq = jnp.zeros((tm, 1), F32)
        for cb in range(N_PAIRS):
            yb = jnp.zeros((tm, 128), F32)
            for i, d in enumerate(DILATIONS):
                _from_sub(o_refs[i], cb, o_nat, i, d, tm)
                w = jnp.where(head_a, w_nat[i, :, 2 * cb:2 * cb + 1], w_nat[i, :, 2 * cb + 1:2 * cb + 2])
                yb = yb + w * o_nat[i]
            yb_ref[:, cb * 128:(cb + 1) * 128] = yb
            sumsq = sumsq + jnp.sum(yb * yb, axis=-1, keepdims=True)
        r = lax.rsqrt(sumsq / WIDTH_B + EPS)
        y_ref[:, :WIDTH_A] = ya_ref[...]
        y_ref[:, WIDTH_A:] = (yb_ref[...] * r * gb_ref[...]).astype(BF16)

    stats = [_sub_spec(d, 1, tm) for d in DILATIONS]
    return _call(
        body, name="attn_combine", grid=(s_len // tm,),
        in_specs=[_sub_spec(d, N_PAIRS, tm) for d in DILATIONS] + stats + [_rows(tm, WIDTH_A), _whole((1, WIDTH_B))],
        out_specs=[_rows(tm, D_MODEL), _rows(tm, WIDTH_B)] + stats,
        out_shape=[jax.ShapeDtypeStruct((s_len, D_MODEL), BF16), jax.ShapeDtypeStruct((s_len, WIDTH_B), F32)]
        + [_sub_shape(s_len, d, 1, F32) for d in DILATIONS],
        scratch_shapes=[pltpu.VMEM((n_br, tm, 128), F32), pltpu.VMEM((n_br, tm, 128), F32), pltpu.VMEM((1, tm, 128), F32)],
        semantics=("parallel",), args=(*outs, *lses, ya_n, gb), exchange=exchange)


def _ffn_up(y, wout, x, g, wg, wu, exchange=()):
    s_len = x.shape[0]
    tm = ROW_TILE // 2

    def body(y_ref, wo_ref, x_ref, g_ref, wg_ref, wu_ref, h_ref, a_ref, b_ref, act_ref, hn_ref):
        hf = x_ref[...] + jnp.dot(y_ref[...], wo_ref[...], preferred_element_type=F32)
        h_ref[...] = hf
        hn = (hf * _rstd(hf) * g_ref[...]).astype(BF16)
        hn_ref[...] = hn
        for j in range(D_FF // 256):
            cols = slice(j * 256, (j + 1) * 256)
            a = lax.dot_general(hn, wg_ref[cols, :], _NT, preferred_element_type=F32)
            b = lax.dot_general(hn, wu_ref[cols, :], _NT, preferred_element_type=F32)
            a_ref[:, cols] = a.astype(BF16)
            b_ref[:, cols] = b.astype(BF16)
            act_ref[:, cols] = (a * jax.nn.sigmoid(a) * b).astype(BF16)

    wide = jax.ShapeDtypeStruct((s_len, D_FF), BF16)
    return _call(
        body, name="ffn_up", grid=(s_len // tm,),
        in_specs=[_rows(tm, D_MODEL), _whole((D_MODEL, D_MODEL)), _rows(tm, D_MODEL), _whole((1, D_MODEL)),
                  _whole((D_FF, D_MODEL)), _whole((D_FF, D_MODEL))],
        out_specs=[_rows(tm, D_MODEL), _rows(tm, D_FF), _rows(tm, D_FF), _rows(tm, D_FF), _rows(tm, D_MODEL)],
        out_shape=[jax.ShapeDtypeStruct((s_len, D_MODEL), F32), wide, wide, wide, jax.ShapeDtypeStruct((s_len, D_MODEL), BF16)],
        semantics=("parallel",), args=(y, wout, x, g, wg, wu), exchange=exchange)


def _ffn_down_ple(act, wd, h1, g, wpg, p, wpp):
    s_len = h1.shape[0]
    tm = ROW_TILE // 2

    def body(act_ref, wd_ref, h1_ref, g_ref, wpg_ref, p_ref, wpp_ref, h2_ref, h3_ref, gate_ref, pp_ref, hn_ref):
        hf = h1_ref[...] + jnp.dot(act_ref[...], wd_ref[...], preferred_element_type=F32)
        h2_ref[...] = hf
        hn = (hf * _rstd(hf) * g_ref[...]).astype(BF16)
        hn_ref[...] = hn
        gate = jax.nn.sigmoid(jnp.dot(hn, wpg_ref[...], preferred_element_type=F32))
        pp = lax.dot_general(p_ref[...].astype(BF16), wpp_ref[...], _NT, preferred_element_type=F32)
        h3_ref[...] = hf + gate * pp
        gate_ref[...] = gate.astype(BF16)
        pp_ref[...] = pp.astype(BF16)

    full = jax.ShapeDtypeStruct((s_len, D_MODEL), F32)
    half = jax.ShapeDtypeStruct((s_len, D_MODEL), BF16)
    return pl.pallas_call(
        body, name="ffn_down_ple", grid=(s_len // tm,),
        in_specs=[_rows(tm, D_FF), _whole((D_FF, D_MODEL)), _rows(tm, D_MODEL), _whole((1, D_MODEL)),
                  _whole((D_MODEL, D_MODEL)), _rows(tm, PLE_DIM), _whole((D_MODEL, PLE_DIM))],
        out_specs=[_rows(tm, D_MODEL)] * 5,
        out_shape=[full, full, half, half, half],
        compiler_params=_params("parallel"),
    )(act, wd, h1, g, wpg, p, wpp)


def _loss_ple_bwd(h3, target, gf, gate, pp, h2, g_ple, wpg, hn3, p):
    s_len = h3.shape[0]
    tm = ROW_TILE
    n_steps = s_len // tm

    def body(h_ref, t_ref, g_ref, gate_ref, pp_ref, h2_ref, gp_ref, w_ref, hn_ref, p_ref,
             dh2_ref, loss_ref, dg_ref, dgp_ref, dwg_ref, dwp_ref, acc_g, acc_p):
        step = pl.program_id(0)

        @pl.when(step == 0)
        def _():
            loss_ref[...] = jnp.zeros_like(loss_ref)
            dg_ref[...] = jnp.zeros_like(dg_ref)
            dgp_ref[...] = jnp.zeros_like(dgp_ref)
            acc_g[...] = jnp.zeros_like(acc_g)
            acc_p[...] = jnp.zeros_like(acc_p)

        hf = h_ref[...]
        gfv = g_ref[...]
        err = hf * _rstd(hf) * gfv - t_ref[...]
        loss_ref[...] += 0.5 * jnp.sum(jnp.sum(err * err, axis=-1, keepdims=True), axis=0, keepdims=True) / D_MODEL
        dh, dg_rows = _norm_bwd(err / D_MODEL, hf, gfv)
        dg_ref[...] += jnp.sum(dg_rows, axis=0, keepdims=True)
        gate = gate_ref[...].astype(F32)
        dz = (dh * pp_ref[...].astype(F32) * gate * (1.0 - gate)).astype(BF16)
        dpp = (dh * gate).astype(BF16)
        dn = lax.dot_general(dz, w_ref[...], _NT, preferred_element_type=F32)
        dh2, dgp_rows = _norm_bwd(dn, h2_ref[...], gp_ref[...])
        dh2 = dh + dh2
        dh2_ref[...] = dh2
        dgp_ref[...] += jnp.sum(dgp_rows, axis=0, keepdims=True)
        acc_g[...] += lax.dot_general(hn_ref[...], dz, _TN, preferred_element_type=F32)
        acc_p[...] += lax.dot_general(dpp, p_ref[...].astype(BF16), _TN, preferred_element_type=F32)

        @pl.when(step == n_steps - 1)
        def _():
            dwg_ref[...] = acc_g[...].astype(BF16)
            dwp_ref[...] = acc_p[...].astype(BF16)

    gain = jax.ShapeDtypeStruct((1, D_MODEL), F32)
    return pl.pallas_call(
        body, name="loss_ple_bwd", grid=(n_steps,),
        in_specs=[_rows(tm, D_MODEL), _rows(tm, D_MODEL), _whole((1, D_MODEL)), _rows(tm, D_MODEL), _rows(tm, D_MODEL),
                  _rows(tm, D_MODEL), _whole((1, D_MODEL)), _whole((D_MODEL, D_MODEL)), _rows(tm, D_MODEL),
                  _rows(tm, PLE_DIM)],
        out_specs=[_rows(tm, D_MODEL), _whole((1, 128)), _whole((1, D_MODEL)), _whole((1, D_MODEL)),
                   _whole((D_MODEL, D_MODEL)), _whole((D_MODEL, PLE_DIM))],
        out_shape=[jax.ShapeDtypeStruct((s_len, D_MODEL), F32), jax.ShapeDtypeStruct((1, 128), F32), gain, gain,
                   jax.ShapeDtypeStruct((D_MODEL, D_MODEL), BF16), jax.ShapeDtypeStruct((D_MODEL, PLE_DIM), BF16)],
        scratch_shapes=[pltpu.VMEM((D_MODEL, D_MODEL), F32), pltpu.VMEM((D_MODEL, PLE_DIM), F32)],
        compiler_params=_params("arbitrary"),
    )(h3, target, gf, gate, pp, h2, g_ple, wpg, hn3, p)


def _mm_norm_bwd(parts, h, g, dres, name, exchange=(), dw_lhs=None):
    s_len = h.shape[0]
    tm = ROW_TILE // 2
    n_parts = len(parts)
    n_steps = s_len // tm
    has_dw = dw_lhs is not None

    def body(*refs):
        a_refs = refs[0:2 * n_parts:2]
        w_refs = refs[1:2 * n_parts:2]
        h_ref, g_ref, r_ref = refs[2 * n_parts:2 * n_parts + 3]
        rest = refs[2 * n_parts + 3:]
        step = pl.program_id(0)
        if has_dw:
            lhs_ref, o_ref, dg_ref, dw_ref, acc_ref = rest
        else:
            o_ref, dg_ref = rest

        @pl.when(step == 0)
        def _():
            dg_ref[...] = jnp.zeros_like(dg_ref)
            if has_dw:
                acc_ref[...] = jnp.zeros_like(acc_ref)

        dn = jnp.dot(a_refs[0][...], w_refs[0][...], preferred_element_type=F32)
        for a_ref, w_ref in zip(a_refs[1:], w_refs[1:]):
            dn = dn + jnp.dot(a_ref[...], w_ref[...], preferred_element_type=F32)
        dh, dg_rows = _norm_bwd(dn, h_ref[...], g_ref[...])
        out = r_ref[...] + dh
        o_ref[...] = out
        dg_ref[...] += jnp.sum(dg_rows, axis=0, keepdims=True)
        if has_dw:
            acc_ref[...] += lax.dot_general(lhs_ref[...], out.astype(BF16), _TN, preferred_element_type=F32)

            @pl.when(step == n_steps - 1)
            def _():
                dw_ref[...] = acc_ref[...].astype(BF16)

    in_specs, args = [], []
    for a, w in parts:
        in_specs += [_rows(tm, a.shape[1]), _whole(w.shape)]
        args += [a, w]
    in_specs += [_rows(tm, D_MODEL), _whole((1, D_MODEL)), _rows(tm, D_MODEL)]
    args += [h, g, dres]
    out_specs = [_rows(tm, D_MODEL), _whole((1, D_MODEL))]
    out_shape = [jax.ShapeDtypeStruct((s_len, D_MODEL), F32), jax.ShapeDtypeStruct((1, D_MODEL), F32)]
    scratch = []
    if has_dw:
        m = dw_lhs.shape[1]
        in_specs.append(_rows(tm, m))
        args.append(dw_lhs)
        out_specs.append(_whole((m, D_MODEL)))
        out_shape.append(jax.ShapeDtypeStruct((m, D_MODEL), BF16))
        scratch.append(pltpu.VMEM((m, D_MODEL), F32))
    return _call(
        body, name=name, grid=(n_steps,), in_specs=in_specs, out_specs=out_specs, out_shape=out_shape,
        scratch_shapes=scratch, semantics=("arbitrary",), args=tuple(args), exchange=exchange)


def _ffn_down_bwd(dh, wdt, a, b, exchange=()):
    s_len = dh.shape[0]
    tm = ROW_TILE // 2

    def body(dh_ref, w_ref, a_ref, b_ref, da_ref, db_ref):
        dhb = dh_ref[...].astype(BF16)
        for j in range(D_FF // 256):
            cols = slice(j * 256, (j + 1) * 256)
            dact = lax.dot_general(dhb, w_ref[cols, :], _NT, preferred_element_type=F32)
            av = a_ref[:, cols].astype(F32)
            bv = b_ref[:, cols].astype(F32)
            sig = jax.nn.sigmoid(av)
            da_ref[:, cols] = (dact * bv * sig * (1.0 + av * (1.0 - sig))).astype(BF16)
            db_ref[:, cols] = (dact * av * sig).astype(BF16)

    wide = jax.ShapeDtypeStruct((s_len, D_FF), BF16)
    return _call(
        body, name="ffn_down_bwd", grid=(s_len // tm,),
        in_specs=[_rows(tm, D_MODEL), _whole((D_FF, D_MODEL)), _rows(tm, D_FF), _rows(tm, D_FF)],
        out_specs=[_rows(tm, D_FF), _rows(tm, D_FF)],
        out_shape=[wide, wide],
        semantics=("parallel",), args=(dh, wdt, a, b), exchange=exchange)


def _outproj_bwd(dh1, woutt, yb, gb, head_sum, exchange=()):
    s_len = dh1.shape[0]
    tm = ROW_TILE
    n_br = len(DILATIONS)

    def body(dh_ref, w_ref, yb_ref, gb_ref, e_ref, dya_ref, dgb_ref, *rest):
        do_refs, dd_refs = rest[:n_br], rest[n_br:2 * n_br]
        do_nat, dd_nat = rest[2 * n_br:]

        @pl.when(pl.program_id(0) == 0)
        def _():
            dgb_ref[...] = jnp.zeros_like(dgb_ref)

        dhb = dh_ref[...].astype(BF16)
        dya_ref[...] = lax.dot_general(dhb, w_ref[:WIDTH_A, :], _NT, preferred_element_type=F32)
        dyn = lax.dot_general(dhb, w_ref[WIDTH_A:, :], _NT, preferred_element_type=F32)
        ybv = yb_ref[...]
        dyb, dg_rows = _norm_bwd(dyn, ybv, gb_ref[...])
        dgb_ref[...] += jnp.sum(dg_rows, axis=0, keepdims=True)
        prod = dyb * ybv
        hi = prod.astype(BF16)
        lo = (prod - hi.astype(F32)).astype(BF16)
        dd_nat[0] = (jnp.dot(hi, e_ref[...], preferred_element_type=F32)
                     + jnp.dot(lo, e_ref[...], preferred_element_type=F32))
        for i, d in enumerate(DILATIONS):
            _to_sub(dd_nat, 0, dd_refs[i], 0, d, tm)
        for cb in range(N_PAIRS):
            piece = dyb[:, cb * 128:(cb + 1) * 128]
            do_nat[cb] = piece
            do_refs[0][0, cb] = piece.astype(BF16)
            for i, d in enumerate(DILATIONS[1:], start=1):
                _to_sub(do_nat, cb, do_refs[i], cb, d, tm)

    return _call(
        body, name="outproj_bwd", grid=(s_len // tm,),
        in_specs=[_rows(tm, D_MODEL), _whole((D_MODEL, D_MODEL)), _rows(tm, WIDTH_B), _whole((1, WIDTH_B)), _whole((WIDTH_B, 128))],
        out_specs=[_rows(tm, WIDTH_A), _whole((1, WIDTH_B))] + [_sub_spec(d, N_PAIRS, tm) for d in DILATIONS]
        + [_sub_spec(d, 1, tm) for d in DILATIONS],
        out_shape=[jax.ShapeDtypeStruct((s_len, WIDTH_A), F32), jax.ShapeDtypeStruct((1, WIDTH_B), F32)]
        + [_sub_shape(s_len, d, N_PAIRS, BF16) for d in DILATIONS] + [_sub_shape(s_len, d, 1, F32) for d in DILATIONS],
        scratch_shapes=[pltpu.VMEM((N_PAIRS, tm, 128), F32), pltpu.VMEM((1, tm, 128), F32)],
        semantics=("arbitrary",), args=(dh1, woutt, yb, gb, head_sum), exchange=exchange)


def _attn_bwd(qkv, do, lse, dd, d, exchange=()):
    sd = qkv.shape[2]
    tile, nb, n_tiles = _attn_geometry(sd)
    last_block = sd // CHUNK - 1

    def nxt(n):
        return jnp.minimum((n + 1) * nb, last_block)

    def block(ref, next_ref, j):
        return ref[j * CHUNK:(j + 1) * CHUNK, :] if j < nb else next_ref[...]

    def body(q_ref, qn_ref, k_ref, v_ref, do_ref, don_ref, l_ref, ln_ref, dd_ref, ddn_ref,
             dq_ref, dk_ref, dv_ref, carry_ref):
        l_t = [block(l_ref, ln_ref, j).T for j in range(nb + 1)]
        dd_t = [block(dd_ref, ddn_ref, j).T for j in range(nb + 1)]
        for hp in range(PAIRS_PER_STEP):
            l_rows = [jnp.concatenate([t[2 * hp:2 * hp + 1, :], t[2 * hp + 1:2 * hp + 2, :]], axis=1) for t in l_t]
            dd_rows = [jnp.concatenate([t[2 * hp:2 * hp + 1, :], t[2 * hp + 1:2 * hp + 2, :]], axis=1) for t in dd_t]
            one_pair(q_ref.at[hp], qn_ref.at[hp], k_ref.at[hp], v_ref.at[hp], do_ref.at[hp], don_ref.at[hp],
                     l_rows, dd_rows, dq_ref.at[hp], dk_ref.at[hp], dv_ref.at[hp], carry_ref.at[hp])

    def one_pair(q_ref, qn_ref, k_ref, v_ref, do_ref, don_ref, l_rows, dd_rows, dq_ref, dk_ref, dv_ref, carry_ref):
        n = pl.program_id(2)

        @pl.when(n == 0)
        def _():
            carry_ref[...] = jnp.zeros_like(carry_ref)

        head_a = lax.broadcasted_iota(jnp.int32, (CHUNK, 128), 1) < HEAD_DIM
        col = lax.broadcasted_iota(jnp.int32, (CHUNK, 4 * CHUNK), 1)
        qi = col % CHUNK
        ki = lax.broadcasted_iota(jnp.int32, (CHUNK, 4 * CHUNK), 0)
        is_after = col >= 2 * CHUNK
        mask = (is_after & (ki >= qi)) | (jnp.logical_not(is_after) & (qi >= ki))
        mask_last = mask & jnp.logical_or(jnp.logical_not(is_after), n < n_tiles - 1)
        dq_acc = [carry_ref[...]] + [jnp.zeros((CHUNK, 128), F32) for _ in range(nb)]

        q_st = [jnp.concatenate(_both_heads(block(q_ref, qn_ref, j), head_a), axis=0) for j in range(nb + 1)]
        do_st = [jnp.concatenate(_both_heads(block(do_ref, don_ref, j), head_a), axis=0) for j in range(nb + 1)]

        for j in range(nb):
            rows = slice(j * CHUNK, (j + 1) * CHUNK)
            kj = k_ref[rows, :]
            vj = v_ref[rows, :]
            msk = mask if j + 1 < nb else mask_last
            qs = jnp.concatenate([q_st[j], q_st[j + 1]], axis=0)
            dos = jnp.concatenate([do_st[j], do_st[j + 1]], axis=0)
            ls = jnp.concatenate([l_rows[j], l_rows[j + 1]], axis=1)
            dds = jnp.concatenate([dd_rows[j], dd_rows[j + 1]], axis=1)
            st = lax.dot_general(kj, qs, _NT, preferred_element_type=F32)
            pt = jnp.exp(jnp.where(msk, st - ls, NEG))
            dpt = lax.dot_general(vj, dos, _NT, preferred_element_type=F32)
            dst = (pt * (dpt - dds)).astype(BF16)
            dv_ref[rows, :] = jnp.dot(pt.astype(BF16), dos, preferred_element_type=F32).astype(BF16)
            dk_ref[rows, :] = jnp.dot(dst, qs, preferred_element_type=F32).astype(BF16)
            dqs = lax.dot_general(dst, kj, _TN, preferred_element_type=F32)
            dq_acc[j] = dq_acc[j] + jnp.where(head_a, dqs[:CHUNK], dqs[CHUNK:2 * CHUNK])
            dq_acc[j + 1] = dq_acc[j + 1] + jnp.where(head_a, dqs[2 * CHUNK:3 * CHUNK], dqs[3 * CHUNK:])
        for j in range(nb):
            dq_ref[j * CHUNK:(j + 1) * CHUNK, :] = dq_acc[j].astype(BF16)
        carry_ref[...] = dq_acc[nb]

    same = lambda n: n
    grad = jax.ShapeDtypeStruct((d, N_PAIRS, sd, 128), BF16)
    return _call(
        body, name=f"attn_bwd_d{d}", grid=(d, N_PAIRS // PAIRS_PER_STEP, n_tiles),
        in_specs=[_attn_spec(0, tile, same), _attn_spec(0, CHUNK, nxt), _attn_spec(N_PAIRS, tile, same),
                  _attn_spec(2 * N_PAIRS, tile, same), _attn_spec(0, tile, same), _attn_spec(0, CHUNK, nxt),
                  _stats_spec(tile, same), _stats_spec(CHUNK, nxt), _stats_spec(tile, same), _stats_spec(CHUNK, nxt)],
        out_specs=[_attn_spec(0, tile, same)] * 3,
        out_shape=[grad, grad, grad],
        scratch_shapes=[pltpu.VMEM((PAIRS_PER_STEP, CHUNK, 128), F32)],
        semantics=("parallel", "parallel", "arbitrary"), args=(qkv, qkv, qkv, qkv, do, do, lse, lse, dd, dd), exchange=exchange)


def _sgu_bwd(ua, sw, b2, gs, ga, dya_n):
    s_len = ua.shape[0]
    tm = ROW_TILE

    def body(ua_ref, sw_ref, b2_ref, gs_ref, ga_ref, dy_ref, dua_ref, dsw_ref, db2_ref, dgs_ref, dga_ref):
        @pl.when(pl.program_id(0) == 0)
        def _():
            dsw_ref[...] = jnp.zeros_like(dsw_ref)
            db2_ref[...] = jnp.zeros_like(db2_ref)
            dgs_ref[...] = jnp.zeros_like(dgs_ref)
            dga_ref[...] = jnp.zeros_like(dga_ref)

        u, va, ug, xhat, rstd, vn = _sgu_core(ua_ref, gs_ref)
        wm, keep = _sgu_mix_weights(sw_ref)
        head = lax.broadcasted_iota(jnp.int32, (CHUNK, WIDTH_A), 1) // HEAD_DIM
        gav = ga_ref[...]
        gsv = gs_ref[...]
        dga = jnp.zeros((1, WIDTH_A), F32)
        dgs = jnp.zeros((1, WIDTH_A), F32)
        db2 = jnp.zeros((CHUNK, WIDTH_A), F32)
        dsw = [jnp.zeros((CHUNK, CHUNK), F32) for _ in range(4)]
        for c in range(tm // CHUNK):
            rows = slice(c * CHUNK, (c + 1) * CHUNK)
            vnc = vn[rows]
            vnb = vnc.astype(BF16)
            mixed = b2_ref[...]
            for h in range(4):
                mixed = mixed + jnp.dot(wm[h], jnp.where(head == h, vnc, 0.0).astype(BF16), preferred_element_type=F32)
            ugc = ug[rows]
            dya, dga_rows = _norm_bwd(dy_ref[rows, :], ugc * mixed, gav)
            dga = dga + jnp.sum(dga_rows, axis=0, keepdims=True)
            dmixed = dya * ugc
            db2 = db2 + dmixed
            dvn = jnp.zeros((CHUNK, WIDTH_A), F32)
            for h in range(4):
                dmh = jnp.where(head == h, dmixed, 0.0).astype(BF16)
                dsw[h] = dsw[h] + lax.dot_general(dmh, vnb, _NT, preferred_element_type=F32)
                dvn = dvn + lax.dot_general(wm[h], dmh, _TN, preferred_element_type=F32)
            xh = xhat[rows]
            dgs = dgs + jnp.sum(dvn * xh, axis=0, keepdims=True)
            dxh = dvn * gsv
            dvg = rstd[rows] * (dxh - jnp.mean(dxh, axis=-1, keepdims=True) - xh * jnp.mean(dxh * xh, axis=-1, keepdims=True))
            dua_ref[rows, :WIDTH_A] = (dya * mixed * _gelu_grad(u[rows])).astype(BF16)
            dua_ref[rows, WIDTH_A:] = (dvg * _gelu_grad(va[rows])).astype(BF16)
        for h in range(4):
            dsw_ref[h] += jnp.where(keep, dsw[h], 0.0)
        db2_ref[...] += db2
        dgs_ref[...] += dgs
        dga_ref[...] += dga

    return pl.pallas_call(
        body, name="sgu_bwd", grid=(s_len // tm,),
        in_specs=[_rows(tm, 2 * WIDTH_A), _whole((4, CHUNK, CHUNK)), _whole((CHUNK, WIDTH_A)), _whole((1, WIDTH_A)),
                  _whole((1, WIDTH_A)), _rows(tm, WIDTH_A)],
        out_specs=[_rows(tm, 2 * WIDTH_A), _whole((4, CHUNK, CHUNK)), _whole((CHUNK, WIDTH_A)), _whole((1, WIDTH_A)), _whole((1, WIDTH_A))],
        out_shape=[jax.ShapeDtypeStruct((s_len, 2 * WIDTH_A), BF16), jax.ShapeDtypeStruct((4, CHUNK, CHUNK), F32),
                   jax.ShapeDtypeStruct((CHUNK, WIDTH_A), F32), jax.ShapeDtypeStruct((1, WIDTH_A), F32),
                   jax.ShapeDtypeStruct((1, WIDTH_A), F32)],
        compiler_params=_params("arbitrary"),
    )(ua, sw, b2, gs, ga, dya_n)


def _dproj(dua, dqs, dks, dvs, cos, sin, hn1, exchange=()):
    s_len = dua.shape[0]
    tm = ROW_TILE
    n_br = len(DILATIONS)
    n_steps = s_len // tm

    def body(dua_ref, *rest):
        groups = [rest[g * n_br:(g + 1) * n_br] for g in range(3)]
        cos_ref, sin_ref, hn_ref, out_ref, dw_ref, acc, dw_acc = rest[3 * n_br:]
        step = pl.program_id(0)

        @pl.when(step == 0)
        def _():
            dw_acc[...] = jnp.zeros_like(dw_acc)

        out_ref[:, :2 * WIDTH_A] = dua_ref[...]
        c = cos_ref[...]
        s = sin_ref[...]
        first_half = (lax.broadcasted_iota(jnp.int32, (tm, 128), 1) % HEAD_DIM) < HEAD_DIM // 2
        for g, refs in enumerate(groups):
            for cb in range(N_PAIRS):
                for i, d in enumerate(DILATIONS):
                    _from_sub(refs[i], cb, acc, 0, d, tm, accumulate=i > 0)
                t = acc[0]
                if g < 2:
                    t = (t * c - _swap_halves(t, first_half) * s) * (0.125 if g == 0 else 1.0)
                col = 2 * WIDTH_A + g * WIDTH_B + cb * 128
                out_ref[:, col:col + 128] = t.astype(BF16)
        hn = hn_ref[...]
        for j in range(IN_COLS // 256):
            cols = slice(j * 256, (j + 1) * 256)
            dw_acc[cols, :] += lax.dot_general(out_ref[:, cols], hn, _TN, preferred_element_type=F32)

        @pl.when(step == n_steps - 1)
        def _():
            dw_ref[...] = dw_acc[...].astype(BF16)

    subs = [_sub_spec(d, N_PAIRS, tm) for d in DILATIONS]
    (dproj, dw), received = _call(
        body, name="dproj_dw_in", grid=(n_steps,),
        in_specs=[_rows(tm, 2 * WIDTH_A)] + subs * 3 + [_rows(tm, 128), _rows(tm, 128), _rows(tm, D_MODEL)],
        out_specs=[_rows(tm, IN_COLS), _whole((IN_COLS, D_MODEL))],
        out_shape=[jax.ShapeDtypeStruct((s_len, IN_COLS), BF16), jax.ShapeDtypeStruct((IN_COLS, D_MODEL), BF16)],
        scratch_shapes=[pltpu.VMEM((1, tm, 128), F32), pltpu.VMEM((IN_COLS, D_MODEL), F32)],
        semantics=("arbitrary",), args=(dua, *dqs, *dks, *dvs, cos, sin, hn1), exchange=exchange)
    return dproj, dw, received


def _mm_tn(a, b, name, exchange=()):
    s_len, m = a.shape
    n = b.shape[1]
    tk = 2 * ROW_TILE
    tm = m if m <= 512 else (1408 if m == D_FF else 512)
    n_k = s_len // tk

    def body(a_ref, b_ref, o_ref, acc_ref):
        k = pl.program_id(1)

        @pl.when(k == 0)
        def _():
            acc_ref[...] = jnp.zeros_like(acc_ref)

        acc_ref[...] += lax.dot_general(a_ref[...].astype(BF16), b_ref[...].astype(BF16), _TN, preferred_element_type=F32)

        @pl.when(k == n_k - 1)
        def _():
            o_ref[...] = acc_ref[...].astype(BF16)

    (grad,), received = _call(
        body, name=name, grid=(m // tm, n_k),
        in_specs=[pl.BlockSpec((tk, tm), lambda i, k: (k, i)), pl.BlockSpec((tk, n), lambda i, k: (k, 0))],
        out_specs=[pl.BlockSpec((tm, n), lambda i, k: (i, 0))],
        out_shape=[jax.ShapeDtypeStruct((m, n), BF16)],
        scratch_shapes=[pltpu.VMEM((tm, n), F32)],
        semantics=("parallel", "arbitrary"), args=(a, b), exchange=exchange)
    return grad, received


def _position():
    x, y, c = lax.axis_index("x"), lax.axis_index("y"), lax.axis_index("c")
    return x, y, c, 4 * x + 2 * y + c


def _peer(x, y, c, rel):
    return (x ^ ((rel >> 2) & 1), y ^ ((rel >> 1) & 1), c ^ (rel & 1))


def _exchange_out_shape(kind, arr):
    return jax.ShapeDtypeStruct(((N_DEV,) + arr.shape) if kind == "gather" else arr.shape, arr.dtype)


def _exchange_sems(n_items):
    return [pltpu.SemaphoreType.DMA((n_items, N_DEV)), pltpu.SemaphoreType.DMA((n_items, N_DEV)), pltpu.SemaphoreType.DMA((n_items,))]


def _exchange_copies(kinds, srcs, dsts, sems):
    send_sems, recv_sems, local_sems = sems
    x, y, c, me = _position()
    local, sends, recvs = [], [], []
    for k, (kind, src, dst) in enumerate(zip(kinds, srcs, dsts)):
        own = src if kind == "gather" else src.at[me]
        local.append(pltpu.make_async_copy(own, dst.at[me], local_sems.at[k]))
        for rel in range(1, N_DEV):
            going = src if kind == "gather" else src.at[me ^ rel]
            common = dict(send_sem=send_sems.at[k, rel], recv_sem=recv_sems.at[k, rel],
                          device_id=_peer(x, y, c, rel), device_id_type=MESH)
            sends.append(pltpu.make_async_remote_copy(src_ref=going, dst_ref=dst.at[me], **common))
            recvs.append(pltpu.make_async_remote_copy(src_ref=own, dst_ref=dst.at[me ^ rel], **common))
    return local, sends, recvs


def _exchange_start(kinds, srcs, dsts, sems):
    local, sends, _ = _exchange_copies(kinds, srcs, dsts, sems)
    for cp in local + sends:
        cp.start()


def _exchange_finish(kinds, srcs, dsts, sems):
    local, sends, recvs = _exchange_copies(kinds, srcs, dsts, sems)
    for cp in recvs:
        cp.wait_recv()
    for cp in sends:
        cp.wait_send()
    for cp in local:
        cp.wait()


def _exchange_only(items, name):
    kinds = [k for k, _ in items]
    n = len(items)

    def body(*refs):
        srcs, dsts, sems = refs[:n], refs[n:2 * n], refs[2 * n:]
        _exchange_start(kinds, srcs, dsts, sems)
        _exchange_finish(kinds, srcs, dsts, sems)

    any_spec = pl.BlockSpec(memory_space=pl.ANY)
    return pl.pallas_call(
        body, name=name, in_specs=[any_spec] * n, out_specs=[any_spec] * n,
        out_shape=[_exchange_out_shape(k, a) for k, a in items],
        scratch_shapes=_exchange_sems(n),
        compiler_params=pltpu.CompilerParams(has_side_effects=True),
    )(*[a for _, a in items])


def _gather_two_level_with_rope_tables(shard, inv_freq, s_len, name):
    rows = ROW_TILE

    def body(inv_ref, src, cos_ref, sin_ref, dst, send_sems, recv_sems, local_sem):
        x, y, c, me = _position()
        sibling = (x, y, 1 - c)
        chips = [(1 - x, y), (x, 1 - y), (1 - x, 1 - y)]

        def block(px, py, pc):
            return dst.at[4 * px + 2 * py + pc]

        def copy(k, blk, to, src_ref=None):
            return pltpu.make_async_remote_copy(
                src_ref=block(*blk) if src_ref is None else src_ref, dst_ref=block(*blk),
                send_sem=send_sems.at[k], recv_sem=recv_sems.at[k], device_id=to, device_id_type=MESH)

        mine = pltpu.make_async_copy(src, dst.at[me], local_sem)
        mine.start()
        first = [copy(0, (x, y, c), sibling, src)] + [copy(1 + j, (x, y, c), (*chip, c), src) for j, chip in enumerate(chips)]
        for cp in first:
            cp.start()

        inv = inv_ref[...]
        lane = lax.broadcasted_iota(jnp.int32, (rows, 128), 1)
        sign = jnp.where((lane // (HEAD_DIM // 2)) % 2 == 0, -1.0, 1.0)
        row = lax.broadcasted_iota(jnp.int32, (rows, 128), 0)

        @pl.loop(0, s_len // rows)
        def _(i):
            at = pl.multiple_of(i * rows, rows)
            ang = (row + at).astype(F32) * inv
            cos_ref[pl.ds(at, rows), :] = jnp.cos(ang)
            sin_ref[pl.ds(at, rows), :] = jnp.sin(ang) * sign

        passed = [copy(4 + j, (*chip, c), sibling) for j, chip in enumerate(chips)]
        for j, chip in enumerate(chips):
            copy(1 + j, (*chip, c), (x, y, c)).wait_recv()
            passed[j].start()
        copy(0, (x, y, 1 - c), (x, y, c)).wait_recv()
        for j, chip in enumerate(chips):
            copy(4 + j, (*chip, 1 - c), (x, y, c)).wait_recv()
        for cp in first + passed:
            cp.wait_send()
        mine.wait()

    any_spec = pl.BlockSpec(memory_space=pl.ANY)
    vmem = pl.BlockSpec(memory_space=pltpu.VMEM)
    table = jax.ShapeDtypeStruct((s_len, 128), F32)
    return pl.pallas_call(
        body, name=name, in_specs=[vmem, any_spec], out_specs=[vmem, vmem, any_spec],
        out_shape=[table, table, _exchange_out_shape("gather", shard)],
        scratch_shapes=[pltpu.SemaphoreType.DMA((N_DEV - 1,)), pltpu.SemaphoreType.DMA((N_DEV - 1,)), pltpu.SemaphoreType.DMA],
        compiler_params=pltpu.CompilerParams(has_side_effects=True, vmem_limit_bytes=V7X_VMEM_LIMIT_BYTES),
    )(inv_freq, shard)


def _call(body, *, name, grid, in_specs, out_specs, out_shape, args, scratch_shapes=(), semantics, exchange=()):
    if not exchange:
        outs = pl.pallas_call(body, name=name, grid=grid, in_specs=in_specs, out_specs=out_specs, out_shape=out_shape,
                              scratch_shapes=list(scratch_shapes), compiler_params=_params(*semantics))(*args)
        return outs, []
    kinds = [k for k, _ in exchange]
    n_in, n_out, n_x, n_scr = len(in_specs), len(out_specs), len(exchange), len(scratch_shapes)

    def wrapped(*refs):
        ins, refs = refs[:n_in], refs[n_in:]
        srcs, refs = refs[:n_x], refs[n_x:]
        outs, refs = refs[:n_out], refs[n_out:]
        dsts, refs = refs[:n_x], refs[n_x:]
        scratch, sems = refs[:n_scr], refs[n_scr:]
        ids = [pl.program_id(a) for a in range(len(grid))]
        first = functools.reduce(jnp.logical_and, [i == 0 for i in ids])
        last = functools.reduce(jnp.logical_and, [i == g - 1 for i, g in zip(ids, grid)])

        @pl.when(first)
        def _():
            _exchange_start(kinds, srcs, dsts, sems)

        body(*ins, *outs, *scratch)

        @pl.when(last)
        def _():
            _exchange_finish(kinds, srcs, dsts, sems)

    any_spec = pl.BlockSpec(memory_space=pl.ANY)
    outs = pl.pallas_call(
        wrapped, name=name, grid=grid,
        in_specs=list(in_specs) + [any_spec] * n_x, out_specs=list(out_specs) + [any_spec] * n_x,
        out_shape=list(out_shape) + [_exchange_out_shape(k, a) for k, a in exchange],
        scratch_shapes=list(scratch_shapes) + _exchange_sems(n_x),
        compiler_params=pltpu.CompilerParams(dimension_semantics=("arbitrary",) * len(grid),
                                             vmem_limit_bytes=V7X_VMEM_LIMIT_BYTES, has_side_effects=True),
    )(*args, *[a for _, a in exchange])
    return outs[:n_out], outs[n_out:]


def _adamw_math(w, g, m, v):
    m = ADAM_B1 * m + (1.0 - ADAM_B1) * g
    v = ADAM_B2 * v + (1.0 - ADAM_B2) * (g * g)
    m_hat = m / (1.0 - ADAM_B1 ** ADAM_STEP)
    v_hat = v / (1.0 - ADAM_B2 ** ADAM_STEP)
    return -ADAM_LR * (m_hat / (jnp.sqrt(v_hat) + ADAM_EPS) + ADAM_WD * w), m, v


def _adamw(parts, w, m, v, name):
    rows, cols = w.shape
    tm = 256 if rows % 256 == 0 and rows > 256 else rows

    def body(p_ref, w_ref, m_ref, v_ref, g_ref, d_ref, nm_ref, nv_ref):
        g = p_ref[0].astype(F32)
        for j in range(1, N_DEV):
            g = g + p_ref[j].astype(F32)
        delta, nm, nv = _adamw_math(w_ref[...], g, m_ref[...], v_ref[...])
        g_ref[...] = g
        d_ref[...] = delta
        nm_ref[...] = nm
        nv_ref[...] = nv

    shard = jax.ShapeDtypeStruct((rows, cols), F32)
    return pl.pallas_call(
        body, name=name, grid=(rows // tm,),
        in_specs=[pl.BlockSpec((N_DEV, tm, cols), lambda i: (0, i, 0))] + [_rows(tm, cols)] * 3,
        out_specs=[_rows(tm, cols)] * 4,
        out_shape=[shard] * 4,
        compiler_params=_params("parallel"),
    )(parts, w, m, v)


_SMALL = ("mix_norm_g", "sgu_w", "sgu_b", "sgu_norm_g", "out_norm_a", "out_norm_b", "ffn_norm_g", "ple_norm_g", "final_norm_g")
_BIG = ("w_in", "w_out", "w_gate", "w_up", "w_down", "w_ple_gate", "w_ple_proj")
_COLUMN_SHARDED = ("w_in", "w_gate", "w_up", "w_ple_proj")
_ORDER = ("mix_norm_g", "w_in", "sgu_w", "sgu_b", "sgu_norm_g", "out_norm_a", "out_norm_b", "w_out", "ffn_norm_g",
          "w_gate", "w_up", "w_down", "ple_norm_g", "w_ple_gate", "w_ple_proj", "final_norm_g")


def _pack_small(values, names=_SMALL):
    flat = jnp.concatenate([values[n].reshape(-1).astype(F32) for n in names])
    pad = (-flat.shape[0]) % (8 * 128)
    return jnp.pad(flat, (0, pad)).reshape(-1, 128)


def _unpack_small(packed, like):
    flat = packed.reshape(-1)
    out, at = {}, 0
    for n in _SMALL:
        size = like[n].size
        out[n] = flat[at:at + size].reshape(like[n].shape)
        at += size
    return out


def _own_orientation(name, value):
    return value[0].T if name in _COLUMN_SHARDED else value[0]


def _reference_orientation(name, value):
    return (value.T if name in _COLUMN_SHARDED else value)[None]


def _full_from_gathered(gathered):
    return gathered.reshape(N_DEV * gathered.shape[1], gathered.shape[2])


def _sliced_for_devices(grad):
    return grad.reshape(N_DEV, grad.shape[0] // N_DEV, grad.shape[1])


def _rope_inv_freq():
    half = HEAD_DIM // 2
    inv = ROPE_THETA ** (-jnp.arange(half, dtype=F32) / half)
    return jnp.tile(inv, 128 // half)[None, :]


def _forward_backward(x, p, target, small, shards):
    def gather(*names):
        return [("gather", shards[n]) for n in names]

    def scatter(**grads):
        return [("scatter", _sliced_for_devices(g)) for g in grads.values()]

    full, parts = {}, {}
    s_len = x.shape[0]
    cos, sin, got = _gather_two_level_with_rope_tables(shards["w_in"], _rope_inv_freq(), s_len, "gather_w_in")
    full["w_in"] = _full_from_gathered(got)

    g_mix, g_ffn, g_ple = small["mix_norm_g"], small["ffn_norm_g"], small["ple_norm_g"]
    g_fin = small["final_norm_g"].reshape(1, D_MODEL)
    sw, gs, ga, gb = small["sgu_w"], small["sgu_norm_g"], small["out_norm_a"], small["out_norm_b"]
    b2 = jnp.repeat(small["sgu_b"].T, HEAD_DIM, axis=1)
    head_sum = (jnp.arange(WIDTH_B)[:, None] // HEAD_DIM == jnp.arange(128)[None, :]).astype(BF16)
    n_br = len(DILATIONS)

    def arrived(names, got):
        for n, g in zip(names, got):
            full[n] = _full_from_gathered(g)

    (ua, hn1, *qkv), got = _inproj(x, g_mix, full["w_in"], cos, sin, exchange=gather("w_gate"))
    arrived(("w_gate",), got)
    ya_n = _sgu_fwd(ua, sw, b2, gs, ga)
    half = shards["w_up"].shape[0] // 2
    riders = [[("gather", shards["w_up"][:half])], [("gather", shards["w_up"][half:])], gather("w_out")]
    branch, got = [], []
    for i, d in enumerate(DILATIONS):
        o_l, g = _attn_fwd(qkv[i], d, exchange=riders[i])
        branch.append(o_l)
        got += g
    arrived(("w_up", "w_out"), [jnp.concatenate(got[:2], axis=1), got[2]])
    (y, yb, *lse), _ = _combine([o for o, _ in branch], [l for _, l in branch], ya_n, gb)
    last_wave = ("w_down", "w_ple_gate", "w_ple_proj")
    (h1, a, b, act, hn2), got = _ffn_up(y, full["w_out"], x, g_ffn, full["w_gate"], full["w_up"], exchange=gather(*last_wave))
    arrived(last_wave, got)
    h2, h3, gate, pp, hn3 = _ffn_down_ple(act, full["w_down"], h1, g_ple, full["w_ple_gate"], p, full["w_ple_proj"])

    dh2, loss, d_fin, d_ple, g_ple_gate, g_ple_proj = _loss_ple_bwd(
        h3, target, g_fin, gate, pp, h2, g_ple, full["w_ple_gate"], hn3, p)
    g_down, (parts["w_ple_gate"], parts["w_ple_proj"]) = _mm_tn(
        act, dh2, "dw_down", exchange=scatter(w_ple_gate=g_ple_gate, w_ple_proj=g_ple_proj))
    (da, db), (parts["w_down"],) = _ffn_down_bwd(dh2, full["w_down"], a, b, exchange=scatter(w_down=g_down))
    g_gate, _ = _mm_tn(da, hn2, "dw_gate")
    g_up, _ = _mm_tn(db, hn2, "dw_up")
    (dh1, d_ffn, g_out), (parts["w_gate"],) = _mm_norm_bwd(
        [(da, full["w_gate"]), (db, full["w_up"])], h1, g_ffn, dh2, "ffn_up_bwd", exchange=scatter(w_gate=g_gate), dw_lhs=y)
    (dya_n, d_gb, *do_dd), (parts["w_out"],) = _outproj_bwd(dh1, full["w_out"], yb, gb, head_sum, exchange=scatter(w_out=g_out))
    grads_b = []
    for i, d in enumerate(DILATIONS):
        g3, got = _attn_bwd(qkv[i], do_dd[i], lse[i], do_dd[n_br + i], d, exchange=scatter(w_up=g_up) if i == 0 else ())
        grads_b.append(g3)
        if i == 0:
            (parts["w_up"],) = got
    dua, d_sw, d_b2, d_gs, d_ga = _sgu_bwd(ua, sw, b2, gs, ga, dya_n)
    early = {
        "sgu_w": d_sw, "sgu_b": d_b2.reshape(CHUNK, 4, HEAD_DIM).sum(axis=-1).T, "sgu_norm_g": d_gs, "out_norm_a": d_ga,
        "out_norm_b": d_gb, "ffn_norm_g": d_ffn, "ple_norm_g": d_ple, "final_norm_g": d_fin,
    }
    dproj, g_in, (early_parts,) = _dproj(
        dua, [g[0] for g in grads_b], [g[1] for g in grads_b], [g[2] for g in grads_b], cos, sin, hn1,
        exchange=[("gather", _pack_small(early, _SMALL[1:]))])
    (dx, d_mix), (parts["w_in"],) = _mm_norm_bwd(
        [(dproj, full["w_in"])], x, g_mix, dh1, "inproj_bwd", exchange=scatter(w_in=g_in))
    late = jnp.concatenate([_pack_small({"mix_norm_g": d_mix}, _SMALL[:1]), jnp.broadcast_to(loss, (8, 128))])
    (late_parts,) = _exchange_only([("gather", late)], "gather_mix_norm_grad_and_loss")
    total_loss = jnp.sum(late_parts[:, 8, 0])
    return total_loss, dx, parts, jnp.concatenate([late_parts[:, :8], early_parts], axis=1)


def kernel(x, p, mix_norm_g, w_in, sgu_w, sgu_b, sgu_norm_g, out_norm_a, out_norm_b, w_out, ffn_norm_g, w_gate, w_up, w_down, ple_norm_g, w_ple_gate, w_ple_proj, final_norm_g, loss_target, m_mix_norm_g, m_w_in, m_sgu_w, m_sgu_b, m_sgu_norm_g, m_out_norm_a, m_out_norm_b, m_w_out, m_ffn_norm_g, m_w_gate, m_w_up, m_w_down, m_ple_norm_g, m_w_ple_gate, m_w_ple_proj, m_final_norm_g, v_mix_norm_g, v_w_in, v_sgu_w, v_sgu_b, v_sgu_norm_g, v_out_norm_a, v_out_norm_b, v_w_out, v_ffn_norm_g, v_w_gate, v_w_up, v_w_down, v_ple_norm_g, v_w_ple_gate, v_w_ple_proj, v_final_norm_g):
    given = dict(locals())
    weights = {n: given[n] for n in _ORDER}
    moments_m = {n: given["m_" + n] for n in _ORDER}
    moments_v = {n: given["v_" + n] for n in _ORDER}

    shards = {n: _own_orientation(n, weights[n]).astype(BF16) for n in _BIG}
    small = {n: (weights[n][0] if n in ("sgu_w", "sgu_b") else weights[n]) for n in _SMALL}

    loss, dx, parts, small_parts = _forward_backward(x[0], p[0, 0], loss_target[0], small, shards)

    small_like = {n: weights[n] for n in _SMALL}
    grads, deltas, new_m, new_v = {}, {}, {}, {}
    for n in _BIG:
        outs = _adamw(parts[n], _own_orientation(n, weights[n]), _own_orientation(n, moments_m[n]),
                      _own_orientation(n, moments_v[n]), "adamw_" + n)
        grads[n], deltas[n], new_m[n], new_v[n] = [_reference_orientation(n, o) for o in outs]
    g, d, nm, nv = _adamw(small_parts, _pack_small(small_like), _pack_small({n: moments_m[n] for n in _SMALL}),
                          _pack_small({n: moments_v[n] for n in _SMALL}), "adamw_small")
    for out, packed in ((grads, g), (deltas, d), (new_m, nm), (new_v, nv)):
        out.update(_unpack_small(packed, small_like))

    return (loss, dx[None], *[grads[n] for n in _ORDER], *[deltas[n] for n in _ORDER],
            *[new_m[n] for n in _ORDER], *[new_v[n] for n in _ORDER])
```

```python
import functools

import jax
import jax.numpy as jnp
from jax import lax
from jax.experimental import pallas as pl
from jax.experimental.pallas import tpu as pltpu

F32 = jnp.float32
BF16 = jnp.bfloat16

D_MODEL = 1024
WIDTH_A = 256
WIDTH_B = 768
D_FF = 2816
IN_COLS = 2 * WIDTH_A + 3 * WIDTH_B
PLE_DIM = 256
HEAD_DIM = 64
N_PAIRS = WIDTH_B // 128
CHUNK = 128
N_BACK = 128
DILATIONS = (1, 4, 16)
ROPE_THETA = 10000.0
EPS = 1e-6
N_DEV = 8

ADAM_LR = 0.001
ADAM_B1 = 0.9
ADAM_B2 = 0.999
ADAM_EPS = 1e-08
ADAM_WD = 0.01
ADAM_STEP = 10

V7X_VMEM_LIMIT_BYTES = 56 * 1024 * 1024
ROW_TILE = 512
MESH = pl.DeviceIdType.MESH
NEG = -1e30

_NT = (((1,), (1,)), ((), ()))
_TN = (((0,), (0,)), ((), ()))


def _params(*semantics):
    return pltpu.CompilerParams(dimension_semantics=semantics, vmem_limit_bytes=V7X_VMEM_LIMIT_BYTES)


def _rows(tm, width):
    return pl.BlockSpec((tm, width), lambda i: (i, 0))


def _whole(shape):
    return pl.BlockSpec(shape, lambda *_: (0,) * len(shape))


def _gelu(x):
    t = jnp.tanh(0.7978845608028654 * (x + 0.044715 * (x * x * x)))
    return 0.5 * x * (1.0 + t)


def _gelu_grad(x):
    t = jnp.tanh(0.7978845608028654 * (x + 0.044715 * (x * x * x)))
    return 0.5 * (1.0 + t) + 0.5 * x * (1.0 - t * t) * (0.7978845608028654 * (1.0 + 3.0 * 0.044715 * (x * x)))


def _rstd(x):
    return lax.rsqrt(jnp.mean(x * x, axis=-1, keepdims=True) + EPS)


def _norm_bwd(dn, h, g):
    r = _rstd(h)
    n = h * r
    t = dn * g
    return r * (t - n * jnp.mean(t * n, axis=-1, keepdims=True)), dn * n


def _swap_halves(x, first_half):
    return jnp.where(first_half, pltpu.roll(x, 96, 1), pltpu.roll(x, 32, 1))


def _sub_spec(d, n_cb, tm):
    return pl.BlockSpec((d, n_cb, tm // d, 128), lambda i: (0, 0, i, 0))


def _sub_shape(s_len, d, n_cb, dtype):
    return jax.ShapeDtypeStruct((d, n_cb, s_len // d, 128), dtype)


def _to_sub(stage_ref, cb_src, out_ref, cb_dst, d, tm):
    slab = stage_ref.at[cb_src]
    for r in range(d):
        out_ref[r, cb_dst] = slab[pl.ds(r, tm // d, stride=d), :].astype(out_ref.dtype)


def _from_sub(in_ref, cb_src, stage_ref, cb_dst, d, tm):
    slab = stage_ref.at[cb_dst]
    for r in range(d):
        slab[pl.ds(r, tm // d, stride=d), :] = in_ref[r, cb_src].astype(F32)


def _inproj(x, g, w, cos, sin, exchange=()):
    s_len = x.shape[0]
    tm = ROW_TILE
    n_cb = 3 * N_PAIRS

    def body(x_ref, g_ref, w_ref, cos_ref, sin_ref, ua_ref, hn_ref, *rest):
        sub_refs, stage = rest[:-1], rest[-1]
        xf = x_ref[...]
        hn = (xf * _rstd(xf) * g_ref[...]).astype(BF16)
        hn_ref[...] = hn
        c = cos_ref[...]
        s = sin_ref[...]
        first_half = (lax.broadcasted_iota(jnp.int32, (tm, 128), 1) % HEAD_DIM) < HEAD_DIM // 2
        for j in range(IN_COLS // 256):
            col = j * 256
            acc = lax.dot_general(hn, w_ref[col:col + 256, :], _NT, preferred_element_type=F32)
            if col < 2 * WIDTH_A:
                ua_ref[:, col:col + 256] = acc
                continue
            for half in range(2):
                cb = (col - 2 * WIDTH_A) // 128 + half
                t = acc[:, half * 128:(half + 1) * 128]
                if cb < 2 * N_PAIRS:
                    t = (t * c + _swap_halves(t, first_half) * s) * (0.125 if cb < N_PAIRS else 1.0)
                stage[cb] = t
                sub_refs[0][0, cb] = t.astype(BF16)
        for cb in range(n_cb):
            for d, out_ref in zip(DILATIONS[1:], sub_refs[1:]):
                _to_sub(stage, cb, out_ref, cb, d, tm)

    return _call(
        body, name="inproj", grid=(s_len // tm,),
        in_specs=[_rows(tm, D_MODEL), _whole((1, D_MODEL)), _whole((IN_COLS, D_MODEL)), _rows(tm, 128), _rows(tm, 128)],
        out_specs=[_rows(tm, 2 * WIDTH_A), _rows(tm, D_MODEL)] + [_sub_spec(d, n_cb, tm) for d in DILATIONS],
        out_shape=[jax.ShapeDtypeStruct((s_len, 2 * WIDTH_A), F32), jax.ShapeDtypeStruct((s_len, D_MODEL), BF16)]
        + [_sub_shape(s_len, d, n_cb, BF16) for d in DILATIONS],
        scratch_shapes=[pltpu.VMEM((n_cb, tm, 128), F32)],
        semantics=("parallel",), args=(x, g, w, cos, sin), exchange=exchange)


def _sgu_mix_weights(sw_ref):
    keep = lax.broadcasted_iota(jnp.int32, (CHUNK, CHUNK), 0) >= lax.broadcasted_iota(jnp.int32, (CHUNK, CHUNK), 1)
    return [jnp.where(keep, sw_ref[h], 0.0).astype(BF16) for h in range(4)], keep


def _sgu_core(ua_ref, gs_ref):
    u = ua_ref[:, :WIDTH_A]
    va = ua_ref[:, WIDTH_A:]
    vg = _gelu(va)
    xc = vg - jnp.mean(vg, axis=-1, keepdims=True)
    rstd = lax.rsqrt(jnp.mean(xc * xc, axis=-1, keepdims=True) + EPS)
    xhat = xc * rstd
    return u, va, _gelu(u), xhat, rstd, xhat * gs_ref[...]


def _sgu_fwd(ua, sw, b2, gs, ga):
    s_len = ua.shape[0]
    tm = ROW_TILE

    def body(ua_ref, sw_ref, b2_ref, gs_ref, ga_ref, out_ref):
        _, _, ug, _, _, vn = _sgu_core(ua_ref, gs_ref)
        wm, _ = _sgu_mix_weights(sw_ref)
        head = lax.broadcasted_iota(jnp.int32, (CHUNK, WIDTH_A), 1) // HEAD_DIM
        for c in range(tm // CHUNK):
            rows = slice(c * CHUNK, (c + 1) * CHUNK)
            vnc = vn[rows]
            mixed = b2_ref[...]
            for h in range(4):
                mixed = mixed + jnp.dot(wm[h], jnp.where(head == h, vnc, 0.0).astype(BF16), preferred_element_type=F32)
            ya = ug[rows] * mixed
            out_ref[rows, :] = (ya * _rstd(ya) * ga_ref[...]).astype(BF16)

    return pl.pallas_call(
        body, name="sgu_fwd", grid=(s_len // tm,),
        in_specs=[_rows(tm, 2 * WIDTH_A), _whole((4, CHUNK, CHUNK)), _whole((CHUNK, WIDTH_A)), _whole((1, WIDTH_A)), _whole((1, WIDTH_A))],
        out_specs=_rows(tm, WIDTH_A),
        out_shape=jax.ShapeDtypeStruct((s_len, WIDTH_A), BF16),
        compiler_params=_params("parallel"),
    )(ua, sw, b2, gs, ga)


def _attn_geometry(sd):
    tile = min(ROW_TILE, sd)
    return tile, tile // CHUNK, sd // tile


PAIRS_PER_STEP = 6


def _attn_spec(cb0, rows, row_index):
    return pl.BlockSpec((None, PAIRS_PER_STEP, rows, 128), lambda r, g, n: (r, cb0 // PAIRS_PER_STEP + g, row_index(n), 0))


assert PAIRS_PER_STEP == N_PAIRS
assert DILATIONS[0] == 1


def _stats_spec(rows, row_index):
    return pl.BlockSpec((None, None, rows, 128), lambda r, g, n: (r, 0, row_index(n), 0))


def _stats_shape(sd, d):
    return jax.ShapeDtypeStruct((d, 1, sd, 128), F32)


def _both_heads(x, head_a):
    zero = jnp.zeros_like(x)
    return [jnp.where(head_a, x, zero), jnp.where(head_a, zero, x)]


def _attn_fwd(qkv, d, exchange=()):
    sd = qkv.shape[2]
    tile, nb, n_tiles = _attn_geometry(sd)

    def prev(n):
        return jnp.maximum(n * nb - 1, 0)

    def body(q_ref, k_ref, kp_ref, v_ref, vp_ref, o_ref, l_ref):
        for hp in range(PAIRS_PER_STEP):
            one_pair(hp, q_ref.at[hp], k_ref.at[hp], kp_ref.at[hp], v_ref.at[hp], vp_ref.at[hp], o_ref.at[hp], l_ref)

    def one_pair(hp, q_ref, k_ref, kp_ref, v_ref, vp_ref, o_ref, l_ref):
        n = pl.program_id(2)
        lane = lax.broadcasted_iota(jnp.int32, (CHUNK, 128), 1)
        head_a = lane < HEAD_DIM
        qi = lax.broadcasted_iota(jnp.int32, (2 * CHUNK, 2 * CHUNK), 0) % CHUNK
        kc = lax.broadcasted_iota(jnp.int32, (2 * CHUNK, 2 * CHUNK), 1)
        band = (kc >= qi) & (kc <= qi + N_BACK)
        for j in range(nb):
            rows = slice(j * CHUNK, (j + 1) * CHUNK)
            if j == 0:
                kcat = jnp.concatenate([kp_ref[...], k_ref[rows, :]], axis=0)
                vcat = jnp.concatenate([vp_ref[...], v_ref[rows, :]], axis=0)
                valid = band & jnp.logical_or(n > 0, kc >= CHUNK)
            else:
                kcat = k_ref[(j - 1) * CHUNK:(j + 1) * CHUNK, :]
                vcat = v_ref[(j - 1) * CHUNK:(j + 1) * CHUNK, :]
                valid = band
            q2 = jnp.concatenate(_both_heads(q_ref[rows, :], head_a), axis=0)
            s = lax.dot_general(q2, kcat, _NT, preferred_element_type=F32)
            s = jnp.where(valid, s, NEG)
            m = jnp.max(s, axis=-1, keepdims=True)
            p = jnp.exp(s - m)
            l = jnp.sum(p, axis=-1, keepdims=True)
            o2 = jnp.dot(p.astype(BF16), vcat, preferred_element_type=F32) / l
            lse2 = m + jnp.log(l)
            o_ref[rows, :] = jnp.where(head_a, o2[:CHUNK], o2[CHUNK:]).astype(BF16)
            others = l_ref[rows, :] if hp > 0 else jnp.zeros((CHUNK, 128), F32)
            l_ref[rows, :] = jnp.where(lane == 2 * hp, lse2[:CHUNK], jnp.where(lane == 2 * hp + 1, lse2[CHUNK:], others))

    same = lambda n: n
    return _call(
        body, name=f"attn_fwd_d{d}", grid=(d, N_PAIRS // PAIRS_PER_STEP, n_tiles),
        in_specs=[_attn_spec(0, tile, same), _attn_spec(N_PAIRS, tile, same), _attn_spec(N_PAIRS, CHUNK, prev),
                  _attn_spec(2 * N_PAIRS, tile, same), _attn_spec(2 * N_PAIRS, CHUNK, prev)],
        out_specs=[_attn_spec(0, tile, same), _stats_spec(tile, same)],
        out_shape=[jax.ShapeDtypeStruct((d, N_PAIRS, sd, 128), BF16), _stats_shape(sd, d)],
        semantics=("parallel", "parallel", "parallel"), args=(qkv, qkv, qkv, qkv, qkv), exchange=exchange)


def _combine(outs, lses, ya_n, gb, head_spread, exchange=()):
    s_len = ya_n.shape[0]
    tm = ROW_TILE
    n_br = len(DILATIONS)

    def body(*refs):
        o_refs, l_refs = refs[:n_br], refs[n_br:2 * n_br]
        ya_ref, gb_ref, spread_ref, y_ref, yb_ref = refs[2 * n_br:2 * n_br + 5]
        lse_refs = refs[2 * n_br + 5:3 * n_br + 5]
        o_nat, l_nat, lse_nat, w_wide = refs[3 * n_br + 5:]
        for i, d in enumerate(DILATIONS):
            _from_sub(l_refs[i], 0, l_nat, i, d, tm)
        ls = [l_nat[i] for i in range(n_br)]
        top = jnp.maximum(jnp.maximum(ls[0], ls[1]), ls[2])
        ws = [jnp.exp(l - top) for l in ls]
        den = ws[0] + ws[1] + ws[2]
        inv = 1.0 / den
        for i in range(n_br):
            w = ws[i] * inv
            hi = w.astype(BF16)
            lo = (w - hi.astype(F32)).astype(BF16)
            w_wide[i] = (jnp.dot(hi, spread_ref[...], preferred_element_type=F32)
                         + jnp.dot(lo, spread_ref[...], preferred_element_type=F32))
        lse_nat[0] = top + jnp.log(den)
        for d, lse_ref in zip(DILATIONS, lse_refs):
            _to_sub(lse_nat, 0, lse_ref, 0, d, tm)
        sumsq = jnp.zeros((tm, 1), F32)
        for cb in range(N_PAIRS):
            cols = slice(cb * 128, (cb + 1) * 128)
            yb = w_wide[0, :, cols] * o_refs[0][0, cb].astype(F32)
            for i, d in enumerate(DILATIONS[1:], start=1):
                _from_sub(o_refs[i], cb, o_nat, i, d, tm)
                yb = yb + w_wide[i, :, cols] * o_nat[i]
            yb_ref[:, cb * 128:(cb + 1) * 128] = yb
            sumsq = sumsq + jnp.sum(yb * yb, axis=-1, keepdims=True)
        r = lax.rsqrt(sumsq / WIDTH_B + EPS)
        y_ref[:, :WIDTH_A] = ya_ref[...]
        y_ref[:, WIDTH_A:] = (yb_ref[...] * r * gb_ref[...]).astype(BF16)

    stats = [_sub_spec(d, 1, tm) for d in DILATIONS]
    return _call(
        body, name="attn_combine", grid=(s_len // tm,),
        in_specs=[_sub_spec(d, N_PAIRS, tm) for d in DILATIONS] + stats
        + [_rows(tm, WIDTH_A), _whole((1, WIDTH_B)), _whole((128, WIDTH_B))],
        out_specs=[_rows(tm, D_MODEL), _rows(tm, WIDTH_B)] + stats,
        out_shape=[jax.ShapeDtypeStruct((s_len, D_MODEL), BF16), jax.ShapeDtypeStruct((s_len, WIDTH_B), F32)]
        + [_sub_shape(s_len, d, 1, F32) for d in DILATIONS],
        scratch_shapes=[pltpu.VMEM((n_br, tm, 128), F32), pltpu.VMEM((n_br, tm, 128), F32), pltpu.VMEM((1, tm, 128), F32),
                        pltpu.VMEM((n_br, tm, WIDTH_B), F32)],
        semantics=("parallel",), args=(*outs, *lses, ya_n, gb, head_spread), exchange=exchange)


def _ffn_up(y, wout, x, g, wg, wu, exchange=()):
    s_len = x.shape[0]
    tm = ROW_TILE // 2

    def body(y_ref, wo_ref, x_ref, g_ref, wg_ref, wu_ref, h_ref, a_ref, b_ref, act_ref, hn_ref):
        hf = x_ref[...] + jnp.dot(y_ref[...], wo_ref[...], preferred_element_type=F32)
        h_ref[...] = hf
        hn = (hf * _rstd(hf) * g_ref[...]).astype(BF16)
        hn_ref[...] = hn
        for j in range(D_FF // 256):
            cols = slice(j * 256, (j + 1) * 256)
            a = lax.dot_general(hn, wg_ref[cols, :], _NT, preferred_element_type=F32)
            b = lax.dot_general(hn, wu_ref[cols, :], _NT, preferred_element_type=F32)
            a_ref[:, cols] = a.astype(BF16)
            b_ref[:, cols] = b.astype(BF16)
            act_ref[:, cols] = (a * jax.nn.sigmoid(a) * b).astype(BF16)

    wide = jax.ShapeDtypeStruct((s_len, D_FF), BF16)
    return _call(
        body, name="ffn_up", grid=(s_len // tm,),
        in_specs=[_rows(tm, D_MODEL), _whole((D_MODEL, D_MODEL)), _rows(tm, D_MODEL), _whole((1, D_MODEL)),
                  _whole((D_FF, D_MODEL)), _whole((D_FF, D_MODEL))],
        out_specs=[_rows(tm, D_MODEL), _rows(tm, D_FF), _rows(tm, D_FF), _rows(tm, D_FF), _rows(tm, D_MODEL)],
        out_shape=[jax.ShapeDtypeStruct((s_len, D_MODEL), F32), wide, wide, wide, jax.ShapeDtypeStruct((s_len, D_MODEL), BF16)],
        semantics=("parallel",), args=(y, wout, x, g, wg, wu), exchange=exchange)


def _ffn_down_ple(act, wd, h1, g, wpg, p, wpp):
    s_len = h1.shape[0]
    tm = ROW_TILE // 2

    def body(act_ref, wd_ref, h1_ref, g_ref, wpg_ref, p_ref, wpp_ref, h2_ref, h3_ref, gate_ref, pp_ref, hn_ref):
        hf = h1_ref[...] + jnp.dot(act_ref[...], wd_ref[...], preferred_element_type=F32)
        h2_ref[...] = hf
        hn = (hf * _rstd(hf) * g_ref[...]).astype(BF16)
        hn_ref[...] = hn
        gate = jax.nn.sigmoid(jnp.dot(hn, wpg_ref[...], preferred_element_type=F32))
        pp = lax.dot_general(p_ref[...].astype(BF16), wpp_ref[...], _NT, preferred_element_type=F32)
        h3_ref[...] = hf + gate * pp
        gate_ref[...] = gate.astype(BF16)
        pp_ref[...] = pp.astype(BF16)

    full = jax.ShapeDtypeStruct((s_len, D_MODEL), F32)
    half = jax.ShapeDtypeStruct((s_len, D_MODEL), BF16)
    return pl.pallas_call(
        body, name="ffn_down_ple", grid=(s_len // tm,),
        in_specs=[_rows(tm, D_FF), _whole((D_FF, D_MODEL)), _rows(tm, D_MODEL), _whole((1, D_MODEL)),
                  _whole((D_MODEL, D_MODEL)), _rows(tm, PLE_DIM), _whole((D_MODEL, PLE_DIM))],
        out_specs=[_rows(tm, D_MODEL)] * 5,
        out_shape=[full, full, half, half, half],
        compiler_params=_params("parallel"),
    )(act, wd, h1, g, wpg, p, wpp)


def _loss_ple_bwd(h3, target, gf, gate, pp, h2, g_ple, wpg, hn3, p):
    s_len = h3.shape[0]
    tm = ROW_TILE
    n_steps = s_len // tm

    def body(h_ref, t_ref, g_ref, gate_ref, pp_ref, h2_ref, gp_ref, w_ref, hn_ref, p_ref,
             dh2_ref, loss_ref, dg_ref, dgp_ref, dwg_ref, dwp_ref, acc_g, acc_p):
        step = pl.program_id(0)

        @pl.when(step == 0)
        def _():
            loss_ref[...] = jnp.zeros_like(loss_ref)
            dg_ref[...] = jnp.zeros_like(dg_ref)
            dgp_ref[...] = jnp.zeros_like(dgp_ref)
            acc_g[...] = jnp.zeros_like(acc_g)
            acc_p[...] = jnp.zeros_like(acc_p)

        hf = h_ref[...]
        gfv = g_ref[...]
        err = hf * _rstd(hf) * gfv - t_ref[...]
        loss_ref[...] += 0.5 * jnp.sum(jnp.sum(err * err, axis=-1, keepdims=True), axis=0, keepdims=True) / D_MODEL
        dh, dg_rows = _norm_bwd(err / D_MODEL, hf, gfv)
        dg_ref[...] += jnp.sum(dg_rows, axis=0, keepdims=True)
        gate = gate_ref[...].astype(F32)
        dz = (dh * pp_ref[...].astype(F32) * gate * (1.0 - gate)).astype(BF16)
        dpp = (dh * gate).astype(BF16)
        dn = lax.dot_general(dz, w_ref[...], _NT, preferred_element_type=F32)
        dh2, dgp_rows = _norm_bwd(dn, h2_ref[...], gp_ref[...])
        dh2 = dh + dh2
        dh2_ref[...] = dh2
        dgp_ref[...] += jnp.sum(dgp_rows, axis=0, keepdims=True)
        acc_g[...] += lax.dot_general(hn_ref[...], dz, _TN, preferred_element_type=F32)
        acc_p[...] += lax.dot_general(dpp, p_ref[...].astype(BF16), _TN, preferred_element_type=F32)

        @pl.when(step == n_steps - 1)
        def _():
            dwg_ref[...] = acc_g[...].astype(BF16)
            dwp_ref[...] = acc_p[...].astype(BF16)

    gain = jax.ShapeDtypeStruct((1, D_MODEL), F32)
    return pl.pallas_call(
        body, name="loss_ple_bwd", grid=(n_steps,),
        in_specs=[_rows(tm, D_MODEL), _rows(tm, D_MODEL), _whole((1, D_MODEL)), _rows(tm, D_MODEL), _rows(tm, D_MODEL),
                  _rows(tm, D_MODEL), _whole((1, D_MODEL)), _whole((D_MODEL, D_MODEL)), _rows(tm, D_MODEL),
                  _rows(tm, PLE_DIM)],
        out_specs=[_rows(tm, D_MODEL), _whole((1, 128)), _whole((1, D_MODEL)), _whole((1, D_MODEL)),
                   _whole((D_MODEL, D_MODEL)), _whole((D_MODEL, PLE_DIM))],
        out_shape=[jax.ShapeDtypeStruct((s_len, D_MODEL), F32), jax.ShapeDtypeStruct((1, 128), F32), gain, gain,
                   jax.ShapeDtypeStruct((D_MODEL, D_MODEL), BF16), jax.ShapeDtypeStruct((D_MODEL, PLE_DIM), BF16)],
        scratch_shapes=[pltpu.VMEM((D_MODEL, D_MODEL), F32), pltpu.VMEM((D_MODEL, PLE_DIM), F32)],
        compiler_params=_params("arbitrary"),
    )(h3, target, gf, gate, pp, h2, g_ple, wpg, hn3, p)


def _mm_norm_bwd(parts, h, g, dres, name, exchange=(), dw_lhs=None):
    s_len = h.shape[0]
    tm = ROW_TILE // 2
    n_parts = len(parts)
    n_steps = s_len // tm
    has_dw = dw_lhs is not None

    def body(*refs):
        a_refs = refs[0:2 * n_parts:2]
        w_refs = refs[1:2 * n_parts:2]
        h_ref, g_ref, r_ref = refs[2 * n_parts:2 * n_parts + 3]
        rest = refs[2 * n_parts + 3:]
        step = pl.program_id(0)
        if has_dw:
            lhs_ref, o_ref, dg_ref, dw_ref, acc_ref = rest
        else:
            o_ref, dg_ref = rest

        @pl.when(step == 0)
        def _():
            dg_ref[...] = jnp.zeros_like(dg_ref)
            if has_dw:
                acc_ref[...] = jnp.zeros_like(acc_ref)

        dn = jnp.dot(a_refs[0][...], w_refs[0][...], preferred_element_type=F32)
        for a_ref, w_ref in zip(a_refs[1:], w_refs[1:]):
            dn = dn + jnp.dot(a_ref[...], w_ref[...], preferred_element_type=F32)
        dh, dg_rows = _norm_bwd(dn, h_ref[...], g_ref[...])
        out = r_ref[...] + dh
        o_ref[...] = out
        dg_ref[...] += jnp.sum(dg_rows, axis=0, keepdims=True)
        if has_dw:
            acc_ref[...] += lax.dot_general(lhs_ref[...], out.astype(BF16), _TN, preferred_element_type=F32)

            @pl.when(step == n_steps - 1)
            def _():
                dw_ref[...] = acc_ref[...].astype(BF16)

    in_specs, args = [], []
    for a, w in parts:
        in_specs += [_rows(tm, a.shape[1]), _whole(w.shape)]
        args += [a, w]
    in_specs += [_rows(tm, D_MODEL), _whole((1, D_MODEL)), _rows(tm, D_MODEL)]
    args += [h, g, dres]
    out_specs = [_rows(tm, D_MODEL), _whole((1, D_MODEL))]
    out_shape = [jax.ShapeDtypeStruct((s_len, D_MODEL), F32), jax.ShapeDtypeStruct((1, D_MODEL), F32)]
    scratch = []
    if has_dw:
        m = dw_lhs.shape[1]
        in_specs.append(_rows(tm, m))
        args.append(dw_lhs)
        out_specs.append(_whole((m, D_MODEL)))
        out_shape.append(jax.ShapeDtypeStruct((m, D_MODEL), BF16))
        scratch.append(pltpu.VMEM((m, D_MODEL), F32))
    return _call(
        body, name=name, grid=(n_steps,), in_specs=in_specs, out_specs=out_specs, out_shape=out_shape,
        scratch_shapes=scratch, semantics=("arbitrary",), args=tuple(args), exchange=exchange)


def _ffn_down_bwd(dh, wdt, a, b, exchange=()):
    s_len = dh.shape[0]
    tm = ROW_TILE // 2

    def body(dh_ref, w_ref, a_ref, b_ref, da_ref, db_ref):
        dhb = dh_ref[...].astype(BF16)
        for j in range(D_FF // 256):
            cols = slice(j * 256, (j + 1) * 256)
            dact = lax.dot_general(dhb, w_ref[cols, :], _NT, preferred_element_type=F32)
            av = a_ref[:, cols].astype(F32)
            bv = b_ref[:, cols].astype(F32)
            sig = jax.nn.sigmoid(av)
            t = dact * sig
            silu = av * sig
            da_ref[:, cols] = (t * bv * (1.0 + av - silu)).astype(BF16)
            db_ref[:, cols] = (dact * silu).astype(BF16)

    wide = jax.ShapeDtypeStruct((s_len, D_FF), BF16)
    return _call(
        body, name="ffn_down_bwd", grid=(s_len // tm,),
        in_specs=[_rows(tm, D_MODEL), _whole((D_FF, D_MODEL)), _rows(tm, D_FF), _rows(tm, D_FF)],
        out_specs=[_rows(tm, D_FF), _rows(tm, D_FF)],
        out_shape=[wide, wide],
        semantics=("parallel",), args=(dh, wdt, a, b), exchange=exchange)


def _outproj_bwd(dh1, woutt, yb, gb, head_sum, exchange=()):
    s_len = dh1.shape[0]
    tm = ROW_TILE
    n_br = len(DILATIONS)

    def body(dh_ref, w_ref, yb_ref, gb_ref, e_ref, dya_ref, dgb_ref, *rest):
        do_refs, dd_refs = rest[:n_br], rest[n_br:2 * n_br]
        do_nat, dd_nat = rest[2 * n_br:]

        @pl.when(pl.program_id(0) == 0)
        def _():
            dgb_ref[...] = jnp.zeros_like(dgb_ref)

        dhb = dh_ref[...].astype(BF16)
        dya_ref[...] = lax.dot_general(dhb, w_ref[:WIDTH_A, :], _NT, preferred_element_type=F32)
        dyn = lax.dot_general(dhb, w_ref[WIDTH_A:, :], _NT, preferred_element_type=F32)
        ybv = yb_ref[...]
        dyb, dg_rows = _norm_bwd(dyn, ybv, gb_ref[...])
        dgb_ref[...] += jnp.sum(dg_rows, axis=0, keepdims=True)
        prod = dyb * ybv
        hi = prod.astype(BF16)
        lo = (prod - hi.astype(F32)).astype(BF16)
        dd_nat[0] = (jnp.dot(hi, e_ref[...], preferred_element_type=F32)
                     + jnp.dot(lo, e_ref[...], preferred_element_type=F32))
        for i, d in enumerate(DILATIONS):
            _to_sub(dd_nat, 0, dd_refs[i], 0, d, tm)
        for cb in range(N_PAIRS):
            piece = dyb[:, cb * 128:(cb + 1) * 128]
            do_nat[cb] = piece
            do_refs[0][0, cb] = piece.astype(BF16)
            for i, d in enumerate(DILATIONS[1:], start=1):
                _to_sub(do_nat, cb, do_refs[i], cb, d, tm)

    return _call(
        body, name="outproj_bwd", grid=(s_len // tm,),
        in_specs=[_rows(tm, D_MODEL), _whole((D_MODEL, D_MODEL)), _rows(tm, WIDTH_B), _whole((1, WIDTH_B)), _whole((WIDTH_B, 128))],
        out_specs=[_rows(tm, WIDTH_A), _whole((1, WIDTH_B))] + [_sub_spec(d, N_PAIRS, tm) for d in DILATIONS]
        + [_sub_spec(d, 1, tm) for d in DILATIONS],
        out_shape=[jax.ShapeDtypeStruct((s_len, WIDTH_A), F32), jax.ShapeDtypeStruct((1, WIDTH_B), F32)]
        + [_sub_shape(s_len, d, N_PAIRS, BF16) for d in DILATIONS] + [_sub_shape(s_len, d, 1, F32) for d in DILATIONS],
        scratch_shapes=[pltpu.VMEM((N_PAIRS, tm, 128), F32), pltpu.VMEM((1, tm, 128), F32)],
        semantics=("arbitrary",), args=(dh1, woutt, yb, gb, head_sum), exchange=exchange)


def _attn_bwd(qkv, do, lse, dd, d, exchange=()):
    sd = qkv.shape[2]
    tile, nb, n_tiles = _attn_geometry(sd)
    last_block = sd // CHUNK - 1

    def nxt(n):
        return jnp.minimum((n + 1) * nb, last_block)

    def block(ref, next_ref, j):
        return ref[j * CHUNK:(j + 1) * CHUNK, :] if j < nb else next_ref[...]

    def body(q_ref, qn_ref, k_ref, v_ref, do_ref, don_ref, l_ref, ln_ref, dd_ref, ddn_ref,
             dq_ref, dk_ref, dv_ref, carry_ref):
        l_t = [block(l_ref, ln_ref, j).T for j in range(nb + 1)]
        dd_t = [block(dd_ref, ddn_ref, j).T for j in range(nb + 1)]
        for hp in range(PAIRS_PER_STEP):
            l_rows = [jnp.concatenate([t[2 * hp:2 * hp + 1, :], t[2 * hp + 1:2 * hp + 2, :]], axis=1) for t in l_t]
            dd_rows = [jnp.concatenate([t[2 * hp:2 * hp + 1, :], t[2 * hp + 1:2 * hp + 2, :]], axis=1) for t in dd_t]
            one_pair(q_ref.at[hp], qn_ref.at[hp], k_ref.at[hp], v_ref.at[hp], do_ref.at[hp], don_ref.at[hp],
                     l_rows, dd_rows, dq_ref.at[hp], dk_ref.at[hp], dv_ref.at[hp], carry_ref.at[hp])

    def one_pair(q_ref, qn_ref, k_ref, v_ref, do_ref, don_ref, l_rows, dd_rows, dq_ref, dk_ref, dv_ref, carry_ref):
        n = pl.program_id(2)

        @pl.when(n == 0)
        def _():
            carry_ref[...] = jnp.zeros_like(carry_ref)

        head_a = lax.broadcasted_iota(jnp.int32, (CHUNK, 128), 1) < HEAD_DIM
        col = lax.broadcasted_iota(jnp.int32, (CHUNK, 4 * CHUNK), 1)
        qi = col % CHUNK
        ki = lax.broadcasted_iota(jnp.int32, (CHUNK, 4 * CHUNK), 0)
        is_after = col >= 2 * CHUNK
        mask = (is_after & (ki >= qi)) | (jnp.logical_not(is_after) & (qi >= ki))
        mask_last = mask & jnp.logical_or(jnp.logical_not(is_after), n < n_tiles - 1)
        dq_acc = [carry_ref[...]] + [jnp.zeros((CHUNK, 128), F32) for _ in range(nb)]

        q_st = [jnp.concatenate(_both_heads(block(q_ref, qn_ref, j), head_a), axis=0) for j in range(nb + 1)]
        do_st = [jnp.concatenate(_both_heads(block(do_ref, don_ref, j), head_a), axis=0) for j in range(nb + 1)]

        for j in range(nb):
            rows = slice(j * CHUNK, (j + 1) * CHUNK)
            kj = k_ref[rows, :]
            vj = v_ref[rows, :]
            msk = mask if j + 1 < nb else mask_last
            qs = jnp.concatenate([q_st[j], q_st[j + 1]], axis=0)
            dos = jnp.concatenate([do_st[j], do_st[j + 1]], axis=0)
            ls = jnp.concatenate([l_rows[j], l_rows[j + 1]], axis=1)
            dds = jnp.concatenate([dd_rows[j], dd_rows[j + 1]], axis=1)
            st = lax.dot_general(kj, qs, _NT, preferred_element_type=F32)
            pt = jnp.exp(jnp.where(msk, st - ls, NEG))
            dpt = lax.dot_general(vj, dos, _NT, preferred_element_type=F32)
            dst = (pt * (dpt - dds)).astype(BF16)
            dv_ref[rows, :] = jnp.dot(pt.astype(BF16), dos, preferred_element_type=F32).astype(BF16)
            dk_ref[rows, :] = jnp.dot(dst, qs, preferred_element_type=F32).astype(BF16)
            dqs = lax.dot_general(dst, kj, _TN, preferred_element_type=F32)
            dq_acc[j] = dq_acc[j] + jnp.where(head_a, dqs[:CHUNK], dqs[CHUNK:2 * CHUNK])
            dq_acc[j + 1] = dq_acc[j + 1] + jnp.where(head_a, dqs[2 * CHUNK:3 * CHUNK], dqs[3 * CHUNK:])
        for j in range(nb):
            dq_ref[j * CHUNK:(j + 1) * CHUNK, :] = dq_acc[j].astype(BF16)
        carry_ref[...] = dq_acc[nb]

    same = lambda n: n
    grad = jax.ShapeDtypeStruct((d, N_PAIRS, sd, 128), BF16)
    return _call(
        body, name=f"attn_bwd_d{d}", grid=(d, N_PAIRS // PAIRS_PER_STEP, n_tiles),
        in_specs=[_attn_spec(0, tile, same), _attn_spec(0, CHUNK, nxt), _attn_spec(N_PAIRS, tile, same),
                  _attn_spec(2 * N_PAIRS, tile, same), _attn_spec(0, tile, same), _attn_spec(0, CHUNK, nxt),
                  _stats_spec(tile, same), _stats_spec(CHUNK, nxt), _stats_spec(tile, same), _stats_spec(CHUNK, nxt)],
        out_specs=[_attn_spec(0, tile, same)] * 3,
        out_shape=[grad, grad, grad],
        scratch_shapes=[pltpu.VMEM((PAIRS_PER_STEP, CHUNK, 128), F32)],
        semantics=("parallel", "parallel", "arbitrary"), args=(qkv, qkv, qkv, qkv, do, do, lse, lse, dd, dd), exchange=exchange)


def _sgu_bwd(ua, sw, b2, gs, ga, dya_n):
    s_len = ua.shape[0]
    tm = ROW_TILE

    def body(ua_ref, sw_ref, b2_ref, gs_ref, ga_ref, dy_ref, dua_ref, dsw_ref, db2_ref, dgs_ref, dga_ref):
        @pl.when(pl.program_id(0) == 0)
        def _():
            dsw_ref[...] = jnp.zeros_like(dsw_ref)
            db2_ref[...] = jnp.zeros_like(db2_ref)
            dgs_ref[...] = jnp.zeros_like(dgs_ref)
            dga_ref[...] = jnp.zeros_like(dga_ref)

        u, va, ug, xhat, rstd, vn = _sgu_core(ua_ref, gs_ref)
        wm, keep = _sgu_mix_weights(sw_ref)
        head = lax.broadcasted_iota(jnp.int32, (CHUNK, WIDTH_A), 1) // HEAD_DIM
        gav = ga_ref[...]
        gsv = gs_ref[...]
        dga = jnp.zeros((1, WIDTH_A), F32)
        dgs = jnp.zeros((1, WIDTH_A), F32)
        db2 = jnp.zeros((CHUNK, WIDTH_A), F32)
        dsw = [jnp.zeros((CHUNK, CHUNK), F32) for _ in range(4)]
        for c in range(tm // CHUNK):
            rows = slice(c * CHUNK, (c + 1) * CHUNK)
            vnc = vn[rows]
            vnb = vnc.astype(BF16)
            mixed = b2_ref[...]
            for h in range(4):
                mixed = mixed + jnp.dot(wm[h], jnp.where(head == h, vnc, 0.0).astype(BF16), preferred_element_type=F32)
            ugc = ug[rows]
            dya, dga_rows = _norm_bwd(dy_ref[rows, :], ugc * mixed, gav)
            dga = dga + jnp.sum(dga_rows, axis=0, keepdims=True)
            dmixed = dya * ugc
            db2 = db2 + dmixed
            dvn = jnp.zeros((CHUNK, WIDTH_A), F32)
            for h in range(4):
                dmh = jnp.where(head == h, dmixed, 0.0).astype(BF16)
                dsw[h] = dsw[h] + lax.dot_general(dmh, vnb, _NT, preferred_element_type=F32)
                dvn = dvn + lax.dot_general(wm[h], dmh, _TN, preferred_element_type=F32)
            xh = xhat[rows]
            dgs = dgs + jnp.sum(dvn * xh, axis=0, keepdims=True)
            dxh = dvn * gsv
            dvg = rstd[rows] * (dxh - jnp.mean(dxh, axis=-1, keepdims=True) - xh * jnp.mean(dxh * xh, axis=-1, keepdims=True))
            dua_ref[rows, :WIDTH_A] = (dya * mixed * _gelu_grad(u[rows])).astype(BF16)
            dua_ref[rows, WIDTH_A:] = (dvg * _gelu_grad(va[rows])).astype(BF16)
        for h in range(4):
            dsw_ref[h] += jnp.where(keep, dsw[h], 0.0)
        db2_ref[...] += db2
        dgs_ref[...] += dgs
        dga_ref[...] += dga

    return pl.pallas_call(
        body, name="sgu_bwd", grid=(s_len // tm,),
        in_specs=[_rows(tm, 2 * WIDTH_A), _whole((4, CHUNK, CHUNK)), _whole((CHUNK, WIDTH_A)), _whole((1, WIDTH_A)),
                  _whole((1, WIDTH_A)), _rows(tm, WIDTH_A)],
        out_specs=[_rows(tm, 2 * WIDTH_A), _whole((4, CHUNK, CHUNK)), _whole((CHUNK, WIDTH_A)), _whole((1, WIDTH_A)), _whole((1, WIDTH_A))],
        out_shape=[jax.ShapeDtypeStruct((s_len, 2 * WIDTH_A), BF16), jax.ShapeDtypeStruct((4, CHUNK, CHUNK), F32),
                   jax.ShapeDtypeStruct((CHUNK, WIDTH_A), F32), jax.ShapeDtypeStruct((1, WIDTH_A), F32),
                   jax.ShapeDtypeStruct((1, WIDTH_A), F32)],
        compiler_params=_params("arbitrary"),
    )(ua, sw, b2, gs, ga, dya_n)


def _dproj(dua, dqs, dks, dvs, cos, sin, hn1, exchange=()):
    s_len = dua.shape[0]
    tm = ROW_TILE
    n_br = len(DILATIONS)
    n_steps = s_len // tm

    def body(dua_ref, *rest):
        groups = [rest[g * n_br:(g + 1) * n_br] for g in range(3)]
        cos_ref, sin_ref, hn_ref, out_ref, dw_ref, acc, dw_acc = rest[3 * n_br:]
        step = pl.program_id(0)

        @pl.when(step == 0)
        def _():
            dw_acc[...] = jnp.zeros_like(dw_acc)

        out_ref[:, :2 * WIDTH_A] = dua_ref[...]
        c = cos_ref[...]
        s = sin_ref[...]
        first_half = (lax.broadcasted_iota(jnp.int32, (tm, 128), 1) % HEAD_DIM) < HEAD_DIM // 2
        for g, refs in enumerate(groups):
            for cb in range(N_PAIRS):
                t = refs[0][0, cb].astype(F32)
                for i, d in enumerate(DILATIONS[1:]):
                    _from_sub(refs[i + 1], cb, acc, i, d, tm)
                    t = t + acc[i]
                if g < 2:
                    t = (t * c - _swap_halves(t, first_half) * s) * (0.125 if g == 0 else 1.0)
                col = 2 * WIDTH_A + g * WIDTH_B + cb * 128
                out_ref[:, col:col + 128] = t.astype(BF16)
        hn = hn_ref[...]
        for j in range(IN_COLS // 256):
            cols = slice(j * 256, (j + 1) * 256)
            dw_acc[cols, :] += lax.dot_general(out_ref[:, cols], hn, _TN, preferred_element_type=F32)

        @pl.when(step == n_steps - 1)
        def _():
            dw_ref[...] = dw_acc[...].astype(BF16)

    subs = [_sub_spec(d, N_PAIRS, tm) for d in DILATIONS]
    (dproj, dw), received = _call(
        body, name="dproj_dw_in", grid=(n_steps,),
        in_specs=[_rows(tm, 2 * WIDTH_A)] + subs * 3 + [_rows(tm, 128), _rows(tm, 128), _rows(tm, D_MODEL)],
        out_specs=[_rows(tm, IN_COLS), _whole((IN_COLS, D_MODEL))],
        out_shape=[jax.ShapeDtypeStruct((s_len, IN_COLS), BF16), jax.ShapeDtypeStruct((IN_COLS, D_MODEL), BF16)],
        scratch_shapes=[pltpu.VMEM((n_br - 1, tm, 128), F32), pltpu.VMEM((IN_COLS, D_MODEL), F32)],
        semantics=("arbitrary",), args=(dua, *dqs, *dks, *dvs, cos, sin, hn1), exchange=exchange)
    return dproj, dw, received


def _mm_tn(a, b, name, exchange=()):
    s_len, m = a.shape
    n = b.shape[1]
    tk = 2 * ROW_TILE
    tm = m if m <= 512 else (1408 if m == D_FF else 512)
    n_k = s_len // tk

    def body(a_ref, b_ref, o_ref, acc_ref):
        k = pl.program_id(1)

        @pl.when(k == 0)
        def _():
            acc_ref[...] = jnp.zeros_like(acc_ref)

        acc_ref[...] += lax.dot_general(a_ref[...].astype(BF16), b_ref[...].astype(BF16), _TN, preferred_element_type=F32)

        @pl.when(k == n_k - 1)
        def _():
            o_ref[...] = acc_ref[...].astype(BF16)

    (grad,), received = _call(
        body, name=name, grid=(m // tm, n_k),
        in_specs=[pl.BlockSpec((tk, tm), lambda i, k: (k, i)), pl.BlockSpec((tk, n), lambda i, k: (k, 0))],
        out_specs=[pl.BlockSpec((tm, n), lambda i, k: (i, 0))],
        out_shape=[jax.ShapeDtypeStruct((m, n), BF16)],
        scratch_shapes=[pltpu.VMEM((tm, n), F32)],
        semantics=("parallel", "arbitrary"), args=(a, b), exchange=exchange)
    return grad, received


def _position():
    x, y, c = lax.axis_index("x"), lax.axis_index("y"), lax.axis_index("c")
    return x, y, c, 4 * x + 2 * y + c


def _peer(x, y, c, rel):
    return (x ^ ((rel >> 2) & 1), y ^ ((rel >> 1) & 1), c ^ (rel & 1))


def _exchange_out_shape(kind, arr):
    return jax.ShapeDtypeStruct(((N_DEV,) + arr.shape) if kind == "gather" else arr.shape, arr.dtype)


def _exchange_sems(n_items):
    return [pltpu.SemaphoreType.DMA((n_items, N_DEV)), pltpu.SemaphoreType.DMA((n_items, N_DEV)), pltpu.SemaphoreType.DMA((n_items,))]


def _exchange_copies(kinds, srcs, dsts, sems):
    send_sems, recv_sems, local_sems = sems
    x, y, c, me = _position()
    local, sends, recvs = [], [], []
    for k, (kind, src, dst) in enumerate(zip(kinds, srcs, dsts)):
        own = src if kind == "gather" else src.at[me]
        local.append(pltpu.make_async_copy(own, dst.at[me], local_sems.at[k]))
        for rel in range(1, N_DEV):
            going = src if kind == "gather" else src.at[me ^ rel]
            common = dict(send_sem=send_sems.at[k, rel], recv_sem=recv_sems.at[k, rel],
                          device_id=_peer(x, y, c, rel), device_id_type=MESH)
            sends.append(pltpu.make_async_remote_copy(src_ref=going, dst_ref=dst.at[me], **common))
            recvs.append(pltpu.make_async_remote_copy(src_ref=own, dst_ref=dst.at[me ^ rel], **common))
    return local, sends, recvs


def _exchange_start(kinds, srcs, dsts, sems):
    local, sends, _ = _exchange_copies(kinds, srcs, dsts, sems)
    for cp in local + sends:
        cp.start()


def _exchange_finish(kinds, srcs, dsts, sems):
    local, sends, recvs = _exchange_copies(kinds, srcs, dsts, sems)
    for cp in recvs:
        cp.wait_recv()
    for cp in sends:
        cp.wait_send()
    for cp in local:
        cp.wait()


def _exchange_only(items, name):
    kinds = [k for k, _ in items]
    n = len(items)

    def body(*refs):
        srcs, dsts, sems = refs[:n], refs[n:2 * n], refs[2 * n:]
        _exchange_start(kinds, srcs, dsts, sems)
        _exchange_finish(kinds, srcs, dsts, sems)

    any_spec = pl.BlockSpec(memory_space=pl.ANY)
    return pl.pallas_call(
        body, name=name, in_specs=[any_spec] * n, out_specs=[any_spec] * n,
        out_shape=[_exchange_out_shape(k, a) for k, a in items],
        scratch_shapes=_exchange_sems(n),
        compiler_params=pltpu.CompilerParams(has_side_effects=True),
    )(*[a for _, a in items])


def _gather_two_level_with_rope_tables(shard, inv_freq, s_len, name):
    rows = ROW_TILE

    def body(inv_ref, src, cos_ref, sin_ref, dst, send_sems, recv_sems, local_sem):
        x, y, c, me = _position()
        sibling = (x, y, 1 - c)
        chips = [(1 - x, y), (x, 1 - y), (1 - x, 1 - y)]

        def block(px, py, pc):
            return dst.at[4 * px + 2 * py + pc]

        def copy(k, blk, to, src_ref=None):
            return pltpu.make_async_remote_copy(
                src_ref=block(*blk) if src_ref is None else src_ref, dst_ref=block(*blk),
                send_sem=send_sems.at[k], recv_sem=recv_sems.at[k], device_id=to, device_id_type=MESH)

        mine = pltpu.make_async_copy(src, dst.at[me], local_sem)
        mine.start()
        first = [copy(0, (x, y, c), sibling, src)] + [copy(1 + j, (x, y, c), (*chip, c), src) for j, chip in enumerate(chips)]
        for cp in first:
            cp.start()

        inv = inv_ref[...]
        lane = lax.broadcasted_iota(jnp.int32, (rows, 128), 1)
        sign = jnp.where((lane // (HEAD_DIM // 2)) % 2 == 0, -1.0, 1.0)
        row = lax.broadcasted_iota(jnp.int32, (rows, 128), 0)

        @pl.loop(0, s_len // rows)
        def _(i):
            at = pl.multiple_of(i * rows, rows)
            ang = (row + at).astype(F32) * inv
            cos_ref[pl.ds(at, rows), :] = jnp.cos(ang)
            sin_ref[pl.ds(at, rows), :] = jnp.sin(ang) * sign

        passed = [copy(4 + j, (*chip, c), sibling) for j, chip in enumerate(chips)]
        for j, chip in enumerate(chips):
            copy(1 + j, (*chip, c), (x, y, c)).wait_recv()
            passed[j].start()
        copy(0, (x, y, 1 - c), (x, y, c)).wait_recv()
        for j, chip in enumerate(chips):
            copy(4 + j, (*chip, 1 - c), (x, y, c)).wait_recv()
        for cp in first + passed:
            cp.wait_send()
        mine.wait()

    any_spec = pl.BlockSpec(memory_space=pl.ANY)
    vmem = pl.BlockSpec(memory_space=pltpu.VMEM)
    table = jax.ShapeDtypeStruct((s_len, 128), F32)
    return pl.pallas_call(
        body, name=name, in_specs=[vmem, any_spec], out_specs=[vmem, vmem, any_spec],
        out_shape=[table, table, _exchange_out_shape("gather", shard)],
        scratch_shapes=[pltpu.SemaphoreType.DMA((N_DEV - 1,)), pltpu.SemaphoreType.DMA((N_DEV - 1,)), pltpu.SemaphoreType.DMA],
        compiler_params=pltpu.CompilerParams(has_side_effects=True, vmem_limit_bytes=V7X_VMEM_LIMIT_BYTES),
    )(inv_freq, shard)


def _call(body, *, name, grid, in_specs, out_specs, out_shape, args, scratch_shapes=(), semantics, exchange=()):
    if not exchange:
        outs = pl.pallas_call(body, name=name, grid=grid, in_specs=in_specs, out_specs=out_specs, out_shape=out_shape,
                              scratch_shapes=list(scratch_shapes), compiler_params=_params(*semantics))(*args)
        return outs, []
    kinds = [k for k, _ in exchange]
    n_in, n_out, n_x, n_scr = len(in_specs), len(out_specs), len(exchange), len(scratch_shapes)

    def wrapped(*refs):
        ins, refs = refs[:n_in], refs[n_in:]
        srcs, refs = refs[:n_x], refs[n_x:]
        outs, refs = refs[:n_out], refs[n_out:]
        dsts, refs = refs[:n_x], refs[n_x:]
        scratch, sems = refs[:n_scr], refs[n_scr:]
        ids = [pl.program_id(a) for a in range(len(grid))]
        first = functools.reduce(jnp.logical_and, [i == 0 for i in ids])
        last = functools.reduce(jnp.logical_and, [i == g - 1 for i, g in zip(ids, grid)])

        @pl.when(first)
        def _():
            _exchange_start(kinds, srcs, dsts, sems)

        body(*ins, *outs, *scratch)

        @pl.when(last)
        def _():
            _exchange_finish(kinds, srcs, dsts, sems)

    any_spec = pl.BlockSpec(memory_space=pl.ANY)
    outs = pl.pallas_call(
        wrapped, name=name, grid=grid,
        in_specs=list(in_specs) + [any_spec] * n_x, out_specs=list(out_specs) + [any_spec] * n_x,
        out_shape=list(out_shape) + [_exchange_out_shape(k, a) for k, a in exchange],
        scratch_shapes=list(scratch_shapes) + _exchange_sems(n_x),
        compiler_params=pltpu.CompilerParams(dimension_semantics=("arbitrary",) * len(grid),
                                             vmem_limit_bytes=V7X_VMEM_LIMIT_BYTES, has_side_effects=True),
    )(*args, *[a for _, a in exchange])
    return outs[:n_out], outs[n_out:]


def _adamw_math(w, g, m, v):
    m = ADAM_B1 * m + (1.0 - ADAM_B1) * g
    v = ADAM_B2 * v + (1.0 - ADAM_B2) * (g * g)
    m_hat = m / (1.0 - ADAM_B1 ** ADAM_STEP)
    v_hat = v / (1.0 - ADAM_B2 ** ADAM_STEP)
    return -ADAM_LR * (m_hat / (jnp.sqrt(v_hat) + ADAM_EPS) + ADAM_WD * w), m, v


def _adamw(parts, w, m, v, name):
    rows, cols = w.shape
    tm = 256 if rows % 256 == 0 and rows > 256 else rows

    def body(p_ref, w_ref, m_ref, v_ref, g_ref, d_ref, nm_ref, nv_ref):
        g = p_ref[0].astype(F32)
        for j in range(1, N_DEV):
            g = g + p_ref[j].astype(F32)
        delta, nm, nv = _adamw_math(w_ref[...], g, m_ref[...], v_ref[...])
        g_ref[...] = g
        d_ref[...] = delta
        nm_ref[...] = nm
        nv_ref[...] = nv

    shard = jax.ShapeDtypeStruct((rows, cols), F32)
    return pl.pallas_call(
        body, name=name, grid=(rows // tm,),
        in_specs=[pl.BlockSpec((N_DEV, tm, cols), lambda i: (0, i, 0))] + [_rows(tm, cols)] * 3,
        out_specs=[_rows(tm, cols)] * 4,
        out_shape=[shard] * 4,
        compiler_params=_params("parallel"),
    )(parts, w, m, v)


_SMALL = ("mix_norm_g", "sgu_w", "sgu_b", "sgu_norm_g", "out_norm_a", "out_norm_b", "ffn_norm_g", "ple_norm_g", "final_norm_g")
_BIG = ("w_in", "w_out", "w_gate", "w_up", "w_down", "w_ple_gate", "w_ple_proj")
_COLUMN_SHARDED = ("w_in", "w_gate", "w_up", "w_ple_proj")
_ORDER = ("mix_norm_g", "w_in", "sgu_w", "sgu_b", "sgu_norm_g", "out_norm_a", "out_norm_b", "w_out", "ffn_norm_g",
          "w_gate", "w_up", "w_down", "ple_norm_g", "w_ple_gate", "w_ple_proj", "final_norm_g")


def _pack_small(values, names=_SMALL):
    flat = jnp.concatenate([values[n].reshape(-1).astype(F32) for n in names])
    pad = (-flat.shape[0]) % (8 * 128)
    return jnp.pad(flat, (0, pad)).reshape(-1, 128)


def _unpack_small(packed, like):
    flat = packed.reshape(-1)
    out, at = {}, 0
    for n in _SMALL:
        size = like[n].size
        out[n] = flat[at:at + size].reshape(like[n].shape)
        at += size
    return out


def _own_orientation(name, value):
    return value[0].T if name in _COLUMN_SHARDED else value[0]


def _reference_orientation(name, value):
    return (value.T if name in _COLUMN_SHARDED else value)[None]


def _full_from_gathered(gathered):
    return gathered.reshape(N_DEV * gathered.shape[1], gathered.shape[2])


def _sliced_for_devices(grad):
    return grad.reshape(N_DEV, grad.shape[0] // N_DEV, grad.shape[1])


def _rope_inv_freq():
    half = HEAD_DIM // 2
    inv = ROPE_THETA ** (-jnp.arange(half, dtype=F32) / half)
    return jnp.tile(inv, 128 // half)[None, :]


def _forward_backward(x, p, target, small, shards):
    def gather(*names):
        return [("gather", shards[n]) for n in names]

    def scatter(**grads):
        return [("scatter", _sliced_for_devices(g)) for g in grads.values()]

    full, parts = {}, {}
    s_len = x.shape[0]
    cos, sin, got = _gather_two_level_with_rope_tables(shards["w_in"], _rope_inv_freq(), s_len, "gather_w_in")
    full["w_in"] = _full_from_gathered(got)

    g_mix, g_ffn, g_ple = small["mix_norm_g"], small["ffn_norm_g"], small["ple_norm_g"]
    g_fin = small["final_norm_g"].reshape(1, D_MODEL)
    sw, gs, ga, gb = small["sgu_w"], small["sgu_norm_g"], small["out_norm_a"], small["out_norm_b"]
    b2 = jnp.repeat(small["sgu_b"].T, HEAD_DIM, axis=1)
    head_sum = (jnp.arange(WIDTH_B)[:, None] // HEAD_DIM == jnp.arange(128)[None, :]).astype(BF16)
    n_br = len(DILATIONS)

    def arrived(names, got):
        for n, g in zip(names, got):
            full[n] = _full_from_gathered(g)

    (ua, hn1, *qkv), got = _inproj(x, g_mix, full["w_in"], cos, sin, exchange=gather("w_gate"))
    arrived(("w_gate",), got)
    ya_n = _sgu_fwd(ua, sw, b2, gs, ga)
    half = shards["w_up"].shape[0] // 2
    riders = [[("gather", shards["w_up"][:half])], [("gather", shards["w_up"][half:])], gather("w_out")]
    branch, got = [], []
    for i, d in enumerate(DILATIONS):
        o_l, g = _attn_fwd(qkv[i], d, exchange=riders[i])
        branch.append(o_l)
        got += g
    arrived(("w_up", "w_out"), [jnp.concatenate(got[:2], axis=1), got[2]])
    (y, yb, *lse), _ = _combine([o for o, _ in branch], [l for _, l in branch], ya_n, gb, head_sum.T)
    last_wave = ("w_down", "w_ple_gate", "w_ple_proj")
    (h1, a, b, act, hn2), got = _ffn_up(y, full["w_out"], x, g_ffn, full["w_gate"], full["w_up"], exchange=gather(*last_wave))
    arrived(last_wave, got)
    h2, h3, gate, pp, hn3 = _ffn_down_ple(act, full["w_down"], h1, g_ple, full["w_ple_gate"], p, full["w_ple_proj"])

    dh2, loss, d_fin, d_ple, g_ple_gate, g_ple_proj = _loss_ple_bwd(
        h3, target, g_fin, gate, pp, h2, g_ple, full["w_ple_gate"], hn3, p)
    g_down, (parts["w_ple_gate"], parts["w_ple_proj"]) = _mm_tn(
        act, dh2, "dw_down", exchange=scatter(w_ple_gate=g_ple_gate, w_ple_proj=g_ple_proj))
    (da, db), (parts["w_down"],) = _ffn_down_bwd(dh2, full["w_down"], a, b, exchange=scatter(w_down=g_down))
    g_gate, _ = _mm_tn(da, hn2, "dw_gate")
    g_up, _ = _mm_tn(db, hn2, "dw_up")
    (dh1, d_ffn, g_out), (parts["w_gate"],) = _mm_norm_bwd(
        [(da, full["w_gate"]), (db, full["w_up"])], h1, g_ffn, dh2, "ffn_up_bwd", exchange=scatter(w_gate=g_gate), dw_lhs=y)
    (dya_n, d_gb, *do_dd), (parts["w_out"],) = _outproj_bwd(dh1, full["w_out"], yb, gb, head_sum, exchange=scatter(w_out=g_out))
    grads_b = []
    for i, d in enumerate(DILATIONS):
        g3, got = _attn_bwd(qkv[i], do_dd[i], lse[i], do_dd[n_br + i], d, exchange=scatter(w_up=g_up) if i == 0 else ())
        grads_b.append(g3)
        if i == 0:
            (parts["w_up"],) = got
    dua, d_sw, d_b2, d_gs, d_ga = _sgu_bwd(ua, sw, b2, gs, ga, dya_n)
    early = {
        "sgu_w": d_sw, "sgu_b": d_b2.reshape(CHUNK, 4, HEAD_DIM).sum(axis=-1).T, "sgu_norm_g": d_gs, "out_norm_a": d_ga,
        "out_norm_b": d_gb, "ffn_norm_g": d_ffn, "ple_norm_g": d_ple, "final_norm_g": d_fin,
    }
    dproj, g_in, (early_parts,) = _dproj(
        dua, [g[0] for g in grads_b], [g[1] for g in grads_b], [g[2] for g in grads_b], cos, sin, hn1,
        exchange=[("gather", _pack_small(early, _SMALL[1:]))])
    (dx, d_mix), (parts["w_in"],) = _mm_norm_bwd(
        [(dproj, full["w_in"])], x, g_mix, dh1, "inproj_bwd", exchange=scatter(w_in=g_in))
    late = jnp.concatenate([_pack_small({"mix_norm_g": d_mix}, _SMALL[:1]), jnp.broadcast_to(loss, (8, 128))])
    (late_parts,) = _exchange_only([("gather", late)], "gather_mix_norm_grad_and_loss")
    total_loss = jnp.sum(late_parts[:, 8, 0])
    return total_loss, dx, parts, jnp.concatenate([late_parts[:, :8], early_parts], axis=1)


def kernel(x, p, mix_norm_g, w_in, sgu_w, sgu_b, sgu_norm_g, out_norm_a, out_norm_b, w_out, ffn_norm_g, w_gate, w_up, w_down, ple_norm_g, w_ple_gate, w_ple_proj, final_norm_g, loss_target, m_mix_norm_g, m_w_in, m_sgu_w, m_sgu_b, m_sgu_norm_g, m_out_norm_a, m_out_norm_b, m_w_out, m_ffn_norm_g, m_w_gate, m_w_up, m_w_down, m_ple_norm_g, m_w_ple_gate, m_w_ple_proj, m_final_norm_g, v_mix_norm_g, v_w_in, v_sgu_w, v_sgu_b, v_sgu_norm_g, v_out_norm_a, v_out_norm_b, v_w_out, v_ffn_norm_g, v_w_gate, v_w_up, v_w_down, v_ple_norm_g, v_w_ple_gate, v_w_ple_proj, v_final_norm_g):
    given = dict(locals())
    weights = {n: given[n] for n in _ORDER}
    moments_m = {n: given["m_" + n] for n in _ORDER}
    moments_v = {n: given["v_" + n] for n in _ORDER}

    shards = {n: _own_orientation(n, weights[n]).astype(BF16) for n in _BIG}
    small = {n: (weights[n][0] if n in ("sgu_w", "sgu_b") else weights[n]) for n in _SMALL}

    loss, dx, parts, small_parts = _forward_backward(x[0], p[0, 0], loss_target[0], small, shards)

    small_like = {n: weights[n] for n in _SMALL}
    grads, deltas, new_m, new_v = {}, {}, {}, {}
    for n in _BIG:
        outs = _adamw(parts[n], _own_orientation(n, weights[n]), _own_orientation(n, moments_m[n]),
                      _own_orientation(n, moments_v[n]), "adamw_" + n)
        grads[n], deltas[n], new_m[n], new_v[n] = [_reference_orientation(n, o) for o in outs]
    g, d, nm, nv = _adamw(small_parts, _pack_small(small_like), _pack_small({n: moments_m[n] for n in _SMALL}),
                          _pack_small({n: moments_v[n] for n in _SMALL}), "adamw_small")
    for out, packed in ((grads, g), (deltas, d), (new_m, nm), (new_v, nv)):
        out.update(_unpack_small(packed, small_like))

    return (loss, dx[None], *[grads[n] for n in _ORDER], *[deltas[n] for n in _ORDER],
            *[new_m[n] for n in _ORDER], *[new_v[n] for n in _ORDER])
```

```python
import functools

import jax
import jax.numpy as jnp
from jax import lax
from jax.experimental import pallas as pl
from jax.experimental.pallas import tpu as pltpu

F32 = jnp.float32
BF16 = jnp.bfloat16

D_MODEL = 1024
WIDTH_A = 256
WIDTH_B = 768
D_FF = 2816
IN_COLS = 2 * WIDTH_A + 3 * WIDTH_B
PLE_DIM = 256
HEAD_DIM = 64
N_PAIRS = WIDTH_B // 128
CHUNK = 128
N_BACK = 128
DILATIONS = (1, 4, 16)
ROPE_THETA = 10000.0
EPS = 1e-6
N_DEV = 8

ADAM_LR = 0.001
ADAM_B1 = 0.9
ADAM_B2 = 0.999
ADAM_EPS = 1e-08
ADAM_WD = 0.01
ADAM_STEP = 10

V7X_VMEM_LIMIT_BYTES = 56 * 1024 * 1024
ROW_TILE = 512
MESH = pl.DeviceIdType.MESH
NEG = -1e30

_NT = (((1,), (1,)), ((), ()))
_TN = (((0,), (0,)), ((), ()))


def _params(*semantics):
    return pltpu.CompilerParams(dimension_semantics=semantics, vmem_limit_bytes=V7X_VMEM_LIMIT_BYTES)


def _rows(tm, width):
    return pl.BlockSpec((tm, width), lambda i: (i, 0))


def _whole(shape):
    return pl.BlockSpec(shape, lambda *_: (0,) * len(shape))


def _resident(shape):
    return pl.BlockSpec(shape, lambda *_: (0,) * len(shape), pipeline_mode=pl.Buffered(1))


def _gelu(x):
    t = jnp.tanh(0.7978845608028654 * (x + 0.044715 * (x * x * x)))
    return 0.5 * x * (1.0 + t)


def _gelu_grad(x):
    t = jnp.tanh(0.7978845608028654 * (x + 0.044715 * (x * x * x)))
    return 0.5 * (1.0 + t) + 0.5 * x * (1.0 - t * t) * (0.7978845608028654 * (1.0 + 3.0 * 0.044715 * (x * x)))


def _rstd(x):
    return lax.rsqrt(jnp.mean(x * x, axis=-1, keepdims=True) + EPS)


def _norm_bwd(dn, h, g):
    r = _rstd(h)
    n = h * r
    t = dn * g
    return r * (t - n * jnp.mean(t * n, axis=-1, keepdims=True)), dn * n


def _swap_halves(x, first_half):
    return jnp.where(first_half, pltpu.roll(x, 96, 1), pltpu.roll(x, 32, 1))


def _sub_spec(d, n_cb, tm):
    return pl.BlockSpec((d, n_cb, tm // d, 128), lambda i: (0, 0, i, 0))


def _sub_shape(s_len, d, n_cb, dtype):
    return jax.ShapeDtypeStruct((d, n_cb, s_len // d, 128), dtype)


def _to_sub(stage_ref, cb_src, out_ref, cb_dst, d, tm):
    slab = stage_ref.at[cb_src]
    for r in range(d):
        out_ref[r, cb_dst] = slab[pl.ds(r, tm // d, stride=d), :].astype(out_ref.dtype)


def _from_sub(in_ref, cb_src, stage_ref, cb_dst, d, tm):
    slab = stage_ref.at[cb_dst]
    for r in range(d):
        slab[pl.ds(r, tm // d, stride=d), :] = in_ref[r, cb_src].astype(F32)


def _inproj(x, g, w, cos, sin, exchange=()):
    s_len = x.shape[0]
    tm = ROW_TILE
    n_cb = 3 * N_PAIRS

    def body(x_ref, g_ref, w_ref, cos_ref, sin_ref, ua_ref, hn_ref, *rest):
        sub_refs, stage = rest[:-1], rest[-1]
        xf = x_ref[...]
        hn = (xf * _rstd(xf) * g_ref[...]).astype(BF16)
        hn_ref[...] = hn
        c = cos_ref[...]
        s = sin_ref[...]
        first_half = (lax.broadcasted_iota(jnp.int32, (tm, 128), 1) % HEAD_DIM) < HEAD_DIM // 2
        for j in range(IN_COLS // 256):
            col = j * 256
            acc = lax.dot_general(hn, w_ref[col:col + 256, :], _NT, preferred_element_type=F32)
            if col < 2 * WIDTH_A:
                ua_ref[:, col:col + 256] = acc
                continue
            for half in range(2):
                cb = (col - 2 * WIDTH_A) // 128 + half
                t = acc[:, half * 128:(half + 1) * 128]
                if cb < 2 * N_PAIRS:
                    t = (t * c + _swap_halves(t, first_half) * s) * (0.125 if cb < N_PAIRS else 1.0)
                stage[cb] = t
                sub_refs[0][0, cb] = t.astype(BF16)
        for cb in range(n_cb):
            for d, out_ref in zip(DILATIONS[1:], sub_refs[1:]):
                _to_sub(stage, cb, out_ref, cb, d, tm)

    return _call(
        body, name="inproj", grid=(s_len // tm,),
        in_specs=[_rows(tm, D_MODEL), _whole((1, D_MODEL)), _resident((IN_COLS, D_MODEL)), _rows(tm, 128), _rows(tm, 128)],
        out_specs=[_rows(tm, 2 * WIDTH_A), _rows(tm, D_MODEL)] + [_sub_spec(d, n_cb, tm) for d in DILATIONS],
        out_shape=[jax.ShapeDtypeStruct((s_len, 2 * WIDTH_A), F32), jax.ShapeDtypeStruct((s_len, D_MODEL), BF16)]
        + [_sub_shape(s_len, d, n_cb, BF16) for d in DILATIONS],
        scratch_shapes=[pltpu.VMEM((n_cb, tm, 128), F32)],
        semantics=("parallel",), args=(x, g, w, cos, sin), exchange=exchange)


def _sgu_mix_weights(sw_ref):
    keep = lax.broadcasted_iota(jnp.int32, (CHUNK, CHUNK), 0) >= lax.broadcasted_iota(jnp.int32, (CHUNK, CHUNK), 1)
    return [jnp.where(keep, sw_ref[h], 0.0).astype(BF16) for h in range(4)], keep


def _sgu_core(ua_ref, gs_ref):
    u = ua_ref[:, :WIDTH_A]
    va = ua_ref[:, WIDTH_A:]
    vg = _gelu(va)
    xc = vg - jnp.mean(vg, axis=-1, keepdims=True)
    rstd = lax.rsqrt(jnp.mean(xc * xc, axis=-1, keepdims=True) + EPS)
    xhat = xc * rstd
    return u, va, _gelu(u), xhat, rstd, xhat * gs_ref[...]


def _sgu_fwd(ua, sw, b2, gs, ga):
    s_len = ua.shape[0]
    tm = ROW_TILE

    def body(ua_ref, sw_ref, b2_ref, gs_ref, ga_ref, out_ref):
        _, _, ug, _, _, vn = _sgu_core(ua_ref, gs_ref)
        wm, _ = _sgu_mix_weights(sw_ref)
        head = lax.broadcasted_iota(jnp.int32, (CHUNK, WIDTH_A), 1) // HEAD_DIM
        for c in range(tm // CHUNK):
            rows = slice(c * CHUNK, (c + 1) * CHUNK)
            vnc = vn[rows]
            mixed = b2_ref[...]
            for h in range(4):
                mixed = mixed + jnp.dot(wm[h], jnp.where(head == h, vnc, 0.0).astype(BF16), preferred_element_type=F32)
            ya = ug[rows] * mixed
            out_ref[rows, :] = (ya * _rstd(ya) * ga_ref[...]).astype(BF16)

    return pl.pallas_call(
        body, name="sgu_fwd", grid=(s_len // tm,),
        in_specs=[_rows(tm, 2 * WIDTH_A), _whole((4, CHUNK, CHUNK)), _whole((CHUNK, WIDTH_A)), _whole((1, WIDTH_A)), _whole((1, WIDTH_A))],
        out_specs=_rows(tm, WIDTH_A),
        out_shape=jax.ShapeDtypeStruct((s_len, WIDTH_A), BF16),
        compiler_params=_params("parallel"),
    )(ua, sw, b2, gs, ga)


def _attn_geometry(sd):
    tile = min(ROW_TILE, sd)
    return tile, tile // CHUNK, sd // tile


PAIRS_PER_STEP = 6


def _attn_spec(cb0, rows, row_index):
    return pl.BlockSpec((None, PAIRS_PER_STEP, rows, 128), lambda r, g, n: (r, cb0 // PAIRS_PER_STEP + g, row_index(n), 0))


assert PAIRS_PER_STEP == N_PAIRS
assert DILATIONS[0] == 1


def _stats_spec(rows, row_index):
    return pl.BlockSpec((None, None, rows, 128), lambda r, g, n: (r, 0, row_index(n), 0))


def _stats_shape(sd, d):
    return jax.ShapeDtypeStruct((d, 1, sd, 128), F32)


def _both_heads(x, head_a):
    zero = jnp.zeros_like(x)
    return [jnp.where(head_a, x, zero), jnp.where(head_a, zero, x)]


def _attn_fwd(qkv, d, exchange=()):
    sd = qkv.shape[2]
    tile, nb, n_tiles = _attn_geometry(sd)

    def prev(n):
        return jnp.maximum(n * nb - 1, 0)

    def body(q_ref, k_ref, kp_ref, v_ref, vp_ref, o_ref, l_ref):
        for hp in range(PAIRS_PER_STEP):
            one_pair(hp, q_ref.at[hp], k_ref.at[hp], kp_ref.at[hp], v_ref.at[hp], vp_ref.at[hp], o_ref.at[hp], l_ref)

    def one_pair(hp, q_ref, k_ref, kp_ref, v_ref, vp_ref, o_ref, l_ref):
        n = pl.program_id(2)
        lane = lax.broadcasted_iota(jnp.int32, (CHUNK, 128), 1)
        head_a = lane < HEAD_DIM
        qi = lax.broadcasted_iota(jnp.int32, (2 * CHUNK, 2 * CHUNK), 0) % CHUNK
        kc = lax.broadcasted_iota(jnp.int32, (2 * CHUNK, 2 * CHUNK), 1)
        band = (kc >= qi) & (kc <= qi + N_BACK)
        for j in range(nb):
            rows = slice(j * CHUNK, (j + 1) * CHUNK)
            if j == 0:
                kcat = jnp.concatenate([kp_ref[...], k_ref[rows, :]], axis=0)
                vcat = jnp.concatenate([vp_ref[...], v_ref[rows, :]], axis=0)
                valid = band & jnp.logical_or(n > 0, kc >= CHUNK)
            else:
                kcat = k_ref[(j - 1) * CHUNK:(j + 1) * CHUNK, :]
                vcat = v_ref[(j - 1) * CHUNK:(j + 1) * CHUNK, :]
                valid = band
            q2 = jnp.concatenate(_both_heads(q_ref[rows, :], head_a), axis=0)
            s = lax.dot_general(q2, kcat, _NT, preferred_element_type=F32)
            s = jnp.where(valid, s, NEG)
            m = jnp.max(s, axis=-1, keepdims=True)
            p = jnp.exp(s - m)
            l = jnp.sum(p, axis=-1, keepdims=True)
            o2 = jnp.dot(p.astype(BF16), vcat, preferred_element_type=F32) / l
            lse2 = m + jnp.log(l)
            o_ref[rows, :] = jnp.where(head_a, o2[:CHUNK], o2[CHUNK:]).astype(BF16)
            others = l_ref[rows, :] if hp > 0 else jnp.zeros((CHUNK, 128), F32)
            l_ref[rows, :] = jnp.where(lane == 2 * hp, lse2[:CHUNK], jnp.where(lane == 2 * hp + 1, lse2[CHUNK:], others))

    same = lambda n: n
    return _call(
        body, name=f"attn_fwd_d{d}", grid=(d, N_PAIRS // PAIRS_PER_STEP, n_tiles),
        in_specs=[_attn_spec(0, tile, same), _attn_spec(N_PAIRS, tile, same), _attn_spec(N_PAIRS, CHUNK, prev),
                  _attn_spec(2 * N_PAIRS, tile, same), _attn_spec(2 * N_PAIRS, CHUNK, prev)],
        out_specs=[_attn_spec(0, tile, same), _stats_spec(tile, same)],
        out_shape=[jax.ShapeDtypeStruct((d, N_PAIRS, sd, 128), BF16), _stats_shape(sd, d)],
        semantics=("parallel", "parallel", "parallel"), args=(qkv, qkv, qkv, qkv, qkv), exchange=exchange)


def _combine(outs, lses, ya_n, gb, head_spread, exchange=()):
    s_len = ya_n.shape[0]
    tm = ROW_TILE
    n_br = len(DILATIONS)

    def body(*refs):
        o_refs, l_refs = refs[:n_br], refs[n_br:2 * n_br]
        ya_ref, gb_ref, spread_ref, y_ref, yb_ref = refs[2 * n_br:2 * n_br + 5]
        lse_refs = refs[2 * n_br + 5:3 * n_br + 5]
        o_nat, l_nat, lse_nat, w_wide = refs[3 * n_br + 5:]
        for i, d in enumerate(DILATIONS):
            _from_sub(l_refs[i], 0, l_nat, i, d, tm)
        ls = [l_nat[i] for i in range(n_br)]
        top = jnp.maximum(jnp.maximum(ls[0], ls[1]), ls[2])
        ws = [jnp.exp(l - top) for l in ls]
        den = ws[0] + ws[1] + ws[2]
        inv = 1.0 / den
        for i in range(n_br):
            w = ws[i] * inv
            hi = w.astype(BF16)
            lo = (w - hi.astype(F32)).astype(BF16)
            w_wide[i] = (jnp.dot(hi, spread_ref[...], preferred_element_type=F32)
                         + jnp.dot(lo, spread_ref[...], preferred_element_type=F32))
        lse_nat[0] = top + jnp.log(den)
        for d, lse_ref in zip(DILATIONS, lse_refs):
            _to_sub(lse_nat, 0, lse_ref, 0, d, tm)
        sumsq = jnp.zeros((tm, 1), F32)
        for cb in range(N_PAIRS):
            cols = slice(cb * 128, (cb + 1) * 128)
            yb = w_wide[0, :, cols] * o_refs[0][0, cb].astype(F32)
            for i, d in enumerate(DILATIONS[1:], start=1):
                _from_sub(o_refs[i], cb, o_nat, i, d, tm)
                yb = yb + w_wide[i, :, cols] * o_nat[i]
            yb_ref[:, cb * 128:(cb + 1) * 128] = yb
            sumsq = sumsq + jnp.sum(yb * yb, axis=-1, keepdims=True)
        r = lax.rsqrt(sumsq / WIDTH_B + EPS)
        y_ref[:, :WIDTH_A] = ya_ref[...]
        y_ref[:, WIDTH_A:] = (yb_ref[...] * r * gb_ref[...]).astype(BF16)

    stats = [_sub_spec(d, 1, tm) for d in DILATIONS]
    return _call(
        body, name="attn_combine", grid=(s_len // tm,),
        in_specs=[_sub_spec(d, N_PAIRS, tm) for d in DILATIONS] + stats
        + [_rows(tm, WIDTH_A), _whole((1, WIDTH_B)), _whole((128, WIDTH_B))],
        out_specs=[_rows(tm, D_MODEL), _rows(tm, WIDTH_B)] + stats,
        out_shape=[jax.ShapeDtypeStruct((s_len, D_MODEL), BF16), jax.ShapeDtypeStruct((s_len, WIDTH_B), F32)]
        + [_sub_shape(s_len, d, 1, F32) for d in DILATIONS],
        scratch_shapes=[pltpu.VMEM((n_br, tm, 128), F32), pltpu.VMEM((n_br, tm, 128), F32), pltpu.VMEM((1, tm, 128), F32),
                        pltpu.VMEM((n_br, tm, WIDTH_B), F32)],
        semantics=("parallel",), args=(*outs, *lses, ya_n, gb, head_spread), exchange=exchange)


def _ffn_up(y, wout, x, g, wg, wu, exchange=()):
    s_len = x.shape[0]
    tm = ROW_TILE

    def body(y_ref, wo_ref, x_ref, g_ref, wg_ref, wu_ref, h_ref, a_ref, b_ref, act_ref, hn_ref):
        hf = x_ref[...] + jnp.dot(y_ref[...], wo_ref[...], preferred_element_type=F32)
        h_ref[...] = hf
        hn = (hf * _rstd(hf) * g_ref[...]).astype(BF16)
        hn_ref[...] = hn
        for j in range(D_FF // 256):
            cols = slice(j * 256, (j + 1) * 256)
            a = lax.dot_general(hn, wg_ref[cols, :], _NT, preferred_element_type=F32)
            b = lax.dot_general(hn, wu_ref[cols, :], _NT, preferred_element_type=F32)
            a_ref[:, cols] = a.astype(BF16)
            b_ref[:, cols] = b.astype(BF16)
            act_ref[:, cols] = (a * jax.nn.sigmoid(a) * b).astype(BF16)

    wide = jax.ShapeDtypeStruct((s_len, D_FF), BF16)
    return _call(
        body, name="ffn_up", grid=(s_len // tm,),
        in_specs=[_rows(tm, D_MODEL), _resident((D_MODEL, D_MODEL)), _rows(tm, D_MODEL), _whole((1, D_MODEL)),
                  _resident((D_FF, D_MODEL)), _resident((D_FF, D_MODEL))],
        out_specs=[_rows(tm, D_MODEL), _rows(tm, D_FF), _rows(tm, D_FF), _rows(tm, D_FF), _rows(tm, D_MODEL)],
        out_shape=[jax.ShapeDtypeStruct((s_len, D_MODEL), F32), wide, wide, wide, jax.ShapeDtypeStruct((s_len, D_MODEL), BF16)],
        semantics=("parallel",), args=(y, wout, x, g, wg, wu), exchange=exchange)


def _ffn_down_ple(act, wd, h1, g, wpg, p, wpp):
    s_len = h1.shape[0]
    tm = ROW_TILE

    def body(act_ref, wd_ref, h1_ref, g_ref, wpg_ref, p_ref, wpp_ref, h2_ref, h3_ref, gate_ref, pp_ref, hn_ref):
        hf = h1_ref[...] + jnp.dot(act_ref[...], wd_ref[...], preferred_element_type=F32)
        h2_ref[...] = hf
        hn = (hf * _rstd(hf) * g_ref[...]).astype(BF16)
        hn_ref[...] = hn
        gate = jax.nn.sigmoid(jnp.dot(hn, wpg_ref[...], preferred_element_type=F32))
        pp = lax.dot_general(p_ref[...].astype(BF16), wpp_ref[...], _NT, preferred_element_type=F32)
        h3_ref[...] = hf + gate * pp
        gate_ref[...] = gate.astype(BF16)
        pp_ref[...] = pp.astype(BF16)

    full = jax.ShapeDtypeStruct((s_len, D_MODEL), F32)
    half = jax.ShapeDtypeStruct((s_len, D_MODEL), BF16)
    return pl.pallas_call(
        body, name="ffn_down_ple", grid=(s_len // tm,),
        in_specs=[_rows(tm, D_FF), _resident((D_FF, D_MODEL)), _rows(tm, D_MODEL), _whole((1, D_MODEL)),
                  _resident((D_MODEL, D_MODEL)), _rows(tm, PLE_DIM), _resident((D_MODEL, PLE_DIM))],
        out_specs=[_rows(tm, D_MODEL)] * 5,
        out_shape=[full, full, half, half, half],
        compiler_params=_params("parallel"),
    )(act, wd, h1, g, wpg, p, wpp)


def _loss_ple_bwd(h3, target, gf, gate, pp, h2, g_ple, wpg, hn3, p):
    s_len = h3.shape[0]
    tm = ROW_TILE
    n_steps = s_len // tm

    def body(h_ref, t_ref, g_ref, gate_ref, pp_ref, h2_ref, gp_ref, w_ref, hn_ref, p_ref,
             dh2_ref, loss_ref, dg_ref, dgp_ref, dwg_ref, dwp_ref, acc_g, acc_p):
        step = pl.program_id(0)

        @pl.when(step == 0)
        def _():
            loss_ref[...] = jnp.zeros_like(loss_ref)
            dg_ref[...] = jnp.zeros_like(dg_ref)
            dgp_ref[...] = jnp.zeros_like(dgp_ref)
            acc_g[...] = jnp.zeros_like(acc_g)
            acc_p[...] = jnp.zeros_like(acc_p)

        hf = h_ref[...]
        gfv = g_ref[...]
        err = hf * _rstd(hf) * gfv - t_ref[...]
        loss_ref[...] += 0.5 * jnp.sum(jnp.sum(err * err, axis=-1, keepdims=True), axis=0, keepdims=True) / D_MODEL
        dh, dg_rows = _norm_bwd(err / D_MODEL, hf, gfv)
        dg_ref[...] += jnp.sum(dg_rows, axis=0, keepdims=True)
        gate = gate_ref[...].astype(F32)
        dz = (dh * pp_ref[...].astype(F32) * gate * (1.0 - gate)).astype(BF16)
        dpp = (dh * gate).astype(BF16)
        dn = lax.dot_general(dz, w_ref[...], _NT, preferred_element_type=F32)
        dh2, dgp_rows = _norm_bwd(dn, h2_ref[...], gp_ref[...])
        dh2 = dh + dh2
        dh2_ref[...] = dh2
        dgp_ref[...] += jnp.sum(dgp_rows, axis=0, keepdims=True)
        acc_g[...] += lax.dot_general(hn_ref[...], dz, _TN, preferred_element_type=F32)
        acc_p[...] += lax.dot_general(dpp, p_ref[...].astype(BF16), _TN, preferred_element_type=F32)

        @pl.when(step == n_steps - 1)
        def _():
            dwg_ref[...] = acc_g[...].astype(BF16)
            dwp_ref[...] = acc_p[...].astype(BF16)

    gain = jax.ShapeDtypeStruct((1, D_MODEL), F32)
    return pl.pallas_call(
        body, name="loss_ple_bwd", grid=(n_steps,),
        in_specs=[_rows(tm, D_MODEL), _rows(tm, D_MODEL), _whole((1, D_MODEL)), _rows(tm, D_MODEL), _rows(tm, D_MODEL),
                  _rows(tm, D_MODEL), _whole((1, D_MODEL)), _resident((D_MODEL, D_MODEL)), _rows(tm, D_MODEL),
                  _rows(tm, PLE_DIM)],
        out_specs=[_rows(tm, D_MODEL), _whole((1, 128)), _whole((1, D_MODEL)), _whole((1, D_MODEL)),
                   _whole((D_MODEL, D_MODEL)), _whole((D_MODEL, PLE_DIM))],
        out_shape=[jax.ShapeDtypeStruct((s_len, D_MODEL), F32), jax.ShapeDtypeStruct((1, 128), F32), gain, gain,
                   jax.ShapeDtypeStruct((D_MODEL, D_MODEL), BF16), jax.ShapeDtypeStruct((D_MODEL, PLE_DIM), BF16)],
        scratch_shapes=[pltpu.VMEM((D_MODEL, D_MODEL), F32), pltpu.VMEM((D_MODEL, PLE_DIM), F32)],
        compiler_params=_params("arbitrary"),
    )(h3, target, gf, gate, pp, h2, g_ple, wpg, hn3, p)


def _mm_norm_bwd(parts, h, g, dres, name, exchange=(), dw_lhs=None):
    s_len = h.shape[0]
    tm = ROW_TILE
    n_parts = len(parts)
    n_steps = s_len // tm
    has_dw = dw_lhs is not None

    def body(*refs):
        a_refs = refs[0:2 * n_parts:2]
        w_refs = refs[1:2 * n_parts:2]
        h_ref, g_ref, r_ref = refs[2 * n_parts:2 * n_parts + 3]
        rest = refs[2 * n_parts + 3:]
        step = pl.program_id(0)
        if has_dw:
            lhs_ref, o_ref, dg_ref, dw_ref, acc_ref = rest
        else:
            o_ref, dg_ref = rest

        @pl.when(step == 0)
        def _():
            dg_ref[...] = jnp.zeros_like(dg_ref)
            if has_dw:
                acc_ref[...] = jnp.zeros_like(acc_ref)

        dn = jnp.dot(a_refs[0][...], w_refs[0][...], preferred_element_type=F32)
        for a_ref, w_ref in zip(a_refs[1:], w_refs[1:]):
            dn = dn + jnp.dot(a_ref[...], w_ref[...], preferred_element_type=F32)
        dh, dg_rows = _norm_bwd(dn, h_ref[...], g_ref[...])
        out = r_ref[...] + dh
        o_ref[...] = out
        dg_ref[...] += jnp.sum(dg_rows, axis=0, keepdims=True)
        if has_dw:
            acc_ref[...] += lax.dot_general(lhs_ref[...], out.astype(BF16), _TN, preferred_element_type=F32)

            @pl.when(step == n_steps - 1)
            def _():
                dw_ref[...] = acc_ref[...].astype(BF16)

    in_specs, args = [], []
    for a, w in parts:
        in_specs += [_rows(tm, a.shape[1]), _resident(w.shape)]
        args += [a, w]
    in_specs += [_rows(tm, D_MODEL), _whole((1, D_MODEL)), _rows(tm, D_MODEL)]
    args += [h, g, dres]
    out_specs = [_rows(tm, D_MODEL), _whole((1, D_MODEL))]
    out_shape = [jax.ShapeDtypeStruct((s_len, D_MODEL), F32), jax.ShapeDtypeStruct((1, D_MODEL), F32)]
    scratch = []
    if has_dw:
        m = dw_lhs.shape[1]
        in_specs.append(_rows(tm, m))
        args.append(dw_lhs)
        out_specs.append(_whole((m, D_MODEL)))
        out_shape.append(jax.ShapeDtypeStruct((m, D_MODEL), BF16))
        scratch.append(pltpu.VMEM((m, D_MODEL), F32))
    return _call(
        body, name=name, grid=(n_steps,), in_specs=in_specs, out_specs=out_specs, out_shape=out_shape,
        scratch_shapes=scratch, semantics=("arbitrary",), args=tuple(args), exchange=exchange)


def _ffn_down_bwd(dh, wdt, a, b, exchange=()):
    s_len = dh.shape[0]
    tm = ROW_TILE

    def body(dh_ref, w_ref, a_ref, b_ref, da_ref, db_ref):
        dhb = dh_ref[...].astype(BF16)
        for j in range(D_FF // 256):
            cols = slice(j * 256, (j + 1) * 256)
            dact = lax.dot_general(dhb, w_ref[cols, :], _NT, preferred_element_type=F32)
            av = a_ref[:, cols].astype(F32)
            bv = b_ref[:, cols].astype(F32)
            sig = jax.nn.sigmoid(av)
            t = dact * sig
            silu = av * sig
            da_ref[:, cols] = (t * bv * (1.0 + av - silu)).astype(BF16)
            db_ref[:, cols] = (dact * silu).astype(BF16)

    wide = jax.ShapeDtypeStruct((s_len, D_FF), BF16)
    return _call(
        body, name="ffn_down_bwd", grid=(s_len // tm,),
        in_specs=[_rows(tm, D_MODEL), _resident((D_FF, D_MODEL)), _rows(tm, D_FF), _rows(tm, D_FF)],
        out_specs=[_rows(tm, D_FF), _rows(tm, D_FF)],
        out_shape=[wide, wide],
        semantics=("parallel",), args=(dh, wdt, a, b), exchange=exchange)


def _outproj_bwd(dh1, woutt, yb, gb, head_sum, exchange=()):
    s_len = dh1.shape[0]
    tm = ROW_TILE
    n_br = len(DILATIONS)

    def body(dh_ref, w_ref, yb_ref, gb_ref, e_ref, dya_ref, dgb_ref, *rest):
        do_refs, dd_refs = rest[:n_br], rest[n_br:2 * n_br]
        do_nat, dd_nat = rest[2 * n_br:]

        @pl.when(pl.program_id(0) == 0)
        def _():
            dgb_ref[...] = jnp.zeros_like(dgb_ref)

        dhb = dh_ref[...].astype(BF16)
        dya_ref[...] = lax.dot_general(dhb, w_ref[:WIDTH_A, :], _NT, preferred_element_type=F32)
        dyn = lax.dot_general(dhb, w_ref[WIDTH_A:, :], _NT, preferred_element_type=F32)
        ybv = yb_ref[...]
        dyb, dg_rows = _norm_bwd(dyn, ybv, gb_ref[...])
        dgb_ref[...] += jnp.sum(dg_rows, axis=0, keepdims=True)
        prod = dyb * ybv
        hi = prod.astype(BF16)
        lo = (prod - hi.astype(F32)).astype(BF16)
        dd_nat[0] = (jnp.dot(hi, e_ref[...], preferred_element_type=F32)
                     + jnp.dot(lo, e_ref[...], preferred_element_type=F32))
        for i, d in enumerate(DILATIONS):
            _to_sub(dd_nat, 0, dd_refs[i], 0, d, tm)
        for cb in range(N_PAIRS):
            piece = dyb[:, cb * 128:(cb + 1) * 128]
            do_nat[cb] = piece
            do_refs[0][0, cb] = piece.astype(BF16)
            for i, d in enumerate(DILATIONS[1:], start=1):
                _to_sub(do_nat, cb, do_refs[i], cb, d, tm)

    return _call(
        body, name="outproj_bwd", grid=(s_len // tm,),
        in_specs=[_rows(tm, D_MODEL), _resident((D_MODEL, D_MODEL)), _rows(tm, WIDTH_B), _whole((1, WIDTH_B)), _whole((WIDTH_B, 128))],
        out_specs=[_rows(tm, WIDTH_A), _whole((1, WIDTH_B))] + [_sub_spec(d, N_PAIRS, tm) for d in DILATIONS]
        + [_sub_spec(d, 1, tm) for d in DILATIONS],
        out_shape=[jax.ShapeDtypeStruct((s_len, WIDTH_A), F32), jax.ShapeDtypeStruct((1, WIDTH_B), F32)]
        + [_sub_shape(s_len, d, N_PAIRS, BF16) for d in DILATIONS] + [_sub_shape(s_len, d, 1, F32) for d in DILATIONS],
        scratch_shapes=[pltpu.VMEM((N_PAIRS, tm, 128), F32), pltpu.VMEM((1, tm, 128), F32)],
        semantics=("arbitrary",), args=(dh1, woutt, yb, gb, head_sum), exchange=exchange)


def _attn_bwd(qkv, do, lse, dd, d, exchange=()):
    sd = qkv.shape[2]
    tile, nb, n_tiles = _attn_geometry(sd)
    last_block = sd // CHUNK - 1

    def nxt(n):
        return jnp.minimum((n + 1) * nb, last_block)

    def block(ref, next_ref, j):
        return ref[j * CHUNK:(j + 1) * CHUNK, :] if j < nb else next_ref[...]

    def body(q_ref, qn_ref, k_ref, v_ref, do_ref, don_ref, l_ref, ln_ref, dd_ref, ddn_ref,
             dq_ref, dk_ref, dv_ref, carry_ref):
        l_t = [block(l_ref, ln_ref, j).T for j in range(nb + 1)]
        dd_t = [block(dd_ref, ddn_ref, j).T for j in range(nb + 1)]
        for hp in range(PAIRS_PER_STEP):
            l_rows = [jnp.concatenate([t[2 * hp:2 * hp + 1, :], t[2 * hp + 1:2 * hp + 2, :]], axis=1) for t in l_t]
            dd_rows = [jnp.concatenate([t[2 * hp:2 * hp + 1, :], t[2 * hp + 1:2 * hp + 2, :]], axis=1) for t in dd_t]
            one_pair(q_ref.at[hp], qn_ref.at[hp], k_ref.at[hp], v_ref.at[hp], do_ref.at[hp], don_ref.at[hp],
                     l_rows, dd_rows, dq_ref.at[hp], dk_ref.at[hp], dv_ref.at[hp], carry_ref.at[hp])

    def one_pair(q_ref, qn_ref, k_ref, v_ref, do_ref, don_ref, l_rows, dd_rows, dq_ref, dk_ref, dv_ref, carry_ref):
        n = pl.program_id(2)

        @pl.when(n == 0)
        def _():
            carry_ref[...] = jnp.zeros_like(carry_ref)

        head_a = lax.broadcasted_iota(jnp.int32, (CHUNK, 128), 1) < HEAD_DIM
        col = lax.broadcasted_iota(jnp.int32, (CHUNK, 4 * CHUNK), 1)
        qi = col % CHUNK
        ki = lax.broadcasted_iota(jnp.int32, (CHUNK, 4 * CHUNK), 0)
        is_after = col >= 2 * CHUNK
        mask = (is_after & (ki >= qi)) | (jnp.logical_not(is_after) & (qi >= ki))
        mask_last = mask & jnp.logical_or(jnp.logical_not(is_after), n < n_tiles - 1)
        dq_acc = [carry_ref[...]] + [jnp.zeros((CHUNK, 128), F32) for _ in range(nb)]

        q_st = [jnp.concatenate(_both_heads(block(q_ref, qn_ref, j), head_a), axis=0) for j in range(nb + 1)]
        do_st = [jnp.concatenate(_both_heads(block(do_ref, don_ref, j), head_a), axis=0) for j in range(nb + 1)]

        for j in range(nb):
            rows = slice(j * CHUNK, (j + 1) * CHUNK)
            kj = k_ref[rows, :]
            vj = v_ref[rows, :]
            msk = mask if j + 1 < nb else mask_last
            qs = jnp.concatenate([q_st[j], q_st[j + 1]], axis=0)
            dos = jnp.concatenate([do_st[j], do_st[j + 1]], axis=0)
            ls = jnp.concatenate([l_rows[j], l_rows[j + 1]], axis=1)
            dds = jnp.concatenate([dd_rows[j], dd_rows[j + 1]], axis=1)
            st = lax.dot_general(kj, qs, _NT, preferred_element_type=F32)
            pt = jnp.exp(jnp.where(msk, st - ls, NEG))
            dpt = lax.dot_general(vj, dos, _NT, preferred_element_type=F32)
            dst = (pt * (dpt - dds)).astype(BF16)
            dv_ref[rows, :] = jnp.dot(pt.astype(BF16), dos, preferred_element_type=F32).astype(BF16)
            dk_ref[rows, :] = jnp.dot(dst, qs, preferred_element_type=F32).astype(BF16)
            dqs = lax.dot_general(dst, kj, _TN, preferred_element_type=F32)
            dq_acc[j] = dq_acc[j] + jnp.where(head_a, dqs[:CHUNK], dqs[CHUNK:2 * CHUNK])
            dq_acc[j + 1] = dq_acc[j + 1] + jnp.where(head_a, dqs[2 * CHUNK:3 * CHUNK], dqs[3 * CHUNK:])
        for j in range(nb):
            dq_ref[j * CHUNK:(j + 1) * CHUNK, :] = dq_acc[j].astype(BF16)
        carry_ref[...] = dq_acc[nb]

    same = lambda n: n
    grad = jax.ShapeDtypeStruct((d, N_PAIRS, sd, 128), BF16)
    return _call(
        body, name=f"attn_bwd_d{d}", grid=(d, N_PAIRS // PAIRS_PER_STEP, n_tiles),
        in_specs=[_attn_spec(0, tile, same), _attn_spec(0, CHUNK, nxt), _attn_spec(N_PAIRS, tile, same),
                  _attn_spec(2 * N_PAIRS, tile, same), _attn_spec(0, tile, same), _attn_spec(0, CHUNK, nxt),
                  _stats_spec(tile, same), _stats_spec(CHUNK, nxt), _stats_spec(tile, same), _stats_spec(CHUNK, nxt)],
        out_specs=[_attn_spec(0, tile, same)] * 3,
        out_shape=[grad, grad, grad],
        scratch_shapes=[pltpu.VMEM((PAIRS_PER_STEP, CHUNK, 128), F32)],
        semantics=("parallel", "parallel", "arbitrary"), args=(qkv, qkv, qkv, qkv, do, do, lse, lse, dd, dd), exchange=exchange)


def _sgu_bwd(ua, sw, b2, gs, ga, dya_n):
    s_len = ua.shape[0]
    tm = ROW_TILE

    def body(ua_ref, sw_ref, b2_ref, gs_ref, ga_ref, dy_ref, dua_ref, dsw_ref, db2_ref, dgs_ref, dga_ref):
        @pl.when(pl.program_id(0) == 0)
        def _():
            dsw_ref[...] = jnp.zeros_like(dsw_ref)
            db2_ref[...] = jnp.zeros_like(db2_ref)
            dgs_ref[...] = jnp.zeros_like(dgs_ref)
            dga_ref[...] = jnp.zeros_like(dga_ref)

        u, va, ug, xhat, rstd, vn = _sgu_core(ua_ref, gs_ref)
        wm, keep = _sgu_mix_weights(sw_ref)
        head = lax.broadcasted_iota(jnp.int32, (CHUNK, WIDTH_A), 1) // HEAD_DIM
        gav = ga_ref[...]
        gsv = gs_ref[...]
        dga = jnp.zeros((1, WIDTH_A), F32)
        dgs = jnp.zeros((1, WIDTH_A), F32)
        db2 = jnp.zeros((CHUNK, WIDTH_A), F32)
        dsw = [jnp.zeros((CHUNK, CHUNK), F32) for _ in range(4)]
        for c in range(tm // CHUNK):
            rows = slice(c * CHUNK, (c + 1) * CHUNK)
            vnc = vn[rows]
            vnb = vnc.astype(BF16)
            mixed = b2_ref[...]
            for h in range(4):
                mixed = mixed + jnp.dot(wm[h], jnp.where(head == h, vnc, 0.0).astype(BF16), preferred_element_type=F32)
            ugc = ug[rows]
            dya, dga_rows = _norm_bwd(dy_ref[rows, :], ugc * mixed, gav)
            dga = dga + jnp.sum(dga_rows, axis=0, keepdims=True)
            dmixed = dya * ugc
            db2 = db2 + dmixed
            dvn = jnp.zeros((CHUNK, WIDTH_A), F32)
            for h in range(4):
                dmh = jnp.where(head == h, dmixed, 0.0).astype(BF16)
                dsw[h] = dsw[h] + lax.dot_general(dmh, vnb, _NT, preferred_element_type=F32)
                dvn = dvn + lax.dot_general(wm[h], dmh, _TN, preferred_element_type=F32)
            xh = xhat[rows]
            dgs = dgs + jnp.sum(dvn * xh, axis=0, keepdims=True)
            dxh = dvn * gsv
            dvg = rstd[rows] * (dxh - jnp.mean(dxh, axis=-1, keepdims=True) - xh * jnp.mean(dxh * xh, axis=-1, keepdims=True))
            dua_ref[rows, :WIDTH_A] = (dya * mixed * _gelu_grad(u[rows])).astype(BF16)
            dua_ref[rows, WIDTH_A:] = (dvg * _gelu_grad(va[rows])).astype(BF16)
        for h in range(4):
            dsw_ref[h] += jnp.where(keep, dsw[h], 0.0)
        db2_ref[...] += db2
        dgs_ref[...] += dgs
        dga_ref[...] += dga

    return pl.pallas_call(
        body, name="sgu_bwd", grid=(s_len // tm,),
        in_specs=[_rows(tm, 2 * WIDTH_A), _whole((4, CHUNK, CHUNK)), _whole((CHUNK, WIDTH_A)), _whole((1, WIDTH_A)),
                  _whole((1, WIDTH_A)), _rows(tm, WIDTH_A)],
        out_specs=[_rows(tm, 2 * WIDTH_A), _whole((4, CHUNK, CHUNK)), _whole((CHUNK, WIDTH_A)), _whole((1, WIDTH_A)), _whole((1, WIDTH_A))],
        out_shape=[jax.ShapeDtypeStruct((s_len, 2 * WIDTH_A), BF16), jax.ShapeDtypeStruct((4, CHUNK, CHUNK), F32),
                   jax.ShapeDtypeStruct((CHUNK, WIDTH_A), F32), jax.ShapeDtypeStruct((1, WIDTH_A), F32),
                   jax.ShapeDtypeStruct((1, WIDTH_A), F32)],
        compiler_params=_params("arbitrary"),
    )(ua, sw, b2, gs, ga, dya_n)


def _dproj(dua, dqs, dks, dvs, cos, sin, hn1, exchange=()):
    s_len = dua.shape[0]
    tm = ROW_TILE
    n_br = len(DILATIONS)
    n_steps = s_len // tm

    def body(dua_ref, *rest):
        groups = [rest[g * n_br:(g + 1) * n_br] for g in range(3)]
        cos_ref, sin_ref, hn_ref, out_ref, dw_ref, acc, dw_acc = rest[3 * n_br:]
        step = pl.program_id(0)

        @pl.when(step == 0)
        def _():
            dw_acc[...] = jnp.zeros_like(dw_acc)

        out_ref[:, :2 * WIDTH_A] = dua_ref[...]
        c = cos_ref[...]
        s = sin_ref[...]
        first_half = (lax.broadcasted_iota(jnp.int32, (tm, 128), 1) % HEAD_DIM) < HEAD_DIM // 2
        for g, refs in enumerate(groups):
            for cb in range(N_PAIRS):
                t = refs[0][0, cb].astype(F32)
                for i, d in enumerate(DILATIONS[1:]):
                    _from_sub(refs[i + 1], cb, acc, i, d, tm)
                    t = t + acc[i]
                if g < 2:
                    t = (t * c - _swap_halves(t, first_half) * s) * (0.125 if g == 0 else 1.0)
                col = 2 * WIDTH_A + g * WIDTH_B + cb * 128
                out_ref[:, col:col + 128] = t.astype(BF16)
        hn = hn_ref[...]
        for j in range(IN_COLS // 256):
            cols = slice(j * 256, (j + 1) * 256)
            dw_acc[cols, :] += lax.dot_general(out_ref[:, cols], hn, _TN, preferred_element_type=F32)

        @pl.when(step == n_steps - 1)
        def _():
            dw_ref[...] = dw_acc[...].astype(BF16)

    subs = [_sub_spec(d, N_PAIRS, tm) for d in DILATIONS]
    (dproj, dw), received = _call(
        body, name="dproj_dw_in", grid=(n_steps,),
        in_specs=[_rows(tm, 2 * WIDTH_A)] + subs * 3 + [_rows(tm, 128), _rows(tm, 128), _rows(tm, D_MODEL)],
        out_specs=[_rows(tm, IN_COLS), _whole((IN_COLS, D_MODEL))],
        out_shape=[jax.ShapeDtypeStruct((s_len, IN_COLS), BF16), jax.ShapeDtypeStruct((IN_COLS, D_MODEL), BF16)],
        scratch_shapes=[pltpu.VMEM((n_br - 1, tm, 128), F32), pltpu.VMEM((IN_COLS, D_MODEL), F32)],
        semantics=("arbitrary",), args=(dua, *dqs, *dks, *dvs, cos, sin, hn1), exchange=exchange)
    return dproj, dw, received


def _mm_tn(a, b, name, exchange=()):
    s_len, m = a.shape
    n = b.shape[1]
    tk = 2 * ROW_TILE
    tm = m if m <= 512 else (1408 if m == D_FF else 512)
    n_k = s_len // tk

    def body(a_ref, b_ref, o_ref, acc_ref):
        k = pl.program_id(1)

        @pl.when(k == 0)
        def _():
            acc_ref[...] = jnp.zeros_like(acc_ref)

        acc_ref[...] += lax.dot_general(a_ref[...].astype(BF16), b_ref[...].astype(BF16), _TN, preferred_element_type=F32)

        @pl.when(k == n_k - 1)
        def _():
            o_ref[...] = acc_ref[...].astype(BF16)

    (grad,), received = _call(
        body, name=name, grid=(m // tm, n_k),
        in_specs=[pl.BlockSpec((tk, tm), lambda i, k: (k, i)), pl.BlockSpec((tk, n), lambda i, k: (k, 0))],
        out_specs=[pl.BlockSpec((tm, n), lambda i, k: (i, 0))],
        out_shape=[jax.ShapeDtypeStruct((m, n), BF16)],
        scratch_shapes=[pltpu.VMEM((tm, n), F32)],
        semantics=("parallel", "arbitrary"), args=(a, b), exchange=exchange)
    return grad, received


def _position():
    x, y, c = lax.axis_index("x"), lax.axis_index("y"), lax.axis_index("c")
    return x, y, c, 4 * x + 2 * y + c


def _peer(x, y, c, rel):
    return (x ^ ((rel >> 2) & 1), y ^ ((rel >> 1) & 1), c ^ (rel & 1))


def _exchange_out_shape(kind, arr):
    return jax.ShapeDtypeStruct(((N_DEV,) + arr.shape) if kind == "gather" else arr.shape, arr.dtype)


def _exchange_sems(n_items):
    return [pltpu.SemaphoreType.DMA((n_items, N_DEV)), pltpu.SemaphoreType.DMA((n_items, N_DEV)), pltpu.SemaphoreType.DMA((n_items,))]


def _exchange_copies(kinds, srcs, dsts, sems):
    send_sems, recv_sems, local_sems = sems
    x, y, c, me = _position()
    local, sends, recvs = [], [], []
    for k, (kind, src, dst) in enumerate(zip(kinds, srcs, dsts)):
        own = src if kind == "gather" else src.at[me]
        local.append(pltpu.make_async_copy(own, dst.at[me], local_sems.at[k]))
        for rel in range(1, N_DEV):
            going = src if kind == "gather" else src.at[me ^ rel]
            common = dict(send_sem=send_sems.at[k, rel], recv_sem=recv_sems.at[k, rel],
                          device_id=_peer(x, y, c, rel), device_id_type=MESH)
            sends.append(pltpu.make_async_remote_copy(src_ref=going, dst_ref=dst.at[me], **common))
            recvs.append(pltpu.make_async_remote_copy(src_ref=own, dst_ref=dst.at[me ^ rel], **common))
    return local, sends, recvs


def _exchange_start(kinds, srcs, dsts, sems):
    local, sends, _ = _exchange_copies(kinds, srcs, dsts, sems)
    for cp in local + sends:
        cp.start()


def _exchange_finish(kinds, srcs, dsts, sems):
    local, sends, recvs = _exchange_copies(kinds, srcs, dsts, sems)
    for cp in recvs:
        cp.wait_recv()
    for cp in sends:
        cp.wait_send()
    for cp in local:
        cp.wait()


def _exchange_only(items, name):
    kinds = [k for k, _ in items]
    n = len(items)

    def body(*refs):
        srcs, dsts, sems = refs[:n], refs[n:2 * n], refs[2 * n:]
        _exchange_start(kinds, srcs, dsts, sems)
        _exchange_finish(kinds, srcs, dsts, sems)

    any_spec = pl.BlockSpec(memory_space=pl.ANY)
    return pl.pallas_call(
        body, name=name, in_specs=[any_spec] * n, out_specs=[any_spec] * n,
        out_shape=[_exchange_out_shape(k, a) for k, a in items],
        scratch_shapes=_exchange_sems(n),
        compiler_params=pltpu.CompilerParams(has_side_effects=True),
    )(*[a for _, a in items])


def _gather_two_level_with_rope_tables(shard, inv_freq, s_len, name):
    rows = ROW_TILE

    def body(inv_ref, src, cos_ref, sin_ref, dst, send_sems, recv_sems, local_sem):
        x, y, c, me = _position()
        sibling = (x, y, 1 - c)
        chips = [(1 - x, y), (x, 1 - y), (1 - x, 1 - y)]

        def block(px, py, pc):
            return dst.at[4 * px + 2 * py + pc]

        def copy(k, blk, to, src_ref=None):
            return pltpu.make_async_remote_copy(
                src_ref=block(*blk) if src_ref is None else src_ref, dst_ref=block(*blk),
                send_sem=send_sems.at[k], recv_sem=recv_sems.at[k], device_id=to, device_id_type=MESH)

        mine = pltpu.make_async_copy(src, dst.at[me], local_sem)
        mine.start()
        first = [copy(0, (x, y, c), sibling, src)] + [copy(1 + j, (x, y, c), (*chip, c), src) for j, chip in enumerate(chips)]
        for cp in first:
            cp.start()

        inv = inv_ref[...]
        lane = lax.broadcasted_iota(jnp.int32, (rows, 128), 1)
        sign = jnp.where((lane // (HEAD_DIM // 2)) % 2 == 0, -1.0, 1.0)
        row = lax.broadcasted_iota(jnp.int32, (rows, 128), 0)

        @pl.loop(0, s_len // rows)
        def _(i):
            at = pl.multiple_of(i * rows, rows)
            ang = (row + at).astype(F32) * inv
            cos_ref[pl.ds(at, rows), :] = jnp.cos(ang)
            sin_ref[pl.ds(at, rows), :] = jnp.sin(ang) * sign

        passed = [copy(4 + j, (*chip, c), sibling) for j, chip in enumerate(chips)]
        for j, chip in enumerate(chips):
            copy(1 + j, (*chip, c), (x, y, c)).wait_recv()
            passed[j].start()
        copy(0, (x, y, 1 - c), (x, y, c)).wait_recv()
        for j, chip in enumerate(chips):
            copy(4 + j, (*chip, 1 - c), (x, y, c)).wait_recv()
        for cp in first + passed:
            cp.wait_send()
        mine.wait()

    any_spec = pl.BlockSpec(memory_space=pl.ANY)
    vmem = pl.BlockSpec(memory_space=pltpu.VMEM)
    table = jax.ShapeDtypeStruct((s_len, 128), F32)
    return pl.pallas_call(
        body, name=name, in_specs=[vmem, any_spec], out_specs=[vmem, vmem, any_spec],
        out_shape=[table, table, _exchange_out_shape("gather", shard)],
        scratch_shapes=[pltpu.SemaphoreType.DMA((N_DEV - 1,)), pltpu.SemaphoreType.DMA((N_DEV - 1,)), pltpu.SemaphoreType.DMA],
        compiler_params=pltpu.CompilerParams(has_side_effects=True, vmem_limit_bytes=V7X_VMEM_LIMIT_BYTES),
    )(inv_freq, shard)


def _call(body, *, name, grid, in_specs, out_specs, out_shape, args, scratch_shapes=(), semantics, exchange=()):
    if not exchange:
        outs = pl.pallas_call(body, name=name, grid=grid, in_specs=in_specs, out_specs=out_specs, out_shape=out_shape,
                              scratch_shapes=list(scratch_shapes), compiler_params=_params(*semantics))(*args)
        return outs, []
    kinds = [k for k, _ in exchange]
    n_in, n_out, n_x, n_scr = len(in_specs), len(out_specs), len(exchange), len(scratch_shapes)

    def wrapped(*refs):
        ins, refs = refs[:n_in], refs[n_in:]
        srcs, refs = refs[:n_x], refs[n_x:]
        outs, refs = refs[:n_out], refs[n_out:]
        dsts, refs = refs[:n_x], refs[n_x:]
        scratch, sems = refs[:n_scr], refs[n_scr:]
        ids = [pl.program_id(a) for a in range(len(grid))]
        first = functools.reduce(jnp.logical_and, [i == 0 for i in ids])
        last = functools.reduce(jnp.logical_and, [i == g - 1 for i, g in zip(ids, grid)])

        @pl.when(first)
        def _():
            _exchange_start(kinds, srcs, dsts, sems)

        body(*ins, *outs, *scratch)

        @pl.when(last)
        def _():
            _exchange_finish(kinds, srcs, dsts, sems)

    any_spec = pl.BlockSpec(memory_space=pl.ANY)
    outs = pl.pallas_call(
        wrapped, name=name, grid=grid,
        in_specs=list(in_specs) + [any_spec] * n_x, out_specs=list(out_specs) + [any_spec] * n_x,
        out_shape=list(out_shape) + [_exchange_out_shape(k, a) for k, a in exchange],
        scratch_shapes=list(scratch_shapes) + _exchange_sems(n_x),
        compiler_params=pltpu.CompilerParams(dimension_semantics=("arbitrary",) * len(grid),
                                             vmem_limit_bytes=V7X_VMEM_LIMIT_BYTES, has_side_effects=True),
    )(*args, *[a for _, a in exchange])
    return outs[:n_out], outs[n_out:]


def _adamw_math(w, g, m, v):
    m = ADAM_B1 * m + (1.0 - ADAM_B1) * g
    v = ADAM_B2 * v + (1.0 - ADAM_B2) * (g * g)
    m_hat = m / (1.0 - ADAM_B1 ** ADAM_STEP)
    v_hat = v / (1.0 - ADAM_B2 ** ADAM_STEP)
    return -ADAM_LR * (m_hat / (jnp.sqrt(v_hat) + ADAM_EPS) + ADAM_WD * w), m, v


def _adamw(parts, w, m, v, name):
    rows, cols = w.shape
    tm = 256 if rows % 256 == 0 and rows > 256 else rows

    def body(p_ref, w_ref, m_ref, v_ref, g_ref, d_ref, nm_ref, nv_ref):
        g = p_ref[0].astype(F32)
        for j in range(1, N_DEV):
            g = g + p_ref[j].astype(F32)
        delta, nm, nv = _adamw_math(w_ref[...], g, m_ref[...], v_ref[...])
        g_ref[...] = g
        d_ref[...] = delta
        nm_ref[...] = nm
        nv_ref[...] = nv

    shard = jax.ShapeDtypeStruct((rows, cols), F32)
    return pl.pallas_call(
        body, name=name, grid=(rows // tm,),
        in_specs=[pl.BlockSpec((N_DEV, tm, cols), lambda i: (0, i, 0))] + [_rows(tm, cols)] * 3,
        out_specs=[_rows(tm, cols)] * 4,
        out_shape=[shard] * 4,
        compiler_params=_params("parallel"),
    )(parts, w, m, v)


_SMALL = ("mix_norm_g", "sgu_w", "sgu_b", "sgu_norm_g", "out_norm_a", "out_norm_b", "ffn_norm_g", "ple_norm_g", "final_norm_g")
_BIG = ("w_in", "w_out", "w_gate", "w_up", "w_down", "w_ple_gate", "w_ple_proj")
_COLUMN_SHARDED = ("w_in", "w_gate", "w_up", "w_ple_proj")
_ORDER = ("mix_norm_g", "w_in", "sgu_w", "sgu_b", "sgu_norm_g", "out_norm_a", "out_norm_b", "w_out", "ffn_norm_g",
          "w_gate", "w_up", "w_down", "ple_norm_g", "w_ple_gate", "w_ple_proj", "final_norm_g")


def _pack_small(values, names=_SMALL):
    flat = jnp.concatenate([values[n].reshape(-1).astype(F32) for n in names])
    pad = (-flat.shape[0]) % (8 * 128)
    return jnp.pad(flat, (0, pad)).reshape(-1, 128)


def _unpack_small(packed, like):
    flat = packed.reshape(-1)
    out, at = {}, 0
    for n in _SMALL:
        size = like[n].size
        out[n] = flat[at:at + size].reshape(like[n].shape)
        at += size
    return out


def _own_orientation(name, value):
    return value[0].T if name in _COLUMN_SHARDED else value[0]


def _reference_orientation(name, value):
    return (value.T if name in _COLUMN_SHARDED else value)[None]


def _full_from_gathered(gathered):
    return gathered.reshape(N_DEV * gathered.shape[1], gathered.shape[2])


def _sliced_for_devices(grad):
    return grad.reshape(N_DEV, grad.shape[0] // N_DEV, grad.shape[1])


def _rope_inv_freq():
    half = HEAD_DIM // 2
    inv = ROPE_THETA ** (-jnp.arange(half, dtype=F32) / half)
    return jnp.tile(inv, 128 // half)[None, :]


def _forward_backward(x, p, target, small, shards):
    def gather(*names):
        return [("gather", shards[n]) for n in names]

    def scatter(**grads):
        return [("scatter", _sliced_for_devices(g)) for g in grads.values()]

    full, parts = {}, {}
    s_len = x.shape[0]
    cos, sin, got = _gather_two_level_with_rope_tables(shards["w_in"], _rope_inv_freq(), s_len, "gather_w_in")
    full["w_in"] = _full_from_gathered(got)

    g_mix, g_ffn, g_ple = small["mix_norm_g"], small["ffn_norm_g"], small["ple_norm_g"]
    g_fin = small["final_norm_g"].reshape(1, D_MODEL)
    sw, gs, ga, gb = small["sgu_w"], small["sgu_norm_g"], small["out_norm_a"], small["out_norm_b"]
    b2 = jnp.repeat(small["sgu_b"].T, HEAD_DIM, axis=1)
    head_sum = (jnp.arange(WIDTH_B)[:, None] // HEAD_DIM == jnp.arange(128)[None, :]).astype(BF16)
    n_br = len(DILATIONS)

    def arrived(names, got):
        for n, g in zip(names, got):
            full[n] = _full_from_gathered(g)

    (ua, hn1, *qkv), got = _inproj(x, g_mix, full["w_in"], cos, sin, exchange=gather("w_gate"))
    arrived(("w_gate",), got)
    ya_n = _sgu_fwd(ua, sw, b2, gs, ga)
    half = shards["w_up"].shape[0] // 2
    riders = [[("gather", shards["w_up"][:half])], [("gather", shards["w_up"][half:])], gather("w_out")]
    branch, got = [], []
    for i, d in enumerate(DILATIONS):
        o_l, g = _attn_fwd(qkv[i], d, exchange=riders[i])
        branch.append(o_l)
        got += g
    arrived(("w_up", "w_out"), [jnp.concatenate(got[:2], axis=1), got[2]])
    (y, yb, *lse), _ = _combine([o for o, _ in branch], [l for _, l in branch], ya_n, gb, head_sum.T)
    last_wave = ("w_down", "w_ple_gate", "w_ple_proj")
    (h1, a, b, act, hn2), got = _ffn_up(y, full["w_out"], x, g_ffn, full["w_gate"], full["w_up"], exchange=gather(*last_wave))
    arrived(last_wave, got)
    h2, h3, gate, pp, hn3 = _ffn_down_ple(act, full["w_down"], h1, g_ple, full["w_ple_gate"], p, full["w_ple_proj"])

    dh2, loss, d_fin, d_ple, g_ple_gate, g_ple_proj = _loss_ple_bwd(
        h3, target, g_fin, gate, pp, h2, g_ple, full["w_ple_gate"], hn3, p)
    g_down, (parts["w_ple_gate"], parts["w_ple_proj"]) = _mm_tn(
        act, dh2, "dw_down", exchange=scatter(w_ple_gate=g_ple_gate, w_ple_proj=g_ple_proj))
    (da, db), (parts["w_down"],) = _ffn_down_bwd(dh2, full["w_down"], a, b, exchange=scatter(w_down=g_down))
    g_gate, _ = _mm_tn(da, hn2, "dw_gate")
    g_up, _ = _mm_tn(db, hn2, "dw_up")
    (dh1, d_ffn, g_out), (parts["w_gate"],) = _mm_norm_bwd(
        [(da, full["w_gate"]), (db, full["w_up"])], h1, g_ffn, dh2, "ffn_up_bwd", exchange=scatter(w_gate=g_gate), dw_lhs=y)
    (dya_n, d_gb, *do_dd), (parts["w_out"],) = _outproj_bwd(dh1, full["w_out"], yb, gb, head_sum, exchange=scatter(w_out=g_out))
    grads_b = []
    for i, d in enumerate(DILATIONS):
        g3, got = _attn_bwd(qkv[i], do_dd[i], lse[i], do_dd[n_br + i], d, exchange=scatter(w_up=g_up) if i == 0 else ())
        grads_b.append(g3)
        if i == 0:
            (parts["w_up"],) = got
    dua, d_sw, d_b2, d_gs, d_ga = _sgu_bwd(ua, sw, b2, gs, ga, dya_n)
    early = {
        "sgu_w": d_sw, "sgu_b": d_b2.reshape(CHUNK, 4, HEAD_DIM).sum(axis=-1).T, "sgu_norm_g": d_gs, "out_norm_a": d_ga,
        "out_norm_b": d_gb, "ffn_norm_g": d_ffn, "ple_norm_g": d_ple, "final_norm_g": d_fin,
    }
    dproj, g_in, (early_parts,) = _dproj(
        dua, [g[0] for g in grads_b], [g[1] for g in grads_b], [g[2] for g in grads_b], cos, sin, hn1,
        exchange=[("gather", _pack_small(early, _SMALL[1:]))])
    (dx, d_mix), (parts["w_in"],) = _mm_norm_bwd(
        [(dproj, full["w_in"])], x, g_mix, dh1, "inproj_bwd", exchange=scatter(w_in=g_in))
    late = jnp.concatenate([_pack_small({"mix_norm_g": d_mix}, _SMALL[:1]), jnp.broadcast_to(loss, (8, 128))])
    (late_parts,) = _exchange_only([("gather", late)], "gather_mix_norm_grad_and_loss")
    total_loss = jnp.sum(late_parts[:, 8, 0])
    return total_loss, dx, parts, jnp.concatenate([late_parts[:, :8], early_parts], axis=1)


def kernel(x, p, mix_norm_g, w_in, sgu_w, sgu_b, sgu_norm_g, out_norm_a, out_norm_b, w_out, ffn_norm_g, w_gate, w_up, w_down, ple_norm_g, w_ple_gate, w_ple_proj, final_norm_g, loss_target, m_mix_norm_g, m_w_in, m_sgu_w, m_sgu_b, m_sgu_norm_g, m_out_norm_a, m_out_norm_b, m_w_out, m_ffn_norm_g, m_w_gate, m_w_up, m_w_down, m_ple_norm_g, m_w_ple_gate, m_w_ple_proj, m_final_norm_g, v_mix_norm_g, v_w_in, v_sgu_w, v_sgu_b, v_sgu_norm_g, v_out_norm_a, v_out_norm_b, v_w_out, v_ffn_norm_g, v_w_gate, v_w_up, v_w_down, v_ple_norm_g, v_w_ple_gate, v_w_ple_proj, v_final_norm_g):
    given = dict(locals())
    weights = {n: given[n] for n in _ORDER}
    moments_m = {n: given["m_" + n] for n in _ORDER}
    moments_v = {n: given["v_" + n] for n in _ORDER}

    shards = {n: _own_orientation(n, weights[n]).astype(BF16) for n in _BIG}
    small = {n: (weights[n][0] if n in ("sgu_w", "sgu_b") else weights[n]) for n in _SMALL}

    loss, dx, parts, small_parts = _forward_backward(x[0], p[0, 0], loss_target[0], small, shards)

    small_like = {n: weights[n] for n in _SMALL}
    grads, deltas, new_m, new_v = {}, {}, {}, {}
    for n in _BIG:
        outs = _adamw(parts[n], _own_orientation(n, weights[n]), _own_orientation(n, moments_m[n]),
                      _own_orientation(n, moments_v[n]), "adamw_" + n)
        grads[n], deltas[n], new_m[n], new_v[n] = [_reference_orientation(n, o) for o in outs]
    g, d, nm, nv = _adamw(small_parts, _pack_small(small_like), _pack_small({n: moments_m[n] for n in _SMALL}),
                          _pack_small({n: moments_v[n] for n in _SMALL}), "adamw_small")
    for out, packed in ((grads, g), (deltas, d), (new_m, nm), (new_v, nv)):
        out.update(_unpack_small(packed, small_like))

    return (loss, dx[None], *[grads[n] for n in _ORDER], *[deltas[n] for n in _ORDER],
            *[new_m[n] for n in _ORDER], *[new_v[n] for n in _ORDER])
```

```python
import functools

import jax
import jax.numpy as jnp
from jax import lax
from jax.experimental import pallas as pl
from jax.experimental.pallas import tpu as pltpu

F32 = jnp.float32
BF16 = jnp.bfloat16

D_MODEL = 1024
WIDTH_A = 256
WIDTH_B = 768
D_FF = 2816
IN_COLS = 2 * WIDTH_A + 3 * WIDTH_B
PLE_DIM = 256
HEAD_DIM = 64
N_PAIRS = WIDTH_B // 128
CHUNK = 128
N_BACK = 128
DILATIONS = (1, 4, 16)
ROPE_THETA = 10000.0
EPS = 1e-6
N_DEV = 8

ADAM_LR = 0.001
ADAM_B1 = 0.9
ADAM_B2 = 0.999
ADAM_EPS = 1e-08
ADAM_WD = 0.01
ADAM_STEP = 10

V7X_VMEM_LIMIT_BYTES = 56 * 1024 * 1024
ROW_TILE = 512
MESH = pl.DeviceIdType.MESH
NEG = -1e30

_NT = (((1,), (1,)), ((), ()))
_TN = (((0,), (0,)), ((), ()))


def _params(*semantics):
    return pltpu.CompilerParams(dimension_semantics=semantics, vmem_limit_bytes=V7X_VMEM_LIMIT_BYTES)


def _rows(tm, width):
    return pl.BlockSpec((tm, width), lambda i: (i, 0))


def _whole(shape):
    return pl.BlockSpec(shape, lambda *_: (0,) * len(shape))


def _resident(shape):
    return pl.BlockSpec(shape, lambda *_: (0,) * len(shape), pipeline_mode=pl.Buffered(1))


def _gelu(x):
    t = jnp.tanh(0.7978845608028654 * (x + 0.044715 * (x * x * x)))
    return 0.5 * x * (1.0 + t)


def _gelu_grad(x):
    t = jnp.tanh(0.7978845608028654 * (x + 0.044715 * (x * x * x)))
    return 0.5 * (1.0 + t) + 0.5 * x * (1.0 - t * t) * (0.7978845608028654 * (1.0 + 3.0 * 0.044715 * (x * x)))


def _rstd(x):
    return lax.rsqrt(jnp.mean(x * x, axis=-1, keepdims=True) + EPS)


def _norm_bwd(dn, h, g):
    r = _rstd(h)
    n = h * r
    t = dn * g
    return r * (t - n * jnp.mean(t * n, axis=-1, keepdims=True)), dn * n


def _swap_halves(x, first_half):
    return jnp.where(first_half, pltpu.roll(x, 96, 1), pltpu.roll(x, 32, 1))


def _sub_spec(d, n_cb, tm):
    return pl.BlockSpec((d, n_cb, tm // d, 128), lambda i: (0, 0, i, 0))


def _sub_shape(s_len, d, n_cb, dtype):
    return jax.ShapeDtypeStruct((d, n_cb, s_len // d, 128), dtype)


def _to_sub(stage_ref, cb_src, out_ref, cb_dst, d, tm):
    slab = stage_ref.at[cb_src]
    for r in range(d):
        out_ref[r, cb_dst] = slab[pl.ds(r, tm // d, stride=d), :].astype(out_ref.dtype)


def _from_sub(in_ref, cb_src, stage_ref, cb_dst, d, tm):
    slab = stage_ref.at[cb_dst]
    for r in range(d):
        slab[pl.ds(r, tm // d, stride=d), :] = in_ref[r, cb_src].astype(F32)


def _inproj(x, g, w, cos, sin, exchange=()):
    s_len = x.shape[0]
    tm = ROW_TILE
    n_cb = 3 * N_PAIRS

    def body(x_ref, g_ref, w_ref, cos_ref, sin_ref, ua_ref, hn_ref, *rest):
        sub_refs, stage = rest[:-1], rest[-1]
        xf = x_ref[...]
        hn = (xf * _rstd(xf) * g_ref[...]).astype(BF16)
        hn_ref[...] = hn
        c = cos_ref[...]
        s = sin_ref[...]
        first_half = (lax.broadcasted_iota(jnp.int32, (tm, 128), 1) % HEAD_DIM) < HEAD_DIM // 2
        for j in range(IN_COLS // 256):
            col = j * 256
            acc = lax.dot_general(hn, w_ref[col:col + 256, :], _NT, preferred_element_type=F32)
            if col < 2 * WIDTH_A:
                ua_ref[:, col:col + 256] = acc
                continue
            for half in range(2):
                cb = (col - 2 * WIDTH_A) // 128 + half
                t = acc[:, half * 128:(half + 1) * 128]
                if cb < 2 * N_PAIRS:
                    t = (t * c + _swap_halves(t, first_half) * s) * (0.125 if cb < N_PAIRS else 1.0)
                stage[cb] = t
                sub_refs[0][0, cb] = t.astype(BF16)
        for cb in range(n_cb):
            for d, out_ref in zip(DILATIONS[1:], sub_refs[1:]):
                _to_sub(stage, cb, out_ref, cb, d, tm)

    return _call(
        body, name="inproj", grid=(s_len // tm,),
        in_specs=[_rows(tm, D_MODEL), _whole((1, D_MODEL)), _resident((IN_COLS, D_MODEL)), _rows(tm, 128), _rows(tm, 128)],
        out_specs=[_rows(tm, 2 * WIDTH_A), _rows(tm, D_MODEL)] + [_sub_spec(d, n_cb, tm) for d in DILATIONS],
        out_shape=[jax.ShapeDtypeStruct((s_len, 2 * WIDTH_A), F32), jax.ShapeDtypeStruct((s_len, D_MODEL), BF16)]
        + [_sub_shape(s_len, d, n_cb, BF16) for d in DILATIONS],
        scratch_shapes=[pltpu.VMEM((n_cb, tm, 128), F32)],
        semantics=("parallel",), args=(x, g, w, cos, sin), exchange=exchange)


def _sgu_mix_weights(sw_ref):
    keep = lax.broadcasted_iota(jnp.int32, (CHUNK, CHUNK), 0) >= lax.broadcasted_iota(jnp.int32, (CHUNK, CHUNK), 1)
    return [jnp.where(keep, sw_ref[h], 0.0).astype(BF16) for h in range(4)], keep


def _sgu_core(ua_ref, gs_ref):
    u = ua_ref[:, :WIDTH_A]
    va = ua_ref[:, WIDTH_A:]
    vg = _gelu(va)
    xc = vg - jnp.mean(vg, axis=-1, keepdims=True)
    rstd = lax.rsqrt(jnp.mean(xc * xc, axis=-1, keepdims=True) + EPS)
    xhat = xc * rstd
    return u, va, _gelu(u), xhat, rstd, xhat * gs_ref[...]


def _sgu_fwd(ua, sw, b2, gs, ga):
    s_len = ua.shape[0]
    tm = ROW_TILE

    def body(ua_ref, sw_ref, b2_ref, gs_ref, ga_ref, out_ref):
        _, _, ug, _, _, vn = _sgu_core(ua_ref, gs_ref)
        wm, _ = _sgu_mix_weights(sw_ref)
        head = lax.broadcasted_iota(jnp.int32, (CHUNK, WIDTH_A), 1) // HEAD_DIM
        for c in range(tm // CHUNK):
            rows = slice(c * CHUNK, (c + 1) * CHUNK)
            vnc = vn[rows]
            mixed = b2_ref[...]
            for h in range(4):
                mixed = mixed + jnp.dot(wm[h], jnp.where(head == h, vnc, 0.0).astype(BF16), preferred_element_type=F32)
            ya = ug[rows] * mixed
            out_ref[rows, :] = (ya * _rstd(ya) * ga_ref[...]).astype(BF16)

    return pl.pallas_call(
        body, name="sgu_fwd", grid=(s_len // tm,),
        in_specs=[_rows(tm, 2 * WIDTH_A), _whole((4, CHUNK, CHUNK)), _whole((CHUNK, WIDTH_A)), _whole((1, WIDTH_A)), _whole((1, WIDTH_A))],
        out_specs=_rows(tm, WIDTH_A),
        out_shape=jax.ShapeDtypeStruct((s_len, WIDTH_A), BF16),
        compiler_params=_params("parallel"),
    )(ua, sw, b2, gs, ga)


def _attn_geometry(sd):
    tile = min(2 * ROW_TILE, sd)
    return tile, tile // CHUNK, sd // tile


PAIRS_PER_STEP = 6


def _attn_spec(cb0, rows, row_index):
    return pl.BlockSpec((None, PAIRS_PER_STEP, rows, 128), lambda r, g, n: (r, cb0 // PAIRS_PER_STEP + g, row_index(n), 0))


assert PAIRS_PER_STEP == N_PAIRS
assert DILATIONS[0] == 1


def _stats_spec(rows, row_index):
    return pl.BlockSpec((None, None, rows, 128), lambda r, g, n: (r, 0, row_index(n), 0))


def _stats_shape(sd, d):
    return jax.ShapeDtypeStruct((d, 1, sd, 128), F32)


def _both_heads(x, head_a):
    zero = jnp.zeros_like(x)
    return [jnp.where(head_a, x, zero), jnp.where(head_a, zero, x)]


def _attn_fwd(qkv, d, exchange=()):
    sd = qkv.shape[2]
    tile, nb, n_tiles = _attn_geometry(sd)

    def prev(n):
        return jnp.maximum(n * nb - 1, 0)

    def body(q_ref, k_ref, kp_ref, v_ref, vp_ref, o_ref, l_ref):
        for hp in range(PAIRS_PER_STEP):
            one_pair(hp, q_ref.at[hp], k_ref.at[hp], kp_ref.at[hp], v_ref.at[hp], vp_ref.at[hp], o_ref.at[hp], l_ref)

    def one_pair(hp, q_ref, k_ref, kp_ref, v_ref, vp_ref, o_ref, l_ref):
        n = pl.program_id(2)
        lane = lax.broadcasted_iota(jnp.int32, (CHUNK, 128), 1)
        head_a = lane < HEAD_DIM
        qi = lax.broadcasted_iota(jnp.int32, (2 * CHUNK, 2 * CHUNK), 0) % CHUNK
        kc = lax.broadcasted_iota(jnp.int32, (2 * CHUNK, 2 * CHUNK), 1)
        band = (kc >= qi) & (kc <= qi + N_BACK)
        for j in range(nb):
            rows = slice(j * CHUNK, (j + 1) * CHUNK)
            if j == 0:
                kcat = jnp.concatenate([kp_ref[...], k_ref[rows, :]], axis=0)
                vcat = jnp.concatenate([vp_ref[...], v_ref[rows, :]], axis=0)
                valid = band & jnp.logical_or(n > 0, kc >= CHUNK)
            else:
                kcat = k_ref[(j - 1) * CHUNK:(j + 1) * CHUNK, :]
                vcat = v_ref[(j - 1) * CHUNK:(j + 1) * CHUNK, :]
                valid = band
            q2 = jnp.concatenate(_both_heads(q_ref[rows, :], head_a), axis=0)
            s = lax.dot_general(q2, kcat, _NT, preferred_element_type=F32)
            s = jnp.where(valid, s, NEG)
            m = jnp.max(s, axis=-1, keepdims=True)
            p = jnp.exp(s - m)
            l = jnp.sum(p, axis=-1, keepdims=True)
            o2 = jnp.dot(p.astype(BF16), vcat, preferred_element_type=F32) / l
            lse2 = m + jnp.log(l)
            o_ref[rows, :] = jnp.where(head_a, o2[:CHUNK], o2[CHUNK:]).astype(BF16)
            others = l_ref[rows, :] if hp > 0 else jnp.zeros((CHUNK, 128), F32)
            l_ref[rows, :] = jnp.where(lane == 2 * hp, lse2[:CHUNK], jnp.where(lane == 2 * hp + 1, lse2[CHUNK:], others))

    same = lambda n: n
    return _call(
        body, name=f"attn_fwd_d{d}", grid=(d, N_PAIRS // PAIRS_PER_STEP, n_tiles),
        in_specs=[_attn_spec(0, tile, same), _attn_spec(N_PAIRS, tile, same), _attn_spec(N_PAIRS, CHUNK, prev),
                  _attn_spec(2 * N_PAIRS, tile, same), _attn_spec(2 * N_PAIRS, CHUNK, prev)],
        out_specs=[_attn_spec(0, tile, same), _stats_spec(tile, same)],
        out_shape=[jax.ShapeDtypeStruct((d, N_PAIRS, sd, 128), BF16), _stats_shape(sd, d)],
        semantics=("parallel", "parallel", "parallel"), args=(qkv, qkv, qkv, qkv, qkv), exchange=exchange)


def _combine(outs, lses, ya_n, gb, head_spread, exchange=()):
    s_len = ya_n.shape[0]
    tm = ROW_TILE
    n_br = len(DILATIONS)

    def body(*refs):
        o_refs, l_refs = refs[:n_br], refs[n_br:2 * n_br]
        ya_ref, gb_ref, spread_ref, y_ref, yb_ref = refs[2 * n_br:2 * n_br + 5]
        lse_refs = refs[2 * n_br + 5:3 * n_br + 5]
        o_nat, l_nat, lse_nat, w_wide = refs[3 * n_br + 5:]
        for i, d in enumerate(DILATIONS):
            _from_sub(l_refs[i], 0, l_nat, i, d, tm)
        ls = [l_nat[i] for i in range(n_br)]
        top = jnp.maximum(jnp.maximum(ls[0], ls[1]), ls[2])
        ws = [jnp.exp(l - top) for l in ls]
        den = ws[0] + ws[1] + ws[2]
        inv = 1.0 / den
        for i in range(n_br):
            w = ws[i] * inv
            hi = w.astype(BF16)
            lo = (w - hi.astype(F32)).astype(BF16)
            w_wide[i] = (jnp.dot(hi, spread_ref[...], preferred_element_type=F32)
                         + jnp.dot(lo, spread_ref[...], preferred_element_type=F32))
        lse_nat[0] = top + jnp.log(den)
        for d, lse_ref in zip(DILATIONS, lse_refs):
            _to_sub(lse_nat, 0, lse_ref, 0, d, tm)
        sumsq = jnp.zeros((tm, 1), F32)
        for cb in range(N_PAIRS):
            cols = slice(cb * 128, (cb + 1) * 128)
            yb = w_wide[0, :, cols] * o_refs[0][0, cb].astype(F32)
            for i, d in enumerate(DILATIONS[1:], start=1):
                _from_sub(o_refs[i], cb, o_nat, i, d, tm)
                yb = yb + w_wide[i, :, cols] * o_nat[i]
            yb_ref[:, cb * 128:(cb + 1) * 128] = yb
            sumsq = sumsq + jnp.sum(yb * yb, axis=-1, keepdims=True)
        r = lax.rsqrt(sumsq / WIDTH_B + EPS)
        y_ref[:, :WIDTH_A] = ya_ref[...]
        y_ref[:, WIDTH_A:] = (yb_ref[...] * r * gb_ref[...]).astype(BF16)

    stats = [_sub_spec(d, 1, tm) for d in DILATIONS]
    return _call(
        body, name="attn_combine", grid=(s_len // tm,),
        in_specs=[_sub_spec(d, N_PAIRS, tm) for d in DILATIONS] + stats
        + [_rows(tm, WIDTH_A), _whole((1, WIDTH_B)), _whole((128, WIDTH_B))],
        out_specs=[_rows(tm, D_MODEL), _rows(tm, WIDTH_B)] + stats,
        out_shape=[jax.ShapeDtypeStruct((s_len, D_MODEL), BF16), jax.ShapeDtypeStruct((s_len, WIDTH_B), F32)]
        + [_sub_shape(s_len, d, 1, F32) for d in DILATIONS],
        scratch_shapes=[pltpu.VMEM((n_br, tm, 128), F32), pltpu.VMEM((n_br, tm, 128), F32), pltpu.VMEM((1, tm, 128), F32),
                        pltpu.VMEM((n_br, tm, WIDTH_B), F32)],
        semantics=("parallel",), args=(*outs, *lses, ya_n, gb, head_spread), exchange=exchange)


def _ffn_up(y, wout, x, g, wg, wu, exchange=()):
    s_len = x.shape[0]
    tm = ROW_TILE

    def body(y_ref, wo_ref, x_ref, g_ref, wg_ref, wu_ref, h_ref, a_ref, b_ref, act_ref, hn_ref):
        hf = x_ref[...] + jnp.dot(y_ref[...], wo_ref[...], preferred_element_type=F32)
        h_ref[...] = hf
        hn = (hf * _rstd(hf) * g_ref[...]).astype(BF16)
        hn_ref[...] = hn
        for j in range(D_FF // 256):
            cols = slice(j * 256, (j + 1) * 256)
            a = lax.dot_general(hn, wg_ref[cols, :], _NT, preferred_element_type=F32)
            b = lax.dot_general(hn, wu_ref[cols, :], _NT, preferred_element_type=F32)
            a_ref[:, cols] = a.astype(BF16)
            b_ref[:, cols] = b.astype(BF16)
            act_ref[:, cols] = (a * jax.nn.sigmoid(a) * b).astype(BF16)

    wide = jax.ShapeDtypeStruct((s_len, D_FF), BF16)
    return _call(
        body, name="ffn_up", grid=(s_len // tm,),
        in_specs=[_rows(tm, D_MODEL), _resident((D_MODEL, D_MODEL)), _rows(tm, D_MODEL), _whole((1, D_MODEL)),
                  _resident((D_FF, D_MODEL)), _resident((D_FF, D_MODEL))],
        out_specs=[_rows(tm, D_MODEL), _rows(tm, D_FF), _rows(tm, D_FF), _rows(tm, D_FF), _rows(tm, D_MODEL)],
        out_shape=[jax.ShapeDtypeStruct((s_len, D_MODEL), F32), wide, wide, wide, jax.ShapeDtypeStruct((s_len, D_MODEL), BF16)],
        semantics=("parallel",), args=(y, wout, x, g, wg, wu), exchange=exchange)


def _ffn_down_ple(act, wd, h1, g, wpg, p, wpp):
    s_len = h1.shape[0]
    tm = ROW_TILE

    def body(act_ref, wd_ref, h1_ref, g_ref, wpg_ref, p_ref, wpp_ref, h2_ref, h3_ref, gate_ref, pp_ref, hn_ref):
        hf = h1_ref[...] + jnp.dot(act_ref[...], wd_ref[...], preferred_element_type=F32)
        h2_ref[...] = hf
        hn = (hf * _rstd(hf) * g_ref[...]).astype(BF16)
        hn_ref[...] = hn
        gate = jax.nn.sigmoid(jnp.dot(hn, wpg_ref[...], preferred_element_type=F32))
        pp = lax.dot_general(p_ref[...].astype(BF16), wpp_ref[...], _NT, preferred_element_type=F32)
        h3_ref[...] = hf + gate * pp
        gate_ref[...] = gate.astype(BF16)
        pp_ref[...] = pp.astype(BF16)

    full = jax.ShapeDtypeStruct((s_len, D_MODEL), F32)
    half = jax.ShapeDtypeStruct((s_len, D_MODEL), BF16)
    return pl.pallas_call(
        body, name="ffn_down_ple", grid=(s_len // tm,),
        in_specs=[_rows(tm, D_FF), _resident((D_FF, D_MODEL)), _rows(tm, D_MODEL), _whole((1, D_MODEL)),
                  _resident((D_MODEL, D_MODEL)), _rows(tm, PLE_DIM), _resident((D_MODEL, PLE_DIM))],
        out_specs=[_rows(tm, D_MODEL)] * 5,
        out_shape=[full, full, half, half, half],
        compiler_params=_params("parallel"),
    )(act, wd, h1, g, wpg, p, wpp)


def _loss_ple_bwd(h3, target, gf, gate, pp, h2, g_ple, wpg, hn3, p):
    s_len = h3.shape[0]
    tm = ROW_TILE
    n_steps = s_len // tm

    def body(h_ref, t_ref, g_ref, gate_ref, pp_ref, h2_ref, gp_ref, w_ref, hn_ref, p_ref,
             dh2_ref, loss_ref, dg_ref, dgp_ref, dwg_ref, dwp_ref, acc_g, acc_p):
        step = pl.program_id(0)

        @pl.when(step == 0)
        def _():
            loss_ref[...] = jnp.zeros_like(loss_ref)
            dg_ref[...] = jnp.zeros_like(dg_ref)
            dgp_ref[...] = jnp.zeros_like(dgp_ref)
            acc_g[...] = jnp.zeros_like(acc_g)
            acc_p[...] = jnp.zeros_like(acc_p)

        hf = h_ref[...]
        gfv = g_ref[...]
        err = hf * _rstd(hf) * gfv - t_ref[...]
        loss_ref[...] += 0.5 * jnp.sum(jnp.sum(err * err, axis=-1, keepdims=True), axis=0, keepdims=True) / D_MODEL
        dh, dg_rows = _norm_bwd(err / D_MODEL, hf, gfv)
        dg_ref[...] += jnp.sum(dg_rows, axis=0, keepdims=True)
        gate = gate_ref[...].astype(F32)
        dz = (dh * pp_ref[...].astype(F32) * gate * (1.0 - gate)).astype(BF16)
        dpp = (dh * gate).astype(BF16)
        dn = lax.dot_general(dz, w_ref[...], _NT, preferred_element_type=F32)
        dh2, dgp_rows = _norm_bwd(dn, h2_ref[...], gp_ref[...])
        dh2 = dh + dh2
        dh2_ref[...] = dh2
        dgp_ref[...] += jnp.sum(dgp_rows, axis=0, keepdims=True)
        acc_g[...] += lax.dot_general(hn_ref[...], dz, _TN, preferred_element_type=F32)
        acc_p[...] += lax.dot_general(dpp, p_ref[...].astype(BF16), _TN, preferred_element_type=F32)

        @pl.when(step == n_steps - 1)
        def _():
            dwg_ref[...] = acc_g[...].astype(BF16)
            dwp_ref[...] = acc_p[...].astype(BF16)

    gain = jax.ShapeDtypeStruct((1, D_MODEL), F32)
    return pl.pallas_call(
        body, name="loss_ple_bwd", grid=(n_steps,),
        in_specs=[_rows(tm, D_MODEL), _rows(tm, D_MODEL), _whole((1, D_MODEL)), _rows(tm, D_MODEL), _rows(tm, D_MODEL),
                  _rows(tm, D_MODEL), _whole((1, D_MODEL)), _resident((D_MODEL, D_MODEL)), _rows(tm, D_MODEL),
                  _rows(tm, PLE_DIM)],
        out_specs=[_rows(tm, D_MODEL), _whole((1, 128)), _whole((1, D_MODEL)), _whole((1, D_MODEL)),
                   _whole((D_MODEL, D_MODEL)), _whole((D_MODEL, PLE_DIM))],
        out_shape=[jax.ShapeDtypeStruct((s_len, D_MODEL), F32), jax.ShapeDtypeStruct((1, 128), F32), gain, gain,
                   jax.ShapeDtypeStruct((D_MODEL, D_MODEL), BF16), jax.ShapeDtypeStruct((D_MODEL, PLE_DIM), BF16)],
        scratch_shapes=[pltpu.VMEM((D_MODEL, D_MODEL), F32), pltpu.VMEM((D_MODEL, PLE_DIM), F32)],
        compiler_params=_params("arbitrary"),
    )(h3, target, gf, gate, pp, h2, g_ple, wpg, hn3, p)


def _mm_norm_bwd(parts, h, g, dres, name, exchange=(), dw_lhs=None):
    s_len = h.shape[0]
    tm = ROW_TILE
    n_parts = len(parts)
    n_steps = s_len // tm
    has_dw = dw_lhs is not None

    def body(*refs):
        a_refs = refs[0:2 * n_parts:2]
        w_refs = refs[1:2 * n_parts:2]
        h_ref, g_ref, r_ref = refs[2 * n_parts:2 * n_parts + 3]
        rest = refs[2 * n_parts + 3:]
        step = pl.program_id(0)
        if has_dw:
            lhs_ref, o_ref, dg_ref, dw_ref, acc_ref = rest
        else:
            o_ref, dg_ref = rest

        @pl.when(step == 0)
        def _():
            dg_ref[...] = jnp.zeros_like(dg_ref)
            if has_dw:
                acc_ref[...] = jnp.zeros_like(acc_ref)

        dn = jnp.dot(a_refs[0][...], w_refs[0][...], preferred_element_type=F32)
        for a_ref, w_ref in zip(a_refs[1:], w_refs[1:]):
            dn = dn + jnp.dot(a_ref[...], w_ref[...], preferred_element_type=F32)
        dh, dg_rows = _norm_bwd(dn, h_ref[...], g_ref[...])
        out = r_ref[...] + dh
        o_ref[...] = out
        dg_ref[...] += jnp.sum(dg_rows, axis=0, keepdims=True)
        if has_dw:
            acc_ref[...] += lax.dot_general(lhs_ref[...], out.astype(BF16), _TN, preferred_element_type=F32)

            @pl.when(step == n_steps - 1)
            def _():
                dw_ref[...] = acc_ref[...].astype(BF16)

    in_specs, args = [], []
    for a, w in parts:
        in_specs += [_rows(tm, a.shape[1]), _resident(w.shape)]
        args += [a, w]
    in_specs += [_rows(tm, D_MODEL), _whole((1, D_MODEL)), _rows(tm, D_MODEL)]
    args += [h, g, dres]
    out_specs = [_rows(tm, D_MODEL), _whole((1, D_MODEL))]
    out_shape = [jax.ShapeDtypeStruct((s_len, D_MODEL), F32), jax.ShapeDtypeStruct((1, D_MODEL), F32)]
    scratch = []
    if has_dw:
        m = dw_lhs.shape[1]
        in_specs.append(_rows(tm, m))
        args.append(dw_lhs)
        out_specs.append(_whole((m, D_MODEL)))
        out_shape.append(jax.ShapeDtypeStruct((m, D_MODEL), BF16))
        scratch.append(pltpu.VMEM((m, D_MODEL), F32))
    return _call(
        body, name=name, grid=(n_steps,), in_specs=in_specs, out_specs=out_specs, out_shape=out_shape,
        scratch_shapes=scratch, semantics=("arbitrary",), args=tuple(args), exchange=exchange)


def _ffn_down_bwd(dh, wdt, a, b, exchange=()):
    s_len = dh.shape[0]
    tm = ROW_TILE

    def body(dh_ref, w_ref, a_ref, b_ref, da_ref, db_ref):
        dhb = dh_ref[...].astype(BF16)
        for j in range(D_FF // 256):
            cols = slice(j * 256, (j + 1) * 256)
            dact = lax.dot_general(dhb, w_ref[cols, :], _NT, preferred_element_type=F32)
            av = a_ref[:, cols].astype(F32)
            bv = b_ref[:, cols].astype(F32)
            sig = jax.nn.sigmoid(av)
            t = dact * sig
            silu = av * sig
            da_ref[:, cols] = (t * bv * (1.0 + av - silu)).astype(BF16)
            db_ref[:, cols] = (dact * silu).astype(BF16)

    wide = jax.ShapeDtypeStruct((s_len, D_FF), BF16)
    return _call(
        body, name="ffn_down_bwd", grid=(s_len // tm,),
        in_specs=[_rows(tm, D_MODEL), _resident((D_FF, D_MODEL)), _rows(tm, D_FF), _rows(tm, D_FF)],
        out_specs=[_rows(tm, D_FF), _rows(tm, D_FF)],
        out_shape=[wide, wide],
        semantics=("parallel",), args=(dh, wdt, a, b), exchange=exchange)


def _outproj_bwd(dh1, woutt, yb, gb, head_sum, exchange=()):
    s_len = dh1.shape[0]
    tm = ROW_TILE
    n_br = len(DILATIONS)

    def body(dh_ref, w_ref, yb_ref, gb_ref, e_ref, dya_ref, dgb_ref, *rest):
        do_refs, dd_refs = rest[:n_br], rest[n_br:2 * n_br]
        do_nat, dd_nat = rest[2 * n_br:]

        @pl.when(pl.program_id(0) == 0)
        def _():
            dgb_ref[...] = jnp.zeros_like(dgb_ref)

        dhb = dh_ref[...].astype(BF16)
        dya_ref[...] = lax.dot_general(dhb, w_ref[:WIDTH_A, :], _NT, preferred_element_type=F32)
        dyn = lax.dot_general(dhb, w_ref[WIDTH_A:, :], _NT, preferred_element_type=F32)
        ybv = yb_ref[...]
        dyb, dg_rows = _norm_bwd(dyn, ybv, gb_ref[...])
        dgb_ref[...] += jnp.sum(dg_rows, axis=0, keepdims=True)
        prod = dyb * ybv
        hi = prod.astype(BF16)
        lo = (prod - hi.astype(F32)).astype(BF16)
        dd_nat[0] = (jnp.dot(hi, e_ref[...], preferred_element_type=F32)
                     + jnp.dot(lo, e_ref[...], preferred_element_type=F32))
        for i, d in enumerate(DILATIONS):
            _to_sub(dd_nat, 0, dd_refs[i], 0, d, tm)
        for cb in range(N_PAIRS):
            piece = dyb[:, cb * 128:(cb + 1) * 128]
            do_nat[cb] = piece
            do_refs[0][0, cb] = piece.astype(BF16)
            for i, d in enumerate(DILATIONS[1:], start=1):
                _to_sub(do_nat, cb, do_refs[i], cb, d, tm)

    return _call(
        body, name="outproj_bwd", grid=(s_len // tm,),
        in_specs=[_rows(tm, D_MODEL), _resident((D_MODEL, D_MODEL)), _rows(tm, WIDTH_B), _whole((1, WIDTH_B)), _whole((WIDTH_B, 128))],
        out_specs=[_rows(tm, WIDTH_A), _whole((1, WIDTH_B))] + [_sub_spec(d, N_PAIRS, tm) for d in DILATIONS]
        + [_sub_spec(d, 1, tm) for d in DILATIONS],
        out_shape=[jax.ShapeDtypeStruct((s_len, WIDTH_A), F32), jax.ShapeDtypeStruct((1, WIDTH_B), F32)]
        + [_sub_shape(s_len, d, N_PAIRS, BF16) for d in DILATIONS] + [_sub_shape(s_len, d, 1, F32) for d in DILATIONS],
        scratch_shapes=[pltpu.VMEM((N_PAIRS, tm, 128), F32), pltpu.VMEM((1, tm, 128), F32)],
        semantics=("arbitrary",), args=(dh1, woutt, yb, gb, head_sum), exchange=exchange)


def _attn_bwd(qkv, do, lse, dd, d, exchange=()):
    sd = qkv.shape[2]
    tile, nb, n_tiles = _attn_geometry(sd)
    last_block = sd // CHUNK - 1

    def nxt(n):
        return jnp.minimum((n + 1) * nb, last_block)

    def block(ref, next_ref, j):
        return ref[j * CHUNK:(j + 1) * CHUNK, :] if j < nb else next_ref[...]

    def body(q_ref, qn_ref, k_ref, v_ref, do_ref, don_ref, l_ref, ln_ref, dd_ref, ddn_ref,
             dq_ref, dk_ref, dv_ref, carry_ref):
        l_t = [block(l_ref, ln_ref, j).T for j in range(nb + 1)]
        dd_t = [block(dd_ref, ddn_ref, j).T for j in range(nb + 1)]
        for hp in range(PAIRS_PER_STEP):
            l_rows = [jnp.concatenate([t[2 * hp:2 * hp + 1, :], t[2 * hp + 1:2 * hp + 2, :]], axis=1) for t in l_t]
            dd_rows = [jnp.concatenate([t[2 * hp:2 * hp + 1, :], t[2 * hp + 1:2 * hp + 2, :]], axis=1) for t in dd_t]
            one_pair(q_ref.at[hp], qn_ref.at[hp], k_ref.at[hp], v_ref.at[hp], do_ref.at[hp], don_ref.at[hp],
                     l_rows, dd_rows, dq_ref.at[hp], dk_ref.at[hp], dv_ref.at[hp], carry_ref.at[hp])

    def one_pair(q_ref, qn_ref, k_ref, v_ref, do_ref, don_ref, l_rows, dd_rows, dq_ref, dk_ref, dv_ref, carry_ref):
        n = pl.program_id(2)

        @pl.when(n == 0)
        def _():
            carry_ref[...] = jnp.zeros_like(carry_ref)

        head_a = lax.broadcasted_iota(jnp.int32, (CHUNK, 128), 1) < HEAD_DIM
        col = lax.broadcasted_iota(jnp.int32, (CHUNK, 4 * CHUNK), 1)
        qi = col % CHUNK
        ki = lax.broadcasted_iota(jnp.int32, (CHUNK, 4 * CHUNK), 0)
        is_after = col >= 2 * CHUNK
        mask = (is_after & (ki >= qi)) | (jnp.logical_not(is_after) & (qi >= ki))
        mask_last = mask & jnp.logical_or(jnp.logical_not(is_after), n < n_tiles - 1)
        dq_acc = [carry_ref[...]] + [jnp.zeros((CHUNK, 128), F32) for _ in range(nb)]

        q_st = [jnp.concatenate(_both_heads(block(q_ref, qn_ref, j), head_a), axis=0) for j in range(nb + 1)]
        do_st = [jnp.concatenate(_both_heads(block(do_ref, don_ref, j), head_a), axis=0) for j in range(nb + 1)]

        for j in range(nb):
            rows = slice(j * CHUNK, (j + 1) * CHUNK)
            kj = k_ref[rows, :]
            vj = v_ref[rows, :]
            msk = mask if j + 1 < nb else mask_last
            qs = jnp.concatenate([q_st[j], q_st[j + 1]], axis=0)
            dos = jnp.concatenate([do_st[j], do_st[j + 1]], axis=0)
            ls = jnp.concatenate([l_rows[j], l_rows[j + 1]], axis=1)
            dds = jnp.concatenate([dd_rows[j], dd_rows[j + 1]], axis=1)
            st = lax.dot_general(kj, qs, _NT, preferred_element_type=F32)
            pt = jnp.exp(jnp.where(msk, st - ls, NEG))
            dpt = lax.dot_general(vj, dos, _NT, preferred_element_type=F32)
            dst = (pt * (dpt - dds)).astype(BF16)
            dv_ref[rows, :] = jnp.dot(pt.astype(BF16), dos, preferred_element_type=F32).astype(BF16)
            dk_ref[rows, :] = jnp.dot(dst, qs, preferred_element_type=F32).astype(BF16)
            dqs = lax.dot_general(dst, kj, _TN, preferred_element_type=F32)
            dq_acc[j] = dq_acc[j] + jnp.where(head_a, dqs[:CHUNK], dqs[CHUNK:2 * CHUNK])
            dq_acc[j + 1] = dq_acc[j + 1] + jnp.where(head_a, dqs[2 * CHUNK:3 * CHUNK], dqs[3 * CHUNK:])
        for j in range(nb):
            dq_ref[j * CHUNK:(j + 1) * CHUNK, :] = dq_acc[j].astype(BF16)
        carry_ref[...] = dq_acc[nb]

    same = lambda n: n
    grad = jax.ShapeDtypeStruct((d, N_PAIRS, sd, 128), BF16)
    return _call(
        body, name=f"attn_bwd_d{d}", grid=(d, N_PAIRS // PAIRS_PER_STEP, n_tiles),
        in_specs=[_attn_spec(0, tile, same), _attn_spec(0, CHUNK, nxt), _attn_spec(N_PAIRS, tile, same),
                  _attn_spec(2 * N_PAIRS, tile, same), _attn_spec(0, tile, same), _attn_spec(0, CHUNK, nxt),
                  _stats_spec(tile, same), _stats_spec(CHUNK, nxt), _stats_spec(tile, same), _stats_spec(CHUNK, nxt)],
        out_specs=[_attn_spec(0, tile, same)] * 3,
        out_shape=[grad, grad, grad],
        scratch_shapes=[pltpu.VMEM((PAIRS_PER_STEP, CHUNK, 128), F32)],
        semantics=("parallel", "parallel", "arbitrary"), args=(qkv, qkv, qkv, qkv, do, do, lse, lse, dd, dd), exchange=exchange)


def _sgu_bwd(ua, sw, b2, gs, ga, dya_n):
    s_len = ua.shape[0]
    tm = ROW_TILE

    def body(ua_ref, sw_ref, b2_ref, gs_ref, ga_ref, dy_ref, dua_ref, dsw_ref, db2_ref, dgs_ref, dga_ref):
        @pl.when(pl.program_id(0) == 0)
        def _():
            dsw_ref[...] = jnp.zeros_like(dsw_ref)
            db2_ref[...] = jnp.zeros_like(db2_ref)
            dgs_ref[...] = jnp.zeros_like(dgs_ref)
            dga_ref[...] = jnp.zeros_like(dga_ref)

        u, va, ug, xhat, rstd, vn = _sgu_core(ua_ref, gs_ref)
        wm, keep = _sgu_mix_weights(sw_ref)
        head = lax.broadcasted_iota(jnp.int32, (CHUNK, WIDTH_A), 1) // HEAD_DIM
        gav = ga_ref[...]
        gsv = gs_ref[...]
        dga = jnp.zeros((1, WIDTH_A), F32)
        dgs = jnp.zeros((1, WIDTH_A), F32)
        db2 = jnp.zeros((CHUNK, WIDTH_A), F32)
        dsw = [jnp.zeros((CHUNK, CHUNK), F32) for _ in range(4)]
        for c in range(tm // CHUNK):
            rows = slice(c * CHUNK, (c + 1) * CHUNK)
            vnc = vn[rows]
            vnb = vnc.astype(BF16)
            mixed = b2_ref[...]
            for h in range(4):
                mixed = mixed + jnp.dot(wm[h], jnp.where(head == h, vnc, 0.0).astype(BF16), preferred_element_type=F32)
            ugc = ug[rows]
            dya, dga_rows = _norm_bwd(dy_ref[rows, :], ugc * mixed, gav)
            dga = dga + jnp.sum(dga_rows, axis=0, keepdims=True)
            dmixed = dya * ugc
            db2 = db2 + dmixed
            dvn = jnp.zeros((CHUNK, WIDTH_A), F32)
            for h in range(4):
                dmh = jnp.where(head == h, dmixed, 0.0).astype(BF16)
                dsw[h] = dsw[h] + lax.dot_general(dmh, vnb, _NT, preferred_element_type=F32)
                dvn = dvn + lax.dot_general(wm[h], dmh, _TN, preferred_element_type=F32)
            xh = xhat[rows]
            dgs = dgs + jnp.sum(dvn * xh, axis=0, keepdims=True)
            dxh = dvn * gsv
            dvg = rstd[rows] * (dxh - jnp.mean(dxh, axis=-1, keepdims=True) - xh * jnp.mean(dxh * xh, axis=-1, keepdims=True))
            dua_ref[rows, :WIDTH_A] = (dya * mixed * _gelu_grad(u[rows])).astype(BF16)
            dua_ref[rows, WIDTH_A:] = (dvg * _gelu_grad(va[rows])).astype(BF16)
        for h in range(4):
            dsw_ref[h] += jnp.where(keep, dsw[h], 0.0)
        db2_ref[...] += db2
        dgs_ref[...] += dgs
        dga_ref[...] += dga

    return pl.pallas_call(
        body, name="sgu_bwd", grid=(s_len // tm,),
        in_specs=[_rows(tm, 2 * WIDTH_A), _whole((4, CHUNK, CHUNK)), _whole((CHUNK, WIDTH_A)), _whole((1, WIDTH_A)),
                  _whole((1, WIDTH_A)), _rows(tm, WIDTH_A)],
        out_specs=[_rows(tm, 2 * WIDTH_A), _whole((4, CHUNK, CHUNK)), _whole((CHUNK, WIDTH_A)), _whole((1, WIDTH_A)), _whole((1, WIDTH_A))],
        out_shape=[jax.ShapeDtypeStruct((s_len, 2 * WIDTH_A), BF16), jax.ShapeDtypeStruct((4, CHUNK, CHUNK), F32),
                   jax.ShapeDtypeStruct((CHUNK, WIDTH_A), F32), jax.ShapeDtypeStruct((1, WIDTH_A), F32),
                   jax.ShapeDtypeStruct((1, WIDTH_A), F32)],
        compiler_params=_params("arbitrary"),
    )(ua, sw, b2, gs, ga, dya_n)


def _dproj(dua, dqs, dks, dvs, cos, sin, hn1, exchange=()):
    s_len = dua.shape[0]
    tm = ROW_TILE
    n_br = len(DILATIONS)
    n_steps = s_len // tm

    def body(dua_ref, *rest):
        groups = [rest[g * n_br:(g + 1) * n_br] for g in range(3)]
        cos_ref, sin_ref, hn_ref, out_ref, dw_ref, acc, dw_acc = rest[3 * n_br:]
        step = pl.program_id(0)

        @pl.when(step == 0)
        def _():
            dw_acc[...] = jnp.zeros_like(dw_acc)

        out_ref[:, :2 * WIDTH_A] = dua_ref[...]
        c = cos_ref[...]
        s = sin_ref[...]
        first_half = (lax.broadcasted_iota(jnp.int32, (tm, 128), 1) % HEAD_DIM) < HEAD_DIM // 2
        for g, refs in enumerate(groups):
            for cb in range(N_PAIRS):
                t = refs[0][0, cb].astype(F32)
                for i, d in enumerate(DILATIONS[1:]):
                    _from_sub(refs[i + 1], cb, acc, i, d, tm)
                    t = t + acc[i]
                if g < 2:
                    t = (t * c - _swap_halves(t, first_half) * s) * (0.125 if g == 0 else 1.0)
                col = 2 * WIDTH_A + g * WIDTH_B + cb * 128
                out_ref[:, col:col + 128] = t.astype(BF16)
        hn = hn_ref[...]
        for j in range(IN_COLS // 256):
            cols = slice(j * 256, (j + 1) * 256)
            dw_acc[cols, :] += lax.dot_general(out_ref[:, cols], hn, _TN, preferred_element_type=F32)

        @pl.when(step == n_steps - 1)
        def _():
            dw_ref[...] = dw_acc[...].astype(BF16)

    subs = [_sub_spec(d, N_PAIRS, tm) for d in DILATIONS]
    (dproj, dw), received = _call(
        body, name="dproj_dw_in", grid=(n_steps,),
        in_specs=[_rows(tm, 2 * WIDTH_A)] + subs * 3 + [_rows(tm, 128), _rows(tm, 128), _rows(tm, D_MODEL)],
        out_specs=[_rows(tm, IN_COLS), _whole((IN_COLS, D_MODEL))],
        out_shape=[jax.ShapeDtypeStruct((s_len, IN_COLS), BF16), jax.ShapeDtypeStruct((IN_COLS, D_MODEL), BF16)],
        scratch_shapes=[pltpu.VMEM((n_br - 1, tm, 128), F32), pltpu.VMEM((IN_COLS, D_MODEL), F32)],
        semantics=("arbitrary",), args=(dua, *dqs, *dks, *dvs, cos, sin, hn1), exchange=exchange)
    return dproj, dw, received


def _mm_tn(a, b, name, exchange=()):
    s_len, m = a.shape
    n = b.shape[1]
    tk = 2 * ROW_TILE
    tm = m if m <= 512 else (1408 if m == D_FF else 512)
    n_k = s_len // tk

    def body(a_ref, b_ref, o_ref, acc_ref):
        k = pl.program_id(1)

        @pl.when(k == 0)
        def _():
            acc_ref[...] = jnp.zeros_like(acc_ref)

        acc_ref[...] += lax.dot_general(a_ref[...].astype(BF16), b_ref[...].astype(BF16), _TN, preferred_element_type=F32)

        @pl.when(k == n_k - 1)
        def _():
            o_ref[...] = acc_ref[...].astype(BF16)

    (grad,), received = _call(
        body, name=name, grid=(m // tm, n_k),
        in_specs=[pl.BlockSpec((tk, tm), lambda i, k: (k, i)), pl.BlockSpec((tk, n), lambda i, k: (k, 0))],
        out_specs=[pl.BlockSpec((tm, n), lambda i, k: (i, 0))],
        out_shape=[jax.ShapeDtypeStruct((m, n), BF16)],
        scratch_shapes=[pltpu.VMEM((tm, n), F32)],
        semantics=("parallel", "arbitrary"), args=(a, b), exchange=exchange)
    return grad, received


def _position():
    x, y, c = lax.axis_index("x"), lax.axis_index("y"), lax.axis_index("c")
    return x, y, c, 4 * x + 2 * y + c


def _peer(x, y, c, rel):
    return (x ^ ((rel >> 2) & 1), y ^ ((rel >> 1) & 1), c ^ (rel & 1))


def _exchange_out_shape(kind, arr):
    return jax.ShapeDtypeStruct(((N_DEV,) + arr.shape) if kind == "gather" else arr.shape, arr.dtype)


def _exchange_sems(n_items):
    return [pltpu.SemaphoreType.DMA((n_items, N_DEV)), pltpu.SemaphoreType.DMA((n_items, N_DEV)), pltpu.SemaphoreType.DMA((n_items,))]


def _exchange_copies(kinds, srcs, dsts, sems):
    send_sems, recv_sems, local_sems = sems
    x, y, c, me = _position()
    local, sends, recvs = [], [], []
    for k, (kind, src, dst) in enumerate(zip(kinds, srcs, dsts)):
        own = src if kind == "gather" else src.at[me]
        local.append(pltpu.make_async_copy(own, dst.at[me], local_sems.at[k]))
        for rel in range(1, N_DEV):
            going = src if kind == "gather" else src.at[me ^ rel]
            common = dict(send_sem=send_sems.at[k, rel], recv_sem=recv_sems.at[k, rel],
                          device_id=_peer(x, y, c, rel), device_id_type=MESH)
            sends.append(pltpu.make_async_remote_copy(src_ref=going, dst_ref=dst.at[me], **common))
            recvs.append(pltpu.make_async_remote_copy(src_ref=own, dst_ref=dst.at[me ^ rel], **common))
    return local, sends, recvs


def _exchange_start(kinds, srcs, dsts, sems):
    local, sends, _ = _exchange_copies(kinds, srcs, dsts, sems)
    for cp in local + sends:
        cp.start()


def _exchange_finish(kinds, srcs, dsts, sems):
    local, sends, recvs = _exchange_copies(kinds, srcs, dsts, sems)
    for cp in recvs:
        cp.wait_recv()
    for cp in sends:
        cp.wait_send()
    for cp in local:
        cp.wait()


def _exchange_only(items, name):
    kinds = [k for k, _ in items]
    n = len(items)

    def body(*refs):
        srcs, dsts, sems = refs[:n], refs[n:2 * n], refs[2 * n:]
        _exchange_start(kinds, srcs, dsts, sems)
        _exchange_finish(kinds, srcs, dsts, sems)

    any_spec = pl.BlockSpec(memory_space=pl.ANY)
    return pl.pallas_call(
        body, name=name, in_specs=[any_spec] * n, out_specs=[any_spec] * n,
        out_shape=[_exchange_out_shape(k, a) for k, a in items],
        scratch_shapes=_exchange_sems(n),
        compiler_params=pltpu.CompilerParams(has_side_effects=True),
    )(*[a for _, a in items])


def _gather_two_level_with_rope_tables(shard, inv_freq, s_len, name):
    rows = ROW_TILE

    def body(inv_ref, src, cos_ref, sin_ref, dst, send_sems, recv_sems, local_sem):
        x, y, c, me = _position()
        sibling = (x, y, 1 - c)
        chips = [(1 - x, y), (x, 1 - y), (1 - x, 1 - y)]

        def block(px, py, pc):
            return dst.at[4 * px + 2 * py + pc]

        def copy(k, blk, to, src_ref=None):
            return pltpu.make_async_remote_copy(
                src_ref=block(*blk) if src_ref is None else src_ref, dst_ref=block(*blk),
                send_sem=send_sems.at[k], recv_sem=recv_sems.at[k], device_id=to, device_id_type=MESH)

        mine = pltpu.make_async_copy(src, dst.at[me], local_sem)
        mine.start()
        first = [copy(0, (x, y, c), sibling, src)] + [copy(1 + j, (x, y, c), (*chip, c), src) for j, chip in enumerate(chips)]
        for cp in first:
            cp.start()

        inv = inv_ref[...]
        lane = lax.broadcasted_iota(jnp.int32, (rows, 128), 1)
        sign = jnp.where((lane // (HEAD_DIM // 2)) % 2 == 0, -1.0, 1.0)
        row = lax.broadcasted_iota(jnp.int32, (rows, 128), 0)

        @pl.loop(0, s_len // rows)
        def _(i):
            at = pl.multiple_of(i * rows, rows)
            ang = (row + at).astype(F32) * inv
            cos_ref[pl.ds(at, rows), :] = jnp.cos(ang)
            sin_ref[pl.ds(at, rows), :] = jnp.sin(ang) * sign

        passed = [copy(4 + j, (*chip, c), sibling) for j, chip in enumerate(chips)]
        for j, chip in enumerate(chips):
            copy(1 + j, (*chip, c), (x, y, c)).wait_recv()
            passed[j].start()
        copy(0, (x, y, 1 - c), (x, y, c)).wait_recv()
        for j, chip in enumerate(chips):
            copy(4 + j, (*chip, 1 - c), (x, y, c)).wait_recv()
        for cp in first + passed:
            cp.wait_send()
        mine.wait()

    any_spec = pl.BlockSpec(memory_space=pl.ANY)
    vmem = pl.BlockSpec(memory_space=pltpu.VMEM)
    table = jax.ShapeDtypeStruct((s_len, 128), F32)
    return pl.pallas_call(
        body, name=name, in_specs=[vmem, any_spec], out_specs=[vmem, vmem, any_spec],
        out_shape=[table, table, _exchange_out_shape("gather", shard)],
        scratch_shapes=[pltpu.SemaphoreType.DMA((N_DEV - 1,)), pltpu.SemaphoreType.DMA((N_DEV - 1,)), pltpu.SemaphoreType.DMA],
        compiler_params=pltpu.CompilerParams(has_side_effects=True, vmem_limit_bytes=V7X_VMEM_LIMIT_BYTES),
    )(inv_freq, shard)


def _call(body, *, name, grid, in_specs, out_specs, out_shape, args, scratch_shapes=(), semantics, exchange=()):
    if not exchange:
        outs = pl.pallas_call(body, name=name, grid=grid, in_specs=in_specs, out_specs=out_specs, out_shape=out_shape,
                              scratch_shapes=list(scratch_shapes), compiler_params=_params(*semantics))(*args)
        return outs, []
    kinds = [k for k, _ in exchange]
    n_in, n_out, n_x, n_scr = len(in_specs), len(out_specs), len(exchange), len(scratch_shapes)

    def wrapped(*refs):
        ins, refs = refs[:n_in], refs[n_in:]
        srcs, refs = refs[:n_x], refs[n_x:]
        outs, refs = refs[:n_out], refs[n_out:]
        dsts, refs = refs[:n_x], refs[n_x:]
        scratch, sems = refs[:n_scr], refs[n_scr:]
        ids = [pl.program_id(a) for a in range(len(grid))]
        first = functools.reduce(jnp.logical_and, [i == 0 for i in ids])
        last = functools.reduce(jnp.logical_and, [i == g - 1 for i, g in zip(ids, grid)])

        @pl.when(first)
        def _():
            _exchange_start(kinds, srcs, dsts, sems)

        body(*ins, *outs, *scratch)

        @pl.when(last)
        def _():
            _exchange_finish(kinds, srcs, dsts, sems)

    any_spec = pl.BlockSpec(memory_space=pl.ANY)
    outs = pl.pallas_call(
        wrapped, name=name, grid=grid,
        in_specs=list(in_specs) + [any_spec] * n_x, out_specs=list(out_specs) + [any_spec] * n_x,
        out_shape=list(out_shape) + [_exchange_out_shape(k, a) for k, a in exchange],
        scratch_shapes=list(scratch_shapes) + _exchange_sems(n_x),
        compiler_params=pltpu.CompilerParams(dimension_semantics=("arbitrary",) * len(grid),
                                             vmem_limit_bytes=V7X_VMEM_LIMIT_BYTES, has_side_effects=True),
    )(*args, *[a for _, a in exchange])
    return outs[:n_out], outs[n_out:]


def _adamw_math(w, g, m, v):
    m = ADAM_B1 * m + (1.0 - ADAM_B1) * g
    v = ADAM_B2 * v + (1.0 - ADAM_B2) * (g * g)
    m_hat = m / (1.0 - ADAM_B1 ** ADAM_STEP)
    v_hat = v / (1.0 - ADAM_B2 ** ADAM_STEP)
    return -ADAM_LR * (m_hat / (jnp.sqrt(v_hat) + ADAM_EPS) + ADAM_WD * w), m, v


def _adamw(parts, w, m, v, name):
    rows, cols = w.shape
    tm = 256 if rows % 256 == 0 and rows > 256 else rows

    def body(p_ref, w_ref, m_ref, v_ref, g_ref, d_ref, nm_ref, nv_ref):
        g = p_ref[0].astype(F32)
        for j in range(1, N_DEV):
            g = g + p_ref[j].astype(F32)
        delta, nm, nv = _adamw_math(w_ref[...], g, m_ref[...], v_ref[...])
        g_ref[...] = g
        d_ref[...] = delta
        nm_ref[...] = nm
        nv_ref[...] = nv

    shard = jax.ShapeDtypeStruct((rows, cols), F32)
    return pl.pallas_call(
        body, name=name, grid=(rows // tm,),
        in_specs=[pl.BlockSpec((N_DEV, tm, cols), lambda i: (0, i, 0))] + [_rows(tm, cols)] * 3,
        out_specs=[_rows(tm, cols)] * 4,
        out_shape=[shard] * 4,
        compiler_params=_params("parallel"),
    )(parts, w, m, v)


_SMALL = ("mix_norm_g", "sgu_w", "sgu_b", "sgu_norm_g", "out_norm_a", "out_norm_b", "ffn_norm_g", "ple_norm_g", "final_norm_g")
_BIG = ("w_in", "w_out", "w_gate", "w_up", "w_down", "w_ple_gate", "w_ple_proj")
_COLUMN_SHARDED = ("w_in", "w_gate", "w_up", "w_ple_proj")
_ORDER = ("mix_norm_g", "w_in", "sgu_w", "sgu_b", "sgu_norm_g", "out_norm_a", "out_norm_b", "w_out", "ffn_norm_g",
          "w_gate", "w_up", "w_down", "ple_norm_g", "w_ple_gate", "w_ple_proj", "final_norm_g")


def _pack_small(values, names=_SMALL):
    flat = jnp.concatenate([values[n].reshape(-1).astype(F32) for n in names])
    pad = (-flat.shape[0]) % (8 * 128)
    return jnp.pad(flat, (0, pad)).reshape(-1, 128)


def _unpack_small(packed, like):
    flat = packed.reshape(-1)
    out, at = {}, 0
    for n in _SMALL:
        size = like[n].size
        out[n] = flat[at:at + size].reshape(like[n].shape)
        at += size
    return out


def _own_orientation(name, value):
    return value[0].T if name in _COLUMN_SHARDED else value[0]


def _reference_orientation(name, value):
    return (value.T if name in _COLUMN_SHARDED else value)[None]


def _full_from_gathered(gathered):
    return gathered.reshape(N_DEV * gathered.shape[1], gathered.shape[2])


def _sliced_for_devices(grad):
    return grad.reshape(N_DEV, grad.shape[0] // N_DEV, grad.shape[1])


def _rope_inv_freq():
    half = HEAD_DIM // 2
    inv = ROPE_THETA ** (-jnp.arange(half, dtype=F32) / half)
    return jnp.tile(inv, 128 // half)[None, :]


def _forward_backward(x, p, target, small, shards):
    def gather(*names):
        return [("gather", shards[n]) for n in names]

    def scatter(**grads):
        return [("scatter", _sliced_for_devices(g)) for g in grads.values()]

    full, parts = {}, {}
    s_len = x.shape[0]
    cos, sin, got = _gather_two_level_with_rope_tables(shards["w_in"], _rope_inv_freq(), s_len, "gather_w_in")
    full["w_in"] = _full_from_gathered(got)

    g_mix, g_ffn, g_ple = small["mix_norm_g"], small["ffn_norm_g"], small["ple_norm_g"]
    g_fin = small["final_norm_g"].reshape(1, D_MODEL)
    sw, gs, ga, gb = small["sgu_w"], small["sgu_norm_g"], small["out_norm_a"], small["out_norm_b"]
    b2 = jnp.repeat(small["sgu_b"].T, HEAD_DIM, axis=1)
    head_sum = (jnp.arange(WIDTH_B)[:, None] // HEAD_DIM == jnp.arange(128)[None, :]).astype(BF16)
    n_br = len(DILATIONS)

    def arrived(names, got):
        for n, g in zip(names, got):
            full[n] = _full_from_gathered(g)

    (ua, hn1, *qkv), got = _inproj(x, g_mix, full["w_in"], cos, sin, exchange=gather("w_gate"))
    arrived(("w_gate",), got)
    ya_n = _sgu_fwd(ua, sw, b2, gs, ga)
    half = shards["w_up"].shape[0] // 2
    riders = [[("gather", shards["w_up"][:half])], [("gather", shards["w_up"][half:])], gather("w_out")]
    branch, got = [], []
    for i, d in enumerate(DILATIONS):
        o_l, g = _attn_fwd(qkv[i], d, exchange=riders[i])
        branch.append(o_l)
        got += g
    arrived(("w_up", "w_out"), [jnp.concatenate(got[:2], axis=1), got[2]])
    (y, yb, *lse), _ = _combine([o for o, _ in branch], [l for _, l in branch], ya_n, gb, head_sum.T)
    last_wave = ("w_down", "w_ple_gate", "w_ple_proj")
    (h1, a, b, act, hn2), got = _ffn_up(y, full["w_out"], x, g_ffn, full["w_gate"], full["w_up"], exchange=gather(*last_wave))
    arrived(last_wave, got)
    h2, h3, gate, pp, hn3 = _ffn_down_ple(act, full["w_down"], h1, g_ple, full["w_ple_gate"], p, full["w_ple_proj"])

    dh2, loss, d_fin, d_ple, g_ple_gate, g_ple_proj = _loss_ple_bwd(
        h3, target, g_fin, gate, pp, h2, g_ple, full["w_ple_gate"], hn3, p)
    g_down, (parts["w_ple_gate"], parts["w_ple_proj"]) = _mm_tn(
        act, dh2, "dw_down", exchange=scatter(w_ple_gate=g_ple_gate, w_ple_proj=g_ple_proj))
    (da, db), (parts["w_down"],) = _ffn_down_bwd(dh2, full["w_down"], a, b, exchange=scatter(w_down=g_down))
    g_gate, _ = _mm_tn(da, hn2, "dw_gate")
    g_up, _ = _mm_tn(db, hn2, "dw_up")
    (dh1, d_ffn, g_out), (parts["w_gate"],) = _mm_norm_bwd(
        [(da, full["w_gate"]), (db, full["w_up"])], h1, g_ffn, dh2, "ffn_up_bwd", exchange=scatter(w_gate=g_gate), dw_lhs=y)
    (dya_n, d_gb, *do_dd), (parts["w_out"],) = _outproj_bwd(dh1, full["w_out"], yb, gb, head_sum, exchange=scatter(w_out=g_out))
    grads_b = []
    for i, d in enumerate(DILATIONS):
        g3, got = _attn_bwd(qkv[i], do_dd[i], lse[i], do_dd[n_br + i], d, exchange=scatter(w_up=g_up) if i == 0 else ())
        grads_b.append(g3)
        if i == 0:
            (parts["w_up"],) = got
    dua, d_sw, d_b2, d_gs, d_ga = _sgu_bwd(ua, sw, b2, gs, ga, dya_n)
    early = {
        "sgu_w": d_sw, "sgu_b": d_b2.reshape(CHUNK, 4, HEAD_DIM).sum(axis=-1).T, "sgu_norm_g": d_gs, "out_norm_a": d_ga,
        "out_norm_b": d_gb, "ffn_norm_g": d_ffn, "ple_norm_g": d_ple, "final_norm_g": d_fin,
    }
    dproj, g_in, (early_parts,) = _dproj(
        dua, [g[0] for g in grads_b], [g[1] for g in grads_b], [g[2] for g in grads_b], cos, sin, hn1,
        exchange=[("gather", _pack_small(early, _SMALL[1:]))])
    (dx, d_mix), (parts["w_in"],) = _mm_norm_bwd(
        [(dproj, full["w_in"])], x, g_mix, dh1, "inproj_bwd", exchange=scatter(w_in=g_in))
    late = jnp.concatenate([_pack_small({"mix_norm_g": d_mix}, _SMALL[:1]), jnp.broadcast_to(loss, (8, 128))])
    (late_parts,) = _exchange_only([("gather", late)], "gather_mix_norm_grad_and_loss")
    total_loss = jnp.sum(late_parts[:, 8, 0])
    return total_loss, dx, parts, jnp.concatenate([late_parts[:, :8], early_parts], axis=1)


def kernel(x, p, mix_norm_g, w_in, sgu_w, sgu_b, sgu_norm_g, out_norm_a, out_norm_b, w_out, ffn_norm_g, w_gate, w_up, w_down, ple_norm_g, w_ple_gate, w_ple_proj, final_norm_g, loss_target, m_mix_norm_g, m_w_in, m_sgu_w, m_sgu_b, m_sgu_norm_g, m_out_norm_a, m_out_norm_b, m_w_out, m_ffn_norm_g, m_w_gate, m_w_up, m_w_down, m_ple_norm_g, m_w_ple_gate, m_w_ple_proj, m_final_norm_g, v_mix_norm_g, v_w_in, v_sgu_w, v_sgu_b, v_sgu_norm_g, v_out_norm_a, v_out_norm_b, v_w_out, v_ffn_norm_g, v_w_gate, v_w_up, v_w_down, v_ple_norm_g, v_w_ple_gate, v_w_ple_proj, v_final_norm_g):
    given = dict(locals())
    weights = {n: given[n] for n in _ORDER}
    moments_m = {n: given["m_" + n] for n in _ORDER}
    moments_v = {n: given["v_" + n] for n in _ORDER}

    shards = {n: _own_orientation(n, weights[n]).astype(BF16) for n in _BIG}
    small = {n: (weights[n][0] if n in ("sgu_w", "sgu_b") else weights[n]) for n in _SMALL}

    loss, dx, parts, small_parts = _forward_backward(x[0], p[0, 0], loss_target[0], small, shards)

    small_like = {n: weights[n] for n in _SMALL}
    grads, deltas, new_m, new_v = {}, {}, {}, {}
    for n in _BIG:
        outs = _adamw(parts[n], _own_orientation(n, weights[n]), _own_orientation(n, moments_m[n]),
                      _own_orientation(n, moments_v[n]), "adamw_" + n)
        grads[n], deltas[n], new_m[n], new_v[n] = [_reference_orientation(n, o) for o in outs]
    g, d, nm, nv = _adamw(small_parts, _pack_small(small_like), _pack_small({n: moments_m[n] for n in _SMALL}),
                          _pack_small({n: moments_v[n] for n in _SMALL}), "adamw_small")
    for out, packed in ((grads, g), (deltas, d), (new_m, nm), (new_v, nv)):
        out.update(_unpack_small(packed, small_like))

    return (loss, dx[None], *[grads[n] for n in _ORDER], *[deltas[n] for n in _ORDER],
            *[new_m[n] for n in _ORDER], *[new_v[n] for n in _ORDER])
```

```python
import functools

import jax
import jax.numpy as jnp
from jax import lax
from jax.experimental import pallas as pl
from jax.experimental.pallas import tpu as pltpu

F32 = jnp.float32
BF16 = jnp.bfloat16

D_MODEL = 1024
WIDTH_A = 256
WIDTH_B = 768
D_FF = 2816
IN_COLS = 2 * WIDTH_A + 3 * WIDTH_B
PLE_DIM = 256
HEAD_DIM = 64
N_PAIRS = WIDTH_B // 128
CHUNK = 128
N_BACK = 128
DILATIONS = (1, 4, 16)
ROPE_THETA = 10000.0
EPS = 1e-6
N_DEV = 8

ADAM_LR = 0.001
ADAM_B1 = 0.9
ADAM_B2 = 0.999
ADAM_EPS = 1e-08
ADAM_WD = 0.01
ADAM_STEP = 10

V7X_VMEM_LIMIT_BYTES = 56 * 1024 * 1024
ROW_TILE = 512
MESH = pl.DeviceIdType.MESH
NEG = -1e30

_NT = (((1,), (1,)), ((), ()))
_TN = (((0,), (0,)), ((), ()))


def _params(*semantics):
    return pltpu.CompilerParams(dimension_semantics=semantics, vmem_limit_bytes=V7X_VMEM_LIMIT_BYTES)


def _rows(tm, width):
    return pl.BlockSpec((tm, width), lambda i: (i, 0))


def _whole(shape):
    return pl.BlockSpec(shape, lambda *_: (0,) * len(shape))


def _resident(shape):
    return pl.BlockSpec(shape, lambda *_: (0,) * len(shape), pipeline_mode=pl.Buffered(1))


def _gelu(x):
    t = jnp.tanh(0.7978845608028654 * (x + 0.044715 * (x * x * x)))
    return 0.5 * x * (1.0 + t)


def _gelu_grad(x):
    t = jnp.tanh(0.7978845608028654 * (x + 0.044715 * (x * x * x)))
    return 0.5 * (1.0 + t) + 0.5 * x * (1.0 - t * t) * (0.7978845608028654 * (1.0 + 3.0 * 0.044715 * (x * x)))


def _rstd(x):
    return lax.rsqrt(jnp.mean(x * x, axis=-1, keepdims=True) + EPS)


def _norm_bwd(dn, h, g):
    r = _rstd(h)
    n = h * r
    t = dn * g
    return r * (t - n * jnp.mean(t * n, axis=-1, keepdims=True)), dn * n


def _swap_halves(x, first_half):
    return jnp.where(first_half, pltpu.roll(x, 96, 1), pltpu.roll(x, 32, 1))


def _sub_spec(d, n_cb, tm):
    return pl.BlockSpec((d, n_cb, tm // d, 128), lambda i: (0, 0, i, 0))


def _sub_shape(s_len, d, n_cb, dtype):
    return jax.ShapeDtypeStruct((d, n_cb, s_len // d, 128), dtype)


def _to_sub(stage_ref, cb_src, out_ref, cb_dst, d, tm):
    slab = stage_ref.at[cb_src]
    for r in range(d):
        out_ref[r, cb_dst] = slab[pl.ds(r, tm // d, stride=d), :].astype(out_ref.dtype)


def _from_sub(in_ref, cb_src, stage_ref, cb_dst, d, tm):
    slab = stage_ref.at[cb_dst]
    for r in range(d):
        slab[pl.ds(r, tm // d, stride=d), :] = in_ref[r, cb_src].astype(F32)


def _inproj(x, g, w, cos, sin, exchange=()):
    s_len = x.shape[0]
    tm = ROW_TILE
    n_cb = 3 * N_PAIRS

    def body(x_ref, g_ref, w_ref, cos_ref, sin_ref, ua_ref, hn_ref, *rest):
        sub_refs, stage = rest[:-1], rest[-1]
        xf = x_ref[...]
        hn = (xf * _rstd(xf) * g_ref[...]).astype(BF16)
        hn_ref[...] = hn
        c = cos_ref[...]
        s = sin_ref[...]
        first_half = (lax.broadcasted_iota(jnp.int32, (tm, 128), 1) % HEAD_DIM) < HEAD_DIM // 2
        for j in range(IN_COLS // 256):
            col = j * 256
            acc = lax.dot_general(hn, w_ref[col:col + 256, :], _NT, preferred_element_type=F32)
            if col < 2 * WIDTH_A:
                ua_ref[:, col:col + 256] = acc
                continue
            for half in range(2):
                cb = (col - 2 * WIDTH_A) // 128 + half
                t = acc[:, half * 128:(half + 1) * 128]
                if cb < 2 * N_PAIRS:
                    t = (t * c + _swap_halves(t, first_half) * s) * (0.125 if cb < N_PAIRS else 1.0)
                stage[cb] = t
                sub_refs[0][0, cb] = t.astype(BF16)
        for cb in range(n_cb):
            for d, out_ref in zip(DILATIONS[1:], sub_refs[1:]):
                _to_sub(stage, cb, out_ref, cb, d, tm)

    return _call(
        body, name="inproj", grid=(s_len // tm,),
        in_specs=[_rows(tm, D_MODEL), _whole((1, D_MODEL)), _resident((IN_COLS, D_MODEL)), _rows(tm, 128), _rows(tm, 128)],
        out_specs=[_rows(tm, 2 * WIDTH_A), _rows(tm, D_MODEL)] + [_sub_spec(d, n_cb, tm) for d in DILATIONS],
        out_shape=[jax.ShapeDtypeStruct((s_len, 2 * WIDTH_A), F32), jax.ShapeDtypeStruct((s_len, D_MODEL), BF16)]
        + [_sub_shape(s_len, d, n_cb, BF16) for d in DILATIONS],
        scratch_shapes=[pltpu.VMEM((n_cb, tm, 128), F32)],
        semantics=("parallel",), args=(x, g, w, cos, sin), exchange=exchange)


def _sgu_mix_weights(sw_ref):
    keep = lax.broadcasted_iota(jnp.int32, (CHUNK, CHUNK), 0) >= lax.broadcasted_iota(jnp.int32, (CHUNK, CHUNK), 1)
    return [jnp.where(keep, sw_ref[h], 0.0).astype(BF16) for h in range(4)], keep


def _sgu_core(ua_ref, gs_ref):
    u = ua_ref[:, :WIDTH_A]
    va = ua_ref[:, WIDTH_A:]
    vg = _gelu(va)
    xc = vg - jnp.mean(vg, axis=-1, keepdims=True)
    rstd = lax.rsqrt(jnp.mean(xc * xc, axis=-1, keepdims=True) + EPS)
    xhat = xc * rstd
    return u, va, _gelu(u), xhat, rstd, xhat * gs_ref[...]


def _sgu_fwd(ua, sw, b2, gs, ga):
    s_len = ua.shape[0]
    tm = ROW_TILE

    def body(ua_ref, sw_ref, b2_ref, gs_ref, ga_ref, out_ref):
        _, _, ug, _, _, vn = _sgu_core(ua_ref, gs_ref)
        wm, _ = _sgu_mix_weights(sw_ref)
        head = lax.broadcasted_iota(jnp.int32, (CHUNK, WIDTH_A), 1) // HEAD_DIM
        for c in range(tm // CHUNK):
            rows = slice(c * CHUNK, (c + 1) * CHUNK)
            vnc = vn[rows]
            mixed = b2_ref[...]
            for h in range(4):
                mixed = mixed + jnp.dot(wm[h], jnp.where(head == h, vnc, 0.0).astype(BF16), preferred_element_type=F32)
            ya = ug[rows] * mixed
            out_ref[rows, :] = (ya * _rstd(ya) * ga_ref[...]).astype(BF16)

    return pl.pallas_call(
        body, name="sgu_fwd", grid=(s_len // tm,),
        in_specs=[_rows(tm, 2 * WIDTH_A), _whole((4, CHUNK, CHUNK)), _whole((CHUNK, WIDTH_A)), _whole((1, WIDTH_A)), _whole((1, WIDTH_A))],
        out_specs=_rows(tm, WIDTH_A),
        out_shape=jax.ShapeDtypeStruct((s_len, WIDTH_A), BF16),
        compiler_params=_params("parallel"),
    )(ua, sw, b2, gs, ga)


STEP_POSITIONS = 2 * ROW_TILE


def _attn_geometry(sd):
    tile = min(STEP_POSITIONS, sd)
    return tile, tile // CHUNK, sd // tile, STEP_POSITIONS // tile


PAIRS_PER_STEP = 6


def _attn_spec(cb0, rows, row_index, res):
    return pl.BlockSpec((res, PAIRS_PER_STEP, rows, 128), lambda r, g, n: (r, cb0 // PAIRS_PER_STEP + g, row_index(n), 0))


assert PAIRS_PER_STEP == N_PAIRS
assert DILATIONS[0] == 1


def _stats_spec(rows, row_index, res):
    return pl.BlockSpec((res, None, rows, 128), lambda r, g, n: (r, 0, row_index(n), 0))


def _stats_shape(sd, d):
    return jax.ShapeDtypeStruct((d, 1, sd, 128), F32)


def _both_heads(x, head_a):
    zero = jnp.zeros_like(x)
    return [jnp.where(head_a, x, zero), jnp.where(head_a, zero, x)]


def _attn_fwd(qkv, d, exchange=()):
    sd = qkv.shape[2]
    tile, nb, n_tiles, res = _attn_geometry(sd)

    def prev(n):
        return jnp.maximum(n * nb - 1, 0)

    def body(q_ref, k_ref, kp_ref, v_ref, vp_ref, o_ref, l_ref):
        for rr in range(res):
            for hp in range(PAIRS_PER_STEP):
                one_pair(hp, q_ref.at[rr, hp], k_ref.at[rr, hp], kp_ref.at[rr, hp], v_ref.at[rr, hp], vp_ref.at[rr, hp],
                         o_ref.at[rr, hp], l_ref.at[rr])

    def one_pair(hp, q_ref, k_ref, kp_ref, v_ref, vp_ref, o_ref, l_ref):
        n = pl.program_id(2)
        lane = lax.broadcasted_iota(jnp.int32, (CHUNK, 128), 1)
        head_a = lane < HEAD_DIM
        qi = lax.broadcasted_iota(jnp.int32, (2 * CHUNK, 2 * CHUNK), 0) % CHUNK
        kc = lax.broadcasted_iota(jnp.int32, (2 * CHUNK, 2 * CHUNK), 1)
        band = (kc >= qi) & (kc <= qi + N_BACK)
        for j in range(nb):
            rows = slice(j * CHUNK, (j + 1) * CHUNK)
            if j == 0:
                kcat = jnp.concatenate([kp_ref[...], k_ref[rows, :]], axis=0)
                vcat = jnp.concatenate([vp_ref[...], v_ref[rows, :]], axis=0)
                valid = band & jnp.logical_or(n > 0, kc >= CHUNK)
            else:
                kcat = k_ref[(j - 1) * CHUNK:(j + 1) * CHUNK, :]
                vcat = v_ref[(j - 1) * CHUNK:(j + 1) * CHUNK, :]
                valid = band
            q2 = jnp.concatenate(_both_heads(q_ref[rows, :], head_a), axis=0)
            s = lax.dot_general(q2, kcat, _NT, preferred_element_type=F32)
            s = jnp.where(valid, s, NEG)
            m = jnp.max(s, axis=-1, keepdims=True)
            p = jnp.exp(s - m)
            l = jnp.sum(p, axis=-1, keepdims=True)
            o2 = jnp.dot(p.astype(BF16), vcat, preferred_element_type=F32) / l
            lse2 = m + jnp.log(l)
            o_ref[rows, :] = jnp.where(head_a, o2[:CHUNK], o2[CHUNK:]).astype(BF16)
            others = l_ref[rows, :] if hp > 0 else jnp.zeros((CHUNK, 128), F32)
            l_ref[rows, :] = jnp.where(lane == 2 * hp, lse2[:CHUNK], jnp.where(lane == 2 * hp + 1, lse2[CHUNK:], others))

    same = lambda n: n
    return _call(
        body, name=f"attn_fwd_d{d}", grid=(d // res, N_PAIRS // PAIRS_PER_STEP, n_tiles),
        in_specs=[_attn_spec(0, tile, same, res), _attn_spec(N_PAIRS, tile, same, res), _attn_spec(N_PAIRS, CHUNK, prev, res),
                  _attn_spec(2 * N_PAIRS, tile, same, res), _attn_spec(2 * N_PAIRS, CHUNK, prev, res)],
        out_specs=[_attn_spec(0, tile, same, res), _stats_spec(tile, same, res)],
        out_shape=[jax.ShapeDtypeStruct((d, N_PAIRS, sd, 128), BF16), _stats_shape(sd, d)],
        semantics=("parallel", "parallel", "parallel"), args=(qkv, qkv, qkv, qkv, qkv), exchange=exchange)


def _combine(outs, lses, ya_n, gb, head_spread, exchange=()):
    s_len = ya_n.shape[0]
    tm = ROW_TILE
    n_br = len(DILATIONS)

    def body(*refs):
        o_refs, l_refs = refs[:n_br], refs[n_br:2 * n_br]
        ya_ref, gb_ref, spread_ref, y_ref, yb_ref = refs[2 * n_br:2 * n_br + 5]
        lse_refs = refs[2 * n_br + 5:3 * n_br + 5]
        o_nat, l_nat, lse_nat, w_wide = refs[3 * n_br + 5:]
        for i, d in enumerate(DILATIONS):
            _from_sub(l_refs[i], 0, l_nat, i, d, tm)
        ls = [l_nat[i] for i in range(n_br)]
        top = jnp.maximum(jnp.maximum(ls[0], ls[1]), ls[2])
        ws = [jnp.exp(l - top) for l in ls]
        den = ws[0] + ws[1] + ws[2]
        inv = 1.0 / den
        for i in range(n_br):
            w = ws[i] * inv
            hi = w.astype(BF16)
            lo = (w - hi.astype(F32)).astype(BF16)
            w_wide[i] = (jnp.dot(hi, spread_ref[...], preferred_element_type=F32)
                         + jnp.dot(lo, spread_ref[...], preferred_element_type=F32))
        lse_nat[0] = top + jnp.log(den)
        for d, lse_ref in zip(DILATIONS, lse_refs):
            _to_sub(lse_nat, 0, lse_ref, 0, d, tm)
        sumsq = jnp.zeros((tm, 1), F32)
        for cb in range(N_PAIRS):
            cols = slice(cb * 128, (cb + 1) * 128)
            yb = w_wide[0, :, cols] * o_refs[0][0, cb].astype(F32)
            for i, d in enumerate(DILATIONS[1:], start=1):
                _from_sub(o_refs[i], cb, o_nat, i, d, tm)
                yb = yb + w_wide[i, :, cols] * o_nat[i]
            yb_ref[:, cb * 128:(cb + 1) * 128] = yb
            sumsq = sumsq + jnp.sum(yb * yb, axis=-1, keepdims=True)
        r = lax.rsqrt(sumsq / WIDTH_B + EPS)
        y_ref[:, :WIDTH_A] = ya_ref[...]
        y_ref[:, WIDTH_A:] = (yb_ref[...] * r * gb_ref[...]).astype(BF16)

    stats = [_sub_spec(d, 1, tm) for d in DILATIONS]
    return _call(
        body, name="attn_combine", grid=(s_len // tm,),
        in_specs=[_sub_spec(d, N_PAIRS, tm) for d in DILATIONS] + stats
        + [_rows(tm, WIDTH_A), _whole((1, WIDTH_B)), _whole((128, WIDTH_B))],
        out_specs=[_rows(tm, D_MODEL), _rows(tm, WIDTH_B)] + stats,
        out_shape=[jax.ShapeDtypeStruct((s_len, D_MODEL), BF16), jax.ShapeDtypeStruct((s_len, WIDTH_B), F32)]
        + [_sub_shape(s_len, d, 1, F32) for d in DILATIONS],
        scratch_shapes=[pltpu.VMEM((n_br, tm, 128), F32), pltpu.VMEM((n_br, tm, 128), F32), pltpu.VMEM((1, tm, 128), F32),
                        pltpu.VMEM((n_br, tm, WIDTH_B), F32)],
        semantics=("parallel",), args=(*outs, *lses, ya_n, gb, head_spread), exchange=exchange)


def _ffn_up(y, wout, x, g, wg, wu, exchange=()):
    s_len = x.shape[0]
    tm = ROW_TILE

    def body(y_ref, wo_ref, x_ref, g_ref, wg_ref, wu_ref, h_ref, a_ref, b_ref, act_ref, hn_ref):
        hf = x_ref[...] + jnp.dot(y_ref[...], wo_ref[...], preferred_element_type=F32)
        h_ref[...] = hf
        hn = (hf * _rstd(hf) * g_ref[...]).astype(BF16)
        hn_ref[...] = hn
        for j in range(D_FF // 256):
            cols = slice(j * 256, (j + 1) * 256)
            a = lax.dot_general(hn, wg_ref[cols, :], _NT, preferred_element_type=F32)
            b = lax.dot_general(hn, wu_ref[cols, :], _NT, preferred_element_type=F32)
            a_ref[:, cols] = a.astype(BF16)
            b_ref[:, cols] = b.astype(BF16)
            act_ref[:, cols] = (a * jax.nn.sigmoid(a) * b).astype(BF16)

    wide = jax.ShapeDtypeStruct((s_len, D_FF), BF16)
    return _call(
        body, name="ffn_up", grid=(s_len // tm,),
        in_specs=[_rows(tm, D_MODEL), _resident((D_MODEL, D_MODEL)), _rows(tm, D_MODEL), _whole((1, D_MODEL)),
                  _resident((D_FF, D_MODEL)), _resident((D_FF, D_MODEL))],
        out_specs=[_rows(tm, D_MODEL), _rows(tm, D_FF), _rows(tm, D_FF), _rows(tm, D_FF), _rows(tm, D_MODEL)],
        out_shape=[jax.ShapeDtypeStruct((s_len, D_MODEL), F32), wide, wide, wide, jax.ShapeDtypeStruct((s_len, D_MODEL), BF16)],
        semantics=("parallel",), args=(y, wout, x, g, wg, wu), exchange=exchange)


def _ffn_down_ple(act, wd, h1, g, wpg, p, wpp):
    s_len = h1.shape[0]
    tm = ROW_TILE

    def body(act_ref, wd_ref, h1_ref, g_ref, wpg_ref, p_ref, wpp_ref, h2_ref, h3_ref, gate_ref, pp_ref, hn_ref):
        hf = h1_ref[...] + jnp.dot(act_ref[...], wd_ref[...], preferred_element_type=F32)
        h2_ref[...] = hf
        hn = (hf * _rstd(hf) * g_ref[...]).astype(BF16)
        hn_ref[...] = hn
        gate = jax.nn.sigmoid(jnp.dot(hn, wpg_ref[...], preferred_element_type=F32))
        pp = lax.dot_general(p_ref[...].astype(BF16), wpp_ref[...], _NT, preferred_element_type=F32)
        h3_ref[...] = hf + gate * pp
        gate_ref[...] = gate.astype(BF16)
        pp_ref[...] = pp.astype(BF16)

    full = jax.ShapeDtypeStruct((s_len, D_MODEL), F32)
    half = jax.ShapeDtypeStruct((s_len, D_MODEL), BF16)
    return pl.pallas_call(
        body, name="ffn_down_ple", grid=(s_len // tm,),
        in_specs=[_rows(tm, D_FF), _resident((D_FF, D_MODEL)), _rows(tm, D_MODEL), _whole((1, D_MODEL)),
                  _resident((D_MODEL, D_MODEL)), _rows(tm, PLE_DIM), _resident((D_MODEL, PLE_DIM))],
        out_specs=[_rows(tm, D_MODEL)] * 5,
        out_shape=[full, full, half, half, half],
        compiler_params=_params("parallel"),
    )(act, wd, h1, g, wpg, p, wpp)


def _loss_ple_bwd(h3, target, gf, gate, pp, h2, g_ple, wpg, hn3, p):
    s_len = h3.shape[0]
    tm = ROW_TILE
    n_steps = s_len // tm

    def body(h_ref, t_ref, g_ref, gate_ref, pp_ref, h2_ref, gp_ref, w_ref, hn_ref, p_ref,
             dh2_ref, loss_ref, dg_ref, dgp_ref, dwg_ref, dwp_ref, acc_g, acc_p):
        step = pl.program_id(0)

        @pl.when(step == 0)
        def _():
            loss_ref[...] = jnp.zeros_like(loss_ref)
            dg_ref[...] = jnp.zeros_like(dg_ref)
            dgp_ref[...] = jnp.zeros_like(dgp_ref)
            acc_g[...] = jnp.zeros_like(acc_g)
            acc_p[...] = jnp.zeros_like(acc_p)

        hf = h_ref[...]
        gfv = g_ref[...]
        err = hf * _rstd(hf) * gfv - t_ref[...]
        loss_ref[...] += 0.5 * jnp.sum(jnp.sum(err * err, axis=-1, keepdims=True), axis=0, keepdims=True) / D_MODEL
        dh, dg_rows = _norm_bwd(err / D_MODEL, hf, gfv)
        dg_ref[...] += jnp.sum(dg_rows, axis=0, keepdims=True)
        gate = gate_ref[...].astype(F32)
        dz = (dh * pp_ref[...].astype(F32) * gate * (1.0 - gate)).astype(BF16)
        dpp = (dh * gate).astype(BF16)
        dn = lax.dot_general(dz, w_ref[...], _NT, preferred_element_type=F32)
        dh2, dgp_rows = _norm_bwd(dn, h2_ref[...], gp_ref[...])
        dh2 = dh + dh2
        dh2_ref[...] = dh2
        dgp_ref[...] += jnp.sum(dgp_rows, axis=0, keepdims=True)
        acc_g[...] += lax.dot_general(hn_ref[...], dz, _TN, preferred_element_type=F32)
        acc_p[...] += lax.dot_general(dpp, p_ref[...].astype(BF16), _TN, preferred_element_type=F32)

        @pl.when(step == n_steps - 1)
        def _():
            dwg_ref[...] = acc_g[...].astype(BF16)
            dwp_ref[...] = acc_p[...].astype(BF16)

    gain = jax.ShapeDtypeStruct((1, D_MODEL), F32)
    return pl.pallas_call(
        body, name="loss_ple_bwd", grid=(n_steps,),
        in_specs=[_rows(tm, D_MODEL), _rows(tm, D_MODEL), _whole((1, D_MODEL)), _rows(tm, D_MODEL), _rows(tm, D_MODEL),
                  _rows(tm, D_MODEL), _whole((1, D_MODEL)), _resident((D_MODEL, D_MODEL)), _rows(tm, D_MODEL),
                  _rows(tm, PLE_DIM)],
        out_specs=[_rows(tm, D_MODEL), _whole((1, 128)), _whole((1, D_MODEL)), _whole((1, D_MODEL)),
                   _whole((D_MODEL, D_MODEL)), _whole((D_MODEL, PLE_DIM))],
        out_shape=[jax.ShapeDtypeStruct((s_len, D_MODEL), F32), jax.ShapeDtypeStruct((1, 128), F32), gain, gain,
                   jax.ShapeDtypeStruct((D_MODEL, D_MODEL), BF16), jax.ShapeDtypeStruct((D_MODEL, PLE_DIM), BF16)],
        scratch_shapes=[pltpu.VMEM((D_MODEL, D_MODEL), F32), pltpu.VMEM((D_MODEL, PLE_DIM), F32)],
        compiler_params=_params("arbitrary"),
    )(h3, target, gf, gate, pp, h2, g_ple, wpg, hn3, p)


def _mm_norm_bwd(parts, h, g, dres, name, exchange=(), dw_lhs=None):
    s_len = h.shape[0]
    tm = ROW_TILE
    n_parts = len(parts)
    n_steps = s_len // tm
    has_dw = dw_lhs is not None

    def body(*refs):
        a_refs = refs[0:2 * n_parts:2]
        w_refs = refs[1:2 * n_parts:2]
        h_ref, g_ref, r_ref = refs[2 * n_parts:2 * n_parts + 3]
        rest = refs[2 * n_parts + 3:]
        step = pl.program_id(0)
        if has_dw:
            lhs_ref, o_ref, dg_ref, dw_ref, acc_ref = rest
        else:
            o_ref, dg_ref = rest

        @pl.when(step == 0)
        def _():
            dg_ref[...] = jnp.zeros_like(dg_ref)
            if has_dw:
                acc_ref[...] = jnp.zeros_like(acc_ref)

        dn = jnp.dot(a_refs[0][...], w_refs[0][...], preferred_element_type=F32)
        for a_ref, w_ref in zip(a_refs[1:], w_refs[1:]):
            dn = dn + jnp.dot(a_ref[...], w_ref[...], preferred_element_type=F32)
        dh, dg_rows = _norm_bwd(dn, h_ref[...], g_ref[...])
        out = r_ref[...] + dh
        o_ref[...] = out
        dg_ref[...] += jnp.sum(dg_rows, axis=0, keepdims=True)
        if has_dw:
            acc_ref[...] += lax.dot_general(lhs_ref[...], out.astype(BF16), _TN, preferred_element_type=F32)

            @pl.when(step == n_steps - 1)
            def _():
                dw_ref[...] = acc_ref[...].astype(BF16)

    in_specs, args = [], []
    for a, w in parts:
        in_specs += [_rows(tm, a.shape[1]), _resident(w.shape)]
        args += [a, w]
    in_specs += [_rows(tm, D_MODEL), _whole((1, D_MODEL)), _rows(tm, D_MODEL)]
    args += [h, g, dres]
    out_specs = [_rows(tm, D_MODEL), _whole((1, D_MODEL))]
    out_shape = [jax.ShapeDtypeStruct((s_len, D_MODEL), F32), jax.ShapeDtypeStruct((1, D_MODEL), F32)]
    scratch = []
    if has_dw:
        m = dw_lhs.shape[1]
        in_specs.append(_rows(tm, m))
        args.append(dw_lhs)
        out_specs.append(_whole((m, D_MODEL)))
        out_shape.append(jax.ShapeDtypeStruct((m, D_MODEL), BF16))
        scratch.append(pltpu.VMEM((m, D_MODEL), F32))
    return _call(
        body, name=name, grid=(n_steps,), in_specs=in_specs, out_specs=out_specs, out_shape=out_shape,
        scratch_shapes=scratch, semantics=("arbitrary",), args=tuple(args), exchange=exchange)


def _ffn_down_bwd(dh, wdt, a, b, exchange=()):
    s_len = dh.shape[0]
    tm = ROW_TILE

    def body(dh_ref, w_ref, a_ref, b_ref, da_ref, db_ref):
        dhb = dh_ref[...].astype(BF16)
        for j in range(D_FF // 256):
            cols = slice(j * 256, (j + 1) * 256)
            dact = lax.dot_general(dhb, w_ref[cols, :], _NT, preferred_element_type=F32)
            av = a_ref[:, cols].astype(F32)
            bv = b_ref[:, cols].astype(F32)
            sig = jax.nn.sigmoid(av)
            t = dact * sig
            silu = av * sig
            da_ref[:, cols] = (t * bv * (1.0 + av - silu)).astype(BF16)
            db_ref[:, cols] = (dact * silu).astype(BF16)

    wide = jax.ShapeDtypeStruct((s_len, D_FF), BF16)
    return _call(
        body, name="ffn_down_bwd", grid=(s_len // tm,),
        in_specs=[_rows(tm, D_MODEL), _resident((D_FF, D_MODEL)), _rows(tm, D_FF), _rows(tm, D_FF)],
        out_specs=[_rows(tm, D_FF), _rows(tm, D_FF)],
        out_shape=[wide, wide],
        semantics=("parallel",), args=(dh, wdt, a, b), exchange=exchange)


def _outproj_bwd(dh1, woutt, yb, gb, head_sum, exchange=()):
    s_len = dh1.shape[0]
    tm = ROW_TILE
    n_br = len(DILATIONS)

    def body(dh_ref, w_ref, yb_ref, gb_ref, e_ref, dya_ref, dgb_ref, *rest):
        do_refs, dd_refs = rest[:n_br], rest[n_br:2 * n_br]
        do_nat, dd_nat = rest[2 * n_br:]

        @pl.when(pl.program_id(0) == 0)
        def _():
            dgb_ref[...] = jnp.zeros_like(dgb_ref)

        dhb = dh_ref[...].astype(BF16)
        dya_ref[...] = lax.dot_general(dhb, w_ref[:WIDTH_A, :], _NT, preferred_element_type=F32)
        dyn = lax.dot_general(dhb, w_ref[WIDTH_A:, :], _NT, preferred_element_type=F32)
        ybv = yb_ref[...]
        dyb, dg_rows = _norm_bwd(dyn, ybv, gb_ref[...])
        dgb_ref[...] += jnp.sum(dg_rows, axis=0, keepdims=True)
        prod = dyb * ybv
        hi = prod.astype(BF16)
        lo = (prod - hi.astype(F32)).astype(BF16)
        dd_nat[0] = (jnp.dot(hi, e_ref[...], preferred_element_type=F32)
                     + jnp.dot(lo, e_ref[...], preferred_element_type=F32))
        for i, d in enumerate(DILATIONS):
            _to_sub(dd_nat, 0, dd_refs[i], 0, d, tm)
        for cb in range(N_PAIRS):
            piece = dyb[:, cb * 128:(cb + 1) * 128]
            do_nat[cb] = piece
            do_refs[0][0, cb] = piece.astype(BF16)
            for i, d in enumerate(DILATIONS[1:], start=1):
                _to_sub(do_nat, cb, do_refs[i], cb, d, tm)

    return _call(
        body, name="outproj_bwd", grid=(s_len // tm,),
        in_specs=[_rows(tm, D_MODEL), _resident((D_MODEL, D_MODEL)), _rows(tm, WIDTH_B), _whole((1, WIDTH_B)), _whole((WIDTH_B, 128))],
        out_specs=[_rows(tm, WIDTH_A), _whole((1, WIDTH_B))] + [_sub_spec(d, N_PAIRS, tm) for d in DILATIONS]
        + [_sub_spec(d, 1, tm) for d in DILATIONS],
        out_shape=[jax.ShapeDtypeStruct((s_len, WIDTH_A), F32), jax.ShapeDtypeStruct((1, WIDTH_B), F32)]
        + [_sub_shape(s_len, d, N_PAIRS, BF16) for d in DILATIONS] + [_sub_shape(s_len, d, 1, F32) for d in DILATIONS],
        scratch_shapes=[pltpu.VMEM((N_PAIRS, tm, 128), F32), pltpu.VMEM((1, tm, 128), F32)],
        semantics=("arbitrary",), args=(dh1, woutt, yb, gb, head_sum), exchange=exchange)


def _attn_bwd(qkv, do, lse, dd, d, exchange=()):
    sd = qkv.shape[2]
    tile, nb, n_tiles, res = _attn_geometry(sd)
    last_block = sd // CHUNK - 1

    def nxt(n):
        return jnp.minimum((n + 1) * nb, last_block)

    def block(ref, next_ref, j):
        return ref[j * CHUNK:(j + 1) * CHUNK, :] if j < nb else next_ref[...]

    def body(q_ref, qn_ref, k_ref, v_ref, do_ref, don_ref, l_ref, ln_ref, dd_ref, ddn_ref,
             dq_ref, dk_ref, dv_ref, carry_ref):
        for rr in range(res):
            l_t = [block(l_ref.at[rr], ln_ref.at[rr], j).T for j in range(nb + 1)]
            dd_t = [block(dd_ref.at[rr], ddn_ref.at[rr], j).T for j in range(nb + 1)]
            for hp in range(PAIRS_PER_STEP):
                l_rows = [jnp.concatenate([t[2 * hp:2 * hp + 1, :], t[2 * hp + 1:2 * hp + 2, :]], axis=1) for t in l_t]
                dd_rows = [jnp.concatenate([t[2 * hp:2 * hp + 1, :], t[2 * hp + 1:2 * hp + 2, :]], axis=1) for t in dd_t]
                one_pair(q_ref.at[rr, hp], qn_ref.at[rr, hp], k_ref.at[rr, hp], v_ref.at[rr, hp], do_ref.at[rr, hp],
                         don_ref.at[rr, hp], l_rows, dd_rows, dq_ref.at[rr, hp], dk_ref.at[rr, hp], dv_ref.at[rr, hp],
                         carry_ref.at[rr, hp])

    def one_pair(q_ref, qn_ref, k_ref, v_ref, do_ref, don_ref, l_rows, dd_rows, dq_ref, dk_ref, dv_ref, carry_ref):
        n = pl.program_id(2)

        @pl.when(n == 0)
        def _():
            carry_ref[...] = jnp.zeros_like(carry_ref)

        head_a = lax.broadcasted_iota(jnp.int32, (CHUNK, 128), 1) < HEAD_DIM
        col = lax.broadcasted_iota(jnp.int32, (CHUNK, 4 * CHUNK), 1)
        qi = col % CHUNK
        ki = lax.broadcasted_iota(jnp.int32, (CHUNK, 4 * CHUNK), 0)
        is_after = col >= 2 * CHUNK
        mask = (is_after & (ki >= qi)) | (jnp.logical_not(is_after) & (qi >= ki))
        mask_last = mask & jnp.logical_or(jnp.logical_not(is_after), n < n_tiles - 1)
        dq_acc = [carry_ref[...]] + [jnp.zeros((CHUNK, 128), F32) for _ in range(nb)]

        q_st = [jnp.concatenate(_both_heads(block(q_ref, qn_ref, j), head_a), axis=0) for j in range(nb + 1)]
        do_st = [jnp.concatenate(_both_heads(block(do_ref, don_ref, j), head_a), axis=0) for j in range(nb + 1)]

        for j in range(nb):
            rows = slice(j * CHUNK, (j + 1) * CHUNK)
            kj = k_ref[rows, :]
            vj = v_ref[rows, :]
            msk = mask if j + 1 < nb else mask_last
            qs = jnp.concatenate([q_st[j], q_st[j + 1]], axis=0)
            dos = jnp.concatenate([do_st[j], do_st[j + 1]], axis=0)
            ls = jnp.concatenate([l_rows[j], l_rows[j + 1]], axis=1)
            dds = jnp.concatenate([dd_rows[j], dd_rows[j + 1]], axis=1)
            st = lax.dot_general(kj, qs, _NT, preferred_element_type=F32)
            pt = jnp.exp(jnp.where(msk, st - ls, NEG))
            dpt = lax.dot_general(vj, dos, _NT, preferred_element_type=F32)
            dst = (pt * (dpt - dds)).astype(BF16)
            dv_ref[rows, :] = jnp.dot(pt.astype(BF16), dos, preferred_element_type=F32).astype(BF16)
            dk_ref[rows, :] = jnp.dot(dst, qs, preferred_element_type=F32).astype(BF16)
            dqs = lax.dot_general(dst, kj, _TN, preferred_element_type=F32)
            dq_acc[j] = dq_acc[j] + jnp.where(head_a, dqs[:CHUNK], dqs[CHUNK:2 * CHUNK])
            dq_acc[j + 1] = dq_acc[j + 1] + jnp.where(head_a, dqs[2 * CHUNK:3 * CHUNK], dqs[3 * CHUNK:])
        for j in range(nb):
            dq_ref[j * CHUNK:(j + 1) * CHUNK, :] = dq_acc[j].astype(BF16)
        carry_ref[...] = dq_acc[nb]

    same = lambda n: n
    grad = jax.ShapeDtypeStruct((d, N_PAIRS, sd, 128), BF16)
    return _call(
        body, name=f"attn_bwd_d{d}", grid=(d // res, N_PAIRS // PAIRS_PER_STEP, n_tiles),
        in_specs=[_attn_spec(0, tile, same, res), _attn_spec(0, CHUNK, nxt, res), _attn_spec(N_PAIRS, tile, same, res),
                  _attn_spec(2 * N_PAIRS, tile, same, res), _attn_spec(0, tile, same, res), _attn_spec(0, CHUNK, nxt, res),
                  _stats_spec(tile, same, res), _stats_spec(CHUNK, nxt, res), _stats_spec(tile, same, res),
                  _stats_spec(CHUNK, nxt, res)],
        out_specs=[_attn_spec(0, tile, same, res)] * 3,
        out_shape=[grad, grad, grad],
        scratch_shapes=[pltpu.VMEM((res, PAIRS_PER_STEP, CHUNK, 128), F32)],
        semantics=("parallel", "parallel", "arbitrary"), args=(qkv, qkv, qkv, qkv, do, do, lse, lse, dd, dd), exchange=exchange)


def _sgu_bwd(ua, sw, b2, gs, ga, dya_n):
    s_len = ua.shape[0]
    tm = ROW_TILE

    def body(ua_ref, sw_ref, b2_ref, gs_ref, ga_ref, dy_ref, dua_ref, dsw_ref, db2_ref, dgs_ref, dga_ref):
        @pl.when(pl.program_id(0) == 0)
        def _():
            dsw_ref[...] = jnp.zeros_like(dsw_ref)
            db2_ref[...] = jnp.zeros_like(db2_ref)
            dgs_ref[...] = jnp.zeros_like(dgs_ref)
            dga_ref[...] = jnp.zeros_like(dga_ref)

        u, va, ug, xhat, rstd, vn = _sgu_core(ua_ref, gs_ref)
        wm, keep = _sgu_mix_weights(sw_ref)
        head = lax.broadcasted_iota(jnp.int32, (CHUNK, WIDTH_A), 1) // HEAD_DIM
        gav = ga_ref[...]
        gsv = gs_ref[...]
        dga = jnp.zeros((1, WIDTH_A), F32)
        dgs = jnp.zeros((1, WIDTH_A), F32)
        db2 = jnp.zeros((CHUNK, WIDTH_A), F32)
        dsw = [jnp.zeros((CHUNK, CHUNK), F32) for _ in range(4)]
        for c in range(tm // CHUNK):
            rows = slice(c * CHUNK, (c + 1) * CHUNK)
            vnc = vn[rows]
            vnb = vnc.astype(BF16)
            mixed = b2_ref[...]
            for h in range(4):
                mixed = mixed + jnp.dot(wm[h], jnp.where(head == h, vnc, 0.0).astype(BF16), preferred_element_type=F32)
            ugc = ug[rows]
            dya, dga_rows = _norm_bwd(dy_ref[rows, :], ugc * mixed, gav)
            dga = dga + jnp.sum(dga_rows, axis=0, keepdims=True)
            dmixed = dya * ugc
            db2 = db2 + dmixed
            dvn = jnp.zeros((CHUNK, WIDTH_A), F32)
            for h in range(4):
                dmh = jnp.where(head == h, dmixed, 0.0).astype(BF16)
                dsw[h] = dsw[h] + lax.dot_general(dmh, vnb, _NT, preferred_element_type=F32)
                dvn = dvn + lax.dot_general(wm[h], dmh, _TN, preferred_element_type=F32)
            xh = xhat[rows]
            dgs = dgs + jnp.sum(dvn * xh, axis=0, keepdims=True)
            dxh = dvn * gsv
            dvg = rstd[rows] * (dxh - jnp.mean(dxh, axis=-1, keepdims=True) - xh * jnp.mean(dxh * xh, axis=-1, keepdims=True))
            dua_ref[rows, :WIDTH_A] = (dya * mixed * _gelu_grad(u[rows])).astype(BF16)
            dua_ref[rows, WIDTH_A:] = (dvg * _gelu_grad(va[rows])).astype(BF16)
        for h in range(4):
            dsw_ref[h] += jnp.where(keep, dsw[h], 0.0)
        db2_ref[...] += db2
        dgs_ref[...] += dgs
        dga_ref[...] += dga

    return pl.pallas_call(
        body, name="sgu_bwd", grid=(s_len // tm,),
        in_specs=[_rows(tm, 2 * WIDTH_A), _whole((4, CHUNK, CHUNK)), _whole((CHUNK, WIDTH_A)), _whole((1, WIDTH_A)),
                  _whole((1, WIDTH_A)), _rows(tm, WIDTH_A)],
        out_specs=[_rows(tm, 2 * WIDTH_A), _whole((4, CHUNK, CHUNK)), _whole((CHUNK, WIDTH_A)), _whole((1, WIDTH_A)), _whole((1, WIDTH_A))],
        out_shape=[jax.ShapeDtypeStruct((s_len, 2 * WIDTH_A), BF16), jax.ShapeDtypeStruct((4, CHUNK, CHUNK), F32),
                   jax.ShapeDtypeStruct((CHUNK, WIDTH_A), F32), jax.ShapeDtypeStruct((1, WIDTH_A), F32),
                   jax.ShapeDtypeStruct((1, WIDTH_A), F32)],
        compiler_params=_params("arbitrary"),
    )(ua, sw, b2, gs, ga, dya_n)


def _dproj(dua, dqs, dks, dvs, cos, sin, hn1, exchange=()):
    s_len = dua.shape[0]
    tm = ROW_TILE
    n_br = len(DILATIONS)
    n_steps = s_len // tm

    def body(dua_ref, *rest):
        groups = [rest[g * n_br:(g + 1) * n_br] for g in range(3)]
        cos_ref, sin_ref, hn_ref, out_ref, dw_ref, acc, dw_acc = rest[3 * n_br:]
        step = pl.program_id(0)

        @pl.when(step == 0)
        def _():
            dw_acc[...] = jnp.zeros_like(dw_acc)

        out_ref[:, :2 * WIDTH_A] = dua_ref[...]
        c = cos_ref[...]
        s = sin_ref[...]
        first_half = (lax.broadcasted_iota(jnp.int32, (tm, 128), 1) % HEAD_DIM) < HEAD_DIM // 2
        for g, refs in enumerate(groups):
            for cb in range(N_PAIRS):
                t = refs[0][0, cb].astype(F32)
                for i, d in enumerate(DILATIONS[1:]):
                    _from_sub(refs[i + 1], cb, acc, i, d, tm)
                    t = t + acc[i]
                if g < 2:
                    t = (t * c - _swap_halves(t, first_half) * s) * (0.125 if g == 0 else 1.0)
                col = 2 * WIDTH_A + g * WIDTH_B + cb * 128
                out_ref[:, col:col + 128] = t.astype(BF16)
        hn = hn_ref[...]
        for j in range(IN_COLS // 256):
            cols = slice(j * 256, (j + 1) * 256)
            dw_acc[cols, :] += lax.dot_general(out_ref[:, cols], hn, _TN, preferred_element_type=F32)

        @pl.when(step == n_steps - 1)
        def _():
            dw_ref[...] = dw_acc[...].astype(BF16)

    subs = [_sub_spec(d, N_PAIRS, tm) for d in DILATIONS]
    (dproj, dw), received = _call(
        body, name="dproj_dw_in", grid=(n_steps,),
        in_specs=[_rows(tm, 2 * WIDTH_A)] + subs * 3 + [_rows(tm, 128), _rows(tm, 128), _rows(tm, D_MODEL)],
        out_specs=[_rows(tm, IN_COLS), _whole((IN_COLS, D_MODEL))],
        out_shape=[jax.ShapeDtypeStruct((s_len, IN_COLS), BF16), jax.ShapeDtypeStruct((IN_COLS, D_MODEL), BF16)],
        scratch_shapes=[pltpu.VMEM((n_br - 1, tm, 128), F32), pltpu.VMEM((IN_COLS, D_MODEL), F32)],
        semantics=("arbitrary",), args=(dua, *dqs, *dks, *dvs, cos, sin, hn1), exchange=exchange)
    return dproj, dw, received


def _mm_tn(a, b, name, exchange=()):
    s_len, m = a.shape
    n = b.shape[1]
    tk = 2 * ROW_TILE
    tm = m if m <= 512 else (1408 if m == D_FF else 512)
    n_k = s_len // tk

    def body(a_ref, b_ref, o_ref, acc_ref):
        k = pl.program_id(1)

        @pl.when(k == 0)
        def _():
            acc_ref[...] = jnp.zeros_like(acc_ref)

        acc_ref[...] += lax.dot_general(a_ref[...].astype(BF16), b_ref[...].astype(BF16), _TN, preferred_element_type=F32)

        @pl.when(k == n_k - 1)
        def _():
            o_ref[...] = acc_ref[...].astype(BF16)

    (grad,), received = _call(
        body, name=name, grid=(m // tm, n_k),
        in_specs=[pl.BlockSpec((tk, tm), lambda i, k: (k, i)), pl.BlockSpec((tk, n), lambda i, k: (k, 0))],
        out_specs=[pl.BlockSpec((tm, n), lambda i, k: (i, 0))],
        out_shape=[jax.ShapeDtypeStruct((m, n), BF16)],
        scratch_shapes=[pltpu.VMEM((tm, n), F32)],
        semantics=("parallel", "arbitrary"), args=(a, b), exchange=exchange)
    return grad, received


def _position():
    x, y, c = lax.axis_index("x"), lax.axis_index("y"), lax.axis_index("c")
    return x, y, c, 4 * x + 2 * y + c


def _peer(x, y, c, rel):
    return (x ^ ((rel >> 2) & 1), y ^ ((rel >> 1) & 1), c ^ (rel & 1))


def _exchange_out_shape(kind, arr):
    return jax.ShapeDtypeStruct(((N_DEV,) + arr.shape) if kind == "gather" else arr.shape, arr.dtype)


def _exchange_sems(n_items):
    return [pltpu.SemaphoreType.DMA((n_items, N_DEV)), pltpu.SemaphoreType.DMA((n_items, N_DEV)), pltpu.SemaphoreType.DMA((n_items,))]


def _exchange_copies(kinds, srcs, dsts, sems, arrivals):
    send_sems, recv_sems, local_sems = sems
    x, y, c, me = _position()
    local, sends, recvs = [], [], []
    for k, (kind, src, dst) in enumerate(zip(kinds, srcs, dsts)):
        own = src if kind == "gather" else src.at[me]
        local.append(pltpu.make_async_copy(own, dst.at[me], local_sems.at[k]))
        for rel in range(1, N_DEV):
            going = src if kind == "gather" else src.at[me ^ rel]
            common = dict(send_sem=send_sems.at[k, rel], recv_sem=recv_sems.at[k, rel],
                          device_id=_peer(x, y, c, rel), device_id_type=MESH)
            sends.append(pltpu.make_async_remote_copy(src_ref=going, dst_ref=dst.at[me], **common))
            if arrivals:
                recvs.append(pltpu.make_async_remote_copy(src_ref=own, dst_ref=dst.at[me ^ rel], **common))
    return local, sends, recvs


def _exchange_start(kinds, srcs, dsts, sems):
    local, sends, _ = _exchange_copies(kinds, srcs, dsts, sems, arrivals=False)
    for cp in local + sends:
        cp.start()


def _exchange_finish(kinds, srcs, dsts, sems):
    local, sends, recvs = _exchange_copies(kinds, srcs, dsts, sems, arrivals=True)
    for cp in recvs:
        cp.wait_recv()
    for cp in sends:
        cp.wait_send()
    for cp in local:
        cp.wait()


def _exchange_only(items, name):
    kinds = [k for k, _ in items]
    n = len(items)

    def body(*refs):
        srcs, dsts, sems = refs[:n], refs[n:2 * n], refs[2 * n:]
        _exchange_start(kinds, srcs, dsts, sems)
        _exchange_finish(kinds, srcs, dsts, sems)

    any_spec = pl.BlockSpec(memory_space=pl.ANY)
    return pl.pallas_call(
        body, name=name, in_specs=[any_spec] * n, out_specs=[any_spec] * n,
        out_shape=[_exchange_out_shape(k, a) for k, a in items],
        scratch_shapes=_exchange_sems(n),
        compiler_params=pltpu.CompilerParams(has_side_effects=True),
    )(*[a for _, a in items])


def _gather_two_level_with_rope_tables(shard, inv_freq, s_len, name):
    rows = ROW_TILE

    def body(inv_ref, src, cos_ref, sin_ref, dst, send_sems, recv_sems, local_sem):
        x, y, c, me = _position()
        sibling = (x, y, 1 - c)
        chips = [(1 - x, y), (x, 1 - y), (1 - x, 1 - y)]

        def block(px, py, pc):
            return dst.at[4 * px + 2 * py + pc]

        def copy(k, blk, to, src_ref=None):
            return pltpu.make_async_remote_copy(
                src_ref=block(*blk) if src_ref is None else src_ref, dst_ref=block(*blk),
                send_sem=send_sems.at[k], recv_sem=recv_sems.at[k], device_id=to, device_id_type=MESH)

        mine = pltpu.make_async_copy(src, dst.at[me], local_sem)
        mine.start()
        first = [copy(0, (x, y, c), sibling, src)] + [copy(1 + j, (x, y, c), (*chip, c), src) for j, chip in enumerate(chips)]
        for cp in first:
            cp.start()

        inv = inv_ref[...]
        lane = lax.broadcasted_iota(jnp.int32, (rows, 128), 1)
        sign = jnp.where((lane // (HEAD_DIM // 2)) % 2 == 0, -1.0, 1.0)
        row = lax.broadcasted_iota(jnp.int32, (rows, 128), 0)

        @pl.loop(0, s_len // rows)
        def _(i):
            at = pl.multiple_of(i * rows, rows)
            ang = (row + at).astype(F32) * inv
            cos_ref[pl.ds(at, rows), :] = jnp.cos(ang)
            sin_ref[pl.ds(at, rows), :] = jnp.sin(ang) * sign

        passed = [copy(4 + j, (*chip, c), sibling) for j, chip in enumerate(chips)]
        for j, chip in enumerate(chips):
            copy(1 + j, (*chip, c), (x, y, c)).wait_recv()
            passed[j].start()
        copy(0, (x, y, 1 - c), (x, y, c)).wait_recv()
        for j, chip in enumerate(chips):
            copy(4 + j, (*chip, 1 - c), (x, y, c)).wait_recv()
        for cp in first + passed:
            cp.wait_send()
        mine.wait()

    any_spec = pl.BlockSpec(memory_space=pl.ANY)
    vmem = pl.BlockSpec(memory_space=pltpu.VMEM)
    table = jax.ShapeDtypeStruct((s_len, 128), F32)
    return pl.pallas_call(
        body, name=name, in_specs=[vmem, any_spec], out_specs=[vmem, vmem, any_spec],
        out_shape=[table, table, _exchange_out_shape("gather", shard)],
        scratch_shapes=[pltpu.SemaphoreType.DMA((N_DEV - 1,)), pltpu.SemaphoreType.DMA((N_DEV - 1,)), pltpu.SemaphoreType.DMA],
        compiler_params=pltpu.CompilerParams(has_side_effects=True, vmem_limit_bytes=V7X_VMEM_LIMIT_BYTES),
    )(inv_freq, shard)


def _call(body, *, name, grid, in_specs, out_specs, out_shape, args, scratch_shapes=(), semantics, exchange=()):
    if not exchange:
        outs = pl.pallas_call(body, name=name, grid=grid, in_specs=in_specs, out_specs=out_specs, out_shape=out_shape,
                              scratch_shapes=list(scratch_shapes), compiler_params=_params(*semantics))(*args)
        return outs, []
    kinds = [k for k, _ in exchange]
    n_in, n_out, n_x, n_scr = len(in_specs), len(out_specs), len(exchange), len(scratch_shapes)

    def wrapped(*refs):
        ins, refs = refs[:n_in], refs[n_in:]
        srcs, refs = refs[:n_x], refs[n_x:]
        outs, refs = refs[:n_out], refs[n_out:]
        dsts, refs = refs[:n_x], refs[n_x:]
        scratch, sems = refs[:n_scr], refs[n_scr:]
        ids = [pl.program_id(a) for a in range(len(grid))]
        first = functools.reduce(jnp.logical_and, [i == 0 for i in ids])
        last = functools.reduce(jnp.logical_and, [i == g - 1 for i, g in zip(ids, grid)])

        @pl.when(first)
        def _():
            _exchange_start(kinds, srcs, dsts, sems)

        body(*ins, *outs, *scratch)

        @pl.when(last)
        def _():
            _exchange_finish(kinds, srcs, dsts, sems)

    any_spec = pl.BlockSpec(memory_space=pl.ANY)
    outs = pl.pallas_call(
        wrapped, name=name, grid=grid,
        in_specs=list(in_specs) + [any_spec] * n_x, out_specs=list(out_specs) + [any_spec] * n_x,
        out_shape=list(out_shape) + [_exchange_out_shape(k, a) for k, a in exchange],
        scratch_shapes=list(scratch_shapes) + _exchange_sems(n_x),
        compiler_params=pltpu.CompilerParams(dimension_semantics=("arbitrary",) * len(grid),
                                             vmem_limit_bytes=V7X_VMEM_LIMIT_BYTES, has_side_effects=True),
    )(*args, *[a for _, a in exchange])
    return outs[:n_out], outs[n_out:]


def _adamw_math(w, g, m, v):
    m = ADAM_B1 * m + (1.0 - ADAM_B1) * g
    v = ADAM_B2 * v + (1.0 - ADAM_B2) * (g * g)
    m_hat = m / (1.0 - ADAM_B1 ** ADAM_STEP)
    v_hat = v / (1.0 - ADAM_B2 ** ADAM_STEP)
    return -ADAM_LR * (m_hat / (jnp.sqrt(v_hat) + ADAM_EPS) + ADAM_WD * w), m, v


def _adamw(parts, w, m, v, name):
    rows, cols = w.shape
    tm = 256 if rows % 256 == 0 and rows > 256 else rows

    def body(p_ref, w_ref, m_ref, v_ref, g_ref, d_ref, nm_ref, nv_ref):
        g = p_ref[0].astype(F32)
        for j in range(1, N_DEV):
            g = g + p_ref[j].astype(F32)
        delta, nm, nv = _adamw_math(w_ref[...], g, m_ref[...], v_ref[...])
        g_ref[...] = g
        d_ref[...] = delta
        nm_ref[...] = nm
        nv_ref[...] = nv

    shard = jax.ShapeDtypeStruct((rows, cols), F32)
    return pl.pallas_call(
        body, name=name, grid=(rows // tm,),
        in_specs=[pl.BlockSpec((N_DEV, tm, cols), lambda i: (0, i, 0))] + [_rows(tm, cols)] * 3,
        out_specs=[_rows(tm, cols)] * 4,
        out_shape=[shard] * 4,
        compiler_params=_params("parallel"),
    )(parts, w, m, v)


_SMALL = ("mix_norm_g", "sgu_w", "sgu_b", "sgu_norm_g", "out_norm_a", "out_norm_b", "ffn_norm_g", "ple_norm_g", "final_norm_g")
_BIG = ("w_in", "w_out", "w_gate", "w_up", "w_down", "w_ple_gate", "w_ple_proj")
_COLUMN_SHARDED = ("w_in", "w_gate", "w_up", "w_ple_proj")
_ORDER = ("mix_norm_g", "w_in", "sgu_w", "sgu_b", "sgu_norm_g", "out_norm_a", "out_norm_b", "w_out", "ffn_norm_g",
          "w_gate", "w_up", "w_down", "ple_norm_g", "w_ple_gate", "w_ple_proj", "final_norm_g")


def _pack_small(values, names=_SMALL):
    flat = jnp.concatenate([values[n].reshape(-1).astype(F32) for n in names])
    pad = (-flat.shape[0]) % (8 * 128)
    return jnp.pad(flat, (0, pad)).reshape(-1, 128)


def _unpack_small(packed, like):
    flat = packed.reshape(-1)
    out, at = {}, 0
    for n in _SMALL:
        size = like[n].size
        out[n] = flat[at:at + size].reshape(like[n].shape)
        at += size
    return out


def _own_orientation(name, value):
    return value[0].T if name in _COLUMN_SHARDED else value[0]


def _reference_orientation(name, value):
    return (value.T if name in _COLUMN_SHARDED else value)[None]


def _full_from_gathered(gathered):
    return gathered.reshape(N_DEV * gathered.shape[1], gathered.shape[2])


def _sliced_for_devices(grad):
    return grad.reshape(N_DEV, grad.shape[0] // N_DEV, grad.shape[1])


def _rope_inv_freq():
    half = HEAD_DIM // 2
    inv = ROPE_THETA ** (-jnp.arange(half, dtype=F32) / half)
    return jnp.tile(inv, 128 // half)[None, :]


def _forward_backward(x, p, target, small, shards):
    def gather(*names):
        return [("gather", shards[n]) for n in names]

    def scatter(**grads):
        return [("scatter", _sliced_for_devices(g)) for g in grads.values()]

    full, parts = {}, {}
    s_len = x.shape[0]
    cos, sin, got = _gather_two_level_with_rope_tables(shards["w_in"], _rope_inv_freq(), s_len, "gather_w_in")
    full["w_in"] = _full_from_gathered(got)

    g_mix, g_ffn, g_ple = small["mix_norm_g"], small["ffn_norm_g"], small["ple_norm_g"]
    g_fin = small["final_norm_g"].reshape(1, D_MODEL)
    sw, gs, ga, gb = small["sgu_w"], small["sgu_norm_g"], small["out_norm_a"], small["out_norm_b"]
    b2 = jnp.repeat(small["sgu_b"].T, HEAD_DIM, axis=1)
    head_sum = (jnp.arange(WIDTH_B)[:, None] // HEAD_DIM == jnp.arange(128)[None, :]).astype(BF16)
    n_br = len(DILATIONS)

    def arrived(names, got):
        for n, g in zip(names, got):
            full[n] = _full_from_gathered(g)

    (ua, hn1, *qkv), got = _inproj(x, g_mix, full["w_in"], cos, sin, exchange=gather("w_gate"))
    arrived(("w_gate",), got)
    ya_n = _sgu_fwd(ua, sw, b2, gs, ga)
    half = shards["w_up"].shape[0] // 2
    riders = [[("gather", shards["w_up"][:half])], [("gather", shards["w_up"][half:])], gather("w_out")]
    branch, got = [], []
    for i, d in enumerate(DILATIONS):
        o_l, g = _attn_fwd(qkv[i], d, exchange=riders[i])
        branch.append(o_l)
        got += g
    arrived(("w_up", "w_out"), [jnp.concatenate(got[:2], axis=1), got[2]])
    (y, yb, *lse), _ = _combine([o for o, _ in branch], [l for _, l in branch], ya_n, gb, head_sum.T)
    last_wave = ("w_down", "w_ple_gate", "w_ple_proj")
    (h1, a, b, act, hn2), got = _ffn_up(y, full["w_out"], x, g_ffn, full["w_gate"], full["w_up"], exchange=gather(*last_wave))
    arrived(last_wave, got)
    h2, h3, gate, pp, hn3 = _ffn_down_ple(act, full["w_down"], h1, g_ple, full["w_ple_gate"], p, full["w_ple_proj"])

    dh2, loss, d_fin, d_ple, g_ple_gate, g_ple_proj = _loss_ple_bwd(
        h3, target, g_fin, gate, pp, h2, g_ple, full["w_ple_gate"], hn3, p)
    g_down, (parts["w_ple_gate"], parts["w_ple_proj"]) = _mm_tn(
        act, dh2, "dw_down", exchange=scatter(w_ple_gate=g_ple_gate, w_ple_proj=g_ple_proj))
    (da, db), (parts["w_down"],) = _ffn_down_bwd(dh2, full["w_down"], a, b, exchange=scatter(w_down=g_down))
    g_gate, _ = _mm_tn(da, hn2, "dw_gate")
    g_up, _ = _mm_tn(db, hn2, "dw_up")
    (dh1, d_ffn, g_out), (parts["w_gate"],) = _mm_norm_bwd(
        [(da, full["w_gate"]), (db, full["w_up"])], h1, g_ffn, dh2, "ffn_up_bwd", exchange=scatter(w_gate=g_gate), dw_lhs=y)
    (dya_n, d_gb, *do_dd), (parts["w_out"],) = _outproj_bwd(dh1, full["w_out"], yb, gb, head_sum, exchange=scatter(w_out=g_out))
    grads_b = []
    for i, d in enumerate(DILATIONS):
        g3, got = _attn_bwd(qkv[i], do_dd[i], lse[i], do_dd[n_br + i], d, exchange=scatter(w_up=g_up) if i == 0 else ())
        grads_b.append(g3)
        if i == 0:
            (parts["w_up"],) = got
    dua, d_sw, d_b2, d_gs, d_ga = _sgu_bwd(ua, sw, b2, gs, ga, dya_n)
    early = {
        "sgu_w": d_sw, "sgu_b": d_b2.reshape(CHUNK, 4, HEAD_DIM).sum(axis=-1).T, "sgu_norm_g": d_gs, "out_norm_a": d_ga,
        "out_norm_b": d_gb, "ffn_norm_g": d_ffn, "ple_norm_g": d_ple, "final_norm_g": d_fin,
    }
    dproj, g_in, (early_parts,) = _dproj(
        dua, [g[0] for g in grads_b], [g[1] for g in grads_b], [g[2] for g in grads_b], cos, sin, hn1,
        exchange=[("gather", _pack_small(early, _SMALL[1:]))])
    (dx, d_mix), (parts["w_in"],) = _mm_norm_bwd(
        [(dproj, full["w_in"])], x, g_mix, dh1, "inproj_bwd", exchange=scatter(w_in=g_in))
    late = jnp.concatenate([_pack_small({"mix_norm_g": d_mix}, _SMALL[:1]), jnp.broadcast_to(loss, (8, 128))])
    (late_parts,) = _exchange_only([("gather", late)], "gather_mix_norm_grad_and_loss")
    total_loss = jnp.sum(late_parts[:, 8, 0])
    return total_loss, dx, parts, jnp.concatenate([late_parts[:, :8], early_parts], axis=1)


def kernel(x, p, mix_norm_g, w_in, sgu_w, sgu_b, sgu_norm_g, out_norm_a, out_norm_b, w_out, ffn_norm_g, w_gate, w_up, w_down, ple_norm_g, w_ple_gate, w_ple_proj, final_norm_g, loss_target, m_mix_norm_g, m_w_in, m_sgu_w, m_sgu_b, m_sgu_norm_g, m_out_norm_a, m_out_norm_b, m_w_out, m_ffn_norm_g, m_w_gate, m_w_up, m_w_down, m_ple_norm_g, m_w_ple_gate, m_w_ple_proj, m_final_norm_g, v_mix_norm_g, v_w_in, v_sgu_w, v_sgu_b, v_sgu_norm_g, v_out_norm_a, v_out_norm_b, v_w_out, v_ffn_norm_g, v_w_gate, v_w_up, v_w_down, v_ple_norm_g, v_w_ple_gate, v_w_ple_proj, v_final_norm_g):
    given = dict(locals())
    weights = {n: given[n] for n in _ORDER}
    moments_m = {n: given["m_" + n] for n in _ORDER}
    moments_v = {n: given["v_" + n] for n in _ORDER}

    shards = {n: _own_orientation(n, weights[n]).astype(BF16) for n in _BIG}
    small = {n: (weights[n][0] if n in ("sgu_w", "sgu_b") else weights[n]) for n in _SMALL}

    loss, dx, parts, small_parts = _forward_backward(x[0], p[0, 0], loss_target[0], small, shards)

    small_like = {n: weights[n] for n in _SMALL}
    grads, deltas, new_m, new_v = {}, {}, {}, {}
    for n in _BIG:
        outs = _adamw(parts[n], _own_orientation(n, weights[n]), _own_orientation(n, moments_m[n]),
                      _own_orientation(n, moments_v[n]), "adamw_" + n)
        grads[n], deltas[n], new_m[n], new_v[n] = [_reference_orientation(n, o) for o in outs]
    g, d, nm, nv = _adamw(small_parts, _pack_small(small_like), _pack_small({n: moments_m[n] for n in _SMALL}),
                          _pack_small({n: moments_v[n] for n in _SMALL}), "adamw_small")
    for out, packed in ((grads, g), (deltas, d), (new_m, nm), (new_v, nv)):
        out.update(_unpack_small(packed, small_like))

    return (loss, dx[None], *[grads[n] for n in _ORDER], *[deltas[n] for n in _ORDER],
            *[new_m[n] for n in _ORDER], *[new_v[n] for n in _ORDER])
```

```python
import functools

import jax
import jax.numpy as jnp
from jax import lax
from jax.experimental import pallas as pl
from jax.experimental.pallas import tpu as pltpu

F32 = jnp.float32
BF16 = jnp.bfloat16

D_MODEL = 1024
WIDTH_A = 256
WIDTH_B = 768
D_FF = 2816
IN_COLS = 2 * WIDTH_A + 3 * WIDTH_B
PLE_DIM = 256
HEAD_DIM = 64
N_PAIRS = WIDTH_B // 128
CHUNK = 128
N_BACK = 128
DILATIONS = (1, 4, 16)
ROPE_THETA = 10000.0
EPS = 1e-6
N_DEV = 8

ADAM_LR = 0.001
ADAM_B1 = 0.9
ADAM_B2 = 0.999
ADAM_EPS = 1e-08
ADAM_WD = 0.01
ADAM_STEP = 10

V7X_VMEM_LIMIT_BYTES = 56 * 1024 * 1024
ROW_TILE = 512
MESH = pl.DeviceIdType.MESH
NEG = -1e30

_NT = (((1,), (1,)), ((), ()))
_TN = (((0,), (0,)), ((), ()))


def _params(*semantics):
    return pltpu.CompilerParams(dimension_semantics=semantics, vmem_limit_bytes=V7X_VMEM_LIMIT_BYTES)


def _rows(tm, width):
    return pl.BlockSpec((tm, width), lambda i: (i, 0))


def _whole(shape):
    return pl.BlockSpec(shape, lambda *_: (0,) * len(shape))


def _resident(shape):
    return pl.BlockSpec(shape, lambda *_: (0,) * len(shape), pipeline_mode=pl.Buffered(1))


def _gelu(x):
    t = jnp.tanh(0.7978845608028654 * (x + 0.044715 * (x * x * x)))
    return 0.5 * x * (1.0 + t)


def _gelu_grad(x):
    t = jnp.tanh(0.7978845608028654 * (x + 0.044715 * (x * x * x)))
    return 0.5 * (1.0 + t) + 0.5 * x * (1.0 - t * t) * (0.7978845608028654 * (1.0 + 3.0 * 0.044715 * (x * x)))


def _rstd(x):
    return lax.rsqrt(jnp.mean(x * x, axis=-1, keepdims=True) + EPS)


def _norm_bwd(dn, h, g, r=None):
    r = _rstd(h) if r is None else r
    n = h * r
    t = dn * g
    return r * (t - n * jnp.mean(t * n, axis=-1, keepdims=True)), dn * n


def _swap_halves(x, first_half):
    return jnp.where(first_half, pltpu.roll(x, 96, 1), pltpu.roll(x, 32, 1))


def _sub_spec(d, n_cb, tm):
    return pl.BlockSpec((d, n_cb, tm // d, 128), lambda i: (0, 0, i, 0))


def _sub_shape(s_len, d, n_cb, dtype):
    return jax.ShapeDtypeStruct((d, n_cb, s_len // d, 128), dtype)


def _to_sub(stage_ref, cb_src, out_ref, cb_dst, d, tm):
    slab = stage_ref.at[cb_src]
    for r in range(d):
        out_ref[r, cb_dst] = slab[pl.ds(r, tm // d, stride=d), :].astype(out_ref.dtype)


def _from_sub(in_ref, cb_src, stage_ref, cb_dst, d, tm):
    slab = stage_ref.at[cb_dst]
    for r in range(d):
        slab[pl.ds(r, tm // d, stride=d), :] = in_ref[r, cb_src].astype(F32)


def _inproj(x, g, w, cos, sin, exchange=()):
    s_len = x.shape[0]
    tm = ROW_TILE
    n_cb = 3 * N_PAIRS

    def body(x_ref, g_ref, w_ref, cos_ref, sin_ref, ua_ref, hn_ref, *rest):
        sub_refs, stage = rest[:-1], rest[-1]
        xf = x_ref[...]
        hn = (xf * _rstd(xf) * g_ref[...]).astype(BF16)
        hn_ref[...] = hn
        c = cos_ref[...]
        s = sin_ref[...]
        first_half = (lax.broadcasted_iota(jnp.int32, (tm, 128), 1) % HEAD_DIM) < HEAD_DIM // 2
        for j in range(IN_COLS // 256):
            col = j * 256
            acc = lax.dot_general(hn, w_ref[col:col + 256, :], _NT, preferred_element_type=F32)
            if col < 2 * WIDTH_A:
                ua_ref[:, col:col + 256] = acc
                continue
            for half in range(2):
                cb = (col - 2 * WIDTH_A) // 128 + half
                t = acc[:, half * 128:(half + 1) * 128]
                if cb < 2 * N_PAIRS:
                    t = (t * c + _swap_halves(t, first_half) * s) * (0.125 if cb < N_PAIRS else 1.0)
                stage[cb] = t
                sub_refs[0][0, cb] = t.astype(BF16)
        for cb in range(n_cb):
            for d, out_ref in zip(DILATIONS[1:], sub_refs[1:]):
                _to_sub(stage, cb, out_ref, cb, d, tm)

    return _call(
        body, name="inproj", grid=(s_len // tm,),
        in_specs=[_rows(tm, D_MODEL), _whole((1, D_MODEL)), _resident((IN_COLS, D_MODEL)), _rows(tm, 128), _rows(tm, 128)],
        out_specs=[_rows(tm, 2 * WIDTH_A), _rows(tm, D_MODEL)] + [_sub_spec(d, n_cb, tm) for d in DILATIONS],
        out_shape=[jax.ShapeDtypeStruct((s_len, 2 * WIDTH_A), F32), jax.ShapeDtypeStruct((s_len, D_MODEL), BF16)]
        + [_sub_shape(s_len, d, n_cb, BF16) for d in DILATIONS],
        scratch_shapes=[pltpu.VMEM((n_cb, tm, 128), F32)],
        semantics=("parallel",), args=(x, g, w, cos, sin), exchange=exchange)


def _sgu_mix_weights(sw_ref):
    keep = lax.broadcasted_iota(jnp.int32, (CHUNK, CHUNK), 0) >= lax.broadcasted_iota(jnp.int32, (CHUNK, CHUNK), 1)
    return [jnp.where(keep, sw_ref[h], 0.0).astype(BF16) for h in range(4)], keep


def _sgu_core(ua_ref, gs_ref):
    u = ua_ref[:, :WIDTH_A]
    va = ua_ref[:, WIDTH_A:]
    vg = _gelu(va)
    xc = vg - jnp.mean(vg, axis=-1, keepdims=True)
    rstd = lax.rsqrt(jnp.mean(xc * xc, axis=-1, keepdims=True) + EPS)
    xhat = xc * rstd
    return u, va, _gelu(u), xhat, rstd, xhat * gs_ref[...]


def _sgu_fwd(ua, sw, b2, gs, ga):
    s_len = ua.shape[0]
    tm = ROW_TILE

    def body(ua_ref, sw_ref, b2_ref, gs_ref, ga_ref, out_ref):
        _, _, ug, _, _, vn = _sgu_core(ua_ref, gs_ref)
        wm, _ = _sgu_mix_weights(sw_ref)
        head = lax.broadcasted_iota(jnp.int32, (CHUNK, WIDTH_A), 1) // HEAD_DIM
        for c in range(tm // CHUNK):
            rows = slice(c * CHUNK, (c + 1) * CHUNK)
            vnc = vn[rows]
            mixed = b2_ref[...]
            for h in range(4):
                mixed = mixed + jnp.dot(wm[h], jnp.where(head == h, vnc, 0.0).astype(BF16), preferred_element_type=F32)
            ya = ug[rows] * mixed
            out_ref[rows, :] = (ya * _rstd(ya) * ga_ref[...]).astype(BF16)

    return pl.pallas_call(
        body, name="sgu_fwd", grid=(s_len // tm,),
        in_specs=[_rows(tm, 2 * WIDTH_A), _whole((4, CHUNK, CHUNK)), _whole((CHUNK, WIDTH_A)), _whole((1, WIDTH_A)), _whole((1, WIDTH_A))],
        out_specs=_rows(tm, WIDTH_A),
        out_shape=jax.ShapeDtypeStruct((s_len, WIDTH_A), BF16),
        compiler_params=_params("parallel"),
    )(ua, sw, b2, gs, ga)


STEP_POSITIONS = 2 * ROW_TILE


def _attn_geometry(sd):
    tile = min(STEP_POSITIONS, sd)
    return tile, tile // CHUNK, sd // tile, STEP_POSITIONS // tile


PAIRS_PER_STEP = 6


def _attn_spec(cb0, rows, row_index, res):
    return pl.BlockSpec((res, PAIRS_PER_STEP, rows, 128), lambda r, g, n: (r, cb0 // PAIRS_PER_STEP + g, row_index(n), 0))


assert PAIRS_PER_STEP == N_PAIRS
assert DILATIONS[0] == 1


def _stats_spec(rows, row_index, res):
    return pl.BlockSpec((res, None, rows, 128), lambda r, g, n: (r, 0, row_index(n), 0))


def _stats_shape(sd, d):
    return jax.ShapeDtypeStruct((d, 1, sd, 128), F32)


def _both_heads(x, head_a):
    zero = jnp.zeros_like(x)
    return [jnp.where(head_a, x, zero), jnp.where(head_a, zero, x)]


def _attn_fwd(qkv, d, exchange=()):
    sd = qkv.shape[2]
    tile, nb, n_tiles, res = _attn_geometry(sd)

    def prev(n):
        return jnp.maximum(n * nb - 1, 0)

    def body(q_ref, k_ref, kp_ref, v_ref, vp_ref, o_ref, l_ref):
        for rr in range(res):
            for hp in range(PAIRS_PER_STEP):
                one_pair(hp, q_ref.at[rr, hp], k_ref.at[rr, hp], kp_ref.at[rr, hp], v_ref.at[rr, hp], vp_ref.at[rr, hp],
                         o_ref.at[rr, hp], l_ref.at[rr])

    def one_pair(hp, q_ref, k_ref, kp_ref, v_ref, vp_ref, o_ref, l_ref):
        n = pl.program_id(2)
        lane = lax.broadcasted_iota(jnp.int32, (CHUNK, 128), 1)
        head_a = lane < HEAD_DIM
        qi = lax.broadcasted_iota(jnp.int32, (2 * CHUNK, 2 * CHUNK), 0) % CHUNK
        kc = lax.broadcasted_iota(jnp.int32, (2 * CHUNK, 2 * CHUNK), 1)
        band = (kc >= qi) & (kc <= qi + N_BACK)
        for j in range(nb):
            rows = slice(j * CHUNK, (j + 1) * CHUNK)
            if j == 0:
                kcat = jnp.concatenate([kp_ref[...], k_ref[rows, :]], axis=0)
                vcat = jnp.concatenate([vp_ref[...], v_ref[rows, :]], axis=0)
                valid = band & jnp.logical_or(n > 0, kc >= CHUNK)
            else:
                kcat = k_ref[(j - 1) * CHUNK:(j + 1) * CHUNK, :]
                vcat = v_ref[(j - 1) * CHUNK:(j + 1) * CHUNK, :]
                valid = band
            q2 = jnp.concatenate(_both_heads(q_ref[rows, :], head_a), axis=0)
            s = lax.dot_general(q2, kcat, _NT, preferred_element_type=F32)
            s = jnp.where(valid, s, NEG)
            m = jnp.max(s, axis=-1, keepdims=True)
            p = jnp.exp(s - m)
            l = jnp.sum(p, axis=-1, keepdims=True)
            o2 = jnp.dot(p.astype(BF16), vcat, preferred_element_type=F32) / l
            lse2 = m + jnp.log(l)
            o_ref[rows, :] = jnp.where(head_a, o2[:CHUNK], o2[CHUNK:]).astype(BF16)
            others = l_ref[rows, :] if hp > 0 else jnp.zeros((CHUNK, 128), F32)
            l_ref[rows, :] = jnp.where(lane == 2 * hp, lse2[:CHUNK], jnp.where(lane == 2 * hp + 1, lse2[CHUNK:], others))

    same = lambda n: n
    return _call(
        body, name=f"attn_fwd_d{d}", grid=(d // res, N_PAIRS // PAIRS_PER_STEP, n_tiles),
        in_specs=[_attn_spec(0, tile, same, res), _attn_spec(N_PAIRS, tile, same, res), _attn_spec(N_PAIRS, CHUNK, prev, res),
                  _attn_spec(2 * N_PAIRS, tile, same, res), _attn_spec(2 * N_PAIRS, CHUNK, prev, res)],
        out_specs=[_attn_spec(0, tile, same, res), _stats_spec(tile, same, res)],
        out_shape=[jax.ShapeDtypeStruct((d, N_PAIRS, sd, 128), BF16), _stats_shape(sd, d)],
        semantics=("parallel", "parallel", "parallel"), args=(qkv, qkv, qkv, qkv, qkv), exchange=exchange)


def _combine(outs, lses, ya_n, gb, head_spread, exchange=()):
    s_len = ya_n.shape[0]
    tm = ROW_TILE
    n_br = len(DILATIONS)

    def body(*refs):
        o_refs, l_refs = refs[:n_br], refs[n_br:2 * n_br]
        ya_ref, gb_ref, spread_ref, y_ref, yb_ref = refs[2 * n_br:2 * n_br + 5]
        lse_refs = refs[2 * n_br + 5:3 * n_br + 5]
        o_nat, l_nat, lse_nat, w_wide = refs[3 * n_br + 5:]
        for i, d in enumerate(DILATIONS):
            _from_sub(l_refs[i], 0, l_nat, i, d, tm)
        ls = [l_nat[i] for i in range(n_br)]
        top = jnp.maximum(jnp.maximum(ls[0], ls[1]), ls[2])
        ws = [jnp.exp(l - top) for l in ls]
        den = ws[0] + ws[1] + ws[2]
        inv = 1.0 / den
        for i in range(n_br):
            w = ws[i] * inv
            hi = w.astype(BF16)
            lo = (w - hi.astype(F32)).astype(BF16)
            w_wide[i] = (jnp.dot(hi, spread_ref[...], preferred_element_type=F32)
                         + jnp.dot(lo, spread_ref[...], preferred_element_type=F32))
        lse_nat[0] = top + jnp.log(den)
        for d, lse_ref in zip(DILATIONS, lse_refs):
            _to_sub(lse_nat, 0, lse_ref, 0, d, tm)
        sumsq = jnp.zeros((tm, 1), F32)
        for cb in range(N_PAIRS):
            cols = slice(cb * 128, (cb + 1) * 128)
            yb = w_wide[0, :, cols] * o_refs[0][0, cb].astype(F32)
            for i, d in enumerate(DILATIONS[1:], start=1):
                _from_sub(o_refs[i], cb, o_nat, i, d, tm)
                yb = yb + w_wide[i, :, cols] * o_nat[i]
            yb_ref[:, cb * 128:(cb + 1) * 128] = yb
            sumsq = sumsq + jnp.sum(yb * yb, axis=-1, keepdims=True)
        r = lax.rsqrt(sumsq / WIDTH_B + EPS)
        y_ref[:, :WIDTH_A] = ya_ref[...]
        y_ref[:, WIDTH_A:] = (yb_ref[...] * r * gb_ref[...]).astype(BF16)

    stats = [_sub_spec(d, 1, tm) for d in DILATIONS]
    return _call(
        body, name="attn_combine", grid=(s_len // tm,),
        in_specs=[_sub_spec(d, N_PAIRS, tm) for d in DILATIONS] + stats
        + [_rows(tm, WIDTH_A), _whole((1, WIDTH_B)), _whole((128, WIDTH_B))],
        out_specs=[_rows(tm, D_MODEL), _rows(tm, WIDTH_B)] + stats,
        out_shape=[jax.ShapeDtypeStruct((s_len, D_MODEL), BF16), jax.ShapeDtypeStruct((s_len, WIDTH_B), F32)]
        + [_sub_shape(s_len, d, 1, F32) for d in DILATIONS],
        scratch_shapes=[pltpu.VMEM((n_br, tm, 128), F32), pltpu.VMEM((n_br, tm, 128), F32), pltpu.VMEM((1, tm, 128), F32),
                        pltpu.VMEM((n_br, tm, WIDTH_B), F32)],
        semantics=("parallel",), args=(*outs, *lses, ya_n, gb, head_spread), exchange=exchange)


def _ffn_up(y, wout, x, g, wg, wu, exchange=()):
    s_len = x.shape[0]
    tm = ROW_TILE

    def body(y_ref, wo_ref, x_ref, g_ref, wg_ref, wu_ref, h_ref, a_ref, b_ref, act_ref, hn_ref):
        hf = x_ref[...] + jnp.dot(y_ref[...], wo_ref[...], preferred_element_type=F32)
        h_ref[...] = hf
        hn = (hf * _rstd(hf) * g_ref[...]).astype(BF16)
        hn_ref[...] = hn
        for j in range(D_FF // 256):
            cols = slice(j * 256, (j + 1) * 256)
            a = lax.dot_general(hn, wg_ref[cols, :], _NT, preferred_element_type=F32)
            b = lax.dot_general(hn, wu_ref[cols, :], _NT, preferred_element_type=F32)
            a_ref[:, cols] = a.astype(BF16)
            b_ref[:, cols] = b.astype(BF16)
            act_ref[:, cols] = (a * jax.nn.sigmoid(a) * b).astype(BF16)

    wide = jax.ShapeDtypeStruct((s_len, D_FF), BF16)
    return _call(
        body, name="ffn_up", grid=(s_len // tm,),
        in_specs=[_rows(tm, D_MODEL), _resident((D_MODEL, D_MODEL)), _rows(tm, D_MODEL), _whole((1, D_MODEL)),
                  _resident((D_FF, D_MODEL)), _resident((D_FF, D_MODEL))],
        out_specs=[_rows(tm, D_MODEL), _rows(tm, D_FF), _rows(tm, D_FF), _rows(tm, D_FF), _rows(tm, D_MODEL)],
        out_shape=[jax.ShapeDtypeStruct((s_len, D_MODEL), F32), wide, wide, wide, jax.ShapeDtypeStruct((s_len, D_MODEL), BF16)],
        semantics=("parallel",), args=(y, wout, x, g, wg, wu), exchange=exchange)


def _ffn_down_ple(act, wd, h1, g, wpg, p, wpp):
    s_len = h1.shape[0]
    tm = ROW_TILE

    def body(act_ref, wd_ref, h1_ref, g_ref, wpg_ref, p_ref, wpp_ref, h2_ref, h3_ref, gate_ref, pp_ref, hn_ref):
        hf = h1_ref[...] + jnp.dot(act_ref[...], wd_ref[...], preferred_element_type=F32)
        h2_ref[...] = hf
        hn = (hf * _rstd(hf) * g_ref[...]).astype(BF16)
        hn_ref[...] = hn
        gate = jax.nn.sigmoid(jnp.dot(hn, wpg_ref[...], preferred_element_type=F32))
        pp = lax.dot_general(p_ref[...].astype(BF16), wpp_ref[...], _NT, preferred_element_type=F32)
        h3_ref[...] = hf + gate * pp
        gate_ref[...] = gate.astype(BF16)
        pp_ref[...] = pp.astype(BF16)

    full = jax.ShapeDtypeStruct((s_len, D_MODEL), F32)
    half = jax.ShapeDtypeStruct((s_len, D_MODEL), BF16)
    return pl.pallas_call(
        body, name="ffn_down_ple", grid=(s_len // tm,),
        in_specs=[_rows(tm, D_FF), _resident((D_FF, D_MODEL)), _rows(tm, D_MODEL), _whole((1, D_MODEL)),
                  _resident((D_MODEL, D_MODEL)), _rows(tm, PLE_DIM), _resident((D_MODEL, PLE_DIM))],
        out_specs=[_rows(tm, D_MODEL)] * 5,
        out_shape=[full, full, half, half, half],
        compiler_params=_params("parallel"),
    )(act, wd, h1, g, wpg, p, wpp)


def _loss_ple_bwd(h3, target, gf, gate, pp, h2, g_ple, wpg, hn3, p):
    s_len = h3.shape[0]
    tm = ROW_TILE
    n_steps = s_len // tm

    def body(h_ref, t_ref, g_ref, gate_ref, pp_ref, h2_ref, gp_ref, w_ref, hn_ref, p_ref,
             dh2_ref, loss_ref, dg_ref, dgp_ref, dwg_ref, dwp_ref, acc_g, acc_p):
        step = pl.program_id(0)

        @pl.when(step == 0)
        def _():
            loss_ref[...] = jnp.zeros_like(loss_ref)
            dg_ref[...] = jnp.zeros_like(dg_ref)
            dgp_ref[...] = jnp.zeros_like(dgp_ref)
            acc_g[...] = jnp.zeros_like(acc_g)
            acc_p[...] = jnp.zeros_like(acc_p)

        hf = h_ref[...]
        gfv = g_ref[...]
        r = _rstd(hf)
        err = hf * r * gfv - t_ref[...]
        loss_ref[...] += 0.5 * jnp.sum(jnp.sum(err * err, axis=-1, keepdims=True), axis=0, keepdims=True) / D_MODEL
        dh, dg_rows = _norm_bwd(err / D_MODEL, hf, gfv, r)
        dg_ref[...] += jnp.sum(dg_rows, axis=0, keepdims=True)
        gate = gate_ref[...].astype(F32)
        dz = (dh * pp_ref[...].astype(F32) * gate * (1.0 - gate)).astype(BF16)
        dpp = (dh * gate).astype(BF16)
        dn = lax.dot_general(dz, w_ref[...], _NT, preferred_element_type=F32)
        dh2, dgp_rows = _norm_bwd(dn, h2_ref[...], gp_ref[...])
        dh2 = dh + dh2
        dh2_ref[...] = dh2
        dgp_ref[...] += jnp.sum(dgp_rows, axis=0, keepdims=True)
        acc_g[...] += lax.dot_general(hn_ref[...], dz, _TN, preferred_element_type=F32)
        acc_p[...] += lax.dot_general(dpp, p_ref[...].astype(BF16), _TN, preferred_element_type=F32)

        @pl.when(step == n_steps - 1)
        def _():
            dwg_ref[...] = acc_g[...].astype(BF16)
            dwp_ref[...] = acc_p[...].astype(BF16)

    gain = jax.ShapeDtypeStruct((1, D_MODEL), F32)
    return pl.pallas_call(
        body, name="loss_ple_bwd", grid=(n_steps,),
        in_specs=[_rows(tm, D_MODEL), _rows(tm, D_MODEL), _whole((1, D_MODEL)), _rows(tm, D_MODEL), _rows(tm, D_MODEL),
                  _rows(tm, D_MODEL), _whole((1, D_MODEL)), _resident((D_MODEL, D_MODEL)), _rows(tm, D_MODEL),
                  _rows(tm, PLE_DIM)],
        out_specs=[_rows(tm, D_MODEL), _whole((1, 128)), _whole((1, D_MODEL)), _whole((1, D_MODEL)),
                   _whole((D_MODEL, D_MODEL)), _whole((D_MODEL, PLE_DIM))],
        out_shape=[jax.ShapeDtypeStruct((s_len, D_MODEL), F32), jax.ShapeDtypeStruct((1, 128), F32), gain, gain,
                   jax.ShapeDtypeStruct((D_MODEL, D_MODEL), BF16), jax.ShapeDtypeStruct((D_MODEL, PLE_DIM), BF16)],
        scratch_shapes=[pltpu.VMEM((D_MODEL, D_MODEL), F32), pltpu.VMEM((D_MODEL, PLE_DIM), F32)],
        compiler_params=_params("arbitrary"),
    )(h3, target, gf, gate, pp, h2, g_ple, wpg, hn3, p)


def _mm_norm_bwd(parts, h, g, dres, name, exchange=(), dw_lhs=None):
    s_len = h.shape[0]
    tm = ROW_TILE
    n_parts = len(parts)
    n_steps = s_len // tm
    has_dw = dw_lhs is not None

    def body(*refs):
        a_refs = refs[0:2 * n_parts:2]
        w_refs = refs[1:2 * n_parts:2]
        h_ref, g_ref, r_ref = refs[2 * n_parts:2 * n_parts + 3]
        rest = refs[2 * n_parts + 3:]
        step = pl.program_id(0)
        if has_dw:
            lhs_ref, o_ref, dg_ref, dw_ref, acc_ref = rest
        else:
            o_ref, dg_ref = rest

        @pl.when(step == 0)
        def _():
            dg_ref[...] = jnp.zeros_like(dg_ref)
            if has_dw:
                acc_ref[...] = jnp.zeros_like(acc_ref)

        dn = jnp.dot(a_refs[0][...], w_refs[0][...], preferred_element_type=F32)
        for a_ref, w_ref in zip(a_refs[1:], w_refs[1:]):
            dn = dn + jnp.dot(a_ref[...], w_ref[...], preferred_element_type=F32)
        dh, dg_rows = _norm_bwd(dn, h_ref[...], g_ref[...])
        out = r_ref[...] + dh
        o_ref[...] = out
        dg_ref[...] += jnp.sum(dg_rows, axis=0, keepdims=True)
        if has_dw:
            acc_ref[...] += lax.dot_general(lhs_ref[...], out.astype(BF16), _TN, preferred_element_type=F32)

            @pl.when(step == n_steps - 1)
            def _():
                dw_ref[...] = acc_ref[...].astype(BF16)

    in_specs, args = [], []
    for a, w in parts:
        in_specs += [_rows(tm, a.shape[1]), _resident(w.shape)]
        args += [a, w]
    in_specs += [_rows(tm, D_MODEL), _whole((1, D_MODEL)), _rows(tm, D_MODEL)]
    args += [h, g, dres]
    out_specs = [_rows(tm, D_MODEL), _whole((1, D_MODEL))]
    out_shape = [jax.ShapeDtypeStruct((s_len, D_MODEL), F32), jax.ShapeDtypeStruct((1, D_MODEL), F32)]
    scratch = []
    if has_dw:
        m = dw_lhs.shape[1]
        in_specs.append(_rows(tm, m))
        args.append(dw_lhs)
        out_specs.append(_whole((m, D_MODEL)))
        out_shape.append(jax.ShapeDtypeStruct((m, D_MODEL), BF16))
        scratch.append(pltpu.VMEM((m, D_MODEL), F32))
    return _call(
        body, name=name, grid=(n_steps,), in_specs=in_specs, out_specs=out_specs, out_shape=out_shape,
        scratch_shapes=scratch, semantics=("arbitrary",), args=tuple(args), exchange=exchange)


def _ffn_down_bwd(dh, wdt, a, b, exchange=()):
    s_len = dh.shape[0]
    tm = ROW_TILE

    def body(dh_ref, w_ref, a_ref, b_ref, da_ref, db_ref):
        dhb = dh_ref[...].astype(BF16)
        for j in range(D_FF // 256):
            cols = slice(j * 256, (j + 1) * 256)
            dact = lax.dot_general(dhb, w_ref[cols, :], _NT, preferred_element_type=F32)
            av = a_ref[:, cols].astype(F32)
            bv = b_ref[:, cols].astype(F32)
            sig = jax.nn.sigmoid(av)
            t = dact * sig
            silu = av * sig
            da_ref[:, cols] = (t * bv * (1.0 + av - silu)).astype(BF16)
            db_ref[:, cols] = (dact * silu).astype(BF16)

    wide = jax.ShapeDtypeStruct((s_len, D_FF), BF16)
    return _call(
        body, name="ffn_down_bwd", grid=(s_len // tm,),
        in_specs=[_rows(tm, D_MODEL), _resident((D_FF, D_MODEL)), _rows(tm, D_FF), _rows(tm, D_FF)],
        out_specs=[_rows(tm, D_FF), _rows(tm, D_FF)],
        out_shape=[wide, wide],
        semantics=("parallel",), args=(dh, wdt, a, b), exchange=exchange)


def _outproj_bwd(dh1, woutt, yb, gb, head_sum, exchange=()):
    s_len = dh1.shape[0]
    tm = ROW_TILE
    n_br = len(DILATIONS)

    def body(dh_ref, w_ref, yb_ref, gb_ref, e_ref, dya_ref, dgb_ref, *rest):
        do_refs, dd_refs = rest[:n_br], rest[n_br:2 * n_br]
        do_nat, dd_nat = rest[2 * n_br:]

        @pl.when(pl.program_id(0) == 0)
        def _():
            dgb_ref[...] = jnp.zeros_like(dgb_ref)

        dhb = dh_ref[...].astype(BF16)
        dya_ref[...] = lax.dot_general(dhb, w_ref[:WIDTH_A, :], _NT, preferred_element_type=F32)
        dyn = lax.dot_general(dhb, w_ref[WIDTH_A:, :], _NT, preferred_element_type=F32)
        ybv = yb_ref[...]
        dyb, dg_rows = _norm_bwd(dyn, ybv, gb_ref[...])
        dgb_ref[...] += jnp.sum(dg_rows, axis=0, keepdims=True)
        prod = dyb * ybv
        hi = prod.astype(BF16)
        lo = (prod - hi.astype(F32)).astype(BF16)
        dd_nat[0] = (jnp.dot(hi, e_ref[...], preferred_element_type=F32)
                     + jnp.dot(lo, e_ref[...], preferred_element_type=F32))
        for i, d in enumerate(DILATIONS):
            _to_sub(dd_nat, 0, dd_refs[i], 0, d, tm)
        for cb in range(N_PAIRS):
            piece = dyb[:, cb * 128:(cb + 1) * 128]
            do_nat[cb] = piece
            do_refs[0][0, cb] = piece.astype(BF16)
            for i, d in enumerate(DILATIONS[1:], start=1):
                _to_sub(do_nat, cb, do_refs[i], cb, d, tm)

    return _call(
        body, name="outproj_bwd", grid=(s_len // tm,),
        in_specs=[_rows(tm, D_MODEL), _resident((D_MODEL, D_MODEL)), _rows(tm, WIDTH_B), _whole((1, WIDTH_B)), _whole((WIDTH_B, 128))],
        out_specs=[_rows(tm, WIDTH_A), _whole((1, WIDTH_B))] + [_sub_spec(d, N_PAIRS, tm) for d in DILATIONS]
        + [_sub_spec(d, 1, tm) for d in DILATIONS],
        out_shape=[jax.ShapeDtypeStruct((s_len, WIDTH_A), F32), jax.ShapeDtypeStruct((1, WIDTH_B), F32)]
        + [_sub_shape(s_len, d, N_PAIRS, BF16) for d in DILATIONS] + [_sub_shape(s_len, d, 1, F32) for d in DILATIONS],
        scratch_shapes=[pltpu.VMEM((N_PAIRS, tm, 128), F32), pltpu.VMEM((1, tm, 128), F32)],
        semantics=("arbitrary",), args=(dh1, woutt, yb, gb, head_sum), exchange=exchange)


def _attn_bwd(qkv, do, lse, dd, d, exchange=()):
    sd = qkv.shape[2]
    tile, nb, n_tiles, res = _attn_geometry(sd)
    last_block = sd // CHUNK - 1

    def nxt(n):
        return jnp.minimum((n + 1) * nb, last_block)

    def block(ref, next_ref, j):
        return ref[j * CHUNK:(j + 1) * CHUNK, :] if j < nb else next_ref[...]

    def body(q_ref, qn_ref, k_ref, v_ref, do_ref, don_ref, l_ref, ln_ref, dd_ref, ddn_ref,
             dq_ref, dk_ref, dv_ref, carry_ref):
        for rr in range(res):
            l_t = [block(l_ref.at[rr], ln_ref.at[rr], j).T for j in range(nb + 1)]
            dd_t = [block(dd_ref.at[rr], ddn_ref.at[rr], j).T for j in range(nb + 1)]
            for hp in range(PAIRS_PER_STEP):
                l_rows = [jnp.concatenate([t[2 * hp:2 * hp + 1, :], t[2 * hp + 1:2 * hp + 2, :]], axis=1) for t in l_t]
                dd_rows = [jnp.concatenate([t[2 * hp:2 * hp + 1, :], t[2 * hp + 1:2 * hp + 2, :]], axis=1) for t in dd_t]
                one_pair(q_ref.at[rr, hp], qn_ref.at[rr, hp], k_ref.at[rr, hp], v_ref.at[rr, hp], do_ref.at[rr, hp],
                         don_ref.at[rr, hp], l_rows, dd_rows, dq_ref.at[rr, hp], dk_ref.at[rr, hp], dv_ref.at[rr, hp],
                         carry_ref.at[rr, hp])

    def one_pair(q_ref, qn_ref, k_ref, v_ref, do_ref, don_ref, l_rows, dd_rows, dq_ref, dk_ref, dv_ref, carry_ref):
        n = pl.program_id(2)

        @pl.when(n == 0)
        def _():
            carry_ref[...] = jnp.zeros_like(carry_ref)

        head_a = lax.broadcasted_iota(jnp.int32, (CHUNK, 128), 1) < HEAD_DIM
        col = lax.broadcasted_iota(jnp.int32, (CHUNK, 4 * CHUNK), 1)
        qi = col % CHUNK
        ki = lax.broadcasted_iota(jnp.int32, (CHUNK, 4 * CHUNK), 0)
        is_after = col >= 2 * CHUNK
        mask = (is_after & (ki >= qi)) | (jnp.logical_not(is_after) & (qi >= ki))
        mask_last = mask & jnp.logical_or(jnp.logical_not(is_after), n < n_tiles - 1)
        dq_acc = [carry_ref[...]] + [jnp.zeros((CHUNK, 128), F32) for _ in range(nb)]

        q_st = [jnp.concatenate(_both_heads(block(q_ref, qn_ref, j), head_a), axis=0) for j in range(nb + 1)]
        do_st = [jnp.concatenate(_both_heads(block(do_ref, don_ref, j), head_a), axis=0) for j in range(nb + 1)]

        for j in range(nb):
            rows = slice(j * CHUNK, (j + 1) * CHUNK)
            kj = k_ref[rows, :]
            vj = v_ref[rows, :]
            msk = mask if j + 1 < nb else mask_last
            qs = jnp.concatenate([q_st[j], q_st[j + 1]], axis=0)
            dos = jnp.concatenate([do_st[j], do_st[j + 1]], axis=0)
            ls = jnp.concatenate([l_rows[j], l_rows[j + 1]], axis=1)
            dds = jnp.concatenate([dd_rows[j], dd_rows[j + 1]], axis=1)
            st = lax.dot_general(kj, qs, _NT, preferred_element_type=F32)
            pt = jnp.exp(jnp.where(msk, st - ls, NEG))
            dpt = lax.dot_general(vj, dos, _NT, preferred_element_type=F32)
            dst = (pt * (dpt - dds)).astype(BF16)
            dv_ref[rows, :] = jnp.dot(pt.astype(BF16), dos, preferred_element_type=F32).astype(BF16)
            dk_ref[rows, :] = jnp.dot(dst, qs, preferred_element_type=F32).astype(BF16)
            dqs = lax.dot_general(dst, kj, _TN, preferred_element_type=F32)
            dq_acc[j] = dq_acc[j] + jnp.where(head_a, dqs[:CHUNK], dqs[CHUNK:2 * CHUNK])
            dq_acc[j + 1] = dq_acc[j + 1] + jnp.where(head_a, dqs[2 * CHUNK:3 * CHUNK], dqs[3 * CHUNK:])
        for j in range(nb):
            dq_ref[j * CHUNK:(j + 1) * CHUNK, :] = dq_acc[j].astype(BF16)
        carry_ref[...] = dq_acc[nb]

    same = lambda n: n
    grad = jax.ShapeDtypeStruct((d, N_PAIRS, sd, 128), BF16)
    return _call(
        body, name=f"attn_bwd_d{d}", grid=(d // res, N_PAIRS // PAIRS_PER_STEP, n_tiles),
        in_specs=[_attn_spec(0, tile, same, res), _attn_spec(0, CHUNK, nxt, res), _attn_spec(N_PAIRS, tile, same, res),
                  _attn_spec(2 * N_PAIRS, tile, same, res), _attn_spec(0, tile, same, res), _attn_spec(0, CHUNK, nxt, res),
                  _stats_spec(tile, same, res), _stats_spec(CHUNK, nxt, res), _stats_spec(tile, same, res),
                  _stats_spec(CHUNK, nxt, res)],
        out_specs=[_attn_spec(0, tile, same, res)] * 3,
        out_shape=[grad, grad, grad],
        scratch_shapes=[pltpu.VMEM((res, PAIRS_PER_STEP, CHUNK, 128), F32)],
        semantics=("parallel", "parallel", "arbitrary"), args=(qkv, qkv, qkv, qkv, do, do, lse, lse, dd, dd), exchange=exchange)


def _sgu_bwd(ua, sw, b2, gs, ga, dya_n):
    s_len = ua.shape[0]
    tm = ROW_TILE

    def body(ua_ref, sw_ref, b2_ref, gs_ref, ga_ref, dy_ref, dua_ref, dsw_ref, db2_ref, dgs_ref, dga_ref):
        @pl.when(pl.program_id(0) == 0)
        def _():
            dsw_ref[...] = jnp.zeros_like(dsw_ref)
            db2_ref[...] = jnp.zeros_like(db2_ref)
            dgs_ref[...] = jnp.zeros_like(dgs_ref)
            dga_ref[...] = jnp.zeros_like(dga_ref)

        u, va, ug, xhat, rstd, vn = _sgu_core(ua_ref, gs_ref)
        wm, keep = _sgu_mix_weights(sw_ref)
        head = lax.broadcasted_iota(jnp.int32, (CHUNK, WIDTH_A), 1) // HEAD_DIM
        gav = ga_ref[...]
        gsv = gs_ref[...]
        dga = jnp.zeros((1, WIDTH_A), F32)
        dgs = jnp.zeros((1, WIDTH_A), F32)
        db2 = jnp.zeros((CHUNK, WIDTH_A), F32)
        dsw = [jnp.zeros((CHUNK, CHUNK), F32) for _ in range(4)]
        for c in range(tm // CHUNK):
            rows = slice(c * CHUNK, (c + 1) * CHUNK)
            vnc = vn[rows]
            vnb = vnc.astype(BF16)
            mixed = b2_ref[...]
            for h in range(4):
                mixed = mixed + jnp.dot(wm[h], jnp.where(head == h, vnc, 0.0).astype(BF16), preferred_element_type=F32)
            ugc = ug[rows]
            dya, dga_rows = _norm_bwd(dy_ref[rows, :], ugc * mixed, gav)
            dga = dga + jnp.sum(dga_rows, axis=0, keepdims=True)
            dmixed = dya * ugc
            db2 = db2 + dmixed
            dvn = jnp.zeros((CHUNK, WIDTH_A), F32)
            for h in range(4):
                dmh = jnp.where(head == h, dmixed, 0.0).astype(BF16)
                dsw[h] = dsw[h] + lax.dot_general(dmh, vnb, _NT, preferred_element_type=F32)
                dvn = dvn + lax.dot_general(wm[h], dmh, _TN, preferred_element_type=F32)
            xh = xhat[rows]
            dgs = dgs + jnp.sum(dvn * xh, axis=0, keepdims=True)
            dxh = dvn * gsv
            dvg = rstd[rows] * (dxh - jnp.mean(dxh, axis=-1, keepdims=True) - xh * jnp.mean(dxh * xh, axis=-1, keepdims=True))
            dua_ref[rows, :WIDTH_A] = (dya * mixed * _gelu_grad(u[rows])).astype(BF16)
            dua_ref[rows, WIDTH_A:] = (dvg * _gelu_grad(va[rows])).astype(BF16)
        for h in range(4):
            dsw_ref[h] += jnp.where(keep, dsw[h], 0.0)
        db2_ref[...] += db2
        dgs_ref[...] += dgs
        dga_ref[...] += dga

    return pl.pallas_call(
        body, name="sgu_bwd", grid=(s_len // tm,),
        in_specs=[_rows(tm, 2 * WIDTH_A), _whole((4, CHUNK, CHUNK)), _whole((CHUNK, WIDTH_A)), _whole((1, WIDTH_A)),
                  _whole((1, WIDTH_A)), _rows(tm, WIDTH_A)],
        out_specs=[_rows(tm, 2 * WIDTH_A), _whole((4, CHUNK, CHUNK)), _whole((CHUNK, WIDTH_A)), _whole((1, WIDTH_A)), _whole((1, WIDTH_A))],
        out_shape=[jax.ShapeDtypeStruct((s_len, 2 * WIDTH_A), BF16), jax.ShapeDtypeStruct((4, CHUNK, CHUNK), F32),
                   jax.ShapeDtypeStruct((CHUNK, WIDTH_A), F32), jax.ShapeDtypeStruct((1, WIDTH_A), F32),
                   jax.ShapeDtypeStruct((1, WIDTH_A), F32)],
        compiler_params=_params("arbitrary"),
    )(ua, sw, b2, gs, ga, dya_n)


def _dproj(dua, dqs, dks, dvs, cos, sin, hn1, exchange=()):
    s_len = dua.shape[0]
    tm = ROW_TILE
    n_br = len(DILATIONS)
    n_steps = s_len // tm

    def body(dua_ref, *rest):
        groups = [rest[g * n_br:(g + 1) * n_br] for g in range(3)]
        cos_ref, sin_ref, hn_ref, out_ref, dw_ref, acc, dw_acc = rest[3 * n_br:]
        step = pl.program_id(0)

        @pl.when(step == 0)
        def _():
            dw_acc[...] = jnp.zeros_like(dw_acc)

        out_ref[:, :2 * WIDTH_A] = dua_ref[...]
        c = cos_ref[...]
        s = sin_ref[...]
        first_half = (lax.broadcasted_iota(jnp.int32, (tm, 128), 1) % HEAD_DIM) < HEAD_DIM // 2
        for g, refs in enumerate(groups):
            for cb in range(N_PAIRS):
                t = refs[0][0, cb].astype(F32)
                for i, d in enumerate(DILATIONS[1:]):
                    _from_sub(refs[i + 1], cb, acc, i, d, tm)
                    t = t + acc[i]
                if g < 2:
                    t = (t * c - _swap_halves(t, first_half) * s) * (0.125 if g == 0 else 1.0)
                col = 2 * WIDTH_A + g * WIDTH_B + cb * 128
                out_ref[:, col:col + 128] = t.astype(BF16)
        hn = hn_ref[...]
        for j in range(IN_COLS // 256):
            cols = slice(j * 256, (j + 1) * 256)
            dw_acc[cols, :] += lax.dot_general(out_ref[:, cols], hn, _TN, preferred_element_type=F32)

        @pl.when(step == n_steps - 1)
        def _():
            dw_ref[...] = dw_acc[...].astype(BF16)

    subs = [_sub_spec(d, N_PAIRS, tm) for d in DILATIONS]
    (dproj, dw), received = _call(
        body, name="dproj_dw_in", grid=(n_steps,),
        in_specs=[_rows(tm, 2 * WIDTH_A)] + subs * 3 + [_rows(tm, 128), _rows(tm, 128), _rows(tm, D_MODEL)],
        out_specs=[_rows(tm, IN_COLS), _whole((IN_COLS, D_MODEL))],
        out_shape=[jax.ShapeDtypeStruct((s_len, IN_COLS), BF16), jax.ShapeDtypeStruct((IN_COLS, D_MODEL), BF16)],
        scratch_shapes=[pltpu.VMEM((n_br - 1, tm, 128), F32), pltpu.VMEM((IN_COLS, D_MODEL), F32)],
        semantics=("arbitrary",), args=(dua, *dqs, *dks, *dvs, cos, sin, hn1), exchange=exchange)
    return dproj, dw, received


def _mm_tn(a, b, name, exchange=()):
    s_len, m = a.shape
    n = b.shape[1]
    tk = 2 * ROW_TILE
    tm = m if m <= 512 else (1408 if m == D_FF else 512)
    n_k = s_len // tk

    def body(a_ref, b_ref, o_ref, acc_ref):
        k = pl.program_id(1)

        @pl.when(k == 0)
        def _():
            acc_ref[...] = jnp.zeros_like(acc_ref)

        acc_ref[...] += lax.dot_general(a_ref[...].astype(BF16), b_ref[...].astype(BF16), _TN, preferred_element_type=F32)

        @pl.when(k == n_k - 1)
        def _():
            o_ref[...] = acc_ref[...].astype(BF16)

    (grad,), received = _call(
        body, name=name, grid=(m // tm, n_k),
        in_specs=[pl.BlockSpec((tk, tm), lambda i, k: (k, i)), pl.BlockSpec((tk, n), lambda i, k: (k, 0))],
        out_specs=[pl.BlockSpec((tm, n), lambda i, k: (i, 0))],
        out_shape=[jax.ShapeDtypeStruct((m, n), BF16)],
        scratch_shapes=[pltpu.VMEM((tm, n), F32)],
        semantics=("parallel", "arbitrary"), args=(a, b), exchange=exchange)
    return grad, received


def _position():
    x, y, c = lax.axis_index("x"), lax.axis_index("y"), lax.axis_index("c")
    return x, y, c, 4 * x + 2 * y + c


def _peer(x, y, c, rel):
    return (x ^ ((rel >> 2) & 1), y ^ ((rel >> 1) & 1), c ^ (rel & 1))


def _exchange_out_shape(kind, arr):
    return jax.ShapeDtypeStruct(((N_DEV,) + arr.shape) if kind == "gather" else arr.shape, arr.dtype)


def _exchange_sems(n_items):
    return [pltpu.SemaphoreType.DMA((n_items, N_DEV)), pltpu.SemaphoreType.DMA((n_items, N_DEV)), pltpu.SemaphoreType.DMA((n_items,))]


def _exchange_copies(kinds, srcs, dsts, sems, arrivals):
    send_sems, recv_sems, local_sems = sems
    x, y, c, me = _position()
    local, sends, recvs = [], [], []
    for k, (kind, src, dst) in enumerate(zip(kinds, srcs, dsts)):
        own = src if kind == "gather" else src.at[me]
        local.append(pltpu.make_async_copy(own, dst.at[me], local_sems.at[k]))
        for rel in range(1, N_DEV):
            going = src if kind == "gather" else src.at[me ^ rel]
            common = dict(send_sem=send_sems.at[k, rel], recv_sem=recv_sems.at[k, rel],
                          device_id=_peer(x, y, c, rel), device_id_type=MESH)
            sends.append(pltpu.make_async_remote_copy(src_ref=going, dst_ref=dst.at[me], **common))
            if arrivals:
                recvs.append(pltpu.make_async_remote_copy(src_ref=own, dst_ref=dst.at[me ^ rel], **common))
    return local, sends, recvs


def _exchange_start(kinds, srcs, dsts, sems):
    local, sends, _ = _exchange_copies(kinds, srcs, dsts, sems, arrivals=False)
    for cp in local + sends:
        cp.start()


def _exchange_finish(kinds, srcs, dsts, sems):
    local, sends, recvs = _exchange_copies(kinds, srcs, dsts, sems, arrivals=True)
    for cp in recvs:
        cp.wait_recv()
    for cp in sends:
        cp.wait_send()
    for cp in local:
        cp.wait()


def _exchange_only(items, name):
    kinds = [k for k, _ in items]
    n = len(items)

    def body(*refs):
        srcs, dsts, sems = refs[:n], refs[n:2 * n], refs[2 * n:]
        _exchange_start(kinds, srcs, dsts, sems)
        _exchange_finish(kinds, srcs, dsts, sems)

    any_spec = pl.BlockSpec(memory_space=pl.ANY)
    return pl.pallas_call(
        body, name=name, in_specs=[any_spec] * n, out_specs=[any_spec] * n,
        out_shape=[_exchange_out_shape(k, a) for k, a in items],
        scratch_shapes=_exchange_sems(n),
        compiler_params=pltpu.CompilerParams(has_side_effects=True),
    )(*[a for _, a in items])


def _gather_two_level_with_rope_tables(shard, inv_freq, s_len, name):
    rows = ROW_TILE

    def body(inv_ref, src, cos_ref, sin_ref, dst, send_sems, recv_sems, local_sem):
        x, y, c, me = _position()
        sibling = (x, y, 1 - c)
        chips = [(1 - x, y), (x, 1 - y), (1 - x, 1 - y)]

        def block(px, py, pc):
            return dst.at[4 * px + 2 * py + pc]

        def copy(k, blk, to, src_ref=None):
            return pltpu.make_async_remote_copy(
                src_ref=block(*blk) if src_ref is None else src_ref, dst_ref=block(*blk),
                send_sem=send_sems.at[k], recv_sem=recv_sems.at[k], device_id=to, device_id_type=MESH)

        mine = pltpu.make_async_copy(src, dst.at[me], local_sem)
        mine.start()
        first = [copy(0, (x, y, c), sibling, src)] + [copy(1 + j, (x, y, c), (*chip, c), src) for j, chip in enumerate(chips)]
        for cp in first:
            cp.start()

        inv = inv_ref[...]
        lane = lax.broadcasted_iota(jnp.int32, (rows, 128), 1)
        sign = jnp.where((lane // (HEAD_DIM // 2)) % 2 == 0, -1.0, 1.0)
        row = lax.broadcasted_iota(jnp.int32, (rows, 128), 0)

        @pl.loop(0, s_len // rows)
        def _(i):
            at = pl.multiple_of(i * rows, rows)
            ang = (row + at).astype(F32) * inv
            cos_ref[pl.ds(at, rows), :] = jnp.cos(ang)
            sin_ref[pl.ds(at, rows), :] = jnp.sin(ang) * sign

        passed = [copy(4 + j, (*chip, c), sibling) for j, chip in enumerate(chips)]
        for j, chip in enumerate(chips):
            copy(1 + j, (*chip, c), (x, y, c)).wait_recv()
            passed[j].start()
        copy(0, (x, y, 1 - c), (x, y, c)).wait_recv()
        for j, chip in enumerate(chips):
            copy(4 + j, (*chip, 1 - c), (x, y, c)).wait_recv()
        for cp in first + passed:
            cp.wait_send()
        mine.wait()

    any_spec = pl.BlockSpec(memory_space=pl.ANY)
    vmem = pl.BlockSpec(memory_space=pltpu.VMEM)
    table = jax.ShapeDtypeStruct((s_len, 128), F32)
    return pl.pallas_call(
        body, name=name, in_specs=[vmem, any_spec], out_specs=[vmem, vmem, any_spec],
        out_shape=[table, table, _exchange_out_shape("gather", shard)],
        scratch_shapes=[pltpu.SemaphoreType.DMA((N_DEV - 1,)), pltpu.SemaphoreType.DMA((N_DEV - 1,)), pltpu.SemaphoreType.DMA],
        compiler_params=pltpu.CompilerParams(has_side_effects=True, vmem_limit_bytes=V7X_VMEM_LIMIT_BYTES),
    )(inv_freq, shard)


def _call(body, *, name, grid, in_specs, out_specs, out_shape, args, scratch_shapes=(), semantics, exchange=()):
    if not exchange:
        outs = pl.pallas_call(body, name=name, grid=grid, in_specs=in_specs, out_specs=out_specs, out_shape=out_shape,
                              scratch_shapes=list(scratch_shapes), compiler_params=_params(*semantics))(*args)
        return outs, []
    kinds = [k for k, _ in exchange]
    n_in, n_out, n_x, n_scr = len(in_specs), len(out_specs), len(exchange), len(scratch_shapes)

    def wrapped(*refs):
        ins, refs = refs[:n_in], refs[n_in:]
        srcs, refs = refs[:n_x], refs[n_x:]
        outs, refs = refs[:n_out], refs[n_out:]
        dsts, refs = refs[:n_x], refs[n_x:]
        scratch, sems = refs[:n_scr], refs[n_scr:]
        ids = [pl.program_id(a) for a in range(len(grid))]
        first = functools.reduce(jnp.logical_and, [i == 0 for i in ids])
        last = functools.reduce(jnp.logical_and, [i == g - 1 for i, g in zip(ids, grid)])

        @pl.when(first)
        def _():
            _exchange_start(kinds, srcs, dsts, sems)

        body(*ins, *outs, *scratch)

        @pl.when(last)
        def _():
            _exchange_finish(kinds, srcs, dsts, sems)

    any_spec = pl.BlockSpec(memory_space=pl.ANY)
    outs = pl.pallas_call(
        wrapped, name=name, grid=grid,
        in_specs=list(in_specs) + [any_spec] * n_x, out_specs=list(out_specs) + [any_spec] * n_x,
        out_shape=list(out_shape) + [_exchange_out_shape(k, a) for k, a in exchange],
        scratch_shapes=list(scratch_shapes) + _exchange_sems(n_x),
        compiler_params=pltpu.CompilerParams(dimension_semantics=("arbitrary",) * len(grid),
                                             vmem_limit_bytes=V7X_VMEM_LIMIT_BYTES, has_side_effects=True),
    )(*args, *[a for _, a in exchange])
    return outs[:n_out], outs[n_out:]


def _adamw_math(w, g, m, v):
    m = ADAM_B1 * m + (1.0 - ADAM_B1) * g
    v = ADAM_B2 * v + (1.0 - ADAM_B2) * (g * g)
    m_hat = m / (1.0 - ADAM_B1 ** ADAM_STEP)
    v_hat = v / (1.0 - ADAM_B2 ** ADAM_STEP)
    return -ADAM_LR * (m_hat / (jnp.sqrt(v_hat) + ADAM_EPS) + ADAM_WD * w), m, v


def _adamw(parts, w, m, v, name):
    rows, cols = w.shape
    tm = 256 if rows % 256 == 0 and rows > 256 else rows

    def body(p_ref, w_ref, m_ref, v_ref, g_ref, d_ref, nm_ref, nv_ref):
        g = p_ref[0].astype(F32)
        for j in range(1, N_DEV):
            g = g + p_ref[j].astype(F32)
        delta, nm, nv = _adamw_math(w_ref[...], g, m_ref[...], v_ref[...])
        g_ref[...] = g
        d_ref[...] = delta
        nm_ref[...] = nm
        nv_ref[...] = nv

    shard = jax.ShapeDtypeStruct((rows, cols), F32)
    return pl.pallas_call(
        body, name=name, grid=(rows // tm,),
        in_specs=[pl.BlockSpec((N_DEV, tm, cols), lambda i: (0, i, 0))] + [_rows(tm, cols)] * 3,
        out_specs=[_rows(tm, cols)] * 4,
        out_shape=[shard] * 4,
        compiler_params=_params("parallel"),
    )(parts, w, m, v)


_SMALL = ("mix_norm_g", "sgu_w", "sgu_b", "sgu_norm_g", "out_norm_a", "out_norm_b", "ffn_norm_g", "ple_norm_g", "final_norm_g")
_BIG = ("w_in", "w_out", "w_gate", "w_up", "w_down", "w_ple_gate", "w_ple_proj")
_COLUMN_SHARDED = ("w_in", "w_gate", "w_up", "w_ple_proj")
_ORDER = ("mix_norm_g", "w_in", "sgu_w", "sgu_b", "sgu_norm_g", "out_norm_a", "out_norm_b", "w_out", "ffn_norm_g",
          "w_gate", "w_up", "w_down", "ple_norm_g", "w_ple_gate", "w_ple_proj", "final_norm_g")


def _pack_small(values, names=_SMALL):
    flat = jnp.concatenate([values[n].reshape(-1).astype(F32) for n in names])
    pad = (-flat.shape[0]) % (8 * 128)
    return jnp.pad(flat, (0, pad)).reshape(-1, 128)


def _unpack_small(packed, like):
    flat = packed.reshape(-1)
    out, at = {}, 0
    for n in _SMALL:
        size = like[n].size
        out[n] = flat[at:at + size].reshape(like[n].shape)
        at += size
    return out


def _own_orientation(name, value):
    return value[0].T if name in _COLUMN_SHARDED else value[0]


def _reference_orientation(name, value):
    return (value.T if name in _COLUMN_SHARDED else value)[None]


def _full_from_gathered(gathered):
    return gathered.reshape(N_DEV * gathered.shape[1], gathered.shape[2])


def _sliced_for_devices(grad):
    return grad.reshape(N_DEV, grad.shape[0] // N_DEV, grad.shape[1])


def _rope_inv_freq():
    half = HEAD_DIM // 2
    inv = ROPE_THETA ** (-jnp.arange(half, dtype=F32) / half)
    return jnp.tile(inv, 128 // half)[None, :]


def _forward_backward(x, p, target, small, shards):
    def gather(*names):
        return [("gather", shards[n]) for n in names]

    def scatter(**grads):
        return [("scatter", _sliced_for_devices(g)) for g in grads.values()]

    full, parts = {}, {}
    s_len = x.shape[0]
    cos, sin, got = _gather_two_level_with_rope_tables(shards["w_in"], _rope_inv_freq(), s_len, "gather_w_in")
    full["w_in"] = _full_from_gathered(got)

    g_mix, g_ffn, g_ple = small["mix_norm_g"], small["ffn_norm_g"], small["ple_norm_g"]
    g_fin = small["final_norm_g"].reshape(1, D_MODEL)
    sw, gs, ga, gb = small["sgu_w"], small["sgu_norm_g"], small["out_norm_a"], small["out_norm_b"]
    b2 = jnp.repeat(small["sgu_b"].T, HEAD_DIM, axis=1)
    head_sum = (jnp.arange(WIDTH_B)[:, None] // HEAD_DIM == jnp.arange(128)[None, :]).astype(BF16)
    n_br = len(DILATIONS)

    def arrived(names, got):
        for n, g in zip(names, got):
            full[n] = _full_from_gathered(g)

    (ua, hn1, *qkv), got = _inproj(x, g_mix, full["w_in"], cos, sin, exchange=gather("w_gate"))
    arrived(("w_gate",), got)
    ya_n = _sgu_fwd(ua, sw, b2, gs, ga)
    half = shards["w_up"].shape[0] // 2
    riders = [[("gather", shards["w_up"][:half])], [("gather", shards["w_up"][half:])], gather("w_out")]
    branch, got = [], []
    for i, d in enumerate(DILATIONS):
        o_l, g = _attn_fwd(qkv[i], d, exchange=riders[i])
        branch.append(o_l)
        got += g
    arrived(("w_up", "w_out"), [jnp.concatenate(got[:2], axis=1), got[2]])
    (y, yb, *lse), _ = _combine([o for o, _ in branch], [l for _, l in branch], ya_n, gb, head_sum.T)
    last_wave = ("w_down", "w_ple_gate", "w_ple_proj")
    (h1, a, b, act, hn2), got = _ffn_up(y, full["w_out"], x, g_ffn, full["w_gate"], full["w_up"], exchange=gather(*last_wave))
    arrived(last_wave, got)
    h2, h3, gate, pp, hn3 = _ffn_down_ple(act, full["w_down"], h1, g_ple, full["w_ple_gate"], p, full["w_ple_proj"])

    dh2, loss, d_fin, d_ple, g_ple_gate, g_ple_proj = _loss_ple_bwd(
        h3, target, g_fin, gate, pp, h2, g_ple, full["w_ple_gate"], hn3, p)
    g_down, _ = _mm_tn(act, dh2, "dw_down")
    (da, db), (parts["w_down"],) = _ffn_down_bwd(dh2, full["w_down"], a, b, exchange=scatter(w_down=g_down))
    g_gate, _ = _mm_tn(da, hn2, "dw_gate")
    g_up, _ = _mm_tn(db, hn2, "dw_up")
    (dh1, d_ffn, g_out), (parts["w_gate"], parts["w_ple_gate"], parts["w_ple_proj"]) = _mm_norm_bwd(
        [(da, full["w_gate"]), (db, full["w_up"])], h1, g_ffn, dh2, "ffn_up_bwd",
        exchange=scatter(w_gate=g_gate, w_ple_gate=g_ple_gate, w_ple_proj=g_ple_proj), dw_lhs=y)
    (dya_n, d_gb, *do_dd), _ = _outproj_bwd(dh1, full["w_out"], yb, gb, head_sum)
    grads_b = []
    for i, d in enumerate(DILATIONS):
        g3, got = _attn_bwd(qkv[i], do_dd[i], lse[i], do_dd[n_br + i], d,
                            exchange=scatter(w_up=g_up, w_out=g_out) if i == 0 else ())
        grads_b.append(g3)
        if i == 0:
            parts["w_up"], parts["w_out"] = got
    dua, d_sw, d_b2, d_gs, d_ga = _sgu_bwd(ua, sw, b2, gs, ga, dya_n)
    early = {
        "sgu_w": d_sw, "sgu_b": d_b2.reshape(CHUNK, 4, HEAD_DIM).sum(axis=-1).T, "sgu_norm_g": d_gs, "out_norm_a": d_ga,
        "out_norm_b": d_gb, "ffn_norm_g": d_ffn, "ple_norm_g": d_ple, "final_norm_g": d_fin,
    }
    dproj, g_in, (early_parts,) = _dproj(
        dua, [g[0] for g in grads_b], [g[1] for g in grads_b], [g[2] for g in grads_b], cos, sin, hn1,
        exchange=[("gather", _pack_small(early, _SMALL[1:]))])
    (dx, d_mix), (parts["w_in"],) = _mm_norm_bwd(
        [(dproj, full["w_in"])], x, g_mix, dh1, "inproj_bwd", exchange=scatter(w_in=g_in))
    late = jnp.concatenate([_pack_small({"mix_norm_g": d_mix}, _SMALL[:1]), jnp.broadcast_to(loss, (8, 128))])
    (late_parts,) = _exchange_only([("gather", late)], "gather_mix_norm_grad_and_loss")
    total_loss = jnp.sum(late_parts[:, 8, 0])
    return total_loss, dx, parts, jnp.concatenate([late_parts[:, :8], early_parts], axis=1)


def kernel(x, p, mix_norm_g, w_in, sgu_w, sgu_b, sgu_norm_g, out_norm_a, out_norm_b, w_out, ffn_norm_g, w_gate, w_up, w_down, ple_norm_g, w_ple_gate, w_ple_proj, final_norm_g, loss_target, m_mix_norm_g, m_w_in, m_sgu_w, m_sgu_b, m_sgu_norm_g, m_out_norm_a, m_out_norm_b, m_w_out, m_ffn_norm_g, m_w_gate, m_w_up, m_w_down, m_ple_norm_g, m_w_ple_gate, m_w_ple_proj, m_final_norm_g, v_mix_norm_g, v_w_in, v_sgu_w, v_sgu_b, v_sgu_norm_g, v_out_norm_a, v_out_norm_b, v_w_out, v_ffn_norm_g, v_w_gate, v_w_up, v_w_down, v_ple_norm_g, v_w_ple_gate, v_w_ple_proj, v_final_norm_g):
    given = dict(locals())
    weights = {n: given[n] for n in _ORDER}
    moments_m = {n: given["m_" + n] for n in _ORDER}
    moments_v = {n: given["v_" + n] for n in _ORDER}

    shards = {n: _own_orientation(n, weights[n]).astype(BF16) for n in _BIG}
    small = {n: (weights[n][0] if n in ("sgu_w", "sgu_b") else weights[n]) for n in _SMALL}

    loss, dx, parts, small_parts = _forward_backward(x[0], p[0, 0], loss_target[0], small, shards)

    small_like = {n: weights[n] for n in _SMALL}
    grads, deltas, new_m, new_v = {}, {}, {}, {}
    for n in _BIG:
        outs = _adamw(parts[n], _own_orientation(n, weights[n]), _own_orientation(n, moments_m[n]),
                      _own_orientation(n, moments_v[n]), "adamw_" + n)
        grads[n], deltas[n], new_m[n], new_v[n] = [_reference_orientation(n, o) for o in outs]
    g, d, nm, nv = _adamw(small_parts, _pack_small(small_like), _pack_small({n: moments_m[n] for n in _SMALL}),
                          _pack_small({n: moments_v[n] for n in _SMALL}), "adamw_small")
    for out, packed in ((grads, g), (deltas, d), (new_m, nm), (new_v, nv)):
        out.update(_unpack_small(packed, small_like))

    return (loss, dx[None], *[grads[n] for n in _ORDER], *[deltas[n] for n in _ORDER],
            *[new_m[n] for n in _ORDER], *[new_v[n] for n in _ORDER])
```

```python
import functools

import jax
import jax.numpy as jnp
from jax import lax
from jax.experimental import pallas as pl
from jax.experimental.pallas import tpu as pltpu

F32 = jnp.float32
BF16 = jnp.bfloat16

D_MODEL = 1024
WIDTH_A = 256
WIDTH_B = 768
D_FF = 2816
IN_COLS = 2 * WIDTH_A + 3 * WIDTH_B
PLE_DIM = 256
HEAD_DIM = 64
N_PAIRS = WIDTH_B // 128
CHUNK = 128
N_BACK = 128
DILATIONS = (1, 4, 16)
ROPE_THETA = 10000.0
EPS = 1e-6
N_DEV = 8

ADAM_LR = 0.001
ADAM_B1 = 0.9
ADAM_B2 = 0.999
ADAM_EPS = 1e-08
ADAM_WD = 0.01
ADAM_STEP = 10

V7X_VMEM_LIMIT_BYTES = 56 * 1024 * 1024
ROW_TILE = 512
MESH = pl.DeviceIdType.MESH
NEG = -1e30

_NT = (((1,), (1,)), ((), ()))
_TN = (((0,), (0,)), ((), ()))


def _params(*semantics):
    return pltpu.CompilerParams(dimension_semantics=semantics, vmem_limit_bytes=V7X_VMEM_LIMIT_BYTES)


def _rows(tm, width):
    return pl.BlockSpec((tm, width), lambda i: (i, 0))


def _whole(shape):
    return pl.BlockSpec(shape, lambda *_: (0,) * len(shape))


def _resident(shape):
    return pl.BlockSpec(shape, lambda *_: (0,) * len(shape), pipeline_mode=pl.Buffered(1))


def _gelu(x):
    t = jnp.tanh(0.7978845608028654 * (x + 0.044715 * (x * x * x)))
    return 0.5 * x * (1.0 + t)


def _gelu_grad(x):
    t = jnp.tanh(0.7978845608028654 * (x + 0.044715 * (x * x * x)))
    return 0.5 * (1.0 + t) + 0.5 * x * (1.0 - t * t) * (0.7978845608028654 * (1.0 + 3.0 * 0.044715 * (x * x)))


def _rstd(x):
    return lax.rsqrt(jnp.mean(x * x, axis=-1, keepdims=True) + EPS)


def _norm_bwd(dn, h, g, r=None):
    r = _rstd(h) if r is None else r
    n = h * r
    t = dn * g
    return r * (t - n * jnp.mean(t * n, axis=-1, keepdims=True)), dn * n


def _swap_halves(x, first_half):
    return jnp.where(first_half, pltpu.roll(x, 96, 1), pltpu.roll(x, 32, 1))


def _sub_spec(d, n_cb, tm):
    return pl.BlockSpec((d, n_cb, tm // d, 128), lambda i: (0, 0, i, 0))


def _sub_shape(s_len, d, n_cb, dtype):
    return jax.ShapeDtypeStruct((d, n_cb, s_len // d, 128), dtype)


def _to_sub(stage_ref, cb_src, out_ref, cb_dst, d, tm):
    slab = stage_ref.at[cb_src]
    for r in range(d):
        out_ref[r, cb_dst] = slab[pl.ds(r, tm // d, stride=d), :].astype(out_ref.dtype)


def _from_sub(in_ref, cb_src, stage_ref, cb_dst, d, tm):
    slab = stage_ref.at[cb_dst]
    for r in range(d):
        slab[pl.ds(r, tm // d, stride=d), :] = in_ref[r, cb_src].astype(F32)


def _inproj(x, g, w, cos, sin, exchange=()):
    s_len = x.shape[0]
    tm = ROW_TILE
    n_cb = 3 * N_PAIRS

    def body(x_ref, g_ref, w_ref, cos_ref, sin_ref, ua_ref, hn_ref, *rest):
        sub_refs, stage = rest[:-1], rest[-1]
        xf = x_ref[...]
        hn = (xf * _rstd(xf) * g_ref[...]).astype(BF16)
        hn_ref[...] = hn
        c = cos_ref[...]
        s = sin_ref[...]
        first_half = (lax.broadcasted_iota(jnp.int32, (tm, 128), 1) % HEAD_DIM) < HEAD_DIM // 2
        for col in range(0, IN_COLS, 512):
            width = min(512, IN_COLS - col)
            acc = lax.dot_general(hn, w_ref[col:col + width, :], _NT, preferred_element_type=F32)
            if col < 2 * WIDTH_A:
                ua_ref[:, col:col + width] = acc
                continue
            for part in range(width // 128):
                cb = (col - 2 * WIDTH_A) // 128 + part
                t = acc[:, part * 128:(part + 1) * 128]
                if cb < 2 * N_PAIRS:
                    t = (t * c + _swap_halves(t, first_half) * s) * (0.125 if cb < N_PAIRS else 1.0)
                stage[cb] = t
                sub_refs[0][0, cb] = t.astype(BF16)
        for cb in range(n_cb):
            for d, out_ref in zip(DILATIONS[1:], sub_refs[1:]):
                _to_sub(stage, cb, out_ref, cb, d, tm)

    return _call(
        body, name="inproj", grid=(s_len // tm,),
        in_specs=[_rows(tm, D_MODEL), _whole((1, D_MODEL)), _resident((IN_COLS, D_MODEL)), _rows(tm, 128), _rows(tm, 128)],
        out_specs=[_rows(tm, 2 * WIDTH_A), _rows(tm, D_MODEL)] + [_sub_spec(d, n_cb, tm) for d in DILATIONS],
        out_shape=[jax.ShapeDtypeStruct((s_len, 2 * WIDTH_A), F32), jax.ShapeDtypeStruct((s_len, D_MODEL), BF16)]
        + [_sub_shape(s_len, d, n_cb, BF16) for d in DILATIONS],
        scratch_shapes=[pltpu.VMEM((n_cb, tm, 128), F32)],
        semantics=("parallel",), args=(x, g, w, cos, sin), exchange=exchange)


def _sgu_mix_weights(sw_ref):
    keep = lax.broadcasted_iota(jnp.int32, (CHUNK, CHUNK), 0) >= lax.broadcasted_iota(jnp.int32, (CHUNK, CHUNK), 1)
    return [jnp.where(keep, sw_ref[h], 0.0).astype(BF16) for h in range(4)], keep


def _sgu_core(ua_ref, gs_ref):
    u = ua_ref[:, :WIDTH_A]
    va = ua_ref[:, WIDTH_A:]
    vg = _gelu(va)
    xc = vg - jnp.mean(vg, axis=-1, keepdims=True)
    rstd = lax.rsqrt(jnp.mean(xc * xc, axis=-1, keepdims=True) + EPS)
    xhat = xc * rstd
    return u, va, _gelu(u), xhat, rstd, xhat * gs_ref[...]


def _sgu_fwd(ua, sw, b2, gs, ga):
    s_len = ua.shape[0]
    tm = ROW_TILE

    def body(ua_ref, sw_ref, b2_ref, gs_ref, ga_ref, out_ref):
        _, _, ug, _, _, vn = _sgu_core(ua_ref, gs_ref)
        wm, _ = _sgu_mix_weights(sw_ref)
        head = lax.broadcasted_iota(jnp.int32, (CHUNK, WIDTH_A), 1) // HEAD_DIM
        for c in range(tm // CHUNK):
            rows = slice(c * CHUNK, (c + 1) * CHUNK)
            vnc = vn[rows]
            mixed = b2_ref[...]
            for h in range(4):
                mixed = mixed + jnp.dot(wm[h], jnp.where(head == h, vnc, 0.0).astype(BF16), preferred_element_type=F32)
            ya = ug[rows] * mixed
            out_ref[rows, :] = (ya * _rstd(ya) * ga_ref[...]).astype(BF16)

    return pl.pallas_call(
        body, name="sgu_fwd", grid=(s_len // tm,),
        in_specs=[_rows(tm, 2 * WIDTH_A), _whole((4, CHUNK, CHUNK)), _whole((CHUNK, WIDTH_A)), _whole((1, WIDTH_A)), _whole((1, WIDTH_A))],
        out_specs=_rows(tm, WIDTH_A),
        out_shape=jax.ShapeDtypeStruct((s_len, WIDTH_A), BF16),
        compiler_params=_params("parallel"),
    )(ua, sw, b2, gs, ga)


STEP_POSITIONS = 2 * ROW_TILE


def _attn_geometry(sd):
    tile = min(STEP_POSITIONS, sd)
    return tile, tile // CHUNK, sd // tile, STEP_POSITIONS // tile


PAIRS_PER_STEP = 6


def _attn_spec(cb0, rows, row_index, res):
    return pl.BlockSpec((res, PAIRS_PER_STEP, rows, 128), lambda r, g, n: (r, cb0 // PAIRS_PER_STEP + g, row_index(n), 0))


assert PAIRS_PER_STEP == N_PAIRS
assert DILATIONS[0] == 1


def _stats_spec(rows, row_index, res):
    return pl.BlockSpec((res, None, rows, 128), lambda r, g, n: (r, 0, row_index(n), 0))


def _stats_shape(sd, d):
    return jax.ShapeDtypeStruct((d, 1, sd, 128), F32)


def _both_heads(x, head_a):
    zero = jnp.zeros_like(x)
    return [jnp.where(head_a, x, zero), jnp.where(head_a, zero, x)]


def _attn_fwd(qkv, d, exchange=()):
    sd = qkv.shape[2]
    tile, nb, n_tiles, res = _attn_geometry(sd)

    def prev(n):
        return jnp.maximum(n * nb - 1, 0)

    def body(q_ref, k_ref, kp_ref, v_ref, vp_ref, o_ref, l_ref):
        for rr in range(res):
            for hp in range(PAIRS_PER_STEP):
                one_pair(hp, q_ref.at[rr, hp], k_ref.at[rr, hp], kp_ref.at[rr, hp], v_ref.at[rr, hp], vp_ref.at[rr, hp],
                         o_ref.at[rr, hp], l_ref.at[rr])

    def one_pair(hp, q_ref, k_ref, kp_ref, v_ref, vp_ref, o_ref, l_ref):
        n = pl.program_id(2)
        lane = lax.broadcasted_iota(jnp.int32, (CHUNK, 128), 1)
        head_a = lane < HEAD_DIM
        qi = lax.broadcasted_iota(jnp.int32, (2 * CHUNK, 2 * CHUNK), 0) % CHUNK
        kc = lax.broadcasted_iota(jnp.int32, (2 * CHUNK, 2 * CHUNK), 1)
        band = (kc >= qi) & (kc <= qi + N_BACK)
        for j in range(nb):
            rows = slice(j * CHUNK, (j + 1) * CHUNK)
            if j == 0:
                kcat = jnp.concatenate([kp_ref[...], k_ref[rows, :]], axis=0)
                vcat = jnp.concatenate([vp_ref[...], v_ref[rows, :]], axis=0)
                valid = band & jnp.logical_or(n > 0, kc >= CHUNK)
            else:
                kcat = k_ref[(j - 1) * CHUNK:(j + 1) * CHUNK, :]
                vcat = v_ref[(j - 1) * CHUNK:(j + 1) * CHUNK, :]
                valid = band
            q2 = jnp.concatenate(_both_heads(q_ref[rows, :], head_a), axis=0)
            s = lax.dot_general(q2, kcat, _NT, preferred_element_type=F32)
            s = jnp.where(valid, s, NEG)
            m = jnp.max(s, axis=-1, keepdims=True)
            p = jnp.exp(s - m)
            l = jnp.sum(p, axis=-1, keepdims=True)
            o2 = jnp.dot(p.astype(BF16), vcat, preferred_element_type=F32) / l
            lse2 = m + jnp.log(l)
            o_ref[rows, :] = jnp.where(head_a, o2[:CHUNK], o2[CHUNK:]).astype(BF16)
            others = l_ref[rows, :] if hp > 0 else jnp.zeros((CHUNK, 128), F32)
            l_ref[rows, :] = jnp.where(lane == 2 * hp, lse2[:CHUNK], jnp.where(lane == 2 * hp + 1, lse2[CHUNK:], others))

    same = lambda n: n
    return _call(
        body, name=f"attn_fwd_d{d}", grid=(d // res, N_PAIRS // PAIRS_PER_STEP, n_tiles),
        in_specs=[_attn_spec(0, tile, same, res), _attn_spec(N_PAIRS, tile, same, res), _attn_spec(N_PAIRS, CHUNK, prev, res),
                  _attn_spec(2 * N_PAIRS, tile, same, res), _attn_spec(2 * N_PAIRS, CHUNK, prev, res)],
        out_specs=[_attn_spec(0, tile, same, res), _stats_spec(tile, same, res)],
        out_shape=[jax.ShapeDtypeStruct((d, N_PAIRS, sd, 128), BF16), _stats_shape(sd, d)],
        semantics=("parallel", "parallel", "parallel"), args=(qkv, qkv, qkv, qkv, qkv), exchange=exchange)


def _combine(outs, lses, ya_n, gb, head_spread, exchange=()):
    s_len = ya_n.shape[0]
    tm = ROW_TILE
    n_br = len(DILATIONS)

    def body(*refs):
        o_refs, l_refs = refs[:n_br], refs[n_br:2 * n_br]
        ya_ref, gb_ref, spread_ref, y_ref, yb_ref = refs[2 * n_br:2 * n_br + 5]
        lse_refs = refs[2 * n_br + 5:3 * n_br + 5]
        o_nat, l_nat, lse_nat, w_wide = refs[3 * n_br + 5:]
        for i, d in enumerate(DILATIONS):
            _from_sub(l_refs[i], 0, l_nat, i, d, tm)
        ls = [l_nat[i] for i in range(n_br)]
        top = jnp.maximum(jnp.maximum(ls[0], ls[1]), ls[2])
        ws = [jnp.exp(l - top) for l in ls]
        den = ws[0] + ws[1] + ws[2]
        inv = 1.0 / den
        for i in range(n_br):
            w = ws[i] * inv
            hi = w.astype(BF16)
            lo = (w - hi.astype(F32)).astype(BF16)
            w_wide[i] = (jnp.dot(hi, spread_ref[...], preferred_element_type=F32)
                         + jnp.dot(lo, spread_ref[...], preferred_element_type=F32))
        lse_nat[0] = top + jnp.log(den)
        for d, lse_ref in zip(DILATIONS, lse_refs):
            _to_sub(lse_nat, 0, lse_ref, 0, d, tm)
        sumsq = jnp.zeros((tm, 1), F32)
        for cb in range(N_PAIRS):
            cols = slice(cb * 128, (cb + 1) * 128)
            yb = w_wide[0, :, cols] * o_refs[0][0, cb].astype(F32)
            for i, d in enumerate(DILATIONS[1:], start=1):
                _from_sub(o_refs[i], cb, o_nat, i, d, tm)
                yb = yb + w_wide[i, :, cols] * o_nat[i]
            yb_ref[:, cb * 128:(cb + 1) * 128] = yb
            sumsq = sumsq + jnp.sum(yb * yb, axis=-1, keepdims=True)
        r = lax.rsqrt(sumsq / WIDTH_B + EPS)
        y_ref[:, :WIDTH_A] = ya_ref[...]
        y_ref[:, WIDTH_A:] = (yb_ref[...] * r * gb_ref[...]).astype(BF16)

    stats = [_sub_spec(d, 1, tm) for d in DILATIONS]
    return _call(
        body, name="attn_combine", grid=(s_len // tm,),
        in_specs=[_sub_spec(d, N_PAIRS, tm) for d in DILATIONS] + stats
        + [_rows(tm, WIDTH_A), _whole((1, WIDTH_B)), _whole((128, WIDTH_B))],
        out_specs=[_rows(tm, D_MODEL), _rows(tm, WIDTH_B)] + stats,
        out_shape=[jax.ShapeDtypeStruct((s_len, D_MODEL), BF16), jax.ShapeDtypeStruct((s_len, WIDTH_B), F32)]
        + [_sub_shape(s_len, d, 1, F32) for d in DILATIONS],
        scratch_shapes=[pltpu.VMEM((n_br, tm, 128), F32), pltpu.VMEM((n_br, tm, 128), F32), pltpu.VMEM((1, tm, 128), F32),
                        pltpu.VMEM((n_br, tm, WIDTH_B), F32)],
        semantics=("parallel",), args=(*outs, *lses, ya_n, gb, head_spread), exchange=exchange)


def _ffn_up(y, wout, x, g, wg, wu, exchange=()):
    s_len = x.shape[0]
    tm = ROW_TILE

    def body(y_ref, wo_ref, x_ref, g_ref, wg_ref, wu_ref, h_ref, a_ref, b_ref, act_ref, hn_ref):
        hf = x_ref[...] + jnp.dot(y_ref[...], wo_ref[...], preferred_element_type=F32)
        h_ref[...] = hf
        hn = (hf * _rstd(hf) * g_ref[...]).astype(BF16)
        hn_ref[...] = hn
        for col in range(0, D_FF, 512):
            cols = slice(col, min(col + 512, D_FF))
            a = lax.dot_general(hn, wg_ref[cols, :], _NT, preferred_element_type=F32)
            b = lax.dot_general(hn, wu_ref[cols, :], _NT, preferred_element_type=F32)
            a_ref[:, cols] = a.astype(BF16)
            b_ref[:, cols] = b.astype(BF16)
            act_ref[:, cols] = (a * jax.nn.sigmoid(a) * b).astype(BF16)

    wide = jax.ShapeDtypeStruct((s_len, D_FF), BF16)
    return _call(
        body, name="ffn_up", grid=(s_len // tm,),
        in_specs=[_rows(tm, D_MODEL), _resident((D_MODEL, D_MODEL)), _rows(tm, D_MODEL), _whole((1, D_MODEL)),
                  _resident((D_FF, D_MODEL)), _resident((D_FF, D_MODEL))],
        out_specs=[_rows(tm, D_MODEL), _rows(tm, D_FF), _rows(tm, D_FF), _rows(tm, D_FF), _rows(tm, D_MODEL)],
        out_shape=[jax.ShapeDtypeStruct((s_len, D_MODEL), F32), wide, wide, wide, jax.ShapeDtypeStruct((s_len, D_MODEL), BF16)],
        semantics=("parallel",), args=(y, wout, x, g, wg, wu), exchange=exchange)


def _ffn_down_ple(act, wd, h1, g, wpg, p, wpp):
    s_len = h1.shape[0]
    tm = ROW_TILE

    def body(act_ref, wd_ref, h1_ref, g_ref, wpg_ref, p_ref, wpp_ref, h2_ref, h3_ref, gate_ref, pp_ref, hn_ref):
        hf = h1_ref[...] + jnp.dot(act_ref[...], wd_ref[...], preferred_element_type=F32)
        h2_ref[...] = hf
        hn = (hf * _rstd(hf) * g_ref[...]).astype(BF16)
        hn_ref[...] = hn
        gate = jax.nn.sigmoid(jnp.dot(hn, wpg_ref[...], preferred_element_type=F32))
        pp = lax.dot_general(p_ref[...].astype(BF16), wpp_ref[...], _NT, preferred_element_type=F32)
        h3_ref[...] = hf + gate * pp
        gate_ref[...] = gate.astype(BF16)
        pp_ref[...] = pp.astype(BF16)

    full = jax.ShapeDtypeStruct((s_len, D_MODEL), F32)
    half = jax.ShapeDtypeStruct((s_len, D_MODEL), BF16)
    return pl.pallas_call(
        body, name="ffn_down_ple", grid=(s_len // tm,),
        in_specs=[_rows(tm, D_FF), _resident((D_FF, D_MODEL)), _rows(tm, D_MODEL), _whole((1, D_MODEL)),
                  _resident((D_MODEL, D_MODEL)), _rows(tm, PLE_DIM), _resident((D_MODEL, PLE_DIM))],
        out_specs=[_rows(tm, D_MODEL)] * 5,
        out_shape=[full, full, half, half, half],
        compiler_params=_params("parallel"),
    )(act, wd, h1, g, wpg, p, wpp)


def _loss_ple_bwd(h3, target, gf, gate, pp, h2, g_ple, wpg, hn3, p):
    s_len = h3.shape[0]
    tm = ROW_TILE
    n_steps = s_len // tm

    def body(h_ref, t_ref, g_ref, gate_ref, pp_ref, h2_ref, gp_ref, w_ref, hn_ref, p_ref,
             dh2_ref, loss_ref, dg_ref, dgp_ref, dwg_ref, dwp_ref, acc_g, acc_p):
        step = pl.program_id(0)

        @pl.when(step == 0)
        def _():
            loss_ref[...] = jnp.zeros_like(loss_ref)
            dg_ref[...] = jnp.zeros_like(dg_ref)
            dgp_ref[...] = jnp.zeros_like(dgp_ref)
            acc_g[...] = jnp.zeros_like(acc_g)
            acc_p[...] = jnp.zeros_like(acc_p)

        hf = h_ref[...]
        gfv = g_ref[...]
        r = _rstd(hf)
        err = hf * r * gfv - t_ref[...]
        loss_ref[...] += 0.5 * jnp.sum(jnp.sum(err * err, axis=-1, keepdims=True), axis=0, keepdims=True) / D_MODEL
        dh, dg_rows = _norm_bwd(err / D_MODEL, hf, gfv, r)
        dg_ref[...] += jnp.sum(dg_rows, axis=0, keepdims=True)
        gate = gate_ref[...].astype(F32)
        dz = (dh * pp_ref[...].astype(F32) * gate * (1.0 - gate)).astype(BF16)
        dpp = (dh * gate).astype(BF16)
        dn = lax.dot_general(dz, w_ref[...], _NT, preferred_element_type=F32)
        dh2, dgp_rows = _norm_bwd(dn, h2_ref[...], gp_ref[...])
        dh2 = dh + dh2
        dh2_ref[...] = dh2
        dgp_ref[...] += jnp.sum(dgp_rows, axis=0, keepdims=True)
        acc_g[...] += lax.dot_general(hn_ref[...], dz, _TN, preferred_element_type=F32)
        acc_p[...] += lax.dot_general(dpp, p_ref[...].astype(BF16), _TN, preferred_element_type=F32)

        @pl.when(step == n_steps - 1)
        def _():
            dwg_ref[...] = acc_g[...].astype(BF16)
            dwp_ref[...] = acc_p[...].astype(BF16)

    gain = jax.ShapeDtypeStruct((1, D_MODEL), F32)
    return pl.pallas_call(
        body, name="loss_ple_bwd", grid=(n_steps,),
        in_specs=[_rows(tm, D_MODEL), _rows(tm, D_MODEL), _whole((1, D_MODEL)), _rows(tm, D_MODEL), _rows(tm, D_MODEL),
                  _rows(tm, D_MODEL), _whole((1, D_MODEL)), _resident((D_MODEL, D_MODEL)), _rows(tm, D_MODEL),
                  _rows(tm, PLE_DIM)],
        out_specs=[_rows(tm, D_MODEL), _whole((1, 128)), _whole((1, D_MODEL)), _whole((1, D_MODEL)),
                   _whole((D_MODEL, D_MODEL)), _whole((D_MODEL, PLE_DIM))],
        out_shape=[jax.ShapeDtypeStruct((s_len, D_MODEL), F32), jax.ShapeDtypeStruct((1, 128), F32), gain, gain,
                   jax.ShapeDtypeStruct((D_MODEL, D_MODEL), BF16), jax.ShapeDtypeStruct((D_MODEL, PLE_DIM), BF16)],
        scratch_shapes=[pltpu.VMEM((D_MODEL, D_MODEL), F32), pltpu.VMEM((D_MODEL, PLE_DIM), F32)],
        compiler_params=_params("arbitrary"),
    )(h3, target, gf, gate, pp, h2, g_ple, wpg, hn3, p)


def _mm_norm_bwd(parts, h, g, dres, name, exchange=(), dw_lhs=None):
    s_len = h.shape[0]
    tm = ROW_TILE
    n_parts = len(parts)
    n_steps = s_len // tm
    has_dw = dw_lhs is not None

    def body(*refs):
        a_refs = refs[0:2 * n_parts:2]
        w_refs = refs[1:2 * n_parts:2]
        h_ref, g_ref, r_ref = refs[2 * n_parts:2 * n_parts + 3]
        rest = refs[2 * n_parts + 3:]
        step = pl.program_id(0)
        if has_dw:
            lhs_ref, o_ref, dg_ref, dw_ref, acc_ref = rest
        else:
            o_ref, dg_ref = rest

        @pl.when(step == 0)
        def _():
            dg_ref[...] = jnp.zeros_like(dg_ref)
            if has_dw:
                acc_ref[...] = jnp.zeros_like(acc_ref)

        dn = jnp.dot(a_refs[0][...], w_refs[0][...], preferred_element_type=F32)
        for a_ref, w_ref in zip(a_refs[1:], w_refs[1:]):
            dn = dn + jnp.dot(a_ref[...], w_ref[...], preferred_element_type=F32)
        dh, dg_rows = _norm_bwd(dn, h_ref[...], g_ref[...])
        out = r_ref[...] + dh
        o_ref[...] = out
        dg_ref[...] += jnp.sum(dg_rows, axis=0, keepdims=True)
        if has_dw:
            acc_ref[...] += lax.dot_general(lhs_ref[...], out.astype(BF16), _TN, preferred_element_type=F32)

            @pl.when(step == n_steps - 1)
            def _():
                dw_ref[...] = acc_ref[...].astype(BF16)

    in_specs, args = [], []
    for a, w in parts:
        in_specs += [_rows(tm, a.shape[1]), _resident(w.shape)]
        args += [a, w]
    in_specs += [_rows(tm, D_MODEL), _whole((1, D_MODEL)), _rows(tm, D_MODEL)]
    args += [h, g, dres]
    out_specs = [_rows(tm, D_MODEL), _whole((1, D_MODEL))]
    out_shape = [jax.ShapeDtypeStruct((s_len, D_MODEL), F32), jax.ShapeDtypeStruct((1, D_MODEL), F32)]
    scratch = []
    if has_dw:
        m = dw_lhs.shape[1]
        in_specs.append(_rows(tm, m))
        args.append(dw_lhs)
        out_specs.append(_whole((m, D_MODEL)))
        out_shape.append(jax.ShapeDtypeStruct((m, D_MODEL), BF16))
        scratch.append(pltpu.VMEM((m, D_MODEL), F32))
    return _call(
        body, name=name, grid=(n_steps,), in_specs=in_specs, out_specs=out_specs, out_shape=out_shape,
        scratch_shapes=scratch, semantics=("arbitrary",), args=tuple(args), exchange=exchange)


def _ffn_down_bwd(dh, wdt, a, b, exchange=()):
    s_len = dh.shape[0]
    tm = ROW_TILE

    def body(dh_ref, w_ref, a_ref, b_ref, da_ref, db_ref):
        dhb = dh_ref[...].astype(BF16)
        for col in range(0, D_FF, 512):
            cols = slice(col, min(col + 512, D_FF))
            dact = lax.dot_general(dhb, w_ref[cols, :], _NT, preferred_element_type=F32)
            av = a_ref[:, cols].astype(F32)
            bv = b_ref[:, cols].astype(F32)
            sig = jax.nn.sigmoid(av)
            t = dact * sig
            silu = av * sig
            da_ref[:, cols] = (t * bv * (1.0 + av - silu)).astype(BF16)
            db_ref[:, cols] = (dact * silu).astype(BF16)

    wide = jax.ShapeDtypeStruct((s_len, D_FF), BF16)
    return _call(
        body, name="ffn_down_bwd", grid=(s_len // tm,),
        in_specs=[_rows(tm, D_MODEL), _resident((D_FF, D_MODEL)), _rows(tm, D_FF), _rows(tm, D_FF)],
        out_specs=[_rows(tm, D_FF), _rows(tm, D_FF)],
        out_shape=[wide, wide],
        semantics=("parallel",), args=(dh, wdt, a, b), exchange=exchange)


def _outproj_bwd(dh1, woutt, yb, gb, head_sum, exchange=()):
    s_len = dh1.shape[0]
    tm = ROW_TILE
    n_br = len(DILATIONS)

    def body(dh_ref, w_ref, yb_ref, gb_ref, e_ref, dya_ref, dgb_ref, *rest):
        do_refs, dd_refs = rest[:n_br], rest[n_br:2 * n_br]
        do_nat, dd_nat = rest[2 * n_br:]

        @pl.when(pl.program_id(0) == 0)
        def _():
            dgb_ref[...] = jnp.zeros_like(dgb_ref)

        dhb = dh_ref[...].astype(BF16)
        dya_ref[...] = lax.dot_general(dhb, w_ref[:WIDTH_A, :], _NT, preferred_element_type=F32)
        dyn = lax.dot_general(dhb, w_ref[WIDTH_A:, :], _NT, preferred_element_type=F32)
        ybv = yb_ref[...]
        dyb, dg_rows = _norm_bwd(dyn, ybv, gb_ref[...])
        dgb_ref[...] += jnp.sum(dg_rows, axis=0, keepdims=True)
        prod = dyb * ybv
        hi = prod.astype(BF16)
        lo = (prod - hi.astype(F32)).astype(BF16)
        dd_nat[0] = (jnp.dot(hi, e_ref[...], preferred_element_type=F32)
                     + jnp.dot(lo, e_ref[...], preferred_element_type=F32))
        for i, d in enumerate(DILATIONS):
            _to_sub(dd_nat, 0, dd_refs[i], 0, d, tm)
        for cb in range(N_PAIRS):
            piece = dyb[:, cb * 128:(cb + 1) * 128]
            do_nat[cb] = piece
            do_refs[0][0, cb] = piece.astype(BF16)
            for i, d in enumerate(DILATIONS[1:], start=1):
                _to_sub(do_nat, cb, do_refs[i], cb, d, tm)

    return _call(
        body, name="outproj_bwd", grid=(s_len // tm,),
        in_specs=[_rows(tm, D_MODEL), _resident((D_MODEL, D_MODEL)), _rows(tm, WIDTH_B), _whole((1, WIDTH_B)), _whole((WIDTH_B, 128))],
        out_specs=[_rows(tm, WIDTH_A), _whole((1, WIDTH_B))] + [_sub_spec(d, N_PAIRS, tm) for d in DILATIONS]
        + [_sub_spec(d, 1, tm) for d in DILATIONS],
        out_shape=[jax.ShapeDtypeStruct((s_len, WIDTH_A), F32), jax.ShapeDtypeStruct((1, WIDTH_B), F32)]
        + [_sub_shape(s_len, d, N_PAIRS, BF16) for d in DILATIONS] + [_sub_shape(s_len, d, 1, F32) for d in DILATIONS],
        scratch_shapes=[pltpu.VMEM((N_PAIRS, tm, 128), F32), pltpu.VMEM((1, tm, 128), F32)],
        semantics=("arbitrary",), args=(dh1, woutt, yb, gb, head_sum), exchange=exchange)


def _attn_bwd(qkv, do, lse, dd, d, exchange=()):
    sd = qkv.shape[2]
    tile, nb, n_tiles, res = _attn_geometry(sd)
    last_block = sd // CHUNK - 1

    def nxt(n):
        return jnp.minimum((n + 1) * nb, last_block)

    def block(ref, next_ref, j):
        return ref[j * CHUNK:(j + 1) * CHUNK, :] if j < nb else next_ref[...]

    def body(q_ref, qn_ref, k_ref, v_ref, do_ref, don_ref, l_ref, ln_ref, dd_ref, ddn_ref,
             dq_ref, dk_ref, dv_ref, carry_ref):
        for rr in range(res):
            l_t = [block(l_ref.at[rr], ln_ref.at[rr], j).T for j in range(nb + 1)]
            dd_t = [block(dd_ref.at[rr], ddn_ref.at[rr], j).T for j in range(nb + 1)]
            for hp in range(PAIRS_PER_STEP):
                l_rows = [jnp.concatenate([t[2 * hp:2 * hp + 1, :], t[2 * hp + 1:2 * hp + 2, :]], axis=1) for t in l_t]
                dd_rows = [jnp.concatenate([t[2 * hp:2 * hp + 1, :], t[2 * hp + 1:2 * hp + 2, :]], axis=1) for t in dd_t]
                one_pair(q_ref.at[rr, hp], qn_ref.at[rr, hp], k_ref.at[rr, hp], v_ref.at[rr, hp], do_ref.at[rr, hp],
                         don_ref.at[rr, hp], l_rows, dd_rows, dq_ref.at[rr, hp], dk_ref.at[rr, hp], dv_ref.at[rr, hp],
                         carry_ref.at[rr, hp])

    def one_pair(q_ref, qn_ref, k_ref, v_ref, do_ref, don_ref, l_rows, dd_rows, dq_ref, dk_ref, dv_ref, carry_ref):
        n = pl.program_id(2)

        @pl.when(n == 0)
        def _():
            carry_ref[...] = jnp.zeros_like(carry_ref)

        head_a = lax.broadcasted_iota(jnp.int32, (CHUNK, 128), 1) < HEAD_DIM
        col = lax.broadcasted_iota(jnp.int32, (CHUNK, 4 * CHUNK), 1)
        qi = col % CHUNK
        ki = lax.broadcasted_iota(jnp.int32, (CHUNK, 4 * CHUNK), 0)
        is_after = col >= 2 * CHUNK
        mask = (is_after & (ki >= qi)) | (jnp.logical_not(is_after) & (qi >= ki))
        mask_last = mask & jnp.logical_or(jnp.logical_not(is_after), n < n_tiles - 1)
        dq_acc = [carry_ref[...]] + [jnp.zeros((CHUNK, 128), F32) for _ in range(nb)]

        q_st = [jnp.concatenate(_both_heads(block(q_ref, qn_ref, j), head_a), axis=0) for j in range(nb + 1)]
        do_st = [jnp.concatenate(_both_heads(block(do_ref, don_ref, j), head_a), axis=0) for j in range(nb + 1)]

        for j in range(nb):
            rows = slice(j * CHUNK, (j + 1) * CHUNK)
            kj = k_ref[rows, :]
            vj = v_ref[rows, :]
            msk = mask if j + 1 < nb else mask_last
            qs = jnp.concatenate([q_st[j], q_st[j + 1]], axis=0)
            dos = jnp.concatenate([do_st[j], do_st[j + 1]], axis=0)
            ls = jnp.concatenate([l_rows[j], l_rows[j + 1]], axis=1)
            dds = jnp.concatenate([dd_rows[j], dd_rows[j + 1]], axis=1)
            st = lax.dot_general(kj, qs, _NT, preferred_element_type=F32)
            pt = jnp.exp(jnp.where(msk, st - ls, NEG))
            dpt = lax.dot_general(vj, dos, _NT, preferred_element_type=F32)
            dst = (pt * (dpt - dds)).astype(BF16)
            dv_ref[rows, :] = jnp.dot(pt.astype(BF16), dos, preferred_element_type=F32).astype(BF16)
            dk_ref[rows, :] = jnp.dot(dst, qs, preferred_element_type=F32).astype(BF16)
            dqs = lax.dot_general(dst, kj, _TN, preferred_element_type=F32)
            dq_acc[j] = dq_acc[j] + jnp.where(head_a, dqs[:CHUNK], dqs[CHUNK:2 * CHUNK])
            dq_acc[j + 1] = dq_acc[j + 1] + jnp.where(head_a, dqs[2 * CHUNK:3 * CHUNK], dqs[3 * CHUNK:])
        for j in range(nb):
            dq_ref[j * CHUNK:(j + 1) * CHUNK, :] = dq_acc[j].astype(BF16)
        carry_ref[...] = dq_acc[nb]

    same = lambda n: n
    grad = jax.ShapeDtypeStruct((d, N_PAIRS, sd, 128), BF16)
    return _call(
        body, name=f"attn_bwd_d{d}", grid=(d // res, N_PAIRS // PAIRS_PER_STEP, n_tiles),
        in_specs=[_attn_spec(0, tile, same, res), _attn_spec(0, CHUNK, nxt, res), _attn_spec(N_PAIRS, tile, same, res),
                  _attn_spec(2 * N_PAIRS, tile, same, res), _attn_spec(0, tile, same, res), _attn_spec(0, CHUNK, nxt, res),
                  _stats_spec(tile, same, res), _stats_spec(CHUNK, nxt, res), _stats_spec(tile, same, res),
                  _stats_spec(CHUNK, nxt, res)],
        out_specs=[_attn_spec(0, tile, same, res)] * 3,
        out_shape=[grad, grad, grad],
        scratch_shapes=[pltpu.VMEM((res, PAIRS_PER_STEP, CHUNK, 128), F32)],
        semantics=("parallel", "parallel", "arbitrary"), args=(qkv, qkv, qkv, qkv, do, do, lse, lse, dd, dd), exchange=exchange)


def _sgu_bwd(ua, sw, b2, gs, ga, dya_n):
    s_len = ua.shape[0]
    tm = ROW_TILE

    def body(ua_ref, sw_ref, b2_ref, gs_ref, ga_ref, dy_ref, dua_ref, dsw_ref, db2_ref, dgs_ref, dga_ref):
        @pl.when(pl.program_id(0) == 0)
        def _():
            dsw_ref[...] = jnp.zeros_like(dsw_ref)
            db2_ref[...] = jnp.zeros_like(db2_ref)
            dgs_ref[...] = jnp.zeros_like(dgs_ref)
            dga_ref[...] = jnp.zeros_like(dga_ref)

        u, va, ug, xhat, rstd, vn = _sgu_core(ua_ref, gs_ref)
        wm, keep = _sgu_mix_weights(sw_ref)
        head = lax.broadcasted_iota(jnp.int32, (CHUNK, WIDTH_A), 1) // HEAD_DIM
        gav = ga_ref[...]
        gsv = gs_ref[...]
        dga = jnp.zeros((1, WIDTH_A), F32)
        dgs = jnp.zeros((1, WIDTH_A), F32)
        db2 = jnp.zeros((CHUNK, WIDTH_A), F32)
        dsw = [jnp.zeros((CHUNK, CHUNK), F32) for _ in range(4)]
        for c in range(tm // CHUNK):
            rows = slice(c * CHUNK, (c + 1) * CHUNK)
            vnc = vn[rows]
            vnb = vnc.astype(BF16)
            mixed = b2_ref[...]
            for h in range(4):
                mixed = mixed + jnp.dot(wm[h], jnp.where(head == h, vnc, 0.0).astype(BF16), preferred_element_type=F32)
            ugc = ug[rows]
            dya, dga_rows = _norm_bwd(dy_ref[rows, :], ugc * mixed, gav)
            dga = dga + jnp.sum(dga_rows, axis=0, keepdims=True)
            dmixed = dya * ugc
            db2 = db2 + dmixed
            dvn = jnp.zeros((CHUNK, WIDTH_A), F32)
            for h in range(4):
                dmh = jnp.where(head == h, dmixed, 0.0).astype(BF16)
                dsw[h] = dsw[h] + lax.dot_general(dmh, vnb, _NT, preferred_element_type=F32)
                dvn = dvn + lax.dot_general(wm[h], dmh, _TN, preferred_element_type=F32)
            xh = xhat[rows]
            dgs = dgs + jnp.sum(dvn * xh, axis=0, keepdims=True)
            dxh = dvn * gsv
            dvg = rstd[rows] * (dxh - jnp.mean(dxh, axis=-1, keepdims=True) - xh * jnp.mean(dxh * xh, axis=-1, keepdims=True))
            dua_ref[rows, :WIDTH_A] = (dya * mixed * _gelu_grad(u[rows])).astype(BF16)
            dua_ref[rows, WIDTH_A:] = (dvg * _gelu_grad(va[rows])).astype(BF16)
        for h in range(4):
            dsw_ref[h] += jnp.where(keep, dsw[h], 0.0)
        db2_ref[...] += db2
        dgs_ref[...] += dgs
        dga_ref[...] += dga

    return pl.pallas_call(
        body, name="sgu_bwd", grid=(s_len // tm,),
        in_specs=[_rows(tm, 2 * WIDTH_A), _whole((4, CHUNK, CHUNK)), _whole((CHUNK, WIDTH_A)), _whole((1, WIDTH_A)),
                  _whole((1, WIDTH_A)), _rows(tm, WIDTH_A)],
        out_specs=[_rows(tm, 2 * WIDTH_A), _whole((4, CHUNK, CHUNK)), _whole((CHUNK, WIDTH_A)), _whole((1, WIDTH_A)), _whole((1, WIDTH_A))],
        out_shape=[jax.ShapeDtypeStruct((s_len, 2 * WIDTH_A), BF16), jax.ShapeDtypeStruct((4, CHUNK, CHUNK), F32),
                   jax.ShapeDtypeStruct((CHUNK, WIDTH_A), F32), jax.ShapeDtypeStruct((1, WIDTH_A), F32),
                   jax.ShapeDtypeStruct((1, WIDTH_A), F32)],
        compiler_params=_params("arbitrary"),
    )(ua, sw, b2, gs, ga, dya_n)


def _dproj(dua, dqs, dks, dvs, cos, sin, hn1, exchange=()):
    s_len = dua.shape[0]
    tm = ROW_TILE
    n_br = len(DILATIONS)
    n_steps = s_len // tm

    def body(dua_ref, *rest):
        groups = [rest[g * n_br:(g + 1) * n_br] for g in range(3)]
        cos_ref, sin_ref, hn_ref, out_ref, dw_ref, acc, dw_acc = rest[3 * n_br:]
        step = pl.program_id(0)

        @pl.when(step == 0)
        def _():
            dw_acc[...] = jnp.zeros_like(dw_acc)

        out_ref[:, :2 * WIDTH_A] = dua_ref[...]
        c = cos_ref[...]
        s = sin_ref[...]
        first_half = (lax.broadcasted_iota(jnp.int32, (tm, 128), 1) % HEAD_DIM) < HEAD_DIM // 2
        for g, refs in enumerate(groups):
            for cb in range(N_PAIRS):
                t = refs[0][0, cb].astype(F32)
                for i, d in enumerate(DILATIONS[1:]):
                    _from_sub(refs[i + 1], cb, acc, i, d, tm)
                    t = t + acc[i]
                if g < 2:
                    t = (t * c - _swap_halves(t, first_half) * s) * (0.125 if g == 0 else 1.0)
                col = 2 * WIDTH_A + g * WIDTH_B + cb * 128
                out_ref[:, col:col + 128] = t.astype(BF16)
        hn = hn_ref[...]
        for j in range(IN_COLS // 256):
            cols = slice(j * 256, (j + 1) * 256)
            dw_acc[cols, :] += lax.dot_general(out_ref[:, cols], hn, _TN, preferred_element_type=F32)

        @pl.when(step == n_steps - 1)
        def _():
            dw_ref[...] = dw_acc[...].astype(BF16)

    subs = [_sub_spec(d, N_PAIRS, tm) for d in DILATIONS]
    (dproj, dw), received = _call(
        body, name="dproj_dw_in", grid=(n_steps,),
        in_specs=[_rows(tm, 2 * WIDTH_A)] + subs * 3 + [_rows(tm, 128), _rows(tm, 128), _rows(tm, D_MODEL)],
        out_specs=[_rows(tm, IN_COLS), _whole((IN_COLS, D_MODEL))],
        out_shape=[jax.ShapeDtypeStruct((s_len, IN_COLS), BF16), jax.ShapeDtypeStruct((IN_COLS, D_MODEL), BF16)],
        scratch_shapes=[pltpu.VMEM((n_br - 1, tm, 128), F32), pltpu.VMEM((IN_COLS, D_MODEL), F32)],
        semantics=("arbitrary",), args=(dua, *dqs, *dks, *dvs, cos, sin, hn1), exchange=exchange)
    return dproj, dw, received


def _mm_tn(a, b, name, exchange=()):
    s_len, m = a.shape
    n = b.shape[1]
    tk = 2 * ROW_TILE
    tm = m if m <= 512 else (1408 if m == D_FF else 512)
    n_k = s_len // tk

    def body(a_ref, b_ref, o_ref, acc_ref):
        k = pl.program_id(1)

        @pl.when(k == 0)
        def _():
            acc_ref[...] = jnp.zeros_like(acc_ref)

        acc_ref[...] += lax.dot_general(a_ref[...].astype(BF16), b_ref[...].astype(BF16), _TN, preferred_element_type=F32)

        @pl.when(k == n_k - 1)
        def _():
            o_ref[...] = acc_ref[...].astype(BF16)

    (grad,), received = _call(
        body, name=name, grid=(m // tm, n_k),
        in_specs=[pl.BlockSpec((tk, tm), lambda i, k: (k, i)), pl.BlockSpec((tk, n), lambda i, k: (k, 0))],
        out_specs=[pl.BlockSpec((tm, n), lambda i, k: (i, 0))],
        out_shape=[jax.ShapeDtypeStruct((m, n), BF16)],
        scratch_shapes=[pltpu.VMEM((tm, n), F32)],
        semantics=("parallel", "arbitrary"), args=(a, b), exchange=exchange)
    return grad, received


def _position():
    x, y, c = lax.axis_index("x"), lax.axis_index("y"), lax.axis_index("c")
    return x, y, c, 4 * x + 2 * y + c


def _peer(x, y, c, rel):
    return (x ^ ((rel >> 2) & 1), y ^ ((rel >> 1) & 1), c ^ (rel & 1))


def _exchange_out_shape(kind, arr):
    return jax.ShapeDtypeStruct(((N_DEV,) + arr.shape) if kind == "gather" else arr.shape, arr.dtype)


def _exchange_sems(n_items):
    return [pltpu.SemaphoreType.DMA((n_items, N_DEV)), pltpu.SemaphoreType.DMA((n_items, N_DEV)), pltpu.SemaphoreType.DMA((n_items,))]


def _exchange_copies(kinds, srcs, dsts, sems, arrivals):
    send_sems, recv_sems, local_sems = sems
    x, y, c, me = _position()
    local, sends, recvs = [], [], []
    for k, (kind, src, dst) in enumerate(zip(kinds, srcs, dsts)):
        own = src if kind == "gather" else src.at[me]
        local.append(pltpu.make_async_copy(own, dst.at[me], local_sems.at[k]))
        for rel in range(1, N_DEV):
            going = src if kind == "gather" else src.at[me ^ rel]
            common = dict(send_sem=send_sems.at[k, rel], recv_sem=recv_sems.at[k, rel],
                          device_id=_peer(x, y, c, rel), device_id_type=MESH)
            sends.append(pltpu.make_async_remote_copy(src_ref=going, dst_ref=dst.at[me], **common))
            if arrivals:
                recvs.append(pltpu.make_async_remote_copy(src_ref=own, dst_ref=dst.at[me ^ rel], **common))
    return local, sends, recvs


def _exchange_start(kinds, srcs, dsts, sems):
    local, sends, _ = _exchange_copies(kinds, srcs, dsts, sems, arrivals=False)
    for cp in local + sends:
        cp.start()


def _exchange_finish(kinds, srcs, dsts, sems):
    local, sends, recvs = _exchange_copies(kinds, srcs, dsts, sems, arrivals=True)
    for cp in recvs:
        cp.wait_recv()
    for cp in sends:
        cp.wait_send()
    for cp in local:
        cp.wait()


def _exchange_only(items, name):
    kinds = [k for k, _ in items]
    n = len(items)

    def body(*refs):
        srcs, dsts, sems = refs[:n], refs[n:2 * n], refs[2 * n:]
        _exchange_start(kinds, srcs, dsts, sems)
        _exchange_finish(kinds, srcs, dsts, sems)

    any_spec = pl.BlockSpec(memory_space=pl.ANY)
    return pl.pallas_call(
        body, name=name, in_specs=[any_spec] * n, out_specs=[any_spec] * n,
        out_shape=[_exchange_out_shape(k, a) for k, a in items],
        scratch_shapes=_exchange_sems(n),
        compiler_params=pltpu.CompilerParams(has_side_effects=True),
    )(*[a for _, a in items])


def _gather_two_level_with_rope_tables(shard, inv_freq, s_len, name):
    rows = ROW_TILE

    def body(inv_ref, src, cos_ref, sin_ref, dst, send_sems, recv_sems, local_sem):
        x, y, c, me = _position()
        sibling = (x, y, 1 - c)
        chips = [(1 - x, y), (x, 1 - y), (1 - x, 1 - y)]

        def block(px, py, pc):
            return dst.at[4 * px + 2 * py + pc]

        def copy(k, blk, to, src_ref=None):
            return pltpu.make_async_remote_copy(
                src_ref=block(*blk) if src_ref is None else src_ref, dst_ref=block(*blk),
                send_sem=send_sems.at[k], recv_sem=recv_sems.at[k], device_id=to, device_id_type=MESH)

        x_nbr, y_nbr, diag = chips
        mine = pltpu.make_async_copy(src, dst.at[me], local_sem)
        mine.start()
        first = [copy(0, (x, y, c), sibling, src), copy(1, (x, y, c), (*x_nbr, c), src), copy(2, (x, y, c), (*y_nbr, c), src)]
        for cp in first:
            cp.start()

        inv = inv_ref[...]
        lane = lax.broadcasted_iota(jnp.int32, (rows, 128), 1)
        sign = jnp.where((lane // (HEAD_DIM // 2)) % 2 == 0, -1.0, 1.0)
        row = lax.broadcasted_iota(jnp.int32, (rows, 128), 0)
        n_chunks = s_len // rows

        def fill_tables(lo, hi):
            @pl.loop(lo, hi)
            def _(i):
                at = pl.multiple_of(i * rows, rows)
                ang = (row + at).astype(F32) * inv
                cos_ref[pl.ds(at, rows), :] = jnp.cos(ang)
                sin_ref[pl.ds(at, rows), :] = jnp.sin(ang) * sign

        fill_tables(0, n_chunks // 2)
        passed = [copy(4 + j, (*chip, c), sibling) for j, chip in enumerate(chips)]
        copy(1, (*x_nbr, c), (x, y, c)).wait_recv()
        copy(2, (*y_nbr, c), (x, y, c)).wait_recv()

        @pl.when(c == 1)
        def _():
            copy(3, (*x_nbr, c), (*y_nbr, c)).start()

        @pl.when(c == 0)
        def _():
            copy(3, (*y_nbr, c), (*x_nbr, c)).start()

        passed[0].start()
        passed[1].start()
        fill_tables(n_chunks // 2, n_chunks)
        copy(3, (*diag, c), (x, y, c)).wait_recv()
        passed[2].start()
        copy(0, (x, y, 1 - c), (x, y, c)).wait_recv()
        for j, chip in enumerate(chips):
            copy(4 + j, (*chip, 1 - c), (x, y, c)).wait_recv()
        for cp in first + passed:
            cp.wait_send()
        copy(3, (*x_nbr, c), (*y_nbr, c)).wait_send()
        mine.wait()

    any_spec = pl.BlockSpec(memory_space=pl.ANY)
    vmem = pl.BlockSpec(memory_space=pltpu.VMEM)
    table = jax.ShapeDtypeStruct((s_len, 128), F32)
    return pl.pallas_call(
        body, name=name, in_specs=[vmem, any_spec], out_specs=[vmem, vmem, any_spec],
        out_shape=[table, table, _exchange_out_shape("gather", shard)],
        scratch_shapes=[pltpu.SemaphoreType.DMA((N_DEV - 1,)), pltpu.SemaphoreType.DMA((N_DEV - 1,)), pltpu.SemaphoreType.DMA],
        compiler_params=pltpu.CompilerParams(has_side_effects=True, vmem_limit_bytes=V7X_VMEM_LIMIT_BYTES),
    )(inv_freq, shard)


def _call(body, *, name, grid, in_specs, out_specs, out_shape, args, scratch_shapes=(), semantics, exchange=()):
    if not exchange:
        outs = pl.pallas_call(body, name=name, grid=grid, in_specs=in_specs, out_specs=out_specs, out_shape=out_shape,
                              scratch_shapes=list(scratch_shapes), compiler_params=_params(*semantics))(*args)
        return outs, []
    kinds = [k for k, _ in exchange]
    n_in, n_out, n_x, n_scr = len(in_specs), len(out_specs), len(exchange), len(scratch_shapes)

    def wrapped(*refs):
        ins, refs = refs[:n_in], refs[n_in:]
        srcs, refs = refs[:n_x], refs[n_x:]
        outs, refs = refs[:n_out], refs[n_out:]
        dsts, refs = refs[:n_x], refs[n_x:]
        scratch, sems = refs[:n_scr], refs[n_scr:]
        ids = [pl.program_id(a) for a in range(len(grid))]
        first = functools.reduce(jnp.logical_and, [i == 0 for i in ids])
        last = functools.reduce(jnp.logical_and, [i == g - 1 for i, g in zip(ids, grid)])

        @pl.when(first)
        def _():
            _exchange_start(kinds, srcs, dsts, sems)

        body(*ins, *outs, *scratch)

        @pl.when(last)
        def _():
            _exchange_finish(kinds, srcs, dsts, sems)

    any_spec = pl.BlockSpec(memory_space=pl.ANY)
    outs = pl.pallas_call(
        wrapped, name=name, grid=grid,
        in_specs=list(in_specs) + [any_spec] * n_x, out_specs=list(out_specs) + [any_spec] * n_x,
        out_shape=list(out_shape) + [_exchange_out_shape(k, a) for k, a in exchange],
        scratch_shapes=list(scratch_shapes) + _exchange_sems(n_x),
        compiler_params=pltpu.CompilerParams(dimension_semantics=("arbitrary",) * len(grid),
                                             vmem_limit_bytes=V7X_VMEM_LIMIT_BYTES, has_side_effects=True),
    )(*args, *[a for _, a in exchange])
    return outs[:n_out], outs[n_out:]


def _adamw_math(w, g, m, v):
    m = ADAM_B1 * m + (1.0 - ADAM_B1) * g
    v = ADAM_B2 * v + (1.0 - ADAM_B2) * (g * g)
    m_hat = m / (1.0 - ADAM_B1 ** ADAM_STEP)
    v_hat = v / (1.0 - ADAM_B2 ** ADAM_STEP)
    return -ADAM_LR * (m_hat / (jnp.sqrt(v_hat) + ADAM_EPS) + ADAM_WD * w), m, v


def _adamw(parts, w, m, v, name):
    rows, cols = w.shape
    tm = 256 if rows % 256 == 0 and rows > 256 else rows

    def body(p_ref, w_ref, m_ref, v_ref, g_ref, d_ref, nm_ref, nv_ref):
        g = p_ref[0].astype(F32)
        for j in range(1, N_DEV):
            g = g + p_ref[j].astype(F32)
        delta, nm, nv = _adamw_math(w_ref[...], g, m_ref[...], v_ref[...])
        g_ref[...] = g
        d_ref[...] = delta
        nm_ref[...] = nm
        nv_ref[...] = nv

    shard = jax.ShapeDtypeStruct((rows, cols), F32)
    return pl.pallas_call(
        body, name=name, grid=(rows // tm,),
        in_specs=[pl.BlockSpec((N_DEV, tm, cols), lambda i: (0, i, 0))] + [_rows(tm, cols)] * 3,
        out_specs=[_rows(tm, cols)] * 4,
        out_shape=[shard] * 4,
        compiler_params=_params("parallel"),
    )(parts, w, m, v)


_SMALL = ("mix_norm_g", "sgu_w", "sgu_b", "sgu_norm_g", "out_norm_a", "out_norm_b", "ffn_norm_g", "ple_norm_g", "final_norm_g")
_BIG = ("w_in", "w_out", "w_gate", "w_up", "w_down", "w_ple_gate", "w_ple_proj")
_COLUMN_SHARDED = ("w_in", "w_gate", "w_up", "w_ple_proj")
_ORDER = ("mix_norm_g", "w_in", "sgu_w", "sgu_b", "sgu_norm_g", "out_norm_a", "out_norm_b", "w_out", "ffn_norm_g",
          "w_gate", "w_up", "w_down", "ple_norm_g", "w_ple_gate", "w_ple_proj", "final_norm_g")


def _pack_small(values, names=_SMALL):
    flat = jnp.concatenate([values[n].reshape(-1).astype(F32) for n in names])
    pad = (-flat.shape[0]) % (8 * 128)
    return jnp.pad(flat, (0, pad)).reshape(-1, 128)


def _unpack_small(packed, like):
    flat = packed.reshape(-1)
    out, at = {}, 0
    for n in _SMALL:
        size = like[n].size
        out[n] = flat[at:at + size].reshape(like[n].shape)
        at += size
    return out


def _own_orientation(name, value):
    return value[0].T if name in _COLUMN_SHARDED else value[0]


def _reference_orientation(name, value):
    return (value.T if name in _COLUMN_SHARDED else value)[None]


def _full_from_gathered(gathered):
    return gathered.reshape(N_DEV * gathered.shape[1], gathered.shape[2])


def _sliced_for_devices(grad):
    return grad.reshape(N_DEV, grad.shape[0] // N_DEV, grad.shape[1])


def _rope_inv_freq():
    half = HEAD_DIM // 2
    inv = ROPE_THETA ** (-jnp.arange(half, dtype=F32) / half)
    return jnp.tile(inv, 128 // half)[None, :]


def _forward_backward(x, p, target, small, shards):
    def gather(*names):
        return [("gather", shards[n]) for n in names]

    def scatter(**grads):
        return [("scatter", _sliced_for_devices(g)) for g in grads.values()]

    full, parts = {}, {}
    s_len = x.shape[0]
    cos, sin, got = _gather_two_level_with_rope_tables(shards["w_in"], _rope_inv_freq(), s_len, "gather_w_in")
    full["w_in"] = _full_from_gathered(got)

    g_mix, g_ffn, g_ple = small["mix_norm_g"], small["ffn_norm_g"], small["ple_norm_g"]
    g_fin = small["final_norm_g"].reshape(1, D_MODEL)
    sw, gs, ga, gb = small["sgu_w"], small["sgu_norm_g"], small["out_norm_a"], small["out_norm_b"]
    b2 = jnp.repeat(small["sgu_b"].T, HEAD_DIM, axis=1)
    head_sum = (jnp.arange(WIDTH_B)[:, None] // HEAD_DIM == jnp.arange(128)[None, :]).astype(BF16)
    n_br = len(DILATIONS)

    def arrived(names, got):
        for n, g in zip(names, got):
            full[n] = _full_from_gathered(g)

    (ua, hn1, *qkv), got = _inproj(x, g_mix, full["w_in"], cos, sin, exchange=gather("w_gate"))
    arrived(("w_gate",), got)
    ya_n = _sgu_fwd(ua, sw, b2, gs, ga)
    half = shards["w_up"].shape[0] // 2
    riders = [[("gather", shards["w_up"][:half])], [("gather", shards["w_up"][half:])], gather("w_out")]
    branch, got = [], []
    for i, d in enumerate(DILATIONS):
        o_l, g = _attn_fwd(qkv[i], d, exchange=riders[i])
        branch.append(o_l)
        got += g
    arrived(("w_up", "w_out"), [jnp.concatenate(got[:2], axis=1), got[2]])
    (y, yb, *lse), _ = _combine([o for o, _ in branch], [l for _, l in branch], ya_n, gb, head_sum.T)
    last_wave = ("w_down", "w_ple_gate", "w_ple_proj")
    (h1, a, b, act, hn2), got = _ffn_up(y, full["w_out"], x, g_ffn, full["w_gate"], full["w_up"], exchange=gather(*last_wave))
    arrived(last_wave, got)
    h2, h3, gate, pp, hn3 = _ffn_down_ple(act, full["w_down"], h1, g_ple, full["w_ple_gate"], p, full["w_ple_proj"])

    dh2, loss, d_fin, d_ple, g_ple_gate, g_ple_proj = _loss_ple_bwd(
        h3, target, g_fin, gate, pp, h2, g_ple, full["w_ple_gate"], hn3, p)
    g_down, _ = _mm_tn(act, dh2, "dw_down")
    (da, db), (parts["w_down"],) = _ffn_down_bwd(dh2, full["w_down"], a, b, exchange=scatter(w_down=g_down))
    g_gate, _ = _mm_tn(da, hn2, "dw_gate")
    g_up, _ = _mm_tn(db, hn2, "dw_up")
    (dh1, d_ffn, g_out), (parts["w_gate"], parts["w_ple_gate"], parts["w_ple_proj"]) = _mm_norm_bwd(
        [(da, full["w_gate"]), (db, full["w_up"])], h1, g_ffn, dh2, "ffn_up_bwd",
        exchange=scatter(w_gate=g_gate, w_ple_gate=g_ple_gate, w_ple_proj=g_ple_proj), dw_lhs=y)
    (dya_n, d_gb, *do_dd), _ = _outproj_bwd(dh1, full["w_out"], yb, gb, head_sum)
    grads_b = []
    for i, d in enumerate(DILATIONS):
        g3, got = _attn_bwd(qkv[i], do_dd[i], lse[i], do_dd[n_br + i], d,
                            exchange=scatter(w_up=g_up, w_out=g_out) if i == 0 else ())
        grads_b.append(g3)
        if i == 0:
            parts["w_up"], parts["w_out"] = got
    dua, d_sw, d_b2, d_gs, d_ga = _sgu_bwd(ua, sw, b2, gs, ga, dya_n)
    early = {
        "sgu_w": d_sw, "sgu_b": d_b2.reshape(CHUNK, 4, HEAD_DIM).sum(axis=-1).T, "sgu_norm_g": d_gs, "out_norm_a": d_ga,
        "out_norm_b": d_gb, "ffn_norm_g": d_ffn, "ple_norm_g": d_ple, "final_norm_g": d_fin,
    }
    dproj, g_in, (early_parts,) = _dproj(
        dua, [g[0] for g in grads_b], [g[1] for g in grads_b], [g[2] for g in grads_b], cos, sin, hn1,
        exchange=[("gather", _pack_small(early, _SMALL[1:]))])
    (dx, d_mix), (parts["w_in"],) = _mm_norm_bwd(
        [(dproj, full["w_in"])], x, g_mix, dh1, "inproj_bwd", exchange=scatter(w_in=g_in))
    late = jnp.concatenate([_pack_small({"mix_norm_g": d_mix}, _SMALL[:1]), jnp.broadcast_to(loss, (8, 128))])
    (late_parts,) = _exchange_only([("gather", late)], "gather_mix_norm_grad_and_loss")
    total_loss = jnp.sum(late_parts[:, 8, 0])
    return total_loss, dx, parts, jnp.concatenate([late_parts[:, :8], early_parts], axis=1)


def kernel(x, p, mix_norm_g, w_in, sgu_w, sgu_b, sgu_norm_g, out_norm_a, out_norm_b, w_out, ffn_norm_g, w_gate, w_up, w_down, ple_norm_g, w_ple_gate, w_ple_proj, final_norm_g, loss_target, m_mix_norm_g, m_w_in, m_sgu_w, m_sgu_b, m_sgu_norm_g, m_out_norm_a, m_out_norm_b, m_w_out, m_ffn_norm_g, m_w_gate, m_w_up, m_w_down, m_ple_norm_g, m_w_ple_gate, m_w_ple_proj, m_final_norm_g, v_mix_norm_g, v_w_in, v_sgu_w, v_sgu_b, v_sgu_norm_g, v_out_norm_a, v_out_norm_b, v_w_out, v_ffn_norm_g, v_w_gate, v_w_up, v_w_down, v_ple_norm_g, v_w_ple_gate, v_w_ple_proj, v_final_norm_g):
    given = dict(locals())
    weights = {n: given[n] for n in _ORDER}
    moments_m = {n: given["m_" + n] for n in _ORDER}
    moments_v = {n: given["v_" + n] for n in _ORDER}

    shards = {n: _own_orientation(n, weights[n]).astype(BF16) for n in _BIG}
    small = {n: (weights[n][0] if n in ("sgu_w", "sgu_b") else weights[n]) for n in _SMALL}

    loss, dx, parts, small_parts = _forward_backward(x[0], p[0, 0], loss_target[0], small, shards)

    small_like = {n: weights[n] for n in _SMALL}
    grads, deltas, new_m, new_v = {}, {}, {}, {}
    for n in _BIG:
        outs = _adamw(parts[n], _own_orientation(n, weights[n]), _own_orientation(n, moments_m[n]),
                      _own_orientation(n, moments_v[n]), "adamw_" + n)
        grads[n], deltas[n], new_m[n], new_v[n] = [_reference_orientation(n, o) for o in outs]
    g, d, nm, nv = _adamw(small_parts, _pack_small(small_like), _pack_small({n: moments_m[n] for n in _SMALL}),
                          _pack_small({n: moments_v[n] for n in _SMALL}), "adamw_small")
    for out, packed in ((grads, g), (deltas, d), (new_m, nm), (new_v, nv)):
        out.update(_unpack_small(packed, small_like))

    return (loss, dx[None], *[grads[n] for n in _ORDER], *[deltas[n] for n in _ORDER],
            *[new_m[n] for n in _ORDER], *[new_v[n] for n in _ORDER])
```

```python
import functools

import jax
import jax.numpy as jnp
from jax import lax
from jax.experimental import pallas as pl
from jax.experimental.pallas import tpu as pltpu

F32 = jnp.float32
BF16 = jnp.bfloat16

D_MODEL = 1024
WIDTH_A = 256
WIDTH_B = 768
D_FF = 2816
IN_COLS = 2 * WIDTH_A + 3 * WIDTH_B
PLE_DIM = 256
HEAD_DIM = 64
N_PAIRS = WIDTH_B // 128
CHUNK = 128
N_BACK = 128
DILATIONS = (1, 4, 16)
ROPE_THETA = 10000.0
EPS = 1e-6
N_DEV = 8

ADAM_LR = 0.001
ADAM_B1 = 0.9
ADAM_B2 = 0.999
ADAM_EPS = 1e-08
ADAM_WD = 0.01
ADAM_STEP = 10

V7X_VMEM_LIMIT_BYTES = 56 * 1024 * 1024
ROW_TILE = 512
MESH = pl.DeviceIdType.MESH
NEG = -1e30

_NT = (((1,), (1,)), ((), ()))
_TN = (((0,), (0,)), ((), ()))


def _params(*semantics):
    return pltpu.CompilerParams(dimension_semantics=semantics, vmem_limit_bytes=V7X_VMEM_LIMIT_BYTES)


def _rows(tm, width):
    return pl.BlockSpec((tm, width), lambda i: (i, 0))


def _whole(shape):
    return pl.BlockSpec(shape, lambda *_: (0,) * len(shape))


def _resident(shape):
    return pl.BlockSpec(shape, lambda *_: (0,) * len(shape), pipeline_mode=pl.Buffered(1))


def _gelu(x):
    t = jnp.tanh(0.7978845608028654 * (x + 0.044715 * (x * x * x)))
    return 0.5 * x * (1.0 + t)


def _gelu_grad(x):
    t = jnp.tanh(0.7978845608028654 * (x + 0.044715 * (x * x * x)))
    return 0.5 * (1.0 + t) + 0.5 * x * (1.0 - t * t) * (0.7978845608028654 * (1.0 + 3.0 * 0.044715 * (x * x)))


def _rstd(x):
    return lax.rsqrt(jnp.mean(x * x, axis=-1, keepdims=True) + EPS)


def _norm_bwd(dn, h, g, r=None):
    r = _rstd(h) if r is None else r
    n = h * r
    t = dn * g
    return r * (t - n * jnp.mean(t * n, axis=-1, keepdims=True)), dn * n


def _swap_halves(x, first_half):
    return jnp.where(first_half, pltpu.roll(x, 96, 1), pltpu.roll(x, 32, 1))


def _sub_spec(d, n_cb, tm):
    return pl.BlockSpec((d, n_cb, tm // d, 128), lambda i: (0, 0, i, 0))


def _sub_shape(s_len, d, n_cb, dtype):
    return jax.ShapeDtypeStruct((d, n_cb, s_len // d, 128), dtype)


def _to_sub(stage_ref, cb_src, out_ref, cb_dst, d, tm):
    slab = stage_ref.at[cb_src]
    for r in range(d):
        out_ref[r, cb_dst] = slab[pl.ds(r, tm // d, stride=d), :].astype(out_ref.dtype)


def _from_sub(in_ref, cb_src, stage_ref, cb_dst, d, tm):
    slab = stage_ref.at[cb_dst]
    for r in range(d):
        slab[pl.ds(r, tm // d, stride=d), :] = in_ref[r, cb_src].astype(F32)


def _inproj(x, g, w, cos, sin, exchange=()):
    s_len = x.shape[0]
    tm = ROW_TILE
    n_cb = 3 * N_PAIRS

    def body(x_ref, g_ref, w_ref, cos_ref, sin_ref, ua_ref, hn_ref, *rest):
        sub_refs, stage = rest[:-1], rest[-1]
        xf = x_ref[...]
        hn = (xf * _rstd(xf) * g_ref[...]).astype(BF16)
        hn_ref[...] = hn
        c = cos_ref[...]
        s = sin_ref[...]
        first_half = (lax.broadcasted_iota(jnp.int32, (tm, 128), 1) % HEAD_DIM) < HEAD_DIM // 2
        for col in range(0, IN_COLS, 512):
            width = min(512, IN_COLS - col)
            acc = lax.dot_general(hn, w_ref[col:col + width, :], _NT, preferred_element_type=F32)
            if col < 2 * WIDTH_A:
                ua_ref[:, col:col + width] = acc
                continue
            for part in range(width // 128):
                cb = (col - 2 * WIDTH_A) // 128 + part
                t = acc[:, part * 128:(part + 1) * 128]
                if cb < 2 * N_PAIRS:
                    t = (t * c + _swap_halves(t, first_half) * s) * (0.125 if cb < N_PAIRS else 1.0)
                stage[cb] = t
                sub_refs[0][0, cb] = t.astype(BF16)
        for cb in range(n_cb):
            for d, out_ref in zip(DILATIONS[1:], sub_refs[1:]):
                _to_sub(stage, cb, out_ref, cb, d, tm)

    return _call(
        body, name="inproj", grid=(s_len // tm,),
        in_specs=[_rows(tm, D_MODEL), _whole((1, D_MODEL)), _resident((IN_COLS, D_MODEL)), _rows(tm, 128), _rows(tm, 128)],
        out_specs=[_rows(tm, 2 * WIDTH_A), _rows(tm, D_MODEL)] + [_sub_spec(d, n_cb, tm) for d in DILATIONS],
        out_shape=[jax.ShapeDtypeStruct((s_len, 2 * WIDTH_A), F32), jax.ShapeDtypeStruct((s_len, D_MODEL), BF16)]
        + [_sub_shape(s_len, d, n_cb, BF16) for d in DILATIONS],
        scratch_shapes=[pltpu.VMEM((n_cb, tm, 128), F32)],
        semantics=("parallel",), args=(x, g, w, cos, sin), exchange=exchange)


def _sgu_mix_weights(sw_ref):
    keep = lax.broadcasted_iota(jnp.int32, (CHUNK, CHUNK), 0) >= lax.broadcasted_iota(jnp.int32, (CHUNK, CHUNK), 1)
    return [jnp.where(keep, sw_ref[h], 0.0).astype(BF16) for h in range(4)], keep


def _sgu_core(ua_ref, gs_ref):
    u = ua_ref[:, :WIDTH_A]
    va = ua_ref[:, WIDTH_A:]
    vg = _gelu(va)
    xc = vg - jnp.mean(vg, axis=-1, keepdims=True)
    rstd = lax.rsqrt(jnp.mean(xc * xc, axis=-1, keepdims=True) + EPS)
    xhat = xc * rstd
    return u, va, _gelu(u), xhat, rstd, xhat * gs_ref[...]


def _sgu_mix(wm, vnb, half, head_a):
    va, vb = _both_heads(vnb[:, half * 128:(half + 1) * 128], head_a)
    return (jnp.dot(wm[2 * half], va, preferred_element_type=F32)
            + jnp.dot(wm[2 * half + 1], vb, preferred_element_type=F32))


def _sgu_fwd(ua, sw, b2, gs, ga):
    s_len = ua.shape[0]
    tm = ROW_TILE

    def body(ua_ref, sw_ref, b2_ref, gs_ref, ga_ref, out_ref):
        _, _, ug, _, _, vn = _sgu_core(ua_ref, gs_ref)
        wm, _ = _sgu_mix_weights(sw_ref)
        head_a = lax.broadcasted_iota(jnp.int32, (CHUNK, 128), 1) < HEAD_DIM
        for c in range(tm // CHUNK):
            rows = slice(c * CHUNK, (c + 1) * CHUNK)
            vnb = vn[rows].astype(BF16)
            mixed = b2_ref[...] + jnp.concatenate([_sgu_mix(wm, vnb, half, head_a) for half in range(2)], axis=1)
            ya = ug[rows] * mixed
            out_ref[rows, :] = (ya * _rstd(ya) * ga_ref[...]).astype(BF16)

    return pl.pallas_call(
        body, name="sgu_fwd", grid=(s_len // tm,),
        in_specs=[_rows(tm, 2 * WIDTH_A), _whole((4, CHUNK, CHUNK)), _whole((CHUNK, WIDTH_A)), _whole((1, WIDTH_A)), _whole((1, WIDTH_A))],
        out_specs=_rows(tm, WIDTH_A),
        out_shape=jax.ShapeDtypeStruct((s_len, WIDTH_A), BF16),
        compiler_params=_params("parallel"),
    )(ua, sw, b2, gs, ga)


STEP_POSITIONS = 2 * ROW_TILE


def _attn_geometry(sd):
    tile = min(STEP_POSITIONS, sd)
    return tile, tile // CHUNK, sd // tile, STEP_POSITIONS // tile


PAIRS_PER_STEP = 6


def _attn_spec(cb0, rows, row_index, res):
    return pl.BlockSpec((res, PAIRS_PER_STEP, rows, 128), lambda r, g, n: (r, cb0 // PAIRS_PER_STEP + g, row_index(n), 0))


assert PAIRS_PER_STEP == N_PAIRS
assert DILATIONS[0] == 1


def _stats_spec(rows, row_index, res):
    return pl.BlockSpec((res, None, rows, 128), lambda r, g, n: (r, 0, row_index(n), 0))


def _stats_shape(sd, d):
    return jax.ShapeDtypeStruct((d, 1, sd, 128), F32)


def _both_heads(x, head_a):
    zero = jnp.zeros_like(x)
    return [jnp.where(head_a, x, zero), jnp.where(head_a, zero, x)]


def _attn_fwd(qkv, d, exchange=()):
    sd = qkv.shape[2]
    tile, nb, n_tiles, res = _attn_geometry(sd)

    def prev(n):
        return jnp.maximum(n * nb - 1, 0)

    def body(q_ref, k_ref, kp_ref, v_ref, vp_ref, o_ref, l_ref):
        for rr in range(res):
            for hp in range(PAIRS_PER_STEP):
                one_pair(hp, q_ref.at[rr, hp], k_ref.at[rr, hp], kp_ref.at[rr, hp], v_ref.at[rr, hp], vp_ref.at[rr, hp],
                         o_ref.at[rr, hp], l_ref.at[rr])

    def one_pair(hp, q_ref, k_ref, kp_ref, v_ref, vp_ref, o_ref, l_ref):
        n = pl.program_id(2)
        lane = lax.broadcasted_iota(jnp.int32, (CHUNK, 128), 1)
        head_a = lane < HEAD_DIM
        qi = lax.broadcasted_iota(jnp.int32, (2 * CHUNK, 2 * CHUNK), 0) % CHUNK
        kc = lax.broadcasted_iota(jnp.int32, (2 * CHUNK, 2 * CHUNK), 1)
        band = (kc >= qi) & (kc <= qi + N_BACK)
        for j in range(nb):
            rows = slice(j * CHUNK, (j + 1) * CHUNK)
            if j == 0:
                kcat = jnp.concatenate([kp_ref[...], k_ref[rows, :]], axis=0)
                vcat = jnp.concatenate([vp_ref[...], v_ref[rows, :]], axis=0)
                valid = band & jnp.logical_or(n > 0, kc >= CHUNK)
            else:
                kcat = k_ref[(j - 1) * CHUNK:(j + 1) * CHUNK, :]
                vcat = v_ref[(j - 1) * CHUNK:(j + 1) * CHUNK, :]
                valid = band
            q2 = jnp.concatenate(_both_heads(q_ref[rows, :], head_a), axis=0)
            s = lax.dot_general(q2, kcat, _NT, preferred_element_type=F32)
            s = jnp.where(valid, s, NEG)
            m = jnp.max(s, axis=-1, keepdims=True)
            p = jnp.exp(s - m)
            l = jnp.sum(p, axis=-1, keepdims=True)
            o2 = jnp.dot(p.astype(BF16), vcat, preferred_element_type=F32) / l
            lse2 = m + jnp.log(l)
            o_ref[rows, :] = jnp.where(head_a, o2[:CHUNK], o2[CHUNK:]).astype(BF16)
            others = l_ref[rows, :] if hp > 0 else jnp.zeros((CHUNK, 128), F32)
            l_ref[rows, :] = jnp.where(lane == 2 * hp, lse2[:CHUNK], jnp.where(lane == 2 * hp + 1, lse2[CHUNK:], others))

    same = lambda n: n
    return _call(
        body, name=f"attn_fwd_d{d}", grid=(d // res, N_PAIRS // PAIRS_PER_STEP, n_tiles),
        in_specs=[_attn_spec(0, tile, same, res), _attn_spec(N_PAIRS, tile, same, res), _attn_spec(N_PAIRS, CHUNK, prev, res),
                  _attn_spec(2 * N_PAIRS, tile, same, res), _attn_spec(2 * N_PAIRS, CHUNK, prev, res)],
        out_specs=[_attn_spec(0, tile, same, res), _stats_spec(tile, same, res)],
        out_shape=[jax.ShapeDtypeStruct((d, N_PAIRS, sd, 128), BF16), _stats_shape(sd, d)],
        semantics=("parallel", "parallel", "parallel"), args=(qkv, qkv, qkv, qkv, qkv), exchange=exchange)


def _combine(outs, lses, ya_n, gb, head_spread, exchange=()):
    s_len = ya_n.shape[0]
    tm = ROW_TILE
    n_br = len(DILATIONS)

    def body(*refs):
        o_refs, l_refs = refs[:n_br], refs[n_br:2 * n_br]
        ya_ref, gb_ref, spread_ref, y_ref, yb_ref = refs[2 * n_br:2 * n_br + 5]
        lse_refs = refs[2 * n_br + 5:3 * n_br + 5]
        o_nat, l_nat, lse_nat, w_wide = refs[3 * n_br + 5:]
        for i, d in enumerate(DILATIONS):
            _from_sub(l_refs[i], 0, l_nat, i, d, tm)
        ls = [l_nat[i] for i in range(n_br)]
        top = jnp.maximum(jnp.maximum(ls[0], ls[1]), ls[2])
        ws = [jnp.exp(l - top) for l in ls]
        den = ws[0] + ws[1] + ws[2]
        inv = 1.0 / den
        for i in range(n_br):
            w = ws[i] * inv
            hi = w.astype(BF16)
            lo = (w - hi.astype(F32)).astype(BF16)
            w_wide[i] = (jnp.dot(hi, spread_ref[...], preferred_element_type=F32)
                         + jnp.dot(lo, spread_ref[...], preferred_element_type=F32))
        lse_nat[0] = top + jnp.log(den)
        for d, lse_ref in zip(DILATIONS, lse_refs):
            _to_sub(lse_nat, 0, lse_ref, 0, d, tm)
        sumsq = jnp.zeros((tm, 1), F32)
        for cb in range(N_PAIRS):
            cols = slice(cb * 128, (cb + 1) * 128)
            yb = w_wide[0, :, cols] * o_refs[0][0, cb].astype(F32)
            for i, d in enumerate(DILATIONS[1:], start=1):
                _from_sub(o_refs[i], cb, o_nat, i, d, tm)
                yb = yb + w_wide[i, :, cols] * o_nat[i]
            yb_ref[:, cb * 128:(cb + 1) * 128] = yb
            sumsq = sumsq + jnp.sum(yb * yb, axis=-1, keepdims=True)
        r = lax.rsqrt(sumsq / WIDTH_B + EPS)
        y_ref[:, :WIDTH_A] = ya_ref[...]
        y_ref[:, WIDTH_A:] = (yb_ref[...] * r * gb_ref[...]).astype(BF16)

    stats = [_sub_spec(d, 1, tm) for d in DILATIONS]
    return _call(
        body, name="attn_combine", grid=(s_len // tm,),
        in_specs=[_sub_spec(d, N_PAIRS, tm) for d in DILATIONS] + stats
        + [_rows(tm, WIDTH_A), _whole((1, WIDTH_B)), _whole((128, WIDTH_B))],
        out_specs=[_rows(tm, D_MODEL), _rows(tm, WIDTH_B)] + stats,
        out_shape=[jax.ShapeDtypeStruct((s_len, D_MODEL), BF16), jax.ShapeDtypeStruct((s_len, WIDTH_B), F32)]
        + [_sub_shape(s_len, d, 1, F32) for d in DILATIONS],
        scratch_shapes=[pltpu.VMEM((n_br, tm, 128), F32), pltpu.VMEM((n_br, tm, 128), F32), pltpu.VMEM((1, tm, 128), F32),
                        pltpu.VMEM((n_br, tm, WIDTH_B), F32)],
        semantics=("parallel",), args=(*outs, *lses, ya_n, gb, head_spread), exchange=exchange)


def _ffn_up(y, wout, x, g, wg, wu, exchange=()):
    s_len = x.shape[0]
    tm = ROW_TILE

    def body(y_ref, wo_ref, x_ref, g_ref, wg_ref, wu_ref, h_ref, a_ref, b_ref, act_ref, hn_ref):
        hf = x_ref[...] + jnp.dot(y_ref[...], wo_ref[...], preferred_element_type=F32)
        h_ref[...] = hf
        hn = (hf * _rstd(hf) * g_ref[...]).astype(BF16)
        hn_ref[...] = hn
        for col in range(0, D_FF, 256):
            cols = slice(col, col + 256)
            a = lax.dot_general(hn, wg_ref[cols, :], _NT, preferred_element_type=F32)
            b = lax.dot_general(hn, wu_ref[cols, :], _NT, preferred_element_type=F32)
            a_ref[:, cols] = a.astype(BF16)
            b_ref[:, cols] = b.astype(BF16)
            act_ref[:, cols] = (a * jax.nn.sigmoid(a) * b).astype(BF16)

    wide = jax.ShapeDtypeStruct((s_len, D_FF), BF16)
    return _call(
        body, name="ffn_up", grid=(s_len // tm,),
        in_specs=[_rows(tm, D_MODEL), _resident((D_MODEL, D_MODEL)), _rows(tm, D_MODEL), _whole((1, D_MODEL)),
                  _resident((D_FF, D_MODEL)), _resident((D_FF, D_MODEL))],
        out_specs=[_rows(tm, D_MODEL), _rows(tm, D_FF), _rows(tm, D_FF), _rows(tm, D_FF), _rows(tm, D_MODEL)],
        out_shape=[jax.ShapeDtypeStruct((s_len, D_MODEL), F32), wide, wide, wide, jax.ShapeDtypeStruct((s_len, D_MODEL), BF16)],
        semantics=("parallel",), args=(y, wout, x, g, wg, wu), exchange=exchange)


def _ffn_down_ple(act, wd, h1, g, wpg, p, wpp):
    s_len = h1.shape[0]
    tm = ROW_TILE

    def body(act_ref, wd_ref, h1_ref, g_ref, wpg_ref, p_ref, wpp_ref, h2_ref, h3_ref, gate_ref, pp_ref, hn_ref):
        hf = h1_ref[...] + jnp.dot(act_ref[...], wd_ref[...], preferred_element_type=F32)
        h2_ref[...] = hf
        hn = (hf * _rstd(hf) * g_ref[...]).astype(BF16)
        hn_ref[...] = hn
        gate = jax.nn.sigmoid(jnp.dot(hn, wpg_ref[...], preferred_element_type=F32))
        pp = lax.dot_general(p_ref[...].astype(BF16), wpp_ref[...], _NT, preferred_element_type=F32)
        h3_ref[...] = hf + gate * pp
        gate_ref[...] = gate.astype(BF16)
        pp_ref[...] = pp.astype(BF16)

    full = jax.ShapeDtypeStruct((s_len, D_MODEL), F32)
    half = jax.ShapeDtypeStruct((s_len, D_MODEL), BF16)
    return pl.pallas_call(
        body, name="ffn_down_ple", grid=(s_len // tm,),
        in_specs=[_rows(tm, D_FF), _resident((D_FF, D_MODEL)), _rows(tm, D_MODEL), _whole((1, D_MODEL)),
                  _resident((D_MODEL, D_MODEL)), _rows(tm, PLE_DIM), _resident((D_MODEL, PLE_DIM))],
        out_specs=[_rows(tm, D_MODEL)] * 5,
        out_shape=[full, full, half, half, half],
        compiler_params=_params("parallel"),
    )(act, wd, h1, g, wpg, p, wpp)


def _loss_ple_bwd(h3, target, gf, gate, pp, h2, g_ple, wpg, hn3, p):
    s_len = h3.shape[0]
    tm = ROW_TILE
    n_steps = s_len // tm

    def body(h_ref, t_ref, g_ref, gate_ref, pp_ref, h2_ref, gp_ref, w_ref, hn_ref, p_ref,
             dh2_ref, loss_ref, dg_ref, dgp_ref, dwg_ref, dwp_ref, acc_g, acc_p):
        step = pl.program_id(0)

        @pl.when(step == 0)
        def _():
            loss_ref[...] = jnp.zeros_like(loss_ref)
            dg_ref[...] = jnp.zeros_like(dg_ref)
            dgp_ref[...] = jnp.zeros_like(dgp_ref)
            acc_g[...] = jnp.zeros_like(acc_g)
            acc_p[...] = jnp.zeros_like(acc_p)

        hf = h_ref[...]
        gfv = g_ref[...]
        r = _rstd(hf)
        err = hf * r * gfv - t_ref[...]
        loss_ref[...] += 0.5 * jnp.sum(jnp.sum(err * err, axis=-1, keepdims=True), axis=0, keepdims=True) / D_MODEL
        dh, dg_rows = _norm_bwd(err / D_MODEL, hf, gfv, r)
        dg_ref[...] += jnp.sum(dg_rows, axis=0, keepdims=True)
        gate = gate_ref[...].astype(F32)
        dz = (dh * pp_ref[...].astype(F32) * gate * (1.0 - gate)).astype(BF16)
        dpp = (dh * gate).astype(BF16)
        dn = lax.dot_general(dz, w_ref[...], _NT, preferred_element_type=F32)
        dh2, dgp_rows = _norm_bwd(dn, h2_ref[...], gp_ref[...])
        dh2 = dh + dh2
        dh2_ref[...] = dh2
        dgp_ref[...] += jnp.sum(dgp_rows, axis=0, keepdims=True)
        acc_g[...] += lax.dot_general(hn_ref[...], dz, _TN, preferred_element_type=F32)
        acc_p[...] += lax.dot_general(dpp, p_ref[...].astype(BF16), _TN, preferred_element_type=F32)

        @pl.when(step == n_steps - 1)
        def _():
            dwg_ref[...] = acc_g[...].astype(BF16)
            dwp_ref[...] = acc_p[...].astype(BF16)

    gain = jax.ShapeDtypeStruct((1, D_MODEL), F32)
    return pl.pallas_call(
        body, name="loss_ple_bwd", grid=(n_steps,),
        in_specs=[_rows(tm, D_MODEL), _rows(tm, D_MODEL), _whole((1, D_MODEL)), _rows(tm, D_MODEL), _rows(tm, D_MODEL),
                  _rows(tm, D_MODEL), _whole((1, D_MODEL)), _resident((D_MODEL, D_MODEL)), _rows(tm, D_MODEL),
                  _rows(tm, PLE_DIM)],
        out_specs=[_rows(tm, D_MODEL), _whole((1, 128)), _whole((1, D_MODEL)), _whole((1, D_MODEL)),
                   _whole((D_MODEL, D_MODEL)), _whole((D_MODEL, PLE_DIM))],
        out_shape=[jax.ShapeDtypeStruct((s_len, D_MODEL), F32), jax.ShapeDtypeStruct((1, 128), F32), gain, gain,
                   jax.ShapeDtypeStruct((D_MODEL, D_MODEL), BF16), jax.ShapeDtypeStruct((D_MODEL, PLE_DIM), BF16)],
        scratch_shapes=[pltpu.VMEM((D_MODEL, D_MODEL), F32), pltpu.VMEM((D_MODEL, PLE_DIM), F32)],
        compiler_params=_params("arbitrary"),
    )(h3, target, gf, gate, pp, h2, g_ple, wpg, hn3, p)


def _mm_norm_bwd(parts, h, g, dres, name, exchange=(), dw_lhs=None):
    s_len = h.shape[0]
    tm = ROW_TILE
    n_parts = len(parts)
    n_steps = s_len // tm
    has_dw = dw_lhs is not None

    def body(*refs):
        a_refs = refs[0:2 * n_parts:2]
        w_refs = refs[1:2 * n_parts:2]
        h_ref, g_ref, r_ref = refs[2 * n_parts:2 * n_parts + 3]
        rest = refs[2 * n_parts + 3:]
        step = pl.program_id(0)
        if has_dw:
            lhs_ref, o_ref, dg_ref, dw_ref, acc_ref = rest
        else:
            o_ref, dg_ref = rest

        @pl.when(step == 0)
        def _():
            dg_ref[...] = jnp.zeros_like(dg_ref)
            if has_dw:
                acc_ref[...] = jnp.zeros_like(acc_ref)

        dn = jnp.dot(a_refs[0][...], w_refs[0][...], preferred_element_type=F32)
        for a_ref, w_ref in zip(a_refs[1:], w_refs[1:]):
            dn = dn + jnp.dot(a_ref[...], w_ref[...], preferred_element_type=F32)
        dh, dg_rows = _norm_bwd(dn, h_ref[...], g_ref[...])
        out = r_ref[...] + dh
        o_ref[...] = out
        dg_ref[...] += jnp.sum(dg_rows, axis=0, keepdims=True)
        if has_dw:
            acc_ref[...] += lax.dot_general(lhs_ref[...], out.astype(BF16), _TN, preferred_element_type=F32)

            @pl.when(step == n_steps - 1)
            def _():
                dw_ref[...] = acc_ref[...].astype(BF16)

    in_specs, args = [], []
    for a, w in parts:
        in_specs += [_rows(tm, a.shape[1]), _resident(w.shape)]
        args += [a, w]
    in_specs += [_rows(tm, D_MODEL), _whole((1, D_MODEL)), _rows(tm, D_MODEL)]
    args += [h, g, dres]
    out_specs = [_rows(tm, D_MODEL), _whole((1, D_MODEL))]
    out_shape = [jax.ShapeDtypeStruct((s_len, D_MODEL), F32), jax.ShapeDtypeStruct((1, D_MODEL), F32)]
    scratch = []
    if has_dw:
        m = dw_lhs.shape[1]
        in_specs.append(_rows(tm, m))
        args.append(dw_lhs)
        out_specs.append(_whole((m, D_MODEL)))
        out_shape.append(jax.ShapeDtypeStruct((m, D_MODEL), BF16))
        scratch.append(pltpu.VMEM((m, D_MODEL), F32))
    return _call(
        body, name=name, grid=(n_steps,), in_specs=in_specs, out_specs=out_specs, out_shape=out_shape,
        scratch_shapes=scratch, semantics=("arbitrary",), args=tuple(args), exchange=exchange)


def _ffn_down_bwd(dh, wdt, a, b, exchange=()):
    s_len = dh.shape[0]
    tm = ROW_TILE

    def body(dh_ref, w_ref, a_ref, b_ref, da_ref, db_ref):
        dhb = dh_ref[...].astype(BF16)
        for col in range(0, D_FF, 512):
            cols = slice(col, min(col + 512, D_FF))
            dact = lax.dot_general(dhb, w_ref[cols, :], _NT, preferred_element_type=F32)
            av = a_ref[:, cols].astype(F32)
            bv = b_ref[:, cols].astype(F32)
            sig = jax.nn.sigmoid(av)
            t = dact * sig
            silu = av * sig
            da_ref[:, cols] = (t * bv * (1.0 + av - silu)).astype(BF16)
            db_ref[:, cols] = (dact * silu).astype(BF16)

    wide = jax.ShapeDtypeStruct((s_len, D_FF), BF16)
    return _call(
        body, name="ffn_down_bwd", grid=(s_len // tm,),
        in_specs=[_rows(tm, D_MODEL), _resident((D_FF, D_MODEL)), _rows(tm, D_FF), _rows(tm, D_FF)],
        out_specs=[_rows(tm, D_FF), _rows(tm, D_FF)],
        out_shape=[wide, wide],
        semantics=("parallel",), args=(dh, wdt, a, b), exchange=exchange)


def _outproj_bwd(dh1, woutt, yb, gb, head_sum, exchange=()):
    s_len = dh1.shape[0]
    tm = ROW_TILE
    n_br = len(DILATIONS)

    def body(dh_ref, w_ref, yb_ref, gb_ref, e_ref, dya_ref, dgb_ref, *rest):
        do_refs, dd_refs = rest[:n_br], rest[n_br:2 * n_br]
        do_nat, dd_nat = rest[2 * n_br:]

        @pl.when(pl.program_id(0) == 0)
        def _():
            dgb_ref[...] = jnp.zeros_like(dgb_ref)

        dhb = dh_ref[...].astype(BF16)
        dya_ref[...] = lax.dot_general(dhb, w_ref[:WIDTH_A, :], _NT, preferred_element_type=F32)
        dyn = lax.dot_general(dhb, w_ref[WIDTH_A:, :], _NT, preferred_element_type=F32)
        ybv = yb_ref[...]
        dyb, dg_rows = _norm_bwd(dyn, ybv, gb_ref[...])
        dgb_ref[...] += jnp.sum(dg_rows, axis=0, keepdims=True)
        prod = dyb * ybv
        hi = prod.astype(BF16)
        lo = (prod - hi.astype(F32)).astype(BF16)
        dd_nat[0] = (jnp.dot(hi, e_ref[...], preferred_element_type=F32)
                     + jnp.dot(lo, e_ref[...], preferred_element_type=F32))
        for i, d in enumerate(DILATIONS):
            _to_sub(dd_nat, 0, dd_refs[i], 0, d, tm)
        for cb in range(N_PAIRS):
            piece = dyb[:, cb * 128:(cb + 1) * 128]
            do_nat[cb] = piece
            do_refs[0][0, cb] = piece.astype(BF16)
            for i, d in enumerate(DILATIONS[1:], start=1):
                _to_sub(do_nat, cb, do_refs[i], cb, d, tm)

    return _call(
        body, name="outproj_bwd", grid=(s_len // tm,),
        in_specs=[_rows(tm, D_MODEL), _resident((D_MODEL, D_MODEL)), _rows(tm, WIDTH_B), _whole((1, WIDTH_B)), _whole((WIDTH_B, 128))],
        out_specs=[_rows(tm, WIDTH_A), _whole((1, WIDTH_B))] + [_sub_spec(d, N_PAIRS, tm) for d in DILATIONS]
        + [_sub_spec(d, 1, tm) for d in DILATIONS],
        out_shape=[jax.ShapeDtypeStruct((s_len, WIDTH_A), F32), jax.ShapeDtypeStruct((1, WIDTH_B), F32)]
        + [_sub_shape(s_len, d, N_PAIRS, BF16) for d in DILATIONS] + [_sub_shape(s_len, d, 1, F32) for d in DILATIONS],
        scratch_shapes=[pltpu.VMEM((N_PAIRS, tm, 128), F32), pltpu.VMEM((1, tm, 128), F32)],
        semantics=("arbitrary",), args=(dh1, woutt, yb, gb, head_sum), exchange=exchange)


def _attn_bwd(qkv, do, lse, dd, d, exchange=()):
    sd = qkv.shape[2]
    tile, nb, n_tiles, res = _attn_geometry(sd)
    last_block = sd // CHUNK - 1

    def nxt(n):
        return jnp.minimum((n + 1) * nb, last_block)

    def block(ref, next_ref, j):
        return ref[j * CHUNK:(j + 1) * CHUNK, :] if j < nb else next_ref[...]

    def body(q_ref, qn_ref, k_ref, v_ref, do_ref, don_ref, l_ref, ln_ref, dd_ref, ddn_ref,
             dq_ref, dk_ref, dv_ref, carry_ref):
        for rr in range(res):
            l_t = [block(l_ref.at[rr], ln_ref.at[rr], j).T for j in range(nb + 1)]
            dd_t = [block(dd_ref.at[rr], ddn_ref.at[rr], j).T for j in range(nb + 1)]
            for hp in range(PAIRS_PER_STEP):
                l_rows = [jnp.concatenate([t[2 * hp:2 * hp + 1, :], t[2 * hp + 1:2 * hp + 2, :]], axis=1) for t in l_t]
                dd_rows = [jnp.concatenate([t[2 * hp:2 * hp + 1, :], t[2 * hp + 1:2 * hp + 2, :]], axis=1) for t in dd_t]
                one_pair(q_ref.at[rr, hp], qn_ref.at[rr, hp], k_ref.at[rr, hp], v_ref.at[rr, hp], do_ref.at[rr, hp],
                         don_ref.at[rr, hp], l_rows, dd_rows, dq_ref.at[rr, hp], dk_ref.at[rr, hp], dv_ref.at[rr, hp],
                         carry_ref.at[rr, hp])

    def one_pair(q_ref, qn_ref, k_ref, v_ref, do_ref, don_ref, l_rows, dd_rows, dq_ref, dk_ref, dv_ref, carry_ref):
        n = pl.program_id(2)

        @pl.when(n == 0)
        def _():
            carry_ref[...] = jnp.zeros_like(carry_ref)

        head_a = lax.broadcasted_iota(jnp.int32, (CHUNK, 128), 1) < HEAD_DIM
        col = lax.broadcasted_iota(jnp.int32, (CHUNK, 4 * CHUNK), 1)
        qi = col % CHUNK
        ki = lax.broadcasted_iota(jnp.int32, (CHUNK, 4 * CHUNK), 0)
        is_after = col >= 2 * CHUNK
        mask = (is_after & (ki >= qi)) | (jnp.logical_not(is_after) & (qi >= ki))
        mask_last = mask & jnp.logical_or(jnp.logical_not(is_after), n < n_tiles - 1)
        dq_acc = [carry_ref[...]] + [jnp.zeros((CHUNK, 128), F32) for _ in range(nb)]

        q_st = [jnp.concatenate(_both_heads(block(q_ref, qn_ref, j), head_a), axis=0) for j in range(nb + 1)]
        do_st = [jnp.concatenate(_both_heads(block(do_ref, don_ref, j), head_a), axis=0) for j in range(nb + 1)]

        for j in range(nb):
            rows = slice(j * CHUNK, (j + 1) * CHUNK)
            kj = k_ref[rows, :]
            vj = v_ref[rows, :]
            msk = mask if j + 1 < nb else mask_last
            qs = jnp.concatenate([q_st[j], q_st[j + 1]], axis=0)
            dos = jnp.concatenate([do_st[j], do_st[j + 1]], axis=0)
            ls = jnp.concatenate([l_rows[j], l_rows[j + 1]], axis=1)
            dds = jnp.concatenate([dd_rows[j], dd_rows[j + 1]], axis=1)
            st = lax.dot_general(kj, qs, _NT, preferred_element_type=F32)
            pt = jnp.exp(jnp.where(msk, st - ls, NEG))
            dpt = lax.dot_general(vj, dos, _NT, preferred_element_type=F32)
            dst = (pt * (dpt - dds)).astype(BF16)
            dv_ref[rows, :] = jnp.dot(pt.astype(BF16), dos, preferred_element_type=F32).astype(BF16)
            dk_ref[rows, :] = jnp.dot(dst, qs, preferred_element_type=F32).astype(BF16)
            dqs = lax.dot_general(dst, kj, _TN, preferred_element_type=F32)
            dq_acc[j] = dq_acc[j] + jnp.where(head_a, dqs[:CHUNK], dqs[CHUNK:2 * CHUNK])
            dq_acc[j + 1] = dq_acc[j + 1] + jnp.where(head_a, dqs[2 * CHUNK:3 * CHUNK], dqs[3 * CHUNK:])
        for j in range(nb):
            dq_ref[j * CHUNK:(j + 1) * CHUNK, :] = dq_acc[j].astype(BF16)
        carry_ref[...] = dq_acc[nb]

    same = lambda n: n
    grad = jax.ShapeDtypeStruct((d, N_PAIRS, sd, 128), BF16)
    return _call(
        body, name=f"attn_bwd_d{d}", grid=(d // res, N_PAIRS // PAIRS_PER_STEP, n_tiles),
        in_specs=[_attn_spec(0, tile, same, res), _attn_spec(0, CHUNK, nxt, res), _attn_spec(N_PAIRS, tile, same, res),
                  _attn_spec(2 * N_PAIRS, tile, same, res), _attn_spec(0, tile, same, res), _attn_spec(0, CHUNK, nxt, res),
                  _stats_spec(tile, same, res), _stats_spec(CHUNK, nxt, res), _stats_spec(tile, same, res),
                  _stats_spec(CHUNK, nxt, res)],
        out_specs=[_attn_spec(0, tile, same, res)] * 3,
        out_shape=[grad, grad, grad],
        scratch_shapes=[pltpu.VMEM((res, PAIRS_PER_STEP, CHUNK, 128), F32)],
        semantics=("parallel", "parallel", "arbitrary"), args=(qkv, qkv, qkv, qkv, do, do, lse, lse, dd, dd), exchange=exchange)


def _sgu_bwd(ua, sw, b2, gs, ga, dya_n):
    s_len = ua.shape[0]
    tm = ROW_TILE

    def body(ua_ref, sw_ref, b2_ref, gs_ref, ga_ref, dy_ref, dua_ref, dsw_ref, db2_ref, dgs_ref, dga_ref):
        @pl.when(pl.program_id(0) == 0)
        def _():
            dsw_ref[...] = jnp.zeros_like(dsw_ref)
            db2_ref[...] = jnp.zeros_like(db2_ref)
            dgs_ref[...] = jnp.zeros_like(dgs_ref)
            dga_ref[...] = jnp.zeros_like(dga_ref)

        u, va, ug, xhat, rstd, vn = _sgu_core(ua_ref, gs_ref)
        wm, keep = _sgu_mix_weights(sw_ref)
        head_a = lax.broadcasted_iota(jnp.int32, (CHUNK, 128), 1) < HEAD_DIM
        gav = ga_ref[...]
        gsv = gs_ref[...]
        dga = jnp.zeros((1, WIDTH_A), F32)
        dgs = jnp.zeros((1, WIDTH_A), F32)
        db2 = jnp.zeros((CHUNK, WIDTH_A), F32)
        dsw = [jnp.zeros((CHUNK, CHUNK), F32) for _ in range(4)]
        for c in range(tm // CHUNK):
            rows = slice(c * CHUNK, (c + 1) * CHUNK)
            vnb = vn[rows].astype(BF16)
            mixed = b2_ref[...] + jnp.concatenate([_sgu_mix(wm, vnb, half, head_a) for half in range(2)], axis=1)
            ugc = ug[rows]
            dya, dga_rows = _norm_bwd(dy_ref[rows, :], ugc * mixed, gav)
            dga = dga + jnp.sum(dga_rows, axis=0, keepdims=True)
            dmixed = dya * ugc
            db2 = db2 + dmixed
            dmb = dmixed.astype(BF16)
            dvn_halves = []
            for half in range(2):
                lanes = slice(half * 128, (half + 1) * 128)
                dm_heads = _both_heads(dmb[:, lanes], head_a)
                dvn_half = jnp.zeros((CHUNK, 128), F32)
                for k, dmh in enumerate(dm_heads):
                    h = 2 * half + k
                    dsw[h] = dsw[h] + lax.dot_general(dmh, vnb[:, lanes], _NT, preferred_element_type=F32)
                    dvn_half = dvn_half + lax.dot_general(wm[h], dmh, _TN, preferred_element_type=F32)
                dvn_halves.append(dvn_half)
            dvn = jnp.concatenate(dvn_halves, axis=1)
            xh = xhat[rows]
            dgs = dgs + jnp.sum(dvn * xh, axis=0, keepdims=True)
            dxh = dvn * gsv
            dvg = rstd[rows] * (dxh - jnp.mean(dxh, axis=-1, keepdims=True) - xh * jnp.mean(dxh * xh, axis=-1, keepdims=True))
            dua_ref[rows, :WIDTH_A] = (dya * mixed * _gelu_grad(u[rows])).astype(BF16)
            dua_ref[rows, WIDTH_A:] = (dvg * _gelu_grad(va[rows])).astype(BF16)
        for h in range(4):
            dsw_ref[h] += jnp.where(keep, dsw[h], 0.0)
        db2_ref[...] += db2
        dgs_ref[...] += dgs
        dga_ref[...] += dga

    return pl.pallas_call(
        body, name="sgu_bwd", grid=(s_len // tm,),
        in_specs=[_rows(tm, 2 * WIDTH_A), _whole((4, CHUNK, CHUNK)), _whole((CHUNK, WIDTH_A)), _whole((1, WIDTH_A)),
                  _whole((1, WIDTH_A)), _rows(tm, WIDTH_A)],
        out_specs=[_rows(tm, 2 * WIDTH_A), _whole((4, CHUNK, CHUNK)), _whole((CHUNK, WIDTH_A)), _whole((1, WIDTH_A)), _whole((1, WIDTH_A))],
        out_shape=[jax.ShapeDtypeStruct((s_len, 2 * WIDTH_A), BF16), jax.ShapeDtypeStruct((4, CHUNK, CHUNK), F32),
                   jax.ShapeDtypeStruct((CHUNK, WIDTH_A), F32), jax.ShapeDtypeStruct((1, WIDTH_A), F32),
                   jax.ShapeDtypeStruct((1, WIDTH_A), F32)],
        compiler_params=_params("arbitrary"),
    )(ua, sw, b2, gs, ga, dya_n)


def _dproj(dua, dqs, dks, dvs, cos, sin, hn1, exchange=()):
    s_len = dua.shape[0]
    tm = ROW_TILE
    n_br = len(DILATIONS)
    n_steps = s_len // tm

    def body(dua_ref, *rest):
        groups = [rest[g * n_br:(g + 1) * n_br] for g in range(3)]
        cos_ref, sin_ref, hn_ref, out_ref, dw_ref, acc, dw_acc = rest[3 * n_br:]
        step = pl.program_id(0)

        @pl.when(step == 0)
        def _():
            dw_acc[...] = jnp.zeros_like(dw_acc)

        out_ref[:, :2 * WIDTH_A] = dua_ref[...]
        c = cos_ref[...]
        s = sin_ref[...]
        first_half = (lax.broadcasted_iota(jnp.int32, (tm, 128), 1) % HEAD_DIM) < HEAD_DIM // 2
        for g, refs in enumerate(groups):
            for cb in range(N_PAIRS):
                t = refs[0][0, cb].astype(F32)
                for i, d in enumerate(DILATIONS[1:]):
                    _from_sub(refs[i + 1], cb, acc, i, d, tm)
                    t = t + acc[i]
                if g < 2:
                    t = (t * c - _swap_halves(t, first_half) * s) * (0.125 if g == 0 else 1.0)
                col = 2 * WIDTH_A + g * WIDTH_B + cb * 128
                out_ref[:, col:col + 128] = t.astype(BF16)
        hn = hn_ref[...]
        for j in range(IN_COLS // 256):
            cols = slice(j * 256, (j + 1) * 256)
            dw_acc[cols, :] += lax.dot_general(out_ref[:, cols], hn, _TN, preferred_element_type=F32)

        @pl.when(step == n_steps - 1)
        def _():
            dw_ref[...] = dw_acc[...].astype(BF16)

    subs = [_sub_spec(d, N_PAIRS, tm) for d in DILATIONS]
    (dproj, dw), received = _call(
        body, name="dproj_dw_in", grid=(n_steps,),
        in_specs=[_rows(tm, 2 * WIDTH_A)] + subs * 3 + [_rows(tm, 128), _rows(tm, 128), _rows(tm, D_MODEL)],
        out_specs=[_rows(tm, IN_COLS), _whole((IN_COLS, D_MODEL))],
        out_shape=[jax.ShapeDtypeStruct((s_len, IN_COLS), BF16), jax.ShapeDtypeStruct((IN_COLS, D_MODEL), BF16)],
        scratch_shapes=[pltpu.VMEM((n_br - 1, tm, 128), F32), pltpu.VMEM((IN_COLS, D_MODEL), F32)],
        semantics=("arbitrary",), args=(dua, *dqs, *dks, *dvs, cos, sin, hn1), exchange=exchange)
    return dproj, dw, received


def _mm_tn(a, b, name, exchange=()):
    s_len, m = a.shape
    n = b.shape[1]
    tk = 2 * ROW_TILE
    tm = m if m <= 512 else (1408 if m == D_FF else 512)
    n_k = s_len // tk

    def body(a_ref, b_ref, o_ref, acc_ref):
        k = pl.program_id(1)

        @pl.when(k == 0)
        def _():
            acc_ref[...] = jnp.zeros_like(acc_ref)

        acc_ref[...] += lax.dot_general(a_ref[...].astype(BF16), b_ref[...].astype(BF16), _TN, preferred_element_type=F32)

        @pl.when(k == n_k - 1)
        def _():
            o_ref[...] = acc_ref[...].astype(BF16)

    (grad,), received = _call(
        body, name=name, grid=(m // tm, n_k),
        in_specs=[pl.BlockSpec((tk, tm), lambda i, k: (k, i)), pl.BlockSpec((tk, n), lambda i, k: (k, 0))],
        out_specs=[pl.BlockSpec((tm, n), lambda i, k: (i, 0))],
        out_shape=[jax.ShapeDtypeStruct((m, n), BF16)],
        scratch_shapes=[pltpu.VMEM((tm, n), F32)],
        semantics=("parallel", "arbitrary"), args=(a, b), exchange=exchange)
    return grad, received


def _position():
    x, y, c = lax.axis_index("x"), lax.axis_index("y"), lax.axis_index("c")
    return x, y, c, 4 * x + 2 * y + c


def _peer(x, y, c, rel):
    return (x ^ ((rel >> 2) & 1), y ^ ((rel >> 1) & 1), c ^ (rel & 1))


def _exchange_out_shape(kind, arr):
    return jax.ShapeDtypeStruct(((N_DEV,) + arr.shape) if kind == "gather" else arr.shape, arr.dtype)


def _exchange_sems(n_items):
    return [pltpu.SemaphoreType.DMA((n_items, N_DEV)), pltpu.SemaphoreType.DMA((n_items, N_DEV)), pltpu.SemaphoreType.DMA((n_items,))]


def _exchange_copies(kinds, srcs, dsts, sems, arrivals):
    send_sems, recv_sems, local_sems = sems
    x, y, c, me = _position()
    local, sends, recvs = [], [], []
    for k, (kind, src, dst) in enumerate(zip(kinds, srcs, dsts)):
        own = src if kind == "gather" else src.at[me]
        local.append(pltpu.make_async_copy(own, dst.at[me], local_sems.at[k]))
        for rel in range(1, N_DEV):
            going = src if kind == "gather" else src.at[me ^ rel]
            common = dict(send_sem=send_sems.at[k, rel], recv_sem=recv_sems.at[k, rel],
                          device_id=_peer(x, y, c, rel), device_id_type=MESH)
            sends.append(pltpu.make_async_remote_copy(src_ref=going, dst_ref=dst.at[me], **common))
            if arrivals:
                recvs.append(pltpu.make_async_remote_copy(src_ref=own, dst_ref=dst.at[me ^ rel], **common))
    return local, sends, recvs


def _exchange_start(kinds, srcs, dsts, sems):
    local, sends, _ = _exchange_copies(kinds, srcs, dsts, sems, arrivals=False)
    for cp in local + sends:
        cp.start()


def _exchange_finish(kinds, srcs, dsts, sems):
    local, sends, recvs = _exchange_copies(kinds, srcs, dsts, sems, arrivals=True)
    for cp in recvs:
        cp.wait_recv()
    for cp in sends:
        cp.wait_send()
    for cp in local:
        cp.wait()


def _exchange_only(items, name):
    kinds = [k for k, _ in items]
    n = len(items)

    def body(*refs):
        srcs, dsts, sems = refs[:n], refs[n:2 * n], refs[2 * n:]
        _exchange_start(kinds, srcs, dsts, sems)
        _exchange_finish(kinds, srcs, dsts, sems)

    any_spec = pl.BlockSpec(memory_space=pl.ANY)
    return pl.pallas_call(
        body, name=name, in_specs=[any_spec] * n, out_specs=[any_spec] * n,
        out_shape=[_exchange_out_shape(k, a) for k, a in items],
        scratch_shapes=_exchange_sems(n),
        compiler_params=pltpu.CompilerParams(has_side_effects=True),
    )(*[a for _, a in items])


def _gather_two_level_with_rope_tables(shard, inv_freq, s_len, name):
    rows = ROW_TILE

    def body(inv_ref, src, cos_ref, sin_ref, dst, send_sems, recv_sems, local_sem):
        x, y, c, me = _position()
        sibling = (x, y, 1 - c)
        chips = [(1 - x, y), (x, 1 - y), (1 - x, 1 - y)]

        def block(px, py, pc):
            return dst.at[4 * px + 2 * py + pc]

        def copy(k, blk, to, src_ref=None):
            return pltpu.make_async_remote_copy(
                src_ref=block(*blk) if src_ref is None else src_ref, dst_ref=block(*blk),
                send_sem=send_sems.at[k], recv_sem=recv_sems.at[k], device_id=to, device_id_type=MESH)

        x_nbr, y_nbr, diag = chips
        mine = pltpu.make_async_copy(src, dst.at[me], local_sem)
        mine.start()
        first = [copy(0, (x, y, c), sibling, src), copy(1, (x, y, c), (*x_nbr, c), src), copy(2, (x, y, c), (*y_nbr, c), src)]
        for cp in first:
            cp.start()

        inv = inv_ref[...]
        lane = lax.broadcasted_iota(jnp.int32, (rows, 128), 1)
        sign = jnp.where((lane // (HEAD_DIM // 2)) % 2 == 0, -1.0, 1.0)
        row = lax.broadcasted_iota(jnp.int32, (rows, 128), 0)
        n_chunks = s_len // rows

        def fill_tables(lo, hi):
            @pl.loop(lo, hi)
            def _(i):
                at = pl.multiple_of(i * rows, rows)
                ang = (row + at).astype(F32) * inv
                cos_ref[pl.ds(at, rows), :] = jnp.cos(ang)
                sin_ref[pl.ds(at, rows), :] = jnp.sin(ang) * sign

        fill_tables(0, n_chunks // 2)
        passed = [copy(4 + j, (*chip, c), sibling) for j, chip in enumerate(chips)]
        copy(1, (*x_nbr, c), (x, y, c)).wait_recv()
        copy(2, (*y_nbr, c), (x, y, c)).wait_recv()

        @pl.when(c == 1)
        def _():
            copy(3, (*x_nbr, c), (*y_nbr, c)).start()

        @pl.when(c == 0)
        def _():
            copy(3, (*y_nbr, c), (*x_nbr, c)).start()

        passed[0].start()
        passed[1].start()
        fill_tables(n_chunks // 2, n_chunks)
        copy(3, (*diag, c), (x, y, c)).wait_recv()
        passed[2].start()
        copy(0, (x, y, 1 - c), (x, y, c)).wait_recv()
        for j, chip in enumerate(chips):
            copy(4 + j, (*chip, 1 - c), (x, y, c)).wait_recv()
        for cp in first + passed:
            cp.wait_send()
        copy(3, (*x_nbr, c), (*y_nbr, c)).wait_send()
        mine.wait()

    any_spec = pl.BlockSpec(memory_space=pl.ANY)
    vmem = pl.BlockSpec(memory_space=pltpu.VMEM)
    table = jax.ShapeDtypeStruct((s_len, 128), F32)
    return pl.pallas_call(
        body, name=name, in_specs=[vmem, any_spec], out_specs=[vmem, vmem, any_spec],
        out_shape=[table, table, _exchange_out_shape("gather", shard)],
        scratch_shapes=[pltpu.SemaphoreType.DMA((N_DEV - 1,)), pltpu.SemaphoreType.DMA((N_DEV - 1,)), pltpu.SemaphoreType.DMA],
        compiler_params=pltpu.CompilerParams(has_side_effects=True, vmem_limit_bytes=V7X_VMEM_LIMIT_BYTES),
    )(inv_freq, shard)


def _call(body, *, name, grid, in_specs, out_specs, out_shape, args, scratch_shapes=(), semantics, exchange=()):
    if not exchange:
        outs = pl.pallas_call(body, name=name, grid=grid, in_specs=in_specs, out_specs=out_specs, out_shape=out_shape,
                              scratch_shapes=list(scratch_shapes), compiler_params=_params(*semantics))(*args)
        return outs, []
    kinds = [k for k, _ in exchange]
    n_in, n_out, n_x, n_scr = len(in_specs), len(out_specs), len(exchange), len(scratch_shapes)

    def wrapped(*refs):
        ins, refs = refs[:n_in], refs[n_in:]
        srcs, refs = refs[:n_x], refs[n_x:]
        outs, refs = refs[:n_out], refs[n_out:]
        dsts, refs = refs[:n_x], refs[n_x:]
        scratch, sems = refs[:n_scr], refs[n_scr:]
        ids = [pl.program_id(a) for a in range(len(grid))]
        first = functools.reduce(jnp.logical_and, [i == 0 for i in ids])
        last = functools.reduce(jnp.logical_and, [i == g - 1 for i, g in zip(ids, grid)])

        @pl.when(first)
        def _():
            _exchange_start(kinds, srcs, dsts, sems)

        body(*ins, *outs, *scratch)

        @pl.when(last)
        def _():
            _exchange_finish(kinds, srcs, dsts, sems)

    any_spec = pl.BlockSpec(memory_space=pl.ANY)
    outs = pl.pallas_call(
        wrapped, name=name, grid=grid,
        in_specs=list(in_specs) + [any_spec] * n_x, out_specs=list(out_specs) + [any_spec] * n_x,
        out_shape=list(out_shape) + [_exchange_out_shape(k, a) for k, a in exchange],
        scratch_shapes=list(scratch_shapes) + _exchange_sems(n_x),
        compiler_params=pltpu.CompilerParams(dimension_semantics=("arbitrary",) * len(grid),
                                             vmem_limit_bytes=V7X_VMEM_LIMIT_BYTES, has_side_effects=True),
    )(*args, *[a for _, a in exchange])
    return outs[:n_out], outs[n_out:]


def _adamw_math(w, g, m, v):
    m = ADAM_B1 * m + (1.0 - ADAM_B1) * g
    v = ADAM_B2 * v + (1.0 - ADAM_B2) * (g * g)
    m_hat = m / (1.0 - ADAM_B1 ** ADAM_STEP)
    v_hat = v / (1.0 - ADAM_B2 ** ADAM_STEP)
    return -ADAM_LR * (m_hat / (jnp.sqrt(v_hat) + ADAM_EPS) + ADAM_WD * w), m, v


def _adamw(parts, w, m, v, name):
    rows, cols = w.shape
    tm = 256 if rows % 256 == 0 and rows > 256 else rows

    def body(p_ref, w_ref, m_ref, v_ref, g_ref, d_ref, nm_ref, nv_ref):
        g = p_ref[0].astype(F32)
        for j in range(1, N_DEV):
            g = g + p_ref[j].astype(F32)
        delta, nm, nv = _adamw_math(w_ref[...], g, m_ref[...], v_ref[...])
        g_ref[...] = g
        d_ref[...] = delta
        nm_ref[...] = nm
        nv_ref[...] = nv

    shard = jax.ShapeDtypeStruct((rows, cols), F32)
    return pl.pallas_call(
        body, name=name, grid=(rows // tm,),
        in_specs=[pl.BlockSpec((N_DEV, tm, cols), lambda i: (0, i, 0))] + [_rows(tm, cols)] * 3,
        out_specs=[_rows(tm, cols)] * 4,
        out_shape=[shard] * 4,
        compiler_params=_params("parallel"),
    )(parts, w, m, v)


_SMALL = ("mix_norm_g", "sgu_w", "sgu_b", "sgu_norm_g", "out_norm_a", "out_norm_b", "ffn_norm_g", "ple_norm_g", "final_norm_g")
_BIG = ("w_in", "w_out", "w_gate", "w_up", "w_down", "w_ple_gate", "w_ple_proj")
_COLUMN_SHARDED = ("w_in", "w_gate", "w_up", "w_ple_proj")
_ORDER = ("mix_norm_g", "w_in", "sgu_w", "sgu_b", "sgu_norm_g", "out_norm_a", "out_norm_b", "w_out", "ffn_norm_g",
          "w_gate", "w_up", "w_down", "ple_norm_g", "w_ple_gate", "w_ple_proj", "final_norm_g")


def _pack_small(values, names=_SMALL):
    flat = jnp.concatenate([values[n].reshape(-1).astype(F32) for n in names])
    pad = (-flat.shape[0]) % (8 * 128)
    return jnp.pad(flat, (0, pad)).reshape(-1, 128)


def _unpack_small(packed, like):
    flat = packed.reshape(-1)
    out, at = {}, 0
    for n in _SMALL:
        size = like[n].size
        out[n] = flat[at:at + size].reshape(like[n].shape)
        at += size
    return out


def _own_orientation(name, value):
    return value[0].T if name in _COLUMN_SHARDED else value[0]


def _reference_orientation(name, value):
    return (value.T if name in _COLUMN_SHARDED else value)[None]


def _full_from_gathered(gathered):
    return gathered.reshape(N_DEV * gathered.shape[1], gathered.shape[2])


def _sliced_for_devices(grad):
    return grad.reshape(N_DEV, grad.shape[0] // N_DEV, grad.shape[1])


def _rope_inv_freq():
    half = HEAD_DIM // 2
    inv = ROPE_THETA ** (-jnp.arange(half, dtype=F32) / half)
    return jnp.tile(inv, 128 // half)[None, :]


def _forward_backward(x, p, target, small, shards):
    def gather(*names):
        return [("gather", shards[n]) for n in names]

    def scatter(**grads):
        return [("scatter", _sliced_for_devices(g)) for g in grads.values()]

    full, parts = {}, {}
    s_len = x.shape[0]
    cos, sin, got = _gather_two_level_with_rope_tables(shards["w_in"], _rope_inv_freq(), s_len, "gather_w_in")
    full["w_in"] = _full_from_gathered(got)

    g_mix, g_ffn, g_ple = small["mix_norm_g"], small["ffn_norm_g"], small["ple_norm_g"]
    g_fin = small["final_norm_g"].reshape(1, D_MODEL)
    sw, gs, ga, gb = small["sgu_w"], small["sgu_norm_g"], small["out_norm_a"], small["out_norm_b"]
    b2 = jnp.repeat(small["sgu_b"].T, HEAD_DIM, axis=1)
    head_sum = (jnp.arange(WIDTH_B)[:, None] // HEAD_DIM == jnp.arange(128)[None, :]).astype(BF16)
    n_br = len(DILATIONS)

    def arrived(names, got):
        for n, g in zip(names, got):
            full[n] = _full_from_gathered(g)

    (ua, hn1, *qkv), got = _inproj(x, g_mix, full["w_in"], cos, sin, exchange=gather("w_gate"))
    arrived(("w_gate",), got)
    ya_n = _sgu_fwd(ua, sw, b2, gs, ga)
    half = shards["w_up"].shape[0] // 2
    riders = [[("gather", shards["w_up"][:half])], [("gather", shards["w_up"][half:])], gather("w_out")]
    branch, got = [], []
    for i, d in enumerate(DILATIONS):
        o_l, g = _attn_fwd(qkv[i], d, exchange=riders[i])
        branch.append(o_l)
        got += g
    arrived(("w_up", "w_out"), [jnp.concatenate(got[:2], axis=1), got[2]])
    (y, yb, *lse), _ = _combine([o for o, _ in branch], [l for _, l in branch], ya_n, gb, head_sum.T)
    last_wave = ("w_down", "w_ple_gate", "w_ple_proj")
    (h1, a, b, act, hn2), got = _ffn_up(y, full["w_out"], x, g_ffn, full["w_gate"], full["w_up"], exchange=gather(*last_wave))
    arrived(last_wave, got)
    h2, h3, gate, pp, hn3 = _ffn_down_ple(act, full["w_down"], h1, g_ple, full["w_ple_gate"], p, full["w_ple_proj"])

    dh2, loss, d_fin, d_ple, g_ple_gate, g_ple_proj = _loss_ple_bwd(
        h3, target, g_fin, gate, pp, h2, g_ple, full["w_ple_gate"], hn3, p)
    g_down, _ = _mm_tn(act, dh2, "dw_down")
    (da, db), (parts["w_down"],) = _ffn_down_bwd(dh2, full["w_down"], a, b, exchange=scatter(w_down=g_down))
    g_gate, _ = _mm_tn(da, hn2, "dw_gate")
    g_up, _ = _mm_tn(db, hn2, "dw_up")
    (dh1, d_ffn, g_out), (parts["w_gate"], parts["w_ple_gate"], parts["w_ple_proj"]) = _mm_norm_bwd(
        [(da, full["w_gate"]), (db, full["w_up"])], h1, g_ffn, dh2, "ffn_up_bwd",
        exchange=scatter(w_gate=g_gate, w_ple_gate=g_ple_gate, w_ple_proj=g_ple_proj), dw_lhs=y)
    (dya_n, d_gb, *do_dd), _ = _outproj_bwd(dh1, full["w_out"], yb, gb, head_sum)
    grads_b = []
    for i, d in enumerate(DILATIONS):
        g3, got = _attn_bwd(qkv[i], do_dd[i], lse[i], do_dd[n_br + i], d,
                            exchange=scatter(w_up=g_up, w_out=g_out) if i == 0 else ())
        grads_b.append(g3)
        if i == 0:
            parts["w_up"], parts["w_out"] = got
    dua, d_sw, d_b2, d_gs, d_ga = _sgu_bwd(ua, sw, b2, gs, ga, dya_n)
    early = {
        "sgu_w": d_sw, "sgu_b": d_b2.reshape(CHUNK, 4, HEAD_DIM).sum(axis=-1).T, "sgu_norm_g": d_gs, "out_norm_a": d_ga,
        "out_norm_b": d_gb, "ffn_norm_g": d_ffn, "ple_norm_g": d_ple, "final_norm_g": d_fin,
    }
    dproj, g_in, (early_parts,) = _dproj(
        dua, [g[0] for g in grads_b], [g[1] for g in grads_b], [g[2] for g in grads_b], cos, sin, hn1,
        exchange=[("gather", _pack_small(early, _SMALL[1:]))])
    (dx, d_mix), (parts["w_in"],) = _mm_norm_bwd(
        [(dproj, full["w_in"])], x, g_mix, dh1, "inproj_bwd", exchange=scatter(w_in=g_in))
    late = jnp.concatenate([_pack_small({"mix_norm_g": d_mix}, _SMALL[:1]), jnp.broadcast_to(loss, (8, 128))])
    (late_parts,) = _exchange_only([("gather", late)], "gather_mix_norm_grad_and_loss")
    total_loss = jnp.sum(late_parts[:, 8, 0])
    return total_loss, dx, parts, jnp.concatenate([late_parts[:, :8], early_parts], axis=1)


def kernel(x, p, mix_norm_g, w_in, sgu_w, sgu_b, sgu_norm_g, out_norm_a, out_norm_b, w_out, ffn_norm_g, w_gate, w_up, w_down, ple_norm_g, w_ple_gate, w_ple_proj, final_norm_g, loss_target, m_mix_norm_g, m_w_in, m_sgu_w, m_sgu_b, m_sgu_norm_g, m_out_norm_a, m_out_norm_b, m_w_out, m_ffn_norm_g, m_w_gate, m_w_up, m_w_down, m_ple_norm_g, m_w_ple_gate, m_w_ple_proj, m_final_norm_g, v_mix_norm_g, v_w_in, v_sgu_w, v_sgu_b, v_sgu_norm_g, v_out_norm_a, v_out_norm_b, v_w_out, v_ffn_norm_g, v_w_gate, v_w_up, v_w_down, v_ple_norm_g, v_w_ple_gate, v_w_ple_proj, v_final_norm_g):
    given = dict(locals())
    weights = {n: given[n] for n in _ORDER}
    moments_m = {n: given["m_" + n] for n in _ORDER}
    moments_v = {n: given["v_" + n] for n in _ORDER}

    shards = {n: _own_orientation(n, weights[n]).astype(BF16) for n in _BIG}
    small = {n: (weights[n][0] if n in ("sgu_w", "sgu_b") else weights[n]) for n in _SMALL}

    loss, dx, parts, small_parts = _forward_backward(x[0], p[0, 0], loss_target[0], small, shards)

    small_like = {n: weights[n] for n in _SMALL}
    grads, deltas, new_m, new_v = {}, {}, {}, {}
    for n in _BIG:
        outs = _adamw(parts[n], _own_orientation(n, weights[n]), _own_orientation(n, moments_m[n]),
                      _own_orientation(n, moments_v[n]), "adamw_" + n)
        grads[n], deltas[n], new_m[n], new_v[n] = [_reference_orientation(n, o) for o in outs]
    g, d, nm, nv = _adamw(small_parts, _pack_small(small_like), _pack_small({n: moments_m[n] for n in _SMALL}),
                          _pack_small({n: moments_v[n] for n in _SMALL}), "adamw_small")
    for out, packed in ((grads, g), (deltas, d), (new_m, nm), (new_v, nv)):
        out.update(_unpack_small(packed, small_like))

    return (loss, dx[None], *[grads[n] for n in _ORDER], *[deltas[n] for n in _ORDER],
            *[new_m[n] for n in _ORDER], *[new_v[n] for n in _ORDER])
```

```python
import functools

import jax
import jax.numpy as jnp
from jax import lax
from jax.experimental import pallas as pl
from jax.experimental.pallas import tpu as pltpu

F32 = jnp.float32
BF16 = jnp.bfloat16

D_MODEL = 1024
WIDTH_A = 256
WIDTH_B = 768
D_FF = 2816
IN_COLS = 2 * WIDTH_A + 3 * WIDTH_B
PLE_DIM = 256
HEAD_DIM = 64
N_PAIRS = WIDTH_B // 128
CHUNK = 128
N_BACK = 128
DILATIONS = (1, 4, 16)
ROPE_THETA = 10000.0
EPS = 1e-6
N_DEV = 8

ADAM_LR = 0.001
ADAM_B1 = 0.9
ADAM_B2 = 0.999
ADAM_EPS = 1e-08
ADAM_WD = 0.01
ADAM_STEP = 10

V7X_VMEM_LIMIT_BYTES = 56 * 1024 * 1024
ROW_TILE = 512
MESH = pl.DeviceIdType.MESH
NEG = -1e30

_NT = (((1,), (1,)), ((), ()))
_TN = (((0,), (0,)), ((), ()))


def _params(*semantics):
    return pltpu.CompilerParams(dimension_semantics=semantics, vmem_limit_bytes=V7X_VMEM_LIMIT_BYTES)


def _rows(tm, width):
    return pl.BlockSpec((tm, width), lambda i: (i, 0))


def _whole(shape):
    return pl.BlockSpec(shape, lambda *_: (0,) * len(shape))


def _resident(shape):
    return pl.BlockSpec(shape, lambda *_: (0,) * len(shape), pipeline_mode=pl.Buffered(1))


def _gelu(x):
    t = jnp.tanh(0.7978845608028654 * (x + 0.044715 * (x * x * x)))
    return 0.5 * x * (1.0 + t)


def _gelu_grad(x):
    t = jnp.tanh(0.7978845608028654 * (x + 0.044715 * (x * x * x)))
    return 0.5 * (1.0 + t) + 0.5 * x * (1.0 - t * t) * (0.7978845608028654 * (1.0 + 3.0 * 0.044715 * (x * x)))


def _rstd(x):
    return lax.rsqrt(jnp.mean(x * x, axis=-1, keepdims=True) + EPS)


def _norm_bwd(dn, h, g, r=None):
    r = _rstd(h) if r is None else r
    n = h * r
    t = dn * g
    return r * (t - n * jnp.mean(t * n, axis=-1, keepdims=True)), dn * n


def _swap_halves(x, first_half):
    return jnp.where(first_half, pltpu.roll(x, 96, 1), pltpu.roll(x, 32, 1))


def _sub_spec(d, n_cb, tm):
    return pl.BlockSpec((d, n_cb, tm // d, 128), lambda i: (0, 0, i, 0))


def _sub_shape(s_len, d, n_cb, dtype):
    return jax.ShapeDtypeStruct((d, n_cb, s_len // d, 128), dtype)


def _to_sub(stage_ref, cb_src, out_ref, cb_dst, d, tm):
    slab = stage_ref.at[cb_src]
    for r in range(d):
        out_ref[r, cb_dst] = slab[pl.ds(r, tm // d, stride=d), :].astype(out_ref.dtype)


def _from_sub(in_ref, cb_src, stage_ref, cb_dst, d, tm):
    slab = stage_ref.at[cb_dst]
    for r in range(d):
        slab[pl.ds(r, tm // d, stride=d), :] = in_ref[r, cb_src].astype(F32)


def _inproj(x, g, w, cos, sin, exchange=()):
    s_len = x.shape[0]
    tm = ROW_TILE
    n_cb = 3 * N_PAIRS

    def body(x_ref, g_ref, w_ref, cos_ref, sin_ref, ua_ref, hn_ref, *rest):
        sub_refs, stage = rest[:-1], rest[-1]
        xf = x_ref[...]
        hn = (xf * _rstd(xf) * g_ref[...]).astype(BF16)
        hn_ref[...] = hn
        c = cos_ref[...]
        s = sin_ref[...]
        first_half = (lax.broadcasted_iota(jnp.int32, (tm, 128), 1) % HEAD_DIM) < HEAD_DIM // 2
        for col in range(0, IN_COLS, 512):
            width = min(512, IN_COLS - col)
            acc = lax.dot_general(hn, w_ref[col:col + width, :], _NT, preferred_element_type=F32)
            if col < 2 * WIDTH_A:
                ua_ref[:, col:col + width] = acc
                continue
            for part in range(width // 128):
                cb = (col - 2 * WIDTH_A) // 128 + part
                t = acc[:, part * 128:(part + 1) * 128]
                if cb < 2 * N_PAIRS:
                    t = (t * c + _swap_halves(t, first_half) * s) * (0.125 if cb < N_PAIRS else 1.0)
                stage[cb] = t
                sub_refs[0][0, cb] = t.astype(BF16)
        for cb in range(n_cb):
            for d, out_ref in zip(DILATIONS[1:], sub_refs[1:]):
                _to_sub(stage, cb, out_ref, cb, d, tm)

    return _call(
        body, name="inproj", grid=(s_len // tm,),
        in_specs=[_rows(tm, D_MODEL), _whole((1, D_MODEL)), _resident((IN_COLS, D_MODEL)), _rows(tm, 128), _rows(tm, 128)],
        out_specs=[_rows(tm, 2 * WIDTH_A), _rows(tm, D_MODEL)] + [_sub_spec(d, n_cb, tm) for d in DILATIONS],
        out_shape=[jax.ShapeDtypeStruct((s_len, 2 * WIDTH_A), F32), jax.ShapeDtypeStruct((s_len, D_MODEL), BF16)]
        + [_sub_shape(s_len, d, n_cb, BF16) for d in DILATIONS],
        scratch_shapes=[pltpu.VMEM((n_cb, tm, 128), F32)],
        semantics=("parallel",), args=(x, g, w, cos, sin), exchange=exchange)


def _sgu_mix_weights(sw_ref):
    keep = lax.broadcasted_iota(jnp.int32, (CHUNK, CHUNK), 0) >= lax.broadcasted_iota(jnp.int32, (CHUNK, CHUNK), 1)
    return [jnp.where(keep, sw_ref[h], 0.0).astype(BF16) for h in range(4)], keep


def _sgu_core(ua_ref, gs_ref):
    u = ua_ref[:, :WIDTH_A]
    va = ua_ref[:, WIDTH_A:]
    vg = _gelu(va)
    xc = vg - jnp.mean(vg, axis=-1, keepdims=True)
    rstd = lax.rsqrt(jnp.mean(xc * xc, axis=-1, keepdims=True) + EPS)
    xhat = xc * rstd
    return u, va, _gelu(u), xhat, rstd, xhat * gs_ref[...]


def _sgu_mix(wm, vnb, half, head_a):
    va, vb = _both_heads(vnb[:, half * 128:(half + 1) * 128], head_a)
    return (jnp.dot(wm[2 * half], va, preferred_element_type=F32)
            + jnp.dot(wm[2 * half + 1], vb, preferred_element_type=F32))


def _sgu_fwd(ua, sw, b2, gs, ga):
    s_len = ua.shape[0]
    tm = ROW_TILE

    def body(ua_ref, sw_ref, b2_ref, gs_ref, ga_ref, out_ref):
        _, _, ug, _, _, vn = _sgu_core(ua_ref, gs_ref)
        wm, _ = _sgu_mix_weights(sw_ref)
        head_a = lax.broadcasted_iota(jnp.int32, (CHUNK, 128), 1) < HEAD_DIM
        for c in range(tm // CHUNK):
            rows = slice(c * CHUNK, (c + 1) * CHUNK)
            vnb = vn[rows].astype(BF16)
            mixed = b2_ref[...] + jnp.concatenate([_sgu_mix(wm, vnb, half, head_a) for half in range(2)], axis=1)
            ya = ug[rows] * mixed
            out_ref[rows, :] = (ya * _rstd(ya) * ga_ref[...]).astype(BF16)

    return pl.pallas_call(
        body, name="sgu_fwd", grid=(s_len // tm,),
        in_specs=[_rows(tm, 2 * WIDTH_A), _whole((4, CHUNK, CHUNK)), _whole((CHUNK, WIDTH_A)), _whole((1, WIDTH_A)), _whole((1, WIDTH_A))],
        out_specs=_rows(tm, WIDTH_A),
        out_shape=jax.ShapeDtypeStruct((s_len, WIDTH_A), BF16),
        compiler_params=_params("parallel"),
    )(ua, sw, b2, gs, ga)


STEP_POSITIONS = 2 * ROW_TILE


def _attn_geometry(sd):
    tile = min(STEP_POSITIONS, sd)
    return tile, tile // CHUNK, sd // tile, STEP_POSITIONS // tile


PAIRS_PER_STEP = 6


def _attn_spec(cb0, rows, row_index, res):
    return pl.BlockSpec((res, PAIRS_PER_STEP, rows, 128), lambda r, g, n: (r, cb0 // PAIRS_PER_STEP + g, row_index(n), 0))


assert PAIRS_PER_STEP == N_PAIRS
assert DILATIONS[0] == 1


def _stats_spec(rows, row_index, res):
    return pl.BlockSpec((res, None, rows, 128), lambda r, g, n: (r, 0, row_index(n), 0))


def _stats_shape(sd, d):
    return jax.ShapeDtypeStruct((d, 1, sd, 128), F32)


def _both_heads(x, head_a):
    zero = jnp.zeros_like(x)
    return [jnp.where(head_a, x, zero), jnp.where(head_a, zero, x)]


def _attn_fwd(qkv, d, exchange=()):
    sd = qkv.shape[2]
    tile, nb, n_tiles, res = _attn_geometry(sd)

    def prev(n):
        return jnp.maximum(n * nb - 1, 0)

    def body(q_ref, k_ref, kp_ref, v_ref, vp_ref, o_ref, l_ref):
        for rr in range(res):
            for hp in range(PAIRS_PER_STEP):
                one_pair(hp, q_ref.at[rr, hp], k_ref.at[rr, hp], kp_ref.at[rr, hp], v_ref.at[rr, hp], vp_ref.at[rr, hp],
                         o_ref.at[rr, hp], l_ref.at[rr])

    def one_pair(hp, q_ref, k_ref, kp_ref, v_ref, vp_ref, o_ref, l_ref):
        n = pl.program_id(2)
        lane = lax.broadcasted_iota(jnp.int32, (CHUNK, 128), 1)
        head_a = lane < HEAD_DIM
        qi = lax.broadcasted_iota(jnp.int32, (2 * CHUNK, 2 * CHUNK), 0) % CHUNK
        kc = lax.broadcasted_iota(jnp.int32, (2 * CHUNK, 2 * CHUNK), 1)
        band = (kc >= qi) & (kc <= qi + N_BACK)
        for j in range(nb):
            rows = slice(j * CHUNK, (j + 1) * CHUNK)
            if j == 0:
                kcat = jnp.concatenate([kp_ref[...], k_ref[rows, :]], axis=0)
                vcat = jnp.concatenate([vp_ref[...], v_ref[rows, :]], axis=0)
                valid = band & jnp.logical_or(n > 0, kc >= CHUNK)
            else:
                kcat = k_ref[(j - 1) * CHUNK:(j + 1) * CHUNK, :]
                vcat = v_ref[(j - 1) * CHUNK:(j + 1) * CHUNK, :]
                valid = band
            q2 = jnp.concatenate(_both_heads(q_ref[rows, :], head_a), axis=0)
            s = lax.dot_general(q2, kcat, _NT, preferred_element_type=F32)
            s = jnp.where(valid, s, NEG)
            m = jnp.max(s, axis=-1, keepdims=True)
            p = jnp.exp(s - m)
            l = jnp.sum(p, axis=-1, keepdims=True)
            o2 = jnp.dot(p.astype(BF16), vcat, preferred_element_type=F32) / l
            lse2 = m + jnp.log(l)
            o_ref[rows, :] = jnp.where(head_a, o2[:CHUNK], o2[CHUNK:]).astype(BF16)
            others = l_ref[rows, :] if hp > 0 else jnp.zeros((CHUNK, 128), F32)
            l_ref[rows, :] = jnp.where(lane == 2 * hp, lse2[:CHUNK], jnp.where(lane == 2 * hp + 1, lse2[CHUNK:], others))

    same = lambda n: n
    return _call(
        body, name=f"attn_fwd_d{d}", grid=(d // res, N_PAIRS // PAIRS_PER_STEP, n_tiles),
        in_specs=[_attn_spec(0, tile, same, res), _attn_spec(N_PAIRS, tile, same, res), _attn_spec(N_PAIRS, CHUNK, prev, res),
                  _attn_spec(2 * N_PAIRS, tile, same, res), _attn_spec(2 * N_PAIRS, CHUNK, prev, res)],
        out_specs=[_attn_spec(0, tile, same, res), _stats_spec(tile, same, res)],
        out_shape=[jax.ShapeDtypeStruct((d, N_PAIRS, sd, 128), BF16), _stats_shape(sd, d)],
        semantics=("parallel", "parallel", "parallel"), args=(qkv, qkv, qkv, qkv, qkv), exchange=exchange)


def _combine(outs, lses, ya_n, gb, head_spread):
    s_len = ya_n.shape[0]
    tm = ROW_TILE
    n_br = len(DILATIONS)

    def body(*refs):
        o_refs, l_refs = refs[:n_br], refs[n_br:2 * n_br]
        ya_ref, gb_ref, spread_ref, y_ref, yb_ref = refs[2 * n_br:2 * n_br + 5]
        lse_refs = refs[2 * n_br + 5:3 * n_br + 5]
        o_nat, l_nat, lse_nat, w_wide = refs[3 * n_br + 5:]
        for i, d in enumerate(DILATIONS):
            _from_sub(l_refs[i], 0, l_nat, i, d, tm)
        ls = [l_nat[i] for i in range(n_br)]
        top = jnp.maximum(jnp.maximum(ls[0], ls[1]), ls[2])
        ws = [jnp.exp(l - top) for l in ls]
        den = ws[0] + ws[1] + ws[2]
        inv = 1.0 / den
        for i in range(n_br):
            w = ws[i] * inv
            hi = w.astype(BF16)
            lo = (w - hi.astype(F32)).astype(BF16)
            w_wide[i] = (jnp.dot(hi, spread_ref[...], preferred_element_type=F32)
                         + jnp.dot(lo, spread_ref[...], preferred_element_type=F32))
        lse_nat[0] = top + jnp.log(den)
        for d, lse_ref in zip(DILATIONS, lse_refs):
            _to_sub(lse_nat, 0, lse_ref, 0, d, tm)
        sumsq = jnp.zeros((tm, 1), F32)
        for cb in range(N_PAIRS):
            cols = slice(cb * 128, (cb + 1) * 128)
            yb = w_wide[0, :, cols] * o_refs[0][0, cb].astype(F32)
            for i, d in enumerate(DILATIONS[1:], start=1):
                _from_sub(o_refs[i], cb, o_nat, i, d, tm)
                yb = yb + w_wide[i, :, cols] * o_nat[i]
            yb_ref[:, cb * 128:(cb + 1) * 128] = yb
            sumsq = sumsq + jnp.sum(yb * yb, axis=-1, keepdims=True)
        r = lax.rsqrt(sumsq / WIDTH_B + EPS)
        y_ref[:, :WIDTH_A] = ya_ref[...]
        y_ref[:, WIDTH_A:] = (yb_ref[...] * r * gb_ref[...]).astype(BF16)

    stats = [_sub_spec(d, 1, tm) for d in DILATIONS]
    return _call(
        body, name="attn_combine", grid=(s_len // tm,),
        in_specs=[_sub_spec(d, N_PAIRS, tm) for d in DILATIONS] + stats
        + [_rows(tm, WIDTH_A), _whole((1, WIDTH_B)), _whole((128, WIDTH_B))],
        out_specs=[_rows(tm, D_MODEL), _rows(tm, WIDTH_B)] + stats,
        out_shape=[jax.ShapeDtypeStruct((s_len, D_MODEL), BF16), jax.ShapeDtypeStruct((s_len, WIDTH_B), F32)]
        + [_sub_shape(s_len, d, 1, F32) for d in DILATIONS],
        scratch_shapes=[pltpu.VMEM((n_br, tm, 128), F32), pltpu.VMEM((n_br, tm, 128), F32), pltpu.VMEM((1, tm, 128), F32),
                        pltpu.VMEM((n_br, tm, WIDTH_B), F32)],
        semantics=("parallel",), args=(*outs, *lses, ya_n, gb, head_spread))[0]


def _ffn_up(y, wout, x, g, wg, wu, exchange=()):
    s_len = x.shape[0]
    tm = ROW_TILE

    def body(y_ref, wo_ref, x_ref, g_ref, wg_ref, wu_ref, h_ref, a_ref, b_ref, act_ref, hn_ref):
        hf = x_ref[...] + jnp.dot(y_ref[...], wo_ref[...], preferred_element_type=F32)
        h_ref[...] = hf
        hn = (hf * _rstd(hf) * g_ref[...]).astype(BF16)
        hn_ref[...] = hn
        for col in range(0, D_FF, 256):
            cols = slice(col, col + 256)
            a = lax.dot_general(hn, wg_ref[cols, :], _NT, preferred_element_type=F32)
            b = lax.dot_general(hn, wu_ref[cols, :], _NT, preferred_element_type=F32)
            a_ref[:, cols] = a.astype(BF16)
            b_ref[:, cols] = b.astype(BF16)
            act_ref[:, cols] = (a * jax.nn.sigmoid(a) * b).astype(BF16)

    wide = jax.ShapeDtypeStruct((s_len, D_FF), BF16)
    return _call(
        body, name="ffn_up", grid=(s_len // tm,),
        in_specs=[_rows(tm, D_MODEL), _resident((D_MODEL, D_MODEL)), _rows(tm, D_MODEL), _whole((1, D_MODEL)),
                  _resident((D_FF, D_MODEL)), _resident((D_FF, D_MODEL))],
        out_specs=[_rows(tm, D_MODEL), _rows(tm, D_FF), _rows(tm, D_FF), _rows(tm, D_FF), _rows(tm, D_MODEL)],
        out_shape=[jax.ShapeDtypeStruct((s_len, D_MODEL), F32), wide, wide, wide, jax.ShapeDtypeStruct((s_len, D_MODEL), BF16)],
        semantics=("parallel",), args=(y, wout, x, g, wg, wu), exchange=exchange)


def _ffn_down_ple(act, wd, h1, g, wpg, p, wpp):
    s_len = h1.shape[0]
    tm = ROW_TILE

    def body(act_ref, wd_ref, h1_ref, g_ref, wpg_ref, p_ref, wpp_ref, h2_ref, h3_ref, gate_ref, pp_ref, hn_ref):
        hf = h1_ref[...] + jnp.dot(act_ref[...], wd_ref[...], preferred_element_type=F32)
        h2_ref[...] = hf
        hn = (hf * _rstd(hf) * g_ref[...]).astype(BF16)
        hn_ref[...] = hn
        gate = jax.nn.sigmoid(jnp.dot(hn, wpg_ref[...], preferred_element_type=F32))
        pp = lax.dot_general(p_ref[...].astype(BF16), wpp_ref[...], _NT, preferred_element_type=F32)
        h3_ref[...] = hf + gate * pp
        gate_ref[...] = gate.astype(BF16)
        pp_ref[...] = pp.astype(BF16)

    full = jax.ShapeDtypeStruct((s_len, D_MODEL), F32)
    half = jax.ShapeDtypeStruct((s_len, D_MODEL), BF16)
    return pl.pallas_call(
        body, name="ffn_down_ple", grid=(s_len // tm,),
        in_specs=[_rows(tm, D_FF), _resident((D_FF, D_MODEL)), _rows(tm, D_MODEL), _whole((1, D_MODEL)),
                  _resident((D_MODEL, D_MODEL)), _rows(tm, PLE_DIM), _resident((D_MODEL, PLE_DIM))],
        out_specs=[_rows(tm, D_MODEL)] * 5,
        out_shape=[full, full, half, half, half],
        compiler_params=_params("parallel"),
    )(act, wd, h1, g, wpg, p, wpp)


def _loss_ple_bwd(h3, target, gf, gate, pp, h2, g_ple, wpg, hn3, p):
    s_len = h3.shape[0]
    tm = ROW_TILE
    n_steps = s_len // tm

    def body(h_ref, t_ref, g_ref, gate_ref, pp_ref, h2_ref, gp_ref, w_ref, hn_ref, p_ref,
             dh2_ref, loss_ref, dg_ref, dgp_ref, dwg_ref, dwp_ref, acc_g, acc_p):
        step = pl.program_id(0)

        @pl.when(step == 0)
        def _():
            loss_ref[...] = jnp.zeros_like(loss_ref)
            dg_ref[...] = jnp.zeros_like(dg_ref)
            dgp_ref[...] = jnp.zeros_like(dgp_ref)
            acc_g[...] = jnp.zeros_like(acc_g)
            acc_p[...] = jnp.zeros_like(acc_p)

        hf = h_ref[...]
        gfv = g_ref[...]
        r = _rstd(hf)
        err = hf * r * gfv - t_ref[...]
        loss_ref[...] += 0.5 * jnp.sum(jnp.sum(err * err, axis=-1, keepdims=True), axis=0, keepdims=True) / D_MODEL
        dh, dg_rows = _norm_bwd(err / D_MODEL, hf, gfv, r)
        dg_ref[...] += jnp.sum(dg_rows, axis=0, keepdims=True)
        gate = gate_ref[...].astype(F32)
        dz = (dh * pp_ref[...].astype(F32) * gate * (1.0 - gate)).astype(BF16)
        dpp = (dh * gate).astype(BF16)
        dn = lax.dot_general(dz, w_ref[...], _NT, preferred_element_type=F32)
        dh2, dgp_rows = _norm_bwd(dn, h2_ref[...], gp_ref[...])
        dh2 = dh + dh2
        dh2_ref[...] = dh2
        dgp_ref[...] += jnp.sum(dgp_rows, axis=0, keepdims=True)
        acc_g[...] += lax.dot_general(hn_ref[...], dz, _TN, preferred_element_type=F32)
        acc_p[...] += lax.dot_general(dpp, p_ref[...].astype(BF16), _TN, preferred_element_type=F32)

        @pl.when(step == n_steps - 1)
        def _():
            dwg_ref[...] = acc_g[...].astype(BF16)
            dwp_ref[...] = acc_p[...].astype(BF16)

    gain = jax.ShapeDtypeStruct((1, D_MODEL), F32)
    return pl.pallas_call(
        body, name="loss_ple_bwd", grid=(n_steps,),
        in_specs=[_rows(tm, D_MODEL), _rows(tm, D_MODEL), _whole((1, D_MODEL)), _rows(tm, D_MODEL), _rows(tm, D_MODEL),
                  _rows(tm, D_MODEL), _whole((1, D_MODEL)), _resident((D_MODEL, D_MODEL)), _rows(tm, D_MODEL),
                  _rows(tm, PLE_DIM)],
        out_specs=[_rows(tm, D_MODEL), _whole((1, 128)), _whole((1, D_MODEL)), _whole((1, D_MODEL)),
                   _whole((D_MODEL, D_MODEL)), _whole((D_MODEL, PLE_DIM))],
        out_shape=[jax.ShapeDtypeStruct((s_len, D_MODEL), F32), jax.ShapeDtypeStruct((1, 128), F32), gain, gain,
                   jax.ShapeDtypeStruct((D_MODEL, D_MODEL), BF16), jax.ShapeDtypeStruct((D_MODEL, PLE_DIM), BF16)],
        scratch_shapes=[pltpu.VMEM((D_MODEL, D_MODEL), F32), pltpu.VMEM((D_MODEL, PLE_DIM), F32)],
        compiler_params=_params("arbitrary"),
    )(h3, target, gf, gate, pp, h2, g_ple, wpg, hn3, p)


def _mm_norm_bwd(parts, h, g, dres, name, exchange=(), dw_lhs=None):
    s_len = h.shape[0]
    tm = ROW_TILE
    n_parts = len(parts)
    n_steps = s_len // tm
    has_dw = dw_lhs is not None

    def body(*refs):
        a_refs = refs[0:2 * n_parts:2]
        w_refs = refs[1:2 * n_parts:2]
        h_ref, g_ref, r_ref = refs[2 * n_parts:2 * n_parts + 3]
        rest = refs[2 * n_parts + 3:]
        step = pl.program_id(0)
        if has_dw:
            lhs_ref, o_ref, dg_ref, dw_ref, acc_ref = rest
        else:
            o_ref, dg_ref = rest

        @pl.when(step == 0)
        def _():
            dg_ref[...] = jnp.zeros_like(dg_ref)
            if has_dw:
                acc_ref[...] = jnp.zeros_like(acc_ref)

        dn = jnp.dot(a_refs[0][...], w_refs[0][...], preferred_element_type=F32)
        for a_ref, w_ref in zip(a_refs[1:], w_refs[1:]):
            dn = dn + jnp.dot(a_ref[...], w_ref[...], preferred_element_type=F32)
        dh, dg_rows = _norm_bwd(dn, h_ref[...], g_ref[...])
        out = r_ref[...] + dh
        o_ref[...] = out
        dg_ref[...] += jnp.sum(dg_rows, axis=0, keepdims=True)
        if has_dw:
            acc_ref[...] += lax.dot_general(lhs_ref[...], out.astype(BF16), _TN, preferred_element_type=F32)

            @pl.when(step == n_steps - 1)
            def _():
                dw_ref[...] = acc_ref[...].astype(BF16)

    in_specs, args = [], []
    for a, w in parts:
        in_specs += [_rows(tm, a.shape[1]), _resident(w.shape)]
        args += [a, w]
    in_specs += [_rows(tm, D_MODEL), _whole((1, D_MODEL)), _rows(tm, D_MODEL)]
    args += [h, g, dres]
    out_specs = [_rows(tm, D_MODEL), _whole((1, D_MODEL))]
    out_shape = [jax.ShapeDtypeStruct((s_len, D_MODEL), F32), jax.ShapeDtypeStruct((1, D_MODEL), F32)]
    scratch = []
    if has_dw:
        m = dw_lhs.shape[1]
        in_specs.append(_rows(tm, m))
        args.append(dw_lhs)
        out_specs.append(_whole((m, D_MODEL)))
        out_shape.append(jax.ShapeDtypeStruct((m, D_MODEL), BF16))
        scratch.append(pltpu.VMEM((m, D_MODEL), F32))
    return _call(
        body, name=name, grid=(n_steps,), in_specs=in_specs, out_specs=out_specs, out_shape=out_shape,
        scratch_shapes=scratch, semantics=("arbitrary",), args=tuple(args), exchange=exchange)


def _ffn_down_bwd(dh, wdt, a, b, exchange=()):
    s_len = dh.shape[0]
    tm = ROW_TILE

    def body(dh_ref, w_ref, a_ref, b_ref, da_ref, db_ref):
        dhb = dh_ref[...].astype(BF16)
        for col in range(0, D_FF, 512):
            cols = slice(col, min(col + 512, D_FF))
            dact = lax.dot_general(dhb, w_ref[cols, :], _NT, preferred_element_type=F32)
            av = a_ref[:, cols].astype(F32)
            bv = b_ref[:, cols].astype(F32)
            sig = jax.nn.sigmoid(av)
            t = dact * sig
            silu = av * sig
            da_ref[:, cols] = (t * bv * (1.0 + av - silu)).astype(BF16)
            db_ref[:, cols] = (dact * silu).astype(BF16)

    wide = jax.ShapeDtypeStruct((s_len, D_FF), BF16)
    return _call(
        body, name="ffn_down_bwd", grid=(s_len // tm,),
        in_specs=[_rows(tm, D_MODEL), _resident((D_FF, D_MODEL)), _rows(tm, D_FF), _rows(tm, D_FF)],
        out_specs=[_rows(tm, D_FF), _rows(tm, D_FF)],
        out_shape=[wide, wide],
        semantics=("parallel",), args=(dh, wdt, a, b), exchange=exchange)


def _outproj_bwd(dh1, woutt, yb, gb, head_sum):
    s_len = dh1.shape[0]
    tm = ROW_TILE
    n_br = len(DILATIONS)

    def body(dh_ref, w_ref, yb_ref, gb_ref, e_ref, dya_ref, dgb_ref, *rest):
        do_refs, dd_refs = rest[:n_br], rest[n_br:2 * n_br]
        do_nat, dd_nat = rest[2 * n_br:]

        @pl.when(pl.program_id(0) == 0)
        def _():
            dgb_ref[...] = jnp.zeros_like(dgb_ref)

        dhb = dh_ref[...].astype(BF16)
        dya_ref[...] = lax.dot_general(dhb, w_ref[:WIDTH_A, :], _NT, preferred_element_type=F32)
        dyn = lax.dot_general(dhb, w_ref[WIDTH_A:, :], _NT, preferred_element_type=F32)
        ybv = yb_ref[...]
        dyb, dg_rows = _norm_bwd(dyn, ybv, gb_ref[...])
        dgb_ref[...] += jnp.sum(dg_rows, axis=0, keepdims=True)
        prod = dyb * ybv
        hi = prod.astype(BF16)
        lo = (prod - hi.astype(F32)).astype(BF16)
        dd_nat[0] = (jnp.dot(hi, e_ref[...], preferred_element_type=F32)
                     + jnp.dot(lo, e_ref[...], preferred_element_type=F32))
        for i, d in enumerate(DILATIONS):
            _to_sub(dd_nat, 0, dd_refs[i], 0, d, tm)
        for cb in range(N_PAIRS):
            piece = dyb[:, cb * 128:(cb + 1) * 128]
            do_nat[cb] = piece
            do_refs[0][0, cb] = piece.astype(BF16)
            for i, d in enumerate(DILATIONS[1:], start=1):
                _to_sub(do_nat, cb, do_refs[i], cb, d, tm)

    return _call(
        body, name="outproj_bwd", grid=(s_len // tm,),
        in_specs=[_rows(tm, D_MODEL), _resident((D_MODEL, D_MODEL)), _rows(tm, WIDTH_B), _whole((1, WIDTH_B)), _whole((WIDTH_B, 128))],
        out_specs=[_rows(tm, WIDTH_A), _whole((1, WIDTH_B))] + [_sub_spec(d, N_PAIRS, tm) for d in DILATIONS]
        + [_sub_spec(d, 1, tm) for d in DILATIONS],
        out_shape=[jax.ShapeDtypeStruct((s_len, WIDTH_A), F32), jax.ShapeDtypeStruct((1, WIDTH_B), F32)]
        + [_sub_shape(s_len, d, N_PAIRS, BF16) for d in DILATIONS] + [_sub_shape(s_len, d, 1, F32) for d in DILATIONS],
        scratch_shapes=[pltpu.VMEM((N_PAIRS, tm, 128), F32), pltpu.VMEM((1, tm, 128), F32)],
        semantics=("arbitrary",), args=(dh1, woutt, yb, gb, head_sum))[0]


def _attn_bwd(qkv, do, lse, dd, d, exchange=()):
    sd = qkv.shape[2]
    tile, nb, n_tiles, res = _attn_geometry(sd)
    last_block = sd // CHUNK - 1

    def nxt(n):
        return jnp.minimum((n + 1) * nb, last_block)

    def block(ref, next_ref, j):
        return ref[j * CHUNK:(j + 1) * CHUNK, :] if j < nb else next_ref[...]

    def body(q_ref, qn_ref, k_ref, v_ref, do_ref, don_ref, l_ref, ln_ref, dd_ref, ddn_ref,
             dq_ref, dk_ref, dv_ref, carry_ref):
        for rr in range(res):
            l_t = [block(l_ref.at[rr], ln_ref.at[rr], j).T for j in range(nb + 1)]
            dd_t = [block(dd_ref.at[rr], ddn_ref.at[rr], j).T for j in range(nb + 1)]
            for hp in range(PAIRS_PER_STEP):
                l_rows = [jnp.concatenate([t[2 * hp:2 * hp + 1, :], t[2 * hp + 1:2 * hp + 2, :]], axis=1) for t in l_t]
                dd_rows = [jnp.concatenate([t[2 * hp:2 * hp + 1, :], t[2 * hp + 1:2 * hp + 2, :]], axis=1) for t in dd_t]
                one_pair(q_ref.at[rr, hp], qn_ref.at[rr, hp], k_ref.at[rr, hp], v_ref.at[rr, hp], do_ref.at[rr, hp],
                         don_ref.at[rr, hp], l_rows, dd_rows, dq_ref.at[rr, hp], dk_ref.at[rr, hp], dv_ref.at[rr, hp],
                         carry_ref.at[rr, hp])

    def one_pair(q_ref, qn_ref, k_ref, v_ref, do_ref, don_ref, l_rows, dd_rows, dq_ref, dk_ref, dv_ref, carry_ref):
        n = pl.program_id(2)

        @pl.when(n == 0)
        def _():
            carry_ref[...] = jnp.zeros_like(carry_ref)

        head_a = lax.broadcasted_iota(jnp.int32, (CHUNK, 128), 1) < HEAD_DIM
        col = lax.broadcasted_iota(jnp.int32, (CHUNK, 4 * CHUNK), 1)
        qi = col % CHUNK
        ki = lax.broadcasted_iota(jnp.int32, (CHUNK, 4 * CHUNK), 0)
        is_after = col >= 2 * CHUNK
        mask = (is_after & (ki >= qi)) | (jnp.logical_not(is_after) & (qi >= ki))
        mask_last = mask & jnp.logical_or(jnp.logical_not(is_after), n < n_tiles - 1)
        dq_acc = [carry_ref[...]] + [jnp.zeros((CHUNK, 128), F32) for _ in range(nb)]

        q_st = [jnp.concatenate(_both_heads(block(q_ref, qn_ref, j), head_a), axis=0) for j in range(nb + 1)]
        do_st = [jnp.concatenate(_both_heads(block(do_ref, don_ref, j), head_a), axis=0) for j in range(nb + 1)]

        for j in range(nb):
            rows = slice(j * CHUNK, (j + 1) * CHUNK)
            kj = k_ref[rows, :]
            vj = v_ref[rows, :]
            msk = mask if j + 1 < nb else mask_last
            qs = jnp.concatenate([q_st[j], q_st[j + 1]], axis=0)
            dos = jnp.concatenate([do_st[j], do_st[j + 1]], axis=0)
            ls = jnp.concatenate([l_rows[j], l_rows[j + 1]], axis=1)
            dds = jnp.concatenate([dd_rows[j], dd_rows[j + 1]], axis=1)
            st = lax.dot_general(kj, qs, _NT, preferred_element_type=F32)
            pt = jnp.exp(jnp.where(msk, st - ls, NEG))
            dpt = lax.dot_general(vj, dos, _NT, preferred_element_type=F32)
            dst = (pt * (dpt - dds)).astype(BF16)
            dv_ref[rows, :] = jnp.dot(pt.astype(BF16), dos, preferred_element_type=F32).astype(BF16)
            dk_ref[rows, :] = jnp.dot(dst, qs, preferred_element_type=F32).astype(BF16)
            dqs = lax.dot_general(dst, kj, _TN, preferred_element_type=F32)
            dq_acc[j] = dq_acc[j] + jnp.where(head_a, dqs[:CHUNK], dqs[CHUNK:2 * CHUNK])
            dq_acc[j + 1] = dq_acc[j + 1] + jnp.where(head_a, dqs[2 * CHUNK:3 * CHUNK], dqs[3 * CHUNK:])
        for j in range(nb):
            dq_ref[j * CHUNK:(j + 1) * CHUNK, :] = dq_acc[j].astype(BF16)
        carry_ref[...] = dq_acc[nb]

    same = lambda n: n
    grad = jax.ShapeDtypeStruct((d, N_PAIRS, sd, 128), BF16)
    return _call(
        body, name=f"attn_bwd_d{d}", grid=(d // res, N_PAIRS // PAIRS_PER_STEP, n_tiles),
        in_specs=[_attn_spec(0, tile, same, res), _attn_spec(0, CHUNK, nxt, res), _attn_spec(N_PAIRS, tile, same, res),
                  _attn_spec(2 * N_PAIRS, tile, same, res), _attn_spec(0, tile, same, res), _attn_spec(0, CHUNK, nxt, res),
                  _stats_spec(tile, same, res), _stats_spec(CHUNK, nxt, res), _stats_spec(tile, same, res),
                  _stats_spec(CHUNK, nxt, res)],
        out_specs=[_attn_spec(0, tile, same, res)] * 3,
        out_shape=[grad, grad, grad],
        scratch_shapes=[pltpu.VMEM((res, PAIRS_PER_STEP, CHUNK, 128), F32)],
        semantics=("parallel", "parallel", "arbitrary"), args=(qkv, qkv, qkv, qkv, do, do, lse, lse, dd, dd), exchange=exchange)


def _sgu_bwd(ua, sw, b2, gs, ga, dya_n):
    s_len = ua.shape[0]
    tm = ROW_TILE

    def body(ua_ref, sw_ref, b2_ref, gs_ref, ga_ref, dy_ref, dua_ref, dsw_ref, db2_ref, dgs_ref, dga_ref):
        @pl.when(pl.program_id(0) == 0)
        def _():
            dsw_ref[...] = jnp.zeros_like(dsw_ref)
            db2_ref[...] = jnp.zeros_like(db2_ref)
            dgs_ref[...] = jnp.zeros_like(dgs_ref)
            dga_ref[...] = jnp.zeros_like(dga_ref)

        u, va, ug, xhat, rstd, vn = _sgu_core(ua_ref, gs_ref)
        wm, keep = _sgu_mix_weights(sw_ref)
        head_a = lax.broadcasted_iota(jnp.int32, (CHUNK, 128), 1) < HEAD_DIM
        gav = ga_ref[...]
        gsv = gs_ref[...]
        dga = jnp.zeros((1, WIDTH_A), F32)
        dgs = jnp.zeros((1, WIDTH_A), F32)
        db2 = jnp.zeros((CHUNK, WIDTH_A), F32)
        dsw = [jnp.zeros((CHUNK, CHUNK), F32) for _ in range(4)]
        for c in range(tm // CHUNK):
            rows = slice(c * CHUNK, (c + 1) * CHUNK)
            vnb = vn[rows].astype(BF16)
            mixed = b2_ref[...] + jnp.concatenate([_sgu_mix(wm, vnb, half, head_a) for half in range(2)], axis=1)
            ugc = ug[rows]
            dya, dga_rows = _norm_bwd(dy_ref[rows, :], ugc * mixed, gav)
            dga = dga + jnp.sum(dga_rows, axis=0, keepdims=True)
            dmixed = dya * ugc
            db2 = db2 + dmixed
            dmb = dmixed.astype(BF16)
            dvn_halves = []
            for half in range(2):
                lanes = slice(half * 128, (half + 1) * 128)
                dm_heads = _both_heads(dmb[:, lanes], head_a)
                dvn_half = jnp.zeros((CHUNK, 128), F32)
                for k, dmh in enumerate(dm_heads):
                    h = 2 * half + k
                    dsw[h] = dsw[h] + lax.dot_general(dmh, vnb[:, lanes], _NT, preferred_element_type=F32)
                    dvn_half = dvn_half + lax.dot_general(wm[h], dmh, _TN, preferred_element_type=F32)
                dvn_halves.append(dvn_half)
            dvn = jnp.concatenate(dvn_halves, axis=1)
            xh = xhat[rows]
            dgs = dgs + jnp.sum(dvn * xh, axis=0, keepdims=True)
            dxh = dvn * gsv
            dvg = rstd[rows] * (dxh - jnp.mean(dxh, axis=-1, keepdims=True) - xh * jnp.mean(dxh * xh, axis=-1, keepdims=True))
            dua_ref[rows, :WIDTH_A] = (dya * mixed * _gelu_grad(u[rows])).astype(BF16)
            dua_ref[rows, WIDTH_A:] = (dvg * _gelu_grad(va[rows])).astype(BF16)
        for h in range(4):
            dsw_ref[h] += jnp.where(keep, dsw[h], 0.0)
        db2_ref[...] += db2
        dgs_ref[...] += dgs
        dga_ref[...] += dga

    return pl.pallas_call(
        body, name="sgu_bwd", grid=(s_len // tm,),
        in_specs=[_rows(tm, 2 * WIDTH_A), _whole((4, CHUNK, CHUNK)), _whole((CHUNK, WIDTH_A)), _whole((1, WIDTH_A)),
                  _whole((1, WIDTH_A)), _rows(tm, WIDTH_A)],
        out_specs=[_rows(tm, 2 * WIDTH_A), _whole((4, CHUNK, CHUNK)), _whole((CHUNK, WIDTH_A)), _whole((1, WIDTH_A)), _whole((1, WIDTH_A))],
        out_shape=[jax.ShapeDtypeStruct((s_len, 2 * WIDTH_A), BF16), jax.ShapeDtypeStruct((4, CHUNK, CHUNK), F32),
                   jax.ShapeDtypeStruct((CHUNK, WIDTH_A), F32), jax.ShapeDtypeStruct((1, WIDTH_A), F32),
                   jax.ShapeDtypeStruct((1, WIDTH_A), F32)],
        compiler_params=_params("arbitrary"),
    )(ua, sw, b2, gs, ga, dya_n)


def _dproj(dua, dqs, dks, dvs, cos, sin, hn1, exchange=()):
    s_len = dua.shape[0]
    tm = ROW_TILE
    n_br = len(DILATIONS)
    n_steps = s_len // tm

    def body(dua_ref, *rest):
        groups = [rest[g * n_br:(g + 1) * n_br] for g in range(3)]
        cos_ref, sin_ref, hn_ref, out_ref, dw_ref, acc, dw_acc = rest[3 * n_br:]
        step = pl.program_id(0)

        @pl.when(step == 0)
        def _():
            dw_acc[...] = jnp.zeros_like(dw_acc)

        out_ref[:, :2 * WIDTH_A] = dua_ref[...]
        c = cos_ref[...]
        s = sin_ref[...]
        first_half = (lax.broadcasted_iota(jnp.int32, (tm, 128), 1) % HEAD_DIM) < HEAD_DIM // 2
        for g, refs in enumerate(groups):
            for cb in range(N_PAIRS):
                t = refs[0][0, cb].astype(F32)
                for i, d in enumerate(DILATIONS[1:]):
                    _from_sub(refs[i + 1], cb, acc, i, d, tm)
                    t = t + acc[i]
                if g < 2:
                    t = (t * c - _swap_halves(t, first_half) * s) * (0.125 if g == 0 else 1.0)
                col = 2 * WIDTH_A + g * WIDTH_B + cb * 128
                out_ref[:, col:col + 128] = t.astype(BF16)
        hn = hn_ref[...]
        for j in range(IN_COLS // 256):
            cols = slice(j * 256, (j + 1) * 256)
            dw_acc[cols, :] += lax.dot_general(out_ref[:, cols], hn, _TN, preferred_element_type=F32)

        @pl.when(step == n_steps - 1)
        def _():
            dw_ref[...] = dw_acc[...].astype(BF16)

    subs = [_sub_spec(d, N_PAIRS, tm) for d in DILATIONS]
    (dproj, dw), received = _call(
        body, name="dproj_dw_in", grid=(n_steps,),
        in_specs=[_rows(tm, 2 * WIDTH_A)] + subs * 3 + [_rows(tm, 128), _rows(tm, 128), _rows(tm, D_MODEL)],
        out_specs=[_rows(tm, IN_COLS), _whole((IN_COLS, D_MODEL))],
        out_shape=[jax.ShapeDtypeStruct((s_len, IN_COLS), BF16), jax.ShapeDtypeStruct((IN_COLS, D_MODEL), BF16)],
        scratch_shapes=[pltpu.VMEM((n_br - 1, tm, 128), F32), pltpu.VMEM((IN_COLS, D_MODEL), F32)],
        semantics=("arbitrary",), args=(dua, *dqs, *dks, *dvs, cos, sin, hn1), exchange=exchange)
    return dproj, dw, received


def _mm_tn(a, b, name):
    s_len, m = a.shape
    n = b.shape[1]
    tk = 2 * ROW_TILE
    tm = m if m <= 512 else (1408 if m == D_FF else 512)
    n_k = s_len // tk

    def body(a_ref, b_ref, o_ref, acc_ref):
        k = pl.program_id(1)

        @pl.when(k == 0)
        def _():
            acc_ref[...] = jnp.zeros_like(acc_ref)

        acc_ref[...] += lax.dot_general(a_ref[...].astype(BF16), b_ref[...].astype(BF16), _TN, preferred_element_type=F32)

        @pl.when(k == n_k - 1)
        def _():
            o_ref[...] = acc_ref[...].astype(BF16)

    return pl.pallas_call(
        body, name=name, grid=(m // tm, n_k),
        in_specs=[pl.BlockSpec((tk, tm), lambda i, k: (k, i)), pl.BlockSpec((tk, n), lambda i, k: (k, 0))],
        out_specs=pl.BlockSpec((tm, n), lambda i, k: (i, 0)),
        out_shape=jax.ShapeDtypeStruct((m, n), BF16),
        scratch_shapes=[pltpu.VMEM((tm, n), F32)],
        compiler_params=_params("parallel", "arbitrary"),
    )(a, b)


def _position():
    x, y, c = lax.axis_index("x"), lax.axis_index("y"), lax.axis_index("c")
    return x, y, c, 4 * x + 2 * y + c


def _peer(x, y, c, rel):
    return (x ^ ((rel >> 2) & 1), y ^ ((rel >> 1) & 1), c ^ (rel & 1))


def _exchange_out_shape(kind, arr):
    return jax.ShapeDtypeStruct(((N_DEV,) + arr.shape) if kind == "gather" else arr.shape, arr.dtype)


def _exchange_sems(n_items):
    return [pltpu.SemaphoreType.DMA((n_items, N_DEV)), pltpu.SemaphoreType.DMA((n_items, N_DEV)), pltpu.SemaphoreType.DMA((n_items,))]


def _exchange_copies(kinds, srcs, dsts, sems, arrivals):
    send_sems, recv_sems, local_sems = sems
    x, y, c, me = _position()
    local, sends, recvs = [], [], []
    for k, (kind, src, dst) in enumerate(zip(kinds, srcs, dsts)):
        own = src if kind == "gather" else src.at[me]
        local.append(pltpu.make_async_copy(own, dst.at[me], local_sems.at[k]))
        for rel in range(1, N_DEV):
            going = src if kind == "gather" else src.at[me ^ rel]
            common = dict(send_sem=send_sems.at[k, rel], recv_sem=recv_sems.at[k, rel],
                          device_id=_peer(x, y, c, rel), device_id_type=MESH)
            sends.append(pltpu.make_async_remote_copy(src_ref=going, dst_ref=dst.at[me], **common))
            if arrivals:
                recvs.append(pltpu.make_async_remote_copy(src_ref=own, dst_ref=dst.at[me ^ rel], **common))
    return local, sends, recvs


def _exchange_start(kinds, srcs, dsts, sems):
    local, sends, _ = _exchange_copies(kinds, srcs, dsts, sems, arrivals=False)
    for cp in local + sends:
        cp.start()


def _exchange_finish(kinds, srcs, dsts, sems):
    local, sends, recvs = _exchange_copies(kinds, srcs, dsts, sems, arrivals=True)
    for cp in recvs:
        cp.wait_recv()
    for cp in sends:
        cp.wait_send()
    for cp in local:
        cp.wait()


def _gather_two_level_with_rope_tables(shard, inv_freq, s_len, name):
    rows = ROW_TILE

    def body(inv_ref, src, cos_ref, sin_ref, dst, send_sems, recv_sems, local_sem):
        x, y, c, me = _position()
        sibling = (x, y, 1 - c)
        chips = [(1 - x, y), (x, 1 - y), (1 - x, 1 - y)]

        def block(px, py, pc):
            return dst.at[4 * px + 2 * py + pc]

        def copy(k, blk, to, src_ref=None):
            return pltpu.make_async_remote_copy(
                src_ref=block(*blk) if src_ref is None else src_ref, dst_ref=block(*blk),
                send_sem=send_sems.at[k], recv_sem=recv_sems.at[k], device_id=to, device_id_type=MESH)

        x_nbr, y_nbr, diag = chips
        mine = pltpu.make_async_copy(src, dst.at[me], local_sem)
        mine.start()
        first = [copy(0, (x, y, c), sibling, src), copy(1, (x, y, c), (*x_nbr, c), src), copy(2, (x, y, c), (*y_nbr, c), src)]
        for cp in first:
            cp.start()

        inv = inv_ref[...]
        lane = lax.broadcasted_iota(jnp.int32, (rows, 128), 1)
        sign = jnp.where((lane // (HEAD_DIM // 2)) % 2 == 0, -1.0, 1.0)
        row = lax.broadcasted_iota(jnp.int32, (rows, 128), 0)
        n_chunks = s_len // rows

        def fill_tables(lo, hi):
            @pl.loop(lo, hi)
            def _(i):
                at = pl.multiple_of(i * rows, rows)
                ang = (row + at).astype(F32) * inv
                cos_ref[pl.ds(at, rows), :] = jnp.cos(ang)
                sin_ref[pl.ds(at, rows), :] = jnp.sin(ang) * sign

        fill_tables(0, n_chunks // 2)
        passed = [copy(4 + j, (*chip, c), sibling) for j, chip in enumerate(chips)]
        copy(1, (*x_nbr, c), (x, y, c)).wait_recv()
        copy(2, (*y_nbr, c), (x, y, c)).wait_recv()

        @pl.when(c == 1)
        def _():
            copy(3, (*x_nbr, c), (*y_nbr, c)).start()

        @pl.when(c == 0)
        def _():
            copy(3, (*y_nbr, c), (*x_nbr, c)).start()

        passed[0].start()
        passed[1].start()
        fill_tables(n_chunks // 2, n_chunks)
        copy(3, (*diag, c), (x, y, c)).wait_recv()
        passed[2].start()
        copy(0, (x, y, 1 - c), (x, y, c)).wait_recv()
        for j, chip in enumerate(chips):
            copy(4 + j, (*chip, 1 - c), (x, y, c)).wait_recv()
        for cp in first + passed:
            cp.wait_send()
        copy(3, (*x_nbr, c), (*y_nbr, c)).wait_send()
        mine.wait()

    any_spec = pl.BlockSpec(memory_space=pl.ANY)
    vmem = pl.BlockSpec(memory_space=pltpu.VMEM)
    table = jax.ShapeDtypeStruct((s_len, 128), F32)
    return pl.pallas_call(
        body, name=name, in_specs=[vmem, any_spec], out_specs=[vmem, vmem, any_spec],
        out_shape=[table, table, _exchange_out_shape("gather", shard)],
        scratch_shapes=[pltpu.SemaphoreType.DMA((N_DEV - 1,)), pltpu.SemaphoreType.DMA((N_DEV - 1,)), pltpu.SemaphoreType.DMA],
        compiler_params=pltpu.CompilerParams(has_side_effects=True, vmem_limit_bytes=V7X_VMEM_LIMIT_BYTES),
    )(inv_freq, shard)


def _call(body, *, name, grid, in_specs, out_specs, out_shape, args, scratch_shapes=(), semantics, exchange=()):
    if not exchange:
        outs = pl.pallas_call(body, name=name, grid=grid, in_specs=in_specs, out_specs=out_specs, out_shape=out_shape,
                              scratch_shapes=list(scratch_shapes), compiler_params=_params(*semantics))(*args)
        return outs, []
    kinds = [k for k, _ in exchange]
    n_in, n_out, n_x, n_scr = len(in_specs), len(out_specs), len(exchange), len(scratch_shapes)

    def wrapped(*refs):
        ins, refs = refs[:n_in], refs[n_in:]
        srcs, refs = refs[:n_x], refs[n_x:]
        outs, refs = refs[:n_out], refs[n_out:]
        dsts, refs = refs[:n_x], refs[n_x:]
        scratch, sems = refs[:n_scr], refs[n_scr:]
        ids = [pl.program_id(a) for a in range(len(grid))]
        first = functools.reduce(jnp.logical_and, [i == 0 for i in ids])
        last = functools.reduce(jnp.logical_and, [i == g - 1 for i, g in zip(ids, grid)])

        @pl.when(first)
        def _():
            _exchange_start(kinds, srcs, dsts, sems)

        body(*ins, *outs, *scratch)

        @pl.when(last)
        def _():
            _exchange_finish(kinds, srcs, dsts, sems)

    any_spec = pl.BlockSpec(memory_space=pl.ANY)
    outs = pl.pallas_call(
        wrapped, name=name, grid=grid,
        in_specs=list(in_specs) + [any_spec] * n_x, out_specs=list(out_specs) + [any_spec] * n_x,
        out_shape=list(out_shape) + [_exchange_out_shape(k, a) for k, a in exchange],
        scratch_shapes=list(scratch_shapes) + _exchange_sems(n_x),
        compiler_params=pltpu.CompilerParams(dimension_semantics=("arbitrary",) * len(grid),
                                             vmem_limit_bytes=V7X_VMEM_LIMIT_BYTES, has_side_effects=True),
    )(*args, *[a for _, a in exchange])
    return outs[:n_out], outs[n_out:]


def _adamw_math(w, g, m, v):
    m = ADAM_B1 * m + (1.0 - ADAM_B1) * g
    v = ADAM_B2 * v + (1.0 - ADAM_B2) * (g * g)
    m_hat = m / (1.0 - ADAM_B1 ** ADAM_STEP)
    v_hat = v / (1.0 - ADAM_B2 ** ADAM_STEP)
    return -ADAM_LR * (m_hat / (jnp.sqrt(v_hat) + ADAM_EPS) + ADAM_WD * w), m, v


def _adamw(parts, w, m, v, name, exchange=()):
    rows, cols = w.shape
    tm = 256 if rows % 256 == 0 and rows > 256 else rows

    def body(p_ref, w_ref, m_ref, v_ref, g_ref, d_ref, nm_ref, nv_ref):
        g = p_ref[0].astype(F32)
        for j in range(1, N_DEV):
            g = g + p_ref[j].astype(F32)
        delta, nm, nv = _adamw_math(w_ref[...], g, m_ref[...], v_ref[...])
        g_ref[...] = g
        d_ref[...] = delta
        nm_ref[...] = nm
        nv_ref[...] = nv

    shard = jax.ShapeDtypeStruct((rows, cols), F32)
    return _call(
        body, name=name, grid=(rows // tm,),
        in_specs=[pl.BlockSpec((N_DEV, tm, cols), lambda i: (0, i, 0))] + [_rows(tm, cols)] * 3,
        out_specs=[_rows(tm, cols)] * 4,
        out_shape=[shard] * 4,
        semantics=("parallel",), args=(parts, w, m, v), exchange=exchange)


_SMALL = ("mix_norm_g", "sgu_w", "sgu_b", "sgu_norm_g", "out_norm_a", "out_norm_b", "ffn_norm_g", "ple_norm_g", "final_norm_g")
_BIG = ("w_in", "w_out", "w_gate", "w_up", "w_down", "w_ple_gate", "w_ple_proj")
_COLUMN_SHARDED = ("w_in", "w_gate", "w_up", "w_ple_proj")
_ORDER = ("mix_norm_g", "w_in", "sgu_w", "sgu_b", "sgu_norm_g", "out_norm_a", "out_norm_b", "w_out", "ffn_norm_g",
          "w_gate", "w_up", "w_down", "ple_norm_g", "w_ple_gate", "w_ple_proj", "final_norm_g")


def _pack_small(values, names=_SMALL):
    flat = jnp.concatenate([values[n].reshape(-1).astype(F32) for n in names])
    pad = (-flat.shape[0]) % (8 * 128)
    return jnp.pad(flat, (0, pad)).reshape(-1, 128)


def _unpack_small(packed, like):
    flat = packed.reshape(-1)
    out, at = {}, 0
    for n in _SMALL:
        size = like[n].size
        out[n] = flat[at:at + size].reshape(like[n].shape)
        at += size
    return out


def _own_orientation(name, value):
    return value[0].T if name in _COLUMN_SHARDED else value[0]


def _reference_orientation(name, value):
    return (value.T if name in _COLUMN_SHARDED else value)[None]


def _full_from_gathered(gathered):
    return gathered.reshape(N_DEV * gathered.shape[1], gathered.shape[2])


def _sliced_for_devices(grad):
    return grad.reshape(N_DEV, grad.shape[0] // N_DEV, grad.shape[1])


def _rope_inv_freq():
    half = HEAD_DIM // 2
    inv = ROPE_THETA ** (-jnp.arange(half, dtype=F32) / half)
    return jnp.tile(inv, 128 // half)[None, :]


def _forward_backward(x, p, target, small, shards):
    def gather(*names):
        return [("gather", shards[n]) for n in names]

    def scatter(**grads):
        return [("scatter", _sliced_for_devices(g)) for g in grads.values()]

    full, parts = {}, {}
    s_len = x.shape[0]
    cos, sin, got = _gather_two_level_with_rope_tables(shards["w_in"], _rope_inv_freq(), s_len, "gather_w_in")
    full["w_in"] = _full_from_gathered(got)

    g_mix, g_ffn, g_ple = small["mix_norm_g"], small["ffn_norm_g"], small["ple_norm_g"]
    g_fin = small["final_norm_g"].reshape(1, D_MODEL)
    sw, gs, ga, gb = small["sgu_w"], small["sgu_norm_g"], small["out_norm_a"], small["out_norm_b"]
    b2 = jnp.repeat(small["sgu_b"].T, HEAD_DIM, axis=1)
    head_sum = (jnp.arange(WIDTH_B)[:, None] // HEAD_DIM == jnp.arange(128)[None, :]).astype(BF16)
    n_br = len(DILATIONS)

    def arrived(names, got):
        for n, g in zip(names, got):
            full[n] = _full_from_gathered(g)

    (ua, hn1, *qkv), got = _inproj(x, g_mix, full["w_in"], cos, sin, exchange=gather("w_gate"))
    arrived(("w_gate",), got)
    ya_n = _sgu_fwd(ua, sw, b2, gs, ga)
    half = shards["w_up"].shape[0] // 2
    riders = [[("gather", shards["w_up"][:half])], [("gather", shards["w_up"][half:])], gather("w_out")]
    branch, got = [], []
    for i, d in enumerate(DILATIONS):
        o_l, g = _attn_fwd(qkv[i], d, exchange=riders[i])
        branch.append(o_l)
        got += g
    arrived(("w_up", "w_out"), [jnp.concatenate(got[:2], axis=1), got[2]])
    y, yb, *lse = _combine([o for o, _ in branch], [l for _, l in branch], ya_n, gb, head_sum.T)
    last_wave = ("w_down", "w_ple_gate", "w_ple_proj")
    (h1, a, b, act, hn2), got = _ffn_up(y, full["w_out"], x, g_ffn, full["w_gate"], full["w_up"], exchange=gather(*last_wave))
    arrived(last_wave, got)
    h2, h3, gate, pp, hn3 = _ffn_down_ple(act, full["w_down"], h1, g_ple, full["w_ple_gate"], p, full["w_ple_proj"])

    dh2, loss, d_fin, d_ple, g_ple_gate, g_ple_proj = _loss_ple_bwd(
        h3, target, g_fin, gate, pp, h2, g_ple, full["w_ple_gate"], hn3, p)
    g_down = _mm_tn(act, dh2, "dw_down")
    (da, db), (parts["w_down"],) = _ffn_down_bwd(dh2, full["w_down"], a, b, exchange=scatter(w_down=g_down))
    g_gate = _mm_tn(da, hn2, "dw_gate")
    g_up = _mm_tn(db, hn2, "dw_up")
    (dh1, d_ffn, g_out), (parts["w_gate"], parts["w_ple_gate"], parts["w_ple_proj"]) = _mm_norm_bwd(
        [(da, full["w_gate"]), (db, full["w_up"])], h1, g_ffn, dh2, "ffn_up_bwd",
        exchange=scatter(w_gate=g_gate, w_ple_gate=g_ple_gate, w_ple_proj=g_ple_proj), dw_lhs=y)
    dya_n, d_gb, *do_dd = _outproj_bwd(dh1, full["w_out"], yb, gb, head_sum)
    grads_b = []
    for i, d in enumerate(DILATIONS):
        g3, got = _attn_bwd(qkv[i], do_dd[i], lse[i], do_dd[n_br + i], d,
                            exchange=scatter(w_up=g_up, w_out=g_out) if i == 0 else ())
        grads_b.append(g3)
        if i == 0:
            parts["w_up"], parts["w_out"] = got
    dua, d_sw, d_b2, d_gs, d_ga = _sgu_bwd(ua, sw, b2, gs, ga, dya_n)
    early = {
        "sgu_w": d_sw, "sgu_b": d_b2.reshape(CHUNK, 4, HEAD_DIM).sum(axis=-1).T, "sgu_norm_g": d_gs, "out_norm_a": d_ga,
        "out_norm_b": d_gb, "ffn_norm_g": d_ffn, "ple_norm_g": d_ple, "final_norm_g": d_fin,
    }
    dproj, g_in, (early_parts,) = _dproj(
        dua, [g[0] for g in grads_b], [g[1] for g in grads_b], [g[2] for g in grads_b], cos, sin, hn1,
        exchange=[("gather", _pack_small(early, _SMALL[1:]))])
    (dx, d_mix), (parts["w_in"],) = _mm_norm_bwd(
        [(dproj, full["w_in"])], x, g_mix, dh1, "inproj_bwd", exchange=scatter(w_in=g_in))
    late = jnp.concatenate([_pack_small({"mix_norm_g": d_mix}, _SMALL[:1]), jnp.broadcast_to(loss, (8, 128))])
    return dx, parts, early_parts, late


def kernel(x, p, mix_norm_g, w_in, sgu_w, sgu_b, sgu_norm_g, out_norm_a, out_norm_b, w_out, ffn_norm_g, w_gate, w_up, w_down, ple_norm_g, w_ple_gate, w_ple_proj, final_norm_g, loss_target, m_mix_norm_g, m_w_in, m_sgu_w, m_sgu_b, m_sgu_norm_g, m_out_norm_a, m_out_norm_b, m_w_out, m_ffn_norm_g, m_w_gate, m_w_up, m_w_down, m_ple_norm_g, m_w_ple_gate, m_w_ple_proj, m_final_norm_g, v_mix_norm_g, v_w_in, v_sgu_w, v_sgu_b, v_sgu_norm_g, v_out_norm_a, v_out_norm_b, v_w_out, v_ffn_norm_g, v_w_gate, v_w_up, v_w_down, v_ple_norm_g, v_w_ple_gate, v_w_ple_proj, v_final_norm_g):
    given = dict(locals())
    weights = {n: given[n] for n in _ORDER}
    moments_m = {n: given["m_" + n] for n in _ORDER}
    moments_v = {n: given["v_" + n] for n in _ORDER}

    shards = {n: _own_orientation(n, weights[n]).astype(BF16) for n in _BIG}
    small = {n: (weights[n][0] if n in ("sgu_w", "sgu_b") else weights[n]) for n in _SMALL}

    dx, parts, early_parts, late = _forward_backward(x[0], p[0, 0], loss_target[0], small, shards)

    small_like = {n: weights[n] for n in _SMALL}
    grads, deltas, new_m, new_v = {}, {}, {}, {}
    for n in _BIG:
        outs, got = _adamw(parts[n], _own_orientation(n, weights[n]), _own_orientation(n, moments_m[n]),
                           _own_orientation(n, moments_v[n]), "adamw_" + n, exchange=[("gather", late)] if n == "w_in" else ())
        grads[n], deltas[n], new_m[n], new_v[n] = [_reference_orientation(n, o) for o in outs]
        if n == "w_in":
            (late_parts,) = got
    loss = jnp.sum(late_parts[:, 8, 0])
    small_parts = jnp.concatenate([late_parts[:, :8], early_parts], axis=1)
    (g, d, nm, nv), _ = _adamw(small_parts, _pack_small(small_like), _pack_small({n: moments_m[n] for n in _SMALL}),
                               _pack_small({n: moments_v[n] for n in _SMALL}), "adamw_small")
    for out, packed in ((grads, g), (deltas, d), (new_m, nm), (new_v, nv)):
        out.update(_unpack_small(packed, small_like))

    return (loss, dx[None], *[grads[n] for n in _ORDER], *[deltas[n] for n in _ORDER],
            *[new_m[n] for n in _ORDER], *[new_v[n] for n in _ORDER])
```

```python
import functools

import jax
import jax.numpy as jnp
from jax import lax
from jax.experimental import pallas as pl
from jax.experimental.pallas import tpu as pltpu

F32 = jnp.float32
BF16 = jnp.bfloat16

D_MODEL = 1024
WIDTH_A = 256
WIDTH_B = 768
D_FF = 2816
IN_COLS = 2 * WIDTH_A + 3 * WIDTH_B
PLE_DIM = 256
HEAD_DIM = 64
N_PAIRS = WIDTH_B // 128
CHUNK = 128
N_BACK = 128
DILATIONS = (1, 4, 16)
ROPE_THETA = 10000.0
EPS = 1e-6
N_DEV = 8

ADAM_LR = 0.001
ADAM_B1 = 0.9
ADAM_B2 = 0.999
ADAM_EPS = 1e-08
ADAM_WD = 0.01
ADAM_STEP = 10

V7X_VMEM_LIMIT_BYTES = 56 * 1024 * 1024
ROW_TILE = 512
MESH = pl.DeviceIdType.MESH
NEG = -1e30

_NT = (((1,), (1,)), ((), ()))
_TN = (((0,), (0,)), ((), ()))


def _params(*semantics):
    return pltpu.CompilerParams(dimension_semantics=semantics, vmem_limit_bytes=V7X_VMEM_LIMIT_BYTES)


def _rows(tm, width):
    return pl.BlockSpec((tm, width), lambda i: (i, 0))


def _whole(shape):
    return pl.BlockSpec(shape, lambda *_: (0,) * len(shape))


def _resident(shape):
    return pl.BlockSpec(shape, lambda *_: (0,) * len(shape), pipeline_mode=pl.Buffered(1))


def _gelu(x):
    t = jnp.tanh(0.7978845608028654 * (x + 0.044715 * (x * x * x)))
    return 0.5 * x * (1.0 + t)


def _gelu_grad(x):
    t = jnp.tanh(0.7978845608028654 * (x + 0.044715 * (x * x * x)))
    return 0.5 * (1.0 + t) + 0.5 * x * (1.0 - t * t) * (0.7978845608028654 * (1.0 + 3.0 * 0.044715 * (x * x)))


def _rstd(x):
    return lax.rsqrt(jnp.mean(x * x, axis=-1, keepdims=True) + EPS)


def _norm_bwd(dn, h, g, r=None):
    r = _rstd(h) if r is None else r
    n = h * r
    t = dn * g
    return r * (t - n * jnp.mean(t * n, axis=-1, keepdims=True)), dn * n


def _swap_halves(x, first_half):
    return jnp.where(first_half, pltpu.roll(x, 96, 1), pltpu.roll(x, 32, 1))


def _sub_spec(d, n_cb, tm):
    return pl.BlockSpec((d, n_cb, tm // d, 128), lambda i: (0, 0, i, 0))


def _sub_shape(s_len, d, n_cb, dtype):
    return jax.ShapeDtypeStruct((d, n_cb, s_len // d, 128), dtype)


def _to_sub(stage_ref, cb_src, out_ref, cb_dst, d, tm):
    slab = stage_ref.at[cb_src]
    for r in range(d):
        out_ref[r, cb_dst] = slab[pl.ds(r, tm // d, stride=d), :].astype(out_ref.dtype)


def _from_sub(in_ref, cb_src, stage_ref, cb_dst, d, tm):
    slab = stage_ref.at[cb_dst]
    for r in range(d):
        slab[pl.ds(r, tm // d, stride=d), :] = in_ref[r, cb_src].astype(F32)


def _inproj(x, g, w, cos, sin, exchange=()):
    s_len = x.shape[0]
    tm = ROW_TILE
    n_cb = 3 * N_PAIRS

    def body(x_ref, g_ref, w_ref, cos_ref, sin_ref, ua_ref, hn_ref, *rest):
        sub_refs, stage = rest[:-1], rest[-1]
        xf = x_ref[...]
        hn = (xf * _rstd(xf) * g_ref[...]).astype(BF16)
        hn_ref[...] = hn
        c = cos_ref[...]
        s = sin_ref[...]
        first_half = (lax.broadcasted_iota(jnp.int32, (tm, 128), 1) % HEAD_DIM) < HEAD_DIM // 2
        for col in range(0, IN_COLS, 512):
            width = min(512, IN_COLS - col)
            acc = lax.dot_general(hn, w_ref[col:col + width, :], _NT, preferred_element_type=F32)
            if col < 2 * WIDTH_A:
                ua_ref[:, col:col + width] = acc
                continue
            for part in range(width // 128):
                cb = (col - 2 * WIDTH_A) // 128 + part
                t = acc[:, part * 128:(part + 1) * 128]
                if cb < 2 * N_PAIRS:
                    t = (t * c + _swap_halves(t, first_half) * s) * (0.125 if cb < N_PAIRS else 1.0)
                stage[cb] = t
                sub_refs[0][0, cb] = t.astype(BF16)
        for cb in range(n_cb):
            for d, out_ref in zip(DILATIONS[1:], sub_refs[1:]):
                _to_sub(stage, cb, out_ref, cb, d, tm)

    return _call(
        body, name="inproj", grid=(s_len // tm,),
        in_specs=[_rows(tm, D_MODEL), _whole((1, D_MODEL)), _resident((IN_COLS, D_MODEL)), _rows(tm, 128), _rows(tm, 128)],
        out_specs=[_rows(tm, 2 * WIDTH_A), _rows(tm, D_MODEL)] + [_sub_spec(d, n_cb, tm) for d in DILATIONS],
        out_shape=[jax.ShapeDtypeStruct((s_len, 2 * WIDTH_A), F32), jax.ShapeDtypeStruct((s_len, D_MODEL), BF16)]
        + [_sub_shape(s_len, d, n_cb, BF16) for d in DILATIONS],
        scratch_shapes=[pltpu.VMEM((n_cb, tm, 128), F32)],
        semantics=("parallel",), args=(x, g, w, cos, sin), exchange=exchange)


def _sgu_mix_weights(sw_ref):
    keep = lax.broadcasted_iota(jnp.int32, (CHUNK, CHUNK), 0) >= lax.broadcasted_iota(jnp.int32, (CHUNK, CHUNK), 1)
    return [jnp.where(keep, sw_ref[h], 0.0).astype(BF16) for h in range(4)], keep


def _sgu_core(ua_ref, gs_ref):
    u = ua_ref[:, :WIDTH_A]
    va = ua_ref[:, WIDTH_A:]
    vg = _gelu(va)
    xc = vg - jnp.mean(vg, axis=-1, keepdims=True)
    rstd = lax.rsqrt(jnp.mean(xc * xc, axis=-1, keepdims=True) + EPS)
    xhat = xc * rstd
    return u, va, _gelu(u), xhat, rstd, xhat * gs_ref[...]


def _sgu_mix(wm, vnb, half, head_a):
    va, vb = _both_heads(vnb[:, half * 128:(half + 1) * 128], head_a)
    return (jnp.dot(wm[2 * half], va, preferred_element_type=F32)
            + jnp.dot(wm[2 * half + 1], vb, preferred_element_type=F32))


def _sgu_fwd(ua, sw, b2, gs, ga):
    s_len = ua.shape[0]
    tm = ROW_TILE

    def body(ua_ref, sw_ref, b2_ref, gs_ref, ga_ref, out_ref):
        _, _, ug, _, _, vn = _sgu_core(ua_ref, gs_ref)
        wm, _ = _sgu_mix_weights(sw_ref)
        head_a = lax.broadcasted_iota(jnp.int32, (CHUNK, 128), 1) < HEAD_DIM
        for c in range(tm // CHUNK):
            rows = slice(c * CHUNK, (c + 1) * CHUNK)
            vnb = vn[rows].astype(BF16)
            mixed = b2_ref[...] + jnp.concatenate([_sgu_mix(wm, vnb, half, head_a) for half in range(2)], axis=1)
            ya = ug[rows] * mixed
            out_ref[rows, :] = (ya * _rstd(ya) * ga_ref[...]).astype(BF16)

    return pl.pallas_call(
        body, name="sgu_fwd", grid=(s_len // tm,),
        in_specs=[_rows(tm, 2 * WIDTH_A), _whole((4, CHUNK, CHUNK)), _whole((CHUNK, WIDTH_A)), _whole((1, WIDTH_A)), _whole((1, WIDTH_A))],
        out_specs=_rows(tm, WIDTH_A),
        out_shape=jax.ShapeDtypeStruct((s_len, WIDTH_A), BF16),
        compiler_params=_params("parallel"),
    )(ua, sw, b2, gs, ga)


STEP_POSITIONS = 2 * ROW_TILE


def _attn_geometry(sd):
    tile = min(STEP_POSITIONS, sd)
    return tile, tile // CHUNK, sd // tile, STEP_POSITIONS // tile


PAIRS_PER_STEP = 6


def _attn_spec(cb0, rows, row_index, res):
    return pl.BlockSpec((res, PAIRS_PER_STEP, rows, 128), lambda r, g, n: (r, cb0 // PAIRS_PER_STEP + g, row_index(n), 0))


assert PAIRS_PER_STEP == N_PAIRS
assert DILATIONS[0] == 1


def _stats_spec(rows, row_index, res):
    return pl.BlockSpec((res, None, rows, 128), lambda r, g, n: (r, 0, row_index(n), 0))


def _stats_shape(sd, d):
    return jax.ShapeDtypeStruct((d, 1, sd, 128), F32)


def _both_heads(x, head_a):
    zero = jnp.zeros_like(x)
    return [jnp.where(head_a, x, zero), jnp.where(head_a, zero, x)]


def _attn_fwd(qkv, d, exchange=()):
    sd = qkv.shape[2]
    tile, nb, n_tiles, res = _attn_geometry(sd)

    def prev(n):
        return jnp.maximum(n * nb - 1, 0)

    def body(q_ref, k_ref, kp_ref, v_ref, vp_ref, o_ref, l_ref):
        for rr in range(res):
            for hp in range(PAIRS_PER_STEP):
                one_pair(hp, q_ref.at[rr, hp], k_ref.at[rr, hp], kp_ref.at[rr, hp], v_ref.at[rr, hp], vp_ref.at[rr, hp],
                         o_ref.at[rr, hp], l_ref.at[rr])

    def one_pair(hp, q_ref, k_ref, kp_ref, v_ref, vp_ref, o_ref, l_ref):
        n = pl.program_id(2)
        lane = lax.broadcasted_iota(jnp.int32, (CHUNK, 128), 1)
        head_a = lane < HEAD_DIM
        qi = lax.broadcasted_iota(jnp.int32, (2 * CHUNK, 2 * CHUNK), 0) % CHUNK
        kc = lax.broadcasted_iota(jnp.int32, (2 * CHUNK, 2 * CHUNK), 1)
        band = (kc >= qi) & (kc <= qi + N_BACK)
        for j in range(nb):
            rows = slice(j * CHUNK, (j + 1) * CHUNK)
            if j == 0:
                kcat = jnp.concatenate([kp_ref[...], k_ref[rows, :]], axis=0)
                vcat = jnp.concatenate([vp_ref[...], v_ref[rows, :]], axis=0)
                valid = band & jnp.logical_or(n > 0, kc >= CHUNK)
            else:
                kcat = k_ref[(j - 1) * CHUNK:(j + 1) * CHUNK, :]
                vcat = v_ref[(j - 1) * CHUNK:(j + 1) * CHUNK, :]
                valid = band
            q2 = jnp.concatenate(_both_heads(q_ref[rows, :], head_a), axis=0)
            s = lax.dot_general(q2, kcat, _NT, preferred_element_type=F32)
            s = jnp.where(valid, s, NEG)
            m = jnp.max(s, axis=-1, keepdims=True)
            p = jnp.exp(s - m)
            l = jnp.sum(p, axis=-1, keepdims=True)
            o2 = jnp.dot(p.astype(BF16), vcat, preferred_element_type=F32) / l
            lse2 = m + jnp.log(l)
            o_ref[rows, :] = jnp.where(head_a, o2[:CHUNK], o2[CHUNK:]).astype(BF16)
            others = l_ref[rows, :] if hp > 0 else jnp.zeros((CHUNK, 128), F32)
            l_ref[rows, :] = jnp.where(lane == 2 * hp, lse2[:CHUNK], jnp.where(lane == 2 * hp + 1, lse2[CHUNK:], others))

    same = lambda n: n
    return _call(
        body, name=f"attn_fwd_d{d}", grid=(d // res, N_PAIRS // PAIRS_PER_STEP, n_tiles),
        in_specs=[_attn_spec(0, tile, same, res), _attn_spec(N_PAIRS, tile, same, res), _attn_spec(N_PAIRS, CHUNK, prev, res),
                  _attn_spec(2 * N_PAIRS, tile, same, res), _attn_spec(2 * N_PAIRS, CHUNK, prev, res)],
        out_specs=[_attn_spec(0, tile, same, res), _stats_spec(tile, same, res)],
        out_shape=[jax.ShapeDtypeStruct((d, N_PAIRS, sd, 128), BF16), _stats_shape(sd, d)],
        semantics=("parallel", "parallel", "parallel"), args=(qkv, qkv, qkv, qkv, qkv), exchange=exchange)


def _combine(outs, lses, ya_n, gb, head_spread):
    s_len = ya_n.shape[0]
    tm = ROW_TILE
    n_br = len(DILATIONS)

    def body(*refs):
        o_refs, l_refs = refs[:n_br], refs[n_br:2 * n_br]
        ya_ref, gb_ref, spread_ref, y_ref, yb_ref = refs[2 * n_br:2 * n_br + 5]
        lse_refs = refs[2 * n_br + 5:3 * n_br + 5]
        o_nat, l_nat, lse_nat, w_wide = refs[3 * n_br + 5:]
        for i, d in enumerate(DILATIONS):
            _from_sub(l_refs[i], 0, l_nat, i, d, tm)
        ls = [l_nat[i] for i in range(n_br)]
        top = jnp.maximum(jnp.maximum(ls[0], ls[1]), ls[2])
        ws = [jnp.exp(l - top) for l in ls]
        den = ws[0] + ws[1] + ws[2]
        inv = 1.0 / den
        for i in range(n_br):
            w = ws[i] * inv
            hi = w.astype(BF16)
            lo = (w - hi.astype(F32)).astype(BF16)
            w_wide[i] = (jnp.dot(hi, spread_ref[...], preferred_element_type=F32)
                         + jnp.dot(lo, spread_ref[...], preferred_element_type=F32))
        lse_nat[0] = top + jnp.log(den)
        for d, lse_ref in zip(DILATIONS, lse_refs):
            _to_sub(lse_nat, 0, lse_ref, 0, d, tm)
        sumsq = jnp.zeros((tm, 1), F32)
        for cb in range(N_PAIRS):
            cols = slice(cb * 128, (cb + 1) * 128)
            yb = w_wide[0, :, cols] * o_refs[0][0, cb].astype(F32)
            for i, d in enumerate(DILATIONS[1:], start=1):
                _from_sub(o_refs[i], cb, o_nat, i, d, tm)
                yb = yb + w_wide[i, :, cols] * o_nat[i]
            yb_ref[:, cb * 128:(cb + 1) * 128] = yb
            sumsq = sumsq + jnp.sum(yb * yb, axis=-1, keepdims=True)
        r = lax.rsqrt(sumsq / WIDTH_B + EPS)
        y_ref[:, :WIDTH_A] = ya_ref[...]
        y_ref[:, WIDTH_A:] = (yb_ref[...] * r * gb_ref[...]).astype(BF16)

    stats = [_sub_spec(d, 1, tm) for d in DILATIONS]
    return _call(
        body, name="attn_combine", grid=(s_len // tm,),
        in_specs=[_sub_spec(d, N_PAIRS, tm) for d in DILATIONS] + stats
        + [_rows(tm, WIDTH_A), _whole((1, WIDTH_B)), _whole((128, WIDTH_B))],
        out_specs=[_rows(tm, D_MODEL), _rows(tm, WIDTH_B)] + stats,
        out_shape=[jax.ShapeDtypeStruct((s_len, D_MODEL), BF16), jax.ShapeDtypeStruct((s_len, WIDTH_B), F32)]
        + [_sub_shape(s_len, d, 1, F32) for d in DILATIONS],
        scratch_shapes=[pltpu.VMEM((n_br, tm, 128), F32), pltpu.VMEM((n_br, tm, 128), F32), pltpu.VMEM((1, tm, 128), F32),
                        pltpu.VMEM((n_br, tm, WIDTH_B), F32)],
        semantics=("parallel",), args=(*outs, *lses, ya_n, gb, head_spread))[0]


def _ffn_up(y, wout, x, g, wg, wu, exchange=()):
    s_len = x.shape[0]
    tm = ROW_TILE

    def body(y_ref, wo_ref, x_ref, g_ref, wg_ref, wu_ref, h_ref, a_ref, b_ref, act_ref, hn_ref):
        hf = x_ref[...] + jnp.dot(y_ref[...], wo_ref[...], preferred_element_type=F32)
        h_ref[...] = hf
        hn = (hf * _rstd(hf) * g_ref[...]).astype(BF16)
        hn_ref[...] = hn
        for col in range(0, D_FF, 256):
            cols = slice(col, col + 256)
            a = lax.dot_general(hn, wg_ref[cols, :], _NT, preferred_element_type=F32)
            b = lax.dot_general(hn, wu_ref[cols, :], _NT, preferred_element_type=F32)
            a_ref[:, cols] = a.astype(BF16)
            b_ref[:, cols] = b.astype(BF16)
            act_ref[:, cols] = (a * jax.nn.sigmoid(a) * b).astype(BF16)

    wide = jax.ShapeDtypeStruct((s_len, D_FF), BF16)
    return _call(
        body, name="ffn_up", grid=(s_len // tm,),
        in_specs=[_rows(tm, D_MODEL), _resident((D_MODEL, D_MODEL)), _rows(tm, D_MODEL), _whole((1, D_MODEL)),
                  _resident((D_FF, D_MODEL)), _resident((D_FF, D_MODEL))],
        out_specs=[_rows(tm, D_MODEL), _rows(tm, D_FF), _rows(tm, D_FF), _rows(tm, D_FF), _rows(tm, D_MODEL)],
        out_shape=[jax.ShapeDtypeStruct((s_len, D_MODEL), F32), wide, wide, wide, jax.ShapeDtypeStruct((s_len, D_MODEL), BF16)],
        semantics=("parallel",), args=(y, wout, x, g, wg, wu), exchange=exchange)


def _ffn_down_ple(act, wd, h1, g, wpg, p, wpp):
    s_len = h1.shape[0]
    tm = ROW_TILE

    def body(act_ref, wd_ref, h1_ref, g_ref, wpg_ref, p_ref, wpp_ref, h2_ref, h3_ref, gate_ref, pp_ref, hn_ref):
        hf = h1_ref[...] + jnp.dot(act_ref[...], wd_ref[...], preferred_element_type=F32)
        h2_ref[...] = hf
        hn = (hf * _rstd(hf) * g_ref[...]).astype(BF16)
        hn_ref[...] = hn
        gate = jax.nn.sigmoid(jnp.dot(hn, wpg_ref[...], preferred_element_type=F32))
        pp = lax.dot_general(p_ref[...].astype(BF16), wpp_ref[...], _NT, preferred_element_type=F32)
        h3_ref[...] = hf + gate * pp
        gate_ref[...] = gate.astype(BF16)
        pp_ref[...] = pp.astype(BF16)

    full = jax.ShapeDtypeStruct((s_len, D_MODEL), F32)
    half = jax.ShapeDtypeStruct((s_len, D_MODEL), BF16)
    return pl.pallas_call(
        body, name="ffn_down_ple", grid=(s_len // tm,),
        in_specs=[_rows(tm, D_FF), _resident((D_FF, D_MODEL)), _rows(tm, D_MODEL), _whole((1, D_MODEL)),
                  _resident((D_MODEL, D_MODEL)), _rows(tm, PLE_DIM), _resident((D_MODEL, PLE_DIM))],
        out_specs=[_rows(tm, D_MODEL)] * 5,
        out_shape=[full, full, half, half, half],
        compiler_params=_params("parallel"),
    )(act, wd, h1, g, wpg, p, wpp)


def _loss_ple_bwd(h3, target, gf, gate, pp, h2, g_ple, wpg, hn3, p):
    s_len = h3.shape[0]
    tm = ROW_TILE
    n_steps = s_len // tm

    def body(h_ref, t_ref, g_ref, gate_ref, pp_ref, h2_ref, gp_ref, w_ref, hn_ref, p_ref,
             dh2_ref, loss_ref, dg_ref, dgp_ref, dwg_ref, dwp_ref, acc_g, acc_p):
        step = pl.program_id(0)

        @pl.when(step == 0)
        def _():
            loss_ref[...] = jnp.zeros_like(loss_ref)
            dg_ref[...] = jnp.zeros_like(dg_ref)
            dgp_ref[...] = jnp.zeros_like(dgp_ref)
            acc_g[...] = jnp.zeros_like(acc_g)
            acc_p[...] = jnp.zeros_like(acc_p)

        hf = h_ref[...]
        gfv = g_ref[...]
        r = _rstd(hf)
        err = hf * r * gfv - t_ref[...]
        loss_ref[...] += 0.5 * jnp.sum(jnp.sum(err * err, axis=-1, keepdims=True), axis=0, keepdims=True) / D_MODEL
        dh, dg_rows = _norm_bwd(err / D_MODEL, hf, gfv, r)
        dg_ref[...] += jnp.sum(dg_rows, axis=0, keepdims=True)
        gate = gate_ref[...].astype(F32)
        dz = (dh * pp_ref[...].astype(F32) * gate * (1.0 - gate)).astype(BF16)
        dpp = (dh * gate).astype(BF16)
        dn = lax.dot_general(dz, w_ref[...], _NT, preferred_element_type=F32)
        dh2, dgp_rows = _norm_bwd(dn, h2_ref[...], gp_ref[...])
        dh2 = dh + dh2
        dh2_ref[...] = dh2
        dgp_ref[...] += jnp.sum(dgp_rows, axis=0, keepdims=True)
        acc_g[...] += lax.dot_general(hn_ref[...], dz, _TN, preferred_element_type=F32)
        acc_p[...] += lax.dot_general(dpp, p_ref[...].astype(BF16), _TN, preferred_element_type=F32)

        @pl.when(step == n_steps - 1)
        def _():
            dwg_ref[...] = acc_g[...].astype(BF16)
            dwp_ref[...] = acc_p[...].astype(BF16)

    gain = jax.ShapeDtypeStruct((1, D_MODEL), F32)
    return pl.pallas_call(
        body, name="loss_ple_bwd", grid=(n_steps,),
        in_specs=[_rows(tm, D_MODEL), _rows(tm, D_MODEL), _whole((1, D_MODEL)), _rows(tm, D_MODEL), _rows(tm, D_MODEL),
                  _rows(tm, D_MODEL), _whole((1, D_MODEL)), _resident((D_MODEL, D_MODEL)), _rows(tm, D_MODEL),
                  _rows(tm, PLE_DIM)],
        out_specs=[_rows(tm, D_MODEL), _whole((1, 128)), _whole((1, D_MODEL)), _whole((1, D_MODEL)),
                   _whole((D_MODEL, D_MODEL)), _whole((D_MODEL, PLE_DIM))],
        out_shape=[jax.ShapeDtypeStruct((s_len, D_MODEL), F32), jax.ShapeDtypeStruct((1, 128), F32), gain, gain,
                   jax.ShapeDtypeStruct((D_MODEL, D_MODEL), BF16), jax.ShapeDtypeStruct((D_MODEL, PLE_DIM), BF16)],
        scratch_shapes=[pltpu.VMEM((D_MODEL, D_MODEL), F32), pltpu.VMEM((D_MODEL, PLE_DIM), F32)],
        compiler_params=_params("arbitrary"),
    )(h3, target, gf, gate, pp, h2, g_ple, wpg, hn3, p)


def _mm_norm_bwd(parts, h, g, dres, name, exchange=(), dw_lhs=None):
    s_len = h.shape[0]
    tm = ROW_TILE
    n_parts = len(parts)
    n_steps = s_len // tm
    has_dw = dw_lhs is not None

    def body(*refs):
        a_refs = refs[0:2 * n_parts:2]
        w_refs = refs[1:2 * n_parts:2]
        h_ref, g_ref, r_ref = refs[2 * n_parts:2 * n_parts + 3]
        rest = refs[2 * n_parts + 3:]
        step = pl.program_id(0)
        if has_dw:
            lhs_ref, o_ref, dg_ref, dw_ref, acc_ref = rest
        else:
            o_ref, dg_ref = rest

        @pl.when(step == 0)
        def _():
            dg_ref[...] = jnp.zeros_like(dg_ref)
            if has_dw:
                acc_ref[...] = jnp.zeros_like(acc_ref)

        dn = jnp.dot(a_refs[0][...], w_refs[0][...], preferred_element_type=F32)
        for a_ref, w_ref in zip(a_refs[1:], w_refs[1:]):
            dn = dn + jnp.dot(a_ref[...], w_ref[...], preferred_element_type=F32)
        dh, dg_rows = _norm_bwd(dn, h_ref[...], g_ref[...])
        out = r_ref[...] + dh
        o_ref[...] = out
        dg_ref[...] += jnp.sum(dg_rows, axis=0, keepdims=True)
        if has_dw:
            acc_ref[...] += lax.dot_general(lhs_ref[...], out.astype(BF16), _TN, preferred_element_type=F32)

            @pl.when(step == n_steps - 1)
            def _():
                dw_ref[...] = acc_ref[...].astype(BF16)

    in_specs, args = [], []
    for a, w in parts:
        in_specs += [_rows(tm, a.shape[1]), _resident(w.shape)]
        args += [a, w]
    in_specs += [_rows(tm, D_MODEL), _whole((1, D_MODEL)), _rows(tm, D_MODEL)]
    args += [h, g, dres]
    out_specs = [_rows(tm, D_MODEL), _whole((1, D_MODEL))]
    out_shape = [jax.ShapeDtypeStruct((s_len, D_MODEL), F32), jax.ShapeDtypeStruct((1, D_MODEL), F32)]
    scratch = []
    if has_dw:
        m = dw_lhs.shape[1]
        in_specs.append(_rows(tm, m))
        args.append(dw_lhs)
        out_specs.append(_whole((m, D_MODEL)))
        out_shape.append(jax.ShapeDtypeStruct((m, D_MODEL), BF16))
        scratch.append(pltpu.VMEM((m, D_MODEL), F32))
    return _call(
        body, name=name, grid=(n_steps,), in_specs=in_specs, out_specs=out_specs, out_shape=out_shape,
        scratch_shapes=scratch, semantics=("arbitrary",), args=tuple(args), exchange=exchange)


def _ffn_down_bwd(dh, wdt, a, b, exchange=()):
    s_len = dh.shape[0]
    tm = ROW_TILE

    def body(dh_ref, w_ref, a_ref, b_ref, da_ref, db_ref):
        dhb = dh_ref[...].astype(BF16)
        for col in range(0, D_FF, 512):
            cols = slice(col, min(col + 512, D_FF))
            dact = lax.dot_general(dhb, w_ref[cols, :], _NT, preferred_element_type=F32)
            av = a_ref[:, cols].astype(F32)
            bv = b_ref[:, cols].astype(F32)
            sig = jax.nn.sigmoid(av)
            t = dact * sig
            silu = av * sig
            da_ref[:, cols] = (t * bv * (1.0 + av - silu)).astype(BF16)
            db_ref[:, cols] = (dact * silu).astype(BF16)

    wide = jax.ShapeDtypeStruct((s_len, D_FF), BF16)
    return _call(
        body, name="ffn_down_bwd", grid=(s_len // tm,),
        in_specs=[_rows(tm, D_MODEL), _resident((D_FF, D_MODEL)), _rows(tm, D_FF), _rows(tm, D_FF)],
        out_specs=[_rows(tm, D_FF), _rows(tm, D_FF)],
        out_shape=[wide, wide],
        semantics=("parallel",), args=(dh, wdt, a, b), exchange=exchange)


def _outproj_bwd(dh1, woutt, yb, gb, head_sum):
    s_len = dh1.shape[0]
    tm = ROW_TILE
    n_br = len(DILATIONS)

    def body(dh_ref, w_ref, yb_ref, gb_ref, e_ref, dya_ref, dgb_ref, *rest):
        do_refs, dd_refs = rest[:n_br], rest[n_br:2 * n_br]
        do_nat, dd_nat = rest[2 * n_br:]

        @pl.when(pl.program_id(0) == 0)
        def _():
            dgb_ref[...] = jnp.zeros_like(dgb_ref)

        dhb = dh_ref[...].astype(BF16)
        dya_ref[...] = lax.dot_general(dhb, w_ref[:WIDTH_A, :], _NT, preferred_element_type=F32)
        dyn = lax.dot_general(dhb, w_ref[WIDTH_A:, :], _NT, preferred_element_type=F32)
        ybv = yb_ref[...]
        dyb, dg_rows = _norm_bwd(dyn, ybv, gb_ref[...])
        dgb_ref[...] += jnp.sum(dg_rows, axis=0, keepdims=True)
        prod = dyb * ybv
        hi = prod.astype(BF16)
        lo = (prod - hi.astype(F32)).astype(BF16)
        dd_nat[0] = (jnp.dot(hi, e_ref[...], preferred_element_type=F32)
                     + jnp.dot(lo, e_ref[...], preferred_element_type=F32))
        for i, d in enumerate(DILATIONS):
            _to_sub(dd_nat, 0, dd_refs[i], 0, d, tm)
        for cb in range(N_PAIRS):
            piece = dyb[:, cb * 128:(cb + 1) * 128]
            do_nat[cb] = piece
            do_refs[0][0, cb] = piece.astype(BF16)
            for i, d in enumerate(DILATIONS[1:], start=1):
                _to_sub(do_nat, cb, do_refs[i], cb, d, tm)

    return _call(
        body, name="outproj_bwd", grid=(s_len // tm,),
        in_specs=[_rows(tm, D_MODEL), _resident((D_MODEL, D_MODEL)), _rows(tm, WIDTH_B), _whole((1, WIDTH_B)), _whole((WIDTH_B, 128))],
        out_specs=[_rows(tm, WIDTH_A), _whole((1, WIDTH_B))] + [_sub_spec(d, N_PAIRS, tm) for d in DILATIONS]
        + [_sub_spec(d, 1, tm) for d in DILATIONS],
        out_shape=[jax.ShapeDtypeStruct((s_len, WIDTH_A), F32), jax.ShapeDtypeStruct((1, WIDTH_B), F32)]
        + [_sub_shape(s_len, d, N_PAIRS, BF16) for d in DILATIONS] + [_sub_shape(s_len, d, 1, F32) for d in DILATIONS],
        scratch_shapes=[pltpu.VMEM((N_PAIRS, tm, 128), F32), pltpu.VMEM((1, tm, 128), F32)],
        semantics=("arbitrary",), args=(dh1, woutt, yb, gb, head_sum))[0]


def _attn_bwd(qkv, do, lse, dd, d, exchange=()):
    sd = qkv.shape[2]
    tile, nb, n_tiles, res = _attn_geometry(sd)
    last_block = sd // CHUNK - 1

    def nxt(n):
        return jnp.minimum((n + 1) * nb, last_block)

    def block(ref, next_ref, j):
        return ref[j * CHUNK:(j + 1) * CHUNK, :] if j < nb else next_ref[...]

    def body(q_ref, qn_ref, k_ref, v_ref, do_ref, don_ref, l_ref, ln_ref, dd_ref, ddn_ref,
             dq_ref, dk_ref, dv_ref, carry_ref):
        for rr in range(res):
            l_t = [block(l_ref.at[rr], ln_ref.at[rr], j).T for j in range(nb + 1)]
            dd_t = [block(dd_ref.at[rr], ddn_ref.at[rr], j).T for j in range(nb + 1)]
            for hp in range(PAIRS_PER_STEP):
                l_rows = [jnp.concatenate([t[2 * hp:2 * hp + 1, :], t[2 * hp + 1:2 * hp + 2, :]], axis=1) for t in l_t]
                dd_rows = [jnp.concatenate([t[2 * hp:2 * hp + 1, :], t[2 * hp + 1:2 * hp + 2, :]], axis=1) for t in dd_t]
                one_pair(q_ref.at[rr, hp], qn_ref.at[rr, hp], k_ref.at[rr, hp], v_ref.at[rr, hp], do_ref.at[rr, hp],
                         don_ref.at[rr, hp], l_rows, dd_rows, dq_ref.at[rr, hp], dk_ref.at[rr, hp], dv_ref.at[rr, hp],
                         carry_ref.at[rr, hp])

    def one_pair(q_ref, qn_ref, k_ref, v_ref, do_ref, don_ref, l_rows, dd_rows, dq_ref, dk_ref, dv_ref, carry_ref):
        n = pl.program_id(2)

        @pl.when(n == 0)
        def _():
            carry_ref[...] = jnp.zeros_like(carry_ref)

        head_a = lax.broadcasted_iota(jnp.int32, (CHUNK, 128), 1) < HEAD_DIM
        col = lax.broadcasted_iota(jnp.int32, (CHUNK, 4 * CHUNK), 1)
        qi = col % CHUNK
        ki = lax.broadcasted_iota(jnp.int32, (CHUNK, 4 * CHUNK), 0)
        is_after = col >= 2 * CHUNK
        mask = (is_after & (ki >= qi)) | (jnp.logical_not(is_after) & (qi >= ki))
        mask_last = mask & jnp.logical_or(jnp.logical_not(is_after), n < n_tiles - 1)
        dq_acc = [carry_ref[...]] + [jnp.zeros((CHUNK, 128), F32) for _ in range(nb)]

        q_st = [jnp.concatenate(_both_heads(block(q_ref, qn_ref, j), head_a), axis=0) for j in range(nb + 1)]
        do_st = [jnp.concatenate(_both_heads(block(do_ref, don_ref, j), head_a), axis=0) for j in range(nb + 1)]

        for j in range(nb):
            rows = slice(j * CHUNK, (j + 1) * CHUNK)
            kj = k_ref[rows, :]
            vj = v_ref[rows, :]
            msk = mask if j + 1 < nb else mask_last
            qs = jnp.concatenate([q_st[j], q_st[j + 1]], axis=0)
            dos = jnp.concatenate([do_st[j], do_st[j + 1]], axis=0)
            ls = jnp.concatenate([l_rows[j], l_rows[j + 1]], axis=1)
            dds = jnp.concatenate([dd_rows[j], dd_rows[j + 1]], axis=1)
            st = lax.dot_general(kj, qs, _NT, preferred_element_type=F32)
            pt = jnp.exp(jnp.where(msk, st - ls, NEG))
            dpt = lax.dot_general(vj, dos, _NT, preferred_element_type=F32)
            dst = (pt * (dpt - dds)).astype(BF16)
            dv_ref[rows, :] = jnp.dot(pt.astype(BF16), dos, preferred_element_type=F32).astype(BF16)
            dk_ref[rows, :] = jnp.dot(dst, qs, preferred_element_type=F32).astype(BF16)
            dqs = lax.dot_general(dst, kj, _TN, preferred_element_type=F32)
            dq_acc[j] = dq_acc[j] + jnp.where(head_a, dqs[:CHUNK], dqs[CHUNK:2 * CHUNK])
            dq_acc[j + 1] = dq_acc[j + 1] + jnp.where(head_a, dqs[2 * CHUNK:3 * CHUNK], dqs[3 * CHUNK:])
        for j in range(nb):
            dq_ref[j * CHUNK:(j + 1) * CHUNK, :] = dq_acc[j].astype(BF16)
        carry_ref[...] = dq_acc[nb]

    same = lambda n: n
    grad = jax.ShapeDtypeStruct((d, N_PAIRS, sd, 128), BF16)
    return _call(
        body, name=f"attn_bwd_d{d}", grid=(d // res, N_PAIRS // PAIRS_PER_STEP, n_tiles),
        in_specs=[_attn_spec(0, tile, same, res), _attn_spec(0, CHUNK, nxt, res), _attn_spec(N_PAIRS, tile, same, res),
                  _attn_spec(2 * N_PAIRS, tile, same, res), _attn_spec(0, tile, same, res), _attn_spec(0, CHUNK, nxt, res),
                  _stats_spec(tile, same, res), _stats_spec(CHUNK, nxt, res), _stats_spec(tile, same, res),
                  _stats_spec(CHUNK, nxt, res)],
        out_specs=[_attn_spec(0, tile, same, res)] * 3,
        out_shape=[grad, grad, grad],
        scratch_shapes=[pltpu.VMEM((res, PAIRS_PER_STEP, CHUNK, 128), F32)],
        semantics=("parallel", "parallel", "arbitrary"), args=(qkv, qkv, qkv, qkv, do, do, lse, lse, dd, dd), exchange=exchange)


def _sgu_bwd(ua, sw, b2, gs, ga, dya_n):
    s_len = ua.shape[0]
    tm = ROW_TILE

    def body(ua_ref, sw_ref, b2_ref, gs_ref, ga_ref, dy_ref, dua_ref, dsw_ref, db2_ref, dgs_ref, dga_ref):
        @pl.when(pl.program_id(0) == 0)
        def _():
            dsw_ref[...] = jnp.zeros_like(dsw_ref)
            db2_ref[...] = jnp.zeros_like(db2_ref)
            dgs_ref[...] = jnp.zeros_like(dgs_ref)
            dga_ref[...] = jnp.zeros_like(dga_ref)

        u, va, ug, xhat, rstd, vn = _sgu_core(ua_ref, gs_ref)
        wm, keep = _sgu_mix_weights(sw_ref)
        head_a = lax.broadcasted_iota(jnp.int32, (CHUNK, 128), 1) < HEAD_DIM
        gav = ga_ref[...]
        gsv = gs_ref[...]
        dga = jnp.zeros((1, WIDTH_A), F32)
        dgs = jnp.zeros((1, WIDTH_A), F32)
        db2 = jnp.zeros((CHUNK, WIDTH_A), F32)
        dsw = [jnp.zeros((CHUNK, CHUNK), F32) for _ in range(4)]
        for c in range(tm // CHUNK):
            rows = slice(c * CHUNK, (c + 1) * CHUNK)
            vnb = vn[rows].astype(BF16)
            mixed = b2_ref[...] + jnp.concatenate([_sgu_mix(wm, vnb, half, head_a) for half in range(2)], axis=1)
            ugc = ug[rows]
            dya, dga_rows = _norm_bwd(dy_ref[rows, :], ugc * mixed, gav)
            dga = dga + jnp.sum(dga_rows, axis=0, keepdims=True)
            dmixed = dya * ugc
            db2 = db2 + dmixed
            dmb = dmixed.astype(BF16)
            dvn_halves = []
            for half in range(2):
                lanes = slice(half * 128, (half + 1) * 128)
                dm_heads = _both_heads(dmb[:, lanes], head_a)
                dvn_half = jnp.zeros((CHUNK, 128), F32)
                for k, dmh in enumerate(dm_heads):
                    h = 2 * half + k
                    dsw[h] = dsw[h] + lax.dot_general(dmh, vnb[:, lanes], _NT, preferred_element_type=F32)
                    dvn_half = dvn_half + lax.dot_general(wm[h], dmh, _TN, preferred_element_type=F32)
                dvn_halves.append(dvn_half)
            dvn = jnp.concatenate(dvn_halves, axis=1)
            xh = xhat[rows]
            dgs = dgs + jnp.sum(dvn * xh, axis=0, keepdims=True)
            dxh = dvn * gsv
            dvg = rstd[rows] * (dxh - jnp.mean(dxh, axis=-1, keepdims=True) - xh * jnp.mean(dxh * xh, axis=-1, keepdims=True))
            dua_ref[rows, :WIDTH_A] = (dya * mixed * _gelu_grad(u[rows])).astype(BF16)
            dua_ref[rows, WIDTH_A:] = (dvg * _gelu_grad(va[rows])).astype(BF16)
        for h in range(4):
            dsw_ref[h] += jnp.where(keep, dsw[h], 0.0)
        db2_ref[...] += db2
        dgs_ref[...] += dgs
        dga_ref[...] += dga

    return pl.pallas_call(
        body, name="sgu_bwd", grid=(s_len // tm,),
        in_specs=[_rows(tm, 2 * WIDTH_A), _whole((4, CHUNK, CHUNK)), _whole((CHUNK, WIDTH_A)), _whole((1, WIDTH_A)),
                  _whole((1, WIDTH_A)), _rows(tm, WIDTH_A)],
        out_specs=[_rows(tm, 2 * WIDTH_A), _whole((4, CHUNK, CHUNK)), _whole((CHUNK, WIDTH_A)), _whole((1, WIDTH_A)), _whole((1, WIDTH_A))],
        out_shape=[jax.ShapeDtypeStruct((s_len, 2 * WIDTH_A), BF16), jax.ShapeDtypeStruct((4, CHUNK, CHUNK), F32),
                   jax.ShapeDtypeStruct((CHUNK, WIDTH_A), F32), jax.ShapeDtypeStruct((1, WIDTH_A), F32),
                   jax.ShapeDtypeStruct((1, WIDTH_A), F32)],
        compiler_params=_params("arbitrary"),
    )(ua, sw, b2, gs, ga, dya_n)


def _dproj(dua, dqs, dks, dvs, cos, sin, hn1, exchange=()):
    s_len = dua.shape[0]
    tm = ROW_TILE
    n_br = len(DILATIONS)
    n_tiles = s_len // tm
    chunk = 256

    def built(i):
        return jnp.minimum(i, n_tiles - 1)

    def accumulated(i):
        return jnp.maximum(i - 1, 0)

    def body(dua_ref, *rest):
        groups = [rest[g * n_br:(g + 1) * n_br] for g in range(3)]
        cos_ref, sin_ref, hn_ref, out_ref, dw_ref, acc, dw_acc, tile_a, tile_b, cast_buf, dw_sem = rest[3 * n_br:]
        step = pl.program_id(0)

        @pl.when(step == 0)
        def _():
            dw_acc[...] = jnp.zeros_like(dw_acc)
            tile_b[...] = jnp.zeros_like(tile_b)

        def run(build_into, accumulate_from):
            build_into[:, :2 * WIDTH_A] = dua_ref[...]
            c = cos_ref[...]
            s = sin_ref[...]
            first_half = (lax.broadcasted_iota(jnp.int32, (tm, 128), 1) % HEAD_DIM) < HEAD_DIM // 2
            for g, refs in enumerate(groups):
                for cb in range(N_PAIRS):
                    t = refs[0][0, cb].astype(F32)
                    for i, d in enumerate(DILATIONS[1:]):
                        _from_sub(refs[i + 1], cb, acc, i, d, tm)
                        t = t + acc[i]
                    if g < 2:
                        t = (t * c - _swap_halves(t, first_half) * s) * (0.125 if g == 0 else 1.0)
                    col = 2 * WIDTH_A + g * WIDTH_B + cb * 128
                    build_into[:, col:col + 128] = t.astype(BF16)
            hn = hn_ref[...]
            for j in range(IN_COLS // chunk):
                cols = slice(j * chunk, (j + 1) * chunk)
                dw_acc[cols, :] += lax.dot_general(accumulate_from[:, cols], hn, _TN, preferred_element_type=F32)
            out_ref[...] = build_into[...]

        @pl.when(step % 2 == 0)
        def _():
            run(tile_a, tile_b)

        @pl.when(step % 2 == 1)
        def _():
            run(tile_b, tile_a)

        @pl.when(step == n_tiles)
        def _():
            for j in range(IN_COLS // chunk):
                rows = slice(j * chunk, (j + 1) * chunk)
                cast_buf[...] = dw_acc[rows, :].astype(BF16)
                cp = pltpu.make_async_copy(cast_buf, dw_ref.at[rows, :], dw_sem)
                cp.start()
                cp.wait()

    def tile_rows(width, which):
        return pl.BlockSpec((tm, width), lambda i: (which(i), 0))

    subs = [pl.BlockSpec((d, N_PAIRS, tm // d, 128), lambda i: (0, 0, built(i), 0)) for d in DILATIONS]
    (dproj, dw), received = _call(
        body, name="dproj_dw_in", grid=(n_tiles + 1,),
        in_specs=[tile_rows(2 * WIDTH_A, built)] + subs * 3
        + [tile_rows(128, built), tile_rows(128, built), tile_rows(D_MODEL, accumulated)],
        out_specs=[tile_rows(IN_COLS, built), pl.BlockSpec(memory_space=pl.ANY)],
        out_shape=[jax.ShapeDtypeStruct((s_len, IN_COLS), BF16), jax.ShapeDtypeStruct((IN_COLS, D_MODEL), BF16)],
        scratch_shapes=[pltpu.VMEM((n_br - 1, tm, 128), F32), pltpu.VMEM((IN_COLS, D_MODEL), F32),
                        pltpu.VMEM((tm, IN_COLS), BF16), pltpu.VMEM((tm, IN_COLS), BF16),
                        pltpu.VMEM((chunk, D_MODEL), BF16), pltpu.SemaphoreType.DMA],
        semantics=("arbitrary",), args=(dua, *dqs, *dks, *dvs, cos, sin, hn1), exchange=exchange)
    return dproj, dw, received


def _mm_tn(a, b, name):
    s_len, m = a.shape
    n = b.shape[1]
    tk = 2 * ROW_TILE
    tm = m if m <= 512 else (1408 if m == D_FF else 512)
    n_k = s_len // tk

    def body(a_ref, b_ref, o_ref, acc_ref):
        k = pl.program_id(1)

        @pl.when(k == 0)
        def _():
            acc_ref[...] = jnp.zeros_like(acc_ref)

        acc_ref[...] += lax.dot_general(a_ref[...].astype(BF16), b_ref[...].astype(BF16), _TN, preferred_element_type=F32)

        @pl.when(k == n_k - 1)
        def _():
            o_ref[...] = acc_ref[...].astype(BF16)

    return pl.pallas_call(
        body, name=name, grid=(m // tm, n_k),
        in_specs=[pl.BlockSpec((tk, tm), lambda i, k: (k, i)), pl.BlockSpec((tk, n), lambda i, k: (k, 0))],
        out_specs=pl.BlockSpec((tm, n), lambda i, k: (i, 0)),
        out_shape=jax.ShapeDtypeStruct((m, n), BF16),
        scratch_shapes=[pltpu.VMEM((tm, n), F32)],
        compiler_params=_params("parallel", "arbitrary"),
    )(a, b)


def _position():
    x, y, c = lax.axis_index("x"), lax.axis_index("y"), lax.axis_index("c")
    return x, y, c, 4 * x + 2 * y + c


def _peer(x, y, c, rel):
    return (x ^ ((rel >> 2) & 1), y ^ ((rel >> 1) & 1), c ^ (rel & 1))


def _exchange_out_shape(kind, arr):
    return jax.ShapeDtypeStruct(((N_DEV,) + arr.shape) if kind == "gather" else arr.shape, arr.dtype)


def _exchange_sems(n_items):
    return [pltpu.SemaphoreType.DMA((n_items, N_DEV)), pltpu.SemaphoreType.DMA((n_items, N_DEV)), pltpu.SemaphoreType.DMA((n_items,))]


def _exchange_copies(kinds, srcs, dsts, sems, arrivals):
    send_sems, recv_sems, local_sems = sems
    x, y, c, me = _position()
    local, sends, recvs = [], [], []
    for k, (kind, src, dst) in enumerate(zip(kinds, srcs, dsts)):
        own = src if kind == "gather" else src.at[me]
        local.append(pltpu.make_async_copy(own, dst.at[me], local_sems.at[k]))
        for rel in range(1, N_DEV):
            going = src if kind == "gather" else src.at[me ^ rel]
            common = dict(send_sem=send_sems.at[k, rel], recv_sem=recv_sems.at[k, rel],
                          device_id=_peer(x, y, c, rel), device_id_type=MESH)
            sends.append(pltpu.make_async_remote_copy(src_ref=going, dst_ref=dst.at[me], **common))
            if arrivals:
                recvs.append(pltpu.make_async_remote_copy(src_ref=own, dst_ref=dst.at[me ^ rel], **common))
    return local, sends, recvs


def _exchange_start(kinds, srcs, dsts, sems):
    local, sends, _ = _exchange_copies(kinds, srcs, dsts, sems, arrivals=False)
    for cp in local + sends:
        cp.start()


def _exchange_finish(kinds, srcs, dsts, sems):
    local, sends, recvs = _exchange_copies(kinds, srcs, dsts, sems, arrivals=True)
    for cp in recvs:
        cp.wait_recv()
    for cp in sends:
        cp.wait_send()
    for cp in local:
        cp.wait()


def _gather_two_level_with_rope_tables(shard, inv_freq, s_len, name):
    rows = ROW_TILE

    def body(inv_ref, src, cos_ref, sin_ref, dst, send_sems, recv_sems, local_sem):
        x, y, c, me = _position()
        sibling = (x, y, 1 - c)
        chips = [(1 - x, y), (x, 1 - y), (1 - x, 1 - y)]

        def block(px, py, pc):
            return dst.at[4 * px + 2 * py + pc]

        def copy(k, blk, to, src_ref=None):
            return pltpu.make_async_remote_copy(
                src_ref=block(*blk) if src_ref is None else src_ref, dst_ref=block(*blk),
                send_sem=send_sems.at[k], recv_sem=recv_sems.at[k], device_id=to, device_id_type=MESH)

        x_nbr, y_nbr, diag = chips
        mine = pltpu.make_async_copy(src, dst.at[me], local_sem)
        mine.start()
        first = [copy(0, (x, y, c), sibling, src), copy(1, (x, y, c), (*x_nbr, c), src), copy(2, (x, y, c), (*y_nbr, c), src)]
        for cp in first:
            cp.start()

        inv = inv_ref[...]
        lane = lax.broadcasted_iota(jnp.int32, (rows, 128), 1)
        sign = jnp.where((lane // (HEAD_DIM // 2)) % 2 == 0, -1.0, 1.0)
        row = lax.broadcasted_iota(jnp.int32, (rows, 128), 0)
        n_chunks = s_len // rows

        def fill_tables(lo, hi):
            @pl.loop(lo, hi)
            def _(i):
                at = pl.multiple_of(i * rows, rows)
                ang = (row + at).astype(F32) * inv
                cos_ref[pl.ds(at, rows), :] = jnp.cos(ang)
                sin_ref[pl.ds(at, rows), :] = jnp.sin(ang) * sign

        fill_tables(0, n_chunks // 2)
        passed = [copy(4 + j, (*chip, c), sibling) for j, chip in enumerate(chips)]
        copy(1, (*x_nbr, c), (x, y, c)).wait_recv()
        copy(2, (*y_nbr, c), (x, y, c)).wait_recv()

        @pl.when(c == 1)
        def _():
            copy(3, (*x_nbr, c), (*y_nbr, c)).start()

        @pl.when(c == 0)
        def _():
            copy(3, (*y_nbr, c), (*x_nbr, c)).start()

        passed[0].start()
        passed[1].start()
        fill_tables(n_chunks // 2, n_chunks)
        copy(3, (*diag, c), (x, y, c)).wait_recv()
        passed[2].start()
        copy(0, (x, y, 1 - c), (x, y, c)).wait_recv()
        for j, chip in enumerate(chips):
            copy(4 + j, (*chip, 1 - c), (x, y, c)).wait_recv()
        for cp in first + passed:
            cp.wait_send()
        copy(3, (*x_nbr, c), (*y_nbr, c)).wait_send()
        mine.wait()

    any_spec = pl.BlockSpec(memory_space=pl.ANY)
    vmem = pl.BlockSpec(memory_space=pltpu.VMEM)
    table = jax.ShapeDtypeStruct((s_len, 128), F32)
    return pl.pallas_call(
        body, name=name, in_specs=[vmem, any_spec], out_specs=[vmem, vmem, any_spec],
        out_shape=[table, table, _exchange_out_shape("gather", shard)],
        scratch_shapes=[pltpu.SemaphoreType.DMA((N_DEV - 1,)), pltpu.SemaphoreType.DMA((N_DEV - 1,)), pltpu.SemaphoreType.DMA],
        compiler_params=pltpu.CompilerParams(has_side_effects=True, vmem_limit_bytes=V7X_VMEM_LIMIT_BYTES),
    )(inv_freq, shard)


def _call(body, *, name, grid, in_specs, out_specs, out_shape, args, scratch_shapes=(), semantics, exchange=()):
    if not exchange:
        outs = pl.pallas_call(body, name=name, grid=grid, in_specs=in_specs, out_specs=out_specs, out_shape=out_shape,
                              scratch_shapes=list(scratch_shapes), compiler_params=_params(*semantics))(*args)
        return outs, []
    kinds = [k for k, _ in exchange]
    n_in, n_out, n_x, n_scr = len(in_specs), len(out_specs), len(exchange), len(scratch_shapes)

    def wrapped(*refs):
        ins, refs = refs[:n_in], refs[n_in:]
        srcs, refs = refs[:n_x], refs[n_x:]
        outs, refs = refs[:n_out], refs[n_out:]
        dsts, refs = refs[:n_x], refs[n_x:]
        scratch, sems = refs[:n_scr], refs[n_scr:]
        ids = [pl.program_id(a) for a in range(len(grid))]
        first = functools.reduce(jnp.logical_and, [i == 0 for i in ids])
        last = functools.reduce(jnp.logical_and, [i == g - 1 for i, g in zip(ids, grid)])

        @pl.when(first)
        def _():
            _exchange_start(kinds, srcs, dsts, sems)

        body(*ins, *outs, *scratch)

        @pl.when(last)
        def _():
            _exchange_finish(kinds, srcs, dsts, sems)

    any_spec = pl.BlockSpec(memory_space=pl.ANY)
    outs = pl.pallas_call(
        wrapped, name=name, grid=grid,
        in_specs=list(in_specs) + [any_spec] * n_x, out_specs=list(out_specs) + [any_spec] * n_x,
        out_shape=list(out_shape) + [_exchange_out_shape(k, a) for k, a in exchange],
        scratch_shapes=list(scratch_shapes) + _exchange_sems(n_x),
        compiler_params=pltpu.CompilerParams(dimension_semantics=("arbitrary",) * len(grid),
                                             vmem_limit_bytes=V7X_VMEM_LIMIT_BYTES, has_side_effects=True),
    )(*args, *[a for _, a in exchange])
    return outs[:n_out], outs[n_out:]


def _adamw_math(w, g, m, v):
    m = ADAM_B1 * m + (1.0 - ADAM_B1) * g
    v = ADAM_B2 * v + (1.0 - ADAM_B2) * (g * g)
    m_hat = m / (1.0 - ADAM_B1 ** ADAM_STEP)
    v_hat = v / (1.0 - ADAM_B2 ** ADAM_STEP)
    return -ADAM_LR * (m_hat / (jnp.sqrt(v_hat) + ADAM_EPS) + ADAM_WD * w), m, v


def _adamw(parts, w, m, v, name, exchange=()):
    rows, cols = w.shape
    tm = 256 if rows % 256 == 0 and rows > 256 else rows

    def body(p_ref, w_ref, m_ref, v_ref, g_ref, d_ref, nm_ref, nv_ref):
        g = p_ref[0].astype(F32)
        for j in range(1, N_DEV):
            g = g + p_ref[j].astype(F32)
        delta, nm, nv = _adamw_math(w_ref[...], g, m_ref[...], v_ref[...])
        g_ref[...] = g
        d_ref[...] = delta
        nm_ref[...] = nm
        nv_ref[...] = nv

    shard = jax.ShapeDtypeStruct((rows, cols), F32)
    return _call(
        body, name=name, grid=(rows // tm,),
        in_specs=[pl.BlockSpec((N_DEV, tm, cols), lambda i: (0, i, 0))] + [_rows(tm, cols)] * 3,
        out_specs=[_rows(tm, cols)] * 4,
        out_shape=[shard] * 4,
        semantics=("parallel",), args=(parts, w, m, v), exchange=exchange)


_SMALL = ("mix_norm_g", "sgu_w", "sgu_b", "sgu_norm_g", "out_norm_a", "out_norm_b", "ffn_norm_g", "ple_norm_g", "final_norm_g")
_BIG = ("w_in", "w_out", "w_gate", "w_up", "w_down", "w_ple_gate", "w_ple_proj")
_COLUMN_SHARDED = ("w_in", "w_gate", "w_up", "w_ple_proj")
_ORDER = ("mix_norm_g", "w_in", "sgu_w", "sgu_b", "sgu_norm_g", "out_norm_a", "out_norm_b", "w_out", "ffn_norm_g",
          "w_gate", "w_up", "w_down", "ple_norm_g", "w_ple_gate", "w_ple_proj", "final_norm_g")


def _pack_small(values, names=_SMALL):
    flat = jnp.concatenate([values[n].reshape(-1).astype(F32) for n in names])
    pad = (-flat.shape[0]) % (8 * 128)
    return jnp.pad(flat, (0, pad)).reshape(-1, 128)


def _unpack_small(packed, like):
    flat = packed.reshape(-1)
    out, at = {}, 0
    for n in _SMALL:
        size = like[n].size
        out[n] = flat[at:at + size].reshape(like[n].shape)
        at += size
    return out


def _own_orientation(name, value):
    return value[0].T if name in _COLUMN_SHARDED else value[0]


def _reference_orientation(name, value):
    return (value.T if name in _COLUMN_SHARDED else value)[None]


def _full_from_gathered(gathered):
    return gathered.reshape(N_DEV * gathered.shape[1], gathered.shape[2])


def _sliced_for_devices(grad):
    return grad.reshape(N_DEV, grad.shape[0] // N_DEV, grad.shape[1])


def _rope_inv_freq():
    half = HEAD_DIM // 2
    inv = ROPE_THETA ** (-jnp.arange(half, dtype=F32) / half)
    return jnp.tile(inv, 128 // half)[None, :]


def _forward_backward(x, p, target, small, shards):
    def gather(*names):
        return [("gather", shards[n]) for n in names]

    def scatter(**grads):
        return [("scatter", _sliced_for_devices(g)) for g in grads.values()]

    full, parts = {}, {}
    s_len = x.shape[0]
    cos, sin, got = _gather_two_level_with_rope_tables(shards["w_in"], _rope_inv_freq(), s_len, "gather_w_in")
    full["w_in"] = _full_from_gathered(got)

    g_mix, g_ffn, g_ple = small["mix_norm_g"], small["ffn_norm_g"], small["ple_norm_g"]
    g_fin = small["final_norm_g"].reshape(1, D_MODEL)
    sw, gs, ga, gb = small["sgu_w"], small["sgu_norm_g"], small["out_norm_a"], small["out_norm_b"]
    b2 = jnp.repeat(small["sgu_b"].T, HEAD_DIM, axis=1)
    head_sum = (jnp.arange(WIDTH_B)[:, None] // HEAD_DIM == jnp.arange(128)[None, :]).astype(BF16)
    n_br = len(DILATIONS)

    def arrived(names, got):
        for n, g in zip(names, got):
            full[n] = _full_from_gathered(g)

    (ua, hn1, *qkv), got = _inproj(x, g_mix, full["w_in"], cos, sin, exchange=gather("w_gate"))
    arrived(("w_gate",), got)
    ya_n = _sgu_fwd(ua, sw, b2, gs, ga)
    half = shards["w_up"].shape[0] // 2
    riders = [[("gather", shards["w_up"][:half])], [("gather", shards["w_up"][half:])], gather("w_out")]
    branch, got = [], []
    for i, d in enumerate(DILATIONS):
        o_l, g = _attn_fwd(qkv[i], d, exchange=riders[i])
        branch.append(o_l)
        got += g
    arrived(("w_up", "w_out"), [jnp.concatenate(got[:2], axis=1), got[2]])
    y, yb, *lse = _combine([o for o, _ in branch], [l for _, l in branch], ya_n, gb, head_sum.T)
    last_wave = ("w_down", "w_ple_gate", "w_ple_proj")
    (h1, a, b, act, hn2), got = _ffn_up(y, full["w_out"], x, g_ffn, full["w_gate"], full["w_up"], exchange=gather(*last_wave))
    arrived(last_wave, got)
    h2, h3, gate, pp, hn3 = _ffn_down_ple(act, full["w_down"], h1, g_ple, full["w_ple_gate"], p, full["w_ple_proj"])

    dh2, loss, d_fin, d_ple, g_ple_gate, g_ple_proj = _loss_ple_bwd(
        h3, target, g_fin, gate, pp, h2, g_ple, full["w_ple_gate"], hn3, p)
    g_down = _mm_tn(act, dh2, "dw_down")
    (da, db), (parts["w_down"],) = _ffn_down_bwd(dh2, full["w_down"], a, b, exchange=scatter(w_down=g_down))
    g_gate = _mm_tn(da, hn2, "dw_gate")
    g_up = _mm_tn(db, hn2, "dw_up")
    (dh1, d_ffn, g_out), (parts["w_gate"], parts["w_ple_gate"], parts["w_ple_proj"]) = _mm_norm_bwd(
        [(da, full["w_gate"]), (db, full["w_up"])], h1, g_ffn, dh2, "ffn_up_bwd",
        exchange=scatter(w_gate=g_gate, w_ple_gate=g_ple_gate, w_ple_proj=g_ple_proj), dw_lhs=y)
    dya_n, d_gb, *do_dd = _outproj_bwd(dh1, full["w_out"], yb, gb, head_sum)
    grads_b = []
    for i, d in enumerate(DILATIONS):
        g3, got = _attn_bwd(qkv[i], do_dd[i], lse[i], do_dd[n_br + i], d,
                            exchange=scatter(w_up=g_up, w_out=g_out) if i == 0 else ())
        grads_b.append(g3)
        if i == 0:
            parts["w_up"], parts["w_out"] = got
    dua, d_sw, d_b2, d_gs, d_ga = _sgu_bwd(ua, sw, b2, gs, ga, dya_n)
    early = {
        "sgu_w": d_sw, "sgu_b": d_b2.reshape(CHUNK, 4, HEAD_DIM).sum(axis=-1).T, "sgu_norm_g": d_gs, "out_norm_a": d_ga,
        "out_norm_b": d_gb, "ffn_norm_g": d_ffn, "ple_norm_g": d_ple, "final_norm_g": d_fin,
    }
    dproj, g_in, (early_parts,) = _dproj(
        dua, [g[0] for g in grads_b], [g[1] for g in grads_b], [g[2] for g in grads_b], cos, sin, hn1,
        exchange=[("gather", _pack_small(early, _SMALL[1:]))])
    (dx, d_mix), (parts["w_in"],) = _mm_norm_bwd(
        [(dproj, full["w_in"])], x, g_mix, dh1, "inproj_bwd", exchange=scatter(w_in=g_in))
    late = jnp.concatenate([_pack_small({"mix_norm_g": d_mix}, _SMALL[:1]), jnp.broadcast_to(loss, (8, 128))])
    return dx, parts, early_parts, late


def kernel(x, p, mix_norm_g, w_in, sgu_w, sgu_b, sgu_norm_g, out_norm_a, out_norm_b, w_out, ffn_norm_g, w_gate, w_up, w_down, ple_norm_g, w_ple_gate, w_ple_proj, final_norm_g, loss_target, m_mix_norm_g, m_w_in, m_sgu_w, m_sgu_b, m_sgu_norm_g, m_out_norm_a, m_out_norm_b, m_w_out, m_ffn_norm_g, m_w_gate, m_w_up, m_w_down, m_ple_norm_g, m_w_ple_gate, m_w_ple_proj, m_final_norm_g, v_mix_norm_g, v_w_in, v_sgu_w, v_sgu_b, v_sgu_norm_g, v_out_norm_a, v_out_norm_b, v_w_out, v_ffn_norm_g, v_w_gate, v_w_up, v_w_down, v_ple_norm_g, v_w_ple_gate, v_w_ple_proj, v_final_norm_g):
    given = dict(locals())
    weights = {n: given[n] for n in _ORDER}
    moments_m = {n: given["m_" + n] for n in _ORDER}
    moments_v = {n: given["v_" + n] for n in _ORDER}

    shards = {n: _own_orientation(n, weights[n]).astype(BF16) for n in _BIG}
    small = {n: (weights[n][0] if n in ("sgu_w", "sgu_b") else weights[n]) for n in _SMALL}

    dx, parts, early_parts, late = _forward_backward(x[0], p[0, 0], loss_target[0], small, shards)

    small_like = {n: weights[n] for n in _SMALL}
    grads, deltas, new_m, new_v = {}, {}, {}, {}
    for n in _BIG:
        outs, got = _adamw(parts[n], _own_orientation(n, weights[n]), _own_orientation(n, moments_m[n]),
                           _own_orientation(n, moments_v[n]), "adamw_" + n, exchange=[("gather", late)] if n == "w_in" else ())
        grads[n], deltas[n], new_m[n], new_v[n] = [_reference_orientation(n, o) for o in outs]
        if n == "w_in":
            (late_parts,) = got
    loss = jnp.sum(late_parts[:, 8, 0])
    small_parts = jnp.concatenate([late_parts[:, :8], early_parts], axis=1)
    (g, d, nm, nv), _ = _adamw(small_parts, _pack_small(small_like), _pack_small({n: moments_m[n] for n in _SMALL}),
                               _pack_small({n: moments_v[n] for n in _SMALL}), "adamw_small")
    for out, packed in ((grads, g), (deltas, d), (new_m, nm), (new_v, nv)):
        out.update(_unpack_small(packed, small_like))

    return (loss, dx[None], *[grads[n] for n in _ORDER], *[deltas[n] for n in _ORDER],
            *[new_m[n] for n in _ORDER], *[new_v[n] for n in _ORDER])
```

```python
import functools

import jax
import jax.numpy as jnp
from jax import lax
from jax.experimental import pallas as pl
from jax.experimental.pallas import tpu as pltpu

F32 = jnp.float32
BF16 = jnp.bfloat16

D_MODEL = 1024
WIDTH_A = 256
WIDTH_B = 768
D_FF = 2816
IN_COLS = 2 * WIDTH_A + 3 * WIDTH_B
PLE_DIM = 256
HEAD_DIM = 64
N_PAIRS = WIDTH_B // 128
CHUNK = 128
N_BACK = 128
DILATIONS = (1, 4, 16)
ROPE_THETA = 10000.0
EPS = 1e-6
N_DEV = 8

ADAM_LR = 0.001
ADAM_B1 = 0.9
ADAM_B2 = 0.999
ADAM_EPS = 1e-08
ADAM_WD = 0.01
ADAM_STEP = 10

V7X_VMEM_LIMIT_BYTES = 56 * 1024 * 1024
ROW_TILE = 512
MESH = pl.DeviceIdType.MESH
NEG = -1e30

_NT = (((1,), (1,)), ((), ()))
_TN = (((0,), (0,)), ((), ()))


def _params(*semantics):
    return pltpu.CompilerParams(dimension_semantics=semantics, vmem_limit_bytes=V7X_VMEM_LIMIT_BYTES)


def _rows(tm, width):
    return pl.BlockSpec((tm, width), lambda i: (i, 0))


def _whole(shape):
    return pl.BlockSpec(shape, lambda *_: (0,) * len(shape))


def _resident(shape):
    return pl.BlockSpec(shape, lambda *_: (0,) * len(shape), pipeline_mode=pl.Buffered(1))


def _gelu_tanh(x):
    return jnp.tanh(0.7978845608028654 * (x + 0.044715 * (x * x * x)))


def _gelu(x, t):
    return 0.5 * x * (1.0 + t)


def _gelu_grad(x, t):
    return 0.5 * (1.0 + t) + 0.5 * x * (1.0 - t * t) * (0.7978845608028654 * (1.0 + 3.0 * 0.044715 * (x * x)))


def _rstd(x):
    return lax.rsqrt(jnp.mean(x * x, axis=-1, keepdims=True) + EPS)


def _norm_bwd(dn, h, g, r=None):
    r = _rstd(h) if r is None else r
    n = h * r
    t = dn * g
    return r * (t - n * jnp.mean(t * n, axis=-1, keepdims=True)), dn * n


def _swap_halves(x, first_half):
    return jnp.where(first_half, pltpu.roll(x, 96, 1), pltpu.roll(x, 32, 1))


def _sub_spec(d, n_cb, tm):
    return pl.BlockSpec((d, n_cb, tm // d, 128), lambda i: (0, 0, i, 0))


def _sub_shape(s_len, d, n_cb, dtype):
    return jax.ShapeDtypeStruct((d, n_cb, s_len // d, 128), dtype)


def _to_sub(stage_ref, cb_src, out_ref, cb_dst, d, tm):
    slab = stage_ref.at[cb_src]
    for r in range(d):
        out_ref[r, cb_dst] = slab[pl.ds(r, tm // d, stride=d), :].astype(out_ref.dtype)


def _from_sub(in_ref, cb_src, stage_ref, cb_dst, d, tm):
    slab = stage_ref.at[cb_dst]
    for r in range(d):
        slab[pl.ds(r, tm // d, stride=d), :] = in_ref[r, cb_src].astype(F32)


def _inproj(x, g, w, cos, sin, exchange=()):
    s_len = x.shape[0]
    tm = ROW_TILE
    n_cb = 3 * N_PAIRS

    def body(x_ref, g_ref, w_ref, cos_ref, sin_ref, ua_ref, hn_ref, *rest):
        sub_refs, stage = rest[:-1], rest[-1]
        xf = x_ref[...]
        hn = (xf * _rstd(xf) * g_ref[...]).astype(BF16)
        hn_ref[...] = hn
        c = cos_ref[...]
        s = sin_ref[...]
        first_half = (lax.broadcasted_iota(jnp.int32, (tm, 128), 1) % HEAD_DIM) < HEAD_DIM // 2
        for col in range(0, IN_COLS, 512):
            width = min(512, IN_COLS - col)
            acc = lax.dot_general(hn, w_ref[col:col + width, :], _NT, preferred_element_type=F32)
            if col < 2 * WIDTH_A:
                ua_ref[:, col:col + width] = acc
                continue
            for part in range(width // 128):
                cb = (col - 2 * WIDTH_A) // 128 + part
                t = acc[:, part * 128:(part + 1) * 128]
                if cb < 2 * N_PAIRS:
                    t = (t * c + _swap_halves(t, first_half) * s) * (0.125 if cb < N_PAIRS else 1.0)
                stage[cb] = t
                sub_refs[0][0, cb] = t.astype(BF16)
        for cb in range(n_cb):
            for d, out_ref in zip(DILATIONS[1:], sub_refs[1:]):
                _to_sub(stage, cb, out_ref, cb, d, tm)

    return _call(
        body, name="inproj", grid=(s_len // tm,),
        in_specs=[_rows(tm, D_MODEL), _whole((1, D_MODEL)), _resident((IN_COLS, D_MODEL)), _rows(tm, 128), _rows(tm, 128)],
        out_specs=[_rows(tm, 2 * WIDTH_A), _rows(tm, D_MODEL)] + [_sub_spec(d, n_cb, tm) for d in DILATIONS],
        out_shape=[jax.ShapeDtypeStruct((s_len, 2 * WIDTH_A), F32), jax.ShapeDtypeStruct((s_len, D_MODEL), BF16)]
        + [_sub_shape(s_len, d, n_cb, BF16) for d in DILATIONS],
        scratch_shapes=[pltpu.VMEM((n_cb, tm, 128), F32)],
        semantics=("parallel",), args=(x, g, w, cos, sin), exchange=exchange)


def _sgu_mix_weights(sw_ref):
    keep = lax.broadcasted_iota(jnp.int32, (CHUNK, CHUNK), 0) >= lax.broadcasted_iota(jnp.int32, (CHUNK, CHUNK), 1)
    return [jnp.where(keep, sw_ref[h], 0.0).astype(BF16) for h in range(4)], keep


def _sgu_core(ua_ref, gs_ref):
    u = ua_ref[:, :WIDTH_A]
    va = ua_ref[:, WIDTH_A:]
    tu, tv = _gelu_tanh(u), _gelu_tanh(va)
    vg = _gelu(va, tv)
    xc = vg - jnp.mean(vg, axis=-1, keepdims=True)
    rstd = lax.rsqrt(jnp.mean(xc * xc, axis=-1, keepdims=True) + EPS)
    xhat = xc * rstd
    return (u, tu), (va, tv), _gelu(u, tu), xhat, rstd, xhat * gs_ref[...]


def _sgu_mix(wm, vnb, half, head_a):
    va, vb = _both_heads(vnb[:, half * 128:(half + 1) * 128], head_a)
    return (jnp.dot(wm[2 * half], va, preferred_element_type=F32)
            + jnp.dot(wm[2 * half + 1], vb, preferred_element_type=F32))


def _sgu_fwd(ua, sw, b2, gs, ga):
    s_len = ua.shape[0]
    tm = ROW_TILE

    def body(ua_ref, sw_ref, b2_ref, gs_ref, ga_ref, out_ref):
        _, _, ug, _, _, vn = _sgu_core(ua_ref, gs_ref)
        wm, _ = _sgu_mix_weights(sw_ref)
        head_a = lax.broadcasted_iota(jnp.int32, (CHUNK, 128), 1) < HEAD_DIM
        for c in range(tm // CHUNK):
            rows = slice(c * CHUNK, (c + 1) * CHUNK)
            vnb = vn[rows].astype(BF16)
            mixed = b2_ref[...] + jnp.concatenate([_sgu_mix(wm, vnb, half, head_a) for half in range(2)], axis=1)
            ya = ug[rows] * mixed
            out_ref[rows, :] = (ya * _rstd(ya) * ga_ref[...]).astype(BF16)

    return pl.pallas_call(
        body, name="sgu_fwd", grid=(s_len // tm,),
        in_specs=[_rows(tm, 2 * WIDTH_A), _whole((4, CHUNK, CHUNK)), _whole((CHUNK, WIDTH_A)), _whole((1, WIDTH_A)), _whole((1, WIDTH_A))],
        out_specs=_rows(tm, WIDTH_A),
        out_shape=jax.ShapeDtypeStruct((s_len, WIDTH_A), BF16),
        compiler_params=_params("parallel"),
    )(ua, sw, b2, gs, ga)


STEP_POSITIONS = 2 * ROW_TILE


def _attn_geometry(sd):
    tile = min(STEP_POSITIONS, sd)
    return tile, tile // CHUNK, sd // tile, STEP_POSITIONS // tile


PAIRS_PER_STEP = 6


def _attn_spec(cb0, rows, row_index, res):
    return pl.BlockSpec((res, PAIRS_PER_STEP, rows, 128), lambda r, g, n: (r, cb0 // PAIRS_PER_STEP + g, row_index(n), 0))


assert PAIRS_PER_STEP == N_PAIRS
assert DILATIONS[0] == 1


def _stats_spec(rows, row_index, res):
    return pl.BlockSpec((res, None, rows, 128), lambda r, g, n: (r, 0, row_index(n), 0))


def _stats_shape(sd, d):
    return jax.ShapeDtypeStruct((d, 1, sd, 128), F32)


def _both_heads(x, head_a):
    zero = jnp.zeros_like(x)
    return [jnp.where(head_a, x, zero), jnp.where(head_a, zero, x)]


def _attn_fwd(qkv, d, exchange=()):
    sd = qkv.shape[2]
    tile, nb, n_tiles, res = _attn_geometry(sd)

    def prev(n):
        return jnp.maximum(n * nb - 1, 0)

    def body(q_ref, k_ref, kp_ref, v_ref, vp_ref, o_ref, l_ref):
        for rr in range(res):
            for hp in range(PAIRS_PER_STEP):
                one_pair(hp, q_ref.at[rr, hp], k_ref.at[rr, hp], kp_ref.at[rr, hp], v_ref.at[rr, hp], vp_ref.at[rr, hp],
                         o_ref.at[rr, hp], l_ref.at[rr])

    def one_pair(hp, q_ref, k_ref, kp_ref, v_ref, vp_ref, o_ref, l_ref):
        n = pl.program_id(2)
        lane = lax.broadcasted_iota(jnp.int32, (CHUNK, 128), 1)
        head_a = lane < HEAD_DIM
        qi = lax.broadcasted_iota(jnp.int32, (2 * CHUNK, 2 * CHUNK), 0) % CHUNK
        kc = lax.broadcasted_iota(jnp.int32, (2 * CHUNK, 2 * CHUNK), 1)
        band = (kc >= qi) & (kc <= qi + N_BACK)
        for j in range(nb):
            rows = slice(j * CHUNK, (j + 1) * CHUNK)
            if j == 0:
                kcat = jnp.concatenate([kp_ref[...], k_ref[rows, :]], axis=0)
                vcat = jnp.concatenate([vp_ref[...], v_ref[rows, :]], axis=0)
                valid = band & jnp.logical_or(n > 0, kc >= CHUNK)
            else:
                kcat = k_ref[(j - 1) * CHUNK:(j + 1) * CHUNK, :]
                vcat = v_ref[(j - 1) * CHUNK:(j + 1) * CHUNK, :]
                valid = band
            q2 = jnp.concatenate(_both_heads(q_ref[rows, :], head_a), axis=0)
            s = lax.dot_general(q2, kcat, _NT, preferred_element_type=F32)
            s = jnp.where(valid, s, NEG)
            m = jnp.max(s, axis=-1, keepdims=True)
            p = jnp.exp(s - m)
            l = jnp.sum(p, axis=-1, keepdims=True)
            o2 = jnp.dot(p.astype(BF16), vcat, preferred_element_type=F32) / l
            lse2 = m + jnp.log(l)
            o_ref[rows, :] = jnp.where(head_a, o2[:CHUNK], o2[CHUNK:]).astype(BF16)
            others = l_ref[rows, :] if hp > 0 else jnp.zeros((CHUNK, 128), F32)
            l_ref[rows, :] = jnp.where(lane == 2 * hp, lse2[:CHUNK], jnp.where(lane == 2 * hp + 1, lse2[CHUNK:], others))

    same = lambda n: n
    return _call(
        body, name=f"attn_fwd_d{d}", grid=(d // res, N_PAIRS // PAIRS_PER_STEP, n_tiles),
        in_specs=[_attn_spec(0, tile, same, res), _attn_spec(N_PAIRS, tile, same, res), _attn_spec(N_PAIRS, CHUNK, prev, res),
                  _attn_spec(2 * N_PAIRS, tile, same, res), _attn_spec(2 * N_PAIRS, CHUNK, prev, res)],
        out_specs=[_attn_spec(0, tile, same, res), _stats_spec(tile, same, res)],
        out_shape=[jax.ShapeDtypeStruct((d, N_PAIRS, sd, 128), BF16), _stats_shape(sd, d)],
        semantics=("parallel", "parallel", "parallel"), args=(qkv, qkv, qkv, qkv, qkv), exchange=exchange)


def _combine(outs, lses, ya_n, gb, head_spread):
    s_len = ya_n.shape[0]
    tm = ROW_TILE
    n_br = len(DILATIONS)

    def body(*refs):
        o_refs, l_refs = refs[:n_br], refs[n_br:2 * n_br]
        ya_ref, gb_ref, spread_ref, y_ref, yb_ref = refs[2 * n_br:2 * n_br + 5]
        lse_refs = refs[2 * n_br + 5:3 * n_br + 5]
        o_nat, l_nat, lse_nat, w_wide = refs[3 * n_br + 5:]
        for i, d in enumerate(DILATIONS):
            _from_sub(l_refs[i], 0, l_nat, i, d, tm)
        ls = [l_nat[i] for i in range(n_br)]
        top = jnp.maximum(jnp.maximum(ls[0], ls[1]), ls[2])
        ws = [jnp.exp(l - top) for l in ls]
        den = ws[0] + ws[1] + ws[2]
        inv = 1.0 / den
        for i in range(n_br):
            w = ws[i] * inv
            hi = w.astype(BF16)
            lo = (w - hi.astype(F32)).astype(BF16)
            w_wide[i] = (jnp.dot(hi, spread_ref[...], preferred_element_type=F32)
                         + jnp.dot(lo, spread_ref[...], preferred_element_type=F32))
        lse_nat[0] = top + jnp.log(den)
        for d, lse_ref in zip(DILATIONS, lse_refs):
            _to_sub(lse_nat, 0, lse_ref, 0, d, tm)
        sumsq = jnp.zeros((tm, 1), F32)
        for cb in range(N_PAIRS):
            cols = slice(cb * 128, (cb + 1) * 128)
            yb = w_wide[0, :, cols] * o_refs[0][0, cb].astype(F32)
            for i, d in enumerate(DILATIONS[1:], start=1):
                _from_sub(o_refs[i], cb, o_nat, i, d, tm)
                yb = yb + w_wide[i, :, cols] * o_nat[i]
            yb_ref[:, cb * 128:(cb + 1) * 128] = yb
            sumsq = sumsq + jnp.sum(yb * yb, axis=-1, keepdims=True)
        r = lax.rsqrt(sumsq / WIDTH_B + EPS)
        y_ref[:, :WIDTH_A] = ya_ref[...]
        y_ref[:, WIDTH_A:] = (yb_ref[...] * r * gb_ref[...]).astype(BF16)

    stats = [_sub_spec(d, 1, tm) for d in DILATIONS]
    return _call(
        body, name="attn_combine", grid=(s_len // tm,),
        in_specs=[_sub_spec(d, N_PAIRS, tm) for d in DILATIONS] + stats
        + [_rows(tm, WIDTH_A), _whole((1, WIDTH_B)), _whole((128, WIDTH_B))],
        out_specs=[_rows(tm, D_MODEL), _rows(tm, WIDTH_B)] + stats,
        out_shape=[jax.ShapeDtypeStruct((s_len, D_MODEL), BF16), jax.ShapeDtypeStruct((s_len, WIDTH_B), F32)]
        + [_sub_shape(s_len, d, 1, F32) for d in DILATIONS],
        scratch_shapes=[pltpu.VMEM((n_br, tm, 128), F32), pltpu.VMEM((n_br, tm, 128), F32), pltpu.VMEM((1, tm, 128), F32),
                        pltpu.VMEM((n_br, tm, WIDTH_B), F32)],
        semantics=("parallel",), args=(*outs, *lses, ya_n, gb, head_spread))[0]


def _ffn_up(y, wout, x, g, wg, wu, exchange=()):
    s_len = x.shape[0]
    tm = ROW_TILE

    def body(y_ref, wo_ref, x_ref, g_ref, wg_ref, wu_ref, h_ref, a_ref, b_ref, act_ref, hn_ref):
        hf = x_ref[...] + jnp.dot(y_ref[...], wo_ref[...], preferred_element_type=F32)
        h_ref[...] = hf
        hn = (hf * _rstd(hf) * g_ref[...]).astype(BF16)
        hn_ref[...] = hn
        for col in range(0, D_FF, 256):
            cols = slice(col, col + 256)
            a = lax.dot_general(hn, wg_ref[cols, :], _NT, preferred_element_type=F32)
            b = lax.dot_general(hn, wu_ref[cols, :], _NT, preferred_element_type=F32)
            a_ref[:, cols] = a.astype(BF16)
            b_ref[:, cols] = b.astype(BF16)
            act_ref[:, cols] = (a * jax.nn.sigmoid(a) * b).astype(BF16)

    wide = jax.ShapeDtypeStruct((s_len, D_FF), BF16)
    return _call(
        body, name="ffn_up", grid=(s_len // tm,),
        in_specs=[_rows(tm, D_MODEL), _resident((D_MODEL, D_MODEL)), _rows(tm, D_MODEL), _whole((1, D_MODEL)),
                  _resident((D_FF, D_MODEL)), _resident((D_FF, D_MODEL))],
        out_specs=[_rows(tm, D_MODEL), _rows(tm, D_FF), _rows(tm, D_FF), _rows(tm, D_FF), _rows(tm, D_MODEL)],
        out_shape=[jax.ShapeDtypeStruct((s_len, D_MODEL), F32), wide, wide, wide, jax.ShapeDtypeStruct((s_len, D_MODEL), BF16)],
        semantics=("parallel",), args=(y, wout, x, g, wg, wu), exchange=exchange)


def _ffn_down_ple(act, wd, h1, g, wpg, p, wpp):
    s_len = h1.shape[0]
    tm = ROW_TILE

    def body(act_ref, wd_ref, h1_ref, g_ref, wpg_ref, p_ref, wpp_ref, h2_ref, h3_ref, gate_ref, pp_ref, hn_ref):
        hf = h1_ref[...] + jnp.dot(act_ref[...], wd_ref[...], preferred_element_type=F32)
        h2_ref[...] = hf
        hn = (hf * _rstd(hf) * g_ref[...]).astype(BF16)
        hn_ref[...] = hn
        gate = jax.nn.sigmoid(jnp.dot(hn, wpg_ref[...], preferred_element_type=F32))
        pp = lax.dot_general(p_ref[...].astype(BF16), wpp_ref[...], _NT, preferred_element_type=F32)
        h3_ref[...] = hf + gate * pp
        gate_ref[...] = gate.astype(BF16)
        pp_ref[...] = pp.astype(BF16)

    full = jax.ShapeDtypeStruct((s_len, D_MODEL), F32)
    half = jax.ShapeDtypeStruct((s_len, D_MODEL), BF16)
    return pl.pallas_call(
        body, name="ffn_down_ple", grid=(s_len // tm,),
        in_specs=[_rows(tm, D_FF), _resident((D_FF, D_MODEL)), _rows(tm, D_MODEL), _whole((1, D_MODEL)),
                  _resident((D_MODEL, D_MODEL)), _rows(tm, PLE_DIM), _resident((D_MODEL, PLE_DIM))],
        out_specs=[_rows(tm, D_MODEL)] * 5,
        out_shape=[full, full, half, half, half],
        compiler_params=_params("parallel"),
    )(act, wd, h1, g, wpg, p, wpp)


def _loss_ple_bwd(h3, target, gf, gate, pp, h2, g_ple, wpg, hn3, p):
    s_len = h3.shape[0]
    tm = ROW_TILE
    n_steps = s_len // tm

    def body(h_ref, t_ref, g_ref, gate_ref, pp_ref, h2_ref, gp_ref, w_ref, hn_ref, p_ref,
             dh2_ref, loss_ref, dg_ref, dgp_ref, dwg_ref, dwp_ref, acc_g, acc_p):
        step = pl.program_id(0)

        @pl.when(step == 0)
        def _():
            loss_ref[...] = jnp.zeros_like(loss_ref)
            dg_ref[...] = jnp.zeros_like(dg_ref)
            dgp_ref[...] = jnp.zeros_like(dgp_ref)
            acc_g[...] = jnp.zeros_like(acc_g)
            acc_p[...] = jnp.zeros_like(acc_p)

        hf = h_ref[...]
        gfv = g_ref[...]
        r = _rstd(hf)
        err = hf * r * gfv - t_ref[...]
        loss_ref[...] += 0.5 * jnp.sum(jnp.sum(err * err, axis=-1, keepdims=True), axis=0, keepdims=True) / D_MODEL
        dh, dg_rows = _norm_bwd(err / D_MODEL, hf, gfv, r)
        dg_ref[...] += jnp.sum(dg_rows, axis=0, keepdims=True)
        gate = gate_ref[...].astype(F32)
        dz = (dh * pp_ref[...].astype(F32) * gate * (1.0 - gate)).astype(BF16)
        dpp = (dh * gate).astype(BF16)
        dn = lax.dot_general(dz, w_ref[...], _NT, preferred_element_type=F32)
        dh2, dgp_rows = _norm_bwd(dn, h2_ref[...], gp_ref[...])
        dh2 = dh + dh2
        dh2_ref[...] = dh2
        dgp_ref[...] += jnp.sum(dgp_rows, axis=0, keepdims=True)
        acc_g[...] += lax.dot_general(hn_ref[...], dz, _TN, preferred_element_type=F32)
        acc_p[...] += lax.dot_general(dpp, p_ref[...].astype(BF16), _TN, preferred_element_type=F32)

        @pl.when(step == n_steps - 1)
        def _():
            dwg_ref[...] = acc_g[...].astype(BF16)
            dwp_ref[...] = acc_p[...].astype(BF16)

    gain = jax.ShapeDtypeStruct((1, D_MODEL), F32)
    return pl.pallas_call(
        body, name="loss_ple_bwd", grid=(n_steps,),
        in_specs=[_rows(tm, D_MODEL), _rows(tm, D_MODEL), _whole((1, D_MODEL)), _rows(tm, D_MODEL), _rows(tm, D_MODEL),
                  _rows(tm, D_MODEL), _whole((1, D_MODEL)), _resident((D_MODEL, D_MODEL)), _rows(tm, D_MODEL),
                  _rows(tm, PLE_DIM)],
        out_specs=[_rows(tm, D_MODEL), _whole((1, 128)), _whole((1, D_MODEL)), _whole((1, D_MODEL)),
                   _whole((D_MODEL, D_MODEL)), _whole((D_MODEL, PLE_DIM))],
        out_shape=[jax.ShapeDtypeStruct((s_len, D_MODEL), F32), jax.ShapeDtypeStruct((1, 128), F32), gain, gain,
                   jax.ShapeDtypeStruct((D_MODEL, D_MODEL), BF16), jax.ShapeDtypeStruct((D_MODEL, PLE_DIM), BF16)],
        scratch_shapes=[pltpu.VMEM((D_MODEL, D_MODEL), F32), pltpu.VMEM((D_MODEL, PLE_DIM), F32)],
        compiler_params=_params("arbitrary"),
    )(h3, target, gf, gate, pp, h2, g_ple, wpg, hn3, p)


def _mm_norm_bwd(parts, h, g, dres, name, exchange=(), dw_lhs=None):
    s_len = h.shape[0]
    tm = ROW_TILE
    n_parts = len(parts)
    n_steps = s_len // tm
    has_dw = dw_lhs is not None

    def body(*refs):
        a_refs = refs[0:2 * n_parts:2]
        w_refs = refs[1:2 * n_parts:2]
        h_ref, g_ref, r_ref = refs[2 * n_parts:2 * n_parts + 3]
        rest = refs[2 * n_parts + 3:]
        step = pl.program_id(0)
        if has_dw:
            lhs_ref, o_ref, dg_ref, dw_ref, acc_ref = rest
        else:
            o_ref, dg_ref = rest

        @pl.when(step == 0)
        def _():
            dg_ref[...] = jnp.zeros_like(dg_ref)
            if has_dw:
                acc_ref[...] = jnp.zeros_like(acc_ref)

        dn = jnp.dot(a_refs[0][...], w_refs[0][...], preferred_element_type=F32)
        for a_ref, w_ref in zip(a_refs[1:], w_refs[1:]):
            dn = dn + jnp.dot(a_ref[...], w_ref[...], preferred_element_type=F32)
        dh, dg_rows = _norm_bwd(dn, h_ref[...], g_ref[...])
        out = r_ref[...] + dh
        o_ref[...] = out
        dg_ref[...] += jnp.sum(dg_rows, axis=0, keepdims=True)
        if has_dw:
            acc_ref[...] += lax.dot_general(lhs_ref[...], out.astype(BF16), _TN, preferred_element_type=F32)

            @pl.when(step == n_steps - 1)
            def _():
                dw_ref[...] = acc_ref[...].astype(BF16)

    in_specs, args = [], []
    for a, w in parts:
        in_specs += [_rows(tm, a.shape[1]), _resident(w.shape)]
        args += [a, w]
    in_specs += [_rows(tm, D_MODEL), _whole((1, D_MODEL)), _rows(tm, D_MODEL)]
    args += [h, g, dres]
    out_specs = [_rows(tm, D_MODEL), _whole((1, D_MODEL))]
    out_shape = [jax.ShapeDtypeStruct((s_len, D_MODEL), F32), jax.ShapeDtypeStruct((1, D_MODEL), F32)]
    scratch = []
    if has_dw:
        m = dw_lhs.shape[1]
        in_specs.append(_rows(tm, m))
        args.append(dw_lhs)
        out_specs.append(_whole((m, D_MODEL)))
        out_shape.append(jax.ShapeDtypeStruct((m, D_MODEL), BF16))
        scratch.append(pltpu.VMEM((m, D_MODEL), F32))
    return _call(
        body, name=name, grid=(n_steps,), in_specs=in_specs, out_specs=out_specs, out_shape=out_shape,
        scratch_shapes=scratch, semantics=("arbitrary",), args=tuple(args), exchange=exchange)


def _ffn_down_bwd(dh, wdt, a, b, exchange=()):
    s_len = dh.shape[0]
    tm = ROW_TILE

    def body(dh_ref, w_ref, a_ref, b_ref, da_ref, db_ref):
        dhb = dh_ref[...].astype(BF16)
        for col in range(0, D_FF, 512):
            cols = slice(col, min(col + 512, D_FF))
            dact = lax.dot_general(dhb, w_ref[cols, :], _NT, preferred_element_type=F32)
            av = a_ref[:, cols].astype(F32)
            bv = b_ref[:, cols].astype(F32)
            sig = jax.nn.sigmoid(av)
            t = dact * sig
            silu = av * sig
            da_ref[:, cols] = (t * bv * (1.0 + av - silu)).astype(BF16)
            db_ref[:, cols] = (dact * silu).astype(BF16)

    wide = jax.ShapeDtypeStruct((s_len, D_FF), BF16)
    return _call(
        body, name="ffn_down_bwd", grid=(s_len // tm,),
        in_specs=[_rows(tm, D_MODEL), _resident((D_FF, D_MODEL)), _rows(tm, D_FF), _rows(tm, D_FF)],
        out_specs=[_rows(tm, D_FF), _rows(tm, D_FF)],
        out_shape=[wide, wide],
        semantics=("parallel",), args=(dh, wdt, a, b), exchange=exchange)


def _outproj_bwd(dh1, woutt, yb, gb, head_sum):
    s_len = dh1.shape[0]
    tm = ROW_TILE
    n_br = len(DILATIONS)

    def body(dh_ref, w_ref, yb_ref, gb_ref, e_ref, dya_ref, dgb_ref, *rest):
        do_refs, dd_refs = rest[:n_br], rest[n_br:2 * n_br]
        do_nat, dd_nat = rest[2 * n_br:]

        @pl.when(pl.program_id(0) == 0)
        def _():
            dgb_ref[...] = jnp.zeros_like(dgb_ref)

        dhb = dh_ref[...].astype(BF16)
        dya_ref[...] = lax.dot_general(dhb, w_ref[:WIDTH_A, :], _NT, preferred_element_type=F32)
        dyn = lax.dot_general(dhb, w_ref[WIDTH_A:, :], _NT, preferred_element_type=F32)
        ybv = yb_ref[...]
        dyb, dg_rows = _norm_bwd(dyn, ybv, gb_ref[...])
        dgb_ref[...] += jnp.sum(dg_rows, axis=0, keepdims=True)
        prod = dyb * ybv
        hi = prod.astype(BF16)
        lo = (prod - hi.astype(F32)).astype(BF16)
        dd_nat[0] = (jnp.dot(hi, e_ref[...], preferred_element_type=F32)
                     + jnp.dot(lo, e_ref[...], preferred_element_type=F32))
        for i, d in enumerate(DILATIONS):
            _to_sub(dd_nat, 0, dd_refs[i], 0, d, tm)
        for cb in range(N_PAIRS):
            piece = dyb[:, cb * 128:(cb + 1) * 128]
            do_nat[cb] = piece
            do_refs[0][0, cb] = piece.astype(BF16)
            for i, d in enumerate(DILATIONS[1:], start=1):
                _to_sub(do_nat, cb, do_refs[i], cb, d, tm)

    return _call(
        body, name="outproj_bwd", grid=(s_len // tm,),
        in_specs=[_rows(tm, D_MODEL), _resident((D_MODEL, D_MODEL)), _rows(tm, WIDTH_B), _whole((1, WIDTH_B)), _whole((WIDTH_B, 128))],
        out_specs=[_rows(tm, WIDTH_A), _whole((1, WIDTH_B))] + [_sub_spec(d, N_PAIRS, tm) for d in DILATIONS]
        + [_sub_spec(d, 1, tm) for d in DILATIONS],
        out_shape=[jax.ShapeDtypeStruct((s_len, WIDTH_A), F32), jax.ShapeDtypeStruct((1, WIDTH_B), F32)]
        + [_sub_shape(s_len, d, N_PAIRS, BF16) for d in DILATIONS] + [_sub_shape(s_len, d, 1, F32) for d in DILATIONS],
        scratch_shapes=[pltpu.VMEM((N_PAIRS, tm, 128), F32), pltpu.VMEM((1, tm, 128), F32)],
        semantics=("arbitrary",), args=(dh1, woutt, yb, gb, head_sum))[0]


def _attn_bwd(qkv, do, lse, dd, d, exchange=()):
    sd = qkv.shape[2]
    tile, nb, n_tiles, res = _attn_geometry(sd)
    last_block = sd // CHUNK - 1

    def nxt(n):
        return jnp.minimum((n + 1) * nb, last_block)

    def block(ref, next_ref, j):
        return ref[j * CHUNK:(j + 1) * CHUNK, :] if j < nb else next_ref[...]

    def body(q_ref, qn_ref, k_ref, v_ref, do_ref, don_ref, l_ref, ln_ref, dd_ref, ddn_ref,
             dq_ref, dk_ref, dv_ref, carry_ref):
        for rr in range(res):
            l_t = [block(l_ref.at[rr], ln_ref.at[rr], j).T for j in range(nb + 1)]
            dd_t = [block(dd_ref.at[rr], ddn_ref.at[rr], j).T for j in range(nb + 1)]
            for hp in range(PAIRS_PER_STEP):
                l_rows = [jnp.concatenate([t[2 * hp:2 * hp + 1, :], t[2 * hp + 1:2 * hp + 2, :]], axis=1) for t in l_t]
                dd_rows = [jnp.concatenate([t[2 * hp:2 * hp + 1, :], t[2 * hp + 1:2 * hp + 2, :]], axis=1) for t in dd_t]
                one_pair(q_ref.at[rr, hp], qn_ref.at[rr, hp], k_ref.at[rr, hp], v_ref.at[rr, hp], do_ref.at[rr, hp],
                         don_ref.at[rr, hp], l_rows, dd_rows, dq_ref.at[rr, hp], dk_ref.at[rr, hp], dv_ref.at[rr, hp],
                         carry_ref.at[rr, hp])

    def one_pair(q_ref, qn_ref, k_ref, v_ref, do_ref, don_ref, l_rows, dd_rows, dq_ref, dk_ref, dv_ref, carry_ref):
        n = pl.program_id(2)

        @pl.when(n == 0)
        def _():
            carry_ref[...] = jnp.zeros_like(carry_ref)

        head_a = lax.broadcasted_iota(jnp.int32, (CHUNK, 128), 1) < HEAD_DIM
        col = lax.broadcasted_iota(jnp.int32, (CHUNK, 4 * CHUNK), 1)
        qi = col % CHUNK
        ki = lax.broadcasted_iota(jnp.int32, (CHUNK, 4 * CHUNK), 0)
        is_after = col >= 2 * CHUNK
        mask = (is_after & (ki >= qi)) | (jnp.logical_not(is_after) & (qi >= ki))
        mask_last = mask & jnp.logical_or(jnp.logical_not(is_after), n < n_tiles - 1)
        dq_acc = [carry_ref[...]] + [jnp.zeros((CHUNK, 128), F32) for _ in range(nb)]

        q_st = [jnp.concatenate(_both_heads(block(q_ref, qn_ref, j), head_a), axis=0) for j in range(nb + 1)]
        do_st = [jnp.concatenate(_both_heads(block(do_ref, don_ref, j), head_a), axis=0) for j in range(nb + 1)]

        for j in range(nb):
            rows = slice(j * CHUNK, (j + 1) * CHUNK)
            kj = k_ref[rows, :]
            vj = v_ref[rows, :]
            msk = mask if j + 1 < nb else mask_last
            qs = jnp.concatenate([q_st[j], q_st[j + 1]], axis=0)
            dos = jnp.concatenate([do_st[j], do_st[j + 1]], axis=0)
            ls = jnp.concatenate([l_rows[j], l_rows[j + 1]], axis=1)
            dds = jnp.concatenate([dd_rows[j], dd_rows[j + 1]], axis=1)
            st = lax.dot_general(kj, qs, _NT, preferred_element_type=F32)
            pt = jnp.exp(jnp.where(msk, st - ls, NEG))
            dpt = lax.dot_general(vj, dos, _NT, preferred_element_type=F32)
            dst = (pt * (dpt - dds)).astype(BF16)
            dv_ref[rows, :] = jnp.dot(pt.astype(BF16), dos, preferred_element_type=F32).astype(BF16)
            dk_ref[rows, :] = jnp.dot(dst, qs, preferred_element_type=F32).astype(BF16)
            dqs = lax.dot_general(dst, kj, _TN, preferred_element_type=F32)
            dq_acc[j] = dq_acc[j] + jnp.where(head_a, dqs[:CHUNK], dqs[CHUNK:2 * CHUNK])
            dq_acc[j + 1] = dq_acc[j + 1] + jnp.where(head_a, dqs[2 * CHUNK:3 * CHUNK], dqs[3 * CHUNK:])
        for j in range(nb):
            dq_ref[j * CHUNK:(j + 1) * CHUNK, :] = dq_acc[j].astype(BF16)
        carry_ref[...] = dq_acc[nb]

    same = lambda n: n
    grad = jax.ShapeDtypeStruct((d, N_PAIRS, sd, 128), BF16)
    return _call(
        body, name=f"attn_bwd_d{d}", grid=(d // res, N_PAIRS // PAIRS_PER_STEP, n_tiles),
        in_specs=[_attn_spec(0, tile, same, res), _attn_spec(0, CHUNK, nxt, res), _attn_spec(N_PAIRS, tile, same, res),
                  _attn_spec(2 * N_PAIRS, tile, same, res), _attn_spec(0, tile, same, res), _attn_spec(0, CHUNK, nxt, res),
                  _stats_spec(tile, same, res), _stats_spec(CHUNK, nxt, res), _stats_spec(tile, same, res),
                  _stats_spec(CHUNK, nxt, res)],
        out_specs=[_attn_spec(0, tile, same, res)] * 3,
        out_shape=[grad, grad, grad],
        scratch_shapes=[pltpu.VMEM((res, PAIRS_PER_STEP, CHUNK, 128), F32)],
        semantics=("parallel", "parallel", "arbitrary"), args=(qkv, qkv, qkv, qkv, do, do, lse, lse, dd, dd), exchange=exchange)


def _sgu_bwd(ua, sw, b2, gs, ga, dya_n):
    s_len = ua.shape[0]
    tm = ROW_TILE

    def body(ua_ref, sw_ref, b2_ref, gs_ref, ga_ref, dy_ref, dua_ref, dsw_ref, db2_ref, dgs_ref, dga_ref):
        @pl.when(pl.program_id(0) == 0)
        def _():
            dsw_ref[...] = jnp.zeros_like(dsw_ref)
            db2_ref[...] = jnp.zeros_like(db2_ref)
            dgs_ref[...] = jnp.zeros_like(dgs_ref)
            dga_ref[...] = jnp.zeros_like(dga_ref)

        (u, tu), (va, tv), ug, xhat, rstd, vn = _sgu_core(ua_ref, gs_ref)
        wm, keep = _sgu_mix_weights(sw_ref)
        head_a = lax.broadcasted_iota(jnp.int32, (CHUNK, 128), 1) < HEAD_DIM
        gav = ga_ref[...]
        gsv = gs_ref[...]
        dga = jnp.zeros((1, WIDTH_A), F32)
        dgs = jnp.zeros((1, WIDTH_A), F32)
        db2 = jnp.zeros((CHUNK, WIDTH_A), F32)
        dsw = [jnp.zeros((CHUNK, CHUNK), F32) for _ in range(4)]
        for c in range(tm // CHUNK):
            rows = slice(c * CHUNK, (c + 1) * CHUNK)
            vnb = vn[rows].astype(BF16)
            mixed = b2_ref[...] + jnp.concatenate([_sgu_mix(wm, vnb, half, head_a) for half in range(2)], axis=1)
            ugc = ug[rows]
            dya, dga_rows = _norm_bwd(dy_ref[rows, :], ugc * mixed, gav)
            dga = dga + jnp.sum(dga_rows, axis=0, keepdims=True)
            dmixed = dya * ugc
            db2 = db2 + dmixed
            dmb = dmixed.astype(BF16)
            dvn_halves = []
            for half in range(2):
                lanes = slice(half * 128, (half + 1) * 128)
                dm_heads = _both_heads(dmb[:, lanes], head_a)
                dvn_half = jnp.zeros((CHUNK, 128), F32)
                for k, dmh in enumerate(dm_heads):
                    h = 2 * half + k
                    dsw[h] = dsw[h] + lax.dot_general(dmh, vnb[:, lanes], _NT, preferred_element_type=F32)
                    dvn_half = dvn_half + lax.dot_general(wm[h], dmh, _TN, preferred_element_type=F32)
                dvn_halves.append(dvn_half)
            dvn = jnp.concatenate(dvn_halves, axis=1)
            xh = xhat[rows]
            dgs = dgs + jnp.sum(dvn * xh, axis=0, keepdims=True)
            dxh = dvn * gsv
            dvg = rstd[rows] * (dxh - jnp.mean(dxh, axis=-1, keepdims=True) - xh * jnp.mean(dxh * xh, axis=-1, keepdims=True))
            dua_ref[rows, :WIDTH_A] = (dya * mixed * _gelu_grad(u[rows], tu[rows])).astype(BF16)
            dua_ref[rows, WIDTH_A:] = (dvg * _gelu_grad(va[rows], tv[rows])).astype(BF16)
        for h in range(4):
            dsw_ref[h] += jnp.where(keep, dsw[h], 0.0)
        db2_ref[...] += db2
        dgs_ref[...] += dgs
        dga_ref[...] += dga

    return pl.pallas_call(
        body, name="sgu_bwd", grid=(s_len // tm,),
        in_specs=[_rows(tm, 2 * WIDTH_A), _whole((4, CHUNK, CHUNK)), _whole((CHUNK, WIDTH_A)), _whole((1, WIDTH_A)),
                  _whole((1, WIDTH_A)), _rows(tm, WIDTH_A)],
        out_specs=[_rows(tm, 2 * WIDTH_A), _whole((4, CHUNK, CHUNK)), _whole((CHUNK, WIDTH_A)), _whole((1, WIDTH_A)), _whole((1, WIDTH_A))],
        out_shape=[jax.ShapeDtypeStruct((s_len, 2 * WIDTH_A), BF16), jax.ShapeDtypeStruct((4, CHUNK, CHUNK), F32),
                   jax.ShapeDtypeStruct((CHUNK, WIDTH_A), F32), jax.ShapeDtypeStruct((1, WIDTH_A), F32),
                   jax.ShapeDtypeStruct((1, WIDTH_A), F32)],
        compiler_params=_params("arbitrary"),
    )(ua, sw, b2, gs, ga, dya_n)


def _dproj(dua, dqs, dks, dvs, cos, sin, hn1, exchange=()):
    s_len = dua.shape[0]
    tm = ROW_TILE
    n_br = len(DILATIONS)
    n_steps = s_len // tm

    def body(dua_ref, *rest):
        groups = [rest[g * n_br:(g + 1) * n_br] for g in range(3)]
        cos_ref, sin_ref, hn_ref, out_ref, dw_ref, acc, dw_acc = rest[3 * n_br:]
        step = pl.program_id(0)

        @pl.when(step == 0)
        def _():
            dw_acc[...] = jnp.zeros_like(dw_acc)

        out_ref[:, :2 * WIDTH_A] = dua_ref[...]
        c = cos_ref[...]
        s = sin_ref[...]
        first_half = (lax.broadcasted_iota(jnp.int32, (tm, 128), 1) % HEAD_DIM) < HEAD_DIM // 2
        for g, refs in enumerate(groups):
            for cb in range(N_PAIRS):
                t = refs[0][0, cb].astype(F32)
                for i, d in enumerate(DILATIONS[1:]):
                    _from_sub(refs[i + 1], cb, acc, i, d, tm)
                    t = t + acc[i]
                if g < 2:
                    t = (t * c - _swap_halves(t, first_half) * s) * (0.125 if g == 0 else 1.0)
                col = 2 * WIDTH_A + g * WIDTH_B + cb * 128
                out_ref[:, col:col + 128] = t.astype(BF16)
        hn = hn_ref[...]
        for j in range(IN_COLS // 256):
            cols = slice(j * 256, (j + 1) * 256)
            dw_acc[cols, :] += lax.dot_general(out_ref[:, cols], hn, _TN, preferred_element_type=F32)

        @pl.when(step == n_steps - 1)
        def _():
            dw_ref[...] = dw_acc[...].astype(BF16)

    subs = [_sub_spec(d, N_PAIRS, tm) for d in DILATIONS]
    (dproj, dw), received = _call(
        body, name="dproj_dw_in", grid=(n_steps,),
        in_specs=[_rows(tm, 2 * WIDTH_A)] + subs * 3 + [_rows(tm, 128), _rows(tm, 128), _rows(tm, D_MODEL)],
        out_specs=[_rows(tm, IN_COLS), _whole((IN_COLS, D_MODEL))],
        out_shape=[jax.ShapeDtypeStruct((s_len, IN_COLS), BF16), jax.ShapeDtypeStruct((IN_COLS, D_MODEL), BF16)],
        scratch_shapes=[pltpu.VMEM((n_br - 1, tm, 128), F32), pltpu.VMEM((IN_COLS, D_MODEL), F32)],
        semantics=("arbitrary",), args=(dua, *dqs, *dks, *dvs, cos, sin, hn1), exchange=exchange)
    return dproj, dw, received


def _mm_tn(a, b, name):
    s_len, m = a.shape
    n = b.shape[1]
    tk = 2 * ROW_TILE
    tm = m if m <= 512 else (1408 if m == D_FF else 512)
    n_k = s_len // tk

    def body(a_ref, b_ref, o_ref, acc_ref):
        k = pl.program_id(1)

        @pl.when(k == 0)
        def _():
            acc_ref[...] = jnp.zeros_like(acc_ref)

        acc_ref[...] += lax.dot_general(a_ref[...].astype(BF16), b_ref[...].astype(BF16), _TN, preferred_element_type=F32)

        @pl.when(k == n_k - 1)
        def _():
            o_ref[...] = acc_ref[...].astype(BF16)

    return pl.pallas_call(
        body, name=name, grid=(m // tm, n_k),
        in_specs=[pl.BlockSpec((tk, tm), lambda i, k: (k, i)), pl.BlockSpec((tk, n), lambda i, k: (k, 0))],
        out_specs=pl.BlockSpec((tm, n), lambda i, k: (i, 0)),
        out_shape=jax.ShapeDtypeStruct((m, n), BF16),
        scratch_shapes=[pltpu.VMEM((tm, n), F32)],
        compiler_params=_params("parallel", "arbitrary"),
    )(a, b)


def _position():
    x, y, c = lax.axis_index("x"), lax.axis_index("y"), lax.axis_index("c")
    return x, y, c, 4 * x + 2 * y + c


def _peer(x, y, c, rel):
    return (x ^ ((rel >> 2) & 1), y ^ ((rel >> 1) & 1), c ^ (rel & 1))


def _exchange_out_shape(kind, arr):
    return jax.ShapeDtypeStruct(((N_DEV,) + arr.shape) if kind == "gather" else arr.shape, arr.dtype)


def _exchange_sems(n_items):
    return [pltpu.SemaphoreType.DMA((n_items, N_DEV)), pltpu.SemaphoreType.DMA((n_items, N_DEV)), pltpu.SemaphoreType.DMA((n_items,))]


def _exchange_copies(kinds, srcs, dsts, sems, arrivals):
    send_sems, recv_sems, local_sems = sems
    x, y, c, me = _position()
    local, sends, recvs = [], [], []
    for k, (kind, src, dst) in enumerate(zip(kinds, srcs, dsts)):
        own = src if kind == "gather" else src.at[me]
        local.append(pltpu.make_async_copy(own, dst.at[me], local_sems.at[k]))
        for rel in range(1, N_DEV):
            going = src if kind == "gather" else src.at[me ^ rel]
            common = dict(send_sem=send_sems.at[k, rel], recv_sem=recv_sems.at[k, rel],
                          device_id=_peer(x, y, c, rel), device_id_type=MESH)
            sends.append(pltpu.make_async_remote_copy(src_ref=going, dst_ref=dst.at[me], **common))
            if arrivals:
                recvs.append(pltpu.make_async_remote_copy(src_ref=own, dst_ref=dst.at[me ^ rel], **common))
    return local, sends, recvs


def _exchange_start(kinds, srcs, dsts, sems):
    local, sends, _ = _exchange_copies(kinds, srcs, dsts, sems, arrivals=False)
    for cp in local + sends:
        cp.start()


def _exchange_finish(kinds, srcs, dsts, sems):
    local, sends, recvs = _exchange_copies(kinds, srcs, dsts, sems, arrivals=True)
    for cp in recvs:
        cp.wait_recv()
    for cp in sends:
        cp.wait_send()
    for cp in local:
        cp.wait()


def _gather_two_level_with_rope_tables(shard, inv_freq, s_len, name):
    rows = ROW_TILE

    def body(inv_ref, src, cos_ref, sin_ref, dst, send_sems, recv_sems, local_sem):
        x, y, c, me = _position()
        sibling = (x, y, 1 - c)
        chips = [(1 - x, y), (x, 1 - y), (1 - x, 1 - y)]

        def block(px, py, pc):
            return dst.at[4 * px + 2 * py + pc]

        def copy(k, blk, to, src_ref=None):
            return pltpu.make_async_remote_copy(
                src_ref=block(*blk) if src_ref is None else src_ref, dst_ref=block(*blk),
                send_sem=send_sems.at[k], recv_sem=recv_sems.at[k], device_id=to, device_id_type=MESH)

        x_nbr, y_nbr, diag = chips
        mine = pltpu.make_async_copy(src, dst.at[me], local_sem)
        mine.start()
        first = [copy(0, (x, y, c), sibling, src), copy(1, (x, y, c), (*x_nbr, c), src), copy(2, (x, y, c), (*y_nbr, c), src)]
        for cp in first:
            cp.start()

        inv = inv_ref[...]
        lane = lax.broadcasted_iota(jnp.int32, (rows, 128), 1)
        sign = jnp.where((lane // (HEAD_DIM // 2)) % 2 == 0, -1.0, 1.0)
        row = lax.broadcasted_iota(jnp.int32, (rows, 128), 0)
        n_chunks = s_len // rows

        def fill_tables(lo, hi):
            @pl.loop(lo, hi)
            def _(i):
                at = pl.multiple_of(i * rows, rows)
                ang = (row + at).astype(F32) * inv
                cos_ref[pl.ds(at, rows), :] = jnp.cos(ang)
                sin_ref[pl.ds(at, rows), :] = jnp.sin(ang) * sign

        fill_tables(0, n_chunks // 2)
        passed = [copy(4 + j, (*chip, c), sibling) for j, chip in enumerate(chips)]
        copy(1, (*x_nbr, c), (x, y, c)).wait_recv()
        copy(2, (*y_nbr, c), (x, y, c)).wait_recv()

        @pl.when(c == 1)
        def _():
            copy(3, (*x_nbr, c), (*y_nbr, c)).start()

        @pl.when(c == 0)
        def _():
            copy(3, (*y_nbr, c), (*x_nbr, c)).start()

        passed[0].start()
        passed[1].start()
        fill_tables(n_chunks // 2, n_chunks)
        copy(3, (*diag, c), (x, y, c)).wait_recv()
        passed[2].start()
        copy(0, (x, y, 1 - c), (x, y, c)).wait_recv()
        for j, chip in enumerate(chips):
            copy(4 + j, (*chip, 1 - c), (x, y, c)).wait_recv()
        for cp in first + passed:
            cp.wait_send()
        copy(3, (*x_nbr, c), (*y_nbr, c)).wait_send()
        mine.wait()

    any_spec = pl.BlockSpec(memory_space=pl.ANY)
    vmem = pl.BlockSpec(memory_space=pltpu.VMEM)
    table = jax.ShapeDtypeStruct((s_len, 128), F32)
    return pl.pallas_call(
        body, name=name, in_specs=[vmem, any_spec], out_specs=[vmem, vmem, any_spec],
        out_shape=[table, table, _exchange_out_shape("gather", shard)],
        scratch_shapes=[pltpu.SemaphoreType.DMA((N_DEV - 1,)), pltpu.SemaphoreType.DMA((N_DEV - 1,)), pltpu.SemaphoreType.DMA],
        compiler_params=pltpu.CompilerParams(has_side_effects=True, vmem_limit_bytes=V7X_VMEM_LIMIT_BYTES),
    )(inv_freq, shard)


def _call(body, *, name, grid, in_specs, out_specs, out_shape, args, scratch_shapes=(), semantics, exchange=()):
    if not exchange:
        outs = pl.pallas_call(body, name=name, grid=grid, in_specs=in_specs, out_specs=out_specs, out_shape=out_shape,
                              scratch_shapes=list(scratch_shapes), compiler_params=_params(*semantics))(*args)
        return outs, []
    kinds = [k for k, _ in exchange]
    n_in, n_out, n_x, n_scr = len(in_specs), len(out_specs), len(exchange), len(scratch_shapes)

    def wrapped(*refs):
        ins, refs = refs[:n_in], refs[n_in:]
        srcs, refs = refs[:n_x], refs[n_x:]
        outs, refs = refs[:n_out], refs[n_out:]
        dsts, refs = refs[:n_x], refs[n_x:]
        scratch, sems = refs[:n_scr], refs[n_scr:]
        ids = [pl.program_id(a) for a in range(len(grid))]
        first = functools.reduce(jnp.logical_and, [i == 0 for i in ids])
        last = functools.reduce(jnp.logical_and, [i == g - 1 for i, g in zip(ids, grid)])

        @pl.when(first)
        def _():
            _exchange_start(kinds, srcs, dsts, sems)

        body(*ins, *outs, *scratch)

        @pl.when(last)
        def _():
            _exchange_finish(kinds, srcs, dsts, sems)

    any_spec = pl.BlockSpec(memory_space=pl.ANY)
    outs = pl.pallas_call(
        wrapped, name=name, grid=grid,
        in_specs=list(in_specs) + [any_spec] * n_x, out_specs=list(out_specs) + [any_spec] * n_x,
        out_shape=list(out_shape) + [_exchange_out_shape(k, a) for k, a in exchange],
        scratch_shapes=list(scratch_shapes) + _exchange_sems(n_x),
        compiler_params=pltpu.CompilerParams(dimension_semantics=("arbitrary",) * len(grid),
                                             vmem_limit_bytes=V7X_VMEM_LIMIT_BYTES, has_side_effects=True),
    )(*args, *[a for _, a in exchange])
    return outs[:n_out], outs[n_out:]


def _adamw_math(w, g, m, v):
    m = ADAM_B1 * m + (1.0 - ADAM_B1) * g
    v = ADAM_B2 * v + (1.0 - ADAM_B2) * (g * g)
    m_hat = m / (1.0 - ADAM_B1 ** ADAM_STEP)
    v_hat = v / (1.0 - ADAM_B2 ** ADAM_STEP)
    return -ADAM_LR * (m_hat / (jnp.sqrt(v_hat) + ADAM_EPS) + ADAM_WD * w), m, v


def _adamw(parts, w, m, v, name, exchange=()):
    rows, cols = w.shape
    tm = 256 if rows % 256 == 0 and rows > 256 else rows

    def body(p_ref, w_ref, m_ref, v_ref, g_ref, d_ref, nm_ref, nv_ref):
        g = p_ref[0].astype(F32)
        for j in range(1, N_DEV):
            g = g + p_ref[j].astype(F32)
        delta, nm, nv = _adamw_math(w_ref[...], g, m_ref[...], v_ref[...])
        g_ref[...] = g
        d_ref[...] = delta
        nm_ref[...] = nm
        nv_ref[...] = nv

    shard = jax.ShapeDtypeStruct((rows, cols), F32)
    return _call(
        body, name=name, grid=(rows // tm,),
        in_specs=[pl.BlockSpec((N_DEV, tm, cols), lambda i: (0, i, 0))] + [_rows(tm, cols)] * 3,
        out_specs=[_rows(tm, cols)] * 4,
        out_shape=[shard] * 4,
        semantics=("parallel",), args=(parts, w, m, v), exchange=exchange)


_SMALL = ("mix_norm_g", "sgu_w", "sgu_b", "sgu_norm_g", "out_norm_a", "out_norm_b", "ffn_norm_g", "ple_norm_g", "final_norm_g")
_BIG = ("w_in", "w_out", "w_gate", "w_up", "w_down", "w_ple_gate", "w_ple_proj")
_COLUMN_SHARDED = ("w_in", "w_gate", "w_up", "w_ple_proj")
_ORDER = ("mix_norm_g", "w_in", "sgu_w", "sgu_b", "sgu_norm_g", "out_norm_a", "out_norm_b", "w_out", "ffn_norm_g",
          "w_gate", "w_up", "w_down", "ple_norm_g", "w_ple_gate", "w_ple_proj", "final_norm_g")


def _pack_small(values, names=_SMALL):
    flat = jnp.concatenate([values[n].reshape(-1).astype(F32) for n in names])
    pad = (-flat.shape[0]) % (8 * 128)
    return jnp.pad(flat, (0, pad)).reshape(-1, 128)


def _unpack_small(packed, like):
    flat = packed.reshape(-1)
    out, at = {}, 0
    for n in _SMALL:
        size = like[n].size
        out[n] = flat[at:at + size].reshape(like[n].shape)
        at += size
    return out


def _own_orientation(name, value):
    return value[0].T if name in _COLUMN_SHARDED else value[0]


def _reference_orientation(name, value):
    return (value.T if name in _COLUMN_SHARDED else value)[None]


def _full_from_gathered(gathered):
    return gathered.reshape(N_DEV * gathered.shape[1], gathered.shape[2])


def _sliced_for_devices(grad):
    return grad.reshape(N_DEV, grad.shape[0] // N_DEV, grad.shape[1])


def _rope_inv_freq():
    half = HEAD_DIM // 2
    inv = ROPE_THETA ** (-jnp.arange(half, dtype=F32) / half)
    return jnp.tile(inv, 128 // half)[None, :]


def _forward_backward(x, p, target, small, shards):
    def gather(*names):
        return [("gather", shards[n]) for n in names]

    def scatter(**grads):
        return [("scatter", _sliced_for_devices(g)) for g in grads.values()]

    full, parts = {}, {}
    s_len = x.shape[0]
    cos, sin, got = _gather_two_level_with_rope_tables(shards["w_in"], _rope_inv_freq(), s_len, "gather_w_in")
    full["w_in"] = _full_from_gathered(got)

    g_mix, g_ffn, g_ple = small["mix_norm_g"], small["ffn_norm_g"], small["ple_norm_g"]
    g_fin = small["final_norm_g"].reshape(1, D_MODEL)
    sw, gs, ga, gb = small["sgu_w"], small["sgu_norm_g"], small["out_norm_a"], small["out_norm_b"]
    b2 = jnp.repeat(small["sgu_b"].T, HEAD_DIM, axis=1)
    head_sum = (jnp.arange(WIDTH_B)[:, None] // HEAD_DIM == jnp.arange(128)[None, :]).astype(BF16)
    n_br = len(DILATIONS)

    def arrived(names, got):
        for n, g in zip(names, got):
            full[n] = _full_from_gathered(g)

    (ua, hn1, *qkv), got = _inproj(x, g_mix, full["w_in"], cos, sin, exchange=gather("w_gate"))
    arrived(("w_gate",), got)
    ya_n = _sgu_fwd(ua, sw, b2, gs, ga)
    half = shards["w_up"].shape[0] // 2
    riders = [[("gather", shards["w_up"][:half])], [("gather", shards["w_up"][half:])], gather("w_out")]
    branch, got = [], []
    for i, d in enumerate(DILATIONS):
        o_l, g = _attn_fwd(qkv[i], d, exchange=riders[i])
        branch.append(o_l)
        got += g
    arrived(("w_up", "w_out"), [jnp.concatenate(got[:2], axis=1), got[2]])
    y, yb, *lse = _combine([o for o, _ in branch], [l for _, l in branch], ya_n, gb, head_sum.T)
    last_wave = ("w_down", "w_ple_gate", "w_ple_proj")
    (h1, a, b, act, hn2), got = _ffn_up(y, full["w_out"], x, g_ffn, full["w_gate"], full["w_up"], exchange=gather(*last_wave))
    arrived(last_wave, got)
    h2, h3, gate, pp, hn3 = _ffn_down_ple(act, full["w_down"], h1, g_ple, full["w_ple_gate"], p, full["w_ple_proj"])

    dh2, loss, d_fin, d_ple, g_ple_gate, g_ple_proj = _loss_ple_bwd(
        h3, target, g_fin, gate, pp, h2, g_ple, full["w_ple_gate"], hn3, p)
    g_down = _mm_tn(act, dh2, "dw_down")
    (da, db), (parts["w_down"],) = _ffn_down_bwd(dh2, full["w_down"], a, b, exchange=scatter(w_down=g_down))
    g_gate = _mm_tn(da, hn2, "dw_gate")
    g_up = _mm_tn(db, hn2, "dw_up")
    (dh1, d_ffn, g_out), (parts["w_gate"], parts["w_ple_gate"], parts["w_ple_proj"]) = _mm_norm_bwd(
        [(da, full["w_gate"]), (db, full["w_up"])], h1, g_ffn, dh2, "ffn_up_bwd",
        exchange=scatter(w_gate=g_gate, w_ple_gate=g_ple_gate, w_ple_proj=g_ple_proj), dw_lhs=y)
    dya_n, d_gb, *do_dd = _outproj_bwd(dh1, full["w_out"], yb, gb, head_sum)
    grads_b = []
    for i, d in enumerate(DILATIONS):
        g3, got = _attn_bwd(qkv[i], do_dd[i], lse[i], do_dd[n_br + i], d,
                            exchange=scatter(w_up=g_up, w_out=g_out) if i == 0 else ())
        grads_b.append(g3)
        if i == 0:
            parts["w_up"], parts["w_out"] = got
    dua, d_sw, d_b2, d_gs, d_ga = _sgu_bwd(ua, sw, b2, gs, ga, dya_n)
    early = {
        "sgu_w": d_sw, "sgu_b": d_b2.reshape(CHUNK, 4, HEAD_DIM).sum(axis=-1).T, "sgu_norm_g": d_gs, "out_norm_a": d_ga,
        "out_norm_b": d_gb, "ffn_norm_g": d_ffn, "ple_norm_g": d_ple, "final_norm_g": d_fin,
    }
    dproj, g_in, (early_parts,) = _dproj(
        dua, [g[0] for g in grads_b], [g[1] for g in grads_b], [g[2] for g in grads_b], cos, sin, hn1,
        exchange=[("gather", _pack_small(early, _SMALL[1:]))])
    (dx, d_mix), (parts["w_in"],) = _mm_norm_bwd(
        [(dproj, full["w_in"])], x, g_mix, dh1, "inproj_bwd", exchange=scatter(w_in=g_in))
    late = jnp.concatenate([_pack_small({"mix_norm_g": d_mix}, _SMALL[:1]), jnp.broadcast_to(loss, (8, 128))])
    return dx, parts, early_parts, late


def kernel(x, p, mix_norm_g, w_in, sgu_w, sgu_b, sgu_norm_g, out_norm_a, out_norm_b, w_out, ffn_norm_g, w_gate, w_up, w_down, ple_norm_g, w_ple_gate, w_ple_proj, final_norm_g, loss_target, m_mix_norm_g, m_w_in, m_sgu_w, m_sgu_b, m_sgu_norm_g, m_out_norm_a, m_out_norm_b, m_w_out, m_ffn_norm_g, m_w_gate, m_w_up, m_w_down, m_ple_norm_g, m_w_ple_gate, m_w_ple_proj, m_final_norm_g, v_mix_norm_g, v_w_in, v_sgu_w, v_sgu_b, v_sgu_norm_g, v_out_norm_a, v_out_norm_b, v_w_out, v_ffn_norm_g, v_w_gate, v_w_up, v_w_down, v_ple_norm_g, v_w_ple_gate, v_w_ple_proj, v_final_norm_g):
    given = dict(locals())
    weights = {n: given[n] for n in _ORDER}
    moments_m = {n: given["m_" + n] for n in _ORDER}
    moments_v = {n: given["v_" + n] for n in _ORDER}

    shards = {n: _own_orientation(n, weights[n]).astype(BF16) for n in _BIG}
    small = {n: (weights[n][0] if n in ("sgu_w", "sgu_b") else weights[n]) for n in _SMALL}

    dx, parts, early_parts, late = _forward_backward(x[0], p[0, 0], loss_target[0], small, shards)

    small_like = {n: weights[n] for n in _SMALL}
    grads, deltas, new_m, new_v = {}, {}, {}, {}
    for n in _BIG:
        outs, got = _adamw(parts[n], _own_orientation(n, weights[n]), _own_orientation(n, moments_m[n]),
                           _own_orientation(n, moments_v[n]), "adamw_" + n, exchange=[("gather", late)] if n == "w_in" else ())
        grads[n], deltas[n], new_m[n], new_v[n] = [_reference_orientation(n, o) for o in outs]
        if n == "w_in":
            (late_parts,) = got
    loss = jnp.sum(late_parts[:, 8, 0])
    small_parts = jnp.concatenate([late_parts[:, :8], early_parts], axis=1)
    (g, d, nm, nv), _ = _adamw(small_parts, _pack_small(small_like), _pack_small({n: moments_m[n] for n in _SMALL}),
                               _pack_small({n: moments_v[n] for n in _SMALL}), "adamw_small")
    for out, packed in ((grads, g), (deltas, d), (new_m, nm), (new_v, nv)):
        out.update(_unpack_small(packed, small_like))

    return (loss, dx[None], *[grads[n] for n in _ORDER], *[deltas[n] for n in _ORDER],
            *[new_m[n] for n in _ORDER], *[new_v[n] for n in _ORDER])
```

```python
import functools

import jax
import jax.numpy as jnp
from jax import lax
from jax.experimental import pallas as pl
from jax.experimental.pallas import tpu as pltpu

F32 = jnp.float32
BF16 = jnp.bfloat16

D_MODEL = 1024
WIDTH_A = 256
WIDTH_B = 768
D_FF = 2816
IN_COLS = 2 * WIDTH_A + 3 * WIDTH_B
PLE_DIM = 256
HEAD_DIM = 64
N_PAIRS = WIDTH_B // 128
CHUNK = 128
N_BACK = 128
DILATIONS = (1, 4, 16)
ROPE_THETA = 10000.0
EPS = 1e-6
N_DEV = 8

ADAM_LR = 0.001
ADAM_B1 = 0.9
ADAM_B2 = 0.999
ADAM_EPS = 1e-08
ADAM_WD = 0.01
ADAM_STEP = 10

V7X_VMEM_LIMIT_BYTES = 56 * 1024 * 1024
ROW_TILE = 512
MESH = pl.DeviceIdType.MESH
NEG = -1e30

_NT = (((1,), (1,)), ((), ()))
_TN = (((0,), (0,)), ((), ()))


def _params(*semantics):
    return pltpu.CompilerParams(dimension_semantics=semantics, vmem_limit_bytes=V7X_VMEM_LIMIT_BYTES)


def _rows(tm, width):
    return pl.BlockSpec((tm, width), lambda i: (i, 0))


def _whole(shape):
    return pl.BlockSpec(shape, lambda *_: (0,) * len(shape))


def _resident(shape):
    return pl.BlockSpec(shape, lambda *_: (0,) * len(shape), pipeline_mode=pl.Buffered(1))


def _gelu(x):
    t = jnp.tanh(0.7978845608028654 * (x + 0.044715 * (x * x * x)))
    return 0.5 * x * (1.0 + t)


def _gelu_grad(x):
    t = jnp.tanh(0.7978845608028654 * (x + 0.044715 * (x * x * x)))
    return 0.5 * (1.0 + t) + 0.5 * x * (1.0 - t * t) * (0.7978845608028654 * (1.0 + 3.0 * 0.044715 * (x * x)))


def _rstd(x):
    return lax.rsqrt(jnp.mean(x * x, axis=-1, keepdims=True) + EPS)


def _norm_bwd(dn, h, g, r=None):
    r = _rstd(h) if r is None else r
    n = h * r
    t = dn * g
    return r * (t - n * jnp.mean(t * n, axis=-1, keepdims=True)), dn * n


def _swap_halves(x, first_half):
    return jnp.where(first_half, pltpu.roll(x, 96, 1), pltpu.roll(x, 32, 1))


def _sub_spec(d, n_cb, tm):
    return pl.BlockSpec((d, n_cb, tm // d, 128), lambda i: (0, 0, i, 0))


def _sub_shape(s_len, d, n_cb, dtype):
    return jax.ShapeDtypeStruct((d, n_cb, s_len // d, 128), dtype)


def _to_sub(stage_ref, cb_src, out_ref, cb_dst, d, tm):
    slab = stage_ref.at[cb_src]
    for r in range(d):
        out_ref[r, cb_dst] = slab[pl.ds(r, tm // d, stride=d), :].astype(out_ref.dtype)


def _from_sub(in_ref, cb_src, stage_ref, cb_dst, d, tm):
    slab = stage_ref.at[cb_dst]
    for r in range(d):
        slab[pl.ds(r, tm // d, stride=d), :] = in_ref[r, cb_src].astype(F32)


def _inproj(x, g, w, cos, sin, exchange=()):
    s_len = x.shape[0]
    tm = ROW_TILE
    n_cb = 3 * N_PAIRS

    def body(x_ref, g_ref, w_ref, cos_ref, sin_ref, ua_ref, hn_ref, *rest):
        sub_refs, stage = rest[:-1], rest[-1]
        xf = x_ref[...]
        hn = (xf * _rstd(xf) * g_ref[...]).astype(BF16)
        hn_ref[...] = hn
        c = cos_ref[...]
        s = sin_ref[...]
        first_half = (lax.broadcasted_iota(jnp.int32, (tm, 128), 1) % HEAD_DIM) < HEAD_DIM // 2
        for col in range(0, IN_COLS, 512):
            width = min(512, IN_COLS - col)
            acc = lax.dot_general(hn, w_ref[col:col + width, :], _NT, preferred_element_type=F32)
            if col < 2 * WIDTH_A:
                ua_ref[:, col:col + width] = acc
                continue
            for part in range(width // 128):
                cb = (col - 2 * WIDTH_A) // 128 + part
                t = acc[:, part * 128:(part + 1) * 128]
                if cb < 2 * N_PAIRS:
                    t = (t * c + _swap_halves(t, first_half) * s) * (0.125 if cb < N_PAIRS else 1.0)
                stage[cb] = t
                sub_refs[0][0, cb] = t.astype(BF16)
        for cb in range(n_cb):
            for d, out_ref in zip(DILATIONS[1:], sub_refs[1:]):
                _to_sub(stage, cb, out_ref, cb, d, tm)

    return _call(
        body, name="inproj", grid=(s_len // tm,),
        in_specs=[_rows(tm, D_MODEL), _whole((1, D_MODEL)), _resident((IN_COLS, D_MODEL)), _rows(tm, 128), _rows(tm, 128)],
        out_specs=[_rows(tm, 2 * WIDTH_A), _rows(tm, D_MODEL)] + [_sub_spec(d, n_cb, tm) for d in DILATIONS],
        out_shape=[jax.ShapeDtypeStruct((s_len, 2 * WIDTH_A), F32), jax.ShapeDtypeStruct((s_len, D_MODEL), BF16)]
        + [_sub_shape(s_len, d, n_cb, BF16) for d in DILATIONS],
        scratch_shapes=[pltpu.VMEM((n_cb, tm, 128), F32)],
        semantics=("parallel",), args=(x, g, w, cos, sin), exchange=exchange)


def _sgu_mix_weights(sw_ref):
    keep = lax.broadcasted_iota(jnp.int32, (CHUNK, CHUNK), 0) >= lax.broadcasted_iota(jnp.int32, (CHUNK, CHUNK), 1)
    return [jnp.where(keep, sw_ref[h], 0.0).astype(BF16) for h in range(4)], keep


def _sgu_core(ua_ref, gs_ref):
    u = ua_ref[:, :WIDTH_A]
    va = ua_ref[:, WIDTH_A:]
    vg = _gelu(va)
    xc = vg - jnp.mean(vg, axis=-1, keepdims=True)
    rstd = lax.rsqrt(jnp.mean(xc * xc, axis=-1, keepdims=True) + EPS)
    xhat = xc * rstd
    return u, va, _gelu(u), xhat, rstd, xhat * gs_ref[...]


def _sgu_mix(wm, vnb, half, head_a):
    va, vb = _both_heads(vnb[:, half * 128:(half + 1) * 128], head_a)
    return (jnp.dot(wm[2 * half], va, preferred_element_type=F32)
            + jnp.dot(wm[2 * half + 1], vb, preferred_element_type=F32))


def _sgu_fwd(ua, sw, b2, gs, ga):
    s_len = ua.shape[0]
    tm = ROW_TILE

    def body(ua_ref, sw_ref, b2_ref, gs_ref, ga_ref, out_ref):
        _, _, ug, _, _, vn = _sgu_core(ua_ref, gs_ref)
        wm, _ = _sgu_mix_weights(sw_ref)
        head_a = lax.broadcasted_iota(jnp.int32, (CHUNK, 128), 1) < HEAD_DIM
        for c in range(tm // CHUNK):
            rows = slice(c * CHUNK, (c + 1) * CHUNK)
            vnb = vn[rows].astype(BF16)
            mixed = b2_ref[...] + jnp.concatenate([_sgu_mix(wm, vnb, half, head_a) for half in range(2)], axis=1)
            ya = ug[rows] * mixed
            out_ref[rows, :] = (ya * _rstd(ya) * ga_ref[...]).astype(BF16)

    return pl.pallas_call(
        body, name="sgu_fwd", grid=(s_len // tm,),
        in_specs=[_rows(tm, 2 * WIDTH_A), _whole((4, CHUNK, CHUNK)), _whole((CHUNK, WIDTH_A)), _whole((1, WIDTH_A)), _whole((1, WIDTH_A))],
        out_specs=_rows(tm, WIDTH_A),
        out_shape=jax.ShapeDtypeStruct((s_len, WIDTH_A), BF16),
        compiler_params=_params("parallel"),
    )(ua, sw, b2, gs, ga)


STEP_POSITIONS = 2 * ROW_TILE


def _attn_geometry(sd):
    tile = min(STEP_POSITIONS, sd)
    return tile, tile // CHUNK, sd // tile, STEP_POSITIONS // tile


PAIRS_PER_STEP = 6


def _attn_spec(cb0, rows, row_index, res):
    return pl.BlockSpec((res, PAIRS_PER_STEP, rows, 128), lambda r, g, n: (r, cb0 // PAIRS_PER_STEP + g, row_index(n), 0))


assert PAIRS_PER_STEP == N_PAIRS
assert DILATIONS[0] == 1


def _stats_spec(rows, row_index, res):
    return pl.BlockSpec((res, None, rows, 128), lambda r, g, n: (r, 0, row_index(n), 0))


def _stats_shape(sd, d):
    return jax.ShapeDtypeStruct((d, 1, sd, 128), F32)


def _both_heads(x, head_a):
    zero = jnp.zeros_like(x)
    return [jnp.where(head_a, x, zero), jnp.where(head_a, zero, x)]


def _attn_fwd(qkv, d, exchange=()):
    sd = qkv.shape[2]
    tile, nb, n_tiles, res = _attn_geometry(sd)

    def prev(n):
        return jnp.maximum(n * nb - 1, 0)

    def body(q_ref, k_ref, kp_ref, v_ref, vp_ref, o_ref, l_ref):
        for rr in range(res):
            for hp in range(PAIRS_PER_STEP):
                one_pair(hp, q_ref.at[rr, hp], k_ref.at[rr, hp], kp_ref.at[rr, hp], v_ref.at[rr, hp], vp_ref.at[rr, hp],
                         o_ref.at[rr, hp], l_ref.at[rr])

    def one_pair(hp, q_ref, k_ref, kp_ref, v_ref, vp_ref, o_ref, l_ref):
        n = pl.program_id(2)
        lane = lax.broadcasted_iota(jnp.int32, (CHUNK, 128), 1)
        head_a = lane < HEAD_DIM
        qi = lax.broadcasted_iota(jnp.int32, (2 * CHUNK, 2 * CHUNK), 0) % CHUNK
        kc = lax.broadcasted_iota(jnp.int32, (2 * CHUNK, 2 * CHUNK), 1)
        band = (kc >= qi) & (kc <= qi + N_BACK)
        for j in range(nb):
            rows = slice(j * CHUNK, (j + 1) * CHUNK)
            if j == 0:
                kcat = jnp.concatenate([kp_ref[...], k_ref[rows, :]], axis=0)
                vcat = jnp.concatenate([vp_ref[...], v_ref[rows, :]], axis=0)
                valid = band & jnp.logical_or(n > 0, kc >= CHUNK)
            else:
                kcat = k_ref[(j - 1) * CHUNK:(j + 1) * CHUNK, :]
                vcat = v_ref[(j - 1) * CHUNK:(j + 1) * CHUNK, :]
                valid = band
            q2 = jnp.concatenate(_both_heads(q_ref[rows, :], head_a), axis=0)
            s = lax.dot_general(q2, kcat, _NT, preferred_element_type=F32)
            s = jnp.where(valid, s, NEG)
            m = jnp.max(s, axis=-1, keepdims=True)
            p = jnp.exp(s - m)
            l = jnp.sum(p, axis=-1, keepdims=True)
            o2 = jnp.dot(p.astype(BF16), vcat, preferred_element_type=F32) / l
            lse2 = m + jnp.log(l)
            o_ref[rows, :] = jnp.where(head_a, o2[:CHUNK], o2[CHUNK:]).astype(BF16)
            others = l_ref[rows, :] if hp > 0 else jnp.zeros((CHUNK, 128), F32)
            l_ref[rows, :] = jnp.where(lane == 2 * hp, lse2[:CHUNK], jnp.where(lane == 2 * hp + 1, lse2[CHUNK:], others))

    same = lambda n: n
    return _call(
        body, name=f"attn_fwd_d{d}", grid=(d // res, N_PAIRS // PAIRS_PER_STEP, n_tiles),
        in_specs=[_attn_spec(0, tile, same, res), _attn_spec(N_PAIRS, tile, same, res), _attn_spec(N_PAIRS, CHUNK, prev, res),
                  _attn_spec(2 * N_PAIRS, tile, same, res), _attn_spec(2 * N_PAIRS, CHUNK, prev, res)],
        out_specs=[_attn_spec(0, tile, same, res), _stats_spec(tile, same, res)],
        out_shape=[jax.ShapeDtypeStruct((d, N_PAIRS, sd, 128), BF16), _stats_shape(sd, d)],
        semantics=("parallel", "parallel", "parallel"), args=(qkv, qkv, qkv, qkv, qkv), exchange=exchange)


def _combine(outs, lses, ya_n, gb, head_spread):
    s_len = ya_n.shape[0]
    tm = ROW_TILE
    n_br = len(DILATIONS)

    def body(*refs):
        o_refs, l_refs = refs[:n_br], refs[n_br:2 * n_br]
        ya_ref, gb_ref, spread_ref, y_ref, yb_ref = refs[2 * n_br:2 * n_br + 5]
        lse_refs = refs[2 * n_br + 5:3 * n_br + 5]
        o_nat, l_nat, lse_nat, w_wide = refs[3 * n_br + 5:]
        for i, d in enumerate(DILATIONS):
            _from_sub(l_refs[i], 0, l_nat, i, d, tm)
        ls = [l_nat[i] for i in range(n_br)]
        top = jnp.maximum(jnp.maximum(ls[0], ls[1]), ls[2])
        ws = [jnp.exp(l - top) for l in ls]
        den = ws[0] + ws[1] + ws[2]
        inv = 1.0 / den
        for i in range(n_br):
            w = ws[i] * inv
            hi = w.astype(BF16)
            lo = (w - hi.astype(F32)).astype(BF16)
            w_wide[i] = (jnp.dot(hi, spread_ref[...], preferred_element_type=F32)
                         + jnp.dot(lo, spread_ref[...], preferred_element_type=F32))
        lse_nat[0] = top + jnp.log(den)
        for d, lse_ref in zip(DILATIONS, lse_refs):
            _to_sub(lse_nat, 0, lse_ref, 0, d, tm)
        sumsq = jnp.zeros((tm, 1), F32)
        for cb in range(N_PAIRS):
            cols = slice(cb * 128, (cb + 1) * 128)
            yb = w_wide[0, :, cols] * o_refs[0][0, cb].astype(F32)
            for i, d in enumerate(DILATIONS[1:], start=1):
                _from_sub(o_refs[i], cb, o_nat, i, d, tm)
                yb = yb + w_wide[i, :, cols] * o_nat[i]
            yb_ref[:, cb * 128:(cb + 1) * 128] = yb
            sumsq = sumsq + jnp.sum(yb * yb, axis=-1, keepdims=True)
        r = lax.rsqrt(sumsq / WIDTH_B + EPS)
        y_ref[:, :WIDTH_A] = ya_ref[...]
        y_ref[:, WIDTH_A:] = (yb_ref[...] * r * gb_ref[...]).astype(BF16)

    stats = [_sub_spec(d, 1, tm) for d in DILATIONS]
    return _call(
        body, name="attn_combine", grid=(s_len // tm,),
        in_specs=[_sub_spec(d, N_PAIRS, tm) for d in DILATIONS] + stats
        + [_rows(tm, WIDTH_A), _whole((1, WIDTH_B)), _whole((128, WIDTH_B))],
        out_specs=[_rows(tm, D_MODEL), _rows(tm, WIDTH_B)] + stats,
        out_shape=[jax.ShapeDtypeStruct((s_len, D_MODEL), BF16), jax.ShapeDtypeStruct((s_len, WIDTH_B), F32)]
        + [_sub_shape(s_len, d, 1, F32) for d in DILATIONS],
        scratch_shapes=[pltpu.VMEM((n_br, tm, 128), F32), pltpu.VMEM((n_br, tm, 128), F32), pltpu.VMEM((1, tm, 128), F32),
                        pltpu.VMEM((n_br, tm, WIDTH_B), F32)],
        semantics=("parallel",), args=(*outs, *lses, ya_n, gb, head_spread))[0]


def _ffn_up(y, wout, x, g, wg, wu, exchange=()):
    s_len = x.shape[0]
    tm = ROW_TILE

    def body(y_ref, wo_ref, x_ref, g_ref, wg_ref, wu_ref, h_ref, a_ref, b_ref, act_ref, hn_ref):
        hf = x_ref[...] + jnp.dot(y_ref[...], wo_ref[...], preferred_element_type=F32)
        h_ref[...] = hf
        hn = (hf * _rstd(hf) * g_ref[...]).astype(BF16)
        hn_ref[...] = hn
        for col in range(0, D_FF, 256):
            cols = slice(col, col + 256)
            a = lax.dot_general(hn, wg_ref[cols, :], _NT, preferred_element_type=F32)
            b = lax.dot_general(hn, wu_ref[cols, :], _NT, preferred_element_type=F32)
            a_ref[:, cols] = a.astype(BF16)
            b_ref[:, cols] = b.astype(BF16)
            act_ref[:, cols] = (a * jax.nn.sigmoid(a) * b).astype(BF16)

    wide = jax.ShapeDtypeStruct((s_len, D_FF), BF16)
    return _call(
        body, name="ffn_up", grid=(s_len // tm,),
        in_specs=[_rows(tm, D_MODEL), _resident((D_MODEL, D_MODEL)), _rows(tm, D_MODEL), _whole((1, D_MODEL)),
                  _resident((D_FF, D_MODEL)), _resident((D_FF, D_MODEL))],
        out_specs=[_rows(tm, D_MODEL), _rows(tm, D_FF), _rows(tm, D_FF), _rows(tm, D_FF), _rows(tm, D_MODEL)],
        out_shape=[jax.ShapeDtypeStruct((s_len, D_MODEL), F32), wide, wide, wide, jax.ShapeDtypeStruct((s_len, D_MODEL), BF16)],
        semantics=("parallel",), args=(y, wout, x, g, wg, wu), exchange=exchange)


def _ffn_down_ple(act, wd, h1, g, wpg, p, wpp):
    s_len = h1.shape[0]
    tm = ROW_TILE

    def body(act_ref, wd_ref, h1_ref, g_ref, wpg_ref, p_ref, wpp_ref, h2_ref, h3_ref, gate_ref, pp_ref, hn_ref):
        hf = h1_ref[...] + jnp.dot(act_ref[...], wd_ref[...], preferred_element_type=F32)
        h2_ref[...] = hf
        hn = (hf * _rstd(hf) * g_ref[...]).astype(BF16)
        hn_ref[...] = hn
        gate = jax.nn.sigmoid(jnp.dot(hn, wpg_ref[...], preferred_element_type=F32))
        pp = lax.dot_general(p_ref[...].astype(BF16), wpp_ref[...], _NT, preferred_element_type=F32)
        h3_ref[...] = hf + gate * pp
        gate_ref[...] = gate.astype(BF16)
        pp_ref[...] = pp.astype(BF16)

    full = jax.ShapeDtypeStruct((s_len, D_MODEL), F32)
    half = jax.ShapeDtypeStruct((s_len, D_MODEL), BF16)
    return pl.pallas_call(
        body, name="ffn_down_ple", grid=(s_len // tm,),
        in_specs=[_rows(tm, D_FF), _resident((D_FF, D_MODEL)), _rows(tm, D_MODEL), _whole((1, D_MODEL)),
                  _resident((D_MODEL, D_MODEL)), _rows(tm, PLE_DIM), _resident((D_MODEL, PLE_DIM))],
        out_specs=[_rows(tm, D_MODEL)] * 5,
        out_shape=[full, full, half, half, half],
        compiler_params=_params("parallel"),
    )(act, wd, h1, g, wpg, p, wpp)


def _loss_ple_bwd(h3, target, gf, gate, pp, h2, g_ple, wpg, hn3, p):
    s_len = h3.shape[0]
    tm = ROW_TILE
    n_steps = s_len // tm

    def body(h_ref, t_ref, g_ref, gate_ref, pp_ref, h2_ref, gp_ref, w_ref, hn_ref, p_ref,
             dh2_ref, loss_ref, dg_ref, dgp_ref, dwg_ref, dwp_ref, acc_g, acc_p):
        step = pl.program_id(0)

        @pl.when(step == 0)
        def _():
            loss_ref[...] = jnp.zeros_like(loss_ref)
            dg_ref[...] = jnp.zeros_like(dg_ref)
            dgp_ref[...] = jnp.zeros_like(dgp_ref)
            acc_g[...] = jnp.zeros_like(acc_g)
            acc_p[...] = jnp.zeros_like(acc_p)

        hf = h_ref[...]
        gfv = g_ref[...]
        r = _rstd(hf)
        err = hf * r * gfv - t_ref[...]
        loss_ref[...] += 0.5 * jnp.sum(jnp.sum(err * err, axis=-1, keepdims=True), axis=0, keepdims=True) / D_MODEL
        dh, dg_rows = _norm_bwd(err / D_MODEL, hf, gfv, r)
        dg_ref[...] += jnp.sum(dg_rows, axis=0, keepdims=True)
        gate = gate_ref[...].astype(F32)
        dz = (dh * pp_ref[...].astype(F32) * gate * (1.0 - gate)).astype(BF16)
        dpp = (dh * gate).astype(BF16)
        dn = lax.dot_general(dz, w_ref[...], _NT, preferred_element_type=F32)
        dh2, dgp_rows = _norm_bwd(dn, h2_ref[...], gp_ref[...])
        dh2 = dh + dh2
        dh2_ref[...] = dh2
        dgp_ref[...] += jnp.sum(dgp_rows, axis=0, keepdims=True)
        acc_g[...] += lax.dot_general(hn_ref[...], dz, _TN, preferred_element_type=F32)
        acc_p[...] += lax.dot_general(dpp, p_ref[...].astype(BF16), _TN, preferred_element_type=F32)

        @pl.when(step == n_steps - 1)
        def _():
            dwg_ref[...] = acc_g[...].astype(BF16)
            dwp_ref[...] = acc_p[...].astype(BF16)

    gain = jax.ShapeDtypeStruct((1, D_MODEL), F32)
    return pl.pallas_call(
        body, name="loss_ple_bwd", grid=(n_steps,),
        in_specs=[_rows(tm, D_MODEL), _rows(tm, D_MODEL), _whole((1, D_MODEL)), _rows(tm, D_MODEL), _rows(tm, D_MODEL),
                  _rows(tm, D_MODEL), _whole((1, D_MODEL)), _resident((D_MODEL, D_MODEL)), _rows(tm, D_MODEL),
                  _rows(tm, PLE_DIM)],
        out_specs=[_rows(tm, D_MODEL), _whole((1, 128)), _whole((1, D_MODEL)), _whole((1, D_MODEL)),
                   _whole((D_MODEL, D_MODEL)), _whole((D_MODEL, PLE_DIM))],
        out_shape=[jax.ShapeDtypeStruct((s_len, D_MODEL), F32), jax.ShapeDtypeStruct((1, 128), F32), gain, gain,
                   jax.ShapeDtypeStruct((D_MODEL, D_MODEL), BF16), jax.ShapeDtypeStruct((D_MODEL, PLE_DIM), BF16)],
        scratch_shapes=[pltpu.VMEM((D_MODEL, D_MODEL), F32), pltpu.VMEM((D_MODEL, PLE_DIM), F32)],
        compiler_params=_params("arbitrary"),
    )(h3, target, gf, gate, pp, h2, g_ple, wpg, hn3, p)


def _mm_norm_bwd(parts, h, g, dres, name, exchange=(), dw_lhs=None):
    s_len = h.shape[0]
    tm = ROW_TILE
    n_parts = len(parts)
    n_steps = s_len // tm
    has_dw = dw_lhs is not None

    def body(*refs):
        a_refs = refs[0:2 * n_parts:2]
        w_refs = refs[1:2 * n_parts:2]
        h_ref, g_ref, r_ref = refs[2 * n_parts:2 * n_parts + 3]
        rest = refs[2 * n_parts + 3:]
        step = pl.program_id(0)
        if has_dw:
            lhs_ref, o_ref, dg_ref, dw_ref, acc_ref = rest
        else:
            o_ref, dg_ref = rest

        @pl.when(step == 0)
        def _():
            dg_ref[...] = jnp.zeros_like(dg_ref)
            if has_dw:
                acc_ref[...] = jnp.zeros_like(acc_ref)

        dn = jnp.dot(a_refs[0][...], w_refs[0][...], preferred_element_type=F32)
        for a_ref, w_ref in zip(a_refs[1:], w_refs[1:]):
            dn = dn + jnp.dot(a_ref[...], w_ref[...], preferred_element_type=F32)
        dh, dg_rows = _norm_bwd(dn, h_ref[...], g_ref[...])
        out = r_ref[...] + dh
        o_ref[...] = out
        dg_ref[...] += jnp.sum(dg_rows, axis=0, keepdims=True)
        if has_dw:
            acc_ref[...] += lax.dot_general(lhs_ref[...], out.astype(BF16), _TN, preferred_element_type=F32)

            @pl.when(step == n_steps - 1)
            def _():
                dw_ref[...] = acc_ref[...].astype(BF16)

    in_specs, args = [], []
    for a, w in parts:
        in_specs += [_rows(tm, a.shape[1]), _resident(w.shape)]
        args += [a, w]
    in_specs += [_rows(tm, D_MODEL), _whole((1, D_MODEL)), _rows(tm, D_MODEL)]
    args += [h, g, dres]
    out_specs = [_rows(tm, D_MODEL), _whole((1, D_MODEL))]
    out_shape = [jax.ShapeDtypeStruct((s_len, D_MODEL), F32), jax.ShapeDtypeStruct((1, D_MODEL), F32)]
    scratch = []
    if has_dw:
        m = dw_lhs.shape[1]
        in_specs.append(_rows(tm, m))
        args.append(dw_lhs)
        out_specs.append(_whole((m, D_MODEL)))
        out_shape.append(jax.ShapeDtypeStruct((m, D_MODEL), BF16))
        scratch.append(pltpu.VMEM((m, D_MODEL), F32))
    return _call(
        body, name=name, grid=(n_steps,), in_specs=in_specs, out_specs=out_specs, out_shape=out_shape,
        scratch_shapes=scratch, semantics=("arbitrary",), args=tuple(args), exchange=exchange)


def _ffn_down_bwd(dh, wdt, a, b, exchange=()):
    s_len = dh.shape[0]
    tm = ROW_TILE

    def body(dh_ref, w_ref, a_ref, b_ref, da_ref, db_ref):
        dhb = dh_ref[...].astype(BF16)
        for col in range(0, D_FF, 512):
            cols = slice(col, min(col + 512, D_FF))
            dact = lax.dot_general(dhb, w_ref[cols, :], _NT, preferred_element_type=F32)
            av = a_ref[:, cols].astype(F32)
            bv = b_ref[:, cols].astype(F32)
            sig = jax.nn.sigmoid(av)
            t = dact * sig
            silu = av * sig
            da_ref[:, cols] = (t * bv * (1.0 + av - silu)).astype(BF16)
            db_ref[:, cols] = (dact * silu).astype(BF16)

    wide = jax.ShapeDtypeStruct((s_len, D_FF), BF16)
    return _call(
        body, name="ffn_down_bwd", grid=(s_len // tm,),
        in_specs=[_rows(tm, D_MODEL), _resident((D_FF, D_MODEL)), _rows(tm, D_FF), _rows(tm, D_FF)],
        out_specs=[_rows(tm, D_FF), _rows(tm, D_FF)],
        out_shape=[wide, wide],
        semantics=("parallel",), args=(dh, wdt, a, b), exchange=exchange)


def _outproj_bwd(dh1, woutt, yb, gb, head_sum):
    s_len = dh1.shape[0]
    tm = ROW_TILE
    n_br = len(DILATIONS)

    def body(dh_ref, w_ref, yb_ref, gb_ref, e_ref, dya_ref, dgb_ref, *rest):
        do_refs, dd_refs = rest[:n_br], rest[n_br:2 * n_br]
        do_nat, dd_nat = rest[2 * n_br:]

        @pl.when(pl.program_id(0) == 0)
        def _():
            dgb_ref[...] = jnp.zeros_like(dgb_ref)

        dhb = dh_ref[...].astype(BF16)
        dya_ref[...] = lax.dot_general(dhb, w_ref[:WIDTH_A, :], _NT, preferred_element_type=F32)
        dyn = lax.dot_general(dhb, w_ref[WIDTH_A:, :], _NT, preferred_element_type=F32)
        ybv = yb_ref[...]
        dyb, dg_rows = _norm_bwd(dyn, ybv, gb_ref[...])
        dgb_ref[...] += jnp.sum(dg_rows, axis=0, keepdims=True)
        prod = dyb * ybv
        hi = prod.astype(BF16)
        lo = (prod - hi.astype(F32)).astype(BF16)
        dd_nat[0] = (jnp.dot(hi, e_ref[...], preferred_element_type=F32)
                     + jnp.dot(lo, e_ref[...], preferred_element_type=F32))
        for i, d in enumerate(DILATIONS):
            _to_sub(dd_nat, 0, dd_refs[i], 0, d, tm)
        for cb in range(N_PAIRS):
            piece = dyb[:, cb * 128:(cb + 1) * 128]
            do_nat[cb] = piece
            do_refs[0][0, cb] = piece.astype(BF16)
            for i, d in enumerate(DILATIONS[1:], start=1):
                _to_sub(do_nat, cb, do_refs[i], cb, d, tm)

    return _call(
        body, name="outproj_bwd", grid=(s_len // tm,),
        in_specs=[_rows(tm, D_MODEL), _resident((D_MODEL, D_MODEL)), _rows(tm, WIDTH_B), _whole((1, WIDTH_B)), _whole((WIDTH_B, 128))],
        out_specs=[_rows(tm, WIDTH_A), _whole((1, WIDTH_B))] + [_sub_spec(d, N_PAIRS, tm) for d in DILATIONS]
        + [_sub_spec(d, 1, tm) for d in DILATIONS],
        out_shape=[jax.ShapeDtypeStruct((s_len, WIDTH_A), F32), jax.ShapeDtypeStruct((1, WIDTH_B), F32)]
        + [_sub_shape(s_len, d, N_PAIRS, BF16) for d in DILATIONS] + [_sub_shape(s_len, d, 1, F32) for d in DILATIONS],
        scratch_shapes=[pltpu.VMEM((N_PAIRS, tm, 128), F32), pltpu.VMEM((1, tm, 128), F32)],
        semantics=("arbitrary",), args=(dh1, woutt, yb, gb, head_sum))[0]


def _attn_bwd(qkv, do, lse, dd, d, exchange=()):
    sd = qkv.shape[2]
    tile, nb, n_tiles, res = _attn_geometry(sd)
    last_block = sd // CHUNK - 1

    def nxt(n):
        return jnp.minimum((n + 1) * nb, last_block)

    def block(ref, next_ref, j):
        return ref[j * CHUNK:(j + 1) * CHUNK, :] if j < nb else next_ref[...]

    def body(q_ref, qn_ref, k_ref, v_ref, do_ref, don_ref, l_ref, ln_ref, dd_ref, ddn_ref,
             dq_ref, dk_ref, dv_ref, carry_ref):
        for rr in range(res):
            l_t = [block(l_ref.at[rr], ln_ref.at[rr], j).T for j in range(nb + 1)]
            dd_t = [block(dd_ref.at[rr], ddn_ref.at[rr], j).T for j in range(nb + 1)]
            for hp in range(PAIRS_PER_STEP):
                l_rows = [jnp.concatenate([t[2 * hp:2 * hp + 1, :], t[2 * hp + 1:2 * hp + 2, :]], axis=1) for t in l_t]
                dd_rows = [jnp.concatenate([t[2 * hp:2 * hp + 1, :], t[2 * hp + 1:2 * hp + 2, :]], axis=1) for t in dd_t]
                one_pair(q_ref.at[rr, hp], qn_ref.at[rr, hp], k_ref.at[rr, hp], v_ref.at[rr, hp], do_ref.at[rr, hp],
                         don_ref.at[rr, hp], l_rows, dd_rows, dq_ref.at[rr, hp], dk_ref.at[rr, hp], dv_ref.at[rr, hp],
                         carry_ref.at[rr, hp])

    def one_pair(q_ref, qn_ref, k_ref, v_ref, do_ref, don_ref, l_rows, dd_rows, dq_ref, dk_ref, dv_ref, carry_ref):
        n = pl.program_id(2)

        @pl.when(n == 0)
        def _():
            carry_ref[...] = jnp.zeros_like(carry_ref)

        head_a = lax.broadcasted_iota(jnp.int32, (CHUNK, 128), 1) < HEAD_DIM
        col = lax.broadcasted_iota(jnp.int32, (CHUNK, 4 * CHUNK), 1)
        qi = col % CHUNK
        ki = lax.broadcasted_iota(jnp.int32, (CHUNK, 4 * CHUNK), 0)
        is_after = col >= 2 * CHUNK
        mask = (is_after & (ki >= qi)) | (jnp.logical_not(is_after) & (qi >= ki))
        mask_last = mask & jnp.logical_or(jnp.logical_not(is_after), n < n_tiles - 1)
        dq_acc = [carry_ref[...]] + [jnp.zeros((CHUNK, 128), F32) for _ in range(nb)]

        q_st = [jnp.concatenate(_both_heads(block(q_ref, qn_ref, j), head_a), axis=0) for j in range(nb + 1)]
        do_st = [jnp.concatenate(_both_heads(block(do_ref, don_ref, j), head_a), axis=0) for j in range(nb + 1)]

        for j in range(nb):
            rows = slice(j * CHUNK, (j + 1) * CHUNK)
            kj = k_ref[rows, :]
            vj = v_ref[rows, :]
            msk = mask if j + 1 < nb else mask_last
            qs = jnp.concatenate([q_st[j], q_st[j + 1]], axis=0)
            dos = jnp.concatenate([do_st[j], do_st[j + 1]], axis=0)
            ls = jnp.concatenate([l_rows[j], l_rows[j + 1]], axis=1)
            dds = jnp.concatenate([dd_rows[j], dd_rows[j + 1]], axis=1)
            st = lax.dot_general(kj, qs, _NT, preferred_element_type=F32)
            pt = jnp.exp(jnp.where(msk, st - ls, NEG))
            dpt = lax.dot_general(vj, dos, _NT, preferred_element_type=F32)
            dst = (pt * (dpt - dds)).astype(BF16)
            dv_ref[rows, :] = jnp.dot(pt.astype(BF16), dos, preferred_element_type=F32).astype(BF16)
            dk_ref[rows, :] = jnp.dot(dst, qs, preferred_element_type=F32).astype(BF16)
            dqs = lax.dot_general(dst, kj, _TN, preferred_element_type=F32)
            dq_acc[j] = dq_acc[j] + jnp.where(head_a, dqs[:CHUNK], dqs[CHUNK:2 * CHUNK])
            dq_acc[j + 1] = dq_acc[j + 1] + jnp.where(head_a, dqs[2 * CHUNK:3 * CHUNK], dqs[3 * CHUNK:])
        for j in range(nb):
            dq_ref[j * CHUNK:(j + 1) * CHUNK, :] = dq_acc[j].astype(BF16)
        carry_ref[...] = dq_acc[nb]

    same = lambda n: n
    grad = jax.ShapeDtypeStruct((d, N_PAIRS, sd, 128), BF16)
    return _call(
        body, name=f"attn_bwd_d{d}", grid=(d // res, N_PAIRS // PAIRS_PER_STEP, n_tiles),
        in_specs=[_attn_spec(0, tile, same, res), _attn_spec(0, CHUNK, nxt, res), _attn_spec(N_PAIRS, tile, same, res),
                  _attn_spec(2 * N_PAIRS, tile, same, res), _attn_spec(0, tile, same, res), _attn_spec(0, CHUNK, nxt, res),
                  _stats_spec(tile, same, res), _stats_spec(CHUNK, nxt, res), _stats_spec(tile, same, res),
                  _stats_spec(CHUNK, nxt, res)],
        out_specs=[_attn_spec(0, tile, same, res)] * 3,
        out_shape=[grad, grad, grad],
        scratch_shapes=[pltpu.VMEM((res, PAIRS_PER_STEP, CHUNK, 128), F32)],
        semantics=("parallel", "parallel", "arbitrary"), args=(qkv, qkv, qkv, qkv, do, do, lse, lse, dd, dd), exchange=exchange)


def _sgu_bwd(ua, sw, b2, gs, ga, dya_n):
    s_len = ua.shape[0]
    tm = ROW_TILE

    def body(ua_ref, sw_ref, b2_ref, gs_ref, ga_ref, dy_ref, dua_ref, dsw_ref, db2_ref, dgs_ref, dga_ref):
        @pl.when(pl.program_id(0) == 0)
        def _():
            dsw_ref[...] = jnp.zeros_like(dsw_ref)
            db2_ref[...] = jnp.zeros_like(db2_ref)
            dgs_ref[...] = jnp.zeros_like(dgs_ref)
            dga_ref[...] = jnp.zeros_like(dga_ref)

        u, va, ug, xhat, rstd, vn = _sgu_core(ua_ref, gs_ref)
        wm, keep = _sgu_mix_weights(sw_ref)
        head_a = lax.broadcasted_iota(jnp.int32, (CHUNK, 128), 1) < HEAD_DIM
        gav = ga_ref[...]
        gsv = gs_ref[...]
        dga = jnp.zeros((1, WIDTH_A), F32)
        dgs = jnp.zeros((1, WIDTH_A), F32)
        db2 = jnp.zeros((CHUNK, WIDTH_A), F32)
        dsw = [jnp.zeros((CHUNK, CHUNK), F32) for _ in range(4)]
        for c in range(tm // CHUNK):
            rows = slice(c * CHUNK, (c + 1) * CHUNK)
            vnb = vn[rows].astype(BF16)
            mixed = b2_ref[...] + jnp.concatenate([_sgu_mix(wm, vnb, half, head_a) for half in range(2)], axis=1)
            ugc = ug[rows]
            dya, dga_rows = _norm_bwd(dy_ref[rows, :], ugc * mixed, gav)
            dga = dga + jnp.sum(dga_rows, axis=0, keepdims=True)
            dmixed = dya * ugc
            db2 = db2 + dmixed
            dmb = dmixed.astype(BF16)
            dvn_halves = []
            for half in range(2):
                lanes = slice(half * 128, (half + 1) * 128)
                dm_heads = _both_heads(dmb[:, lanes], head_a)
                dvn_half = jnp.zeros((CHUNK, 128), F32)
                for k, dmh in enumerate(dm_heads):
                    h = 2 * half + k
                    dsw[h] = dsw[h] + lax.dot_general(dmh, vnb[:, lanes], _NT, preferred_element_type=F32)
                    dvn_half = dvn_half + lax.dot_general(wm[h], dmh, _TN, preferred_element_type=F32)
                dvn_halves.append(dvn_half)
            dvn = jnp.concatenate(dvn_halves, axis=1)
            xh = xhat[rows]
            dgs = dgs + jnp.sum(dvn * xh, axis=0, keepdims=True)
            dxh = dvn * gsv
            dvg = rstd[rows] * (dxh - jnp.mean(dxh, axis=-1, keepdims=True) - xh * jnp.mean(dxh * xh, axis=-1, keepdims=True))
            dua_ref[rows, :WIDTH_A] = (dya * mixed * _gelu_grad(u[rows])).astype(BF16)
            dua_ref[rows, WIDTH_A:] = (dvg * _gelu_grad(va[rows])).astype(BF16)
        for h in range(4):
            dsw_ref[h] += jnp.where(keep, dsw[h], 0.0)
        db2_ref[...] += db2
        dgs_ref[...] += dgs
        dga_ref[...] += dga

    return pl.pallas_call(
        body, name="sgu_bwd", grid=(s_len // tm,),
        in_specs=[_rows(tm, 2 * WIDTH_A), _whole((4, CHUNK, CHUNK)), _whole((CHUNK, WIDTH_A)), _whole((1, WIDTH_A)),
                  _whole((1, WIDTH_A)), _rows(tm, WIDTH_A)],
        out_specs=[_rows(tm, 2 * WIDTH_A), _whole((4, CHUNK, CHUNK)), _whole((CHUNK, WIDTH_A)), _whole((1, WIDTH_A)), _whole((1, WIDTH_A))],
        out_shape=[jax.ShapeDtypeStruct((s_len, 2 * WIDTH_A), BF16), jax.ShapeDtypeStruct((4, CHUNK, CHUNK), F32),
                   jax.ShapeDtypeStruct((CHUNK, WIDTH_A), F32), jax.ShapeDtypeStruct((1, WIDTH_A), F32),
                   jax.ShapeDtypeStruct((1, WIDTH_A), F32)],
        compiler_params=_params("arbitrary"),
    )(ua, sw, b2, gs, ga, dya_n)


def _dproj(dua, dqs, dks, dvs, cos, sin, hn1, exchange=()):
    s_len = dua.shape[0]
    tm = ROW_TILE
    n_br = len(DILATIONS)
    n_steps = s_len // tm

    def body(dua_ref, *rest):
        groups = [rest[g * n_br:(g + 1) * n_br] for g in range(3)]
        cos_ref, sin_ref, hn_ref, out_ref, dw_ref, acc, dw_acc = rest[3 * n_br:]
        step = pl.program_id(0)

        @pl.when(step == 0)
        def _():
            dw_acc[...] = jnp.zeros_like(dw_acc)

        out_ref[:, :2 * WIDTH_A] = dua_ref[...]
        c = cos_ref[...]
        s = sin_ref[...]
        first_half = (lax.broadcasted_iota(jnp.int32, (tm, 128), 1) % HEAD_DIM) < HEAD_DIM // 2
        for g, refs in enumerate(groups):
            for cb in range(N_PAIRS):
                t = refs[0][0, cb].astype(F32)
                for i, d in enumerate(DILATIONS[1:]):
                    _from_sub(refs[i + 1], cb, acc, i, d, tm)
                    t = t + acc[i]
                if g < 2:
                    t = (t * c - _swap_halves(t, first_half) * s) * (0.125 if g == 0 else 1.0)
                col = 2 * WIDTH_A + g * WIDTH_B + cb * 128
                out_ref[:, col:col + 128] = t.astype(BF16)
        hn = hn_ref[...]
        for j in range(IN_COLS // 256):
            cols = slice(j * 256, (j + 1) * 256)
            dw_acc[cols, :] += lax.dot_general(out_ref[:, cols], hn, _TN, preferred_element_type=F32)

        @pl.when(step == n_steps - 1)
        def _():
            dw_ref[...] = dw_acc[...].astype(BF16)

    subs = [_sub_spec(d, N_PAIRS, tm) for d in DILATIONS]
    (dproj, dw), received = _call(
        body, name="dproj_dw_in", grid=(n_steps,),
        in_specs=[_rows(tm, 2 * WIDTH_A)] + subs * 3 + [_rows(tm, 128), _rows(tm, 128), _rows(tm, D_MODEL)],
        out_specs=[_rows(tm, IN_COLS), _whole((IN_COLS, D_MODEL))],
        out_shape=[jax.ShapeDtypeStruct((s_len, IN_COLS), BF16), jax.ShapeDtypeStruct((IN_COLS, D_MODEL), BF16)],
        scratch_shapes=[pltpu.VMEM((n_br - 1, tm, 128), F32), pltpu.VMEM((IN_COLS, D_MODEL), F32)],
        semantics=("arbitrary",), args=(dua, *dqs, *dks, *dvs, cos, sin, hn1), exchange=exchange)
    return dproj, dw, received


def _mm_tn(a, b, name):
    s_len, m = a.shape
    n = b.shape[1]
    tk = 2 * ROW_TILE
    tm = m
    n_k = s_len // tk

    def body(a_ref, b_ref, o_ref, acc_ref):
        k = pl.program_id(1)

        @pl.when(k == 0)
        def _():
            acc_ref[...] = jnp.zeros_like(acc_ref)

        bb = b_ref[...].astype(BF16)
        for j in range(tm // 256):
            rows = slice(j * 256, (j + 1) * 256)
            acc_ref[rows, :] += lax.dot_general(a_ref[:, rows].astype(BF16), bb, _TN, preferred_element_type=F32)

        @pl.when(k == n_k - 1)
        def _():
            o_ref[...] = acc_ref[...].astype(BF16)

    return pl.pallas_call(
        body, name=name, grid=(m // tm, n_k),
        in_specs=[pl.BlockSpec((tk, tm), lambda i, k: (k, i)), pl.BlockSpec((tk, n), lambda i, k: (k, 0))],
        out_specs=pl.BlockSpec((tm, n), lambda i, k: (i, 0)),
        out_shape=jax.ShapeDtypeStruct((m, n), BF16),
        scratch_shapes=[pltpu.VMEM((tm, n), F32)],
        compiler_params=_params("parallel", "arbitrary"),
    )(a, b)


def _position():
    x, y, c = lax.axis_index("x"), lax.axis_index("y"), lax.axis_index("c")
    return x, y, c, 4 * x + 2 * y + c


def _peer(x, y, c, rel):
    return (x ^ ((rel >> 2) & 1), y ^ ((rel >> 1) & 1), c ^ (rel & 1))


def _exchange_out_shape(kind, arr):
    return jax.ShapeDtypeStruct(((N_DEV,) + arr.shape) if kind == "gather" else arr.shape, arr.dtype)


def _exchange_sems(n_items):
    return [pltpu.SemaphoreType.DMA((n_items, N_DEV)), pltpu.SemaphoreType.DMA((n_items, N_DEV)), pltpu.SemaphoreType.DMA((n_items,))]


def _exchange_copies(kinds, srcs, dsts, sems, arrivals):
    send_sems, recv_sems, local_sems = sems
    x, y, c, me = _position()
    local, sends, recvs = [], [], []
    for k, (kind, src, dst) in enumerate(zip(kinds, srcs, dsts)):
        own = src if kind == "gather" else src.at[me]
        local.append(pltpu.make_async_copy(own, dst.at[me], local_sems.at[k]))
        for rel in range(1, N_DEV):
            going = src if kind == "gather" else src.at[me ^ rel]
            common = dict(send_sem=send_sems.at[k, rel], recv_sem=recv_sems.at[k, rel],
                          device_id=_peer(x, y, c, rel), device_id_type=MESH)
            sends.append(pltpu.make_async_remote_copy(src_ref=going, dst_ref=dst.at[me], **common))
            if arrivals:
                recvs.append(pltpu.make_async_remote_copy(src_ref=own, dst_ref=dst.at[me ^ rel], **common))
    return local, sends, recvs


def _exchange_start(kinds, srcs, dsts, sems):
    local, sends, _ = _exchange_copies(kinds, srcs, dsts, sems, arrivals=False)
    for cp in local + sends:
        cp.start()


def _exchange_finish(kinds, srcs, dsts, sems):
    local, sends, recvs = _exchange_copies(kinds, srcs, dsts, sems, arrivals=True)
    for cp in recvs:
        cp.wait_recv()
    for cp in sends:
        cp.wait_send()
    for cp in local:
        cp.wait()


def _gather_two_level_with_rope_tables(shard, inv_freq, s_len, name):
    rows = ROW_TILE

    def body(inv_ref, src, cos_ref, sin_ref, dst, send_sems, recv_sems, local_sem):
        x, y, c, me = _position()
        sibling = (x, y, 1 - c)
        chips = [(1 - x, y), (x, 1 - y), (1 - x, 1 - y)]

        def block(px, py, pc):
            return dst.at[4 * px + 2 * py + pc]

        def copy(k, blk, to, src_ref=None):
            return pltpu.make_async_remote_copy(
                src_ref=block(*blk) if src_ref is None else src_ref, dst_ref=block(*blk),
                send_sem=send_sems.at[k], recv_sem=recv_sems.at[k], device_id=to, device_id_type=MESH)

        x_nbr, y_nbr, diag = chips
        mine = pltpu.make_async_copy(src, dst.at[me], local_sem)
        mine.start()
        first = [copy(0, (x, y, c), sibling, src), copy(1, (x, y, c), (*x_nbr, c), src), copy(2, (x, y, c), (*y_nbr, c), src)]
        for cp in first:
            cp.start()

        inv = inv_ref[...]
        lane = lax.broadcasted_iota(jnp.int32, (rows, 128), 1)
        sign = jnp.where((lane // (HEAD_DIM // 2)) % 2 == 0, -1.0, 1.0)
        row = lax.broadcasted_iota(jnp.int32, (rows, 128), 0)
        n_chunks = s_len // rows

        def fill_tables(lo, hi):
            @pl.loop(lo, hi)
            def _(i):
                at = pl.multiple_of(i * rows, rows)
                ang = (row + at).astype(F32) * inv
                cos_ref[pl.ds(at, rows), :] = jnp.cos(ang)
                sin_ref[pl.ds(at, rows), :] = jnp.sin(ang) * sign

        fill_tables(0, n_chunks // 2)
        passed = [copy(4 + j, (*chip, c), sibling) for j, chip in enumerate(chips)]
        copy(1, (*x_nbr, c), (x, y, c)).wait_recv()
        copy(2, (*y_nbr, c), (x, y, c)).wait_recv()

        @pl.when(c == 1)
        def _():
            copy(3, (*x_nbr, c), (*y_nbr, c)).start()

        @pl.when(c == 0)
        def _():
            copy(3, (*y_nbr, c), (*x_nbr, c)).start()

        passed[0].start()
        passed[1].start()
        fill_tables(n_chunks // 2, n_chunks)
        copy(3, (*diag, c), (x, y, c)).wait_recv()
        passed[2].start()
        copy(0, (x, y, 1 - c), (x, y, c)).wait_recv()
        for j, chip in enumerate(chips):
            copy(4 + j, (*chip, 1 - c), (x, y, c)).wait_recv()
        for cp in first + passed:
            cp.wait_send()
        copy(3, (*x_nbr, c), (*y_nbr, c)).wait_send()
        mine.wait()

    any_spec = pl.BlockSpec(memory_space=pl.ANY)
    vmem = pl.BlockSpec(memory_space=pltpu.VMEM)
    table = jax.ShapeDtypeStruct((s_len, 128), F32)
    return pl.pallas_call(
        body, name=name, in_specs=[vmem, any_spec], out_specs=[vmem, vmem, any_spec],
        out_shape=[table, table, _exchange_out_shape("gather", shard)],
        scratch_shapes=[pltpu.SemaphoreType.DMA((N_DEV - 1,)), pltpu.SemaphoreType.DMA((N_DEV - 1,)), pltpu.SemaphoreType.DMA],
        compiler_params=pltpu.CompilerParams(has_side_effects=True, vmem_limit_bytes=V7X_VMEM_LIMIT_BYTES),
    )(inv_freq, shard)


def _call(body, *, name, grid, in_specs, out_specs, out_shape, args, scratch_shapes=(), semantics, exchange=()):
    if not exchange:
        outs = pl.pallas_call(body, name=name, grid=grid, in_specs=in_specs, out_specs=out_specs, out_shape=out_shape,
                              scratch_shapes=list(scratch_shapes), compiler_params=_params(*semantics))(*args)
        return outs, []
    kinds = [k for k, _ in exchange]
    n_in, n_out, n_x, n_scr = len(in_specs), len(out_specs), len(exchange), len(scratch_shapes)

    def wrapped(*refs):
        ins, refs = refs[:n_in], refs[n_in:]
        srcs, refs = refs[:n_x], refs[n_x:]
        outs, refs = refs[:n_out], refs[n_out:]
        dsts, refs = refs[:n_x], refs[n_x:]
        scratch, sems = refs[:n_scr], refs[n_scr:]
        ids = [pl.program_id(a) for a in range(len(grid))]
        first = functools.reduce(jnp.logical_and, [i == 0 for i in ids])
        last = functools.reduce(jnp.logical_and, [i == g - 1 for i, g in zip(ids, grid)])

        @pl.when(first)
        def _():
            _exchange_start(kinds, srcs, dsts, sems)

        body(*ins, *outs, *scratch)

        @pl.when(last)
        def _():
            _exchange_finish(kinds, srcs, dsts, sems)

    any_spec = pl.BlockSpec(memory_space=pl.ANY)
    outs = pl.pallas_call(
        wrapped, name=name, grid=grid,
        in_specs=list(in_specs) + [any_spec] * n_x, out_specs=list(out_specs) + [any_spec] * n_x,
        out_shape=list(out_shape) + [_exchange_out_shape(k, a) for k, a in exchange],
        scratch_shapes=list(scratch_shapes) + _exchange_sems(n_x),
        compiler_params=pltpu.CompilerParams(dimension_semantics=("arbitrary",) * len(grid),
                                             vmem_limit_bytes=V7X_VMEM_LIMIT_BYTES, has_side_effects=True),
    )(*args, *[a for _, a in exchange])
    return outs[:n_out], outs[n_out:]


def _adamw_math(w, g, m, v):
    m = ADAM_B1 * m + (1.0 - ADAM_B1) * g
    v = ADAM_B2 * v + (1.0 - ADAM_B2) * (g * g)
    m_hat = m / (1.0 - ADAM_B1 ** ADAM_STEP)
    v_hat = v / (1.0 - ADAM_B2 ** ADAM_STEP)
    return -ADAM_LR * (m_hat / (jnp.sqrt(v_hat) + ADAM_EPS) + ADAM_WD * w), m, v


def _adamw(parts, w, m, v, name, exchange=()):
    rows, cols = w.shape
    tm = 256 if rows % 256 == 0 and rows > 256 else rows

    def body(p_ref, w_ref, m_ref, v_ref, g_ref, d_ref, nm_ref, nv_ref):
        g = p_ref[0].astype(F32)
        for j in range(1, N_DEV):
            g = g + p_ref[j].astype(F32)
        delta, nm, nv = _adamw_math(w_ref[...], g, m_ref[...], v_ref[...])
        g_ref[...] = g
        d_ref[...] = delta
        nm_ref[...] = nm
        nv_ref[...] = nv

    shard = jax.ShapeDtypeStruct((rows, cols), F32)
    return _call(
        body, name=name, grid=(rows // tm,),
        in_specs=[pl.BlockSpec((N_DEV, tm, cols), lambda i: (0, i, 0))] + [_rows(tm, cols)] * 3,
        out_specs=[_rows(tm, cols)] * 4,
        out_shape=[shard] * 4,
        semantics=("parallel",), args=(parts, w, m, v), exchange=exchange)


_SMALL = ("mix_norm_g", "sgu_w", "sgu_b", "sgu_norm_g", "out_norm_a", "out_norm_b", "ffn_norm_g", "ple_norm_g", "final_norm_g")
_BIG = ("w_in", "w_out", "w_gate", "w_up", "w_down", "w_ple_gate", "w_ple_proj")
_COLUMN_SHARDED = ("w_in", "w_gate", "w_up", "w_ple_proj")
_ORDER = ("mix_norm_g", "w_in", "sgu_w", "sgu_b", "sgu_norm_g", "out_norm_a", "out_norm_b", "w_out", "ffn_norm_g",
          "w_gate", "w_up", "w_down", "ple_norm_g", "w_ple_gate", "w_ple_proj", "final_norm_g")


def _pack_small(values, names=_SMALL):
    flat = jnp.concatenate([values[n].reshape(-1).astype(F32) for n in names])
    pad = (-flat.shape[0]) % (8 * 128)
    return jnp.pad(flat, (0, pad)).reshape(-1, 128)


def _unpack_small(packed, like):
    flat = packed.reshape(-1)
    out, at = {}, 0
    for n in _SMALL:
        size = like[n].size
        out[n] = flat[at:at + size].reshape(like[n].shape)
        at += size
    return out


def _own_orientation(name, value):
    return value[0].T if name in _COLUMN_SHARDED else value[0]


def _reference_orientation(name, value):
    return (value.T if name in _COLUMN_SHARDED else value)[None]


def _full_from_gathered(gathered):
    return gathered.reshape(N_DEV * gathered.shape[1], gathered.shape[2])


def _sliced_for_devices(grad):
    return grad.reshape(N_DEV, grad.shape[0] // N_DEV, grad.shape[1])


def _rope_inv_freq():
    half = HEAD_DIM // 2
    inv = ROPE_THETA ** (-jnp.arange(half, dtype=F32) / half)
    return jnp.tile(inv, 128 // half)[None, :]


def _forward_backward(x, p, target, small, shards):
    def gather(*names):
        return [("gather", shards[n]) for n in names]

    def scatter(**grads):
        return [("scatter", _sliced_for_devices(g)) for g in grads.values()]

    full, parts = {}, {}
    s_len = x.shape[0]
    cos, sin, got = _gather_two_level_with_rope_tables(shards["w_in"], _rope_inv_freq(), s_len, "gather_w_in")
    full["w_in"] = _full_from_gathered(got)

    g_mix, g_ffn, g_ple = small["mix_norm_g"], small["ffn_norm_g"], small["ple_norm_g"]
    g_fin = small["final_norm_g"].reshape(1, D_MODEL)
    sw, gs, ga, gb = small["sgu_w"], small["sgu_norm_g"], small["out_norm_a"], small["out_norm_b"]
    b2 = jnp.repeat(small["sgu_b"].T, HEAD_DIM, axis=1)
    head_sum = (jnp.arange(WIDTH_B)[:, None] // HEAD_DIM == jnp.arange(128)[None, :]).astype(BF16)
    n_br = len(DILATIONS)

    def arrived(names, got):
        for n, g in zip(names, got):
            full[n] = _full_from_gathered(g)

    (ua, hn1, *qkv), got = _inproj(x, g_mix, full["w_in"], cos, sin, exchange=gather("w_gate"))
    arrived(("w_gate",), got)
    ya_n = _sgu_fwd(ua, sw, b2, gs, ga)
    half = shards["w_up"].shape[0] // 2
    riders = [[("gather", shards["w_up"][:half])], [("gather", shards["w_up"][half:])], gather("w_out")]
    branch, got = [], []
    for i, d in enumerate(DILATIONS):
        o_l, g = _attn_fwd(qkv[i], d, exchange=riders[i])
        branch.append(o_l)
        got += g
    arrived(("w_up", "w_out"), [jnp.concatenate(got[:2], axis=1), got[2]])
    y, yb, *lse = _combine([o for o, _ in branch], [l for _, l in branch], ya_n, gb, head_sum.T)
    last_wave = ("w_down", "w_ple_gate", "w_ple_proj")
    (h1, a, b, act, hn2), got = _ffn_up(y, full["w_out"], x, g_ffn, full["w_gate"], full["w_up"], exchange=gather(*last_wave))
    arrived(last_wave, got)
    h2, h3, gate, pp, hn3 = _ffn_down_ple(act, full["w_down"], h1, g_ple, full["w_ple_gate"], p, full["w_ple_proj"])

    dh2, loss, d_fin, d_ple, g_ple_gate, g_ple_proj = _loss_ple_bwd(
        h3, target, g_fin, gate, pp, h2, g_ple, full["w_ple_gate"], hn3, p)
    g_down = _mm_tn(act, dh2, "dw_down")
    (da, db), (parts["w_down"],) = _ffn_down_bwd(dh2, full["w_down"], a, b, exchange=scatter(w_down=g_down))
    g_gate = _mm_tn(da, hn2, "dw_gate")
    g_up = _mm_tn(db, hn2, "dw_up")
    (dh1, d_ffn, g_out), (parts["w_gate"], parts["w_ple_gate"], parts["w_ple_proj"]) = _mm_norm_bwd(
        [(da, full["w_gate"]), (db, full["w_up"])], h1, g_ffn, dh2, "ffn_up_bwd",
        exchange=scatter(w_gate=g_gate, w_ple_gate=g_ple_gate, w_ple_proj=g_ple_proj), dw_lhs=y)
    dya_n, d_gb, *do_dd = _outproj_bwd(dh1, full["w_out"], yb, gb, head_sum)
    grads_b = []
    for i, d in enumerate(DILATIONS):
        g3, got = _attn_bwd(qkv[i], do_dd[i], lse[i], do_dd[n_br + i], d,
                            exchange=scatter(w_up=g_up, w_out=g_out) if i == 0 else ())
        grads_b.append(g3)
        if i == 0:
            parts["w_up"], parts["w_out"] = got
    dua, d_sw, d_b2, d_gs, d_ga = _sgu_bwd(ua, sw, b2, gs, ga, dya_n)
    early = {
        "sgu_w": d_sw, "sgu_b": d_b2.reshape(CHUNK, 4, HEAD_DIM).sum(axis=-1).T, "sgu_norm_g": d_gs, "out_norm_a": d_ga,
        "out_norm_b": d_gb, "ffn_norm_g": d_ffn, "ple_norm_g": d_ple, "final_norm_g": d_fin,
    }
    dproj, g_in, (early_parts,) = _dproj(
        dua, [g[0] for g in grads_b], [g[1] for g in grads_b], [g[2] for g in grads_b], cos, sin, hn1,
        exchange=[("gather", _pack_small(early, _SMALL[1:]))])
    (dx, d_mix), (parts["w_in"],) = _mm_norm_bwd(
        [(dproj, full["w_in"])], x, g_mix, dh1, "inproj_bwd", exchange=scatter(w_in=g_in))
    late = jnp.concatenate([_pack_small({"mix_norm_g": d_mix}, _SMALL[:1]), jnp.broadcast_to(loss, (8, 128))])
    return dx, parts, early_parts, late


def kernel(x, p, mix_norm_g, w_in, sgu_w, sgu_b, sgu_norm_g, out_norm_a, out_norm_b, w_out, ffn_norm_g, w_gate, w_up, w_down, ple_norm_g, w_ple_gate, w_ple_proj, final_norm_g, loss_target, m_mix_norm_g, m_w_in, m_sgu_w, m_sgu_b, m_sgu_norm_g, m_out_norm_a, m_out_norm_b, m_w_out, m_ffn_norm_g, m_w_gate, m_w_up, m_w_down, m_ple_norm_g, m_w_ple_gate, m_w_ple_proj, m_final_norm_g, v_mix_norm_g, v_w_in, v_sgu_w, v_sgu_b, v_sgu_norm_g, v_out_norm_a, v_out_norm_b, v_w_out, v_ffn_norm_g, v_w_gate, v_w_up, v_w_down, v_ple_norm_g, v_w_ple_gate, v_w_ple_proj, v_final_norm_g):
    given = dict(locals())
    weights = {n: given[n] for n in _ORDER}
    moments_m = {n: given["m_" + n] for n in _ORDER}
    moments_v = {n: given["v_" + n] for n in _ORDER}

    shards = {n: _own_orientation(n, weights[n]).astype(BF16) for n in _BIG}
    small = {n: (weights[n][0] if n in ("sgu_w", "sgu_b") else weights[n]) for n in _SMALL}

    dx, parts, early_parts, late = _forward_backward(x[0], p[0, 0], loss_target[0], small, shards)

    small_like = {n: weights[n] for n in _SMALL}
    grads, deltas, new_m, new_v = {}, {}, {}, {}
    for n in _BIG:
        outs, got = _adamw(parts[n], _own_orientation(n, weights[n]), _own_orientation(n, moments_m[n]),
                           _own_orientation(n, moments_v[n]), "adamw_" + n, exchange=[("gather", late)] if n == "w_in" else ())
        grads[n], deltas[n], new_m[n], new_v[n] = [_reference_orientation(n, o) for o in outs]
        if n == "w_in":
            (late_parts,) = got
    loss = jnp.sum(late_parts[:, 8, 0])
    small_parts = jnp.concatenate([late_parts[:, :8], early_parts], axis=1)
    (g, d, nm, nv), _ = _adamw(small_parts, _pack_small(small_like), _pack_small({n: moments_m[n] for n in _SMALL}),
                               _pack_small({n: moments_v[n] for n in _SMALL}), "adamw_small")
    for out, packed in ((grads, g), (deltas, d), (new_m, nm), (new_v, nv)):
        out.update(_unpack_small(packed, small_like))

    return (loss, dx[None], *[grads[n] for n in _ORDER], *[deltas[n] for n in _ORDER],
            *[new_m[n] for n in _ORDER], *[new_v[n] for n in _ORDER])
```

```python
import functools

import jax
import jax.numpy as jnp
from jax import lax
from jax.experimental import pallas as pl
from jax.experimental.pallas import tpu as pltpu

F32 = jnp.float32
BF16 = jnp.bfloat16

D_MODEL = 1024
WIDTH_A = 256
WIDTH_B = 768
D_FF = 2816
IN_COLS = 2 * WIDTH_A + 3 * WIDTH_B
PLE_DIM = 256
HEAD_DIM = 64
N_PAIRS = WIDTH_B // 128
CHUNK = 128
N_BACK = 128
DILATIONS = (1, 4, 16)
ROPE_THETA = 10000.0
EPS = 1e-6
N_DEV = 8

ADAM_LR = 0.001
ADAM_B1 = 0.9
ADAM_B2 = 0.999
ADAM_EPS = 1e-08
ADAM_WD = 0.01
ADAM_STEP = 10

V7X_VMEM_LIMIT_BYTES = 56 * 1024 * 1024
ROW_TILE = 512
MESH = pl.DeviceIdType.MESH
NEG = -1e30

_NT = (((1,), (1,)), ((), ()))
_TN = (((0,), (0,)), ((), ()))


def _params(*semantics):
    return pltpu.CompilerParams(dimension_semantics=semantics, vmem_limit_bytes=V7X_VMEM_LIMIT_BYTES)


def _rows(tm, width):
    return pl.BlockSpec((tm, width), lambda i: (i, 0))


def _whole(shape):
    return pl.BlockSpec(shape, lambda *_: (0,) * len(shape))


def _resident(shape):
    return pl.BlockSpec(shape, lambda *_: (0,) * len(shape), pipeline_mode=pl.Buffered(1))


def _gelu(x):
    t = jnp.tanh(0.7978845608028654 * (x + 0.044715 * (x * x * x)))
    return 0.5 * x * (1.0 + t)


def _gelu_grad(x):
    t = jnp.tanh(0.7978845608028654 * (x + 0.044715 * (x * x * x)))
    return 0.5 * (1.0 + t) + 0.5 * x * (1.0 - t * t) * (0.7978845608028654 * (1.0 + 3.0 * 0.044715 * (x * x)))


def _rstd(x):
    return lax.rsqrt(jnp.mean(x * x, axis=-1, keepdims=True) + EPS)


def _norm_bwd(dn, h, g, r=None):
    r = _rstd(h) if r is None else r
    n = h * r
    t = dn * g
    return r * (t - n * jnp.mean(t * n, axis=-1, keepdims=True)), dn * n


def _swap_halves(x, first_half):
    return jnp.where(first_half, pltpu.roll(x, 96, 1), pltpu.roll(x, 32, 1))


def _sub_spec(d, n_cb, tm):
    return pl.BlockSpec((d, n_cb, tm // d, 128), lambda i: (0, 0, i, 0))


def _sub_shape(s_len, d, n_cb, dtype):
    return jax.ShapeDtypeStruct((d, n_cb, s_len // d, 128), dtype)


def _to_sub(stage_ref, cb_src, out_ref, cb_dst, d, tm):
    slab = stage_ref.at[cb_src]
    for r in range(d):
        out_ref[r, cb_dst] = slab[pl.ds(r, tm // d, stride=d), :].astype(out_ref.dtype)


def _from_sub(in_ref, cb_src, stage_ref, cb_dst, d, tm):
    slab = stage_ref.at[cb_dst]
    for r in range(d):
        slab[pl.ds(r, tm // d, stride=d), :] = in_ref[r, cb_src].astype(F32)


def _inproj(x, g, w, cos, sin, exchange=()):
    s_len = x.shape[0]
    tm = ROW_TILE
    n_cb = 3 * N_PAIRS

    def body(x_ref, g_ref, w_ref, cos_ref, sin_ref, ua_ref, hn_ref, *rest):
        sub_refs, stage = rest[:-1], rest[-1]
        xf = x_ref[...]
        hn = (xf * _rstd(xf) * g_ref[...]).astype(BF16)
        hn_ref[...] = hn
        c = cos_ref[...]
        s = sin_ref[...]
        first_half = (lax.broadcasted_iota(jnp.int32, (tm, 128), 1) % HEAD_DIM) < HEAD_DIM // 2
        for col in range(0, IN_COLS, 512):
            width = min(512, IN_COLS - col)
            acc = lax.dot_general(hn, w_ref[col:col + width, :], _NT, preferred_element_type=F32)
            if col < 2 * WIDTH_A:
                ua_ref[:, col:col + width] = acc
                continue
            for part in range(width // 128):
                cb = (col - 2 * WIDTH_A) // 128 + part
                t = acc[:, part * 128:(part + 1) * 128]
                if cb < 2 * N_PAIRS:
                    t = (t * c + _swap_halves(t, first_half) * s) * (0.125 if cb < N_PAIRS else 1.0)
                stage[cb] = t
                sub_refs[0][0, cb] = t.astype(BF16)
        for cb in range(n_cb):
            for d, out_ref in zip(DILATIONS[1:], sub_refs[1:]):
                _to_sub(stage, cb, out_ref, cb, d, tm)

    return _call(
        body, name="inproj", grid=(s_len // tm,),
        in_specs=[_rows(tm, D_MODEL), _whole((1, D_MODEL)), _resident((IN_COLS, D_MODEL)), _rows(tm, 128), _rows(tm, 128)],
        out_specs=[_rows(tm, 2 * WIDTH_A), _rows(tm, D_MODEL)] + [_sub_spec(d, n_cb, tm) for d in DILATIONS],
        out_shape=[jax.ShapeDtypeStruct((s_len, 2 * WIDTH_A), F32), jax.ShapeDtypeStruct((s_len, D_MODEL), BF16)]
        + [_sub_shape(s_len, d, n_cb, BF16) for d in DILATIONS],
        scratch_shapes=[pltpu.VMEM((n_cb, tm, 128), F32)],
        semantics=("parallel",), args=(x, g, w, cos, sin), exchange=exchange)


def _sgu_mix_weights(sw_ref):
    keep = lax.broadcasted_iota(jnp.int32, (CHUNK, CHUNK), 0) >= lax.broadcasted_iota(jnp.int32, (CHUNK, CHUNK), 1)
    return [jnp.where(keep, sw_ref[h], 0.0).astype(BF16) for h in range(4)], keep


def _sgu_core(ua_ref, gs_ref):
    u = ua_ref[:, :WIDTH_A]
    va = ua_ref[:, WIDTH_A:]
    vg = _gelu(va)
    xc = vg - jnp.mean(vg, axis=-1, keepdims=True)
    rstd = lax.rsqrt(jnp.mean(xc * xc, axis=-1, keepdims=True) + EPS)
    xhat = xc * rstd
    return u, va, _gelu(u), xhat, rstd, xhat * gs_ref[...]


def _sgu_mix(wm, vnb, half, head_a):
    va, vb = _both_heads(vnb[:, half * 128:(half + 1) * 128], head_a)
    return (jnp.dot(wm[2 * half], va, preferred_element_type=F32)
            + jnp.dot(wm[2 * half + 1], vb, preferred_element_type=F32))


def _sgu_fwd(ua, sw, b2, gs, ga):
    s_len = ua.shape[0]
    tm = ROW_TILE

    def body(ua_ref, sw_ref, b2_ref, gs_ref, ga_ref, out_ref):
        _, _, ug, _, _, vn = _sgu_core(ua_ref, gs_ref)
        wm, _ = _sgu_mix_weights(sw_ref)
        head_a = lax.broadcasted_iota(jnp.int32, (CHUNK, 128), 1) < HEAD_DIM
        for c in range(tm // CHUNK):
            rows = slice(c * CHUNK, (c + 1) * CHUNK)
            vnb = vn[rows].astype(BF16)
            mixed = b2_ref[...] + jnp.concatenate([_sgu_mix(wm, vnb, half, head_a) for half in range(2)], axis=1)
            ya = ug[rows] * mixed
            out_ref[rows, :] = (ya * _rstd(ya) * ga_ref[...]).astype(BF16)

    return pl.pallas_call(
        body, name="sgu_fwd", grid=(s_len // tm,),
        in_specs=[_rows(tm, 2 * WIDTH_A), _whole((4, CHUNK, CHUNK)), _whole((CHUNK, WIDTH_A)), _whole((1, WIDTH_A)), _whole((1, WIDTH_A))],
        out_specs=_rows(tm, WIDTH_A),
        out_shape=jax.ShapeDtypeStruct((s_len, WIDTH_A), BF16),
        compiler_params=_params("parallel"),
    )(ua, sw, b2, gs, ga)


STEP_POSITIONS = 2 * ROW_TILE


def _attn_geometry(sd):
    tile = min(STEP_POSITIONS, sd)
    return tile, tile // CHUNK, sd // tile, STEP_POSITIONS // tile


PAIRS_PER_STEP = 6


def _attn_spec(cb0, rows, row_index, res):
    return pl.BlockSpec((res, PAIRS_PER_STEP, rows, 128), lambda r, g, n: (r, cb0 // PAIRS_PER_STEP + g, row_index(n), 0))


assert PAIRS_PER_STEP == N_PAIRS
assert DILATIONS[0] == 1


def _stats_spec(rows, row_index, res):
    return pl.BlockSpec((res, None, rows, 128), lambda r, g, n: (r, 0, row_index(n), 0))


def _stats_shape(sd, d):
    return jax.ShapeDtypeStruct((d, 1, sd, 128), F32)


def _both_heads(x, head_a):
    zero = jnp.zeros_like(x)
    return [jnp.where(head_a, x, zero), jnp.where(head_a, zero, x)]


def _attn_fwd(qkv, d, exchange=()):
    sd = qkv.shape[2]
    tile, nb, n_tiles, res = _attn_geometry(sd)

    def prev(n):
        return jnp.maximum(n * nb - 1, 0)

    def body(q_ref, k_ref, kp_ref, v_ref, vp_ref, o_ref, l_ref):
        for rr in range(res):
            for hp in range(PAIRS_PER_STEP):
                one_pair(hp, q_ref.at[rr, hp], k_ref.at[rr, hp], kp_ref.at[rr, hp], v_ref.at[rr, hp], vp_ref.at[rr, hp],
                         o_ref.at[rr, hp], l_ref.at[rr])

    def one_pair(hp, q_ref, k_ref, kp_ref, v_ref, vp_ref, o_ref, l_ref):
        n = pl.program_id(2)
        lane = lax.broadcasted_iota(jnp.int32, (CHUNK, 128), 1)
        head_a = lane < HEAD_DIM
        qi = lax.broadcasted_iota(jnp.int32, (2 * CHUNK, 2 * CHUNK), 0) % CHUNK
        kc = lax.broadcasted_iota(jnp.int32, (2 * CHUNK, 2 * CHUNK), 1)
        band = (kc >= qi) & (kc <= qi + N_BACK)
        for j in range(nb):
            rows = slice(j * CHUNK, (j + 1) * CHUNK)
            if j == 0:
                kcat = jnp.concatenate([kp_ref[...], k_ref[rows, :]], axis=0)
                vcat = jnp.concatenate([vp_ref[...], v_ref[rows, :]], axis=0)
                valid = band & jnp.logical_or(n > 0, kc >= CHUNK)
            else:
                kcat = k_ref[(j - 1) * CHUNK:(j + 1) * CHUNK, :]
                vcat = v_ref[(j - 1) * CHUNK:(j + 1) * CHUNK, :]
                valid = band
            q2 = jnp.concatenate(_both_heads(q_ref[rows, :], head_a), axis=0)
            s = lax.dot_general(q2, kcat, _NT, preferred_element_type=F32)
            s = jnp.where(valid, s, NEG)
            m = jnp.max(s, axis=-1, keepdims=True)
            p = jnp.exp(s - m)
            l = jnp.sum(p, axis=-1, keepdims=True)
            o2 = jnp.dot(p.astype(BF16), vcat, preferred_element_type=F32) / l
            lse2 = m + jnp.log(l)
            o_ref[rows, :] = jnp.where(head_a, o2[:CHUNK], o2[CHUNK:]).astype(BF16)
            others = l_ref[rows, :] if hp > 0 else jnp.zeros((CHUNK, 128), F32)
            l_ref[rows, :] = jnp.where(lane == 2 * hp, lse2[:CHUNK], jnp.where(lane == 2 * hp + 1, lse2[CHUNK:], others))

    same = lambda n: n
    return _call(
        body, name=f"attn_fwd_d{d}", grid=(d // res, N_PAIRS // PAIRS_PER_STEP, n_tiles),
        in_specs=[_attn_spec(0, tile, same, res), _attn_spec(N_PAIRS, tile, same, res), _attn_spec(N_PAIRS, CHUNK, prev, res),
                  _attn_spec(2 * N_PAIRS, tile, same, res), _attn_spec(2 * N_PAIRS, CHUNK, prev, res)],
        out_specs=[_attn_spec(0, tile, same, res), _stats_spec(tile, same, res)],
        out_shape=[jax.ShapeDtypeStruct((d, N_PAIRS, sd, 128), BF16), _stats_shape(sd, d)],
        semantics=("parallel", "parallel", "parallel"), args=(qkv, qkv, qkv, qkv, qkv), exchange=exchange)


def _combine(outs, lses, ya_n, gb, head_spread):
    s_len = ya_n.shape[0]
    tm = ROW_TILE
    n_br = len(DILATIONS)

    def body(*refs):
        o_refs, l_refs = refs[:n_br], refs[n_br:2 * n_br]
        ya_ref, gb_ref, spread_ref, y_ref, yb_ref = refs[2 * n_br:2 * n_br + 5]
        lse_refs = refs[2 * n_br + 5:3 * n_br + 5]
        o_nat, l_nat, lse_nat, w_wide = refs[3 * n_br + 5:]
        for i, d in enumerate(DILATIONS):
            _from_sub(l_refs[i], 0, l_nat, i, d, tm)
        ls = [l_nat[i] for i in range(n_br)]
        top = jnp.maximum(jnp.maximum(ls[0], ls[1]), ls[2])
        ws = [jnp.exp(l - top) for l in ls]
        den = ws[0] + ws[1] + ws[2]
        inv = 1.0 / den
        for i in range(n_br):
            w = ws[i] * inv
            hi = w.astype(BF16)
            lo = (w - hi.astype(F32)).astype(BF16)
            w_wide[i] = (jnp.dot(hi, spread_ref[...], preferred_element_type=F32)
                         + jnp.dot(lo, spread_ref[...], preferred_element_type=F32))
        lse_nat[0] = top + jnp.log(den)
        for d, lse_ref in zip(DILATIONS, lse_refs):
            _to_sub(lse_nat, 0, lse_ref, 0, d, tm)
        sumsq = jnp.zeros((tm, 1), F32)
        for cb in range(N_PAIRS):
            cols = slice(cb * 128, (cb + 1) * 128)
            yb = w_wide[0, :, cols] * o_refs[0][0, cb].astype(F32)
            for i, d in enumerate(DILATIONS[1:], start=1):
                _from_sub(o_refs[i], cb, o_nat, i, d, tm)
                yb = yb + w_wide[i, :, cols] * o_nat[i]
            yb_ref[:, cb * 128:(cb + 1) * 128] = yb
            sumsq = sumsq + jnp.sum(yb * yb, axis=-1, keepdims=True)
        r = lax.rsqrt(sumsq / WIDTH_B + EPS)
        y_ref[:, :WIDTH_A] = ya_ref[...]
        y_ref[:, WIDTH_A:] = (yb_ref[...] * r * gb_ref[...]).astype(BF16)

    stats = [_sub_spec(d, 1, tm) for d in DILATIONS]
    return _call(
        body, name="attn_combine", grid=(s_len // tm,),
        in_specs=[_sub_spec(d, N_PAIRS, tm) for d in DILATIONS] + stats
        + [_rows(tm, WIDTH_A), _whole((1, WIDTH_B)), _whole((128, WIDTH_B))],
        out_specs=[_rows(tm, D_MODEL), _rows(tm, WIDTH_B)] + stats,
        out_shape=[jax.ShapeDtypeStruct((s_len, D_MODEL), BF16), jax.ShapeDtypeStruct((s_len, WIDTH_B), F32)]
        + [_sub_shape(s_len, d, 1, F32) for d in DILATIONS],
        scratch_shapes=[pltpu.VMEM((n_br, tm, 128), F32), pltpu.VMEM((n_br, tm, 128), F32), pltpu.VMEM((1, tm, 128), F32),
                        pltpu.VMEM((n_br, tm, WIDTH_B), F32)],
        semantics=("parallel",), args=(*outs, *lses, ya_n, gb, head_spread))[0]


def _ffn_up(y, wout, x, g, wg, wu, exchange=()):
    s_len = x.shape[0]
    tm = ROW_TILE

    def body(y_ref, wo_ref, x_ref, g_ref, wg_ref, wu_ref, h_ref, a_ref, b_ref, act_ref, hn_ref):
        hf = x_ref[...] + jnp.dot(y_ref[...], wo_ref[...], preferred_element_type=F32)
        h_ref[...] = hf
        hn = (hf * _rstd(hf) * g_ref[...]).astype(BF16)
        hn_ref[...] = hn
        for col in range(0, D_FF, 256):
            cols = slice(col, col + 256)
            a = lax.dot_general(hn, wg_ref[cols, :], _NT, preferred_element_type=F32)
            b = lax.dot_general(hn, wu_ref[cols, :], _NT, preferred_element_type=F32)
            a_ref[:, cols] = a.astype(BF16)
            b_ref[:, cols] = b.astype(BF16)
            act_ref[:, cols] = (a * jax.nn.sigmoid(a) * b).astype(BF16)

    wide = jax.ShapeDtypeStruct((s_len, D_FF), BF16)
    return _call(
        body, name="ffn_up", grid=(s_len // tm,),
        in_specs=[_rows(tm, D_MODEL), _resident((D_MODEL, D_MODEL)), _rows(tm, D_MODEL), _whole((1, D_MODEL)),
                  _resident((D_FF, D_MODEL)), _resident((D_FF, D_MODEL))],
        out_specs=[_rows(tm, D_MODEL), _rows(tm, D_FF), _rows(tm, D_FF), _rows(tm, D_FF), _rows(tm, D_MODEL)],
        out_shape=[jax.ShapeDtypeStruct((s_len, D_MODEL), F32), wide, wide, wide, jax.ShapeDtypeStruct((s_len, D_MODEL), BF16)],
        semantics=("parallel",), args=(y, wout, x, g, wg, wu), exchange=exchange)


def _ffn_down_ple(act, wd, h1, g, wpg, p, wpp):
    s_len = h1.shape[0]
    tm = ROW_TILE

    def body(act_ref, wd_ref, h1_ref, g_ref, wpg_ref, p_ref, wpp_ref, h2_ref, h3_ref, gate_ref, pp_ref, hn_ref):
        hf = h1_ref[...] + jnp.dot(act_ref[...], wd_ref[...], preferred_element_type=F32)
        h2_ref[...] = hf
        hn = (hf * _rstd(hf) * g_ref[...]).astype(BF16)
        hn_ref[...] = hn
        gate = jax.nn.sigmoid(jnp.dot(hn, wpg_ref[...], preferred_element_type=F32))
        pp = lax.dot_general(p_ref[...].astype(BF16), wpp_ref[...], _NT, preferred_element_type=F32)
        h3_ref[...] = hf + gate * pp
        gate_ref[...] = gate.astype(BF16)
        pp_ref[...] = pp.astype(BF16)

    full = jax.ShapeDtypeStruct((s_len, D_MODEL), F32)
    half = jax.ShapeDtypeStruct((s_len, D_MODEL), BF16)
    return pl.pallas_call(
        body, name="ffn_down_ple", grid=(s_len // tm,),
        in_specs=[_rows(tm, D_FF), _resident((D_FF, D_MODEL)), _rows(tm, D_MODEL), _whole((1, D_MODEL)),
                  _resident((D_MODEL, D_MODEL)), _rows(tm, PLE_DIM), _resident((D_MODEL, PLE_DIM))],
        out_specs=[_rows(tm, D_MODEL)] * 5,
        out_shape=[full, full, half, half, half],
        compiler_params=_params("parallel"),
    )(act, wd, h1, g, wpg, p, wpp)


def _loss_ple_bwd(h3, target, gf, gate, pp, h2, g_ple, wpg, hn3, p):
    s_len = h3.shape[0]
    tm = ROW_TILE
    n_steps = s_len // tm

    def body(h_ref, t_ref, g_ref, gate_ref, pp_ref, h2_ref, gp_ref, w_ref, hn_ref, p_ref,
             dh2_ref, loss_ref, dg_ref, dgp_ref, dwg_ref, dwp_ref, acc_g, acc_p):
        step = pl.program_id(0)

        @pl.when(step == 0)
        def _():
            loss_ref[...] = jnp.zeros_like(loss_ref)
            dg_ref[...] = jnp.zeros_like(dg_ref)
            dgp_ref[...] = jnp.zeros_like(dgp_ref)
            acc_g[...] = jnp.zeros_like(acc_g)
            acc_p[...] = jnp.zeros_like(acc_p)

        hf = h_ref[...]
        gfv = g_ref[...]
        r = _rstd(hf)
        err = hf * r * gfv - t_ref[...]
        loss_ref[...] += 0.5 * jnp.sum(jnp.sum(err * err, axis=-1, keepdims=True), axis=0, keepdims=True) / D_MODEL
        dh, dg_rows = _norm_bwd(err / D_MODEL, hf, gfv, r)
        dg_ref[...] += jnp.sum(dg_rows, axis=0, keepdims=True)
        gate = gate_ref[...].astype(F32)
        dz = (dh * pp_ref[...].astype(F32) * gate * (1.0 - gate)).astype(BF16)
        dpp = (dh * gate).astype(BF16)
        dn = lax.dot_general(dz, w_ref[...], _NT, preferred_element_type=F32)
        dh2, dgp_rows = _norm_bwd(dn, h2_ref[...], gp_ref[...])
        dh2 = dh + dh2
        dh2_ref[...] = dh2
        dgp_ref[...] += jnp.sum(dgp_rows, axis=0, keepdims=True)
        pb = p_ref[...].astype(BF16)
        for j in range(D_MODEL // 256):
            rows = slice(j * 256, (j + 1) * 256)
            acc_g[rows, :] += lax.dot_general(hn_ref[:, rows], dz, _TN, preferred_element_type=F32)
            acc_p[rows, :] += lax.dot_general(dpp[:, rows], pb, _TN, preferred_element_type=F32)

        @pl.when(step == n_steps - 1)
        def _():
            dwg_ref[...] = acc_g[...].astype(BF16)
            dwp_ref[...] = acc_p[...].astype(BF16)

    gain = jax.ShapeDtypeStruct((1, D_MODEL), F32)
    return pl.pallas_call(
        body, name="loss_ple_bwd", grid=(n_steps,),
        in_specs=[_rows(tm, D_MODEL), _rows(tm, D_MODEL), _whole((1, D_MODEL)), _rows(tm, D_MODEL), _rows(tm, D_MODEL),
                  _rows(tm, D_MODEL), _whole((1, D_MODEL)), _resident((D_MODEL, D_MODEL)), _rows(tm, D_MODEL),
                  _rows(tm, PLE_DIM)],
        out_specs=[_rows(tm, D_MODEL), _whole((1, 128)), _whole((1, D_MODEL)), _whole((1, D_MODEL)),
                   _whole((D_MODEL, D_MODEL)), _whole((D_MODEL, PLE_DIM))],
        out_shape=[jax.ShapeDtypeStruct((s_len, D_MODEL), F32), jax.ShapeDtypeStruct((1, 128), F32), gain, gain,
                   jax.ShapeDtypeStruct((D_MODEL, D_MODEL), BF16), jax.ShapeDtypeStruct((D_MODEL, PLE_DIM), BF16)],
        scratch_shapes=[pltpu.VMEM((D_MODEL, D_MODEL), F32), pltpu.VMEM((D_MODEL, PLE_DIM), F32)],
        compiler_params=_params("arbitrary"),
    )(h3, target, gf, gate, pp, h2, g_ple, wpg, hn3, p)


def _mm_norm_bwd(parts, h, g, dres, name, exchange=(), dw_lhs=None):
    s_len = h.shape[0]
    tm = ROW_TILE
    n_parts = len(parts)
    n_steps = s_len // tm
    has_dw = dw_lhs is not None

    def body(*refs):
        a_refs = refs[0:2 * n_parts:2]
        w_refs = refs[1:2 * n_parts:2]
        h_ref, g_ref, r_ref = refs[2 * n_parts:2 * n_parts + 3]
        rest = refs[2 * n_parts + 3:]
        step = pl.program_id(0)
        if has_dw:
            lhs_ref, o_ref, dg_ref, dw_ref, acc_ref = rest
        else:
            o_ref, dg_ref = rest

        @pl.when(step == 0)
        def _():
            dg_ref[...] = jnp.zeros_like(dg_ref)
            if has_dw:
                acc_ref[...] = jnp.zeros_like(acc_ref)

        dn = jnp.dot(a_refs[0][...], w_refs[0][...], preferred_element_type=F32)
        for a_ref, w_ref in zip(a_refs[1:], w_refs[1:]):
            dn = dn + jnp.dot(a_ref[...], w_ref[...], preferred_element_type=F32)
        dh, dg_rows = _norm_bwd(dn, h_ref[...], g_ref[...])
        out = r_ref[...] + dh
        o_ref[...] = out
        dg_ref[...] += jnp.sum(dg_rows, axis=0, keepdims=True)
        if has_dw:
            out_b = out.astype(BF16)
            for j in range(lhs_ref.shape[1] // 256):
                rows = slice(j * 256, (j + 1) * 256)
                acc_ref[rows, :] += lax.dot_general(lhs_ref[:, rows], out_b, _TN, preferred_element_type=F32)

            @pl.when(step == n_steps - 1)
            def _():
                dw_ref[...] = acc_ref[...].astype(BF16)

    in_specs, args = [], []
    for a, w in parts:
        in_specs += [_rows(tm, a.shape[1]), _resident(w.shape)]
        args += [a, w]
    in_specs += [_rows(tm, D_MODEL), _whole((1, D_MODEL)), _rows(tm, D_MODEL)]
    args += [h, g, dres]
    out_specs = [_rows(tm, D_MODEL), _whole((1, D_MODEL))]
    out_shape = [jax.ShapeDtypeStruct((s_len, D_MODEL), F32), jax.ShapeDtypeStruct((1, D_MODEL), F32)]
    scratch = []
    if has_dw:
        m = dw_lhs.shape[1]
        in_specs.append(_rows(tm, m))
        args.append(dw_lhs)
        out_specs.append(_whole((m, D_MODEL)))
        out_shape.append(jax.ShapeDtypeStruct((m, D_MODEL), BF16))
        scratch.append(pltpu.VMEM((m, D_MODEL), F32))
    return _call(
        body, name=name, grid=(n_steps,), in_specs=in_specs, out_specs=out_specs, out_shape=out_shape,
        scratch_shapes=scratch, semantics=("arbitrary",), args=tuple(args), exchange=exchange)


def _ffn_down_bwd(dh, wdt, a, b, exchange=()):
    s_len = dh.shape[0]
    tm = ROW_TILE

    def body(dh_ref, w_ref, a_ref, b_ref, da_ref, db_ref):
        dhb = dh_ref[...].astype(BF16)
        for col in range(0, D_FF, 512):
            cols = slice(col, min(col + 512, D_FF))
            dact = lax.dot_general(dhb, w_ref[cols, :], _NT, preferred_element_type=F32)
            av = a_ref[:, cols].astype(F32)
            bv = b_ref[:, cols].astype(F32)
            sig = jax.nn.sigmoid(av)
            t = dact * sig
            silu = av * sig
            da_ref[:, cols] = (t * bv * (1.0 + av - silu)).astype(BF16)
            db_ref[:, cols] = (dact * silu).astype(BF16)

    wide = jax.ShapeDtypeStruct((s_len, D_FF), BF16)
    return _call(
        body, name="ffn_down_bwd", grid=(s_len // tm,),
        in_specs=[_rows(tm, D_MODEL), _resident((D_FF, D_MODEL)), _rows(tm, D_FF), _rows(tm, D_FF)],
        out_specs=[_rows(tm, D_FF), _rows(tm, D_FF)],
        out_shape=[wide, wide],
        semantics=("parallel",), args=(dh, wdt, a, b), exchange=exchange)


def _outproj_bwd(dh1, woutt, yb, gb, head_sum):
    s_len = dh1.shape[0]
    tm = ROW_TILE
    n_br = len(DILATIONS)

    def body(dh_ref, w_ref, yb_ref, gb_ref, e_ref, dya_ref, dgb_ref, *rest):
        do_refs, dd_refs = rest[:n_br], rest[n_br:2 * n_br]
        do_nat, dd_nat = rest[2 * n_br:]

        @pl.when(pl.program_id(0) == 0)
        def _():
            dgb_ref[...] = jnp.zeros_like(dgb_ref)

        dhb = dh_ref[...].astype(BF16)
        dya_ref[...] = lax.dot_general(dhb, w_ref[:WIDTH_A, :], _NT, preferred_element_type=F32)
        dyn = lax.dot_general(dhb, w_ref[WIDTH_A:, :], _NT, preferred_element_type=F32)
        ybv = yb_ref[...]
        dyb, dg_rows = _norm_bwd(dyn, ybv, gb_ref[...])
        dgb_ref[...] += jnp.sum(dg_rows, axis=0, keepdims=True)
        prod = dyb * ybv
        hi = prod.astype(BF16)
        lo = (prod - hi.astype(F32)).astype(BF16)
        dd_nat[0] = (jnp.dot(hi, e_ref[...], preferred_element_type=F32)
                     + jnp.dot(lo, e_ref[...], preferred_element_type=F32))
        for i, d in enumerate(DILATIONS):
            _to_sub(dd_nat, 0, dd_refs[i], 0, d, tm)
        for cb in range(N_PAIRS):
            piece = dyb[:, cb * 128:(cb + 1) * 128]
            do_nat[cb] = piece
            do_refs[0][0, cb] = piece.astype(BF16)
            for i, d in enumerate(DILATIONS[1:], start=1):
                _to_sub(do_nat, cb, do_refs[i], cb, d, tm)

    return _call(
        body, name="outproj_bwd", grid=(s_len // tm,),
        in_specs=[_rows(tm, D_MODEL), _resident((D_MODEL, D_MODEL)), _rows(tm, WIDTH_B), _whole((1, WIDTH_B)), _whole((WIDTH_B, 128))],
        out_specs=[_rows(tm, WIDTH_A), _whole((1, WIDTH_B))] + [_sub_spec(d, N_PAIRS, tm) for d in DILATIONS]
        + [_sub_spec(d, 1, tm) for d in DILATIONS],
        out_shape=[jax.ShapeDtypeStruct((s_len, WIDTH_A), F32), jax.ShapeDtypeStruct((1, WIDTH_B), F32)]
        + [_sub_shape(s_len, d, N_PAIRS, BF16) for d in DILATIONS] + [_sub_shape(s_len, d, 1, F32) for d in DILATIONS],
        scratch_shapes=[pltpu.VMEM((N_PAIRS, tm, 128), F32), pltpu.VMEM((1, tm, 128), F32)],
        semantics=("arbitrary",), args=(dh1, woutt, yb, gb, head_sum))[0]


def _attn_bwd(qkv, do, lse, dd, d, exchange=()):
    sd = qkv.shape[2]
    tile, nb, n_tiles, res = _attn_geometry(sd)
    last_block = sd // CHUNK - 1

    def nxt(n):
        return jnp.minimum((n + 1) * nb, last_block)

    def block(ref, next_ref, j):
        return ref[j * CHUNK:(j + 1) * CHUNK, :] if j < nb else next_ref[...]

    def body(q_ref, qn_ref, k_ref, v_ref, do_ref, don_ref, l_ref, ln_ref, dd_ref, ddn_ref,
             dq_ref, dk_ref, dv_ref, carry_ref):
        for rr in range(res):
            l_t = [block(l_ref.at[rr], ln_ref.at[rr], j).T for j in range(nb + 1)]
            dd_t = [block(dd_ref.at[rr], ddn_ref.at[rr], j).T for j in range(nb + 1)]
            for hp in range(PAIRS_PER_STEP):
                l_rows = [jnp.concatenate([t[2 * hp:2 * hp + 1, :], t[2 * hp + 1:2 * hp + 2, :]], axis=1) for t in l_t]
                dd_rows = [jnp.concatenate([t[2 * hp:2 * hp + 1, :], t[2 * hp + 1:2 * hp + 2, :]], axis=1) for t in dd_t]
                one_pair(q_ref.at[rr, hp], qn_ref.at[rr, hp], k_ref.at[rr, hp], v_ref.at[rr, hp], do_ref.at[rr, hp],
                         don_ref.at[rr, hp], l_rows, dd_rows, dq_ref.at[rr, hp], dk_ref.at[rr, hp], dv_ref.at[rr, hp],
                         carry_ref.at[rr, hp])

    def one_pair(q_ref, qn_ref, k_ref, v_ref, do_ref, don_ref, l_rows, dd_rows, dq_ref, dk_ref, dv_ref, carry_ref):
        n = pl.program_id(2)

        @pl.when(n == 0)
        def _():
            carry_ref[...] = jnp.zeros_like(carry_ref)

        head_a = lax.broadcasted_iota(jnp.int32, (CHUNK, 128), 1) < HEAD_DIM
        col = lax.broadcasted_iota(jnp.int32, (CHUNK, 4 * CHUNK), 1)
        qi = col % CHUNK
        ki = lax.broadcasted_iota(jnp.int32, (CHUNK, 4 * CHUNK), 0)
        is_after = col >= 2 * CHUNK
        mask = (is_after & (ki >= qi)) | (jnp.logical_not(is_after) & (qi >= ki))
        mask_last = mask & jnp.logical_or(jnp.logical_not(is_after), n < n_tiles - 1)
        dq_acc = [carry_ref[...]] + [jnp.zeros((CHUNK, 128), F32) for _ in range(nb)]

        q_st = [jnp.concatenate(_both_heads(block(q_ref, qn_ref, j), head_a), axis=0) for j in range(nb + 1)]
        do_st = [jnp.concatenate(_both_heads(block(do_ref, don_ref, j), head_a), axis=0) for j in range(nb + 1)]

        for j in range(nb):
            rows = slice(j * CHUNK, (j + 1) * CHUNK)
            kj = k_ref[rows, :]
            vj = v_ref[rows, :]
            msk = mask if j + 1 < nb else mask_last
            qs = jnp.concatenate([q_st[j], q_st[j + 1]], axis=0)
            dos = jnp.concatenate([do_st[j], do_st[j + 1]], axis=0)
            ls = jnp.concatenate([l_rows[j], l_rows[j + 1]], axis=1)
            dds = jnp.concatenate([dd_rows[j], dd_rows[j + 1]], axis=1)
            st = lax.dot_general(kj, qs, _NT, preferred_element_type=F32)
            pt = jnp.exp(jnp.where(msk, st - ls, NEG))
            dpt = lax.dot_general(vj, dos, _NT, preferred_element_type=F32)
            dst = (pt * (dpt - dds)).astype(BF16)
            dv_ref[rows, :] = jnp.dot(pt.astype(BF16), dos, preferred_element_type=F32).astype(BF16)
            dk_ref[rows, :] = jnp.dot(dst, qs, preferred_element_type=F32).astype(BF16)
            dqs = lax.dot_general(dst, kj, _TN, preferred_element_type=F32)
            dq_acc[j] = dq_acc[j] + jnp.where(head_a, dqs[:CHUNK], dqs[CHUNK:2 * CHUNK])
            dq_acc[j + 1] = dq_acc[j + 1] + jnp.where(head_a, dqs[2 * CHUNK:3 * CHUNK], dqs[3 * CHUNK:])
        for j in range(nb):
            dq_ref[j * CHUNK:(j + 1) * CHUNK, :] = dq_acc[j].astype(BF16)
        carry_ref[...] = dq_acc[nb]

    same = lambda n: n
    grad = jax.ShapeDtypeStruct((d, N_PAIRS, sd, 128), BF16)
    return _call(
        body, name=f"attn_bwd_d{d}", grid=(d // res, N_PAIRS // PAIRS_PER_STEP, n_tiles),
        in_specs=[_attn_spec(0, tile, same, res), _attn_spec(0, CHUNK, nxt, res), _attn_spec(N_PAIRS, tile, same, res),
                  _attn_spec(2 * N_PAIRS, tile, same, res), _attn_spec(0, tile, same, res), _attn_spec(0, CHUNK, nxt, res),
                  _stats_spec(tile, same, res), _stats_spec(CHUNK, nxt, res), _stats_spec(tile, same, res),
                  _stats_spec(CHUNK, nxt, res)],
        out_specs=[_attn_spec(0, tile, same, res)] * 3,
        out_shape=[grad, grad, grad],
        scratch_shapes=[pltpu.VMEM((res, PAIRS_PER_STEP, CHUNK, 128), F32)],
        semantics=("parallel", "parallel", "arbitrary"), args=(qkv, qkv, qkv, qkv, do, do, lse, lse, dd, dd), exchange=exchange)


def _sgu_bwd(ua, sw, b2, gs, ga, dya_n):
    s_len = ua.shape[0]
    tm = ROW_TILE

    def body(ua_ref, sw_ref, b2_ref, gs_ref, ga_ref, dy_ref, dua_ref, dsw_ref, db2_ref, dgs_ref, dga_ref):
        @pl.when(pl.program_id(0) == 0)
        def _():
            dsw_ref[...] = jnp.zeros_like(dsw_ref)
            db2_ref[...] = jnp.zeros_like(db2_ref)
            dgs_ref[...] = jnp.zeros_like(dgs_ref)
            dga_ref[...] = jnp.zeros_like(dga_ref)

        u, va, ug, xhat, rstd, vn = _sgu_core(ua_ref, gs_ref)
        wm, keep = _sgu_mix_weights(sw_ref)
        head_a = lax.broadcasted_iota(jnp.int32, (CHUNK, 128), 1) < HEAD_DIM
        gav = ga_ref[...]
        gsv = gs_ref[...]
        dga = jnp.zeros((1, WIDTH_A), F32)
        dgs = jnp.zeros((1, WIDTH_A), F32)
        db2 = jnp.zeros((CHUNK, WIDTH_A), F32)
        dsw = [jnp.zeros((CHUNK, CHUNK), F32) for _ in range(4)]
        for c in range(tm // CHUNK):
            rows = slice(c * CHUNK, (c + 1) * CHUNK)
            vnb = vn[rows].astype(BF16)
            mixed = b2_ref[...] + jnp.concatenate([_sgu_mix(wm, vnb, half, head_a) for half in range(2)], axis=1)
            ugc = ug[rows]
            dya, dga_rows = _norm_bwd(dy_ref[rows, :], ugc * mixed, gav)
            dga = dga + jnp.sum(dga_rows, axis=0, keepdims=True)
            dmixed = dya * ugc
            db2 = db2 + dmixed
            dmb = dmixed.astype(BF16)
            dvn_halves = []
            for half in range(2):
                lanes = slice(half * 128, (half + 1) * 128)
                dm_heads = _both_heads(dmb[:, lanes], head_a)
                dvn_half = jnp.zeros((CHUNK, 128), F32)
                for k, dmh in enumerate(dm_heads):
                    h = 2 * half + k
                    dsw[h] = dsw[h] + lax.dot_general(dmh, vnb[:, lanes], _NT, preferred_element_type=F32)
                    dvn_half = dvn_half + lax.dot_general(wm[h], dmh, _TN, preferred_element_type=F32)
                dvn_halves.append(dvn_half)
            dvn = jnp.concatenate(dvn_halves, axis=1)
            xh = xhat[rows]
            dgs = dgs + jnp.sum(dvn * xh, axis=0, keepdims=True)
            dxh = dvn * gsv
            dvg = rstd[rows] * (dxh - jnp.mean(dxh, axis=-1, keepdims=True) - xh * jnp.mean(dxh * xh, axis=-1, keepdims=True))
            dua_ref[rows, :WIDTH_A] = (dya * mixed * _gelu_grad(u[rows])).astype(BF16)
            dua_ref[rows, WIDTH_A:] = (dvg * _gelu_grad(va[rows])).astype(BF16)
        for h in range(4):
            dsw_ref[h] += jnp.where(keep, dsw[h], 0.0)
        db2_ref[...] += db2
        dgs_ref[...] += dgs
        dga_ref[...] += dga

    return pl.pallas_call(
        body, name="sgu_bwd", grid=(s_len // tm,),
        in_specs=[_rows(tm, 2 * WIDTH_A), _whole((4, CHUNK, CHUNK)), _whole((CHUNK, WIDTH_A)), _whole((1, WIDTH_A)),
                  _whole((1, WIDTH_A)), _rows(tm, WIDTH_A)],
        out_specs=[_rows(tm, 2 * WIDTH_A), _whole((4, CHUNK, CHUNK)), _whole((CHUNK, WIDTH_A)), _whole((1, WIDTH_A)), _whole((1, WIDTH_A))],
        out_shape=[jax.ShapeDtypeStruct((s_len, 2 * WIDTH_A), BF16), jax.ShapeDtypeStruct((4, CHUNK, CHUNK), F32),
                   jax.ShapeDtypeStruct((CHUNK, WIDTH_A), F32), jax.ShapeDtypeStruct((1, WIDTH_A), F32),
                   jax.ShapeDtypeStruct((1, WIDTH_A), F32)],
        compiler_params=_params("arbitrary"),
    )(ua, sw, b2, gs, ga, dya_n)


def _dproj(dua, dqs, dks, dvs, cos, sin, hn1, exchange=()):
    s_len = dua.shape[0]
    tm = ROW_TILE
    n_br = len(DILATIONS)
    n_steps = s_len // tm

    def body(dua_ref, *rest):
        groups = [rest[g * n_br:(g + 1) * n_br] for g in range(3)]
        cos_ref, sin_ref, hn_ref, out_ref, dw_ref, acc, dw_acc = rest[3 * n_br:]
        step = pl.program_id(0)

        @pl.when(step == 0)
        def _():
            dw_acc[...] = jnp.zeros_like(dw_acc)

        out_ref[:, :2 * WIDTH_A] = dua_ref[...]
        c = cos_ref[...]
        s = sin_ref[...]
        first_half = (lax.broadcasted_iota(jnp.int32, (tm, 128), 1) % HEAD_DIM) < HEAD_DIM // 2
        for g, refs in enumerate(groups):
            for cb in range(N_PAIRS):
                t = refs[0][0, cb].astype(F32)
                for i, d in enumerate(DILATIONS[1:]):
                    _from_sub(refs[i + 1], cb, acc, i, d, tm)
                    t = t + acc[i]
                if g < 2:
                    t = (t * c - _swap_halves(t, first_half) * s) * (0.125 if g == 0 else 1.0)
                col = 2 * WIDTH_A + g * WIDTH_B + cb * 128
                out_ref[:, col:col + 128] = t.astype(BF16)
        hn = hn_ref[...]
        for j in range(IN_COLS // 256):
            cols = slice(j * 256, (j + 1) * 256)
            dw_acc[cols, :] += lax.dot_general(out_ref[:, cols], hn, _TN, preferred_element_type=F32)

        @pl.when(step == n_steps - 1)
        def _():
            dw_ref[...] = dw_acc[...].astype(BF16)

    subs = [_sub_spec(d, N_PAIRS, tm) for d in DILATIONS]
    (dproj, dw), received = _call(
        body, name="dproj_dw_in", grid=(n_steps,),
        in_specs=[_rows(tm, 2 * WIDTH_A)] + subs * 3 + [_rows(tm, 128), _rows(tm, 128), _rows(tm, D_MODEL)],
        out_specs=[_rows(tm, IN_COLS), _whole((IN_COLS, D_MODEL))],
        out_shape=[jax.ShapeDtypeStruct((s_len, IN_COLS), BF16), jax.ShapeDtypeStruct((IN_COLS, D_MODEL), BF16)],
        scratch_shapes=[pltpu.VMEM((n_br - 1, tm, 128), F32), pltpu.VMEM((IN_COLS, D_MODEL), F32)],
        semantics=("arbitrary",), args=(dua, *dqs, *dks, *dvs, cos, sin, hn1), exchange=exchange)
    return dproj, dw, received


def _mm_tn(a, b, name):
    s_len, m = a.shape
    n = b.shape[1]
    tk = 2 * ROW_TILE
    tm = m
    n_k = s_len // tk

    def body(a_ref, b_ref, o_ref, acc_ref):
        k = pl.program_id(1)

        @pl.when(k == 0)
        def _():
            acc_ref[...] = jnp.zeros_like(acc_ref)

        bb = b_ref[...].astype(BF16)
        for j in range(tm // 256):
            rows = slice(j * 256, (j + 1) * 256)
            acc_ref[rows, :] += lax.dot_general(a_ref[:, rows].astype(BF16), bb, _TN, preferred_element_type=F32)

        @pl.when(k == n_k - 1)
        def _():
            o_ref[...] = acc_ref[...].astype(BF16)

    return pl.pallas_call(
        body, name=name, grid=(m // tm, n_k),
        in_specs=[pl.BlockSpec((tk, tm), lambda i, k: (k, i)), pl.BlockSpec((tk, n), lambda i, k: (k, 0))],
        out_specs=pl.BlockSpec((tm, n), lambda i, k: (i, 0)),
        out_shape=jax.ShapeDtypeStruct((m, n), BF16),
        scratch_shapes=[pltpu.VMEM((tm, n), F32)],
        compiler_params=_params("parallel", "arbitrary"),
    )(a, b)


def _position():
    x, y, c = lax.axis_index("x"), lax.axis_index("y"), lax.axis_index("c")
    return x, y, c, 4 * x + 2 * y + c


def _peer(x, y, c, rel):
    return (x ^ ((rel >> 2) & 1), y ^ ((rel >> 1) & 1), c ^ (rel & 1))


def _exchange_out_shape(kind, arr):
    return jax.ShapeDtypeStruct(((N_DEV,) + arr.shape) if kind == "gather" else arr.shape, arr.dtype)


def _exchange_sems(n_items):
    return [pltpu.SemaphoreType.DMA((n_items, N_DEV)), pltpu.SemaphoreType.DMA((n_items, N_DEV)), pltpu.SemaphoreType.DMA((n_items,))]


def _exchange_copies(kinds, srcs, dsts, sems, arrivals):
    send_sems, recv_sems, local_sems = sems
    x, y, c, me = _position()
    local, sends, recvs = [], [], []
    for k, (kind, src, dst) in enumerate(zip(kinds, srcs, dsts)):
        own = src if kind == "gather" else src.at[me]
        local.append(pltpu.make_async_copy(own, dst.at[me], local_sems.at[k]))
        for rel in range(1, N_DEV):
            going = src if kind == "gather" else src.at[me ^ rel]
            common = dict(send_sem=send_sems.at[k, rel], recv_sem=recv_sems.at[k, rel],
                          device_id=_peer(x, y, c, rel), device_id_type=MESH)
            sends.append(pltpu.make_async_remote_copy(src_ref=going, dst_ref=dst.at[me], **common))
            if arrivals:
                recvs.append(pltpu.make_async_remote_copy(src_ref=own, dst_ref=dst.at[me ^ rel], **common))
    return local, sends, recvs


def _exchange_start(kinds, srcs, dsts, sems):
    local, sends, _ = _exchange_copies(kinds, srcs, dsts, sems, arrivals=False)
    for cp in local + sends:
        cp.start()


def _exchange_finish(kinds, srcs, dsts, sems):
    local, sends, recvs = _exchange_copies(kinds, srcs, dsts, sems, arrivals=True)
    for cp in recvs:
        cp.wait_recv()
    for cp in sends:
        cp.wait_send()
    for cp in local:
        cp.wait()


def _gather_two_level_with_rope_tables(shard, inv_freq, s_len, name):
    rows = ROW_TILE

    def body(inv_ref, src, cos_ref, sin_ref, dst, send_sems, recv_sems, local_sem):
        x, y, c, me = _position()
        sibling = (x, y, 1 - c)
        chips = [(1 - x, y), (x, 1 - y), (1 - x, 1 - y)]

        def block(px, py, pc):
            return dst.at[4 * px + 2 * py + pc]

        def copy(k, blk, to, src_ref=None):
            return pltpu.make_async_remote_copy(
                src_ref=block(*blk) if src_ref is None else src_ref, dst_ref=block(*blk),
                send_sem=send_sems.at[k], recv_sem=recv_sems.at[k], device_id=to, device_id_type=MESH)

        x_nbr, y_nbr, diag = chips
        mine = pltpu.make_async_copy(src, dst.at[me], local_sem)
        mine.start()
        first = [copy(0, (x, y, c), sibling, src), copy(1, (x, y, c), (*x_nbr, c), src), copy(2, (x, y, c), (*y_nbr, c), src)]
        for cp in first:
            cp.start()

        inv = inv_ref[...]
        lane = lax.broadcasted_iota(jnp.int32, (rows, 128), 1)
        sign = jnp.where((lane // (HEAD_DIM // 2)) % 2 == 0, -1.0, 1.0)
        row = lax.broadcasted_iota(jnp.int32, (rows, 128), 0)
        n_chunks = s_len // rows

        def fill_tables(lo, hi):
            @pl.loop(lo, hi)
            def _(i):
                at = pl.multiple_of(i * rows, rows)
                ang = (row + at).astype(F32) * inv
                cos_ref[pl.ds(at, rows), :] = jnp.cos(ang)
                sin_ref[pl.ds(at, rows), :] = jnp.sin(ang) * sign

        fill_tables(0, n_chunks // 2)
        passed = [copy(4 + j, (*chip, c), sibling) for j, chip in enumerate(chips)]
        copy(1, (*x_nbr, c), (x, y, c)).wait_recv()
        copy(2, (*y_nbr, c), (x, y, c)).wait_recv()

        @pl.when(c == 1)
        def _():
            copy(3, (*x_nbr, c), (*y_nbr, c)).start()

        @pl.when(c == 0)
        def _():
            copy(3, (*y_nbr, c), (*x_nbr, c)).start()

        passed[0].start()
        passed[1].start()
        fill_tables(n_chunks // 2, n_chunks)
        copy(3, (*diag, c), (x, y, c)).wait_recv()
        passed[2].start()
        copy(0, (x, y, 1 - c), (x, y, c)).wait_recv()
        for j, chip in enumerate(chips):
            copy(4 + j, (*chip, 1 - c), (x, y, c)).wait_recv()
        for cp in first + passed:
            cp.wait_send()
        copy(3, (*x_nbr, c), (*y_nbr, c)).wait_send()
        mine.wait()

    any_spec = pl.BlockSpec(memory_space=pl.ANY)
    vmem = pl.BlockSpec(memory_space=pltpu.VMEM)
    table = jax.ShapeDtypeStruct((s_len, 128), F32)
    return pl.pallas_call(
        body, name=name, in_specs=[vmem, any_spec], out_specs=[vmem, vmem, any_spec],
        out_shape=[table, table, _exchange_out_shape("gather", shard)],
        scratch_shapes=[pltpu.SemaphoreType.DMA((N_DEV - 1,)), pltpu.SemaphoreType.DMA((N_DEV - 1,)), pltpu.SemaphoreType.DMA],
        compiler_params=pltpu.CompilerParams(has_side_effects=True, vmem_limit_bytes=V7X_VMEM_LIMIT_BYTES),
    )(inv_freq, shard)


def _call(body, *, name, grid, in_specs, out_specs, out_shape, args, scratch_shapes=(), semantics, exchange=()):
    if not exchange:
        outs = pl.pallas_call(body, name=name, grid=grid, in_specs=in_specs, out_specs=out_specs, out_shape=out_shape,
                              scratch_shapes=list(scratch_shapes), compiler_params=_params(*semantics))(*args)
        return outs, []
    kinds = [k for k, _ in exchange]
    n_in, n_out, n_x, n_scr = len(in_specs), len(out_specs), len(exchange), len(scratch_shapes)

    def wrapped(*refs):
        ins, refs = refs[:n_in], refs[n_in:]
        srcs, refs = refs[:n_x], refs[n_x:]
        outs, refs = refs[:n_out], refs[n_out:]
        dsts, refs = refs[:n_x], refs[n_x:]
        scratch, sems = refs[:n_scr], refs[n_scr:]
        ids = [pl.program_id(a) for a in range(len(grid))]
        first = functools.reduce(jnp.logical_and, [i == 0 for i in ids])
        last = functools.reduce(jnp.logical_and, [i == g - 1 for i, g in zip(ids, grid)])

        @pl.when(first)
        def _():
            _exchange_start(kinds, srcs, dsts, sems)

        body(*ins, *outs, *scratch)

        @pl.when(last)
        def _():
            _exchange_finish(kinds, srcs, dsts, sems)

    any_spec = pl.BlockSpec(memory_space=pl.ANY)
    outs = pl.pallas_call(
        wrapped, name=name, grid=grid,
        in_specs=list(in_specs) + [any_spec] * n_x, out_specs=list(out_specs) + [any_spec] * n_x,
        out_shape=list(out_shape) + [_exchange_out_shape(k, a) for k, a in exchange],
        scratch_shapes=list(scratch_shapes) + _exchange_sems(n_x),
        compiler_params=pltpu.CompilerParams(dimension_semantics=("arbitrary",) * len(grid),
                                             vmem_limit_bytes=V7X_VMEM_LIMIT_BYTES, has_side_effects=True),
    )(*args, *[a for _, a in exchange])
    return outs[:n_out], outs[n_out:]


def _adamw_math(w, g, m, v):
    m = ADAM_B1 * m + (1.0 - ADAM_B1) * g
    v = ADAM_B2 * v + (1.0 - ADAM_B2) * (g * g)
    m_hat = m / (1.0 - ADAM_B1 ** ADAM_STEP)
    v_hat = v / (1.0 - ADAM_B2 ** ADAM_STEP)
    return -ADAM_LR * (m_hat / (jnp.sqrt(v_hat) + ADAM_EPS) + ADAM_WD * w), m, v


def _adamw(parts, w, m, v, name, exchange=()):
    rows, cols = w.shape
    tm = 256 if rows % 256 == 0 and rows > 256 else rows

    def body(p_ref, w_ref, m_ref, v_ref, g_ref, d_ref, nm_ref, nv_ref):
        g = p_ref[0].astype(F32)
        for j in range(1, N_DEV):
            g = g + p_ref[j].astype(F32)
        delta, nm, nv = _adamw_math(w_ref[...], g, m_ref[...], v_ref[...])
        g_ref[...] = g
        d_ref[...] = delta
        nm_ref[...] = nm
        nv_ref[...] = nv

    shard = jax.ShapeDtypeStruct((rows, cols), F32)
    return _call(
        body, name=name, grid=(rows // tm,),
        in_specs=[pl.BlockSpec((N_DEV, tm, cols), lambda i: (0, i, 0))] + [_rows(tm, cols)] * 3,
        out_specs=[_rows(tm, cols)] * 4,
        out_shape=[shard] * 4,
        semantics=("parallel",), args=(parts, w, m, v), exchange=exchange)


_SMALL = ("mix_norm_g", "sgu_w", "sgu_b", "sgu_norm_g", "out_norm_a", "out_norm_b", "ffn_norm_g", "ple_norm_g", "final_norm_g")
_BIG = ("w_in", "w_out", "w_gate", "w_up", "w_down", "w_ple_gate", "w_ple_proj")
_COLUMN_SHARDED = ("w_in", "w_gate", "w_up", "w_ple_proj")
_ORDER = ("mix_norm_g", "w_in", "sgu_w", "sgu_b", "sgu_norm_g", "out_norm_a", "out_norm_b", "w_out", "ffn_norm_g",
          "w_gate", "w_up", "w_down", "ple_norm_g", "w_ple_gate", "w_ple_proj", "final_norm_g")


def _pack_small(values, names=_SMALL):
    flat = jnp.concatenate([values[n].reshape(-1).astype(F32) for n in names])
    pad = (-flat.shape[0]) % (8 * 128)
    return jnp.pad(flat, (0, pad)).reshape(-1, 128)


def _unpack_small(packed, like):
    flat = packed.reshape(-1)
    out, at = {}, 0
    for n in _SMALL:
        size = like[n].size
        out[n] = flat[at:at + size].reshape(like[n].shape)
        at += size
    return out


def _own_orientation(name, value):
    return value[0].T if name in _COLUMN_SHARDED else value[0]


def _reference_orientation(name, value):
    return (value.T if name in _COLUMN_SHARDED else value)[None]


def _full_from_gathered(gathered):
    return gathered.reshape(N_DEV * gathered.shape[1], gathered.shape[2])


def _sliced_for_devices(grad):
    return grad.reshape(N_DEV, grad.shape[0] // N_DEV, grad.shape[1])


def _rope_inv_freq():
    half = HEAD_DIM // 2
    inv = ROPE_THETA ** (-jnp.arange(half, dtype=F32) / half)
    return jnp.tile(inv, 128 // half)[None, :]


def _forward_backward(x, p, target, small, shards):
    def gather(*names):
        return [("gather", shards[n]) for n in names]

    def scatter(**grads):
        return [("scatter", _sliced_for_devices(g)) for g in grads.values()]

    full, parts = {}, {}
    s_len = x.shape[0]
    cos, sin, got = _gather_two_level_with_rope_tables(shards["w_in"], _rope_inv_freq(), s_len, "gather_w_in")
    full["w_in"] = _full_from_gathered(got)

    g_mix, g_ffn, g_ple = small["mix_norm_g"], small["ffn_norm_g"], small["ple_norm_g"]
    g_fin = small["final_norm_g"].reshape(1, D_MODEL)
    sw, gs, ga, gb = small["sgu_w"], small["sgu_norm_g"], small["out_norm_a"], small["out_norm_b"]
    b2 = jnp.repeat(small["sgu_b"].T, HEAD_DIM, axis=1)
    head_sum = (jnp.arange(WIDTH_B)[:, None] // HEAD_DIM == jnp.arange(128)[None, :]).astype(BF16)
    n_br = len(DILATIONS)

    def arrived(names, got):
        for n, g in zip(names, got):
            full[n] = _full_from_gathered(g)

    (ua, hn1, *qkv), got = _inproj(x, g_mix, full["w_in"], cos, sin, exchange=gather("w_gate"))
    arrived(("w_gate",), got)
    ya_n = _sgu_fwd(ua, sw, b2, gs, ga)
    half = shards["w_up"].shape[0] // 2
    riders = [[("gather", shards["w_up"][:half])], [("gather", shards["w_up"][half:])], gather("w_out")]
    branch, got = [], []
    for i, d in enumerate(DILATIONS):
        o_l, g = _attn_fwd(qkv[i], d, exchange=riders[i])
        branch.append(o_l)
        got += g
    arrived(("w_up", "w_out"), [jnp.concatenate(got[:2], axis=1), got[2]])
    y, yb, *lse = _combine([o for o, _ in branch], [l for _, l in branch], ya_n, gb, head_sum.T)
    last_wave = ("w_down", "w_ple_gate", "w_ple_proj")
    (h1, a, b, act, hn2), got = _ffn_up(y, full["w_out"], x, g_ffn, full["w_gate"], full["w_up"], exchange=gather(*last_wave))
    arrived(last_wave, got)
    h2, h3, gate, pp, hn3 = _ffn_down_ple(act, full["w_down"], h1, g_ple, full["w_ple_gate"], p, full["w_ple_proj"])

    dh2, loss, d_fin, d_ple, g_ple_gate, g_ple_proj = _loss_ple_bwd(
        h3, target, g_fin, gate, pp, h2, g_ple, full["w_ple_gate"], hn3, p)
    g_down = _mm_tn(act, dh2, "dw_down")
    (da, db), (parts["w_down"],) = _ffn_down_bwd(dh2, full["w_down"], a, b, exchange=scatter(w_down=g_down))
    g_gate = _mm_tn(da, hn2, "dw_gate")
    g_up = _mm_tn(db, hn2, "dw_up")
    (dh1, d_ffn, g_out), (parts["w_gate"], parts["w_ple_gate"], parts["w_ple_proj"]) = _mm_norm_bwd(
        [(da, full["w_gate"]), (db, full["w_up"])], h1, g_ffn, dh2, "ffn_up_bwd",
        exchange=scatter(w_gate=g_gate, w_ple_gate=g_ple_gate, w_ple_proj=g_ple_proj), dw_lhs=y)
    dya_n, d_gb, *do_dd = _outproj_bwd(dh1, full["w_out"], yb, gb, head_sum)
    grads_b = []
    for i, d in enumerate(DILATIONS):
        g3, got = _attn_bwd(qkv[i], do_dd[i], lse[i], do_dd[n_br + i], d,
                            exchange=scatter(w_up=g_up, w_out=g_out) if i == 0 else ())
        grads_b.append(g3)
        if i == 0:
            parts["w_up"], parts["w_out"] = got
    dua, d_sw, d_b2, d_gs, d_ga = _sgu_bwd(ua, sw, b2, gs, ga, dya_n)
    early = {
        "sgu_w": d_sw, "sgu_b": d_b2.reshape(CHUNK, 4, HEAD_DIM).sum(axis=-1).T, "sgu_norm_g": d_gs, "out_norm_a": d_ga,
        "out_norm_b": d_gb, "ffn_norm_g": d_ffn, "ple_norm_g": d_ple, "final_norm_g": d_fin,
    }
    dproj, g_in, (early_parts,) = _dproj(
        dua, [g[0] for g in grads_b], [g[1] for g in grads_b], [g[2] for g in grads_b], cos, sin, hn1,
        exchange=[("gather", _pack_small(early, _SMALL[1:]))])
    (dx, d_mix), (parts["w_in"],) = _mm_norm_bwd(
        [(dproj, full["w_in"])], x, g_mix, dh1, "inproj_bwd", exchange=scatter(w_in=g_in))
    late = jnp.concatenate([_pack_small({"mix_norm_g": d_mix}, _SMALL[:1]), jnp.broadcast_to(loss, (8, 128))])
    return dx, parts, early_parts, late


def kernel(x, p, mix_norm_g, w_in, sgu_w, sgu_b, sgu_norm_g, out_norm_a, out_norm_b, w_out, ffn_norm_g, w_gate, w_up, w_down, ple_norm_g, w_ple_gate, w_ple_proj, final_norm_g, loss_target, m_mix_norm_g, m_w_in, m_sgu_w, m_sgu_b, m_sgu_norm_g, m_out_norm_a, m_out_norm_b, m_w_out, m_ffn_norm_g, m_w_gate, m_w_up, m_w_down, m_ple_norm_g, m_w_ple_gate, m_w_ple_proj, m_final_norm_g, v_mix_norm_g, v_w_in, v_sgu_w, v_sgu_b, v_sgu_norm_g, v_out_norm_a, v_out_norm_b, v_w_out, v_ffn_norm_g, v_w_gate, v_w_up, v_w_down, v_ple_norm_g, v_w_ple_gate, v_w_ple_proj, v_final_norm_g):
    given = dict(locals())
    weights = {n: given[n] for n in _ORDER}
    moments_m = {n: given["m_" + n] for n in _ORDER}
    moments_v = {n: given["v_" + n] for n in _ORDER}

    shards = {n: _own_orientation(n, weights[n]).astype(BF16) for n in _BIG}
    small = {n: (weights[n][0] if n in ("sgu_w", "sgu_b") else weights[n]) for n in _SMALL}

    dx, parts, early_parts, late = _forward_backward(x[0], p[0, 0], loss_target[0], small, shards)

    small_like = {n: weights[n] for n in _SMALL}
    grads, deltas, new_m, new_v = {}, {}, {}, {}
    for n in _BIG:
        outs, got = _adamw(parts[n], _own_orientation(n, weights[n]), _own_orientation(n, moments_m[n]),
                           _own_orientation(n, moments_v[n]), "adamw_" + n, exchange=[("gather", late)] if n == "w_in" else ())
        grads[n], deltas[n], new_m[n], new_v[n] = [_reference_orientation(n, o) for o in outs]
        if n == "w_in":
            (late_parts,) = got
    loss = jnp.sum(late_parts[:, 8, 0])
    small_parts = jnp.concatenate([late_parts[:, :8], early_parts], axis=1)
    (g, d, nm, nv), _ = _adamw(small_parts, _pack_small(small_like), _pack_small({n: moments_m[n] for n in _SMALL}),
                               _pack_small({n: moments_v[n] for n in _SMALL}), "adamw_small")
    for out, packed in ((grads, g), (deltas, d), (new_m, nm), (new_v, nv)):
        out.update(_unpack_small(packed, small_like))

    return (loss, dx[None], *[grads[n] for n in _ORDER], *[deltas[n] for n in _ORDER],
            *[new_m[n] for n in _ORDER], *[new_v[n] for n in _ORDER])
```

```python
import functools

import jax
import jax.numpy as jnp
from jax import lax
from jax.experimental import pallas as pl
from jax.experimental.pallas import tpu as pltpu

F32 = jnp.float32
BF16 = jnp.bfloat16

D_MODEL = 1024
WIDTH_A = 256
WIDTH_B = 768
D_FF = 2816
IN_COLS = 2 * WIDTH_A + 3 * WIDTH_B
PLE_DIM = 256
HEAD_DIM = 64
N_PAIRS = WIDTH_B // 128
CHUNK = 128
N_BACK = 128
DILATIONS = (1, 4, 16)
ROPE_THETA = 10000.0
EPS = 1e-6
N_DEV = 8

ADAM_LR = 0.001
ADAM_B1 = 0.9
ADAM_B2 = 0.999
ADAM_EPS = 1e-08
ADAM_WD = 0.01
ADAM_STEP = 10

V7X_VMEM_LIMIT_BYTES = 56 * 1024 * 1024
ROW_TILE = 512
MESH = pl.DeviceIdType.MESH
NEG = -1e30

_NT = (((1,), (1,)), ((), ()))
_TN = (((0,), (0,)), ((), ()))


def _params(*semantics):
    return pltpu.CompilerParams(dimension_semantics=semantics, vmem_limit_bytes=V7X_VMEM_LIMIT_BYTES)


def _rows(tm, width):
    return pl.BlockSpec((tm, width), lambda i: (i, 0))


def _whole(shape):
    return pl.BlockSpec(shape, lambda *_: (0,) * len(shape))


def _resident(shape):
    return pl.BlockSpec(shape, lambda *_: (0,) * len(shape), pipeline_mode=pl.Buffered(1))


def _gelu(x):
    t = jnp.tanh(0.7978845608028654 * (x + 0.044715 * (x * x * x)))
    return 0.5 * x * (1.0 + t)


def _gelu_grad(x):
    t = jnp.tanh(0.7978845608028654 * (x + 0.044715 * (x * x * x)))
    return 0.5 * (1.0 + t) + 0.5 * x * (1.0 - t * t) * (0.7978845608028654 * (1.0 + 3.0 * 0.044715 * (x * x)))


def _rstd(x):
    return lax.rsqrt(jnp.mean(x * x, axis=-1, keepdims=True) + EPS)


def _norm_bwd(dn, h, g, r=None):
    r = _rstd(h) if r is None else r
    n = h * r
    t = dn * g
    return r * (t - n * jnp.mean(t * n, axis=-1, keepdims=True)), dn * n


def _swap_halves(x, first_half):
    return jnp.where(first_half, pltpu.roll(x, 96, 1), pltpu.roll(x, 32, 1))


def _sub_spec(d, n_cb, tm):
    return pl.BlockSpec((d, n_cb, tm // d, 128), lambda i: (0, 0, i, 0))


def _sub_shape(s_len, d, n_cb, dtype):
    return jax.ShapeDtypeStruct((d, n_cb, s_len // d, 128), dtype)


def _to_sub(stage_ref, cb_src, out_ref, cb_dst, d, tm):
    slab = stage_ref.at[cb_src]
    for r in range(d):
        out_ref[r, cb_dst] = slab[pl.ds(r, tm // d, stride=d), :].astype(out_ref.dtype)


def _from_sub(in_ref, cb_src, stage_ref, cb_dst, d, tm):
    slab = stage_ref.at[cb_dst]
    for r in range(d):
        slab[pl.ds(r, tm // d, stride=d), :] = in_ref[r, cb_src].astype(F32)


def _inproj(x, g, w, cos, sin, exchange=()):
    s_len = x.shape[0]
    tm = ROW_TILE
    n_cb = 3 * N_PAIRS

    def body(x_ref, g_ref, w_ref, cos_ref, sin_ref, ua_ref, hn_ref, *rest):
        sub_refs, stage = rest[:-1], rest[-1]
        xf = x_ref[...]
        hn = (xf * _rstd(xf) * g_ref[...]).astype(BF16)
        hn_ref[...] = hn
        c = cos_ref[...]
        s = sin_ref[...]
        first_half = (lax.broadcasted_iota(jnp.int32, (tm, 128), 1) % HEAD_DIM) < HEAD_DIM // 2
        for col in range(0, IN_COLS, 512):
            width = min(512, IN_COLS - col)
            acc = lax.dot_general(hn, w_ref[col:col + width, :], _NT, preferred_element_type=F32)
            if col < 2 * WIDTH_A:
                ua_ref[:, col:col + width] = acc
                continue
            for part in range(width // 128):
                cb = (col - 2 * WIDTH_A) // 128 + part
                t = acc[:, part * 128:(part + 1) * 128]
                if cb < 2 * N_PAIRS:
                    t = (t * c + _swap_halves(t, first_half) * s) * (0.125 if cb < N_PAIRS else 1.0)
                stage[cb] = t
                sub_refs[0][0, cb] = t.astype(BF16)
        for cb in range(n_cb):
            for d, out_ref in zip(DILATIONS[1:], sub_refs[1:]):
                _to_sub(stage, cb, out_ref, cb, d, tm)

    return _call(
        body, name="inproj", grid=(s_len // tm,),
        in_specs=[_rows(tm, D_MODEL), _whole((1, D_MODEL)), _resident((IN_COLS, D_MODEL)), _rows(tm, 128), _rows(tm, 128)],
        out_specs=[_rows(tm, 2 * WIDTH_A), _rows(tm, D_MODEL)] + [_sub_spec(d, n_cb, tm) for d in DILATIONS],
        out_shape=[jax.ShapeDtypeStruct((s_len, 2 * WIDTH_A), F32), jax.ShapeDtypeStruct((s_len, D_MODEL), BF16)]
        + [_sub_shape(s_len, d, n_cb, BF16) for d in DILATIONS],
        scratch_shapes=[pltpu.VMEM((n_cb, tm, 128), F32)],
        semantics=("parallel",), args=(x, g, w, cos, sin), exchange=exchange)


def _sgu_mix_weights(sw_ref):
    keep = lax.broadcasted_iota(jnp.int32, (CHUNK, CHUNK), 0) >= lax.broadcasted_iota(jnp.int32, (CHUNK, CHUNK), 1)
    return [jnp.where(keep, sw_ref[h], 0.0).astype(BF16) for h in range(4)], keep


def _sgu_core(ua_ref, gs_ref):
    u = ua_ref[:, :WIDTH_A]
    va = ua_ref[:, WIDTH_A:]
    vg = _gelu(va)
    xc = vg - jnp.mean(vg, axis=-1, keepdims=True)
    rstd = lax.rsqrt(jnp.mean(xc * xc, axis=-1, keepdims=True) + EPS)
    xhat = xc * rstd
    return u, va, _gelu(u), xhat, rstd, xhat * gs_ref[...]


def _sgu_mix(wm, vnb, half, head_a):
    va, vb = _both_heads(vnb[:, half * 128:(half + 1) * 128], head_a)
    return (jnp.dot(wm[2 * half], va, preferred_element_type=F32)
            + jnp.dot(wm[2 * half + 1], vb, preferred_element_type=F32))


def _sgu_fwd(ua, sw, b2, gs, ga):
    s_len = ua.shape[0]
    tm = ROW_TILE

    def body(ua_ref, sw_ref, b2_ref, gs_ref, ga_ref, out_ref):
        _, _, ug, _, _, vn = _sgu_core(ua_ref, gs_ref)
        wm, _ = _sgu_mix_weights(sw_ref)
        head_a = lax.broadcasted_iota(jnp.int32, (CHUNK, 128), 1) < HEAD_DIM
        for c in range(tm // CHUNK):
            rows = slice(c * CHUNK, (c + 1) * CHUNK)
            vnb = vn[rows].astype(BF16)
            mixed = b2_ref[...] + jnp.concatenate([_sgu_mix(wm, vnb, half, head_a) for half in range(2)], axis=1)
            ya = ug[rows] * mixed
            out_ref[rows, :] = (ya * _rstd(ya) * ga_ref[...]).astype(BF16)

    return pl.pallas_call(
        body, name="sgu_fwd", grid=(s_len // tm,),
        in_specs=[_rows(tm, 2 * WIDTH_A), _whole((4, CHUNK, CHUNK)), _whole((CHUNK, WIDTH_A)), _whole((1, WIDTH_A)), _whole((1, WIDTH_A))],
        out_specs=_rows(tm, WIDTH_A),
        out_shape=jax.ShapeDtypeStruct((s_len, WIDTH_A), BF16),
        compiler_params=_params("parallel"),
    )(ua, sw, b2, gs, ga)


STEP_POSITIONS = 2 * ROW_TILE


def _attn_geometry(sd):
    tile = min(STEP_POSITIONS, sd)
    return tile, tile // CHUNK, sd // tile, STEP_POSITIONS // tile


PAIRS_PER_STEP = 6


def _attn_spec(cb0, rows, row_index, res):
    return pl.BlockSpec((res, PAIRS_PER_STEP, rows, 128), lambda r, g, n: (r, cb0 // PAIRS_PER_STEP + g, row_index(n), 0))


assert PAIRS_PER_STEP == N_PAIRS
assert DILATIONS[0] == 1


def _stats_spec(rows, row_index, res):
    return pl.BlockSpec((res, None, rows, 128), lambda r, g, n: (r, 0, row_index(n), 0))


def _stats_shape(sd, d):
    return jax.ShapeDtypeStruct((d, 1, sd, 128), F32)


def _both_heads(x, head_a):
    zero = jnp.zeros_like(x)
    return [jnp.where(head_a, x, zero), jnp.where(head_a, zero, x)]


def _attn_fwd(qkv, d, exchange=()):
    sd = qkv.shape[2]
    tile, nb, n_tiles, res = _attn_geometry(sd)

    def prev(n):
        return jnp.maximum(n * nb - 1, 0)

    def body(q_ref, k_ref, kp_ref, v_ref, vp_ref, o_ref, l_ref):
        for rr in range(res):
            for hp in range(PAIRS_PER_STEP):
                one_pair(hp, q_ref.at[rr, hp], k_ref.at[rr, hp], kp_ref.at[rr, hp], v_ref.at[rr, hp], vp_ref.at[rr, hp],
                         o_ref.at[rr, hp], l_ref.at[rr])

    def one_pair(hp, q_ref, k_ref, kp_ref, v_ref, vp_ref, o_ref, l_ref):
        n = pl.program_id(2)
        lane = lax.broadcasted_iota(jnp.int32, (CHUNK, 128), 1)
        head_a = lane < HEAD_DIM
        qi = lax.broadcasted_iota(jnp.int32, (2 * CHUNK, 2 * CHUNK), 0) % CHUNK
        kc = lax.broadcasted_iota(jnp.int32, (2 * CHUNK, 2 * CHUNK), 1)
        band = (kc >= qi) & (kc <= qi + N_BACK)
        for j in range(nb):
            rows = slice(j * CHUNK, (j + 1) * CHUNK)
            if j == 0:
                kcat = jnp.concatenate([kp_ref[...], k_ref[rows, :]], axis=0)
                vcat = jnp.concatenate([vp_ref[...], v_ref[rows, :]], axis=0)
                valid = band & jnp.logical_or(n > 0, kc >= CHUNK)
            else:
                kcat = k_ref[(j - 1) * CHUNK:(j + 1) * CHUNK, :]
                vcat = v_ref[(j - 1) * CHUNK:(j + 1) * CHUNK, :]
                valid = band
            q2 = jnp.concatenate(_both_heads(q_ref[rows, :], head_a), axis=0)
            s = lax.dot_general(q2, kcat, _NT, preferred_element_type=F32)
            s = jnp.where(valid, s, NEG)
            m = jnp.max(s, axis=-1, keepdims=True)
            p = jnp.exp(s - m)
            l = jnp.sum(p, axis=-1, keepdims=True)
            o2 = jnp.dot(p.astype(BF16), vcat, preferred_element_type=F32) / l
            lse2 = m + jnp.log(l)
            o_ref[rows, :] = jnp.where(head_a, o2[:CHUNK], o2[CHUNK:]).astype(BF16)
            others = l_ref[rows, :] if hp > 0 else jnp.zeros((CHUNK, 128), F32)
            l_ref[rows, :] = jnp.where(lane == 2 * hp, lse2[:CHUNK], jnp.where(lane == 2 * hp + 1, lse2[CHUNK:], others))

    same = lambda n: n
    return _call(
        body, name=f"attn_fwd_d{d}", grid=(d // res, N_PAIRS // PAIRS_PER_STEP, n_tiles),
        in_specs=[_attn_spec(0, tile, same, res), _attn_spec(N_PAIRS, tile, same, res), _attn_spec(N_PAIRS, CHUNK, prev, res),
                  _attn_spec(2 * N_PAIRS, tile, same, res), _attn_spec(2 * N_PAIRS, CHUNK, prev, res)],
        out_specs=[_attn_spec(0, tile, same, res), _stats_spec(tile, same, res)],
        out_shape=[jax.ShapeDtypeStruct((d, N_PAIRS, sd, 128), BF16), _stats_shape(sd, d)],
        semantics=("parallel", "parallel", "parallel"), args=(qkv, qkv, qkv, qkv, qkv), exchange=exchange)


def _combine(outs, lses, ya_n, gb, head_spread):
    s_len = ya_n.shape[0]
    tm = ROW_TILE
    n_br = len(DILATIONS)

    def body(*refs):
        o_refs, l_refs = refs[:n_br], refs[n_br:2 * n_br]
        ya_ref, gb_ref, spread_ref, y_ref, yb_ref = refs[2 * n_br:2 * n_br + 5]
        lse_refs = refs[2 * n_br + 5:3 * n_br + 5]
        o_nat, l_nat, lse_nat, w_wide = refs[3 * n_br + 5:]
        for i, d in enumerate(DILATIONS):
            _from_sub(l_refs[i], 0, l_nat, i, d, tm)
        ls = [l_nat[i] for i in range(n_br)]
        top = jnp.maximum(jnp.maximum(ls[0], ls[1]), ls[2])
        ws = [jnp.exp(l - top) for l in ls]
        den = ws[0] + ws[1] + ws[2]
        inv = 1.0 / den
        for i in range(n_br):
            w = ws[i] * inv
            hi = w.astype(BF16)
            lo = (w - hi.astype(F32)).astype(BF16)
            w_wide[i] = (jnp.dot(hi, spread_ref[...], preferred_element_type=F32)
                         + jnp.dot(lo, spread_ref[...], preferred_element_type=F32))
        lse_nat[0] = top + jnp.log(den)
        for d, lse_ref in zip(DILATIONS, lse_refs):
            _to_sub(lse_nat, 0, lse_ref, 0, d, tm)
        sumsq = jnp.zeros((tm, 1), F32)
        for cb in range(N_PAIRS):
            cols = slice(cb * 128, (cb + 1) * 128)
            yb = w_wide[0, :, cols] * o_refs[0][0, cb].astype(F32)
            for i, d in enumerate(DILATIONS[1:], start=1):
                _from_sub(o_refs[i], cb, o_nat, i, d, tm)
                yb = yb + w_wide[i, :, cols] * o_nat[i]
            yb_ref[:, cb * 128:(cb + 1) * 128] = yb
            sumsq = sumsq + jnp.sum(yb * yb, axis=-1, keepdims=True)
        r = lax.rsqrt(sumsq / WIDTH_B + EPS)
        y_ref[:, :WIDTH_A] = ya_ref[...]
        y_ref[:, WIDTH_A:] = (yb_ref[...] * r * gb_ref[...]).astype(BF16)

    stats = [_sub_spec(d, 1, tm) for d in DILATIONS]
    return _call(
        body, name="attn_combine", grid=(s_len // tm,),
        in_specs=[_sub_spec(d, N_PAIRS, tm) for d in DILATIONS] + stats
        + [_rows(tm, WIDTH_A), _whole((1, WIDTH_B)), _whole((128, WIDTH_B))],
        out_specs=[_rows(tm, D_MODEL), _rows(tm, WIDTH_B)] + stats,
        out_shape=[jax.ShapeDtypeStruct((s_len, D_MODEL), BF16), jax.ShapeDtypeStruct((s_len, WIDTH_B), F32)]
        + [_sub_shape(s_len, d, 1, F32) for d in DILATIONS],
        scratch_shapes=[pltpu.VMEM((n_br, tm, 128), F32), pltpu.VMEM((n_br, tm, 128), F32), pltpu.VMEM((1, tm, 128), F32),
                        pltpu.VMEM((n_br, tm, WIDTH_B), F32)],
        semantics=("parallel",), args=(*outs, *lses, ya_n, gb, head_spread))[0]


def _ffn_up(y, wout, x, g, wg, wu, exchange=()):
    s_len = x.shape[0]
    tm = ROW_TILE

    def body(y_ref, wo_ref, x_ref, g_ref, wg_ref, wu_ref, h_ref, a_ref, b_ref, act_ref, hn_ref):
        hf = x_ref[...] + jnp.dot(y_ref[...], wo_ref[...], preferred_element_type=F32)
        h_ref[...] = hf
        hn = (hf * _rstd(hf) * g_ref[...]).astype(BF16)
        hn_ref[...] = hn
        for col in range(0, D_FF, 256):
            cols = slice(col, col + 256)
            a = lax.dot_general(hn, wg_ref[cols, :], _NT, preferred_element_type=F32)
            b = lax.dot_general(hn, wu_ref[cols, :], _NT, preferred_element_type=F32)
            a_ref[:, cols] = a.astype(BF16)
            b_ref[:, cols] = b.astype(BF16)
            act_ref[:, cols] = (a * jax.nn.sigmoid(a) * b).astype(BF16)

    wide = jax.ShapeDtypeStruct((s_len, D_FF), BF16)
    return _call(
        body, name="ffn_up", grid=(s_len // tm,),
        in_specs=[_rows(tm, D_MODEL), _resident((D_MODEL, D_MODEL)), _rows(tm, D_MODEL), _whole((1, D_MODEL)),
                  _resident((D_FF, D_MODEL)), _resident((D_FF, D_MODEL))],
        out_specs=[_rows(tm, D_MODEL), _rows(tm, D_FF), _rows(tm, D_FF), _rows(tm, D_FF), _rows(tm, D_MODEL)],
        out_shape=[jax.ShapeDtypeStruct((s_len, D_MODEL), F32), wide, wide, wide, jax.ShapeDtypeStruct((s_len, D_MODEL), BF16)],
        semantics=("parallel",), args=(y, wout, x, g, wg, wu), exchange=exchange)


def _ffn_down_ple(act, wd, h1, g, wpg, p, wpp):
    s_len = h1.shape[0]
    tm = ROW_TILE

    def body(act_ref, wd_ref, h1_ref, g_ref, wpg_ref, p_ref, wpp_ref, h2_ref, h3_ref, gate_ref, pp_ref, hn_ref):
        hf = h1_ref[...] + jnp.dot(act_ref[...], wd_ref[...], preferred_element_type=F32)
        h2_ref[...] = hf
        hn = (hf * _rstd(hf) * g_ref[...]).astype(BF16)
        hn_ref[...] = hn
        gate = jax.nn.sigmoid(jnp.dot(hn, wpg_ref[...], preferred_element_type=F32))
        pp = lax.dot_general(p_ref[...].astype(BF16), wpp_ref[...], _NT, preferred_element_type=F32)
        h3_ref[...] = hf + gate * pp
        gate_ref[...] = gate.astype(BF16)
        pp_ref[...] = pp.astype(BF16)

    full = jax.ShapeDtypeStruct((s_len, D_MODEL), F32)
    half = jax.ShapeDtypeStruct((s_len, D_MODEL), BF16)
    return pl.pallas_call(
        body, name="ffn_down_ple", grid=(s_len // tm,),
        in_specs=[_rows(tm, D_FF), _resident((D_FF, D_MODEL)), _rows(tm, D_MODEL), _whole((1, D_MODEL)),
                  _resident((D_MODEL, D_MODEL)), _rows(tm, PLE_DIM), _resident((D_MODEL, PLE_DIM))],
        out_specs=[_rows(tm, D_MODEL)] * 5,
        out_shape=[full, full, half, half, half],
        compiler_params=_params("parallel"),
    )(act, wd, h1, g, wpg, p, wpp)


def _loss_ple_bwd(h3, target, gf, gate, pp, h2, g_ple, wpg, hn3, p):
    s_len = h3.shape[0]
    tm = ROW_TILE
    n_steps = s_len // tm

    def body(h_ref, t_ref, g_ref, gate_ref, pp_ref, h2_ref, gp_ref, w_ref, hn_ref, p_ref,
             dh2_ref, loss_ref, dg_ref, dgp_ref, dwg_ref, dwp_ref, acc_g, acc_p):
        step = pl.program_id(0)

        @pl.when(step == 0)
        def _():
            loss_ref[...] = jnp.zeros_like(loss_ref)
            dg_ref[...] = jnp.zeros_like(dg_ref)
            dgp_ref[...] = jnp.zeros_like(dgp_ref)
            acc_g[...] = jnp.zeros_like(acc_g)
            acc_p[...] = jnp.zeros_like(acc_p)

        hf = h_ref[...]
        gfv = g_ref[...]
        r = _rstd(hf)
        err = hf * r * gfv - t_ref[...]
        loss_ref[...] += 0.5 * jnp.sum(jnp.sum(err * err, axis=-1, keepdims=True), axis=0, keepdims=True) / D_MODEL
        dh, dg_rows = _norm_bwd(err / D_MODEL, hf, gfv, r)
        dg_ref[...] += jnp.sum(dg_rows, axis=0, keepdims=True)
        gate = gate_ref[...].astype(F32)
        dz = (dh * pp_ref[...].astype(F32) * gate * (1.0 - gate)).astype(BF16)
        dpp = (dh * gate).astype(BF16)
        dn = lax.dot_general(dz, w_ref[...], _NT, preferred_element_type=F32)
        dh2, dgp_rows = _norm_bwd(dn, h2_ref[...], gp_ref[...])
        dh2 = dh + dh2
        dh2_ref[...] = dh2
        dgp_ref[...] += jnp.sum(dgp_rows, axis=0, keepdims=True)
        acc_g[...] += lax.dot_general(hn_ref[...], dz, _TN, preferred_element_type=F32)
        acc_p[...] += lax.dot_general(dpp, p_ref[...].astype(BF16), _TN, preferred_element_type=F32)

        @pl.when(step == n_steps - 1)
        def _():
            dwg_ref[...] = acc_g[...].astype(BF16)
            dwp_ref[...] = acc_p[...].astype(BF16)

    gain = jax.ShapeDtypeStruct((1, D_MODEL), F32)
    return pl.pallas_call(
        body, name="loss_ple_bwd", grid=(n_steps,),
        in_specs=[_rows(tm, D_MODEL), _rows(tm, D_MODEL), _whole((1, D_MODEL)), _rows(tm, D_MODEL), _rows(tm, D_MODEL),
                  _rows(tm, D_MODEL), _whole((1, D_MODEL)), _resident((D_MODEL, D_MODEL)), _rows(tm, D_MODEL),
                  _rows(tm, PLE_DIM)],
        out_specs=[_rows(tm, D_MODEL), _whole((1, 128)), _whole((1, D_MODEL)), _whole((1, D_MODEL)),
                   _whole((D_MODEL, D_MODEL)), _whole((D_MODEL, PLE_DIM))],
        out_shape=[jax.ShapeDtypeStruct((s_len, D_MODEL), F32), jax.ShapeDtypeStruct((1, 128), F32), gain, gain,
                   jax.ShapeDtypeStruct((D_MODEL, D_MODEL), BF16), jax.ShapeDtypeStruct((D_MODEL, PLE_DIM), BF16)],
        scratch_shapes=[pltpu.VMEM((D_MODEL, D_MODEL), F32), pltpu.VMEM((D_MODEL, PLE_DIM), F32)],
        compiler_params=_params("arbitrary"),
    )(h3, target, gf, gate, pp, h2, g_ple, wpg, hn3, p)


def _mm_norm_bwd(parts, h, g, dres, name, exchange=(), dw_lhs=None):
    s_len = h.shape[0]
    tm = ROW_TILE
    n_parts = len(parts)
    n_steps = s_len // tm
    has_dw = dw_lhs is not None

    def body(*refs):
        a_refs = refs[0:2 * n_parts:2]
        w_refs = refs[1:2 * n_parts:2]
        h_ref, g_ref, r_ref = refs[2 * n_parts:2 * n_parts + 3]
        rest = refs[2 * n_parts + 3:]
        step = pl.program_id(0)
        if has_dw:
            lhs_ref, o_ref, dg_ref, dw_ref, acc_ref = rest
        else:
            o_ref, dg_ref = rest

        @pl.when(step == 0)
        def _():
            dg_ref[...] = jnp.zeros_like(dg_ref)
            if has_dw:
                acc_ref[...] = jnp.zeros_like(acc_ref)

        dn = jnp.dot(a_refs[0][...], w_refs[0][...], preferred_element_type=F32)
        for a_ref, w_ref in zip(a_refs[1:], w_refs[1:]):
            dn = dn + jnp.dot(a_ref[...], w_ref[...], preferred_element_type=F32)
        dh, dg_rows = _norm_bwd(dn, h_ref[...], g_ref[...])
        out = r_ref[...] + dh
        o_ref[...] = out
        dg_ref[...] += jnp.sum(dg_rows, axis=0, keepdims=True)
        if has_dw:
            acc_ref[...] += lax.dot_general(lhs_ref[...], out.astype(BF16), _TN, preferred_element_type=F32)

            @pl.when(step == n_steps - 1)
            def _():
                dw_ref[...] = acc_ref[...].astype(BF16)

    in_specs, args = [], []
    for a, w in parts:
        in_specs += [_rows(tm, a.shape[1]), _resident(w.shape)]
        args += [a, w]
    in_specs += [_rows(tm, D_MODEL), _whole((1, D_MODEL)), _rows(tm, D_MODEL)]
    args += [h, g, dres]
    out_specs = [_rows(tm, D_MODEL), _whole((1, D_MODEL))]
    out_shape = [jax.ShapeDtypeStruct((s_len, D_MODEL), F32), jax.ShapeDtypeStruct((1, D_MODEL), F32)]
    scratch = []
    if has_dw:
        m = dw_lhs.shape[1]
        in_specs.append(_rows(tm, m))
        args.append(dw_lhs)
        out_specs.append(_whole((m, D_MODEL)))
        out_shape.append(jax.ShapeDtypeStruct((m, D_MODEL), BF16))
        scratch.append(pltpu.VMEM((m, D_MODEL), F32))
    return _call(
        body, name=name, grid=(n_steps,), in_specs=in_specs, out_specs=out_specs, out_shape=out_shape,
        scratch_shapes=scratch, semantics=("arbitrary",), args=tuple(args), exchange=exchange)


def _ffn_down_bwd(dh, wdt, a, b, exchange=()):
    s_len = dh.shape[0]
    tm = ROW_TILE

    def body(dh_ref, w_ref, a_ref, b_ref, da_ref, db_ref):
        dhb = dh_ref[...].astype(BF16)
        for col in range(0, D_FF, 512):
            cols = slice(col, min(col + 512, D_FF))
            dact = lax.dot_general(dhb, w_ref[cols, :], _NT, preferred_element_type=F32)
            av = a_ref[:, cols].astype(F32)
            bv = b_ref[:, cols].astype(F32)
            sig = jax.nn.sigmoid(av)
            t = dact * sig
            silu = av * sig
            da_ref[:, cols] = (t * bv * (1.0 + av - silu)).astype(BF16)
            db_ref[:, cols] = (dact * silu).astype(BF16)

    wide = jax.ShapeDtypeStruct((s_len, D_FF), BF16)
    return _call(
        body, name="ffn_down_bwd", grid=(s_len // tm,),
        in_specs=[_rows(tm, D_MODEL), _resident((D_FF, D_MODEL)), _rows(tm, D_FF), _rows(tm, D_FF)],
        out_specs=[_rows(tm, D_FF), _rows(tm, D_FF)],
        out_shape=[wide, wide],
        semantics=("parallel",), args=(dh, wdt, a, b), exchange=exchange)


def _outproj_bwd(dh1, woutt, yb, gb, head_sum, ua, sw, b2, gs, ga):
    s_len = dh1.shape[0]
    tm = ROW_TILE
    n_br = len(DILATIONS)

    def body(dh_ref, w_ref, yb_ref, gb_ref, e_ref, ua_ref, sw_ref, b2_ref, gs_ref, ga_ref,
             dgb_ref, dua_ref, dsw_ref, db2_ref, dgs_ref, dga_ref, *rest):
        do_refs, dd_refs = rest[:n_br], rest[n_br:2 * n_br]
        do_nat, dd_nat = rest[2 * n_br:]

        @pl.when(pl.program_id(0) == 0)
        def _():
            for ref in (dgb_ref, dsw_ref, db2_ref, dgs_ref, dga_ref):
                ref[...] = jnp.zeros_like(ref)

        dhb = dh_ref[...].astype(BF16)
        dya_n = lax.dot_general(dhb, w_ref[:WIDTH_A, :], _NT, preferred_element_type=F32)
        _sgu_bwd_tile(ua_ref, sw_ref, b2_ref, gs_ref, ga_ref, dya_n, dua_ref, dsw_ref, db2_ref, dgs_ref, dga_ref, tm)
        dyn = lax.dot_general(dhb, w_ref[WIDTH_A:, :], _NT, preferred_element_type=F32)
        ybv = yb_ref[...]
        dyb, dg_rows = _norm_bwd(dyn, ybv, gb_ref[...])
        dgb_ref[...] += jnp.sum(dg_rows, axis=0, keepdims=True)
        prod = dyb * ybv
        hi = prod.astype(BF16)
        lo = (prod - hi.astype(F32)).astype(BF16)
        dd_nat[0] = (jnp.dot(hi, e_ref[...], preferred_element_type=F32)
                     + jnp.dot(lo, e_ref[...], preferred_element_type=F32))
        for i, d in enumerate(DILATIONS):
            _to_sub(dd_nat, 0, dd_refs[i], 0, d, tm)
        for cb in range(N_PAIRS):
            piece = dyb[:, cb * 128:(cb + 1) * 128]
            do_nat[cb] = piece
            do_refs[0][0, cb] = piece.astype(BF16)
            for i, d in enumerate(DILATIONS[1:], start=1):
                _to_sub(do_nat, cb, do_refs[i], cb, d, tm)

    return _call(
        body, name="outproj_bwd", grid=(s_len // tm,),
        in_specs=[_rows(tm, D_MODEL), _resident((D_MODEL, D_MODEL)), _rows(tm, WIDTH_B), _whole((1, WIDTH_B)), _whole((WIDTH_B, 128)),
                  _rows(tm, 2 * WIDTH_A), _whole((4, CHUNK, CHUNK)), _whole((CHUNK, WIDTH_A)), _whole((1, WIDTH_A)), _whole((1, WIDTH_A))],
        out_specs=[_whole((1, WIDTH_B)), _rows(tm, 2 * WIDTH_A), _whole((4, CHUNK, CHUNK)), _whole((CHUNK, WIDTH_A)),
                   _whole((1, WIDTH_A)), _whole((1, WIDTH_A))]
        + [_sub_spec(d, N_PAIRS, tm) for d in DILATIONS] + [_sub_spec(d, 1, tm) for d in DILATIONS],
        out_shape=[jax.ShapeDtypeStruct((1, WIDTH_B), F32), jax.ShapeDtypeStruct((s_len, 2 * WIDTH_A), BF16),
                   jax.ShapeDtypeStruct((4, CHUNK, CHUNK), F32), jax.ShapeDtypeStruct((CHUNK, WIDTH_A), F32),
                   jax.ShapeDtypeStruct((1, WIDTH_A), F32), jax.ShapeDtypeStruct((1, WIDTH_A), F32)]
        + [_sub_shape(s_len, d, N_PAIRS, BF16) for d in DILATIONS] + [_sub_shape(s_len, d, 1, F32) for d in DILATIONS],
        scratch_shapes=[pltpu.VMEM((N_PAIRS, tm, 128), F32), pltpu.VMEM((1, tm, 128), F32)],
        semantics=("arbitrary",), args=(dh1, woutt, yb, gb, head_sum, ua, sw, b2, gs, ga))[0]


def _attn_bwd(qkv, do, lse, dd, d, exchange=()):
    sd = qkv.shape[2]
    tile, nb, n_tiles, res = _attn_geometry(sd)
    last_block = sd // CHUNK - 1

    def nxt(n):
        return jnp.minimum((n + 1) * nb, last_block)

    def block(ref, next_ref, j):
        return ref[j * CHUNK:(j + 1) * CHUNK, :] if j < nb else next_ref[...]

    def body(q_ref, qn_ref, k_ref, v_ref, do_ref, don_ref, l_ref, ln_ref, dd_ref, ddn_ref,
             dq_ref, dk_ref, dv_ref, carry_ref):
        for rr in range(res):
            l_t = [block(l_ref.at[rr], ln_ref.at[rr], j).T for j in range(nb + 1)]
            dd_t = [block(dd_ref.at[rr], ddn_ref.at[rr], j).T for j in range(nb + 1)]
            for hp in range(PAIRS_PER_STEP):
                l_rows = [jnp.concatenate([t[2 * hp:2 * hp + 1, :], t[2 * hp + 1:2 * hp + 2, :]], axis=1) for t in l_t]
                dd_rows = [jnp.concatenate([t[2 * hp:2 * hp + 1, :], t[2 * hp + 1:2 * hp + 2, :]], axis=1) for t in dd_t]
                one_pair(q_ref.at[rr, hp], qn_ref.at[rr, hp], k_ref.at[rr, hp], v_ref.at[rr, hp], do_ref.at[rr, hp],
                         don_ref.at[rr, hp], l_rows, dd_rows, dq_ref.at[rr, hp], dk_ref.at[rr, hp], dv_ref.at[rr, hp],
                         carry_ref.at[rr, hp])

    def one_pair(q_ref, qn_ref, k_ref, v_ref, do_ref, don_ref, l_rows, dd_rows, dq_ref, dk_ref, dv_ref, carry_ref):
        n = pl.program_id(2)

        @pl.when(n == 0)
        def _():
            carry_ref[...] = jnp.zeros_like(carry_ref)

        head_a = lax.broadcasted_iota(jnp.int32, (CHUNK, 128), 1) < HEAD_DIM
        col = lax.broadcasted_iota(jnp.int32, (CHUNK, 4 * CHUNK), 1)
        qi = col % CHUNK
        ki = lax.broadcasted_iota(jnp.int32, (CHUNK, 4 * CHUNK), 0)
        is_after = col >= 2 * CHUNK
        mask = (is_after & (ki >= qi)) | (jnp.logical_not(is_after) & (qi >= ki))
        mask_last = mask & jnp.logical_or(jnp.logical_not(is_after), n < n_tiles - 1)
        dq_acc = [carry_ref[...]] + [jnp.zeros((CHUNK, 128), F32) for _ in range(nb)]

        q_st = [jnp.concatenate(_both_heads(block(q_ref, qn_ref, j), head_a), axis=0) for j in range(nb + 1)]
        do_st = [jnp.concatenate(_both_heads(block(do_ref, don_ref, j), head_a), axis=0) for j in range(nb + 1)]

        for j in range(nb):
            rows = slice(j * CHUNK, (j + 1) * CHUNK)
            kj = k_ref[rows, :]
            vj = v_ref[rows, :]
            msk = mask if j + 1 < nb else mask_last
            qs = jnp.concatenate([q_st[j], q_st[j + 1]], axis=0)
            dos = jnp.concatenate([do_st[j], do_st[j + 1]], axis=0)
            ls = jnp.concatenate([l_rows[j], l_rows[j + 1]], axis=1)
            dds = jnp.concatenate([dd_rows[j], dd_rows[j + 1]], axis=1)
            st = lax.dot_general(kj, qs, _NT, preferred_element_type=F32)
            pt = jnp.exp(jnp.where(msk, st - ls, NEG))
            dpt = lax.dot_general(vj, dos, _NT, preferred_element_type=F32)
            dst = (pt * (dpt - dds)).astype(BF16)
            dv_ref[rows, :] = jnp.dot(pt.astype(BF16), dos, preferred_element_type=F32).astype(BF16)
            dk_ref[rows, :] = jnp.dot(dst, qs, preferred_element_type=F32).astype(BF16)
            dqs = lax.dot_general(dst, kj, _TN, preferred_element_type=F32)
            dq_acc[j] = dq_acc[j] + jnp.where(head_a, dqs[:CHUNK], dqs[CHUNK:2 * CHUNK])
            dq_acc[j + 1] = dq_acc[j + 1] + jnp.where(head_a, dqs[2 * CHUNK:3 * CHUNK], dqs[3 * CHUNK:])
        for j in range(nb):
            dq_ref[j * CHUNK:(j + 1) * CHUNK, :] = dq_acc[j].astype(BF16)
        carry_ref[...] = dq_acc[nb]

    same = lambda n: n
    grad = jax.ShapeDtypeStruct((d, N_PAIRS, sd, 128), BF16)
    return _call(
        body, name=f"attn_bwd_d{d}", grid=(d // res, N_PAIRS // PAIRS_PER_STEP, n_tiles),
        in_specs=[_attn_spec(0, tile, same, res), _attn_spec(0, CHUNK, nxt, res), _attn_spec(N_PAIRS, tile, same, res),
                  _attn_spec(2 * N_PAIRS, tile, same, res), _attn_spec(0, tile, same, res), _attn_spec(0, CHUNK, nxt, res),
                  _stats_spec(tile, same, res), _stats_spec(CHUNK, nxt, res), _stats_spec(tile, same, res),
                  _stats_spec(CHUNK, nxt, res)],
        out_specs=[_attn_spec(0, tile, same, res)] * 3,
        out_shape=[grad, grad, grad],
        scratch_shapes=[pltpu.VMEM((res, PAIRS_PER_STEP, CHUNK, 128), F32)],
        semantics=("parallel", "parallel", "arbitrary"), args=(qkv, qkv, qkv, qkv, do, do, lse, lse, dd, dd), exchange=exchange)


def _sgu_bwd_tile(ua_ref, sw_ref, b2_ref, gs_ref, ga_ref, dy, dua_ref, dsw_ref, db2_ref, dgs_ref, dga_ref, tm):
    u, va, ug, xhat, rstd, vn = _sgu_core(ua_ref, gs_ref)
    wm, keep = _sgu_mix_weights(sw_ref)
    head_a = lax.broadcasted_iota(jnp.int32, (CHUNK, 128), 1) < HEAD_DIM
    gav = ga_ref[...]
    gsv = gs_ref[...]
    dga = jnp.zeros((1, WIDTH_A), F32)
    dgs = jnp.zeros((1, WIDTH_A), F32)
    db2 = jnp.zeros((CHUNK, WIDTH_A), F32)
    dsw = [jnp.zeros((CHUNK, CHUNK), F32) for _ in range(4)]
    for c in range(tm // CHUNK):
        rows = slice(c * CHUNK, (c + 1) * CHUNK)
        vnb = vn[rows].astype(BF16)
        mixed = b2_ref[...] + jnp.concatenate([_sgu_mix(wm, vnb, half, head_a) for half in range(2)], axis=1)
        ugc = ug[rows]
        dya, dga_rows = _norm_bwd(dy[rows], ugc * mixed, gav)
        dga = dga + jnp.sum(dga_rows, axis=0, keepdims=True)
        dmixed = dya * ugc
        db2 = db2 + dmixed
        dmb = dmixed.astype(BF16)
        dvn_halves = []
        for half in range(2):
            lanes = slice(half * 128, (half + 1) * 128)
            dm_heads = _both_heads(dmb[:, lanes], head_a)
            dvn_half = jnp.zeros((CHUNK, 128), F32)
            for k, dmh in enumerate(dm_heads):
                h = 2 * half + k
                dsw[h] = dsw[h] + lax.dot_general(dmh, vnb[:, lanes], _NT, preferred_element_type=F32)
                dvn_half = dvn_half + lax.dot_general(wm[h], dmh, _TN, preferred_element_type=F32)
            dvn_halves.append(dvn_half)
        dvn = jnp.concatenate(dvn_halves, axis=1)
        xh = xhat[rows]
        dgs = dgs + jnp.sum(dvn * xh, axis=0, keepdims=True)
        dxh = dvn * gsv
        dvg = rstd[rows] * (dxh - jnp.mean(dxh, axis=-1, keepdims=True) - xh * jnp.mean(dxh * xh, axis=-1, keepdims=True))
        dua_ref[rows, :WIDTH_A] = (dya * mixed * _gelu_grad(u[rows])).astype(BF16)
        dua_ref[rows, WIDTH_A:] = (dvg * _gelu_grad(va[rows])).astype(BF16)
    for h in range(4):
        dsw_ref[h] += jnp.where(keep, dsw[h], 0.0)
    db2_ref[...] += db2
    dgs_ref[...] += dgs
    dga_ref[...] += dga


def _dproj(dua, dqs, dks, dvs, cos, sin, hn1, exchange=()):
    s_len = dua.shape[0]
    tm = ROW_TILE
    n_br = len(DILATIONS)
    n_steps = s_len // tm

    def body(dua_ref, *rest):
        groups = [rest[g * n_br:(g + 1) * n_br] for g in range(3)]
        cos_ref, sin_ref, hn_ref, out_ref, dw_ref, acc, dw_acc = rest[3 * n_br:]
        step = pl.program_id(0)

        @pl.when(step == 0)
        def _():
            dw_acc[...] = jnp.zeros_like(dw_acc)

        out_ref[:, :2 * WIDTH_A] = dua_ref[...]
        c = cos_ref[...]
        s = sin_ref[...]
        first_half = (lax.broadcasted_iota(jnp.int32, (tm, 128), 1) % HEAD_DIM) < HEAD_DIM // 2
        for g, refs in enumerate(groups):
            for cb in range(N_PAIRS):
                t = refs[0][0, cb].astype(F32)
                for i, d in enumerate(DILATIONS[1:]):
                    _from_sub(refs[i + 1], cb, acc, i, d, tm)
                    t = t + acc[i]
                if g < 2:
                    t = (t * c - _swap_halves(t, first_half) * s) * (0.125 if g == 0 else 1.0)
                col = 2 * WIDTH_A + g * WIDTH_B + cb * 128
                out_ref[:, col:col + 128] = t.astype(BF16)
        hn = hn_ref[...]
        for j in range(IN_COLS // 256):
            cols = slice(j * 256, (j + 1) * 256)
            dw_acc[cols, :] += lax.dot_general(out_ref[:, cols], hn, _TN, preferred_element_type=F32)

        @pl.when(step == n_steps - 1)
        def _():
            dw_ref[...] = dw_acc[...].astype(BF16)

    subs = [_sub_spec(d, N_PAIRS, tm) for d in DILATIONS]
    (dproj, dw), received = _call(
        body, name="dproj_dw_in", grid=(n_steps,),
        in_specs=[_rows(tm, 2 * WIDTH_A)] + subs * 3 + [_rows(tm, 128), _rows(tm, 128), _rows(tm, D_MODEL)],
        out_specs=[_rows(tm, IN_COLS), _whole((IN_COLS, D_MODEL))],
        out_shape=[jax.ShapeDtypeStruct((s_len, IN_COLS), BF16), jax.ShapeDtypeStruct((IN_COLS, D_MODEL), BF16)],
        scratch_shapes=[pltpu.VMEM((n_br - 1, tm, 128), F32), pltpu.VMEM((IN_COLS, D_MODEL), F32)],
        semantics=("arbitrary",), args=(dua, *dqs, *dks, *dvs, cos, sin, hn1), exchange=exchange)
    return dproj, dw, received


def _mm_tn(a, b, name):
    s_len, m = a.shape
    n = b.shape[1]
    tk = 2 * ROW_TILE
    tm = m
    n_k = s_len // tk

    def body(a_ref, b_ref, o_ref, acc_ref):
        k = pl.program_id(1)

        @pl.when(k == 0)
        def _():
            acc_ref[...] = jnp.zeros_like(acc_ref)

        bb = b_ref[...].astype(BF16)
        for j in range(tm // 256):
            rows = slice(j * 256, (j + 1) * 256)
            acc_ref[rows, :] += lax.dot_general(a_ref[:, rows].astype(BF16), bb, _TN, preferred_element_type=F32)

        @pl.when(k == n_k - 1)
        def _():
            o_ref[...] = acc_ref[...].astype(BF16)

    return pl.pallas_call(
        body, name=name, grid=(m // tm, n_k),
        in_specs=[pl.BlockSpec((tk, tm), lambda i, k: (k, i)), pl.BlockSpec((tk, n), lambda i, k: (k, 0))],
        out_specs=pl.BlockSpec((tm, n), lambda i, k: (i, 0)),
        out_shape=jax.ShapeDtypeStruct((m, n), BF16),
        scratch_shapes=[pltpu.VMEM((tm, n), F32)],
        compiler_params=_params("parallel", "arbitrary"),
    )(a, b)


def _position():
    x, y, c = lax.axis_index("x"), lax.axis_index("y"), lax.axis_index("c")
    return x, y, c, 4 * x + 2 * y + c


def _peer(x, y, c, rel):
    return (x ^ ((rel >> 2) & 1), y ^ ((rel >> 1) & 1), c ^ (rel & 1))


def _exchange_out_shape(kind, arr):
    return jax.ShapeDtypeStruct(((N_DEV,) + arr.shape) if kind == "gather" else arr.shape, arr.dtype)


def _exchange_sems(n_items):
    return [pltpu.SemaphoreType.DMA((n_items, N_DEV)), pltpu.SemaphoreType.DMA((n_items, N_DEV)), pltpu.SemaphoreType.DMA((n_items,))]


def _exchange_copies(kinds, srcs, dsts, sems, arrivals):
    send_sems, recv_sems, local_sems = sems
    x, y, c, me = _position()
    local, sends, recvs = [], [], []
    for k, (kind, src, dst) in enumerate(zip(kinds, srcs, dsts)):
        own = src if kind == "gather" else src.at[me]
        local.append(pltpu.make_async_copy(own, dst.at[me], local_sems.at[k]))
        for rel in range(1, N_DEV):
            going = src if kind == "gather" else src.at[me ^ rel]
            common = dict(send_sem=send_sems.at[k, rel], recv_sem=recv_sems.at[k, rel],
                          device_id=_peer(x, y, c, rel), device_id_type=MESH)
            sends.append(pltpu.make_async_remote_copy(src_ref=going, dst_ref=dst.at[me], **common))
            if arrivals:
                recvs.append(pltpu.make_async_remote_copy(src_ref=own, dst_ref=dst.at[me ^ rel], **common))
    return local, sends, recvs


def _exchange_start(kinds, srcs, dsts, sems):
    local, sends, _ = _exchange_copies(kinds, srcs, dsts, sems, arrivals=False)
    for cp in local + sends:
        cp.start()


def _exchange_finish(kinds, srcs, dsts, sems):
    local, sends, recvs = _exchange_copies(kinds, srcs, dsts, sems, arrivals=True)
    for cp in recvs:
        cp.wait_recv()
    for cp in sends:
        cp.wait_send()
    for cp in local:
        cp.wait()


def _gather_two_level_with_rope_tables(shard, inv_freq, s_len, name):
    rows = ROW_TILE

    def body(inv_ref, src, cos_ref, sin_ref, dst, send_sems, recv_sems, local_sem):
        x, y, c, me = _position()
        sibling = (x, y, 1 - c)
        chips = [(1 - x, y), (x, 1 - y), (1 - x, 1 - y)]

        def block(px, py, pc):
            return dst.at[4 * px + 2 * py + pc]

        def copy(k, blk, to, src_ref=None):
            return pltpu.make_async_remote_copy(
                src_ref=block(*blk) if src_ref is None else src_ref, dst_ref=block(*blk),
                send_sem=send_sems.at[k], recv_sem=recv_sems.at[k], device_id=to, device_id_type=MESH)

        x_nbr, y_nbr, diag = chips
        mine = pltpu.make_async_copy(src, dst.at[me], local_sem)
        mine.start()
        first = [copy(0, (x, y, c), sibling, src), copy(1, (x, y, c), (*x_nbr, c), src), copy(2, (x, y, c), (*y_nbr, c), src)]
        for cp in first:
            cp.start()

        inv = inv_ref[...]
        lane = lax.broadcasted_iota(jnp.int32, (rows, 128), 1)
        sign = jnp.where((lane // (HEAD_DIM // 2)) % 2 == 0, -1.0, 1.0)
        row = lax.broadcasted_iota(jnp.int32, (rows, 128), 0)
        n_chunks = s_len // rows

        def fill_tables(lo, hi):
            @pl.loop(lo, hi)
            def _(i):
                at = pl.multiple_of(i * rows, rows)
                ang = (row + at).astype(F32) * inv
                cos_ref[pl.ds(at, rows), :] = jnp.cos(ang)
                sin_ref[pl.ds(at, rows), :] = jnp.sin(ang) * sign

        fill_tables(0, n_chunks // 2)
        passed = [copy(4 + j, (*chip, c), sibling) for j, chip in enumerate(chips)]
        copy(1, (*x_nbr, c), (x, y, c)).wait_recv()
        copy(2, (*y_nbr, c), (x, y, c)).wait_recv()

        @pl.when(c == 1)
        def _():
            copy(3, (*x_nbr, c), (*y_nbr, c)).start()

        @pl.when(c == 0)
        def _():
            copy(3, (*y_nbr, c), (*x_nbr, c)).start()

        passed[0].start()
        passed[1].start()
        fill_tables(n_chunks // 2, n_chunks)
        copy(3, (*diag, c), (x, y, c)).wait_recv()
        passed[2].start()
        copy(0, (x, y, 1 - c), (x, y, c)).wait_recv()
        for j, chip in enumerate(chips):
            copy(4 + j, (*chip, 1 - c), (x, y, c)).wait_recv()
        for cp in first + passed:
            cp.wait_send()
        copy(3, (*x_nbr, c), (*y_nbr, c)).wait_send()
        mine.wait()

    any_spec = pl.BlockSpec(memory_space=pl.ANY)
    vmem = pl.BlockSpec(memory_space=pltpu.VMEM)
    table = jax.ShapeDtypeStruct((s_len, 128), F32)
    return pl.pallas_call(
        body, name=name, in_specs=[vmem, any_spec], out_specs=[vmem, vmem, any_spec],
        out_shape=[table, table, _exchange_out_shape("gather", shard)],
        scratch_shapes=[pltpu.SemaphoreType.DMA((N_DEV - 1,)), pltpu.SemaphoreType.DMA((N_DEV - 1,)), pltpu.SemaphoreType.DMA],
        compiler_params=pltpu.CompilerParams(has_side_effects=True, vmem_limit_bytes=V7X_VMEM_LIMIT_BYTES),
    )(inv_freq, shard)


def _call(body, *, name, grid, in_specs, out_specs, out_shape, args, scratch_shapes=(), semantics, exchange=()):
    if not exchange:
        outs = pl.pallas_call(body, name=name, grid=grid, in_specs=in_specs, out_specs=out_specs, out_shape=out_shape,
                              scratch_shapes=list(scratch_shapes), compiler_params=_params(*semantics))(*args)
        return outs, []
    kinds = [k for k, _ in exchange]
    n_in, n_out, n_x, n_scr = len(in_specs), len(out_specs), len(exchange), len(scratch_shapes)

    def wrapped(*refs):
        ins, refs = refs[:n_in], refs[n_in:]
        srcs, refs = refs[:n_x], refs[n_x:]
        outs, refs = refs[:n_out], refs[n_out:]
        dsts, refs = refs[:n_x], refs[n_x:]
        scratch, sems = refs[:n_scr], refs[n_scr:]
        ids = [pl.program_id(a) for a in range(len(grid))]
        first = functools.reduce(jnp.logical_and, [i == 0 for i in ids])
        last = functools.reduce(jnp.logical_and, [i == g - 1 for i, g in zip(ids, grid)])

        @pl.when(first)
        def _():
            _exchange_start(kinds, srcs, dsts, sems)

        body(*ins, *outs, *scratch)

        @pl.when(last)
        def _():
            _exchange_finish(kinds, srcs, dsts, sems)

    any_spec = pl.BlockSpec(memory_space=pl.ANY)
    outs = pl.pallas_call(
        wrapped, name=name, grid=grid,
        in_specs=list(in_specs) + [any_spec] * n_x, out_specs=list(out_specs) + [any_spec] * n_x,
        out_shape=list(out_shape) + [_exchange_out_shape(k, a) for k, a in exchange],
        scratch_shapes=list(scratch_shapes) + _exchange_sems(n_x),
        compiler_params=pltpu.CompilerParams(dimension_semantics=("arbitrary",) * len(grid),
                                             vmem_limit_bytes=V7X_VMEM_LIMIT_BYTES, has_side_effects=True),
    )(*args, *[a for _, a in exchange])
    return outs[:n_out], outs[n_out:]


def _adamw_math(w, g, m, v):
    m = ADAM_B1 * m + (1.0 - ADAM_B1) * g
    v = ADAM_B2 * v + (1.0 - ADAM_B2) * (g * g)
    m_hat = m / (1.0 - ADAM_B1 ** ADAM_STEP)
    v_hat = v / (1.0 - ADAM_B2 ** ADAM_STEP)
    return -ADAM_LR * (m_hat / (jnp.sqrt(v_hat) + ADAM_EPS) + ADAM_WD * w), m, v


def _adamw(parts, w, m, v, name, exchange=()):
    rows, cols = w.shape
    tm = 256 if rows % 256 == 0 and rows > 256 else rows

    def body(p_ref, w_ref, m_ref, v_ref, g_ref, d_ref, nm_ref, nv_ref):
        g = p_ref[0].astype(F32)
        for j in range(1, N_DEV):
            g = g + p_ref[j].astype(F32)
        delta, nm, nv = _adamw_math(w_ref[...], g, m_ref[...], v_ref[...])
        g_ref[...] = g
        d_ref[...] = delta
        nm_ref[...] = nm
        nv_ref[...] = nv

    shard = jax.ShapeDtypeStruct((rows, cols), F32)
    return _call(
        body, name=name, grid=(rows // tm,),
        in_specs=[pl.BlockSpec((N_DEV, tm, cols), lambda i: (0, i, 0))] + [_rows(tm, cols)] * 3,
        out_specs=[_rows(tm, cols)] * 4,
        out_shape=[shard] * 4,
        semantics=("parallel",), args=(parts, w, m, v), exchange=exchange)


_SMALL = ("mix_norm_g", "sgu_w", "sgu_b", "sgu_norm_g", "out_norm_a", "out_norm_b", "ffn_norm_g", "ple_norm_g", "final_norm_g")
_BIG = ("w_in", "w_out", "w_gate", "w_up", "w_down", "w_ple_gate", "w_ple_proj")
_COLUMN_SHARDED = ("w_in", "w_gate", "w_up", "w_ple_proj")
_ORDER = ("mix_norm_g", "w_in", "sgu_w", "sgu_b", "sgu_norm_g", "out_norm_a", "out_norm_b", "w_out", "ffn_norm_g",
          "w_gate", "w_up", "w_down", "ple_norm_g", "w_ple_gate", "w_ple_proj", "final_norm_g")


def _pack_small(values, names=_SMALL):
    flat = jnp.concatenate([values[n].reshape(-1).astype(F32) for n in names])
    pad = (-flat.shape[0]) % (8 * 128)
    return jnp.pad(flat, (0, pad)).reshape(-1, 128)


def _unpack_small(packed, like):
    flat = packed.reshape(-1)
    out, at = {}, 0
    for n in _SMALL:
        size = like[n].size
        out[n] = flat[at:at + size].reshape(like[n].shape)
        at += size
    return out


def _own_orientation(name, value):
    return value[0].T if name in _COLUMN_SHARDED else value[0]


def _reference_orientation(name, value):
    return (value.T if name in _COLUMN_SHARDED else value)[None]


def _full_from_gathered(gathered):
    return gathered.reshape(N_DEV * gathered.shape[1], gathered.shape[2])


def _sliced_for_devices(grad):
    return grad.reshape(N_DEV, grad.shape[0] // N_DEV, grad.shape[1])


def _rope_inv_freq():
    half = HEAD_DIM // 2
    inv = ROPE_THETA ** (-jnp.arange(half, dtype=F32) / half)
    return jnp.tile(inv, 128 // half)[None, :]


def _forward_backward(x, p, target, small, shards):
    def gather(*names):
        return [("gather", shards[n]) for n in names]

    def scatter(**grads):
        return [("scatter", _sliced_for_devices(g)) for g in grads.values()]

    full, parts = {}, {}
    s_len = x.shape[0]
    cos, sin, got = _gather_two_level_with_rope_tables(shards["w_in"], _rope_inv_freq(), s_len, "gather_w_in")
    full["w_in"] = _full_from_gathered(got)

    g_mix, g_ffn, g_ple = small["mix_norm_g"], small["ffn_norm_g"], small["ple_norm_g"]
    g_fin = small["final_norm_g"].reshape(1, D_MODEL)
    sw, gs, ga, gb = small["sgu_w"], small["sgu_norm_g"], small["out_norm_a"], small["out_norm_b"]
    b2 = jnp.repeat(small["sgu_b"].T, HEAD_DIM, axis=1)
    head_sum = (jnp.arange(WIDTH_B)[:, None] // HEAD_DIM == jnp.arange(128)[None, :]).astype(BF16)
    n_br = len(DILATIONS)

    def arrived(names, got):
        for n, g in zip(names, got):
            full[n] = _full_from_gathered(g)

    (ua, hn1, *qkv), got = _inproj(x, g_mix, full["w_in"], cos, sin, exchange=gather("w_gate"))
    arrived(("w_gate",), got)
    ya_n = _sgu_fwd(ua, sw, b2, gs, ga)
    half = shards["w_up"].shape[0] // 2
    riders = [[("gather", shards["w_up"][:half])], [("gather", shards["w_up"][half:])], gather("w_out")]
    branch, got = [], []
    for i, d in enumerate(DILATIONS):
        o_l, g = _attn_fwd(qkv[i], d, exchange=riders[i])
        branch.append(o_l)
        got += g
    arrived(("w_up", "w_out"), [jnp.concatenate(got[:2], axis=1), got[2]])
    y, yb, *lse = _combine([o for o, _ in branch], [l for _, l in branch], ya_n, gb, head_sum.T)
    last_wave = ("w_down", "w_ple_gate", "w_ple_proj")
    (h1, a, b, act, hn2), got = _ffn_up(y, full["w_out"], x, g_ffn, full["w_gate"], full["w_up"], exchange=gather(*last_wave))
    arrived(last_wave, got)
    h2, h3, gate, pp, hn3 = _ffn_down_ple(act, full["w_down"], h1, g_ple, full["w_ple_gate"], p, full["w_ple_proj"])

    dh2, loss, d_fin, d_ple, g_ple_gate, g_ple_proj = _loss_ple_bwd(
        h3, target, g_fin, gate, pp, h2, g_ple, full["w_ple_gate"], hn3, p)
    g_down = _mm_tn(act, dh2, "dw_down")
    (da, db), (parts["w_down"],) = _ffn_down_bwd(dh2, full["w_down"], a, b, exchange=scatter(w_down=g_down))
    g_gate = _mm_tn(da, hn2, "dw_gate")
    g_up = _mm_tn(db, hn2, "dw_up")
    (dh1, d_ffn, g_out), (parts["w_gate"], parts["w_ple_gate"], parts["w_ple_proj"]) = _mm_norm_bwd(
        [(da, full["w_gate"]), (db, full["w_up"])], h1, g_ffn, dh2, "ffn_up_bwd",
        exchange=scatter(w_gate=g_gate, w_ple_gate=g_ple_gate, w_ple_proj=g_ple_proj), dw_lhs=y)
    d_gb, dua, d_sw, d_b2, d_gs, d_ga, *do_dd = _outproj_bwd(dh1, full["w_out"], yb, gb, head_sum, ua, sw, b2, gs, ga)
    grads_b = []
    for i, d in enumerate(DILATIONS):
        g3, got = _attn_bwd(qkv[i], do_dd[i], lse[i], do_dd[n_br + i], d,
                            exchange=scatter(w_up=g_up, w_out=g_out) if i == 0 else ())
        grads_b.append(g3)
        if i == 0:
            parts["w_up"], parts["w_out"] = got
    early = {
        "sgu_w": d_sw, "sgu_b": d_b2.reshape(CHUNK, 4, HEAD_DIM).sum(axis=-1).T, "sgu_norm_g": d_gs, "out_norm_a": d_ga,
        "out_norm_b": d_gb, "ffn_norm_g": d_ffn, "ple_norm_g": d_ple, "final_norm_g": d_fin,
    }
    dproj, g_in, (early_parts,) = _dproj(
        dua, [g[0] for g in grads_b], [g[1] for g in grads_b], [g[2] for g in grads_b], cos, sin, hn1,
        exchange=[("gather", _pack_small(early, _SMALL[1:]))])
    (dx, d_mix), (parts["w_in"],) = _mm_norm_bwd(
        [(dproj, full["w_in"])], x, g_mix, dh1, "inproj_bwd", exchange=scatter(w_in=g_in))
    late = jnp.concatenate([_pack_small({"mix_norm_g": d_mix}, _SMALL[:1]), jnp.broadcast_to(loss, (8, 128))])
    return dx, parts, early_parts, late


def kernel(x, p, mix_norm_g, w_in, sgu_w, sgu_b, sgu_norm_g, out_norm_a, out_norm_b, w_out, ffn_norm_g, w_gate, w_up, w_down, ple_norm_g, w_ple_gate, w_ple_proj, final_norm_g, loss_target, m_mix_norm_g, m_w_in, m_sgu_w, m_sgu_b, m_sgu_norm_g, m_out_norm_a, m_out_norm_b, m_w_out, m_ffn_norm_g, m_w_gate, m_w_up, m_w_down, m_ple_norm_g, m_w_ple_gate, m_w_ple_proj, m_final_norm_g, v_mix_norm_g, v_w_in, v_sgu_w, v_sgu_b, v_sgu_norm_g, v_out_norm_a, v_out_norm_b, v_w_out, v_ffn_norm_g, v_w_gate, v_w_up, v_w_down, v_ple_norm_g, v_w_ple_gate, v_w_ple_proj, v_final_norm_g):
    given = dict(locals())
    weights = {n: given[n] for n in _ORDER}
    moments_m = {n: given["m_" + n] for n in _ORDER}
    moments_v = {n: given["v_" + n] for n in _ORDER}

    shards = {n: _own_orientation(n, weights[n]).astype(BF16) for n in _BIG}
    small = {n: (weights[n][0] if n in ("sgu_w", "sgu_b") else weights[n]) for n in _SMALL}

    dx, parts, early_parts, late = _forward_backward(x[0], p[0, 0], loss_target[0], small, shards)

    small_like = {n: weights[n] for n in _SMALL}
    grads, deltas, new_m, new_v = {}, {}, {}, {}
    for n in _BIG:
        outs, got = _adamw(parts[n], _own_orientation(n, weights[n]), _own_orientation(n, moments_m[n]),
                           _own_orientation(n, moments_v[n]), "adamw_" + n, exchange=[("gather", late)] if n == "w_in" else ())
        grads[n], deltas[n], new_m[n], new_v[n] = [_reference_orientation(n, o) for o in outs]
        if n == "w_in":
            (late_parts,) = got
    loss = jnp.sum(late_parts[:, 8, 0])
    small_parts = jnp.concatenate([late_parts[:, :8], early_parts], axis=1)
    (g, d, nm, nv), _ = _adamw(small_parts, _pack_small(small_like), _pack_small({n: moments_m[n] for n in _SMALL}),
                               _pack_small({n: moments_v[n] for n in _SMALL}), "adamw_small")
    for out, packed in ((grads, g), (deltas, d), (new_m, nm), (new_v, nv)):
        out.update(_unpack_small(packed, small_like))

    return (loss, dx[None], *[grads[n] for n in _ORDER], *[deltas[n] for n in _ORDER],
            *[new_m[n] for n in _ORDER], *[new_v[n] for n in _ORDER])
```

```python
import functools

import jax
import jax.numpy as jnp
from jax import lax
from jax.experimental import pallas as pl
from jax.experimental.pallas import tpu as pltpu

F32 = jnp.float32
BF16 = jnp.bfloat16

D_MODEL = 1024
WIDTH_A = 256
WIDTH_B = 768
D_FF = 2816
IN_COLS = 2 * WIDTH_A + 3 * WIDTH_B
PLE_DIM = 256
HEAD_DIM = 64
N_PAIRS = WIDTH_B // 128
CHUNK = 128
N_BACK = 128
DILATIONS = (1, 4, 16)
ROPE_THETA = 10000.0
EPS = 1e-6
N_DEV = 8

ADAM_LR = 0.001
ADAM_B1 = 0.9
ADAM_B2 = 0.999
ADAM_EPS = 1e-08
ADAM_WD = 0.01
ADAM_STEP = 10

V7X_VMEM_LIMIT_BYTES = 56 * 1024 * 1024
ROW_TILE = 512
MESH = pl.DeviceIdType.MESH
NEG = -1e30

_NT = (((1,), (1,)), ((), ()))
_TN = (((0,), (0,)), ((), ()))


def _params(*semantics):
    return pltpu.CompilerParams(dimension_semantics=semantics, vmem_limit_bytes=V7X_VMEM_LIMIT_BYTES)


def _rows(tm, width):
    return pl.BlockSpec((tm, width), lambda i: (i, 0))


def _whole(shape):
    return pl.BlockSpec(shape, lambda *_: (0,) * len(shape))


def _resident(shape):
    return pl.BlockSpec(shape, lambda *_: (0,) * len(shape), pipeline_mode=pl.Buffered(1))


def _gelu(x):
    t = jnp.tanh(0.7978845608028654 * (x + 0.044715 * (x * x * x)))
    return 0.5 * x * (1.0 + t)


def _gelu_grad(x):
    t = jnp.tanh(0.7978845608028654 * (x + 0.044715 * (x * x * x)))
    return 0.5 * (1.0 + t) + 0.5 * x * (1.0 - t * t) * (0.7978845608028654 * (1.0 + 3.0 * 0.044715 * (x * x)))


def _rstd(x):
    return lax.rsqrt(jnp.mean(x * x, axis=-1, keepdims=True) + EPS)


def _norm_bwd(dn, h, g, r=None):
    r = _rstd(h) if r is None else r
    n = h * r
    t = dn * g
    return r * (t - n * jnp.mean(t * n, axis=-1, keepdims=True)), dn * n


def _swap_halves(x, first_half):
    return jnp.where(first_half, pltpu.roll(x, 96, 1), pltpu.roll(x, 32, 1))


def _sub_spec(d, n_cb, tm):
    return pl.BlockSpec((d, n_cb, tm // d, 128), lambda i: (0, 0, i, 0))


def _sub_shape(s_len, d, n_cb, dtype):
    return jax.ShapeDtypeStruct((d, n_cb, s_len // d, 128), dtype)


def _to_sub(stage_ref, cb_src, out_ref, cb_dst, d, tm):
    slab = stage_ref.at[cb_src]
    for r in range(d):
        out_ref[r, cb_dst] = slab[pl.ds(r, tm // d, stride=d), :].astype(out_ref.dtype)


def _from_sub(in_ref, cb_src, stage_ref, cb_dst, d, tm):
    slab = stage_ref.at[cb_dst]
    for r in range(d):
        slab[pl.ds(r, tm // d, stride=d), :] = in_ref[r, cb_src].astype(F32)


def _inproj(x, g, w, cos, sin, exchange=()):
    s_len = x.shape[0]
    tm = ROW_TILE
    n_cb = 3 * N_PAIRS

    def body(x_ref, g_ref, w_ref, cos_ref, sin_ref, ua_ref, hn_ref, *rest):
        sub_refs, stage = rest[:-1], rest[-1]
        xf = x_ref[...]
        hn = (xf * _rstd(xf) * g_ref[...]).astype(BF16)
        hn_ref[...] = hn
        c = cos_ref[...]
        s = sin_ref[...]
        first_half = (lax.broadcasted_iota(jnp.int32, (tm, 128), 1) % HEAD_DIM) < HEAD_DIM // 2
        for col in range(0, IN_COLS, 512):
            width = min(512, IN_COLS - col)
            acc = lax.dot_general(hn, w_ref[col:col + width, :], _NT, preferred_element_type=F32)
            if col < 2 * WIDTH_A:
                ua_ref[:, col:col + width] = acc
                continue
            for part in range(width // 128):
                cb = (col - 2 * WIDTH_A) // 128 + part
                t = acc[:, part * 128:(part + 1) * 128]
                if cb < 2 * N_PAIRS:
                    t = (t * c + _swap_halves(t, first_half) * s) * (0.125 if cb < N_PAIRS else 1.0)
                stage[cb] = t
                sub_refs[0][0, cb] = t.astype(BF16)
        for cb in range(n_cb):
            for d, out_ref in zip(DILATIONS[1:], sub_refs[1:]):
                _to_sub(stage, cb, out_ref, cb, d, tm)

    return _call(
        body, name="inproj", grid=(s_len // tm,),
        in_specs=[_rows(tm, D_MODEL), _whole((1, D_MODEL)), _resident((IN_COLS, D_MODEL)), _rows(tm, 128), _rows(tm, 128)],
        out_specs=[_rows(tm, 2 * WIDTH_A), _rows(tm, D_MODEL)] + [_sub_spec(d, n_cb, tm) for d in DILATIONS],
        out_shape=[jax.ShapeDtypeStruct((s_len, 2 * WIDTH_A), F32), jax.ShapeDtypeStruct((s_len, D_MODEL), BF16)]
        + [_sub_shape(s_len, d, n_cb, BF16) for d in DILATIONS],
        scratch_shapes=[pltpu.VMEM((n_cb, tm, 128), F32)],
        semantics=("parallel",), args=(x, g, w, cos, sin), exchange=exchange)


def _sgu_mix_weights(sw_ref):
    keep = lax.broadcasted_iota(jnp.int32, (CHUNK, CHUNK), 0) >= lax.broadcasted_iota(jnp.int32, (CHUNK, CHUNK), 1)
    return [jnp.where(keep, sw_ref[h], 0.0).astype(BF16) for h in range(4)], keep


def _sgu_core(ua_ref, gs_ref):
    u = ua_ref[:, :WIDTH_A]
    va = ua_ref[:, WIDTH_A:]
    vg = _gelu(va)
    xc = vg - jnp.mean(vg, axis=-1, keepdims=True)
    rstd = lax.rsqrt(jnp.mean(xc * xc, axis=-1, keepdims=True) + EPS)
    xhat = xc * rstd
    return u, va, _gelu(u), xhat, rstd, xhat * gs_ref[...]


def _sgu_mix(wm, vnb, half, head_a):
    va, vb = _both_heads(vnb[:, half * 128:(half + 1) * 128], head_a)
    return (jnp.dot(wm[2 * half], va, preferred_element_type=F32)
            + jnp.dot(wm[2 * half + 1], vb, preferred_element_type=F32))


def _sgu_fwd(ua, sw, b2, gs, ga):
    s_len = ua.shape[0]
    tm = ROW_TILE

    def body(ua_ref, sw_ref, b2_ref, gs_ref, ga_ref, out_ref):
        _, _, ug, _, _, vn = _sgu_core(ua_ref, gs_ref)
        wm, _ = _sgu_mix_weights(sw_ref)
        head_a = lax.broadcasted_iota(jnp.int32, (CHUNK, 128), 1) < HEAD_DIM
        for c in range(tm // CHUNK):
            rows = slice(c * CHUNK, (c + 1) * CHUNK)
            vnb = vn[rows].astype(BF16)
            mixed = b2_ref[...] + jnp.concatenate([_sgu_mix(wm, vnb, half, head_a) for half in range(2)], axis=1)
            ya = ug[rows] * mixed
            out_ref[rows, :] = (ya * _rstd(ya) * ga_ref[...]).astype(BF16)

    return pl.pallas_call(
        body, name="sgu_fwd", grid=(s_len // tm,),
        in_specs=[_rows(tm, 2 * WIDTH_A), _whole((4, CHUNK, CHUNK)), _whole((CHUNK, WIDTH_A)), _whole((1, WIDTH_A)), _whole((1, WIDTH_A))],
        out_specs=_rows(tm, WIDTH_A),
        out_shape=jax.ShapeDtypeStruct((s_len, WIDTH_A), BF16),
        compiler_params=_params("parallel"),
    )(ua, sw, b2, gs, ga)


STEP_POSITIONS = 2 * ROW_TILE


def _attn_geometry(sd):
    tile = min(STEP_POSITIONS, sd)
    return tile, tile // CHUNK, sd // tile, STEP_POSITIONS // tile


PAIRS_PER_STEP = 6


def _attn_spec(cb0, rows, row_index, res):
    return pl.BlockSpec((res, PAIRS_PER_STEP, rows, 128), lambda r, g, n: (r, cb0 // PAIRS_PER_STEP + g, row_index(n), 0))


assert PAIRS_PER_STEP == N_PAIRS
assert DILATIONS[0] == 1


def _stats_spec(rows, row_index, res):
    return pl.BlockSpec((res, None, rows, 128), lambda r, g, n: (r, 0, row_index(n), 0))


def _stats_shape(sd, d):
    return jax.ShapeDtypeStruct((d, 1, sd, 128), F32)


def _both_heads(x, head_a):
    zero = jnp.zeros_like(x)
    return [jnp.where(head_a, x, zero), jnp.where(head_a, zero, x)]


def _attn_fwd(qkv, d, exchange=()):
    sd = qkv.shape[2]
    tile, nb, n_tiles, res = _attn_geometry(sd)

    def prev(n):
        return jnp.maximum(n * nb - 1, 0)

    def body(q_ref, k_ref, kp_ref, v_ref, vp_ref, o_ref, l_ref):
        for rr in range(res):
            for hp in range(PAIRS_PER_STEP):
                one_pair(hp, q_ref.at[rr, hp], k_ref.at[rr, hp], kp_ref.at[rr, hp], v_ref.at[rr, hp], vp_ref.at[rr, hp],
                         o_ref.at[rr, hp], l_ref.at[rr])

    def one_pair(hp, q_ref, k_ref, kp_ref, v_ref, vp_ref, o_ref, l_ref):
        n = pl.program_id(2)
        lane = lax.broadcasted_iota(jnp.int32, (CHUNK, 128), 1)
        head_a = lane < HEAD_DIM
        qi = lax.broadcasted_iota(jnp.int32, (2 * CHUNK, 2 * CHUNK), 0) % CHUNK
        kc = lax.broadcasted_iota(jnp.int32, (2 * CHUNK, 2 * CHUNK), 1)
        band = (kc >= qi) & (kc <= qi + N_BACK)
        for j in range(nb):
            rows = slice(j * CHUNK, (j + 1) * CHUNK)
            if j == 0:
                kcat = jnp.concatenate([kp_ref[...], k_ref[rows, :]], axis=0)
                vcat = jnp.concatenate([vp_ref[...], v_ref[rows, :]], axis=0)
                valid = band & jnp.logical_or(n > 0, kc >= CHUNK)
            else:
                kcat = k_ref[(j - 1) * CHUNK:(j + 1) * CHUNK, :]
                vcat = v_ref[(j - 1) * CHUNK:(j + 1) * CHUNK, :]
                valid = band
            q2 = jnp.concatenate(_both_heads(q_ref[rows, :], head_a), axis=0)
            s = lax.dot_general(q2, kcat, _NT, preferred_element_type=F32)
            s = jnp.where(valid, s, NEG)
            m = jnp.max(s, axis=-1, keepdims=True)
            p = jnp.exp(s - m)
            l = jnp.sum(p, axis=-1, keepdims=True)
            o2 = jnp.dot(p.astype(BF16), vcat, preferred_element_type=F32) / l
            lse2 = m + jnp.log(l)
            o_ref[rows, :] = jnp.where(head_a, o2[:CHUNK], o2[CHUNK:]).astype(BF16)
            others = l_ref[rows, :] if hp > 0 else jnp.zeros((CHUNK, 128), F32)
            l_ref[rows, :] = jnp.where(lane == 2 * hp, lse2[:CHUNK], jnp.where(lane == 2 * hp + 1, lse2[CHUNK:], others))

    same = lambda n: n
    return _call(
        body, name=f"attn_fwd_d{d}", grid=(d // res, N_PAIRS // PAIRS_PER_STEP, n_tiles),
        in_specs=[_attn_spec(0, tile, same, res), _attn_spec(N_PAIRS, tile, same, res), _attn_spec(N_PAIRS, CHUNK, prev, res),
                  _attn_spec(2 * N_PAIRS, tile, same, res), _attn_spec(2 * N_PAIRS, CHUNK, prev, res)],
        out_specs=[_attn_spec(0, tile, same, res), _stats_spec(tile, same, res)],
        out_shape=[jax.ShapeDtypeStruct((d, N_PAIRS, sd, 128), BF16), _stats_shape(sd, d)],
        semantics=("parallel", "parallel", "parallel"), args=(qkv, qkv, qkv, qkv, qkv), exchange=exchange)


def _combine(outs, lses, ya_n, gb, head_spread):
    s_len = ya_n.shape[0]
    tm = ROW_TILE
    n_br = len(DILATIONS)

    def body(*refs):
        o_refs, l_refs = refs[:n_br], refs[n_br:2 * n_br]
        ya_ref, gb_ref, spread_ref, y_ref, yb_ref = refs[2 * n_br:2 * n_br + 5]
        lse_refs = refs[2 * n_br + 5:3 * n_br + 5]
        o_nat, l_nat, lse_nat, w_wide = refs[3 * n_br + 5:]
        for i, d in enumerate(DILATIONS):
            _from_sub(l_refs[i], 0, l_nat, i, d, tm)
        ls = [l_nat[i] for i in range(n_br)]
        top = jnp.maximum(jnp.maximum(ls[0], ls[1]), ls[2])
        ws = [jnp.exp(l - top) for l in ls]
        den = ws[0] + ws[1] + ws[2]
        inv = 1.0 / den
        for i in range(n_br):
            w = ws[i] * inv
            hi = w.astype(BF16)
            lo = (w - hi.astype(F32)).astype(BF16)
            w_wide[i] = (jnp.dot(hi, spread_ref[...], preferred_element_type=F32)
                         + jnp.dot(lo, spread_ref[...], preferred_element_type=F32))
        lse_nat[0] = top + jnp.log(den)
        for d, lse_ref in zip(DILATIONS, lse_refs):
            _to_sub(lse_nat, 0, lse_ref, 0, d, tm)
        sumsq = jnp.zeros((tm, 1), F32)
        for cb in range(N_PAIRS):
            cols = slice(cb * 128, (cb + 1) * 128)
            yb = w_wide[0, :, cols] * o_refs[0][0, cb].astype(F32)
            for i, d in enumerate(DILATIONS[1:], start=1):
                _from_sub(o_refs[i], cb, o_nat, i, d, tm)
                yb = yb + w_wide[i, :, cols] * o_nat[i]
            yb_ref[:, cb * 128:(cb + 1) * 128] = yb
            sumsq = sumsq + jnp.sum(yb * yb, axis=-1, keepdims=True)
        r = lax.rsqrt(sumsq / WIDTH_B + EPS)
        y_ref[:, :WIDTH_A] = ya_ref[...]
        y_ref[:, WIDTH_A:] = (yb_ref[...] * r * gb_ref[...]).astype(BF16)

    stats = [_sub_spec(d, 1, tm) for d in DILATIONS]
    return _call(
        body, name="attn_combine", grid=(s_len // tm,),
        in_specs=[_sub_spec(d, N_PAIRS, tm) for d in DILATIONS] + stats
        + [_rows(tm, WIDTH_A), _whole((1, WIDTH_B)), _whole((128, WIDTH_B))],
        out_specs=[_rows(tm, D_MODEL), _rows(tm, WIDTH_B)] + stats,
        out_shape=[jax.ShapeDtypeStruct((s_len, D_MODEL), BF16), jax.ShapeDtypeStruct((s_len, WIDTH_B), F32)]
        + [_sub_shape(s_len, d, 1, F32) for d in DILATIONS],
        scratch_shapes=[pltpu.VMEM((n_br, tm, 128), F32), pltpu.VMEM((n_br, tm, 128), F32), pltpu.VMEM((1, tm, 128), F32),
                        pltpu.VMEM((n_br, tm, WIDTH_B), F32)],
        semantics=("parallel",), args=(*outs, *lses, ya_n, gb, head_spread))[0]


def _ffn_up(y, wout, x, g, wg, wu, exchange=()):
    s_len = x.shape[0]
    tm = ROW_TILE

    def body(y_ref, wo_ref, x_ref, g_ref, wg_ref, wu_ref, h_ref, a_ref, b_ref, act_ref, hn_ref):
        hf = x_ref[...] + jnp.dot(y_ref[...], wo_ref[...], preferred_element_type=F32)
        h_ref[...] = hf
        hn = (hf * _rstd(hf) * g_ref[...]).astype(BF16)
        hn_ref[...] = hn
        for col in range(0, D_FF, 256):
            cols = slice(col, col + 256)
            a = lax.dot_general(hn, wg_ref[cols, :], _NT, preferred_element_type=F32)
            b = lax.dot_general(hn, wu_ref[cols, :], _NT, preferred_element_type=F32)
            a_ref[:, cols] = a.astype(BF16)
            b_ref[:, cols] = b.astype(BF16)
            act_ref[:, cols] = (a * jax.nn.sigmoid(a) * b).astype(BF16)

    wide = jax.ShapeDtypeStruct((s_len, D_FF), BF16)
    return _call(
        body, name="ffn_up", grid=(s_len // tm,),
        in_specs=[_rows(tm, D_MODEL), _resident((D_MODEL, D_MODEL)), _rows(tm, D_MODEL), _whole((1, D_MODEL)),
                  _resident((D_FF, D_MODEL)), _resident((D_FF, D_MODEL))],
        out_specs=[_rows(tm, D_MODEL), _rows(tm, D_FF), _rows(tm, D_FF), _rows(tm, D_FF), _rows(tm, D_MODEL)],
        out_shape=[jax.ShapeDtypeStruct((s_len, D_MODEL), F32), wide, wide, wide, jax.ShapeDtypeStruct((s_len, D_MODEL), BF16)],
        semantics=("parallel",), args=(y, wout, x, g, wg, wu), exchange=exchange)


def _ffn_down_ple(act, wd, h1, g, wpg, p, wpp):
    s_len = h1.shape[0]
    tm = ROW_TILE

    def body(act_ref, wd_ref, h1_ref, g_ref, wpg_ref, p_ref, wpp_ref, h2_ref, h3_ref, gate_ref, pp_ref, hn_ref):
        hf = h1_ref[...] + jnp.dot(act_ref[...], wd_ref[...], preferred_element_type=F32)
        h2_ref[...] = hf
        hn = (hf * _rstd(hf) * g_ref[...]).astype(BF16)
        hn_ref[...] = hn
        gate = jax.nn.sigmoid(jnp.dot(hn, wpg_ref[...], preferred_element_type=F32))
        pp = lax.dot_general(p_ref[...].astype(BF16), wpp_ref[...], _NT, preferred_element_type=F32)
        h3_ref[...] = hf + gate * pp
        gate_ref[...] = gate.astype(BF16)
        pp_ref[...] = pp.astype(BF16)

    full = jax.ShapeDtypeStruct((s_len, D_MODEL), F32)
    half = jax.ShapeDtypeStruct((s_len, D_MODEL), BF16)
    return pl.pallas_call(
        body, name="ffn_down_ple", grid=(s_len // tm,),
        in_specs=[_rows(tm, D_FF), _resident((D_FF, D_MODEL)), _rows(tm, D_MODEL), _whole((1, D_MODEL)),
                  _resident((D_MODEL, D_MODEL)), _rows(tm, PLE_DIM), _resident((D_MODEL, PLE_DIM))],
        out_specs=[_rows(tm, D_MODEL)] * 5,
        out_shape=[full, full, half, half, half],
        compiler_params=_params("parallel"),
    )(act, wd, h1, g, wpg, p, wpp)


def _loss_ple_bwd(h3, target, gf, gate, pp, h2, g_ple, wpg, hn3, p):
    s_len = h3.shape[0]
    tm = ROW_TILE
    n_steps = s_len // tm

    def body(h_ref, t_ref, g_ref, gate_ref, pp_ref, h2_ref, gp_ref, w_ref, hn_ref, p_ref,
             dh2_ref, loss_ref, dg_ref, dgp_ref, dwg_ref, dwp_ref, acc_g, acc_p):
        step = pl.program_id(0)

        @pl.when(step == 0)
        def _():
            loss_ref[...] = jnp.zeros_like(loss_ref)
            dg_ref[...] = jnp.zeros_like(dg_ref)
            dgp_ref[...] = jnp.zeros_like(dgp_ref)
            acc_g[...] = jnp.zeros_like(acc_g)
            acc_p[...] = jnp.zeros_like(acc_p)

        hf = h_ref[...]
        gfv = g_ref[...]
        r = _rstd(hf)
        err = hf * r * gfv - t_ref[...]
        loss_ref[...] += 0.5 * jnp.sum(jnp.sum(err * err, axis=-1, keepdims=True), axis=0, keepdims=True) / D_MODEL
        dh, dg_rows = _norm_bwd(err / D_MODEL, hf, gfv, r)
        dg_ref[...] += jnp.sum(dg_rows, axis=0, keepdims=True)
        gate = gate_ref[...].astype(F32)
        dz = (dh * pp_ref[...].astype(F32) * gate * (1.0 - gate)).astype(BF16)
        dpp = (dh * gate).astype(BF16)
        dn = lax.dot_general(dz, w_ref[...], _NT, preferred_element_type=F32)
        dh2, dgp_rows = _norm_bwd(dn, h2_ref[...], gp_ref[...])
        dh2 = dh + dh2
        dh2_ref[...] = dh2
        dgp_ref[...] += jnp.sum(dgp_rows, axis=0, keepdims=True)
        acc_g[...] += lax.dot_general(hn_ref[...], dz, _TN, preferred_element_type=F32)
        acc_p[...] += lax.dot_general(dpp, p_ref[...].astype(BF16), _TN, preferred_element_type=F32)

        @pl.when(step == n_steps - 1)
        def _():
            dwg_ref[...] = acc_g[...].astype(BF16)
            dwp_ref[...] = acc_p[...].astype(BF16)

    gain = jax.ShapeDtypeStruct((1, D_MODEL), F32)
    return pl.pallas_call(
        body, name="loss_ple_bwd", grid=(n_steps,),
        in_specs=[_rows(tm, D_MODEL), _rows(tm, D_MODEL), _whole((1, D_MODEL)), _rows(tm, D_MODEL), _rows(tm, D_MODEL),
                  _rows(tm, D_MODEL), _whole((1, D_MODEL)), _resident((D_MODEL, D_MODEL)), _rows(tm, D_MODEL),
                  _rows(tm, PLE_DIM)],
        out_specs=[_rows(tm, D_MODEL), _whole((1, 128)), _whole((1, D_MODEL)), _whole((1, D_MODEL)),
                   _whole((D_MODEL, D_MODEL)), _whole((D_MODEL, PLE_DIM))],
        out_shape=[jax.ShapeDtypeStruct((s_len, D_MODEL), F32), jax.ShapeDtypeStruct((1, 128), F32), gain, gain,
                   jax.ShapeDtypeStruct((D_MODEL, D_MODEL), BF16), jax.ShapeDtypeStruct((D_MODEL, PLE_DIM), BF16)],
        scratch_shapes=[pltpu.VMEM((D_MODEL, D_MODEL), F32), pltpu.VMEM((D_MODEL, PLE_DIM), F32)],
        compiler_params=_params("arbitrary"),
    )(h3, target, gf, gate, pp, h2, g_ple, wpg, hn3, p)


def _mm_norm_bwd(parts, h, g, dres, name, exchange=(), dw_lhs=None):
    s_len = h.shape[0]
    tm = ROW_TILE
    n_parts = len(parts)
    n_steps = s_len // tm
    has_dw = dw_lhs is not None

    def body(*refs):
        a_refs = refs[0:2 * n_parts:2]
        w_refs = refs[1:2 * n_parts:2]
        h_ref, g_ref, r_ref = refs[2 * n_parts:2 * n_parts + 3]
        rest = refs[2 * n_parts + 3:]
        step = pl.program_id(0)
        if has_dw:
            lhs_ref, o_ref, dg_ref, dw_ref, acc_ref = rest
        else:
            o_ref, dg_ref = rest

        @pl.when(step == 0)
        def _():
            dg_ref[...] = jnp.zeros_like(dg_ref)
            if has_dw:
                acc_ref[...] = jnp.zeros_like(acc_ref)

        dn = jnp.dot(a_refs[0][...], w_refs[0][...], preferred_element_type=F32)
        for a_ref, w_ref in zip(a_refs[1:], w_refs[1:]):
            dn = dn + jnp.dot(a_ref[...], w_ref[...], preferred_element_type=F32)
        dh, dg_rows = _norm_bwd(dn, h_ref[...], g_ref[...])
        out = r_ref[...] + dh
        o_ref[...] = out
        dg_ref[...] += jnp.sum(dg_rows, axis=0, keepdims=True)
        if has_dw:
            acc_ref[...] += lax.dot_general(lhs_ref[...], out.astype(BF16), _TN, preferred_element_type=F32)

            @pl.when(step == n_steps - 1)
            def _():
                dw_ref[...] = acc_ref[...].astype(BF16)

    in_specs, args = [], []
    for a, w in parts:
        in_specs += [_rows(tm, a.shape[1]), _resident(w.shape)]
        args += [a, w]
    in_specs += [_rows(tm, D_MODEL), _whole((1, D_MODEL)), _rows(tm, D_MODEL)]
    args += [h, g, dres]
    out_specs = [_rows(tm, D_MODEL), _whole((1, D_MODEL))]
    out_shape = [jax.ShapeDtypeStruct((s_len, D_MODEL), F32), jax.ShapeDtypeStruct((1, D_MODEL), F32)]
    scratch = []
    if has_dw:
        m = dw_lhs.shape[1]
        in_specs.append(_rows(tm, m))
        args.append(dw_lhs)
        out_specs.append(_whole((m, D_MODEL)))
        out_shape.append(jax.ShapeDtypeStruct((m, D_MODEL), BF16))
        scratch.append(pltpu.VMEM((m, D_MODEL), F32))
    return _call(
        body, name=name, grid=(n_steps,), in_specs=in_specs, out_specs=out_specs, out_shape=out_shape,
        scratch_shapes=scratch, semantics=("arbitrary",), args=tuple(args), exchange=exchange)


def _ffn_down_bwd(dh, wdt, a, b, exchange=()):
    s_len = dh.shape[0]
    tm = ROW_TILE
    n_steps = s_len // tm
    depth = 3

    def body(dh_ref, w_ref, a_hbm, b_hbm, da_ref, db_ref, a_buf, b_buf, sems):
        step = pl.program_id(0)

        def copies(tile, slot):
            rows = pl.ds(pl.multiple_of(tile * tm, tm), tm)
            return (pltpu.make_async_copy(a_hbm.at[rows, :], a_buf.at[slot], sems.at[0, slot]),
                    pltpu.make_async_copy(b_hbm.at[rows, :], b_buf.at[slot], sems.at[1, slot]))

        @pl.when(step == 0)
        def _():
            for t in range(min(depth, n_steps)):
                for cp in copies(t, t):
                    cp.start()

        slot = step % depth
        for cp in copies(step, slot):
            cp.wait()
        a_ref = a_buf.at[slot]
        b_ref = b_buf.at[slot]
        dhb = dh_ref[...].astype(BF16)
        for col in range(0, D_FF, 512):
            cols = slice(col, min(col + 512, D_FF))
            dact = lax.dot_general(dhb, w_ref[cols, :], _NT, preferred_element_type=F32)
            av = a_ref[:, cols].astype(F32)
            bv = b_ref[:, cols].astype(F32)
            sig = jax.nn.sigmoid(av)
            t = dact * sig
            silu = av * sig
            da_ref[:, cols] = (t * bv * (1.0 + av - silu)).astype(BF16)
            db_ref[:, cols] = (dact * silu).astype(BF16)

        @pl.when(step + depth < n_steps)
        def _():
            for cp in copies(step + depth, slot):
                cp.start()

    wide = jax.ShapeDtypeStruct((s_len, D_FF), BF16)
    any_spec = pl.BlockSpec(memory_space=pl.ANY)
    return _call(
        body, name="ffn_down_bwd", grid=(n_steps,),
        in_specs=[_rows(tm, D_MODEL), _resident((D_FF, D_MODEL)), any_spec, any_spec],
        out_specs=[_rows(tm, D_FF), _rows(tm, D_FF)],
        out_shape=[wide, wide],
        scratch_shapes=[pltpu.VMEM((depth, tm, D_FF), BF16), pltpu.VMEM((depth, tm, D_FF), BF16),
                        pltpu.SemaphoreType.DMA((2, depth))],
        semantics=("arbitrary",), args=(dh, wdt, a, b), exchange=exchange)


def _outproj_bwd(dh1, woutt, yb, gb, head_sum):
    s_len = dh1.shape[0]
    tm = ROW_TILE
    n_br = len(DILATIONS)

    def body(dh_ref, w_ref, yb_ref, gb_ref, e_ref, dya_ref, dgb_ref, *rest):
        do_refs, dd_refs = rest[:n_br], rest[n_br:2 * n_br]
        do_nat, dd_nat = rest[2 * n_br:]

        @pl.when(pl.program_id(0) == 0)
        def _():
            dgb_ref[...] = jnp.zeros_like(dgb_ref)

        dhb = dh_ref[...].astype(BF16)
        dya_ref[...] = lax.dot_general(dhb, w_ref[:WIDTH_A, :], _NT, preferred_element_type=F32)
        dyn = lax.dot_general(dhb, w_ref[WIDTH_A:, :], _NT, preferred_element_type=F32)
        ybv = yb_ref[...]
        dyb, dg_rows = _norm_bwd(dyn, ybv, gb_ref[...])
        dgb_ref[...] += jnp.sum(dg_rows, axis=0, keepdims=True)
        prod = dyb * ybv
        hi = prod.astype(BF16)
        lo = (prod - hi.astype(F32)).astype(BF16)
        dd_nat[0] = (jnp.dot(hi, e_ref[...], preferred_element_type=F32)
                     + jnp.dot(lo, e_ref[...], preferred_element_type=F32))
        for i, d in enumerate(DILATIONS):
            _to_sub(dd_nat, 0, dd_refs[i], 0, d, tm)
        for cb in range(N_PAIRS):
            piece = dyb[:, cb * 128:(cb + 1) * 128]
            do_nat[cb] = piece
            do_refs[0][0, cb] = piece.astype(BF16)
            for i, d in enumerate(DILATIONS[1:], start=1):
                _to_sub(do_nat, cb, do_refs[i], cb, d, tm)

    return _call(
        body, name="outproj_bwd", grid=(s_len // tm,),
        in_specs=[_rows(tm, D_MODEL), _resident((D_MODEL, D_MODEL)), _rows(tm, WIDTH_B), _whole((1, WIDTH_B)), _whole((WIDTH_B, 128))],
        out_specs=[_rows(tm, WIDTH_A), _whole((1, WIDTH_B))] + [_sub_spec(d, N_PAIRS, tm) for d in DILATIONS]
        + [_sub_spec(d, 1, tm) for d in DILATIONS],
        out_shape=[jax.ShapeDtypeStruct((s_len, WIDTH_A), F32), jax.ShapeDtypeStruct((1, WIDTH_B), F32)]
        + [_sub_shape(s_len, d, N_PAIRS, BF16) for d in DILATIONS] + [_sub_shape(s_len, d, 1, F32) for d in DILATIONS],
        scratch_shapes=[pltpu.VMEM((N_PAIRS, tm, 128), F32), pltpu.VMEM((1, tm, 128), F32)],
        semantics=("arbitrary",), args=(dh1, woutt, yb, gb, head_sum))[0]


def _attn_bwd(qkv, do, lse, dd, d, exchange=()):
    sd = qkv.shape[2]
    tile, nb, n_tiles, res = _attn_geometry(sd)
    last_block = sd // CHUNK - 1

    def nxt(n):
        return jnp.minimum((n + 1) * nb, last_block)

    def block(ref, next_ref, j):
        return ref[j * CHUNK:(j + 1) * CHUNK, :] if j < nb else next_ref[...]

    def body(q_ref, qn_ref, k_ref, v_ref, do_ref, don_ref, l_ref, ln_ref, dd_ref, ddn_ref,
             dq_ref, dk_ref, dv_ref, carry_ref):
        for rr in range(res):
            l_t = [block(l_ref.at[rr], ln_ref.at[rr], j).T for j in range(nb + 1)]
            dd_t = [block(dd_ref.at[rr], ddn_ref.at[rr], j).T for j in range(nb + 1)]
            for hp in range(PAIRS_PER_STEP):
                l_rows = [jnp.concatenate([t[2 * hp:2 * hp + 1, :], t[2 * hp + 1:2 * hp + 2, :]], axis=1) for t in l_t]
                dd_rows = [jnp.concatenate([t[2 * hp:2 * hp + 1, :], t[2 * hp + 1:2 * hp + 2, :]], axis=1) for t in dd_t]
                one_pair(q_ref.at[rr, hp], qn_ref.at[rr, hp], k_ref.at[rr, hp], v_ref.at[rr, hp], do_ref.at[rr, hp],
                         don_ref.at[rr, hp], l_rows, dd_rows, dq_ref.at[rr, hp], dk_ref.at[rr, hp], dv_ref.at[rr, hp],
                         carry_ref.at[rr, hp])

    def one_pair(q_ref, qn_ref, k_ref, v_ref, do_ref, don_ref, l_rows, dd_rows, dq_ref, dk_ref, dv_ref, carry_ref):
        n = pl.program_id(2)

        @pl.when(n == 0)
        def _():
            carry_ref[...] = jnp.zeros_like(carry_ref)

        head_a = lax.broadcasted_iota(jnp.int32, (CHUNK, 128), 1) < HEAD_DIM
        col = lax.broadcasted_iota(jnp.int32, (CHUNK, 4 * CHUNK), 1)
        qi = col % CHUNK
        ki = lax.broadcasted_iota(jnp.int32, (CHUNK, 4 * CHUNK), 0)
        is_after = col >= 2 * CHUNK
        mask = (is_after & (ki >= qi)) | (jnp.logical_not(is_after) & (qi >= ki))
        mask_last = mask & jnp.logical_or(jnp.logical_not(is_after), n < n_tiles - 1)
        dq_acc = [carry_ref[...]] + [jnp.zeros((CHUNK, 128), F32) for _ in range(nb)]

        q_st = [jnp.concatenate(_both_heads(block(q_ref, qn_ref, j), head_a), axis=0) for j in range(nb + 1)]
        do_st = [jnp.concatenate(_both_heads(block(do_ref, don_ref, j), head_a), axis=0) for j in range(nb + 1)]

        for j in range(nb):
            rows = slice(j * CHUNK, (j + 1) * CHUNK)
            kj = k_ref[rows, :]
            vj = v_ref[rows, :]
            msk = mask if j + 1 < nb else mask_last
            qs = jnp.concatenate([q_st[j], q_st[j + 1]], axis=0)
            dos = jnp.concatenate([do_st[j], do_st[j + 1]], axis=0)
            ls = jnp.concatenate([l_rows[j], l_rows[j + 1]], axis=1)
            dds = jnp.concatenate([dd_rows[j], dd_rows[j + 1]], axis=1)
            st = lax.dot_general(kj, qs, _NT, preferred_element_type=F32)
            pt = jnp.exp(jnp.where(msk, st - ls, NEG))
            dpt = lax.dot_general(vj, dos, _NT, preferred_element_type=F32)
            dst = (pt * (dpt - dds)).astype(BF16)
            dv_ref[rows, :] = jnp.dot(pt.astype(BF16), dos, preferred_element_type=F32).astype(BF16)
            dk_ref[rows, :] = jnp.dot(dst, qs, preferred_element_type=F32).astype(BF16)
            dqs = lax.dot_general(dst, kj, _TN, preferred_element_type=F32)
            dq_acc[j] = dq_acc[j] + jnp.where(head_a, dqs[:CHUNK], dqs[CHUNK:2 * CHUNK])
            dq_acc[j + 1] = dq_acc[j + 1] + jnp.where(head_a, dqs[2 * CHUNK:3 * CHUNK], dqs[3 * CHUNK:])
        for j in range(nb):
            dq_ref[j * CHUNK:(j + 1) * CHUNK, :] = dq_acc[j].astype(BF16)
        carry_ref[...] = dq_acc[nb]

    same = lambda n: n
    grad = jax.ShapeDtypeStruct((d, N_PAIRS, sd, 128), BF16)
    return _call(
        body, name=f"attn_bwd_d{d}", grid=(d // res, N_PAIRS // PAIRS_PER_STEP, n_tiles),
        in_specs=[_attn_spec(0, tile, same, res), _attn_spec(0, CHUNK, nxt, res), _attn_spec(N_PAIRS, tile, same, res),
                  _attn_spec(2 * N_PAIRS, tile, same, res), _attn_spec(0, tile, same, res), _attn_spec(0, CHUNK, nxt, res),
                  _stats_spec(tile, same, res), _stats_spec(CHUNK, nxt, res), _stats_spec(tile, same, res),
                  _stats_spec(CHUNK, nxt, res)],
        out_specs=[_attn_spec(0, tile, same, res)] * 3,
        out_shape=[grad, grad, grad],
        scratch_shapes=[pltpu.VMEM((res, PAIRS_PER_STEP, CHUNK, 128), F32)],
        semantics=("parallel", "parallel", "arbitrary"), args=(qkv, qkv, qkv, qkv, do, do, lse, lse, dd, dd), exchange=exchange)


def _sgu_bwd(ua, sw, b2, gs, ga, dya_n):
    s_len = ua.shape[0]
    tm = ROW_TILE

    def body(ua_ref, sw_ref, b2_ref, gs_ref, ga_ref, dy_ref, dua_ref, dsw_ref, db2_ref, dgs_ref, dga_ref):
        @pl.when(pl.program_id(0) == 0)
        def _():
            dsw_ref[...] = jnp.zeros_like(dsw_ref)
            db2_ref[...] = jnp.zeros_like(db2_ref)
            dgs_ref[...] = jnp.zeros_like(dgs_ref)
            dga_ref[...] = jnp.zeros_like(dga_ref)

        u, va, ug, xhat, rstd, vn = _sgu_core(ua_ref, gs_ref)
        wm, keep = _sgu_mix_weights(sw_ref)
        head_a = lax.broadcasted_iota(jnp.int32, (CHUNK, 128), 1) < HEAD_DIM
        gav = ga_ref[...]
        gsv = gs_ref[...]
        dga = jnp.zeros((1, WIDTH_A), F32)
        dgs = jnp.zeros((1, WIDTH_A), F32)
        db2 = jnp.zeros((CHUNK, WIDTH_A), F32)
        dsw = [jnp.zeros((CHUNK, CHUNK), F32) for _ in range(4)]
        for c in range(tm // CHUNK):
            rows = slice(c * CHUNK, (c + 1) * CHUNK)
            vnb = vn[rows].astype(BF16)
            mixed = b2_ref[...] + jnp.concatenate([_sgu_mix(wm, vnb, half, head_a) for half in range(2)], axis=1)
            ugc = ug[rows]
            dya, dga_rows = _norm_bwd(dy_ref[rows, :], ugc * mixed, gav)
            dga = dga + jnp.sum(dga_rows, axis=0, keepdims=True)
            dmixed = dya * ugc
            db2 = db2 + dmixed
            dmb = dmixed.astype(BF16)
            dvn_halves = []
            for half in range(2):
                lanes = slice(half * 128, (half + 1) * 128)
                dm_heads = _both_heads(dmb[:, lanes], head_a)
                dvn_half = jnp.zeros((CHUNK, 128), F32)
                for k, dmh in enumerate(dm_heads):
                    h = 2 * half + k
                    dsw[h] = dsw[h] + lax.dot_general(dmh, vnb[:, lanes], _NT, preferred_element_type=F32)
                    dvn_half = dvn_half + lax.dot_general(wm[h], dmh, _TN, preferred_element_type=F32)
                dvn_halves.append(dvn_half)
            dvn = jnp.concatenate(dvn_halves, axis=1)
            xh = xhat[rows]
            dgs = dgs + jnp.sum(dvn * xh, axis=0, keepdims=True)
            dxh = dvn * gsv
            dvg = rstd[rows] * (dxh - jnp.mean(dxh, axis=-1, keepdims=True) - xh * jnp.mean(dxh * xh, axis=-1, keepdims=True))
            dua_ref[rows, :WIDTH_A] = (dya * mixed * _gelu_grad(u[rows])).astype(BF16)
            dua_ref[rows, WIDTH_A:] = (dvg * _gelu_grad(va[rows])).astype(BF16)
        for h in range(4):
            dsw_ref[h] += jnp.where(keep, dsw[h], 0.0)
        db2_ref[...] += db2
        dgs_ref[...] += dgs
        dga_ref[...] += dga

    return pl.pallas_call(
        body, name="sgu_bwd", grid=(s_len // tm,),
        in_specs=[_rows(tm, 2 * WIDTH_A), _whole((4, CHUNK, CHUNK)), _whole((CHUNK, WIDTH_A)), _whole((1, WIDTH_A)),
                  _whole((1, WIDTH_A)), _rows(tm, WIDTH_A)],
        out_specs=[_rows(tm, 2 * WIDTH_A), _whole((4, CHUNK, CHUNK)), _whole((CHUNK, WIDTH_A)), _whole((1, WIDTH_A)), _whole((1, WIDTH_A))],
        out_shape=[jax.ShapeDtypeStruct((s_len, 2 * WIDTH_A), BF16), jax.ShapeDtypeStruct((4, CHUNK, CHUNK), F32),
                   jax.ShapeDtypeStruct((CHUNK, WIDTH_A), F32), jax.ShapeDtypeStruct((1, WIDTH_A), F32),
                   jax.ShapeDtypeStruct((1, WIDTH_A), F32)],
        compiler_params=_params("arbitrary"),
    )(ua, sw, b2, gs, ga, dya_n)


def _dproj(dua, dqs, dks, dvs, cos, sin, hn1, exchange=()):
    s_len = dua.shape[0]
    tm = ROW_TILE
    n_br = len(DILATIONS)
    n_steps = s_len // tm

    def body(dua_ref, *rest):
        groups = [rest[g * n_br:(g + 1) * n_br] for g in range(3)]
        cos_ref, sin_ref, hn_ref, out_ref, dw_ref, acc, dw_acc = rest[3 * n_br:]
        step = pl.program_id(0)

        @pl.when(step == 0)
        def _():
            dw_acc[...] = jnp.zeros_like(dw_acc)

        out_ref[:, :2 * WIDTH_A] = dua_ref[...]
        c = cos_ref[...]
        s = sin_ref[...]
        first_half = (lax.broadcasted_iota(jnp.int32, (tm, 128), 1) % HEAD_DIM) < HEAD_DIM // 2
        for g, refs in enumerate(groups):
            for cb in range(N_PAIRS):
                t = refs[0][0, cb].astype(F32)
                for i, d in enumerate(DILATIONS[1:]):
                    _from_sub(refs[i + 1], cb, acc, i, d, tm)
                    t = t + acc[i]
                if g < 2:
                    t = (t * c - _swap_halves(t, first_half) * s) * (0.125 if g == 0 else 1.0)
                col = 2 * WIDTH_A + g * WIDTH_B + cb * 128
                out_ref[:, col:col + 128] = t.astype(BF16)
        hn = hn_ref[...]
        for j in range(IN_COLS // 256):
            cols = slice(j * 256, (j + 1) * 256)
            dw_acc[cols, :] += lax.dot_general(out_ref[:, cols], hn, _TN, preferred_element_type=F32)

        @pl.when(step == n_steps - 1)
        def _():
            dw_ref[...] = dw_acc[...].astype(BF16)

    subs = [_sub_spec(d, N_PAIRS, tm) for d in DILATIONS]
    (dproj, dw), received = _call(
        body, name="dproj_dw_in", grid=(n_steps,),
        in_specs=[_rows(tm, 2 * WIDTH_A)] + subs * 3 + [_rows(tm, 128), _rows(tm, 128), _rows(tm, D_MODEL)],
        out_specs=[_rows(tm, IN_COLS), _whole((IN_COLS, D_MODEL))],
        out_shape=[jax.ShapeDtypeStruct((s_len, IN_COLS), BF16), jax.ShapeDtypeStruct((IN_COLS, D_MODEL), BF16)],
        scratch_shapes=[pltpu.VMEM((n_br - 1, tm, 128), F32), pltpu.VMEM((IN_COLS, D_MODEL), F32)],
        semantics=("arbitrary",), args=(dua, *dqs, *dks, *dvs, cos, sin, hn1), exchange=exchange)
    return dproj, dw, received


def _mm_tn(a, b, name):
    s_len, m = a.shape
    n = b.shape[1]
    tk = 2 * ROW_TILE
    tm = m
    n_k = s_len // tk

    def body(a_ref, b_ref, o_ref, acc_ref):
        k = pl.program_id(1)

        @pl.when(k == 0)
        def _():
            acc_ref[...] = jnp.zeros_like(acc_ref)

        bb = b_ref[...].astype(BF16)
        for j in range(tm // 256):
            rows = slice(j * 256, (j + 1) * 256)
            acc_ref[rows, :] += lax.dot_general(a_ref[:, rows].astype(BF16), bb, _TN, preferred_element_type=F32)

        @pl.when(k == n_k - 1)
        def _():
            o_ref[...] = acc_ref[...].astype(BF16)

    return pl.pallas_call(
        body, name=name, grid=(m // tm, n_k),
        in_specs=[pl.BlockSpec((tk, tm), lambda i, k: (k, i)), pl.BlockSpec((tk, n), lambda i, k: (k, 0))],
        out_specs=pl.BlockSpec((tm, n), lambda i, k: (i, 0)),
        out_shape=jax.ShapeDtypeStruct((m, n), BF16),
        scratch_shapes=[pltpu.VMEM((tm, n), F32)],
        compiler_params=_params("parallel", "arbitrary"),
    )(a, b)


def _position():
    x, y, c = lax.axis_index("x"), lax.axis_index("y"), lax.axis_index("c")
    return x, y, c, 4 * x + 2 * y + c


def _peer(x, y, c, rel):
    return (x ^ ((rel >> 2) & 1), y ^ ((rel >> 1) & 1), c ^ (rel & 1))


def _exchange_out_shape(kind, arr):
    return jax.ShapeDtypeStruct(((N_DEV,) + arr.shape) if kind == "gather" else arr.shape, arr.dtype)


def _exchange_sems(n_items):
    return [pltpu.SemaphoreType.DMA((n_items, N_DEV)), pltpu.SemaphoreType.DMA((n_items, N_DEV)), pltpu.SemaphoreType.DMA((n_items,))]


def _exchange_copies(kinds, srcs, dsts, sems, arrivals):
    send_sems, recv_sems, local_sems = sems
    x, y, c, me = _position()
    local, sends, recvs = [], [], []
    for k, (kind, src, dst) in enumerate(zip(kinds, srcs, dsts)):
        own = src if kind == "gather" else src.at[me]
        local.append(pltpu.make_async_copy(own, dst.at[me], local_sems.at[k]))
        for rel in range(1, N_DEV):
            going = src if kind == "gather" else src.at[me ^ rel]
            common = dict(send_sem=send_sems.at[k, rel], recv_sem=recv_sems.at[k, rel],
                          device_id=_peer(x, y, c, rel), device_id_type=MESH)
            sends.append(pltpu.make_async_remote_copy(src_ref=going, dst_ref=dst.at[me], **common))
            if arrivals:
                recvs.append(pltpu.make_async_remote_copy(src_ref=own, dst_ref=dst.at[me ^ rel], **common))
    return local, sends, recvs


def _exchange_start(kinds, srcs, dsts, sems):
    local, sends, _ = _exchange_copies(kinds, srcs, dsts, sems, arrivals=False)
    for cp in local + sends:
        cp.start()


def _exchange_finish(kinds, srcs, dsts, sems):
    local, sends, recvs = _exchange_copies(kinds, srcs, dsts, sems, arrivals=True)
    for cp in recvs:
        cp.wait_recv()
    for cp in sends:
        cp.wait_send()
    for cp in local:
        cp.wait()


def _gather_two_level_with_rope_tables(shard, inv_freq, s_len, name):
    rows = ROW_TILE

    def body(inv_ref, src, cos_ref, sin_ref, dst, send_sems, recv_sems, local_sem):
        x, y, c, me = _position()
        sibling = (x, y, 1 - c)
        chips = [(1 - x, y), (x, 1 - y), (1 - x, 1 - y)]

        def block(px, py, pc):
            return dst.at[4 * px + 2 * py + pc]

        def copy(k, blk, to, src_ref=None):
            return pltpu.make_async_remote_copy(
                src_ref=block(*blk) if src_ref is None else src_ref, dst_ref=block(*blk),
                send_sem=send_sems.at[k], recv_sem=recv_sems.at[k], device_id=to, device_id_type=MESH)

        x_nbr, y_nbr, diag = chips
        mine = pltpu.make_async_copy(src, dst.at[me], local_sem)
        mine.start()
        first = [copy(0, (x, y, c), sibling, src), copy(1, (x, y, c), (*x_nbr, c), src), copy(2, (x, y, c), (*y_nbr, c), src)]
        for cp in first:
            cp.start()

        inv = inv_ref[...]
        lane = lax.broadcasted_iota(jnp.int32, (rows, 128), 1)
        sign = jnp.where((lane // (HEAD_DIM // 2)) % 2 == 0, -1.0, 1.0)
        row = lax.broadcasted_iota(jnp.int32, (rows, 128), 0)
        n_chunks = s_len // rows

        def fill_tables(lo, hi):
            @pl.loop(lo, hi)
            def _(i):
                at = pl.multiple_of(i * rows, rows)
                ang = (row + at).astype(F32) * inv
                cos_ref[pl.ds(at, rows), :] = jnp.cos(ang)
                sin_ref[pl.ds(at, rows), :] = jnp.sin(ang) * sign

        fill_tables(0, n_chunks // 2)
        passed = [copy(4 + j, (*chip, c), sibling) for j, chip in enumerate(chips)]
        copy(1, (*x_nbr, c), (x, y, c)).wait_recv()
        copy(2, (*y_nbr, c), (x, y, c)).wait_recv()

        @pl.when(c == 1)
        def _():
            copy(3, (*x_nbr, c), (*y_nbr, c)).start()

        @pl.when(c == 0)
        def _():
            copy(3, (*y_nbr, c), (*x_nbr, c)).start()

        passed[0].start()
        passed[1].start()
        fill_tables(n_chunks // 2, n_chunks)
        copy(3, (*diag, c), (x, y, c)).wait_recv()
        passed[2].start()
        copy(0, (x, y, 1 - c), (x, y, c)).wait_recv()
        for j, chip in enumerate(chips):
            copy(4 + j, (*chip, 1 - c), (x, y, c)).wait_recv()
        for cp in first + passed:
            cp.wait_send()
        copy(3, (*x_nbr, c), (*y_nbr, c)).wait_send()
        mine.wait()

    any_spec = pl.BlockSpec(memory_space=pl.ANY)
    vmem = pl.BlockSpec(memory_space=pltpu.VMEM)
    table = jax.ShapeDtypeStruct((s_len, 128), F32)
    return pl.pallas_call(
        body, name=name, in_specs=[vmem, any_spec], out_specs=[vmem, vmem, any_spec],
        out_shape=[table, table, _exchange_out_shape("gather", shard)],
        scratch_shapes=[pltpu.SemaphoreType.DMA((N_DEV - 1,)), pltpu.SemaphoreType.DMA((N_DEV - 1,)), pltpu.SemaphoreType.DMA],
        compiler_params=pltpu.CompilerParams(has_side_effects=True, vmem_limit_bytes=V7X_VMEM_LIMIT_BYTES),
    )(inv_freq, shard)


def _call(body, *, name, grid, in_specs, out_specs, out_shape, args, scratch_shapes=(), semantics, exchange=()):
    if not exchange:
        outs = pl.pallas_call(body, name=name, grid=grid, in_specs=in_specs, out_specs=out_specs, out_shape=out_shape,
                              scratch_shapes=list(scratch_shapes), compiler_params=_params(*semantics))(*args)
        return outs, []
    kinds = [k for k, _ in exchange]
    n_in, n_out, n_x, n_scr = len(in_specs), len(out_specs), len(exchange), len(scratch_shapes)

    def wrapped(*refs):
        ins, refs = refs[:n_in], refs[n_in:]
        srcs, refs = refs[:n_x], refs[n_x:]
        outs, refs = refs[:n_out], refs[n_out:]
        dsts, refs = refs[:n_x], refs[n_x:]
        scratch, sems = refs[:n_scr], refs[n_scr:]
        ids = [pl.program_id(a) for a in range(len(grid))]
        first = functools.reduce(jnp.logical_and, [i == 0 for i in ids])
        last = functools.reduce(jnp.logical_and, [i == g - 1 for i, g in zip(ids, grid)])

        @pl.when(first)
        def _():
            _exchange_start(kinds, srcs, dsts, sems)

        body(*ins, *outs, *scratch)

        @pl.when(last)
        def _():
            _exchange_finish(kinds, srcs, dsts, sems)

    any_spec = pl.BlockSpec(memory_space=pl.ANY)
    outs = pl.pallas_call(
        wrapped, name=name, grid=grid,
        in_specs=list(in_specs) + [any_spec] * n_x, out_specs=list(out_specs) + [any_spec] * n_x,
        out_shape=list(out_shape) + [_exchange_out_shape(k, a) for k, a in exchange],
        scratch_shapes=list(scratch_shapes) + _exchange_sems(n_x),
        compiler_params=pltpu.CompilerParams(dimension_semantics=("arbitrary",) * len(grid),
                                             vmem_limit_bytes=V7X_VMEM_LIMIT_BYTES, has_side_effects=True),
    )(*args, *[a for _, a in exchange])
    return outs[:n_out], outs[n_out:]


def _adamw_math(w, g, m, v):
    m = ADAM_B1 * m + (1.0 - ADAM_B1) * g
    v = ADAM_B2 * v + (1.0 - ADAM_B2) * (g * g)
    m_hat = m / (1.0 - ADAM_B1 ** ADAM_STEP)
    v_hat = v / (1.0 - ADAM_B2 ** ADAM_STEP)
    return -ADAM_LR * (m_hat / (jnp.sqrt(v_hat) + ADAM_EPS) + ADAM_WD * w), m, v


def _adamw(parts, w, m, v, name, exchange=()):
    rows, cols = w.shape
    tm = 256 if rows % 256 == 0 and rows > 256 else rows

    def body(p_ref, w_ref, m_ref, v_ref, g_ref, d_ref, nm_ref, nv_ref):
        g = p_ref[0].astype(F32)
        for j in range(1, N_DEV):
            g = g + p_ref[j].astype(F32)
        delta, nm, nv = _adamw_math(w_ref[...], g, m_ref[...], v_ref[...])
        g_ref[...] = g
        d_ref[...] = delta
        nm_ref[...] = nm
        nv_ref[...] = nv

    shard = jax.ShapeDtypeStruct((rows, cols), F32)
    return _call(
        body, name=name, grid=(rows // tm,),
        in_specs=[pl.BlockSpec((N_DEV, tm, cols), lambda i: (0, i, 0))] + [_rows(tm, cols)] * 3,
        out_specs=[_rows(tm, cols)] * 4,
        out_shape=[shard] * 4,
        semantics=("parallel",), args=(parts, w, m, v), exchange=exchange)


_SMALL = ("mix_norm_g", "sgu_w", "sgu_b", "sgu_norm_g", "out_norm_a", "out_norm_b", "ffn_norm_g", "ple_norm_g", "final_norm_g")
_BIG = ("w_in", "w_out", "w_gate", "w_up", "w_down", "w_ple_gate", "w_ple_proj")
_COLUMN_SHARDED = ("w_in", "w_gate", "w_up", "w_ple_proj")
_ORDER = ("mix_norm_g", "w_in", "sgu_w", "sgu_b", "sgu_norm_g", "out_norm_a", "out_norm_b", "w_out", "ffn_norm_g",
          "w_gate", "w_up", "w_down", "ple_norm_g", "w_ple_gate", "w_ple_proj", "final_norm_g")


def _pack_small(values, names=_SMALL):
    flat = jnp.concatenate([values[n].reshape(-1).astype(F32) for n in names])
    pad = (-flat.shape[0]) % (8 * 128)
    return jnp.pad(flat, (0, pad)).reshape(-1, 128)


def _unpack_small(packed, like):
    flat = packed.reshape(-1)
    out, at = {}, 0
    for n in _SMALL:
        size = like[n].size
        out[n] = flat[at:at + size].reshape(like[n].shape)
        at += size
    return out


def _own_orientation(name, value):
    return value[0].T if name in _COLUMN_SHARDED else value[0]


def _reference_orientation(name, value):
    return (value.T if name in _COLUMN_SHARDED else value)[None]


def _full_from_gathered(gathered):
    return gathered.reshape(N_DEV * gathered.shape[1], gathered.shape[2])


def _sliced_for_devices(grad):
    return grad.reshape(N_DEV, grad.shape[0] // N_DEV, grad.shape[1])


def _rope_inv_freq():
    half = HEAD_DIM // 2
    inv = ROPE_THETA ** (-jnp.arange(half, dtype=F32) / half)
    return jnp.tile(inv, 128 // half)[None, :]


def _forward_backward(x, p, target, small, shards):
    def gather(*names):
        return [("gather", shards[n]) for n in names]

    def scatter(**grads):
        return [("scatter", _sliced_for_devices(g)) for g in grads.values()]

    full, parts = {}, {}
    s_len = x.shape[0]
    cos, sin, got = _gather_two_level_with_rope_tables(shards["w_in"], _rope_inv_freq(), s_len, "gather_w_in")
    full["w_in"] = _full_from_gathered(got)

    g_mix, g_ffn, g_ple = small["mix_norm_g"], small["ffn_norm_g"], small["ple_norm_g"]
    g_fin = small["final_norm_g"].reshape(1, D_MODEL)
    sw, gs, ga, gb = small["sgu_w"], small["sgu_norm_g"], small["out_norm_a"], small["out_norm_b"]
    b2 = jnp.repeat(small["sgu_b"].T, HEAD_DIM, axis=1)
    head_sum = (jnp.arange(WIDTH_B)[:, None] // HEAD_DIM == jnp.arange(128)[None, :]).astype(BF16)
    n_br = len(DILATIONS)

    def arrived(names, got):
        for n, g in zip(names, got):
            full[n] = _full_from_gathered(g)

    (ua, hn1, *qkv), got = _inproj(x, g_mix, full["w_in"], cos, sin, exchange=gather("w_gate"))
    arrived(("w_gate",), got)
    ya_n = _sgu_fwd(ua, sw, b2, gs, ga)
    half = shards["w_up"].shape[0] // 2
    riders = [[("gather", shards["w_up"][:half])], [("gather", shards["w_up"][half:])], gather("w_out")]
    branch, got = [], []
    for i, d in enumerate(DILATIONS):
        o_l, g = _attn_fwd(qkv[i], d, exchange=riders[i])
        branch.append(o_l)
        got += g
    arrived(("w_up", "w_out"), [jnp.concatenate(got[:2], axis=1), got[2]])
    y, yb, *lse = _combine([o for o, _ in branch], [l for _, l in branch], ya_n, gb, head_sum.T)
    last_wave = ("w_down", "w_ple_gate", "w_ple_proj")
    (h1, a, b, act, hn2), got = _ffn_up(y, full["w_out"], x, g_ffn, full["w_gate"], full["w_up"], exchange=gather(*last_wave))
    arrived(last_wave, got)
    h2, h3, gate, pp, hn3 = _ffn_down_ple(act, full["w_down"], h1, g_ple, full["w_ple_gate"], p, full["w_ple_proj"])

    dh2, loss, d_fin, d_ple, g_ple_gate, g_ple_proj = _loss_ple_bwd(
        h3, target, g_fin, gate, pp, h2, g_ple, full["w_ple_gate"], hn3, p)
    g_down = _mm_tn(act, dh2, "dw_down")
    (da, db), (parts["w_down"],) = _ffn_down_bwd(dh2, full["w_down"], a, b, exchange=scatter(w_down=g_down))
    g_gate = _mm_tn(da, hn2, "dw_gate")
    g_up = _mm_tn(db, hn2, "dw_up")
    (dh1, d_ffn, g_out), (parts["w_gate"], parts["w_ple_gate"], parts["w_ple_proj"]) = _mm_norm_bwd(
        [(da, full["w_gate"]), (db, full["w_up"])], h1, g_ffn, dh2, "ffn_up_bwd",
        exchange=scatter(w_gate=g_gate, w_ple_gate=g_ple_gate, w_ple_proj=g_ple_proj), dw_lhs=y)
    dya_n, d_gb, *do_dd = _outproj_bwd(dh1, full["w_out"], yb, gb, head_sum)
    grads_b = []
    for i, d in enumerate(DILATIONS):
        g3, got = _attn_bwd(qkv[i], do_dd[i], lse[i], do_dd[n_br + i], d,
                            exchange=scatter(w_up=g_up, w_out=g_out) if i == 0 else ())
        grads_b.append(g3)
        if i == 0:
            parts["w_up"], parts["w_out"] = got
    dua, d_sw, d_b2, d_gs, d_ga = _sgu_bwd(ua, sw, b2, gs, ga, dya_n)
    early = {
        "sgu_w": d_sw, "sgu_b": d_b2.reshape(CHUNK, 4, HEAD_DIM).sum(axis=-1).T, "sgu_norm_g": d_gs, "out_norm_a": d_ga,
        "out_norm_b": d_gb, "ffn_norm_g": d_ffn, "ple_norm_g": d_ple, "final_norm_g": d_fin,
    }
    dproj, g_in, (early_parts,) = _dproj(
        dua, [g[0] for g in grads_b], [g[1] for g in grads_b], [g[2] for g in grads_b], cos, sin, hn1,
        exchange=[("gather", _pack_small(early, _SMALL[1:]))])
    (dx, d_mix), (parts["w_in"],) = _mm_norm_bwd(
        [(dproj, full["w_in"])], x, g_mix, dh1, "inproj_bwd", exchange=scatter(w_in=g_in))
    late = jnp.concatenate([_pack_small({"mix_norm_g": d_mix}, _SMALL[:1]), jnp.broadcast_to(loss, (8, 128))])
    return dx, parts, early_parts, late


def kernel(x, p, mix_norm_g, w_in, sgu_w, sgu_b, sgu_norm_g, out_norm_a, out_norm_b, w_out, ffn_norm_g, w_gate, w_up, w_down, ple_norm_g, w_ple_gate, w_ple_proj, final_norm_g, loss_target, m_mix_norm_g, m_w_in, m_sgu_w, m_sgu_b, m_sgu_norm_g, m_out_norm_a, m_out_norm_b, m_w_out, m_ffn_norm_g, m_w_gate, m_w_up, m_w_down, m_ple_norm_g, m_w_ple_gate, m_w_ple_proj, m_final_norm_g, v_mix_norm_g, v_w_in, v_sgu_w, v_sgu_b, v_sgu_norm_g, v_out_norm_a, v_out_norm_b, v_w_out, v_ffn_norm_g, v_w_gate, v_w_up, v_w_down, v_ple_norm_g, v_w_ple_gate, v_w_ple_proj, v_final_norm_g):
    given = dict(locals())
    weights = {n: given[n] for n in _ORDER}
    moments_m = {n: given["m_" + n] for n in _ORDER}
    moments_v = {n: given["v_" + n] for n in _ORDER}

    shards = {n: _own_orientation(n, weights[n]).astype(BF16) for n in _BIG}
    small = {n: (weights[n][0] if n in ("sgu_w", "sgu_b") else weights[n]) for n in _SMALL}

    dx, parts, early_parts, late = _forward_backward(x[0], p[0, 0], loss_target[0], small, shards)

    small_like = {n: weights[n] for n in _SMALL}
    grads, deltas, new_m, new_v = {}, {}, {}, {}
    for n in _BIG:
        outs, got = _adamw(parts[n], _own_orientation(n, weights[n]), _own_orientation(n, moments_m[n]),
                           _own_orientation(n, moments_v[n]), "adamw_" + n, exchange=[("gather", late)] if n == "w_in" else ())
        grads[n], deltas[n], new_m[n], new_v[n] = [_reference_orientation(n, o) for o in outs]
        if n == "w_in":
            (late_parts,) = got
    loss = jnp.sum(late_parts[:, 8, 0])
    small_parts = jnp.concatenate([late_parts[:, :8], early_parts], axis=1)
    (g, d, nm, nv), _ = _adamw(small_parts, _pack_small(small_like), _pack_small({n: moments_m[n] for n in _SMALL}),
                               _pack_small({n: moments_v[n] for n in _SMALL}), "adamw_small")
    for out, packed in ((grads, g), (deltas, d), (new_m, nm), (new_v, nv)):
        out.update(_unpack_small(packed, small_like))

    return (loss, dx[None], *[grads[n] for n in _ORDER], *[deltas[n] for n in _ORDER],
            *[new_m[n] for n in _ORDER], *[new_v[n] for n in _ORDER])
```
